```python
import jax
import jax.numpy as jnp
from jax import lax
import numpy as np

D_MODEL = 1024
BATCH = 8
SEQ = 4096
DEPTH = 1

GRID_W = 64
CTX_LEN = 256
GM_WIDTH = 1024
GM_GROUPS = 8
GM_GROUP_DIM = GM_WIDTH // GM_GROUPS
GM_CHUNK = 2 * GRID_W
GDN_HEADS = 8
GDN_DK = 128
GDN_DV = 128
GDN_KD = GDN_HEADS * GDN_DK
GDN_VD = GDN_HEADS * GDN_DV
GDN_CONV = 3
GDN_CHUNK = 64
EPS = 1e-6

OFF_Q = 0
OFF_K = OFF_Q + GDN_KD
OFF_V = OFF_K + GDN_KD
OFF_A = OFF_V + GDN_VD
OFF_B = OFF_A + 2 * GDN_HEADS
OFF_ZB = OFF_B + 2 * GDN_HEADS
OFF_UA = OFF_ZB + GDN_VD
OFF_VA = OFF_UA + GM_WIDTH
OFF_ZA = OFF_VA + GM_WIDTH
OFF_G = OFF_ZA + GM_WIDTH
IN_COLS = OFF_G + 2 * D_MODEL

kernel_name = 'hybrid_gmlp_gdeltanet_prefix_dit'


def _rmsnorm(x, g):
    xf = x.astype(jnp.float32)
    y = xf * lax.rsqrt(jnp.mean(xf * xf, axis=-1, keepdims=True) + EPS)
    return (y * g.astype(jnp.float32)).astype(x.dtype)


def _layernorm(x, g, b):
    xf = x.astype(jnp.float32)
    xc = xf - jnp.mean(xf, axis=-1, keepdims=True)
    y = xc * lax.rsqrt(jnp.mean(xc * xc, axis=-1, keepdims=True) + EPS)
    return (y * g.astype(jnp.float32) + b.astype(jnp.float32)).astype(x.dtype)


def _l2norm(x):
    return x * lax.rsqrt(jnp.sum(x * x, axis=-1, keepdims=True) + EPS)


def _heads(u, d):
    return u.reshape(u.shape[:-1] + (u.shape[-1] // d, d))


def _modulation(cond, w_mod, b_mod):
    m = jax.nn.silu(cond) @ w_mod + b_mod
    m = m.reshape((-1, 1, 3, D_MODEL))
    return m[:, :, 0], m[:, :, 1], m[:, :, 2]


def _projector(h, w_in, full):
    if full:
        p = h @ w_in
        return lambda lo, hi: p[..., lo:hi]
    return lambda lo, hi: h @ w_in[:, lo:hi]


def _conv_silu(u, w):
    pad = w.shape[0] // 2
    y = lax.conv_general_dilated(u, w[:, None, :].astype(u.dtype), window_strides=(1,),
                                 padding=[(pad, pad)], dimension_numbers=('NWC', 'WIO', 'NWC'),
                                 feature_group_count=u.shape[-1])
    return jax.nn.silu(y)


def _gdn_side(col, w_conv, a_log, dt_bias, with_q):
    lo = OFF_Q if with_q else OFF_K
    qkv = _conv_silu(col(lo, OFF_A), w_conv[:, lo - OFF_Q:]).astype(jnp.float32)
    k = _l2norm(_heads(qkv[..., -(GDN_KD + GDN_VD):-GDN_VD], GDN_DK))
    v = _heads(qkv[..., -GDN_VD:], GDN_DV)
    q = _l2norm(_heads(qkv[..., :GDN_KD], GDN_DK)) if with_q else None
    ab = col(OFF_A, OFF_ZB).astype(jnp.float32)
    ab = ab.reshape(ab.shape[:-1] + (2, 2, GDN_HEADS))
    g = -jnp.exp(a_log.astype(jnp.float32)) * jax.nn.softplus(ab[..., 0, :, :] + dt_bias.astype(jnp.float32))
    beta = jax.nn.sigmoid(ab[..., 1, :, :])
    return q, k, v, g, beta


def _chunks(t):
    t = jnp.moveaxis(t, 2, 1)
    t = t.reshape(t.shape[:2] + (-1, GDN_CHUNK) + t.shape[3:])
    return jnp.moveaxis(t, 2, 0)


def _unchunk(t):
    n, b, h, cl, d = t.shape
    return t.transpose(1, 0, 3, 2, 4).reshape(b, n * cl, h, d)


def _gdn_direction(q, k, v, beta, g, s0):
    kc, vc, bc, gc = (_chunks(t) for t in (k, v, beta, g))
    gcum = jnp.cumsum(gc, axis=-1)
    idx = jnp.arange(GDN_CHUNK)
    incl = idx[:, None] >= idx[None, :]
    decay = jnp.exp(jnp.where(incl, gcum[..., :, None] - gcum[..., None, :], -jnp.inf))
    kb = kc * bc[..., None]
    a_mat = jnp.where(idx[:, None] > idx[None, :],
                      jnp.einsum('nbhcd,nbhsd->nbhcs', kb, kc) * decay, 0.0)
    rhs = jnp.concatenate([vc * bc[..., None], kb * jnp.exp(gcum)[..., None]], axis=-1)
    sol = lax.linalg.triangular_solve(a_mat + jnp.eye(GDN_CHUNK, dtype=jnp.float32), rhs,
                                      left_side=True, lower=True)
    u_val, w_key = sol[..., :GDN_DV], sol[..., GDN_DV:]
    g_last = gcum[..., -1]
    k_tail = kc * jnp.exp(g_last[..., None] - gcum)[..., None]

    def advance(S, u_c, w_c, kt_c, gl_c):
        v_new = u_c - jnp.einsum('bhcd,bhde->bhce', w_c, S)
        S_new = S * jnp.exp(gl_c)[..., None, None] + jnp.einsum('bhcd,bhce->bhde', kt_c, v_new)
        return v_new, S_new

    if q is None:
        def state_step(S, xs):
            _, S_new = advance(S, *xs)
            return S_new, None
        s_final, _ = lax.scan(state_step, s0, (u_val, w_key, k_tail, g_last))
        return None, s_final

    qc = _chunks(q) * (GDN_DK ** -0.5)
    attn = jnp.einsum('nbhcd,nbhsd->nbhcs', qc, kc) * decay
    q_dec = qc * jnp.exp(gcum)[..., None]

    def out_step(S, xs):
        u_c, w_c, kt_c, gl_c, qd_c, at_c = xs
        v_new, S_new = advance(S, u_c, w_c, kt_c, gl_c)
        o = jnp.einsum('bhcd,bhde->bhce', qd_c, S) + jnp.einsum('bhcs,bhse->bhce', at_c, v_new)
        return S_new, o

    s_final, o = lax.scan(out_step, s0, (u_val, w_key, k_tail, g_last, q_dec, attn))
    return _unchunk(o), s_final


def _gdn_bidir(q, k, v, g, beta, s0_f, s0_b):
    rev = lambda t: None if t is None else jnp.flip(t, axis=1)
    o_f, s_f = _gdn_direction(q, k, v, beta[:, :, 0], g[:, :, 0], s0_f)
    o_b, s_b = _gdn_direction(rev(q), rev(k), rev(v), rev(beta[:, :, 1]), rev(g[:, :, 1]), s0_b)
    o = None if q is None else o_f + rev(o_b)
    return o, s_f, s_b


def _gmlp(pu, pv, pz, ln_g, ln_b, w_sp, b_sp, n_chunks):
    u = jax.nn.gelu(pu)
    v = _layernorm(jax.nn.gelu(pv), ln_g, ln_b)
    b, l, _ = v.shape
    v = v.reshape(b, n_chunks, GM_CHUNK, GM_GROUPS, GM_GROUP_DIM)
    s = jnp.einsum('gpq,bnqgc->bnpgc', w_sp, v) + jnp.transpose(b_sp)[:, :, None]
    return u * s.reshape(b, l, GM_WIDTH) * jax.nn.silu(pz)


def _stream_out(col, o_gdn, n_chunks, g_onorm, gm_ln_g, gm_ln_b, w_sp, b_sp, w_pa, w_pb, w_out):
    zb_raw = col(OFF_ZB, OFF_UA)
    z_b = _heads(zb_raw, GDN_DV).astype(jnp.float32)
    y_b = _rmsnorm(o_gdn, g_onorm) * jax.nn.silu(z_b)
    y_b = y_b.reshape(y_b.shape[:-2] + (GDN_VD,)).astype(zb_raw.dtype)
    y_a = _gmlp(col(OFF_UA, OFF_VA), col(OFF_VA, OFF_ZA), col(OFF_ZA, OFF_G),
                gm_ln_g, gm_ln_b, w_sp, b_sp, n_chunks)
    gates = jax.nn.sigmoid(col(OFF_G, IN_COLS))
    merged = gates[..., :D_MODEL] * (y_a @ w_pa) + gates[..., D_MODEL:] * (y_b @ w_pb)
    return merged @ w_out


def _hybrid_layer(x, ctx, mod_x, mod_c, n_chunks, update_ctx, g_pre, g_post, w_in, w_conv,
                  a_log, dt_bias, g_onorm, gm_ln_g, gm_ln_b, w_sp, b_sp, w_pa, w_pb, w_out):
    shift_x, scale_x, gate_x = mod_x
    shift_c, scale_c, gate_c = mod_c
    s_zero = jnp.zeros((ctx.shape[0], GDN_HEADS, GDN_DK, GDN_DV), jnp.float32)
    branch_w = (g_onorm, gm_ln_g, gm_ln_b, w_sp, b_sp, w_pa, w_pb, w_out)
    h_c = _rmsnorm(ctx, g_pre) * (1 + scale_c) + shift_c
    col_c = _projector(h_c, w_in, update_ctx)
    q_c, k_c, v_c, g_c, beta_c = _gdn_side(col_c, w_conv, a_log, dt_bias, update_ctx)
    o_c, s_fwd, s_bwd = _gdn_bidir(q_c, k_c, v_c, g_c, beta_c, s_zero, s_zero)
    h_x = _rmsnorm(x, g_pre) * (1 + scale_x) + shift_x
    col_x = _projector(h_x, w_in, True)
    q_x, k_x, v_x, g_x, beta_x = _gdn_side(col_x, w_conv, a_log, dt_bias, True)
    o_x, _, _ = _gdn_bidir(q_x, k_x, v_x, g_x, beta_x, s_fwd, s_bwd)
    x = x + gate_x * _rmsnorm(_stream_out(col_x, o_x, n_chunks, *branch_w), g_post)
    if update_ctx:
        ctx = ctx + gate_c * _rmsnorm(_stream_out(col_c, o_c, ctx.shape[1] // GM_CHUNK, *branch_w), g_post)
    return x, ctx


def _fwd_setup_inputs(seed: int = 0) -> dict:
    key = jax.random.key(seed)
    ks = jax.random.split(key, 20)
    nrm = lambda k, shape, s: s * jax.random.normal(k, shape, jnp.float32)
    d = D_MODEL
    x = nrm(ks[0], (BATCH, SEQ, d), 1.0)
    c = nrm(ks[1], (BATCH, d), 1.0)
    ctx = nrm(ks[2], (BATCH, CTX_LEN, d), 1.0)
    c_ctx = nrm(ks[3], (d,), 1.0)
    w_mod = nrm(ks[4], (DEPTH, d, 3 * d), 0.5 * d ** -0.5)
    b_mod = nrm(ks[5], (DEPTH, 3 * d), 0.02)
    g_pre = 1.0 + nrm(ks[6], (DEPTH, d), 0.05)
    g_post = 1.0 + nrm(ks[7], (DEPTH, d), 0.05)
    w_in = nrm(ks[8], (DEPTH, d, IN_COLS), d ** -0.5)
    w_conv = nrm(ks[9], (DEPTH, GDN_CONV, OFF_A), GDN_CONV ** -0.5)
    a_log = jnp.log(jax.random.uniform(ks[10], (DEPTH, 2, GDN_HEADS), jnp.float32, 1.0, 16.0))
    dt = jnp.exp(jax.random.uniform(ks[11], (DEPTH, 2, GDN_HEADS), jnp.float32,
                                    float(np.log(1e-3)), float(np.log(1e-1))))
    dt_bias = dt + jnp.log(-jnp.expm1(-dt))
    g_onorm = 1.0 + nrm(ks[12], (DEPTH, GDN_DV), 0.05)
    gm_ln_g = 1.0 + nrm(ks[13], (DEPTH, GM_WIDTH), 0.05)
    gm_ln_b = nrm(ks[14], (DEPTH, GM_WIDTH), 0.02)
    w_sp = nrm(ks[15], (DEPTH, GM_GROUPS, GM_CHUNK, GM_CHUNK), GM_CHUNK ** -0.5)
    b_sp = 1.0 + nrm(ks[16], (DEPTH, GM_GROUPS, GM_CHUNK), 0.02)
    w_pa = nrm(ks[17], (DEPTH, GM_WIDTH, d), GM_WIDTH ** -0.5)
    w_pb = nrm(ks[18], (DEPTH, GDN_VD, d), GDN_VD ** -0.5)
    w_out = nrm(ks[19], (DEPTH, d, d), d ** -0.5)
    return {'x': x, 'c': c, 'ctx': ctx, 'c_ctx': c_ctx, 'w_mod': w_mod, 'b_mod': b_mod,
            'g_pre': g_pre, 'g_post': g_post, 'w_in': w_in, 'w_conv': w_conv, 'a_log': a_log,
            'dt_bias': dt_bias, 'g_onorm': g_onorm, 'gm_ln_g': gm_ln_g, 'gm_ln_b': gm_ln_b,
            'w_sp': w_sp, 'b_sp': b_sp, 'w_pa': w_pa, 'w_pb': w_pb, 'w_out': w_out}


def _fwd_reference(x, c, ctx, c_ctx, w_mod, b_mod, g_pre, g_post, w_in, w_conv, a_log, dt_bias,
              g_onorm, gm_ln_g, gm_ln_b, w_sp, b_sp, w_pa, w_pb, w_out):
    rows = x.shape[1] // GRID_W
    n_chunks = rows // (GM_CHUNK // GRID_W)
    for i in range(DEPTH):
        mod_x = _modulation(c, w_mod[i], b_mod[i])
        mod_c = _modulation(c_ctx, w_mod[i], b_mod[i])
        x, ctx = _hybrid_layer(x, ctx, mod_x, mod_c, n_chunks, i + 1 < DEPTH, g_pre[i], g_post[i],
                               w_in[i], w_conv[i], a_log[i], dt_bias[i], g_onorm[i], gm_ln_g[i],
                               gm_ln_b[i], w_sp[i], b_sp[i], w_pa[i], w_pb[i], w_out[i])
    return x


import jax as _jax
import jax.numpy as _jnp

TWIN_FORMAT = 'train_step'
FWD_PARAMS = ['x', 'c', 'ctx', 'c_ctx', 'w_mod', 'b_mod', 'g_pre', 'g_post', 'w_in', 'w_conv', 'a_log', 'dt_bias', 'g_onorm', 'gm_ln_g', 'gm_ln_b', 'w_sp', 'b_sp', 'w_pa', 'w_pb', 'w_out']
TWIN_WEIGHTS = ['c_ctx', 'w_mod', 'b_mod', 'g_pre', 'g_post', 'w_in', 'w_conv', 'a_log', 'dt_bias', 'g_onorm', 'gm_ln_g', 'gm_ln_b', 'w_sp', 'b_sp', 'w_pa', 'w_pb', 'w_out']
TWIN_DIFF_INPUT = 'x'
TWIN_INPUTS = ['x', 'c', 'ctx', 'c_ctx', 'w_mod', 'b_mod', 'g_pre', 'g_post', 'w_in', 'w_conv', 'a_log', 'dt_bias', 'g_onorm', 'gm_ln_g', 'gm_ln_b', 'w_sp', 'b_sp', 'w_pa', 'w_pb', 'w_out', 'loss_target', 'm_c_ctx', 'm_w_mod', 'm_b_mod', 'm_g_pre', 'm_g_post', 'm_w_in', 'm_w_conv', 'm_a_log', 'm_dt_bias', 'm_g_onorm', 'm_gm_ln_g', 'm_gm_ln_b', 'm_w_sp', 'm_b_sp', 'm_w_pa', 'm_w_pb', 'm_w_out', 'v_c_ctx', 'v_w_mod', 'v_b_mod', 'v_g_pre', 'v_g_post', 'v_w_in', 'v_w_conv', 'v_a_log', 'v_dt_bias', 'v_g_onorm', 'v_gm_ln_g', 'v_gm_ln_b', 'v_w_sp', 'v_b_sp', 'v_w_pa', 'v_w_pb', 'v_w_out']
TWIN_OUTPUTS = ['loss', 'grad_x', 'grad_c_ctx', 'grad_w_mod', 'grad_b_mod', 'grad_g_pre', 'grad_g_post', 'grad_w_in', 'grad_w_conv', 'grad_a_log', 'grad_dt_bias', 'grad_g_onorm', 'grad_gm_ln_g', 'grad_gm_ln_b', 'grad_w_sp', 'grad_b_sp', 'grad_w_pa', 'grad_w_pb', 'grad_w_out', 'delta_c_ctx', 'delta_w_mod', 'delta_b_mod', 'delta_g_pre', 'delta_g_post', 'delta_w_in', 'delta_w_conv', 'delta_a_log', 'delta_dt_bias', 'delta_g_onorm', 'delta_gm_ln_g', 'delta_gm_ln_b', 'delta_w_sp', 'delta_b_sp', 'delta_w_pa', 'delta_w_pb', 'delta_w_out', 'new_m_c_ctx', 'new_m_w_mod', 'new_m_b_mod', 'new_m_g_pre', 'new_m_g_post', 'new_m_w_in', 'new_m_w_conv', 'new_m_a_log', 'new_m_dt_bias', 'new_m_g_onorm', 'new_m_gm_ln_g', 'new_m_gm_ln_b', 'new_m_w_sp', 'new_m_b_sp', 'new_m_w_pa', 'new_m_w_pb', 'new_m_w_out', 'new_v_c_ctx', 'new_v_w_mod', 'new_v_b_mod', 'new_v_g_pre', 'new_v_g_post', 'new_v_w_in', 'new_v_w_conv', 'new_v_a_log', 'new_v_dt_bias', 'new_v_g_onorm', 'new_v_gm_ln_g', 'new_v_gm_ln_b', 'new_v_w_sp', 'new_v_b_sp', 'new_v_w_pa', 'new_v_w_pb', 'new_v_w_out']
TWIN_LEAF_KINDS = {'loss': 'loss', 'grad_x': 'grad_x', 'grad_c_ctx': 'grad_w', 'grad_w_mod': 'grad_w', 'grad_b_mod': 'grad_w', 'grad_g_pre': 'grad_w', 'grad_g_post': 'grad_w', 'grad_w_in': 'grad_w', 'grad_w_conv': 'grad_w', 'grad_a_log': 'grad_w', 'grad_dt_bias': 'grad_w', 'grad_g_onorm': 'grad_w', 'grad_gm_ln_g': 'grad_w', 'grad_gm_ln_b': 'grad_w', 'grad_w_sp': 'grad_w', 'grad_b_sp': 'grad_w', 'grad_w_pa': 'grad_w', 'grad_w_pb': 'grad_w', 'grad_w_out': 'grad_w', 'delta_c_ctx': 'delta_w', 'delta_w_mod': 'delta_w', 'delta_b_mod': 'delta_w', 'delta_g_pre': 'delta_w', 'delta_g_post': 'delta_w', 'delta_w_in': 'delta_w', 'delta_w_conv': 'delta_w', 'delta_a_log': 'delta_w', 'delta_dt_bias': 'delta_w', 'delta_g_onorm': 'delta_w', 'delta_gm_ln_g': 'delta_w', 'delta_gm_ln_b': 'delta_w', 'delta_w_sp': 'delta_w', 'delta_b_sp': 'delta_w', 'delta_w_pa': 'delta_w', 'delta_w_pb': 'delta_w', 'delta_w_out': 'delta_w', 'new_m_c_ctx': 'new_m', 'new_m_w_mod': 'new_m', 'new_m_b_mod': 'new_m', 'new_m_g_pre': 'new_m', 'new_m_g_post': 'new_m', 'new_m_w_in': 'new_m', 'new_m_w_conv': 'new_m', 'new_m_a_log': 'new_m', 'new_m_dt_bias': 'new_m', 'new_m_g_onorm': 'new_m', 'new_m_gm_ln_g': 'new_m', 'new_m_gm_ln_b': 'new_m', 'new_m_w_sp': 'new_m', 'new_m_b_sp': 'new_m', 'new_m_w_pa': 'new_m', 'new_m_w_pb': 'new_m', 'new_m_w_out': 'new_m', 'new_v_c_ctx': 'new_v', 'new_v_w_mod': 'new_v', 'new_v_b_mod': 'new_v', 'new_v_g_pre': 'new_v', 'new_v_g_post': 'new_v', 'new_v_w_in': 'new_v', 'new_v_w_conv': 'new_v', 'new_v_a_log': 'new_v', 'new_v_dt_bias': 'new_v', 'new_v_g_onorm': 'new_v', 'new_v_gm_ln_g': 'new_v', 'new_v_gm_ln_b': 'new_v', 'new_v_w_sp': 'new_v', 'new_v_b_sp': 'new_v', 'new_v_w_pa': 'new_v', 'new_v_w_pb': 'new_v', 'new_v_w_out': 'new_v'}


def _forward(args):
    return _fwd_reference(*[args[k] for k in FWD_PARAMS])


def _output_shape():
    def fwd():
        inp = _fwd_setup_inputs(0)
        return _fwd_reference(*[inp[k] for k in FWD_PARAMS])
    out = _jax.eval_shape(fwd)
    return out.shape, out.dtype

N_MICROBATCH = 1
ADAM_LR = 0.001
ADAM_B1 = 0.9
ADAM_B2 = 0.999
ADAM_EPS = 1e-08
ADAM_WD = 0.01
ADAM_STEP = 10
PER_EXAMPLE_BATCH_AXIS = {'x': 0, 'c': 0, 'ctx': 0, 'loss_target': 0}
SHARED_INPUTS = []
_WEIGHT_DTYPES = {'c_ctx': _jnp.float32, 'w_mod': _jnp.float32, 'b_mod': _jnp.float32, 'g_pre': _jnp.float32, 'g_post': _jnp.float32, 'w_in': _jnp.float32, 'w_conv': _jnp.float32, 'a_log': _jnp.float32, 'dt_bias': _jnp.float32, 'g_onorm': _jnp.float32, 'gm_ln_g': _jnp.float32, 'gm_ln_b': _jnp.float32, 'w_sp': _jnp.float32, 'b_sp': _jnp.float32, 'w_pa': _jnp.float32, 'w_pb': _jnp.float32, 'w_out': _jnp.float32}
MOMENT_SCALE = {'c_ctx': 3.926593e-03, 'w_mod': 1.248215e+00, 'b_mod': 2.746204e+00, 'g_pre': 1.529875e-01, 'g_post': 3.327065e+00, 'w_in': 5.069374e-02, 'w_conv': 5.522759e-02, 'a_log': 2.040485e-01, 'dt_bias': 1.973014e-01, 'g_onorm': 2.841124e-01, 'gm_ln_g': 3.847940e-02, 'gm_ln_b': 3.943520e-02, 'w_sp': 3.985324e-02, 'b_sp': 3.959796e-02, 'w_pa': 6.619598e-02, 'w_pb': 9.497846e-02, 'w_out': 1.204595e-01}


def _to_microbatches(a, axis):
    t = _jnp.moveaxis(a, axis, 0)
    t = t.reshape((N_MICROBATCH, t.shape[0] // N_MICROBATCH) + t.shape[1:])
    return _jnp.moveaxis(t, 1, axis + 1)


def setup_inputs(seed: int = 0) -> dict:
    inp = _fwd_setup_inputs(seed)
    key = _jax.random.fold_in(_jax.random.key(seed), 7919)
    shape, _ = _output_shape()
    out = dict(inp)
    out["loss_target"] = _jax.random.normal(_jax.random.fold_in(key, 0), shape, _jnp.float32)
    for i, name in enumerate(TWIN_WEIGHTS):
        w = inp[name].astype(_jnp.float32)
        if MOMENT_SCALE is None:
            s = _jnp.sqrt(_jnp.mean(_jnp.square(w)) + 1e-30)
        else:
            s = MOMENT_SCALE[name]
        km, kv = _jax.random.split(_jax.random.fold_in(key, i + 1))
        out[name] = w
        out["m_" + name] = s * _jax.random.normal(km, w.shape, _jnp.float32)
        out["v_" + name] = (s * s) * _jax.random.uniform(kv, w.shape, _jnp.float32, 0.5, 1.5)
    if N_MICROBATCH > 1:
        for name, axis in PER_EXAMPLE_BATCH_AXIS.items():
            out[name] = _to_microbatches(out[name], axis)
    return {'x': out['x'], 'c': out['c'], 'ctx': out['ctx'], 'c_ctx': out['c_ctx'], 'w_mod': out['w_mod'], 'b_mod': out['b_mod'], 'g_pre': out['g_pre'], 'g_post': out['g_post'], 'w_in': out['w_in'], 'w_conv': out['w_conv'], 'a_log': out['a_log'], 'dt_bias': out['dt_bias'], 'g_onorm': out['g_onorm'], 'gm_ln_g': out['gm_ln_g'], 'gm_ln_b': out['gm_ln_b'], 'w_sp': out['w_sp'], 'b_sp': out['b_sp'], 'w_pa': out['w_pa'], 'w_pb': out['w_pb'], 'w_out': out['w_out'], 'loss_target': out['loss_target'], 'm_c_ctx': out['m_c_ctx'], 'm_w_mod': out['m_w_mod'], 'm_b_mod': out['m_b_mod'], 'm_g_pre': out['m_g_pre'], 'm_g_post': out['m_g_post'], 'm_w_in': out['m_w_in'], 'm_w_conv': out['m_w_conv'], 'm_a_log': out['m_a_log'], 'm_dt_bias': out['m_dt_bias'], 'm_g_onorm': out['m_g_onorm'], 'm_gm_ln_g': out['m_gm_ln_g'], 'm_gm_ln_b': out['m_gm_ln_b'], 'm_w_sp': out['m_w_sp'], 'm_b_sp': out['m_b_sp'], 'm_w_pa': out['m_w_pa'], 'm_w_pb': out['m_w_pb'], 'm_w_out': out['m_w_out'], 'v_c_ctx': out['v_c_ctx'], 'v_w_mod': out['v_w_mod'], 'v_b_mod': out['v_b_mod'], 'v_g_pre': out['v_g_pre'], 'v_g_post': out['v_g_post'], 'v_w_in': out['v_w_in'], 'v_w_conv': out['v_w_conv'], 'v_a_log': out['v_a_log'], 'v_dt_bias': out['v_dt_bias'], 'v_g_onorm': out['v_g_onorm'], 'v_gm_ln_g': out['v_gm_ln_g'], 'v_gm_ln_b': out['v_gm_ln_b'], 'v_w_sp': out['v_w_sp'], 'v_b_sp': out['v_b_sp'], 'v_w_pa': out['v_w_pa'], 'v_w_pb': out['v_w_pb'], 'v_w_out': out['v_w_out']}


def _loss(weights, diff, rest, loss_target):
    with _jax.named_scope("forward"):
        args = {**rest, TWIN_DIFF_INPUT: diff, **{k: w.astype(_WEIGHT_DTYPES[k]) for k, w in weights.items()}}
        y = _forward(args)
    with _jax.named_scope("loss_head"):
        err = _jnp.square(y.astype(_jnp.float32) - loss_target)
        return 0.5 * _jnp.sum(_jnp.mean(err, axis=-1)) if err.ndim else 0.5 * err


def _adamw(w, g, m, v):
    m = ADAM_B1 * m + (1.0 - ADAM_B1) * g
    v = ADAM_B2 * v + (1.0 - ADAM_B2) * _jnp.square(g)
    m_hat = m / (1.0 - ADAM_B1 ** ADAM_STEP)
    v_hat = v / (1.0 - ADAM_B2 ** ADAM_STEP)
    delta = -ADAM_LR * (m_hat / (_jnp.sqrt(v_hat) + ADAM_EPS) + ADAM_WD * w)
    return delta, m, v


def reference(x, c, ctx, c_ctx, w_mod, b_mod, g_pre, g_post, w_in, w_conv, a_log, dt_bias, g_onorm, gm_ln_g, gm_ln_b, w_sp, b_sp, w_pa, w_pb, w_out, loss_target, m_c_ctx, m_w_mod, m_b_mod, m_g_pre, m_g_post, m_w_in, m_w_conv, m_a_log, m_dt_bias, m_g_onorm, m_gm_ln_g, m_gm_ln_b, m_w_sp, m_b_sp, m_w_pa, m_w_pb, m_w_out, v_c_ctx, v_w_mod, v_b_mod, v_g_pre, v_g_post, v_w_in, v_w_conv, v_a_log, v_dt_bias, v_g_onorm, v_gm_ln_g, v_gm_ln_b, v_w_sp, v_b_sp, v_w_pa, v_w_pb, v_w_out):
    given = dict(x=x, c=c, ctx=ctx, c_ctx=c_ctx, w_mod=w_mod, b_mod=b_mod, g_pre=g_pre, g_post=g_post, w_in=w_in, w_conv=w_conv, a_log=a_log, dt_bias=dt_bias, g_onorm=g_onorm, gm_ln_g=gm_ln_g, gm_ln_b=gm_ln_b, w_sp=w_sp, b_sp=b_sp, w_pa=w_pa, w_pb=w_pb, w_out=w_out, loss_target=loss_target, m_c_ctx=m_c_ctx, m_w_mod=m_w_mod, m_b_mod=m_b_mod, m_g_pre=m_g_pre, m_g_post=m_g_post, m_w_in=m_w_in, m_w_conv=m_w_conv, m_a_log=m_a_log, m_dt_bias=m_dt_bias, m_g_onorm=m_g_onorm, m_gm_ln_g=m_gm_ln_g, m_gm_ln_b=m_gm_ln_b, m_w_sp=m_w_sp, m_b_sp=m_b_sp, m_w_pa=m_w_pa, m_w_pb=m_w_pb, m_w_out=m_w_out, v_c_ctx=v_c_ctx, v_w_mod=v_w_mod, v_b_mod=v_b_mod, v_g_pre=v_g_pre, v_g_post=v_g_post, v_w_in=v_w_in, v_w_conv=v_w_conv, v_a_log=v_a_log, v_dt_bias=v_dt_bias, v_g_onorm=v_g_onorm, v_gm_ln_g=v_gm_ln_g, v_gm_ln_b=v_gm_ln_b, v_w_sp=v_w_sp, v_b_sp=v_b_sp, v_w_pa=v_w_pa, v_w_pb=v_w_pb, v_w_out=v_w_out)
    weights = {n: given[n] for n in TWIN_WEIGHTS}
    shared = {n: given[n] for n in SHARED_INPUTS}
    per_example = {n: given[n] for n in ['x', 'c', 'ctx']}
    grad_fn = _jax.value_and_grad(_loss, argnums=(0, 1))

    def one_microbatch(ex, loss_target):
        ex = dict(ex)
        diff = ex.pop(TWIN_DIFF_INPUT)
        return grad_fn(weights, diff, {**shared, **ex}, loss_target)

    if N_MICROBATCH == 1:
        loss, (grad_w, grad_x) = one_microbatch(per_example, given["loss_target"])
    else:
        def body(carry, xs):
            loss_sum, grad_sum = carry
            l_k, (gw_k, gx_k) = one_microbatch(xs[0], xs[1])
            with _jax.named_scope("update"):
                return (loss_sum + l_k, _jax.tree.map(_jnp.add, grad_sum, gw_k)), gx_k

        init = (_jnp.zeros((), _jnp.float32), _jax.tree.map(_jnp.zeros_like, weights))
        (loss, grad_w), grad_x = _jax.lax.scan(body, init, (per_example, given["loss_target"]))
    with _jax.named_scope("update"):
        delta_w, new_m, new_v = {}, {}, {}
        for n in TWIN_WEIGHTS:
            delta_w[n], new_m[n], new_v[n] = _adamw(weights[n], grad_w[n], given["m_" + n], given["v_" + n])
    return (loss, grad_x, *[grad_w[n] for n in TWIN_WEIGHTS], *[delta_w[n] for n in TWIN_WEIGHTS],
            *[new_m[n] for n in TWIN_WEIGHTS], *[new_v[n] for n in TWIN_WEIGHTS])
```

```python
import functools

import jax
import jax.numpy as jnp
from jax import lax
from jax.experimental import pallas as pl
from jax.experimental.pallas import tpu as pltpu

F32 = jnp.float32
BF16 = jnp.bfloat16
HI = lax.Precision.HIGHEST
MESH = pl.DeviceIdType.MESH

D = 1024
H = 8
DH = 128
CH = 64
GM = 128
EPS = 1e-6
N_CHIPS = 4
OFF_A = 3 * D
OFF_ZB = OFF_A + 4 * H
IN_COLS = OFF_ZB + 6 * D
VMEM_LIMIT_V7X = 56 * 1024 * 1024

ADAM_LR, ADAM_B1, ADAM_B2, ADAM_EPS, ADAM_WD, ADAM_STEP = 0.001, 0.9, 0.999, 1e-08, 0.01, 10

SMALL_LAYOUT = {}
_off = 0
for _n, _s in (("c_ctx", D), ("b_mod", 3 * D), ("g_pre", D), ("g_post", D), ("gm_ln_g", D), ("gm_ln_b", D),
               ("b_sp", H * GM), ("g_onorm", DH), ("a_log", 2 * H), ("dt_bias", 2 * H), ("_pad", 96),
               ("w_conv", 3 * 3 * D), ("w_sp", H * GM * GM)):
    SMALL_LAYOUT[_n] = (_off, _s)
    _off += _s
SMALL_ROWS = 304
assert N_CHIPS * SMALL_ROWS * 128 >= _off and (SMALL_ROWS // 2) % 8 == 0


def _params(sem=None):
    return pltpu.CompilerParams(dimension_semantics=sem, vmem_limit_bytes=VMEM_LIMIT_V7X)


def _tile(n, cands=(256, 128, 64, 32, 16, 8)):
    for cand in cands:
        if n % cand == 0:
            return cand
    return n


def _silu(x):
    return x * jax.nn.sigmoid(x)


def _gelu(x):
    return 0.5 * x * (1.0 + jnp.tanh(0.7978845608028654 * (x + 0.044715 * (x * x * x))))


def _mm(a, b, *, ta=False, tb=False, out_dtype=F32, tm=512, tn=1024, tk=1024, pre_silu=False, name):
    m, k = (a.shape[1], a.shape[0]) if ta else a.shape
    n = b.shape[0] if tb else b.shape[1]
    tm, tn, tk = min(tm, m), min(tn, n), min(tk, k)
    assert m % tm == 0 and n % tn == 0 and k % tk == 0, (name, m, n, k, tm, tn, tk)
    nk = k // tk
    dims = (((0,) if ta else (1,), (1,) if tb else (0,)), ((), ()))

    def body(a_ref, b_ref, o_ref, acc_ref):
        kk = pl.program_id(2)
        av = a_ref[...]
        if pre_silu:
            av = _silu(av.astype(F32))
        part = lax.dot_general(av.astype(BF16), b_ref[...].astype(BF16), dims, preferred_element_type=F32)

        @pl.when(kk == 0)
        def _():
            acc_ref[...] = part

        @pl.when(kk > 0)
        def _():
            acc_ref[...] += part

        @pl.when(kk == nk - 1)
        def _():
            o_ref[...] = acc_ref[...].astype(out_dtype)

    a_spec = pl.BlockSpec((tk, tm), lambda i, j, q: (q, i)) if ta else pl.BlockSpec((tm, tk), lambda i, j, q: (i, q))
    b_spec = pl.BlockSpec((tn, tk), lambda i, j, q: (j, q)) if tb else pl.BlockSpec((tk, tn), lambda i, j, q: (q, j))
    return pl.pallas_call(
        body, name=name, grid=(m // tm, n // tn, nk),
        in_specs=[a_spec, b_spec], out_specs=pl.BlockSpec((tm, tn), lambda i, j, q: (i, j)),
        out_shape=jax.ShapeDtypeStruct((m, n), out_dtype),
        scratch_shapes=[pltpu.VMEM((tm, tn), F32)],
        compiler_params=_params(("parallel", "parallel", "arbitrary")),
    )(a, b)


def _h_fn(x, g, m):
    shift, scale = m[:, 0:D], m[:, D:2 * D]
    r = lax.rsqrt(jnp.mean(x * x, axis=-1, keepdims=True) + EPS)
    return (x * r * g) * (1.0 + scale) + shift


def _norm_fwd(x, g, mod, bmod, *, name):
    rows = x.shape[0]
    t = min(512, rows)

    def body(x_ref, g_ref, m_ref, b_ref, h_ref):
        h_ref[...] = _h_fn(x_ref[...], g_ref[...], m_ref[...] + b_ref[...]).astype(BF16)

    vec = lambda w: pl.BlockSpec((1, w), lambda i: (0, 0))
    return pl.pallas_call(
        body, name=name, grid=(rows // t,),
        in_specs=[pl.BlockSpec((t, D), lambda i: (i, 0)), vec(D), vec(3 * D), vec(3 * D)],
        out_specs=pl.BlockSpec((t, D), lambda i: (i, 0)),
        out_shape=jax.ShapeDtypeStruct((rows, D), BF16),
        compiler_params=_params(("parallel",)),
    )(x, g, mod, bmod)


def _norm_bwd(x, g, mod, bmod, dh_parts, resid, *, name):
    rows = x.shape[0]
    t = min(512, rows)
    n_parts = len(dh_parts)
    has_resid = resid is not None

    def body(*refs):
        x_ref, g_ref, m_ref, b_ref = refs[:4]
        parts = refs[4:4 + n_parts]
        r_ref = refs[4 + n_parts] if has_resid else None
        dx_ref, dg_ref, dm_ref = refs[-3:]
        i = pl.program_id(0)
        dh = parts[0][...]
        for p in parts[1:]:
            dh = dh + p[...]
        _, vjp = jax.vjp(_h_fn, x_ref[...], g_ref[...], m_ref[...] + b_ref[...])
        dx, dg, dm = vjp(dh)
        if has_resid:
            dx = dx + r_ref[...]
        dx_ref[...] = dx

        @pl.when(i == 0)
        def _():
            dg_ref[...] = dg
            dm_ref[...] = dm

        @pl.when(i > 0)
        def _():
            dg_ref[...] += dg
            dm_ref[...] += dm

    vec = lambda w: pl.BlockSpec((1, w), lambda i: (0, 0))
    tile = pl.BlockSpec((t, D), lambda i: (i, 0))
    ins = [x, g, mod, bmod, *dh_parts] + ([resid] if has_resid else [])
    return pl.pallas_call(
        body, name=name, grid=(rows // t,),
        in_specs=[tile, vec(D), vec(3 * D), vec(3 * D)] + [tile] * (n_parts + int(has_resid)),
        out_specs=[tile, vec(D), vec(3 * D)],
        out_shape=[jax.ShapeDtypeStruct((rows, D), F32), jax.ShapeDtypeStruct((1, D), F32),
                   jax.ShapeDtypeStruct((1, 3 * D), F32)],
        compiler_params=_params(("arbitrary",)),
    )(*ins)


def _conv_tile(u_ref, r0, t, rows, w0, w1, w2):
    u = u_ref[pl.ds(r0, t), :]
    prev8 = u_ref[pl.ds(pl.multiple_of(jnp.maximum(r0 - 8, 0), 8), 8), :]
    next8 = u_ref[pl.ds(pl.multiple_of(jnp.minimum(r0 + t, rows - 8), 8), 8), :]
    r8 = lax.broadcasted_iota(jnp.int32, (8, DH), 0)
    prev_row = jnp.sum(jnp.where(r8 == 7, prev8, 0.0), axis=0, keepdims=True)
    next_row = jnp.sum(jnp.where(r8 == 0, next8, 0.0), axis=0, keepdims=True)
    prev_row = jnp.where(r0 > 0, prev_row, 0.0)
    next_row = jnp.where(r0 + t < rows, next_row, 0.0)
    ri = lax.broadcasted_iota(jnp.int32, (t, DH), 0)
    um1 = jnp.where(ri == 0, prev_row, pltpu.roll(u, 1, 0))
    up1 = jnp.where(ri == t - 1, next_row, pltpu.roll(u, t - 1, 0))
    return w0 * um1 + w1 * u + w2 * up1, um1, u, up1


def _rowlocal(z, is_norm):
    y = _silu(z)
    yn = y * lax.rsqrt(jnp.sum(y * y, axis=-1, keepdims=True) + EPS)
    return jnp.where(is_norm, yn, y)


def _prep_fwd(p, wconv, n_norm, *, name):
    rows, nb = p.shape[0], p.shape[1] // DH
    t = min(512, rows)

    def body(u_ref, w_ref, o_ref):
        is_norm = pl.program_id(0) < n_norm
        w0, w1, w2 = w_ref[0:1, :], w_ref[1:2, :], w_ref[2:3, :]

        def step(s, carry):
            r0 = pl.multiple_of(s * t, t)
            z, _, _, _ = _conv_tile(u_ref, r0, t, rows, w0, w1, w2)
            o_ref[0, pl.ds(r0, t), :] = _rowlocal(z, is_norm)
            return carry

        lax.fori_loop(0, rows // t, step, 0)

    return pl.pallas_call(
        body, name=name, grid=(nb,),
        in_specs=[pl.BlockSpec((rows, DH), lambda j: (0, j)), pl.BlockSpec((3, DH), lambda j: (0, j))],
        out_specs=pl.BlockSpec((1, rows, DH), lambda j: (j, 0, 0)),
        out_shape=jax.ShapeDtypeStruct((nb, rows, DH), F32),
        compiler_params=_params(("parallel",)),
    )(p, wconv)


def _prep_bwd(p, wconv, d_a, d_b, n_norm, *, name):
    rows, nb = p.shape[0], p.shape[1] // DH
    t = min(512, rows)

    def body(u_ref, w_ref, da_ref, db_ref, du_ref, dw_ref, dz_ref):
        is_norm = pl.program_id(0) < n_norm
        w0, w1, w2 = w_ref[0:1, :], w_ref[1:2, :], w_ref[2:3, :]

        def step1(s, carry):
            a0, a1, a2 = carry
            r0 = pl.multiple_of(s * t, t)
            z, um1, u, up1 = _conv_tile(u_ref, r0, t, rows, w0, w1, w2)
            _, vjp = jax.vjp(lambda zz: _rowlocal(zz, is_norm), z)
            (dz,) = vjp(da_ref[0, pl.ds(r0, t), :] + db_ref[0, pl.ds(r0, t), :])
            dz_ref[pl.ds(r0, t), :] = dz
            red = lambda v: jnp.sum(v, axis=0, keepdims=True)
            return a0 + red(dz * um1), a1 + red(dz * u), a2 + red(dz * up1)

        zero = jnp.zeros((1, DH), F32)
        a0, a1, a2 = lax.fori_loop(0, rows // t, step1, (zero, zero, zero))
        dw_ref[0:1, :] = a0
        dw_ref[1:2, :] = a1
        dw_ref[2:3, :] = a2

        def step2(s, carry):
            r0 = pl.multiple_of(s * t, t)
            du, _, _, _ = _conv_tile(dz_ref, r0, t, rows, w2, w1, w0)
            du_ref[pl.ds(r0, t), :] = du.astype(BF16)
            return carry

        lax.fori_loop(0, rows // t, step2, 0)

    col = pl.BlockSpec((rows, DH), lambda j: (0, j))
    w_spec = pl.BlockSpec((3, DH), lambda j: (0, j))
    d_spec = pl.BlockSpec((1, rows, DH), lambda j: (j, 0, 0))
    return pl.pallas_call(
        body, name=name, grid=(nb,),
        in_specs=[col, w_spec, d_spec, d_spec], out_specs=[col, w_spec],
        out_shape=[jax.ShapeDtypeStruct((rows, nb * DH), BF16), jax.ShapeDtypeStruct((3, nb * DH), F32)],
        scratch_shapes=[pltpu.VMEM((rows, DH), F32)],
        compiler_params=_params(("parallel",)),
    )(p, wconv, d_a, d_b)


def _gates_fn(pab, avec, dvec):
    lane = lax.broadcasted_iota(jnp.int32, pab.shape, 1)
    xg = pab + dvec
    sp = jnp.maximum(xg, 0.0) + jnp.log(1.0 + jnp.exp(-jnp.abs(xg)))
    g = -jnp.exp(avec) * sp
    return jnp.where(lane < 2 * H, g, jnp.where(lane < 4 * H, jax.nn.sigmoid(pab), 0.0))


def _gates_fwd(pab, avec, dvec, *, name):
    rows = pab.shape[0]
    t = min(512, rows)

    def body(p_ref, a_ref, d_ref, o_ref):
        o_ref[...] = _gates_fn(p_ref[...], a_ref[...], d_ref[...])

    vec = pl.BlockSpec((1, DH), lambda i: (0, 0))
    tile = pl.BlockSpec((t, DH), lambda i: (i, 0))
    return pl.pallas_call(
        body, name=name, grid=(rows // t,), in_specs=[tile, vec, vec], out_specs=tile,
        out_shape=jax.ShapeDtypeStruct((rows, DH), F32), compiler_params=_params(("parallel",)),
    )(pab, avec, dvec)


def _gates_bwd(pab, avec, dvec, d_a, d_b, *, name):
    rows = pab.shape[0]
    t = min(512, rows)

    def body(p_ref, a_ref, d_ref, da_ref, db_ref, dp_ref, dav_ref, ddv_ref):
        i = pl.program_id(0)
        _, vjp = jax.vjp(_gates_fn, p_ref[...], a_ref[...], d_ref[...])
        dp, dav, ddv = vjp(da_ref[...] + db_ref[...])
        dp_ref[...] = dp.astype(BF16)

        @pl.when(i == 0)
        def _():
            dav_ref[...] = dav
            ddv_ref[...] = ddv

        @pl.when(i > 0)
        def _():
            dav_ref[...] += dav
            ddv_ref[...] += ddv

    vec = pl.BlockSpec((1, DH), lambda i: (0, 0))
    tile = pl.BlockSpec((t, DH), lambda i: (i, 0))
    return pl.pallas_call(
        body, name=name, grid=(rows // t,), in_specs=[tile, vec, vec, tile, tile], out_specs=[tile, vec, vec],
        out_shape=[jax.ShapeDtypeStruct((rows, DH), BF16), jax.ShapeDtypeStruct((1, DH), F32),
                   jax.ShapeDtypeStruct((1, DH), F32)],
        compiler_params=_params(("arbitrary",)),
    )(pab, avec, dvec, d_a, d_b)


def _dot(a, b, dims=((1,), (0,)), precision=None):
    return lax.dot_general(a, b, (dims, ((), ())), precision=precision, preferred_element_type=F32)


_NT = ((1,), (1,))
_TN = ((0,), (0,))


def _chunk(s, q, k, v, gcol, bcol, rev):
    ii = lax.broadcasted_iota(jnp.int32, (CH, CH), 0)
    jj = lax.broadcasted_iota(jnp.int32, (CH, CH), 1)
    incl = (ii <= jj) if rev else (ii >= jj)
    strict = (ii < jj) if rev else (ii > jj)
    eye = (ii == jj).astype(F32)
    g_b = jnp.broadcast_to(gcol, (CH, DH))
    gcum = _dot(incl.astype(F32), g_b, precision=HI)
    glast = jnp.sum(g_b, axis=0, keepdims=True)
    gc_i = _dot(incl.astype(F32), jnp.broadcast_to(gcol, (CH, CH)), precision=HI)
    gc_j = _dot(jnp.ones((CH, CH), F32), gc_i * eye, precision=HI)
    decay = jnp.where(incl, jnp.exp(jnp.where(incl, gc_i - gc_j, 0.0)), 0.0)
    b_b = jnp.broadcast_to(bcol, (CH, DH))
    kb = k * b_b
    a_mat = jnp.where(strict, _dot(kb, k, _NT) * decay, 0.0)
    bp = -a_mat
    inv = eye + bp
    for _ in range(5):
        bp = _dot(bp, bp, precision=HI)
        inv = inv + _dot(inv, bp, precision=HI)
    eg = jnp.exp(gcum)
    u_val = _dot(inv, v * b_b, precision=HI)
    w_key = _dot(inv, kb * eg, precision=HI)
    k_tail = k * jnp.exp(glast - gcum)
    v_new = u_val - _dot(w_key, s)
    s_new = s * jnp.exp(glast) + _dot(k_tail, v_new, _TN)
    if q is None:
        return s_new, None
    qc = q * (DH ** -0.5)
    attn = _dot(qc, k, _NT) * decay
    o = _dot(qc * eg, s) + _dot(attn, v_new)
    return s_new, o


def _lane_col(tile, idx):
    lane = lax.broadcasted_iota(jnp.int32, tile.shape, 1)
    return jnp.sum(jnp.where(lane == idx, tile, 0.0), axis=1, keepdims=True)


def _gdn_fwd(qkv, gb, s0f, s0b, has_q, *, name):
    nb, rows, _ = qkv.shape
    n = rows // CH
    qoff = H if has_q else 0

    def body(qf_ref, qb_ref, gf_ref, gr_ref, s0f_ref, s0b_ref, of_ref, ob_ref, ssf_ref, ssb_ref, sf_ref, sb_ref):
        @pl.when(pl.program_id(0) == 0)
        def _():
            sf_ref[...] = s0f_ref[...]
            sb_ref[...] = s0b_ref[...]

        gtiles = (gf_ref[...], gr_ref[...])

        def head(h, carry):
            for d, (q_ref, s_ref, ss_ref, o_ref) in enumerate(((qf_ref, sf_ref, ssf_ref, of_ref),
                                                              (qb_ref, sb_ref, ssb_ref, ob_ref))):
                gcol = _lane_col(gtiles[d], d * H + h)
                bcol = _lane_col(gtiles[d], 2 * H + d * H + h)
                s = s_ref[h]
                ss_ref[0, h] = s
                q = q_ref[h] if has_q else None
                s_new, o = _chunk(s, q, q_ref[qoff + h], q_ref[qoff + H + h], gcol, bcol, rev=(d == 1))
                s_ref[h] = s_new
                o_ref[h] = o if has_q else jnp.zeros((CH, DH), F32)
            return carry

        lax.fori_loop(0, H, head, 0)

    fwd3 = lambda i: (0, i, 0)
    rev3 = lambda i: (0, n - 1 - i, 0)
    state = pl.BlockSpec((H, DH, DH), lambda i: (0, 0, 0))
    saved = pl.BlockSpec((1, H, DH, DH), lambda i: (i, 0, 0, 0))
    return pl.pallas_call(
        body, name=name, grid=(n,),
        in_specs=[pl.BlockSpec((nb, CH, DH), fwd3), pl.BlockSpec((nb, CH, DH), rev3),
                  pl.BlockSpec((CH, DH), lambda i: (i, 0)), pl.BlockSpec((CH, DH), lambda i: (n - 1 - i, 0)),
                  state, state],
        out_specs=[pl.BlockSpec((H, CH, DH), fwd3), pl.BlockSpec((H, CH, DH), rev3), saved, saved, state, state],
        out_shape=[jax.ShapeDtypeStruct((H, rows, DH), F32)] * 2 + [jax.ShapeDtypeStruct((n, H, DH, DH), F32)] * 2
        + [jax.ShapeDtypeStruct((H, DH, DH), F32)] * 2,
        compiler_params=_params(("arbitrary",)),
    )(qkv, qkv, gb, gb, s0f, s0b)


def _gdn_bwd(qkv, gb, ssf, ssb, do, dsf, dsb, has_q, *, name):
    nb, rows, _ = qkv.shape
    n = rows // CH
    qoff = H if has_q else 0

    def body(qf_ref, qb_ref, gf_ref, gr_ref, ssf_ref, ssb_ref, dof_ref, dob_ref, dsf0_ref, dsb0_ref,
             dqf_ref, dqb_ref, dgf_ref, dgr_ref, dsf_ref, dsb_ref):
        @pl.when(pl.program_id(0) == 0)
        def _():
            dsf_ref[...] = dsf0_ref[...]
            dsb_ref[...] = dsb0_ref[...]

        gtiles = (gf_ref[...], gr_ref[...])
        lane = lax.broadcasted_iota(jnp.int32, (CH, DH), 1)

        def head(h, carry):
            out = []
            for d, (q_ref, ss_ref, do_ref, ds_ref, dq_ref) in enumerate((
                    (qf_ref, ssf_ref, dof_ref, dsf_ref, dqf_ref), (qb_ref, ssb_ref, dob_ref, dsb_ref, dqb_ref))):
                gi, bi = d * H + h, 2 * H + d * H + h
                gcol, bcol = _lane_col(gtiles[d], gi), _lane_col(gtiles[d], bi)
                s = ss_ref[0, h]
                k, v = q_ref[qoff + h], q_ref[qoff + H + h]
                if has_q:
                    fn = lambda s_, q_, k_, v_, g_, b_: _chunk(s_, q_, k_, v_, g_, b_, rev=(d == 1))
                    _, vjp = jax.vjp(fn, s, q_ref[h], k, v, gcol, bcol)
                    ds, dq, dk, dv, dg, db = vjp((ds_ref[h], do_ref[h]))
                    dq_ref[h] = dq
                else:
                    fn = lambda s_, k_, v_, g_, b_: _chunk(s_, None, k_, v_, g_, b_, rev=(d == 1))[0]
                    _, vjp = jax.vjp(fn, s, k, v, gcol, bcol)
                    ds, dk, dv, dg, db = vjp(ds_ref[h])
                ds_ref[h] = ds
                dq_ref[qoff + h] = dk
                dq_ref[qoff + H + h] = dv
                out.append(carry[d] + jnp.where(lane == gi, dg, 0.0) + jnp.where(lane == bi, db, 0.0))
            return tuple(out)

        zero = jnp.zeros((CH, DH), F32)
        dgf, dgr = lax.fori_loop(0, H, head, (zero, zero))
        dgf_ref[...] = dgf
        dgr_ref[...] = dgr

    fwd3 = lambda i: (0, n - 1 - i, 0)
    rev3 = lambda i: (0, i, 0)
    state = pl.BlockSpec((H, DH, DH), lambda i: (0, 0, 0))
    saved = pl.BlockSpec((1, H, DH, DH), lambda i: (n - 1 - i, 0, 0, 0))
    gf_spec = pl.BlockSpec((CH, DH), lambda i: (n - 1 - i, 0))
    gr_spec = pl.BlockSpec((CH, DH), lambda i: (i, 0))
    return pl.pallas_call(
        body, name=name, grid=(n,),
        in_specs=[pl.BlockSpec((nb, CH, DH), fwd3), pl.BlockSpec((nb, CH, DH), rev3), gf_spec, gr_spec, saved, saved,
                  pl.BlockSpec((H, CH, DH), fwd3), pl.BlockSpec((H, CH, DH), rev3), state, state],
        out_specs=[pl.BlockSpec((nb, CH, DH), fwd3), pl.BlockSpec((nb, CH, DH), rev3), gf_spec, gr_spec, state, state],
        out_shape=[jax.ShapeDtypeStruct((nb, rows, DH), F32)] * 2 + [jax.ShapeDtypeStruct((rows, DH), F32)] * 2
        + [jax.ShapeDtypeStruct((H, DH, DH), F32)] * 2,
        compiler_params=_params(("arbitrary",)),
    )(qkv, qkv, gb, gb, ssf, ssb, do, do, dsf, dsb)


def _stage1(zb, ua, va, za, o, gon, lng, lnb, wsp, bsp):
    gv = [_gelu(t) for t in va]
    mu = sum(jnp.sum(t, axis=-1, keepdims=True) for t in gv) * (1.0 / D)
    xc = [t - mu for t in gv]
    var = sum(jnp.sum(t * t, axis=-1, keepdims=True) for t in xc) * (1.0 / D)
    rs = lax.rsqrt(var + EPS)
    ya, yb = [], []
    for g in range(H):
        vv = xc[g] * rs * lng[g] + lnb[g]
        s = _dot(wsp[g], vv) + bsp[g]
        ya.append(_gelu(ua[g]) * s * _silu(za[g]))
        r = lax.rsqrt(jnp.mean(o[g] * o[g], axis=-1, keepdims=True) + EPS)
        yb.append(o[g] * r * gon * _silu(zb[g]))
    return ya, yb


def _stage2(ma, mb, ga, gb):
    return jax.nn.sigmoid(ga) * ma + jax.nn.sigmoid(gb) * mb


def _stage3(out, x, tgt, gpost, gate):
    r = out * lax.rsqrt(jnp.mean(out * out, axis=-1, keepdims=True) + EPS) * gpost
    err = x + gate * r - tgt
    return 0.5 * jnp.sum(jnp.mean(err * err, axis=-1, keepdims=True), axis=0, keepdims=True)


def _post(p_rest, o_f, o_b, x, tgt, mod, bmod, gon, lng, lnb, wsp, bspb, wpa, wpb, wout, gpost, *, name):
    rows = x.shape[0]
    n = rows // GM
    lanes = lambda g: slice(g * DH, (g + 1) * DH)
    bdot = lambda a, w_ref: _dot(a.astype(BF16), w_ref[...])
    bdot_t = lambda a, w_ref: _dot(a.astype(BF16), w_ref[...], _NT)

    def body(p_ref, of_ref, ob_ref, x_ref, t_ref, m_ref, bm_ref, gon_ref, lng_ref, lnb_ref, wsp_ref, bsp_ref,
             wpa_ref, wpb_ref, wout_ref, gp_ref,
             loss_ref, dp_ref, do_ref, dx_ref, ya_ref, yb_ref, mg_ref, dma_ref, dmb_ref, dout_ref,
             dvec_ref, dgon_ref, dwsp_ref, dbsp_ref):
        @pl.when(pl.program_id(0) == 0)
        def _():
            loss_ref[...] = jnp.zeros_like(loss_ref)
            dvec_ref[...] = jnp.zeros_like(dvec_ref)
            dgon_ref[...] = jnp.zeros_like(dgon_ref)
            dwsp_ref[...] = jnp.zeros_like(dwsp_ref)
            dbsp_ref[...] = jnp.zeros_like(dbsp_ref)

        piece = lambda blk: [p_ref[:, blk * D + g * DH: blk * D + (g + 1) * DH] for g in range(H)]
        zb, ua, va, za = piece(0), piece(1), piece(2), piece(3)
        o = [of_ref[g] + ob_ref[g] for g in range(H)]
        gon = gon_ref[...]
        lng = [lng_ref[:, lanes(g)] for g in range(H)]
        lnb = [lnb_ref[:, lanes(g)] for g in range(H)]
        wsp = [wsp_ref[g] for g in range(H)]
        bsp = [bsp_ref[g] for g in range(H)]
        (ya, yb), vjp1 = jax.vjp(_stage1, zb, ua, va, za, o, gon, lng, lnb, wsp, bsp)
        y_a, y_b = jnp.concatenate(ya, axis=1), jnp.concatenate(yb, axis=1)
        ma, mb = bdot(y_a, wpa_ref), bdot(y_b, wpb_ref)
        ga, gb = p_ref[:, 4 * D:5 * D], p_ref[:, 5 * D:6 * D]
        merged, vjp2 = jax.vjp(_stage2, ma, mb, ga, gb)
        out = bdot(merged, wout_ref)
        gate = m_ref[:, 2 * D:3 * D] + bm_ref[:, 2 * D:3 * D]
        loss, vjp3 = jax.vjp(_stage3, out, x_ref[...], t_ref[...], gp_ref[...], gate)
        loss_ref[...] += jnp.broadcast_to(loss, loss_ref.shape)

        dout, dx, _, dgpost, dgate = vjp3(jnp.ones((1, 1), F32))
        dx_ref[...] = dx
        dmerged = bdot_t(dout, wout_ref)
        dma, dmb, dga, dgb = vjp2(dmerged)
        dya, dyb = bdot_t(dma, wpa_ref), bdot_t(dmb, wpb_ref)
        dzb, dua, dva, dza, do, dgon, dlng, dlnb, dwsp, dbsp = vjp1(
            ([dya[:, lanes(g)] for g in range(H)], [dyb[:, lanes(g)] for g in range(H)]))

        for blk, dlist in enumerate((dzb, dua, dva, dza)):
            for g in range(H):
                dp_ref[:, blk * D + g * DH: blk * D + (g + 1) * DH] = dlist[g].astype(BF16)
        dp_ref[:, 4 * D:5 * D] = dga.astype(BF16)
        dp_ref[:, 5 * D:6 * D] = dgb.astype(BF16)
        for g in range(H):
            do_ref[g] = do[g]
            dwsp_ref[g] += dwsp[g]
            dbsp_ref[g] += dbsp[g]
            dvec_ref[2:3, lanes(g)] += dlng[g]
            dvec_ref[3:4, lanes(g)] += dlnb[g]
        dvec_ref[0:1, :] += dgpost
        dvec_ref[1:2, :] += dgate
        dgon_ref[0:1, :] += dgon
        ya_ref[...] = y_a.astype(BF16)
        yb_ref[...] = y_b.astype(BF16)
        mg_ref[...] = merged.astype(BF16)
        dma_ref[...] = dma.astype(BF16)
        dmb_ref[...] = dmb.astype(BF16)
        dout_ref[...] = dout.astype(BF16)

    row = lambda w: pl.BlockSpec((GM, w), lambda i: (i, 0))
    heads = pl.BlockSpec((H, GM, DH), lambda i: (0, i, 0))
    full = lambda shape: pl.BlockSpec(shape, lambda i: tuple(0 for _ in shape))
    sds = jax.ShapeDtypeStruct
    return pl.pallas_call(
        body, name=name, grid=(n,),
        in_specs=[row(6 * D), heads, heads, row(D), row(D), full((1, 3 * D)), full((1, 3 * D)), full((1, DH)),
                  full((1, D)), full((1, D)), full((H, GM, GM)), full((H, GM, GM)),
                  full((D, D)), full((D, D)), full((D, D)), full((1, D))],
        out_specs=[full((8, DH)), row(6 * D), heads, row(D)] + [row(D)] * 6
        + [full((8, D)), full((8, DH)), full((H, GM, GM)), full((H, GM, GM))],
        out_shape=[sds((8, DH), F32), sds((rows, 6 * D), BF16), sds((H, rows, DH), F32), sds((rows, D), F32)]
        + [sds((rows, D), BF16)] * 6 + [sds((8, D), F32), sds((8, DH), F32), sds((H, GM, GM), F32), sds((H, GM, GM), F32)],
        compiler_params=_params(("arbitrary",)),
    )(p_rest, o_f, o_b, x, tgt, mod, bmod, gon, lng, lnb, wsp, bspb, wpa, wpb, wout, gpost)


def _dsilu_mul(c, d, *, name):
    def body(c_ref, d_ref, o_ref):
        _, vjp = jax.vjp(_silu, c_ref[...])
        (o_ref[...],) = vjp(d_ref[...])

    return pl.pallas_call(body, name=name, out_shape=jax.ShapeDtypeStruct(c.shape, F32))(c, d)


def _adamw(w, g, m, v, *, g_row_block=0, name):
    rows, cols = w.shape
    t = _tile(rows)
    gb = rows // t * g_row_block
    c1 = 1.0 / (1.0 - ADAM_B1 ** ADAM_STEP)
    c2 = 1.0 / (1.0 - ADAM_B2 ** ADAM_STEP)

    def body(w_ref, g_ref, m_ref, v_ref, go_ref, d_ref, mo_ref, vo_ref):
        gv = g_ref[...]
        mn = ADAM_B1 * m_ref[...] + (1.0 - ADAM_B1) * gv
        vn = ADAM_B2 * v_ref[...] + (1.0 - ADAM_B2) * (gv * gv)
        go_ref[...] = gv
        mo_ref[...] = mn
        vo_ref[...] = vn
        d_ref[...] = -ADAM_LR * ((mn * c1) / (jnp.sqrt(vn * c2) + ADAM_EPS) + ADAM_WD * w_ref[...])

    tile = pl.BlockSpec((t, cols), lambda i: (i, 0))
    return pl.pallas_call(
        body, name=name, grid=(rows // t,),
        in_specs=[tile, pl.BlockSpec((t, cols), lambda i: (gb + i, 0)), tile, tile], out_specs=[tile] * 4,
        out_shape=[jax.ShapeDtypeStruct((rows, cols), F32)] * 4,
        compiler_params=_params(("parallel",)),
    )(w, g, m, v)


def _add2(a, b, *, name):
    s, rows, cols = a.shape
    t = _tile(rows)

    def body(a_ref, b_ref, o_ref):
        o_ref[...] = a_ref[...] + b_ref[...]

    tile = pl.BlockSpec((1, t, cols), lambda j, i: (j, i, 0))
    return pl.pallas_call(
        body, name=name, grid=(s, rows // t), in_specs=[tile, tile], out_specs=tile,
        out_shape=jax.ShapeDtypeStruct(a.shape, F32), compiler_params=_params(("parallel", "parallel")),
    )(a, b)


def _sum4(a, *, name):
    _, rows, cols = a.shape
    t = _tile(rows)

    def body(a_ref, o_ref):
        o_ref[...] = ((a_ref[0] + a_ref[1]) + a_ref[2]) + a_ref[3]

    return pl.pallas_call(
        body, name=name, grid=(rows // t,), in_specs=[pl.BlockSpec((N_CHIPS, t, cols), lambda i: (0, i, 0))],
        out_specs=pl.BlockSpec((t, cols), lambda i: (i, 0)),
        out_shape=jax.ShapeDtypeStruct((rows, cols), F32), compiler_params=_params(("parallel",)),
    )(a)


def _local_step(x, c, ctx, c_ctx, tgt, w_mod, b_mod, g_pre, g_post, w_qkv, w_ab, w_rest, w_conv, a_log, dt_bias,
                g_onorm, gm_ln_g, gm_ln_b, w_sp, b_sp, w_pa, w_pb, w_out):
    rows, rows_c = x.shape[0], ctx.shape[0]
    cc = jnp.zeros((16, D), F32).at[0].set(c[0]).at[1].set(c_ctx)
    mod = _mm(cc, w_mod, pre_silu=True, name="mod_fwd")
    mod_x, mod_c = mod[0:1], mod[1:2]
    avec = jnp.zeros((1, DH), F32).at[0, :2 * H].set(a_log.reshape(-1))
    dvec = jnp.zeros((1, DH), F32).at[0, :2 * H].set(dt_bias.reshape(-1))
    bspb = jnp.broadcast_to(b_sp[:, :, None], (H, GM, GM))
    w_kv, wconv_kv = w_qkv[:, D:], w_conv[:, D:]

    h_c = _norm_fwd(ctx, g_pre, mod_c, b_mod, name="norm_fwd_ctx")
    pc_kv = _mm(h_c, w_kv, name="inproj_ctx_kv")
    pc_ab = _mm(h_c, w_ab, name="inproj_ctx_ab")
    kv_c = _prep_fwd(pc_kv, wconv_kv, H, name="prep_fwd_ctx")
    gb_c = _gates_fwd(pc_ab, avec, dvec, name="gates_fwd_ctx")
    s_zero = jnp.zeros((H, DH, DH), F32)
    _, _, ssf_c, ssb_c, s_f, s_b = _gdn_fwd(kv_c, gb_c, s_zero, s_zero, False, name="gdn_fwd_ctx")

    h_x = _norm_fwd(x, g_pre, mod_x, b_mod, name="norm_fwd_x")
    p_qkv = _mm(h_x, w_qkv, name="inproj_qkv")
    p_ab = _mm(h_x, w_ab, name="inproj_ab")
    p_rest = _mm(h_x, w_rest, name="inproj_rest")
    qkv = _prep_fwd(p_qkv, w_conv, 2 * H, name="prep_fwd_x")
    gb_x = _gates_fwd(p_ab, avec, dvec, name="gates_fwd_x")
    o_f, o_b, ssf, ssb, _, _ = _gdn_fwd(qkv, gb_x, s_f, s_b, True, name="gdn_fwd_x")

    (loss_acc, dp_rest, do, dx_res, ya, yb, mg, dma, dmb, dout, dvec_post, dgon, dwsp, dbspb) = _post(
        p_rest, o_f, o_b, x, tgt, mod_x, b_mod, g_onorm, gm_ln_g, gm_ln_b, w_sp, bspb, w_pa, w_pb, w_out, g_post,
        name="post")
    g = {}
    g["w_pa"] = _mm(ya, dma, ta=True, name="dw_pa")
    g["w_pb"] = _mm(yb, dmb, ta=True, name="dw_pb")
    g["w_out"] = _mm(mg, dout, ta=True, name="dw_out")

    zeros_s = jnp.zeros((H, DH, DH), F32)
    dq_f, dq_b, dg_f, dg_b, ds0_f, ds0_b = _gdn_bwd(qkv, gb_x, ssf, ssb, do, zeros_s, zeros_s, True, name="gdn_bwd_x")
    dp_qkv, dwc_x = _prep_bwd(p_qkv, w_conv, dq_f, dq_b, 2 * H, name="prep_bwd_x")
    dp_ab, dav_x, ddv_x = _gates_bwd(p_ab, avec, dvec, dg_f, dg_b, name="gates_bwd_x")
    dkv_f, dkv_b, dgc_f, dgc_b, _, _ = _gdn_bwd(kv_c, gb_c, ssf_c, ssb_c, jnp.zeros((H, rows_c, DH), F32), ds0_f, ds0_b,
                                                 False, name="gdn_bwd_ctx")
    dpc_kv, dwc_c = _prep_bwd(pc_kv, wconv_kv, dkv_f, dkv_b, H, name="prep_bwd_ctx")
    dpc_ab, dav_c, ddv_c = _gates_bwd(pc_ab, avec, dvec, dgc_f, dgc_b, name="gates_bwd_ctx")

    h_all = jnp.concatenate([h_x, h_c], axis=0)
    dpc_qkv = jnp.concatenate([jnp.zeros((rows_c, D), BF16), dpc_kv], axis=1)
    tk = _tile(rows + rows_c, (512, 256, 128, 64))
    dw_qkv = _mm(h_all, jnp.concatenate([dp_qkv, dpc_qkv], axis=0), ta=True, tk=tk, name="dw_qkv")
    dw_ab = _mm(h_all, jnp.concatenate([dp_ab, dpc_ab], axis=0), ta=True, tk=tk, name="dw_ab")
    dw_rest = _mm(h_x, dp_rest, ta=True, tk=512, name="dw_rest")
    g["w_in"] = jnp.concatenate([dw_qkv, dw_ab[:, :4 * H], dw_rest], axis=1)
    dh_parts = [_mm(dp_qkv, w_qkv, tb=True, name="dh_qkv"), _mm(dp_rest, w_rest, tb=True, name="dh_rest"),
                _mm(dp_ab, w_ab, tb=True, name="dh_ab")]
    grad_x, dgpre_x, dm_x = _norm_bwd(x, g_pre, mod_x, b_mod, dh_parts, dx_res, name="norm_bwd_x")
    dhc_parts = [_mm(dpc_kv, w_kv, tb=True, name="dhc_kv"), _mm(dpc_ab, w_ab, tb=True, name="dhc_ab")]
    _, dgpre_c, dm_c = _norm_bwd(ctx, g_pre, mod_c, b_mod, dhc_parts, None, name="norm_bwd_ctx")

    dm_x = dm_x.at[:, 2 * D:].add(dvec_post[1:2])
    dmod = jnp.zeros((16, 3 * D), F32).at[0].set(dm_x[0]).at[1].set(dm_c[0])
    g["w_mod"] = _mm(cc, dmod, ta=True, pre_silu=True, name="dw_mod")
    dcc = _mm(dmod, w_mod, tb=True, name="dcc")
    g["c_ctx"] = _dsilu_mul(cc[:8], dcc[:8], name="dc_ctx")[1]
    g["b_mod"] = dm_x + dm_c
    g["g_pre"] = dgpre_x + dgpre_c
    g["g_post"] = dvec_post[0:1]
    g["gm_ln_g"], g["gm_ln_b"] = dvec_post[2:3], dvec_post[3:4]
    g["g_onorm"] = dgon[0:1]
    g["w_sp"] = dwsp
    g["b_sp"] = jnp.sum(dbspb, axis=-1)
    g["w_conv"] = dwc_x.at[:, D:].add(dwc_c)
    g["a_log"] = (dav_x + dav_c)[0, :2 * H].reshape(2, H)
    g["dt_bias"] = (ddv_x + ddv_c)[0, :2 * H].reshape(2, H)
    return loss_acc[0, 0], grad_x, g


ANY = pl.BlockSpec(memory_space=pl.ANY)


def _place():
    x, y, c = lax.axis_index("x"), lax.axis_index("y"), lax.axis_index("c")
    chips = [(1 - x, y), (x, 1 - y), (1 - x, 1 - y)]
    return x, y, c, (x, y, 1 - c), chips


def _gather_shards(big, small, *, name):
    nb = len(big)

    def body(*refs):
        ins, sm_in = refs[:nb], refs[nb]
        outs, sm_out = refs[nb + 1:2 * nb + 1], refs[2 * nb + 1]
        send, recv, lsem = refs[2 * nb + 2:]
        x, y, c, sibling, chips = _place()
        mine = 2 * x + y
        local = [pltpu.make_async_copy(ins[a], outs[a].at[mine], lsem.at[a]) for a in range(nb)]
        local.append(pltpu.make_async_copy(sm_in, sm_out.at[mine], lsem.at[nb]))
        for cp in local:
            cp.start()

        def half(a, shard, hc):
            hr = big[a].shape[0] // 2
            return outs[a].at[shard, pl.ds(hc * hr, hr), :]

        def remote(k, src, dst, to):
            return pltpu.make_async_remote_copy(src_ref=src, dst_ref=dst, send_sem=send.at[k], recv_sem=recv.at[k],
                                                device_id=to, device_id_type=MESH)

        sends = []
        for a in range(nb):
            hr = big[a].shape[0] // 2
            for j, chip in enumerate(chips):
                sends.append(remote(a * 3 + j, ins[a].at[pl.ds(c * hr, hr), :], half(a, mine, c), (*chip, c)))
        for j, chip in enumerate(chips):
            sends.append(remote(nb * 3 + j, sm_in, sm_out.at[mine], (*chip, c)))
        for cp in sends:
            cp.start()
        base = nb * 3 + 3
        passed = []
        for a in range(nb):
            for j, (px, py) in enumerate(chips):
                theirs = 2 * px + py
                remote(a * 3 + j, half(a, theirs, c), half(a, theirs, c), sibling).wait_recv()
                fw = remote(base + a * 3 + j, half(a, theirs, c), half(a, theirs, c), sibling)
                fw.start()
                passed.append(fw)
        for a in range(nb):
            for j, (px, py) in enumerate(chips):
                theirs = 2 * px + py
                remote(base + a * 3 + j, half(a, theirs, 1 - c), half(a, theirs, 1 - c), sibling).wait_recv()
        for j, (px, py) in enumerate(chips):
            remote(nb * 3 + j, sm_in, sm_out.at[2 * px + py], sibling).wait_recv()
        for cp in sends + passed:
            cp.wait_send()
        for cp in local:
            cp.wait()

    n_remote = 2 * nb * 3 + 3
    outs = pl.pallas_call(
        body, name=name, in_specs=[ANY] * (nb + 1), out_specs=[ANY] * (nb + 1),
        out_shape=[jax.ShapeDtypeStruct((N_CHIPS,) + a.shape, a.dtype) for a in big + [small]],
        scratch_shapes=[pltpu.SemaphoreType.DMA((n_remote,)), pltpu.SemaphoreType.DMA((n_remote,)),
                        pltpu.SemaphoreType.DMA((nb + 1,))],
    )(*big, small)
    return outs[:nb], outs[nb]


def _exchange_halves(arrs, *, name):
    na = len(arrs)

    def body(*refs):
        ins, mine_out, got_out = refs[:na], refs[na:2 * na], refs[2 * na:3 * na]
        send, recv, lsem = refs[3 * na:]
        x, y, c, sibling, _ = _place()
        copies = []
        for a in range(na):
            hr = arrs[a].shape[1] // 2
            keep = pltpu.make_async_copy(ins[a].at[:, pl.ds(c * hr, hr), :], mine_out[a], lsem.at[a])
            give = pltpu.make_async_remote_copy(src_ref=ins[a].at[:, pl.ds((1 - c) * hr, hr), :], dst_ref=got_out[a],
                                                send_sem=send.at[a], recv_sem=recv.at[a], device_id=sibling,
                                                device_id_type=MESH)
            keep.start()
            give.start()
            copies.append((keep, give))
        for keep, give in copies:
            give.wait()
            keep.wait()

    half = [jax.ShapeDtypeStruct((N_CHIPS, a.shape[1] // 2, a.shape[2]), F32) for a in arrs]
    outs = pl.pallas_call(
        body, name=name, in_specs=[ANY] * na, out_specs=[ANY] * (2 * na), out_shape=half + half,
        scratch_shapes=[pltpu.SemaphoreType.DMA((na,)), pltpu.SemaphoreType.DMA((na,)), pltpu.SemaphoreType.DMA((na,))],
    )(*arrs)
    return outs[:na], outs[na:]


def _scatter_sections(arrs, *, name):
    na = len(arrs)

    def body(*refs):
        ins, outs = refs[:na], refs[na:2 * na]
        send, recv, lsem = refs[2 * na:]
        x, y, c, _, chips = _place()
        mine = 2 * x + y
        copies = []
        for a in range(na):
            keep = pltpu.make_async_copy(ins[a].at[mine], outs[a].at[mine], lsem.at[a])
            keep.start()
            copies.append(keep)
        sends = []
        for a in range(na):
            for j, (px, py) in enumerate(chips):
                cp = pltpu.make_async_remote_copy(src_ref=ins[a].at[2 * px + py], dst_ref=outs[a].at[mine],
                                                  send_sem=send.at[a * 3 + j], recv_sem=recv.at[a * 3 + j],
                                                  device_id=(px, py, c), device_id_type=MESH)
                cp.start()
                sends.append(cp)
        for a in range(na):
            for j, (px, py) in enumerate(chips):
                pltpu.make_async_remote_copy(src_ref=ins[a].at[mine], dst_ref=outs[a].at[2 * px + py],
                                             send_sem=send.at[a * 3 + j], recv_sem=recv.at[a * 3 + j],
                                             device_id=(px, py, c), device_id_type=MESH).wait_recv()
        for cp in sends:
            cp.wait_send()
        for cp in copies:
            cp.wait()

    return pl.pallas_call(
        body, name=name, in_specs=[ANY] * na, out_specs=[ANY] * na,
        out_shape=[jax.ShapeDtypeStruct(a.shape, F32) for a in arrs],
        scratch_shapes=[pltpu.SemaphoreType.DMA((3 * na,)), pltpu.SemaphoreType.DMA((3 * na,)),
                        pltpu.SemaphoreType.DMA((na,))],
    )(*arrs)


def _finish_reduce(big, small, *, name):
    nb = len(big)
    hs = small.shape[0]

    def body(*refs):
        ins, sm_in = refs[:nb], refs[nb]
        outs, sm_out = refs[nb + 1:2 * nb + 1], refs[2 * nb + 1]
        send, recv, lsem = refs[2 * nb + 2:]
        x, y, c, sibling, chips = _place()

        def remote(k, src, dst, to):
            return pltpu.make_async_remote_copy(src_ref=src, dst_ref=dst, send_sem=send.at[k], recv_sem=recv.at[k],
                                                device_id=to, device_id_type=MESH)

        def rows(a, hc):
            hr = big[a].shape[0]
            return outs[a].at[pl.ds(hc * hr, hr), :]

        def blk(px, py, pc):
            return sm_out.at[pl.ds((4 * px + 2 * py + pc) * hs, hs), :]

        local = [pltpu.make_async_copy(ins[a], rows(a, c), lsem.at[a]) for a in range(nb)]
        local.append(pltpu.make_async_copy(sm_in, blk(x, y, c), lsem.at[nb]))
        for cp in local:
            cp.start()
        first = [remote(a, ins[a], rows(a, c), sibling) for a in range(nb)]
        first.append(remote(nb, sm_in, blk(x, y, c), sibling))
        first += [remote(nb + 1 + j, sm_in, blk(x, y, c), (*chip, c)) for j, chip in enumerate(chips)]
        for cp in first:
            cp.start()
        passed = []
        for j, (px, py) in enumerate(chips):
            remote(nb + 1 + j, blk(px, py, c), blk(px, py, c), sibling).wait_recv()
            fw = remote(nb + 4 + j, blk(px, py, c), blk(px, py, c), sibling)
            fw.start()
            passed.append(fw)
        for a in range(nb):
            remote(a, ins[a], rows(a, 1 - c), sibling).wait_recv()
        remote(nb, sm_in, blk(x, y, 1 - c), sibling).wait_recv()
        for j, (px, py) in enumerate(chips):
            remote(nb + 4 + j, blk(px, py, 1 - c), blk(px, py, 1 - c), sibling).wait_recv()
        for cp in first + passed:
            cp.wait_send()
        for cp in local:
            cp.wait()

    n_remote = nb + 7
    outs = pl.pallas_call(
        body, name=name, in_specs=[ANY] * (nb + 1), out_specs=[ANY] * (nb + 1),
        out_shape=[jax.ShapeDtypeStruct((2 * a.shape[0], a.shape[1]), F32) for a in big]
        + [jax.ShapeDtypeStruct((8 * hs, small.shape[1]), F32)],
        scratch_shapes=[pltpu.SemaphoreType.DMA((n_remote,)), pltpu.SemaphoreType.DMA((n_remote,)),
                        pltpu.SemaphoreType.DMA((nb + 1,))],
    )(*big, small)
    return outs[:nb], outs[nb]


def _reduce_gradients(sectioned, small):
    arrs = list(sectioned) + [small]
    mine, got = _exchange_halves(arrs, name="rs_exchange_halves")
    chip_sum = [_add2(m, g, name=f"rs_add_sibling_{i}") for i, (m, g) in enumerate(zip(mine, got))]
    slots = _scatter_sections(chip_sum, name="rs_scatter_sections")
    red = [_sum4(s, name=f"rs_sum_chips_{i}") for i, s in enumerate(slots)]
    big, sm = _finish_reduce(red[:-1], red[-1], name="rs_finish")
    return big, sm


def kernel(x, c, ctx, c_ctx, w_mod, b_mod, g_pre, g_post, w_in, w_conv, a_log, dt_bias, g_onorm, gm_ln_g, gm_ln_b, w_sp, b_sp, w_pa, w_pb, w_out, loss_target, m_c_ctx, m_w_mod, m_b_mod, m_g_pre, m_g_post, m_w_in, m_w_conv, m_a_log, m_dt_bias, m_g_onorm, m_gm_ln_g, m_gm_ln_b, m_w_sp, m_b_sp, m_w_pa, m_w_pb, m_w_out, v_c_ctx, v_w_mod, v_b_mod, v_g_pre, v_g_post, v_w_in, v_w_conv, v_a_log, v_dt_bias, v_g_onorm, v_gm_ln_g, v_gm_ln_b, v_w_sp, v_b_sp, v_w_pa, v_w_pb, v_w_out):
    names = ["c_ctx", "w_mod", "b_mod", "g_pre", "g_post", "w_in", "w_conv", "a_log", "dt_bias", "g_onorm", "gm_ln_g",
             "gm_ln_b", "w_sp", "b_sp", "w_pa", "w_pb", "w_out"]
    w = dict(zip(names, (c_ctx, w_mod, b_mod, g_pre, g_post, w_in, w_conv, a_log, dt_bias, g_onorm, gm_ln_g, gm_ln_b,
                         w_sp, b_sp, w_pa, w_pb, w_out)))
    m = dict(zip(names, (m_c_ctx, m_w_mod, m_b_mod, m_g_pre, m_g_post, m_w_in, m_w_conv, m_a_log, m_dt_bias, m_g_onorm,
                         m_gm_ln_g, m_gm_ln_b, m_w_sp, m_b_sp, m_w_pa, m_w_pb, m_w_out)))
    v = dict(zip(names, (v_c_ctx, v_w_mod, v_b_mod, v_g_pre, v_g_post, v_w_in, v_w_conv, v_a_log, v_dt_bias, v_g_onorm,
                         v_gm_ln_g, v_gm_ln_b, v_w_sp, v_b_sp, v_w_pa, v_w_pb, v_w_out)))
    xy = 2 * lax.axis_index("x") + lax.axis_index("y")

    rowpack = jnp.concatenate([w_pa[0], w_pb[0], w_out[0]], axis=0).astype(BF16)
    (wm_all, win_all, row_all), wconv_all = _gather_shards(
        [w_mod[0].astype(BF16), w_in[0].astype(BF16), rowpack], w_conv[0], name="gather_weights")
    cols = lambda a: jnp.concatenate([a[s] for s in range(N_CHIPS)], axis=1)
    w_mod_f, w_in_f, w_conv_f = cols(wm_all), cols(win_all), cols(wconv_all)
    blk = D // N_CHIPS
    rows_of = lambda i: jnp.concatenate([row_all[s, i * blk:(i + 1) * blk] for s in range(N_CHIPS)], axis=0)
    w_ab = jnp.pad(w_in_f[:, OFF_A:OFF_ZB], ((0, 0), (0, DH - 4 * H)))

    loss_local, grad_x, g = _local_step(
        x[0], c, ctx[0], c_ctx, loss_target[0], w_mod_f, b_mod, g_pre, g_post, w_in_f[:, :OFF_A], w_ab, w_in_f[:, OFF_ZB:],
        w_conv_f, a_log[0], dt_bias[0], g_onorm, gm_ln_g, gm_ln_b, w_sp[0], b_sp[0], rows_of(0), rows_of(1), rows_of(2))
    loss = lax.psum(loss_local, ("x", "y", "c"))

    sect_cols = lambda a: jnp.stack(jnp.split(a, N_CHIPS, axis=1))
    g_rows = jnp.stack([jnp.concatenate([g[k][s * blk:(s + 1) * blk] for k in ("w_pa", "w_pb", "w_out")], axis=0)
                        for s in range(N_CHIPS)])
    flat = jnp.zeros((N_CHIPS * SMALL_ROWS * 128,), F32)
    for k, (off, size) in SMALL_LAYOUT.items():
        if k != "_pad":
            flat = lax.dynamic_update_slice(flat, g[k].reshape(-1), (off,))
    (gr_wm, gr_win, gr_rows), gr_small = _reduce_gradients(
        [sect_cols(g["w_mod"]), sect_cols(g["w_in"]), g_rows], flat.reshape(N_CHIPS, SMALL_ROWS, 128))
    gr_flat = gr_small.reshape(-1)

    res = {}
    res["w_mod"] = _adamw(w_mod[0], gr_wm, m_w_mod[0], v_w_mod[0], name="adamw_w_mod")
    res["w_in"] = _adamw(w_in[0], gr_win, m_w_in[0], v_w_in[0], name="adamw_w_in")
    for i, k in enumerate(("w_pa", "w_pb", "w_out")):
        res[k] = _adamw(w[k][0], gr_rows, m[k][0], v[k][0], g_row_block=i, name=f"adamw_{k}")
    small_names = [k for k in SMALL_LAYOUT if k not in ("_pad", "w_conv")]

    def pack(src):
        buf = jnp.zeros((N_CHIPS * SMALL_ROWS * 128,), F32)
        for k in small_names:
            buf = lax.dynamic_update_slice(buf, src[k].reshape(-1), (SMALL_LAYOUT[k][0],))
        return buf.reshape(-1, 128)

    sm_res = _adamw(pack(w), gr_small, pack(m), pack(v), name="adamw_small")
    for k in small_names:
        off, size = SMALL_LAYOUT[k]
        res[k] = [r.reshape(-1)[off:off + size].reshape(w[k].shape) for r in sm_res]
    off, size = SMALL_LAYOUT["w_conv"]
    g_conv = lax.dynamic_slice(gr_flat[off:off + size].reshape(3, 3 * D), (0, xy * (3 * D // N_CHIPS)), (3, 3 * D // N_CHIPS))
    conv_res = _adamw(jnp.pad(w_conv[0], ((0, 5), (0, 0))), jnp.pad(g_conv, ((0, 5), (0, 0))),
                      jnp.pad(m_w_conv[0], ((0, 5), (0, 0))), jnp.pad(v_w_conv[0], ((0, 5), (0, 0))), name="adamw_w_conv")
    res["w_conv"] = [r[:3] for r in conv_res]
    for k in ("w_mod", "w_in", "w_pa", "w_pb", "w_out", "w_conv"):
        res[k] = [r.reshape(w[k].shape) for r in res[k]]

    out = [loss, grad_x[None]]
    for i in range(4):
        out += [res[k][i] for k in names]
    return tuple(out)
```

```python
import functools

import jax
import jax.numpy as jnp
from jax import lax
from jax.experimental import pallas as pl
from jax.experimental.pallas import tpu as pltpu

F32 = jnp.float32
BF16 = jnp.bfloat16
HI = lax.Precision.HIGHEST
MESH = pl.DeviceIdType.MESH

D = 1024
H = 8
DH = 128
CH = 64
LOG_CH = 6
PAIR = 2 * CH
GM = 128
assert 1 << LOG_CH == CH and PAIR == DH
PREC_SOLVE = lax.Precision.HIGH
PREC_POWERS = (PREC_SOLVE,) * (LOG_CH - 1)
HEADS_PER_ITER_FWD = 8
HEADS_PER_ITER_BWD = 4
EPS = 1e-6
N_CHIPS = 4
OFF_A = 3 * D
OFF_ZB = OFF_A + 4 * H
IN_COLS = OFF_ZB + 6 * D
VMEM_LIMIT_V7X = 56 * 1024 * 1024

ADAM_LR, ADAM_B1, ADAM_B2, ADAM_EPS, ADAM_WD, ADAM_STEP = 0.001, 0.9, 0.999, 1e-08, 0.01, 10

SMALL_LAYOUT = {}
_off = 0
for _n, _s in (("c_ctx", D), ("b_mod", 3 * D), ("g_pre", D), ("g_post", D), ("gm_ln_g", D), ("gm_ln_b", D),
               ("b_sp", H * GM), ("g_onorm", DH), ("a_log", 2 * H), ("dt_bias", 2 * H), ("_pad", 96),
               ("w_conv", 3 * 3 * D), ("w_sp", H * GM * GM)):
    SMALL_LAYOUT[_n] = (_off, _s)
    _off += _s
SMALL_ROWS = 304
assert N_CHIPS * SMALL_ROWS * 128 >= _off and (SMALL_ROWS // 2) % 8 == 0


def _params(sem=None):
    return pltpu.CompilerParams(dimension_semantics=sem, vmem_limit_bytes=VMEM_LIMIT_V7X)


def _tile(n, cands=(256, 128, 64, 32, 16, 8)):
    for cand in cands:
        if n % cand == 0:
            return cand
    return n


def _silu(x):
    return x * jax.nn.sigmoid(x)


def _gelu(x):
    return 0.5 * x * (1.0 + jnp.tanh(0.7978845608028654 * (x + 0.044715 * (x * x * x))))


def _mm(a, b, *, ta=False, tb=False, out_dtype=F32, tm=512, tn=1024, tk=1024, pre_silu=False, name):
    m, k = (a.shape[1], a.shape[0]) if ta else a.shape
    n = b.shape[0] if tb else b.shape[1]
    tm, tn, tk = min(tm, m), min(tn, n), min(tk, k)
    assert m % tm == 0 and n % tn == 0 and k % tk == 0, (name, m, n, k, tm, tn, tk)
    nk = k // tk
    dims = (((0,) if ta else (1,), (1,) if tb else (0,)), ((), ()))

    def body(a_ref, b_ref, o_ref, acc_ref):
        kk = pl.program_id(2)
        av = a_ref[...]
        if pre_silu:
            av = _silu(av.astype(F32))
        part = lax.dot_general(av.astype(BF16), b_ref[...].astype(BF16), dims, preferred_element_type=F32)

        @pl.when(kk == 0)
        def _():
            acc_ref[...] = part

        @pl.when(kk > 0)
        def _():
            acc_ref[...] += part

        @pl.when(kk == nk - 1)
        def _():
            o_ref[...] = acc_ref[...].astype(out_dtype)

    a_spec = pl.BlockSpec((tk, tm), lambda i, j, q: (q, i)) if ta else pl.BlockSpec((tm, tk), lambda i, j, q: (i, q))
    b_spec = pl.BlockSpec((tn, tk), lambda i, j, q: (j, q)) if tb else pl.BlockSpec((tk, tn), lambda i, j, q: (q, j))
    return pl.pallas_call(
        body, name=name, grid=(m // tm, n // tn, nk),
        in_specs=[a_spec, b_spec], out_specs=pl.BlockSpec((tm, tn), lambda i, j, q: (i, j)),
        out_shape=jax.ShapeDtypeStruct((m, n), out_dtype),
        scratch_shapes=[pltpu.VMEM((tm, tn), F32)],
        compiler_params=_params(("parallel", "parallel", "arbitrary")),
    )(a, b)


def _h_fn(x, g, m):
    shift, scale = m[:, 0:D], m[:, D:2 * D]
    r = lax.rsqrt(jnp.mean(x * x, axis=-1, keepdims=True) + EPS)
    return (x * r * g) * (1.0 + scale) + shift


def _norm_fwd(x, g, mod, bmod, *, name):
    rows = x.shape[0]
    t = min(512, rows)

    def body(x_ref, g_ref, m_ref, b_ref, h_ref):
        h_ref[...] = _h_fn(x_ref[...], g_ref[...], m_ref[...] + b_ref[...]).astype(BF16)

    vec = lambda w: pl.BlockSpec((1, w), lambda i: (0, 0))
    return pl.pallas_call(
        body, name=name, grid=(rows // t,),
        in_specs=[pl.BlockSpec((t, D), lambda i: (i, 0)), vec(D), vec(3 * D), vec(3 * D)],
        out_specs=pl.BlockSpec((t, D), lambda i: (i, 0)),
        out_shape=jax.ShapeDtypeStruct((rows, D), BF16),
        compiler_params=_params(("parallel",)),
    )(x, g, mod, bmod)


def _norm_bwd(x, g, mod, bmod, dh_parts, resid, *, name):
    rows = x.shape[0]
    t = min(512, rows)
    n_parts = len(dh_parts)
    has_resid = resid is not None

    def body(*refs):
        x_ref, g_ref, m_ref, b_ref = refs[:4]
        parts = refs[4:4 + n_parts]
        r_ref = refs[4 + n_parts] if has_resid else None
        dx_ref, dg_ref, dm_ref = refs[-3:]
        i = pl.program_id(0)
        dh = parts[0][...]
        for p in parts[1:]:
            dh = dh + p[...]
        _, vjp = jax.vjp(_h_fn, x_ref[...], g_ref[...], m_ref[...] + b_ref[...])
        dx, dg, dm = vjp(dh)
        if has_resid:
            dx = dx + r_ref[...]
        dx_ref[...] = dx

        @pl.when(i == 0)
        def _():
            dg_ref[...] = dg
            dm_ref[...] = dm

        @pl.when(i > 0)
        def _():
            dg_ref[...] += dg
            dm_ref[...] += dm

    vec = lambda w: pl.BlockSpec((1, w), lambda i: (0, 0))
    tile = pl.BlockSpec((t, D), lambda i: (i, 0))
    ins = [x, g, mod, bmod, *dh_parts] + ([resid] if has_resid else [])
    return pl.pallas_call(
        body, name=name, grid=(rows // t,),
        in_specs=[tile, vec(D), vec(3 * D), vec(3 * D)] + [tile] * (n_parts + int(has_resid)),
        out_specs=[tile, vec(D), vec(3 * D)],
        out_shape=[jax.ShapeDtypeStruct((rows, D), F32), jax.ShapeDtypeStruct((1, D), F32),
                   jax.ShapeDtypeStruct((1, 3 * D), F32)],
        compiler_params=_params(("arbitrary",)),
    )(*ins)


def _conv_tile(u_ref, r0, t, rows, w0, w1, w2):
    u = u_ref[pl.ds(r0, t), :]
    prev8 = u_ref[pl.ds(pl.multiple_of(jnp.maximum(r0 - 8, 0), 8), 8), :]
    next8 = u_ref[pl.ds(pl.multiple_of(jnp.minimum(r0 + t, rows - 8), 8), 8), :]
    r8 = lax.broadcasted_iota(jnp.int32, (8, DH), 0)
    prev_row = jnp.sum(jnp.where(r8 == 7, prev8, 0.0), axis=0, keepdims=True)
    next_row = jnp.sum(jnp.where(r8 == 0, next8, 0.0), axis=0, keepdims=True)
    prev_row = jnp.where(r0 > 0, prev_row, 0.0)
    next_row = jnp.where(r0 + t < rows, next_row, 0.0)
    ri = lax.broadcasted_iota(jnp.int32, (t, DH), 0)
    um1 = jnp.where(ri == 0, prev_row, pltpu.roll(u, 1, 0))
    up1 = jnp.where(ri == t - 1, next_row, pltpu.roll(u, t - 1, 0))
    return w0 * um1 + w1 * u + w2 * up1, um1, u, up1


def _rowlocal(z, is_norm):
    y = _silu(z)
    yn = y * lax.rsqrt(jnp.sum(y * y, axis=-1, keepdims=True) + EPS)
    return jnp.where(is_norm, yn, y)


def _prep_fwd(p, wconv, n_norm, *, name):
    rows, nb = p.shape[0], p.shape[1] // DH
    t = min(512, rows)

    def body(u_ref, w_ref, o_ref):
        is_norm = pl.program_id(0) < n_norm
        w0, w1, w2 = w_ref[0:1, :], w_ref[1:2, :], w_ref[2:3, :]

        def step(s, carry):
            r0 = pl.multiple_of(s * t, t)
            z, _, _, _ = _conv_tile(u_ref, r0, t, rows, w0, w1, w2)
            o_ref[0, pl.ds(r0, t), :] = _rowlocal(z, is_norm)
            return carry

        lax.fori_loop(0, rows // t, step, 0)

    return pl.pallas_call(
        body, name=name, grid=(nb,),
        in_specs=[pl.BlockSpec((rows, DH), lambda j: (0, j)), pl.BlockSpec((3, DH), lambda j: (0, j))],
        out_specs=pl.BlockSpec((1, rows, DH), lambda j: (j, 0, 0)),
        out_shape=jax.ShapeDtypeStruct((nb, rows, DH), F32),
        compiler_params=_params(("parallel",)),
    )(p, wconv)


def _prep_bwd(p, wconv, d_a, d_b, n_norm, *, name):
    rows, nb = p.shape[0], p.shape[1] // DH
    t = min(512, rows)

    def body(u_ref, w_ref, da_ref, db_ref, du_ref, dw_ref, dz_ref):
        is_norm = pl.program_id(0) < n_norm
        w0, w1, w2 = w_ref[0:1, :], w_ref[1:2, :], w_ref[2:3, :]

        def step1(s, carry):
            a0, a1, a2 = carry
            r0 = pl.multiple_of(s * t, t)
            z, um1, u, up1 = _conv_tile(u_ref, r0, t, rows, w0, w1, w2)
            _, vjp = jax.vjp(lambda zz: _rowlocal(zz, is_norm), z)
            (dz,) = vjp(da_ref[0, pl.ds(r0, t), :] + db_ref[0, pl.ds(r0, t), :])
            dz_ref[pl.ds(r0, t), :] = dz
            red = lambda v: jnp.sum(v, axis=0, keepdims=True)
            return a0 + red(dz * um1), a1 + red(dz * u), a2 + red(dz * up1)

        zero = jnp.zeros((1, DH), F32)
        a0, a1, a2 = lax.fori_loop(0, rows // t, step1, (zero, zero, zero))
        dw_ref[0:1, :] = a0
        dw_ref[1:2, :] = a1
        dw_ref[2:3, :] = a2

        def step2(s, carry):
            r0 = pl.multiple_of(s * t, t)
            du, _, _, _ = _conv_tile(dz_ref, r0, t, rows, w2, w1, w0)
            du_ref[pl.ds(r0, t), :] = du.astype(BF16)
            return carry

        lax.fori_loop(0, rows // t, step2, 0)

    col = pl.BlockSpec((rows, DH), lambda j: (0, j))
    w_spec = pl.BlockSpec((3, DH), lambda j: (0, j))
    d_spec = pl.BlockSpec((1, rows, DH), lambda j: (j, 0, 0))
    return pl.pallas_call(
        body, name=name, grid=(nb,),
        in_specs=[col, w_spec, d_spec, d_spec], out_specs=[col, w_spec],
        out_shape=[jax.ShapeDtypeStruct((rows, nb * DH), BF16), jax.ShapeDtypeStruct((3, nb * DH), F32)],
        scratch_shapes=[pltpu.VMEM((rows, DH), F32)],
        compiler_params=_params(("parallel",)),
    )(p, wconv, d_a, d_b)


def _gates_fn(pab, avec, dvec):
    t = pab.shape[0]
    lane = lax.broadcasted_iota(jnp.int32, pab.shape, 1)
    xg = pab + dvec
    sp = jnp.maximum(xg, 0.0) + jnp.log(1.0 + jnp.exp(-jnp.abs(xg)))
    g = jnp.where(lane < 2 * H, -jnp.exp(avec) * sp, 0.0)
    ii = lax.broadcasted_iota(jnp.int32, (t, t), 0)
    jj = lax.broadcasted_iota(jnp.int32, (t, t), 1)
    same = (ii >> LOG_CH) == (jj >> LOG_CH)
    cum_f = _dot(jnp.where(same & (jj <= ii), 1.0, 0.0), g, precision=HI)
    cum_r = _dot(jnp.where(same & (jj >= ii), 1.0, 0.0), g, precision=HI)
    return jnp.where(lane < H, cum_f, jnp.where(lane < 2 * H, cum_r, jnp.where(lane < 4 * H, jax.nn.sigmoid(pab), 0.0)))


def _gates_fwd(pab, avec, dvec, *, name):
    rows = pab.shape[0]
    t = min(512, rows)

    def body(p_ref, a_ref, d_ref, o_ref):
        o_ref[...] = _gates_fn(p_ref[...], a_ref[...], d_ref[...])

    vec = pl.BlockSpec((1, DH), lambda i: (0, 0))
    tile = pl.BlockSpec((t, DH), lambda i: (i, 0))
    return pl.pallas_call(
        body, name=name, grid=(rows // t,), in_specs=[tile, vec, vec], out_specs=tile,
        out_shape=jax.ShapeDtypeStruct((rows, DH), F32), compiler_params=_params(("parallel",)),
    )(pab, avec, dvec)


def _gates_bwd(pab, avec, dvec, d_a, d_b, *, name):
    rows = pab.shape[0]
    t = min(512, rows)

    def body(p_ref, a_ref, d_ref, da_ref, db_ref, dp_ref, dav_ref, ddv_ref):
        i = pl.program_id(0)
        _, vjp = jax.vjp(_gates_fn, p_ref[...], a_ref[...], d_ref[...])
        dp, dav, ddv = vjp(da_ref[...] + db_ref[...])
        dp_ref[...] = dp.astype(BF16)

        @pl.when(i == 0)
        def _():
            dav_ref[...] = dav
            ddv_ref[...] = ddv

        @pl.when(i > 0)
        def _():
            dav_ref[...] += dav
            ddv_ref[...] += ddv

    vec = pl.BlockSpec((1, DH), lambda i: (0, 0))
    tile = pl.BlockSpec((t, DH), lambda i: (i, 0))
    return pl.pallas_call(
        body, name=name, grid=(rows // t,), in_specs=[tile, vec, vec, tile, tile], out_specs=[tile, vec, vec],
        out_shape=[jax.ShapeDtypeStruct((rows, DH), BF16), jax.ShapeDtypeStruct((1, DH), F32),
                   jax.ShapeDtypeStruct((1, DH), F32)],
        compiler_params=_params(("arbitrary",)),
    )(pab, avec, dvec, d_a, d_b)


def _dot(a, b, dims=((1,), (0,)), precision=None):
    return lax.dot_general(a, b, (dims, ((), ())), precision=precision, preferred_element_type=F32)


_NT = ((1,), (1,))
_TN = ((0,), (0,))


def _pairs(s, q, k, v, gcol, bcol, revs):
    idx = range(len(revs))
    ii = lax.broadcasted_iota(jnp.int32, (PAIR, PAIR), 0)
    jj = lax.broadcasted_iota(jnp.int32, (PAIR, PAIR), 1)
    same = (ii >> LOG_CH) == (jj >> LOG_CH)
    incl_d = (same & (ii >= jj), same & (ii <= jj))
    strict_d = (same & (ii > jj), same & (ii < jj))
    incl = [incl_d[int(r)] for r in revs]
    strict = [strict_d[int(r)] for r in revs]
    eye = jnp.where(ii == jj, 1.0, 0.0)
    gc_i = [jnp.broadcast_to(gcol[i], (PAIR, DH)) for i in idx]
    gc_j = [gc_i[i].T for i in idx]
    decay = [jnp.where(incl[i], jnp.exp(jnp.where(incl[i], gc_i[i] - gc_j[i], 0.0)), 0.0) for i in idx]
    b_b = [jnp.broadcast_to(bcol[i], (PAIR, DH)) for i in idx]
    kb = [k[i] * b_b[i] for i in idx]
    kk = [_dot(kb[i], k[i], _NT) for i in idx]
    bp = [jnp.where(strict[i], -kk[i] * decay[i], 0.0) for i in idx]
    inv = [eye + bp[i] for i in idx]
    for prec in PREC_POWERS:
        bp = [_dot(bp[i], bp[i], precision=prec) for i in idx]
        more = [_dot(inv[i], bp[i], precision=prec) for i in idx]
        inv = [inv[i] + more[i] for i in idx]
    eg = [jnp.exp(gc_i[i]) for i in idx]
    sol = [_dot(inv[i], jnp.concatenate([v[i] * b_b[i], kb[i] * eg[i]], axis=1), precision=PREC_SOLVE) for i in idx]
    u_val = [sol[i][:, :DH] for i in idx]
    w_key = [sol[i][:, DH:] for i in idx]
    row = lax.broadcasted_iota(jnp.int32, (PAIR, 1), 0)
    has_q = q[0] is not None
    if has_q:
        qc = [q[i] * (DH ** -0.5) for i in idx]
        qk = [_dot(qc[i], k[i], _NT) for i in idx]
        attn = [qk[i] * decay[i] for i in idx]
        qd = [qc[i] * eg[i] for i in idx]
    outs = [[None, None] for _ in idx]
    zeros = jnp.zeros((CH, DH), F32)
    for step in range(2):
        cidx = [(1 - step) if revs[i] else step for i in idx]
        sl = [slice(c * CH, (c + 1) * CH) for c in cidx]
        last = [c * CH if revs[i] else c * CH + CH - 1 for i, c in zip(idx, cidx)]
        gl = [jnp.sum(jnp.where(row == last[i], gcol[i], 0.0), axis=0, keepdims=True) for i in idx]
        k_tail = [k[i][sl[i]] * jnp.exp(gl[i] - gc_i[i][sl[i]]) for i in idx]
        ws = [_dot(w_key[i][sl[i]], s[i]) for i in idx]
        v_new = [u_val[i][sl[i]] - ws[i] for i in idx]
        if has_q:
            v_pad = [jnp.concatenate([v_new[i], zeros] if cidx[i] == 0 else [zeros, v_new[i]], axis=0) for i in idx]
            o_state = [_dot(qd[i][sl[i]], s[i]) for i in idx]
            o_local = [_dot(attn[i][sl[i]], v_pad[i]) for i in idx]
            for i in idx:
                outs[i][cidx[i]] = o_state[i] + o_local[i]
        kv = [_dot(k_tail[i], v_new[i], _TN) for i in idx]
        s = [s[i] * jnp.exp(gl[i]) + kv[i] for i in idx]
    return s, ([jnp.concatenate(outs[i], axis=0) for i in idx] if has_q else None)


def _lane_col(tile, idx):
    lane = lax.broadcasted_iota(jnp.int32, tile.shape, 1)
    return jnp.sum(jnp.where(lane == idx, tile, 0.0), axis=1, keepdims=True)


def _gdn_fwd(qkv, gb, s0f, s0b, has_q, *, name):
    nb, rows, _ = qkv.shape
    n = rows // PAIR
    qoff = H if has_q else 0

    def body(qf_ref, qb_ref, gf_ref, gr_ref, s0f_ref, s0b_ref, of_ref, ob_ref, ssf_ref, ssb_ref, sf_ref, sb_ref):
        @pl.when(pl.program_id(0) == 0)
        def _():
            sf_ref[...] = s0f_ref[...]
            sb_ref[...] = s0b_ref[...]

        gtiles = (gf_ref[...], gr_ref[...])

        dirs = ((qf_ref, sf_ref, ssf_ref, of_ref), (qb_ref, sb_ref, ssb_ref, ob_ref))

        def heads(hg, carry):
            work = [(hg * HEADS_PER_ITER_FWD + j, d) for j in range(HEADS_PER_ITER_FWD) for d in range(2)]
            loaded = []
            for h, d in work:
                q_ref, s_ref, _, _ = dirs[d]
                loaded.append((s_ref[h], q_ref[h] if has_q else None, q_ref[qoff + h], q_ref[qoff + H + h],
                               _lane_col(gtiles[d], d * H + h), _lane_col(gtiles[d], 2 * H + d * H + h)))
            s_new, o = _pairs(*[list(col) for col in zip(*loaded)], revs=[d == 1 for _, d in work])
            for i, (h, d) in enumerate(work):
                _, s_ref, ss_ref, o_ref = dirs[d]
                ss_ref[0, h] = loaded[i][0]
                s_ref[h] = s_new[i]
                o_ref[h] = o[i] if has_q else jnp.zeros((PAIR, DH), F32)
            return carry

        if HEADS_PER_ITER_FWD == H:
            heads(0, 0)
        else:
            lax.fori_loop(0, H // HEADS_PER_ITER_FWD, heads, 0)

    fwd3 = lambda i: (0, i, 0)
    rev3 = lambda i: (0, n - 1 - i, 0)
    state = pl.BlockSpec((H, DH, DH), lambda i: (0, 0, 0))
    saved = pl.BlockSpec((1, H, DH, DH), lambda i: (i, 0, 0, 0))
    return pl.pallas_call(
        body, name=name, grid=(n,),
        in_specs=[pl.BlockSpec((nb, PAIR, DH), fwd3), pl.BlockSpec((nb, PAIR, DH), rev3),
                  pl.BlockSpec((PAIR, DH), lambda i: (i, 0)), pl.BlockSpec((PAIR, DH), lambda i: (n - 1 - i, 0)),
                  state, state],
        out_specs=[pl.BlockSpec((H, PAIR, DH), fwd3), pl.BlockSpec((H, PAIR, DH), rev3), saved, saved, state, state],
        out_shape=[jax.ShapeDtypeStruct((H, rows, DH), F32)] * 2 + [jax.ShapeDtypeStruct((n, H, DH, DH), F32)] * 2
        + [jax.ShapeDtypeStruct((H, DH, DH), F32)] * 2,
        compiler_params=_params(("arbitrary",)),
    )(qkv, qkv, gb, gb, s0f, s0b)


def _gdn_bwd(qkv, gb, ssf, ssb, do, dsf, dsb, has_q, *, name):
    nb, rows, _ = qkv.shape
    n = rows // PAIR
    qoff = H if has_q else 0

    def body(qf_ref, qb_ref, gf_ref, gr_ref, ssf_ref, ssb_ref, dof_ref, dob_ref, dsf0_ref, dsb0_ref,
             dqf_ref, dqb_ref, dgf_ref, dgr_ref, dsf_ref, dsb_ref):
        @pl.when(pl.program_id(0) == 0)
        def _():
            dsf_ref[...] = dsf0_ref[...]
            dsb_ref[...] = dsb0_ref[...]

        gtiles = (gf_ref[...], gr_ref[...])
        lane = lax.broadcasted_iota(jnp.int32, (PAIR, DH), 1)

        dirs = ((qf_ref, ssf_ref, dof_ref, dsf_ref, dqf_ref), (qb_ref, ssb_ref, dob_ref, dsb_ref, dqb_ref))

        def heads(hg, carry):
            out = list(carry)
            work = [(hg * HEADS_PER_ITER_BWD + j, d) for j in range(HEADS_PER_ITER_BWD) for d in range(2)]
            revs = [d == 1 for _, d in work]
            s_in, q_in, k_in, v_in, g_in, b_in, ds_out, do_out = [], [], [], [], [], [], [], []
            for h, d in work:
                q_ref, ss_ref, do_ref, ds_ref, _ = dirs[d]
                s_in.append(ss_ref[0, h])
                q_in.append(q_ref[h] if has_q else None)
                k_in.append(q_ref[qoff + h])
                v_in.append(q_ref[qoff + H + h])
                g_in.append(_lane_col(gtiles[d], d * H + h))
                b_in.append(_lane_col(gtiles[d], 2 * H + d * H + h))
                ds_out.append(ds_ref[h])
                do_out.append(do_ref[h] if has_q else None)
            if has_q:
                _, vjp = jax.vjp(lambda s_, q_, k_, v_, g_, b_: _pairs(s_, q_, k_, v_, g_, b_, revs),
                                 s_in, q_in, k_in, v_in, g_in, b_in)
                ds, dq, dk, dv, dg, db = vjp((ds_out, do_out))
            else:
                _, vjp = jax.vjp(lambda s_, k_, v_, g_, b_: _pairs(s_, q_in, k_, v_, g_, b_, revs)[0],
                                 s_in, k_in, v_in, g_in, b_in)
                ds, dk, dv, dg, db = vjp(ds_out)
            for i, (h, d) in enumerate(work):
                dq_ref, ds_ref = dirs[d][4], dirs[d][3]
                ds_ref[h] = ds[i]
                if has_q:
                    dq_ref[h] = dq[i]
                dq_ref[qoff + h], dq_ref[qoff + H + h] = dk[i], dv[i]
                out[d] = (out[d] + jnp.where(lane == d * H + h, dg[i], 0.0)
                          + jnp.where(lane == 2 * H + d * H + h, db[i], 0.0))
            return tuple(out)

        zero = jnp.zeros((PAIR, DH), F32)
        if HEADS_PER_ITER_BWD == H:
            dgf, dgr = heads(0, (zero, zero))
        else:
            dgf, dgr = lax.fori_loop(0, H // HEADS_PER_ITER_BWD, heads, (zero, zero))
        dgf_ref[...] = dgf
        dgr_ref[...] = dgr

    fwd3 = lambda i: (0, n - 1 - i, 0)
    rev3 = lambda i: (0, i, 0)
    state = pl.BlockSpec((H, DH, DH), lambda i: (0, 0, 0))
    saved = pl.BlockSpec((1, H, DH, DH), lambda i: (n - 1 - i, 0, 0, 0))
    gf_spec = pl.BlockSpec((PAIR, DH), lambda i: (n - 1 - i, 0))
    gr_spec = pl.BlockSpec((PAIR, DH), lambda i: (i, 0))
    return pl.pallas_call(
        body, name=name, grid=(n,),
        in_specs=[pl.BlockSpec((nb, PAIR, DH), fwd3), pl.BlockSpec((nb, PAIR, DH), rev3), gf_spec, gr_spec, saved, saved,
                  pl.BlockSpec((H, PAIR, DH), fwd3), pl.BlockSpec((H, PAIR, DH), rev3), state, state],
        out_specs=[pl.BlockSpec((nb, PAIR, DH), fwd3), pl.BlockSpec((nb, PAIR, DH), rev3), gf_spec, gr_spec, state, state],
        out_shape=[jax.ShapeDtypeStruct((nb, rows, DH), F32)] * 2 + [jax.ShapeDtypeStruct((rows, DH), F32)] * 2
        + [jax.ShapeDtypeStruct((H, DH, DH), F32)] * 2,
        compiler_params=_params(("arbitrary",)),
    )(qkv, qkv, gb, gb, ssf, ssb, do, do, dsf, dsb)


def _stage1(zb, ua, va, za, o, gon, lng, lnb, wsp, bsp):
    gv = [_gelu(t) for t in va]
    mu = sum(jnp.sum(t, axis=-1, keepdims=True) for t in gv) * (1.0 / D)
    xc = [t - mu for t in gv]
    var = sum(jnp.sum(t * t, axis=-1, keepdims=True) for t in xc) * (1.0 / D)
    rs = lax.rsqrt(var + EPS)
    ya, yb = [], []
    for g in range(H):
        vv = xc[g] * rs * lng[g] + lnb[g]
        s = _dot(wsp[g], vv) + bsp[g]
        ya.append(_gelu(ua[g]) * s * _silu(za[g]))
        r = lax.rsqrt(jnp.mean(o[g] * o[g], axis=-1, keepdims=True) + EPS)
        yb.append(o[g] * r * gon * _silu(zb[g]))
    return ya, yb


def _stage2(ma, mb, ga, gb):
    return jax.nn.sigmoid(ga) * ma + jax.nn.sigmoid(gb) * mb


def _stage3(out, x, tgt, gpost, gate):
    r = out * lax.rsqrt(jnp.mean(out * out, axis=-1, keepdims=True) + EPS) * gpost
    err = x + gate * r - tgt
    return 0.5 * jnp.sum(jnp.mean(err * err, axis=-1, keepdims=True), axis=0, keepdims=True)


def _post(p_rest, o_f, o_b, x, tgt, mod, bmod, gon, lng, lnb, wsp, bspb, wpa, wpb, wout, gpost, *, name):
    rows = x.shape[0]
    n = rows // GM
    lanes = lambda g: slice(g * DH, (g + 1) * DH)
    bdot = lambda a, w_ref: _dot(a.astype(BF16), w_ref[...])
    bdot_t = lambda a, w_ref: _dot(a.astype(BF16), w_ref[...], _NT)

    def body(p_ref, of_ref, ob_ref, x_ref, t_ref, m_ref, bm_ref, gon_ref, lng_ref, lnb_ref, wsp_ref, bsp_ref,
             wpa_ref, wpb_ref, wout_ref, gp_ref,
             loss_ref, dp_ref, do_ref, dx_ref, ya_ref, yb_ref, mg_ref, dma_ref, dmb_ref, dout_ref,
             dvec_ref, dgon_ref, dwsp_ref, dbsp_ref):
        @pl.when(pl.program_id(0) == 0)
        def _():
            loss_ref[...] = jnp.zeros_like(loss_ref)
            dvec_ref[...] = jnp.zeros_like(dvec_ref)
            dgon_ref[...] = jnp.zeros_like(dgon_ref)
            dwsp_ref[...] = jnp.zeros_like(dwsp_ref)
            dbsp_ref[...] = jnp.zeros_like(dbsp_ref)

        piece = lambda blk: [p_ref[:, blk * D + g * DH: blk * D + (g + 1) * DH] for g in range(H)]
        zb, ua, va, za = piece(0), piece(1), piece(2), piece(3)
        o = [of_ref[g] + ob_ref[g] for g in range(H)]
        gon = gon_ref[...]
        lng = [lng_ref[:, lanes(g)] for g in range(H)]
        lnb = [lnb_ref[:, lanes(g)] for g in range(H)]
        wsp = [wsp_ref[g] for g in range(H)]
        bsp = [bsp_ref[g] for g in range(H)]
        (ya, yb), vjp1 = jax.vjp(_stage1, zb, ua, va, za, o, gon, lng, lnb, wsp, bsp)
        y_a, y_b = jnp.concatenate(ya, axis=1), jnp.concatenate(yb, axis=1)
        ma, mb = bdot(y_a, wpa_ref), bdot(y_b, wpb_ref)
        ga, gb = p_ref[:, 4 * D:5 * D], p_ref[:, 5 * D:6 * D]
        merged, vjp2 = jax.vjp(_stage2, ma, mb, ga, gb)
        out = bdot(merged, wout_ref)
        gate = m_ref[:, 2 * D:3 * D] + bm_ref[:, 2 * D:3 * D]
        loss, vjp3 = jax.vjp(_stage3, out, x_ref[...], t_ref[...], gp_ref[...], gate)
        loss_ref[...] += jnp.broadcast_to(loss, loss_ref.shape)

        dout, dx, _, dgpost, dgate = vjp3(jnp.ones((1, 1), F32))
        dx_ref[...] = dx
        dmerged = bdot_t(dout, wout_ref)
        dma, dmb, dga, dgb = vjp2(dmerged)
        dya, dyb = bdot_t(dma, wpa_ref), bdot_t(dmb, wpb_ref)
        dzb, dua, dva, dza, do, dgon, dlng, dlnb, dwsp, dbsp = vjp1(
            ([dya[:, lanes(g)] for g in range(H)], [dyb[:, lanes(g)] for g in range(H)]))

        for blk, dlist in enumerate((dzb, dua, dva, dza)):
            for g in range(H):
                dp_ref[:, blk * D + g * DH: blk * D + (g + 1) * DH] = dlist[g].astype(BF16)
        dp_ref[:, 4 * D:5 * D] = dga.astype(BF16)
        dp_ref[:, 5 * D:6 * D] = dgb.astype(BF16)
        for g in range(H):
            do_ref[g] = do[g]
            dwsp_ref[g] += dwsp[g]
            dbsp_ref[g] += dbsp[g]
            dvec_ref[2:3, lanes(g)] += dlng[g]
            dvec_ref[3:4, lanes(g)] += dlnb[g]
        dvec_ref[0:1, :] += dgpost
        dvec_ref[1:2, :] += dgate
        dgon_ref[0:1, :] += dgon
        ya_ref[...] = y_a.astype(BF16)
        yb_ref[...] = y_b.astype(BF16)
        mg_ref[...] = merged.astype(BF16)
        dma_ref[...] = dma.astype(BF16)
        dmb_ref[...] = dmb.astype(BF16)
        dout_ref[...] = dout.astype(BF16)

    row = lambda w: pl.BlockSpec((GM, w), lambda i: (i, 0))
    heads = pl.BlockSpec((H, GM, DH), lambda i: (0, i, 0))
    full = lambda shape: pl.BlockSpec(shape, lambda i: tuple(0 for _ in shape))
    sds = jax.ShapeDtypeStruct
    return pl.pallas_call(
        body, name=name, grid=(n,),
        in_specs=[row(6 * D), heads, heads, row(D), row(D), full((1, 3 * D)), full((1, 3 * D)), full((1, DH)),
                  full((1, D)), full((1, D)), full((H, GM, GM)), full((H, GM, GM)),
                  full((D, D)), full((D, D)), full((D, D)), full((1, D))],
        out_specs=[full((8, DH)), row(6 * D), heads, row(D)] + [row(D)] * 6
        + [full((8, D)), full((8, DH)), full((H, GM, GM)), full((H, GM, GM))],
        out_shape=[sds((8, DH), F32), sds((rows, 6 * D), BF16), sds((H, rows, DH), F32), sds((rows, D), F32)]
        + [sds((rows, D), BF16)] * 6 + [sds((8, D), F32), sds((8, DH), F32), sds((H, GM, GM), F32), sds((H, GM, GM), F32)],
        compiler_params=_params(("arbitrary",)),
    )(p_rest, o_f, o_b, x, tgt, mod, bmod, gon, lng, lnb, wsp, bspb, wpa, wpb, wout, gpost)


def _dsilu_mul(c, d, *, name):
    def body(c_ref, d_ref, o_ref):
        _, vjp = jax.vjp(_silu, c_ref[...])
        (o_ref[...],) = vjp(d_ref[...])

    return pl.pallas_call(body, name=name, out_shape=jax.ShapeDtypeStruct(c.shape, F32))(c, d)


def _adamw(w, g, m, v, *, g_row_block=0, name):
    rows, cols = w.shape
    t = _tile(rows)
    gb = rows // t * g_row_block
    c1 = 1.0 / (1.0 - ADAM_B1 ** ADAM_STEP)
    c2 = 1.0 / (1.0 - ADAM_B2 ** ADAM_STEP)

    def body(w_ref, g_ref, m_ref, v_ref, go_ref, d_ref, mo_ref, vo_ref):
        gv = g_ref[...]
        mn = ADAM_B1 * m_ref[...] + (1.0 - ADAM_B1) * gv
        vn = ADAM_B2 * v_ref[...] + (1.0 - ADAM_B2) * (gv * gv)
        go_ref[...] = gv
        mo_ref[...] = mn
        vo_ref[...] = vn
        d_ref[...] = -ADAM_LR * ((mn * c1) / (jnp.sqrt(vn * c2) + ADAM_EPS) + ADAM_WD * w_ref[...])

    tile = pl.BlockSpec((t, cols), lambda i: (i, 0))
    return pl.pallas_call(
        body, name=name, grid=(rows // t,),
        in_specs=[tile, pl.BlockSpec((t, cols), lambda i: (gb + i, 0)), tile, tile], out_specs=[tile] * 4,
        out_shape=[jax.ShapeDtypeStruct((rows, cols), F32)] * 4,
        compiler_params=_params(("parallel",)),
    )(w, g, m, v)


def _add2(a, b, *, name):
    s, rows, cols = a.shape
    t = _tile(rows)

    def body(a_ref, b_ref, o_ref):
        o_ref[...] = a_ref[...] + b_ref[...]

    tile = pl.BlockSpec((1, t, cols), lambda j, i: (j, i, 0))
    return pl.pallas_call(
        body, name=name, grid=(s, rows // t), in_specs=[tile, tile], out_specs=tile,
        out_shape=jax.ShapeDtypeStruct(a.shape, F32), compiler_params=_params(("parallel", "parallel")),
    )(a, b)


def _sum4(a, *, name):
    _, rows, cols = a.shape
    t = _tile(rows)

    def body(a_ref, o_ref):
        o_ref[...] = ((a_ref[0] + a_ref[1]) + a_ref[2]) + a_ref[3]

    return pl.pallas_call(
        body, name=name, grid=(rows // t,), in_specs=[pl.BlockSpec((N_CHIPS, t, cols), lambda i: (0, i, 0))],
        out_specs=pl.BlockSpec((t, cols), lambda i: (i, 0)),
        out_shape=jax.ShapeDtypeStruct((rows, cols), F32), compiler_params=_params(("parallel",)),
    )(a)


def _local_step(x, c, ctx, c_ctx, tgt, w_mod, b_mod, g_pre, g_post, w_qkv, w_ab, w_rest, w_conv, a_log, dt_bias,
                g_onorm, gm_ln_g, gm_ln_b, w_sp, b_sp, w_pa, w_pb, w_out):
    rows, rows_c = x.shape[0], ctx.shape[0]
    cc = jnp.zeros((16, D), F32).at[0].set(c[0]).at[1].set(c_ctx)
    mod = _mm(cc, w_mod, pre_silu=True, name="mod_fwd")
    mod_x, mod_c = mod[0:1], mod[1:2]
    avec = jnp.zeros((1, DH), F32).at[0, :2 * H].set(a_log.reshape(-1))
    dvec = jnp.zeros((1, DH), F32).at[0, :2 * H].set(dt_bias.reshape(-1))
    bspb = jnp.broadcast_to(b_sp[:, :, None], (H, GM, GM))
    w_kv, wconv_kv = w_qkv[:, D:], w_conv[:, D:]

    h_c = _norm_fwd(ctx, g_pre, mod_c, b_mod, name="norm_fwd_ctx")
    pc_kv = _mm(h_c, w_kv, name="inproj_ctx_kv")
    pc_ab = _mm(h_c, w_ab, name="inproj_ctx_ab")
    kv_c = _prep_fwd(pc_kv, wconv_kv, H, name="prep_fwd_ctx")
    gb_c = _gates_fwd(pc_ab, avec, dvec, name="gates_fwd_ctx")
    s_zero = jnp.zeros((H, DH, DH), F32)
    _, _, ssf_c, ssb_c, s_f, s_b = _gdn_fwd(kv_c, gb_c, s_zero, s_zero, False, name="gdn_fwd_ctx")

    h_x = _norm_fwd(x, g_pre, mod_x, b_mod, name="norm_fwd_x")
    p_qkv = _mm(h_x, w_qkv, name="inproj_qkv")
    p_ab = _mm(h_x, w_ab, name="inproj_ab")
    p_rest = _mm(h_x, w_rest, name="inproj_rest")
    qkv = _prep_fwd(p_qkv, w_conv, 2 * H, name="prep_fwd_x")
    gb_x = _gates_fwd(p_ab, avec, dvec, name="gates_fwd_x")
    o_f, o_b, ssf, ssb, _, _ = _gdn_fwd(qkv, gb_x, s_f, s_b, True, name="gdn_fwd_x")

    (loss_acc, dp_rest, do, dx_res, ya, yb, mg, dma, dmb, dout, dvec_post, dgon, dwsp, dbspb) = _post(
        p_rest, o_f, o_b, x, tgt, mod_x, b_mod, g_onorm, gm_ln_g, gm_ln_b, w_sp, bspb, w_pa, w_pb, w_out, g_post,
        name="post")
    g = {}
    g["w_pa"] = _mm(ya, dma, ta=True, name="dw_pa")
    g["w_pb"] = _mm(yb, dmb, ta=True, name="dw_pb")
    g["w_out"] = _mm(mg, dout, ta=True, name="dw_out")

    zeros_s = jnp.zeros((H, DH, DH), F32)
    dq_f, dq_b, dg_f, dg_b, ds0_f, ds0_b = _gdn_bwd(qkv, gb_x, ssf, ssb, do, zeros_s, zeros_s, True, name="gdn_bwd_x")
    dp_qkv, dwc_x = _prep_bwd(p_qkv, w_conv, dq_f, dq_b, 2 * H, name="prep_bwd_x")
    dp_ab, dav_x, ddv_x = _gates_bwd(p_ab, avec, dvec, dg_f, dg_b, name="gates_bwd_x")
    dkv_f, dkv_b, dgc_f, dgc_b, _, _ = _gdn_bwd(kv_c, gb_c, ssf_c, ssb_c, jnp.zeros((H, rows_c, DH), F32), ds0_f, ds0_b,
                                                 False, name="gdn_bwd_ctx")
    dpc_kv, dwc_c = _prep_bwd(pc_kv, wconv_kv, dkv_f, dkv_b, H, name="prep_bwd_ctx")
    dpc_ab, dav_c, ddv_c = _gates_bwd(pc_ab, avec, dvec, dgc_f, dgc_b, name="gates_bwd_ctx")

    h_all = jnp.concatenate([h_x, h_c], axis=0)
    dpc_qkv = jnp.concatenate([jnp.zeros((rows_c, D), BF16), dpc_kv], axis=1)
    tk = _tile(rows + rows_c, (512, 256, 128, 64))
    dw_qkv = _mm(h_all, jnp.concatenate([dp_qkv, dpc_qkv], axis=0), ta=True, tk=tk, name="dw_qkv")
    dw_ab = _mm(h_all, jnp.concatenate([dp_ab, dpc_ab], axis=0), ta=True, tk=tk, name="dw_ab")
    dw_rest = _mm(h_x, dp_rest, ta=True, tk=512, name="dw_rest")
    g["w_in"] = jnp.concatenate([dw_qkv, dw_ab[:, :4 * H], dw_rest], axis=1)
    dh_parts = [_mm(dp_qkv, w_qkv, tb=True, name="dh_qkv"), _mm(dp_rest, w_rest, tb=True, name="dh_rest"),
                _mm(dp_ab, w_ab, tb=True, name="dh_ab")]
    grad_x, dgpre_x, dm_x = _norm_bwd(x, g_pre, mod_x, b_mod, dh_parts, dx_res, name="norm_bwd_x")
    dhc_parts = [_mm(dpc_kv, w_kv, tb=True, name="dhc_kv"), _mm(dpc_ab, w_ab, tb=True, name="dhc_ab")]
    _, dgpre_c, dm_c = _norm_bwd(ctx, g_pre, mod_c, b_mod, dhc_parts, None, name="norm_bwd_ctx")

    dm_x = dm_x.at[:, 2 * D:].add(dvec_post[1:2])
    dmod = jnp.zeros((16, 3 * D), F32).at[0].set(dm_x[0]).at[1].set(dm_c[0])
    g["w_mod"] = _mm(cc, dmod, ta=True, pre_silu=True, name="dw_mod")
    dcc = _mm(dmod, w_mod, tb=True, name="dcc")
    g["c_ctx"] = _dsilu_mul(cc[:8], dcc[:8], name="dc_ctx")[1]
    g["b_mod"] = dm_x + dm_c
    g["g_pre"] = dgpre_x + dgpre_c
    g["g_post"] = dvec_post[0:1]
    g["gm_ln_g"], g["gm_ln_b"] = dvec_post[2:3], dvec_post[3:4]
    g["g_onorm"] = dgon[0:1]
    g["w_sp"] = dwsp
    g["b_sp"] = jnp.sum(dbspb, axis=-1)
    g["w_conv"] = dwc_x.at[:, D:].add(dwc_c)
    g["a_log"] = (dav_x + dav_c)[0, :2 * H].reshape(2, H)
    g["dt_bias"] = (ddv_x + ddv_c)[0, :2 * H].reshape(2, H)
    return loss_acc[0, 0], grad_x, g


ANY = pl.BlockSpec(memory_space=pl.ANY)


def _place():
    x, y, c = lax.axis_index("x"), lax.axis_index("y"), lax.axis_index("c")
    chips = [(1 - x, y), (x, 1 - y), (1 - x, 1 - y)]
    return x, y, c, (x, y, 1 - c), chips


def _gather_shards(big, small, *, name):
    nb = len(big)

    def body(*refs):
        ins, sm_in = refs[:nb], refs[nb]
        outs, sm_out = refs[nb + 1:2 * nb + 1], refs[2 * nb + 1]
        send, recv, lsem = refs[2 * nb + 2:]
        x, y, c, sibling, chips = _place()
        mine = 2 * x + y
        local = [pltpu.make_async_copy(ins[a], outs[a].at[mine], lsem.at[a]) for a in range(nb)]
        local.append(pltpu.make_async_copy(sm_in, sm_out.at[mine], lsem.at[nb]))
        for cp in local:
            cp.start()

        def half(a, shard, hc):
            hr = big[a].shape[0] // 2
            return outs[a].at[shard, pl.ds(hc * hr, hr), :]

        def remote(k, src, dst, to):
            return pltpu.make_async_remote_copy(src_ref=src, dst_ref=dst, send_sem=send.at[k], recv_sem=recv.at[k],
                                                device_id=to, device_id_type=MESH)

        sends = []
        for a in range(nb):
            hr = big[a].shape[0] // 2
            for j, chip in enumerate(chips):
                sends.append(remote(a * 3 + j, ins[a].at[pl.ds(c * hr, hr), :], half(a, mine, c), (*chip, c)))
        for j, chip in enumerate(chips):
            sends.append(remote(nb * 3 + j, sm_in, sm_out.at[mine], (*chip, c)))
        for cp in sends:
            cp.start()
        base = nb * 3 + 3
        passed = []
        for a in range(nb):
            for j, (px, py) in enumerate(chips):
                theirs = 2 * px + py
                remote(a * 3 + j, half(a, theirs, c), half(a, theirs, c), sibling).wait_recv()
                fw = remote(base + a * 3 + j, half(a, theirs, c), half(a, theirs, c), sibling)
                fw.start()
                passed.append(fw)
        for a in range(nb):
            for j, (px, py) in enumerate(chips):
                theirs = 2 * px + py
                remote(base + a * 3 + j, half(a, theirs, 1 - c), half(a, theirs, 1 - c), sibling).wait_recv()
        for j, (px, py) in enumerate(chips):
            remote(nb * 3 + j, sm_in, sm_out.at[2 * px + py], sibling).wait_recv()
        for cp in sends + passed:
            cp.wait_send()
        for cp in local:
            cp.wait()

    n_remote = 2 * nb * 3 + 3
    outs = pl.pallas_call(
        body, name=name, in_specs=[ANY] * (nb + 1), out_specs=[ANY] * (nb + 1),
        out_shape=[jax.ShapeDtypeStruct((N_CHIPS,) + a.shape, a.dtype) for a in big + [small]],
        scratch_shapes=[pltpu.SemaphoreType.DMA((n_remote,)), pltpu.SemaphoreType.DMA((n_remote,)),
                        pltpu.SemaphoreType.DMA((nb + 1,))],
    )(*big, small)
    return outs[:nb], outs[nb]


def _exchange_halves(arrs, *, name):
    na = len(arrs)

    def body(*refs):
        ins, mine_out, got_out = refs[:na], refs[na:2 * na], refs[2 * na:3 * na]
        send, recv, lsem = refs[3 * na:]
        x, y, c, sibling, _ = _place()
        copies = []
        for a in range(na):
            hr = arrs[a].shape[1] // 2
            keep = pltpu.make_async_copy(ins[a].at[:, pl.ds(c * hr, hr), :], mine_out[a], lsem.at[a])
            give = pltpu.make_async_remote_copy(src_ref=ins[a].at[:, pl.ds((1 - c) * hr, hr), :], dst_ref=got_out[a],
                                                send_sem=send.at[a], recv_sem=recv.at[a], device_id=sibling,
                                                device_id_type=MESH)
            keep.start()
            give.start()
            copies.append((keep, give))
        for keep, give in copies:
            give.wait()
            keep.wait()

    half = [jax.ShapeDtypeStruct((N_CHIPS, a.shape[1] // 2, a.shape[2]), F32) for a in arrs]
    outs = pl.pallas_call(
        body, name=name, in_specs=[ANY] * na, out_specs=[ANY] * (2 * na), out_shape=half + half,
        scratch_shapes=[pltpu.SemaphoreType.DMA((na,)), pltpu.SemaphoreType.DMA((na,)), pltpu.SemaphoreType.DMA((na,))],
    )(*arrs)
    return outs[:na], outs[na:]


def _scatter_sections(arrs, *, name):
    na = len(arrs)

    def body(*refs):
        ins, outs = refs[:na], refs[na:2 * na]
        send, recv, lsem = refs[2 * na:]
        x, y, c, _, chips = _place()
        mine = 2 * x + y
        copies = []
        for a in range(na):
            keep = pltpu.make_async_copy(ins[a].at[mine], outs[a].at[mine], lsem.at[a])
            keep.start()
            copies.append(keep)
        sends = []
        for a in range(na):
            for j, (px, py) in enumerate(chips):
                cp = pltpu.make_async_remote_copy(src_ref=ins[a].at[2 * px + py], dst_ref=outs[a].at[mine],
                                                  send_sem=send.at[a * 3 + j], recv_sem=recv.at[a * 3 + j],
                                                  device_id=(px, py, c), device_id_type=MESH)
                cp.start()
                sends.append(cp)
        for a in range(na):
            for j, (px, py) in enumerate(chips):
                pltpu.make_async_remote_copy(src_ref=ins[a].at[mine], dst_ref=outs[a].at[2 * px + py],
                                             send_sem=send.at[a * 3 + j], recv_sem=recv.at[a * 3 + j],
                                             device_id=(px, py, c), device_id_type=MESH).wait_recv()
        for cp in sends:
            cp.wait_send()
        for cp in copies:
            cp.wait()

    return pl.pallas_call(
        body, name=name, in_specs=[ANY] * na, out_specs=[ANY] * na,
        out_shape=[jax.ShapeDtypeStruct(a.shape, F32) for a in arrs],
        scratch_shapes=[pltpu.SemaphoreType.DMA((3 * na,)), pltpu.SemaphoreType.DMA((3 * na,)),
                        pltpu.SemaphoreType.DMA((na,))],
    )(*arrs)


def _finish_reduce(big, small, *, name):
    nb = len(big)
    hs = small.shape[0]

    def body(*refs):
        ins, sm_in = refs[:nb], refs[nb]
        outs, sm_out = refs[nb + 1:2 * nb + 1], refs[2 * nb + 1]
        send, recv, lsem = refs[2 * nb + 2:]
        x, y, c, sibling, chips = _place()

        def remote(k, src, dst, to):
            return pltpu.make_async_remote_copy(src_ref=src, dst_ref=dst, send_sem=send.at[k], recv_sem=recv.at[k],
                                                device_id=to, device_id_type=MESH)

        def rows(a, hc):
            hr = big[a].shape[0]
            return outs[a].at[pl.ds(hc * hr, hr), :]

        def blk(px, py, pc):
            return sm_out.at[pl.ds((4 * px + 2 * py + pc) * hs, hs), :]

        local = [pltpu.make_async_copy(ins[a], rows(a, c), lsem.at[a]) for a in range(nb)]
        local.append(pltpu.make_async_copy(sm_in, blk(x, y, c), lsem.at[nb]))
        for cp in local:
            cp.start()
        first = [remote(a, ins[a], rows(a, c), sibling) for a in range(nb)]
        first.append(remote(nb, sm_in, blk(x, y, c), sibling))
        first += [remote(nb + 1 + j, sm_in, blk(x, y, c), (*chip, c)) for j, chip in enumerate(chips)]
        for cp in first:
            cp.start()
        passed = []
        for j, (px, py) in enumerate(chips):
            remote(nb + 1 + j, blk(px, py, c), blk(px, py, c), sibling).wait_recv()
            fw = remote(nb + 4 + j, blk(px, py, c), blk(px, py, c), sibling)
            fw.start()
            passed.append(fw)
        for a in range(nb):
            remote(a, ins[a], rows(a, 1 - c), sibling).wait_recv()
        remote(nb, sm_in, blk(x, y, 1 - c), sibling).wait_recv()
        for j, (px, py) in enumerate(chips):
            remote(nb + 4 + j, blk(px, py, 1 - c), blk(px, py, 1 - c), sibling).wait_recv()
        for cp in first + passed:
            cp.wait_send()
        for cp in local:
            cp.wait()

    n_remote = nb + 7
    outs = pl.pallas_call(
        body, name=name, in_specs=[ANY] * (nb + 1), out_specs=[ANY] * (nb + 1),
        out_shape=[jax.ShapeDtypeStruct((2 * a.shape[0], a.shape[1]), F32) for a in big]
        + [jax.ShapeDtypeStruct((8 * hs, small.shape[1]), F32)],
        scratch_shapes=[pltpu.SemaphoreType.DMA((n_remote,)), pltpu.SemaphoreType.DMA((n_remote,)),
                        pltpu.SemaphoreType.DMA((nb + 1,))],
    )(*big, small)
    return outs[:nb], outs[nb]


def _reduce_gradients(sectioned, small):
    arrs = list(sectioned) + [small]
    mine, got = _exchange_halves(arrs, name="rs_exchange_halves")
    chip_sum = [_add2(m, g, name=f"rs_add_sibling_{i}") for i, (m, g) in enumerate(zip(mine, got))]
    slots = _scatter_sections(chip_sum, name="rs_scatter_sections")
    red = [_sum4(s, name=f"rs_sum_chips_{i}") for i, s in enumerate(slots)]
    big, sm = _finish_reduce(red[:-1], red[-1], name="rs_finish")
    return big, sm


def kernel(x, c, ctx, c_ctx, w_mod, b_mod, g_pre, g_post, w_in, w_conv, a_log, dt_bias, g_onorm, gm_ln_g, gm_ln_b, w_sp, b_sp, w_pa, w_pb, w_out, loss_target, m_c_ctx, m_w_mod, m_b_mod, m_g_pre, m_g_post, m_w_in, m_w_conv, m_a_log, m_dt_bias, m_g_onorm, m_gm_ln_g, m_gm_ln_b, m_w_sp, m_b_sp, m_w_pa, m_w_pb, m_w_out, v_c_ctx, v_w_mod, v_b_mod, v_g_pre, v_g_post, v_w_in, v_w_conv, v_a_log, v_dt_bias, v_g_onorm, v_gm_ln_g, v_gm_ln_b, v_w_sp, v_b_sp, v_w_pa, v_w_pb, v_w_out):
    names = ["c_ctx", "w_mod", "b_mod", "g_pre", "g_post", "w_in", "w_conv", "a_log", "dt_bias", "g_onorm", "gm_ln_g",
             "gm_ln_b", "w_sp", "b_sp", "w_pa", "w_pb", "w_out"]
    w = dict(zip(names, (c_ctx, w_mod, b_mod, g_pre, g_post, w_in, w_conv, a_log, dt_bias, g_onorm, gm_ln_g, gm_ln_b,
                         w_sp, b_sp, w_pa, w_pb, w_out)))
    m = dict(zip(names, (m_c_ctx, m_w_mod, m_b_mod, m_g_pre, m_g_post, m_w_in, m_w_conv, m_a_log, m_dt_bias, m_g_onorm,
                         m_gm_ln_g, m_gm_ln_b, m_w_sp, m_b_sp, m_w_pa, m_w_pb, m_w_out)))
    v = dict(zip(names, (v_c_ctx, v_w_mod, v_b_mod, v_g_pre, v_g_post, v_w_in, v_w_conv, v_a_log, v_dt_bias, v_g_onorm,
                         v_gm_ln_g, v_gm_ln_b, v_w_sp, v_b_sp, v_w_pa, v_w_pb, v_w_out)))
    xy = 2 * lax.axis_index("x") + lax.axis_index("y")

    rowpack = jnp.concatenate([w_pa[0], w_pb[0], w_out[0]], axis=0).astype(BF16)
    (wm_all, win_all, row_all), wconv_all = _gather_shards(
        [w_mod[0].astype(BF16), w_in[0].astype(BF16), rowpack], w_conv[0], name="gather_weights")
    cols = lambda a: jnp.concatenate([a[s] for s in range(N_CHIPS)], axis=1)
    w_mod_f, w_in_f, w_conv_f = cols(wm_all), cols(win_all), cols(wconv_all)
    blk = D // N_CHIPS
    rows_of = lambda i: jnp.concatenate([row_all[s, i * blk:(i + 1) * blk] for s in range(N_CHIPS)], axis=0)
    w_ab = jnp.pad(w_in_f[:, OFF_A:OFF_ZB], ((0, 0), (0, DH - 4 * H)))

    loss_local, grad_x, g = _local_step(
        x[0], c, ctx[0], c_ctx, loss_target[0], w_mod_f, b_mod, g_pre, g_post, w_in_f[:, :OFF_A], w_ab, w_in_f[:, OFF_ZB:],
        w_conv_f, a_log[0], dt_bias[0], g_onorm, gm_ln_g, gm_ln_b, w_sp[0], b_sp[0], rows_of(0), rows_of(1), rows_of(2))
    loss = lax.psum(loss_local, ("x", "y", "c"))

    sect_cols = lambda a: jnp.stack(jnp.split(a, N_CHIPS, axis=1))
    g_rows = jnp.stack([jnp.concatenate([g[k][s * blk:(s + 1) * blk] for k in ("w_pa", "w_pb", "w_out")], axis=0)
                        for s in range(N_CHIPS)])
    flat = jnp.zeros((N_CHIPS * SMALL_ROWS * 128,), F32)
    for k, (off, size) in SMALL_LAYOUT.items():
        if k != "_pad":
            flat = lax.dynamic_update_slice(flat, g[k].reshape(-1), (off,))
    (gr_wm, gr_win, gr_rows), gr_small = _reduce_gradients(
        [sect_cols(g["w_mod"]), sect_cols(g["w_in"]), g_rows], flat.reshape(N_CHIPS, SMALL_ROWS, 128))
    gr_flat = gr_small.reshape(-1)

    res = {}
    res["w_mod"] = _adamw(w_mod[0], gr_wm, m_w_mod[0], v_w_mod[0], name="adamw_w_mod")
    res["w_in"] = _adamw(w_in[0], gr_win, m_w_in[0], v_w_in[0], name="adamw_w_in")
    for i, k in enumerate(("w_pa", "w_pb", "w_out")):
        res[k] = _adamw(w[k][0], gr_rows, m[k][0], v[k][0], g_row_block=i, name=f"adamw_{k}")
    small_names = [k for k in SMALL_LAYOUT if k not in ("_pad", "w_conv")]

    def pack(src):
        buf = jnp.zeros((N_CHIPS * SMALL_ROWS * 128,), F32)
        for k in small_names:
            buf = lax.dynamic_update_slice(buf, src[k].reshape(-1), (SMALL_LAYOUT[k][0],))
        return buf.reshape(-1, 128)

    sm_res = _adamw(pack(w), gr_small, pack(m), pack(v), name="adamw_small")
    for k in small_names:
        off, size = SMALL_LAYOUT[k]
        res[k] = [r.reshape(-1)[off:off + size].reshape(w[k].shape) for r in sm_res]
    off, size = SMALL_LAYOUT["w_conv"]
    g_conv = lax.dynamic_slice(gr_flat[off:off + size].reshape(3, 3 * D), (0, xy * (3 * D // N_CHIPS)), (3, 3 * D // N_CHIPS))
    conv_res = _adamw(jnp.pad(w_conv[0], ((0, 5), (0, 0))), jnp.pad(g_conv, ((0, 5), (0, 0))),
                      jnp.pad(m_w_conv[0], ((0, 5), (0, 0))), jnp.pad(v_w_conv[0], ((0, 5), (0, 0))), name="adamw_w_conv")
    res["w_conv"] = [r[:3] for r in conv_res]
    for k in ("w_mod", "w_in", "w_pa", "w_pb", "w_out", "w_conv"):
        res[k] = [r.reshape(w[k].shape) for r in res[k]]

    out = [loss, grad_x[None]]
    for i in range(4):
        out += [res[k][i] for k in names]
    return tuple(out)
```

```python
import functools

import jax
import jax.numpy as jnp
from jax import lax
from jax.experimental import pallas as pl
from jax.experimental.pallas import tpu as pltpu

F32 = jnp.float32
BF16 = jnp.bfloat16
HI = lax.Precision.HIGHEST
MESH = pl.DeviceIdType.MESH

D = 1024
H = 8
DH = 128
CH = 64
LOG_CH = 6
PAIR = 2 * CH
GM = 128
assert 1 << LOG_CH == CH and PAIR == DH
PREC_SOLVE = lax.Precision.HIGH
PREC_POWERS = (PREC_SOLVE,) * (LOG_CH - 1)
HEADS_PER_ITER_FWD = 8
HEADS_PER_ITER_BWD = 4
EPS = 1e-6
N_CHIPS = 4
OFF_A = 3 * D
OFF_ZB = OFF_A + 4 * H
IN_COLS = OFF_ZB + 6 * D
VMEM_LIMIT_V7X = 56 * 1024 * 1024
DMA_CHUNK_BYTES = 2 * 1024 * 1024

ADAM_LR, ADAM_B1, ADAM_B2, ADAM_EPS, ADAM_WD, ADAM_STEP = 0.001, 0.9, 0.999, 1e-08, 0.01, 10

SMALL_LAYOUT = {}
_off = 0
for _n, _s in (("c_ctx", D), ("b_mod", 3 * D), ("g_pre", D), ("g_post", D), ("gm_ln_g", D), ("gm_ln_b", D),
               ("b_sp", H * GM), ("g_onorm", DH), ("a_log", 2 * H), ("dt_bias", 2 * H), ("_pad", 96),
               ("w_conv", 3 * 3 * D), ("w_sp", H * GM * GM)):
    SMALL_LAYOUT[_n] = (_off, _s)
    _off += _s
SMALL_ROWS = 304
assert N_CHIPS * SMALL_ROWS * 128 >= _off and (SMALL_ROWS // 2) % 8 == 0


def _params(sem=None):
    return pltpu.CompilerParams(dimension_semantics=sem, vmem_limit_bytes=VMEM_LIMIT_V7X)


def _tile(n, cands=(256, 128, 64, 32, 16, 8)):
    for cand in cands:
        if n % cand == 0:
            return cand
    return n


def _silu(x):
    return x * jax.nn.sigmoid(x)


def _gelu(x):
    return 0.5 * x * (1.0 + jnp.tanh(0.7978845608028654 * (x + 0.044715 * (x * x * x))))


def _mm(a, b, *, ta=False, tb=False, out_dtype=F32, tm=512, tn=1024, tk=1024, pre_silu=False, name):
    m, k = (a.shape[1], a.shape[0]) if ta else a.shape
    n = b.shape[0] if tb else b.shape[1]
    tm, tn, tk = min(tm, m), min(tn, n), min(tk, k)
    assert m % tm == 0 and n % tn == 0 and k % tk == 0, (name, m, n, k, tm, tn, tk)
    nk = k // tk
    dims = (((0,) if ta else (1,), (1,) if tb else (0,)), ((), ()))

    def body(a_ref, b_ref, o_ref, acc_ref):
        kk = pl.program_id(2)
        av = a_ref[...]
        if pre_silu:
            av = _silu(av.astype(F32))
        part = lax.dot_general(av.astype(BF16), b_ref[...].astype(BF16), dims, preferred_element_type=F32)

        @pl.when(kk == 0)
        def _():
            acc_ref[...] = part

        @pl.when(kk > 0)
        def _():
            acc_ref[...] += part

        @pl.when(kk == nk - 1)
        def _():
            o_ref[...] = acc_ref[...].astype(out_dtype)

    a_spec = pl.BlockSpec((tk, tm), lambda i, j, q: (q, i)) if ta else pl.BlockSpec((tm, tk), lambda i, j, q: (i, q))
    b_spec = pl.BlockSpec((tn, tk), lambda i, j, q: (j, q)) if tb else pl.BlockSpec((tk, tn), lambda i, j, q: (q, j))
    return pl.pallas_call(
        body, name=name, grid=(m // tm, n // tn, nk),
        in_specs=[a_spec, b_spec], out_specs=pl.BlockSpec((tm, tn), lambda i, j, q: (i, j)),
        out_shape=jax.ShapeDtypeStruct((m, n), out_dtype),
        scratch_shapes=[pltpu.VMEM((tm, tn), F32)],
        compiler_params=_params(("parallel", "parallel", "arbitrary")),
    )(a, b)


def _h_fn(x, g, m):
    shift, scale = m[:, 0:D], m[:, D:2 * D]
    r = lax.rsqrt(jnp.mean(x * x, axis=-1, keepdims=True) + EPS)
    return (x * r * g) * (1.0 + scale) + shift


def _norm_fwd(x, g, mod, bmod, *, name):
    rows = x.shape[0]
    t = min(512, rows)

    def body(x_ref, g_ref, m_ref, b_ref, h_ref):
        h_ref[...] = _h_fn(x_ref[...], g_ref[...], m_ref[...] + b_ref[...]).astype(BF16)

    vec = lambda w: pl.BlockSpec((1, w), lambda i: (0, 0))
    return pl.pallas_call(
        body, name=name, grid=(rows // t,),
        in_specs=[pl.BlockSpec((t, D), lambda i: (i, 0)), vec(D), vec(3 * D), vec(3 * D)],
        out_specs=pl.BlockSpec((t, D), lambda i: (i, 0)),
        out_shape=jax.ShapeDtypeStruct((rows, D), BF16),
        compiler_params=_params(("parallel",)),
    )(x, g, mod, bmod)


def _norm_bwd(x, g, mod, bmod, dh_parts, resid, *, name):
    rows = x.shape[0]
    t = min(512, rows)
    n_parts = len(dh_parts)
    has_resid = resid is not None

    def body(*refs):
        x_ref, g_ref, m_ref, b_ref = refs[:4]
        parts = refs[4:4 + n_parts]
        r_ref = refs[4 + n_parts] if has_resid else None
        dx_ref, dg_ref, dm_ref = refs[-3:]
        i = pl.program_id(0)
        dh = parts[0][...]
        for p in parts[1:]:
            dh = dh + p[...]
        _, vjp = jax.vjp(_h_fn, x_ref[...], g_ref[...], m_ref[...] + b_ref[...])
        dx, dg, dm = vjp(dh)
        if has_resid:
            dx = dx + r_ref[...]
        dx_ref[...] = dx

        @pl.when(i == 0)
        def _():
            dg_ref[...] = dg
            dm_ref[...] = dm

        @pl.when(i > 0)
        def _():
            dg_ref[...] += dg
            dm_ref[...] += dm

    vec = lambda w: pl.BlockSpec((1, w), lambda i: (0, 0))
    tile = pl.BlockSpec((t, D), lambda i: (i, 0))
    ins = [x, g, mod, bmod, *dh_parts] + ([resid] if has_resid else [])
    return pl.pallas_call(
        body, name=name, grid=(rows // t,),
        in_specs=[tile, vec(D), vec(3 * D), vec(3 * D)] + [tile] * (n_parts + int(has_resid)),
        out_specs=[tile, vec(D), vec(3 * D)],
        out_shape=[jax.ShapeDtypeStruct((rows, D), F32), jax.ShapeDtypeStruct((1, D), F32),
                   jax.ShapeDtypeStruct((1, 3 * D), F32)],
        compiler_params=_params(("arbitrary",)),
    )(*ins)


def _conv_tile(u_ref, r0, t, rows, w0, w1, w2):
    u = u_ref[pl.ds(r0, t), :]
    prev8 = u_ref[pl.ds(pl.multiple_of(jnp.maximum(r0 - 8, 0), 8), 8), :]
    next8 = u_ref[pl.ds(pl.multiple_of(jnp.minimum(r0 + t, rows - 8), 8), 8), :]
    r8 = lax.broadcasted_iota(jnp.int32, (8, DH), 0)
    prev_row = jnp.sum(jnp.where(r8 == 7, prev8, 0.0), axis=0, keepdims=True)
    next_row = jnp.sum(jnp.where(r8 == 0, next8, 0.0), axis=0, keepdims=True)
    prev_row = jnp.where(r0 > 0, prev_row, 0.0)
    next_row = jnp.where(r0 + t < rows, next_row, 0.0)
    ri = lax.broadcasted_iota(jnp.int32, (t, DH), 0)
    um1 = jnp.where(ri == 0, prev_row, pltpu.roll(u, 1, 0))
    up1 = jnp.where(ri == t - 1, next_row, pltpu.roll(u, t - 1, 0))
    return w0 * um1 + w1 * u + w2 * up1, um1, u, up1


def _rowlocal(z, is_norm):
    y = _silu(z)
    yn = y * lax.rsqrt(jnp.sum(y * y, axis=-1, keepdims=True) + EPS)
    return jnp.where(is_norm, yn, y)


def _prep_fwd(p, wconv, n_norm, *, name):
    rows, nb = p.shape[0], p.shape[1] // DH
    t = min(512, rows)

    def body(u_ref, w_ref, o_ref):
        is_norm = pl.program_id(0) < n_norm
        w0, w1, w2 = w_ref[0:1, :], w_ref[1:2, :], w_ref[2:3, :]

        def step(s, carry):
            r0 = pl.multiple_of(s * t, t)
            z, _, _, _ = _conv_tile(u_ref, r0, t, rows, w0, w1, w2)
            o_ref[0, pl.ds(r0, t), :] = _rowlocal(z, is_norm)
            return carry

        lax.fori_loop(0, rows // t, step, 0)

    return pl.pallas_call(
        body, name=name, grid=(nb,),
        in_specs=[pl.BlockSpec((rows, DH), lambda j: (0, j)), pl.BlockSpec((3, DH), lambda j: (0, j))],
        out_specs=pl.BlockSpec((1, rows, DH), lambda j: (j, 0, 0)),
        out_shape=jax.ShapeDtypeStruct((nb, rows, DH), F32),
        compiler_params=_params(("parallel",)),
    )(p, wconv)


def _prep_bwd(p, wconv, d_a, d_b, n_norm, *, name):
    rows, nb = p.shape[0], p.shape[1] // DH
    t = min(512, rows)

    def body(u_ref, w_ref, da_ref, db_ref, du_ref, dw_ref, dz_ref):
        is_norm = pl.program_id(0) < n_norm
        w0, w1, w2 = w_ref[0:1, :], w_ref[1:2, :], w_ref[2:3, :]

        def step1(s, carry):
            a0, a1, a2 = carry
            r0 = pl.multiple_of(s * t, t)
            z, um1, u, up1 = _conv_tile(u_ref, r0, t, rows, w0, w1, w2)
            _, vjp = jax.vjp(lambda zz: _rowlocal(zz, is_norm), z)
            (dz,) = vjp(da_ref[0, pl.ds(r0, t), :] + db_ref[0, pl.ds(r0, t), :])
            dz_ref[pl.ds(r0, t), :] = dz
            red = lambda v: jnp.sum(v, axis=0, keepdims=True)
            return a0 + red(dz * um1), a1 + red(dz * u), a2 + red(dz * up1)

        zero = jnp.zeros((1, DH), F32)
        a0, a1, a2 = lax.fori_loop(0, rows // t, step1, (zero, zero, zero))
        dw_ref[0:1, :] = a0
        dw_ref[1:2, :] = a1
        dw_ref[2:3, :] = a2

        def step2(s, carry):
            r0 = pl.multiple_of(s * t, t)
            du, _, _, _ = _conv_tile(dz_ref, r0, t, rows, w2, w1, w0)
            du_ref[pl.ds(r0, t), :] = du.astype(BF16)
            return carry

        lax.fori_loop(0, rows // t, step2, 0)

    col = pl.BlockSpec((rows, DH), lambda j: (0, j))
    w_spec = pl.BlockSpec((3, DH), lambda j: (0, j))
    d_spec = pl.BlockSpec((1, rows, DH), lambda j: (j, 0, 0))
    return pl.pallas_call(
        body, name=name, grid=(nb,),
        in_specs=[col, w_spec, d_spec, d_spec], out_specs=[col, w_spec],
        out_shape=[jax.ShapeDtypeStruct((rows, nb * DH), BF16), jax.ShapeDtypeStruct((3, nb * DH), F32)],
        scratch_shapes=[pltpu.VMEM((rows, DH), F32)],
        compiler_params=_params(("parallel",)),
    )(p, wconv, d_a, d_b)


def _gates_fn(pab, avec, dvec):
    t = pab.shape[0]
    lane = lax.broadcasted_iota(jnp.int32, pab.shape, 1)
    xg = pab + dvec
    sp = jnp.maximum(xg, 0.0) + jnp.log(1.0 + jnp.exp(-jnp.abs(xg)))
    g = jnp.where(lane < 2 * H, -jnp.exp(avec) * sp, 0.0)
    ii = lax.broadcasted_iota(jnp.int32, (t, t), 0)
    jj = lax.broadcasted_iota(jnp.int32, (t, t), 1)
    same = (ii >> LOG_CH) == (jj >> LOG_CH)
    cum_f = _dot(jnp.where(same & (jj <= ii), 1.0, 0.0), g, precision=HI)
    cum_r = _dot(jnp.where(same & (jj >= ii), 1.0, 0.0), g, precision=HI)
    return jnp.where(lane < H, cum_f, jnp.where(lane < 2 * H, cum_r, jnp.where(lane < 4 * H, jax.nn.sigmoid(pab), 0.0)))


def _gates_fwd(pab, avec, dvec, *, name):
    rows = pab.shape[0]
    t = min(512, rows)

    def body(p_ref, a_ref, d_ref, o_ref):
        o_ref[...] = _gates_fn(p_ref[...], a_ref[...], d_ref[...])

    vec = pl.BlockSpec((1, DH), lambda i: (0, 0))
    tile = pl.BlockSpec((t, DH), lambda i: (i, 0))
    return pl.pallas_call(
        body, name=name, grid=(rows // t,), in_specs=[tile, vec, vec], out_specs=tile,
        out_shape=jax.ShapeDtypeStruct((rows, DH), F32), compiler_params=_params(("parallel",)),
    )(pab, avec, dvec)


def _gates_bwd(pab, avec, dvec, d_a, d_b, *, name):
    rows = pab.shape[0]
    t = min(512, rows)

    def body(p_ref, a_ref, d_ref, da_ref, db_ref, dp_ref, dav_ref, ddv_ref):
        i = pl.program_id(0)
        _, vjp = jax.vjp(_gates_fn, p_ref[...], a_ref[...], d_ref[...])
        dp, dav, ddv = vjp(da_ref[...] + db_ref[...])
        dp_ref[...] = dp.astype(BF16)

        @pl.when(i == 0)
        def _():
            dav_ref[...] = dav
            ddv_ref[...] = ddv

        @pl.when(i > 0)
        def _():
            dav_ref[...] += dav
            ddv_ref[...] += ddv

    vec = pl.BlockSpec((1, DH), lambda i: (0, 0))
    tile = pl.BlockSpec((t, DH), lambda i: (i, 0))
    return pl.pallas_call(
        body, name=name, grid=(rows // t,), in_specs=[tile, vec, vec, tile, tile], out_specs=[tile, vec, vec],
        out_shape=[jax.ShapeDtypeStruct((rows, DH), BF16), jax.ShapeDtypeStruct((1, DH), F32),
                   jax.ShapeDtypeStruct((1, DH), F32)],
        compiler_params=_params(("arbitrary",)),
    )(pab, avec, dvec, d_a, d_b)


def _dot(a, b, dims=((1,), (0,)), precision=None):
    return lax.dot_general(a, b, (dims, ((), ())), precision=precision, preferred_element_type=F32)


_NT = ((1,), (1,))
_TN = ((0,), (0,))


def _pairs(s, q, k, v, gcol, bcol, revs):
    idx = range(len(revs))
    ii = lax.broadcasted_iota(jnp.int32, (PAIR, PAIR), 0)
    jj = lax.broadcasted_iota(jnp.int32, (PAIR, PAIR), 1)
    same = (ii >> LOG_CH) == (jj >> LOG_CH)
    incl_d = (same & (ii >= jj), same & (ii <= jj))
    strict_d = (same & (ii > jj), same & (ii < jj))
    incl = [incl_d[int(r)] for r in revs]
    strict = [strict_d[int(r)] for r in revs]
    eye = jnp.where(ii == jj, 1.0, 0.0)
    gc_i = [jnp.broadcast_to(gcol[i], (PAIR, DH)) for i in idx]
    gc_j = [gc_i[i].T for i in idx]
    decay = [jnp.where(incl[i], jnp.exp(jnp.where(incl[i], gc_i[i] - gc_j[i], 0.0)), 0.0) for i in idx]
    b_b = [jnp.broadcast_to(bcol[i], (PAIR, DH)) for i in idx]
    kb = [k[i] * b_b[i] for i in idx]
    kk = [_dot(kb[i], k[i], _NT) for i in idx]
    bp = [jnp.where(strict[i], -kk[i] * decay[i], 0.0) for i in idx]
    inv = [eye + bp[i] for i in idx]
    for prec in PREC_POWERS:
        bp = [_dot(bp[i], bp[i], precision=prec) for i in idx]
        more = [_dot(inv[i], bp[i], precision=prec) for i in idx]
        inv = [inv[i] + more[i] for i in idx]
    eg = [jnp.exp(gc_i[i]) for i in idx]
    sol = [_dot(inv[i], jnp.concatenate([v[i] * b_b[i], kb[i] * eg[i]], axis=1), precision=PREC_SOLVE) for i in idx]
    u_val = [sol[i][:, :DH] for i in idx]
    w_key = [sol[i][:, DH:] for i in idx]
    row = lax.broadcasted_iota(jnp.int32, (PAIR, 1), 0)
    has_q = q[0] is not None
    if has_q:
        qc = [q[i] * (DH ** -0.5) for i in idx]
        qk = [_dot(qc[i], k[i], _NT) for i in idx]
        attn = [qk[i] * decay[i] for i in idx]
        qd = [qc[i] * eg[i] for i in idx]
    outs = [[None, None] for _ in idx]
    zeros = jnp.zeros((CH, DH), F32)
    for step in range(2):
        cidx = [(1 - step) if revs[i] else step for i in idx]
        sl = [slice(c * CH, (c + 1) * CH) for c in cidx]
        last = [c * CH if revs[i] else c * CH + CH - 1 for i, c in zip(idx, cidx)]
        gl = [jnp.sum(jnp.where(row == last[i], gcol[i], 0.0), axis=0, keepdims=True) for i in idx]
        k_tail = [k[i][sl[i]] * jnp.exp(gl[i] - gc_i[i][sl[i]]) for i in idx]
        ws = [_dot(w_key[i][sl[i]], s[i]) for i in idx]
        v_new = [u_val[i][sl[i]] - ws[i] for i in idx]
        if has_q:
            v_pad = [jnp.concatenate([v_new[i], zeros] if cidx[i] == 0 else [zeros, v_new[i]], axis=0) for i in idx]
            o_state = [_dot(qd[i][sl[i]], s[i]) for i in idx]
            o_local = [_dot(attn[i][sl[i]], v_pad[i]) for i in idx]
            for i in idx:
                outs[i][cidx[i]] = o_state[i] + o_local[i]
        kv = [_dot(k_tail[i], v_new[i], _TN) for i in idx]
        s = [s[i] * jnp.exp(gl[i]) + kv[i] for i in idx]
    return s, ([jnp.concatenate(outs[i], axis=0) for i in idx] if has_q else None)


def _lane_col(tile, idx):
    lane = lax.broadcasted_iota(jnp.int32, tile.shape, 1)
    return jnp.sum(jnp.where(lane == idx, tile, 0.0), axis=1, keepdims=True)


def _gdn_fwd(qkv, gb, s0f, s0b, has_q, *, name):
    nb, rows, _ = qkv.shape
    n = rows // PAIR
    qoff = H if has_q else 0

    def body(qf_ref, qb_ref, gf_ref, gr_ref, s0f_ref, s0b_ref, of_ref, ob_ref, ssf_ref, ssb_ref, sf_ref, sb_ref):
        @pl.when(pl.program_id(0) == 0)
        def _():
            sf_ref[...] = s0f_ref[...]
            sb_ref[...] = s0b_ref[...]

        gtiles = (gf_ref[...], gr_ref[...])

        dirs = ((qf_ref, sf_ref, ssf_ref, of_ref), (qb_ref, sb_ref, ssb_ref, ob_ref))

        def heads(hg, carry):
            work = [(hg * HEADS_PER_ITER_FWD + j, d) for j in range(HEADS_PER_ITER_FWD) for d in range(2)]
            loaded = []
            for h, d in work:
                q_ref, s_ref, _, _ = dirs[d]
                loaded.append((s_ref[h], q_ref[h] if has_q else None, q_ref[qoff + h], q_ref[qoff + H + h],
                               _lane_col(gtiles[d], d * H + h), _lane_col(gtiles[d], 2 * H + d * H + h)))
            s_new, o = _pairs(*[list(col) for col in zip(*loaded)], revs=[d == 1 for _, d in work])
            for i, (h, d) in enumerate(work):
                _, s_ref, ss_ref, o_ref = dirs[d]
                ss_ref[0, h] = loaded[i][0]
                s_ref[h] = s_new[i]
                o_ref[h] = o[i] if has_q else jnp.zeros((PAIR, DH), F32)
            return carry

        if HEADS_PER_ITER_FWD == H:
            heads(0, 0)
        else:
            lax.fori_loop(0, H // HEADS_PER_ITER_FWD, heads, 0)

    fwd3 = lambda i: (0, i, 0)
    rev3 = lambda i: (0, n - 1 - i, 0)
    state = pl.BlockSpec((H, DH, DH), lambda i: (0, 0, 0))
    saved = pl.BlockSpec((1, H, DH, DH), lambda i: (i, 0, 0, 0))
    return pl.pallas_call(
        body, name=name, grid=(n,),
        in_specs=[pl.BlockSpec((nb, PAIR, DH), fwd3), pl.BlockSpec((nb, PAIR, DH), rev3),
                  pl.BlockSpec((PAIR, DH), lambda i: (i, 0)), pl.BlockSpec((PAIR, DH), lambda i: (n - 1 - i, 0)),
                  state, state],
        out_specs=[pl.BlockSpec((H, PAIR, DH), fwd3), pl.BlockSpec((H, PAIR, DH), rev3), saved, saved, state, state],
        out_shape=[jax.ShapeDtypeStruct((H, rows, DH), F32)] * 2 + [jax.ShapeDtypeStruct((n, H, DH, DH), F32)] * 2
        + [jax.ShapeDtypeStruct((H, DH, DH), F32)] * 2,
        compiler_params=_params(("arbitrary",)),
    )(qkv, qkv, gb, gb, s0f, s0b)


def _gdn_bwd(qkv, gb, ssf, ssb, do, dsf, dsb, has_q, *, name):
    nb, rows, _ = qkv.shape
    n = rows // PAIR
    qoff = H if has_q else 0

    def body(qf_ref, qb_ref, gf_ref, gr_ref, ssf_ref, ssb_ref, dof_ref, dob_ref, dsf0_ref, dsb0_ref,
             dqf_ref, dqb_ref, dgf_ref, dgr_ref, dsf_ref, dsb_ref):
        @pl.when(pl.program_id(0) == 0)
        def _():
            dsf_ref[...] = dsf0_ref[...]
            dsb_ref[...] = dsb0_ref[...]

        gtiles = (gf_ref[...], gr_ref[...])
        lane = lax.broadcasted_iota(jnp.int32, (PAIR, DH), 1)

        dirs = ((qf_ref, ssf_ref, dof_ref, dsf_ref, dqf_ref), (qb_ref, ssb_ref, dob_ref, dsb_ref, dqb_ref))

        def heads(hg, carry):
            out = list(carry)
            work = [(hg * HEADS_PER_ITER_BWD + j, d) for j in range(HEADS_PER_ITER_BWD) for d in range(2)]
            revs = [d == 1 for _, d in work]
            s_in, q_in, k_in, v_in, g_in, b_in, ds_out, do_out = [], [], [], [], [], [], [], []
            for h, d in work:
                q_ref, ss_ref, do_ref, ds_ref, _ = dirs[d]
                s_in.append(ss_ref[0, h])
                q_in.append(q_ref[h] if has_q else None)
                k_in.append(q_ref[qoff + h])
                v_in.append(q_ref[qoff + H + h])
                g_in.append(_lane_col(gtiles[d], d * H + h))
                b_in.append(_lane_col(gtiles[d], 2 * H + d * H + h))
                ds_out.append(ds_ref[h])
                do_out.append(do_ref[h] if has_q else None)
            if has_q:
                _, vjp = jax.vjp(lambda s_, q_, k_, v_, g_, b_: _pairs(s_, q_, k_, v_, g_, b_, revs),
                                 s_in, q_in, k_in, v_in, g_in, b_in)
                ds, dq, dk, dv, dg, db = vjp((ds_out, do_out))
            else:
                _, vjp = jax.vjp(lambda s_, k_, v_, g_, b_: _pairs(s_, q_in, k_, v_, g_, b_, revs)[0],
                                 s_in, k_in, v_in, g_in, b_in)
                ds, dk, dv, dg, db = vjp(ds_out)
            for i, (h, d) in enumerate(work):
                dq_ref, ds_ref = dirs[d][4], dirs[d][3]
                ds_ref[h] = ds[i]
                if has_q:
                    dq_ref[h] = dq[i]
                dq_ref[qoff + h], dq_ref[qoff + H + h] = dk[i], dv[i]
                out[d] = (out[d] + jnp.where(lane == d * H + h, dg[i], 0.0)
                          + jnp.where(lane == 2 * H + d * H + h, db[i], 0.0))
            return tuple(out)

        zero = jnp.zeros((PAIR, DH), F32)
        if HEADS_PER_ITER_BWD == H:
            dgf, dgr = heads(0, (zero, zero))
        else:
            dgf, dgr = lax.fori_loop(0, H // HEADS_PER_ITER_BWD, heads, (zero, zero))
        dgf_ref[...] = dgf
        dgr_ref[...] = dgr

    fwd3 = lambda i: (0, n - 1 - i, 0)
    rev3 = lambda i: (0, i, 0)
    state = pl.BlockSpec((H, DH, DH), lambda i: (0, 0, 0))
    saved = pl.BlockSpec((1, H, DH, DH), lambda i: (n - 1 - i, 0, 0, 0))
    gf_spec = pl.BlockSpec((PAIR, DH), lambda i: (n - 1 - i, 0))
    gr_spec = pl.BlockSpec((PAIR, DH), lambda i: (i, 0))
    return pl.pallas_call(
        body, name=name, grid=(n,),
        in_specs=[pl.BlockSpec((nb, PAIR, DH), fwd3), pl.BlockSpec((nb, PAIR, DH), rev3), gf_spec, gr_spec, saved, saved,
                  pl.BlockSpec((H, PAIR, DH), fwd3), pl.BlockSpec((H, PAIR, DH), rev3), state, state],
        out_specs=[pl.BlockSpec((nb, PAIR, DH), fwd3), pl.BlockSpec((nb, PAIR, DH), rev3), gf_spec, gr_spec, state, state],
        out_shape=[jax.ShapeDtypeStruct((nb, rows, DH), F32)] * 2 + [jax.ShapeDtypeStruct((rows, DH), F32)] * 2
        + [jax.ShapeDtypeStruct((H, DH, DH), F32)] * 2,
        compiler_params=_params(("arbitrary",)),
    )(qkv, qkv, gb, gb, ssf, ssb, do, do, dsf, dsb)


def _stage1(zb, ua, va, za, o, gon, lng, lnb, wsp, bsp):
    gv = [_gelu(t) for t in va]
    mu = sum(jnp.sum(t, axis=-1, keepdims=True) for t in gv) * (1.0 / D)
    xc = [t - mu for t in gv]
    var = sum(jnp.sum(t * t, axis=-1, keepdims=True) for t in xc) * (1.0 / D)
    rs = lax.rsqrt(var + EPS)
    ya, yb = [], []
    for g in range(H):
        vv = xc[g] * rs * lng[g] + lnb[g]
        s = _dot(wsp[g], vv) + bsp[g]
        ya.append(_gelu(ua[g]) * s * _silu(za[g]))
        r = lax.rsqrt(jnp.mean(o[g] * o[g], axis=-1, keepdims=True) + EPS)
        yb.append(o[g] * r * gon * _silu(zb[g]))
    return ya, yb


def _stage2(ma, mb, ga, gb):
    return jax.nn.sigmoid(ga) * ma + jax.nn.sigmoid(gb) * mb


def _stage3(out, x, tgt, gpost, gate):
    r = out * lax.rsqrt(jnp.mean(out * out, axis=-1, keepdims=True) + EPS) * gpost
    err = x + gate * r - tgt
    return 0.5 * jnp.sum(jnp.mean(err * err, axis=-1, keepdims=True), axis=0, keepdims=True)


def _post(p_rest, o_f, o_b, x, tgt, mod, bmod, gon, lng, lnb, wsp, bspb, wpa, wpb, wout, gpost, *, name):
    rows = x.shape[0]
    n = rows // GM
    lanes = lambda g: slice(g * DH, (g + 1) * DH)
    bdot = lambda a, w_ref: _dot(a.astype(BF16), w_ref[...])
    bdot_t = lambda a, w_ref: _dot(a.astype(BF16), w_ref[...], _NT)

    def body(p_ref, of_ref, ob_ref, x_ref, t_ref, m_ref, bm_ref, gon_ref, lng_ref, lnb_ref, wsp_ref, bsp_ref,
             wpa_ref, wpb_ref, wout_ref, gp_ref,
             loss_ref, dp_ref, do_ref, dx_ref, ya_ref, yb_ref, mg_ref, dma_ref, dmb_ref, dout_ref,
             dvec_ref, dgon_ref, dwsp_ref, dbsp_ref):
        @pl.when(pl.program_id(0) == 0)
        def _():
            loss_ref[...] = jnp.zeros_like(loss_ref)
            dvec_ref[...] = jnp.zeros_like(dvec_ref)
            dgon_ref[...] = jnp.zeros_like(dgon_ref)
            dwsp_ref[...] = jnp.zeros_like(dwsp_ref)
            dbsp_ref[...] = jnp.zeros_like(dbsp_ref)

        piece = lambda blk: [p_ref[:, blk * D + g * DH: blk * D + (g + 1) * DH] for g in range(H)]
        zb, ua, va, za = piece(0), piece(1), piece(2), piece(3)
        o = [of_ref[g] + ob_ref[g] for g in range(H)]
        gon = gon_ref[...]
        lng = [lng_ref[:, lanes(g)] for g in range(H)]
        lnb = [lnb_ref[:, lanes(g)] for g in range(H)]
        wsp = [wsp_ref[g] for g in range(H)]
        bsp = [bsp_ref[g] for g in range(H)]
        (ya, yb), vjp1 = jax.vjp(_stage1, zb, ua, va, za, o, gon, lng, lnb, wsp, bsp)
        y_a, y_b = jnp.concatenate(ya, axis=1), jnp.concatenate(yb, axis=1)
        ma, mb = bdot(y_a, wpa_ref), bdot(y_b, wpb_ref)
        ga, gb = p_ref[:, 4 * D:5 * D], p_ref[:, 5 * D:6 * D]
        merged, vjp2 = jax.vjp(_stage2, ma, mb, ga, gb)
        out = bdot(merged, wout_ref)
        gate = m_ref[:, 2 * D:3 * D] + bm_ref[:, 2 * D:3 * D]
        loss, vjp3 = jax.vjp(_stage3, out, x_ref[...], t_ref[...], gp_ref[...], gate)
        loss_ref[...] += jnp.broadcast_to(loss, loss_ref.shape)

        dout, dx, _, dgpost, dgate = vjp3(jnp.ones((1, 1), F32))
        dx_ref[...] = dx
        dmerged = bdot_t(dout, wout_ref)
        dma, dmb, dga, dgb = vjp2(dmerged)
        dya, dyb = bdot_t(dma, wpa_ref), bdot_t(dmb, wpb_ref)
        dzb, dua, dva, dza, do, dgon, dlng, dlnb, dwsp, dbsp = vjp1(
            ([dya[:, lanes(g)] for g in range(H)], [dyb[:, lanes(g)] for g in range(H)]))

        for blk, dlist in enumerate((dzb, dua, dva, dza)):
            for g in range(H):
                dp_ref[:, blk * D + g * DH: blk * D + (g + 1) * DH] = dlist[g].astype(BF16)
        dp_ref[:, 4 * D:5 * D] = dga.astype(BF16)
        dp_ref[:, 5 * D:6 * D] = dgb.astype(BF16)
        for g in range(H):
            do_ref[g] = do[g]
            dwsp_ref[g] += dwsp[g]
            dbsp_ref[g] += dbsp[g]
            dvec_ref[2:3, lanes(g)] += dlng[g]
            dvec_ref[3:4, lanes(g)] += dlnb[g]
        dvec_ref[0:1, :] += dgpost
        dvec_ref[1:2, :] += dgate
        dgon_ref[0:1, :] += dgon
        ya_ref[...] = y_a.astype(BF16)
        yb_ref[...] = y_b.astype(BF16)
        mg_ref[...] = merged.astype(BF16)
        dma_ref[...] = dma.astype(BF16)
        dmb_ref[...] = dmb.astype(BF16)
        dout_ref[...] = dout.astype(BF16)

    row = lambda w: pl.BlockSpec((GM, w), lambda i: (i, 0))
    heads = pl.BlockSpec((H, GM, DH), lambda i: (0, i, 0))
    full = lambda shape: pl.BlockSpec(shape, lambda i: tuple(0 for _ in shape))
    sds = jax.ShapeDtypeStruct
    return pl.pallas_call(
        body, name=name, grid=(n,),
        in_specs=[row(6 * D), heads, heads, row(D), row(D), full((1, 3 * D)), full((1, 3 * D)), full((1, DH)),
                  full((1, D)), full((1, D)), full((H, GM, GM)), full((H, GM, GM)),
                  full((D, D)), full((D, D)), full((D, D)), full((1, D))],
        out_specs=[full((8, DH)), row(6 * D), heads, row(D)] + [row(D)] * 6
        + [full((8, D)), full((8, DH)), full((H, GM, GM)), full((H, GM, GM))],
        out_shape=[sds((8, DH), F32), sds((rows, 6 * D), BF16), sds((H, rows, DH), F32), sds((rows, D), F32)]
        + [sds((rows, D), BF16)] * 6 + [sds((8, D), F32), sds((8, DH), F32), sds((H, GM, GM), F32), sds((H, GM, GM), F32)],
        compiler_params=_params(("arbitrary",)),
    )(p_rest, o_f, o_b, x, tgt, mod, bmod, gon, lng, lnb, wsp, bspb, wpa, wpb, wout, gpost)


def _dsilu_mul(c, d, *, name):
    def body(c_ref, d_ref, o_ref):
        _, vjp = jax.vjp(_silu, c_ref[...])
        (o_ref[...],) = vjp(d_ref[...])

    return pl.pallas_call(body, name=name, out_shape=jax.ShapeDtypeStruct(c.shape, F32))(c, d)


def _adamw(w, g, m, v, *, g_row_block=0, name):
    rows, cols = w.shape
    t = _tile(rows)
    gb = rows // t * g_row_block
    c1 = 1.0 / (1.0 - ADAM_B1 ** ADAM_STEP)
    c2 = 1.0 / (1.0 - ADAM_B2 ** ADAM_STEP)

    def body(w_ref, g_ref, m_ref, v_ref, go_ref, d_ref, mo_ref, vo_ref):
        gv = g_ref[...]
        mn = ADAM_B1 * m_ref[...] + (1.0 - ADAM_B1) * gv
        vn = ADAM_B2 * v_ref[...] + (1.0 - ADAM_B2) * (gv * gv)
        go_ref[...] = gv
        mo_ref[...] = mn
        vo_ref[...] = vn
        d_ref[...] = -ADAM_LR * ((mn * c1) / (jnp.sqrt(vn * c2) + ADAM_EPS) + ADAM_WD * w_ref[...])

    tile = pl.BlockSpec((t, cols), lambda i: (i, 0))
    return pl.pallas_call(
        body, name=name, grid=(rows // t,),
        in_specs=[tile, pl.BlockSpec((t, cols), lambda i: (gb + i, 0)), tile, tile], out_specs=[tile] * 4,
        out_shape=[jax.ShapeDtypeStruct((rows, cols), F32)] * 4,
        compiler_params=_params(("parallel",)),
    )(w, g, m, v)


def _add_sibling(full, got, where, *, name):
    s, rows, cols = full.shape
    hr = rows // 2
    t = _tile(hr)

    def body(w_ref, a_ref, b_ref, o_ref):
        o_ref[...] = a_ref[0] + b_ref[...]

    tile = pl.BlockSpec((1, t, cols), lambda j, i, w: (j, i, 0))
    return pl.pallas_call(
        body, name=name,
        grid_spec=pltpu.PrefetchScalarGridSpec(
            num_scalar_prefetch=1, grid=(s, hr // t),
            in_specs=[pl.BlockSpec((1, 1, t, cols), lambda j, i, w: (j, w[0], i, 0)), tile], out_specs=tile),
        out_shape=jax.ShapeDtypeStruct((s, hr, cols), F32), compiler_params=_params(("parallel", "parallel")),
    )(where, full.reshape(s, 2, hr, cols), got)


def _sum_chips(own, slots, where, n_out, which, *, name):
    _, hr, cols = own.shape
    t = _tile(hr)

    def body(w_ref, a_ref, s_ref, o_ref):
        o_ref[0] = ((a_ref[0] + s_ref[0]) + s_ref[1]) + s_ref[2]

    return pl.pallas_call(
        body, name=name,
        grid_spec=pltpu.PrefetchScalarGridSpec(
            num_scalar_prefetch=1, grid=(hr // t,),
            in_specs=[pl.BlockSpec((1, t, cols), lambda i, w: (w[1], i, 0)),
                      pl.BlockSpec((N_CHIPS - 1, t, cols), lambda i, w: (0, i, 0))],
            out_specs=pl.BlockSpec((1, t, cols), lambda i, w: (w[which], i, 0))),
        out_shape=jax.ShapeDtypeStruct((n_out, hr, cols), F32), compiler_params=_params(("parallel",)),
    )(where, own, slots)


def _local_step(x, c, ctx, c_ctx, tgt, w_mod, b_mod, g_pre, g_post, w_qkv, w_ab, w_rest, w_conv, a_log, dt_bias,
                g_onorm, gm_ln_g, gm_ln_b, w_sp, b_sp, w_pa, w_pb, w_out):
    rows, rows_c = x.shape[0], ctx.shape[0]
    cc = jnp.zeros((16, D), F32).at[0].set(c[0]).at[1].set(c_ctx)
    mod = _mm(cc, w_mod, pre_silu=True, name="mod_fwd")
    mod_x, mod_c = mod[0:1], mod[1:2]
    avec = jnp.zeros((1, DH), F32).at[0, :2 * H].set(a_log.reshape(-1))
    dvec = jnp.zeros((1, DH), F32).at[0, :2 * H].set(dt_bias.reshape(-1))
    bspb = jnp.broadcast_to(b_sp[:, :, None], (H, GM, GM))
    w_kv, wconv_kv = w_qkv[:, D:], w_conv[:, D:]

    h_c = _norm_fwd(ctx, g_pre, mod_c, b_mod, name="norm_fwd_ctx")
    pc_kv = _mm(h_c, w_kv, name="inproj_ctx_kv")
    pc_ab = _mm(h_c, w_ab, name="inproj_ctx_ab")
    kv_c = _prep_fwd(pc_kv, wconv_kv, H, name="prep_fwd_ctx")
    gb_c = _gates_fwd(pc_ab, avec, dvec, name="gates_fwd_ctx")
    s_zero = jnp.zeros((H, DH, DH), F32)
    _, _, ssf_c, ssb_c, s_f, s_b = _gdn_fwd(kv_c, gb_c, s_zero, s_zero, False, name="gdn_fwd_ctx")

    h_x = _norm_fwd(x, g_pre, mod_x, b_mod, name="norm_fwd_x")
    p_qkv = _mm(h_x, w_qkv, name="inproj_qkv")
    p_ab = _mm(h_x, w_ab, name="inproj_ab")
    p_rest = _mm(h_x, w_rest, name="inproj_rest")
    qkv = _prep_fwd(p_qkv, w_conv, 2 * H, name="prep_fwd_x")
    gb_x = _gates_fwd(p_ab, avec, dvec, name="gates_fwd_x")
    o_f, o_b, ssf, ssb, _, _ = _gdn_fwd(qkv, gb_x, s_f, s_b, True, name="gdn_fwd_x")

    (loss_acc, dp_rest, do, dx_res, ya, yb, mg, dma, dmb, dout, dvec_post, dgon, dwsp, dbspb) = _post(
        p_rest, o_f, o_b, x, tgt, mod_x, b_mod, g_onorm, gm_ln_g, gm_ln_b, w_sp, bspb, w_pa, w_pb, w_out, g_post,
        name="post")
    g = {}
    g["w_pa"] = _mm(ya, dma, ta=True, name="dw_pa")
    g["w_pb"] = _mm(yb, dmb, ta=True, name="dw_pb")
    g["w_out"] = _mm(mg, dout, ta=True, name="dw_out")

    zeros_s = jnp.zeros((H, DH, DH), F32)
    dq_f, dq_b, dg_f, dg_b, ds0_f, ds0_b = _gdn_bwd(qkv, gb_x, ssf, ssb, do, zeros_s, zeros_s, True, name="gdn_bwd_x")
    dp_qkv, dwc_x = _prep_bwd(p_qkv, w_conv, dq_f, dq_b, 2 * H, name="prep_bwd_x")
    dp_ab, dav_x, ddv_x = _gates_bwd(p_ab, avec, dvec, dg_f, dg_b, name="gates_bwd_x")
    dkv_f, dkv_b, dgc_f, dgc_b, _, _ = _gdn_bwd(kv_c, gb_c, ssf_c, ssb_c, jnp.zeros((H, rows_c, DH), F32), ds0_f, ds0_b,
                                                 False, name="gdn_bwd_ctx")
    dpc_kv, dwc_c = _prep_bwd(pc_kv, wconv_kv, dkv_f, dkv_b, H, name="prep_bwd_ctx")
    dpc_ab, dav_c, ddv_c = _gates_bwd(pc_ab, avec, dvec, dgc_f, dgc_b, name="gates_bwd_ctx")

    h_all = jnp.concatenate([h_x, h_c], axis=0)
    dpc_qkv = jnp.concatenate([jnp.zeros((rows_c, D), BF16), dpc_kv], axis=1)
    tk = _tile(rows + rows_c, (512, 256, 128, 64))
    dw_qkv = _mm(h_all, jnp.concatenate([dp_qkv, dpc_qkv], axis=0), ta=True, tk=tk, name="dw_qkv")
    dw_ab = _mm(h_all, jnp.concatenate([dp_ab, dpc_ab], axis=0), ta=True, tk=tk, name="dw_ab")
    dw_rest = _mm(h_x, dp_rest, ta=True, tk=512, name="dw_rest")
    g["w_in"] = jnp.concatenate([dw_qkv, dw_ab[:, :4 * H], dw_rest], axis=1)
    dh_parts = [_mm(dp_qkv, w_qkv, tb=True, name="dh_qkv"), _mm(dp_rest, w_rest, tb=True, name="dh_rest"),
                _mm(dp_ab, w_ab, tb=True, name="dh_ab")]
    grad_x, dgpre_x, dm_x = _norm_bwd(x, g_pre, mod_x, b_mod, dh_parts, dx_res, name="norm_bwd_x")
    dhc_parts = [_mm(dpc_kv, w_kv, tb=True, name="dhc_kv"), _mm(dpc_ab, w_ab, tb=True, name="dhc_ab")]
    _, dgpre_c, dm_c = _norm_bwd(ctx, g_pre, mod_c, b_mod, dhc_parts, None, name="norm_bwd_ctx")

    dm_x = dm_x.at[:, 2 * D:].add(dvec_post[1:2])
    dmod = jnp.zeros((16, 3 * D), F32).at[0].set(dm_x[0]).at[1].set(dm_c[0])
    g["w_mod"] = _mm(cc, dmod, ta=True, pre_silu=True, name="dw_mod")
    dcc = _mm(dmod, w_mod, tb=True, name="dcc")
    g["c_ctx"] = _dsilu_mul(cc[:8], dcc[:8], name="dc_ctx")[1]
    g["b_mod"] = dm_x + dm_c
    g["g_pre"] = dgpre_x + dgpre_c
    g["g_post"] = dvec_post[0:1]
    g["gm_ln_g"], g["gm_ln_b"] = dvec_post[2:3], dvec_post[3:4]
    g["g_onorm"] = dgon[0:1]
    g["w_sp"] = dwsp
    g["b_sp"] = jnp.sum(dbspb, axis=-1)
    g["w_conv"] = dwc_x.at[:, D:].add(dwc_c)
    g["a_log"] = (dav_x + dav_c)[0, :2 * H].reshape(2, H)
    g["dt_bias"] = (ddv_x + ddv_c)[0, :2 * H].reshape(2, H)
    return loss_acc[0, 0], grad_x, g


ANY = pl.BlockSpec(memory_space=pl.ANY)


def _place():
    x, y, c = lax.axis_index("x"), lax.axis_index("y"), lax.axis_index("c")
    chips = [(1 - x, y), (x, 1 - y), (1 - x, 1 - y)]
    return x, y, c, (x, y, 1 - c), chips


def _gather_shards(big, small, *, name):
    nb = len(big)

    def body(*refs):
        ins, sm_in = refs[:nb], refs[nb]
        outs, sm_out = refs[nb + 1:2 * nb + 1], refs[2 * nb + 1]
        send, recv = refs[2 * nb + 2:]
        x, y, c, sibling, chips = _place()
        mine = 2 * x + y

        def half(a, shard, hc):
            hr = big[a].shape[0] // 2
            return outs[a].at[shard, pl.ds(hc * hr, hr), :]

        def remote(k, src, dst, to):
            return pltpu.make_async_remote_copy(src_ref=src, dst_ref=dst, send_sem=send.at[k], recv_sem=recv.at[k],
                                                device_id=to, device_id_type=MESH)

        sends = []
        for a in range(nb):
            hr = big[a].shape[0] // 2
            for j, chip in enumerate(chips):
                sends.append(remote(a * 3 + j, ins[a].at[pl.ds(c * hr, hr), :], half(a, mine, c), (*chip, c)))
        for j, chip in enumerate(chips):
            sends.append(remote(nb * 3 + j, sm_in, sm_out.at[mine], (*chip, c)))
        for cp in sends:
            cp.start()
        base = nb * 3 + 3
        passed = []
        for a in range(nb):
            for j, (px, py) in enumerate(chips):
                theirs = 2 * px + py
                remote(a * 3 + j, half(a, theirs, c), half(a, theirs, c), sibling).wait_recv()
                fw = remote(base + a * 3 + j, half(a, theirs, c), half(a, theirs, c), sibling)
                fw.start()
                passed.append(fw)
        for a in range(nb):
            for j, (px, py) in enumerate(chips):
                theirs = 2 * px + py
                remote(base + a * 3 + j, half(a, theirs, 1 - c), half(a, theirs, 1 - c), sibling).wait_recv()
        for j, (px, py) in enumerate(chips):
            remote(nb * 3 + j, sm_in, sm_out.at[2 * px + py], sibling).wait_recv()
        for cp in sends + passed:
            cp.wait_send()

    n_remote = 2 * nb * 3 + 3
    outs = pl.pallas_call(
        body, name=name, in_specs=[ANY] * (nb + 1), out_specs=[ANY] * (nb + 1),
        out_shape=[jax.ShapeDtypeStruct((N_CHIPS,) + a.shape, a.dtype) for a in big + [small]],
        scratch_shapes=[pltpu.SemaphoreType.DMA((n_remote,)), pltpu.SemaphoreType.DMA((n_remote,))],
    )(*big, small)
    return outs[:nb], outs[nb]


def _row_chunks(rows, row_bytes, align=8):
    n = max(1, min(rows // align, -(-rows * row_bytes // DMA_CHUNK_BYTES)))
    per = -(-(-(-rows // n)) // align) * align
    return [(r, min(per, rows - r)) for r in range(0, rows, per)]


def _remote(src, dst, send, recv, to):
    return pltpu.make_async_remote_copy(src_ref=src, dst_ref=dst, send_sem=send, recv_sem=recv, device_id=to,
                                        device_id_type=MESH)


def _exchange_halves(arrs, *, name):
    na = len(arrs)

    def body(*refs):
        ins, got = refs[:na], refs[na:2 * na]
        send, recv = refs[2 * na:]
        x, y, c, sibling, _ = _place()
        for a in range(na):
            ns, rows, cols = arrs[a].shape
            hr = rows // 2
            for s in range(ns):
                for r0, nr in _row_chunks(hr, cols * 4):
                    _remote(ins[a].at[s, pl.ds((1 - c) * hr + r0, nr), :], got[a].at[s, pl.ds(r0, nr), :],
                            send.at[a], recv.at[a], sibling).start()
        for a in range(na):
            hr = arrs[a].shape[1] // 2
            _remote(ins[a].at[:, pl.ds((1 - c) * hr, hr), :], got[a], send.at[a], recv.at[a], sibling).wait()

    return pl.pallas_call(
        body, name=name, in_specs=[ANY] * na, out_specs=[ANY] * na,
        out_shape=[jax.ShapeDtypeStruct((N_CHIPS, a.shape[1] // 2, a.shape[2]), F32) for a in arrs],
        scratch_shapes=[pltpu.SemaphoreType.DMA((na,)), pltpu.SemaphoreType.DMA((na,))],
    )(*arrs)


def _scatter_sections(arrs, *, name):
    na = len(arrs)

    def body(*refs):
        ins, outs = refs[:na], refs[na:2 * na]
        send, recv = refs[2 * na:]
        x, y, c, _, chips = _place()
        for a in range(na):
            _, hr, cols = arrs[a].shape
            for j, (px, py) in enumerate(chips):
                for r0, nr in _row_chunks(hr, cols * 4):
                    _remote(ins[a].at[2 * px + py, pl.ds(r0, nr), :], outs[a].at[j, pl.ds(r0, nr), :],
                            send.at[a * 3 + j], recv.at[a * 3 + j], (px, py, c)).start()
        for a in range(na):
            for j, (px, py) in enumerate(chips):
                _remote(ins[a].at[2 * px + py], outs[a].at[j], send.at[a * 3 + j], recv.at[a * 3 + j], (px, py, c)).wait()

    return pl.pallas_call(
        body, name=name, in_specs=[ANY] * na, out_specs=[ANY] * na,
        out_shape=[jax.ShapeDtypeStruct((N_CHIPS - 1,) + a.shape[1:], F32) for a in arrs],
        scratch_shapes=[pltpu.SemaphoreType.DMA((3 * na,)), pltpu.SemaphoreType.DMA((3 * na,))],
    )(*arrs)


def _finish_reduce(big, small, *, name):
    nb = len(big)

    def body(*refs):
        outs, sm = refs[nb + 1:2 * nb + 1], refs[2 * nb + 1]
        send, recv = refs[2 * nb + 2:]
        x, y, c, sibling, chips = _place()
        blk = lambda px, py, pc: sm.at[4 * px + 2 * py + pc]
        for a in range(nb):
            _, hr, cols = big[a].shape
            for r0, nr in _row_chunks(hr, cols * 4):
                _remote(outs[a].at[c, pl.ds(r0, nr), :], outs[a].at[c, pl.ds(r0, nr), :], send.at[a], recv.at[a],
                        sibling).start()
        first = [_remote(blk(x, y, c), blk(x, y, c), send.at[nb], recv.at[nb], sibling)]
        first += [_remote(blk(x, y, c), blk(x, y, c), send.at[nb + 1 + j], recv.at[nb + 1 + j], (*chip, c))
                  for j, chip in enumerate(chips)]
        for cp in first:
            cp.start()
        passed = []
        for j, (px, py) in enumerate(chips):
            _remote(blk(px, py, c), blk(px, py, c), send.at[nb + 1 + j], recv.at[nb + 1 + j], sibling).wait_recv()
            fw = _remote(blk(px, py, c), blk(px, py, c), send.at[nb + 4 + j], recv.at[nb + 4 + j], sibling)
            fw.start()
            passed.append(fw)
        for a in range(nb):
            _remote(outs[a].at[c], outs[a].at[1 - c], send.at[a], recv.at[a], sibling).wait()
        _remote(blk(x, y, c), blk(x, y, 1 - c), send.at[nb], recv.at[nb], sibling).wait_recv()
        for j, (px, py) in enumerate(chips):
            _remote(blk(px, py, c), blk(px, py, 1 - c), send.at[nb + 4 + j], recv.at[nb + 4 + j], sibling).wait_recv()
        for cp in first + passed:
            cp.wait_send()

    n_remote = nb + 7
    arrs = list(big) + [small]
    outs = pl.pallas_call(
        body, name=name, in_specs=[ANY] * (nb + 1), out_specs=[ANY] * (nb + 1),
        out_shape=[jax.ShapeDtypeStruct(a.shape, F32) for a in arrs],
        input_output_aliases={i: i for i in range(nb + 1)},
        scratch_shapes=[pltpu.SemaphoreType.DMA((n_remote,)), pltpu.SemaphoreType.DMA((n_remote,))],
    )(*arrs)
    return outs[:nb], outs[nb]


def _reduce_gradients(sectioned, small, where):
    arrs = list(sectioned) + [small]
    got = _exchange_halves(arrs, name="rs_exchange_halves")
    chip_sum = [_add_sibling(a, g, where, name=f"rs_add_sibling_{i}") for i, (a, g) in enumerate(zip(arrs, got))]
    slots = _scatter_sections(chip_sum, name="rs_scatter_sections")
    red = [_sum_chips(p, s, where, 2, 0, name=f"rs_sum_chips_{i}")
           for i, (p, s) in enumerate(zip(chip_sum[:-1], slots[:-1]))]
    red_small = _sum_chips(chip_sum[-1], slots[-1], where, 2 * N_CHIPS, 2, name="rs_sum_chips_small")
    big, sm = _finish_reduce(red, red_small, name="rs_finish")
    return [b.reshape(-1, b.shape[-1]) for b in big], sm.reshape(-1, sm.shape[-1])


def kernel(x, c, ctx, c_ctx, w_mod, b_mod, g_pre, g_post, w_in, w_conv, a_log, dt_bias, g_onorm, gm_ln_g, gm_ln_b, w_sp, b_sp, w_pa, w_pb, w_out, loss_target, m_c_ctx, m_w_mod, m_b_mod, m_g_pre, m_g_post, m_w_in, m_w_conv, m_a_log, m_dt_bias, m_g_onorm, m_gm_ln_g, m_gm_ln_b, m_w_sp, m_b_sp, m_w_pa, m_w_pb, m_w_out, v_c_ctx, v_w_mod, v_b_mod, v_g_pre, v_g_post, v_w_in, v_w_conv, v_a_log, v_dt_bias, v_g_onorm, v_gm_ln_g, v_gm_ln_b, v_w_sp, v_b_sp, v_w_pa, v_w_pb, v_w_out):
    names = ["c_ctx", "w_mod", "b_mod", "g_pre", "g_post", "w_in", "w_conv", "a_log", "dt_bias", "g_onorm", "gm_ln_g",
             "gm_ln_b", "w_sp", "b_sp", "w_pa", "w_pb", "w_out"]
    w = dict(zip(names, (c_ctx, w_mod, b_mod, g_pre, g_post, w_in, w_conv, a_log, dt_bias, g_onorm, gm_ln_g, gm_ln_b,
                         w_sp, b_sp, w_pa, w_pb, w_out)))
    m = dict(zip(names, (m_c_ctx, m_w_mod, m_b_mod, m_g_pre, m_g_post, m_w_in, m_w_conv, m_a_log, m_dt_bias, m_g_onorm,
                         m_gm_ln_g, m_gm_ln_b, m_w_sp, m_b_sp, m_w_pa, m_w_pb, m_w_out)))
    v = dict(zip(names, (v_c_ctx, v_w_mod, v_b_mod, v_g_pre, v_g_post, v_w_in, v_w_conv, v_a_log, v_dt_bias, v_g_onorm,
                         v_gm_ln_g, v_gm_ln_b, v_w_sp, v_b_sp, v_w_pa, v_w_pb, v_w_out)))
    xy = 2 * lax.axis_index("x") + lax.axis_index("y")
    where = jnp.stack([lax.axis_index("c"), xy, 2 * xy + lax.axis_index("c")]).astype(jnp.int32)

    rowpack = jnp.concatenate([w_pa[0], w_pb[0], w_out[0]], axis=0).astype(BF16)
    shards = [w_mod[0].astype(BF16), w_in[0].astype(BF16), rowpack]
    gathered, wconv_all = _gather_shards(shards, w_conv[0], name="gather_weights")
    own = lambda full, shard: lax.dynamic_update_slice(full, shard[None], (xy, 0, 0))
    wm_all, win_all, row_all = [own(f, s) for f, s in zip(gathered, shards)]
    wconv_all = own(wconv_all, w_conv[0])
    cols = lambda a: jnp.concatenate([a[s] for s in range(N_CHIPS)], axis=1)
    w_mod_f, w_in_f, w_conv_f = cols(wm_all), cols(win_all), cols(wconv_all)
    blk = D // N_CHIPS
    rows_of = lambda i: jnp.concatenate([row_all[s, i * blk:(i + 1) * blk] for s in range(N_CHIPS)], axis=0)
    w_ab = jnp.pad(w_in_f[:, OFF_A:OFF_ZB], ((0, 0), (0, DH - 4 * H)))

    loss_local, grad_x, g = _local_step(
        x[0], c, ctx[0], c_ctx, loss_target[0], w_mod_f, b_mod, g_pre, g_post, w_in_f[:, :OFF_A], w_ab, w_in_f[:, OFF_ZB:],
        w_conv_f, a_log[0], dt_bias[0], g_onorm, gm_ln_g, gm_ln_b, w_sp[0], b_sp[0], rows_of(0), rows_of(1), rows_of(2))
    loss = lax.psum(loss_local, ("x", "y", "c"))

    sect_cols = lambda a: jnp.stack(jnp.split(a, N_CHIPS, axis=1))
    g_rows = jnp.stack([jnp.concatenate([g[k][s * blk:(s + 1) * blk] for k in ("w_pa", "w_pb", "w_out")], axis=0)
                        for s in range(N_CHIPS)])
    flat = jnp.zeros((N_CHIPS * SMALL_ROWS * 128,), F32)
    for k, (off, size) in SMALL_LAYOUT.items():
        if k != "_pad":
            flat = lax.dynamic_update_slice(flat, g[k].reshape(-1), (off,))
    (gr_wm, gr_win, gr_rows), gr_small = _reduce_gradients(
        [sect_cols(g["w_mod"]), sect_cols(g["w_in"]), g_rows], flat.reshape(N_CHIPS, SMALL_ROWS, 128), where)
    gr_flat = gr_small.reshape(-1)

    res = {}
    res["w_mod"] = _adamw(w_mod[0], gr_wm, m_w_mod[0], v_w_mod[0], name="adamw_w_mod")
    res["w_in"] = _adamw(w_in[0], gr_win, m_w_in[0], v_w_in[0], name="adamw_w_in")
    for i, k in enumerate(("w_pa", "w_pb", "w_out")):
        res[k] = _adamw(w[k][0], gr_rows, m[k][0], v[k][0], g_row_block=i, name=f"adamw_{k}")
    small_names = [k for k in SMALL_LAYOUT if k not in ("_pad", "w_conv")]

    def pack(src):
        buf = jnp.zeros((N_CHIPS * SMALL_ROWS * 128,), F32)
        for k in small_names:
            buf = lax.dynamic_update_slice(buf, src[k].reshape(-1), (SMALL_LAYOUT[k][0],))
        return buf.reshape(-1, 128)

    sm_res = _adamw(pack(w), gr_small, pack(m), pack(v), name="adamw_small")
    for k in small_names:
        off, size = SMALL_LAYOUT[k]
        res[k] = [r.reshape(-1)[off:off + size].reshape(w[k].shape) for r in sm_res]
    off, size = SMALL_LAYOUT["w_conv"]
    g_conv = lax.dynamic_slice(gr_flat[off:off + size].reshape(3, 3 * D), (0, xy * (3 * D // N_CHIPS)), (3, 3 * D // N_CHIPS))
    conv_res = _adamw(jnp.pad(w_conv[0], ((0, 5), (0, 0))), jnp.pad(g_conv, ((0, 5), (0, 0))),
                      jnp.pad(m_w_conv[0], ((0, 5), (0, 0))), jnp.pad(v_w_conv[0], ((0, 5), (0, 0))), name="adamw_w_conv")
    res["w_conv"] = [r[:3] for r in conv_res]
    for k in ("w_mod", "w_in", "w_pa", "w_pb", "w_out", "w_conv"):
        res[k] = [r.reshape(w[k].shape) for r in res[k]]

    out = [loss, grad_x[None]]
    for i in range(4):
        out += [res[k][i] for k in names]
    return tuple(out)
```

```python
import functools

import jax
import jax.numpy as jnp
from jax import lax
from jax.experimental import pallas as pl
from jax.experimental.pallas import tpu as pltpu

F32 = jnp.float32
BF16 = jnp.bfloat16
HI = lax.Precision.HIGHEST
MESH = pl.DeviceIdType.MESH

D = 1024
H = 8
DH = 128
CH = 64
LOG_CH = 6
PAIR = 2 * CH
GM = 128
assert 1 << LOG_CH == CH and PAIR == DH
PREC_SOLVE = lax.Precision.HIGH
PREC_POWERS = (PREC_SOLVE,) * (LOG_CH - 1)
HEADS_PER_ITER_FWD = 8
HEADS_PER_ITER_BWD = 8
EPS = 1e-6
N_CHIPS = 4
OFF_A = 3 * D
OFF_ZB = OFF_A + 4 * H
IN_COLS = OFF_ZB + 6 * D
VMEM_LIMIT_V7X = 56 * 1024 * 1024
DMA_CHUNK_BYTES = 2 * 1024 * 1024

ADAM_LR, ADAM_B1, ADAM_B2, ADAM_EPS, ADAM_WD, ADAM_STEP = 0.001, 0.9, 0.999, 1e-08, 0.01, 10

SMALL_LAYOUT = {}
_off = 0
for _n, _s in (("c_ctx", D), ("b_mod", 3 * D), ("g_pre", D), ("g_post", D), ("gm_ln_g", D), ("gm_ln_b", D),
               ("b_sp", H * GM), ("g_onorm", DH), ("a_log", 2 * H), ("dt_bias", 2 * H), ("_pad", 96),
               ("w_conv", 3 * 3 * D), ("w_sp", H * GM * GM)):
    SMALL_LAYOUT[_n] = (_off, _s)
    _off += _s
SMALL_ROWS = 304
assert N_CHIPS * SMALL_ROWS * 128 >= _off and (SMALL_ROWS // 2) % 8 == 0


def _params(sem=None):
    return pltpu.CompilerParams(dimension_semantics=sem, vmem_limit_bytes=VMEM_LIMIT_V7X)


def _tile(n, cands=(256, 128, 64, 32, 16, 8)):
    for cand in cands:
        if n % cand == 0:
            return cand
    return n


def _silu(x):
    return x * jax.nn.sigmoid(x)


def _gelu(x):
    return 0.5 * x * (1.0 + jnp.tanh(0.7978845608028654 * (x + 0.044715 * (x * x * x))))


def _mm(a, b, *, ta=False, tb=False, out_dtype=F32, tm=512, tn=1024, tk=1024, pre_silu=False, add=None, add_from=0, name):
    m, k = (a.shape[1], a.shape[0]) if ta else a.shape
    n = b.shape[0] if tb else b.shape[1]
    tm, tn, tk = min(tm, m), min(tn, n), min(tk, k)
    assert m % tm == 0 and n % tn == 0 and k % tk == 0, (name, m, n, k, tm, tn, tk)
    nk = k // tk
    dims = (((0,) if ta else (1,), (1,) if tb else (0,)), ((), ()))
    has_add = add is not None
    assert not has_add or add.shape == (m, n - add_from * tn), (name, add.shape)

    def body(*refs):
        a_ref, b_ref = refs[:2]
        o_ref, acc_ref = refs[-2:]
        kk = pl.program_id(2)
        av = a_ref[...]
        if pre_silu:
            av = _silu(av.astype(F32))
        part = lax.dot_general(av.astype(BF16), b_ref[...].astype(BF16), dims, preferred_element_type=F32)

        @pl.when(kk == 0)
        def _():
            acc_ref[...] = part

        @pl.when(kk > 0)
        def _():
            acc_ref[...] += part

        @pl.when(kk == nk - 1)
        def _():
            res = acc_ref[...]
            if has_add:
                res = res + jnp.where(pl.program_id(1) >= add_from, refs[2][...], 0.0)
            o_ref[...] = res.astype(out_dtype)

    a_spec = pl.BlockSpec((tk, tm), lambda i, j, q: (q, i)) if ta else pl.BlockSpec((tm, tk), lambda i, j, q: (i, q))
    b_spec = pl.BlockSpec((tn, tk), lambda i, j, q: (j, q)) if tb else pl.BlockSpec((tk, tn), lambda i, j, q: (q, j))
    add_spec = [pl.BlockSpec((tm, tn), lambda i, j, q: (i, jnp.maximum(j - add_from, 0)))] if has_add else []
    return pl.pallas_call(
        body, name=name, grid=(m // tm, n // tn, nk),
        in_specs=[a_spec, b_spec] + add_spec, out_specs=pl.BlockSpec((tm, tn), lambda i, j, q: (i, j)),
        out_shape=jax.ShapeDtypeStruct((m, n), out_dtype),
        scratch_shapes=[pltpu.VMEM((tm, tn), F32)],
        compiler_params=_params(("parallel", "parallel", "arbitrary")),
    )(*([a, b] + ([add] if has_add else [])))


def _h_fn(x, g, m):
    shift, scale = m[:, 0:D], m[:, D:2 * D]
    r = lax.rsqrt(jnp.mean(x * x, axis=-1, keepdims=True) + EPS)
    return (x * r * g) * (1.0 + scale) + shift


def _norm_fwd(x, g, mod, bmod, *, name):
    rows = x.shape[0]
    t = min(512, rows)

    def body(x_ref, g_ref, m_ref, b_ref, h_ref):
        h_ref[...] = _h_fn(x_ref[...], g_ref[...], m_ref[...] + b_ref[...]).astype(BF16)

    vec = lambda w: pl.BlockSpec((1, w), lambda i: (0, 0))
    return pl.pallas_call(
        body, name=name, grid=(rows // t,),
        in_specs=[pl.BlockSpec((t, D), lambda i: (i, 0)), vec(D), vec(3 * D), vec(3 * D)],
        out_specs=pl.BlockSpec((t, D), lambda i: (i, 0)),
        out_shape=jax.ShapeDtypeStruct((rows, D), BF16),
        compiler_params=_params(("parallel",)),
    )(x, g, mod, bmod)


def _norm_bwd(x, g, mod, bmod, dh_parts, resid, *, name):
    rows = x.shape[0]
    t = min(512, rows)
    n_parts = len(dh_parts)
    has_resid = resid is not None

    def body(*refs):
        x_ref, g_ref, m_ref, b_ref = refs[:4]
        parts = refs[4:4 + n_parts]
        r_ref = refs[4 + n_parts] if has_resid else None
        dx_ref, dg_ref, dm_ref = refs[-3:]
        i = pl.program_id(0)
        dh = parts[0][...]
        for p in parts[1:]:
            dh = dh + p[...]
        _, vjp = jax.vjp(_h_fn, x_ref[...], g_ref[...], m_ref[...] + b_ref[...])
        dx, dg, dm = vjp(dh)
        if has_resid:
            dx = dx + r_ref[...]
        dx_ref[...] = dx

        @pl.when(i == 0)
        def _():
            dg_ref[...] = dg
            dm_ref[...] = dm

        @pl.when(i > 0)
        def _():
            dg_ref[...] += dg
            dm_ref[...] += dm

    vec = lambda w: pl.BlockSpec((1, w), lambda i: (0, 0))
    tile = pl.BlockSpec((t, D), lambda i: (i, 0))
    ins = [x, g, mod, bmod, *dh_parts] + ([resid] if has_resid else [])
    return pl.pallas_call(
        body, name=name, grid=(rows // t,),
        in_specs=[tile, vec(D), vec(3 * D), vec(3 * D)] + [tile] * (n_parts + int(has_resid)),
        out_specs=[tile, vec(D), vec(3 * D)],
        out_shape=[jax.ShapeDtypeStruct((rows, D), F32), jax.ShapeDtypeStruct((1, D), F32),
                   jax.ShapeDtypeStruct((1, 3 * D), F32)],
        compiler_params=_params(("arbitrary",)),
    )(*ins)


def _conv_tile(u_ref, r0, t, rows, w0, w1, w2):
    u = u_ref[pl.ds(r0, t), :]
    prev8 = u_ref[pl.ds(pl.multiple_of(jnp.maximum(r0 - 8, 0), 8), 8), :]
    next8 = u_ref[pl.ds(pl.multiple_of(jnp.minimum(r0 + t, rows - 8), 8), 8), :]
    r8 = lax.broadcasted_iota(jnp.int32, (8, DH), 0)
    prev_row = jnp.sum(jnp.where(r8 == 7, prev8, 0.0), axis=0, keepdims=True)
    next_row = jnp.sum(jnp.where(r8 == 0, next8, 0.0), axis=0, keepdims=True)
    prev_row = jnp.where(r0 > 0, prev_row, 0.0)
    next_row = jnp.where(r0 + t < rows, next_row, 0.0)
    ri = lax.broadcasted_iota(jnp.int32, (t, DH), 0)
    um1 = jnp.where(ri == 0, prev_row, pltpu.roll(u, 1, 0))
    up1 = jnp.where(ri == t - 1, next_row, pltpu.roll(u, t - 1, 0))
    return w0 * um1 + w1 * u + w2 * up1, um1, u, up1


def _rowlocal(z, is_norm):
    y = _silu(z)
    yn = y * lax.rsqrt(jnp.sum(y * y, axis=-1, keepdims=True) + EPS)
    return jnp.where(is_norm, yn, y)


def _prep_fwd(p, wconv, n_norm, *, name):
    rows, nb = p.shape[0], p.shape[1] // DH
    t = min(512, rows)

    def body(u_ref, w_ref, o_ref):
        is_norm = pl.program_id(0) < n_norm
        w0, w1, w2 = w_ref[0:1, :], w_ref[1:2, :], w_ref[2:3, :]

        def step(s, carry):
            r0 = pl.multiple_of(s * t, t)
            z, _, _, _ = _conv_tile(u_ref, r0, t, rows, w0, w1, w2)
            o_ref[0, pl.ds(r0, t), :] = _rowlocal(z, is_norm)
            return carry

        lax.fori_loop(0, rows // t, step, 0)

    return pl.pallas_call(
        body, name=name, grid=(nb,),
        in_specs=[pl.BlockSpec((rows, DH), lambda j: (0, j)), pl.BlockSpec((3, DH), lambda j: (0, j))],
        out_specs=pl.BlockSpec((1, rows, DH), lambda j: (j, 0, 0)),
        out_shape=jax.ShapeDtypeStruct((nb, rows, DH), F32),
        compiler_params=_params(("parallel",)),
    )(p, wconv)


def _prep_bwd(p, wconv, d_a, d_b, n_norm, *, name):
    rows, nb = p.shape[0], p.shape[1] // DH
    t = min(512, rows)

    def body(u_ref, w_ref, da_ref, db_ref, du_ref, dw_ref, dz_ref):
        is_norm = pl.program_id(0) < n_norm
        w0, w1, w2 = w_ref[0:1, :], w_ref[1:2, :], w_ref[2:3, :]

        def step1(s, carry):
            a0, a1, a2 = carry
            r0 = pl.multiple_of(s * t, t)
            z, um1, u, up1 = _conv_tile(u_ref, r0, t, rows, w0, w1, w2)
            _, vjp = jax.vjp(lambda zz: _rowlocal(zz, is_norm), z)
            (dz,) = vjp(da_ref[0, pl.ds(r0, t), :] + db_ref[0, pl.ds(r0, t), :])
            dz_ref[pl.ds(r0, t), :] = dz
            red = lambda v: jnp.sum(v, axis=0, keepdims=True)
            return a0 + red(dz * um1), a1 + red(dz * u), a2 + red(dz * up1)

        zero = jnp.zeros((1, DH), F32)
        a0, a1, a2 = lax.fori_loop(0, rows // t, step1, (zero, zero, zero))
        dw_ref[0:1, :] = a0
        dw_ref[1:2, :] = a1
        dw_ref[2:3, :] = a2

        def step2(s, carry):
            r0 = pl.multiple_of(s * t, t)
            du, _, _, _ = _conv_tile(dz_ref, r0, t, rows, w2, w1, w0)
            du_ref[pl.ds(r0, t), :] = du.astype(BF16)
            return carry

        lax.fori_loop(0, rows // t, step2, 0)

    col = pl.BlockSpec((rows, DH), lambda j: (0, j))
    w_spec = pl.BlockSpec((3, DH), lambda j: (0, j))
    d_spec = pl.BlockSpec((1, rows, DH), lambda j: (j, 0, 0))
    return pl.pallas_call(
        body, name=name, grid=(nb,),
        in_specs=[col, w_spec, d_spec, d_spec], out_specs=[col, w_spec],
        out_shape=[jax.ShapeDtypeStruct((rows, nb * DH), BF16), jax.ShapeDtypeStruct((3, nb * DH), F32)],
        scratch_shapes=[pltpu.VMEM((rows, DH), F32)],
        compiler_params=_params(("parallel",)),
    )(p, wconv, d_a, d_b)


def _gates_fn(pab, avec, dvec):
    t = pab.shape[0]
    lane = lax.broadcasted_iota(jnp.int32, pab.shape, 1)
    xg = pab + dvec
    sp = jnp.maximum(xg, 0.0) + jnp.log(1.0 + jnp.exp(-jnp.abs(xg)))
    g = jnp.where(lane < 2 * H, -jnp.exp(avec) * sp, 0.0)
    ii = lax.broadcasted_iota(jnp.int32, (t, t), 0)
    jj = lax.broadcasted_iota(jnp.int32, (t, t), 1)
    same = (ii >> LOG_CH) == (jj >> LOG_CH)
    cum_f = _dot(jnp.where(same & (jj <= ii), 1.0, 0.0), g, precision=HI)
    cum_r = _dot(jnp.where(same & (jj >= ii), 1.0, 0.0), g, precision=HI)
    return jnp.where(lane < H, cum_f, jnp.where(lane < 2 * H, cum_r, jnp.where(lane < 4 * H, jax.nn.sigmoid(pab), 0.0)))


def _gates_fwd(pab, avec, dvec, *, name):
    rows = pab.shape[0]
    t = min(512, rows)

    def body(p_ref, a_ref, d_ref, o_ref):
        o_ref[...] = _gates_fn(p_ref[...], a_ref[...], d_ref[...])

    vec = pl.BlockSpec((1, DH), lambda i: (0, 0))
    tile = pl.BlockSpec((t, DH), lambda i: (i, 0))
    return pl.pallas_call(
        body, name=name, grid=(rows // t,), in_specs=[tile, vec, vec], out_specs=tile,
        out_shape=jax.ShapeDtypeStruct((rows, DH), F32), compiler_params=_params(("parallel",)),
    )(pab, avec, dvec)


def _gates_bwd(pab, avec, dvec, d_a, d_b, *, name):
    rows = pab.shape[0]
    t = min(512, rows)

    def body(p_ref, a_ref, d_ref, da_ref, db_ref, dp_ref, dav_ref, ddv_ref):
        i = pl.program_id(0)
        _, vjp = jax.vjp(_gates_fn, p_ref[...], a_ref[...], d_ref[...])
        dp, dav, ddv = vjp(da_ref[...] + db_ref[...])
        dp_ref[...] = dp.astype(BF16)

        @pl.when(i == 0)
        def _():
            dav_ref[...] = dav
            ddv_ref[...] = ddv

        @pl.when(i > 0)
        def _():
            dav_ref[...] += dav
            ddv_ref[...] += ddv

    vec = pl.BlockSpec((1, DH), lambda i: (0, 0))
    tile = pl.BlockSpec((t, DH), lambda i: (i, 0))
    return pl.pallas_call(
        body, name=name, grid=(rows // t,), in_specs=[tile, vec, vec, tile, tile], out_specs=[tile, vec, vec],
        out_shape=[jax.ShapeDtypeStruct((rows, DH), BF16), jax.ShapeDtypeStruct((1, DH), F32),
                   jax.ShapeDtypeStruct((1, DH), F32)],
        compiler_params=_params(("arbitrary",)),
    )(pab, avec, dvec, d_a, d_b)


def _dot(a, b, dims=((1,), (0,)), precision=None):
    return lax.dot_general(a, b, (dims, ((), ())), precision=precision, preferred_element_type=F32)


_NT = ((1,), (1,))
_TN = ((0,), (0,))


@jax.custom_vjp
def _saved_inverse(neg_a, inv):
    return inv


def _saved_inverse_fwd(neg_a, inv):
    return inv, inv


def _saved_inverse_bwd(inv, d_inv):
    idx = range(len(inv))
    left = [_dot(inv[i], d_inv[i], _TN, precision=PREC_SOLVE) for i in idx]
    d_neg_a = [_dot(left[i], inv[i], _NT, precision=PREC_SOLVE) for i in idx]
    return d_neg_a, [jnp.zeros_like(t) for t in inv]


_saved_inverse.defvjp(_saved_inverse_fwd, _saved_inverse_bwd)


def _pairs(s, q, k, v, gcol, bcol, revs, inv_saved=None):
    idx = range(len(revs))
    ii = lax.broadcasted_iota(jnp.int32, (PAIR, PAIR), 0)
    jj = lax.broadcasted_iota(jnp.int32, (PAIR, PAIR), 1)
    same = (ii >> LOG_CH) == (jj >> LOG_CH)
    incl_d = (same & (ii >= jj), same & (ii <= jj))
    strict_d = (same & (ii > jj), same & (ii < jj))
    incl = [incl_d[int(r)] for r in revs]
    strict = [strict_d[int(r)] for r in revs]
    eye = jnp.where(ii == jj, 1.0, 0.0)
    gc_i = [jnp.broadcast_to(gcol[i], (PAIR, DH)) for i in idx]
    gc_j = [gc_i[i].T for i in idx]
    decay = [jnp.where(incl[i], jnp.exp(jnp.where(incl[i], gc_i[i] - gc_j[i], 0.0)), 0.0) for i in idx]
    b_b = [jnp.broadcast_to(bcol[i], (PAIR, DH)) for i in idx]
    kb = [k[i] * b_b[i] for i in idx]
    kk = [_dot(kb[i], k[i], _NT) for i in idx]
    bp = [jnp.where(strict[i], -kk[i] * decay[i], 0.0) for i in idx]
    if inv_saved is not None:
        inv = _saved_inverse(bp, inv_saved)
    else:
        inv = [eye + bp[i] for i in idx]
        for prec in PREC_POWERS:
            bp = [_dot(bp[i], bp[i], precision=prec) for i in idx]
            more = [_dot(inv[i], bp[i], precision=prec) for i in idx]
            inv = [inv[i] + more[i] for i in idx]
    eg = [jnp.exp(gc_i[i]) for i in idx]
    sol = [_dot(inv[i], jnp.concatenate([v[i] * b_b[i], kb[i] * eg[i]], axis=1), precision=PREC_SOLVE) for i in idx]
    u_val = [sol[i][:, :DH] for i in idx]
    w_key = [sol[i][:, DH:] for i in idx]
    row = lax.broadcasted_iota(jnp.int32, (PAIR, 1), 0)
    has_q = q[0] is not None
    if has_q:
        qc = [q[i] * (DH ** -0.5) for i in idx]
        qk = [_dot(qc[i], k[i], _NT) for i in idx]
        attn = [qk[i] * decay[i] for i in idx]
        qd = [qc[i] * eg[i] for i in idx]
    outs = [[None, None] for _ in idx]
    zeros = jnp.zeros((CH, DH), F32)
    for step in range(2):
        cidx = [(1 - step) if revs[i] else step for i in idx]
        sl = [slice(c * CH, (c + 1) * CH) for c in cidx]
        last = [c * CH if revs[i] else c * CH + CH - 1 for i, c in zip(idx, cidx)]
        gl = [jnp.sum(jnp.where(row == last[i], gcol[i], 0.0), axis=0, keepdims=True) for i in idx]
        k_tail = [k[i][sl[i]] * jnp.exp(gl[i] - gc_i[i][sl[i]]) for i in idx]
        ws = [_dot(w_key[i][sl[i]], s[i]) for i in idx]
        v_new = [u_val[i][sl[i]] - ws[i] for i in idx]
        if has_q:
            v_pad = [jnp.concatenate([v_new[i], zeros] if cidx[i] == 0 else [zeros, v_new[i]], axis=0) for i in idx]
            o_state = [_dot(qd[i][sl[i]], s[i]) for i in idx]
            o_local = [_dot(attn[i][sl[i]], v_pad[i]) for i in idx]
            for i in idx:
                outs[i][cidx[i]] = o_state[i] + o_local[i]
        kv = [_dot(k_tail[i], v_new[i], _TN) for i in idx]
        s = [s[i] * jnp.exp(gl[i]) + kv[i] for i in idx]
    return s, ([jnp.concatenate(outs[i], axis=0) for i in idx] if has_q else None), inv


def _lane_col(tile, idx):
    lane = lax.broadcasted_iota(jnp.int32, tile.shape, 1)
    return jnp.sum(jnp.where(lane == idx, tile, 0.0), axis=1, keepdims=True)


def _gdn_fwd(qkv, gb, s0f, s0b, has_q, *, name):
    nb, rows, _ = qkv.shape
    n = rows // PAIR
    qoff = H if has_q else 0

    def body(qf_ref, qb_ref, gf_ref, gr_ref, s0f_ref, s0b_ref, of_ref, ob_ref, ssf_ref, ssb_ref, tsf_ref, tsb_ref,
             sf_ref, sb_ref):
        @pl.when(pl.program_id(0) == 0)
        def _():
            sf_ref[...] = s0f_ref[...]
            sb_ref[...] = s0b_ref[...]

        gtiles = (gf_ref[...], gr_ref[...])

        dirs = ((qf_ref, sf_ref, ssf_ref, of_ref), (qb_ref, sb_ref, ssb_ref, ob_ref))
        ts_refs = (tsf_ref, tsb_ref)

        def heads(hg, carry):
            work = [(hg * HEADS_PER_ITER_FWD + j, d) for j in range(HEADS_PER_ITER_FWD) for d in range(2)]
            loaded = []
            for h, d in work:
                q_ref, s_ref, _, _ = dirs[d]
                loaded.append((s_ref[h], q_ref[h] if has_q else None, q_ref[qoff + h], q_ref[qoff + H + h],
                               _lane_col(gtiles[d], d * H + h), _lane_col(gtiles[d], 2 * H + d * H + h)))
            s_new, o, inv = _pairs(*[list(col) for col in zip(*loaded)], revs=[d == 1 for _, d in work])
            for i, (h, d) in enumerate(work):
                _, s_ref, ss_ref, o_ref = dirs[d]
                ss_ref[0, h] = loaded[i][0]
                ts_refs[d][0, h] = inv[i]
                s_ref[h] = s_new[i]
                o_ref[h] = o[i] if has_q else jnp.zeros((PAIR, DH), F32)
            return carry

        if HEADS_PER_ITER_FWD == H:
            heads(0, 0)
        else:
            lax.fori_loop(0, H // HEADS_PER_ITER_FWD, heads, 0)

    fwd3 = lambda i: (0, i, 0)
    rev3 = lambda i: (0, n - 1 - i, 0)
    state = pl.BlockSpec((H, DH, DH), lambda i: (0, 0, 0))
    saved = pl.BlockSpec((1, H, DH, DH), lambda i: (i, 0, 0, 0))
    return pl.pallas_call(
        body, name=name, grid=(n,),
        in_specs=[pl.BlockSpec((nb, PAIR, DH), fwd3), pl.BlockSpec((nb, PAIR, DH), rev3),
                  pl.BlockSpec((PAIR, DH), lambda i: (i, 0)), pl.BlockSpec((PAIR, DH), lambda i: (n - 1 - i, 0)),
                  state, state],
        out_specs=[pl.BlockSpec((H, PAIR, DH), fwd3), pl.BlockSpec((H, PAIR, DH), rev3), saved, saved, saved, saved,
                   state, state],
        out_shape=[jax.ShapeDtypeStruct((H, rows, DH), F32)] * 2 + [jax.ShapeDtypeStruct((n, H, DH, DH), F32)] * 4
        + [jax.ShapeDtypeStruct((H, DH, DH), F32)] * 2,
        compiler_params=_params(("arbitrary",)),
    )(qkv, qkv, gb, gb, s0f, s0b)


def _gdn_bwd(qkv, gb, ssf, ssb, tsf, tsb, do, dsf, dsb, has_q, *, name):
    nb, rows, _ = qkv.shape
    n = rows // PAIR
    qoff = H if has_q else 0

    def body(qf_ref, qb_ref, gf_ref, gr_ref, ssf_ref, ssb_ref, tsf_ref, tsb_ref, dof_ref, dob_ref, dsf0_ref, dsb0_ref,
             dqf_ref, dqb_ref, dgf_ref, dgr_ref, dsf_ref, dsb_ref):
        ts_refs = (tsf_ref, tsb_ref)
        @pl.when(pl.program_id(0) == 0)
        def _():
            dsf_ref[...] = dsf0_ref[...]
            dsb_ref[...] = dsb0_ref[...]

        gtiles = (gf_ref[...], gr_ref[...])
        lane = lax.broadcasted_iota(jnp.int32, (PAIR, DH), 1)

        dirs = ((qf_ref, ssf_ref, dof_ref, dsf_ref, dqf_ref), (qb_ref, ssb_ref, dob_ref, dsb_ref, dqb_ref))

        def heads(hg, carry):
            out = list(carry)
            work = [(hg * HEADS_PER_ITER_BWD + j, d) for j in range(HEADS_PER_ITER_BWD) for d in range(2)]
            revs = [d == 1 for _, d in work]
            s_in, q_in, k_in, v_in, g_in, b_in, ds_out, do_out, inv_in = [], [], [], [], [], [], [], [], []
            for h, d in work:
                q_ref, ss_ref, do_ref, ds_ref, _ = dirs[d]
                s_in.append(ss_ref[0, h])
                inv_in.append(ts_refs[d][0, h])
                q_in.append(q_ref[h] if has_q else None)
                k_in.append(q_ref[qoff + h])
                v_in.append(q_ref[qoff + H + h])
                g_in.append(_lane_col(gtiles[d], d * H + h))
                b_in.append(_lane_col(gtiles[d], 2 * H + d * H + h))
                ds_out.append(ds_ref[h])
                do_out.append(do_ref[h] if has_q else None)
            if has_q:
                _, vjp = jax.vjp(lambda s_, q_, k_, v_, g_, b_: _pairs(s_, q_, k_, v_, g_, b_, revs, inv_in)[:2],
                                 s_in, q_in, k_in, v_in, g_in, b_in)
                ds, dq, dk, dv, dg, db = vjp((ds_out, do_out))
            else:
                _, vjp = jax.vjp(lambda s_, k_, v_, g_, b_: _pairs(s_, q_in, k_, v_, g_, b_, revs, inv_in)[0],
                                 s_in, k_in, v_in, g_in, b_in)
                ds, dk, dv, dg, db = vjp(ds_out)
            for i, (h, d) in enumerate(work):
                dq_ref, ds_ref = dirs[d][4], dirs[d][3]
                ds_ref[h] = ds[i]
                if has_q:
                    dq_ref[h] = dq[i]
                dq_ref[qoff + h], dq_ref[qoff + H + h] = dk[i], dv[i]
                out[d] = (out[d] + jnp.where(lane == d * H + h, dg[i], 0.0)
                          + jnp.where(lane == 2 * H + d * H + h, db[i], 0.0))
            return tuple(out)

        zero = jnp.zeros((PAIR, DH), F32)
        if HEADS_PER_ITER_BWD == H:
            dgf, dgr = heads(0, (zero, zero))
        else:
            dgf, dgr = lax.fori_loop(0, H // HEADS_PER_ITER_BWD, heads, (zero, zero))
        dgf_ref[...] = dgf
        dgr_ref[...] = dgr

    fwd3 = lambda i: (0, n - 1 - i, 0)
    rev3 = lambda i: (0, i, 0)
    state = pl.BlockSpec((H, DH, DH), lambda i: (0, 0, 0))
    saved = pl.BlockSpec((1, H, DH, DH), lambda i: (n - 1 - i, 0, 0, 0))
    gf_spec = pl.BlockSpec((PAIR, DH), lambda i: (n - 1 - i, 0))
    gr_spec = pl.BlockSpec((PAIR, DH), lambda i: (i, 0))
    return pl.pallas_call(
        body, name=name, grid=(n,),
        in_specs=[pl.BlockSpec((nb, PAIR, DH), fwd3), pl.BlockSpec((nb, PAIR, DH), rev3), gf_spec, gr_spec,
                  saved, saved, saved, saved,
                  pl.BlockSpec((H, PAIR, DH), fwd3), pl.BlockSpec((H, PAIR, DH), rev3), state, state],
        out_specs=[pl.BlockSpec((nb, PAIR, DH), fwd3), pl.BlockSpec((nb, PAIR, DH), rev3), gf_spec, gr_spec, state, state],
        out_shape=[jax.ShapeDtypeStruct((nb, rows, DH), F32)] * 2 + [jax.ShapeDtypeStruct((rows, DH), F32)] * 2
        + [jax.ShapeDtypeStruct((H, DH, DH), F32)] * 2,
        compiler_params=_params(("arbitrary",)),
    )(qkv, qkv, gb, gb, ssf, ssb, tsf, tsb, do, do, dsf, dsb)


def _stage1(zb, ua, va, za, o, gon, lng, lnb, wsp, bsp):
    gv = [_gelu(t) for t in va]
    mu = sum(jnp.sum(t, axis=-1, keepdims=True) for t in gv) * (1.0 / D)
    xc = [t - mu for t in gv]
    var = sum(jnp.sum(t * t, axis=-1, keepdims=True) for t in xc) * (1.0 / D)
    rs = lax.rsqrt(var + EPS)
    ya, yb = [], []
    for g in range(H):
        vv = xc[g] * rs * lng[g] + lnb[g]
        s = _dot(wsp[g], vv) + bsp[g]
        ya.append(_gelu(ua[g]) * s * _silu(za[g]))
        r = lax.rsqrt(jnp.mean(o[g] * o[g], axis=-1, keepdims=True) + EPS)
        yb.append(o[g] * r * gon * _silu(zb[g]))
    return ya, yb


def _stage2(ma, mb, ga, gb):
    return jax.nn.sigmoid(ga) * ma + jax.nn.sigmoid(gb) * mb


def _stage3(out, x, tgt, gpost, gate):
    r = out * lax.rsqrt(jnp.mean(out * out, axis=-1, keepdims=True) + EPS) * gpost
    err = x + gate * r - tgt
    return 0.5 * jnp.sum(jnp.mean(err * err, axis=-1, keepdims=True), axis=0, keepdims=True)


def _post(p_rest, o_f, o_b, x, tgt, mod, bmod, gon, lng, lnb, wsp, bspb, wpa, wpb, wout, gpost, *, name):
    rows = x.shape[0]
    n = rows // GM
    lanes = lambda g: slice(g * DH, (g + 1) * DH)
    bdot = lambda a, w_ref: _dot(a.astype(BF16), w_ref[...])
    bdot_t = lambda a, w_ref: _dot(a.astype(BF16), w_ref[...], _NT)

    def body(p_ref, of_ref, ob_ref, x_ref, t_ref, m_ref, bm_ref, gon_ref, lng_ref, lnb_ref, wsp_ref, bsp_ref,
             wpa_ref, wpb_ref, wout_ref, gp_ref,
             loss_ref, dp_ref, do_ref, dx_ref, ya_ref, yb_ref, mg_ref, dma_ref, dmb_ref, dout_ref,
             dvec_ref, dgon_ref, dwsp_ref, dbsp_ref):
        @pl.when(pl.program_id(0) == 0)
        def _():
            loss_ref[...] = jnp.zeros_like(loss_ref)
            dvec_ref[...] = jnp.zeros_like(dvec_ref)
            dgon_ref[...] = jnp.zeros_like(dgon_ref)
            dwsp_ref[...] = jnp.zeros_like(dwsp_ref)
            dbsp_ref[...] = jnp.zeros_like(dbsp_ref)

        piece = lambda blk: [p_ref[:, blk * D + g * DH: blk * D + (g + 1) * DH] for g in range(H)]
        zb, ua, va, za = piece(0), piece(1), piece(2), piece(3)
        o = [of_ref[g] + ob_ref[g] for g in range(H)]
        gon = gon_ref[...]
        lng = [lng_ref[:, lanes(g)] for g in range(H)]
        lnb = [lnb_ref[:, lanes(g)] for g in range(H)]
        wsp = [wsp_ref[g] for g in range(H)]
        bsp = [bsp_ref[g] for g in range(H)]
        (ya, yb), vjp1 = jax.vjp(_stage1, zb, ua, va, za, o, gon, lng, lnb, wsp, bsp)
        y_a, y_b = jnp.concatenate(ya, axis=1), jnp.concatenate(yb, axis=1)
        ma, mb = bdot(y_a, wpa_ref), bdot(y_b, wpb_ref)
        ga, gb = p_ref[:, 4 * D:5 * D], p_ref[:, 5 * D:6 * D]
        merged, vjp2 = jax.vjp(_stage2, ma, mb, ga, gb)
        out = bdot(merged, wout_ref)
        gate = m_ref[:, 2 * D:3 * D] + bm_ref[:, 2 * D:3 * D]
        loss, vjp3 = jax.vjp(_stage3, out, x_ref[...], t_ref[...], gp_ref[...], gate)
        loss_ref[...] += jnp.broadcast_to(loss, loss_ref.shape)

        dout, dx, _, dgpost, dgate = vjp3(jnp.ones((1, 1), F32))
        dx_ref[...] = dx
        dmerged = bdot_t(dout, wout_ref)
        dma, dmb, dga, dgb = vjp2(dmerged)
        dya, dyb = bdot_t(dma, wpa_ref), bdot_t(dmb, wpb_ref)
        dzb, dua, dva, dza, do, dgon, dlng, dlnb, dwsp, dbsp = vjp1(
            ([dya[:, lanes(g)] for g in range(H)], [dyb[:, lanes(g)] for g in range(H)]))

        for blk, dlist in enumerate((dzb, dua, dva, dza)):
            for g in range(H):
                dp_ref[:, blk * D + g * DH: blk * D + (g + 1) * DH] = dlist[g].astype(BF16)
        dp_ref[:, 4 * D:5 * D] = dga.astype(BF16)
        dp_ref[:, 5 * D:6 * D] = dgb.astype(BF16)
        for g in range(H):
            do_ref[g] = do[g]
            dwsp_ref[g] += dwsp[g]
            dbsp_ref[g] += dbsp[g]
            dvec_ref[2:3, lanes(g)] += dlng[g]
            dvec_ref[3:4, lanes(g)] += dlnb[g]
        dvec_ref[0:1, :] += dgpost
        dvec_ref[1:2, :] += dgate
        dgon_ref[0:1, :] += dgon
        ya_ref[...] = y_a.astype(BF16)
        yb_ref[...] = y_b.astype(BF16)
        mg_ref[...] = merged.astype(BF16)
        dma_ref[...] = dma.astype(BF16)
        dmb_ref[...] = dmb.astype(BF16)
        dout_ref[...] = dout.astype(BF16)

    row = lambda w: pl.BlockSpec((GM, w), lambda i: (i, 0))
    heads = pl.BlockSpec((H, GM, DH), lambda i: (0, i, 0))
    full = lambda shape: pl.BlockSpec(shape, lambda i: tuple(0 for _ in shape))
    sds = jax.ShapeDtypeStruct
    return pl.pallas_call(
        body, name=name, grid=(n,),
        in_specs=[row(6 * D), heads, heads, row(D), row(D), full((1, 3 * D)), full((1, 3 * D)), full((1, DH)),
                  full((1, D)), full((1, D)), full((H, GM, GM)), full((H, GM, GM)),
                  full((D, D)), full((D, D)), full((D, D)), full((1, D))],
        out_specs=[full((8, DH)), row(6 * D), heads, row(D)] + [row(D)] * 6
        + [full((8, D)), full((8, DH)), full((H, GM, GM)), full((H, GM, GM))],
        out_shape=[sds((8, DH), F32), sds((rows, 6 * D), BF16), sds((H, rows, DH), F32), sds((rows, D), F32)]
        + [sds((rows, D), BF16)] * 6 + [sds((8, D), F32), sds((8, DH), F32), sds((H, GM, GM), F32), sds((H, GM, GM), F32)],
        compiler_params=_params(("arbitrary",)),
    )(p_rest, o_f, o_b, x, tgt, mod, bmod, gon, lng, lnb, wsp, bspb, wpa, wpb, wout, gpost)


def _dsilu_mul(c, d, *, name):
    def body(c_ref, d_ref, o_ref):
        _, vjp = jax.vjp(_silu, c_ref[...])
        (o_ref[...],) = vjp(d_ref[...])

    return pl.pallas_call(body, name=name, out_shape=jax.ShapeDtypeStruct(c.shape, F32))(c, d)


def _adamw(w, g, m, v, *, g_row_block=0, name):
    rows, cols = w.shape
    t = _tile(rows)
    gb = rows // t * g_row_block
    c1 = 1.0 / (1.0 - ADAM_B1 ** ADAM_STEP)
    c2 = 1.0 / (1.0 - ADAM_B2 ** ADAM_STEP)

    def body(w_ref, g_ref, m_ref, v_ref, go_ref, d_ref, mo_ref, vo_ref):
        gv = g_ref[...]
        mn = ADAM_B1 * m_ref[...] + (1.0 - ADAM_B1) * gv
        vn = ADAM_B2 * v_ref[...] + (1.0 - ADAM_B2) * (gv * gv)
        go_ref[...] = gv
        mo_ref[...] = mn
        vo_ref[...] = vn
        d_ref[...] = -ADAM_LR * ((mn * c1) / (jnp.sqrt(vn * c2) + ADAM_EPS) + ADAM_WD * w_ref[...])

    tile = pl.BlockSpec((t, cols), lambda i: (i, 0))
    return pl.pallas_call(
        body, name=name, grid=(rows // t,),
        in_specs=[tile, pl.BlockSpec((t, cols), lambda i: (gb + i, 0)), tile, tile], out_specs=[tile] * 4,
        out_shape=[jax.ShapeDtypeStruct((rows, cols), F32)] * 4,
        compiler_params=_params(("parallel",)),
    )(w, g, m, v)


def _add_sibling(full, got, where, *, name):
    s, rows, cols = full.shape
    hr = rows // 2
    t = _tile(hr)

    def body(w_ref, a_ref, b_ref, o_ref):
        o_ref[...] = a_ref[0] + b_ref[...]

    tile = pl.BlockSpec((1, t, cols), lambda j, i, w: (j, i, 0))
    return pl.pallas_call(
        body, name=name,
        grid_spec=pltpu.PrefetchScalarGridSpec(
            num_scalar_prefetch=1, grid=(s, hr // t),
            in_specs=[pl.BlockSpec((1, 1, t, cols), lambda j, i, w: (j, w[0], i, 0)), tile], out_specs=tile),
        out_shape=jax.ShapeDtypeStruct((s, hr, cols), F32), compiler_params=_params(("parallel", "parallel")),
    )(where, full.reshape(s, 2, hr, cols), got)


def _sum_chips(own, slots, where, n_out, which, *, name):
    _, hr, cols = own.shape
    t = _tile(hr)

    def body(w_ref, a_ref, s_ref, o_ref):
        o_ref[0] = ((a_ref[0] + s_ref[0]) + s_ref[1]) + s_ref[2]

    return pl.pallas_call(
        body, name=name,
        grid_spec=pltpu.PrefetchScalarGridSpec(
            num_scalar_prefetch=1, grid=(hr // t,),
            in_specs=[pl.BlockSpec((1, t, cols), lambda i, w: (w[1], i, 0)),
                      pl.BlockSpec((N_CHIPS - 1, t, cols), lambda i, w: (0, i, 0))],
            out_specs=pl.BlockSpec((1, t, cols), lambda i, w: (w[which], i, 0))),
        out_shape=jax.ShapeDtypeStruct((n_out, hr, cols), F32), compiler_params=_params(("parallel",)),
    )(where, own, slots)


def _local_step(x, c, ctx, c_ctx, tgt, w_mod, b_mod, g_pre, g_post, w_qkv, w_ab, w_rest, w_conv, a_log, dt_bias,
                g_onorm, gm_ln_g, gm_ln_b, w_sp, b_sp, w_pa, w_pb, w_out):
    rows, rows_c = x.shape[0], ctx.shape[0]
    cc = jnp.zeros((16, D), F32).at[0].set(c[0]).at[1].set(c_ctx)
    mod = _mm(cc, w_mod, pre_silu=True, name="mod_fwd")
    mod_x, mod_c = mod[0:1], mod[1:2]
    avec = jnp.zeros((1, DH), F32).at[0, :2 * H].set(a_log.reshape(-1))
    dvec = jnp.zeros((1, DH), F32).at[0, :2 * H].set(dt_bias.reshape(-1))
    bspb = jnp.broadcast_to(b_sp[:, :, None], (H, GM, GM))
    w_kv, wconv_kv = w_qkv[:, D:], w_conv[:, D:]

    h_c = _norm_fwd(ctx, g_pre, mod_c, b_mod, name="norm_fwd_ctx")
    pc_kv = _mm(h_c, w_kv, name="inproj_ctx_kv")
    pc_ab = _mm(h_c, w_ab, name="inproj_ctx_ab")
    kv_c = _prep_fwd(pc_kv, wconv_kv, H, name="prep_fwd_ctx")
    gb_c = _gates_fwd(pc_ab, avec, dvec, name="gates_fwd_ctx")
    s_zero = jnp.zeros((H, DH, DH), F32)
    _, _, ssf_c, ssb_c, tsf_c, tsb_c, s_f, s_b = _gdn_fwd(kv_c, gb_c, s_zero, s_zero, False, name="gdn_fwd_ctx")

    h_x = _norm_fwd(x, g_pre, mod_x, b_mod, name="norm_fwd_x")
    p_qkv = _mm(h_x, w_qkv, name="inproj_qkv")
    p_ab = _mm(h_x, w_ab, name="inproj_ab")
    p_rest = _mm(h_x, w_rest, name="inproj_rest")
    qkv = _prep_fwd(p_qkv, w_conv, 2 * H, name="prep_fwd_x")
    gb_x = _gates_fwd(p_ab, avec, dvec, name="gates_fwd_x")
    o_f, o_b, ssf, ssb, tsf, tsb, _, _ = _gdn_fwd(qkv, gb_x, s_f, s_b, True, name="gdn_fwd_x")

    (loss_acc, dp_rest, do, dx_res, ya, yb, mg, dma, dmb, dout, dvec_post, dgon, dwsp, dbspb) = _post(
        p_rest, o_f, o_b, x, tgt, mod_x, b_mod, g_onorm, gm_ln_g, gm_ln_b, w_sp, bspb, w_pa, w_pb, w_out, g_post,
        name="post")
    g = {}
    g["w_pa"] = _mm(ya, dma, ta=True, name="dw_pa")
    g["w_pb"] = _mm(yb, dmb, ta=True, name="dw_pb")
    g["w_out"] = _mm(mg, dout, ta=True, name="dw_out")

    zeros_s = jnp.zeros((H, DH, DH), F32)
    dq_f, dq_b, dg_f, dg_b, ds0_f, ds0_b = _gdn_bwd(qkv, gb_x, ssf, ssb, tsf, tsb, do, zeros_s, zeros_s, True,
                                                    name="gdn_bwd_x")
    dp_qkv, dwc_x = _prep_bwd(p_qkv, w_conv, dq_f, dq_b, 2 * H, name="prep_bwd_x")
    dp_ab, dav_x, ddv_x = _gates_bwd(p_ab, avec, dvec, dg_f, dg_b, name="gates_bwd_x")
    dkv_f, dkv_b, dgc_f, dgc_b, _, _ = _gdn_bwd(kv_c, gb_c, ssf_c, ssb_c, tsf_c, tsb_c, jnp.zeros((H, rows_c, DH), F32),
                                                 ds0_f, ds0_b, False, name="gdn_bwd_ctx")
    dpc_kv, dwc_c = _prep_bwd(pc_kv, wconv_kv, dkv_f, dkv_b, H, name="prep_bwd_ctx")
    dpc_ab, dav_c, ddv_c = _gates_bwd(pc_ab, avec, dvec, dgc_f, dgc_b, name="gates_bwd_ctx")

    dw_kv_c = _mm(h_c, dpc_kv, ta=True, tk=512, name="dw_kv_ctx")
    dw_qkv = _mm(h_x, dp_qkv, ta=True, tk=512, tn=D, add=dw_kv_c, add_from=1, name="dw_qkv")
    dw_ab_c = _mm(h_c, dpc_ab, ta=True, tk=512, name="dw_ab_ctx")
    dw_ab = _mm(h_x, dp_ab, ta=True, tk=512, add=dw_ab_c, name="dw_ab")
    dw_rest = _mm(h_x, dp_rest, ta=True, tk=512, name="dw_rest")
    g["w_in"] = (dw_qkv, dw_ab[:, :4 * H], dw_rest)
    dh_parts = [_mm(dp_qkv, w_qkv, tb=True, name="dh_qkv"), _mm(dp_rest, w_rest, tb=True, name="dh_rest"),
                _mm(dp_ab, w_ab, tb=True, name="dh_ab")]
    grad_x, dgpre_x, dm_x = _norm_bwd(x, g_pre, mod_x, b_mod, dh_parts, dx_res, name="norm_bwd_x")
    dhc_parts = [_mm(dpc_kv, w_kv, tb=True, name="dhc_kv"), _mm(dpc_ab, w_ab, tb=True, name="dhc_ab")]
    _, dgpre_c, dm_c = _norm_bwd(ctx, g_pre, mod_c, b_mod, dhc_parts, None, name="norm_bwd_ctx")

    dm_x = dm_x.at[:, 2 * D:].add(dvec_post[1:2])
    dmod = jnp.zeros((16, 3 * D), F32).at[0].set(dm_x[0]).at[1].set(dm_c[0])
    g["w_mod"] = _mm(cc, dmod, ta=True, pre_silu=True, name="dw_mod")
    dcc = _mm(dmod, w_mod, tb=True, name="dcc")
    g["c_ctx"] = _dsilu_mul(cc[:8], dcc[:8], name="dc_ctx")[1]
    g["b_mod"] = dm_x + dm_c
    g["g_pre"] = dgpre_x + dgpre_c
    g["g_post"] = dvec_post[0:1]
    g["gm_ln_g"], g["gm_ln_b"] = dvec_post[2:3], dvec_post[3:4]
    g["g_onorm"] = dgon[0:1]
    g["w_sp"] = dwsp
    g["b_sp"] = jnp.sum(dbspb, axis=-1)
    g["w_conv"] = dwc_x.at[:, D:].add(dwc_c)
    g["a_log"] = (dav_x + dav_c)[0, :2 * H].reshape(2, H)
    g["dt_bias"] = (ddv_x + ddv_c)[0, :2 * H].reshape(2, H)
    return loss_acc[0, 0], grad_x, g


ANY = pl.BlockSpec(memory_space=pl.ANY)


def _place():
    x, y, c = lax.axis_index("x"), lax.axis_index("y"), lax.axis_index("c")
    chips = [(1 - x, y), (x, 1 - y), (1 - x, 1 - y)]
    return x, y, c, (x, y, 1 - c), chips


def _gather_shards(big, small, *, name):
    nb = len(big)

    def body(*refs):
        ins, sm_in = refs[:nb], refs[nb]
        outs, sm_out = refs[nb + 1:2 * nb + 1], refs[2 * nb + 1]
        send, recv = refs[2 * nb + 2:]
        x, y, c, sibling, chips = _place()
        mine = 2 * x + y

        def half(a, shard, hc):
            hr = big[a].shape[0] // 2
            return outs[a].at[shard, pl.ds(hc * hr, hr), :]

        def remote(k, src, dst, to):
            return pltpu.make_async_remote_copy(src_ref=src, dst_ref=dst, send_sem=send.at[k], recv_sem=recv.at[k],
                                                device_id=to, device_id_type=MESH)

        sends = []
        for a in range(nb):
            hr = big[a].shape[0] // 2
            for j, chip in enumerate(chips):
                sends.append(remote(a * 3 + j, ins[a].at[pl.ds(c * hr, hr), :], half(a, mine, c), (*chip, c)))
        for j, chip in enumerate(chips):
            sends.append(remote(nb * 3 + j, sm_in, sm_out.at[mine], (*chip, c)))
        for cp in sends:
            cp.start()
        base = nb * 3 + 3
        passed = []
        for a in range(nb):
            for j, (px, py) in enumerate(chips):
                theirs = 2 * px + py
                remote(a * 3 + j, half(a, theirs, c), half(a, theirs, c), sibling).wait_recv()
                fw = remote(base + a * 3 + j, half(a, theirs, c), half(a, theirs, c), sibling)
                fw.start()
                passed.append(fw)
        for a in range(nb):
            for j, (px, py) in enumerate(chips):
                theirs = 2 * px + py
                remote(base + a * 3 + j, half(a, theirs, 1 - c), half(a, theirs, 1 - c), sibling).wait_recv()
        for j, (px, py) in enumerate(chips):
            remote(nb * 3 + j, sm_in, sm_out.at[2 * px + py], sibling).wait_recv()
        for cp in sends + passed:
            cp.wait_send()

    n_remote = 2 * nb * 3 + 3
    outs = pl.pallas_call(
        body, name=name, in_specs=[ANY] * (nb + 1), out_specs=[ANY] * (nb + 1),
        out_shape=[jax.ShapeDtypeStruct((N_CHIPS,) + a.shape, a.dtype) for a in big + [small]],
        scratch_shapes=[pltpu.SemaphoreType.DMA((n_remote,)), pltpu.SemaphoreType.DMA((n_remote,))],
    )(*big, small)
    return outs[:nb], outs[nb]


def _row_chunks(rows, row_bytes, align=8):
    n = max(1, min(rows // align, -(-rows * row_bytes // DMA_CHUNK_BYTES)))
    per = -(-(-(-rows // n)) // align) * align
    return [(r, min(per, rows - r)) for r in range(0, rows, per)]


def _remote(src, dst, send, recv, to):
    return pltpu.make_async_remote_copy(src_ref=src, dst_ref=dst, send_sem=send, recv_sem=recv, device_id=to,
                                        device_id_type=MESH)


def _exchange_halves(arrs, *, name):
    na = len(arrs)

    def body(*refs):
        ins, got = refs[:na], refs[na:2 * na]
        send, recv = refs[2 * na:]
        x, y, c, sibling, _ = _place()
        for a in range(na):
            ns, rows, cols = arrs[a].shape
            hr = rows // 2
            for s in range(ns):
                for r0, nr in _row_chunks(hr, cols * 4):
                    _remote(ins[a].at[s, pl.ds((1 - c) * hr + r0, nr), :], got[a].at[s, pl.ds(r0, nr), :],
                            send.at[a], recv.at[a], sibling).start()
        for a in range(na):
            hr = arrs[a].shape[1] // 2
            _remote(ins[a].at[:, pl.ds((1 - c) * hr, hr), :], got[a], send.at[a], recv.at[a], sibling).wait()

    return pl.pallas_call(
        body, name=name, in_specs=[ANY] * na, out_specs=[ANY] * na,
        out_shape=[jax.ShapeDtypeStruct((N_CHIPS, a.shape[1] // 2, a.shape[2]), F32) for a in arrs],
        scratch_shapes=[pltpu.SemaphoreType.DMA((na,)), pltpu.SemaphoreType.DMA((na,))],
    )(*arrs)


def _scatter_sections(arrs, *, name):
    na = len(arrs)

    def body(*refs):
        ins, outs = refs[:na], refs[na:2 * na]
        send, recv = refs[2 * na:]
        x, y, c, _, chips = _place()
        for a in range(na):
            _, hr, cols = arrs[a].shape
            for j, (px, py) in enumerate(chips):
                for r0, nr in _row_chunks(hr, cols * 4):
                    _remote(ins[a].at[2 * px + py, pl.ds(r0, nr), :], outs[a].at[j, pl.ds(r0, nr), :],
                            send.at[a * 3 + j], recv.at[a * 3 + j], (px, py, c)).start()
        for a in range(na):
            for j, (px, py) in enumerate(chips):
                _remote(ins[a].at[2 * px + py], outs[a].at[j], send.at[a * 3 + j], recv.at[a * 3 + j], (px, py, c)).wait()

    return pl.pallas_call(
        body, name=name, in_specs=[ANY] * na, out_specs=[ANY] * na,
        out_shape=[jax.ShapeDtypeStruct((N_CHIPS - 1,) + a.shape[1:], F32) for a in arrs],
        scratch_shapes=[pltpu.SemaphoreType.DMA((3 * na,)), pltpu.SemaphoreType.DMA((3 * na,))],
    )(*arrs)


def _finish_reduce(big, small, *, name):
    nb = len(big)

    def body(*refs):
        outs, sm = refs[nb + 1:2 * nb + 1], refs[2 * nb + 1]
        send, recv = refs[2 * nb + 2:]
        x, y, c, sibling, chips = _place()
        blk = lambda px, py, pc: sm.at[4 * px + 2 * py + pc]
        for a in range(nb):
            _, hr, cols = big[a].shape
            for r0, nr in _row_chunks(hr, cols * 4):
                _remote(outs[a].at[c, pl.ds(r0, nr), :], outs[a].at[c, pl.ds(r0, nr), :], send.at[a], recv.at[a],
                        sibling).start()
        first = [_remote(blk(x, y, c), blk(x, y, c), send.at[nb], recv.at[nb], sibling)]
        first += [_remote(blk(x, y, c), blk(x, y, c), send.at[nb + 1 + j], recv.at[nb + 1 + j], (*chip, c))
                  for j, chip in enumerate(chips)]
        for cp in first:
            cp.start()
        passed = []
        for j, (px, py) in enumerate(chips):
            _remote(blk(px, py, c), blk(px, py, c), send.at[nb + 1 + j], recv.at[nb + 1 + j], sibling).wait_recv()
            fw = _remote(blk(px, py, c), blk(px, py, c), send.at[nb + 4 + j], recv.at[nb + 4 + j], sibling)
            fw.start()
            passed.append(fw)
        for a in range(nb):
            _remote(outs[a].at[c], outs[a].at[1 - c], send.at[a], recv.at[a], sibling).wait()
        _remote(blk(x, y, c), blk(x, y, 1 - c), send.at[nb], recv.at[nb], sibling).wait_recv()
        for j, (px, py) in enumerate(chips):
            _remote(blk(px, py, c), blk(px, py, 1 - c), send.at[nb + 4 + j], recv.at[nb + 4 + j], sibling).wait_recv()
        for cp in first + passed:
            cp.wait_send()

    n_remote = nb + 7
    arrs = list(big) + [small]
    outs = pl.pallas_call(
        body, name=name, in_specs=[ANY] * (nb + 1), out_specs=[ANY] * (nb + 1),
        out_shape=[jax.ShapeDtypeStruct(a.shape, F32) for a in arrs],
        input_output_aliases={i: i for i in range(nb + 1)},
        scratch_shapes=[pltpu.SemaphoreType.DMA((n_remote,)), pltpu.SemaphoreType.DMA((n_remote,))],
    )(*arrs)
    return outs[:nb], outs[nb]


def _reduce_gradients(sectioned, small, where):
    arrs = list(sectioned) + [small]
    got = _exchange_halves(arrs, name="rs_exchange_halves")
    chip_sum = [_add_sibling(a, g, where, name=f"rs_add_sibling_{i}") for i, (a, g) in enumerate(zip(arrs, got))]
    slots = _scatter_sections(chip_sum, name="rs_scatter_sections")
    red = [_sum_chips(p, s, where, 2, 0, name=f"rs_sum_chips_{i}")
           for i, (p, s) in enumerate(zip(chip_sum[:-1], slots[:-1]))]
    red_small = _sum_chips(chip_sum[-1], slots[-1], where, 2 * N_CHIPS, 2, name="rs_sum_chips_small")
    big, sm = _finish_reduce(red, red_small, name="rs_finish")
    return [b.reshape(-1, b.shape[-1]) for b in big], sm.reshape(-1, sm.shape[-1])


def kernel(x, c, ctx, c_ctx, w_mod, b_mod, g_pre, g_post, w_in, w_conv, a_log, dt_bias, g_onorm, gm_ln_g, gm_ln_b, w_sp, b_sp, w_pa, w_pb, w_out, loss_target, m_c_ctx, m_w_mod, m_b_mod, m_g_pre, m_g_post, m_w_in, m_w_conv, m_a_log, m_dt_bias, m_g_onorm, m_gm_ln_g, m_gm_ln_b, m_w_sp, m_b_sp, m_w_pa, m_w_pb, m_w_out, v_c_ctx, v_w_mod, v_b_mod, v_g_pre, v_g_post, v_w_in, v_w_conv, v_a_log, v_dt_bias, v_g_onorm, v_gm_ln_g, v_gm_ln_b, v_w_sp, v_b_sp, v_w_pa, v_w_pb, v_w_out):
    names = ["c_ctx", "w_mod", "b_mod", "g_pre", "g_post", "w_in", "w_conv", "a_log", "dt_bias", "g_onorm", "gm_ln_g",
             "gm_ln_b", "w_sp", "b_sp", "w_pa", "w_pb", "w_out"]
    w = dict(zip(names, (c_ctx, w_mod, b_mod, g_pre, g_post, w_in, w_conv, a_log, dt_bias, g_onorm, gm_ln_g, gm_ln_b,
                         w_sp, b_sp, w_pa, w_pb, w_out)))
    m = dict(zip(names, (m_c_ctx, m_w_mod, m_b_mod, m_g_pre, m_g_post, m_w_in, m_w_conv, m_a_log, m_dt_bias, m_g_onorm,
                         m_gm_ln_g, m_gm_ln_b, m_w_sp, m_b_sp, m_w_pa, m_w_pb, m_w_out)))
    v = dict(zip(names, (v_c_ctx, v_w_mod, v_b_mod, v_g_pre, v_g_post, v_w_in, v_w_conv, v_a_log, v_dt_bias, v_g_onorm,
                         v_gm_ln_g, v_gm_ln_b, v_w_sp, v_b_sp, v_w_pa, v_w_pb, v_w_out)))
    xy = 2 * lax.axis_index("x") + lax.axis_index("y")
    where = jnp.stack([lax.axis_index("c"), xy, 2 * xy + lax.axis_index("c")]).astype(jnp.int32)

    rowpack = jnp.concatenate([w_pa[0], w_pb[0], w_out[0]], axis=0).astype(BF16)
    shards = [w_mod[0].astype(BF16), w_in[0].astype(BF16), rowpack]
    gathered, wconv_all = _gather_shards(shards, w_conv[0], name="gather_weights")
    own = lambda full, shard: lax.dynamic_update_slice(full, shard[None], (xy, 0, 0))
    wm_all, win_all, row_all = [own(f, s) for f, s in zip(gathered, shards)]
    wconv_all = own(wconv_all, w_conv[0])
    cols = lambda a: jnp.concatenate([a[s] for s in range(N_CHIPS)], axis=1)
    w_mod_f, w_conv_f = cols(wm_all), cols(wconv_all)
    blk = D // N_CHIPS
    rows_of = lambda i: jnp.concatenate([row_all[s, i * blk:(i + 1) * blk] for s in range(N_CHIPS)], axis=0)
    shard_cols = IN_COLS // N_CHIPS
    cut = OFF_A - shard_cols
    assert 0 < cut and cut + 4 * H < shard_cols
    w_qkv = jnp.concatenate([win_all[0], win_all[1][:, :cut]], axis=1)
    w_ab = jnp.pad(win_all[1][:, cut:cut + 4 * H], ((0, 0), (0, DH - 4 * H)))
    w_rest = jnp.concatenate([win_all[1][:, cut + 4 * H:], win_all[2], win_all[3]], axis=1)

    loss_local, grad_x, g = _local_step(
        x[0], c, ctx[0], c_ctx, loss_target[0], w_mod_f, b_mod, g_pre, g_post, w_qkv, w_ab, w_rest,
        w_conv_f, a_log[0], dt_bias[0], g_onorm, gm_ln_g, gm_ln_b, w_sp[0], b_sp[0], rows_of(0), rows_of(1), rows_of(2))
    loss = lax.psum(loss_local, ("x", "y", "c"))

    sect_cols = lambda a: jnp.stack(jnp.split(a, N_CHIPS, axis=1))
    g_rows = jnp.stack([jnp.concatenate([g[k][s * blk:(s + 1) * blk] for k in ("w_pa", "w_pb", "w_out")], axis=0)
                        for s in range(N_CHIPS)])
    dw_qkv, dw_ab, dw_rest = g["w_in"]
    rest_cut = shard_cols - cut - 4 * H
    g_win = jnp.stack([dw_qkv[:, :shard_cols],
                       jnp.concatenate([dw_qkv[:, shard_cols:], dw_ab, dw_rest[:, :rest_cut]], axis=1),
                       dw_rest[:, rest_cut:rest_cut + shard_cols], dw_rest[:, rest_cut + shard_cols:]])
    g["_pad"] = jnp.zeros((SMALL_LAYOUT["_pad"][1],), F32)
    tail = jnp.zeros((N_CHIPS * SMALL_ROWS * 128 - sum(s for _, s in SMALL_LAYOUT.values()),), F32)
    flat = jnp.concatenate([g[k].reshape(-1) for k in SMALL_LAYOUT] + [tail])
    (gr_wm, gr_win, gr_rows), gr_small = _reduce_gradients(
        [sect_cols(g["w_mod"]), g_win, g_rows], flat.reshape(N_CHIPS, SMALL_ROWS, 128), where)
    gr_flat = gr_small.reshape(-1)

    res = {}
    res["w_mod"] = _adamw(w_mod[0], gr_wm, m_w_mod[0], v_w_mod[0], name="adamw_w_mod")
    res["w_in"] = _adamw(w_in[0], gr_win, m_w_in[0], v_w_in[0], name="adamw_w_in")
    for i, k in enumerate(("w_pa", "w_pb", "w_out")):
        res[k] = _adamw(w[k][0], gr_rows, m[k][0], v[k][0], g_row_block=i, name=f"adamw_{k}")
    small_names = [k for k in SMALL_LAYOUT if k not in ("_pad", "w_conv")]

    def pack(src):
        parts = [src[k].reshape(-1) if k in small_names else jnp.zeros((SMALL_LAYOUT[k][1],), F32) for k in SMALL_LAYOUT]
        return jnp.concatenate(parts + [tail]).reshape(-1, 128)

    sm_res = _adamw(pack(w), gr_small, pack(m), pack(v), name="adamw_small")
    for k in small_names:
        off, size = SMALL_LAYOUT[k]
        res[k] = [r.reshape(-1)[off:off + size].reshape(w[k].shape) for r in sm_res]
    off, size = SMALL_LAYOUT["w_conv"]
    g_conv = lax.dynamic_slice(gr_flat[off:off + size].reshape(3, 3 * D), (0, xy * (3 * D // N_CHIPS)), (3, 3 * D // N_CHIPS))
    conv_res = _adamw(jnp.pad(w_conv[0], ((0, 5), (0, 0))), jnp.pad(g_conv, ((0, 5), (0, 0))),
                      jnp.pad(m_w_conv[0], ((0, 5), (0, 0))), jnp.pad(v_w_conv[0], ((0, 5), (0, 0))), name="adamw_w_conv")
    res["w_conv"] = [r[:3] for r in conv_res]
    for k in ("w_mod", "w_in", "w_pa", "w_pb", "w_out", "w_conv"):
        res[k] = [r.reshape(w[k].shape) for r in res[k]]

    out = [loss, grad_x[None]]
    for i in range(4):
        out += [res[k][i] for k in names]
    return tuple(out)
```

```python
import functools

import jax
import jax.numpy as jnp
from jax import lax
from jax.experimental import pallas as pl
from jax.experimental.pallas import tpu as pltpu

F32 = jnp.float32
BF16 = jnp.bfloat16
HI = lax.Precision.HIGHEST
MESH = pl.DeviceIdType.MESH

D = 1024
H = 8
DH = 128
CH = 64
LOG_CH = 6
PAIR = 2 * CH
GM = 128
assert 1 << LOG_CH == CH and PAIR == DH
PREC_SOLVE = lax.Precision.HIGH
PREC_POWERS = (PREC_SOLVE,) * (LOG_CH - 1)
HEADS_PER_ITER_FWD = 8
HEADS_PER_ITER_BWD = 8
EPS = 1e-6
N_CHIPS = 4
OFF_A = 3 * D
OFF_ZB = OFF_A + 4 * H
IN_COLS = OFF_ZB + 6 * D
VMEM_LIMIT_V7X = 56 * 1024 * 1024
DMA_CHUNK_BYTES = 2 * 1024 * 1024

ADAM_LR, ADAM_B1, ADAM_B2, ADAM_EPS, ADAM_WD, ADAM_STEP = 0.001, 0.9, 0.999, 1e-08, 0.01, 10

SMALL_LAYOUT = {}
_off = 0
for _n, _s in (("c_ctx", D), ("b_mod", 3 * D), ("g_pre", D), ("g_post", D), ("gm_ln_g", D), ("gm_ln_b", D),
               ("b_sp", H * GM), ("g_onorm", DH), ("a_log", 2 * H), ("dt_bias", 2 * H), ("_pad", 96),
               ("w_conv", 3 * 3 * D), ("w_sp", H * GM * GM)):
    SMALL_LAYOUT[_n] = (_off, _s)
    _off += _s
SMALL_ROWS = 304
assert N_CHIPS * SMALL_ROWS * 128 >= _off and (SMALL_ROWS // 2) % 8 == 0


def _params(sem=None):
    return pltpu.CompilerParams(dimension_semantics=sem, vmem_limit_bytes=VMEM_LIMIT_V7X)


def _tile(n, cands=(256, 128, 64, 32, 16, 8)):
    for cand in cands:
        if n % cand == 0:
            return cand
    return n


def _silu(x):
    return x * jax.nn.sigmoid(x)


def _gelu(x):
    return 0.5 * x * (1.0 + jnp.tanh(0.7978845608028654 * (x + 0.044715 * (x * x * x))))


def _mm(a, b, *, ta=False, tb=False, out_dtype=F32, tm=512, tn=1024, tk=1024, pre_silu=False, add=None, add_from=0,
        b_sections=False, out_sections=False, name):
    m, k = (a.shape[1], a.shape[0]) if ta else a.shape
    if b_sections:
        sect = b.shape[2]
        n = b.shape[1] if tb else b.shape[0] * sect
        tn, tk = (tn, sect) if tb else (sect, tk)
    else:
        n = b.shape[0] if tb else b.shape[1]
    tm, tn, tk = min(tm, m), min(tn, n), min(tk, k)
    assert m % tm == 0 and n % tn == 0 and k % tk == 0, (name, m, n, k, tm, tn, tk)
    nk = k // tk
    dims = (((0,) if ta else (1,), (1,) if tb else (0,)), ((), ()))
    has_add = add is not None
    assert not has_add or add.shape == (m, n - add_from * tn), (name, add.shape)

    def body(*refs):
        a_ref, b_ref = refs[:2]
        o_ref, acc_ref = refs[-2:]
        kk = pl.program_id(2)
        av = a_ref[...]
        if pre_silu:
            av = _silu(av.astype(F32))
        bv = b_ref[0] if b_sections else b_ref[...]
        part = lax.dot_general(av.astype(BF16), bv.astype(BF16), dims, preferred_element_type=F32)

        @pl.when(kk == 0)
        def _():
            acc_ref[...] = part

        @pl.when(kk > 0)
        def _():
            acc_ref[...] += part

        @pl.when(kk == nk - 1)
        def _():
            res = acc_ref[...]
            if has_add:
                res = res + jnp.where(pl.program_id(1) >= add_from, refs[2][...], 0.0)
            if out_sections:
                o_ref[0] = res.astype(out_dtype)
            else:
                o_ref[...] = res.astype(out_dtype)

    a_spec = pl.BlockSpec((tk, tm), lambda i, j, q: (q, i)) if ta else pl.BlockSpec((tm, tk), lambda i, j, q: (i, q))
    if b_sections:
        b_spec = (pl.BlockSpec((1, tn, tk), lambda i, j, q: (q, j, 0)) if tb
                  else pl.BlockSpec((1, tk, tn), lambda i, j, q: (j, q, 0)))
    else:
        b_spec = pl.BlockSpec((tn, tk), lambda i, j, q: (j, q)) if tb else pl.BlockSpec((tk, tn), lambda i, j, q: (q, j))
    add_spec = [pl.BlockSpec((tm, tn), lambda i, j, q: (i, jnp.maximum(j - add_from, 0)))] if has_add else []
    if out_sections:
        out_spec, out_shape = pl.BlockSpec((1, tm, tn), lambda i, j, q: (j, i, 0)), (n // tn, m, tn)
    else:
        out_spec, out_shape = pl.BlockSpec((tm, tn), lambda i, j, q: (i, j)), (m, n)
    return pl.pallas_call(
        body, name=name, grid=(m // tm, n // tn, nk),
        in_specs=[a_spec, b_spec] + add_spec, out_specs=out_spec,
        out_shape=jax.ShapeDtypeStruct(out_shape, out_dtype),
        scratch_shapes=[pltpu.VMEM((tm, tn), F32)],
        compiler_params=_params(("parallel", "parallel", "arbitrary")),
    )(*([a, b] + ([add] if has_add else [])))


def _h_fn(x, g, m):
    shift, scale = m[:, 0:D], m[:, D:2 * D]
    r = lax.rsqrt(jnp.mean(x * x, axis=-1, keepdims=True) + EPS)
    return (x * r * g) * (1.0 + scale) + shift


def _norm_fwd(x, g, mod, bmod, *, name):
    rows = x.shape[0]
    t = min(512, rows)

    def body(x_ref, g_ref, m_ref, b_ref, h_ref):
        h_ref[...] = _h_fn(x_ref[...], g_ref[...], m_ref[...] + b_ref[...]).astype(BF16)

    vec = lambda w: pl.BlockSpec((1, w), lambda i: (0, 0))
    return pl.pallas_call(
        body, name=name, grid=(rows // t,),
        in_specs=[pl.BlockSpec((t, D), lambda i: (i, 0)), vec(D), vec(3 * D), vec(3 * D)],
        out_specs=pl.BlockSpec((t, D), lambda i: (i, 0)),
        out_shape=jax.ShapeDtypeStruct((rows, D), BF16),
        compiler_params=_params(("parallel",)),
    )(x, g, mod, bmod)


def _norm_bwd(x, g, mod, bmod, dh_parts, resid, *, name):
    rows = x.shape[0]
    t = min(512, rows)
    n_parts = len(dh_parts)
    has_resid = resid is not None

    def body(*refs):
        x_ref, g_ref, m_ref, b_ref = refs[:4]
        parts = refs[4:4 + n_parts]
        r_ref = refs[4 + n_parts] if has_resid else None
        dx_ref, dg_ref, dm_ref = refs[-3:]
        i = pl.program_id(0)
        dh = parts[0][...]
        for p in parts[1:]:
            dh = dh + p[...]
        _, vjp = jax.vjp(_h_fn, x_ref[...], g_ref[...], m_ref[...] + b_ref[...])
        dx, dg, dm = vjp(dh)
        if has_resid:
            dx = dx + r_ref[...]
        dx_ref[...] = dx

        @pl.when(i == 0)
        def _():
            dg_ref[...] = dg
            dm_ref[...] = dm

        @pl.when(i > 0)
        def _():
            dg_ref[...] += dg
            dm_ref[...] += dm

    vec = lambda w: pl.BlockSpec((1, w), lambda i: (0, 0))
    tile = pl.BlockSpec((t, D), lambda i: (i, 0))
    ins = [x, g, mod, bmod, *dh_parts] + ([resid] if has_resid else [])
    return pl.pallas_call(
        body, name=name, grid=(rows // t,),
        in_specs=[tile, vec(D), vec(3 * D), vec(3 * D)] + [tile] * (n_parts + int(has_resid)),
        out_specs=[tile, vec(D), vec(3 * D)],
        out_shape=[jax.ShapeDtypeStruct((rows, D), F32), jax.ShapeDtypeStruct((1, D), F32),
                   jax.ShapeDtypeStruct((1, 3 * D), F32)],
        compiler_params=_params(("arbitrary",)),
    )(*ins)


def _conv_tile(u_ref, r0, t, rows, w0, w1, w2):
    u = u_ref[pl.ds(r0, t), :]
    prev8 = u_ref[pl.ds(pl.multiple_of(jnp.maximum(r0 - 8, 0), 8), 8), :]
    next8 = u_ref[pl.ds(pl.multiple_of(jnp.minimum(r0 + t, rows - 8), 8), 8), :]
    r8 = lax.broadcasted_iota(jnp.int32, (8, DH), 0)
    prev_row = jnp.sum(jnp.where(r8 == 7, prev8, 0.0), axis=0, keepdims=True)
    next_row = jnp.sum(jnp.where(r8 == 0, next8, 0.0), axis=0, keepdims=True)
    prev_row = jnp.where(r0 > 0, prev_row, 0.0)
    next_row = jnp.where(r0 + t < rows, next_row, 0.0)
    ri = lax.broadcasted_iota(jnp.int32, (t, DH), 0)
    um1 = jnp.where(ri == 0, prev_row, pltpu.roll(u, 1, 0))
    up1 = jnp.where(ri == t - 1, next_row, pltpu.roll(u, t - 1, 0))
    return w0 * um1 + w1 * u + w2 * up1, um1, u, up1


def _rowlocal(z, is_norm):
    y = _silu(z)
    yn = y * lax.rsqrt(jnp.sum(y * y, axis=-1, keepdims=True) + EPS)
    return jnp.where(is_norm, yn, y)


def _prep_fwd(p, wconv, n_norm, *, name):
    rows, nb = p.shape[0], p.shape[1] // DH
    t = min(512, rows)

    def body(u_ref, w_ref, o_ref):
        is_norm = pl.program_id(0) < n_norm
        w0, w1, w2 = w_ref[0:1, :], w_ref[1:2, :], w_ref[2:3, :]

        def step(s, carry):
            r0 = pl.multiple_of(s * t, t)
            z, _, _, _ = _conv_tile(u_ref, r0, t, rows, w0, w1, w2)
            o_ref[0, pl.ds(r0, t), :] = _rowlocal(z, is_norm)
            return carry

        lax.fori_loop(0, rows // t, step, 0)

    return pl.pallas_call(
        body, name=name, grid=(nb,),
        in_specs=[pl.BlockSpec((rows, DH), lambda j: (0, j)), pl.BlockSpec((3, DH), lambda j: (0, j))],
        out_specs=pl.BlockSpec((1, rows, DH), lambda j: (j, 0, 0)),
        out_shape=jax.ShapeDtypeStruct((nb, rows, DH), F32),
        compiler_params=_params(("parallel",)),
    )(p, wconv)


def _prep_bwd(p, wconv, d_a, d_b, n_norm, *, name):
    rows, nb = p.shape[0], p.shape[1] // DH
    t = min(512, rows)

    def body(u_ref, w_ref, da_ref, db_ref, du_ref, dw_ref, dz_ref):
        is_norm = pl.program_id(0) < n_norm
        w0, w1, w2 = w_ref[0:1, :], w_ref[1:2, :], w_ref[2:3, :]

        def step1(s, carry):
            a0, a1, a2 = carry
            r0 = pl.multiple_of(s * t, t)
            z, um1, u, up1 = _conv_tile(u_ref, r0, t, rows, w0, w1, w2)
            _, vjp = jax.vjp(lambda zz: _rowlocal(zz, is_norm), z)
            (dz,) = vjp(da_ref[0, pl.ds(r0, t), :] + db_ref[0, pl.ds(r0, t), :])
            dz_ref[pl.ds(r0, t), :] = dz
            red = lambda v: jnp.sum(v, axis=0, keepdims=True)
            return a0 + red(dz * um1), a1 + red(dz * u), a2 + red(dz * up1)

        zero = jnp.zeros((1, DH), F32)
        a0, a1, a2 = lax.fori_loop(0, rows // t, step1, (zero, zero, zero))
        dw_ref[0:1, :] = a0
        dw_ref[1:2, :] = a1
        dw_ref[2:3, :] = a2

        def step2(s, carry):
            r0 = pl.multiple_of(s * t, t)
            du, _, _, _ = _conv_tile(dz_ref, r0, t, rows, w2, w1, w0)
            du_ref[pl.ds(r0, t), :] = du.astype(BF16)
            return carry

        lax.fori_loop(0, rows // t, step2, 0)

    col = pl.BlockSpec((rows, DH), lambda j: (0, j))
    w_spec = pl.BlockSpec((3, DH), lambda j: (0, j))
    d_spec = pl.BlockSpec((1, rows, DH), lambda j: (j, 0, 0))
    return pl.pallas_call(
        body, name=name, grid=(nb,),
        in_specs=[col, w_spec, d_spec, d_spec], out_specs=[col, w_spec],
        out_shape=[jax.ShapeDtypeStruct((rows, nb * DH), BF16), jax.ShapeDtypeStruct((3, nb * DH), F32)],
        scratch_shapes=[pltpu.VMEM((rows, DH), F32)],
        compiler_params=_params(("parallel",)),
    )(p, wconv, d_a, d_b)


def _gates_fn(pab, avec, dvec):
    t = pab.shape[0]
    lane = lax.broadcasted_iota(jnp.int32, pab.shape, 1)
    xg = pab + dvec
    sp = jnp.maximum(xg, 0.0) + jnp.log(1.0 + jnp.exp(-jnp.abs(xg)))
    g = jnp.where(lane < 2 * H, -jnp.exp(avec) * sp, 0.0)
    ii = lax.broadcasted_iota(jnp.int32, (t, t), 0)
    jj = lax.broadcasted_iota(jnp.int32, (t, t), 1)
    same = (ii >> LOG_CH) == (jj >> LOG_CH)
    cum_f = _dot(jnp.where(same & (jj <= ii), 1.0, 0.0), g, precision=HI)
    cum_r = _dot(jnp.where(same & (jj >= ii), 1.0, 0.0), g, precision=HI)
    return jnp.where(lane < H, cum_f, jnp.where(lane < 2 * H, cum_r, jnp.where(lane < 4 * H, jax.nn.sigmoid(pab), 0.0)))


def _gates_fwd(pab, avec, dvec, *, name):
    rows = pab.shape[0]
    t = min(512, rows)

    def body(p_ref, a_ref, d_ref, o_ref):
        o_ref[...] = _gates_fn(p_ref[...], a_ref[...], d_ref[...])

    vec = pl.BlockSpec((1, DH), lambda i: (0, 0))
    tile = pl.BlockSpec((t, DH), lambda i: (i, 0))
    return pl.pallas_call(
        body, name=name, grid=(rows // t,), in_specs=[tile, vec, vec], out_specs=tile,
        out_shape=jax.ShapeDtypeStruct((rows, DH), F32), compiler_params=_params(("parallel",)),
    )(pab, avec, dvec)


def _gates_bwd(pab, avec, dvec, d_a, d_b, *, name):
    rows = pab.shape[0]
    t = min(512, rows)

    def body(p_ref, a_ref, d_ref, da_ref, db_ref, dp_ref, dav_ref, ddv_ref):
        i = pl.program_id(0)
        _, vjp = jax.vjp(_gates_fn, p_ref[...], a_ref[...], d_ref[...])
        dp, dav, ddv = vjp(da_ref[...] + db_ref[...])
        dp_ref[...] = dp.astype(BF16)

        @pl.when(i == 0)
        def _():
            dav_ref[...] = dav
            ddv_ref[...] = ddv

        @pl.when(i > 0)
        def _():
            dav_ref[...] += dav
            ddv_ref[...] += ddv

    vec = pl.BlockSpec((1, DH), lambda i: (0, 0))
    tile = pl.BlockSpec((t, DH), lambda i: (i, 0))
    return pl.pallas_call(
        body, name=name, grid=(rows // t,), in_specs=[tile, vec, vec, tile, tile], out_specs=[tile, vec, vec],
        out_shape=[jax.ShapeDtypeStruct((rows, DH), BF16), jax.ShapeDtypeStruct((1, DH), F32),
                   jax.ShapeDtypeStruct((1, DH), F32)],
        compiler_params=_params(("arbitrary",)),
    )(pab, avec, dvec, d_a, d_b)


def _dot(a, b, dims=((1,), (0,)), precision=None):
    return lax.dot_general(a, b, (dims, ((), ())), precision=precision, preferred_element_type=F32)


_NT = ((1,), (1,))
_TN = ((0,), (0,))


@jax.custom_vjp
def _saved_inverse(neg_a, inv):
    return inv


def _saved_inverse_fwd(neg_a, inv):
    return inv, inv


def _saved_inverse_bwd(inv, d_inv):
    idx = range(len(inv))
    left = [_dot(inv[i], d_inv[i], _TN, precision=PREC_SOLVE) for i in idx]
    d_neg_a = [_dot(left[i], inv[i], _NT, precision=PREC_SOLVE) for i in idx]
    return d_neg_a, [jnp.zeros_like(t) for t in inv]


_saved_inverse.defvjp(_saved_inverse_fwd, _saved_inverse_bwd)


def _pairs(s, q, k, v, gcol, bcol, revs, inv_saved=None):
    idx = range(len(revs))
    ii = lax.broadcasted_iota(jnp.int32, (PAIR, PAIR), 0)
    jj = lax.broadcasted_iota(jnp.int32, (PAIR, PAIR), 1)
    same = (ii >> LOG_CH) == (jj >> LOG_CH)
    incl_d = (same & (ii >= jj), same & (ii <= jj))
    strict_d = (same & (ii > jj), same & (ii < jj))
    incl = [incl_d[int(r)] for r in revs]
    strict = [strict_d[int(r)] for r in revs]
    eye = jnp.where(ii == jj, 1.0, 0.0)
    gc_i = [jnp.broadcast_to(gcol[i], (PAIR, DH)) for i in idx]
    gc_j = [gc_i[i].T for i in idx]
    decay = [jnp.where(incl[i], jnp.exp(jnp.where(incl[i], gc_i[i] - gc_j[i], 0.0)), 0.0) for i in idx]
    b_b = [jnp.broadcast_to(bcol[i], (PAIR, DH)) for i in idx]
    kb = [k[i] * b_b[i] for i in idx]
    kk = [_dot(kb[i], k[i], _NT) for i in idx]
    bp = [jnp.where(strict[i], -kk[i] * decay[i], 0.0) for i in idx]
    if inv_saved is not None:
        inv = _saved_inverse(bp, inv_saved)
    else:
        inv = [eye + bp[i] for i in idx]
        for prec in PREC_POWERS:
            bp = [_dot(bp[i], bp[i], precision=prec) for i in idx]
            more = [_dot(inv[i], bp[i], precision=prec) for i in idx]
            inv = [inv[i] + more[i] for i in idx]
    eg = [jnp.exp(gc_i[i]) for i in idx]
    sol = [_dot(inv[i], jnp.concatenate([v[i] * b_b[i], kb[i] * eg[i]], axis=1), precision=PREC_SOLVE) for i in idx]
    u_val = [sol[i][:, :DH] for i in idx]
    w_key = [sol[i][:, DH:] for i in idx]
    row = lax.broadcasted_iota(jnp.int32, (PAIR, 1), 0)
    has_q = q[0] is not None
    if has_q:
        qc = [q[i] * (DH ** -0.5) for i in idx]
        qk = [_dot(qc[i], k[i], _NT) for i in idx]
        attn = [qk[i] * decay[i] for i in idx]
        qd = [qc[i] * eg[i] for i in idx]
    outs = [[None, None] for _ in idx]
    zeros = jnp.zeros((CH, DH), F32)
    for step in range(2):
        cidx = [(1 - step) if revs[i] else step for i in idx]
        sl = [slice(c * CH, (c + 1) * CH) for c in cidx]
        last = [c * CH if revs[i] else c * CH + CH - 1 for i, c in zip(idx, cidx)]
        gl = [jnp.sum(jnp.where(row == last[i], gcol[i], 0.0), axis=0, keepdims=True) for i in idx]
        k_tail = [k[i][sl[i]] * jnp.exp(gl[i] - gc_i[i][sl[i]]) for i in idx]
        ws = [_dot(w_key[i][sl[i]], s[i]) for i in idx]
        v_new = [u_val[i][sl[i]] - ws[i] for i in idx]
        if has_q:
            v_pad = [jnp.concatenate([v_new[i], zeros] if cidx[i] == 0 else [zeros, v_new[i]], axis=0) for i in idx]
            o_state = [_dot(qd[i][sl[i]], s[i]) for i in idx]
            o_local = [_dot(attn[i][sl[i]], v_pad[i]) for i in idx]
            for i in idx:
                outs[i][cidx[i]] = o_state[i] + o_local[i]
        kv = [_dot(k_tail[i], v_new[i], _TN) for i in idx]
        s = [s[i] * jnp.exp(gl[i]) + kv[i] for i in idx]
    return s, ([jnp.concatenate(outs[i], axis=0) for i in idx] if has_q else None), inv


def _lane_col(tile, idx):
    lane = lax.broadcasted_iota(jnp.int32, tile.shape, 1)
    return jnp.sum(jnp.where(lane == idx, tile, 0.0), axis=1, keepdims=True)


def _gdn_fwd(qkv, gb, s0f, s0b, has_q, *, name):
    nb, rows, _ = qkv.shape
    n = rows // PAIR
    qoff = H if has_q else 0

    def body(qf_ref, qb_ref, gf_ref, gr_ref, s0f_ref, s0b_ref, of_ref, ob_ref, ssf_ref, ssb_ref, tsf_ref, tsb_ref,
             sf_ref, sb_ref):
        @pl.when(pl.program_id(0) == 0)
        def _():
            sf_ref[...] = s0f_ref[...]
            sb_ref[...] = s0b_ref[...]

        gtiles = (gf_ref[...], gr_ref[...])

        dirs = ((qf_ref, sf_ref, ssf_ref, of_ref), (qb_ref, sb_ref, ssb_ref, ob_ref))
        ts_refs = (tsf_ref, tsb_ref)

        def heads(hg, carry):
            work = [(hg * HEADS_PER_ITER_FWD + j, d) for j in range(HEADS_PER_ITER_FWD) for d in range(2)]
            loaded = []
            for h, d in work:
                q_ref, s_ref, _, _ = dirs[d]
                loaded.append((s_ref[h], q_ref[h] if has_q else None, q_ref[qoff + h], q_ref[qoff + H + h],
                               _lane_col(gtiles[d], d * H + h), _lane_col(gtiles[d], 2 * H + d * H + h)))
            s_new, o, inv = _pairs(*[list(col) for col in zip(*loaded)], revs=[d == 1 for _, d in work])
            for i, (h, d) in enumerate(work):
                _, s_ref, ss_ref, o_ref = dirs[d]
                ss_ref[0, h] = loaded[i][0]
                ts_refs[d][0, h] = inv[i]
                s_ref[h] = s_new[i]
                o_ref[h] = o[i] if has_q else jnp.zeros((PAIR, DH), F32)
            return carry

        if HEADS_PER_ITER_FWD == H:
            heads(0, 0)
        else:
            lax.fori_loop(0, H // HEADS_PER_ITER_FWD, heads, 0)

    fwd3 = lambda i: (0, i, 0)
    rev3 = lambda i: (0, n - 1 - i, 0)
    state = pl.BlockSpec((H, DH, DH), lambda i: (0, 0, 0))
    saved = pl.BlockSpec((1, H, DH, DH), lambda i: (i, 0, 0, 0))
    return pl.pallas_call(
        body, name=name, grid=(n,),
        in_specs=[pl.BlockSpec((nb, PAIR, DH), fwd3), pl.BlockSpec((nb, PAIR, DH), rev3),
                  pl.BlockSpec((PAIR, DH), lambda i: (i, 0)), pl.BlockSpec((PAIR, DH), lambda i: (n - 1 - i, 0)),
                  state, state],
        out_specs=[pl.BlockSpec((H, PAIR, DH), fwd3), pl.BlockSpec((H, PAIR, DH), rev3), saved, saved, saved, saved,
                   state, state],
        out_shape=[jax.ShapeDtypeStruct((H, rows, DH), F32)] * 2 + [jax.ShapeDtypeStruct((n, H, DH, DH), F32)] * 4
        + [jax.ShapeDtypeStruct((H, DH, DH), F32)] * 2,
        compiler_params=_params(("arbitrary",)),
    )(qkv, qkv, gb, gb, s0f, s0b)


def _gdn_bwd(qkv, gb, ssf, ssb, tsf, tsb, do, dsf, dsb, has_q, *, name):
    nb, rows, _ = qkv.shape
    n = rows // PAIR
    qoff = H if has_q else 0

    def body(qf_ref, qb_ref, gf_ref, gr_ref, ssf_ref, ssb_ref, tsf_ref, tsb_ref, dof_ref, dob_ref, dsf0_ref, dsb0_ref,
             dqf_ref, dqb_ref, dgf_ref, dgr_ref, dsf_ref, dsb_ref):
        ts_refs = (tsf_ref, tsb_ref)
        @pl.when(pl.program_id(0) == 0)
        def _():
            dsf_ref[...] = dsf0_ref[...]
            dsb_ref[...] = dsb0_ref[...]

        gtiles = (gf_ref[...], gr_ref[...])
        lane = lax.broadcasted_iota(jnp.int32, (PAIR, DH), 1)

        dirs = ((qf_ref, ssf_ref, dof_ref, dsf_ref, dqf_ref), (qb_ref, ssb_ref, dob_ref, dsb_ref, dqb_ref))

        def heads(hg, carry):
            out = list(carry)
            work = [(hg * HEADS_PER_ITER_BWD + j, d) for j in range(HEADS_PER_ITER_BWD) for d in range(2)]
            revs = [d == 1 for _, d in work]
            s_in, q_in, k_in, v_in, g_in, b_in, ds_out, do_out, inv_in = [], [], [], [], [], [], [], [], []
            for h, d in work:
                q_ref, ss_ref, do_ref, ds_ref, _ = dirs[d]
                s_in.append(ss_ref[0, h])
                inv_in.append(ts_refs[d][0, h])
                q_in.append(q_ref[h] if has_q else None)
                k_in.append(q_ref[qoff + h])
                v_in.append(q_ref[qoff + H + h])
                g_in.append(_lane_col(gtiles[d], d * H + h))
                b_in.append(_lane_col(gtiles[d], 2 * H + d * H + h))
                ds_out.append(ds_ref[h])
                do_out.append(do_ref[h] if has_q else None)
            if has_q:
                _, vjp = jax.vjp(lambda s_, q_, k_, v_, g_, b_: _pairs(s_, q_, k_, v_, g_, b_, revs, inv_in)[:2],
                                 s_in, q_in, k_in, v_in, g_in, b_in)
                ds, dq, dk, dv, dg, db = vjp((ds_out, do_out))
            else:
                _, vjp = jax.vjp(lambda s_, k_, v_, g_, b_: _pairs(s_, q_in, k_, v_, g_, b_, revs, inv_in)[0],
                                 s_in, k_in, v_in, g_in, b_in)
                ds, dk, dv, dg, db = vjp(ds_out)
            for i, (h, d) in enumerate(work):
                dq_ref, ds_ref = dirs[d][4], dirs[d][3]
                ds_ref[h] = ds[i]
                if has_q:
                    dq_ref[h] = dq[i]
                dq_ref[qoff + h], dq_ref[qoff + H + h] = dk[i], dv[i]
                out[d] = (out[d] + jnp.where(lane == d * H + h, dg[i], 0.0)
                          + jnp.where(lane == 2 * H + d * H + h, db[i], 0.0))
            return tuple(out)

        zero = jnp.zeros((PAIR, DH), F32)
        if HEADS_PER_ITER_BWD == H:
            dgf, dgr = heads(0, (zero, zero))
        else:
            dgf, dgr = lax.fori_loop(0, H // HEADS_PER_ITER_BWD, heads, (zero, zero))
        dgf_ref[...] = dgf
        dgr_ref[...] = dgr

    fwd3 = lambda i: (0, n - 1 - i, 0)
    rev3 = lambda i: (0, i, 0)
    state = pl.BlockSpec((H, DH, DH), lambda i: (0, 0, 0))
    saved = pl.BlockSpec((1, H, DH, DH), lambda i: (n - 1 - i, 0, 0, 0))
    gf_spec = pl.BlockSpec((PAIR, DH), lambda i: (n - 1 - i, 0))
    gr_spec = pl.BlockSpec((PAIR, DH), lambda i: (i, 0))
    return pl.pallas_call(
        body, name=name, grid=(n,),
        in_specs=[pl.BlockSpec((nb, PAIR, DH), fwd3), pl.BlockSpec((nb, PAIR, DH), rev3), gf_spec, gr_spec,
                  saved, saved, saved, saved,
                  pl.BlockSpec((H, PAIR, DH), fwd3), pl.BlockSpec((H, PAIR, DH), rev3), state, state],
        out_specs=[pl.BlockSpec((nb, PAIR, DH), fwd3), pl.BlockSpec((nb, PAIR, DH), rev3), gf_spec, gr_spec, state, state],
        out_shape=[jax.ShapeDtypeStruct((nb, rows, DH), F32)] * 2 + [jax.ShapeDtypeStruct((rows, DH), F32)] * 2
        + [jax.ShapeDtypeStruct((H, DH, DH), F32)] * 2,
        compiler_params=_params(("arbitrary",)),
    )(qkv, qkv, gb, gb, ssf, ssb, tsf, tsb, do, do, dsf, dsb)


def _stage1(zb, ua, va, za, o, gon, lng, lnb, wsp, bsp):
    gv = [_gelu(t) for t in va]
    mu = sum(jnp.sum(t, axis=-1, keepdims=True) for t in gv) * (1.0 / D)
    xc = [t - mu for t in gv]
    var = sum(jnp.sum(t * t, axis=-1, keepdims=True) for t in xc) * (1.0 / D)
    rs = lax.rsqrt(var + EPS)
    ya, yb = [], []
    for g in range(H):
        vv = xc[g] * rs * lng[g] + lnb[g]
        s = _dot(wsp[g], vv) + bsp[g]
        ya.append(_gelu(ua[g]) * s * _silu(za[g]))
        r = lax.rsqrt(jnp.mean(o[g] * o[g], axis=-1, keepdims=True) + EPS)
        yb.append(o[g] * r * gon * _silu(zb[g]))
    return ya, yb


def _stage2(ma, mb, ga, gb):
    return jax.nn.sigmoid(ga) * ma + jax.nn.sigmoid(gb) * mb


def _stage3(out, x, tgt, gpost, gate):
    r = out * lax.rsqrt(jnp.mean(out * out, axis=-1, keepdims=True) + EPS) * gpost
    err = x + gate * r - tgt
    return 0.5 * jnp.sum(jnp.mean(err * err, axis=-1, keepdims=True), axis=0, keepdims=True)


def _post(p_rest, o_f, o_b, x, tgt, mod, bmod, gon, lng, lnb, wsp, bspb, wpa, wpb, wout, gpost, *, name):
    rows = x.shape[0]
    n = rows // GM
    lanes = lambda g: slice(g * DH, (g + 1) * DH)
    bdot = lambda a, w_ref: _dot(a.astype(BF16), w_ref[...])
    bdot_t = lambda a, w_ref: _dot(a.astype(BF16), w_ref[...], _NT)

    def body(p_ref, of_ref, ob_ref, x_ref, t_ref, m_ref, bm_ref, gon_ref, lng_ref, lnb_ref, wsp_ref, bsp_ref,
             wpa_ref, wpb_ref, wout_ref, gp_ref,
             loss_ref, dp_ref, do_ref, dx_ref, ya_ref, yb_ref, mg_ref, dma_ref, dmb_ref, dout_ref,
             dvec_ref, dgon_ref, dwsp_ref, dbsp_ref):
        @pl.when(pl.program_id(0) == 0)
        def _():
            loss_ref[...] = jnp.zeros_like(loss_ref)
            dvec_ref[...] = jnp.zeros_like(dvec_ref)
            dgon_ref[...] = jnp.zeros_like(dgon_ref)
            dwsp_ref[...] = jnp.zeros_like(dwsp_ref)
            dbsp_ref[...] = jnp.zeros_like(dbsp_ref)

        piece = lambda blk: [p_ref[:, blk * D + g * DH: blk * D + (g + 1) * DH] for g in range(H)]
        zb, ua, va, za = piece(0), piece(1), piece(2), piece(3)
        o = [of_ref[g] + ob_ref[g] for g in range(H)]
        gon = gon_ref[...]
        lng = [lng_ref[:, lanes(g)] for g in range(H)]
        lnb = [lnb_ref[:, lanes(g)] for g in range(H)]
        wsp = [wsp_ref[g] for g in range(H)]
        bsp = [bsp_ref[g] for g in range(H)]
        (ya, yb), vjp1 = jax.vjp(_stage1, zb, ua, va, za, o, gon, lng, lnb, wsp, bsp)
        y_a, y_b = jnp.concatenate(ya, axis=1), jnp.concatenate(yb, axis=1)
        ma, mb = bdot(y_a, wpa_ref), bdot(y_b, wpb_ref)
        ga, gb = p_ref[:, 4 * D:5 * D], p_ref[:, 5 * D:6 * D]
        merged, vjp2 = jax.vjp(_stage2, ma, mb, ga, gb)
        out = bdot(merged, wout_ref)
        gate = m_ref[:, 2 * D:3 * D] + bm_ref[:, 2 * D:3 * D]
        loss, vjp3 = jax.vjp(_stage3, out, x_ref[...], t_ref[...], gp_ref[...], gate)
        loss_ref[...] += jnp.broadcast_to(loss, loss_ref.shape)

        dout, dx, _, dgpost, dgate = vjp3(jnp.ones((1, 1), F32))
        dx_ref[...] = dx
        dmerged = bdot_t(dout, wout_ref)
        dma, dmb, dga, dgb = vjp2(dmerged)
        dya, dyb = bdot_t(dma, wpa_ref), bdot_t(dmb, wpb_ref)
        dzb, dua, dva, dza, do, dgon, dlng, dlnb, dwsp, dbsp = vjp1(
            ([dya[:, lanes(g)] for g in range(H)], [dyb[:, lanes(g)] for g in range(H)]))

        for blk, dlist in enumerate((dzb, dua, dva, dza)):
            for g in range(H):
                dp_ref[:, blk * D + g * DH: blk * D + (g + 1) * DH] = dlist[g].astype(BF16)
        dp_ref[:, 4 * D:5 * D] = dga.astype(BF16)
        dp_ref[:, 5 * D:6 * D] = dgb.astype(BF16)
        for g in range(H):
            do_ref[g] = do[g]
            dwsp_ref[g] += dwsp[g]
            dbsp_ref[g] += dbsp[g]
            dvec_ref[2:3, lanes(g)] += dlng[g]
            dvec_ref[3:4, lanes(g)] += dlnb[g]
        dvec_ref[0:1, :] += dgpost
        dvec_ref[1:2, :] += dgate
        dgon_ref[0:1, :] += dgon
        ya_ref[...] = y_a.astype(BF16)
        yb_ref[...] = y_b.astype(BF16)
        mg_ref[...] = merged.astype(BF16)
        dma_ref[...] = dma.astype(BF16)
        dmb_ref[...] = dmb.astype(BF16)
        dout_ref[...] = dout.astype(BF16)

    row = lambda w: pl.BlockSpec((GM, w), lambda i: (i, 0))
    heads = pl.BlockSpec((H, GM, DH), lambda i: (0, i, 0))
    full = lambda shape: pl.BlockSpec(shape, lambda i: tuple(0 for _ in shape))
    sds = jax.ShapeDtypeStruct
    return pl.pallas_call(
        body, name=name, grid=(n,),
        in_specs=[row(6 * D), heads, heads, row(D), row(D), full((1, 3 * D)), full((1, 3 * D)), full((1, DH)),
                  full((1, D)), full((1, D)), full((H, GM, GM)), full((H, GM, GM)),
                  full((D, D)), full((D, D)), full((D, D)), full((1, D))],
        out_specs=[full((8, DH)), row(6 * D), heads, row(D)] + [row(D)] * 6
        + [full((8, D)), full((8, DH)), full((H, GM, GM)), full((H, GM, GM))],
        out_shape=[sds((8, DH), F32), sds((rows, 6 * D), BF16), sds((H, rows, DH), F32), sds((rows, D), F32)]
        + [sds((rows, D), BF16)] * 6 + [sds((8, D), F32), sds((8, DH), F32), sds((H, GM, GM), F32), sds((H, GM, GM), F32)],
        compiler_params=_params(("arbitrary",)),
    )(p_rest, o_f, o_b, x, tgt, mod, bmod, gon, lng, lnb, wsp, bspb, wpa, wpb, wout, gpost)


def _dsilu_mul(c, d, *, name):
    def body(c_ref, d_ref, o_ref):
        _, vjp = jax.vjp(_silu, c_ref[...])
        (o_ref[...],) = vjp(d_ref[...])

    return pl.pallas_call(body, name=name, out_shape=jax.ShapeDtypeStruct(c.shape, F32))(c, d)


def _adamw(w, g, m, v, *, name):
    rows, cols = w.shape
    t = _tile(rows)
    c1 = 1.0 / (1.0 - ADAM_B1 ** ADAM_STEP)
    c2 = 1.0 / (1.0 - ADAM_B2 ** ADAM_STEP)

    def body(w_ref, g_ref, m_ref, v_ref, go_ref, d_ref, mo_ref, vo_ref):
        gv = g_ref[...]
        mn = ADAM_B1 * m_ref[...] + (1.0 - ADAM_B1) * gv
        vn = ADAM_B2 * v_ref[...] + (1.0 - ADAM_B2) * (gv * gv)
        go_ref[...] = gv
        mo_ref[...] = mn
        vo_ref[...] = vn
        d_ref[...] = -ADAM_LR * ((mn * c1) / (jnp.sqrt(vn * c2) + ADAM_EPS) + ADAM_WD * w_ref[...])

    tile = pl.BlockSpec((t, cols), lambda i: (i, 0))
    return pl.pallas_call(
        body, name=name, grid=(rows // t,),
        in_specs=[tile] * 4, out_specs=[tile] * 4,
        out_shape=[jax.ShapeDtypeStruct((rows, cols), F32)] * 4,
        compiler_params=_params(("parallel",)),
    )(w, g, m, v)


def _add_sibling(full, got, where, out_dtype, *, name):
    s, rows, cols = full.shape
    hr = rows // 2
    t = _tile(hr)
    nt = hr // t

    def body(w_ref, a_ref, b_ref, o_ref):
        o_ref[...] = (a_ref[...].astype(F32) + b_ref[...].astype(F32)).astype(out_dtype)

    tile = pl.BlockSpec((1, t, cols), lambda j, i, w: (j, i, 0))
    return pl.pallas_call(
        body, name=name,
        grid_spec=pltpu.PrefetchScalarGridSpec(
            num_scalar_prefetch=1, grid=(s, nt),
            in_specs=[pl.BlockSpec((1, t, cols), lambda j, i, w: (j, w[0] * nt + i, 0)), tile], out_specs=tile),
        out_shape=jax.ShapeDtypeStruct((s, hr, cols), out_dtype), compiler_params=_params(("parallel", "parallel")),
    )(where, full, got)


def _sum_chips(own, slots, where, n_out, which, *, name):
    _, hr, cols = own.shape
    t = _tile(hr)

    def body(w_ref, a_ref, s_ref, o_ref):
        f = lambda v: v.astype(F32)
        o_ref[0] = ((f(a_ref[0]) + f(s_ref[0])) + f(s_ref[1])) + f(s_ref[2])

    return pl.pallas_call(
        body, name=name,
        grid_spec=pltpu.PrefetchScalarGridSpec(
            num_scalar_prefetch=1, grid=(hr // t,),
            in_specs=[pl.BlockSpec((1, t, cols), lambda i, w: (w[1], i, 0)),
                      pl.BlockSpec((N_CHIPS - 1, t, cols), lambda i, w: (0, i, 0))],
            out_specs=pl.BlockSpec((1, t, cols), lambda i, w: (w[which], i, 0))),
        out_shape=jax.ShapeDtypeStruct((n_out, hr, cols), F32), compiler_params=_params(("parallel",)),
    )(where, own, slots)


def _local_step(x, c, ctx, c_ctx, tgt, w_mod, b_mod, g_pre, g_post, w_qkv, w_ab, w_rest, w_conv, a_log, dt_bias,
                g_onorm, gm_ln_g, gm_ln_b, w_sp, b_sp, w_pa, w_pb, w_out):
    rows, rows_c = x.shape[0], ctx.shape[0]
    cc = jnp.zeros((16, D), F32).at[0].set(c[0]).at[1].set(c_ctx)
    mod = _mm(cc, w_mod, pre_silu=True, b_sections=True, name="mod_fwd")
    mod_x, mod_c = mod[0:1], mod[1:2]
    avec = jnp.zeros((1, DH), F32).at[0, :2 * H].set(a_log.reshape(-1))
    dvec = jnp.zeros((1, DH), F32).at[0, :2 * H].set(dt_bias.reshape(-1))
    bspb = jnp.broadcast_to(b_sp[:, :, None], (H, GM, GM))
    w_kv, wconv_kv = w_qkv[:, D:], w_conv[:, D:]

    h_c = _norm_fwd(ctx, g_pre, mod_c, b_mod, name="norm_fwd_ctx")
    pc_kv = _mm(h_c, w_kv, name="inproj_ctx_kv")
    pc_ab = _mm(h_c, w_ab, name="inproj_ctx_ab")
    kv_c = _prep_fwd(pc_kv, wconv_kv, H, name="prep_fwd_ctx")
    gb_c = _gates_fwd(pc_ab, avec, dvec, name="gates_fwd_ctx")
    s_zero = jnp.zeros((H, DH, DH), F32)
    _, _, ssf_c, ssb_c, tsf_c, tsb_c, s_f, s_b = _gdn_fwd(kv_c, gb_c, s_zero, s_zero, False, name="gdn_fwd_ctx")

    h_x = _norm_fwd(x, g_pre, mod_x, b_mod, name="norm_fwd_x")
    p_qkv = _mm(h_x, w_qkv, name="inproj_qkv")
    p_ab = _mm(h_x, w_ab, name="inproj_ab")
    p_rest = _mm(h_x, w_rest, name="inproj_rest")
    qkv = _prep_fwd(p_qkv, w_conv, 2 * H, name="prep_fwd_x")
    gb_x = _gates_fwd(p_ab, avec, dvec, name="gates_fwd_x")
    o_f, o_b, ssf, ssb, tsf, tsb, _, _ = _gdn_fwd(qkv, gb_x, s_f, s_b, True, name="gdn_fwd_x")

    (loss_acc, dp_rest, do, dx_res, ya, yb, mg, dma, dmb, dout, dvec_post, dgon, dwsp, dbspb) = _post(
        p_rest, o_f, o_b, x, tgt, mod_x, b_mod, g_onorm, gm_ln_g, gm_ln_b, w_sp, bspb, w_pa, w_pb, w_out, g_post,
        name="post")
    g = {}
    g["w_pa"] = _mm(ya, dma, ta=True, name="dw_pa")
    g["w_pb"] = _mm(yb, dmb, ta=True, name="dw_pb")
    g["w_out"] = _mm(mg, dout, ta=True, name="dw_out")

    zeros_s = jnp.zeros((H, DH, DH), F32)
    dq_f, dq_b, dg_f, dg_b, ds0_f, ds0_b = _gdn_bwd(qkv, gb_x, ssf, ssb, tsf, tsb, do, zeros_s, zeros_s, True,
                                                    name="gdn_bwd_x")
    dp_qkv, dwc_x = _prep_bwd(p_qkv, w_conv, dq_f, dq_b, 2 * H, name="prep_bwd_x")
    dp_ab, dav_x, ddv_x = _gates_bwd(p_ab, avec, dvec, dg_f, dg_b, name="gates_bwd_x")
    dkv_f, dkv_b, dgc_f, dgc_b, _, _ = _gdn_bwd(kv_c, gb_c, ssf_c, ssb_c, tsf_c, tsb_c, jnp.zeros((H, rows_c, DH), F32),
                                                 ds0_f, ds0_b, False, name="gdn_bwd_ctx")
    dpc_kv, dwc_c = _prep_bwd(pc_kv, wconv_kv, dkv_f, dkv_b, H, name="prep_bwd_ctx")
    dpc_ab, dav_c, ddv_c = _gates_bwd(pc_ab, avec, dvec, dgc_f, dgc_b, name="gates_bwd_ctx")

    dw_kv_c = _mm(h_c, dpc_kv, ta=True, tk=512, name="dw_kv_ctx")
    dw_qkv = _mm(h_x, dp_qkv, ta=True, tk=512, tn=D, add=dw_kv_c, add_from=1, name="dw_qkv")
    dw_ab_c = _mm(h_c, dpc_ab, ta=True, tk=512, name="dw_ab_ctx")
    dw_ab = _mm(h_x, dp_ab, ta=True, tk=512, add=dw_ab_c, name="dw_ab")
    dw_rest = _mm(h_x, dp_rest, ta=True, tk=512, name="dw_rest")
    g["w_in"] = (dw_qkv, dw_ab[:, :4 * H], dw_rest)
    dh_parts = [_mm(dp_qkv, w_qkv, tb=True, name="dh_qkv"), _mm(dp_rest, w_rest, tb=True, name="dh_rest"),
                _mm(dp_ab, w_ab, tb=True, name="dh_ab")]
    grad_x, dgpre_x, dm_x = _norm_bwd(x, g_pre, mod_x, b_mod, dh_parts, dx_res, name="norm_bwd_x")
    dhc_parts = [_mm(dpc_kv, w_kv, tb=True, name="dhc_kv"), _mm(dpc_ab, w_ab, tb=True, name="dhc_ab")]
    _, dgpre_c, dm_c = _norm_bwd(ctx, g_pre, mod_c, b_mod, dhc_parts, None, name="norm_bwd_ctx")

    dm_x = dm_x.at[:, 2 * D:].add(dvec_post[1:2])
    dmod = jnp.zeros((16, 3 * D), F32).at[0].set(dm_x[0]).at[1].set(dm_c[0])
    g["w_mod"] = _mm(cc, dmod, ta=True, pre_silu=True, tn=w_mod.shape[2], out_sections=True, name="dw_mod")
    dcc = _mm(dmod, w_mod, tb=True, b_sections=True, name="dcc")
    g["c_ctx"] = _dsilu_mul(cc[:8], dcc[:8], name="dc_ctx")[1]
    g["b_mod"] = dm_x + dm_c
    g["g_pre"] = dgpre_x + dgpre_c
    g["g_post"] = dvec_post[0:1]
    g["gm_ln_g"], g["gm_ln_b"] = dvec_post[2:3], dvec_post[3:4]
    g["g_onorm"] = dgon[0:1]
    g["w_sp"] = dwsp
    g["b_sp"] = jnp.sum(dbspb, axis=-1)
    g["w_conv"] = dwc_x.at[:, D:].add(dwc_c)
    g["a_log"] = (dav_x + dav_c)[0, :2 * H].reshape(2, H)
    g["dt_bias"] = (ddv_x + ddv_c)[0, :2 * H].reshape(2, H)
    return loss_acc[0, 0], grad_x, g


ANY = pl.BlockSpec(memory_space=pl.ANY)


def _place():
    x, y, c = lax.axis_index("x"), lax.axis_index("y"), lax.axis_index("c")
    chips = [(1 - x, y), (x, 1 - y), (1 - x, 1 - y)]
    return x, y, c, (x, y, 1 - c), chips


def _gather_shards(big, small, *, name):
    nb = len(big)

    def body(*refs):
        ins, sm_in = refs[:nb], refs[nb]
        outs, sm_out = refs[nb + 1:2 * nb + 1], refs[2 * nb + 1]
        send, recv = refs[2 * nb + 2:]
        x, y, c, sibling, chips = _place()
        mine = 2 * x + y

        def half(a, shard, hc):
            hr = big[a].shape[0] // 2
            return outs[a].at[shard, pl.ds(hc * hr, hr), :]

        def remote(k, src, dst, to):
            return pltpu.make_async_remote_copy(src_ref=src, dst_ref=dst, send_sem=send.at[k], recv_sem=recv.at[k],
                                                device_id=to, device_id_type=MESH)

        sends = []
        for a in range(nb):
            hr = big[a].shape[0] // 2
            for j, chip in enumerate(chips):
                sends.append(remote(a * 3 + j, ins[a].at[pl.ds(c * hr, hr), :], half(a, mine, c), (*chip, c)))
        for j, chip in enumerate(chips):
            sends.append(remote(nb * 3 + j, sm_in, sm_out.at[mine], (*chip, c)))
        for cp in sends:
            cp.start()
        base = nb * 3 + 3
        passed = []
        for a in range(nb):
            for j, (px, py) in enumerate(chips):
                theirs = 2 * px + py
                remote(a * 3 + j, half(a, theirs, c), half(a, theirs, c), sibling).wait_recv()
                fw = remote(base + a * 3 + j, half(a, theirs, c), half(a, theirs, c), sibling)
                fw.start()
                passed.append(fw)
        for a in range(nb):
            for j, (px, py) in enumerate(chips):
                theirs = 2 * px + py
                remote(base + a * 3 + j, half(a, theirs, 1 - c), half(a, theirs, 1 - c), sibling).wait_recv()
        for j, (px, py) in enumerate(chips):
            remote(nb * 3 + j, sm_in, sm_out.at[2 * px + py], sibling).wait_recv()
        for cp in sends + passed:
            cp.wait_send()

    n_remote = 2 * nb * 3 + 3
    outs = pl.pallas_call(
        body, name=name, in_specs=[ANY] * (nb + 1), out_specs=[ANY] * (nb + 1),
        out_shape=[jax.ShapeDtypeStruct((N_CHIPS,) + a.shape, a.dtype) for a in big + [small]],
        scratch_shapes=[pltpu.SemaphoreType.DMA((n_remote,)), pltpu.SemaphoreType.DMA((n_remote,))],
    )(*big, small)
    return outs[:nb], outs[nb]


def _row_chunks(rows, row_bytes, align=8):
    n = max(1, min(rows // align, -(-rows * row_bytes // DMA_CHUNK_BYTES)))
    per = -(-(-(-rows // n)) // align) * align
    return [(r, min(per, rows - r)) for r in range(0, rows, per)]


def _remote(src, dst, send, recv, to):
    return pltpu.make_async_remote_copy(src_ref=src, dst_ref=dst, send_sem=send, recv_sem=recv, device_id=to,
                                        device_id_type=MESH)


def _exchange_halves(arrs, *, name):
    na = len(arrs)

    def body(*refs):
        ins, got = refs[:na], refs[na:2 * na]
        send, recv = refs[2 * na:]
        x, y, c, sibling, _ = _place()
        for a in range(na):
            ns, rows, cols = arrs[a].shape
            hr = rows // 2
            for s in range(ns):
                for r0, nr in _row_chunks(hr, cols * arrs[a].dtype.itemsize, 16):
                    _remote(ins[a].at[s, pl.ds((1 - c) * hr + r0, nr), :], got[a].at[s, pl.ds(r0, nr), :],
                            send.at[a], recv.at[a], sibling).start()
        for a in range(na):
            hr = arrs[a].shape[1] // 2
            _remote(ins[a].at[:, pl.ds((1 - c) * hr, hr), :], got[a], send.at[a], recv.at[a], sibling).wait()

    return pl.pallas_call(
        body, name=name, in_specs=[ANY] * na, out_specs=[ANY] * na,
        out_shape=[jax.ShapeDtypeStruct((N_CHIPS, a.shape[1] // 2, a.shape[2]), a.dtype) for a in arrs],
        scratch_shapes=[pltpu.SemaphoreType.DMA((na,)), pltpu.SemaphoreType.DMA((na,))],
    )(*arrs)


def _scatter_sections(arrs, *, name):
    na = len(arrs)

    def body(*refs):
        ins, outs = refs[:na], refs[na:2 * na]
        send, recv = refs[2 * na:]
        x, y, c, _, chips = _place()
        for a in range(na):
            _, hr, cols = arrs[a].shape
            for j, (px, py) in enumerate(chips):
                for r0, nr in _row_chunks(hr, cols * arrs[a].dtype.itemsize, 16):
                    _remote(ins[a].at[2 * px + py, pl.ds(r0, nr), :], outs[a].at[j, pl.ds(r0, nr), :],
                            send.at[a * 3 + j], recv.at[a * 3 + j], (px, py, c)).start()
        for a in range(na):
            for j, (px, py) in enumerate(chips):
                _remote(ins[a].at[2 * px + py], outs[a].at[j], send.at[a * 3 + j], recv.at[a * 3 + j], (px, py, c)).wait()

    return pl.pallas_call(
        body, name=name, in_specs=[ANY] * na, out_specs=[ANY] * na,
        out_shape=[jax.ShapeDtypeStruct((N_CHIPS - 1,) + a.shape[1:], a.dtype) for a in arrs],
        scratch_shapes=[pltpu.SemaphoreType.DMA((3 * na,)), pltpu.SemaphoreType.DMA((3 * na,))],
    )(*arrs)


def _finish_reduce(big, small, *, name):
    nb = len(big)

    def body(*refs):
        outs, sm = refs[nb + 1:2 * nb + 1], refs[2 * nb + 1]
        send, recv = refs[2 * nb + 2:]
        x, y, c, sibling, chips = _place()
        blk = lambda px, py, pc: sm.at[4 * px + 2 * py + pc]
        for a in range(nb):
            _, hr, cols = big[a].shape
            for r0, nr in _row_chunks(hr, cols * 4):
                _remote(outs[a].at[c, pl.ds(r0, nr), :], outs[a].at[c, pl.ds(r0, nr), :], send.at[a], recv.at[a],
                        sibling).start()
        first = [_remote(blk(x, y, c), blk(x, y, c), send.at[nb], recv.at[nb], sibling)]
        first += [_remote(blk(x, y, c), blk(x, y, c), send.at[nb + 1 + j], recv.at[nb + 1 + j], (*chip, c))
                  for j, chip in enumerate(chips)]
        for cp in first:
            cp.start()
        passed = []
        for j, (px, py) in enumerate(chips):
            _remote(blk(px, py, c), blk(px, py, c), send.at[nb + 1 + j], recv.at[nb + 1 + j], sibling).wait_recv()
            fw = _remote(blk(px, py, c), blk(px, py, c), send.at[nb + 4 + j], recv.at[nb + 4 + j], sibling)
            fw.start()
            passed.append(fw)
        for a in range(nb):
            _remote(outs[a].at[c], outs[a].at[1 - c], send.at[a], recv.at[a], sibling).wait()
        _remote(blk(x, y, c), blk(x, y, 1 - c), send.at[nb], recv.at[nb], sibling).wait_recv()
        for j, (px, py) in enumerate(chips):
            _remote(blk(px, py, c), blk(px, py, 1 - c), send.at[nb + 4 + j], recv.at[nb + 4 + j], sibling).wait_recv()
        for cp in first + passed:
            cp.wait_send()

    n_remote = nb + 7
    arrs = list(big) + [small]
    outs = pl.pallas_call(
        body, name=name, in_specs=[ANY] * (nb + 1), out_specs=[ANY] * (nb + 1),
        out_shape=[jax.ShapeDtypeStruct(a.shape, F32) for a in arrs],
        input_output_aliases={i: i for i in range(nb + 1)},
        scratch_shapes=[pltpu.SemaphoreType.DMA((n_remote,)), pltpu.SemaphoreType.DMA((n_remote,))],
    )(*arrs)
    return outs[:nb], outs[nb]


def _reduce_gradients(sectioned, small, where):
    arrs = list(sectioned) + [small]
    got = _exchange_halves(arrs, name="rs_exchange_halves")
    pay = [BF16] * len(sectioned) + [F32]
    chip_sum = [_add_sibling(a, g, where, p, name=f"rs_add_sibling_{i}")
                for i, (a, g, p) in enumerate(zip(arrs, got, pay))]
    slots = _scatter_sections(chip_sum, name="rs_scatter_sections")
    red = [_sum_chips(p, s, where, 2, 0, name=f"rs_sum_chips_{i}")
           for i, (p, s) in enumerate(zip(chip_sum[:-1], slots[:-1]))]
    red_small = _sum_chips(chip_sum[-1], slots[-1], where, 2 * N_CHIPS, 2, name="rs_sum_chips_small")
    big, sm = _finish_reduce(red, red_small, name="rs_finish")
    return [b.reshape(-1, b.shape[-1]) for b in big], sm.reshape(-1, sm.shape[-1])


def kernel(x, c, ctx, c_ctx, w_mod, b_mod, g_pre, g_post, w_in, w_conv, a_log, dt_bias, g_onorm, gm_ln_g, gm_ln_b, w_sp, b_sp, w_pa, w_pb, w_out, loss_target, m_c_ctx, m_w_mod, m_b_mod, m_g_pre, m_g_post, m_w_in, m_w_conv, m_a_log, m_dt_bias, m_g_onorm, m_gm_ln_g, m_gm_ln_b, m_w_sp, m_b_sp, m_w_pa, m_w_pb, m_w_out, v_c_ctx, v_w_mod, v_b_mod, v_g_pre, v_g_post, v_w_in, v_w_conv, v_a_log, v_dt_bias, v_g_onorm, v_gm_ln_g, v_gm_ln_b, v_w_sp, v_b_sp, v_w_pa, v_w_pb, v_w_out):
    names = ["c_ctx", "w_mod", "b_mod", "g_pre", "g_post", "w_in", "w_conv", "a_log", "dt_bias", "g_onorm", "gm_ln_g",
             "gm_ln_b", "w_sp", "b_sp", "w_pa", "w_pb", "w_out"]
    w = dict(zip(names, (c_ctx, w_mod, b_mod, g_pre, g_post, w_in, w_conv, a_log, dt_bias, g_onorm, gm_ln_g, gm_ln_b,
                         w_sp, b_sp, w_pa, w_pb, w_out)))
    m = dict(zip(names, (m_c_ctx, m_w_mod, m_b_mod, m_g_pre, m_g_post, m_w_in, m_w_conv, m_a_log, m_dt_bias, m_g_onorm,
                         m_gm_ln_g, m_gm_ln_b, m_w_sp, m_b_sp, m_w_pa, m_w_pb, m_w_out)))
    v = dict(zip(names, (v_c_ctx, v_w_mod, v_b_mod, v_g_pre, v_g_post, v_w_in, v_w_conv, v_a_log, v_dt_bias, v_g_onorm,
                         v_gm_ln_g, v_gm_ln_b, v_w_sp, v_b_sp, v_w_pa, v_w_pb, v_w_out)))
    xy = 2 * lax.axis_index("x") + lax.axis_index("y")
    where = jnp.stack([lax.axis_index("c"), xy, 2 * xy + lax.axis_index("c")]).astype(jnp.int32)

    shards = [a[0].astype(BF16) for a in (w_mod, w_in, w_pa, w_pb, w_out)]
    gathered, wconv_all = _gather_shards(shards, w_conv[0], name="gather_weights")
    own = lambda full, shard: lax.dynamic_update_slice(full, shard[None], (xy, 0, 0))
    wm_all, win_all, wpa_all, wpb_all, wout_all = [own(f, s) for f, s in zip(gathered, shards)]
    wconv_all = own(wconv_all, w_conv[0])
    w_conv_f = jnp.concatenate([wconv_all[s] for s in range(N_CHIPS)], axis=1)
    shard_cols = IN_COLS // N_CHIPS
    cut = OFF_A - shard_cols
    assert 0 < cut and cut + 4 * H < shard_cols
    w_qkv = jnp.concatenate([win_all[0], win_all[1][:, :cut]], axis=1)
    w_ab = jnp.pad(win_all[1][:, cut:cut + 4 * H], ((0, 0), (0, DH - 4 * H)))
    w_rest = jnp.concatenate([win_all[1][:, cut + 4 * H:], win_all[2], win_all[3]], axis=1)

    loss_local, grad_x, g = _local_step(
        x[0], c, ctx[0], c_ctx, loss_target[0], wm_all, b_mod, g_pre, g_post, w_qkv, w_ab, w_rest,
        w_conv_f, a_log[0], dt_bias[0], g_onorm, gm_ln_g, gm_ln_b, w_sp[0], b_sp[0],
        wpa_all.reshape(D, D), wpb_all.reshape(D, D), wout_all.reshape(D, D))
    loss = lax.psum(loss_local, ("x", "y", "c"))

    blk = D // N_CHIPS
    dw_qkv, dw_ab, dw_rest = g["w_in"]
    rest_cut = shard_cols - cut - 4 * H
    g_win = jnp.stack([dw_qkv[:, :shard_cols],
                       jnp.concatenate([dw_qkv[:, shard_cols:], dw_ab, dw_rest[:, :rest_cut]], axis=1),
                       dw_rest[:, rest_cut:rest_cut + shard_cols], dw_rest[:, rest_cut + shard_cols:]]).astype(BF16)
    g["_pad"] = jnp.zeros((SMALL_LAYOUT["_pad"][1],), F32)
    tail = jnp.zeros((N_CHIPS * SMALL_ROWS * 128 - sum(s for _, s in SMALL_LAYOUT.values()),), F32)
    flat = jnp.concatenate([g[k].reshape(-1) for k in SMALL_LAYOUT] + [tail])
    big_names = ("w_mod", "w_in", "w_pa", "w_pb", "w_out")
    sectioned = [g["w_mod"], g_win] + [g[k].reshape(N_CHIPS, blk, D) for k in ("w_pa", "w_pb", "w_out")]
    reduced, gr_small = _reduce_gradients(sectioned, flat.reshape(N_CHIPS, SMALL_ROWS, 128), where)
    gr_flat = gr_small.reshape(-1)

    res = {k: _adamw(w[k][0], gr, m[k][0], v[k][0], name=f"adamw_{k}") for k, gr in zip(big_names, reduced)}
    small_names = [k for k in SMALL_LAYOUT if k not in ("_pad", "w_conv")]

    def pack(src):
        parts = [src[k].reshape(-1) if k in small_names else jnp.zeros((SMALL_LAYOUT[k][1],), F32) for k in SMALL_LAYOUT]
        return jnp.concatenate(parts + [tail]).reshape(-1, 128)

    sm_res = _adamw(pack(w), gr_small, pack(m), pack(v), name="adamw_small")
    for k in small_names:
        off, size = SMALL_LAYOUT[k]
        res[k] = [r.reshape(-1)[off:off + size].reshape(w[k].shape) for r in sm_res]
    off, size = SMALL_LAYOUT["w_conv"]
    g_conv = lax.dynamic_slice(gr_flat[off:off + size].reshape(3, 3 * D), (0, xy * (3 * D // N_CHIPS)), (3, 3 * D // N_CHIPS))
    conv_res = _adamw(jnp.pad(w_conv[0], ((0, 5), (0, 0))), jnp.pad(g_conv, ((0, 5), (0, 0))),
                      jnp.pad(m_w_conv[0], ((0, 5), (0, 0))), jnp.pad(v_w_conv[0], ((0, 5), (0, 0))), name="adamw_w_conv")
    res["w_conv"] = [r[:3] for r in conv_res]
    for k in ("w_mod", "w_in", "w_pa", "w_pb", "w_out", "w_conv"):
        res[k] = [r.reshape(w[k].shape) for r in res[k]]

    out = [loss, grad_x[None]]
    for i in range(4):
        out += [res[k][i] for k in names]
    return tuple(out)
```

```python
import functools

import jax
import jax.numpy as jnp
from jax import lax
from jax.experimental import pallas as pl
from jax.experimental.pallas import tpu as pltpu

F32 = jnp.float32
BF16 = jnp.bfloat16
HI = lax.Precision.HIGHEST
MESH = pl.DeviceIdType.MESH

D = 1024
H = 8
DH = 128
CH = 64
LOG_CH = 6
PAIR = 2 * CH
GM = 128
assert 1 << LOG_CH == CH and PAIR == DH
PREC_SOLVE = lax.Precision.HIGH
PREC_POWERS = (PREC_SOLVE,) * (LOG_CH - 1)
HEADS_PER_ITER_FWD = 8
HEADS_PER_ITER_BWD = 8
EPS = 1e-6
N_CHIPS = 4
OFF_A = 3 * D
OFF_ZB = OFF_A + 4 * H
IN_COLS = OFF_ZB + 6 * D
VMEM_LIMIT_V7X = 56 * 1024 * 1024
DMA_CHUNK_BYTES = 2 * 1024 * 1024

ADAM_LR, ADAM_B1, ADAM_B2, ADAM_EPS, ADAM_WD, ADAM_STEP = 0.001, 0.9, 0.999, 1e-08, 0.01, 10

SMALL_LAYOUT = {}
_off = 0
for _n, _s in (("w_sp", H * GM * GM), ("w_conv", 3 * 3 * D), ("c_ctx", D), ("b_mod", 3 * D), ("g_pre", D), ("g_post", D),
               ("gm_ln_g", D), ("gm_ln_b", D), ("b_sp", H * GM), ("g_onorm", DH), ("a_log", 2 * H), ("dt_bias", 2 * H),
               ("loss", 1), ("_pad", 128 - 4 * H - 1)):
    SMALL_LAYOUT[_n] = (_off, _s)
    _off += _s
SMALL_ROWS = 304
assert N_CHIPS * SMALL_ROWS * 128 >= _off and (SMALL_ROWS // 2) % 8 == 0 and _off % 128 == 0
TINY_ROW0 = SMALL_LAYOUT["w_conv"][0] // 128


def _params(sem=None):
    return pltpu.CompilerParams(dimension_semantics=sem, vmem_limit_bytes=VMEM_LIMIT_V7X)


def _tile(n, cands=(256, 128, 64, 32, 16, 8)):
    if n <= cands[0]:
        return n
    for cand in cands:
        if n % cand == 0:
            return cand
    return n


def _silu(x):
    return x * jax.nn.sigmoid(x)


def _gelu(x):
    return 0.5 * x * (1.0 + jnp.tanh(0.7978845608028654 * (x + 0.044715 * (x * x * x))))


def _mm(a, b, *, ta=False, tb=False, out_dtype=F32, tm=1024, tn=1024, tk=1024, pre_silu=False, add=None, add_from=0,
        b_sections=False, out_sections=False, name):
    m, k = (a.shape[1], a.shape[0]) if ta else a.shape
    if b_sections:
        sect = b.shape[2]
        n = b.shape[1] if tb else b.shape[0] * sect
        tn, tk = (tn, sect) if tb else (sect, tk)
    else:
        n = b.shape[0] if tb else b.shape[1]
    tm, tn, tk = min(tm, m), min(tn, n), min(tk, k)
    assert m % tm == 0 and n % tn == 0 and k % tk == 0, (name, m, n, k, tm, tn, tk)
    nk = k // tk
    dims = (((0,) if ta else (1,), (1,) if tb else (0,)), ((), ()))
    has_add = add is not None
    assert not has_add or add.shape == (m, n - add_from * tn), (name, add.shape)

    def body(*refs):
        a_ref, b_ref = refs[:2]
        o_ref = refs[2 + int(has_add)]
        acc_ref = refs[-1]
        kk = pl.program_id(2)
        av = a_ref[...]
        if pre_silu:
            av = _silu(av.astype(F32))
        bv = b_ref[0] if b_sections else b_ref[...]
        part = lax.dot_general(av.astype(BF16), bv.astype(BF16), dims, preferred_element_type=F32)

        def finish(res):
            if has_add:
                res = res + jnp.where(pl.program_id(1) >= add_from, refs[2][...], 0.0)
            if out_sections:
                o_ref[0] = res.astype(out_dtype)
            else:
                o_ref[...] = res.astype(out_dtype)

        if nk == 1:
            finish(part)
            return

        @pl.when(kk == 0)
        def _():
            acc_ref[...] = part

        @pl.when((kk > 0) & (kk < nk - 1))
        def _():
            acc_ref[...] += part

        @pl.when(kk == nk - 1)
        def _():
            finish(acc_ref[...] + part)

    a_spec = pl.BlockSpec((tk, tm), lambda i, j, q: (q, i)) if ta else pl.BlockSpec((tm, tk), lambda i, j, q: (i, q))
    if b_sections:
        b_spec = (pl.BlockSpec((1, tn, tk), lambda i, j, q: (q, j, 0)) if tb
                  else pl.BlockSpec((1, tk, tn), lambda i, j, q: (j, q, 0)))
    else:
        b_spec = pl.BlockSpec((tn, tk), lambda i, j, q: (j, q)) if tb else pl.BlockSpec((tk, tn), lambda i, j, q: (q, j))
    add_spec = [pl.BlockSpec((tm, tn), lambda i, j, q: (i, jnp.maximum(j - add_from, 0)))] if has_add else []
    if out_sections:
        out_spec, out_shape = pl.BlockSpec((1, tm, tn), lambda i, j, q: (j, i, 0)), (n // tn, m, tn)
    else:
        out_spec, out_shape = pl.BlockSpec((tm, tn), lambda i, j, q: (i, j)), (m, n)
    return pl.pallas_call(
        body, name=name, grid=(m // tm, n // tn, nk),
        in_specs=[a_spec, b_spec] + add_spec, out_specs=out_spec,
        out_shape=jax.ShapeDtypeStruct(out_shape, out_dtype),
        scratch_shapes=[pltpu.VMEM((tm, tn), F32)] if nk > 1 else [],
        compiler_params=_params(("parallel", "parallel", "arbitrary")),
    )(*([a, b] + ([add] if has_add else [])))


def _h_fn(x, g, m):
    shift, scale = m[:, 0:D], m[:, D:2 * D]
    r = lax.rsqrt(jnp.mean(x * x, axis=-1, keepdims=True) + EPS)
    return (x * r * g) * (1.0 + scale) + shift


def _norm_fwd(x, g, mod, bmod, *, name):
    rows = x.shape[0]
    t = min(512, rows)

    def body(x_ref, g_ref, m_ref, b_ref, h_ref):
        h_ref[...] = _h_fn(x_ref[...], g_ref[...], m_ref[...] + b_ref[...]).astype(BF16)

    vec = lambda w: pl.BlockSpec((1, w), lambda i: (0, 0))
    return pl.pallas_call(
        body, name=name, grid=(rows // t,),
        in_specs=[pl.BlockSpec((t, D), lambda i: (i, 0)), vec(D), vec(3 * D), vec(3 * D)],
        out_specs=pl.BlockSpec((t, D), lambda i: (i, 0)),
        out_shape=jax.ShapeDtypeStruct((rows, D), BF16),
        compiler_params=_params(("parallel",)),
    )(x, g, mod, bmod)


def _norm_bwd(x, g, mod, bmod, dh_parts, resid, *, name):
    rows = x.shape[0]
    t = min(512, rows)
    n_parts = len(dh_parts)
    has_resid = resid is not None

    def body(*refs):
        x_ref, g_ref, m_ref, b_ref = refs[:4]
        parts = refs[4:4 + n_parts]
        r_ref = refs[4 + n_parts] if has_resid else None
        dx_ref, dg_ref, dm_ref = refs[-3:]
        i = pl.program_id(0)
        dh = parts[0][...]
        for p in parts[1:]:
            dh = dh + p[...]
        _, vjp = jax.vjp(_h_fn, x_ref[...], g_ref[...], m_ref[...] + b_ref[...])
        dx, dg, dm = vjp(dh)
        if has_resid:
            dx = dx + r_ref[...]
        dx_ref[...] = dx

        @pl.when(i == 0)
        def _():
            dg_ref[...] = dg
            dm_ref[...] = dm

        @pl.when(i > 0)
        def _():
            dg_ref[...] += dg
            dm_ref[...] += dm

    vec = lambda w: pl.BlockSpec((1, w), lambda i: (0, 0))
    tile = pl.BlockSpec((t, D), lambda i: (i, 0))
    ins = [x, g, mod, bmod, *dh_parts] + ([resid] if has_resid else [])
    return pl.pallas_call(
        body, name=name, grid=(rows // t,),
        in_specs=[tile, vec(D), vec(3 * D), vec(3 * D)] + [tile] * (n_parts + int(has_resid)),
        out_specs=[tile, vec(D), vec(3 * D)],
        out_shape=[jax.ShapeDtypeStruct((rows, D), F32), jax.ShapeDtypeStruct((1, D), F32),
                   jax.ShapeDtypeStruct((1, 3 * D), F32)],
        compiler_params=_params(("arbitrary",)),
    )(*ins)


def _conv_tile(u_ref, r0, t, rows, w0, w1, w2):
    u = u_ref[pl.ds(r0, t), :]
    prev8 = u_ref[pl.ds(pl.multiple_of(jnp.maximum(r0 - 8, 0), 8), 8), :]
    next8 = u_ref[pl.ds(pl.multiple_of(jnp.minimum(r0 + t, rows - 8), 8), 8), :]
    r8 = lax.broadcasted_iota(jnp.int32, (8, DH), 0)
    prev_row = jnp.sum(jnp.where(r8 == 7, prev8, 0.0), axis=0, keepdims=True)
    next_row = jnp.sum(jnp.where(r8 == 0, next8, 0.0), axis=0, keepdims=True)
    prev_row = jnp.where(r0 > 0, prev_row, 0.0)
    next_row = jnp.where(r0 + t < rows, next_row, 0.0)
    ri = lax.broadcasted_iota(jnp.int32, (t, DH), 0)
    um1 = jnp.where(ri == 0, prev_row, pltpu.roll(u, 1, 0))
    up1 = jnp.where(ri == t - 1, next_row, pltpu.roll(u, t - 1, 0))
    return w0 * um1 + w1 * u + w2 * up1, um1, u, up1


def _rowlocal(z, is_norm):
    y = _silu(z)
    yn = y * lax.rsqrt(jnp.sum(y * y, axis=-1, keepdims=True) + EPS)
    return jnp.where(is_norm, yn, y)


def _prep_fwd(p, wconv, n_norm, *, name):
    rows, nb = p.shape[0], p.shape[1] // DH
    t = min(512, rows)

    def body(u_ref, w_ref, o_ref):
        is_norm = pl.program_id(0) < n_norm
        w0, w1, w2 = w_ref[0:1, :], w_ref[1:2, :], w_ref[2:3, :]

        def step(s, carry):
            r0 = pl.multiple_of(s * t, t)
            z, _, _, _ = _conv_tile(u_ref, r0, t, rows, w0, w1, w2)
            o_ref[0, pl.ds(r0, t), :] = _rowlocal(z, is_norm)
            return carry

        lax.fori_loop(0, rows // t, step, 0)

    return pl.pallas_call(
        body, name=name, grid=(nb,),
        in_specs=[pl.BlockSpec((rows, DH), lambda j: (0, j)), pl.BlockSpec((3, DH), lambda j: (0, j))],
        out_specs=pl.BlockSpec((1, rows, DH), lambda j: (j, 0, 0)),
        out_shape=jax.ShapeDtypeStruct((nb, rows, DH), F32),
        compiler_params=_params(("parallel",)),
    )(p, wconv)


def _prep_bwd(p, wconv, d_a, d_b, n_norm, *, name):
    rows, nb = p.shape[0], p.shape[1] // DH
    t = min(512, rows)

    def body(u_ref, w_ref, da_ref, db_ref, du_ref, dw_ref, dz_ref):
        is_norm = pl.program_id(0) < n_norm
        w0, w1, w2 = w_ref[0:1, :], w_ref[1:2, :], w_ref[2:3, :]

        def step1(s, carry):
            a0, a1, a2 = carry
            r0 = pl.multiple_of(s * t, t)
            z, um1, u, up1 = _conv_tile(u_ref, r0, t, rows, w0, w1, w2)
            _, vjp = jax.vjp(lambda zz: _rowlocal(zz, is_norm), z)
            (dz,) = vjp(da_ref[0, pl.ds(r0, t), :] + db_ref[0, pl.ds(r0, t), :])
            dz_ref[pl.ds(r0, t), :] = dz
            red = lambda v: jnp.sum(v, axis=0, keepdims=True)
            return a0 + red(dz * um1), a1 + red(dz * u), a2 + red(dz * up1)

        zero = jnp.zeros((1, DH), F32)
        a0, a1, a2 = lax.fori_loop(0, rows // t, step1, (zero, zero, zero))
        dw_ref[0:1, :] = a0
        dw_ref[1:2, :] = a1
        dw_ref[2:3, :] = a2

        def step2(s, carry):
            r0 = pl.multiple_of(s * t, t)
            du, _, _, _ = _conv_tile(dz_ref, r0, t, rows, w2, w1, w0)
            du_ref[pl.ds(r0, t), :] = du.astype(BF16)
            return carry

        lax.fori_loop(0, rows // t, step2, 0)

    col = pl.BlockSpec((rows, DH), lambda j: (0, j))
    w_spec = pl.BlockSpec((3, DH), lambda j: (0, j))
    d_spec = pl.BlockSpec((1, rows, DH), lambda j: (j, 0, 0))
    return pl.pallas_call(
        body, name=name, grid=(nb,),
        in_specs=[col, w_spec, d_spec, d_spec], out_specs=[col, w_spec],
        out_shape=[jax.ShapeDtypeStruct((rows, nb * DH), BF16), jax.ShapeDtypeStruct((3, nb * DH), F32)],
        scratch_shapes=[pltpu.VMEM((rows, DH), F32)],
        compiler_params=_params(("parallel",)),
    )(p, wconv, d_a, d_b)


def _gates_fn(pab, avec, dvec):
    t = pab.shape[0]
    lane = lax.broadcasted_iota(jnp.int32, pab.shape, 1)
    xg = pab + dvec
    sp = jnp.maximum(xg, 0.0) + jnp.log(1.0 + jnp.exp(-jnp.abs(xg)))
    g = jnp.where(lane < 2 * H, -jnp.exp(avec) * sp, 0.0)
    ii = lax.broadcasted_iota(jnp.int32, (t, t), 0)
    jj = lax.broadcasted_iota(jnp.int32, (t, t), 1)
    same = (ii >> LOG_CH) == (jj >> LOG_CH)
    cum_f = _dot(jnp.where(same & (jj <= ii), 1.0, 0.0), g, precision=HI)
    cum_r = _dot(jnp.where(same & (jj >= ii), 1.0, 0.0), g, precision=HI)
    return jnp.where(lane < H, cum_f, jnp.where(lane < 2 * H, cum_r, jnp.where(lane < 4 * H, jax.nn.sigmoid(pab), 0.0)))


def _gates_fwd(pab, avec, dvec, *, name):
    rows = pab.shape[0]
    t = min(512, rows)

    def body(p_ref, a_ref, d_ref, o_ref):
        o_ref[...] = _gates_fn(p_ref[...], a_ref[...], d_ref[...])

    vec = pl.BlockSpec((1, DH), lambda i: (0, 0))
    tile = pl.BlockSpec((t, DH), lambda i: (i, 0))
    return pl.pallas_call(
        body, name=name, grid=(rows // t,), in_specs=[tile, vec, vec], out_specs=tile,
        out_shape=jax.ShapeDtypeStruct((rows, DH), F32), compiler_params=_params(("parallel",)),
    )(pab, avec, dvec)


def _gates_bwd(pab, avec, dvec, d_a, d_b, *, name):
    rows = pab.shape[0]
    t = min(512, rows)

    def body(p_ref, a_ref, d_ref, da_ref, db_ref, dp_ref, dav_ref, ddv_ref):
        i = pl.program_id(0)
        _, vjp = jax.vjp(_gates_fn, p_ref[...], a_ref[...], d_ref[...])
        dp, dav, ddv = vjp(da_ref[...] + db_ref[...])
        dp_ref[...] = dp.astype(BF16)

        @pl.when(i == 0)
        def _():
            dav_ref[...] = dav
            ddv_ref[...] = ddv

        @pl.when(i > 0)
        def _():
            dav_ref[...] += dav
            ddv_ref[...] += ddv

    vec = pl.BlockSpec((1, DH), lambda i: (0, 0))
    tile = pl.BlockSpec((t, DH), lambda i: (i, 0))
    return pl.pallas_call(
        body, name=name, grid=(rows // t,), in_specs=[tile, vec, vec, tile, tile], out_specs=[tile, vec, vec],
        out_shape=[jax.ShapeDtypeStruct((rows, DH), BF16), jax.ShapeDtypeStruct((1, DH), F32),
                   jax.ShapeDtypeStruct((1, DH), F32)],
        compiler_params=_params(("arbitrary",)),
    )(pab, avec, dvec, d_a, d_b)


def _dot(a, b, dims=((1,), (0,)), precision=None):
    return lax.dot_general(a, b, (dims, ((), ())), precision=precision, preferred_element_type=F32)


_NT = ((1,), (1,))
_TN = ((0,), (0,))


@jax.custom_vjp
def _saved_inverse(neg_a, inv):
    return inv


def _saved_inverse_fwd(neg_a, inv):
    return inv, inv


def _saved_inverse_bwd(inv, d_inv):
    idx = range(len(inv))
    left = [_dot(inv[i], d_inv[i], _TN, precision=PREC_SOLVE) for i in idx]
    d_neg_a = [_dot(left[i], inv[i], _NT, precision=PREC_SOLVE) for i in idx]
    return d_neg_a, [jnp.zeros_like(t) for t in inv]


_saved_inverse.defvjp(_saved_inverse_fwd, _saved_inverse_bwd)


def _pairs(s, q, k, v, gcol, bcol, revs, inv_saved=None):
    idx = range(len(revs))
    ii = lax.broadcasted_iota(jnp.int32, (PAIR, PAIR), 0)
    jj = lax.broadcasted_iota(jnp.int32, (PAIR, PAIR), 1)
    same = (ii >> LOG_CH) == (jj >> LOG_CH)
    incl_d = (same & (ii >= jj), same & (ii <= jj))
    strict_d = (same & (ii > jj), same & (ii < jj))
    incl = [incl_d[int(r)] for r in revs]
    strict = [strict_d[int(r)] for r in revs]
    eye = jnp.where(ii == jj, 1.0, 0.0)
    gc_i = [jnp.broadcast_to(gcol[i], (PAIR, DH)) for i in idx]
    gc_j = [gc_i[i].T for i in idx]
    decay = [jnp.where(incl[i], jnp.exp(jnp.where(incl[i], gc_i[i] - gc_j[i], 0.0)), 0.0) for i in idx]
    b_b = [jnp.broadcast_to(bcol[i], (PAIR, DH)) for i in idx]
    kb = [k[i] * b_b[i] for i in idx]
    kk = [_dot(kb[i], k[i], _NT) for i in idx]
    bp = [jnp.where(strict[i], -kk[i] * decay[i], 0.0) for i in idx]
    if inv_saved is not None:
        inv = _saved_inverse(bp, inv_saved)
    else:
        inv = [eye + bp[i] for i in idx]
        for prec in PREC_POWERS:
            bp = [_dot(bp[i], bp[i], precision=prec) for i in idx]
            more = [_dot(inv[i], bp[i], precision=prec) for i in idx]
            inv = [inv[i] + more[i] for i in idx]
    eg = [jnp.exp(gc_i[i]) for i in idx]
    sol = [_dot(inv[i], jnp.concatenate([v[i] * b_b[i], kb[i] * eg[i]], axis=1), precision=PREC_SOLVE) for i in idx]
    u_val = [sol[i][:, :DH] for i in idx]
    w_key = [sol[i][:, DH:] for i in idx]
    row = lax.broadcasted_iota(jnp.int32, (PAIR, 1), 0)
    has_q = q[0] is not None
    if has_q:
        qc = [q[i] * (DH ** -0.5) for i in idx]
        qk = [_dot(qc[i], k[i], _NT) for i in idx]
        attn = [qk[i] * decay[i] for i in idx]
        qd = [qc[i] * eg[i] for i in idx]
    outs = [[None, None] for _ in idx]
    zeros = jnp.zeros((CH, DH), F32)
    for step in range(2):
        cidx = [(1 - step) if revs[i] else step for i in idx]
        sl = [slice(c * CH, (c + 1) * CH) for c in cidx]
        last = [c * CH if revs[i] else c * CH + CH - 1 for i, c in zip(idx, cidx)]
        gl = [jnp.sum(jnp.where(row == last[i], gcol[i], 0.0), axis=0, keepdims=True) for i in idx]
        k_tail = [k[i][sl[i]] * jnp.exp(gl[i] - gc_i[i][sl[i]]) for i in idx]
        ws = [_dot(w_key[i][sl[i]], s[i]) for i in idx]
        v_new = [u_val[i][sl[i]] - ws[i] for i in idx]
        if has_q:
            v_pad = [jnp.concatenate([v_new[i], zeros] if cidx[i] == 0 else [zeros, v_new[i]], axis=0) for i in idx]
            o_state = [_dot(qd[i][sl[i]], s[i]) for i in idx]
            o_local = [_dot(attn[i][sl[i]], v_pad[i]) for i in idx]
            for i in idx:
                outs[i][cidx[i]] = o_state[i] + o_local[i]
        kv = [_dot(k_tail[i], v_new[i], _TN) for i in idx]
        s = [s[i] * jnp.exp(gl[i]) + kv[i] for i in idx]
    return s, ([jnp.concatenate(outs[i], axis=0) for i in idx] if has_q else None), inv


def _lane_col(tile, idx):
    lane = lax.broadcasted_iota(jnp.int32, tile.shape, 1)
    return jnp.sum(jnp.where(lane == idx, tile, 0.0), axis=1, keepdims=True)


def _gdn_fwd(qkv, gb, s0f, s0b, has_q, *, name):
    nb, rows, _ = qkv.shape
    n = rows // PAIR
    qoff = H if has_q else 0

    def body(qf_ref, qb_ref, gf_ref, gr_ref, s0f_ref, s0b_ref, of_ref, ob_ref, ssf_ref, ssb_ref, tsf_ref, tsb_ref,
             sf_ref, sb_ref):
        @pl.when(pl.program_id(0) == 0)
        def _():
            sf_ref[...] = s0f_ref[...]
            sb_ref[...] = s0b_ref[...]

        gtiles = (gf_ref[...], gr_ref[...])

        dirs = ((qf_ref, sf_ref, ssf_ref, of_ref), (qb_ref, sb_ref, ssb_ref, ob_ref))
        ts_refs = (tsf_ref, tsb_ref)

        def heads(hg, carry):
            work = [(hg * HEADS_PER_ITER_FWD + j, d) for j in range(HEADS_PER_ITER_FWD) for d in range(2)]
            loaded = []
            for h, d in work:
                q_ref, s_ref, _, _ = dirs[d]
                loaded.append((s_ref[h], q_ref[h] if has_q else None, q_ref[qoff + h], q_ref[qoff + H + h],
                               _lane_col(gtiles[d], d * H + h), _lane_col(gtiles[d], 2 * H + d * H + h)))
            s_new, o, inv = _pairs(*[list(col) for col in zip(*loaded)], revs=[d == 1 for _, d in work])
            for i, (h, d) in enumerate(work):
                _, s_ref, ss_ref, o_ref = dirs[d]
                ss_ref[0, h] = loaded[i][0]
                ts_refs[d][0, h] = inv[i]
                s_ref[h] = s_new[i]
                o_ref[h] = o[i] if has_q else jnp.zeros((PAIR, DH), F32)
            return carry

        if HEADS_PER_ITER_FWD == H:
            heads(0, 0)
        else:
            lax.fori_loop(0, H // HEADS_PER_ITER_FWD, heads, 0)

    fwd3 = lambda i: (0, i, 0)
    rev3 = lambda i: (0, n - 1 - i, 0)
    state = pl.BlockSpec((H, DH, DH), lambda i: (0, 0, 0))
    saved = pl.BlockSpec((1, H, DH, DH), lambda i: (i, 0, 0, 0))
    return pl.pallas_call(
        body, name=name, grid=(n,),
        in_specs=[pl.BlockSpec((nb, PAIR, DH), fwd3), pl.BlockSpec((nb, PAIR, DH), rev3),
                  pl.BlockSpec((PAIR, DH), lambda i: (i, 0)), pl.BlockSpec((PAIR, DH), lambda i: (n - 1 - i, 0)),
                  state, state],
        out_specs=[pl.BlockSpec((H, PAIR, DH), fwd3), pl.BlockSpec((H, PAIR, DH), rev3), saved, saved, saved, saved,
                   state, state],
        out_shape=[jax.ShapeDtypeStruct((H, rows, DH), F32)] * 2 + [jax.ShapeDtypeStruct((n, H, DH, DH), F32)] * 4
        + [jax.ShapeDtypeStruct((H, DH, DH), F32)] * 2,
        compiler_params=_params(("arbitrary",)),
    )(qkv, qkv, gb, gb, s0f, s0b)


def _gdn_bwd(qkv, gb, ssf, ssb, tsf, tsb, do, dsf, dsb, has_q, *, name):
    nb, rows, _ = qkv.shape
    n = rows // PAIR
    qoff = H if has_q else 0

    def body(qf_ref, qb_ref, gf_ref, gr_ref, ssf_ref, ssb_ref, tsf_ref, tsb_ref, dof_ref, dob_ref, dsf0_ref, dsb0_ref,
             dqf_ref, dqb_ref, dgf_ref, dgr_ref, dsf_ref, dsb_ref):
        ts_refs = (tsf_ref, tsb_ref)
        @pl.when(pl.program_id(0) == 0)
        def _():
            dsf_ref[...] = dsf0_ref[...]
            dsb_ref[...] = dsb0_ref[...]

        gtiles = (gf_ref[...], gr_ref[...])
        lane = lax.broadcasted_iota(jnp.int32, (PAIR, DH), 1)

        dirs = ((qf_ref, ssf_ref, dof_ref, dsf_ref, dqf_ref), (qb_ref, ssb_ref, dob_ref, dsb_ref, dqb_ref))

        def heads(hg, carry):
            out = list(carry)
            work = [(hg * HEADS_PER_ITER_BWD + j, d) for j in range(HEADS_PER_ITER_BWD) for d in range(2)]
            revs = [d == 1 for _, d in work]
            s_in, q_in, k_in, v_in, g_in, b_in, ds_out, do_out, inv_in = [], [], [], [], [], [], [], [], []
            for h, d in work:
                q_ref, ss_ref, do_ref, ds_ref, _ = dirs[d]
                s_in.append(ss_ref[0, h])
                inv_in.append(ts_refs[d][0, h])
                q_in.append(q_ref[h] if has_q else None)
                k_in.append(q_ref[qoff + h])
                v_in.append(q_ref[qoff + H + h])
                g_in.append(_lane_col(gtiles[d], d * H + h))
                b_in.append(_lane_col(gtiles[d], 2 * H + d * H + h))
                ds_out.append(ds_ref[h])
                do_out.append(do_ref[h] if has_q else None)
            if has_q:
                _, vjp = jax.vjp(lambda s_, q_, k_, v_, g_, b_: _pairs(s_, q_, k_, v_, g_, b_, revs, inv_in)[:2],
                                 s_in, q_in, k_in, v_in, g_in, b_in)
                ds, dq, dk, dv, dg, db = vjp((ds_out, do_out))
            else:
                _, vjp = jax.vjp(lambda s_, k_, v_, g_, b_: _pairs(s_, q_in, k_, v_, g_, b_, revs, inv_in)[0],
                                 s_in, k_in, v_in, g_in, b_in)
                ds, dk, dv, dg, db = vjp(ds_out)
            for i, (h, d) in enumerate(work):
                dq_ref, ds_ref = dirs[d][4], dirs[d][3]
                ds_ref[h] = ds[i]
                if has_q:
                    dq_ref[h] = dq[i]
                dq_ref[qoff + h], dq_ref[qoff + H + h] = dk[i], dv[i]
                out[d] = (out[d] + jnp.where(lane == d * H + h, dg[i], 0.0)
                          + jnp.where(lane == 2 * H + d * H + h, db[i], 0.0))
            return tuple(out)

        zero = jnp.zeros((PAIR, DH), F32)
        if HEADS_PER_ITER_BWD == H:
            dgf, dgr = heads(0, (zero, zero))
        else:
            dgf, dgr = lax.fori_loop(0, H // HEADS_PER_ITER_BWD, heads, (zero, zero))
        dgf_ref[...] = dgf
        dgr_ref[...] = dgr

    fwd3 = lambda i: (0, n - 1 - i, 0)
    rev3 = lambda i: (0, i, 0)
    state = pl.BlockSpec((H, DH, DH), lambda i: (0, 0, 0))
    saved = pl.BlockSpec((1, H, DH, DH), lambda i: (n - 1 - i, 0, 0, 0))
    gf_spec = pl.BlockSpec((PAIR, DH), lambda i: (n - 1 - i, 0))
    gr_spec = pl.BlockSpec((PAIR, DH), lambda i: (i, 0))
    return pl.pallas_call(
        body, name=name, grid=(n,),
        in_specs=[pl.BlockSpec((nb, PAIR, DH), fwd3), pl.BlockSpec((nb, PAIR, DH), rev3), gf_spec, gr_spec,
                  saved, saved, saved, saved,
                  pl.BlockSpec((H, PAIR, DH), fwd3), pl.BlockSpec((H, PAIR, DH), rev3), state, state],
        out_specs=[pl.BlockSpec((nb, PAIR, DH), fwd3), pl.BlockSpec((nb, PAIR, DH), rev3), gf_spec, gr_spec, state, state],
        out_shape=[jax.ShapeDtypeStruct((nb, rows, DH), F32)] * 2 + [jax.ShapeDtypeStruct((rows, DH), F32)] * 2
        + [jax.ShapeDtypeStruct((H, DH, DH), F32)] * 2,
        compiler_params=_params(("arbitrary",)),
    )(qkv, qkv, gb, gb, ssf, ssb, tsf, tsb, do, do, dsf, dsb)


def _stage1(zb, ua, va, za, o, gon, lng, lnb, wsp, bsp):
    gv = [_gelu(t) for t in va]
    mu = sum(jnp.sum(t, axis=-1, keepdims=True) for t in gv) * (1.0 / D)
    xc = [t - mu for t in gv]
    var = sum(jnp.sum(t * t, axis=-1, keepdims=True) for t in xc) * (1.0 / D)
    rs = lax.rsqrt(var + EPS)
    ya, yb = [], []
    for g in range(H):
        vv = xc[g] * rs * lng[g] + lnb[g]
        s = _dot(wsp[g], vv) + bsp[g]
        ya.append(_gelu(ua[g]) * s * _silu(za[g]))
        r = lax.rsqrt(jnp.mean(o[g] * o[g], axis=-1, keepdims=True) + EPS)
        yb.append(o[g] * r * gon * _silu(zb[g]))
    return ya, yb


def _stage2(ma, mb, ga, gb):
    return jax.nn.sigmoid(ga) * ma + jax.nn.sigmoid(gb) * mb


def _stage3(out, x, tgt, gpost, gate):
    r = out * lax.rsqrt(jnp.mean(out * out, axis=-1, keepdims=True) + EPS) * gpost
    err = x + gate * r - tgt
    return 0.5 * jnp.sum(jnp.mean(err * err, axis=-1, keepdims=True), axis=0, keepdims=True)


def _post(p_rest, o_f, o_b, x, tgt, mod, bmod, gon, lng, lnb, wsp, bspb, wpa, wpb, wout, gpost, *, name):
    rows = x.shape[0]
    n = rows // GM
    lanes = lambda g: slice(g * DH, (g + 1) * DH)
    bdot = lambda a, w_ref: _dot(a.astype(BF16), w_ref[...])
    bdot_t = lambda a, w_ref: _dot(a.astype(BF16), w_ref[...], _NT)

    def body(p_ref, of_ref, ob_ref, x_ref, t_ref, m_ref, bm_ref, gon_ref, lng_ref, lnb_ref, wsp_ref, bsp_ref,
             wpa_ref, wpb_ref, wout_ref, gp_ref,
             loss_ref, dp_ref, do_ref, dx_ref, ya_ref, yb_ref, mg_ref, dma_ref, dmb_ref, dout_ref,
             dvec_ref, dgon_ref, dwsp_ref, dbsp_ref):
        @pl.when(pl.program_id(0) == 0)
        def _():
            loss_ref[...] = jnp.zeros_like(loss_ref)
            dvec_ref[...] = jnp.zeros_like(dvec_ref)
            dgon_ref[...] = jnp.zeros_like(dgon_ref)
            dwsp_ref[...] = jnp.zeros_like(dwsp_ref)
            dbsp_ref[...] = jnp.zeros_like(dbsp_ref)

        piece = lambda blk: [p_ref[:, blk * D + g * DH: blk * D + (g + 1) * DH].astype(F32) for g in range(H)]
        zb, ua, va, za = piece(0), piece(1), piece(2), piece(3)
        o = [of_ref[g] + ob_ref[g] for g in range(H)]
        gon = gon_ref[...]
        lng = [lng_ref[:, lanes(g)] for g in range(H)]
        lnb = [lnb_ref[:, lanes(g)] for g in range(H)]
        wsp = [wsp_ref[g] for g in range(H)]
        bsp = [bsp_ref[g] for g in range(H)]
        (ya, yb), vjp1 = jax.vjp(_stage1, zb, ua, va, za, o, gon, lng, lnb, wsp, bsp)
        y_a, y_b = jnp.concatenate(ya, axis=1), jnp.concatenate(yb, axis=1)
        ma, mb = bdot(y_a, wpa_ref), bdot(y_b, wpb_ref)
        ga, gb = p_ref[:, 4 * D:5 * D].astype(F32), p_ref[:, 5 * D:6 * D].astype(F32)
        merged, vjp2 = jax.vjp(_stage2, ma, mb, ga, gb)
        out = bdot(merged, wout_ref)
        gate = m_ref[:, 2 * D:3 * D] + bm_ref[:, 2 * D:3 * D]
        loss, vjp3 = jax.vjp(_stage3, out, x_ref[...], t_ref[...], gp_ref[...], gate)
        loss_ref[...] += jnp.broadcast_to(loss, loss_ref.shape)

        dout, dx, _, dgpost, dgate = vjp3(jnp.ones((1, 1), F32))
        dx_ref[...] = dx
        dmerged = bdot_t(dout, wout_ref)
        dma, dmb, dga, dgb = vjp2(dmerged)
        dya, dyb = bdot_t(dma, wpa_ref), bdot_t(dmb, wpb_ref)
        dzb, dua, dva, dza, do, dgon, dlng, dlnb, dwsp, dbsp = vjp1(
            ([dya[:, lanes(g)] for g in range(H)], [dyb[:, lanes(g)] for g in range(H)]))

        for blk, dlist in enumerate((dzb, dua, dva, dza)):
            for g in range(H):
                dp_ref[:, blk * D + g * DH: blk * D + (g + 1) * DH] = dlist[g].astype(BF16)
        dp_ref[:, 4 * D:5 * D] = dga.astype(BF16)
        dp_ref[:, 5 * D:6 * D] = dgb.astype(BF16)
        for g in range(H):
            do_ref[g] = do[g]
            dwsp_ref[g] += dwsp[g]
            dbsp_ref[g] += dbsp[g]
            dvec_ref[2:3, lanes(g)] += dlng[g]
            dvec_ref[3:4, lanes(g)] += dlnb[g]
        dvec_ref[0:1, :] += dgpost
        dvec_ref[1:2, :] += dgate
        dgon_ref[0:1, :] += dgon
        ya_ref[...] = y_a.astype(BF16)
        yb_ref[...] = y_b.astype(BF16)
        mg_ref[...] = merged.astype(BF16)
        dma_ref[...] = dma.astype(BF16)
        dmb_ref[...] = dmb.astype(BF16)
        dout_ref[...] = dout.astype(BF16)

    row = lambda w: pl.BlockSpec((GM, w), lambda i: (i, 0))
    heads = pl.BlockSpec((H, GM, DH), lambda i: (0, i, 0))
    full = lambda shape: pl.BlockSpec(shape, lambda i: tuple(0 for _ in shape))
    sds = jax.ShapeDtypeStruct
    return pl.pallas_call(
        body, name=name, grid=(n,),
        in_specs=[row(6 * D), heads, heads, row(D), row(D), full((1, 3 * D)), full((1, 3 * D)), full((1, DH)),
                  full((1, D)), full((1, D)), full((H, GM, GM)), full((H, GM, GM)),
                  full((D, D)), full((D, D)), full((D, D)), full((1, D))],
        out_specs=[full((8, DH)), row(6 * D), heads, row(D)] + [row(D)] * 6
        + [full((8, D)), full((8, DH)), full((H, GM, GM)), full((H, GM, GM))],
        out_shape=[sds((8, DH), F32), sds((rows, 6 * D), BF16), sds((H, rows, DH), F32), sds((rows, D), F32)]
        + [sds((rows, D), BF16)] * 6 + [sds((8, D), F32), sds((8, DH), F32), sds((H, GM, GM), F32), sds((H, GM, GM), F32)],
        compiler_params=_params(("arbitrary",)),
    )(p_rest, o_f, o_b, x, tgt, mod, bmod, gon, lng, lnb, wsp, bspb, wpa, wpb, wout, gpost)


def _dsilu_mul(c, d, *, name):
    def body(c_ref, d_ref, o_ref):
        _, vjp = jax.vjp(_silu, c_ref[...])
        (o_ref[...],) = vjp(d_ref[...])

    return pl.pallas_call(body, name=name, out_shape=jax.ShapeDtypeStruct(c.shape, F32))(c, d)


def _adamw(w, g, m, v, *, name):
    rows, cols = w.shape
    t = _tile(rows)
    c1 = 1.0 / (1.0 - ADAM_B1 ** ADAM_STEP)
    c2 = 1.0 / (1.0 - ADAM_B2 ** ADAM_STEP)

    def body(w_ref, g_ref, m_ref, v_ref, go_ref, d_ref, mo_ref, vo_ref):
        gv = g_ref[...]
        mn = ADAM_B1 * m_ref[...] + (1.0 - ADAM_B1) * gv
        vn = ADAM_B2 * v_ref[...] + (1.0 - ADAM_B2) * (gv * gv)
        go_ref[...] = gv
        mo_ref[...] = mn
        vo_ref[...] = vn
        d_ref[...] = -ADAM_LR * ((mn * c1) / (jnp.sqrt(vn * c2) + ADAM_EPS) + ADAM_WD * w_ref[...])

    tile = pl.BlockSpec((t, cols), lambda i: (i, 0))
    return pl.pallas_call(
        body, name=name, grid=(rows // t,),
        in_specs=[tile] * 4, out_specs=[tile] * 4,
        out_shape=[jax.ShapeDtypeStruct((rows, cols), F32)] * 4,
        compiler_params=_params(("parallel",)),
    )(w, g, m, v)


def _add_sibling(full, got, where, out_dtype, *, name):
    s, rows, cols = full.shape
    hr = rows // 2
    t = _tile(hr)
    nt = hr // t

    def body(w_ref, a_ref, b_ref, o_ref):
        o_ref[...] = (a_ref[...].astype(F32) + b_ref[...].astype(F32)).astype(out_dtype)

    tile = pl.BlockSpec((1, t, cols), lambda j, i, w: (j, i, 0))
    return pl.pallas_call(
        body, name=name,
        grid_spec=pltpu.PrefetchScalarGridSpec(
            num_scalar_prefetch=1, grid=(s, nt),
            in_specs=[pl.BlockSpec((1, t, cols), lambda j, i, w: (j, w[0] * nt + i, 0)), tile], out_specs=tile),
        out_shape=jax.ShapeDtypeStruct((s, hr, cols), out_dtype), compiler_params=_params(("parallel", "parallel")),
    )(where, full, got)


def _sum_chips(own, slots, where, n_out, which, *, name):
    _, hr, cols = own.shape
    t = _tile(hr)

    def body(w_ref, a_ref, s_ref, o_ref):
        f = lambda v: v.astype(F32)
        o_ref[0] = ((f(a_ref[0]) + f(s_ref[0])) + f(s_ref[1])) + f(s_ref[2])

    return pl.pallas_call(
        body, name=name,
        grid_spec=pltpu.PrefetchScalarGridSpec(
            num_scalar_prefetch=1, grid=(hr // t,),
            in_specs=[pl.BlockSpec((1, t, cols), lambda i, w: (w[1], i, 0)),
                      pl.BlockSpec((N_CHIPS - 1, t, cols), lambda i, w: (0, i, 0))],
            out_specs=pl.BlockSpec((1, t, cols), lambda i, w: (w[which], i, 0))),
        out_shape=jax.ShapeDtypeStruct((n_out, hr, cols), F32), compiler_params=_params(("parallel",)),
    )(where, own, slots)


def _local_step(x, c, ctx, c_ctx, tgt, w_mod, b_mod, g_pre, g_post, w_qkv, w_ab, w_rest, w_conv, a_log, dt_bias,
                g_onorm, gm_ln_g, gm_ln_b, w_sp, b_sp, w_pa, w_pb, w_out):
    rows, rows_c = x.shape[0], ctx.shape[0]
    cc = jnp.zeros((16, D), F32).at[0].set(c[0]).at[1].set(c_ctx)
    mod = _mm(cc, w_mod, pre_silu=True, b_sections=True, name="mod_fwd")
    mod_x, mod_c = mod[0:1], mod[1:2]
    avec = jnp.zeros((1, DH), F32).at[0, :2 * H].set(a_log.reshape(-1))
    dvec = jnp.zeros((1, DH), F32).at[0, :2 * H].set(dt_bias.reshape(-1))
    bspb = jnp.broadcast_to(b_sp[:, :, None], (H, GM, GM))
    w_kv, wconv_kv = w_qkv[:, D:], w_conv[:, D:]

    h_c = _norm_fwd(ctx, g_pre, mod_c, b_mod, name="norm_fwd_ctx")
    pc_kv = _mm(h_c, w_kv, name="inproj_ctx_kv")
    pc_ab = _mm(h_c, w_ab, name="inproj_ctx_ab")
    kv_c = _prep_fwd(pc_kv, wconv_kv, H, name="prep_fwd_ctx")
    gb_c = _gates_fwd(pc_ab, avec, dvec, name="gates_fwd_ctx")
    s_zero = jnp.zeros((H, DH, DH), F32)
    _, _, ssf_c, ssb_c, tsf_c, tsb_c, s_f, s_b = _gdn_fwd(kv_c, gb_c, s_zero, s_zero, False, name="gdn_fwd_ctx")

    h_x = _norm_fwd(x, g_pre, mod_x, b_mod, name="norm_fwd_x")
    p_qkv = _mm(h_x, w_qkv, name="inproj_qkv")
    p_ab = _mm(h_x, w_ab, name="inproj_ab")
    p_rest = _mm(h_x, w_rest, out_dtype=BF16, name="inproj_rest")
    qkv = _prep_fwd(p_qkv, w_conv, 2 * H, name="prep_fwd_x")
    gb_x = _gates_fwd(p_ab, avec, dvec, name="gates_fwd_x")
    o_f, o_b, ssf, ssb, tsf, tsb, _, _ = _gdn_fwd(qkv, gb_x, s_f, s_b, True, name="gdn_fwd_x")

    (loss_acc, dp_rest, do, dx_res, ya, yb, mg, dma, dmb, dout, dvec_post, dgon, dwsp, dbspb) = _post(
        p_rest, o_f, o_b, x, tgt, mod_x, b_mod, g_onorm, gm_ln_g, gm_ln_b, w_sp, bspb, w_pa, w_pb, w_out, g_post,
        name="post")
    g = {}
    g["w_pa"] = _mm(ya, dma, ta=True, name="dw_pa")
    g["w_pb"] = _mm(yb, dmb, ta=True, name="dw_pb")
    g["w_out"] = _mm(mg, dout, ta=True, name="dw_out")

    zeros_s = jnp.zeros((H, DH, DH), F32)
    dq_f, dq_b, dg_f, dg_b, ds0_f, ds0_b = _gdn_bwd(qkv, gb_x, ssf, ssb, tsf, tsb, do, zeros_s, zeros_s, True,
                                                    name="gdn_bwd_x")
    dp_qkv, dwc_x = _prep_bwd(p_qkv, w_conv, dq_f, dq_b, 2 * H, name="prep_bwd_x")
    dp_ab, dav_x, ddv_x = _gates_bwd(p_ab, avec, dvec, dg_f, dg_b, name="gates_bwd_x")
    dkv_f, dkv_b, dgc_f, dgc_b, _, _ = _gdn_bwd(kv_c, gb_c, ssf_c, ssb_c, tsf_c, tsb_c, jnp.zeros((H, rows_c, DH), F32),
                                                 ds0_f, ds0_b, False, name="gdn_bwd_ctx")
    dpc_kv, dwc_c = _prep_bwd(pc_kv, wconv_kv, dkv_f, dkv_b, H, name="prep_bwd_ctx")
    dpc_ab, dav_c, ddv_c = _gates_bwd(pc_ab, avec, dvec, dgc_f, dgc_b, name="gates_bwd_ctx")

    dw_kv_c = _mm(h_c, dpc_kv, ta=True, name="dw_kv_ctx")
    dw_qkv = _mm(h_x, dp_qkv, ta=True, tn=D, add=dw_kv_c, add_from=1, name="dw_qkv")
    dw_ab_c = _mm(h_c, dpc_ab, ta=True, name="dw_ab_ctx")
    dw_ab = _mm(h_x, dp_ab, ta=True, add=dw_ab_c, name="dw_ab")
    dw_rest = _mm(h_x, dp_rest, ta=True, name="dw_rest")
    g["w_in"] = (dw_qkv, dw_ab[:, :4 * H], dw_rest)
    dh_parts = [_mm(dp_qkv, w_qkv, tb=True, tk=3 * D // 2, name="dh_qkv"),
                _mm(dp_rest, w_rest, tb=True, tk=3 * D // 2, name="dh_rest"), _mm(dp_ab, w_ab, tb=True, name="dh_ab")]
    grad_x, dgpre_x, dm_x = _norm_bwd(x, g_pre, mod_x, b_mod, dh_parts, dx_res, name="norm_bwd_x")
    dhc_parts = [_mm(dpc_kv, w_kv, tb=True, name="dhc_kv"), _mm(dpc_ab, w_ab, tb=True, name="dhc_ab")]
    _, dgpre_c, dm_c = _norm_bwd(ctx, g_pre, mod_c, b_mod, dhc_parts, None, name="norm_bwd_ctx")

    dm_x = dm_x.at[:, 2 * D:].add(dvec_post[1:2])
    dmod = jnp.zeros((16, 3 * D), F32).at[0].set(dm_x[0]).at[1].set(dm_c[0])
    g["w_mod"] = _mm(cc, dmod, ta=True, pre_silu=True, tn=w_mod.shape[2], out_sections=True, name="dw_mod")
    dcc = _mm(dmod, w_mod, tb=True, b_sections=True, name="dcc")
    g["c_ctx"] = _dsilu_mul(cc[:8], dcc[:8], name="dc_ctx")[1]
    g["b_mod"] = dm_x + dm_c
    g["g_pre"] = dgpre_x + dgpre_c
    g["g_post"] = dvec_post[0:1]
    g["gm_ln_g"], g["gm_ln_b"] = dvec_post[2:3], dvec_post[3:4]
    g["g_onorm"] = dgon[0:1]
    g["w_sp"] = dwsp
    g["b_sp"] = jnp.sum(dbspb, axis=-1)
    g["w_conv"] = dwc_x.at[:, D:].add(dwc_c)
    g["a_log"] = (dav_x + dav_c)[0, :2 * H].reshape(2, H)
    g["dt_bias"] = (ddv_x + ddv_c)[0, :2 * H].reshape(2, H)
    return loss_acc[0, 0], grad_x, g


ANY = pl.BlockSpec(memory_space=pl.ANY)


def _place():
    x, y, c = lax.axis_index("x"), lax.axis_index("y"), lax.axis_index("c")
    chips = [(1 - x, y), (x, 1 - y), (1 - x, 1 - y)]
    return x, y, c, (x, y, 1 - c), chips


def _gather_shards(big, small, *, name):
    nb = len(big)

    def body(*refs):
        ins, sm_in = refs[:nb], refs[nb]
        outs, sm_out = refs[nb + 1:2 * nb + 1], refs[2 * nb + 1]
        send, recv = refs[2 * nb + 2:]
        x, y, c, sibling, chips = _place()
        mine = 2 * x + y

        def half(a, shard, hc):
            hr = big[a].shape[0] // 2
            return outs[a].at[shard, pl.ds(hc * hr, hr), :]

        def remote(k, src, dst, to):
            return pltpu.make_async_remote_copy(src_ref=src, dst_ref=dst, send_sem=send.at[k], recv_sem=recv.at[k],
                                                device_id=to, device_id_type=MESH)

        sends = []
        for a in range(nb):
            hr = big[a].shape[0] // 2
            for j, chip in enumerate(chips):
                sends.append(remote(a * 3 + j, ins[a].at[pl.ds(c * hr, hr), :], half(a, mine, c), (*chip, c)))
        for j, chip in enumerate(chips):
            sends.append(remote(nb * 3 + j, sm_in, sm_out.at[mine], (*chip, c)))
        for cp in sends:
            cp.start()
        base = nb * 3 + 3
        passed = []
        for a in range(nb):
            for j, (px, py) in enumerate(chips):
                theirs = 2 * px + py
                remote(a * 3 + j, half(a, theirs, c), half(a, theirs, c), sibling).wait_recv()
                fw = remote(base + a * 3 + j, half(a, theirs, c), half(a, theirs, c), sibling)
                fw.start()
                passed.append(fw)
        for a in range(nb):
            for j, (px, py) in enumerate(chips):
                theirs = 2 * px + py
                remote(base + a * 3 + j, half(a, theirs, 1 - c), half(a, theirs, 1 - c), sibling).wait_recv()
        for j, (px, py) in enumerate(chips):
            remote(nb * 3 + j, sm_in, sm_out.at[2 * px + py], sibling).wait_recv()
        for cp in sends + passed:
            cp.wait_send()

    n_remote = 2 * nb * 3 + 3
    outs = pl.pallas_call(
        body, name=name, in_specs=[ANY] * (nb + 1), out_specs=[ANY] * (nb + 1),
        out_shape=[jax.ShapeDtypeStruct((N_CHIPS,) + a.shape, a.dtype) for a in big + [small]],
        scratch_shapes=[pltpu.SemaphoreType.DMA((n_remote,)), pltpu.SemaphoreType.DMA((n_remote,))],
    )(*big, small)
    return outs[:nb], outs[nb]


def _row_chunks(rows, row_bytes, align=8):
    n = max(1, min(rows // align, -(-rows * row_bytes // DMA_CHUNK_BYTES)))
    per = -(-(-(-rows // n)) // align) * align
    return [(r, min(per, rows - r)) for r in range(0, rows, per)]


def _remote(src, dst, send, recv, to):
    return pltpu.make_async_remote_copy(src_ref=src, dst_ref=dst, send_sem=send, recv_sem=recv, device_id=to,
                                        device_id_type=MESH)


def _exchange_halves(arrs, *, name):
    na = len(arrs)

    def body(*refs):
        ins, got = refs[:na], refs[na:2 * na]
        send, recv = refs[2 * na:]
        x, y, c, sibling, _ = _place()
        for a in range(na):
            ns, rows, cols = arrs[a].shape
            hr = rows // 2
            for s in range(ns):
                for r0, nr in _row_chunks(hr, cols * arrs[a].dtype.itemsize, 16):
                    _remote(ins[a].at[s, pl.ds((1 - c) * hr + r0, nr), :], got[a].at[s, pl.ds(r0, nr), :],
                            send.at[a], recv.at[a], sibling).start()
        for a in range(na):
            hr = arrs[a].shape[1] // 2
            _remote(ins[a].at[:, pl.ds((1 - c) * hr, hr), :], got[a], send.at[a], recv.at[a], sibling).wait()

    return pl.pallas_call(
        body, name=name, in_specs=[ANY] * na, out_specs=[ANY] * na,
        out_shape=[jax.ShapeDtypeStruct((N_CHIPS, a.shape[1] // 2, a.shape[2]), a.dtype) for a in arrs],
        scratch_shapes=[pltpu.SemaphoreType.DMA((na,)), pltpu.SemaphoreType.DMA((na,))],
    )(*arrs)


def _scatter_sections(arrs, *, name):
    na = len(arrs)

    def body(*refs):
        ins, outs = refs[:na], refs[na:2 * na]
        send, recv = refs[2 * na:]
        x, y, c, _, chips = _place()
        for a in range(na):
            _, hr, cols = arrs[a].shape
            for j, (px, py) in enumerate(chips):
                for r0, nr in _row_chunks(hr, cols * arrs[a].dtype.itemsize, 16):
                    _remote(ins[a].at[2 * px + py, pl.ds(r0, nr), :], outs[a].at[j, pl.ds(r0, nr), :],
                            send.at[a * 3 + j], recv.at[a * 3 + j], (px, py, c)).start()
        for a in range(na):
            for j, (px, py) in enumerate(chips):
                _remote(ins[a].at[2 * px + py], outs[a].at[j], send.at[a * 3 + j], recv.at[a * 3 + j], (px, py, c)).wait()

    return pl.pallas_call(
        body, name=name, in_specs=[ANY] * na, out_specs=[ANY] * na,
        out_shape=[jax.ShapeDtypeStruct((N_CHIPS - 1,) + a.shape[1:], a.dtype) for a in arrs],
        scratch_shapes=[pltpu.SemaphoreType.DMA((3 * na,)), pltpu.SemaphoreType.DMA((3 * na,))],
    )(*arrs)


def _finish_reduce(big, small, *, name):
    nb = len(big)

    def body(*refs):
        outs, sm = refs[nb + 1:2 * nb + 1], refs[2 * nb + 1]
        send, recv = refs[2 * nb + 2:]
        x, y, c, sibling, chips = _place()
        blk = lambda px, py, pc: sm.at[4 * px + 2 * py + pc]
        for a in range(nb):
            _, hr, cols = big[a].shape
            for r0, nr in _row_chunks(hr, cols * 4):
                _remote(outs[a].at[c, pl.ds(r0, nr), :], outs[a].at[c, pl.ds(r0, nr), :], send.at[a], recv.at[a],
                        sibling).start()
        first = [_remote(blk(x, y, c), blk(x, y, c), send.at[nb], recv.at[nb], sibling)]
        first += [_remote(blk(x, y, c), blk(x, y, c), send.at[nb + 1 + j], recv.at[nb + 1 + j], (*chip, c))
                  for j, chip in enumerate(chips)]
        for cp in first:
            cp.start()
        passed = []
        for j, (px, py) in enumerate(chips):
            _remote(blk(px, py, c), blk(px, py, c), send.at[nb + 1 + j], recv.at[nb + 1 + j], sibling).wait_recv()
            fw = _remote(blk(px, py, c), blk(px, py, c), send.at[nb + 4 + j], recv.at[nb + 4 + j], sibling)
            fw.start()
            passed.append(fw)
        for a in range(nb):
            _remote(outs[a].at[c], outs[a].at[1 - c], send.at[a], recv.at[a], sibling).wait()
        _remote(blk(x, y, c), blk(x, y, 1 - c), send.at[nb], recv.at[nb], sibling).wait_recv()
        for j, (px, py) in enumerate(chips):
            _remote(blk(px, py, c), blk(px, py, 1 - c), send.at[nb + 4 + j], recv.at[nb + 4 + j], sibling).wait_recv()
        for cp in first + passed:
            cp.wait_send()

    n_remote = nb + 7
    arrs = list(big) + [small]
    outs = pl.pallas_call(
        body, name=name, in_specs=[ANY] * (nb + 1), out_specs=[ANY] * (nb + 1),
        out_shape=[jax.ShapeDtypeStruct(a.shape, F32) for a in arrs],
        input_output_aliases={i: i for i in range(nb + 1)},
        scratch_shapes=[pltpu.SemaphoreType.DMA((n_remote,)), pltpu.SemaphoreType.DMA((n_remote,))],
    )(*arrs)
    return outs[:nb], outs[nb]


def _reduce_gradients(sectioned, small, where):
    arrs = list(sectioned) + [small]
    got = _exchange_halves(arrs, name="rs_exchange_halves")
    pay = [BF16] * len(sectioned) + [F32]
    chip_sum = [_add_sibling(a, g, where, p, name=f"rs_add_sibling_{i}")
                for i, (a, g, p) in enumerate(zip(arrs, got, pay))]
    slots = _scatter_sections(chip_sum, name="rs_scatter_sections")
    red = [_sum_chips(p, s, where, 2, 0, name=f"rs_sum_chips_{i}")
           for i, (p, s) in enumerate(zip(chip_sum[:-1], slots[:-1]))]
    red_small = _sum_chips(chip_sum[-1], slots[-1], where, 2 * N_CHIPS, 2, name="rs_sum_chips_small")
    big, sm = _finish_reduce(red, red_small, name="rs_finish")
    return [b.reshape(-1, b.shape[-1]) for b in big], sm.reshape(-1, sm.shape[-1])


def kernel(x, c, ctx, c_ctx, w_mod, b_mod, g_pre, g_post, w_in, w_conv, a_log, dt_bias, g_onorm, gm_ln_g, gm_ln_b, w_sp, b_sp, w_pa, w_pb, w_out, loss_target, m_c_ctx, m_w_mod, m_b_mod, m_g_pre, m_g_post, m_w_in, m_w_conv, m_a_log, m_dt_bias, m_g_onorm, m_gm_ln_g, m_gm_ln_b, m_w_sp, m_b_sp, m_w_pa, m_w_pb, m_w_out, v_c_ctx, v_w_mod, v_b_mod, v_g_pre, v_g_post, v_w_in, v_w_conv, v_a_log, v_dt_bias, v_g_onorm, v_gm_ln_g, v_gm_ln_b, v_w_sp, v_b_sp, v_w_pa, v_w_pb, v_w_out):
    names = ["c_ctx", "w_mod", "b_mod", "g_pre", "g_post", "w_in", "w_conv", "a_log", "dt_bias", "g_onorm", "gm_ln_g",
             "gm_ln_b", "w_sp", "b_sp", "w_pa", "w_pb", "w_out"]
    w = dict(zip(names, (c_ctx, w_mod, b_mod, g_pre, g_post, w_in, w_conv, a_log, dt_bias, g_onorm, gm_ln_g, gm_ln_b,
                         w_sp, b_sp, w_pa, w_pb, w_out)))
    m = dict(zip(names, (m_c_ctx, m_w_mod, m_b_mod, m_g_pre, m_g_post, m_w_in, m_w_conv, m_a_log, m_dt_bias, m_g_onorm,
                         m_gm_ln_g, m_gm_ln_b, m_w_sp, m_b_sp, m_w_pa, m_w_pb, m_w_out)))
    v = dict(zip(names, (v_c_ctx, v_w_mod, v_b_mod, v_g_pre, v_g_post, v_w_in, v_w_conv, v_a_log, v_dt_bias, v_g_onorm,
                         v_gm_ln_g, v_gm_ln_b, v_w_sp, v_b_sp, v_w_pa, v_w_pb, v_w_out)))
    xy = 2 * lax.axis_index("x") + lax.axis_index("y")
    where = jnp.stack([lax.axis_index("c"), xy, 2 * xy + lax.axis_index("c")]).astype(jnp.int32)

    shards = [a[0].astype(BF16) for a in (w_mod, w_in, w_pa, w_pb, w_out)]
    gathered, wconv_all = _gather_shards(shards, w_conv[0], name="gather_weights")
    own = lambda full, shard: lax.dynamic_update_slice(full, shard[None], (xy, 0, 0))
    wm_all, win_all, wpa_all, wpb_all, wout_all = [own(f, s) for f, s in zip(gathered, shards)]
    wconv_all = own(wconv_all, w_conv[0])
    w_conv_f = jnp.concatenate([wconv_all[s] for s in range(N_CHIPS)], axis=1)
    shard_cols = IN_COLS // N_CHIPS
    cut = OFF_A - shard_cols
    assert 0 < cut and cut + 4 * H < shard_cols
    w_qkv = jnp.concatenate([win_all[0], win_all[1][:, :cut]], axis=1)
    w_ab = jnp.pad(win_all[1][:, cut:cut + 4 * H], ((0, 0), (0, DH - 4 * H)))
    w_rest = jnp.concatenate([win_all[1][:, cut + 4 * H:], win_all[2], win_all[3]], axis=1)

    loss_local, grad_x, g = _local_step(
        x[0], c, ctx[0], c_ctx, loss_target[0], wm_all, b_mod, g_pre, g_post, w_qkv, w_ab, w_rest,
        w_conv_f, a_log[0], dt_bias[0], g_onorm, gm_ln_g, gm_ln_b, w_sp[0], b_sp[0],
        wpa_all.reshape(D, D), wpb_all.reshape(D, D), wout_all.reshape(D, D))
    g["loss"] = loss_local
    blk = D // N_CHIPS
    dw_qkv, dw_ab, dw_rest = g["w_in"]
    rest_cut = shard_cols - cut - 4 * H
    g_win = jnp.stack([dw_qkv[:, :shard_cols],
                       jnp.concatenate([dw_qkv[:, shard_cols:], dw_ab, dw_rest[:, :rest_cut]], axis=1),
                       dw_rest[:, rest_cut:rest_cut + shard_cols], dw_rest[:, rest_cut + shard_cols:]]).astype(BF16)
    g["_pad"] = jnp.zeros((SMALL_LAYOUT["_pad"][1],), F32)
    tail = jnp.zeros((N_CHIPS * SMALL_ROWS * 128 - sum(s for _, s in SMALL_LAYOUT.values()),), F32)
    flat = jnp.concatenate([g[k].reshape(-1) for k in SMALL_LAYOUT] + [tail])
    big_names = ("w_mod", "w_in", "w_pa", "w_pb", "w_out")
    sectioned = [g["w_mod"], g_win] + [g[k].reshape(N_CHIPS, blk, D) for k in ("w_pa", "w_pb", "w_out")]
    reduced, gr_small = _reduce_gradients(sectioned, flat.reshape(N_CHIPS, SMALL_ROWS, 128), where)
    gr_tiny = gr_small[TINY_ROW0:]

    def entry(arr, k):
        off, size = SMALL_LAYOUT[k]
        row, col = off // 128 - TINY_ROW0, off % 128
        return arr[row:row + size // 128].reshape(-1) if size >= 128 else arr[row, col:col + size]

    res = {k: _adamw(w[k][0], gr, m[k][0], v[k][0], name=f"adamw_{k}") for k, gr in zip(big_names, reduced)}
    res["w_sp"] = _adamw(w_sp.reshape(-1, 128), gr_small, m_w_sp.reshape(-1, 128), v_w_sp.reshape(-1, 128),
                         name="adamw_w_sp")
    tiny_names = [k for k in SMALL_LAYOUT if k not in ("w_sp", "w_conv", "loss", "_pad")]

    def pack(src):
        parts = [src[k].reshape(-1) if k in tiny_names else jnp.zeros((s,), F32)
                 for k, (_, s) in SMALL_LAYOUT.items() if k != "w_sp"]
        return jnp.concatenate(parts + [tail]).reshape(-1, 128)

    tiny_res = _adamw(pack(w), gr_tiny, pack(m), pack(v), name="adamw_tiny")
    for k in tiny_names:
        res[k] = [entry(r, k) for r in tiny_res]
    g_conv = lax.dynamic_slice(entry(gr_tiny, "w_conv").reshape(3, 3 * D), (0, xy * (3 * D // N_CHIPS)),
                               (3, 3 * D // N_CHIPS))
    conv_res = _adamw(jnp.pad(w_conv[0], ((0, 5), (0, 0))), jnp.pad(g_conv, ((0, 5), (0, 0))),
                      jnp.pad(m_w_conv[0], ((0, 5), (0, 0))), jnp.pad(v_w_conv[0], ((0, 5), (0, 0))), name="adamw_w_conv")
    res["w_conv"] = [r[:3] for r in conv_res]
    res = {k: [r.reshape(w[k].shape) for r in res[k]] for k in names}

    out = [entry(gr_tiny, "loss").reshape(()), grad_x[None]]
    for i in range(4):
        out += [res[k][i] for k in names]
    return tuple(out)
```

```python
import functools

import jax
import jax.numpy as jnp
from jax import lax
from jax.experimental import pallas as pl
from jax.experimental.pallas import tpu as pltpu

F32 = jnp.float32
BF16 = jnp.bfloat16
HI = lax.Precision.HIGHEST
MESH = pl.DeviceIdType.MESH

D = 1024
H = 8
DH = 128
CH = 64
LOG_CH = 6
PAIR = 2 * CH
GM = 128
assert 1 << LOG_CH == CH and PAIR == DH
PREC_POWERS = (lax.Precision.HIGH,) * 3 + (None,) * 2
assert len(PREC_POWERS) == LOG_CH - 1
HEADS_PER_ITER_FWD = 8
HEADS_PER_ITER_BWD = 8
EPS = 1e-6
N_CHIPS = 4
OFF_A = 3 * D
OFF_ZB = OFF_A + 4 * H
IN_COLS = OFF_ZB + 6 * D
VMEM_LIMIT_V7X = 56 * 1024 * 1024
DMA_CHUNK_BYTES = 2 * 1024 * 1024

ADAM_LR, ADAM_B1, ADAM_B2, ADAM_EPS, ADAM_WD, ADAM_STEP = 0.001, 0.9, 0.999, 1e-08, 0.01, 10

SMALL_LAYOUT = {}
_off = 0
for _n, _s in (("w_sp", H * GM * GM), ("w_conv", 3 * 3 * D), ("c_ctx", D), ("b_mod", 3 * D), ("g_pre", D), ("g_post", D),
               ("gm_ln_g", D), ("gm_ln_b", D), ("b_sp", H * GM), ("g_onorm", DH), ("a_log", 2 * H), ("dt_bias", 2 * H),
               ("loss", 1), ("_pad", 128 - 4 * H - 1)):
    SMALL_LAYOUT[_n] = (_off, _s)
    _off += _s
SMALL_ROWS = 304
assert N_CHIPS * SMALL_ROWS * 128 >= _off and (SMALL_ROWS // 2) % 8 == 0 and _off % 128 == 0
TINY_ROW0 = SMALL_LAYOUT["w_conv"][0] // 128


def _params(sem=None):
    return pltpu.CompilerParams(dimension_semantics=sem, vmem_limit_bytes=VMEM_LIMIT_V7X)


def _tile(n, cands=(256, 128, 64, 32, 16, 8)):
    if n <= cands[0]:
        return n
    for cand in cands:
        if n % cand == 0:
            return cand
    return n


def _silu(x):
    return x * jax.nn.sigmoid(x)


def _gelu(x):
    return 0.5 * x * (1.0 + jnp.tanh(0.7978845608028654 * (x + 0.044715 * (x * x * x))))


def _mm(a, b, *, ta=False, tb=False, out_dtype=F32, tm=1024, tn=1024, tk=1024, pre_silu=False, add=None, add_from=0,
        b_sections=False, out_sections=False, name):
    m, k = (a.shape[1], a.shape[0]) if ta else a.shape
    if b_sections:
        sect = b.shape[2]
        n = b.shape[1] if tb else b.shape[0] * sect
        tn, tk = (tn, sect) if tb else (sect, tk)
    else:
        n = b.shape[0] if tb else b.shape[1]
    tm, tn, tk = min(tm, m), min(tn, n), min(tk, k)
    assert m % tm == 0 and n % tn == 0 and k % tk == 0, (name, m, n, k, tm, tn, tk)
    nk = k // tk
    dims = (((0,) if ta else (1,), (1,) if tb else (0,)), ((), ()))
    has_add = add is not None
    assert not has_add or add.shape == (m, n - add_from * tn), (name, add.shape)

    def body(*refs):
        a_ref, b_ref = refs[:2]
        o_ref = refs[2 + int(has_add)]
        acc_ref = refs[-1]
        kk = pl.program_id(2)
        av = a_ref[...]
        if pre_silu:
            av = _silu(av.astype(F32))
        bv = b_ref[0] if b_sections else b_ref[...]
        part = lax.dot_general(av.astype(BF16), bv.astype(BF16), dims, preferred_element_type=F32)

        def finish(res):
            if has_add:
                res = res + jnp.where(pl.program_id(1) >= add_from, refs[2][...], 0.0)
            if out_sections:
                o_ref[0] = res.astype(out_dtype)
            else:
                o_ref[...] = res.astype(out_dtype)

        if nk == 1:
            finish(part)
            return

        @pl.when(kk == 0)
        def _():
            acc_ref[...] = part

        @pl.when((kk > 0) & (kk < nk - 1))
        def _():
            acc_ref[...] += part

        @pl.when(kk == nk - 1)
        def _():
            finish(acc_ref[...] + part)

    a_spec = pl.BlockSpec((tk, tm), lambda i, j, q: (q, i)) if ta else pl.BlockSpec((tm, tk), lambda i, j, q: (i, q))
    if b_sections:
        b_spec = (pl.BlockSpec((1, tn, tk), lambda i, j, q: (q, j, 0)) if tb
                  else pl.BlockSpec((1, tk, tn), lambda i, j, q: (j, q, 0)))
    else:
        b_spec = pl.BlockSpec((tn, tk), lambda i, j, q: (j, q)) if tb else pl.BlockSpec((tk, tn), lambda i, j, q: (q, j))
    add_spec = [pl.BlockSpec((tm, tn), lambda i, j, q: (i, jnp.maximum(j - add_from, 0)))] if has_add else []
    if out_sections:
        out_spec, out_shape = pl.BlockSpec((1, tm, tn), lambda i, j, q: (j, i, 0)), (n // tn, m, tn)
    else:
        out_spec, out_shape = pl.BlockSpec((tm, tn), lambda i, j, q: (i, j)), (m, n)
    return pl.pallas_call(
        body, name=name, grid=(m // tm, n // tn, nk),
        in_specs=[a_spec, b_spec] + add_spec, out_specs=out_spec,
        out_shape=jax.ShapeDtypeStruct(out_shape, out_dtype),
        scratch_shapes=[pltpu.VMEM((tm, tn), F32)] if nk > 1 else [],
        compiler_params=_params(("parallel", "parallel", "arbitrary")),
    )(*([a, b] + ([add] if has_add else [])))


def _h_fn(x, g, m):
    shift, scale = m[:, 0:D], m[:, D:2 * D]
    r = lax.rsqrt(jnp.mean(x * x, axis=-1, keepdims=True) + EPS)
    return (x * r * g) * (1.0 + scale) + shift


def _norm_fwd(x, g, mod, bmod, *, name):
    rows = x.shape[0]
    t = min(512, rows)

    def body(x_ref, g_ref, m_ref, b_ref, h_ref):
        h_ref[...] = _h_fn(x_ref[...], g_ref[...], m_ref[...] + b_ref[...]).astype(BF16)

    vec = lambda w: pl.BlockSpec((1, w), lambda i: (0, 0))
    return pl.pallas_call(
        body, name=name, grid=(rows // t,),
        in_specs=[pl.BlockSpec((t, D), lambda i: (i, 0)), vec(D), vec(3 * D), vec(3 * D)],
        out_specs=pl.BlockSpec((t, D), lambda i: (i, 0)),
        out_shape=jax.ShapeDtypeStruct((rows, D), BF16),
        compiler_params=_params(("parallel",)),
    )(x, g, mod, bmod)


def _norm_bwd(x, g, mod, bmod, dh_parts, resid, *, name):
    rows = x.shape[0]
    t = min(512, rows)
    n_parts = len(dh_parts)
    has_resid = resid is not None

    def body(*refs):
        x_ref, g_ref, m_ref, b_ref = refs[:4]
        parts = refs[4:4 + n_parts]
        r_ref = refs[4 + n_parts] if has_resid else None
        dx_ref, dg_ref, dm_ref = refs[-3:]
        i = pl.program_id(0)
        dh = parts[0][...]
        for p in parts[1:]:
            dh = dh + p[...]
        _, vjp = jax.vjp(_h_fn, x_ref[...], g_ref[...], m_ref[...] + b_ref[...])
        dx, dg, dm = vjp(dh)
        if has_resid:
            dx = dx + r_ref[...]
        dx_ref[...] = dx

        @pl.when(i == 0)
        def _():
            dg_ref[...] = dg
            dm_ref[...] = dm

        @pl.when(i > 0)
        def _():
            dg_ref[...] += dg
            dm_ref[...] += dm

    vec = lambda w: pl.BlockSpec((1, w), lambda i: (0, 0))
    tile = pl.BlockSpec((t, D), lambda i: (i, 0))
    ins = [x, g, mod, bmod, *dh_parts] + ([resid] if has_resid else [])
    return pl.pallas_call(
        body, name=name, grid=(rows // t,),
        in_specs=[tile, vec(D), vec(3 * D), vec(3 * D)] + [tile] * (n_parts + int(has_resid)),
        out_specs=[tile, vec(D), vec(3 * D)],
        out_shape=[jax.ShapeDtypeStruct((rows, D), F32), jax.ShapeDtypeStruct((1, D), F32),
                   jax.ShapeDtypeStruct((1, 3 * D), F32)],
        compiler_params=_params(("arbitrary",)),
    )(*ins)


def _conv_tile(u_ref, r0, t, rows, w0, w1, w2):
    u = u_ref[pl.ds(r0, t), :]
    prev8 = u_ref[pl.ds(pl.multiple_of(jnp.maximum(r0 - 8, 0), 8), 8), :]
    next8 = u_ref[pl.ds(pl.multiple_of(jnp.minimum(r0 + t, rows - 8), 8), 8), :]
    r8 = lax.broadcasted_iota(jnp.int32, (8, DH), 0)
    prev_row = jnp.sum(jnp.where(r8 == 7, prev8, 0.0), axis=0, keepdims=True)
    next_row = jnp.sum(jnp.where(r8 == 0, next8, 0.0), axis=0, keepdims=True)
    prev_row = jnp.where(r0 > 0, prev_row, 0.0)
    next_row = jnp.where(r0 + t < rows, next_row, 0.0)
    ri = lax.broadcasted_iota(jnp.int32, (t, DH), 0)
    um1 = jnp.where(ri == 0, prev_row, pltpu.roll(u, 1, 0))
    up1 = jnp.where(ri == t - 1, next_row, pltpu.roll(u, t - 1, 0))
    return w0 * um1 + w1 * u + w2 * up1, um1, u, up1


def _rowlocal(z, is_norm):
    y = _silu(z)
    yn = y * lax.rsqrt(jnp.sum(y * y, axis=-1, keepdims=True) + EPS)
    return jnp.where(is_norm, yn, y)


def _prep_fwd(p, wconv, n_norm, *, name):
    rows, nb = p.shape[0], p.shape[1] // DH
    t = min(512, rows)

    def body(u_ref, w_ref, o_ref):
        is_norm = pl.program_id(0) < n_norm
        w0, w1, w2 = w_ref[0:1, :], w_ref[1:2, :], w_ref[2:3, :]

        def step(s, carry):
            r0 = pl.multiple_of(s * t, t)
            z, _, _, _ = _conv_tile(u_ref, r0, t, rows, w0, w1, w2)
            o_ref[0, pl.ds(r0, t), :] = _rowlocal(z, is_norm)
            return carry

        lax.fori_loop(0, rows // t, step, 0)

    return pl.pallas_call(
        body, name=name, grid=(nb,),
        in_specs=[pl.BlockSpec((rows, DH), lambda j: (0, j)), pl.BlockSpec((3, DH), lambda j: (0, j))],
        out_specs=pl.BlockSpec((1, rows, DH), lambda j: (j, 0, 0)),
        out_shape=jax.ShapeDtypeStruct((nb, rows, DH), F32),
        compiler_params=_params(("parallel",)),
    )(p, wconv)


def _prep_bwd(p, wconv, d_a, d_b, n_norm, *, name):
    rows, nb = p.shape[0], p.shape[1] // DH
    t = min(512, rows)

    def body(u_ref, w_ref, da_ref, db_ref, du_ref, dw_ref, dz_ref):
        is_norm = pl.program_id(0) < n_norm
        w0, w1, w2 = w_ref[0:1, :], w_ref[1:2, :], w_ref[2:3, :]

        def step1(s, carry):
            a0, a1, a2 = carry
            r0 = pl.multiple_of(s * t, t)
            z, um1, u, up1 = _conv_tile(u_ref, r0, t, rows, w0, w1, w2)
            _, vjp = jax.vjp(lambda zz: _rowlocal(zz, is_norm), z)
            (dz,) = vjp(da_ref[0, pl.ds(r0, t), :] + db_ref[0, pl.ds(r0, t), :])
            dz_ref[pl.ds(r0, t), :] = dz
            red = lambda v: jnp.sum(v, axis=0, keepdims=True)
            return a0 + red(dz * um1), a1 + red(dz * u), a2 + red(dz * up1)

        zero = jnp.zeros((1, DH), F32)
        a0, a1, a2 = lax.fori_loop(0, rows // t, step1, (zero, zero, zero))
        dw_ref[0:1, :] = a0
        dw_ref[1:2, :] = a1
        dw_ref[2:3, :] = a2

        def step2(s, carry):
            r0 = pl.multiple_of(s * t, t)
            du, _, _, _ = _conv_tile(dz_ref, r0, t, rows, w2, w1, w0)
            du_ref[pl.ds(r0, t), :] = du.astype(BF16)
            return carry

        lax.fori_loop(0, rows // t, step2, 0)

    col = pl.BlockSpec((rows, DH), lambda j: (0, j))
    w_spec = pl.BlockSpec((3, DH), lambda j: (0, j))
    d_spec = pl.BlockSpec((1, rows, DH), lambda j: (j, 0, 0))
    return pl.pallas_call(
        body, name=name, grid=(nb,),
        in_specs=[col, w_spec, d_spec, d_spec], out_specs=[col, w_spec],
        out_shape=[jax.ShapeDtypeStruct((rows, nb * DH), BF16), jax.ShapeDtypeStruct((3, nb * DH), F32)],
        scratch_shapes=[pltpu.VMEM((rows, DH), F32)],
        compiler_params=_params(("parallel",)),
    )(p, wconv, d_a, d_b)


def _gates_fn(pab, avec, dvec):
    t = pab.shape[0]
    lane = lax.broadcasted_iota(jnp.int32, pab.shape, 1)
    xg = pab + dvec
    sp = jnp.maximum(xg, 0.0) + jnp.log(1.0 + jnp.exp(-jnp.abs(xg)))
    g = jnp.where(lane < 2 * H, -jnp.exp(avec) * sp, 0.0)
    ii = lax.broadcasted_iota(jnp.int32, (t, t), 0)
    jj = lax.broadcasted_iota(jnp.int32, (t, t), 1)
    same = (ii >> LOG_CH) == (jj >> LOG_CH)
    cum_f = _dot(jnp.where(same & (jj <= ii), 1.0, 0.0), g, precision=HI)
    cum_r = _dot(jnp.where(same & (jj >= ii), 1.0, 0.0), g, precision=HI)
    return jnp.where(lane < H, cum_f, jnp.where(lane < 2 * H, cum_r, jnp.where(lane < 4 * H, jax.nn.sigmoid(pab), 0.0)))


def _gates_fwd(pab, avec, dvec, *, name):
    rows = pab.shape[0]
    t = min(512, rows)

    def body(p_ref, a_ref, d_ref, o_ref):
        o_ref[...] = _gates_fn(p_ref[...], a_ref[...], d_ref[...])

    vec = pl.BlockSpec((1, DH), lambda i: (0, 0))
    tile = pl.BlockSpec((t, DH), lambda i: (i, 0))
    return pl.pallas_call(
        body, name=name, grid=(rows // t,), in_specs=[tile, vec, vec], out_specs=tile,
        out_shape=jax.ShapeDtypeStruct((rows, DH), F32), compiler_params=_params(("parallel",)),
    )(pab, avec, dvec)


def _gates_bwd(pab, avec, dvec, d_a, d_b, *, name):
    rows = pab.shape[0]
    t = min(512, rows)

    def body(p_ref, a_ref, d_ref, da_ref, db_ref, dp_ref, dav_ref, ddv_ref):
        i = pl.program_id(0)
        _, vjp = jax.vjp(_gates_fn, p_ref[...], a_ref[...], d_ref[...])
        dp, dav, ddv = vjp(da_ref[...] + db_ref[...])
        dp_ref[...] = dp.astype(BF16)

        @pl.when(i == 0)
        def _():
            dav_ref[...] = dav
            ddv_ref[...] = ddv

        @pl.when(i > 0)
        def _():
            dav_ref[...] += dav
            ddv_ref[...] += ddv

    vec = pl.BlockSpec((1, DH), lambda i: (0, 0))
    tile = pl.BlockSpec((t, DH), lambda i: (i, 0))
    return pl.pallas_call(
        body, name=name, grid=(rows // t,), in_specs=[tile, vec, vec, tile, tile], out_specs=[tile, vec, vec],
        out_shape=[jax.ShapeDtypeStruct((rows, DH), BF16), jax.ShapeDtypeStruct((1, DH), F32),
                   jax.ShapeDtypeStruct((1, DH), F32)],
        compiler_params=_params(("arbitrary",)),
    )(pab, avec, dvec, d_a, d_b)


def _dot(a, b, dims=((1,), (0,)), precision=None):
    return lax.dot_general(a, b, (dims, ((), ())), precision=precision, preferred_element_type=F32)


_NT = ((1,), (1,))
_TN = ((0,), (0,))


@jax.custom_vjp
def _saved_inverse(neg_a, inv):
    return inv


def _saved_inverse_fwd(neg_a, inv):
    return inv, inv


def _saved_inverse_bwd(inv, d_inv):
    idx = range(len(inv))
    left = [_dot(inv[i], d_inv[i], _TN) for i in idx]
    d_neg_a = [_dot(left[i], inv[i], _NT) for i in idx]
    return d_neg_a, [jnp.zeros_like(t) for t in inv]


_saved_inverse.defvjp(_saved_inverse_fwd, _saved_inverse_bwd)


def _pairs(s, q, k, v, gcol, bcol, revs, inv_saved=None):
    idx = range(len(revs))
    ii = lax.broadcasted_iota(jnp.int32, (PAIR, PAIR), 0)
    jj = lax.broadcasted_iota(jnp.int32, (PAIR, PAIR), 1)
    same = (ii >> LOG_CH) == (jj >> LOG_CH)
    incl_d = (same & (ii >= jj), same & (ii <= jj))
    strict_d = (same & (ii > jj), same & (ii < jj))
    incl = [incl_d[int(r)] for r in revs]
    strict = [strict_d[int(r)] for r in revs]
    eye = jnp.where(ii == jj, 1.0, 0.0)
    gc_i = [jnp.broadcast_to(gcol[i], (PAIR, DH)) for i in idx]
    gc_j = [gc_i[i].T for i in idx]
    decay = [jnp.where(incl[i], jnp.exp(jnp.where(incl[i], gc_i[i] - gc_j[i], 0.0)), 0.0) for i in idx]
    b_b = [jnp.broadcast_to(bcol[i], (PAIR, DH)) for i in idx]
    kb = [k[i] * b_b[i] for i in idx]
    kk = [_dot(kb[i], k[i], _NT) for i in idx]
    bp = [jnp.where(strict[i], -kk[i] * decay[i], 0.0) for i in idx]
    if inv_saved is not None:
        inv = _saved_inverse(bp, inv_saved)
    else:
        inv = [eye + bp[i] for i in idx]
        for prec in PREC_POWERS:
            bp = [_dot(bp[i], bp[i], precision=prec) for i in idx]
            more = [_dot(inv[i], bp[i], precision=prec) for i in idx]
            inv = [inv[i] + more[i] for i in idx]
    eg = [jnp.exp(gc_i[i]) for i in idx]
    sol = [_dot(inv[i], jnp.concatenate([v[i] * b_b[i], kb[i] * eg[i]], axis=1)) for i in idx]
    u_val = [sol[i][:, :DH] for i in idx]
    w_key = [sol[i][:, DH:] for i in idx]
    row = lax.broadcasted_iota(jnp.int32, (PAIR, 1), 0)
    has_q = q[0] is not None
    if has_q:
        qc = [q[i] * (DH ** -0.5) for i in idx]
        qk = [_dot(qc[i], k[i], _NT) for i in idx]
        attn = [qk[i] * decay[i] for i in idx]
        qd = [qc[i] * eg[i] for i in idx]
    outs = [[None, None] for _ in idx]
    zeros = jnp.zeros((CH, DH), F32)
    for step in range(2):
        cidx = [(1 - step) if revs[i] else step for i in idx]
        sl = [slice(c * CH, (c + 1) * CH) for c in cidx]
        last = [c * CH if revs[i] else c * CH + CH - 1 for i, c in zip(idx, cidx)]
        gl = [jnp.sum(jnp.where(row == last[i], gcol[i], 0.0), axis=0, keepdims=True) for i in idx]
        k_tail = [k[i][sl[i]] * jnp.exp(gl[i] - gc_i[i][sl[i]]) for i in idx]
        ws = [_dot(w_key[i][sl[i]], s[i]) for i in idx]
        v_new = [u_val[i][sl[i]] - ws[i] for i in idx]
        if has_q:
            v_pad = [jnp.concatenate([v_new[i], zeros] if cidx[i] == 0 else [zeros, v_new[i]], axis=0) for i in idx]
            o_state = [_dot(qd[i][sl[i]], s[i]) for i in idx]
            o_local = [_dot(attn[i][sl[i]], v_pad[i]) for i in idx]
            for i in idx:
                outs[i][cidx[i]] = o_state[i] + o_local[i]
        kv = [_dot(k_tail[i], v_new[i], _TN) for i in idx]
        s = [s[i] * jnp.exp(gl[i]) + kv[i] for i in idx]
    return s, ([jnp.concatenate(outs[i], axis=0) for i in idx] if has_q else None), inv


def _lane_col(tile, idx):
    lane = lax.broadcasted_iota(jnp.int32, tile.shape, 1)
    return jnp.sum(jnp.where(lane == idx, tile, 0.0), axis=1, keepdims=True)


def _gdn_fwd(qkv, gb, s0f, s0b, has_q, *, name):
    nb, rows, _ = qkv.shape
    n = rows // PAIR
    qoff = H if has_q else 0

    def body(qf_ref, qb_ref, gf_ref, gr_ref, s0f_ref, s0b_ref, of_ref, ob_ref, ssf_ref, ssb_ref, tsf_ref, tsb_ref,
             sf_ref, sb_ref):
        @pl.when(pl.program_id(0) == 0)
        def _():
            sf_ref[...] = s0f_ref[...]
            sb_ref[...] = s0b_ref[...]

        gtiles = (gf_ref[...], gr_ref[...])

        dirs = ((qf_ref, sf_ref, ssf_ref, of_ref), (qb_ref, sb_ref, ssb_ref, ob_ref))
        ts_refs = (tsf_ref, tsb_ref)

        def heads(hg, carry):
            work = [(hg * HEADS_PER_ITER_FWD + j, d) for j in range(HEADS_PER_ITER_FWD) for d in range(2)]
            loaded = []
            for h, d in work:
                q_ref, s_ref, _, _ = dirs[d]
                loaded.append((s_ref[h], q_ref[h] if has_q else None, q_ref[qoff + h], q_ref[qoff + H + h],
                               _lane_col(gtiles[d], d * H + h), _lane_col(gtiles[d], 2 * H + d * H + h)))
            s_new, o, inv = _pairs(*[list(col) for col in zip(*loaded)], revs=[d == 1 for _, d in work])
            for i, (h, d) in enumerate(work):
                _, s_ref, ss_ref, o_ref = dirs[d]
                ss_ref[0, h] = loaded[i][0]
                ts_refs[d][0, h] = inv[i]
                s_ref[h] = s_new[i]
                o_ref[h] = o[i] if has_q else jnp.zeros((PAIR, DH), F32)
            return carry

        if HEADS_PER_ITER_FWD == H:
            heads(0, 0)
        else:
            lax.fori_loop(0, H // HEADS_PER_ITER_FWD, heads, 0)

    fwd3 = lambda i: (0, i, 0)
    rev3 = lambda i: (0, n - 1 - i, 0)
    state = pl.BlockSpec((H, DH, DH), lambda i: (0, 0, 0))
    saved = pl.BlockSpec((1, H, DH, DH), lambda i: (i, 0, 0, 0))
    return pl.pallas_call(
        body, name=name, grid=(n,),
        in_specs=[pl.BlockSpec((nb, PAIR, DH), fwd3), pl.BlockSpec((nb, PAIR, DH), rev3),
                  pl.BlockSpec((PAIR, DH), lambda i: (i, 0)), pl.BlockSpec((PAIR, DH), lambda i: (n - 1 - i, 0)),
                  state, state],
        out_specs=[pl.BlockSpec((H, PAIR, DH), fwd3), pl.BlockSpec((H, PAIR, DH), rev3), saved, saved, saved, saved,
                   state, state],
        out_shape=[jax.ShapeDtypeStruct((H, rows, DH), F32)] * 2 + [jax.ShapeDtypeStruct((n, H, DH, DH), F32)] * 4
        + [jax.ShapeDtypeStruct((H, DH, DH), F32)] * 2,
        compiler_params=_params(("arbitrary",)),
    )(qkv, qkv, gb, gb, s0f, s0b)


def _gdn_bwd(qkv, gb, ssf, ssb, tsf, tsb, do, dsf, dsb, has_q, *, name):
    nb, rows, _ = qkv.shape
    n = rows // PAIR
    qoff = H if has_q else 0

    def body(qf_ref, qb_ref, gf_ref, gr_ref, ssf_ref, ssb_ref, tsf_ref, tsb_ref, dof_ref, dob_ref, dsf0_ref, dsb0_ref,
             dqf_ref, dqb_ref, dgf_ref, dgr_ref, dsf_ref, dsb_ref):
        ts_refs = (tsf_ref, tsb_ref)
        @pl.when(pl.program_id(0) == 0)
        def _():
            dsf_ref[...] = dsf0_ref[...]
            dsb_ref[...] = dsb0_ref[...]

        gtiles = (gf_ref[...], gr_ref[...])
        lane = lax.broadcasted_iota(jnp.int32, (PAIR, DH), 1)

        dirs = ((qf_ref, ssf_ref, dof_ref, dsf_ref, dqf_ref), (qb_ref, ssb_ref, dob_ref, dsb_ref, dqb_ref))

        def heads(hg, carry):
            out = list(carry)
            work = [(hg * HEADS_PER_ITER_BWD + j, d) for j in range(HEADS_PER_ITER_BWD) for d in range(2)]
            revs = [d == 1 for _, d in work]
            s_in, q_in, k_in, v_in, g_in, b_in, ds_out, do_out, inv_in = [], [], [], [], [], [], [], [], []
            for h, d in work:
                q_ref, ss_ref, do_ref, ds_ref, _ = dirs[d]
                s_in.append(ss_ref[0, h])
                inv_in.append(ts_refs[d][0, h])
                q_in.append(q_ref[h] if has_q else None)
                k_in.append(q_ref[qoff + h])
                v_in.append(q_ref[qoff + H + h])
                g_in.append(_lane_col(gtiles[d], d * H + h))
                b_in.append(_lane_col(gtiles[d], 2 * H + d * H + h))
                ds_out.append(ds_ref[h])
                do_out.append(do_ref[h] if has_q else None)
            if has_q:
                _, vjp = jax.vjp(lambda s_, q_, k_, v_, g_, b_: _pairs(s_, q_, k_, v_, g_, b_, revs, inv_in)[:2],
                                 s_in, q_in, k_in, v_in, g_in, b_in)
                ds, dq, dk, dv, dg, db = vjp((ds_out, do_out))
            else:
                _, vjp = jax.vjp(lambda s_, k_, v_, g_, b_: _pairs(s_, q_in, k_, v_, g_, b_, revs, inv_in)[0],
                                 s_in, k_in, v_in, g_in, b_in)
                ds, dk, dv, dg, db = vjp(ds_out)
            for i, (h, d) in enumerate(work):
                dq_ref, ds_ref = dirs[d][4], dirs[d][3]
                ds_ref[h] = ds[i]
                if has_q:
                    dq_ref[h] = dq[i]
                dq_ref[qoff + h], dq_ref[qoff + H + h] = dk[i], dv[i]
                out[d] = (out[d] + jnp.where(lane == d * H + h, dg[i], 0.0)
                          + jnp.where(lane == 2 * H + d * H + h, db[i], 0.0))
            return tuple(out)

        zero = jnp.zeros((PAIR, DH), F32)
        if HEADS_PER_ITER_BWD == H:
            dgf, dgr = heads(0, (zero, zero))
        else:
            dgf, dgr = lax.fori_loop(0, H // HEADS_PER_ITER_BWD, heads, (zero, zero))
        dgf_ref[...] = dgf
        dgr_ref[...] = dgr

    fwd3 = lambda i: (0, n - 1 - i, 0)
    rev3 = lambda i: (0, i, 0)
    state = pl.BlockSpec((H, DH, DH), lambda i: (0, 0, 0))
    saved = pl.BlockSpec((1, H, DH, DH), lambda i: (n - 1 - i, 0, 0, 0))
    gf_spec = pl.BlockSpec((PAIR, DH), lambda i: (n - 1 - i, 0))
    gr_spec = pl.BlockSpec((PAIR, DH), lambda i: (i, 0))
    return pl.pallas_call(
        body, name=name, grid=(n,),
        in_specs=[pl.BlockSpec((nb, PAIR, DH), fwd3), pl.BlockSpec((nb, PAIR, DH), rev3), gf_spec, gr_spec,
                  saved, saved, saved, saved,
                  pl.BlockSpec((H, PAIR, DH), fwd3), pl.BlockSpec((H, PAIR, DH), rev3), state, state],
        out_specs=[pl.BlockSpec((nb, PAIR, DH), fwd3), pl.BlockSpec((nb, PAIR, DH), rev3), gf_spec, gr_spec, state, state],
        out_shape=[jax.ShapeDtypeStruct((nb, rows, DH), F32)] * 2 + [jax.ShapeDtypeStruct((rows, DH), F32)] * 2
        + [jax.ShapeDtypeStruct((H, DH, DH), F32)] * 2,
        compiler_params=_params(("arbitrary",)),
    )(qkv, qkv, gb, gb, ssf, ssb, tsf, tsb, do, do, dsf, dsb)


def _stage1(zb, ua, va, za, o, gon, lng, lnb, wsp, bsp):
    gv = [_gelu(t) for t in va]
    mu = sum(jnp.sum(t, axis=-1, keepdims=True) for t in gv) * (1.0 / D)
    xc = [t - mu for t in gv]
    var = sum(jnp.sum(t * t, axis=-1, keepdims=True) for t in xc) * (1.0 / D)
    rs = lax.rsqrt(var + EPS)
    ya, yb = [], []
    for g in range(H):
        vv = xc[g] * rs * lng[g] + lnb[g]
        s = _dot(wsp[g], vv) + bsp[g]
        ya.append(_gelu(ua[g]) * s * _silu(za[g]))
        r = lax.rsqrt(jnp.mean(o[g] * o[g], axis=-1, keepdims=True) + EPS)
        yb.append(o[g] * r * gon * _silu(zb[g]))
    return ya, yb


def _stage2(ma, mb, ga, gb):
    return jax.nn.sigmoid(ga) * ma + jax.nn.sigmoid(gb) * mb


def _stage3(out, x, tgt, gpost, gate):
    r = out * lax.rsqrt(jnp.mean(out * out, axis=-1, keepdims=True) + EPS) * gpost
    err = x + gate * r - tgt
    return 0.5 * jnp.sum(jnp.mean(err * err, axis=-1, keepdims=True), axis=0, keepdims=True)


def _post(p_rest, o_f, o_b, x, tgt, mod, bmod, gon, lng, lnb, wsp, bspb, wpa, wpb, wout, gpost, *, name):
    rows = x.shape[0]
    n = rows // GM
    lanes = lambda g: slice(g * DH, (g + 1) * DH)
    bdot = lambda a, w_ref: _dot(a.astype(BF16), w_ref[...])
    bdot_t = lambda a, w_ref: _dot(a.astype(BF16), w_ref[...], _NT)

    def body(p_ref, of_ref, ob_ref, x_ref, t_ref, m_ref, bm_ref, gon_ref, lng_ref, lnb_ref, wsp_ref, bsp_ref,
             wpa_ref, wpb_ref, wout_ref, gp_ref,
             loss_ref, dp_ref, do_ref, dx_ref, ya_ref, yb_ref, mg_ref, dma_ref, dmb_ref, dout_ref,
             dvec_ref, dgon_ref, dwsp_ref, dbsp_ref):
        @pl.when(pl.program_id(0) == 0)
        def _():
            loss_ref[...] = jnp.zeros_like(loss_ref)
            dvec_ref[...] = jnp.zeros_like(dvec_ref)
            dgon_ref[...] = jnp.zeros_like(dgon_ref)
            dwsp_ref[...] = jnp.zeros_like(dwsp_ref)
            dbsp_ref[...] = jnp.zeros_like(dbsp_ref)

        piece = lambda blk: [p_ref[:, blk * D + g * DH: blk * D + (g + 1) * DH].astype(F32) for g in range(H)]
        zb, ua, va, za = piece(0), piece(1), piece(2), piece(3)
        o = [of_ref[g] + ob_ref[g] for g in range(H)]
        gon = gon_ref[...]
        lng = [lng_ref[:, lanes(g)] for g in range(H)]
        lnb = [lnb_ref[:, lanes(g)] for g in range(H)]
        wsp = [wsp_ref[g] for g in range(H)]
        bsp = [bsp_ref[g] for g in range(H)]
        (ya, yb), vjp1 = jax.vjp(_stage1, zb, ua, va, za, o, gon, lng, lnb, wsp, bsp)
        y_a, y_b = jnp.concatenate(ya, axis=1), jnp.concatenate(yb, axis=1)
        ma, mb = bdot(y_a, wpa_ref), bdot(y_b, wpb_ref)
        ga, gb = p_ref[:, 4 * D:5 * D].astype(F32), p_ref[:, 5 * D:6 * D].astype(F32)
        merged, vjp2 = jax.vjp(_stage2, ma, mb, ga, gb)
        out = bdot(merged, wout_ref)
        gate = m_ref[:, 2 * D:3 * D] + bm_ref[:, 2 * D:3 * D]
        loss, vjp3 = jax.vjp(_stage3, out, x_ref[...], t_ref[...], gp_ref[...], gate)
        loss_ref[...] += jnp.broadcast_to(loss, loss_ref.shape)

        dout, dx, _, dgpost, dgate = vjp3(jnp.ones((1, 1), F32))
        dx_ref[...] = dx
        dmerged = bdot_t(dout, wout_ref)
        dma, dmb, dga, dgb = vjp2(dmerged)
        dya, dyb = bdot_t(dma, wpa_ref), bdot_t(dmb, wpb_ref)
        dzb, dua, dva, dza, do, dgon, dlng, dlnb, dwsp, dbsp = vjp1(
            ([dya[:, lanes(g)] for g in range(H)], [dyb[:, lanes(g)] for g in range(H)]))

        for blk, dlist in enumerate((dzb, dua, dva, dza)):
            for g in range(H):
                dp_ref[:, blk * D + g * DH: blk * D + (g + 1) * DH] = dlist[g].astype(BF16)
        dp_ref[:, 4 * D:5 * D] = dga.astype(BF16)
        dp_ref[:, 5 * D:6 * D] = dgb.astype(BF16)
        for g in range(H):
            do_ref[g] = do[g]
            dwsp_ref[g] += dwsp[g]
            dbsp_ref[g] += dbsp[g]
            dvec_ref[2:3, lanes(g)] += dlng[g]
            dvec_ref[3:4, lanes(g)] += dlnb[g]
        dvec_ref[0:1, :] += dgpost
        dvec_ref[1:2, :] += dgate
        dgon_ref[0:1, :] += dgon
        ya_ref[...] = y_a.astype(BF16)
        yb_ref[...] = y_b.astype(BF16)
        mg_ref[...] = merged.astype(BF16)
        dma_ref[...] = dma.astype(BF16)
        dmb_ref[...] = dmb.astype(BF16)
        dout_ref[...] = dout.astype(BF16)

    row = lambda w: pl.BlockSpec((GM, w), lambda i: (i, 0))
    heads = pl.BlockSpec((H, GM, DH), lambda i: (0, i, 0))
    full = lambda shape: pl.BlockSpec(shape, lambda i: tuple(0 for _ in shape))
    sds = jax.ShapeDtypeStruct
    return pl.pallas_call(
        body, name=name, grid=(n,),
        in_specs=[row(6 * D), heads, heads, row(D), row(D), full((1, 3 * D)), full((1, 3 * D)), full((1, DH)),
                  full((1, D)), full((1, D)), full((H, GM, GM)), full((H, GM, GM)),
                  full((D, D)), full((D, D)), full((D, D)), full((1, D))],
        out_specs=[full((8, DH)), row(6 * D), heads, row(D)] + [row(D)] * 6
        + [full((8, D)), full((8, DH)), full((H, GM, GM)), full((H, GM, GM))],
        out_shape=[sds((8, DH), F32), sds((rows, 6 * D), BF16), sds((H, rows, DH), F32), sds((rows, D), F32)]
        + [sds((rows, D), BF16)] * 6 + [sds((8, D), F32), sds((8, DH), F32), sds((H, GM, GM), F32), sds((H, GM, GM), F32)],
        compiler_params=_params(("arbitrary",)),
    )(p_rest, o_f, o_b, x, tgt, mod, bmod, gon, lng, lnb, wsp, bspb, wpa, wpb, wout, gpost)


def _dsilu_mul(c, d, *, name):
    def body(c_ref, d_ref, o_ref):
        _, vjp = jax.vjp(_silu, c_ref[...])
        (o_ref[...],) = vjp(d_ref[...])

    return pl.pallas_call(body, name=name, out_shape=jax.ShapeDtypeStruct(c.shape, F32))(c, d)


def _adamw(w, g, m, v, *, name):
    rows, cols = w.shape
    t = _tile(rows)
    c1 = 1.0 / (1.0 - ADAM_B1 ** ADAM_STEP)
    c2 = 1.0 / (1.0 - ADAM_B2 ** ADAM_STEP)

    def body(w_ref, g_ref, m_ref, v_ref, go_ref, d_ref, mo_ref, vo_ref):
        gv = g_ref[...]
        mn = ADAM_B1 * m_ref[...] + (1.0 - ADAM_B1) * gv
        vn = ADAM_B2 * v_ref[...] + (1.0 - ADAM_B2) * (gv * gv)
        go_ref[...] = gv
        mo_ref[...] = mn
        vo_ref[...] = vn
        d_ref[...] = -ADAM_LR * ((mn * c1) / (jnp.sqrt(vn * c2) + ADAM_EPS) + ADAM_WD * w_ref[...])

    tile = pl.BlockSpec((t, cols), lambda i: (i, 0))
    return pl.pallas_call(
        body, name=name, grid=(rows // t,),
        in_specs=[tile] * 4, out_specs=[tile] * 4,
        out_shape=[jax.ShapeDtypeStruct((rows, cols), F32)] * 4,
        compiler_params=_params(("parallel",)),
    )(w, g, m, v)


def _add_sibling(full, got, where, out_dtype, *, name):
    s, rows, cols = full.shape
    hr = rows // 2
    t = _tile(hr)
    nt = hr // t

    def body(w_ref, a_ref, b_ref, o_ref):
        o_ref[...] = (a_ref[...].astype(F32) + b_ref[...].astype(F32)).astype(out_dtype)

    tile = pl.BlockSpec((1, t, cols), lambda j, i, w: (j, i, 0))
    return pl.pallas_call(
        body, name=name,
        grid_spec=pltpu.PrefetchScalarGridSpec(
            num_scalar_prefetch=1, grid=(s, nt),
            in_specs=[pl.BlockSpec((1, t, cols), lambda j, i, w: (j, w[0] * nt + i, 0)), tile], out_specs=tile),
        out_shape=jax.ShapeDtypeStruct((s, hr, cols), out_dtype), compiler_params=_params(("parallel", "parallel")),
    )(where, full, got)


def _sum_chips(own, slots, where, n_out, which, *, name):
    _, hr, cols = own.shape
    t = _tile(hr)

    def body(w_ref, a_ref, s_ref, o_ref):
        f = lambda v: v.astype(F32)
        o_ref[0] = ((f(a_ref[0]) + f(s_ref[0])) + f(s_ref[1])) + f(s_ref[2])

    return pl.pallas_call(
        body, name=name,
        grid_spec=pltpu.PrefetchScalarGridSpec(
            num_scalar_prefetch=1, grid=(hr // t,),
            in_specs=[pl.BlockSpec((1, t, cols), lambda i, w: (w[1], i, 0)),
                      pl.BlockSpec((N_CHIPS - 1, t, cols), lambda i, w: (0, i, 0))],
            out_specs=pl.BlockSpec((1, t, cols), lambda i, w: (w[which], i, 0))),
        out_shape=jax.ShapeDtypeStruct((n_out, hr, cols), F32), compiler_params=_params(("parallel",)),
    )(where, own, slots)


def _local_step(x, c, ctx, c_ctx, tgt, w_mod, b_mod, g_pre, g_post, w_qkv, w_ab, w_rest, w_conv, a_log, dt_bias,
                g_onorm, gm_ln_g, gm_ln_b, w_sp, b_sp, w_pa, w_pb, w_out):
    rows, rows_c = x.shape[0], ctx.shape[0]
    cc = jnp.zeros((16, D), F32).at[0].set(c[0]).at[1].set(c_ctx)
    mod = _mm(cc, w_mod, pre_silu=True, b_sections=True, name="mod_fwd")
    mod_x, mod_c = mod[0:1], mod[1:2]
    avec = jnp.zeros((1, DH), F32).at[0, :2 * H].set(a_log.reshape(-1))
    dvec = jnp.zeros((1, DH), F32).at[0, :2 * H].set(dt_bias.reshape(-1))
    bspb = jnp.broadcast_to(b_sp[:, :, None], (H, GM, GM))
    w_kv, wconv_kv = w_qkv[:, D:], w_conv[:, D:]

    h_c = _norm_fwd(ctx, g_pre, mod_c, b_mod, name="norm_fwd_ctx")
    pc_kv = _mm(h_c, w_kv, name="inproj_ctx_kv")
    pc_ab = _mm(h_c, w_ab, name="inproj_ctx_ab")
    kv_c = _prep_fwd(pc_kv, wconv_kv, H, name="prep_fwd_ctx")
    gb_c = _gates_fwd(pc_ab, avec, dvec, name="gates_fwd_ctx")
    s_zero = jnp.zeros((H, DH, DH), F32)
    _, _, ssf_c, ssb_c, tsf_c, tsb_c, s_f, s_b = _gdn_fwd(kv_c, gb_c, s_zero, s_zero, False, name="gdn_fwd_ctx")

    h_x = _norm_fwd(x, g_pre, mod_x, b_mod, name="norm_fwd_x")
    p_qkv = _mm(h_x, w_qkv, name="inproj_qkv")
    p_ab = _mm(h_x, w_ab, name="inproj_ab")
    p_rest = _mm(h_x, w_rest, out_dtype=BF16, name="inproj_rest")
    qkv = _prep_fwd(p_qkv, w_conv, 2 * H, name="prep_fwd_x")
    gb_x = _gates_fwd(p_ab, avec, dvec, name="gates_fwd_x")
    o_f, o_b, ssf, ssb, tsf, tsb, _, _ = _gdn_fwd(qkv, gb_x, s_f, s_b, True, name="gdn_fwd_x")

    (loss_acc, dp_rest, do, dx_res, ya, yb, mg, dma, dmb, dout, dvec_post, dgon, dwsp, dbspb) = _post(
        p_rest, o_f, o_b, x, tgt, mod_x, b_mod, g_onorm, gm_ln_g, gm_ln_b, w_sp, bspb, w_pa, w_pb, w_out, g_post,
        name="post")
    g = {}
    g["w_pa"] = _mm(ya, dma, ta=True, name="dw_pa")
    g["w_pb"] = _mm(yb, dmb, ta=True, name="dw_pb")
    g["w_out"] = _mm(mg, dout, ta=True, name="dw_out")

    zeros_s = jnp.zeros((H, DH, DH), F32)
    dq_f, dq_b, dg_f, dg_b, ds0_f, ds0_b = _gdn_bwd(qkv, gb_x, ssf, ssb, tsf, tsb, do, zeros_s, zeros_s, True,
                                                    name="gdn_bwd_x")
    dp_qkv, dwc_x = _prep_bwd(p_qkv, w_conv, dq_f, dq_b, 2 * H, name="prep_bwd_x")
    dp_ab, dav_x, ddv_x = _gates_bwd(p_ab, avec, dvec, dg_f, dg_b, name="gates_bwd_x")
    dkv_f, dkv_b, dgc_f, dgc_b, _, _ = _gdn_bwd(kv_c, gb_c, ssf_c, ssb_c, tsf_c, tsb_c, jnp.zeros((H, rows_c, DH), F32),
                                                 ds0_f, ds0_b, False, name="gdn_bwd_ctx")
    dpc_kv, dwc_c = _prep_bwd(pc_kv, wconv_kv, dkv_f, dkv_b, H, name="prep_bwd_ctx")
    dpc_ab, dav_c, ddv_c = _gates_bwd(pc_ab, avec, dvec, dgc_f, dgc_b, name="gates_bwd_ctx")

    dw_kv_c = _mm(h_c, dpc_kv, ta=True, name="dw_kv_ctx")
    dw_qkv = _mm(h_x, dp_qkv, ta=True, tn=D, add=dw_kv_c, add_from=1, name="dw_qkv")
    dw_ab_c = _mm(h_c, dpc_ab, ta=True, name="dw_ab_ctx")
    dw_ab = _mm(h_x, dp_ab, ta=True, add=dw_ab_c, name="dw_ab")
    dw_rest = _mm(h_x, dp_rest, ta=True, name="dw_rest")
    g["w_in"] = (dw_qkv, dw_ab[:, :4 * H], dw_rest)
    dh_parts = [_mm(dp_qkv, w_qkv, tb=True, tk=3 * D // 2, name="dh_qkv"),
                _mm(dp_rest, w_rest, tb=True, tk=3 * D // 2, name="dh_rest"), _mm(dp_ab, w_ab, tb=True, name="dh_ab")]
    grad_x, dgpre_x, dm_x = _norm_bwd(x, g_pre, mod_x, b_mod, dh_parts, dx_res, name="norm_bwd_x")
    dhc_parts = [_mm(dpc_kv, w_kv, tb=True, name="dhc_kv"), _mm(dpc_ab, w_ab, tb=True, name="dhc_ab")]
    _, dgpre_c, dm_c = _norm_bwd(ctx, g_pre, mod_c, b_mod, dhc_parts, None, name="norm_bwd_ctx")

    dm_x = dm_x.at[:, 2 * D:].add(dvec_post[1:2])
    dmod = jnp.zeros((16, 3 * D), F32).at[0].set(dm_x[0]).at[1].set(dm_c[0])
    g["w_mod"] = _mm(cc, dmod, ta=True, pre_silu=True, tn=w_mod.shape[2], out_sections=True, name="dw_mod")
    dcc = _mm(dmod, w_mod, tb=True, b_sections=True, name="dcc")
    g["c_ctx"] = _dsilu_mul(cc[:8], dcc[:8], name="dc_ctx")[1]
    g["b_mod"] = dm_x + dm_c
    g["g_pre"] = dgpre_x + dgpre_c
    g["g_post"] = dvec_post[0:1]
    g["gm_ln_g"], g["gm_ln_b"] = dvec_post[2:3], dvec_post[3:4]
    g["g_onorm"] = dgon[0:1]
    g["w_sp"] = dwsp
    g["b_sp"] = jnp.sum(dbspb, axis=-1)
    g["w_conv"] = dwc_x.at[:, D:].add(dwc_c)
    g["a_log"] = (dav_x + dav_c)[0, :2 * H].reshape(2, H)
    g["dt_bias"] = (ddv_x + ddv_c)[0, :2 * H].reshape(2, H)
    return loss_acc[0, 0], grad_x, g


ANY = pl.BlockSpec(memory_space=pl.ANY)


def _place():
    x, y, c = lax.axis_index("x"), lax.axis_index("y"), lax.axis_index("c")
    chips = [(1 - x, y), (x, 1 - y), (1 - x, 1 - y)]
    return x, y, c, (x, y, 1 - c), chips


def _gather_shards(big, small, *, name):
    nb = len(big)

    def body(*refs):
        ins, sm_in = refs[:nb], refs[nb]
        outs, sm_out = refs[nb + 1:2 * nb + 1], refs[2 * nb + 1]
        send, recv = refs[2 * nb + 2:]
        x, y, c, sibling, chips = _place()
        mine = 2 * x + y

        def half(a, shard, hc):
            hr = big[a].shape[0] // 2
            return outs[a].at[shard, pl.ds(hc * hr, hr), :]

        def remote(k, src, dst, to):
            return pltpu.make_async_remote_copy(src_ref=src, dst_ref=dst, send_sem=send.at[k], recv_sem=recv.at[k],
                                                device_id=to, device_id_type=MESH)

        sends = []
        for a in range(nb):
            hr = big[a].shape[0] // 2
            for j, chip in enumerate(chips):
                sends.append(remote(a * 3 + j, ins[a].at[pl.ds(c * hr, hr), :], half(a, mine, c), (*chip, c)))
        for j, chip in enumerate(chips):
            sends.append(remote(nb * 3 + j, sm_in, sm_out.at[mine], (*chip, c)))
        for cp in sends:
            cp.start()
        base = nb * 3 + 3
        passed = []
        for a in range(nb):
            for j, (px, py) in enumerate(chips):
                theirs = 2 * px + py
                remote(a * 3 + j, half(a, theirs, c), half(a, theirs, c), sibling).wait_recv()
                fw = remote(base + a * 3 + j, half(a, theirs, c), half(a, theirs, c), sibling)
                fw.start()
                passed.append(fw)
        for a in range(nb):
            for j, (px, py) in enumerate(chips):
                theirs = 2 * px + py
                remote(base + a * 3 + j, half(a, theirs, 1 - c), half(a, theirs, 1 - c), sibling).wait_recv()
        for j, (px, py) in enumerate(chips):
            remote(nb * 3 + j, sm_in, sm_out.at[2 * px + py], sibling).wait_recv()
        for cp in sends + passed:
            cp.wait_send()

    n_remote = 2 * nb * 3 + 3
    outs = pl.pallas_call(
        body, name=name, in_specs=[ANY] * (nb + 1), out_specs=[ANY] * (nb + 1),
        out_shape=[jax.ShapeDtypeStruct((N_CHIPS,) + a.shape, a.dtype) for a in big + [small]],
        scratch_shapes=[pltpu.SemaphoreType.DMA((n_remote,)), pltpu.SemaphoreType.DMA((n_remote,))],
    )(*big, small)
    return outs[:nb], outs[nb]


def _row_chunks(rows, row_bytes, align=8):
    n = max(1, min(rows // align, -(-rows * row_bytes // DMA_CHUNK_BYTES)))
    per = -(-(-(-rows // n)) // align) * align
    return [(r, min(per, rows - r)) for r in range(0, rows, per)]


def _remote(src, dst, send, recv, to):
    return pltpu.make_async_remote_copy(src_ref=src, dst_ref=dst, send_sem=send, recv_sem=recv, device_id=to,
                                        device_id_type=MESH)


def _exchange_halves(arrs, *, name):
    na = len(arrs)

    def body(*refs):
        ins, got = refs[:na], refs[na:2 * na]
        send, recv = refs[2 * na:]
        x, y, c, sibling, _ = _place()
        for a in range(na):
            ns, rows, cols = arrs[a].shape
            hr = rows // 2
            for s in range(ns):
                for r0, nr in _row_chunks(hr, cols * arrs[a].dtype.itemsize, 16):
                    _remote(ins[a].at[s, pl.ds((1 - c) * hr + r0, nr), :], got[a].at[s, pl.ds(r0, nr), :],
                            send.at[a], recv.at[a], sibling).start()
        for a in range(na):
            hr = arrs[a].shape[1] // 2
            _remote(ins[a].at[:, pl.ds((1 - c) * hr, hr), :], got[a], send.at[a], recv.at[a], sibling).wait()

    return pl.pallas_call(
        body, name=name, in_specs=[ANY] * na, out_specs=[ANY] * na,
        out_shape=[jax.ShapeDtypeStruct((N_CHIPS, a.shape[1] // 2, a.shape[2]), a.dtype) for a in arrs],
        scratch_shapes=[pltpu.SemaphoreType.DMA((na,)), pltpu.SemaphoreType.DMA((na,))],
    )(*arrs)


def _scatter_sections(arrs, *, name):
    na = len(arrs)

    def body(*refs):
        ins, outs = refs[:na], refs[na:2 * na]
        send, recv = refs[2 * na:]
        x, y, c, _, chips = _place()
        for a in range(na):
            _, hr, cols = arrs[a].shape
            for j, (px, py) in enumerate(chips):
                for r0, nr in _row_chunks(hr, cols * arrs[a].dtype.itemsize, 16):
                    _remote(ins[a].at[2 * px + py, pl.ds(r0, nr), :], outs[a].at[j, pl.ds(r0, nr), :],
                            send.at[a * 3 + j], recv.at[a * 3 + j], (px, py, c)).start()
        for a in range(na):
            for j, (px, py) in enumerate(chips):
                _remote(ins[a].at[2 * px + py], outs[a].at[j], send.at[a * 3 + j], recv.at[a * 3 + j], (px, py, c)).wait()

    return pl.pallas_call(
        body, name=name, in_specs=[ANY] * na, out_specs=[ANY] * na,
        out_shape=[jax.ShapeDtypeStruct((N_CHIPS - 1,) + a.shape[1:], a.dtype) for a in arrs],
        scratch_shapes=[pltpu.SemaphoreType.DMA((3 * na,)), pltpu.SemaphoreType.DMA((3 * na,))],
    )(*arrs)


def _finish_reduce(big, small, *, name):
    nb = len(big)

    def body(*refs):
        outs, sm = refs[nb + 1:2 * nb + 1], refs[2 * nb + 1]
        send, recv = refs[2 * nb + 2:]
        x, y, c, sibling, chips = _place()
        blk = lambda px, py, pc: sm.at[4 * px + 2 * py + pc]
        for a in range(nb):
            _, hr, cols = big[a].shape
            for r0, nr in _row_chunks(hr, cols * 4):
                _remote(outs[a].at[c, pl.ds(r0, nr), :], outs[a].at[c, pl.ds(r0, nr), :], send.at[a], recv.at[a],
                        sibling).start()
        first = [_remote(blk(x, y, c), blk(x, y, c), send.at[nb], recv.at[nb], sibling)]
        first += [_remote(blk(x, y, c), blk(x, y, c), send.at[nb + 1 + j], recv.at[nb + 1 + j], (*chip, c))
                  for j, chip in enumerate(chips)]
        for cp in first:
            cp.start()
        passed = []
        for j, (px, py) in enumerate(chips):
            _remote(blk(px, py, c), blk(px, py, c), send.at[nb + 1 + j], recv.at[nb + 1 + j], sibling).wait_recv()
            fw = _remote(blk(px, py, c), blk(px, py, c), send.at[nb + 4 + j], recv.at[nb + 4 + j], sibling)
            fw.start()
            passed.append(fw)
        for a in range(nb):
            _remote(outs[a].at[c], outs[a].at[1 - c], send.at[a], recv.at[a], sibling).wait()
        _remote(blk(x, y, c), blk(x, y, 1 - c), send.at[nb], recv.at[nb], sibling).wait_recv()
        for j, (px, py) in enumerate(chips):
            _remote(blk(px, py, c), blk(px, py, 1 - c), send.at[nb + 4 + j], recv.at[nb + 4 + j], sibling).wait_recv()
        for cp in first + passed:
            cp.wait_send()

    n_remote = nb + 7
    arrs = list(big) + [small]
    outs = pl.pallas_call(
        body, name=name, in_specs=[ANY] * (nb + 1), out_specs=[ANY] * (nb + 1),
        out_shape=[jax.ShapeDtypeStruct(a.shape, F32) for a in arrs],
        input_output_aliases={i: i for i in range(nb + 1)},
        scratch_shapes=[pltpu.SemaphoreType.DMA((n_remote,)), pltpu.SemaphoreType.DMA((n_remote,))],
    )(*arrs)
    return outs[:nb], outs[nb]


def _reduce_gradients(sectioned, small, where):
    arrs = list(sectioned) + [small]
    got = _exchange_halves(arrs, name="rs_exchange_halves")
    pay = [BF16] * len(sectioned) + [F32]
    chip_sum = [_add_sibling(a, g, where, p, name=f"rs_add_sibling_{i}")
                for i, (a, g, p) in enumerate(zip(arrs, got, pay))]
    slots = _scatter_sections(chip_sum, name="rs_scatter_sections")
    red = [_sum_chips(p, s, where, 2, 0, name=f"rs_sum_chips_{i}")
           for i, (p, s) in enumerate(zip(chip_sum[:-1], slots[:-1]))]
    red_small = _sum_chips(chip_sum[-1], slots[-1], where, 2 * N_CHIPS, 2, name="rs_sum_chips_small")
    big, sm = _finish_reduce(red, red_small, name="rs_finish")
    return [b.reshape(-1, b.shape[-1]) for b in big], sm.reshape(-1, sm.shape[-1])


def kernel(x, c, ctx, c_ctx, w_mod, b_mod, g_pre, g_post, w_in, w_conv, a_log, dt_bias, g_onorm, gm_ln_g, gm_ln_b, w_sp, b_sp, w_pa, w_pb, w_out, loss_target, m_c_ctx, m_w_mod, m_b_mod, m_g_pre, m_g_post, m_w_in, m_w_conv, m_a_log, m_dt_bias, m_g_onorm, m_gm_ln_g, m_gm_ln_b, m_w_sp, m_b_sp, m_w_pa, m_w_pb, m_w_out, v_c_ctx, v_w_mod, v_b_mod, v_g_pre, v_g_post, v_w_in, v_w_conv, v_a_log, v_dt_bias, v_g_onorm, v_gm_ln_g, v_gm_ln_b, v_w_sp, v_b_sp, v_w_pa, v_w_pb, v_w_out):
    names = ["c_ctx", "w_mod", "b_mod", "g_pre", "g_post", "w_in", "w_conv", "a_log", "dt_bias", "g_onorm", "gm_ln_g",
             "gm_ln_b", "w_sp", "b_sp", "w_pa", "w_pb", "w_out"]
    w = dict(zip(names, (c_ctx, w_mod, b_mod, g_pre, g_post, w_in, w_conv, a_log, dt_bias, g_onorm, gm_ln_g, gm_ln_b,
                         w_sp, b_sp, w_pa, w_pb, w_out)))
    m = dict(zip(names, (m_c_ctx, m_w_mod, m_b_mod, m_g_pre, m_g_post, m_w_in, m_w_conv, m_a_log, m_dt_bias, m_g_onorm,
                         m_gm_ln_g, m_gm_ln_b, m_w_sp, m_b_sp, m_w_pa, m_w_pb, m_w_out)))
    v = dict(zip(names, (v_c_ctx, v_w_mod, v_b_mod, v_g_pre, v_g_post, v_w_in, v_w_conv, v_a_log, v_dt_bias, v_g_onorm,
                         v_gm_ln_g, v_gm_ln_b, v_w_sp, v_b_sp, v_w_pa, v_w_pb, v_w_out)))
    xy = 2 * lax.axis_index("x") + lax.axis_index("y")
    where = jnp.stack([lax.axis_index("c"), xy, 2 * xy + lax.axis_index("c")]).astype(jnp.int32)

    shards = [a[0].astype(BF16) for a in (w_mod, w_in, w_pa, w_pb, w_out)]
    gathered, wconv_all = _gather_shards(shards, w_conv[0], name="gather_weights")
    own = lambda full, shard: lax.dynamic_update_slice(full, shard[None], (xy, 0, 0))
    wm_all, win_all, wpa_all, wpb_all, wout_all = [own(f, s) for f, s in zip(gathered, shards)]
    wconv_all = own(wconv_all, w_conv[0])
    w_conv_f = jnp.concatenate([wconv_all[s] for s in range(N_CHIPS)], axis=1)
    shard_cols = IN_COLS // N_CHIPS
    cut = OFF_A - shard_cols
    assert 0 < cut and cut + 4 * H < shard_cols
    w_qkv = jnp.concatenate([win_all[0], win_all[1][:, :cut]], axis=1)
    w_ab = jnp.pad(win_all[1][:, cut:cut + 4 * H], ((0, 0), (0, DH - 4 * H)))
    w_rest = jnp.concatenate([win_all[1][:, cut + 4 * H:], win_all[2], win_all[3]], axis=1)

    loss_local, grad_x, g = _local_step(
        x[0], c, ctx[0], c_ctx, loss_target[0], wm_all, b_mod, g_pre, g_post, w_qkv, w_ab, w_rest,
        w_conv_f, a_log[0], dt_bias[0], g_onorm, gm_ln_g, gm_ln_b, w_sp[0], b_sp[0],
        wpa_all.reshape(D, D), wpb_all.reshape(D, D), wout_all.reshape(D, D))
    g["loss"] = loss_local
    blk = D // N_CHIPS
    dw_qkv, dw_ab, dw_rest = g["w_in"]
    rest_cut = shard_cols - cut - 4 * H
    g_win = jnp.stack([dw_qkv[:, :shard_cols],
                       jnp.concatenate([dw_qkv[:, shard_cols:], dw_ab, dw_rest[:, :rest_cut]], axis=1),
                       dw_rest[:, rest_cut:rest_cut + shard_cols], dw_rest[:, rest_cut + shard_cols:]]).astype(BF16)
    g["_pad"] = jnp.zeros((SMALL_LAYOUT["_pad"][1],), F32)
    tail = jnp.zeros((N_CHIPS * SMALL_ROWS * 128 - sum(s for _, s in SMALL_LAYOUT.values()),), F32)
    flat = jnp.concatenate([g[k].reshape(-1) for k in SMALL_LAYOUT] + [tail])
    big_names = ("w_mod", "w_in", "w_pa", "w_pb", "w_out")
    sectioned = [g["w_mod"], g_win] + [g[k].reshape(N_CHIPS, blk, D) for k in ("w_pa", "w_pb", "w_out")]
    reduced, gr_small = _reduce_gradients(sectioned, flat.reshape(N_CHIPS, SMALL_ROWS, 128), where)
    gr_tiny = gr_small[TINY_ROW0:]

    def entry(arr, k):
        off, size = SMALL_LAYOUT[k]
        row, col = off // 128 - TINY_ROW0, off % 128
        return arr[row:row + size // 128].reshape(-1) if size >= 128 else arr[row, col:col + size]

    res = {k: _adamw(w[k][0], gr, m[k][0], v[k][0], name=f"adamw_{k}") for k, gr in zip(big_names, reduced)}
    res["w_sp"] = _adamw(w_sp.reshape(-1, 128), gr_small, m_w_sp.reshape(-1, 128), v_w_sp.reshape(-1, 128),
                         name="adamw_w_sp")
    tiny_names = [k for k in SMALL_LAYOUT if k not in ("w_sp", "w_conv", "loss", "_pad")]

    def pack(src):
        parts = [src[k].reshape(-1) if k in tiny_names else jnp.zeros((s,), F32)
                 for k, (_, s) in SMALL_LAYOUT.items() if k != "w_sp"]
        return jnp.concatenate(parts + [tail]).reshape(-1, 128)

    tiny_res = _adamw(pack(w), gr_tiny, pack(m), pack(v), name="adamw_tiny")
    for k in tiny_names:
        res[k] = [entry(r, k) for r in tiny_res]
    g_conv = lax.dynamic_slice(entry(gr_tiny, "w_conv").reshape(3, 3 * D), (0, xy * (3 * D // N_CHIPS)),
                               (3, 3 * D // N_CHIPS))
    conv_res = _adamw(jnp.pad(w_conv[0], ((0, 5), (0, 0))), jnp.pad(g_conv, ((0, 5), (0, 0))),
                      jnp.pad(m_w_conv[0], ((0, 5), (0, 0))), jnp.pad(v_w_conv[0], ((0, 5), (0, 0))), name="adamw_w_conv")
    res["w_conv"] = [r[:3] for r in conv_res]
    res = {k: [r.reshape(w[k].shape) for r in res[k]] for k in names}

    out = [entry(gr_tiny, "loss").reshape(()), grad_x[None]]
    for i in range(4):
        out += [res[k][i] for k in names]
    return tuple(out)
```

```python
import functools

import jax
import jax.numpy as jnp
from jax import lax
from jax.experimental import pallas as pl
from jax.experimental.pallas import tpu as pltpu

F32 = jnp.float32
BF16 = jnp.bfloat16
HI = lax.Precision.HIGHEST
MESH = pl.DeviceIdType.MESH

D = 1024
H = 8
DH = 128
CH = 64
LOG_CH = 6
PAIR = 2 * CH
GM = 128
assert 1 << LOG_CH == CH and PAIR == DH
PREC_POWERS = (lax.Precision.HIGH,) * 3 + (None,) * 2
assert len(PREC_POWERS) == LOG_CH - 1
HEADS_PER_ITER_FWD = 8
HEADS_PER_ITER_BWD = 8
EPS = 1e-6
N_CHIPS = 4
OFF_A = 3 * D
OFF_ZB = OFF_A + 4 * H
IN_COLS = OFF_ZB + 6 * D
VMEM_LIMIT_V7X = 56 * 1024 * 1024
DMA_CHUNK_BYTES = 2 * 1024 * 1024

ADAM_LR, ADAM_B1, ADAM_B2, ADAM_EPS, ADAM_WD, ADAM_STEP = 0.001, 0.9, 0.999, 1e-08, 0.01, 10

SMALL_LAYOUT = {}
_off = 0
for _n, _s in (("w_sp", H * GM * GM), ("w_conv", 3 * 3 * D), ("c_ctx", D), ("b_mod", 3 * D), ("g_pre", D), ("g_post", D),
               ("gm_ln_g", D), ("gm_ln_b", D), ("b_sp", H * GM), ("g_onorm", DH), ("a_log", 2 * H), ("dt_bias", 2 * H),
               ("loss", 1), ("_pad", 128 - 4 * H - 1)):
    SMALL_LAYOUT[_n] = (_off, _s)
    _off += _s
SMALL_ROWS = 304
assert N_CHIPS * SMALL_ROWS * 128 >= _off and (SMALL_ROWS // 2) % 8 == 0 and _off % 128 == 0
TINY_ROW0 = SMALL_LAYOUT["w_conv"][0] // 128


def _params(sem=None):
    return pltpu.CompilerParams(dimension_semantics=sem, vmem_limit_bytes=VMEM_LIMIT_V7X)


def _tile(n, cands=(256, 128, 64, 32, 16, 8)):
    if n <= cands[0]:
        return n
    for cand in cands:
        if n % cand == 0:
            return cand
    return n


def _silu(x):
    return x * jax.nn.sigmoid(x)


def _gelu(x):
    return 0.5 * x * (1.0 + jnp.tanh(0.7978845608028654 * (x + 0.044715 * (x * x * x))))


def _mm(a, b, *, ta=False, tb=False, out_dtype=F32, tm=1024, tn=1024, tk=1024, pre_silu=False, add=None, add_from=0,
        b_sections=False, out_sections=False, name):
    m, k = (a.shape[1], a.shape[0]) if ta else a.shape
    if b_sections:
        sect = b.shape[2]
        n = b.shape[1] if tb else b.shape[0] * sect
        tn, tk = (tn, sect) if tb else (sect, tk)
    else:
        n = b.shape[0] if tb else b.shape[1]
    tm, tn, tk = min(tm, m), min(tn, n), min(tk, k)
    assert m % tm == 0 and n % tn == 0 and k % tk == 0, (name, m, n, k, tm, tn, tk)
    nk = k // tk
    dims = (((0,) if ta else (1,), (1,) if tb else (0,)), ((), ()))
    has_add = add is not None
    assert not has_add or add.shape == (m, n - add_from * tn), (name, add.shape)

    def body(*refs):
        a_ref, b_ref = refs[:2]
        o_ref = refs[2 + int(has_add)]
        acc_ref = refs[-1]
        kk = pl.program_id(2)
        av = a_ref[...]
        if pre_silu:
            av = _silu(av.astype(F32))
        bv = b_ref[0] if b_sections else b_ref[...]
        part = lax.dot_general(av.astype(BF16), bv.astype(BF16), dims, preferred_element_type=F32)

        def finish(res):
            if has_add:
                res = res + jnp.where(pl.program_id(1) >= add_from, refs[2][...], 0.0)
            if out_sections:
                o_ref[0] = res.astype(out_dtype)
            else:
                o_ref[...] = res.astype(out_dtype)

        if nk == 1:
            finish(part)
            return

        @pl.when(kk == 0)
        def _():
            acc_ref[...] = part

        @pl.when((kk > 0) & (kk < nk - 1))
        def _():
            acc_ref[...] += part

        @pl.when(kk == nk - 1)
        def _():
            finish(acc_ref[...] + part)

    a_spec = pl.BlockSpec((tk, tm), lambda i, j, q: (q, i)) if ta else pl.BlockSpec((tm, tk), lambda i, j, q: (i, q))
    if b_sections:
        b_spec = (pl.BlockSpec((1, tn, tk), lambda i, j, q: (q, j, 0)) if tb
                  else pl.BlockSpec((1, tk, tn), lambda i, j, q: (j, q, 0)))
    else:
        b_spec = pl.BlockSpec((tn, tk), lambda i, j, q: (j, q)) if tb else pl.BlockSpec((tk, tn), lambda i, j, q: (q, j))
    add_spec = [pl.BlockSpec((tm, tn), lambda i, j, q: (i, jnp.maximum(j - add_from, 0)))] if has_add else []
    if out_sections:
        out_spec, out_shape = pl.BlockSpec((1, tm, tn), lambda i, j, q: (j, i, 0)), (n // tn, m, tn)
    else:
        out_spec, out_shape = pl.BlockSpec((tm, tn), lambda i, j, q: (i, j)), (m, n)
    return pl.pallas_call(
        body, name=name, grid=(m // tm, n // tn, nk),
        in_specs=[a_spec, b_spec] + add_spec, out_specs=out_spec,
        out_shape=jax.ShapeDtypeStruct(out_shape, out_dtype),
        scratch_shapes=[pltpu.VMEM((tm, tn), F32)] if nk > 1 else [],
        compiler_params=_params(("parallel", "parallel", "arbitrary")),
    )(*([a, b] + ([add] if has_add else [])))


def _h_fn(x, g, m):
    shift, scale = m[:, 0:D], m[:, D:2 * D]
    r = lax.rsqrt(jnp.mean(x * x, axis=-1, keepdims=True) + EPS)
    return (x * r * g) * (1.0 + scale) + shift


def _norm_fwd(x, g, mod, bmod, *, name):
    rows = x.shape[0]
    t = min(512, rows)

    def body(x_ref, g_ref, m_ref, b_ref, h_ref):
        h_ref[...] = _h_fn(x_ref[...], g_ref[...], m_ref[...] + b_ref[...]).astype(BF16)

    vec = lambda w: pl.BlockSpec((1, w), lambda i: (0, 0))
    return pl.pallas_call(
        body, name=name, grid=(rows // t,),
        in_specs=[pl.BlockSpec((t, D), lambda i: (i, 0)), vec(D), vec(3 * D), vec(3 * D)],
        out_specs=pl.BlockSpec((t, D), lambda i: (i, 0)),
        out_shape=jax.ShapeDtypeStruct((rows, D), BF16),
        compiler_params=_params(("parallel",)),
    )(x, g, mod, bmod)


def _norm_bwd(x, g, mod, bmod, dh_parts, resid, *, name):
    rows = x.shape[0]
    t = min(512, rows)
    n_parts = len(dh_parts)
    has_resid = resid is not None

    def body(*refs):
        x_ref, g_ref, m_ref, b_ref = refs[:4]
        parts = refs[4:4 + n_parts]
        r_ref = refs[4 + n_parts] if has_resid else None
        dx_ref, dg_ref, dm_ref = refs[-3:]
        i = pl.program_id(0)
        dh = parts[0][...]
        for p in parts[1:]:
            dh = dh + p[...]
        _, vjp = jax.vjp(_h_fn, x_ref[...], g_ref[...], m_ref[...] + b_ref[...])
        dx, dg, dm = vjp(dh)
        if has_resid:
            dx = dx + r_ref[...]
        dx_ref[...] = dx

        @pl.when(i == 0)
        def _():
            dg_ref[...] = dg
            dm_ref[...] = dm

        @pl.when(i > 0)
        def _():
            dg_ref[...] += dg
            dm_ref[...] += dm

    vec = lambda w: pl.BlockSpec((1, w), lambda i: (0, 0))
    tile = pl.BlockSpec((t, D), lambda i: (i, 0))
    ins = [x, g, mod, bmod, *dh_parts] + ([resid] if has_resid else [])
    return pl.pallas_call(
        body, name=name, grid=(rows // t,),
        in_specs=[tile, vec(D), vec(3 * D), vec(3 * D)] + [tile] * (n_parts + int(has_resid)),
        out_specs=[tile, vec(D), vec(3 * D)],
        out_shape=[jax.ShapeDtypeStruct((rows, D), F32), jax.ShapeDtypeStruct((1, D), F32),
                   jax.ShapeDtypeStruct((1, 3 * D), F32)],
        compiler_params=_params(("arbitrary",)),
    )(*ins)


def _conv_tile(u_ref, r0, t, rows, w0, w1, w2):
    u = u_ref[pl.ds(r0, t), :]
    prev8 = u_ref[pl.ds(pl.multiple_of(jnp.maximum(r0 - 8, 0), 8), 8), :]
    next8 = u_ref[pl.ds(pl.multiple_of(jnp.minimum(r0 + t, rows - 8), 8), 8), :]
    r8 = lax.broadcasted_iota(jnp.int32, (8, DH), 0)
    prev_row = jnp.sum(jnp.where(r8 == 7, prev8, 0.0), axis=0, keepdims=True)
    next_row = jnp.sum(jnp.where(r8 == 0, next8, 0.0), axis=0, keepdims=True)
    prev_row = jnp.where(r0 > 0, prev_row, 0.0)
    next_row = jnp.where(r0 + t < rows, next_row, 0.0)
    ri = lax.broadcasted_iota(jnp.int32, (t, DH), 0)
    um1 = jnp.where(ri == 0, prev_row, pltpu.roll(u, 1, 0))
    up1 = jnp.where(ri == t - 1, next_row, pltpu.roll(u, t - 1, 0))
    return w0 * um1 + w1 * u + w2 * up1, um1, u, up1


def _rowlocal(z, is_norm):
    y = _silu(z)
    yn = y * lax.rsqrt(jnp.sum(y * y, axis=-1, keepdims=True) + EPS)
    return jnp.where(is_norm, yn, y)


def _prep_fwd(p, wconv, n_norm, *, name):
    rows, nb = p.shape[0], p.shape[1] // DH
    t = min(512, rows)

    def body(u_ref, w_ref, o_ref):
        is_norm = pl.program_id(0) < n_norm
        w0, w1, w2 = w_ref[0:1, :], w_ref[1:2, :], w_ref[2:3, :]

        def step(s, carry):
            r0 = pl.multiple_of(s * t, t)
            z, _, _, _ = _conv_tile(u_ref, r0, t, rows, w0, w1, w2)
            o_ref[0, pl.ds(r0, t), :] = _rowlocal(z, is_norm)
            return carry

        lax.fori_loop(0, rows // t, step, 0)

    return pl.pallas_call(
        body, name=name, grid=(nb,),
        in_specs=[pl.BlockSpec((rows, DH), lambda j: (0, j)), pl.BlockSpec((3, DH), lambda j: (0, j))],
        out_specs=pl.BlockSpec((1, rows, DH), lambda j: (j, 0, 0)),
        out_shape=jax.ShapeDtypeStruct((nb, rows, DH), F32),
        compiler_params=_params(("parallel",)),
    )(p, wconv)


def _prep_bwd(p, wconv, d_a, d_b, n_norm, *, name):
    rows, nb = p.shape[0], p.shape[1] // DH
    t = min(512, rows)

    def body(u_ref, w_ref, da_ref, db_ref, du_ref, dw_ref, dz_ref):
        is_norm = pl.program_id(0) < n_norm
        w0, w1, w2 = w_ref[0:1, :], w_ref[1:2, :], w_ref[2:3, :]

        def step1(s, carry):
            a0, a1, a2 = carry
            r0 = pl.multiple_of(s * t, t)
            z, um1, u, up1 = _conv_tile(u_ref, r0, t, rows, w0, w1, w2)
            _, vjp = jax.vjp(lambda zz: _rowlocal(zz, is_norm), z)
            (dz,) = vjp(da_ref[0, pl.ds(r0, t), :] + db_ref[0, pl.ds(r0, t), :])
            dz_ref[pl.ds(r0, t), :] = dz
            red = lambda v: jnp.sum(v, axis=0, keepdims=True)
            return a0 + red(dz * um1), a1 + red(dz * u), a2 + red(dz * up1)

        zero = jnp.zeros((1, DH), F32)
        a0, a1, a2 = lax.fori_loop(0, rows // t, step1, (zero, zero, zero))
        dw_ref[0:1, :] = a0
        dw_ref[1:2, :] = a1
        dw_ref[2:3, :] = a2

        def step2(s, carry):
            r0 = pl.multiple_of(s * t, t)
            du, _, _, _ = _conv_tile(dz_ref, r0, t, rows, w2, w1, w0)
            du_ref[pl.ds(r0, t), :] = du.astype(BF16)
            return carry

        lax.fori_loop(0, rows // t, step2, 0)

    col = pl.BlockSpec((rows, DH), lambda j: (0, j))
    w_spec = pl.BlockSpec((3, DH), lambda j: (0, j))
    d_spec = pl.BlockSpec((1, rows, DH), lambda j: (j, 0, 0))
    return pl.pallas_call(
        body, name=name, grid=(nb,),
        in_specs=[col, w_spec, d_spec, d_spec], out_specs=[col, w_spec],
        out_shape=[jax.ShapeDtypeStruct((rows, nb * DH), BF16), jax.ShapeDtypeStruct((3, nb * DH), F32)],
        scratch_shapes=[pltpu.VMEM((rows, DH), F32)],
        compiler_params=_params(("parallel",)),
    )(p, wconv, d_a, d_b)


def _gates_fn(pab, avec, dvec):
    t = pab.shape[0]
    lane = lax.broadcasted_iota(jnp.int32, pab.shape, 1)
    xg = pab + dvec
    sp = jnp.maximum(xg, 0.0) + jnp.log(1.0 + jnp.exp(-jnp.abs(xg)))
    g = jnp.where(lane < 2 * H, -jnp.exp(avec) * sp, 0.0)
    ii = lax.broadcasted_iota(jnp.int32, (t, t), 0)
    jj = lax.broadcasted_iota(jnp.int32, (t, t), 1)
    same = (ii >> LOG_CH) == (jj >> LOG_CH)
    cum_f = _dot(jnp.where(same & (jj <= ii), 1.0, 0.0), g, precision=HI)
    cum_r = _dot(jnp.where(same & (jj >= ii), 1.0, 0.0), g, precision=HI)
    return jnp.where(lane < H, cum_f, jnp.where(lane < 2 * H, cum_r, jnp.where(lane < 4 * H, jax.nn.sigmoid(pab), 0.0)))


def _gates_fwd(pab, avec, dvec, *, name):
    rows = pab.shape[0]
    t = min(512, rows)

    def body(p_ref, a_ref, d_ref, o_ref):
        o_ref[...] = _gates_fn(p_ref[...], a_ref[...], d_ref[...])

    vec = pl.BlockSpec((1, DH), lambda i: (0, 0))
    tile = pl.BlockSpec((t, DH), lambda i: (i, 0))
    return pl.pallas_call(
        body, name=name, grid=(rows // t,), in_specs=[tile, vec, vec], out_specs=tile,
        out_shape=jax.ShapeDtypeStruct((rows, DH), F32), compiler_params=_params(("parallel",)),
    )(pab, avec, dvec)


def _gates_bwd(pab, avec, dvec, d_a, d_b, *, name):
    rows = pab.shape[0]
    t = min(512, rows)

    def body(p_ref, a_ref, d_ref, da_ref, db_ref, dp_ref, dav_ref, ddv_ref):
        i = pl.program_id(0)
        _, vjp = jax.vjp(_gates_fn, p_ref[...], a_ref[...], d_ref[...])
        dp, dav, ddv = vjp(da_ref[...] + db_ref[...])
        dp_ref[...] = dp.astype(BF16)

        @pl.when(i == 0)
        def _():
            dav_ref[...] = dav
            ddv_ref[...] = ddv

        @pl.when(i > 0)
        def _():
            dav_ref[...] += dav
            ddv_ref[...] += ddv

    vec = pl.BlockSpec((1, DH), lambda i: (0, 0))
    tile = pl.BlockSpec((t, DH), lambda i: (i, 0))
    return pl.pallas_call(
        body, name=name, grid=(rows // t,), in_specs=[tile, vec, vec, tile, tile], out_specs=[tile, vec, vec],
        out_shape=[jax.ShapeDtypeStruct((rows, DH), BF16), jax.ShapeDtypeStruct((1, DH), F32),
                   jax.ShapeDtypeStruct((1, DH), F32)],
        compiler_params=_params(("arbitrary",)),
    )(pab, avec, dvec, d_a, d_b)


def _dot_general(a, b, dims, precision=None):
    return lax.dot_general(a, b, (dims, ((), ())), precision=precision, preferred_element_type=F32)


@functools.partial(jax.custom_vjp, nondiff_argnums=(2,))
def _dot_bf16(a, b, dims):
    return _dot_general(a.astype(BF16), b.astype(BF16), dims)


def _dot_bf16_fwd(a, b, dims):
    return _dot_bf16(a, b, dims), (a, b)


def _dot_bf16_bwd(dims, res, g):
    a, b = res
    (ca,), (cb,) = dims
    da = _dot_bf16(g, b, ((1,), (1 - cb,))) if ca == 1 else _dot_bf16(b, g, ((1 - cb,), (1,)))
    db = _dot_bf16(a, g, ((1 - ca,), (0,))) if cb == 0 else _dot_bf16(g, a, ((0,), (1 - ca,)))
    return da, db


_dot_bf16.defvjp(_dot_bf16_fwd, _dot_bf16_bwd)


def _dot(a, b, dims=((1,), (0,)), precision=None):
    if precision is None and a.dtype == F32 and b.dtype == F32:
        return _dot_bf16(a, b, dims)
    return _dot_general(a, b, dims, precision)


_NT = ((1,), (1,))
_TN = ((0,), (0,))


@jax.custom_vjp
def _saved_inverse(neg_a, inv):
    return inv


def _saved_inverse_fwd(neg_a, inv):
    return inv, inv


def _saved_inverse_bwd(inv, d_inv):
    idx = range(len(inv))
    left = [_dot(inv[i], d_inv[i], _TN) for i in idx]
    d_neg_a = [_dot(left[i], inv[i], _NT) for i in idx]
    return d_neg_a, [jnp.zeros_like(t) for t in inv]


_saved_inverse.defvjp(_saved_inverse_fwd, _saved_inverse_bwd)


def _pairs(s, q, k, v, gcol, bcol, revs, inv_saved=None):
    idx = range(len(revs))
    ii = lax.broadcasted_iota(jnp.int32, (PAIR, PAIR), 0)
    jj = lax.broadcasted_iota(jnp.int32, (PAIR, PAIR), 1)
    same = (ii >> LOG_CH) == (jj >> LOG_CH)
    incl_d = (same & (ii >= jj), same & (ii <= jj))
    strict_d = (same & (ii > jj), same & (ii < jj))
    incl = [incl_d[int(r)] for r in revs]
    strict = [strict_d[int(r)] for r in revs]
    eye = jnp.where(ii == jj, 1.0, 0.0)
    gc_i = [jnp.broadcast_to(gcol[i], (PAIR, DH)) for i in idx]
    gc_j = [gc_i[i].T for i in idx]
    decay = [jnp.where(incl[i], jnp.exp(jnp.where(incl[i], gc_i[i] - gc_j[i], 0.0)), 0.0) for i in idx]
    b_b = [jnp.broadcast_to(bcol[i], (PAIR, DH)) for i in idx]
    kb = [k[i] * b_b[i] for i in idx]
    kk = [_dot(kb[i], k[i], _NT) for i in idx]
    bp = [jnp.where(strict[i], -kk[i] * decay[i], 0.0) for i in idx]
    if inv_saved is not None:
        inv = _saved_inverse(bp, inv_saved)
    else:
        inv = [eye + bp[i] for i in idx]
        for prec in PREC_POWERS:
            bp = [_dot(bp[i], bp[i], precision=prec) for i in idx]
            more = [_dot(inv[i], bp[i], precision=prec) for i in idx]
            inv = [inv[i] + more[i] for i in idx]
    eg = [jnp.exp(gc_i[i]) for i in idx]
    sol = [_dot(inv[i], jnp.concatenate([v[i] * b_b[i], kb[i] * eg[i]], axis=1)) for i in idx]
    u_val = [sol[i][:, :DH] for i in idx]
    w_key = [sol[i][:, DH:] for i in idx]
    row = lax.broadcasted_iota(jnp.int32, (PAIR, 1), 0)
    has_q = q[0] is not None
    if has_q:
        qc = [q[i] * (DH ** -0.5) for i in idx]
        qk = [_dot(qc[i], k[i], _NT) for i in idx]
        attn = [qk[i] * decay[i] for i in idx]
        qd = [qc[i] * eg[i] for i in idx]
    outs = [[None, None] for _ in idx]
    zeros = jnp.zeros((CH, DH), F32)
    for step in range(2):
        cidx = [(1 - step) if revs[i] else step for i in idx]
        sl = [slice(c * CH, (c + 1) * CH) for c in cidx]
        last = [c * CH if revs[i] else c * CH + CH - 1 for i, c in zip(idx, cidx)]
        gl = [jnp.sum(jnp.where(row == last[i], gcol[i], 0.0), axis=0, keepdims=True) for i in idx]
        k_tail = [k[i][sl[i]] * jnp.exp(gl[i] - gc_i[i][sl[i]]) for i in idx]
        ws = [_dot(w_key[i][sl[i]], s[i]) for i in idx]
        v_new = [u_val[i][sl[i]] - ws[i] for i in idx]
        if has_q:
            v_pad = [jnp.concatenate([v_new[i], zeros] if cidx[i] == 0 else [zeros, v_new[i]], axis=0) for i in idx]
            o_state = [_dot(qd[i][sl[i]], s[i]) for i in idx]
            o_local = [_dot(attn[i][sl[i]], v_pad[i]) for i in idx]
            for i in idx:
                outs[i][cidx[i]] = o_state[i] + o_local[i]
        kv = [_dot(k_tail[i], v_new[i], _TN) for i in idx]
        s = [s[i] * jnp.exp(gl[i]) + kv[i] for i in idx]
    return s, ([jnp.concatenate(outs[i], axis=0) for i in idx] if has_q else None), inv


def _lane_col(tile, idx):
    lane = lax.broadcasted_iota(jnp.int32, tile.shape, 1)
    return jnp.sum(jnp.where(lane == idx, tile, 0.0), axis=1, keepdims=True)


def _gdn_fwd(qkv, gb, s0f, s0b, has_q, *, name):
    nb, rows, _ = qkv.shape
    n = rows // PAIR
    qoff = H if has_q else 0

    def body(qf_ref, qb_ref, gf_ref, gr_ref, s0f_ref, s0b_ref, of_ref, ob_ref, ssf_ref, ssb_ref, tsf_ref, tsb_ref,
             sf_ref, sb_ref):
        @pl.when(pl.program_id(0) == 0)
        def _():
            sf_ref[...] = s0f_ref[...]
            sb_ref[...] = s0b_ref[...]

        gtiles = (gf_ref[...], gr_ref[...])

        dirs = ((qf_ref, sf_ref, ssf_ref, of_ref), (qb_ref, sb_ref, ssb_ref, ob_ref))
        ts_refs = (tsf_ref, tsb_ref)

        def heads(hg, carry):
            work = [(hg * HEADS_PER_ITER_FWD + j, d) for j in range(HEADS_PER_ITER_FWD) for d in range(2)]
            loaded = []
            for h, d in work:
                q_ref, s_ref, _, _ = dirs[d]
                loaded.append((s_ref[h], q_ref[h] if has_q else None, q_ref[qoff + h], q_ref[qoff + H + h],
                               _lane_col(gtiles[d], d * H + h), _lane_col(gtiles[d], 2 * H + d * H + h)))
            s_new, o, inv = _pairs(*[list(col) for col in zip(*loaded)], revs=[d == 1 for _, d in work])
            for i, (h, d) in enumerate(work):
                _, s_ref, ss_ref, o_ref = dirs[d]
                ss_ref[0, h] = loaded[i][0]
                ts_refs[d][0, h] = inv[i]
                s_ref[h] = s_new[i]
                o_ref[h] = o[i] if has_q else jnp.zeros((PAIR, DH), F32)
            return carry

        if HEADS_PER_ITER_FWD == H:
            heads(0, 0)
        else:
            lax.fori_loop(0, H // HEADS_PER_ITER_FWD, heads, 0)

    fwd3 = lambda i: (0, i, 0)
    rev3 = lambda i: (0, n - 1 - i, 0)
    state = pl.BlockSpec((H, DH, DH), lambda i: (0, 0, 0))
    saved = pl.BlockSpec((1, H, DH, DH), lambda i: (i, 0, 0, 0))
    return pl.pallas_call(
        body, name=name, grid=(n,),
        in_specs=[pl.BlockSpec((nb, PAIR, DH), fwd3), pl.BlockSpec((nb, PAIR, DH), rev3),
                  pl.BlockSpec((PAIR, DH), lambda i: (i, 0)), pl.BlockSpec((PAIR, DH), lambda i: (n - 1 - i, 0)),
                  state, state],
        out_specs=[pl.BlockSpec((H, PAIR, DH), fwd3), pl.BlockSpec((H, PAIR, DH), rev3), saved, saved, saved, saved,
                   state, state],
        out_shape=[jax.ShapeDtypeStruct((H, rows, DH), F32)] * 2 + [jax.ShapeDtypeStruct((n, H, DH, DH), F32)] * 4
        + [jax.ShapeDtypeStruct((H, DH, DH), F32)] * 2,
        compiler_params=_params(("arbitrary",)),
    )(qkv, qkv, gb, gb, s0f, s0b)


def _gdn_bwd(qkv, gb, ssf, ssb, tsf, tsb, do, dsf, dsb, has_q, *, name):
    nb, rows, _ = qkv.shape
    n = rows // PAIR
    qoff = H if has_q else 0

    def body(qf_ref, qb_ref, gf_ref, gr_ref, ssf_ref, ssb_ref, tsf_ref, tsb_ref, dof_ref, dob_ref, dsf0_ref, dsb0_ref,
             dqf_ref, dqb_ref, dgf_ref, dgr_ref, dsf_ref, dsb_ref):
        ts_refs = (tsf_ref, tsb_ref)
        @pl.when(pl.program_id(0) == 0)
        def _():
            dsf_ref[...] = dsf0_ref[...]
            dsb_ref[...] = dsb0_ref[...]

        gtiles = (gf_ref[...], gr_ref[...])
        lane = lax.broadcasted_iota(jnp.int32, (PAIR, DH), 1)

        dirs = ((qf_ref, ssf_ref, dof_ref, dsf_ref, dqf_ref), (qb_ref, ssb_ref, dob_ref, dsb_ref, dqb_ref))

        def heads(hg, carry):
            out = list(carry)
            work = [(hg * HEADS_PER_ITER_BWD + j, d) for j in range(HEADS_PER_ITER_BWD) for d in range(2)]
            revs = [d == 1 for _, d in work]
            s_in, q_in, k_in, v_in, g_in, b_in, ds_out, do_out, inv_in = [], [], [], [], [], [], [], [], []
            for h, d in work:
                q_ref, ss_ref, do_ref, ds_ref, _ = dirs[d]
                s_in.append(ss_ref[0, h])
                inv_in.append(ts_refs[d][0, h])
                q_in.append(q_ref[h] if has_q else None)
                k_in.append(q_ref[qoff + h])
                v_in.append(q_ref[qoff + H + h])
                g_in.append(_lane_col(gtiles[d], d * H + h))
                b_in.append(_lane_col(gtiles[d], 2 * H + d * H + h))
                ds_out.append(ds_ref[h])
                do_out.append(do_ref[h] if has_q else None)
            if has_q:
                _, vjp = jax.vjp(lambda s_, q_, k_, v_, g_, b_: _pairs(s_, q_, k_, v_, g_, b_, revs, inv_in)[:2],
                                 s_in, q_in, k_in, v_in, g_in, b_in)
                ds, dq, dk, dv, dg, db = vjp((ds_out, do_out))
            else:
                _, vjp = jax.vjp(lambda s_, k_, v_, g_, b_: _pairs(s_, q_in, k_, v_, g_, b_, revs, inv_in)[0],
                                 s_in, k_in, v_in, g_in, b_in)
                ds, dk, dv, dg, db = vjp(ds_out)
            for i, (h, d) in enumerate(work):
                dq_ref, ds_ref = dirs[d][4], dirs[d][3]
                ds_ref[h] = ds[i]
                if has_q:
                    dq_ref[h] = dq[i]
                dq_ref[qoff + h], dq_ref[qoff + H + h] = dk[i], dv[i]
                out[d] = (out[d] + jnp.where(lane == d * H + h, dg[i], 0.0)
                          + jnp.where(lane == 2 * H + d * H + h, db[i], 0.0))
            return tuple(out)

        zero = jnp.zeros((PAIR, DH), F32)
        if HEADS_PER_ITER_BWD == H:
            dgf, dgr = heads(0, (zero, zero))
        else:
            dgf, dgr = lax.fori_loop(0, H // HEADS_PER_ITER_BWD, heads, (zero, zero))
        dgf_ref[...] = dgf
        dgr_ref[...] = dgr

    fwd3 = lambda i: (0, n - 1 - i, 0)
    rev3 = lambda i: (0, i, 0)
    state = pl.BlockSpec((H, DH, DH), lambda i: (0, 0, 0))
    saved = pl.BlockSpec((1, H, DH, DH), lambda i: (n - 1 - i, 0, 0, 0))
    gf_spec = pl.BlockSpec((PAIR, DH), lambda i: (n - 1 - i, 0))
    gr_spec = pl.BlockSpec((PAIR, DH), lambda i: (i, 0))
    return pl.pallas_call(
        body, name=name, grid=(n,),
        in_specs=[pl.BlockSpec((nb, PAIR, DH), fwd3), pl.BlockSpec((nb, PAIR, DH), rev3), gf_spec, gr_spec,
                  saved, saved, saved, saved,
                  pl.BlockSpec((H, PAIR, DH), fwd3), pl.BlockSpec((H, PAIR, DH), rev3), state, state],
        out_specs=[pl.BlockSpec((nb, PAIR, DH), fwd3), pl.BlockSpec((nb, PAIR, DH), rev3), gf_spec, gr_spec, state, state],
        out_shape=[jax.ShapeDtypeStruct((nb, rows, DH), F32)] * 2 + [jax.ShapeDtypeStruct((rows, DH), F32)] * 2
        + [jax.ShapeDtypeStruct((H, DH, DH), F32)] * 2,
        compiler_params=_params(("arbitrary",)),
    )(qkv, qkv, gb, gb, ssf, ssb, tsf, tsb, do, do, dsf, dsb)


def _stage1(zb, ua, va, za, o, gon, lng, lnb, wsp, bsp):
    gv = [_gelu(t) for t in va]
    mu = sum(jnp.sum(t, axis=-1, keepdims=True) for t in gv) * (1.0 / D)
    xc = [t - mu for t in gv]
    var = sum(jnp.sum(t * t, axis=-1, keepdims=True) for t in xc) * (1.0 / D)
    rs = lax.rsqrt(var + EPS)
    ya, yb = [], []
    for g in range(H):
        vv = xc[g] * rs * lng[g] + lnb[g]
        s = _dot(wsp[g], vv) + bsp[g]
        ya.append(_gelu(ua[g]) * s * _silu(za[g]))
        r = lax.rsqrt(jnp.mean(o[g] * o[g], axis=-1, keepdims=True) + EPS)
        yb.append(o[g] * r * gon * _silu(zb[g]))
    return ya, yb


def _stage2(ma, mb, ga, gb):
    return jax.nn.sigmoid(ga) * ma + jax.nn.sigmoid(gb) * mb


def _stage3(out, x, tgt, gpost, gate):
    r = out * lax.rsqrt(jnp.mean(out * out, axis=-1, keepdims=True) + EPS) * gpost
    err = x + gate * r - tgt
    return 0.5 * jnp.sum(jnp.mean(err * err, axis=-1, keepdims=True), axis=0, keepdims=True)


def _post(p_rest, o_f, o_b, x, tgt, mod, bmod, gon, lng, lnb, wsp, bspb, wpa, wpb, wout, gpost, *, name):
    rows = x.shape[0]
    n = rows // GM
    lanes = lambda g: slice(g * DH, (g + 1) * DH)
    bdot = lambda a, w_ref: _dot(a.astype(BF16), w_ref[...])
    bdot_t = lambda a, w_ref: _dot(a.astype(BF16), w_ref[...], _NT)

    def body(p_ref, of_ref, ob_ref, x_ref, t_ref, m_ref, bm_ref, gon_ref, lng_ref, lnb_ref, wsp_ref, bsp_ref,
             wpa_ref, wpb_ref, wout_ref, gp_ref,
             loss_ref, dp_ref, do_ref, dx_ref, ya_ref, yb_ref, mg_ref, dma_ref, dmb_ref, dout_ref,
             dvec_ref, dgon_ref, dwsp_ref, dbsp_ref):
        @pl.when(pl.program_id(0) == 0)
        def _():
            loss_ref[...] = jnp.zeros_like(loss_ref)
            dvec_ref[...] = jnp.zeros_like(dvec_ref)
            dgon_ref[...] = jnp.zeros_like(dgon_ref)
            dwsp_ref[...] = jnp.zeros_like(dwsp_ref)
            dbsp_ref[...] = jnp.zeros_like(dbsp_ref)

        piece = lambda blk: [p_ref[:, blk * D + g * DH: blk * D + (g + 1) * DH].astype(F32) for g in range(H)]
        zb, ua, va, za = piece(0), piece(1), piece(2), piece(3)
        o = [of_ref[g] + ob_ref[g] for g in range(H)]
        gon = gon_ref[...]
        lng = [lng_ref[:, lanes(g)] for g in range(H)]
        lnb = [lnb_ref[:, lanes(g)] for g in range(H)]
        wsp = [wsp_ref[g] for g in range(H)]
        bsp = [bsp_ref[g] for g in range(H)]
        (ya, yb), vjp1 = jax.vjp(_stage1, zb, ua, va, za, o, gon, lng, lnb, wsp, bsp)
        y_a, y_b = jnp.concatenate(ya, axis=1), jnp.concatenate(yb, axis=1)
        ma, mb = bdot(y_a, wpa_ref), bdot(y_b, wpb_ref)
        ga, gb = p_ref[:, 4 * D:5 * D].astype(F32), p_ref[:, 5 * D:6 * D].astype(F32)
        merged, vjp2 = jax.vjp(_stage2, ma, mb, ga, gb)
        out = bdot(merged, wout_ref)
        gate = m_ref[:, 2 * D:3 * D] + bm_ref[:, 2 * D:3 * D]
        loss, vjp3 = jax.vjp(_stage3, out, x_ref[...], t_ref[...], gp_ref[...], gate)
        loss_ref[...] += jnp.broadcast_to(loss, loss_ref.shape)

        dout, dx, _, dgpost, dgate = vjp3(jnp.ones((1, 1), F32))
        dx_ref[...] = dx
        dmerged = bdot_t(dout, wout_ref)
        dma, dmb, dga, dgb = vjp2(dmerged)
        dya, dyb = bdot_t(dma, wpa_ref), bdot_t(dmb, wpb_ref)
        dzb, dua, dva, dza, do, dgon, dlng, dlnb, dwsp, dbsp = vjp1(
            ([dya[:, lanes(g)] for g in range(H)], [dyb[:, lanes(g)] for g in range(H)]))

        for blk, dlist in enumerate((dzb, dua, dva, dza)):
            for g in range(H):
                dp_ref[:, blk * D + g * DH: blk * D + (g + 1) * DH] = dlist[g].astype(BF16)
        dp_ref[:, 4 * D:5 * D] = dga.astype(BF16)
        dp_ref[:, 5 * D:6 * D] = dgb.astype(BF16)
        for g in range(H):
            do_ref[g] = do[g]
            dwsp_ref[g] += dwsp[g]
            dbsp_ref[g] += dbsp[g]
            dvec_ref[2:3, lanes(g)] += dlng[g]
            dvec_ref[3:4, lanes(g)] += dlnb[g]
        dvec_ref[0:1, :] += dgpost
        dvec_ref[1:2, :] += dgate
        dgon_ref[0:1, :] += dgon
        ya_ref[...] = y_a.astype(BF16)
        yb_ref[...] = y_b.astype(BF16)
        mg_ref[...] = merged.astype(BF16)
        dma_ref[...] = dma.astype(BF16)
        dmb_ref[...] = dmb.astype(BF16)
        dout_ref[...] = dout.astype(BF16)

    row = lambda w: pl.BlockSpec((GM, w), lambda i: (i, 0))
    heads = pl.BlockSpec((H, GM, DH), lambda i: (0, i, 0))
    full = lambda shape: pl.BlockSpec(shape, lambda i: tuple(0 for _ in shape))
    sds = jax.ShapeDtypeStruct
    return pl.pallas_call(
        body, name=name, grid=(n,),
        in_specs=[row(6 * D), heads, heads, row(D), row(D), full((1, 3 * D)), full((1, 3 * D)), full((1, DH)),
                  full((1, D)), full((1, D)), full((H, GM, GM)), full((H, GM, GM)),
                  full((D, D)), full((D, D)), full((D, D)), full((1, D))],
        out_specs=[full((8, DH)), row(6 * D), heads, row(D)] + [row(D)] * 6
        + [full((8, D)), full((8, DH)), full((H, GM, GM)), full((H, GM, GM))],
        out_shape=[sds((8, DH), F32), sds((rows, 6 * D), BF16), sds((H, rows, DH), F32), sds((rows, D), F32)]
        + [sds((rows, D), BF16)] * 6 + [sds((8, D), F32), sds((8, DH), F32), sds((H, GM, GM), F32), sds((H, GM, GM), F32)],
        compiler_params=_params(("arbitrary",)),
    )(p_rest, o_f, o_b, x, tgt, mod, bmod, gon, lng, lnb, wsp, bspb, wpa, wpb, wout, gpost)


def _dsilu_mul(c, d, *, name):
    def body(c_ref, d_ref, o_ref):
        _, vjp = jax.vjp(_silu, c_ref[...])
        (o_ref[...],) = vjp(d_ref[...])

    return pl.pallas_call(body, name=name, out_shape=jax.ShapeDtypeStruct(c.shape, F32))(c, d)


def _adamw(w, g, m, v, *, name):
    rows, cols = w.shape
    t = _tile(rows)
    c1 = 1.0 / (1.0 - ADAM_B1 ** ADAM_STEP)
    c2 = 1.0 / (1.0 - ADAM_B2 ** ADAM_STEP)

    def body(w_ref, g_ref, m_ref, v_ref, go_ref, d_ref, mo_ref, vo_ref):
        gv = g_ref[...]
        mn = ADAM_B1 * m_ref[...] + (1.0 - ADAM_B1) * gv
        vn = ADAM_B2 * v_ref[...] + (1.0 - ADAM_B2) * (gv * gv)
        go_ref[...] = gv
        mo_ref[...] = mn
        vo_ref[...] = vn
        d_ref[...] = -ADAM_LR * ((mn * c1) / (jnp.sqrt(vn * c2) + ADAM_EPS) + ADAM_WD * w_ref[...])

    tile = pl.BlockSpec((t, cols), lambda i: (i, 0))
    return pl.pallas_call(
        body, name=name, grid=(rows // t,),
        in_specs=[tile] * 4, out_specs=[tile] * 4,
        out_shape=[jax.ShapeDtypeStruct((rows, cols), F32)] * 4,
        compiler_params=_params(("parallel",)),
    )(w, g, m, v)


def _add_sibling(full, got, where, out_dtype, *, name):
    s, rows, cols = full.shape
    hr = rows // 2
    t = _tile(hr)
    nt = hr // t

    def body(w_ref, a_ref, b_ref, o_ref):
        o_ref[...] = (a_ref[...].astype(F32) + b_ref[...].astype(F32)).astype(out_dtype)

    tile = pl.BlockSpec((1, t, cols), lambda j, i, w: (j, i, 0))
    return pl.pallas_call(
        body, name=name,
        grid_spec=pltpu.PrefetchScalarGridSpec(
            num_scalar_prefetch=1, grid=(s, nt),
            in_specs=[pl.BlockSpec((1, t, cols), lambda j, i, w: (j, w[0] * nt + i, 0)), tile], out_specs=tile),
        out_shape=jax.ShapeDtypeStruct((s, hr, cols), out_dtype), compiler_params=_params(("parallel", "parallel")),
    )(where, full, got)


def _sum_chips(own, slots, where, n_out, which, *, name):
    _, hr, cols = own.shape
    t = _tile(hr)

    def body(w_ref, a_ref, s_ref, o_ref):
        f = lambda v: v.astype(F32)
        o_ref[0] = ((f(a_ref[0]) + f(s_ref[0])) + f(s_ref[1])) + f(s_ref[2])

    return pl.pallas_call(
        body, name=name,
        grid_spec=pltpu.PrefetchScalarGridSpec(
            num_scalar_prefetch=1, grid=(hr // t,),
            in_specs=[pl.BlockSpec((1, t, cols), lambda i, w: (w[1], i, 0)),
                      pl.BlockSpec((N_CHIPS - 1, t, cols), lambda i, w: (0, i, 0))],
            out_specs=pl.BlockSpec((1, t, cols), lambda i, w: (w[which], i, 0))),
        out_shape=jax.ShapeDtypeStruct((n_out, hr, cols), F32), compiler_params=_params(("parallel",)),
    )(where, own, slots)


def _local_step(x, c, ctx, c_ctx, tgt, w_mod, b_mod, g_pre, g_post, w_qkv, w_ab, w_rest, w_conv, a_log, dt_bias,
                g_onorm, gm_ln_g, gm_ln_b, w_sp, b_sp, w_pa, w_pb, w_out):
    rows, rows_c = x.shape[0], ctx.shape[0]
    cc = jnp.zeros((16, D), F32).at[0].set(c[0]).at[1].set(c_ctx)
    mod = _mm(cc, w_mod, pre_silu=True, b_sections=True, name="mod_fwd")
    mod_x, mod_c = mod[0:1], mod[1:2]
    avec = jnp.zeros((1, DH), F32).at[0, :2 * H].set(a_log.reshape(-1))
    dvec = jnp.zeros((1, DH), F32).at[0, :2 * H].set(dt_bias.reshape(-1))
    bspb = jnp.broadcast_to(b_sp[:, :, None], (H, GM, GM))
    w_kv, wconv_kv = w_qkv[:, D:], w_conv[:, D:]

    h_c = _norm_fwd(ctx, g_pre, mod_c, b_mod, name="norm_fwd_ctx")
    pc_kv = _mm(h_c, w_kv, name="inproj_ctx_kv")
    pc_ab = _mm(h_c, w_ab, name="inproj_ctx_ab")
    kv_c = _prep_fwd(pc_kv, wconv_kv, H, name="prep_fwd_ctx")
    gb_c = _gates_fwd(pc_ab, avec, dvec, name="gates_fwd_ctx")
    s_zero = jnp.zeros((H, DH, DH), F32)
    _, _, ssf_c, ssb_c, tsf_c, tsb_c, s_f, s_b = _gdn_fwd(kv_c, gb_c, s_zero, s_zero, False, name="gdn_fwd_ctx")

    h_x = _norm_fwd(x, g_pre, mod_x, b_mod, name="norm_fwd_x")
    p_qkv = _mm(h_x, w_qkv, name="inproj_qkv")
    p_ab = _mm(h_x, w_ab, name="inproj_ab")
    p_rest = _mm(h_x, w_rest, out_dtype=BF16, name="inproj_rest")
    qkv = _prep_fwd(p_qkv, w_conv, 2 * H, name="prep_fwd_x")
    gb_x = _gates_fwd(p_ab, avec, dvec, name="gates_fwd_x")
    o_f, o_b, ssf, ssb, tsf, tsb, _, _ = _gdn_fwd(qkv, gb_x, s_f, s_b, True, name="gdn_fwd_x")

    (loss_acc, dp_rest, do, dx_res, ya, yb, mg, dma, dmb, dout, dvec_post, dgon, dwsp, dbspb) = _post(
        p_rest, o_f, o_b, x, tgt, mod_x, b_mod, g_onorm, gm_ln_g, gm_ln_b, w_sp, bspb, w_pa, w_pb, w_out, g_post,
        name="post")
    g = {}
    g["w_pa"] = _mm(ya, dma, ta=True, name="dw_pa")
    g["w_pb"] = _mm(yb, dmb, ta=True, name="dw_pb")
    g["w_out"] = _mm(mg, dout, ta=True, name="dw_out")

    zeros_s = jnp.zeros((H, DH, DH), F32)
    dq_f, dq_b, dg_f, dg_b, ds0_f, ds0_b = _gdn_bwd(qkv, gb_x, ssf, ssb, tsf, tsb, do, zeros_s, zeros_s, True,
                                                    name="gdn_bwd_x")
    dp_qkv, dwc_x = _prep_bwd(p_qkv, w_conv, dq_f, dq_b, 2 * H, name="prep_bwd_x")
    dp_ab, dav_x, ddv_x = _gates_bwd(p_ab, avec, dvec, dg_f, dg_b, name="gates_bwd_x")
    dkv_f, dkv_b, dgc_f, dgc_b, _, _ = _gdn_bwd(kv_c, gb_c, ssf_c, ssb_c, tsf_c, tsb_c, jnp.zeros((H, rows_c, DH), F32),
                                                 ds0_f, ds0_b, False, name="gdn_bwd_ctx")
    dpc_kv, dwc_c = _prep_bwd(pc_kv, wconv_kv, dkv_f, dkv_b, H, name="prep_bwd_ctx")
    dpc_ab, dav_c, ddv_c = _gates_bwd(pc_ab, avec, dvec, dgc_f, dgc_b, name="gates_bwd_ctx")

    dw_kv_c = _mm(h_c, dpc_kv, ta=True, name="dw_kv_ctx")
    dw_qkv = _mm(h_x, dp_qkv, ta=True, tn=D, add=dw_kv_c, add_from=1, name="dw_qkv")
    dw_ab_c = _mm(h_c, dpc_ab, ta=True, name="dw_ab_ctx")
    dw_ab = _mm(h_x, dp_ab, ta=True, add=dw_ab_c, name="dw_ab")
    dw_rest = _mm(h_x, dp_rest, ta=True, name="dw_rest")
    g["w_in"] = (dw_qkv, dw_ab[:, :4 * H], dw_rest)
    dh_parts = [_mm(dp_qkv, w_qkv, tb=True, tk=3 * D // 2, name="dh_qkv"),
                _mm(dp_rest, w_rest, tb=True, tk=3 * D // 2, name="dh_rest"), _mm(dp_ab, w_ab, tb=True, name="dh_ab")]
    grad_x, dgpre_x, dm_x = _norm_bwd(x, g_pre, mod_x, b_mod, dh_parts, dx_res, name="norm_bwd_x")
    dhc_parts = [_mm(dpc_kv, w_kv, tb=True, name="dhc_kv"), _mm(dpc_ab, w_ab, tb=True, name="dhc_ab")]
    _, dgpre_c, dm_c = _norm_bwd(ctx, g_pre, mod_c, b_mod, dhc_parts, None, name="norm_bwd_ctx")

    dm_x = dm_x.at[:, 2 * D:].add(dvec_post[1:2])
    dmod = jnp.zeros((16, 3 * D), F32).at[0].set(dm_x[0]).at[1].set(dm_c[0])
    g["w_mod"] = _mm(cc, dmod, ta=True, pre_silu=True, tn=w_mod.shape[2], out_sections=True, name="dw_mod")
    dcc = _mm(dmod, w_mod, tb=True, b_sections=True, name="dcc")
    g["c_ctx"] = _dsilu_mul(cc[:8], dcc[:8], name="dc_ctx")[1]
    g["b_mod"] = dm_x + dm_c
    g["g_pre"] = dgpre_x + dgpre_c
    g["g_post"] = dvec_post[0:1]
    g["gm_ln_g"], g["gm_ln_b"] = dvec_post[2:3], dvec_post[3:4]
    g["g_onorm"] = dgon[0:1]
    g["w_sp"] = dwsp
    g["b_sp"] = jnp.sum(dbspb, axis=-1)
    g["w_conv"] = dwc_x.at[:, D:].add(dwc_c)
    g["a_log"] = (dav_x + dav_c)[0, :2 * H].reshape(2, H)
    g["dt_bias"] = (ddv_x + ddv_c)[0, :2 * H].reshape(2, H)
    return loss_acc[0, 0], grad_x, g


ANY = pl.BlockSpec(memory_space=pl.ANY)


def _place():
    x, y, c = lax.axis_index("x"), lax.axis_index("y"), lax.axis_index("c")
    chips = [(1 - x, y), (x, 1 - y), (1 - x, 1 - y)]
    return x, y, c, (x, y, 1 - c), chips


def _gather_shards(big, small, *, name):
    nb = len(big)

    def body(*refs):
        ins, sm_in = refs[:nb], refs[nb]
        outs, sm_out = refs[nb + 1:2 * nb + 1], refs[2 * nb + 1]
        send, recv = refs[2 * nb + 2:]
        x, y, c, sibling, chips = _place()
        mine = 2 * x + y

        def half(a, shard, hc):
            hr = big[a].shape[0] // 2
            return outs[a].at[shard, pl.ds(hc * hr, hr), :]

        def remote(k, src, dst, to):
            return pltpu.make_async_remote_copy(src_ref=src, dst_ref=dst, send_sem=send.at[k], recv_sem=recv.at[k],
                                                device_id=to, device_id_type=MESH)

        sends = []
        for a in range(nb):
            hr = big[a].shape[0] // 2
            for j, chip in enumerate(chips):
                sends.append(remote(a * 3 + j, ins[a].at[pl.ds(c * hr, hr), :], half(a, mine, c), (*chip, c)))
        for j, chip in enumerate(chips):
            sends.append(remote(nb * 3 + j, sm_in, sm_out.at[mine], (*chip, c)))
        for cp in sends:
            cp.start()
        base = nb * 3 + 3
        passed = []
        for a in range(nb):
            for j, (px, py) in enumerate(chips):
                theirs = 2 * px + py
                remote(a * 3 + j, half(a, theirs, c), half(a, theirs, c), sibling).wait_recv()
                fw = remote(base + a * 3 + j, half(a, theirs, c), half(a, theirs, c), sibling)
                fw.start()
                passed.append(fw)
        for a in range(nb):
            for j, (px, py) in enumerate(chips):
                theirs = 2 * px + py
                remote(base + a * 3 + j, half(a, theirs, 1 - c), half(a, theirs, 1 - c), sibling).wait_recv()
        for j, (px, py) in enumerate(chips):
            remote(nb * 3 + j, sm_in, sm_out.at[2 * px + py], sibling).wait_recv()
        for cp in sends + passed:
            cp.wait_send()

    n_remote = 2 * nb * 3 + 3
    outs = pl.pallas_call(
        body, name=name, in_specs=[ANY] * (nb + 1), out_specs=[ANY] * (nb + 1),
        out_shape=[jax.ShapeDtypeStruct((N_CHIPS,) + a.shape, a.dtype) for a in big + [small]],
        scratch_shapes=[pltpu.SemaphoreType.DMA((n_remote,)), pltpu.SemaphoreType.DMA((n_remote,))],
    )(*big, small)
    return outs[:nb], outs[nb]


def _row_chunks(rows, row_bytes, align=8):
    n = max(1, min(rows // align, -(-rows * row_bytes // DMA_CHUNK_BYTES)))
    per = -(-(-(-rows // n)) // align) * align
    return [(r, min(per, rows - r)) for r in range(0, rows, per)]


def _remote(src, dst, send, recv, to):
    return pltpu.make_async_remote_copy(src_ref=src, dst_ref=dst, send_sem=send, recv_sem=recv, device_id=to,
                                        device_id_type=MESH)


def _exchange_halves(arrs, *, name):
    na = len(arrs)

    def body(*refs):
        ins, got = refs[:na], refs[na:2 * na]
        send, recv = refs[2 * na:]
        x, y, c, sibling, _ = _place()
        for a in range(na):
            ns, rows, cols = arrs[a].shape
            hr = rows // 2
            for s in range(ns):
                for r0, nr in _row_chunks(hr, cols * arrs[a].dtype.itemsize, 16):
                    _remote(ins[a].at[s, pl.ds((1 - c) * hr + r0, nr), :], got[a].at[s, pl.ds(r0, nr), :],
                            send.at[a], recv.at[a], sibling).start()
        for a in range(na):
            hr = arrs[a].shape[1] // 2
            _remote(ins[a].at[:, pl.ds((1 - c) * hr, hr), :], got[a], send.at[a], recv.at[a], sibling).wait()

    return pl.pallas_call(
        body, name=name, in_specs=[ANY] * na, out_specs=[ANY] * na,
        out_shape=[jax.ShapeDtypeStruct((N_CHIPS, a.shape[1] // 2, a.shape[2]), a.dtype) for a in arrs],
        scratch_shapes=[pltpu.SemaphoreType.DMA((na,)), pltpu.SemaphoreType.DMA((na,))],
    )(*arrs)


def _scatter_sections(arrs, *, name):
    na = len(arrs)

    def body(*refs):
        ins, outs = refs[:na], refs[na:2 * na]
        send, recv = refs[2 * na:]
        x, y, c, _, chips = _place()
        for a in range(na):
            _, hr, cols = arrs[a].shape
            for j, (px, py) in enumerate(chips):
                for r0, nr in _row_chunks(hr, cols * arrs[a].dtype.itemsize, 16):
                    _remote(ins[a].at[2 * px + py, pl.ds(r0, nr), :], outs[a].at[j, pl.ds(r0, nr), :],
                            send.at[a * 3 + j], recv.at[a * 3 + j], (px, py, c)).start()
        for a in range(na):
            for j, (px, py) in enumerate(chips):
                _remote(ins[a].at[2 * px + py], outs[a].at[j], send.at[a * 3 + j], recv.at[a * 3 + j], (px, py, c)).wait()

    return pl.pallas_call(
        body, name=name, in_specs=[ANY] * na, out_specs=[ANY] * na,
        out_shape=[jax.ShapeDtypeStruct((N_CHIPS - 1,) + a.shape[1:], a.dtype) for a in arrs],
        scratch_shapes=[pltpu.SemaphoreType.DMA((3 * na,)), pltpu.SemaphoreType.DMA((3 * na,))],
    )(*arrs)


def _finish_reduce(big, small, *, name):
    nb = len(big)

    def body(*refs):
        outs, sm = refs[nb + 1:2 * nb + 1], refs[2 * nb + 1]
        send, recv = refs[2 * nb + 2:]
        x, y, c, sibling, chips = _place()
        blk = lambda px, py, pc: sm.at[4 * px + 2 * py + pc]
        for a in range(nb):
            _, hr, cols = big[a].shape
            for r0, nr in _row_chunks(hr, cols * 4):
                _remote(outs[a].at[c, pl.ds(r0, nr), :], outs[a].at[c, pl.ds(r0, nr), :], send.at[a], recv.at[a],
                        sibling).start()
        first = [_remote(blk(x, y, c), blk(x, y, c), send.at[nb], recv.at[nb], sibling)]
        first += [_remote(blk(x, y, c), blk(x, y, c), send.at[nb + 1 + j], recv.at[nb + 1 + j], (*chip, c))
                  for j, chip in enumerate(chips)]
        for cp in first:
            cp.start()
        passed = []
        for j, (px, py) in enumerate(chips):
            _remote(blk(px, py, c), blk(px, py, c), send.at[nb + 1 + j], recv.at[nb + 1 + j], sibling).wait_recv()
            fw = _remote(blk(px, py, c), blk(px, py, c), send.at[nb + 4 + j], recv.at[nb + 4 + j], sibling)
            fw.start()
            passed.append(fw)
        for a in range(nb):
            _remote(outs[a].at[c], outs[a].at[1 - c], send.at[a], recv.at[a], sibling).wait()
        _remote(blk(x, y, c), blk(x, y, 1 - c), send.at[nb], recv.at[nb], sibling).wait_recv()
        for j, (px, py) in enumerate(chips):
            _remote(blk(px, py, c), blk(px, py, 1 - c), send.at[nb + 4 + j], recv.at[nb + 4 + j], sibling).wait_recv()
        for cp in first + passed:
            cp.wait_send()

    n_remote = nb + 7
    arrs = list(big) + [small]
    outs = pl.pallas_call(
        body, name=name, in_specs=[ANY] * (nb + 1), out_specs=[ANY] * (nb + 1),
        out_shape=[jax.ShapeDtypeStruct(a.shape, F32) for a in arrs],
        input_output_aliases={i: i for i in range(nb + 1)},
        scratch_shapes=[pltpu.SemaphoreType.DMA((n_remote,)), pltpu.SemaphoreType.DMA((n_remote,))],
    )(*arrs)
    return outs[:nb], outs[nb]


def _reduce_gradients(sectioned, small, where):
    arrs = list(sectioned) + [small]
    got = _exchange_halves(arrs, name="rs_exchange_halves")
    pay = [BF16] * len(sectioned) + [F32]
    chip_sum = [_add_sibling(a, g, where, p, name=f"rs_add_sibling_{i}")
                for i, (a, g, p) in enumerate(zip(arrs, got, pay))]
    slots = _scatter_sections(chip_sum, name="rs_scatter_sections")
    red = [_sum_chips(p, s, where, 2, 0, name=f"rs_sum_chips_{i}")
           for i, (p, s) in enumerate(zip(chip_sum[:-1], slots[:-1]))]
    red_small = _sum_chips(chip_sum[-1], slots[-1], where, 2 * N_CHIPS, 2, name="rs_sum_chips_small")
    big, sm = _finish_reduce(red, red_small, name="rs_finish")
    return [b.reshape(-1, b.shape[-1]) for b in big], sm.reshape(-1, sm.shape[-1])


def kernel(x, c, ctx, c_ctx, w_mod, b_mod, g_pre, g_post, w_in, w_conv, a_log, dt_bias, g_onorm, gm_ln_g, gm_ln_b, w_sp, b_sp, w_pa, w_pb, w_out, loss_target, m_c_ctx, m_w_mod, m_b_mod, m_g_pre, m_g_post, m_w_in, m_w_conv, m_a_log, m_dt_bias, m_g_onorm, m_gm_ln_g, m_gm_ln_b, m_w_sp, m_b_sp, m_w_pa, m_w_pb, m_w_out, v_c_ctx, v_w_mod, v_b_mod, v_g_pre, v_g_post, v_w_in, v_w_conv, v_a_log, v_dt_bias, v_g_onorm, v_gm_ln_g, v_gm_ln_b, v_w_sp, v_b_sp, v_w_pa, v_w_pb, v_w_out):
    names = ["c_ctx", "w_mod", "b_mod", "g_pre", "g_post", "w_in", "w_conv", "a_log", "dt_bias", "g_onorm", "gm_ln_g",
             "gm_ln_b", "w_sp", "b_sp", "w_pa", "w_pb", "w_out"]
    w = dict(zip(names, (c_ctx, w_mod, b_mod, g_pre, g_post, w_in, w_conv, a_log, dt_bias, g_onorm, gm_ln_g, gm_ln_b,
                         w_sp, b_sp, w_pa, w_pb, w_out)))
    m = dict(zip(names, (m_c_ctx, m_w_mod, m_b_mod, m_g_pre, m_g_post, m_w_in, m_w_conv, m_a_log, m_dt_bias, m_g_onorm,
                         m_gm_ln_g, m_gm_ln_b, m_w_sp, m_b_sp, m_w_pa, m_w_pb, m_w_out)))
    v = dict(zip(names, (v_c_ctx, v_w_mod, v_b_mod, v_g_pre, v_g_post, v_w_in, v_w_conv, v_a_log, v_dt_bias, v_g_onorm,
                         v_gm_ln_g, v_gm_ln_b, v_w_sp, v_b_sp, v_w_pa, v_w_pb, v_w_out)))
    xy = 2 * lax.axis_index("x") + lax.axis_index("y")
    where = jnp.stack([lax.axis_index("c"), xy, 2 * xy + lax.axis_index("c")]).astype(jnp.int32)

    shards = [a[0].astype(BF16) for a in (w_mod, w_in, w_pa, w_pb, w_out)]
    gathered, wconv_all = _gather_shards(shards, w_conv[0], name="gather_weights")
    own = lambda full, shard: lax.dynamic_update_slice(full, shard[None], (xy, 0, 0))
    wm_all, win_all, wpa_all, wpb_all, wout_all = [own(f, s) for f, s in zip(gathered, shards)]
    wconv_all = own(wconv_all, w_conv[0])
    w_conv_f = jnp.concatenate([wconv_all[s] for s in range(N_CHIPS)], axis=1)
    shard_cols = IN_COLS // N_CHIPS
    cut = OFF_A - shard_cols
    assert 0 < cut and cut + 4 * H < shard_cols
    w_qkv = jnp.concatenate([win_all[0], win_all[1][:, :cut]], axis=1)
    w_ab = jnp.pad(win_all[1][:, cut:cut + 4 * H], ((0, 0), (0, DH - 4 * H)))
    w_rest = jnp.concatenate([win_all[1][:, cut + 4 * H:], win_all[2], win_all[3]], axis=1)

    loss_local, grad_x, g = _local_step(
        x[0], c, ctx[0], c_ctx, loss_target[0], wm_all, b_mod, g_pre, g_post, w_qkv, w_ab, w_rest,
        w_conv_f, a_log[0], dt_bias[0], g_onorm, gm_ln_g, gm_ln_b, w_sp[0], b_sp[0],
        wpa_all.reshape(D, D), wpb_all.reshape(D, D), wout_all.reshape(D, D))
    g["loss"] = loss_local
    blk = D // N_CHIPS
    dw_qkv, dw_ab, dw_rest = g["w_in"]
    rest_cut = shard_cols - cut - 4 * H
    g_win = jnp.stack([dw_qkv[:, :shard_cols],
                       jnp.concatenate([dw_qkv[:, shard_cols:], dw_ab, dw_rest[:, :rest_cut]], axis=1),
                       dw_rest[:, rest_cut:rest_cut + shard_cols], dw_rest[:, rest_cut + shard_cols:]]).astype(BF16)
    g["_pad"] = jnp.zeros((SMALL_LAYOUT["_pad"][1],), F32)
    tail = jnp.zeros((N_CHIPS * SMALL_ROWS * 128 - sum(s for _, s in SMALL_LAYOUT.values()),), F32)
    flat = jnp.concatenate([g[k].reshape(-1) for k in SMALL_LAYOUT] + [tail])
    big_names = ("w_mod", "w_in", "w_pa", "w_pb", "w_out")
    sectioned = [g["w_mod"], g_win] + [g[k].reshape(N_CHIPS, blk, D) for k in ("w_pa", "w_pb", "w_out")]
    reduced, gr_small = _reduce_gradients(sectioned, flat.reshape(N_CHIPS, SMALL_ROWS, 128), where)
    gr_tiny = gr_small[TINY_ROW0:]

    def entry(arr, k):
        off, size = SMALL_LAYOUT[k]
        row, col = off // 128 - TINY_ROW0, off % 128
        return arr[row:row + size // 128].reshape(-1) if size >= 128 else arr[row, col:col + size]

    res = {k: _adamw(w[k][0], gr, m[k][0], v[k][0], name=f"adamw_{k}") for k, gr in zip(big_names, reduced)}
    res["w_sp"] = _adamw(w_sp.reshape(-1, 128), gr_small, m_w_sp.reshape(-1, 128), v_w_sp.reshape(-1, 128),
                         name="adamw_w_sp")
    tiny_names = [k for k in SMALL_LAYOUT if k not in ("w_sp", "w_conv", "loss", "_pad")]

    def pack(src):
        parts = [src[k].reshape(-1) if k in tiny_names else jnp.zeros((s,), F32)
                 for k, (_, s) in SMALL_LAYOUT.items() if k != "w_sp"]
        return jnp.concatenate(parts + [tail]).reshape(-1, 128)

    tiny_res = _adamw(pack(w), gr_tiny, pack(m), pack(v), name="adamw_tiny")
    for k in tiny_names:
        res[k] = [entry(r, k) for r in tiny_res]
    g_conv = lax.dynamic_slice(entry(gr_tiny, "w_conv").reshape(3, 3 * D), (0, xy * (3 * D // N_CHIPS)),
                               (3, 3 * D // N_CHIPS))
    conv_res = _adamw(jnp.pad(w_conv[0], ((0, 5), (0, 0))), jnp.pad(g_conv, ((0, 5), (0, 0))),
                      jnp.pad(m_w_conv[0], ((0, 5), (0, 0))), jnp.pad(v_w_conv[0], ((0, 5), (0, 0))), name="adamw_w_conv")
    res["w_conv"] = [r[:3] for r in conv_res]
    res = {k: [r.reshape(w[k].shape) for r in res[k]] for k in names}

    out = [entry(gr_tiny, "loss").reshape(()), grad_x[None]]
    for i in range(4):
        out += [res[k][i] for k in names]
    return tuple(out)
```

```python
import functools

import jax
import jax.numpy as jnp
from jax import lax
from jax.experimental import pallas as pl
from jax.experimental.pallas import tpu as pltpu

F32 = jnp.float32
BF16 = jnp.bfloat16
HI = lax.Precision.HIGHEST
MESH = pl.DeviceIdType.MESH

D = 1024
H = 8
DH = 128
CH = 64
LOG_CH = 6
PAIR = 2 * CH
GM = 128
assert 1 << LOG_CH == CH and PAIR == DH
PREC_POWERS = ((lax.Precision.HIGH, lax.Precision.HIGH),) * 3 + ((None, None),) * 2
assert len(PREC_POWERS) == LOG_CH - 1
HEADS_PER_ITER_FWD = 8
HEADS_PER_ITER_BWD = 8
EPS = 1e-6
N_CHIPS = 4
OFF_A = 3 * D
OFF_ZB = OFF_A + 4 * H
IN_COLS = OFF_ZB + 6 * D
VMEM_LIMIT_V7X = 56 * 1024 * 1024
DMA_CHUNK_BYTES = 2 * 1024 * 1024

ADAM_LR, ADAM_B1, ADAM_B2, ADAM_EPS, ADAM_WD, ADAM_STEP = 0.001, 0.9, 0.999, 1e-08, 0.01, 10

SMALL_LAYOUT = {}
_off = 0
for _n, _s in (("w_sp", H * GM * GM), ("w_conv", 3 * 3 * D), ("c_ctx", D), ("b_mod", 3 * D), ("g_pre", D), ("g_post", D),
               ("gm_ln_g", D), ("gm_ln_b", D), ("b_sp", H * GM), ("dm_ctx", 3 * D), ("mod_factors", 8 * 4 * D),
               ("g_onorm", DH), ("a_log", 2 * H), ("dt_bias", 2 * H), ("loss", 1), ("_pad", 128 - 4 * H - 1)):
    SMALL_LAYOUT[_n] = (_off, _s)
    _off += _s
SMALL_ROWS = 368
assert N_CHIPS * SMALL_ROWS * 128 >= _off and (SMALL_ROWS // 2) % 8 == 0 and _off % 128 == 0
TINY_ROW0 = SMALL_LAYOUT["w_conv"][0] // 128


def _params(sem=None):
    return pltpu.CompilerParams(dimension_semantics=sem, vmem_limit_bytes=VMEM_LIMIT_V7X)


def _tile(n, cands=(256, 128, 64, 32, 16, 8)):
    if n <= cands[0]:
        return n
    for cand in cands:
        if n % cand == 0:
            return cand
    return n


def _silu(x):
    return x * jax.nn.sigmoid(x)


def _gelu(x):
    return 0.5 * x * (1.0 + jnp.tanh(0.7978845608028654 * (x + 0.044715 * (x * x * x))))


def _mm(a, b, *, ta=False, tb=False, out_dtype=F32, tm=1024, tn=1024, tk=1024, add=None, add_from=0,
        b_sections=False, out_sections=False, name):
    m, k = (a.shape[1], a.shape[0]) if ta else a.shape
    if b_sections:
        sect = b.shape[2]
        n = b.shape[1] if tb else b.shape[0] * sect
        tn, tk = (tn, sect) if tb else (sect, tk)
    else:
        n = b.shape[0] if tb else b.shape[1]
    tm, tn, tk = min(tm, m), min(tn, n), min(tk, k)
    assert m % tm == 0 and n % tn == 0 and k % tk == 0, (name, m, n, k, tm, tn, tk)
    nk = k // tk
    dims = (((0,) if ta else (1,), (1,) if tb else (0,)), ((), ()))
    has_add = add is not None
    assert not has_add or add.shape == (m, n - add_from * tn), (name, add.shape)

    def body(*refs):
        a_ref, b_ref = refs[:2]
        o_ref = refs[2 + int(has_add)]
        acc_ref = refs[-1]
        kk = pl.program_id(2)
        bv = b_ref[0] if b_sections else b_ref[...]
        part = lax.dot_general(a_ref[...].astype(BF16), bv.astype(BF16), dims, preferred_element_type=F32)

        def finish(res):
            if has_add:
                res = res + jnp.where(pl.program_id(1) >= add_from, refs[2][...], 0.0)
            if out_sections:
                o_ref[0] = res.astype(out_dtype)
            else:
                o_ref[...] = res.astype(out_dtype)

        if nk == 1:
            finish(part)
            return

        @pl.when(kk == 0)
        def _():
            acc_ref[...] = part

        @pl.when((kk > 0) & (kk < nk - 1))
        def _():
            acc_ref[...] += part

        @pl.when(kk == nk - 1)
        def _():
            finish(acc_ref[...] + part)

    a_spec = pl.BlockSpec((tk, tm), lambda i, j, q: (q, i)) if ta else pl.BlockSpec((tm, tk), lambda i, j, q: (i, q))
    if b_sections:
        b_spec = (pl.BlockSpec((1, tn, tk), lambda i, j, q: (q, j, 0)) if tb
                  else pl.BlockSpec((1, tk, tn), lambda i, j, q: (j, q, 0)))
    else:
        b_spec = pl.BlockSpec((tn, tk), lambda i, j, q: (j, q)) if tb else pl.BlockSpec((tk, tn), lambda i, j, q: (q, j))
    add_spec = [pl.BlockSpec((tm, tn), lambda i, j, q: (i, jnp.maximum(j - add_from, 0)))] if has_add else []
    if out_sections:
        out_spec, out_shape = pl.BlockSpec((1, tm, tn), lambda i, j, q: (j, i, 0)), (n // tn, m, tn)
    else:
        out_spec, out_shape = pl.BlockSpec((tm, tn), lambda i, j, q: (i, j)), (m, n)
    return pl.pallas_call(
        body, name=name, grid=(m // tm, n // tn, nk),
        in_specs=[a_spec, b_spec] + add_spec, out_specs=out_spec,
        out_shape=jax.ShapeDtypeStruct(out_shape, out_dtype),
        scratch_shapes=[pltpu.VMEM((tm, tn), F32)] if nk > 1 else [],
        compiler_params=_params(("parallel", "parallel", "arbitrary")),
    )(*([a, b] + ([add] if has_add else [])))


def _h_fn(x, g, m):
    shift, scale = m[:, 0:D], m[:, D:2 * D]
    r = lax.rsqrt(jnp.mean(x * x, axis=-1, keepdims=True) + EPS)
    return (x * r * g) * (1.0 + scale) + shift


def _norm_fwd(x, g, mod, bmod, *, name):
    rows = x.shape[0]
    t = min(512, rows)

    def body(x_ref, g_ref, m_ref, b_ref, h_ref):
        h_ref[...] = _h_fn(x_ref[...], g_ref[...], m_ref[...] + b_ref[...]).astype(BF16)

    vec = lambda w: pl.BlockSpec((1, w), lambda i: (0, 0))
    return pl.pallas_call(
        body, name=name, grid=(rows // t,),
        in_specs=[pl.BlockSpec((t, D), lambda i: (i, 0)), vec(D), vec(3 * D), vec(3 * D)],
        out_specs=pl.BlockSpec((t, D), lambda i: (i, 0)),
        out_shape=jax.ShapeDtypeStruct((rows, D), BF16),
        compiler_params=_params(("parallel",)),
    )(x, g, mod, bmod)


def _norm_bwd(x, g, mod, bmod, dh_parts, resid, *, name):
    rows = x.shape[0]
    t = min(512, rows)
    n_parts = len(dh_parts)
    has_resid = resid is not None

    def body(*refs):
        x_ref, g_ref, m_ref, b_ref = refs[:4]
        parts = refs[4:4 + n_parts]
        r_ref = refs[4 + n_parts] if has_resid else None
        dx_ref, dg_ref, dm_ref = refs[-3:]
        i = pl.program_id(0)
        dh = parts[0][...]
        for p in parts[1:]:
            dh = dh + p[...]
        _, vjp = jax.vjp(_h_fn, x_ref[...], g_ref[...], m_ref[...] + b_ref[...])
        dx, dg, dm = vjp(dh)
        if has_resid:
            dx = dx + r_ref[...]
        dx_ref[...] = dx

        @pl.when(i == 0)
        def _():
            dg_ref[...] = dg
            dm_ref[...] = dm

        @pl.when(i > 0)
        def _():
            dg_ref[...] += dg
            dm_ref[...] += dm

    vec = lambda w: pl.BlockSpec((1, w), lambda i: (0, 0))
    tile = pl.BlockSpec((t, D), lambda i: (i, 0))
    ins = [x, g, mod, bmod, *dh_parts] + ([resid] if has_resid else [])
    return pl.pallas_call(
        body, name=name, grid=(rows // t,),
        in_specs=[tile, vec(D), vec(3 * D), vec(3 * D)] + [tile] * (n_parts + int(has_resid)),
        out_specs=[tile, vec(D), vec(3 * D)],
        out_shape=[jax.ShapeDtypeStruct((rows, D), F32), jax.ShapeDtypeStruct((1, D), F32),
                   jax.ShapeDtypeStruct((1, 3 * D), F32)],
        compiler_params=_params(("arbitrary",)),
    )(*ins)


def _conv_tile(u_ref, r0, t, rows, w0, w1, w2):
    u = u_ref[pl.ds(r0, t), :]
    prev8 = u_ref[pl.ds(pl.multiple_of(jnp.maximum(r0 - 8, 0), 8), 8), :]
    next8 = u_ref[pl.ds(pl.multiple_of(jnp.minimum(r0 + t, rows - 8), 8), 8), :]
    r8 = lax.broadcasted_iota(jnp.int32, (8, DH), 0)
    prev_row = jnp.sum(jnp.where(r8 == 7, prev8, 0.0), axis=0, keepdims=True)
    next_row = jnp.sum(jnp.where(r8 == 0, next8, 0.0), axis=0, keepdims=True)
    prev_row = jnp.where(r0 > 0, prev_row, 0.0)
    next_row = jnp.where(r0 + t < rows, next_row, 0.0)
    ri = lax.broadcasted_iota(jnp.int32, (t, DH), 0)
    um1 = jnp.where(ri == 0, prev_row, pltpu.roll(u, 1, 0))
    up1 = jnp.where(ri == t - 1, next_row, pltpu.roll(u, t - 1, 0))
    return w0 * um1 + w1 * u + w2 * up1, um1, u, up1


def _rowlocal(z, is_norm):
    y = _silu(z)
    yn = y * lax.rsqrt(jnp.sum(y * y, axis=-1, keepdims=True) + EPS)
    return jnp.where(is_norm, yn, y)


def _prep_fwd(p, wconv, n_norm, *, name):
    rows, nb = p.shape[0], p.shape[1] // DH
    t = min(512, rows)

    def body(u_ref, w_ref, o_ref):
        is_norm = pl.program_id(0) < n_norm
        w0, w1, w2 = w_ref[0:1, :], w_ref[1:2, :], w_ref[2:3, :]

        def step(s, carry):
            r0 = pl.multiple_of(s * t, t)
            z, _, _, _ = _conv_tile(u_ref, r0, t, rows, w0, w1, w2)
            o_ref[0, pl.ds(r0, t), :] = _rowlocal(z, is_norm)
            return carry

        lax.fori_loop(0, rows // t, step, 0)

    return pl.pallas_call(
        body, name=name, grid=(nb,),
        in_specs=[pl.BlockSpec((rows, DH), lambda j: (0, j)), pl.BlockSpec((3, DH), lambda j: (0, j))],
        out_specs=pl.BlockSpec((1, rows, DH), lambda j: (j, 0, 0)),
        out_shape=jax.ShapeDtypeStruct((nb, rows, DH), F32),
        compiler_params=_params(("parallel",)),
    )(p, wconv)


def _prep_bwd(p, wconv, d_a, d_b, n_norm, *, name):
    rows, nb = p.shape[0], p.shape[1] // DH
    t = min(512, rows)

    def body(u_ref, w_ref, da_ref, db_ref, du_ref, dw_ref, dz_ref):
        is_norm = pl.program_id(0) < n_norm
        w0, w1, w2 = w_ref[0:1, :], w_ref[1:2, :], w_ref[2:3, :]

        def step1(s, carry):
            a0, a1, a2 = carry
            r0 = pl.multiple_of(s * t, t)
            z, um1, u, up1 = _conv_tile(u_ref, r0, t, rows, w0, w1, w2)
            _, vjp = jax.vjp(lambda zz: _rowlocal(zz, is_norm), z)
            (dz,) = vjp(da_ref[0, pl.ds(r0, t), :] + db_ref[0, pl.ds(r0, t), :])
            dz_ref[pl.ds(r0, t), :] = dz
            red = lambda v: jnp.sum(v, axis=0, keepdims=True)
            return a0 + red(dz * um1), a1 + red(dz * u), a2 + red(dz * up1)

        zero = jnp.zeros((1, DH), F32)
        a0, a1, a2 = lax.fori_loop(0, rows // t, step1, (zero, zero, zero))
        dw_ref[0:1, :] = a0
        dw_ref[1:2, :] = a1
        dw_ref[2:3, :] = a2

        def step2(s, carry):
            r0 = pl.multiple_of(s * t, t)
            du, _, _, _ = _conv_tile(dz_ref, r0, t, rows, w2, w1, w0)
            du_ref[pl.ds(r0, t), :] = du.astype(BF16)
            return carry

        lax.fori_loop(0, rows // t, step2, 0)

    col = pl.BlockSpec((rows, DH), lambda j: (0, j))
    w_spec = pl.BlockSpec((3, DH), lambda j: (0, j))
    d_spec = pl.BlockSpec((1, rows, DH), lambda j: (j, 0, 0))
    return pl.pallas_call(
        body, name=name, grid=(nb,),
        in_specs=[col, w_spec, d_spec, d_spec], out_specs=[col, w_spec],
        out_shape=[jax.ShapeDtypeStruct((rows, nb * DH), BF16), jax.ShapeDtypeStruct((3, nb * DH), F32)],
        scratch_shapes=[pltpu.VMEM((rows, DH), F32)],
        compiler_params=_params(("parallel",)),
    )(p, wconv, d_a, d_b)


def _gates_fn(pab, avec, dvec):
    t = pab.shape[0]
    lane = lax.broadcasted_iota(jnp.int32, pab.shape, 1)
    xg = pab + dvec
    sp = jnp.maximum(xg, 0.0) + jnp.log(1.0 + jnp.exp(-jnp.abs(xg)))
    g = jnp.where(lane < 2 * H, -jnp.exp(avec) * sp, 0.0)
    ii = lax.broadcasted_iota(jnp.int32, (t, t), 0)
    jj = lax.broadcasted_iota(jnp.int32, (t, t), 1)
    same = (ii >> LOG_CH) == (jj >> LOG_CH)
    cum_f = _dot(jnp.where(same & (jj <= ii), 1.0, 0.0), g, precision=HI)
    cum_r = _dot(jnp.where(same & (jj >= ii), 1.0, 0.0), g, precision=HI)
    return jnp.where(lane < H, cum_f, jnp.where(lane < 2 * H, cum_r, jnp.where(lane < 4 * H, jax.nn.sigmoid(pab), 0.0)))


def _gates_fwd(pab, avec, dvec, *, name):
    rows = pab.shape[0]
    t = min(512, rows)

    def body(p_ref, a_ref, d_ref, o_ref):
        o_ref[...] = _gates_fn(p_ref[...], a_ref[...], d_ref[...])

    vec = pl.BlockSpec((1, DH), lambda i: (0, 0))
    tile = pl.BlockSpec((t, DH), lambda i: (i, 0))
    return pl.pallas_call(
        body, name=name, grid=(rows // t,), in_specs=[tile, vec, vec], out_specs=tile,
        out_shape=jax.ShapeDtypeStruct((rows, DH), F32), compiler_params=_params(("parallel",)),
    )(pab, avec, dvec)


def _gates_bwd(pab, avec, dvec, d_a, d_b, *, name):
    rows = pab.shape[0]
    t = min(512, rows)

    def body(p_ref, a_ref, d_ref, da_ref, db_ref, dp_ref, dav_ref, ddv_ref):
        i = pl.program_id(0)
        _, vjp = jax.vjp(_gates_fn, p_ref[...], a_ref[...], d_ref[...])
        dp, dav, ddv = vjp(da_ref[...] + db_ref[...])
        dp_ref[...] = dp.astype(BF16)

        @pl.when(i == 0)
        def _():
            dav_ref[...] = dav
            ddv_ref[...] = ddv

        @pl.when(i > 0)
        def _():
            dav_ref[...] += dav
            ddv_ref[...] += ddv

    vec = pl.BlockSpec((1, DH), lambda i: (0, 0))
    tile = pl.BlockSpec((t, DH), lambda i: (i, 0))
    return pl.pallas_call(
        body, name=name, grid=(rows // t,), in_specs=[tile, vec, vec, tile, tile], out_specs=[tile, vec, vec],
        out_shape=[jax.ShapeDtypeStruct((rows, DH), BF16), jax.ShapeDtypeStruct((1, DH), F32),
                   jax.ShapeDtypeStruct((1, DH), F32)],
        compiler_params=_params(("arbitrary",)),
    )(pab, avec, dvec, d_a, d_b)


def _dot_general(a, b, dims, precision=None):
    return lax.dot_general(a, b, (dims, ((), ())), precision=precision, preferred_element_type=F32)


@functools.partial(jax.custom_vjp, nondiff_argnums=(2,))
def _dot_bf16(a, b, dims):
    return _dot_general(a.astype(BF16), b.astype(BF16), dims)


def _dot_bf16_fwd(a, b, dims):
    return _dot_bf16(a, b, dims), (a, b)


def _dot_bf16_bwd(dims, res, g):
    a, b = res
    (ca,), (cb,) = dims
    da = _dot_bf16(g, b, ((1,), (1 - cb,))) if ca == 1 else _dot_bf16(b, g, ((1 - cb,), (1,)))
    db = _dot_bf16(a, g, ((1 - ca,), (0,))) if cb == 0 else _dot_bf16(g, a, ((0,), (1 - ca,)))
    return da, db


_dot_bf16.defvjp(_dot_bf16_fwd, _dot_bf16_bwd)


def _dot(a, b, dims=((1,), (0,)), precision=None):
    if precision is None and a.dtype == F32 and b.dtype == F32:
        return _dot_bf16(a, b, dims)
    return _dot_general(a, b, dims, precision)


_NT = ((1,), (1,))
_TN = ((0,), (0,))


@jax.custom_vjp
def _saved_inverse(neg_a, inv):
    return inv


def _saved_inverse_fwd(neg_a, inv):
    return inv, inv


def _saved_inverse_bwd(inv, d_inv):
    idx = range(len(inv))
    left = [_dot(inv[i], d_inv[i], _TN) for i in idx]
    d_neg_a = [_dot(left[i], inv[i], _NT) for i in idx]
    return d_neg_a, [jnp.zeros_like(t) for t in inv]


_saved_inverse.defvjp(_saved_inverse_fwd, _saved_inverse_bwd)


def _pairs(s, q, k, v, gcol, bcol, revs, inv_saved=None):
    idx = range(len(revs))
    ii = lax.broadcasted_iota(jnp.int32, (PAIR, PAIR), 0)
    jj = lax.broadcasted_iota(jnp.int32, (PAIR, PAIR), 1)
    same = (ii >> LOG_CH) == (jj >> LOG_CH)
    incl_d = (same & (ii >= jj), same & (ii <= jj))
    strict_d = (same & (ii > jj), same & (ii < jj))
    incl = [incl_d[int(r)] for r in revs]
    strict = [strict_d[int(r)] for r in revs]
    eye = jnp.where(ii == jj, 1.0, 0.0)
    gc_i = [jnp.broadcast_to(gcol[i], (PAIR, DH)) for i in idx]
    gc_j = [gc_i[i].T for i in idx]
    decay = [jnp.where(incl[i], jnp.exp(jnp.where(incl[i], gc_i[i] - gc_j[i], 0.0)), 0.0) for i in idx]
    b_b = [jnp.broadcast_to(bcol[i], (PAIR, DH)) for i in idx]
    kb = [k[i] * b_b[i] for i in idx]
    kk = [_dot(kb[i], k[i], _NT) for i in idx]
    bp = [jnp.where(strict[i], -kk[i] * decay[i], 0.0) for i in idx]
    if inv_saved is not None:
        inv = _saved_inverse(bp, inv_saved)
    else:
        low = bp
        for prec_sq, prec_acc in PREC_POWERS:
            bp = [_dot(bp[i], bp[i], precision=prec_sq) for i in idx]
            more = [_dot(low[i], bp[i], precision=prec_acc) for i in idx]
            low = [low[i] + bp[i] + more[i] for i in idx]
        inv = [eye + low[i] for i in idx]
    eg = [jnp.exp(gc_i[i]) for i in idx]
    sol = [_dot(inv[i], jnp.concatenate([v[i] * b_b[i], kb[i] * eg[i]], axis=1)) for i in idx]
    u_val = [sol[i][:, :DH] for i in idx]
    w_key = [sol[i][:, DH:] for i in idx]
    row = lax.broadcasted_iota(jnp.int32, (PAIR, 1), 0)
    has_q = q[0] is not None
    if has_q:
        qc = [q[i] * (DH ** -0.5) for i in idx]
        qk = [_dot(qc[i], k[i], _NT) for i in idx]
        attn = [qk[i] * decay[i] for i in idx]
        qd = [qc[i] * eg[i] for i in idx]
    outs = [[None, None] for _ in idx]
    zeros = jnp.zeros((CH, DH), F32)
    for step in range(2):
        cidx = [(1 - step) if revs[i] else step for i in idx]
        sl = [slice(c * CH, (c + 1) * CH) for c in cidx]
        last = [c * CH if revs[i] else c * CH + CH - 1 for i, c in zip(idx, cidx)]
        gl = [jnp.sum(jnp.where(row == last[i], gcol[i], 0.0), axis=0, keepdims=True) for i in idx]
        k_tail = [k[i][sl[i]] * jnp.exp(gl[i] - gc_i[i][sl[i]]) for i in idx]
        ws = [_dot(w_key[i][sl[i]], s[i]) for i in idx]
        v_new = [u_val[i][sl[i]] - ws[i] for i in idx]
        if has_q:
            v_pad = [jnp.concatenate([v_new[i], zeros] if cidx[i] == 0 else [zeros, v_new[i]], axis=0) for i in idx]
            o_state = [_dot(qd[i][sl[i]], s[i]) for i in idx]
            o_local = [_dot(attn[i][sl[i]], v_pad[i]) for i in idx]
            for i in idx:
                outs[i][cidx[i]] = o_state[i] + o_local[i]
        kv = [_dot(k_tail[i], v_new[i], _TN) for i in idx]
        s = [s[i] * jnp.exp(gl[i]) + kv[i] for i in idx]
    return s, ([jnp.concatenate(outs[i], axis=0) for i in idx] if has_q else None), inv


def _lane_col(tile, idx):
    lane = lax.broadcasted_iota(jnp.int32, tile.shape, 1)
    return jnp.sum(jnp.where(lane == idx, tile, 0.0), axis=1, keepdims=True)


def _gdn_fwd(qkv, gb, s0f, s0b, has_q, *, name):
    nb, rows, _ = qkv.shape
    n = rows // PAIR
    qoff = H if has_q else 0

    def body(qf_ref, qb_ref, gf_ref, gr_ref, s0f_ref, s0b_ref, of_ref, ob_ref, ssf_ref, ssb_ref, tsf_ref, tsb_ref,
             sf_ref, sb_ref):
        @pl.when(pl.program_id(0) == 0)
        def _():
            sf_ref[...] = s0f_ref[...]
            sb_ref[...] = s0b_ref[...]

        gtiles = (gf_ref[...], gr_ref[...])

        dirs = ((qf_ref, sf_ref, ssf_ref, of_ref), (qb_ref, sb_ref, ssb_ref, ob_ref))
        ts_refs = (tsf_ref, tsb_ref)

        def heads(hg, carry):
            work = [(hg * HEADS_PER_ITER_FWD + j, d) for j in range(HEADS_PER_ITER_FWD) for d in range(2)]
            loaded = []
            for h, d in work:
                q_ref, s_ref, _, _ = dirs[d]
                loaded.append((s_ref[h], q_ref[h] if has_q else None, q_ref[qoff + h], q_ref[qoff + H + h],
                               _lane_col(gtiles[d], d * H + h), _lane_col(gtiles[d], 2 * H + d * H + h)))
            s_new, o, inv = _pairs(*[list(col) for col in zip(*loaded)], revs=[d == 1 for _, d in work])
            for i, (h, d) in enumerate(work):
                _, s_ref, ss_ref, o_ref = dirs[d]
                ss_ref[0, h] = loaded[i][0]
                ts_refs[d][0, h] = inv[i]
                s_ref[h] = s_new[i]
                o_ref[h] = o[i] if has_q else jnp.zeros((PAIR, DH), F32)
            return carry

        if HEADS_PER_ITER_FWD == H:
            heads(0, 0)
        else:
            lax.fori_loop(0, H // HEADS_PER_ITER_FWD, heads, 0)

    fwd3 = lambda i: (0, i, 0)
    rev3 = lambda i: (0, n - 1 - i, 0)
    state = pl.BlockSpec((H, DH, DH), lambda i: (0, 0, 0))
    saved = pl.BlockSpec((1, H, DH, DH), lambda i: (i, 0, 0, 0))
    return pl.pallas_call(
        body, name=name, grid=(n,),
        in_specs=[pl.BlockSpec((nb, PAIR, DH), fwd3), pl.BlockSpec((nb, PAIR, DH), rev3),
                  pl.BlockSpec((PAIR, DH), lambda i: (i, 0)), pl.BlockSpec((PAIR, DH), lambda i: (n - 1 - i, 0)),
                  state, state],
        out_specs=[pl.BlockSpec((H, PAIR, DH), fwd3), pl.BlockSpec((H, PAIR, DH), rev3), saved, saved, saved, saved,
                   state, state],
        out_shape=[jax.ShapeDtypeStruct((H, rows, DH), F32)] * 2 + [jax.ShapeDtypeStruct((n, H, DH, DH), F32)] * 4
        + [jax.ShapeDtypeStruct((H, DH, DH), F32)] * 2,
        compiler_params=_params(("arbitrary",)),
    )(qkv, qkv, gb, gb, s0f, s0b)


def _gdn_bwd(qkv, gb, ssf, ssb, tsf, tsb, do, dsf, dsb, has_q, *, name):
    nb, rows, _ = qkv.shape
    n = rows // PAIR
    qoff = H if has_q else 0

    def body(qf_ref, qb_ref, gf_ref, gr_ref, ssf_ref, ssb_ref, tsf_ref, tsb_ref, dof_ref, dob_ref, dsf0_ref, dsb0_ref,
             dqf_ref, dqb_ref, dgf_ref, dgr_ref, dsf_ref, dsb_ref):
        ts_refs = (tsf_ref, tsb_ref)
        @pl.when(pl.program_id(0) == 0)
        def _():
            dsf_ref[...] = dsf0_ref[...]
            dsb_ref[...] = dsb0_ref[...]

        gtiles = (gf_ref[...], gr_ref[...])
        lane = lax.broadcasted_iota(jnp.int32, (PAIR, DH), 1)

        dirs = ((qf_ref, ssf_ref, dof_ref, dsf_ref, dqf_ref), (qb_ref, ssb_ref, dob_ref, dsb_ref, dqb_ref))

        def heads(hg, carry):
            out = list(carry)
            work = [(hg * HEADS_PER_ITER_BWD + j, d) for j in range(HEADS_PER_ITER_BWD) for d in range(2)]
            revs = [d == 1 for _, d in work]
            s_in, q_in, k_in, v_in, g_in, b_in, ds_out, do_out, inv_in = [], [], [], [], [], [], [], [], []
            for h, d in work:
                q_ref, ss_ref, do_ref, ds_ref, _ = dirs[d]
                s_in.append(ss_ref[0, h])
                inv_in.append(ts_refs[d][0, h])
                q_in.append(q_ref[h] if has_q else None)
                k_in.append(q_ref[qoff + h])
                v_in.append(q_ref[qoff + H + h])
                g_in.append(_lane_col(gtiles[d], d * H + h))
                b_in.append(_lane_col(gtiles[d], 2 * H + d * H + h))
                ds_out.append(ds_ref[h])
                do_out.append(do_ref[h] if has_q else None)
            if has_q:
                _, vjp = jax.vjp(lambda s_, q_, k_, v_, g_, b_: _pairs(s_, q_, k_, v_, g_, b_, revs, inv_in)[:2],
                                 s_in, q_in, k_in, v_in, g_in, b_in)
                ds, dq, dk, dv, dg, db = vjp((ds_out, do_out))
            else:
                _, vjp = jax.vjp(lambda s_, k_, v_, g_, b_: _pairs(s_, q_in, k_, v_, g_, b_, revs, inv_in)[0],
                                 s_in, k_in, v_in, g_in, b_in)
                ds, dk, dv, dg, db = vjp(ds_out)
            for i, (h, d) in enumerate(work):
                dq_ref, ds_ref = dirs[d][4], dirs[d][3]
                ds_ref[h] = ds[i]
                if has_q:
                    dq_ref[h] = dq[i]
                dq_ref[qoff + h], dq_ref[qoff + H + h] = dk[i], dv[i]
                out[d] = (out[d] + jnp.where(lane == d * H + h, dg[i], 0.0)
                          + jnp.where(lane == 2 * H + d * H + h, db[i], 0.0))
            return tuple(out)

        zero = jnp.zeros((PAIR, DH), F32)
        if HEADS_PER_ITER_BWD == H:
            dgf, dgr = heads(0, (zero, zero))
        else:
            dgf, dgr = lax.fori_loop(0, H // HEADS_PER_ITER_BWD, heads, (zero, zero))
        dgf_ref[...] = dgf
        dgr_ref[...] = dgr

    fwd3 = lambda i: (0, n - 1 - i, 0)
    rev3 = lambda i: (0, i, 0)
    state = pl.BlockSpec((H, DH, DH), lambda i: (0, 0, 0))
    saved = pl.BlockSpec((1, H, DH, DH), lambda i: (n - 1 - i, 0, 0, 0))
    gf_spec = pl.BlockSpec((PAIR, DH), lambda i: (n - 1 - i, 0))
    gr_spec = pl.BlockSpec((PAIR, DH), lambda i: (i, 0))
    return pl.pallas_call(
        body, name=name, grid=(n,),
        in_specs=[pl.BlockSpec((nb, PAIR, DH), fwd3), pl.BlockSpec((nb, PAIR, DH), rev3), gf_spec, gr_spec,
                  saved, saved, saved, saved,
                  pl.BlockSpec((H, PAIR, DH), fwd3), pl.BlockSpec((H, PAIR, DH), rev3), state, state],
        out_specs=[pl.BlockSpec((nb, PAIR, DH), fwd3), pl.BlockSpec((nb, PAIR, DH), rev3), gf_spec, gr_spec, state, state],
        out_shape=[jax.ShapeDtypeStruct((nb, rows, DH), F32)] * 2 + [jax.ShapeDtypeStruct((rows, DH), F32)] * 2
        + [jax.ShapeDtypeStruct((H, DH, DH), F32)] * 2,
        compiler_params=_params(("arbitrary",)),
    )(qkv, qkv, gb, gb, ssf, ssb, tsf, tsb, do, do, dsf, dsb)


def _stage1(zb, ua, va, za, o, gon, lng, lnb, wsp, bsp):
    gv = [_gelu(t) for t in va]
    mu = sum(jnp.sum(t, axis=-1, keepdims=True) for t in gv) * (1.0 / D)
    xc = [t - mu for t in gv]
    var = sum(jnp.sum(t * t, axis=-1, keepdims=True) for t in xc) * (1.0 / D)
    rs = lax.rsqrt(var + EPS)
    ya, yb = [], []
    for g in range(H):
        vv = xc[g] * rs * lng[g] + lnb[g]
        s = _dot(wsp[g], vv) + bsp[g]
        ya.append(_gelu(ua[g]) * s * _silu(za[g]))
        r = lax.rsqrt(jnp.mean(o[g] * o[g], axis=-1, keepdims=True) + EPS)
        yb.append(o[g] * r * gon * _silu(zb[g]))
    return ya, yb


def _stage2(ma, mb, ga, gb):
    return jax.nn.sigmoid(ga) * ma + jax.nn.sigmoid(gb) * mb


def _stage3(out, x, tgt, gpost, gate):
    r = out * lax.rsqrt(jnp.mean(out * out, axis=-1, keepdims=True) + EPS) * gpost
    err = x + gate * r - tgt
    return 0.5 * jnp.sum(jnp.mean(err * err, axis=-1, keepdims=True), axis=0, keepdims=True)


def _post(p_rest, o_f, o_b, x, tgt, mod, bmod, gon, lng, lnb, wsp, bspb, wpa, wpb, wout, gpost, *, name):
    rows = x.shape[0]
    n = rows // GM
    lanes = lambda g: slice(g * DH, (g + 1) * DH)
    bdot = lambda a, w_ref: _dot(a.astype(BF16), w_ref[...])
    bdot_t = lambda a, w_ref: _dot(a.astype(BF16), w_ref[...], _NT)

    def body(p_ref, of_ref, ob_ref, x_ref, t_ref, m_ref, bm_ref, gon_ref, lng_ref, lnb_ref, wsp_ref, bsp_ref,
             wpa_ref, wpb_ref, wout_ref, gp_ref,
             loss_ref, dp_ref, do_ref, dx_ref, ya_ref, yb_ref, mg_ref, dma_ref, dmb_ref, dout_ref,
             dvec_ref, dgon_ref, dwsp_ref, dbsp_ref):
        @pl.when(pl.program_id(0) == 0)
        def _():
            loss_ref[...] = jnp.zeros_like(loss_ref)
            dvec_ref[...] = jnp.zeros_like(dvec_ref)
            dgon_ref[...] = jnp.zeros_like(dgon_ref)
            dwsp_ref[...] = jnp.zeros_like(dwsp_ref)
            dbsp_ref[...] = jnp.zeros_like(dbsp_ref)

        piece = lambda blk: [p_ref[:, blk * D + g * DH: blk * D + (g + 1) * DH].astype(F32) for g in range(H)]
        zb, ua, va, za = piece(0), piece(1), piece(2), piece(3)
        o = [of_ref[g] + ob_ref[g] for g in range(H)]
        gon = gon_ref[...]
        lng = [lng_ref[:, lanes(g)] for g in range(H)]
        lnb = [lnb_ref[:, lanes(g)] for g in range(H)]
        wsp = [wsp_ref[g] for g in range(H)]
        bsp = [bsp_ref[g] for g in range(H)]
        (ya, yb), vjp1 = jax.vjp(_stage1, zb, ua, va, za, o, gon, lng, lnb, wsp, bsp)
        y_a, y_b = jnp.concatenate(ya, axis=1), jnp.concatenate(yb, axis=1)
        ma, mb = bdot(y_a, wpa_ref), bdot(y_b, wpb_ref)
        ga, gb = p_ref[:, 4 * D:5 * D].astype(F32), p_ref[:, 5 * D:6 * D].astype(F32)
        merged, vjp2 = jax.vjp(_stage2, ma, mb, ga, gb)
        out = bdot(merged, wout_ref)
        gate = m_ref[:, 2 * D:3 * D] + bm_ref[:, 2 * D:3 * D]
        loss, vjp3 = jax.vjp(_stage3, out, x_ref[...], t_ref[...], gp_ref[...], gate)
        loss_ref[...] += jnp.broadcast_to(loss, loss_ref.shape)

        dout, dx, _, dgpost, dgate = vjp3(jnp.ones((1, 1), F32))
        dx_ref[...] = dx
        dmerged = bdot_t(dout, wout_ref)
        dma, dmb, dga, dgb = vjp2(dmerged)
        dya, dyb = bdot_t(dma, wpa_ref), bdot_t(dmb, wpb_ref)
        dzb, dua, dva, dza, do, dgon, dlng, dlnb, dwsp, dbsp = vjp1(
            ([dya[:, lanes(g)] for g in range(H)], [dyb[:, lanes(g)] for g in range(H)]))

        for blk, dlist in enumerate((dzb, dua, dva, dza)):
            for g in range(H):
                dp_ref[:, blk * D + g * DH: blk * D + (g + 1) * DH] = dlist[g].astype(BF16)
        dp_ref[:, 4 * D:5 * D] = dga.astype(BF16)
        dp_ref[:, 5 * D:6 * D] = dgb.astype(BF16)
        for g in range(H):
            do_ref[g] = do[g]
            dwsp_ref[g] += dwsp[g]
            dbsp_ref[g] += dbsp[g]
            dvec_ref[2:3, lanes(g)] += dlng[g]
            dvec_ref[3:4, lanes(g)] += dlnb[g]
        dvec_ref[0:1, :] += dgpost
        dvec_ref[1:2, :] += dgate
        dgon_ref[0:1, :] += dgon
        ya_ref[...] = y_a.astype(BF16)
        yb_ref[...] = y_b.astype(BF16)
        mg_ref[...] = merged.astype(BF16)
        dma_ref[...] = dma.astype(BF16)
        dmb_ref[...] = dmb.astype(BF16)
        dout_ref[...] = dout.astype(BF16)

    row = lambda w: pl.BlockSpec((GM, w), lambda i: (i, 0))
    heads = pl.BlockSpec((H, GM, DH), lambda i: (0, i, 0))
    full = lambda shape: pl.BlockSpec(shape, lambda i: tuple(0 for _ in shape))
    sds = jax.ShapeDtypeStruct
    return pl.pallas_call(
        body, name=name, grid=(n,),
        in_specs=[row(6 * D), heads, heads, row(D), row(D), full((1, 3 * D)), full((1, 3 * D)), full((1, DH)),
                  full((1, D)), full((1, D)), full((H, GM, GM)), full((H, GM, GM)),
                  full((D, D)), full((D, D)), full((D, D)), full((1, D))],
        out_specs=[full((8, DH)), row(6 * D), heads, row(D)] + [row(D)] * 6
        + [full((8, D)), full((8, DH)), full((H, GM, GM)), full((H, GM, GM))],
        out_shape=[sds((8, DH), F32), sds((rows, 6 * D), BF16), sds((H, rows, DH), F32), sds((rows, D), F32)]
        + [sds((rows, D), BF16)] * 6 + [sds((8, D), F32), sds((8, DH), F32), sds((H, GM, GM), F32), sds((H, GM, GM), F32)],
        compiler_params=_params(("arbitrary",)),
    )(p_rest, o_f, o_b, x, tgt, mod, bmod, gon, lng, lnb, wsp, bspb, wpa, wpb, wout, gpost)


def _silu_rows(c, *, name):
    def body(c_ref, o_ref):
        o_ref[...] = _silu(c_ref[...])

    return pl.pallas_call(body, name=name, out_shape=jax.ShapeDtypeStruct(c.shape, F32))(c)


def _dsilu_mul(c, d, *, name):
    def body(c_ref, d_ref, o_ref):
        _, vjp = jax.vjp(_silu, c_ref[...])
        (o_ref[...],) = vjp(d_ref[...])

    return pl.pallas_call(body, name=name, out_shape=jax.ShapeDtypeStruct(c.shape, F32))(c, d)


def _adamw(w, g, m, v, *, name):
    rows, cols = w.shape[-2:]
    t = _tile(rows if g.ndim == 2 else g.shape[1])
    c1 = 1.0 / (1.0 - ADAM_B1 ** ADAM_STEP)
    c2 = 1.0 / (1.0 - ADAM_B2 ** ADAM_STEP)

    def body(w_ref, g_ref, m_ref, v_ref, go_ref, d_ref, mo_ref, vo_ref):
        blk = lambda r: r[...].reshape(t, cols)
        gv = blk(g_ref)
        mn = ADAM_B1 * blk(m_ref) + (1.0 - ADAM_B1) * gv
        vn = ADAM_B2 * blk(v_ref) + (1.0 - ADAM_B2) * (gv * gv)
        delta = -ADAM_LR * ((mn * c1) / (jnp.sqrt(vn * c2) + ADAM_EPS) + ADAM_WD * blk(w_ref))
        for ref, val in ((go_ref, gv), (d_ref, delta), (mo_ref, mn), (vo_ref, vn)):
            ref[...] = val.reshape(ref.shape)

    tile = (pl.BlockSpec((1, t, cols), lambda i: (0, i, 0)) if w.ndim == 3 else pl.BlockSpec((t, cols), lambda i: (i, 0)))
    if g.ndim == 3:
        per = g.shape[1] // t
        g_tile = pl.BlockSpec((1, t, cols), lambda i: (i // per, i % per, 0))
    else:
        g_tile = pl.BlockSpec((t, cols), lambda i: (i, 0))
    return pl.pallas_call(
        body, name=name, grid=(rows // t,),
        in_specs=[tile, g_tile, tile, tile], out_specs=[tile] * 4,
        out_shape=[jax.ShapeDtypeStruct(w.shape, F32)] * 4,
        compiler_params=_params(("parallel",)),
    )(w, g, m, v)


def _add_sibling(full, got, where, out_dtype, *, name):
    s, rows, cols = full.shape
    hr = rows // 2
    t = _tile(hr)
    nt = hr // t

    def body(w_ref, a_ref, b_ref, o_ref):
        o_ref[...] = (a_ref[...].astype(F32) + b_ref[...].astype(F32)).astype(out_dtype)

    tile = pl.BlockSpec((1, t, cols), lambda j, i, w: (j, i, 0))
    return pl.pallas_call(
        body, name=name,
        grid_spec=pltpu.PrefetchScalarGridSpec(
            num_scalar_prefetch=1, grid=(s, nt),
            in_specs=[pl.BlockSpec((1, t, cols), lambda j, i, w: (j, w[0] * nt + i, 0)), tile], out_specs=tile),
        out_shape=jax.ShapeDtypeStruct((s, hr, cols), out_dtype), compiler_params=_params(("parallel", "parallel")),
    )(where, full, got)


def _sum_chips(own, slots, where, n_out, which, *, name):
    _, hr, cols = own.shape
    t = _tile(hr)

    def body(w_ref, a_ref, s_ref, o_ref):
        f = lambda v: v.astype(F32)
        o_ref[0] = ((f(a_ref[0]) + f(s_ref[0])) + f(s_ref[1])) + f(s_ref[2])

    return pl.pallas_call(
        body, name=name,
        grid_spec=pltpu.PrefetchScalarGridSpec(
            num_scalar_prefetch=1, grid=(hr // t,),
            in_specs=[pl.BlockSpec((1, t, cols), lambda i, w: (w[1], i, 0)),
                      pl.BlockSpec((N_CHIPS - 1, t, cols), lambda i, w: (0, i, 0))],
            out_specs=pl.BlockSpec((1, t, cols), lambda i, w: (w[which], i, 0))),
        out_shape=jax.ShapeDtypeStruct((n_out, hr, cols), F32), compiler_params=_params(("parallel",)),
    )(where, own, slots)


def _local_step(x, c, ctx, c_ctx, tgt, w_mod, b_mod, g_pre, g_post, w_qkv, w_ab, w_rest, w_conv, a_log, dt_bias,
                g_onorm, gm_ln_g, gm_ln_b, w_sp, b_sp, w_pa, w_pb, w_out):
    rows, rows_c = x.shape[0], ctx.shape[0]
    cc = jnp.zeros((16, D), F32).at[0].set(c[0]).at[1].set(c_ctx)
    scc = _silu_rows(cc, name="silu_cond")
    mod = _mm(scc, w_mod, b_sections=True, name="mod_fwd")
    mod_x, mod_c = mod[0:1], mod[1:2]
    avec = jnp.zeros((1, DH), F32).at[0, :2 * H].set(a_log.reshape(-1))
    dvec = jnp.zeros((1, DH), F32).at[0, :2 * H].set(dt_bias.reshape(-1))
    bspb = jnp.broadcast_to(b_sp[:, :, None], (H, GM, GM))
    w_kv, wconv_kv = w_qkv[:, D:], w_conv[:, D:]

    h_c = _norm_fwd(ctx, g_pre, mod_c, b_mod, name="norm_fwd_ctx")
    pc_kv = _mm(h_c, w_kv, name="inproj_ctx_kv")
    pc_ab = _mm(h_c, w_ab, name="inproj_ctx_ab")
    kv_c = _prep_fwd(pc_kv, wconv_kv, H, name="prep_fwd_ctx")
    gb_c = _gates_fwd(pc_ab, avec, dvec, name="gates_fwd_ctx")
    s_zero = jnp.zeros((H, DH, DH), F32)
    _, _, ssf_c, ssb_c, tsf_c, tsb_c, s_f, s_b = _gdn_fwd(kv_c, gb_c, s_zero, s_zero, False, name="gdn_fwd_ctx")

    h_x = _norm_fwd(x, g_pre, mod_x, b_mod, name="norm_fwd_x")
    p_qkv = _mm(h_x, w_qkv, name="inproj_qkv")
    p_ab = _mm(h_x, w_ab, name="inproj_ab")
    p_rest = _mm(h_x, w_rest, out_dtype=BF16, name="inproj_rest")
    qkv = _prep_fwd(p_qkv, w_conv, 2 * H, name="prep_fwd_x")
    gb_x = _gates_fwd(p_ab, avec, dvec, name="gates_fwd_x")
    o_f, o_b, ssf, ssb, tsf, tsb, _, _ = _gdn_fwd(qkv, gb_x, s_f, s_b, True, name="gdn_fwd_x")

    (loss_acc, dp_rest, do, dx_res, ya, yb, mg, dma, dmb, dout, dvec_post, dgon, dwsp, dbspb) = _post(
        p_rest, o_f, o_b, x, tgt, mod_x, b_mod, g_onorm, gm_ln_g, gm_ln_b, w_sp, bspb, w_pa, w_pb, w_out, g_post,
        name="post")
    g = {}
    g["w_pa"] = _mm(ya, dma, ta=True, name="dw_pa")
    g["w_pb"] = _mm(yb, dmb, ta=True, name="dw_pb")
    g["w_out"] = _mm(mg, dout, ta=True, name="dw_out")

    zeros_s = jnp.zeros((H, DH, DH), F32)
    dq_f, dq_b, dg_f, dg_b, ds0_f, ds0_b = _gdn_bwd(qkv, gb_x, ssf, ssb, tsf, tsb, do, zeros_s, zeros_s, True,
                                                    name="gdn_bwd_x")
    dp_qkv, dwc_x = _prep_bwd(p_qkv, w_conv, dq_f, dq_b, 2 * H, name="prep_bwd_x")
    dp_ab, dav_x, ddv_x = _gates_bwd(p_ab, avec, dvec, dg_f, dg_b, name="gates_bwd_x")
    dkv_f, dkv_b, dgc_f, dgc_b, _, _ = _gdn_bwd(kv_c, gb_c, ssf_c, ssb_c, tsf_c, tsb_c, jnp.zeros((H, rows_c, DH), F32),
                                                 ds0_f, ds0_b, False, name="gdn_bwd_ctx")
    dpc_kv, dwc_c = _prep_bwd(pc_kv, wconv_kv, dkv_f, dkv_b, H, name="prep_bwd_ctx")
    dpc_ab, dav_c, ddv_c = _gates_bwd(pc_ab, avec, dvec, dgc_f, dgc_b, name="gates_bwd_ctx")

    dw_kv_c = _mm(h_c, dpc_kv, ta=True, name="dw_kv_ctx")
    dw_qkv = _mm(h_x, dp_qkv, ta=True, tn=D, add=dw_kv_c, add_from=1, name="dw_qkv")
    dw_ab_c = _mm(h_c, dpc_ab, ta=True, name="dw_ab_ctx")
    dw_ab = _mm(h_x, dp_ab, ta=True, add=dw_ab_c, name="dw_ab")
    dw_rest = _mm(h_x, dp_rest, ta=True, name="dw_rest")
    g["w_in"] = (dw_qkv, dw_ab[:, :4 * H], dw_rest)
    dh_parts = [_mm(dp_qkv, w_qkv, tb=True, tk=3 * D // 2, name="dh_qkv"),
                _mm(dp_rest, w_rest, tb=True, tk=3 * D // 2, name="dh_rest"), _mm(dp_ab, w_ab, tb=True, name="dh_ab")]
    grad_x, dgpre_x, dm_x = _norm_bwd(x, g_pre, mod_x, b_mod, dh_parts, dx_res, name="norm_bwd_x")
    dhc_parts = [_mm(dpc_kv, w_kv, tb=True, name="dhc_kv"), _mm(dpc_ab, w_ab, tb=True, name="dhc_ab")]
    _, dgpre_c, dm_c = _norm_bwd(ctx, g_pre, mod_c, b_mod, dhc_parts, None, name="norm_bwd_ctx")

    dm_x = dm_x.at[:, 2 * D:].add(dvec_post[1:2])
    dmod = jnp.zeros((16, 3 * D), F32).at[0].set(dm_x[0]).at[1].set(dm_c[0])
    g["mod_factors"] = (scc[0], dm_x[0])
    g["dm_ctx"] = dm_c[0]
    g["silu_c_ctx"] = scc[1:2]
    dcc = _mm(dmod, w_mod, tb=True, b_sections=True, name="dcc")
    g["c_ctx"] = _dsilu_mul(cc[:8], dcc[:8], name="dc_ctx")[1]
    g["b_mod"] = dm_x + dm_c
    g["g_pre"] = dgpre_x + dgpre_c
    g["g_post"] = dvec_post[0:1]
    g["gm_ln_g"], g["gm_ln_b"] = dvec_post[2:3], dvec_post[3:4]
    g["g_onorm"] = dgon[0:1]
    g["w_sp"] = dwsp
    g["b_sp"] = jnp.sum(dbspb, axis=-1)
    g["w_conv"] = dwc_x.at[:, D:].add(dwc_c)
    g["a_log"] = (dav_x + dav_c)[0, :2 * H].reshape(2, H)
    g["dt_bias"] = (ddv_x + ddv_c)[0, :2 * H].reshape(2, H)
    return loss_acc[0, 0], grad_x, g


ANY = pl.BlockSpec(memory_space=pl.ANY)


def _place():
    x, y, c = lax.axis_index("x"), lax.axis_index("y"), lax.axis_index("c")
    chips = [(1 - x, y), (x, 1 - y), (1 - x, 1 - y)]
    return x, y, c, (x, y, 1 - c), chips


def _gather_shards(big, small, *, name):
    nb = len(big)

    def body(*refs):
        ins, sm_in = refs[:nb], refs[nb]
        outs, sm_out = refs[nb + 1:2 * nb + 1], refs[2 * nb + 1]
        send, recv = refs[2 * nb + 2:]
        x, y, c, sibling, chips = _place()
        mine = 2 * x + y

        def half(a, shard, hc):
            hr = big[a].shape[0] // 2
            return outs[a].at[shard, pl.ds(hc * hr, hr), :]

        def remote(k, src, dst, to):
            return pltpu.make_async_remote_copy(src_ref=src, dst_ref=dst, send_sem=send.at[k], recv_sem=recv.at[k],
                                                device_id=to, device_id_type=MESH)

        sends = []
        for a in range(nb):
            hr = big[a].shape[0] // 2
            for j, chip in enumerate(chips):
                sends.append(remote(a * 3 + j, ins[a].at[pl.ds(c * hr, hr), :], half(a, mine, c), (*chip, c)))
        for j, chip in enumerate(chips):
            sends.append(remote(nb * 3 + j, sm_in, sm_out.at[mine], (*chip, c)))
        for cp in sends:
            cp.start()
        base = nb * 3 + 3
        passed = []
        for a in range(nb):
            for j, (px, py) in enumerate(chips):
                theirs = 2 * px + py
                remote(a * 3 + j, half(a, theirs, c), half(a, theirs, c), sibling).wait_recv()
                fw = remote(base + a * 3 + j, half(a, theirs, c), half(a, theirs, c), sibling)
                fw.start()
                passed.append(fw)
        for a in range(nb):
            for j, (px, py) in enumerate(chips):
                theirs = 2 * px + py
                remote(base + a * 3 + j, half(a, theirs, 1 - c), half(a, theirs, 1 - c), sibling).wait_recv()
        for j, (px, py) in enumerate(chips):
            remote(nb * 3 + j, sm_in, sm_out.at[2 * px + py], sibling).wait_recv()
        for cp in sends + passed:
            cp.wait_send()

    n_remote = 2 * nb * 3 + 3
    outs = pl.pallas_call(
        body, name=name, in_specs=[ANY] * (nb + 1), out_specs=[ANY] * (nb + 1),
        out_shape=[jax.ShapeDtypeStruct((N_CHIPS,) + a.shape, a.dtype) for a in big + [small]],
        scratch_shapes=[pltpu.SemaphoreType.DMA((n_remote,)), pltpu.SemaphoreType.DMA((n_remote,))],
    )(*big, small)
    return outs[:nb], outs[nb]


def _row_chunks(rows, row_bytes, align=8):
    n = max(1, min(rows // align, -(-rows * row_bytes // DMA_CHUNK_BYTES)))
    per = -(-(-(-rows // n)) // align) * align
    return [(r, min(per, rows - r)) for r in range(0, rows, per)]


def _remote(src, dst, send, recv, to):
    return pltpu.make_async_remote_copy(src_ref=src, dst_ref=dst, send_sem=send, recv_sem=recv, device_id=to,
                                        device_id_type=MESH)


def _exchange_halves(arrs, *, name):
    na = len(arrs)

    def body(*refs):
        ins, got = refs[:na], refs[na:2 * na]
        send, recv = refs[2 * na:]
        x, y, c, sibling, _ = _place()
        for a in range(na):
            ns, rows, cols = arrs[a].shape
            hr = rows // 2
            for s in range(ns):
                for r0, nr in _row_chunks(hr, cols * arrs[a].dtype.itemsize, 16):
                    _remote(ins[a].at[s, pl.ds((1 - c) * hr + r0, nr), :], got[a].at[s, pl.ds(r0, nr), :],
                            send.at[a], recv.at[a], sibling).start()
        for a in range(na):
            hr = arrs[a].shape[1] // 2
            _remote(ins[a].at[:, pl.ds((1 - c) * hr, hr), :], got[a], send.at[a], recv.at[a], sibling).wait()

    return pl.pallas_call(
        body, name=name, in_specs=[ANY] * na, out_specs=[ANY] * na,
        out_shape=[jax.ShapeDtypeStruct((N_CHIPS, a.shape[1] // 2, a.shape[2]), a.dtype) for a in arrs],
        scratch_shapes=[pltpu.SemaphoreType.DMA((na,)), pltpu.SemaphoreType.DMA((na,))],
    )(*arrs)


def _scatter_sections(arrs, *, name):
    na = len(arrs)

    def body(*refs):
        ins, outs = refs[:na], refs[na:2 * na]
        send, recv = refs[2 * na:]
        x, y, c, _, chips = _place()
        for a in range(na):
            _, hr, cols = arrs[a].shape
            for j, (px, py) in enumerate(chips):
                for r0, nr in _row_chunks(hr, cols * arrs[a].dtype.itemsize, 16):
                    _remote(ins[a].at[2 * px + py, pl.ds(r0, nr), :], outs[a].at[j, pl.ds(r0, nr), :],
                            send.at[a * 3 + j], recv.at[a * 3 + j], (px, py, c)).start()
        for a in range(na):
            for j, (px, py) in enumerate(chips):
                _remote(ins[a].at[2 * px + py], outs[a].at[j], send.at[a * 3 + j], recv.at[a * 3 + j], (px, py, c)).wait()

    return pl.pallas_call(
        body, name=name, in_specs=[ANY] * na, out_specs=[ANY] * na,
        out_shape=[jax.ShapeDtypeStruct((N_CHIPS - 1,) + a.shape[1:], a.dtype) for a in arrs],
        scratch_shapes=[pltpu.SemaphoreType.DMA((3 * na,)), pltpu.SemaphoreType.DMA((3 * na,))],
    )(*arrs)


def _finish_reduce(big, small, *, name):
    nb = len(big)

    def body(*refs):
        outs, sm = refs[nb + 1:2 * nb + 1], refs[2 * nb + 1]
        send, recv = refs[2 * nb + 2:]
        x, y, c, sibling, chips = _place()
        blk = lambda px, py, pc: sm.at[4 * px + 2 * py + pc]
        for a in range(nb):
            _, hr, cols = big[a].shape
            for r0, nr in _row_chunks(hr, cols * 4):
                _remote(outs[a].at[c, pl.ds(r0, nr), :], outs[a].at[c, pl.ds(r0, nr), :], send.at[a], recv.at[a],
                        sibling).start()
        first = [_remote(blk(x, y, c), blk(x, y, c), send.at[nb], recv.at[nb], sibling)]
        first += [_remote(blk(x, y, c), blk(x, y, c), send.at[nb + 1 + j], recv.at[nb + 1 + j], (*chip, c))
                  for j, chip in enumerate(chips)]
        for cp in first:
            cp.start()
        passed = []
        for j, (px, py) in enumerate(chips):
            _remote(blk(px, py, c), blk(px, py, c), send.at[nb + 1 + j], recv.at[nb + 1 + j], sibling).wait_recv()
            fw = _remote(blk(px, py, c), blk(px, py, c), send.at[nb + 4 + j], recv.at[nb + 4 + j], sibling)
            fw.start()
            passed.append(fw)
        for a in range(nb):
            _remote(outs[a].at[c], outs[a].at[1 - c], send.at[a], recv.at[a], sibling).wait()
        _remote(blk(x, y, c), blk(x, y, 1 - c), send.at[nb], recv.at[nb], sibling).wait_recv()
        for j, (px, py) in enumerate(chips):
            _remote(blk(px, py, c), blk(px, py, 1 - c), send.at[nb + 4 + j], recv.at[nb + 4 + j], sibling).wait_recv()
        for cp in first + passed:
            cp.wait_send()

    n_remote = nb + 7
    arrs = list(big) + [small]
    outs = pl.pallas_call(
        body, name=name, in_specs=[ANY] * (nb + 1), out_specs=[ANY] * (nb + 1),
        out_shape=[jax.ShapeDtypeStruct(a.shape, F32) for a in arrs],
        input_output_aliases={i: i for i in range(nb + 1)},
        scratch_shapes=[pltpu.SemaphoreType.DMA((n_remote,)), pltpu.SemaphoreType.DMA((n_remote,))],
    )(*arrs)
    return outs[:nb], outs[nb]


def _reduce_gradients(sectioned, small, where):
    arrs = list(sectioned) + [small]
    got = _exchange_halves(arrs, name="rs_exchange_halves")
    pay = [BF16] * len(sectioned) + [F32]
    chip_sum = [_add_sibling(a, g, where, p, name=f"rs_add_sibling_{i}")
                for i, (a, g, p) in enumerate(zip(arrs, got, pay))]
    slots = _scatter_sections(chip_sum, name="rs_scatter_sections")
    red = [_sum_chips(p, s, where, 2, 0, name=f"rs_sum_chips_{i}")
           for i, (p, s) in enumerate(zip(chip_sum[:-1], slots[:-1]))]
    red_small = _sum_chips(chip_sum[-1], slots[-1], where, 2 * N_CHIPS, 2, name="rs_sum_chips_small")
    big, sm = _finish_reduce(red, red_small, name="rs_finish")
    return big, sm.reshape(-1, sm.shape[-1])


def kernel(x, c, ctx, c_ctx, w_mod, b_mod, g_pre, g_post, w_in, w_conv, a_log, dt_bias, g_onorm, gm_ln_g, gm_ln_b, w_sp, b_sp, w_pa, w_pb, w_out, loss_target, m_c_ctx, m_w_mod, m_b_mod, m_g_pre, m_g_post, m_w_in, m_w_conv, m_a_log, m_dt_bias, m_g_onorm, m_gm_ln_g, m_gm_ln_b, m_w_sp, m_b_sp, m_w_pa, m_w_pb, m_w_out, v_c_ctx, v_w_mod, v_b_mod, v_g_pre, v_g_post, v_w_in, v_w_conv, v_a_log, v_dt_bias, v_g_onorm, v_gm_ln_g, v_gm_ln_b, v_w_sp, v_b_sp, v_w_pa, v_w_pb, v_w_out):
    names = ["c_ctx", "w_mod", "b_mod", "g_pre", "g_post", "w_in", "w_conv", "a_log", "dt_bias", "g_onorm", "gm_ln_g",
             "gm_ln_b", "w_sp", "b_sp", "w_pa", "w_pb", "w_out"]
    w = dict(zip(names, (c_ctx, w_mod, b_mod, g_pre, g_post, w_in, w_conv, a_log, dt_bias, g_onorm, gm_ln_g, gm_ln_b,
                         w_sp, b_sp, w_pa, w_pb, w_out)))
    m = dict(zip(names, (m_c_ctx, m_w_mod, m_b_mod, m_g_pre, m_g_post, m_w_in, m_w_conv, m_a_log, m_dt_bias, m_g_onorm,
                         m_gm_ln_g, m_gm_ln_b, m_w_sp, m_b_sp, m_w_pa, m_w_pb, m_w_out)))
    v = dict(zip(names, (v_c_ctx, v_w_mod, v_b_mod, v_g_pre, v_g_post, v_w_in, v_w_conv, v_a_log, v_dt_bias, v_g_onorm,
                         v_gm_ln_g, v_gm_ln_b, v_w_sp, v_b_sp, v_w_pa, v_w_pb, v_w_out)))
    xy = 2 * lax.axis_index("x") + lax.axis_index("y")
    where = jnp.stack([lax.axis_index("c"), xy, 2 * xy + lax.axis_index("c")]).astype(jnp.int32)

    shards = [a[0].astype(BF16) for a in (w_mod, w_in, w_pa, w_pb, w_out)]
    gathered, wconv_all = _gather_shards(shards, w_conv[0], name="gather_weights")
    own = lambda full, shard: lax.dynamic_update_slice(full, shard[None], (xy, 0, 0))
    wm_all, win_all, wpa_all, wpb_all, wout_all = [own(f, s) for f, s in zip(gathered, shards)]
    wconv_all = own(wconv_all, w_conv[0])
    w_conv_f = jnp.concatenate([wconv_all[s] for s in range(N_CHIPS)], axis=1)
    shard_cols = IN_COLS // N_CHIPS
    cut = OFF_A - shard_cols
    assert 0 < cut and cut + 4 * H < shard_cols
    w_qkv = jnp.concatenate([win_all[0], win_all[1][:, :cut]], axis=1)
    w_ab = jnp.pad(win_all[1][:, cut:cut + 4 * H], ((0, 0), (0, DH - 4 * H)))
    w_rest = jnp.concatenate([win_all[1][:, cut + 4 * H:], win_all[2], win_all[3]], axis=1)

    loss_local, grad_x, g = _local_step(
        x[0], c, ctx[0], c_ctx, loss_target[0], wm_all, b_mod, g_pre, g_post, w_qkv, w_ab, w_rest,
        w_conv_f, a_log[0], dt_bias[0], g_onorm, gm_ln_g, gm_ln_b, w_sp[0], b_sp[0],
        wpa_all.reshape(D, D), wpb_all.reshape(D, D), wout_all.reshape(D, D))
    g["loss"] = loss_local
    blk = D // N_CHIPS
    dw_qkv, dw_ab, dw_rest = g["w_in"]
    rest_cut = shard_cols - cut - 4 * H
    g_win = jnp.stack([dw_qkv[:, :shard_cols],
                       jnp.concatenate([dw_qkv[:, shard_cols:], dw_ab, dw_rest[:, :rest_cut]], axis=1),
                       dw_rest[:, rest_cut:rest_cut + shard_cols], dw_rest[:, rest_cut + shard_cols:]]).astype(BF16)
    g["_pad"] = jnp.zeros((SMALL_LAYOUT["_pad"][1],), F32)
    g["mod_factors"] = lax.dynamic_update_slice(jnp.zeros((2 * N_CHIPS, 4 * D), F32),
                                                jnp.concatenate(g["mod_factors"])[None], (where[2], 0))
    tail = jnp.zeros((N_CHIPS * SMALL_ROWS * 128 - sum(s for _, s in SMALL_LAYOUT.values()),), F32)
    flat = jnp.concatenate([g[k].reshape(-1) for k in SMALL_LAYOUT] + [tail])
    big_names = ("w_in", "w_pa", "w_pb", "w_out")
    sectioned = [g_win] + [g[k].reshape(N_CHIPS, blk, D) for k in ("w_pa", "w_pb", "w_out")]
    reduced, gr_small = _reduce_gradients(sectioned, flat.reshape(N_CHIPS, SMALL_ROWS, 128), where)
    gr_tiny = gr_small[TINY_ROW0:]

    def entry(arr, k):
        off, size = SMALL_LAYOUT[k]
        row, col = off // 128 - TINY_ROW0, off % 128
        return arr[row:row + size // 128].reshape(-1) if size >= 128 else arr[row, col:col + size]

    factors = entry(gr_tiny, "mod_factors").reshape(2 * N_CHIPS, 4 * D)
    pad_rows = lambda a: jnp.pad(a, ((0, 16 - a.shape[0]), (0, 0)))
    lhs = pad_rows(jnp.concatenate([factors[:, :D], g["silu_c_ctx"]], axis=0))
    rhs = pad_rows(jnp.concatenate([factors[:, D:], entry(gr_tiny, "dm_ctx")[None]], axis=0))
    mod_cols = 3 * D // N_CHIPS
    gr_wm = _mm(lhs, lax.dynamic_slice(rhs, (0, xy * mod_cols), (16, mod_cols)), ta=True, name="dw_mod")

    res = {k: _adamw(w[k], gr, m[k], v[k], name=f"adamw_{k}") for k, gr in zip(big_names, reduced)}
    res["w_mod"] = _adamw(w_mod[0], gr_wm, m_w_mod[0], v_w_mod[0], name="adamw_w_mod")
    res["w_sp"] = _adamw(w_sp.reshape(-1, 128), gr_small, m_w_sp.reshape(-1, 128), v_w_sp.reshape(-1, 128),
                         name="adamw_w_sp")
    tiny_names = [k for k in SMALL_LAYOUT if k not in ("w_sp", "w_conv", "dm_ctx", "mod_factors", "loss", "_pad")]

    def pack(src):
        parts = [src[k].reshape(-1) if k in tiny_names else jnp.zeros((s,), F32)
                 for k, (_, s) in SMALL_LAYOUT.items() if k != "w_sp"]
        return jnp.concatenate(parts + [tail]).reshape(-1, 128)

    tiny_res = _adamw(pack(w), gr_tiny, pack(m), pack(v), name="adamw_tiny")
    for k in tiny_names:
        res[k] = [entry(r, k) for r in tiny_res]
    g_conv = lax.dynamic_slice(entry(gr_tiny, "w_conv").reshape(3, 3 * D), (0, xy * (3 * D // N_CHIPS)),
                               (3, 3 * D // N_CHIPS))
    conv_res = _adamw(jnp.pad(w_conv[0], ((0, 5), (0, 0))), jnp.pad(g_conv, ((0, 5), (0, 0))),
                      jnp.pad(m_w_conv[0], ((0, 5), (0, 0))), jnp.pad(v_w_conv[0], ((0, 5), (0, 0))), name="adamw_w_conv")
    res["w_conv"] = [r[:3] for r in conv_res]
    res = {k: [r.reshape(w[k].shape) for r in res[k]] for k in names}

    out = [entry(gr_tiny, "loss").reshape(()), grad_x[None]]
    for i in range(4):
        out += [res[k][i] for k in names]
    return tuple(out)
```

```python
import functools

import jax
import jax.numpy as jnp
from jax import lax
from jax.experimental import pallas as pl
from jax.experimental.pallas import tpu as pltpu

F32 = jnp.float32
BF16 = jnp.bfloat16
HI = lax.Precision.HIGHEST
MESH = pl.DeviceIdType.MESH

D = 1024
H = 8
DH = 128
CH = 64
LOG_CH = 6
PAIR = 2 * CH
GM = 128
assert 1 << LOG_CH == CH and PAIR == DH
PREC_POWERS = ((lax.Precision.HIGH, lax.Precision.HIGH),) * 3 + ((None, None),) * 2
assert len(PREC_POWERS) == LOG_CH - 1
HEADS_PER_ITER_FWD = 8
HEADS_PER_ITER_BWD = 8
EPS = 1e-6
N_CHIPS = 4
OFF_A = 3 * D
OFF_ZB = OFF_A + 4 * H
IN_COLS = OFF_ZB + 6 * D
VMEM_LIMIT_V7X = 56 * 1024 * 1024
DMA_CHUNK_BYTES = 2 * 1024 * 1024

ADAM_LR, ADAM_B1, ADAM_B2, ADAM_EPS, ADAM_WD, ADAM_STEP = 0.001, 0.9, 0.999, 1e-08, 0.01, 10

SMALL_LAYOUT = {}
_off = 0
for _n, _s in (("w_sp", H * GM * GM), ("w_conv", 3 * 3 * D), ("c_ctx", D), ("b_mod", 3 * D), ("g_pre", D), ("g_post", D),
               ("gm_ln_g", D), ("gm_ln_b", D), ("b_sp", H * GM), ("dm_ctx", 3 * D), ("mod_factors", 8 * 4 * D),
               ("g_onorm", DH), ("a_log", 2 * H), ("dt_bias", 2 * H), ("loss", 1), ("_pad", 128 - 4 * H - 1)):
    SMALL_LAYOUT[_n] = (_off, _s)
    _off += _s
SMALL_ROWS = 368
assert N_CHIPS * SMALL_ROWS * 128 >= _off and (SMALL_ROWS // 2) % 8 == 0 and _off % 128 == 0
TINY_ROW0 = SMALL_LAYOUT["w_conv"][0] // 128


def _params(sem=None):
    return pltpu.CompilerParams(dimension_semantics=sem, vmem_limit_bytes=VMEM_LIMIT_V7X)


def _tile(n, cands=(256, 128, 64, 32, 16, 8)):
    if n <= cands[0]:
        return n
    for cand in cands:
        if n % cand == 0:
            return cand
    return n


def _silu(x):
    return x * jax.nn.sigmoid(x)


def _gelu(x):
    return 0.5 * x * (1.0 + jnp.tanh(0.7978845608028654 * (x + 0.044715 * (x * x * x))))


def _mm(a, b, *, ta=False, tb=False, out_dtype=F32, tm=1024, tn=1024, tk=1024, add=None, add_from=0,
        b_sections=False, out_sections=False, name):
    m, k = (a.shape[1], a.shape[0]) if ta else a.shape
    if b_sections:
        sect = b.shape[2]
        n = b.shape[1] if tb else b.shape[0] * sect
        tn, tk = (tn, sect) if tb else (sect, tk)
    else:
        n = b.shape[0] if tb else b.shape[1]
    tm, tn, tk = min(tm, m), min(tn, n), min(tk, k)
    assert m % tm == 0 and n % tn == 0 and k % tk == 0, (name, m, n, k, tm, tn, tk)
    nk = k // tk
    dims = (((0,) if ta else (1,), (1,) if tb else (0,)), ((), ()))
    has_add = add is not None
    assert not has_add or add.shape == (m, n - add_from * tn), (name, add.shape)

    def body(*refs):
        a_ref, b_ref = refs[:2]
        o_ref = refs[2 + int(has_add)]
        acc_ref = refs[-1]
        kk = pl.program_id(2)
        bv = b_ref[0] if b_sections else b_ref[...]
        part = lax.dot_general(a_ref[...].astype(BF16), bv.astype(BF16), dims, preferred_element_type=F32)

        def finish(res):
            if has_add:
                res = res + jnp.where(pl.program_id(1) >= add_from, refs[2][...], 0.0)
            if out_sections:
                o_ref[0] = res.astype(out_dtype)
            else:
                o_ref[...] = res.astype(out_dtype)

        if nk == 1:
            finish(part)
            return

        @pl.when(kk == 0)
        def _():
            acc_ref[...] = part

        @pl.when((kk > 0) & (kk < nk - 1))
        def _():
            acc_ref[...] += part

        @pl.when(kk == nk - 1)
        def _():
            finish(acc_ref[...] + part)

    a_spec = pl.BlockSpec((tk, tm), lambda i, j, q: (q, i)) if ta else pl.BlockSpec((tm, tk), lambda i, j, q: (i, q))
    if b_sections:
        b_spec = (pl.BlockSpec((1, tn, tk), lambda i, j, q: (q, j, 0)) if tb
                  else pl.BlockSpec((1, tk, tn), lambda i, j, q: (j, q, 0)))
    else:
        b_spec = pl.BlockSpec((tn, tk), lambda i, j, q: (j, q)) if tb else pl.BlockSpec((tk, tn), lambda i, j, q: (q, j))
    add_spec = [pl.BlockSpec((tm, tn), lambda i, j, q: (i, jnp.maximum(j - add_from, 0)))] if has_add else []
    if out_sections:
        out_spec, out_shape = pl.BlockSpec((1, tm, tn), lambda i, j, q: (j, i, 0)), (n // tn, m, tn)
    else:
        out_spec, out_shape = pl.BlockSpec((tm, tn), lambda i, j, q: (i, j)), (m, n)
    return pl.pallas_call(
        body, name=name, grid=(m // tm, n // tn, nk),
        in_specs=[a_spec, b_spec] + add_spec, out_specs=out_spec,
        out_shape=jax.ShapeDtypeStruct(out_shape, out_dtype),
        scratch_shapes=[pltpu.VMEM((tm, tn), F32)] if nk > 1 else [],
        compiler_params=_params(("parallel", "parallel", "arbitrary")),
    )(*([a, b] + ([add] if has_add else [])))


def _h_fn(x, g, m):
    shift, scale = m[:, 0:D], m[:, D:2 * D]
    r = lax.rsqrt(jnp.mean(x * x, axis=-1, keepdims=True) + EPS)
    return (x * r * g) * (1.0 + scale) + shift


def _norm_fwd(x, g, mod, bmod, *, name):
    rows = x.shape[0]
    t = min(512, rows)

    def body(x_ref, g_ref, m_ref, b_ref, h_ref):
        h_ref[...] = _h_fn(x_ref[...], g_ref[...], m_ref[...] + b_ref[...]).astype(BF16)

    vec = lambda w: pl.BlockSpec((1, w), lambda i: (0, 0))
    return pl.pallas_call(
        body, name=name, grid=(rows // t,),
        in_specs=[pl.BlockSpec((t, D), lambda i: (i, 0)), vec(D), vec(3 * D), vec(3 * D)],
        out_specs=pl.BlockSpec((t, D), lambda i: (i, 0)),
        out_shape=jax.ShapeDtypeStruct((rows, D), BF16),
        compiler_params=_params(("parallel",)),
    )(x, g, mod, bmod)


def _norm_bwd(x, g, mod, bmod, dh_parts, resid, *, name):
    rows = x.shape[0]
    t = min(512, rows)
    n_parts = len(dh_parts)
    has_resid = resid is not None

    def body(*refs):
        x_ref, g_ref, m_ref, b_ref = refs[:4]
        parts = refs[4:4 + n_parts]
        r_ref = refs[4 + n_parts] if has_resid else None
        dx_ref, dg_ref, dm_ref = refs[-3:]
        i = pl.program_id(0)
        dh = parts[0][...]
        for p in parts[1:]:
            dh = dh + p[...]
        _, vjp = jax.vjp(_h_fn, x_ref[...], g_ref[...], m_ref[...] + b_ref[...])
        dx, dg, dm = vjp(dh)
        if has_resid:
            dx = dx + r_ref[...]
        dx_ref[...] = dx

        @pl.when(i == 0)
        def _():
            dg_ref[...] = dg
            dm_ref[...] = dm

        @pl.when(i > 0)
        def _():
            dg_ref[...] += dg
            dm_ref[...] += dm

    vec = lambda w: pl.BlockSpec((1, w), lambda i: (0, 0))
    tile = pl.BlockSpec((t, D), lambda i: (i, 0))
    ins = [x, g, mod, bmod, *dh_parts] + ([resid] if has_resid else [])
    return pl.pallas_call(
        body, name=name, grid=(rows // t,),
        in_specs=[tile, vec(D), vec(3 * D), vec(3 * D)] + [tile] * (n_parts + int(has_resid)),
        out_specs=[tile, vec(D), vec(3 * D)],
        out_shape=[jax.ShapeDtypeStruct((rows, D), F32), jax.ShapeDtypeStruct((1, D), F32),
                   jax.ShapeDtypeStruct((1, 3 * D), F32)],
        compiler_params=_params(("arbitrary",)),
    )(*ins)


def _conv_tile(u_ref, r0, t, rows, w0, w1, w2):
    u = u_ref[pl.ds(r0, t), :]
    prev8 = u_ref[pl.ds(pl.multiple_of(jnp.maximum(r0 - 8, 0), 8), 8), :]
    next8 = u_ref[pl.ds(pl.multiple_of(jnp.minimum(r0 + t, rows - 8), 8), 8), :]
    r8 = lax.broadcasted_iota(jnp.int32, (8, DH), 0)
    prev_row = jnp.sum(jnp.where(r8 == 7, prev8, 0.0), axis=0, keepdims=True)
    next_row = jnp.sum(jnp.where(r8 == 0, next8, 0.0), axis=0, keepdims=True)
    prev_row = jnp.where(r0 > 0, prev_row, 0.0)
    next_row = jnp.where(r0 + t < rows, next_row, 0.0)
    ri = lax.broadcasted_iota(jnp.int32, (t, DH), 0)
    um1 = jnp.where(ri == 0, prev_row, pltpu.roll(u, 1, 0))
    up1 = jnp.where(ri == t - 1, next_row, pltpu.roll(u, t - 1, 0))
    return w0 * um1 + w1 * u + w2 * up1, um1, u, up1


def _rowlocal(z, is_norm):
    y = _silu(z)
    yn = y * lax.rsqrt(jnp.sum(y * y, axis=-1, keepdims=True) + EPS)
    return jnp.where(is_norm, yn, y)


def _prep_fwd(p, wconv, n_norm, *, name):
    rows, nb = p.shape[0], p.shape[1] // DH
    t = min(512, rows)

    def body(u_ref, w_ref, o_ref):
        is_norm = pl.program_id(0) < n_norm
        w0, w1, w2 = w_ref[0:1, :], w_ref[1:2, :], w_ref[2:3, :]

        def step(s, carry):
            r0 = pl.multiple_of(s * t, t)
            z, _, _, _ = _conv_tile(u_ref, r0, t, rows, w0, w1, w2)
            o_ref[0, pl.ds(r0, t), :] = _rowlocal(z, is_norm)
            return carry

        lax.fori_loop(0, rows // t, step, 0)

    return pl.pallas_call(
        body, name=name, grid=(nb,),
        in_specs=[pl.BlockSpec((rows, DH), lambda j: (0, j)), pl.BlockSpec((3, DH), lambda j: (0, j))],
        out_specs=pl.BlockSpec((1, rows, DH), lambda j: (j, 0, 0)),
        out_shape=jax.ShapeDtypeStruct((nb, rows, DH), F32),
        compiler_params=_params(("parallel",)),
    )(p, wconv)


def _prep_bwd(p, wconv, d_a, d_b, n_norm, *, name):
    rows, nb = p.shape[0], p.shape[1] // DH
    t = min(512, rows)

    def body(u_ref, w_ref, da_ref, db_ref, du_ref, dw_ref, dz_ref):
        is_norm = pl.program_id(0) < n_norm
        w0, w1, w2 = w_ref[0:1, :], w_ref[1:2, :], w_ref[2:3, :]

        def step1(s, carry):
            a0, a1, a2 = carry
            r0 = pl.multiple_of(s * t, t)
            z, um1, u, up1 = _conv_tile(u_ref, r0, t, rows, w0, w1, w2)
            _, vjp = jax.vjp(lambda zz: _rowlocal(zz, is_norm), z)
            (dz,) = vjp(da_ref[0, pl.ds(r0, t), :] + db_ref[0, pl.ds(r0, t), :])
            dz_ref[pl.ds(r0, t), :] = dz
            red = lambda v: jnp.sum(v, axis=0, keepdims=True)
            return a0 + red(dz * um1), a1 + red(dz * u), a2 + red(dz * up1)

        zero = jnp.zeros((1, DH), F32)
        a0, a1, a2 = lax.fori_loop(0, rows // t, step1, (zero, zero, zero))
        dw_ref[0:1, :] = a0
        dw_ref[1:2, :] = a1
        dw_ref[2:3, :] = a2

        def step2(s, carry):
            r0 = pl.multiple_of(s * t, t)
            du, _, _, _ = _conv_tile(dz_ref, r0, t, rows, w2, w1, w0)
            du_ref[pl.ds(r0, t), :] = du.astype(BF16)
            return carry

        lax.fori_loop(0, rows // t, step2, 0)

    col = pl.BlockSpec((rows, DH), lambda j: (0, j))
    w_spec = pl.BlockSpec((3, DH), lambda j: (0, j))
    d_spec = pl.BlockSpec((1, rows, DH), lambda j: (j, 0, 0))
    return pl.pallas_call(
        body, name=name, grid=(nb,),
        in_specs=[col, w_spec, d_spec, d_spec], out_specs=[col, w_spec],
        out_shape=[jax.ShapeDtypeStruct((rows, nb * DH), BF16), jax.ShapeDtypeStruct((3, nb * DH), F32)],
        scratch_shapes=[pltpu.VMEM((rows, DH), F32)],
        compiler_params=_params(("parallel",)),
    )(p, wconv, d_a, d_b)


def _gates_fn(pab, avec, dvec):
    t = pab.shape[0]
    lane = lax.broadcasted_iota(jnp.int32, pab.shape, 1)
    xg = pab + dvec
    sp = jnp.maximum(xg, 0.0) + jnp.log(1.0 + jnp.exp(-jnp.abs(xg)))
    g = jnp.where(lane < 2 * H, -jnp.exp(avec) * sp, 0.0)
    ii = lax.broadcasted_iota(jnp.int32, (t, t), 0)
    jj = lax.broadcasted_iota(jnp.int32, (t, t), 1)
    same = (ii >> LOG_CH) == (jj >> LOG_CH)
    cum_f = _dot(jnp.where(same & (jj <= ii), 1.0, 0.0), g, precision=HI)
    cum_r = _dot(jnp.where(same & (jj >= ii), 1.0, 0.0), g, precision=HI)
    return jnp.where(lane < H, cum_f, jnp.where(lane < 2 * H, cum_r, jnp.where(lane < 4 * H, jax.nn.sigmoid(pab), 0.0)))


def _gates_fwd(pab, avec, dvec, *, name):
    rows = pab.shape[0]
    t = min(512, rows)

    def body(p_ref, a_ref, d_ref, o_ref):
        o_ref[...] = _gates_fn(p_ref[...], a_ref[...], d_ref[...])

    vec = pl.BlockSpec((1, DH), lambda i: (0, 0))
    tile = pl.BlockSpec((t, DH), lambda i: (i, 0))
    return pl.pallas_call(
        body, name=name, grid=(rows // t,), in_specs=[tile, vec, vec], out_specs=tile,
        out_shape=jax.ShapeDtypeStruct((rows, DH), F32), compiler_params=_params(("parallel",)),
    )(pab, avec, dvec)


def _gates_bwd(pab, avec, dvec, d_a, d_b, *, name):
    rows = pab.shape[0]
    t = min(512, rows)

    def body(p_ref, a_ref, d_ref, da_ref, db_ref, dp_ref, dav_ref, ddv_ref):
        i = pl.program_id(0)
        _, vjp = jax.vjp(_gates_fn, p_ref[...], a_ref[...], d_ref[...])
        dp, dav, ddv = vjp(da_ref[...] + db_ref[...])
        dp_ref[...] = dp.astype(BF16)

        @pl.when(i == 0)
        def _():
            dav_ref[...] = dav
            ddv_ref[...] = ddv

        @pl.when(i > 0)
        def _():
            dav_ref[...] += dav
            ddv_ref[...] += ddv

    vec = pl.BlockSpec((1, DH), lambda i: (0, 0))
    tile = pl.BlockSpec((t, DH), lambda i: (i, 0))
    return pl.pallas_call(
        body, name=name, grid=(rows // t,), in_specs=[tile, vec, vec, tile, tile], out_specs=[tile, vec, vec],
        out_shape=[jax.ShapeDtypeStruct((rows, DH), BF16), jax.ShapeDtypeStruct((1, DH), F32),
                   jax.ShapeDtypeStruct((1, DH), F32)],
        compiler_params=_params(("arbitrary",)),
    )(pab, avec, dvec, d_a, d_b)


def _dot_general(a, b, dims, precision=None):
    return lax.dot_general(a, b, (dims, ((), ())), precision=precision, preferred_element_type=F32)


@functools.partial(jax.custom_vjp, nondiff_argnums=(2,))
def _dot_bf16(a, b, dims):
    return _dot_general(a.astype(BF16), b.astype(BF16), dims)


def _dot_bf16_fwd(a, b, dims):
    return _dot_bf16(a, b, dims), (a, b)


def _dot_bf16_bwd(dims, res, g):
    a, b = res
    (ca,), (cb,) = dims
    da = _dot_bf16(g, b, ((1,), (1 - cb,))) if ca == 1 else _dot_bf16(b, g, ((1 - cb,), (1,)))
    db = _dot_bf16(a, g, ((1 - ca,), (0,))) if cb == 0 else _dot_bf16(g, a, ((0,), (1 - ca,)))
    return da, db


_dot_bf16.defvjp(_dot_bf16_fwd, _dot_bf16_bwd)


def _dot(a, b, dims=((1,), (0,)), precision=None):
    if precision is None and a.dtype == F32 and b.dtype == F32:
        return _dot_bf16(a, b, dims)
    return _dot_general(a, b, dims, precision)


_NT = ((1,), (1,))
_TN = ((0,), (0,))


@jax.custom_vjp
def _saved_inverse(neg_a, inv):
    return inv


def _saved_inverse_fwd(neg_a, inv):
    return inv, inv


def _saved_inverse_bwd(inv, d_inv):
    idx = range(len(inv))
    left = [_dot(inv[i], d_inv[i], _TN) for i in idx]
    d_neg_a = [_dot(left[i], inv[i], _NT) for i in idx]
    return d_neg_a, [jnp.zeros_like(t) for t in inv]


_saved_inverse.defvjp(_saved_inverse_fwd, _saved_inverse_bwd)


def _pairs(s, q, k, v, gcol, bcol, revs, inv_saved=None):
    idx = range(len(revs))
    ii = lax.broadcasted_iota(jnp.int32, (PAIR, PAIR), 0)
    jj = lax.broadcasted_iota(jnp.int32, (PAIR, PAIR), 1)
    same = (ii >> LOG_CH) == (jj >> LOG_CH)
    incl_d = (same & (ii >= jj), same & (ii <= jj))
    strict_d = (same & (ii > jj), same & (ii < jj))
    incl = [incl_d[int(r)] for r in revs]
    strict = [strict_d[int(r)] for r in revs]
    eye = jnp.where(ii == jj, 1.0, 0.0)
    gc_i = [jnp.broadcast_to(gcol[i], (PAIR, DH)) for i in idx]
    gc_j = [gc_i[i].T for i in idx]
    decay = [jnp.where(incl[i], jnp.exp(jnp.where(incl[i], gc_i[i] - gc_j[i], 0.0)), 0.0) for i in idx]
    b_b = [jnp.broadcast_to(bcol[i], (PAIR, DH)) for i in idx]
    kb = [k[i] * b_b[i] for i in idx]
    kk = [_dot(kb[i], k[i], _NT) for i in idx]
    bp = [jnp.where(strict[i], -kk[i] * decay[i], 0.0) for i in idx]
    if inv_saved is not None:
        inv = _saved_inverse(bp, inv_saved)
    else:
        low = bp
        for prec_sq, prec_acc in PREC_POWERS:
            bp = [_dot(bp[i], bp[i], precision=prec_sq) for i in idx]
            more = [_dot(low[i], bp[i], precision=prec_acc) for i in idx]
            low = [low[i] + bp[i] + more[i] for i in idx]
        inv = [eye + low[i] for i in idx]
    eg = [jnp.exp(gc_i[i]) for i in idx]
    sol = [_dot(inv[i], jnp.concatenate([v[i] * b_b[i], kb[i] * eg[i]], axis=1)) for i in idx]
    u_val = [sol[i][:, :DH] for i in idx]
    w_key = [sol[i][:, DH:] for i in idx]
    row = lax.broadcasted_iota(jnp.int32, (PAIR, 1), 0)
    has_q = q[0] is not None
    if has_q:
        qc = [q[i] * (DH ** -0.5) for i in idx]
        qk = [_dot(qc[i], k[i], _NT) for i in idx]
        attn = [qk[i] * decay[i] for i in idx]
        qd = [qc[i] * eg[i] for i in idx]
    outs = [[None, None] for _ in idx]
    zeros = jnp.zeros((CH, DH), F32)
    for step in range(2):
        cidx = [(1 - step) if revs[i] else step for i in idx]
        sl = [slice(c * CH, (c + 1) * CH) for c in cidx]
        last = [c * CH if revs[i] else c * CH + CH - 1 for i, c in zip(idx, cidx)]
        gl = [jnp.sum(jnp.where(row == last[i], gcol[i], 0.0), axis=0, keepdims=True) for i in idx]
        k_tail = [k[i][sl[i]] * jnp.exp(gl[i] - gc_i[i][sl[i]]) for i in idx]
        ws = [_dot(w_key[i][sl[i]], s[i]) for i in idx]
        v_new = [u_val[i][sl[i]] - ws[i] for i in idx]
        if has_q:
            v_pad = [jnp.concatenate([v_new[i], zeros] if cidx[i] == 0 else [zeros, v_new[i]], axis=0) for i in idx]
            o_state = [_dot(qd[i][sl[i]], s[i]) for i in idx]
            o_local = [_dot(attn[i][sl[i]], v_pad[i]) for i in idx]
            for i in idx:
                outs[i][cidx[i]] = o_state[i] + o_local[i]
        kv = [_dot(k_tail[i], v_new[i], _TN) for i in idx]
        s = [s[i] * jnp.exp(gl[i]) + kv[i] for i in idx]
    return s, ([jnp.concatenate(outs[i], axis=0) for i in idx] if has_q else None), inv


def _lane_col(tile, idx):
    lane = lax.broadcasted_iota(jnp.int32, tile.shape, 1)
    return jnp.sum(jnp.where(lane == idx, tile, 0.0), axis=1, keepdims=True)


def _gdn_fwd(qkv, gb, s0f, s0b, has_q, *, name):
    nb, rows, _ = qkv.shape
    n = rows // PAIR
    qoff = H if has_q else 0

    def body(qf_ref, qb_ref, gf_ref, gr_ref, s0f_ref, s0b_ref, of_ref, ob_ref, ssf_ref, ssb_ref, tsf_ref, tsb_ref,
             sf_ref, sb_ref):
        @pl.when(pl.program_id(0) == 0)
        def _():
            sf_ref[...] = s0f_ref[...]
            sb_ref[...] = s0b_ref[...]

        gtiles = (gf_ref[...], gr_ref[...])

        dirs = ((qf_ref, sf_ref, ssf_ref, of_ref), (qb_ref, sb_ref, ssb_ref, ob_ref))
        ts_refs = (tsf_ref, tsb_ref)

        def heads(hg, carry):
            work = [(hg * HEADS_PER_ITER_FWD + j, d) for j in range(HEADS_PER_ITER_FWD) for d in range(2)]
            loaded = []
            for h, d in work:
                q_ref, s_ref, _, _ = dirs[d]
                loaded.append((s_ref[h], q_ref[h] if has_q else None, q_ref[qoff + h], q_ref[qoff + H + h],
                               _lane_col(gtiles[d], d * H + h), _lane_col(gtiles[d], 2 * H + d * H + h)))
            s_new, o, inv = _pairs(*[list(col) for col in zip(*loaded)], revs=[d == 1 for _, d in work])
            for i, (h, d) in enumerate(work):
                _, s_ref, ss_ref, o_ref = dirs[d]
                ss_ref[0, h] = loaded[i][0]
                ts_refs[d][0, h] = inv[i]
                s_ref[h] = s_new[i]
                o_ref[h] = o[i] if has_q else jnp.zeros((PAIR, DH), F32)
            return carry

        if HEADS_PER_ITER_FWD == H:
            heads(0, 0)
        else:
            lax.fori_loop(0, H // HEADS_PER_ITER_FWD, heads, 0)

    fwd3 = lambda i: (0, i, 0)
    rev3 = lambda i: (0, n - 1 - i, 0)
    state = pl.BlockSpec((H, DH, DH), lambda i: (0, 0, 0))
    saved = pl.BlockSpec((1, H, DH, DH), lambda i: (i, 0, 0, 0))
    return pl.pallas_call(
        body, name=name, grid=(n,),
        in_specs=[pl.BlockSpec((nb, PAIR, DH), fwd3), pl.BlockSpec((nb, PAIR, DH), rev3),
                  pl.BlockSpec((PAIR, DH), lambda i: (i, 0)), pl.BlockSpec((PAIR, DH), lambda i: (n - 1 - i, 0)),
                  state, state],
        out_specs=[pl.BlockSpec((H, PAIR, DH), fwd3), pl.BlockSpec((H, PAIR, DH), rev3), saved, saved, saved, saved,
                   state, state],
        out_shape=[jax.ShapeDtypeStruct((H, rows, DH), F32)] * 2 + [jax.ShapeDtypeStruct((n, H, DH, DH), F32)] * 4
        + [jax.ShapeDtypeStruct((H, DH, DH), F32)] * 2,
        compiler_params=_params(("arbitrary",)),
    )(qkv, qkv, gb, gb, s0f, s0b)


def _gdn_bwd(qkv, gb, ssf, ssb, tsf, tsb, do, dsf, dsb, has_q, *, name):
    nb, rows, _ = qkv.shape
    n = rows // PAIR
    qoff = H if has_q else 0

    def body(qf_ref, qb_ref, gf_ref, gr_ref, ssf_ref, ssb_ref, tsf_ref, tsb_ref, dof_ref, dob_ref, dsf0_ref, dsb0_ref,
             dqf_ref, dqb_ref, dgf_ref, dgr_ref, dsf_ref, dsb_ref):
        ts_refs = (tsf_ref, tsb_ref)
        @pl.when(pl.program_id(0) == 0)
        def _():
            dsf_ref[...] = dsf0_ref[...]
            dsb_ref[...] = dsb0_ref[...]

        gtiles = (gf_ref[...], gr_ref[...])
        lane = lax.broadcasted_iota(jnp.int32, (PAIR, DH), 1)

        dirs = ((qf_ref, ssf_ref, dof_ref, dsf_ref, dqf_ref), (qb_ref, ssb_ref, dob_ref, dsb_ref, dqb_ref))

        def heads(hg, carry):
            out = list(carry)
            work = [(hg * HEADS_PER_ITER_BWD + j, d) for j in range(HEADS_PER_ITER_BWD) for d in range(2)]
            revs = [d == 1 for _, d in work]
            s_in, q_in, k_in, v_in, g_in, b_in, ds_out, do_out, inv_in = [], [], [], [], [], [], [], [], []
            for h, d in work:
                q_ref, ss_ref, do_ref, ds_ref, _ = dirs[d]
                s_in.append(ss_ref[0, h])
                inv_in.append(ts_refs[d][0, h])
                q_in.append(q_ref[h] if has_q else None)
                k_in.append(q_ref[qoff + h])
                v_in.append(q_ref[qoff + H + h])
                g_in.append(_lane_col(gtiles[d], d * H + h))
                b_in.append(_lane_col(gtiles[d], 2 * H + d * H + h))
                ds_out.append(ds_ref[h])
                do_out.append(do_ref[h] if has_q else None)
            if has_q:
                _, vjp = jax.vjp(lambda s_, q_, k_, v_, g_, b_: _pairs(s_, q_, k_, v_, g_, b_, revs, inv_in)[:2],
                                 s_in, q_in, k_in, v_in, g_in, b_in)
                ds, dq, dk, dv, dg, db = vjp((ds_out, do_out))
            else:
                _, vjp = jax.vjp(lambda s_, k_, v_, g_, b_: _pairs(s_, q_in, k_, v_, g_, b_, revs, inv_in)[0],
                                 s_in, k_in, v_in, g_in, b_in)
                ds, dk, dv, dg, db = vjp(ds_out)
            for i, (h, d) in enumerate(work):
                dq_ref, ds_ref = dirs[d][4], dirs[d][3]
                ds_ref[h] = ds[i]
                if has_q:
                    dq_ref[h] = dq[i]
                dq_ref[qoff + h], dq_ref[qoff + H + h] = dk[i], dv[i]
                out[d] = (out[d] + jnp.where(lane == d * H + h, dg[i], 0.0)
                          + jnp.where(lane == 2 * H + d * H + h, db[i], 0.0))
            return tuple(out)

        zero = jnp.zeros((PAIR, DH), F32)
        if HEADS_PER_ITER_BWD == H:
            dgf, dgr = heads(0, (zero, zero))
        else:
            dgf, dgr = lax.fori_loop(0, H // HEADS_PER_ITER_BWD, heads, (zero, zero))
        dgf_ref[...] = dgf
        dgr_ref[...] = dgr

    fwd3 = lambda i: (0, n - 1 - i, 0)
    rev3 = lambda i: (0, i, 0)
    state = pl.BlockSpec((H, DH, DH), lambda i: (0, 0, 0))
    saved = pl.BlockSpec((1, H, DH, DH), lambda i: (n - 1 - i, 0, 0, 0))
    gf_spec = pl.BlockSpec((PAIR, DH), lambda i: (n - 1 - i, 0))
    gr_spec = pl.BlockSpec((PAIR, DH), lambda i: (i, 0))
    return pl.pallas_call(
        body, name=name, grid=(n,),
        in_specs=[pl.BlockSpec((nb, PAIR, DH), fwd3), pl.BlockSpec((nb, PAIR, DH), rev3), gf_spec, gr_spec,
                  saved, saved, saved, saved,
                  pl.BlockSpec((H, PAIR, DH), fwd3), pl.BlockSpec((H, PAIR, DH), rev3), state, state],
        out_specs=[pl.BlockSpec((nb, PAIR, DH), fwd3), pl.BlockSpec((nb, PAIR, DH), rev3), gf_spec, gr_spec, state, state],
        out_shape=[jax.ShapeDtypeStruct((nb, rows, DH), F32)] * 2 + [jax.ShapeDtypeStruct((rows, DH), F32)] * 2
        + [jax.ShapeDtypeStruct((H, DH, DH), F32)] * 2,
        compiler_params=_params(("arbitrary",)),
    )(qkv, qkv, gb, gb, ssf, ssb, tsf, tsb, do, do, dsf, dsb)


def _stage1(zb, ua, va, za, o, gon, lng, lnb, wsp, bsp):
    gv = [_gelu(t) for t in va]
    mu = sum(jnp.sum(t, axis=-1, keepdims=True) for t in gv) * (1.0 / D)
    xc = [t - mu for t in gv]
    var = sum(jnp.sum(t * t, axis=-1, keepdims=True) for t in xc) * (1.0 / D)
    rs = lax.rsqrt(var + EPS)
    ya, yb = [], []
    for g in range(H):
        vv = xc[g] * rs * lng[g] + lnb[g]
        s = _dot(wsp[g], vv) + bsp[g]
        ya.append(_gelu(ua[g]) * s * _silu(za[g]))
        r = lax.rsqrt(jnp.mean(o[g] * o[g], axis=-1, keepdims=True) + EPS)
        yb.append(o[g] * r * gon * _silu(zb[g]))
    return ya, yb


def _stage2(ma, mb, ga, gb):
    return jax.nn.sigmoid(ga) * ma + jax.nn.sigmoid(gb) * mb


def _stage3(out, x, tgt, gpost, gate):
    r = out * lax.rsqrt(jnp.mean(out * out, axis=-1, keepdims=True) + EPS) * gpost
    err = x + gate * r - tgt
    return 0.5 * jnp.sum(jnp.mean(err * err, axis=-1, keepdims=True), axis=0, keepdims=True)


def _post(p_rest, o_f, o_b, x, tgt, mod, bmod, gon, lng, lnb, wsp, bspb, wpa, wpb, wout, gpost, *, name):
    rows = x.shape[0]
    n = rows // GM
    lanes = lambda g: slice(g * DH, (g + 1) * DH)
    bdot = lambda a, w_ref: _dot(a.astype(BF16), w_ref[...])
    bdot_t = lambda a, w_ref: _dot(a.astype(BF16), w_ref[...], _NT)

    def body(p_ref, of_ref, ob_ref, x_ref, t_ref, m_ref, bm_ref, gon_ref, lng_ref, lnb_ref, wsp_ref, bsp_ref,
             wpa_ref, wpb_ref, wout_ref, gp_ref,
             loss_ref, dp_ref, do_ref, dx_ref, ya_ref, yb_ref, mg_ref, dma_ref, dmb_ref, dout_ref,
             dvec_ref, dgon_ref, dwsp_ref, dbsp_ref):
        @pl.when(pl.program_id(0) == 0)
        def _():
            loss_ref[...] = jnp.zeros_like(loss_ref)
            dvec_ref[...] = jnp.zeros_like(dvec_ref)
            dgon_ref[...] = jnp.zeros_like(dgon_ref)
            dwsp_ref[...] = jnp.zeros_like(dwsp_ref)
            dbsp_ref[...] = jnp.zeros_like(dbsp_ref)

        piece = lambda blk: [p_ref[:, blk * D + g * DH: blk * D + (g + 1) * DH].astype(F32) for g in range(H)]
        zb, ua, va, za = piece(0), piece(1), piece(2), piece(3)
        o = [of_ref[g] + ob_ref[g] for g in range(H)]
        gon = gon_ref[...]
        lng = [lng_ref[:, lanes(g)] for g in range(H)]
        lnb = [lnb_ref[:, lanes(g)] for g in range(H)]
        wsp = [wsp_ref[g] for g in range(H)]
        bsp = [bsp_ref[g] for g in range(H)]
        (ya, yb), vjp1 = jax.vjp(_stage1, zb, ua, va, za, o, gon, lng, lnb, wsp, bsp)
        y_a, y_b = jnp.concatenate(ya, axis=1), jnp.concatenate(yb, axis=1)
        ma, mb = bdot(y_a, wpa_ref), bdot(y_b, wpb_ref)
        ga, gb = p_ref[:, 4 * D:5 * D].astype(F32), p_ref[:, 5 * D:6 * D].astype(F32)
        merged, vjp2 = jax.vjp(_stage2, ma, mb, ga, gb)
        out = bdot(merged, wout_ref)
        gate = m_ref[:, 2 * D:3 * D] + bm_ref[:, 2 * D:3 * D]
        loss, vjp3 = jax.vjp(_stage3, out, x_ref[...], t_ref[...], gp_ref[...], gate)
        loss_ref[...] += jnp.broadcast_to(loss, loss_ref.shape)

        dout, dx, _, dgpost, dgate = vjp3(jnp.ones((1, 1), F32))
        dx_ref[...] = dx
        dmerged = bdot_t(dout, wout_ref)
        dma, dmb, dga, dgb = vjp2(dmerged)
        dya, dyb = bdot_t(dma, wpa_ref), bdot_t(dmb, wpb_ref)
        dzb, dua, dva, dza, do, dgon, dlng, dlnb, dwsp, dbsp = vjp1(
            ([dya[:, lanes(g)] for g in range(H)], [dyb[:, lanes(g)] for g in range(H)]))

        for blk, dlist in enumerate((dzb, dua, dva, dza)):
            for g in range(H):
                dp_ref[:, blk * D + g * DH: blk * D + (g + 1) * DH] = dlist[g].astype(BF16)
        dp_ref[:, 4 * D:5 * D] = dga.astype(BF16)
        dp_ref[:, 5 * D:6 * D] = dgb.astype(BF16)
        for g in range(H):
            do_ref[g] = do[g]
            dwsp_ref[g] += dwsp[g]
            dbsp_ref[g] += dbsp[g]
            dvec_ref[2:3, lanes(g)] += dlng[g]
            dvec_ref[3:4, lanes(g)] += dlnb[g]
        dvec_ref[0:1, :] += dgpost
        dvec_ref[1:2, :] += dgate
        dgon_ref[0:1, :] += dgon
        ya_ref[...] = y_a.astype(BF16)
        yb_ref[...] = y_b.astype(BF16)
        mg_ref[...] = merged.astype(BF16)
        dma_ref[...] = dma.astype(BF16)
        dmb_ref[...] = dmb.astype(BF16)
        dout_ref[...] = dout.astype(BF16)

    row = lambda w: pl.BlockSpec((GM, w), lambda i: (i, 0))
    heads = pl.BlockSpec((H, GM, DH), lambda i: (0, i, 0))
    full = lambda shape: pl.BlockSpec(shape, lambda i: tuple(0 for _ in shape))
    sds = jax.ShapeDtypeStruct
    return pl.pallas_call(
        body, name=name, grid=(n,),
        in_specs=[row(6 * D), heads, heads, row(D), row(D), full((1, 3 * D)), full((1, 3 * D)), full((1, DH)),
                  full((1, D)), full((1, D)), full((H, GM, GM)), full((H, GM, GM)),
                  full((D, D)), full((D, D)), full((D, D)), full((1, D))],
        out_specs=[full((8, DH)), row(6 * D), heads, row(D)] + [row(D)] * 6
        + [full((8, D)), full((8, DH)), full((H, GM, GM)), full((H, GM, GM))],
        out_shape=[sds((8, DH), F32), sds((rows, 6 * D), BF16), sds((H, rows, DH), F32), sds((rows, D), F32)]
        + [sds((rows, D), BF16)] * 6 + [sds((8, D), F32), sds((8, DH), F32), sds((H, GM, GM), F32), sds((H, GM, GM), F32)],
        compiler_params=_params(("arbitrary",)),
    )(p_rest, o_f, o_b, x, tgt, mod, bmod, gon, lng, lnb, wsp, bspb, wpa, wpb, wout, gpost)


def _silu_rows(c, *, name):
    def body(c_ref, o_ref):
        o_ref[...] = _silu(c_ref[...])

    return pl.pallas_call(body, name=name, out_shape=jax.ShapeDtypeStruct(c.shape, F32))(c)


def _dsilu_mul(c, d, *, name):
    def body(c_ref, d_ref, o_ref):
        _, vjp = jax.vjp(_silu, c_ref[...])
        (o_ref[...],) = vjp(d_ref[...])

    return pl.pallas_call(body, name=name, out_shape=jax.ShapeDtypeStruct(c.shape, F32))(c, d)


def _adamw(w, g, m, v, *, name):
    rows, cols = w.shape[-2:]
    t = _tile(rows if g.ndim == 2 else g.shape[1])
    c1 = 1.0 / (1.0 - ADAM_B1 ** ADAM_STEP)
    c2 = 1.0 / (1.0 - ADAM_B2 ** ADAM_STEP)

    def body(w_ref, g_ref, m_ref, v_ref, go_ref, d_ref, mo_ref, vo_ref):
        blk = lambda r: r[...].reshape(t, cols)
        gv = blk(g_ref)
        mn = ADAM_B1 * blk(m_ref) + (1.0 - ADAM_B1) * gv
        vn = ADAM_B2 * blk(v_ref) + (1.0 - ADAM_B2) * (gv * gv)
        delta = -ADAM_LR * ((mn * c1) / (jnp.sqrt(vn * c2) + ADAM_EPS) + ADAM_WD * blk(w_ref))
        for ref, val in ((go_ref, gv), (d_ref, delta), (mo_ref, mn), (vo_ref, vn)):
            ref[...] = val.reshape(ref.shape)

    tile = (pl.BlockSpec((1, t, cols), lambda i: (0, i, 0)) if w.ndim == 3 else pl.BlockSpec((t, cols), lambda i: (i, 0)))
    if g.ndim == 3:
        per = g.shape[1] // t
        g_tile = pl.BlockSpec((1, t, cols), lambda i: (i // per, i % per, 0))
    else:
        g_tile = pl.BlockSpec((t, cols), lambda i: (i, 0))
    return pl.pallas_call(
        body, name=name, grid=(rows // t,),
        in_specs=[tile, g_tile, tile, tile], out_specs=[tile] * 4,
        out_shape=[jax.ShapeDtypeStruct(w.shape, F32)] * 4,
        compiler_params=_params(("parallel",)),
    )(w, g, m, v)


def _add_sibling(full, got, where, out_dtype, *, name):
    s, rows, cols = full.shape
    hr = rows // 2
    t = _tile(hr)
    nt = hr // t

    def body(w_ref, a_ref, b_ref, o_ref):
        o_ref[...] = (a_ref[...].astype(F32) + b_ref[...].astype(F32)).astype(out_dtype)

    tile = pl.BlockSpec((1, t, cols), lambda j, i, w: (j, i, 0))
    return pl.pallas_call(
        body, name=name,
        grid_spec=pltpu.PrefetchScalarGridSpec(
            num_scalar_prefetch=1, grid=(s, nt),
            in_specs=[pl.BlockSpec((1, t, cols), lambda j, i, w: (j, w[0] * nt + i, 0)), tile], out_specs=tile),
        out_shape=jax.ShapeDtypeStruct((s, hr, cols), out_dtype), compiler_params=_params(("parallel", "parallel")),
    )(where, full, got)


def _sum_chips(own, slots, where, n_out, which, *, name):
    _, hr, cols = own.shape
    t = _tile(hr)

    def body(w_ref, a_ref, s_ref, o_ref):
        f = lambda v: v.astype(F32)
        o_ref[0] = ((f(a_ref[0]) + f(s_ref[0])) + f(s_ref[1])) + f(s_ref[2])

    return pl.pallas_call(
        body, name=name,
        grid_spec=pltpu.PrefetchScalarGridSpec(
            num_scalar_prefetch=1, grid=(hr // t,),
            in_specs=[pl.BlockSpec((1, t, cols), lambda i, w: (w[1], i, 0)),
                      pl.BlockSpec((N_CHIPS - 1, t, cols), lambda i, w: (0, i, 0))],
            out_specs=pl.BlockSpec((1, t, cols), lambda i, w: (w[which], i, 0))),
        out_shape=jax.ShapeDtypeStruct((n_out, hr, cols), F32), compiler_params=_params(("parallel",)),
    )(where, own, slots)


def _local_step(x, c, ctx, c_ctx, tgt, w_mod, b_mod, g_pre, g_post, w_qkv, w_ab, w_rest, w_conv, a_log, dt_bias,
                g_onorm, gm_ln_g, gm_ln_b, w_sp, b_sp, w_pa, w_pb, w_out, on_large_gradients=None):
    rows, rows_c = x.shape[0], ctx.shape[0]
    cc = jnp.zeros((16, D), F32).at[0].set(c[0]).at[1].set(c_ctx)
    scc = _silu_rows(cc, name="silu_cond")
    mod = _mm(scc, w_mod, b_sections=True, name="mod_fwd")
    mod_x, mod_c = mod[0:1], mod[1:2]
    avec = jnp.zeros((1, DH), F32).at[0, :2 * H].set(a_log.reshape(-1))
    dvec = jnp.zeros((1, DH), F32).at[0, :2 * H].set(dt_bias.reshape(-1))
    bspb = jnp.broadcast_to(b_sp[:, :, None], (H, GM, GM))
    w_kv, wconv_kv = w_qkv[:, D:], w_conv[:, D:]

    h_c = _norm_fwd(ctx, g_pre, mod_c, b_mod, name="norm_fwd_ctx")
    pc_kv = _mm(h_c, w_kv, name="inproj_ctx_kv")
    pc_ab = _mm(h_c, w_ab, name="inproj_ctx_ab")
    kv_c = _prep_fwd(pc_kv, wconv_kv, H, name="prep_fwd_ctx")
    gb_c = _gates_fwd(pc_ab, avec, dvec, name="gates_fwd_ctx")
    s_zero = jnp.zeros((H, DH, DH), F32)
    _, _, ssf_c, ssb_c, tsf_c, tsb_c, s_f, s_b = _gdn_fwd(kv_c, gb_c, s_zero, s_zero, False, name="gdn_fwd_ctx")

    h_x = _norm_fwd(x, g_pre, mod_x, b_mod, name="norm_fwd_x")
    p_qkv = _mm(h_x, w_qkv, name="inproj_qkv")
    p_ab = _mm(h_x, w_ab, name="inproj_ab")
    p_rest = _mm(h_x, w_rest, out_dtype=BF16, name="inproj_rest")
    qkv = _prep_fwd(p_qkv, w_conv, 2 * H, name="prep_fwd_x")
    gb_x = _gates_fwd(p_ab, avec, dvec, name="gates_fwd_x")
    o_f, o_b, ssf, ssb, tsf, tsb, _, _ = _gdn_fwd(qkv, gb_x, s_f, s_b, True, name="gdn_fwd_x")

    (loss_acc, dp_rest, do, dx_res, ya, yb, mg, dma, dmb, dout, dvec_post, dgon, dwsp, dbspb) = _post(
        p_rest, o_f, o_b, x, tgt, mod_x, b_mod, g_onorm, gm_ln_g, gm_ln_b, w_sp, bspb, w_pa, w_pb, w_out, g_post,
        name="post")
    g = {}
    g["w_pa"] = _mm(ya, dma, ta=True, name="dw_pa")
    g["w_pb"] = _mm(yb, dmb, ta=True, name="dw_pb")
    g["w_out"] = _mm(mg, dout, ta=True, name="dw_out")

    zeros_s = jnp.zeros((H, DH, DH), F32)
    dq_f, dq_b, dg_f, dg_b, ds0_f, ds0_b = _gdn_bwd(qkv, gb_x, ssf, ssb, tsf, tsb, do, zeros_s, zeros_s, True,
                                                    name="gdn_bwd_x")
    dp_qkv, dwc_x = _prep_bwd(p_qkv, w_conv, dq_f, dq_b, 2 * H, name="prep_bwd_x")
    dp_ab, dav_x, ddv_x = _gates_bwd(p_ab, avec, dvec, dg_f, dg_b, name="gates_bwd_x")
    dkv_f, dkv_b, dgc_f, dgc_b, _, _ = _gdn_bwd(kv_c, gb_c, ssf_c, ssb_c, tsf_c, tsb_c, jnp.zeros((H, rows_c, DH), F32),
                                                 ds0_f, ds0_b, False, name="gdn_bwd_ctx")
    dpc_kv, dwc_c = _prep_bwd(pc_kv, wconv_kv, dkv_f, dkv_b, H, name="prep_bwd_ctx")
    dpc_ab, dav_c, ddv_c = _gates_bwd(pc_ab, avec, dvec, dgc_f, dgc_b, name="gates_bwd_ctx")

    dw_kv_c = _mm(h_c, dpc_kv, ta=True, name="dw_kv_ctx")
    dw_qkv = _mm(h_x, dp_qkv, ta=True, tn=D, add=dw_kv_c, add_from=1, name="dw_qkv")
    dw_ab_c = _mm(h_c, dpc_ab, ta=True, name="dw_ab_ctx")
    dw_ab = _mm(h_x, dp_ab, ta=True, add=dw_ab_c, name="dw_ab")
    dw_rest = _mm(h_x, dp_rest, ta=True, name="dw_rest")
    g["w_in"] = (dw_qkv, dw_ab[:, :4 * H], dw_rest)
    if on_large_gradients is not None:
        token = on_large_gradients(g)
        _, dp_qkv, dp_rest, dp_ab, dpc_kv, dpc_ab = lax.optimization_barrier(
            (token, dp_qkv, dp_rest, dp_ab, dpc_kv, dpc_ab))
    dh_parts = [_mm(dp_qkv, w_qkv, tb=True, tk=3 * D // 2, name="dh_qkv"),
                _mm(dp_rest, w_rest, tb=True, tk=3 * D // 2, name="dh_rest"), _mm(dp_ab, w_ab, tb=True, name="dh_ab")]
    grad_x, dgpre_x, dm_x = _norm_bwd(x, g_pre, mod_x, b_mod, dh_parts, dx_res, name="norm_bwd_x")
    dhc_parts = [_mm(dpc_kv, w_kv, tb=True, name="dhc_kv"), _mm(dpc_ab, w_ab, tb=True, name="dhc_ab")]
    _, dgpre_c, dm_c = _norm_bwd(ctx, g_pre, mod_c, b_mod, dhc_parts, None, name="norm_bwd_ctx")

    dm_x = dm_x.at[:, 2 * D:].add(dvec_post[1:2])
    dmod = jnp.zeros((16, 3 * D), F32).at[0].set(dm_x[0]).at[1].set(dm_c[0])
    g["mod_factors"] = (scc[0], dm_x[0])
    g["dm_ctx"] = dm_c[0]
    g["silu_c_ctx"] = scc[1:2]
    dcc = _mm(dmod, w_mod, tb=True, b_sections=True, name="dcc")
    g["c_ctx"] = _dsilu_mul(cc[:8], dcc[:8], name="dc_ctx")[1]
    g["b_mod"] = dm_x + dm_c
    g["g_pre"] = dgpre_x + dgpre_c
    g["g_post"] = dvec_post[0:1]
    g["gm_ln_g"], g["gm_ln_b"] = dvec_post[2:3], dvec_post[3:4]
    g["g_onorm"] = dgon[0:1]
    g["w_sp"] = dwsp
    g["b_sp"] = jnp.sum(dbspb, axis=-1)
    g["w_conv"] = dwc_x.at[:, D:].add(dwc_c)
    g["a_log"] = (dav_x + dav_c)[0, :2 * H].reshape(2, H)
    g["dt_bias"] = (ddv_x + ddv_c)[0, :2 * H].reshape(2, H)
    return loss_acc[0, 0], grad_x, g


ANY = pl.BlockSpec(memory_space=pl.ANY)


def _place():
    x, y, c = lax.axis_index("x"), lax.axis_index("y"), lax.axis_index("c")
    chips = [(1 - x, y), (x, 1 - y), (1 - x, 1 - y)]
    return x, y, c, (x, y, 1 - c), chips


def _gather_shards(big, small, *, name):
    nb = len(big)

    def body(*refs):
        ins, sm_in = refs[:nb], refs[nb]
        outs, sm_out = refs[nb + 1:2 * nb + 1], refs[2 * nb + 1]
        send, recv = refs[2 * nb + 2:]
        x, y, c, sibling, chips = _place()
        mine = 2 * x + y

        def half(a, shard, hc):
            hr = big[a].shape[0] // 2
            return outs[a].at[shard, pl.ds(hc * hr, hr), :]

        def remote(k, src, dst, to):
            return pltpu.make_async_remote_copy(src_ref=src, dst_ref=dst, send_sem=send.at[k], recv_sem=recv.at[k],
                                                device_id=to, device_id_type=MESH)

        sends = []
        for a in range(nb):
            hr = big[a].shape[0] // 2
            for j, chip in enumerate(chips):
                sends.append(remote(a * 3 + j, ins[a].at[pl.ds(c * hr, hr), :], half(a, mine, c), (*chip, c)))
        for j, chip in enumerate(chips):
            sends.append(remote(nb * 3 + j, sm_in, sm_out.at[mine], (*chip, c)))
        for cp in sends:
            cp.start()
        base = nb * 3 + 3
        passed = []
        for a in range(nb):
            for j, (px, py) in enumerate(chips):
                theirs = 2 * px + py
                remote(a * 3 + j, half(a, theirs, c), half(a, theirs, c), sibling).wait_recv()
                fw = remote(base + a * 3 + j, half(a, theirs, c), half(a, theirs, c), sibling)
                fw.start()
                passed.append(fw)
        for a in range(nb):
            for j, (px, py) in enumerate(chips):
                theirs = 2 * px + py
                remote(base + a * 3 + j, half(a, theirs, 1 - c), half(a, theirs, 1 - c), sibling).wait_recv()
        for j, (px, py) in enumerate(chips):
            remote(nb * 3 + j, sm_in, sm_out.at[2 * px + py], sibling).wait_recv()
        for cp in sends + passed:
            cp.wait_send()

    n_remote = 2 * nb * 3 + 3
    outs = pl.pallas_call(
        body, name=name, in_specs=[ANY] * (nb + 1), out_specs=[ANY] * (nb + 1),
        out_shape=[jax.ShapeDtypeStruct((N_CHIPS,) + a.shape, a.dtype) for a in big + [small]],
        scratch_shapes=[pltpu.SemaphoreType.DMA((n_remote,)), pltpu.SemaphoreType.DMA((n_remote,))],
    )(*big, small)
    return outs[:nb], outs[nb]


def _row_chunks(rows, row_bytes, align=8):
    n = max(1, min(rows // align, -(-rows * row_bytes // DMA_CHUNK_BYTES)))
    per = -(-(-(-rows // n)) // align) * align
    return [(r, min(per, rows - r)) for r in range(0, rows, per)]


def _remote(src, dst, send, recv, to):
    return pltpu.make_async_remote_copy(src_ref=src, dst_ref=dst, send_sem=send, recv_sem=recv, device_id=to,
                                        device_id_type=MESH)


def _exchange_halves(arrs, *, name):
    na = len(arrs)

    def body(*refs):
        ins, got = refs[:na], refs[na:2 * na]
        send, recv = refs[2 * na:]
        x, y, c, sibling, _ = _place()
        for a in range(na):
            ns, rows, cols = arrs[a].shape
            hr = rows // 2
            for s in range(ns):
                for r0, nr in _row_chunks(hr, cols * arrs[a].dtype.itemsize, 16):
                    _remote(ins[a].at[s, pl.ds((1 - c) * hr + r0, nr), :], got[a].at[s, pl.ds(r0, nr), :],
                            send.at[a], recv.at[a], sibling).start()
        for a in range(na):
            hr = arrs[a].shape[1] // 2
            _remote(ins[a].at[:, pl.ds((1 - c) * hr, hr), :], got[a], send.at[a], recv.at[a], sibling).wait()

    return pl.pallas_call(
        body, name=name, in_specs=[ANY] * na, out_specs=[ANY] * na,
        out_shape=[jax.ShapeDtypeStruct((N_CHIPS, a.shape[1] // 2, a.shape[2]), a.dtype) for a in arrs],
        scratch_shapes=[pltpu.SemaphoreType.DMA((na,)), pltpu.SemaphoreType.DMA((na,))],
    )(*arrs)


def _scatter_sections(arrs, *, name):
    na = len(arrs)

    def body(*refs):
        ins, outs = refs[:na], refs[na:2 * na]
        send, recv = refs[2 * na:]
        x, y, c, _, chips = _place()
        for a in range(na):
            _, hr, cols = arrs[a].shape
            for j, (px, py) in enumerate(chips):
                for r0, nr in _row_chunks(hr, cols * arrs[a].dtype.itemsize, 16):
                    _remote(ins[a].at[2 * px + py, pl.ds(r0, nr), :], outs[a].at[j, pl.ds(r0, nr), :],
                            send.at[a * 3 + j], recv.at[a * 3 + j], (px, py, c)).start()
        for a in range(na):
            for j, (px, py) in enumerate(chips):
                _remote(ins[a].at[2 * px + py], outs[a].at[j], send.at[a * 3 + j], recv.at[a * 3 + j], (px, py, c)).wait()

    return pl.pallas_call(
        body, name=name, in_specs=[ANY] * na, out_specs=[ANY] * na,
        out_shape=[jax.ShapeDtypeStruct((N_CHIPS - 1,) + a.shape[1:], a.dtype) for a in arrs],
        scratch_shapes=[pltpu.SemaphoreType.DMA((3 * na,)), pltpu.SemaphoreType.DMA((3 * na,))],
    )(*arrs)


HBM = pl.BlockSpec(memory_space=pltpu.HBM)
SEM = pl.BlockSpec(memory_space=pltpu.SEMAPHORE)


def _scatter_start(arrs, *, name):
    na = len(arrs)

    def body(*refs):
        ins, land = refs[:na], refs[na:2 * na]
        send, recv = refs[2 * na], refs[2 * na + 1]
        token = refs[-1]
        x, y, c, _, chips = _place()
        for a in range(na):
            _, hr, cols = arrs[a].shape
            for j, (px, py) in enumerate(chips):
                for r0, nr in _row_chunks(hr, cols * arrs[a].dtype.itemsize, 16):
                    _remote(ins[a].at[2 * px + py, pl.ds(r0, nr), :], land[a].at[j, pl.ds(r0, nr), :],
                            send.at[a * 3 + j], recv.at[a * 3 + j], (px, py, c)).start()
        token[...] = jnp.zeros_like(token)

    slots = [((N_CHIPS - 1,) + a.shape[1:], a.dtype) for a in arrs]
    held = lambda a: pltpu.with_memory_space_constraint(a, pltpu.HBM)
    outs = pl.pallas_call(
        body, name=name, in_specs=[HBM] * (2 * na),
        out_specs=[SEM, SEM] + [HBM] * (2 * na) + [pl.BlockSpec(memory_space=pltpu.VMEM)],
        out_shape=[pltpu.SemaphoreType.DMA((3 * na,)), pltpu.SemaphoreType.DMA((3 * na,))]
        + [pltpu.HBM(a.shape, a.dtype) for a in arrs] + [pltpu.HBM(s, d) for s, d in slots]
        + [jax.ShapeDtypeStruct((8, 128), F32)],
        input_output_aliases={i: 2 + i for i in range(2 * na)},
        compiler_params=pltpu.CompilerParams(has_side_effects=pltpu.SideEffectType.DATAFLOW_SIDE_EFFECTING),
    )(*[held(a) for a in arrs], *[held(lax.empty(s, d)) for s, d in slots])
    return outs[0], outs[1], outs[2:2 + na], outs[2 + na:2 + 2 * na], outs[-1]


def _scatter_wait(send, recv, thru, land, after, *, name):
    na = len(thru)

    def body(*refs):
        ins, slots = refs[:na], refs[na:2 * na]
        send_sem, recv_sem = refs[2 * na], refs[2 * na + 1]
        x, y, c, _, chips = _place()
        for a in range(na):
            for j, (px, py) in enumerate(chips):
                cp = _remote(ins[a].at[2 * px + py], slots[a].at[j], send_sem.at[a * 3 + j], recv_sem.at[a * 3 + j],
                             (px, py, c))
                cp.wait_send()
                cp.wait_recv()

    outs = pl.pallas_call(
        body, name=name, in_specs=[HBM] * (2 * na) + [SEM, SEM, ANY], out_specs=[HBM] * (2 * na),
        out_shape=[pltpu.HBM(a.shape, a.dtype) for a in list(thru) + list(land)],
        input_output_aliases={i: i for i in range(2 * na)},
        compiler_params=pltpu.CompilerParams(has_side_effects=pltpu.SideEffectType.DATAFLOW_SIDE_EFFECTING),
    )(*thru, *land, send, recv, after)
    return outs[:na], outs[na:]


def _finish_reduce(big, small, *, name):
    nb = len(big)

    def body(*refs):
        outs, sm = refs[nb + 1:2 * nb + 1], refs[2 * nb + 1]
        send, recv = refs[2 * nb + 2:]
        x, y, c, sibling, chips = _place()
        blk = lambda px, py, pc: sm.at[4 * px + 2 * py + pc]
        for a in range(nb):
            _, hr, cols = big[a].shape
            for r0, nr in _row_chunks(hr, cols * 4):
                _remote(outs[a].at[c, pl.ds(r0, nr), :], outs[a].at[c, pl.ds(r0, nr), :], send.at[a], recv.at[a],
                        sibling).start()
        first = [_remote(blk(x, y, c), blk(x, y, c), send.at[nb], recv.at[nb], sibling)]
        first += [_remote(blk(x, y, c), blk(x, y, c), send.at[nb + 1 + j], recv.at[nb + 1 + j], (*chip, c))
                  for j, chip in enumerate(chips)]
        for cp in first:
            cp.start()
        passed = []
        for j, (px, py) in enumerate(chips):
            _remote(blk(px, py, c), blk(px, py, c), send.at[nb + 1 + j], recv.at[nb + 1 + j], sibling).wait_recv()
            fw = _remote(blk(px, py, c), blk(px, py, c), send.at[nb + 4 + j], recv.at[nb + 4 + j], sibling)
            fw.start()
            passed.append(fw)
        for a in range(nb):
            _remote(outs[a].at[c], outs[a].at[1 - c], send.at[a], recv.at[a], sibling).wait()
        _remote(blk(x, y, c), blk(x, y, 1 - c), send.at[nb], recv.at[nb], sibling).wait_recv()
        for j, (px, py) in enumerate(chips):
            _remote(blk(px, py, c), blk(px, py, 1 - c), send.at[nb + 4 + j], recv.at[nb + 4 + j], sibling).wait_recv()
        for cp in first + passed:
            cp.wait_send()

    n_remote = nb + 7
    arrs = list(big) + [small]
    outs = pl.pallas_call(
        body, name=name, in_specs=[ANY] * (nb + 1), out_specs=[ANY] * (nb + 1),
        out_shape=[jax.ShapeDtypeStruct(a.shape, F32) for a in arrs],
        input_output_aliases={i: i for i in range(nb + 1)},
        scratch_shapes=[pltpu.SemaphoreType.DMA((n_remote,)), pltpu.SemaphoreType.DMA((n_remote,))],
    )(*arrs)
    return outs[:nb], outs[nb]


def _reduce_start(sectioned, where):
    got = _exchange_halves(sectioned, name="rs_exchange_halves")
    chip_sum = [_add_sibling(a, g, where, BF16, name=f"rs_add_sibling_{i}")
                for i, (a, g) in enumerate(zip(sectioned, got))]
    send, recv, thru, land, token = _scatter_start(chip_sum, name="rs_scatter_start")
    return (send, recv, thru, land), token


def _reduce_finish(handle, small, where, after):
    send, recv, thru, land = handle
    got = _exchange_halves([small], name="rs_exchange_small")
    small_sum = _add_sibling(small, got[0], where, F32, name="rs_add_sibling_small")
    small_slots = _scatter_sections([small_sum], name="rs_scatter_small")[0]
    own, slots = _scatter_wait(send, recv, thru, land, after, name="rs_scatter_wait")
    red = [_sum_chips(p, s, where, 2, 0, name=f"rs_sum_chips_{i}") for i, (p, s) in enumerate(zip(own, slots))]
    red_small = _sum_chips(small_sum, small_slots, where, 2 * N_CHIPS, 2, name="rs_sum_chips_small")
    big, sm = _finish_reduce(red, red_small, name="rs_finish")
    return big, sm.reshape(-1, sm.shape[-1])


def kernel(x, c, ctx, c_ctx, w_mod, b_mod, g_pre, g_post, w_in, w_conv, a_log, dt_bias, g_onorm, gm_ln_g, gm_ln_b, w_sp, b_sp, w_pa, w_pb, w_out, loss_target, m_c_ctx, m_w_mod, m_b_mod, m_g_pre, m_g_post, m_w_in, m_w_conv, m_a_log, m_dt_bias, m_g_onorm, m_gm_ln_g, m_gm_ln_b, m_w_sp, m_b_sp, m_w_pa, m_w_pb, m_w_out, v_c_ctx, v_w_mod, v_b_mod, v_g_pre, v_g_post, v_w_in, v_w_conv, v_a_log, v_dt_bias, v_g_onorm, v_gm_ln_g, v_gm_ln_b, v_w_sp, v_b_sp, v_w_pa, v_w_pb, v_w_out):
    names = ["c_ctx", "w_mod", "b_mod", "g_pre", "g_post", "w_in", "w_conv", "a_log", "dt_bias", "g_onorm", "gm_ln_g",
             "gm_ln_b", "w_sp", "b_sp", "w_pa", "w_pb", "w_out"]
    w = dict(zip(names, (c_ctx, w_mod, b_mod, g_pre, g_post, w_in, w_conv, a_log, dt_bias, g_onorm, gm_ln_g, gm_ln_b,
                         w_sp, b_sp, w_pa, w_pb, w_out)))
    m = dict(zip(names, (m_c_ctx, m_w_mod, m_b_mod, m_g_pre, m_g_post, m_w_in, m_w_conv, m_a_log, m_dt_bias, m_g_onorm,
                         m_gm_ln_g, m_gm_ln_b, m_w_sp, m_b_sp, m_w_pa, m_w_pb, m_w_out)))
    v = dict(zip(names, (v_c_ctx, v_w_mod, v_b_mod, v_g_pre, v_g_post, v_w_in, v_w_conv, v_a_log, v_dt_bias, v_g_onorm,
                         v_gm_ln_g, v_gm_ln_b, v_w_sp, v_b_sp, v_w_pa, v_w_pb, v_w_out)))
    xy = 2 * lax.axis_index("x") + lax.axis_index("y")
    where = jnp.stack([lax.axis_index("c"), xy, 2 * xy + lax.axis_index("c")]).astype(jnp.int32)

    shards = [a[0].astype(BF16) for a in (w_mod, w_in, w_pa, w_pb, w_out)]
    gathered, wconv_all = _gather_shards(shards, w_conv[0], name="gather_weights")
    own = lambda full, shard: lax.dynamic_update_slice(full, shard[None], (xy, 0, 0))
    wm_all, win_all, wpa_all, wpb_all, wout_all = [own(f, s) for f, s in zip(gathered, shards)]
    wconv_all = own(wconv_all, w_conv[0])
    w_conv_f = jnp.concatenate([wconv_all[s] for s in range(N_CHIPS)], axis=1)
    shard_cols = IN_COLS // N_CHIPS
    cut = OFF_A - shard_cols
    assert 0 < cut and cut + 4 * H < shard_cols
    w_qkv = jnp.concatenate([win_all[0], win_all[1][:, :cut]], axis=1)
    w_ab = jnp.pad(win_all[1][:, cut:cut + 4 * H], ((0, 0), (0, DH - 4 * H)))
    w_rest = jnp.concatenate([win_all[1][:, cut + 4 * H:], win_all[2], win_all[3]], axis=1)

    blk = D // N_CHIPS
    rest_cut = shard_cols - cut - 4 * H
    big_names = ("w_in", "w_pa", "w_pb", "w_out")
    in_flight = []

    def start_reduce(grads):
        dw_qkv, dw_ab, dw_rest = grads["w_in"]
        g_win = jnp.stack([dw_qkv[:, :shard_cols],
                           jnp.concatenate([dw_qkv[:, shard_cols:], dw_ab, dw_rest[:, :rest_cut]], axis=1),
                           dw_rest[:, rest_cut:rest_cut + shard_cols], dw_rest[:, rest_cut + shard_cols:]]).astype(BF16)
        sectioned = [g_win] + [grads[k].reshape(N_CHIPS, blk, D) for k in ("w_pa", "w_pb", "w_out")]
        handle, token = _reduce_start(sectioned, where)
        in_flight.append(handle)
        return token

    loss_local, grad_x, g = _local_step(
        x[0], c, ctx[0], c_ctx, loss_target[0], wm_all, b_mod, g_pre, g_post, w_qkv, w_ab, w_rest,
        w_conv_f, a_log[0], dt_bias[0], g_onorm, gm_ln_g, gm_ln_b, w_sp[0], b_sp[0],
        wpa_all.reshape(D, D), wpb_all.reshape(D, D), wout_all.reshape(D, D), on_large_gradients=start_reduce)
    g["loss"] = loss_local
    g["_pad"] = jnp.zeros((SMALL_LAYOUT["_pad"][1],), F32)
    g["mod_factors"] = lax.dynamic_update_slice(jnp.zeros((2 * N_CHIPS, 4 * D), F32),
                                                jnp.concatenate(g["mod_factors"])[None], (where[2], 0))
    tail = jnp.zeros((N_CHIPS * SMALL_ROWS * 128 - sum(s for _, s in SMALL_LAYOUT.values()),), F32)
    flat = jnp.concatenate([g[k].reshape(-1) for k in SMALL_LAYOUT] + [tail])
    reduced, gr_small = _reduce_finish(in_flight[0], flat.reshape(N_CHIPS, SMALL_ROWS, 128), where, g["b_mod"])
    gr_tiny = gr_small[TINY_ROW0:]

    def entry(arr, k):
        off, size = SMALL_LAYOUT[k]
        row, col = off // 128 - TINY_ROW0, off % 128
        return arr[row:row + size // 128].reshape(-1) if size >= 128 else arr[row, col:col + size]

    factors = entry(gr_tiny, "mod_factors").reshape(2 * N_CHIPS, 4 * D)
    pad_rows = lambda a: jnp.pad(a, ((0, 16 - a.shape[0]), (0, 0)))
    lhs = pad_rows(jnp.concatenate([factors[:, :D], g["silu_c_ctx"]], axis=0))
    rhs = pad_rows(jnp.concatenate([factors[:, D:], entry(gr_tiny, "dm_ctx")[None]], axis=0))
    mod_cols = 3 * D // N_CHIPS
    gr_wm = _mm(lhs, lax.dynamic_slice(rhs, (0, xy * mod_cols), (16, mod_cols)), ta=True, name="dw_mod")

    res = {k: _adamw(w[k], gr, m[k], v[k], name=f"adamw_{k}") for k, gr in zip(big_names, reduced)}
    res["w_mod"] = _adamw(w_mod[0], gr_wm, m_w_mod[0], v_w_mod[0], name="adamw_w_mod")
    res["w_sp"] = _adamw(w_sp.reshape(-1, 128), gr_small, m_w_sp.reshape(-1, 128), v_w_sp.reshape(-1, 128),
                         name="adamw_w_sp")
    tiny_names = [k for k in SMALL_LAYOUT if k not in ("w_sp", "w_conv", "dm_ctx", "mod_factors", "loss", "_pad")]

    def pack(src):
        parts = [src[k].reshape(-1) if k in tiny_names else jnp.zeros((s,), F32)
                 for k, (_, s) in SMALL_LAYOUT.items() if k != "w_sp"]
        return jnp.concatenate(parts + [tail]).reshape(-1, 128)

    tiny_res = _adamw(pack(w), gr_tiny, pack(m), pack(v), name="adamw_tiny")
    for k in tiny_names:
        res[k] = [entry(r, k) for r in tiny_res]
    g_conv = lax.dynamic_slice(entry(gr_tiny, "w_conv").reshape(3, 3 * D), (0, xy * (3 * D // N_CHIPS)),
                               (3, 3 * D // N_CHIPS))
    conv_res = _adamw(jnp.pad(w_conv[0], ((0, 5), (0, 0))), jnp.pad(g_conv, ((0, 5), (0, 0))),
                      jnp.pad(m_w_conv[0], ((0, 5), (0, 0))), jnp.pad(v_w_conv[0], ((0, 5), (0, 0))), name="adamw_w_conv")
    res["w_conv"] = [r[:3] for r in conv_res]
    res = {k: [r.reshape(w[k].shape) for r in res[k]] for k in names}

    out = [entry(gr_tiny, "loss").reshape(()), grad_x[None]]
    for i in range(4):
        out += [res[k][i] for k in names]
    return tuple(out)
```

```python
import functools

import jax
import jax.numpy as jnp
from jax import lax
from jax.experimental import pallas as pl
from jax.experimental.pallas import tpu as pltpu

F32 = jnp.float32
BF16 = jnp.bfloat16
HI = lax.Precision.HIGHEST
MESH = pl.DeviceIdType.MESH

D = 1024
H = 8
DH = 128
CH = 64
LOG_CH = 6
PAIR = 2 * CH
GM = 128
assert 1 << LOG_CH == CH and PAIR == DH
PREC_POWERS = ((lax.Precision.HIGH, lax.Precision.HIGH),) * 3 + ((None, None),) * 2
assert len(PREC_POWERS) == LOG_CH - 1
HEADS_PER_ITER_FWD = 8
HEADS_PER_ITER_BWD = 8
EPS = 1e-6
N_CHIPS = 4
OFF_A = 3 * D
OFF_ZB = OFF_A + 4 * H
IN_COLS = OFF_ZB + 6 * D
VMEM_LIMIT_V7X = 56 * 1024 * 1024
DMA_CHUNK_BYTES = 2 * 1024 * 1024

ADAM_LR, ADAM_B1, ADAM_B2, ADAM_EPS, ADAM_WD, ADAM_STEP = 0.001, 0.9, 0.999, 1e-08, 0.01, 10

SMALL_LAYOUT = {}
_off = 0
for _n, _s in (("w_sp", H * GM * GM), ("w_conv", 3 * 3 * D), ("c_ctx", D), ("b_mod", 3 * D), ("g_pre", D), ("g_post", D),
               ("gm_ln_g", D), ("gm_ln_b", D), ("b_sp", H * GM), ("dm_ctx", 3 * D), ("mod_factors", 8 * 4 * D),
               ("g_onorm", DH), ("a_log", 2 * H), ("dt_bias", 2 * H), ("loss", 1), ("_pad", 128 - 4 * H - 1)):
    SMALL_LAYOUT[_n] = (_off, _s)
    _off += _s
SMALL_ROWS = 368
assert N_CHIPS * SMALL_ROWS * 128 >= _off and (SMALL_ROWS // 2) % 8 == 0 and _off % 128 == 0
TINY_ROW0 = SMALL_LAYOUT["w_conv"][0] // 128


def _params(sem=None):
    return pltpu.CompilerParams(dimension_semantics=sem, vmem_limit_bytes=VMEM_LIMIT_V7X)


def _tile(n, cands=(256, 128, 64, 32, 16, 8)):
    if n <= cands[0]:
        return n
    for cand in cands:
        if n % cand == 0:
            return cand
    return n


def _silu(x):
    return x * jax.nn.sigmoid(x)


def _gelu(x):
    return 0.5 * x * (1.0 + jnp.tanh(0.7978845608028654 * (x + 0.044715 * (x * x * x))))


def _mm(a, b, *, ta=False, tb=False, out_dtype=F32, tm=1024, tn=1024, tk=1024, add=None, add_from=0,
        b_sections=False, out_sections=False, after=None, name):
    m, k = (a.shape[1], a.shape[0]) if ta else a.shape
    if b_sections:
        sect = b.shape[2]
        n = b.shape[1] if tb else b.shape[0] * sect
        tn, tk = (tn, sect) if tb else (sect, tk)
    else:
        n = b.shape[0] if tb else b.shape[1]
    tm, tn, tk = min(tm, m), min(tn, n), min(tk, k)
    assert m % tm == 0 and n % tn == 0 and k % tk == 0, (name, m, n, k, tm, tn, tk)
    nk = k // tk
    dims = (((0,) if ta else (1,), (1,) if tb else (0,)), ((), ()))
    has_add = add is not None
    assert not has_add or add.shape == (m, n - add_from * tn), (name, add.shape)

    def body(*refs):
        a_ref, b_ref = refs[:2]
        o_ref = refs[2 + int(has_add) + int(after is not None)]
        acc_ref = refs[-1]
        kk = pl.program_id(2)
        bv = b_ref[0] if b_sections else b_ref[...]
        part = lax.dot_general(a_ref[...].astype(BF16), bv.astype(BF16), dims, preferred_element_type=F32)

        def finish(res):
            if has_add:
                res = res + jnp.where(pl.program_id(1) >= add_from, refs[2][...], 0.0)
            if out_sections:
                o_ref[0] = res.astype(out_dtype)
            else:
                o_ref[...] = res.astype(out_dtype)

        if nk == 1:
            finish(part)
            return

        @pl.when(kk == 0)
        def _():
            acc_ref[...] = part

        @pl.when((kk > 0) & (kk < nk - 1))
        def _():
            acc_ref[...] += part

        @pl.when(kk == nk - 1)
        def _():
            finish(acc_ref[...] + part)

    a_spec = pl.BlockSpec((tk, tm), lambda i, j, q: (q, i)) if ta else pl.BlockSpec((tm, tk), lambda i, j, q: (i, q))
    if b_sections:
        b_spec = (pl.BlockSpec((1, tn, tk), lambda i, j, q: (q, j, 0)) if tb
                  else pl.BlockSpec((1, tk, tn), lambda i, j, q: (j, q, 0)))
    else:
        b_spec = pl.BlockSpec((tn, tk), lambda i, j, q: (j, q)) if tb else pl.BlockSpec((tk, tn), lambda i, j, q: (q, j))
    add_spec = [pl.BlockSpec((tm, tn), lambda i, j, q: (i, jnp.maximum(j - add_from, 0)))] if has_add else []
    if out_sections:
        out_spec, out_shape = pl.BlockSpec((1, tm, tn), lambda i, j, q: (j, i, 0)), (n // tn, m, tn)
    else:
        out_spec, out_shape = pl.BlockSpec((tm, tn), lambda i, j, q: (i, j)), (m, n)
    return pl.pallas_call(
        body, name=name, grid=(m // tm, n // tn, nk),
        in_specs=[a_spec, b_spec] + add_spec + ([pl.BlockSpec(memory_space=pl.ANY)] if after is not None else []),
        out_specs=out_spec, out_shape=jax.ShapeDtypeStruct(out_shape, out_dtype),
        scratch_shapes=[pltpu.VMEM((tm, tn), F32)] if nk > 1 else [],
        compiler_params=_params(("parallel", "parallel", "arbitrary")),
    )(*([a, b] + ([add] if has_add else []) + ([after] if after is not None else [])))


def _h_fn(x, g, m):
    shift, scale = m[:, 0:D], m[:, D:2 * D]
    r = lax.rsqrt(jnp.mean(x * x, axis=-1, keepdims=True) + EPS)
    return (x * r * g) * (1.0 + scale) + shift


def _norm_fwd(x, g, mod, bmod, *, name):
    rows = x.shape[0]
    t = min(512, rows)

    def body(x_ref, g_ref, m_ref, b_ref, h_ref):
        h_ref[...] = _h_fn(x_ref[...], g_ref[...], m_ref[...] + b_ref[...]).astype(BF16)

    vec = lambda w: pl.BlockSpec((1, w), lambda i: (0, 0))
    return pl.pallas_call(
        body, name=name, grid=(rows // t,),
        in_specs=[pl.BlockSpec((t, D), lambda i: (i, 0)), vec(D), vec(3 * D), vec(3 * D)],
        out_specs=pl.BlockSpec((t, D), lambda i: (i, 0)),
        out_shape=jax.ShapeDtypeStruct((rows, D), BF16),
        compiler_params=_params(("parallel",)),
    )(x, g, mod, bmod)


def _norm_bwd(x, g, mod, bmod, dh_parts, resid, *, name):
    rows = x.shape[0]
    t = min(512, rows)
    n_parts = len(dh_parts)
    has_resid = resid is not None

    def body(*refs):
        x_ref, g_ref, m_ref, b_ref = refs[:4]
        parts = refs[4:4 + n_parts]
        r_ref = refs[4 + n_parts] if has_resid else None
        dx_ref, dg_ref, dm_ref = refs[-3:]
        i = pl.program_id(0)
        dh = parts[0][...]
        for p in parts[1:]:
            dh = dh + p[...]
        _, vjp = jax.vjp(_h_fn, x_ref[...], g_ref[...], m_ref[...] + b_ref[...])
        dx, dg, dm = vjp(dh)
        if has_resid:
            dx = dx + r_ref[...]
        dx_ref[...] = dx

        @pl.when(i == 0)
        def _():
            dg_ref[...] = dg
            dm_ref[...] = dm

        @pl.when(i > 0)
        def _():
            dg_ref[...] += dg
            dm_ref[...] += dm

    vec = lambda w: pl.BlockSpec((1, w), lambda i: (0, 0))
    tile = pl.BlockSpec((t, D), lambda i: (i, 0))
    ins = [x, g, mod, bmod, *dh_parts] + ([resid] if has_resid else [])
    return pl.pallas_call(
        body, name=name, grid=(rows // t,),
        in_specs=[tile, vec(D), vec(3 * D), vec(3 * D)] + [tile] * (n_parts + int(has_resid)),
        out_specs=[tile, vec(D), vec(3 * D)],
        out_shape=[jax.ShapeDtypeStruct((rows, D), F32), jax.ShapeDtypeStruct((1, D), F32),
                   jax.ShapeDtypeStruct((1, 3 * D), F32)],
        compiler_params=_params(("arbitrary",)),
    )(*ins)


def _conv_tile(u_ref, r0, t, rows, w0, w1, w2):
    u = u_ref[pl.ds(r0, t), :]
    prev8 = u_ref[pl.ds(pl.multiple_of(jnp.maximum(r0 - 8, 0), 8), 8), :]
    next8 = u_ref[pl.ds(pl.multiple_of(jnp.minimum(r0 + t, rows - 8), 8), 8), :]
    r8 = lax.broadcasted_iota(jnp.int32, (8, DH), 0)
    prev_row = jnp.sum(jnp.where(r8 == 7, prev8, 0.0), axis=0, keepdims=True)
    next_row = jnp.sum(jnp.where(r8 == 0, next8, 0.0), axis=0, keepdims=True)
    prev_row = jnp.where(r0 > 0, prev_row, 0.0)
    next_row = jnp.where(r0 + t < rows, next_row, 0.0)
    ri = lax.broadcasted_iota(jnp.int32, (t, DH), 0)
    um1 = jnp.where(ri == 0, prev_row, pltpu.roll(u, 1, 0))
    up1 = jnp.where(ri == t - 1, next_row, pltpu.roll(u, t - 1, 0))
    return w0 * um1 + w1 * u + w2 * up1, um1, u, up1


def _rowlocal(z, is_norm):
    y = _silu(z)
    yn = y * lax.rsqrt(jnp.sum(y * y, axis=-1, keepdims=True) + EPS)
    return jnp.where(is_norm, yn, y)


def _prep_fwd(p, wconv, n_norm, *, name):
    rows, nb = p.shape[0], p.shape[1] // DH
    t = min(512, rows)

    def body(u_ref, w_ref, o_ref):
        is_norm = pl.program_id(0) < n_norm
        w0, w1, w2 = w_ref[0:1, :], w_ref[1:2, :], w_ref[2:3, :]

        def step(s, carry):
            r0 = pl.multiple_of(s * t, t)
            z, _, _, _ = _conv_tile(u_ref, r0, t, rows, w0, w1, w2)
            o_ref[0, pl.ds(r0, t), :] = _rowlocal(z, is_norm)
            return carry

        lax.fori_loop(0, rows // t, step, 0)

    return pl.pallas_call(
        body, name=name, grid=(nb,),
        in_specs=[pl.BlockSpec((rows, DH), lambda j: (0, j)), pl.BlockSpec((3, DH), lambda j: (0, j))],
        out_specs=pl.BlockSpec((1, rows, DH), lambda j: (j, 0, 0)),
        out_shape=jax.ShapeDtypeStruct((nb, rows, DH), F32),
        compiler_params=_params(("parallel",)),
    )(p, wconv)


def _prep_bwd(p, wconv, d_a, d_b, n_norm, *, name):
    rows, nb = p.shape[0], p.shape[1] // DH
    t = min(512, rows)

    def body(u_ref, w_ref, da_ref, db_ref, du_ref, dw_ref, dz_ref):
        is_norm = pl.program_id(0) < n_norm
        w0, w1, w2 = w_ref[0:1, :], w_ref[1:2, :], w_ref[2:3, :]

        def step1(s, carry):
            a0, a1, a2 = carry
            r0 = pl.multiple_of(s * t, t)
            z, um1, u, up1 = _conv_tile(u_ref, r0, t, rows, w0, w1, w2)
            _, vjp = jax.vjp(lambda zz: _rowlocal(zz, is_norm), z)
            (dz,) = vjp(da_ref[0, pl.ds(r0, t), :] + db_ref[0, pl.ds(r0, t), :])
            dz_ref[pl.ds(r0, t), :] = dz
            red = lambda v: jnp.sum(v, axis=0, keepdims=True)
            return a0 + red(dz * um1), a1 + red(dz * u), a2 + red(dz * up1)

        zero = jnp.zeros((1, DH), F32)
        a0, a1, a2 = lax.fori_loop(0, rows // t, step1, (zero, zero, zero))
        dw_ref[0:1, :] = a0
        dw_ref[1:2, :] = a1
        dw_ref[2:3, :] = a2

        def step2(s, carry):
            r0 = pl.multiple_of(s * t, t)
            du, _, _, _ = _conv_tile(dz_ref, r0, t, rows, w2, w1, w0)
            du_ref[pl.ds(r0, t), :] = du.astype(BF16)
            return carry

        lax.fori_loop(0, rows // t, step2, 0)

    col = pl.BlockSpec((rows, DH), lambda j: (0, j))
    w_spec = pl.BlockSpec((3, DH), lambda j: (0, j))
    d_spec = pl.BlockSpec((1, rows, DH), lambda j: (j, 0, 0))
    return pl.pallas_call(
        body, name=name, grid=(nb,),
        in_specs=[col, w_spec, d_spec, d_spec], out_specs=[col, w_spec],
        out_shape=[jax.ShapeDtypeStruct((rows, nb * DH), BF16), jax.ShapeDtypeStruct((3, nb * DH), F32)],
        scratch_shapes=[pltpu.VMEM((rows, DH), F32)],
        compiler_params=_params(("parallel",)),
    )(p, wconv, d_a, d_b)


def _gates_fn(pab, avec, dvec):
    t = pab.shape[0]
    lane = lax.broadcasted_iota(jnp.int32, pab.shape, 1)
    xg = pab + dvec
    sp = jnp.maximum(xg, 0.0) + jnp.log(1.0 + jnp.exp(-jnp.abs(xg)))
    g = jnp.where(lane < 2 * H, -jnp.exp(avec) * sp, 0.0)
    ii = lax.broadcasted_iota(jnp.int32, (t, t), 0)
    jj = lax.broadcasted_iota(jnp.int32, (t, t), 1)
    same = (ii >> LOG_CH) == (jj >> LOG_CH)
    cum_f = _dot(jnp.where(same & (jj <= ii), 1.0, 0.0), g, precision=HI)
    cum_r = _dot(jnp.where(same & (jj >= ii), 1.0, 0.0), g, precision=HI)
    return jnp.where(lane < H, cum_f, jnp.where(lane < 2 * H, cum_r, jnp.where(lane < 4 * H, jax.nn.sigmoid(pab), 0.0)))


def _gates_fwd(pab, avec, dvec, *, name):
    rows = pab.shape[0]
    t = min(512, rows)

    def body(p_ref, a_ref, d_ref, o_ref):
        o_ref[...] = _gates_fn(p_ref[...], a_ref[...], d_ref[...])

    vec = pl.BlockSpec((1, DH), lambda i: (0, 0))
    tile = pl.BlockSpec((t, DH), lambda i: (i, 0))
    return pl.pallas_call(
        body, name=name, grid=(rows // t,), in_specs=[tile, vec, vec], out_specs=tile,
        out_shape=jax.ShapeDtypeStruct((rows, DH), F32), compiler_params=_params(("parallel",)),
    )(pab, avec, dvec)


def _gates_bwd(pab, avec, dvec, d_a, d_b, *, name):
    rows = pab.shape[0]
    t = min(512, rows)

    def body(p_ref, a_ref, d_ref, da_ref, db_ref, dp_ref, dav_ref, ddv_ref):
        i = pl.program_id(0)
        _, vjp = jax.vjp(_gates_fn, p_ref[...], a_ref[...], d_ref[...])
        dp, dav, ddv = vjp(da_ref[...] + db_ref[...])
        dp_ref[...] = dp.astype(BF16)

        @pl.when(i == 0)
        def _():
            dav_ref[...] = dav
            ddv_ref[...] = ddv

        @pl.when(i > 0)
        def _():
            dav_ref[...] += dav
            ddv_ref[...] += ddv

    vec = pl.BlockSpec((1, DH), lambda i: (0, 0))
    tile = pl.BlockSpec((t, DH), lambda i: (i, 0))
    return pl.pallas_call(
        body, name=name, grid=(rows // t,), in_specs=[tile, vec, vec, tile, tile], out_specs=[tile, vec, vec],
        out_shape=[jax.ShapeDtypeStruct((rows, DH), BF16), jax.ShapeDtypeStruct((1, DH), F32),
                   jax.ShapeDtypeStruct((1, DH), F32)],
        compiler_params=_params(("arbitrary",)),
    )(pab, avec, dvec, d_a, d_b)


def _dot_general(a, b, dims, precision=None):
    return lax.dot_general(a, b, (dims, ((), ())), precision=precision, preferred_element_type=F32)


@functools.partial(jax.custom_vjp, nondiff_argnums=(2,))
def _dot_bf16(a, b, dims):
    return _dot_general(a.astype(BF16), b.astype(BF16), dims)


def _dot_bf16_fwd(a, b, dims):
    return _dot_bf16(a, b, dims), (a, b)


def _dot_bf16_bwd(dims, res, g):
    a, b = res
    (ca,), (cb,) = dims
    da = _dot_bf16(g, b, ((1,), (1 - cb,))) if ca == 1 else _dot_bf16(b, g, ((1 - cb,), (1,)))
    db = _dot_bf16(a, g, ((1 - ca,), (0,))) if cb == 0 else _dot_bf16(g, a, ((0,), (1 - ca,)))
    return da, db


_dot_bf16.defvjp(_dot_bf16_fwd, _dot_bf16_bwd)


def _dot(a, b, dims=((1,), (0,)), precision=None):
    if precision is None and a.dtype == F32 and b.dtype == F32:
        return _dot_bf16(a, b, dims)
    return _dot_general(a, b, dims, precision)


_NT = ((1,), (1,))
_TN = ((0,), (0,))


@jax.custom_vjp
def _saved_inverse(neg_a, inv):
    return inv


def _saved_inverse_fwd(neg_a, inv):
    return inv, inv


def _saved_inverse_bwd(inv, d_inv):
    idx = range(len(inv))
    left = [_dot(inv[i], d_inv[i], _TN) for i in idx]
    d_neg_a = [_dot(left[i], inv[i], _NT) for i in idx]
    return d_neg_a, [jnp.zeros_like(t) for t in inv]


_saved_inverse.defvjp(_saved_inverse_fwd, _saved_inverse_bwd)


def _pairs(s, q, k, v, gcol, bcol, revs, inv_saved=None):
    idx = range(len(revs))
    ii = lax.broadcasted_iota(jnp.int32, (PAIR, PAIR), 0)
    jj = lax.broadcasted_iota(jnp.int32, (PAIR, PAIR), 1)
    same = (ii >> LOG_CH) == (jj >> LOG_CH)
    incl_d = (same & (ii >= jj), same & (ii <= jj))
    strict_d = (same & (ii > jj), same & (ii < jj))
    incl = [incl_d[int(r)] for r in revs]
    strict = [strict_d[int(r)] for r in revs]
    eye = jnp.where(ii == jj, 1.0, 0.0)
    gc_i = [jnp.broadcast_to(gcol[i], (PAIR, DH)) for i in idx]
    gc_j = [gc_i[i].T for i in idx]
    decay = [jnp.where(incl[i], jnp.exp(jnp.where(incl[i], gc_i[i] - gc_j[i], 0.0)), 0.0) for i in idx]
    b_b = [jnp.broadcast_to(bcol[i], (PAIR, DH)) for i in idx]
    kb = [k[i] * b_b[i] for i in idx]
    kk = [_dot(kb[i], k[i], _NT) for i in idx]
    bp = [jnp.where(strict[i], -kk[i] * decay[i], 0.0) for i in idx]
    if inv_saved is not None:
        inv = _saved_inverse(bp, inv_saved)
    else:
        low = bp
        for prec_sq, prec_acc in PREC_POWERS:
            bp = [_dot(bp[i], bp[i], precision=prec_sq) for i in idx]
            more = [_dot(low[i], bp[i], precision=prec_acc) for i in idx]
            low = [low[i] + bp[i] + more[i] for i in idx]
        inv = [eye + low[i] for i in idx]
    eg = [jnp.exp(gc_i[i]) for i in idx]
    sol = [_dot(inv[i], jnp.concatenate([v[i] * b_b[i], kb[i] * eg[i]], axis=1)) for i in idx]
    u_val = [sol[i][:, :DH] for i in idx]
    w_key = [sol[i][:, DH:] for i in idx]
    row = lax.broadcasted_iota(jnp.int32, (PAIR, 1), 0)
    has_q = q[0] is not None
    if has_q:
        qc = [q[i] * (DH ** -0.5) for i in idx]
        qk = [_dot(qc[i], k[i], _NT) for i in idx]
        attn = [qk[i] * decay[i] for i in idx]
        qd = [qc[i] * eg[i] for i in idx]
    outs = [[None, None] for _ in idx]
    zeros = jnp.zeros((CH, DH), F32)
    for step in range(2):
        cidx = [(1 - step) if revs[i] else step for i in idx]
        sl = [slice(c * CH, (c + 1) * CH) for c in cidx]
        last = [c * CH if revs[i] else c * CH + CH - 1 for i, c in zip(idx, cidx)]
        gl = [jnp.sum(jnp.where(row == last[i], gcol[i], 0.0), axis=0, keepdims=True) for i in idx]
        k_tail = [k[i][sl[i]] * jnp.exp(gl[i] - gc_i[i][sl[i]]) for i in idx]
        ws = [_dot(w_key[i][sl[i]], s[i]) for i in idx]
        v_new = [u_val[i][sl[i]] - ws[i] for i in idx]
        if has_q:
            v_pad = [jnp.concatenate([v_new[i], zeros] if cidx[i] == 0 else [zeros, v_new[i]], axis=0) for i in idx]
            o_state = [_dot(qd[i][sl[i]], s[i]) for i in idx]
            o_local = [_dot(attn[i][sl[i]], v_pad[i]) for i in idx]
            for i in idx:
                outs[i][cidx[i]] = o_state[i] + o_local[i]
        kv = [_dot(k_tail[i], v_new[i], _TN) for i in idx]
        s = [s[i] * jnp.exp(gl[i]) + kv[i] for i in idx]
    return s, ([jnp.concatenate(outs[i], axis=0) for i in idx] if has_q else None), inv


def _lane_col(tile, idx):
    lane = lax.broadcasted_iota(jnp.int32, tile.shape, 1)
    return jnp.sum(jnp.where(lane == idx, tile, 0.0), axis=1, keepdims=True)


def _gdn_fwd(qkv, gb, s0f, s0b, has_q, *, name):
    nb, rows, _ = qkv.shape
    n = rows // PAIR
    qoff = H if has_q else 0

    def body(qf_ref, qb_ref, gf_ref, gr_ref, s0f_ref, s0b_ref, of_ref, ob_ref, ssf_ref, ssb_ref, tsf_ref, tsb_ref,
             sf_ref, sb_ref):
        @pl.when(pl.program_id(0) == 0)
        def _():
            sf_ref[...] = s0f_ref[...]
            sb_ref[...] = s0b_ref[...]

        gtiles = (gf_ref[...], gr_ref[...])

        dirs = ((qf_ref, sf_ref, ssf_ref, of_ref), (qb_ref, sb_ref, ssb_ref, ob_ref))
        ts_refs = (tsf_ref, tsb_ref)

        def heads(hg, carry):
            work = [(hg * HEADS_PER_ITER_FWD + j, d) for j in range(HEADS_PER_ITER_FWD) for d in range(2)]
            loaded = []
            for h, d in work:
                q_ref, s_ref, _, _ = dirs[d]
                loaded.append((s_ref[h], q_ref[h] if has_q else None, q_ref[qoff + h], q_ref[qoff + H + h],
                               _lane_col(gtiles[d], d * H + h), _lane_col(gtiles[d], 2 * H + d * H + h)))
            s_new, o, inv = _pairs(*[list(col) for col in zip(*loaded)], revs=[d == 1 for _, d in work])
            for i, (h, d) in enumerate(work):
                _, s_ref, ss_ref, o_ref = dirs[d]
                ss_ref[0, h] = loaded[i][0]
                ts_refs[d][0, h] = inv[i]
                s_ref[h] = s_new[i]
                o_ref[h] = o[i] if has_q else jnp.zeros((PAIR, DH), F32)
            return carry

        if HEADS_PER_ITER_FWD == H:
            heads(0, 0)
        else:
            lax.fori_loop(0, H // HEADS_PER_ITER_FWD, heads, 0)

    fwd3 = lambda i: (0, i, 0)
    rev3 = lambda i: (0, n - 1 - i, 0)
    state = pl.BlockSpec((H, DH, DH), lambda i: (0, 0, 0))
    saved = pl.BlockSpec((1, H, DH, DH), lambda i: (i, 0, 0, 0))
    return pl.pallas_call(
        body, name=name, grid=(n,),
        in_specs=[pl.BlockSpec((nb, PAIR, DH), fwd3), pl.BlockSpec((nb, PAIR, DH), rev3),
                  pl.BlockSpec((PAIR, DH), lambda i: (i, 0)), pl.BlockSpec((PAIR, DH), lambda i: (n - 1 - i, 0)),
                  state, state],
        out_specs=[pl.BlockSpec((H, PAIR, DH), fwd3), pl.BlockSpec((H, PAIR, DH), rev3), saved, saved, saved, saved,
                   state, state],
        out_shape=[jax.ShapeDtypeStruct((H, rows, DH), F32)] * 2 + [jax.ShapeDtypeStruct((n, H, DH, DH), F32)] * 4
        + [jax.ShapeDtypeStruct((H, DH, DH), F32)] * 2,
        compiler_params=_params(("arbitrary",)),
    )(qkv, qkv, gb, gb, s0f, s0b)


def _gdn_bwd(qkv, gb, ssf, ssb, tsf, tsb, do, dsf, dsb, has_q, *, name):
    nb, rows, _ = qkv.shape
    n = rows // PAIR
    qoff = H if has_q else 0

    def body(qf_ref, qb_ref, gf_ref, gr_ref, ssf_ref, ssb_ref, tsf_ref, tsb_ref, dof_ref, dob_ref, dsf0_ref, dsb0_ref,
             dqf_ref, dqb_ref, dgf_ref, dgr_ref, dsf_ref, dsb_ref):
        ts_refs = (tsf_ref, tsb_ref)
        @pl.when(pl.program_id(0) == 0)
        def _():
            dsf_ref[...] = dsf0_ref[...]
            dsb_ref[...] = dsb0_ref[...]

        gtiles = (gf_ref[...], gr_ref[...])
        lane = lax.broadcasted_iota(jnp.int32, (PAIR, DH), 1)

        dirs = ((qf_ref, ssf_ref, dof_ref, dsf_ref, dqf_ref), (qb_ref, ssb_ref, dob_ref, dsb_ref, dqb_ref))

        def heads(hg, carry):
            out = list(carry)
            work = [(hg * HEADS_PER_ITER_BWD + j, d) for j in range(HEADS_PER_ITER_BWD) for d in range(2)]
            revs = [d == 1 for _, d in work]
            s_in, q_in, k_in, v_in, g_in, b_in, ds_out, do_out, inv_in = [], [], [], [], [], [], [], [], []
            for h, d in work:
                q_ref, ss_ref, do_ref, ds_ref, _ = dirs[d]
                s_in.append(ss_ref[0, h])
                inv_in.append(ts_refs[d][0, h])
                q_in.append(q_ref[h] if has_q else None)
                k_in.append(q_ref[qoff + h])
                v_in.append(q_ref[qoff + H + h])
                g_in.append(_lane_col(gtiles[d], d * H + h))
                b_in.append(_lane_col(gtiles[d], 2 * H + d * H + h))
                ds_out.append(ds_ref[h])
                do_out.append(do_ref[h] if has_q else None)
            if has_q:
                _, vjp = jax.vjp(lambda s_, q_, k_, v_, g_, b_: _pairs(s_, q_, k_, v_, g_, b_, revs, inv_in)[:2],
                                 s_in, q_in, k_in, v_in, g_in, b_in)
                ds, dq, dk, dv, dg, db = vjp((ds_out, do_out))
            else:
                _, vjp = jax.vjp(lambda s_, k_, v_, g_, b_: _pairs(s_, q_in, k_, v_, g_, b_, revs, inv_in)[0],
                                 s_in, k_in, v_in, g_in, b_in)
                ds, dk, dv, dg, db = vjp(ds_out)
            for i, (h, d) in enumerate(work):
                dq_ref, ds_ref = dirs[d][4], dirs[d][3]
                ds_ref[h] = ds[i]
                if has_q:
                    dq_ref[h] = dq[i]
                dq_ref[qoff + h], dq_ref[qoff + H + h] = dk[i], dv[i]
                out[d] = (out[d] + jnp.where(lane == d * H + h, dg[i], 0.0)
                          + jnp.where(lane == 2 * H + d * H + h, db[i], 0.0))
            return tuple(out)

        zero = jnp.zeros((PAIR, DH), F32)
        if HEADS_PER_ITER_BWD == H:
            dgf, dgr = heads(0, (zero, zero))
        else:
            dgf, dgr = lax.fori_loop(0, H // HEADS_PER_ITER_BWD, heads, (zero, zero))
        dgf_ref[...] = dgf
        dgr_ref[...] = dgr

    fwd3 = lambda i: (0, n - 1 - i, 0)
    rev3 = lambda i: (0, i, 0)
    state = pl.BlockSpec((H, DH, DH), lambda i: (0, 0, 0))
    saved = pl.BlockSpec((1, H, DH, DH), lambda i: (n - 1 - i, 0, 0, 0))
    gf_spec = pl.BlockSpec((PAIR, DH), lambda i: (n - 1 - i, 0))
    gr_spec = pl.BlockSpec((PAIR, DH), lambda i: (i, 0))
    return pl.pallas_call(
        body, name=name, grid=(n,),
        in_specs=[pl.BlockSpec((nb, PAIR, DH), fwd3), pl.BlockSpec((nb, PAIR, DH), rev3), gf_spec, gr_spec,
                  saved, saved, saved, saved,
                  pl.BlockSpec((H, PAIR, DH), fwd3), pl.BlockSpec((H, PAIR, DH), rev3), state, state],
        out_specs=[pl.BlockSpec((nb, PAIR, DH), fwd3), pl.BlockSpec((nb, PAIR, DH), rev3), gf_spec, gr_spec, state, state],
        out_shape=[jax.ShapeDtypeStruct((nb, rows, DH), F32)] * 2 + [jax.ShapeDtypeStruct((rows, DH), F32)] * 2
        + [jax.ShapeDtypeStruct((H, DH, DH), F32)] * 2,
        compiler_params=_params(("arbitrary",)),
    )(qkv, qkv, gb, gb, ssf, ssb, tsf, tsb, do, do, dsf, dsb)


def _stage1(zb, ua, va, za, o, gon, lng, lnb, wsp, bsp):
    gv = [_gelu(t) for t in va]
    mu = sum(jnp.sum(t, axis=-1, keepdims=True) for t in gv) * (1.0 / D)
    xc = [t - mu for t in gv]
    var = sum(jnp.sum(t * t, axis=-1, keepdims=True) for t in xc) * (1.0 / D)
    rs = lax.rsqrt(var + EPS)
    ya, yb = [], []
    for g in range(H):
        vv = xc[g] * rs * lng[g] + lnb[g]
        s = _dot(wsp[g], vv) + bsp[g]
        ya.append(_gelu(ua[g]) * s * _silu(za[g]))
        r = lax.rsqrt(jnp.mean(o[g] * o[g], axis=-1, keepdims=True) + EPS)
        yb.append(o[g] * r * gon * _silu(zb[g]))
    return ya, yb


def _stage2(ma, mb, ga, gb):
    return jax.nn.sigmoid(ga) * ma + jax.nn.sigmoid(gb) * mb


def _stage3(out, x, tgt, gpost, gate):
    r = out * lax.rsqrt(jnp.mean(out * out, axis=-1, keepdims=True) + EPS) * gpost
    err = x + gate * r - tgt
    return 0.5 * jnp.sum(jnp.mean(err * err, axis=-1, keepdims=True), axis=0, keepdims=True)


def _post(p_rest, o_f, o_b, x, tgt, mod, bmod, gon, lng, lnb, wsp, bspb, wpa, wpb, wout, gpost, *, name):
    rows = x.shape[0]
    n = rows // GM
    lanes = lambda g: slice(g * DH, (g + 1) * DH)
    bdot = lambda a, w_ref: _dot(a.astype(BF16), w_ref[...])
    bdot_t = lambda a, w_ref: _dot(a.astype(BF16), w_ref[...], _NT)

    def body(p_ref, of_ref, ob_ref, x_ref, t_ref, m_ref, bm_ref, gon_ref, lng_ref, lnb_ref, wsp_ref, bsp_ref,
             wpa_ref, wpb_ref, wout_ref, gp_ref,
             loss_ref, dp_ref, do_ref, dx_ref, ya_ref, yb_ref, mg_ref, dma_ref, dmb_ref, dout_ref,
             dvec_ref, dgon_ref, dwsp_ref, dbsp_ref):
        @pl.when(pl.program_id(0) == 0)
        def _():
            loss_ref[...] = jnp.zeros_like(loss_ref)
            dvec_ref[...] = jnp.zeros_like(dvec_ref)
            dgon_ref[...] = jnp.zeros_like(dgon_ref)
            dwsp_ref[...] = jnp.zeros_like(dwsp_ref)
            dbsp_ref[...] = jnp.zeros_like(dbsp_ref)

        piece = lambda blk: [p_ref[:, blk * D + g * DH: blk * D + (g + 1) * DH].astype(F32) for g in range(H)]
        zb, ua, va, za = piece(0), piece(1), piece(2), piece(3)
        o = [of_ref[g] + ob_ref[g] for g in range(H)]
        gon = gon_ref[...]
        lng = [lng_ref[:, lanes(g)] for g in range(H)]
        lnb = [lnb_ref[:, lanes(g)] for g in range(H)]
        wsp = [wsp_ref[g] for g in range(H)]
        bsp = [bsp_ref[g] for g in range(H)]
        (ya, yb), vjp1 = jax.vjp(_stage1, zb, ua, va, za, o, gon, lng, lnb, wsp, bsp)
        y_a, y_b = jnp.concatenate(ya, axis=1), jnp.concatenate(yb, axis=1)
        ma, mb = bdot(y_a, wpa_ref), bdot(y_b, wpb_ref)
        ga, gb = p_ref[:, 4 * D:5 * D].astype(F32), p_ref[:, 5 * D:6 * D].astype(F32)
        merged, vjp2 = jax.vjp(_stage2, ma, mb, ga, gb)
        out = bdot(merged, wout_ref)
        gate = m_ref[:, 2 * D:3 * D] + bm_ref[:, 2 * D:3 * D]
        loss, vjp3 = jax.vjp(_stage3, out, x_ref[...], t_ref[...], gp_ref[...], gate)
        loss_ref[...] += jnp.broadcast_to(loss, loss_ref.shape)

        dout, dx, _, dgpost, dgate = vjp3(jnp.ones((1, 1), F32))
        dx_ref[...] = dx
        dmerged = bdot_t(dout, wout_ref)
        dma, dmb, dga, dgb = vjp2(dmerged)
        dya, dyb = bdot_t(dma, wpa_ref), bdot_t(dmb, wpb_ref)
        dzb, dua, dva, dza, do, dgon, dlng, dlnb, dwsp, dbsp = vjp1(
            ([dya[:, lanes(g)] for g in range(H)], [dyb[:, lanes(g)] for g in range(H)]))

        for blk, dlist in enumerate((dzb, dua, dva, dza)):
            for g in range(H):
                dp_ref[:, blk * D + g * DH: blk * D + (g + 1) * DH] = dlist[g].astype(BF16)
        dp_ref[:, 4 * D:5 * D] = dga.astype(BF16)
        dp_ref[:, 5 * D:6 * D] = dgb.astype(BF16)
        for g in range(H):
            do_ref[g] = do[g]
            dwsp_ref[g] += dwsp[g]
            dbsp_ref[g] += dbsp[g]
            dvec_ref[2:3, lanes(g)] += dlng[g]
            dvec_ref[3:4, lanes(g)] += dlnb[g]
        dvec_ref[0:1, :] += dgpost
        dvec_ref[1:2, :] += dgate
        dgon_ref[0:1, :] += dgon
        ya_ref[...] = y_a.astype(BF16)
        yb_ref[...] = y_b.astype(BF16)
        mg_ref[...] = merged.astype(BF16)
        dma_ref[...] = dma.astype(BF16)
        dmb_ref[...] = dmb.astype(BF16)
        dout_ref[...] = dout.astype(BF16)

    row = lambda w: pl.BlockSpec((GM, w), lambda i: (i, 0))
    heads = pl.BlockSpec((H, GM, DH), lambda i: (0, i, 0))
    full = lambda shape: pl.BlockSpec(shape, lambda i: tuple(0 for _ in shape))
    sds = jax.ShapeDtypeStruct
    return pl.pallas_call(
        body, name=name, grid=(n,),
        in_specs=[row(6 * D), heads, heads, row(D), row(D), full((1, 3 * D)), full((1, 3 * D)), full((1, DH)),
                  full((1, D)), full((1, D)), full((H, GM, GM)), full((H, GM, GM)),
                  full((D, D)), full((D, D)), full((D, D)), full((1, D))],
        out_specs=[full((8, DH)), row(6 * D), heads, row(D)] + [row(D)] * 6
        + [full((8, D)), full((8, DH)), full((H, GM, GM)), full((H, GM, GM))],
        out_shape=[sds((8, DH), F32), sds((rows, 6 * D), BF16), sds((H, rows, DH), F32), sds((rows, D), F32)]
        + [sds((rows, D), BF16)] * 6 + [sds((8, D), F32), sds((8, DH), F32), sds((H, GM, GM), F32), sds((H, GM, GM), F32)],
        compiler_params=_params(("arbitrary",)),
    )(p_rest, o_f, o_b, x, tgt, mod, bmod, gon, lng, lnb, wsp, bspb, wpa, wpb, wout, gpost)


def _silu_rows(c, *, name):
    def body(c_ref, o_ref):
        o_ref[...] = _silu(c_ref[...])

    return pl.pallas_call(body, name=name, out_shape=jax.ShapeDtypeStruct(c.shape, F32))(c)


def _dsilu_mul(c, d, *, name):
    def body(c_ref, d_ref, o_ref):
        _, vjp = jax.vjp(_silu, c_ref[...])
        (o_ref[...],) = vjp(d_ref[...])

    return pl.pallas_call(body, name=name, out_shape=jax.ShapeDtypeStruct(c.shape, F32))(c, d)


def _adamw(w, g, m, v, *, name):
    rows, cols = w.shape[-2:]
    t = _tile(rows if g.ndim == 2 else g.shape[1])
    c1 = 1.0 / (1.0 - ADAM_B1 ** ADAM_STEP)
    c2 = 1.0 / (1.0 - ADAM_B2 ** ADAM_STEP)

    def body(w_ref, g_ref, m_ref, v_ref, go_ref, d_ref, mo_ref, vo_ref):
        blk = lambda r: r[...].reshape(t, cols)
        gv = blk(g_ref)
        mn = ADAM_B1 * blk(m_ref) + (1.0 - ADAM_B1) * gv
        vn = ADAM_B2 * blk(v_ref) + (1.0 - ADAM_B2) * (gv * gv)
        delta = -ADAM_LR * ((mn * c1) / (jnp.sqrt(vn * c2) + ADAM_EPS) + ADAM_WD * blk(w_ref))
        for ref, val in ((go_ref, gv), (d_ref, delta), (mo_ref, mn), (vo_ref, vn)):
            ref[...] = val.reshape(ref.shape)

    tile = (pl.BlockSpec((1, t, cols), lambda i: (0, i, 0)) if w.ndim == 3 else pl.BlockSpec((t, cols), lambda i: (i, 0)))
    if g.ndim == 3:
        per = g.shape[1] // t
        g_tile = pl.BlockSpec((1, t, cols), lambda i: (i // per, i % per, 0))
    else:
        g_tile = pl.BlockSpec((t, cols), lambda i: (i, 0))
    return pl.pallas_call(
        body, name=name, grid=(rows // t,),
        in_specs=[tile, g_tile, tile, tile], out_specs=[tile] * 4,
        out_shape=[jax.ShapeDtypeStruct(w.shape, F32)] * 4,
        compiler_params=_params(("parallel",)),
    )(w, g, m, v)


def _add_sibling(full, got, where, out_dtype, *, name):
    s, rows, cols = full.shape
    hr = rows // 2
    t = _tile(hr)
    nt = hr // t

    def body(w_ref, a_ref, b_ref, o_ref):
        o_ref[...] = (a_ref[...].astype(F32) + b_ref[...].astype(F32)).astype(out_dtype)

    tile = pl.BlockSpec((1, t, cols), lambda j, i, w: (j, i, 0))
    return pl.pallas_call(
        body, name=name,
        grid_spec=pltpu.PrefetchScalarGridSpec(
            num_scalar_prefetch=1, grid=(s, nt),
            in_specs=[pl.BlockSpec((1, t, cols), lambda j, i, w: (j, w[0] * nt + i, 0)), tile], out_specs=tile),
        out_shape=jax.ShapeDtypeStruct((s, hr, cols), out_dtype), compiler_params=_params(("parallel", "parallel")),
    )(where, full, got)


def _sum_chips(own, slots, where, n_out, which, *, name):
    _, hr, cols = own.shape
    t = _tile(hr)

    def body(w_ref, a_ref, s_ref, o_ref):
        f = lambda v: v.astype(F32)
        o_ref[0] = ((f(a_ref[0]) + f(s_ref[0])) + f(s_ref[1])) + f(s_ref[2])

    return pl.pallas_call(
        body, name=name,
        grid_spec=pltpu.PrefetchScalarGridSpec(
            num_scalar_prefetch=1, grid=(hr // t,),
            in_specs=[pl.BlockSpec((1, t, cols), lambda i, w: (w[1], i, 0)),
                      pl.BlockSpec((N_CHIPS - 1, t, cols), lambda i, w: (0, i, 0))],
            out_specs=pl.BlockSpec((1, t, cols), lambda i, w: (w[which], i, 0))),
        out_shape=jax.ShapeDtypeStruct((n_out, hr, cols), F32), compiler_params=_params(("parallel",)),
    )(where, own, slots)


def _local_step(x, c, ctx, c_ctx, tgt, w_mod, b_mod, g_pre, g_post, w_qkv, w_ab, w_rest, w_conv, a_log, dt_bias,
                g_onorm, gm_ln_g, gm_ln_b, w_sp, b_sp, w_pa, w_pb, w_out, on_large_gradients=None):
    rows, rows_c = x.shape[0], ctx.shape[0]
    cc = jnp.zeros((16, D), F32).at[0].set(c[0]).at[1].set(c_ctx)
    scc = _silu_rows(cc, name="silu_cond")
    mod = _mm(scc, w_mod, b_sections=True, name="mod_fwd")
    mod_x, mod_c = mod[0:1], mod[1:2]
    avec = jnp.zeros((1, DH), F32).at[0, :2 * H].set(a_log.reshape(-1))
    dvec = jnp.zeros((1, DH), F32).at[0, :2 * H].set(dt_bias.reshape(-1))
    bspb = jnp.broadcast_to(b_sp[:, :, None], (H, GM, GM))
    w_kv, wconv_kv = w_qkv[:, D:], w_conv[:, D:]

    h_c = _norm_fwd(ctx, g_pre, mod_c, b_mod, name="norm_fwd_ctx")
    pc_kv = _mm(h_c, w_kv, name="inproj_ctx_kv")
    pc_ab = _mm(h_c, w_ab, name="inproj_ctx_ab")
    kv_c = _prep_fwd(pc_kv, wconv_kv, H, name="prep_fwd_ctx")
    gb_c = _gates_fwd(pc_ab, avec, dvec, name="gates_fwd_ctx")
    s_zero = jnp.zeros((H, DH, DH), F32)
    _, _, ssf_c, ssb_c, tsf_c, tsb_c, s_f, s_b = _gdn_fwd(kv_c, gb_c, s_zero, s_zero, False, name="gdn_fwd_ctx")

    h_x = _norm_fwd(x, g_pre, mod_x, b_mod, name="norm_fwd_x")
    p_qkv = _mm(h_x, w_qkv, name="inproj_qkv")
    p_ab = _mm(h_x, w_ab, name="inproj_ab")
    p_rest = _mm(h_x, w_rest, out_dtype=BF16, name="inproj_rest")
    qkv = _prep_fwd(p_qkv, w_conv, 2 * H, name="prep_fwd_x")
    gb_x = _gates_fwd(p_ab, avec, dvec, name="gates_fwd_x")
    o_f, o_b, ssf, ssb, tsf, tsb, _, _ = _gdn_fwd(qkv, gb_x, s_f, s_b, True, name="gdn_fwd_x")

    (loss_acc, dp_rest, do, dx_res, ya, yb, mg, dma, dmb, dout, dvec_post, dgon, dwsp, dbspb) = _post(
        p_rest, o_f, o_b, x, tgt, mod_x, b_mod, g_onorm, gm_ln_g, gm_ln_b, w_sp, bspb, w_pa, w_pb, w_out, g_post,
        name="post")
    g = {}
    g["w_pa"] = _mm(ya, dma, ta=True, name="dw_pa")
    g["w_pb"] = _mm(yb, dmb, ta=True, name="dw_pb")
    g["w_out"] = _mm(mg, dout, ta=True, name="dw_out")

    zeros_s = jnp.zeros((H, DH, DH), F32)
    dq_f, dq_b, dg_f, dg_b, ds0_f, ds0_b = _gdn_bwd(qkv, gb_x, ssf, ssb, tsf, tsb, do, zeros_s, zeros_s, True,
                                                    name="gdn_bwd_x")
    dp_qkv, dwc_x = _prep_bwd(p_qkv, w_conv, dq_f, dq_b, 2 * H, name="prep_bwd_x")
    dp_ab, dav_x, ddv_x = _gates_bwd(p_ab, avec, dvec, dg_f, dg_b, name="gates_bwd_x")
    dkv_f, dkv_b, dgc_f, dgc_b, _, _ = _gdn_bwd(kv_c, gb_c, ssf_c, ssb_c, tsf_c, tsb_c, jnp.zeros((H, rows_c, DH), F32),
                                                 ds0_f, ds0_b, False, name="gdn_bwd_ctx")
    dpc_kv, dwc_c = _prep_bwd(pc_kv, wconv_kv, dkv_f, dkv_b, H, name="prep_bwd_ctx")
    dpc_ab, dav_c, ddv_c = _gates_bwd(pc_ab, avec, dvec, dgc_f, dgc_b, name="gates_bwd_ctx")

    dw_kv_c = _mm(h_c, dpc_kv, ta=True, name="dw_kv_ctx")
    dw_qkv = _mm(h_x, dp_qkv, ta=True, tn=D, add=dw_kv_c, add_from=1, name="dw_qkv")
    dw_ab_c = _mm(h_c, dpc_ab, ta=True, name="dw_ab_ctx")
    dw_ab = _mm(h_x, dp_ab, ta=True, add=dw_ab_c, name="dw_ab")
    dw_rest = _mm(h_x, dp_rest, ta=True, name="dw_rest")
    g["w_in"] = (dw_qkv, dw_ab[:, :4 * H], dw_rest)
    token = on_large_gradients(g) if on_large_gradients is not None else None
    dh_parts = [_mm(dp_qkv, w_qkv, tb=True, tk=3 * D // 2, after=token, name="dh_qkv"),
                _mm(dp_rest, w_rest, tb=True, tk=3 * D // 2, after=token, name="dh_rest"),
                _mm(dp_ab, w_ab, tb=True, after=token, name="dh_ab")]
    grad_x, dgpre_x, dm_x = _norm_bwd(x, g_pre, mod_x, b_mod, dh_parts, dx_res, name="norm_bwd_x")
    dhc_parts = [_mm(dpc_kv, w_kv, tb=True, after=token, name="dhc_kv"),
                 _mm(dpc_ab, w_ab, tb=True, after=token, name="dhc_ab")]
    _, dgpre_c, dm_c = _norm_bwd(ctx, g_pre, mod_c, b_mod, dhc_parts, None, name="norm_bwd_ctx")

    dm_x = dm_x.at[:, 2 * D:].add(dvec_post[1:2])
    dmod = jnp.zeros((16, 3 * D), F32).at[0].set(dm_x[0]).at[1].set(dm_c[0])
    g["mod_factors"] = (scc[0], dm_x[0])
    g["dm_ctx"] = dm_c[0]
    g["silu_c_ctx"] = scc[1:2]
    dcc = _mm(dmod, w_mod, tb=True, b_sections=True, name="dcc")
    g["c_ctx"] = _dsilu_mul(cc[:8], dcc[:8], name="dc_ctx")[1]
    g["b_mod"] = dm_x + dm_c
    g["g_pre"] = dgpre_x + dgpre_c
    g["g_post"] = dvec_post[0:1]
    g["gm_ln_g"], g["gm_ln_b"] = dvec_post[2:3], dvec_post[3:4]
    g["g_onorm"] = dgon[0:1]
    g["w_sp"] = dwsp
    g["b_sp"] = jnp.sum(dbspb, axis=-1)
    g["w_conv"] = dwc_x.at[:, D:].add(dwc_c)
    g["a_log"] = (dav_x + dav_c)[0, :2 * H].reshape(2, H)
    g["dt_bias"] = (ddv_x + ddv_c)[0, :2 * H].reshape(2, H)
    return loss_acc[0, 0], grad_x, g


ANY = pl.BlockSpec(memory_space=pl.ANY)


def _place():
    x, y, c = lax.axis_index("x"), lax.axis_index("y"), lax.axis_index("c")
    chips = [(1 - x, y), (x, 1 - y), (1 - x, 1 - y)]
    return x, y, c, (x, y, 1 - c), chips


def _gather_shards(big, small, *, name):
    nb = len(big)

    def body(*refs):
        ins, sm_in = refs[:nb], refs[nb]
        outs, sm_out = refs[nb + 1:2 * nb + 1], refs[2 * nb + 1]
        send, recv = refs[2 * nb + 2:]
        x, y, c, sibling, chips = _place()
        mine = 2 * x + y

        def half(a, shard, hc):
            hr = big[a].shape[0] // 2
            return outs[a].at[shard, pl.ds(hc * hr, hr), :]

        def remote(k, src, dst, to):
            return pltpu.make_async_remote_copy(src_ref=src, dst_ref=dst, send_sem=send.at[k], recv_sem=recv.at[k],
                                                device_id=to, device_id_type=MESH)

        sends = []
        for a in range(nb):
            hr = big[a].shape[0] // 2
            for j, chip in enumerate(chips):
                sends.append(remote(a * 3 + j, ins[a].at[pl.ds(c * hr, hr), :], half(a, mine, c), (*chip, c)))
        for j, chip in enumerate(chips):
            sends.append(remote(nb * 3 + j, sm_in, sm_out.at[mine], (*chip, c)))
        for cp in sends:
            cp.start()
        base = nb * 3 + 3
        passed = []
        for a in range(nb):
            for j, (px, py) in enumerate(chips):
                theirs = 2 * px + py
                remote(a * 3 + j, half(a, theirs, c), half(a, theirs, c), sibling).wait_recv()
                fw = remote(base + a * 3 + j, half(a, theirs, c), half(a, theirs, c), sibling)
                fw.start()
                passed.append(fw)
        for a in range(nb):
            for j, (px, py) in enumerate(chips):
                theirs = 2 * px + py
                remote(base + a * 3 + j, half(a, theirs, 1 - c), half(a, theirs, 1 - c), sibling).wait_recv()
        for j, (px, py) in enumerate(chips):
            remote(nb * 3 + j, sm_in, sm_out.at[2 * px + py], sibling).wait_recv()
        for cp in sends + passed:
            cp.wait_send()

    n_remote = 2 * nb * 3 + 3
    outs = pl.pallas_call(
        body, name=name, in_specs=[ANY] * (nb + 1), out_specs=[ANY] * (nb + 1),
        out_shape=[jax.ShapeDtypeStruct((N_CHIPS,) + a.shape, a.dtype) for a in big + [small]],
        scratch_shapes=[pltpu.SemaphoreType.DMA((n_remote,)), pltpu.SemaphoreType.DMA((n_remote,))],
    )(*big, small)
    return outs[:nb], outs[nb]


def _row_chunks(rows, row_bytes, align=8):
    n = max(1, min(rows // align, -(-rows * row_bytes // DMA_CHUNK_BYTES)))
    per = -(-(-(-rows // n)) // align) * align
    return [(r, min(per, rows - r)) for r in range(0, rows, per)]


def _remote(src, dst, send, recv, to):
    return pltpu.make_async_remote_copy(src_ref=src, dst_ref=dst, send_sem=send, recv_sem=recv, device_id=to,
                                        device_id_type=MESH)


def _exchange_halves(arrs, *, name):
    na = len(arrs)

    def body(*refs):
        ins, got = refs[:na], refs[na:2 * na]
        send, recv = refs[2 * na:]
        x, y, c, sibling, _ = _place()
        for a in range(na):
            ns, rows, cols = arrs[a].shape
            hr = rows // 2
            for s in range(ns):
                for r0, nr in _row_chunks(hr, cols * arrs[a].dtype.itemsize, 16):
                    _remote(ins[a].at[s, pl.ds((1 - c) * hr + r0, nr), :], got[a].at[s, pl.ds(r0, nr), :],
                            send.at[a], recv.at[a], sibling).start()
        for a in range(na):
            hr = arrs[a].shape[1] // 2
            _remote(ins[a].at[:, pl.ds((1 - c) * hr, hr), :], got[a], send.at[a], recv.at[a], sibling).wait()

    return pl.pallas_call(
        body, name=name, in_specs=[ANY] * na, out_specs=[ANY] * na,
        out_shape=[jax.ShapeDtypeStruct((N_CHIPS, a.shape[1] // 2, a.shape[2]), a.dtype) for a in arrs],
        scratch_shapes=[pltpu.SemaphoreType.DMA((na,)), pltpu.SemaphoreType.DMA((na,))],
    )(*arrs)


def _scatter_sections(arrs, *, name):
    na = len(arrs)

    def body(*refs):
        ins, outs = refs[:na], refs[na:2 * na]
        send, recv = refs[2 * na:]
        x, y, c, _, chips = _place()
        for a in range(na):
            _, hr, cols = arrs[a].shape
            for j, (px, py) in enumerate(chips):
                for r0, nr in _row_chunks(hr, cols * arrs[a].dtype.itemsize, 16):
                    _remote(ins[a].at[2 * px + py, pl.ds(r0, nr), :], outs[a].at[j, pl.ds(r0, nr), :],
                            send.at[a * 3 + j], recv.at[a * 3 + j], (px, py, c)).start()
        for a in range(na):
            for j, (px, py) in enumerate(chips):
                _remote(ins[a].at[2 * px + py], outs[a].at[j], send.at[a * 3 + j], recv.at[a * 3 + j], (px, py, c)).wait()

    return pl.pallas_call(
        body, name=name, in_specs=[ANY] * na, out_specs=[ANY] * na,
        out_shape=[jax.ShapeDtypeStruct((N_CHIPS - 1,) + a.shape[1:], a.dtype) for a in arrs],
        scratch_shapes=[pltpu.SemaphoreType.DMA((3 * na,)), pltpu.SemaphoreType.DMA((3 * na,))],
    )(*arrs)


HBM = pl.BlockSpec(memory_space=pltpu.HBM)
SEM = pl.BlockSpec(memory_space=pltpu.SEMAPHORE)


def _scatter_start(arrs, *, name):
    na = len(arrs)

    def body(*refs):
        ins, land = refs[:na], refs[na:2 * na]
        send, recv = refs[2 * na], refs[2 * na + 1]
        token = refs[-1]
        x, y, c, _, chips = _place()
        for a in range(na):
            _, hr, cols = arrs[a].shape
            for j, (px, py) in enumerate(chips):
                for r0, nr in _row_chunks(hr, cols * arrs[a].dtype.itemsize, 16):
                    _remote(ins[a].at[2 * px + py, pl.ds(r0, nr), :], land[a].at[j, pl.ds(r0, nr), :],
                            send.at[a * 3 + j], recv.at[a * 3 + j], (px, py, c)).start()
        token[...] = jnp.zeros_like(token)

    slots = [((N_CHIPS - 1,) + a.shape[1:], a.dtype) for a in arrs]
    held = lambda a: pltpu.with_memory_space_constraint(a, pltpu.HBM)
    outs = pl.pallas_call(
        body, name=name, in_specs=[HBM] * (2 * na),
        out_specs=[SEM, SEM] + [HBM] * (2 * na) + [pl.BlockSpec(memory_space=pltpu.VMEM)],
        out_shape=[pltpu.SemaphoreType.DMA((3 * na,)), pltpu.SemaphoreType.DMA((3 * na,))]
        + [pltpu.HBM(a.shape, a.dtype) for a in arrs] + [pltpu.HBM(s, d) for s, d in slots]
        + [jax.ShapeDtypeStruct((8, 128), F32)],
        input_output_aliases={i: 2 + i for i in range(2 * na)},
        compiler_params=pltpu.CompilerParams(has_side_effects=pltpu.SideEffectType.DATAFLOW_SIDE_EFFECTING),
    )(*[held(a) for a in arrs], *[held(lax.empty(s, d)) for s, d in slots])
    return outs[0], outs[1], outs[2:2 + na], outs[2 + na:2 + 2 * na], outs[-1]


def _scatter_wait(send, recv, thru, land, after, *, name):
    na = len(thru)

    def body(*refs):
        ins, slots = refs[:na], refs[na:2 * na]
        send_sem, recv_sem = refs[2 * na], refs[2 * na + 1]
        x, y, c, _, chips = _place()
        for a in range(na):
            for j, (px, py) in enumerate(chips):
                cp = _remote(ins[a].at[2 * px + py], slots[a].at[j], send_sem.at[a * 3 + j], recv_sem.at[a * 3 + j],
                             (px, py, c))
                cp.wait_send()
                cp.wait_recv()

    outs = pl.pallas_call(
        body, name=name, in_specs=[HBM] * (2 * na) + [SEM, SEM, ANY], out_specs=[HBM] * (2 * na),
        out_shape=[pltpu.HBM(a.shape, a.dtype) for a in list(thru) + list(land)],
        input_output_aliases={i: i for i in range(2 * na)},
        compiler_params=pltpu.CompilerParams(has_side_effects=pltpu.SideEffectType.DATAFLOW_SIDE_EFFECTING),
    )(*thru, *land, send, recv, after)
    return outs[:na], outs[na:]


def _finish_reduce(big, small, *, name):
    nb = len(big)

    def body(*refs):
        outs, sm = refs[nb + 1:2 * nb + 1], refs[2 * nb + 1]
        send, recv = refs[2 * nb + 2:]
        x, y, c, sibling, chips = _place()
        blk = lambda px, py, pc: sm.at[4 * px + 2 * py + pc]
        for a in range(nb):
            _, hr, cols = big[a].shape
            for r0, nr in _row_chunks(hr, cols * 4):
                _remote(outs[a].at[c, pl.ds(r0, nr), :], outs[a].at[c, pl.ds(r0, nr), :], send.at[a], recv.at[a],
                        sibling).start()
        first = [_remote(blk(x, y, c), blk(x, y, c), send.at[nb], recv.at[nb], sibling)]
        first += [_remote(blk(x, y, c), blk(x, y, c), send.at[nb + 1 + j], recv.at[nb + 1 + j], (*chip, c))
                  for j, chip in enumerate(chips)]
        for cp in first:
            cp.start()
        passed = []
        for j, (px, py) in enumerate(chips):
            _remote(blk(px, py, c), blk(px, py, c), send.at[nb + 1 + j], recv.at[nb + 1 + j], sibling).wait_recv()
            fw = _remote(blk(px, py, c), blk(px, py, c), send.at[nb + 4 + j], recv.at[nb + 4 + j], sibling)
            fw.start()
            passed.append(fw)
        for a in range(nb):
            _remote(outs[a].at[c], outs[a].at[1 - c], send.at[a], recv.at[a], sibling).wait()
        _remote(blk(x, y, c), blk(x, y, 1 - c), send.at[nb], recv.at[nb], sibling).wait_recv()
        for j, (px, py) in enumerate(chips):
            _remote(blk(px, py, c), blk(px, py, 1 - c), send.at[nb + 4 + j], recv.at[nb + 4 + j], sibling).wait_recv()
        for cp in first + passed:
            cp.wait_send()

    n_remote = nb + 7
    arrs = list(big) + [small]
    outs = pl.pallas_call(
        body, name=name, in_specs=[ANY] * (nb + 1), out_specs=[ANY] * (nb + 1),
        out_shape=[jax.ShapeDtypeStruct(a.shape, F32) for a in arrs],
        input_output_aliases={i: i for i in range(nb + 1)},
        scratch_shapes=[pltpu.SemaphoreType.DMA((n_remote,)), pltpu.SemaphoreType.DMA((n_remote,))],
    )(*arrs)
    return outs[:nb], outs[nb]


def _reduce_start(sectioned, where):
    got = _exchange_halves(sectioned, name="rs_exchange_halves")
    chip_sum = [_add_sibling(a, g, where, BF16, name=f"rs_add_sibling_{i}")
                for i, (a, g) in enumerate(zip(sectioned, got))]
    send, recv, thru, land, token = _scatter_start(chip_sum, name="rs_scatter_start")
    return (send, recv, thru, land), token


def _reduce_finish(handle, small, where, after):
    send, recv, thru, land = handle
    got = _exchange_halves([small], name="rs_exchange_small")
    small_sum = _add_sibling(small, got[0], where, F32, name="rs_add_sibling_small")
    small_slots = _scatter_sections([small_sum], name="rs_scatter_small")[0]
    own, slots = _scatter_wait(send, recv, thru, land, after, name="rs_scatter_wait")
    red = [_sum_chips(p, s, where, 2, 0, name=f"rs_sum_chips_{i}") for i, (p, s) in enumerate(zip(own, slots))]
    red_small = _sum_chips(small_sum, small_slots, where, 2 * N_CHIPS, 2, name="rs_sum_chips_small")
    big, sm = _finish_reduce(red, red_small, name="rs_finish")
    return big, sm.reshape(-1, sm.shape[-1])


def kernel(x, c, ctx, c_ctx, w_mod, b_mod, g_pre, g_post, w_in, w_conv, a_log, dt_bias, g_onorm, gm_ln_g, gm_ln_b, w_sp, b_sp, w_pa, w_pb, w_out, loss_target, m_c_ctx, m_w_mod, m_b_mod, m_g_pre, m_g_post, m_w_in, m_w_conv, m_a_log, m_dt_bias, m_g_onorm, m_gm_ln_g, m_gm_ln_b, m_w_sp, m_b_sp, m_w_pa, m_w_pb, m_w_out, v_c_ctx, v_w_mod, v_b_mod, v_g_pre, v_g_post, v_w_in, v_w_conv, v_a_log, v_dt_bias, v_g_onorm, v_gm_ln_g, v_gm_ln_b, v_w_sp, v_b_sp, v_w_pa, v_w_pb, v_w_out):
    names = ["c_ctx", "w_mod", "b_mod", "g_pre", "g_post", "w_in", "w_conv", "a_log", "dt_bias", "g_onorm", "gm_ln_g",
             "gm_ln_b", "w_sp", "b_sp", "w_pa", "w_pb", "w_out"]
    w = dict(zip(names, (c_ctx, w_mod, b_mod, g_pre, g_post, w_in, w_conv, a_log, dt_bias, g_onorm, gm_ln_g, gm_ln_b,
                         w_sp, b_sp, w_pa, w_pb, w_out)))
    m = dict(zip(names, (m_c_ctx, m_w_mod, m_b_mod, m_g_pre, m_g_post, m_w_in, m_w_conv, m_a_log, m_dt_bias, m_g_onorm,
                         m_gm_ln_g, m_gm_ln_b, m_w_sp, m_b_sp, m_w_pa, m_w_pb, m_w_out)))
    v = dict(zip(names, (v_c_ctx, v_w_mod, v_b_mod, v_g_pre, v_g_post, v_w_in, v_w_conv, v_a_log, v_dt_bias, v_g_onorm,
                         v_gm_ln_g, v_gm_ln_b, v_w_sp, v_b_sp, v_w_pa, v_w_pb, v_w_out)))
    xy = 2 * lax.axis_index("x") + lax.axis_index("y")
    where = jnp.stack([lax.axis_index("c"), xy, 2 * xy + lax.axis_index("c")]).astype(jnp.int32)

    shards = [a[0].astype(BF16) for a in (w_mod, w_in, w_pa, w_pb, w_out)]
    gathered, wconv_all = _gather_shards(shards, w_conv[0], name="gather_weights")
    own = lambda full, shard: lax.dynamic_update_slice(full, shard[None], (xy, 0, 0))
    wm_all, win_all, wpa_all, wpb_all, wout_all = [own(f, s) for f, s in zip(gathered, shards)]
    wconv_all = own(wconv_all, w_conv[0])
    w_conv_f = jnp.concatenate([wconv_all[s] for s in range(N_CHIPS)], axis=1)
    shard_cols = IN_COLS // N_CHIPS
    cut = OFF_A - shard_cols
    assert 0 < cut and cut + 4 * H < shard_cols
    w_qkv = jnp.concatenate([win_all[0], win_all[1][:, :cut]], axis=1)
    w_ab = jnp.pad(win_all[1][:, cut:cut + 4 * H], ((0, 0), (0, DH - 4 * H)))
    w_rest = jnp.concatenate([win_all[1][:, cut + 4 * H:], win_all[2], win_all[3]], axis=1)

    blk = D // N_CHIPS
    rest_cut = shard_cols - cut - 4 * H
    big_names = ("w_in", "w_pa", "w_pb", "w_out")
    in_flight = []

    def start_reduce(grads):
        dw_qkv, dw_ab, dw_rest = grads["w_in"]
        g_win = jnp.stack([dw_qkv[:, :shard_cols],
                           jnp.concatenate([dw_qkv[:, shard_cols:], dw_ab, dw_rest[:, :rest_cut]], axis=1),
                           dw_rest[:, rest_cut:rest_cut + shard_cols], dw_rest[:, rest_cut + shard_cols:]]).astype(BF16)
        sectioned = [g_win] + [grads[k].reshape(N_CHIPS, blk, D) for k in ("w_pa", "w_pb", "w_out")]
        handle, token = _reduce_start(sectioned, where)
        in_flight.append(handle)
        return token

    loss_local, grad_x, g = _local_step(
        x[0], c, ctx[0], c_ctx, loss_target[0], wm_all, b_mod, g_pre, g_post, w_qkv, w_ab, w_rest,
        w_conv_f, a_log[0], dt_bias[0], g_onorm, gm_ln_g, gm_ln_b, w_sp[0], b_sp[0],
        wpa_all.reshape(D, D), wpb_all.reshape(D, D), wout_all.reshape(D, D), on_large_gradients=start_reduce)
    g["loss"] = loss_local
    g["_pad"] = jnp.zeros((SMALL_LAYOUT["_pad"][1],), F32)
    g["mod_factors"] = lax.dynamic_update_slice(jnp.zeros((2 * N_CHIPS, 4 * D), F32),
                                                jnp.concatenate(g["mod_factors"])[None], (where[2], 0))
    tail = jnp.zeros((N_CHIPS * SMALL_ROWS * 128 - sum(s for _, s in SMALL_LAYOUT.values()),), F32)
    flat = jnp.concatenate([g[k].reshape(-1) for k in SMALL_LAYOUT] + [tail])
    reduced, gr_small = _reduce_finish(in_flight[0], flat.reshape(N_CHIPS, SMALL_ROWS, 128), where, g["b_mod"])
    gr_tiny = gr_small[TINY_ROW0:]

    def entry(arr, k):
        off, size = SMALL_LAYOUT[k]
        row, col = off // 128 - TINY_ROW0, off % 128
        return arr[row:row + size // 128].reshape(-1) if size >= 128 else arr[row, col:col + size]

    factors = entry(gr_tiny, "mod_factors").reshape(2 * N_CHIPS, 4 * D)
    pad_rows = lambda a: jnp.pad(a, ((0, 16 - a.shape[0]), (0, 0)))
    lhs = pad_rows(jnp.concatenate([factors[:, :D], g["silu_c_ctx"]], axis=0))
    rhs = pad_rows(jnp.concatenate([factors[:, D:], entry(gr_tiny, "dm_ctx")[None]], axis=0))
    mod_cols = 3 * D // N_CHIPS
    gr_wm = _mm(lhs, lax.dynamic_slice(rhs, (0, xy * mod_cols), (16, mod_cols)), ta=True, name="dw_mod")

    res = {k: _adamw(w[k], gr, m[k], v[k], name=f"adamw_{k}") for k, gr in zip(big_names, reduced)}
    res["w_mod"] = _adamw(w_mod[0], gr_wm, m_w_mod[0], v_w_mod[0], name="adamw_w_mod")
    res["w_sp"] = _adamw(w_sp.reshape(-1, 128), gr_small, m_w_sp.reshape(-1, 128), v_w_sp.reshape(-1, 128),
                         name="adamw_w_sp")
    tiny_names = [k for k in SMALL_LAYOUT if k not in ("w_sp", "w_conv", "dm_ctx", "mod_factors", "loss", "_pad")]

    def pack(src):
        parts = [src[k].reshape(-1) if k in tiny_names else jnp.zeros((s,), F32)
                 for k, (_, s) in SMALL_LAYOUT.items() if k != "w_sp"]
        return jnp.concatenate(parts + [tail]).reshape(-1, 128)

    tiny_res = _adamw(pack(w), gr_tiny, pack(m), pack(v), name="adamw_tiny")
    for k in tiny_names:
        res[k] = [entry(r, k) for r in tiny_res]
    g_conv = lax.dynamic_slice(entry(gr_tiny, "w_conv").reshape(3, 3 * D), (0, xy * (3 * D // N_CHIPS)),
                               (3, 3 * D // N_CHIPS))
    conv_res = _adamw(jnp.pad(w_conv[0], ((0, 5), (0, 0))), jnp.pad(g_conv, ((0, 5), (0, 0))),
                      jnp.pad(m_w_conv[0], ((0, 5), (0, 0))), jnp.pad(v_w_conv[0], ((0, 5), (0, 0))), name="adamw_w_conv")
    res["w_conv"] = [r[:3] for r in conv_res]
    res = {k: [r.reshape(w[k].shape) for r in res[k]] for k in names}

    out = [entry(gr_tiny, "loss").reshape(()), grad_x[None]]
    for i in range(4):
        out += [res[k][i] for k in names]
    return tuple(out)
```

```python
import functools

import jax
import jax.numpy as jnp
from jax import lax
from jax.experimental import pallas as pl
from jax.experimental.pallas import tpu as pltpu

F32 = jnp.float32
BF16 = jnp.bfloat16
HI = lax.Precision.HIGHEST
MESH = pl.DeviceIdType.MESH

D = 1024
H = 8
DH = 128
CH = 64
LOG_CH = 6
PAIR = 2 * CH
GM = 128
assert 1 << LOG_CH == CH and PAIR == DH
PREC_POWERS = ((lax.Precision.HIGH, lax.Precision.HIGH),) * 3 + ((None, None),) * 2
assert len(PREC_POWERS) == LOG_CH - 1
HEADS_PER_ITER_FWD = 8
HEADS_PER_ITER_BWD = 8
EPS = 1e-6
N_CHIPS = 4
OFF_A = 3 * D
OFF_ZB = OFF_A + 4 * H
IN_COLS = OFF_ZB + 6 * D
VMEM_LIMIT_V7X = 56 * 1024 * 1024
DMA_CHUNK_BYTES = 2 * 1024 * 1024

ADAM_LR, ADAM_B1, ADAM_B2, ADAM_EPS, ADAM_WD, ADAM_STEP = 0.001, 0.9, 0.999, 1e-08, 0.01, 10

SMALL_LAYOUT = {}
_off = 0
for _n, _s in (("w_sp", H * GM * GM), ("w_conv", 3 * 3 * D), ("c_ctx", D), ("b_mod", 3 * D), ("g_pre", D), ("g_post", D),
               ("gm_ln_g", D), ("gm_ln_b", D), ("b_sp", H * GM), ("dm_ctx", 3 * D), ("mod_factors", 8 * 4 * D),
               ("g_onorm", DH), ("a_log", 2 * H), ("dt_bias", 2 * H), ("loss", 1), ("_pad", 128 - 4 * H - 1)):
    SMALL_LAYOUT[_n] = (_off, _s)
    _off += _s
SMALL_ROWS = 368
assert N_CHIPS * SMALL_ROWS * 128 >= _off and (SMALL_ROWS // 2) % 8 == 0 and _off % 128 == 0
TINY_ROW0 = SMALL_LAYOUT["w_conv"][0] // 128


def _params(sem=None):
    return pltpu.CompilerParams(dimension_semantics=sem, vmem_limit_bytes=VMEM_LIMIT_V7X)


def _tile(n, cands=(256, 128, 64, 32, 16, 8)):
    if n <= cands[0]:
        return n
    for cand in cands:
        if n % cand == 0 and cand >= 64:
            return cand
    return max(d for d in range(8, cands[0], 8) if n % d == 0)


def _silu(x):
    return x * jax.nn.sigmoid(x)


def _gelu(x):
    return 0.5 * x * (1.0 + jnp.tanh(0.7978845608028654 * (x + 0.044715 * (x * x * x))))


def _mm(a, b, *, ta=False, tb=False, out_dtype=F32, tm=1024, tn=1024, tk=1024, add=None, add_from=0,
        b_sections=False, out_sections=False, after=None, name):
    m, k = (a.shape[1], a.shape[0]) if ta else a.shape
    if b_sections:
        sect = b.shape[2]
        n = b.shape[1] if tb else b.shape[0] * sect
        tn, tk = (tn, sect) if tb else (sect, tk)
    else:
        n = b.shape[0] if tb else b.shape[1]
    tm, tn, tk = min(tm, m), min(tn, n), min(tk, k)
    assert m % tm == 0 and n % tn == 0 and k % tk == 0, (name, m, n, k, tm, tn, tk)
    nk = k // tk
    dims = (((0,) if ta else (1,), (1,) if tb else (0,)), ((), ()))
    has_add = add is not None
    assert not has_add or add.shape == (m, n - add_from * tn), (name, add.shape)

    def body(*refs):
        a_ref, b_ref = refs[:2]
        o_ref = refs[2 + int(has_add) + int(after is not None)]
        acc_ref = refs[-1]
        kk = pl.program_id(2)
        bv = b_ref[0] if b_sections else b_ref[...]
        part = lax.dot_general(a_ref[...].astype(BF16), bv.astype(BF16), dims, preferred_element_type=F32)

        def finish(res):
            if has_add:
                res = res + jnp.where(pl.program_id(1) >= add_from, refs[2][...], 0.0)
            if out_sections:
                o_ref[0] = res.astype(out_dtype)
            else:
                o_ref[...] = res.astype(out_dtype)

        if nk == 1:
            finish(part)
            return

        @pl.when(kk == 0)
        def _():
            acc_ref[...] = part

        @pl.when((kk > 0) & (kk < nk - 1))
        def _():
            acc_ref[...] += part

        @pl.when(kk == nk - 1)
        def _():
            finish(acc_ref[...] + part)

    a_spec = pl.BlockSpec((tk, tm), lambda i, j, q: (q, i)) if ta else pl.BlockSpec((tm, tk), lambda i, j, q: (i, q))
    if b_sections:
        b_spec = (pl.BlockSpec((1, tn, tk), lambda i, j, q: (q, j, 0)) if tb
                  else pl.BlockSpec((1, tk, tn), lambda i, j, q: (j, q, 0)))
    else:
        b_spec = pl.BlockSpec((tn, tk), lambda i, j, q: (j, q)) if tb else pl.BlockSpec((tk, tn), lambda i, j, q: (q, j))
    add_spec = [pl.BlockSpec((tm, tn), lambda i, j, q: (i, jnp.maximum(j - add_from, 0)))] if has_add else []
    if out_sections:
        out_spec, out_shape = pl.BlockSpec((1, tm, tn), lambda i, j, q: (j, i, 0)), (n // tn, m, tn)
    else:
        out_spec, out_shape = pl.BlockSpec((tm, tn), lambda i, j, q: (i, j)), (m, n)
    return pl.pallas_call(
        body, name=name, grid=(m // tm, n // tn, nk),
        in_specs=[a_spec, b_spec] + add_spec + ([pl.BlockSpec(memory_space=pl.ANY)] if after is not None else []),
        out_specs=out_spec, out_shape=jax.ShapeDtypeStruct(out_shape, out_dtype),
        scratch_shapes=[pltpu.VMEM((tm, tn), F32)] if nk > 1 else [],
        compiler_params=_params(("parallel", "parallel", "arbitrary")),
    )(*([a, b] + ([add] if has_add else []) + ([after] if after is not None else [])))


def _h_fn(x, g, m):
    shift, scale = m[:, 0:D], m[:, D:2 * D]
    r = lax.rsqrt(jnp.mean(x * x, axis=-1, keepdims=True) + EPS)
    return (x * r * g) * (1.0 + scale) + shift


def _norm_fwd(x, g, mod, bmod, *, name):
    rows = x.shape[0]
    t = min(512, rows)

    def body(x_ref, g_ref, m_ref, b_ref, h_ref):
        h_ref[...] = _h_fn(x_ref[...], g_ref[...], m_ref[...] + b_ref[...]).astype(BF16)

    vec = lambda w: pl.BlockSpec((1, w), lambda i: (0, 0))
    return pl.pallas_call(
        body, name=name, grid=(rows // t,),
        in_specs=[pl.BlockSpec((t, D), lambda i: (i, 0)), vec(D), vec(3 * D), vec(3 * D)],
        out_specs=pl.BlockSpec((t, D), lambda i: (i, 0)),
        out_shape=jax.ShapeDtypeStruct((rows, D), BF16),
        compiler_params=_params(("parallel",)),
    )(x, g, mod, bmod)


def _norm_bwd(x, g, mod, bmod, dh_parts, resid, *, name):
    rows = x.shape[0]
    t = min(512, rows)
    n_parts = len(dh_parts)
    has_resid = resid is not None

    def body(*refs):
        x_ref, g_ref, m_ref, b_ref = refs[:4]
        parts = refs[4:4 + n_parts]
        r_ref = refs[4 + n_parts] if has_resid else None
        dx_ref, dg_ref, dm_ref = refs[-3:]
        i = pl.program_id(0)
        dh = parts[0][...]
        for p in parts[1:]:
            dh = dh + p[...]
        _, vjp = jax.vjp(_h_fn, x_ref[...], g_ref[...], m_ref[...] + b_ref[...])
        dx, dg, dm = vjp(dh)
        if has_resid:
            dx = dx + r_ref[...]
        dx_ref[...] = dx

        @pl.when(i == 0)
        def _():
            dg_ref[...] = dg
            dm_ref[...] = dm

        @pl.when(i > 0)
        def _():
            dg_ref[...] += dg
            dm_ref[...] += dm

    vec = lambda w: pl.BlockSpec((1, w), lambda i: (0, 0))
    tile = pl.BlockSpec((t, D), lambda i: (i, 0))
    ins = [x, g, mod, bmod, *dh_parts] + ([resid] if has_resid else [])
    return pl.pallas_call(
        body, name=name, grid=(rows // t,),
        in_specs=[tile, vec(D), vec(3 * D), vec(3 * D)] + [tile] * (n_parts + int(has_resid)),
        out_specs=[tile, vec(D), vec(3 * D)],
        out_shape=[jax.ShapeDtypeStruct((rows, D), F32), jax.ShapeDtypeStruct((1, D), F32),
                   jax.ShapeDtypeStruct((1, 3 * D), F32)],
        compiler_params=_params(("arbitrary",)),
    )(*ins)


def _conv_tile(u_ref, r0, t, rows, w0, w1, w2):
    u = u_ref[pl.ds(r0, t), :]
    prev8 = u_ref[pl.ds(pl.multiple_of(jnp.maximum(r0 - 8, 0), 8), 8), :]
    next8 = u_ref[pl.ds(pl.multiple_of(jnp.minimum(r0 + t, rows - 8), 8), 8), :]
    r8 = lax.broadcasted_iota(jnp.int32, (8, DH), 0)
    prev_row = jnp.sum(jnp.where(r8 == 7, prev8, 0.0), axis=0, keepdims=True)
    next_row = jnp.sum(jnp.where(r8 == 0, next8, 0.0), axis=0, keepdims=True)
    prev_row = jnp.where(r0 > 0, prev_row, 0.0)
    next_row = jnp.where(r0 + t < rows, next_row, 0.0)
    ri = lax.broadcasted_iota(jnp.int32, (t, DH), 0)
    um1 = jnp.where(ri == 0, prev_row, pltpu.roll(u, 1, 0))
    up1 = jnp.where(ri == t - 1, next_row, pltpu.roll(u, t - 1, 0))
    return w0 * um1 + w1 * u + w2 * up1, um1, u, up1


def _rowlocal(z, is_norm):
    y = _silu(z)
    yn = y * lax.rsqrt(jnp.sum(y * y, axis=-1, keepdims=True) + EPS)
    return jnp.where(is_norm, yn, y)


def _prep_fwd(p, wconv, n_norm, *, name):
    rows, nb = p.shape[0], p.shape[1] // DH
    t = min(512, rows)

    def body(u_ref, w_ref, o_ref):
        is_norm = pl.program_id(0) < n_norm
        w0, w1, w2 = w_ref[0:1, :], w_ref[1:2, :], w_ref[2:3, :]

        def step(s, carry):
            r0 = pl.multiple_of(s * t, t)
            z, _, _, _ = _conv_tile(u_ref, r0, t, rows, w0, w1, w2)
            o_ref[0, pl.ds(r0, t), :] = _rowlocal(z, is_norm)
            return carry

        lax.fori_loop(0, rows // t, step, 0)

    return pl.pallas_call(
        body, name=name, grid=(nb,),
        in_specs=[pl.BlockSpec((rows, DH), lambda j: (0, j)), pl.BlockSpec((3, DH), lambda j: (0, j))],
        out_specs=pl.BlockSpec((1, rows, DH), lambda j: (j, 0, 0)),
        out_shape=jax.ShapeDtypeStruct((nb, rows, DH), F32),
        compiler_params=_params(("parallel",)),
    )(p, wconv)


def _prep_bwd(p, wconv, d_a, d_b, n_norm, *, name):
    rows, nb = p.shape[0], p.shape[1] // DH
    t = min(512, rows)

    def body(u_ref, w_ref, da_ref, db_ref, du_ref, dw_ref, dz_ref):
        is_norm = pl.program_id(0) < n_norm
        w0, w1, w2 = w_ref[0:1, :], w_ref[1:2, :], w_ref[2:3, :]

        def step1(s, carry):
            a0, a1, a2 = carry
            r0 = pl.multiple_of(s * t, t)
            z, um1, u, up1 = _conv_tile(u_ref, r0, t, rows, w0, w1, w2)
            _, vjp = jax.vjp(lambda zz: _rowlocal(zz, is_norm), z)
            (dz,) = vjp(da_ref[0, pl.ds(r0, t), :] + db_ref[0, pl.ds(r0, t), :])
            dz_ref[pl.ds(r0, t), :] = dz
            red = lambda v: jnp.sum(v, axis=0, keepdims=True)
            return a0 + red(dz * um1), a1 + red(dz * u), a2 + red(dz * up1)

        zero = jnp.zeros((1, DH), F32)
        a0, a1, a2 = lax.fori_loop(0, rows // t, step1, (zero, zero, zero))
        dw_ref[0:1, :] = a0
        dw_ref[1:2, :] = a1
        dw_ref[2:3, :] = a2

        def step2(s, carry):
            r0 = pl.multiple_of(s * t, t)
            du, _, _, _ = _conv_tile(dz_ref, r0, t, rows, w2, w1, w0)
            du_ref[pl.ds(r0, t), :] = du.astype(BF16)
            return carry

        lax.fori_loop(0, rows // t, step2, 0)

    col = pl.BlockSpec((rows, DH), lambda j: (0, j))
    w_spec = pl.BlockSpec((3, DH), lambda j: (0, j))
    d_spec = pl.BlockSpec((1, rows, DH), lambda j: (j, 0, 0))
    return pl.pallas_call(
        body, name=name, grid=(nb,),
        in_specs=[col, w_spec, d_spec, d_spec], out_specs=[col, w_spec],
        out_shape=[jax.ShapeDtypeStruct((rows, nb * DH), BF16), jax.ShapeDtypeStruct((3, nb * DH), F32)],
        scratch_shapes=[pltpu.VMEM((rows, DH), F32)],
        compiler_params=_params(("parallel",)),
    )(p, wconv, d_a, d_b)


def _gates_fn(pab, avec, dvec):
    t = pab.shape[0]
    lane = lax.broadcasted_iota(jnp.int32, pab.shape, 1)
    xg = pab + dvec
    sp = jnp.maximum(xg, 0.0) + jnp.log(1.0 + jnp.exp(-jnp.abs(xg)))
    g = jnp.where(lane < 2 * H, -jnp.exp(avec) * sp, 0.0)
    ii = lax.broadcasted_iota(jnp.int32, (t, t), 0)
    jj = lax.broadcasted_iota(jnp.int32, (t, t), 1)
    same = (ii >> LOG_CH) == (jj >> LOG_CH)
    cum_f = _dot(jnp.where(same & (jj <= ii), 1.0, 0.0), g, precision=HI)
    cum_r = _dot(jnp.where(same & (jj >= ii), 1.0, 0.0), g, precision=HI)
    return jnp.where(lane < H, cum_f, jnp.where(lane < 2 * H, cum_r, jnp.where(lane < 4 * H, jax.nn.sigmoid(pab), 0.0)))


def _gates_fwd(pab, avec, dvec, *, name):
    rows = pab.shape[0]
    t = min(512, rows)

    def body(p_ref, a_ref, d_ref, o_ref):
        o_ref[...] = _gates_fn(p_ref[...], a_ref[...], d_ref[...])

    vec = pl.BlockSpec((1, DH), lambda i: (0, 0))
    tile = pl.BlockSpec((t, DH), lambda i: (i, 0))
    return pl.pallas_call(
        body, name=name, grid=(rows // t,), in_specs=[tile, vec, vec], out_specs=tile,
        out_shape=jax.ShapeDtypeStruct((rows, DH), F32), compiler_params=_params(("parallel",)),
    )(pab, avec, dvec)


def _gates_bwd(pab, avec, dvec, d_a, d_b, *, name):
    rows = pab.shape[0]
    t = min(512, rows)

    def body(p_ref, a_ref, d_ref, da_ref, db_ref, dp_ref, dav_ref, ddv_ref):
        i = pl.program_id(0)
        _, vjp = jax.vjp(_gates_fn, p_ref[...], a_ref[...], d_ref[...])
        dp, dav, ddv = vjp(da_ref[...] + db_ref[...])
        dp_ref[...] = dp.astype(BF16)

        @pl.when(i == 0)
        def _():
            dav_ref[...] = dav
            ddv_ref[...] = ddv

        @pl.when(i > 0)
        def _():
            dav_ref[...] += dav
            ddv_ref[...] += ddv

    vec = pl.BlockSpec((1, DH), lambda i: (0, 0))
    tile = pl.BlockSpec((t, DH), lambda i: (i, 0))
    return pl.pallas_call(
        body, name=name, grid=(rows // t,), in_specs=[tile, vec, vec, tile, tile], out_specs=[tile, vec, vec],
        out_shape=[jax.ShapeDtypeStruct((rows, DH), BF16), jax.ShapeDtypeStruct((1, DH), F32),
                   jax.ShapeDtypeStruct((1, DH), F32)],
        compiler_params=_params(("arbitrary",)),
    )(pab, avec, dvec, d_a, d_b)


def _dot_general(a, b, dims, precision=None):
    return lax.dot_general(a, b, (dims, ((), ())), precision=precision, preferred_element_type=F32)


@functools.partial(jax.custom_vjp, nondiff_argnums=(2,))
def _dot_bf16(a, b, dims):
    return _dot_general(a.astype(BF16), b.astype(BF16), dims)


def _dot_bf16_fwd(a, b, dims):
    return _dot_bf16(a, b, dims), (a, b)


def _dot_bf16_bwd(dims, res, g):
    a, b = res
    (ca,), (cb,) = dims
    da = _dot_bf16(g, b, ((1,), (1 - cb,))) if ca == 1 else _dot_bf16(b, g, ((1 - cb,), (1,)))
    db = _dot_bf16(a, g, ((1 - ca,), (0,))) if cb == 0 else _dot_bf16(g, a, ((0,), (1 - ca,)))
    return da, db


_dot_bf16.defvjp(_dot_bf16_fwd, _dot_bf16_bwd)


def _dot(a, b, dims=((1,), (0,)), precision=None):
    if precision is None and a.dtype == F32 and b.dtype == F32:
        return _dot_bf16(a, b, dims)
    return _dot_general(a, b, dims, precision)


_NT = ((1,), (1,))
_TN = ((0,), (0,))


@jax.custom_vjp
def _saved_inverse(neg_a, inv):
    return inv


def _saved_inverse_fwd(neg_a, inv):
    return inv, inv


def _saved_inverse_bwd(inv, d_inv):
    idx = range(len(inv))
    left = [_dot(inv[i], d_inv[i], _TN) for i in idx]
    d_neg_a = [_dot(left[i], inv[i], _NT) for i in idx]
    return d_neg_a, [jnp.zeros_like(t) for t in inv]


_saved_inverse.defvjp(_saved_inverse_fwd, _saved_inverse_bwd)


def _pairs(s, q, k, v, gcol, bcol, revs, inv_saved=None):
    idx = range(len(revs))
    ii = lax.broadcasted_iota(jnp.int32, (PAIR, PAIR), 0)
    jj = lax.broadcasted_iota(jnp.int32, (PAIR, PAIR), 1)
    same = (ii >> LOG_CH) == (jj >> LOG_CH)
    incl_d = (same & (ii >= jj), same & (ii <= jj))
    strict_d = (same & (ii > jj), same & (ii < jj))
    incl = [incl_d[int(r)] for r in revs]
    strict = [strict_d[int(r)] for r in revs]
    eye = jnp.where(ii == jj, 1.0, 0.0)
    gc_i = [jnp.broadcast_to(gcol[i], (PAIR, DH)) for i in idx]
    gc_j = [gc_i[i].T for i in idx]
    decay = [jnp.where(incl[i], jnp.exp(jnp.where(incl[i], gc_i[i] - gc_j[i], 0.0)), 0.0) for i in idx]
    b_b = [jnp.broadcast_to(bcol[i], (PAIR, DH)) for i in idx]
    kb = [k[i] * b_b[i] for i in idx]
    kk = [_dot(kb[i], k[i], _NT) for i in idx]
    bp = [jnp.where(strict[i], -kk[i] * decay[i], 0.0) for i in idx]
    if inv_saved is not None:
        inv = _saved_inverse(bp, inv_saved)
    else:
        low = bp
        for prec_sq, prec_acc in PREC_POWERS:
            bp = [_dot(bp[i], bp[i], precision=prec_sq) for i in idx]
            more = [_dot(low[i], bp[i], precision=prec_acc) for i in idx]
            low = [low[i] + bp[i] + more[i] for i in idx]
        inv = [eye + low[i] for i in idx]
    eg = [jnp.exp(gc_i[i]) for i in idx]
    sol = [_dot(inv[i], jnp.concatenate([v[i] * b_b[i], kb[i] * eg[i]], axis=1)) for i in idx]
    u_val = [sol[i][:, :DH] for i in idx]
    w_key = [sol[i][:, DH:] for i in idx]
    row = lax.broadcasted_iota(jnp.int32, (PAIR, 1), 0)
    has_q = q[0] is not None
    if has_q:
        qc = [q[i] * (DH ** -0.5) for i in idx]
        qk = [_dot(qc[i], k[i], _NT) for i in idx]
        attn = [qk[i] * decay[i] for i in idx]
        qd = [qc[i] * eg[i] for i in idx]
    outs = [[None, None] for _ in idx]
    zeros = jnp.zeros((CH, DH), F32)
    for step in range(2):
        cidx = [(1 - step) if revs[i] else step for i in idx]
        sl = [slice(c * CH, (c + 1) * CH) for c in cidx]
        last = [c * CH if revs[i] else c * CH + CH - 1 for i, c in zip(idx, cidx)]
        gl = [jnp.sum(jnp.where(row == last[i], gcol[i], 0.0), axis=0, keepdims=True) for i in idx]
        k_tail = [k[i][sl[i]] * jnp.exp(gl[i] - gc_i[i][sl[i]]) for i in idx]
        ws = [_dot(w_key[i][sl[i]], s[i]) for i in idx]
        v_new = [u_val[i][sl[i]] - ws[i] for i in idx]
        if has_q:
            v_pad = [jnp.concatenate([v_new[i], zeros] if cidx[i] == 0 else [zeros, v_new[i]], axis=0) for i in idx]
            o_state = [_dot(qd[i][sl[i]], s[i]) for i in idx]
            o_local = [_dot(attn[i][sl[i]], v_pad[i]) for i in idx]
            for i in idx:
                outs[i][cidx[i]] = o_state[i] + o_local[i]
        kv = [_dot(k_tail[i], v_new[i], _TN) for i in idx]
        s = [s[i] * jnp.exp(gl[i]) + kv[i] for i in idx]
    return s, ([jnp.concatenate(outs[i], axis=0) for i in idx] if has_q else None), inv


def _lane_col(tile, idx):
    lane = lax.broadcasted_iota(jnp.int32, tile.shape, 1)
    return jnp.sum(jnp.where(lane == idx, tile, 0.0), axis=1, keepdims=True)


def _gdn_fwd(qkv, gb, s0f, s0b, has_q, *, name):
    nb, rows, _ = qkv.shape
    n = rows // PAIR
    qoff = H if has_q else 0

    def body(qf_ref, qb_ref, gf_ref, gr_ref, s0f_ref, s0b_ref, of_ref, ob_ref, ssf_ref, ssb_ref, tsf_ref, tsb_ref,
             sf_ref, sb_ref):
        @pl.when(pl.program_id(0) == 0)
        def _():
            sf_ref[...] = s0f_ref[...]
            sb_ref[...] = s0b_ref[...]

        gtiles = (gf_ref[...], gr_ref[...])

        dirs = ((qf_ref, sf_ref, ssf_ref, of_ref), (qb_ref, sb_ref, ssb_ref, ob_ref))
        ts_refs = (tsf_ref, tsb_ref)

        def heads(hg, carry):
            work = [(hg * HEADS_PER_ITER_FWD + j, d) for j in range(HEADS_PER_ITER_FWD) for d in range(2)]
            loaded = []
            for h, d in work:
                q_ref, s_ref, _, _ = dirs[d]
                loaded.append((s_ref[h], q_ref[h] if has_q else None, q_ref[qoff + h], q_ref[qoff + H + h],
                               _lane_col(gtiles[d], d * H + h), _lane_col(gtiles[d], 2 * H + d * H + h)))
            s_new, o, inv = _pairs(*[list(col) for col in zip(*loaded)], revs=[d == 1 for _, d in work])
            for i, (h, d) in enumerate(work):
                _, s_ref, ss_ref, o_ref = dirs[d]
                ss_ref[0, h] = loaded[i][0]
                ts_refs[d][0, h] = inv[i]
                s_ref[h] = s_new[i]
                o_ref[h] = o[i] if has_q else jnp.zeros((PAIR, DH), F32)
            return carry

        if HEADS_PER_ITER_FWD == H:
            heads(0, 0)
        else:
            lax.fori_loop(0, H // HEADS_PER_ITER_FWD, heads, 0)

    fwd3 = lambda i: (0, i, 0)
    rev3 = lambda i: (0, n - 1 - i, 0)
    state = pl.BlockSpec((H, DH, DH), lambda i: (0, 0, 0))
    saved = pl.BlockSpec((1, H, DH, DH), lambda i: (i, 0, 0, 0))
    return pl.pallas_call(
        body, name=name, grid=(n,),
        in_specs=[pl.BlockSpec((nb, PAIR, DH), fwd3), pl.BlockSpec((nb, PAIR, DH), rev3),
                  pl.BlockSpec((PAIR, DH), lambda i: (i, 0)), pl.BlockSpec((PAIR, DH), lambda i: (n - 1 - i, 0)),
                  state, state],
        out_specs=[pl.BlockSpec((H, PAIR, DH), fwd3), pl.BlockSpec((H, PAIR, DH), rev3), saved, saved, saved, saved,
                   state, state],
        out_shape=[jax.ShapeDtypeStruct((H, rows, DH), F32)] * 2 + [jax.ShapeDtypeStruct((n, H, DH, DH), F32)] * 4
        + [jax.ShapeDtypeStruct((H, DH, DH), F32)] * 2,
        compiler_params=_params(("arbitrary",)),
    )(qkv, qkv, gb, gb, s0f, s0b)


def _gdn_bwd(qkv, gb, ssf, ssb, tsf, tsb, do, dsf, dsb, has_q, *, name):
    nb, rows, _ = qkv.shape
    n = rows // PAIR
    qoff = H if has_q else 0

    def body(qf_ref, qb_ref, gf_ref, gr_ref, ssf_ref, ssb_ref, tsf_ref, tsb_ref, dof_ref, dob_ref, dsf0_ref, dsb0_ref,
             dqf_ref, dqb_ref, dgf_ref, dgr_ref, dsf_ref, dsb_ref):
        ts_refs = (tsf_ref, tsb_ref)
        @pl.when(pl.program_id(0) == 0)
        def _():
            dsf_ref[...] = dsf0_ref[...]
            dsb_ref[...] = dsb0_ref[...]

        gtiles = (gf_ref[...], gr_ref[...])
        lane = lax.broadcasted_iota(jnp.int32, (PAIR, DH), 1)

        dirs = ((qf_ref, ssf_ref, dof_ref, dsf_ref, dqf_ref), (qb_ref, ssb_ref, dob_ref, dsb_ref, dqb_ref))

        def heads(hg, carry):
            out = list(carry)
            work = [(hg * HEADS_PER_ITER_BWD + j, d) for j in range(HEADS_PER_ITER_BWD) for d in range(2)]
            revs = [d == 1 for _, d in work]
            s_in, q_in, k_in, v_in, g_in, b_in, ds_out, do_out, inv_in = [], [], [], [], [], [], [], [], []
            for h, d in work:
                q_ref, ss_ref, do_ref, ds_ref, _ = dirs[d]
                s_in.append(ss_ref[0, h])
                inv_in.append(ts_refs[d][0, h])
                q_in.append(q_ref[h] if has_q else None)
                k_in.append(q_ref[qoff + h])
                v_in.append(q_ref[qoff + H + h])
                g_in.append(_lane_col(gtiles[d], d * H + h))
                b_in.append(_lane_col(gtiles[d], 2 * H + d * H + h))
                ds_out.append(ds_ref[h])
                do_out.append(do_ref[h] if has_q else None)
            if has_q:
                _, vjp = jax.vjp(lambda s_, q_, k_, v_, g_, b_: _pairs(s_, q_, k_, v_, g_, b_, revs, inv_in)[:2],
                                 s_in, q_in, k_in, v_in, g_in, b_in)
                ds, dq, dk, dv, dg, db = vjp((ds_out, do_out))
            else:
                _, vjp = jax.vjp(lambda s_, k_, v_, g_, b_: _pairs(s_, q_in, k_, v_, g_, b_, revs, inv_in)[0],
                                 s_in, k_in, v_in, g_in, b_in)
                ds, dk, dv, dg, db = vjp(ds_out)
            for i, (h, d) in enumerate(work):
                dq_ref, ds_ref = dirs[d][4], dirs[d][3]
                ds_ref[h] = ds[i]
                if has_q:
                    dq_ref[h] = dq[i]
                dq_ref[qoff + h], dq_ref[qoff + H + h] = dk[i], dv[i]
                out[d] = (out[d] + jnp.where(lane == d * H + h, dg[i], 0.0)
                          + jnp.where(lane == 2 * H + d * H + h, db[i], 0.0))
            return tuple(out)

        zero = jnp.zeros((PAIR, DH), F32)
        if HEADS_PER_ITER_BWD == H:
            dgf, dgr = heads(0, (zero, zero))
        else:
            dgf, dgr = lax.fori_loop(0, H // HEADS_PER_ITER_BWD, heads, (zero, zero))
        dgf_ref[...] = dgf
        dgr_ref[...] = dgr

    fwd3 = lambda i: (0, n - 1 - i, 0)
    rev3 = lambda i: (0, i, 0)
    state = pl.BlockSpec((H, DH, DH), lambda i: (0, 0, 0))
    saved = pl.BlockSpec((1, H, DH, DH), lambda i: (n - 1 - i, 0, 0, 0))
    gf_spec = pl.BlockSpec((PAIR, DH), lambda i: (n - 1 - i, 0))
    gr_spec = pl.BlockSpec((PAIR, DH), lambda i: (i, 0))
    return pl.pallas_call(
        body, name=name, grid=(n,),
        in_specs=[pl.BlockSpec((nb, PAIR, DH), fwd3), pl.BlockSpec((nb, PAIR, DH), rev3), gf_spec, gr_spec,
                  saved, saved, saved, saved,
                  pl.BlockSpec((H, PAIR, DH), fwd3), pl.BlockSpec((H, PAIR, DH), rev3), state, state],
        out_specs=[pl.BlockSpec((nb, PAIR, DH), fwd3), pl.BlockSpec((nb, PAIR, DH), rev3), gf_spec, gr_spec, state, state],
        out_shape=[jax.ShapeDtypeStruct((nb, rows, DH), F32)] * 2 + [jax.ShapeDtypeStruct((rows, DH), F32)] * 2
        + [jax.ShapeDtypeStruct((H, DH, DH), F32)] * 2,
        compiler_params=_params(("arbitrary",)),
    )(qkv, qkv, gb, gb, ssf, ssb, tsf, tsb, do, do, dsf, dsb)


def _stage1(zb, ua, va, za, o, gon, lng, lnb, wsp, bsp):
    gv = [_gelu(t) for t in va]
    mu = sum(jnp.sum(t, axis=-1, keepdims=True) for t in gv) * (1.0 / D)
    xc = [t - mu for t in gv]
    var = sum(jnp.sum(t * t, axis=-1, keepdims=True) for t in xc) * (1.0 / D)
    rs = lax.rsqrt(var + EPS)
    ya, yb = [], []
    for g in range(H):
        vv = xc[g] * rs * lng[g] + lnb[g]
        s = _dot(wsp[g], vv) + bsp[g]
        ya.append(_gelu(ua[g]) * s * _silu(za[g]))
        r = lax.rsqrt(jnp.mean(o[g] * o[g], axis=-1, keepdims=True) + EPS)
        yb.append(o[g] * r * gon * _silu(zb[g]))
    return ya, yb


def _stage2(ma, mb, ga, gb):
    return jax.nn.sigmoid(ga) * ma + jax.nn.sigmoid(gb) * mb


def _stage3(out, x, tgt, gpost, gate):
    r = out * lax.rsqrt(jnp.mean(out * out, axis=-1, keepdims=True) + EPS) * gpost
    err = x + gate * r - tgt
    return 0.5 * jnp.sum(jnp.mean(err * err, axis=-1, keepdims=True), axis=0, keepdims=True)


def _post(p_rest, o_f, o_b, x, tgt, mod, bmod, gon, lng, lnb, wsp, bspb, wpa, wpb, wout, gpost, *, name):
    rows = x.shape[0]
    n = rows // GM
    lanes = lambda g: slice(g * DH, (g + 1) * DH)
    bdot = lambda a, w_ref: _dot(a.astype(BF16), w_ref[...])
    bdot_t = lambda a, w_ref: _dot(a.astype(BF16), w_ref[...], _NT)

    def body(p_ref, of_ref, ob_ref, x_ref, t_ref, m_ref, bm_ref, gon_ref, lng_ref, lnb_ref, wsp_ref, bsp_ref,
             wpa_ref, wpb_ref, wout_ref, gp_ref,
             loss_ref, dp_ref, do_ref, dx_ref, ya_ref, yb_ref, mg_ref, dma_ref, dmb_ref, dout_ref,
             dvec_ref, dgon_ref, dwsp_ref, dbsp_ref):
        @pl.when(pl.program_id(0) == 0)
        def _():
            loss_ref[...] = jnp.zeros_like(loss_ref)
            dvec_ref[...] = jnp.zeros_like(dvec_ref)
            dgon_ref[...] = jnp.zeros_like(dgon_ref)
            dwsp_ref[...] = jnp.zeros_like(dwsp_ref)
            dbsp_ref[...] = jnp.zeros_like(dbsp_ref)

        piece = lambda blk: [p_ref[:, blk * D + g * DH: blk * D + (g + 1) * DH].astype(F32) for g in range(H)]
        zb, ua, va, za = piece(0), piece(1), piece(2), piece(3)
        o = [of_ref[g] + ob_ref[g] for g in range(H)]
        gon = gon_ref[...]
        lng = [lng_ref[:, lanes(g)] for g in range(H)]
        lnb = [lnb_ref[:, lanes(g)] for g in range(H)]
        wsp = [wsp_ref[g] for g in range(H)]
        bsp = [bsp_ref[g] for g in range(H)]
        (ya, yb), vjp1 = jax.vjp(_stage1, zb, ua, va, za, o, gon, lng, lnb, wsp, bsp)
        y_a, y_b = jnp.concatenate(ya, axis=1), jnp.concatenate(yb, axis=1)
        ma, mb = bdot(y_a, wpa_ref), bdot(y_b, wpb_ref)
        ga, gb = p_ref[:, 4 * D:5 * D].astype(F32), p_ref[:, 5 * D:6 * D].astype(F32)
        merged, vjp2 = jax.vjp(_stage2, ma, mb, ga, gb)
        out = bdot(merged, wout_ref)
        gate = m_ref[:, 2 * D:3 * D] + bm_ref[:, 2 * D:3 * D]
        loss, vjp3 = jax.vjp(_stage3, out, x_ref[...], t_ref[...], gp_ref[...], gate)
        loss_ref[...] += jnp.broadcast_to(loss, loss_ref.shape)

        dout, dx, _, dgpost, dgate = vjp3(jnp.ones((1, 1), F32))
        dx_ref[...] = dx
        dmerged = bdot_t(dout, wout_ref)
        dma, dmb, dga, dgb = vjp2(dmerged)
        dya, dyb = bdot_t(dma, wpa_ref), bdot_t(dmb, wpb_ref)
        dzb, dua, dva, dza, do, dgon, dlng, dlnb, dwsp, dbsp = vjp1(
            ([dya[:, lanes(g)] for g in range(H)], [dyb[:, lanes(g)] for g in range(H)]))

        for blk, dlist in enumerate((dzb, dua, dva, dza)):
            for g in range(H):
                dp_ref[:, blk * D + g * DH: blk * D + (g + 1) * DH] = dlist[g].astype(BF16)
        dp_ref[:, 4 * D:5 * D] = dga.astype(BF16)
        dp_ref[:, 5 * D:6 * D] = dgb.astype(BF16)
        for g in range(H):
            do_ref[g] = do[g]
            dwsp_ref[g] += dwsp[g]
            dbsp_ref[g] += dbsp[g]
            dvec_ref[2:3, lanes(g)] += dlng[g]
            dvec_ref[3:4, lanes(g)] += dlnb[g]
        dvec_ref[0:1, :] += dgpost
        dvec_ref[1:2, :] += dgate
        dgon_ref[0:1, :] += dgon
        ya_ref[...] = y_a.astype(BF16)
        yb_ref[...] = y_b.astype(BF16)
        mg_ref[...] = merged.astype(BF16)
        dma_ref[...] = dma.astype(BF16)
        dmb_ref[...] = dmb.astype(BF16)
        dout_ref[...] = dout.astype(BF16)

    row = lambda w: pl.BlockSpec((GM, w), lambda i: (i, 0))
    heads = pl.BlockSpec((H, GM, DH), lambda i: (0, i, 0))
    full = lambda shape: pl.BlockSpec(shape, lambda i: tuple(0 for _ in shape))
    sds = jax.ShapeDtypeStruct
    return pl.pallas_call(
        body, name=name, grid=(n,),
        in_specs=[row(6 * D), heads, heads, row(D), row(D), full((1, 3 * D)), full((1, 3 * D)), full((1, DH)),
                  full((1, D)), full((1, D)), full((H, GM, GM)), full((H, GM, GM)),
                  full((D, D)), full((D, D)), full((D, D)), full((1, D))],
        out_specs=[full((8, DH)), row(6 * D), heads, row(D)] + [row(D)] * 6
        + [full((8, D)), full((8, DH)), full((H, GM, GM)), full((H, GM, GM))],
        out_shape=[sds((8, DH), F32), sds((rows, 6 * D), BF16), sds((H, rows, DH), F32), sds((rows, D), F32)]
        + [sds((rows, D), BF16)] * 6 + [sds((8, D), F32), sds((8, DH), F32), sds((H, GM, GM), F32), sds((H, GM, GM), F32)],
        compiler_params=_params(("arbitrary",)),
    )(p_rest, o_f, o_b, x, tgt, mod, bmod, gon, lng, lnb, wsp, bspb, wpa, wpb, wout, gpost)


def _silu_rows(c, *, name):
    def body(c_ref, o_ref):
        o_ref[...] = _silu(c_ref[...])

    return pl.pallas_call(body, name=name, out_shape=jax.ShapeDtypeStruct(c.shape, F32))(c)


def _dsilu_mul(c, d, *, name):
    def body(c_ref, d_ref, o_ref):
        _, vjp = jax.vjp(_silu, c_ref[...])
        (o_ref[...],) = vjp(d_ref[...])

    return pl.pallas_call(body, name=name, out_shape=jax.ShapeDtypeStruct(c.shape, F32))(c, d)


def _adamw(w, g, m, v, *, name):
    rows, cols = w.shape[-2:]
    t = _tile(rows if g.ndim == 2 else g.shape[1])
    c1 = 1.0 / (1.0 - ADAM_B1 ** ADAM_STEP)
    c2 = 1.0 / (1.0 - ADAM_B2 ** ADAM_STEP)

    def body(w_ref, g_ref, m_ref, v_ref, go_ref, d_ref, mo_ref, vo_ref):
        blk = lambda r: r[...].reshape(t, cols)
        gv = blk(g_ref)
        mn = ADAM_B1 * blk(m_ref) + (1.0 - ADAM_B1) * gv
        vn = ADAM_B2 * blk(v_ref) + (1.0 - ADAM_B2) * (gv * gv)
        delta = -ADAM_LR * ((mn * c1) / (jnp.sqrt(vn * c2) + ADAM_EPS) + ADAM_WD * blk(w_ref))
        for ref, val in ((go_ref, gv), (d_ref, delta), (mo_ref, mn), (vo_ref, vn)):
            ref[...] = val.reshape(ref.shape)

    tile = (pl.BlockSpec((1, t, cols), lambda i: (0, i, 0)) if w.ndim == 3 else pl.BlockSpec((t, cols), lambda i: (i, 0)))
    if g.ndim == 3:
        per = g.shape[1] // t
        g_tile = pl.BlockSpec((1, t, cols), lambda i: (i // per, i % per, 0))
    else:
        g_tile = pl.BlockSpec((t, cols), lambda i: (i, 0))
    return pl.pallas_call(
        body, name=name, grid=(rows // t,),
        in_specs=[tile, g_tile, tile, tile], out_specs=[tile] * 4,
        out_shape=[jax.ShapeDtypeStruct(w.shape, F32)] * 4,
        compiler_params=_params(("parallel",)),
    )(w, g, m, v)


def _add_sibling(full, got, where, out_dtype, *, name):
    s, rows, cols = full.shape
    hr = rows // 2
    t = _tile(hr)
    nt = hr // t

    def body(w_ref, a_ref, b_ref, o_ref):
        o_ref[...] = (a_ref[...].astype(F32) + b_ref[...].astype(F32)).astype(out_dtype)

    tile = pl.BlockSpec((1, t, cols), lambda j, i, w: (j, i, 0))
    return pl.pallas_call(
        body, name=name,
        grid_spec=pltpu.PrefetchScalarGridSpec(
            num_scalar_prefetch=1, grid=(s, nt),
            in_specs=[pl.BlockSpec((1, t, cols), lambda j, i, w: (j, w[0] * nt + i, 0)), tile], out_specs=tile),
        out_shape=jax.ShapeDtypeStruct((s, hr, cols), out_dtype), compiler_params=_params(("parallel", "parallel")),
    )(where, full, got)


def _sum_chips(own, slots, where, n_out, which, *, name):
    _, hr, cols = own.shape
    t = _tile(hr)

    def body(w_ref, a_ref, s_ref, o_ref):
        f = lambda v: v.astype(F32)
        o_ref[0] = ((f(a_ref[0]) + f(s_ref[0])) + f(s_ref[1])) + f(s_ref[2])

    return pl.pallas_call(
        body, name=name,
        grid_spec=pltpu.PrefetchScalarGridSpec(
            num_scalar_prefetch=1, grid=(hr // t,),
            in_specs=[pl.BlockSpec((1, t, cols), lambda i, w: (w[1], i, 0)),
                      pl.BlockSpec((N_CHIPS - 1, t, cols), lambda i, w: (0, i, 0))],
            out_specs=pl.BlockSpec((1, t, cols), lambda i, w: (w[which], i, 0))),
        out_shape=jax.ShapeDtypeStruct((n_out, hr, cols), F32), compiler_params=_params(("parallel",)),
    )(where, own, slots)


def _local_step(x, c, ctx, c_ctx, tgt, w_mod, b_mod, g_pre, g_post, w_qkv, w_ab, w_rest, w_conv, a_log, dt_bias,
                g_onorm, gm_ln_g, gm_ln_b, w_sp, b_sp, late_weights, on_large_gradients=None, run_after=None):
    rows, rows_c = x.shape[0], ctx.shape[0]
    cc = jnp.zeros((16, D), F32).at[0].set(c[0]).at[1].set(c_ctx)
    scc = _silu_rows(cc, name="silu_cond")
    mod = _mm(scc, w_mod, b_sections=True, after=run_after, name="mod_fwd")
    mod_x, mod_c = mod[0:1], mod[1:2]
    avec = jnp.zeros((1, DH), F32).at[0, :2 * H].set(a_log.reshape(-1))
    dvec = jnp.zeros((1, DH), F32).at[0, :2 * H].set(dt_bias.reshape(-1))
    bspb = jnp.broadcast_to(b_sp[:, :, None], (H, GM, GM))
    w_kv, wconv_kv = w_qkv[:, D:], w_conv[:, D:]

    h_c = _norm_fwd(ctx, g_pre, mod_c, b_mod, name="norm_fwd_ctx")
    pc_kv = _mm(h_c, w_kv, name="inproj_ctx_kv")
    pc_ab = _mm(h_c, w_ab, name="inproj_ctx_ab")
    kv_c = _prep_fwd(pc_kv, wconv_kv, H, name="prep_fwd_ctx")
    gb_c = _gates_fwd(pc_ab, avec, dvec, name="gates_fwd_ctx")
    s_zero = jnp.zeros((H, DH, DH), F32)
    _, _, ssf_c, ssb_c, tsf_c, tsb_c, s_f, s_b = _gdn_fwd(kv_c, gb_c, s_zero, s_zero, False, name="gdn_fwd_ctx")

    h_x = _norm_fwd(x, g_pre, mod_x, b_mod, name="norm_fwd_x")
    p_qkv = _mm(h_x, w_qkv, name="inproj_qkv")
    p_ab = _mm(h_x, w_ab, name="inproj_ab")
    p_rest = _mm(h_x, w_rest, out_dtype=BF16, name="inproj_rest")
    qkv = _prep_fwd(p_qkv, w_conv, 2 * H, name="prep_fwd_x")
    gb_x = _gates_fwd(p_ab, avec, dvec, name="gates_fwd_x")
    o_f, o_b, ssf, ssb, tsf, tsb, _, _ = _gdn_fwd(qkv, gb_x, s_f, s_b, True, name="gdn_fwd_x")

    w_pa, w_pb, w_out = late_weights(o_f)
    (loss_acc, dp_rest, do, dx_res, ya, yb, mg, dma, dmb, dout, dvec_post, dgon, dwsp, dbspb) = _post(
        p_rest, o_f, o_b, x, tgt, mod_x, b_mod, g_onorm, gm_ln_g, gm_ln_b, w_sp, bspb, w_pa, w_pb, w_out, g_post,
        name="post")
    g = {}
    g["w_pa"] = _mm(ya, dma, ta=True, name="dw_pa")
    g["w_pb"] = _mm(yb, dmb, ta=True, name="dw_pb")
    g["w_out"] = _mm(mg, dout, ta=True, name="dw_out")

    zeros_s = jnp.zeros((H, DH, DH), F32)
    dq_f, dq_b, dg_f, dg_b, ds0_f, ds0_b = _gdn_bwd(qkv, gb_x, ssf, ssb, tsf, tsb, do, zeros_s, zeros_s, True,
                                                    name="gdn_bwd_x")
    dp_qkv, dwc_x = _prep_bwd(p_qkv, w_conv, dq_f, dq_b, 2 * H, name="prep_bwd_x")
    dp_ab, dav_x, ddv_x = _gates_bwd(p_ab, avec, dvec, dg_f, dg_b, name="gates_bwd_x")
    dkv_f, dkv_b, dgc_f, dgc_b, _, _ = _gdn_bwd(kv_c, gb_c, ssf_c, ssb_c, tsf_c, tsb_c, jnp.zeros((H, rows_c, DH), F32),
                                                 ds0_f, ds0_b, False, name="gdn_bwd_ctx")
    dpc_kv, dwc_c = _prep_bwd(pc_kv, wconv_kv, dkv_f, dkv_b, H, name="prep_bwd_ctx")
    dpc_ab, dav_c, ddv_c = _gates_bwd(pc_ab, avec, dvec, dgc_f, dgc_b, name="gates_bwd_ctx")

    dw_kv_c = _mm(h_c, dpc_kv, ta=True, name="dw_kv_ctx")
    dw_qkv = _mm(h_x, dp_qkv, ta=True, tn=D, add=dw_kv_c, add_from=1, name="dw_qkv")
    dw_ab_c = _mm(h_c, dpc_ab, ta=True, name="dw_ab_ctx")
    dw_ab = _mm(h_x, dp_ab, ta=True, add=dw_ab_c, name="dw_ab")
    dw_rest = _mm(h_x, dp_rest, ta=True, name="dw_rest")
    g["w_in"] = (dw_qkv, dw_ab[:, :4 * H], dw_rest)
    token = on_large_gradients(g) if on_large_gradients is not None else None
    dh_parts = [_mm(dp_qkv, w_qkv, tb=True, tk=3 * D // 2, after=token, name="dh_qkv"),
                _mm(dp_rest, w_rest, tb=True, tk=3 * D // 2, after=token, name="dh_rest"),
                _mm(dp_ab, w_ab, tb=True, after=token, name="dh_ab")]
    grad_x, dgpre_x, dm_x = _norm_bwd(x, g_pre, mod_x, b_mod, dh_parts, dx_res, name="norm_bwd_x")
    dhc_parts = [_mm(dpc_kv, w_kv, tb=True, after=token, name="dhc_kv"),
                 _mm(dpc_ab, w_ab, tb=True, after=token, name="dhc_ab")]
    _, dgpre_c, dm_c = _norm_bwd(ctx, g_pre, mod_c, b_mod, dhc_parts, None, name="norm_bwd_ctx")

    dm_x = dm_x.at[:, 2 * D:].add(dvec_post[1:2])
    dmod = jnp.zeros((16, 3 * D), F32).at[0].set(dm_x[0]).at[1].set(dm_c[0])
    g["mod_factors"] = (scc[0], dm_x[0])
    g["dm_ctx"] = dm_c[0]
    g["silu_c_ctx"] = scc[1:2]
    dcc = _mm(dmod, w_mod, tb=True, b_sections=True, name="dcc")
    g["c_ctx"] = _dsilu_mul(cc[:8], dcc[:8], name="dc_ctx")[1]
    g["b_mod"] = dm_x + dm_c
    g["g_pre"] = dgpre_x + dgpre_c
    g["g_post"] = dvec_post[0:1]
    g["gm_ln_g"], g["gm_ln_b"] = dvec_post[2:3], dvec_post[3:4]
    g["g_onorm"] = dgon[0:1]
    g["w_sp"] = dwsp
    g["b_sp"] = jnp.sum(dbspb, axis=-1)
    g["w_conv"] = dwc_x.at[:, D:].add(dwc_c)
    g["a_log"] = (dav_x + dav_c)[0, :2 * H].reshape(2, H)
    g["dt_bias"] = (ddv_x + ddv_c)[0, :2 * H].reshape(2, H)
    return loss_acc[0, 0], grad_x, g


ANY = pl.BlockSpec(memory_space=pl.ANY)


def _place():
    x, y, c = lax.axis_index("x"), lax.axis_index("y"), lax.axis_index("c")
    chips = [(1 - x, y), (x, 1 - y), (1 - x, 1 - y)]
    return x, y, c, (x, y, 1 - c), chips


def _gather_shards(big, small, *, name):
    nb = len(big)

    def body(*refs):
        ins, sm_in = refs[:nb], refs[nb]
        outs, sm_out = refs[nb + 1:2 * nb + 1], refs[2 * nb + 1]
        send, recv = refs[2 * nb + 2:]
        x, y, c, sibling, chips = _place()
        mine = 2 * x + y

        def half(a, shard, hc):
            hr = big[a].shape[0] // 2
            return outs[a].at[shard, pl.ds(hc * hr, hr), :]

        def remote(k, src, dst, to):
            return pltpu.make_async_remote_copy(src_ref=src, dst_ref=dst, send_sem=send.at[k], recv_sem=recv.at[k],
                                                device_id=to, device_id_type=MESH)

        sends = []
        for a in range(nb):
            hr = big[a].shape[0] // 2
            for j, chip in enumerate(chips):
                sends.append(remote(a * 3 + j, ins[a].at[pl.ds(c * hr, hr), :], half(a, mine, c), (*chip, c)))
        for j, chip in enumerate(chips):
            sends.append(remote(nb * 3 + j, sm_in, sm_out.at[mine], (*chip, c)))
        for cp in sends:
            cp.start()
        base = nb * 3 + 3
        passed = []
        for a in range(nb):
            for j, (px, py) in enumerate(chips):
                theirs = 2 * px + py
                remote(a * 3 + j, half(a, theirs, c), half(a, theirs, c), sibling).wait_recv()
                fw = remote(base + a * 3 + j, half(a, theirs, c), half(a, theirs, c), sibling)
                fw.start()
                passed.append(fw)
        for a in range(nb):
            for j, (px, py) in enumerate(chips):
                theirs = 2 * px + py
                remote(base + a * 3 + j, half(a, theirs, 1 - c), half(a, theirs, 1 - c), sibling).wait_recv()
        for j, (px, py) in enumerate(chips):
            remote(nb * 3 + j, sm_in, sm_out.at[2 * px + py], sibling).wait_recv()
        for cp in sends + passed:
            cp.wait_send()

    n_remote = 2 * nb * 3 + 3
    outs = pl.pallas_call(
        body, name=name, in_specs=[ANY] * (nb + 1), out_specs=[ANY] * (nb + 1),
        out_shape=[jax.ShapeDtypeStruct((N_CHIPS,) + a.shape, a.dtype) for a in big + [small]],
        scratch_shapes=[pltpu.SemaphoreType.DMA((n_remote,)), pltpu.SemaphoreType.DMA((n_remote,))],
    )(*big, small)
    return outs[:nb], outs[nb]


def _gather_late_start(arrs, after, *, name):
    na = len(arrs)

    def body(*refs):
        ins, land = refs[:na], refs[na:2 * na]
        send, recv = refs[2 * na + 1], refs[2 * na + 2]
        token = refs[-1]
        x, y, c, _, chips = _place()
        for a in range(na):
            hr = arrs[a].shape[0] // 2
            for j, (px, py) in enumerate(chips):
                for other in range(2):
                    k = (a * 3 + j) * 2 + other
                    _remote(ins[a].at[pl.ds(c * hr, hr), :], land[a].at[2 * x + y, pl.ds(c * hr, hr), :],
                            send.at[k], recv.at[k], (px, py, c ^ other)).start()
        token[...] = jnp.zeros_like(token)

    lands = [((N_CHIPS,) + a.shape, a.dtype) for a in arrs]
    held = lambda a: pltpu.with_memory_space_constraint(a, pltpu.HBM)
    outs = pl.pallas_call(
        body, name=name, in_specs=[HBM] * (2 * na) + [ANY],
        out_specs=[SEM, SEM] + [HBM] * (2 * na) + [pl.BlockSpec(memory_space=pltpu.VMEM)],
        out_shape=[pltpu.SemaphoreType.DMA((6 * na,)), pltpu.SemaphoreType.DMA((6 * na,))]
        + [pltpu.HBM(a.shape, a.dtype) for a in arrs] + [pltpu.HBM(s, d) for s, d in lands]
        + [jax.ShapeDtypeStruct((8, 128), F32)],
        input_output_aliases={i: 2 + i for i in range(2 * na)},
        compiler_params=pltpu.CompilerParams(has_side_effects=pltpu.SideEffectType.DATAFLOW_SIDE_EFFECTING),
    )(*[held(a) for a in arrs], *[held(lax.empty(s, d)) for s, d in lands], after)
    return outs[0], outs[1], outs[2:2 + na], outs[2 + na:2 + 2 * na], outs[-1]


def _gather_late_wait(send, recv, thru, land, after, *, name):
    na = len(thru)

    def body(*refs):
        ins, slots = refs[:na], refs[na:2 * na]
        send_sem, recv_sem = refs[2 * na], refs[2 * na + 1]
        x, y, c, _, chips = _place()
        for a in range(na):
            hr = thru[a].shape[0] // 2
            for j, (px, py) in enumerate(chips):
                for other in range(2):
                    k = (a * 3 + j) * 2 + other
                    half = c ^ other
                    cp = _remote(ins[a].at[pl.ds(c * hr, hr), :], slots[a].at[2 * px + py, pl.ds(half * hr, hr), :],
                                 send_sem.at[k], recv_sem.at[k], (px, py, half))
                    cp.wait_send()
                    cp.wait_recv()

    outs = pl.pallas_call(
        body, name=name, in_specs=[HBM] * (2 * na) + [SEM, SEM, ANY], out_specs=[HBM] * (2 * na),
        out_shape=[pltpu.HBM(a.shape, a.dtype) for a in list(thru) + list(land)],
        input_output_aliases={i: i for i in range(2 * na)},
        compiler_params=pltpu.CompilerParams(has_side_effects=pltpu.SideEffectType.DATAFLOW_SIDE_EFFECTING),
    )(*thru, *land, send, recv, after)
    return outs[na:]


def _row_chunks(rows, row_bytes, align=8):
    n = max(1, min(rows // align, -(-rows * row_bytes // DMA_CHUNK_BYTES)))
    per = -(-(-(-rows // n)) // align) * align
    return [(r, min(per, rows - r)) for r in range(0, rows, per)]


def _remote(src, dst, send, recv, to):
    return pltpu.make_async_remote_copy(src_ref=src, dst_ref=dst, send_sem=send, recv_sem=recv, device_id=to,
                                        device_id_type=MESH)


def _exchange_halves(arrs, *, name):
    na = len(arrs)

    def body(*refs):
        ins, got = refs[:na], refs[na:2 * na]
        send, recv = refs[2 * na:]
        x, y, c, sibling, _ = _place()
        for a in range(na):
            ns, rows, cols = arrs[a].shape
            hr = rows // 2
            for s in range(ns):
                for r0, nr in _row_chunks(hr, cols * arrs[a].dtype.itemsize, 16):
                    _remote(ins[a].at[s, pl.ds((1 - c) * hr + r0, nr), :], got[a].at[s, pl.ds(r0, nr), :],
                            send.at[a], recv.at[a], sibling).start()
        for a in range(na):
            hr = arrs[a].shape[1] // 2
            _remote(ins[a].at[:, pl.ds((1 - c) * hr, hr), :], got[a], send.at[a], recv.at[a], sibling).wait()

    return pl.pallas_call(
        body, name=name, in_specs=[ANY] * na, out_specs=[ANY] * na,
        out_shape=[jax.ShapeDtypeStruct((N_CHIPS, a.shape[1] // 2, a.shape[2]), a.dtype) for a in arrs],
        scratch_shapes=[pltpu.SemaphoreType.DMA((na,)), pltpu.SemaphoreType.DMA((na,))],
    )(*arrs)


def _scatter_sections(arrs, *, name):
    na = len(arrs)

    def body(*refs):
        ins, outs = refs[:na], refs[na:2 * na]
        send, recv = refs[2 * na:]
        x, y, c, _, chips = _place()
        for a in range(na):
            _, hr, cols = arrs[a].shape
            for j, (px, py) in enumerate(chips):
                for r0, nr in _row_chunks(hr, cols * arrs[a].dtype.itemsize, 16):
                    _remote(ins[a].at[2 * px + py, pl.ds(r0, nr), :], outs[a].at[j, pl.ds(r0, nr), :],
                            send.at[a * 3 + j], recv.at[a * 3 + j], (px, py, c)).start()
        for a in range(na):
            for j, (px, py) in enumerate(chips):
                _remote(ins[a].at[2 * px + py], outs[a].at[j], send.at[a * 3 + j], recv.at[a * 3 + j], (px, py, c)).wait()

    return pl.pallas_call(
        body, name=name, in_specs=[ANY] * na, out_specs=[ANY] * na,
        out_shape=[jax.ShapeDtypeStruct((N_CHIPS - 1,) + a.shape[1:], a.dtype) for a in arrs],
        scratch_shapes=[pltpu.SemaphoreType.DMA((3 * na,)), pltpu.SemaphoreType.DMA((3 * na,))],
    )(*arrs)


HBM = pl.BlockSpec(memory_space=pltpu.HBM)
SEM = pl.BlockSpec(memory_space=pltpu.SEMAPHORE)


def _scatter_start(arrs, *, name):
    na = len(arrs)

    def body(*refs):
        ins, land = refs[:na], refs[na:2 * na]
        send, recv = refs[2 * na], refs[2 * na + 1]
        token = refs[-1]
        x, y, c, _, chips = _place()
        for a in range(na):
            _, hr, cols = arrs[a].shape
            for j, (px, py) in enumerate(chips):
                for r0, nr in _row_chunks(hr, cols * arrs[a].dtype.itemsize, 16):
                    _remote(ins[a].at[2 * px + py, pl.ds(r0, nr), :], land[a].at[j, pl.ds(r0, nr), :],
                            send.at[a * 3 + j], recv.at[a * 3 + j], (px, py, c)).start()
        token[...] = jnp.zeros_like(token)

    slots = [((N_CHIPS - 1,) + a.shape[1:], a.dtype) for a in arrs]
    held = lambda a: pltpu.with_memory_space_constraint(a, pltpu.HBM)
    outs = pl.pallas_call(
        body, name=name, in_specs=[HBM] * (2 * na),
        out_specs=[SEM, SEM] + [HBM] * (2 * na) + [pl.BlockSpec(memory_space=pltpu.VMEM)],
        out_shape=[pltpu.SemaphoreType.DMA((3 * na,)), pltpu.SemaphoreType.DMA((3 * na,))]
        + [pltpu.HBM(a.shape, a.dtype) for a in arrs] + [pltpu.HBM(s, d) for s, d in slots]
        + [jax.ShapeDtypeStruct((8, 128), F32)],
        input_output_aliases={i: 2 + i for i in range(2 * na)},
        compiler_params=pltpu.CompilerParams(has_side_effects=pltpu.SideEffectType.DATAFLOW_SIDE_EFFECTING),
    )(*[held(a) for a in arrs], *[held(lax.empty(s, d)) for s, d in slots])
    return outs[0], outs[1], outs[2:2 + na], outs[2 + na:2 + 2 * na], outs[-1]


def _scatter_wait(send, recv, thru, land, after, *, name):
    na = len(thru)

    def body(*refs):
        ins, slots = refs[:na], refs[na:2 * na]
        send_sem, recv_sem = refs[2 * na], refs[2 * na + 1]
        x, y, c, _, chips = _place()
        for a in range(na):
            for j, (px, py) in enumerate(chips):
                cp = _remote(ins[a].at[2 * px + py], slots[a].at[j], send_sem.at[a * 3 + j], recv_sem.at[a * 3 + j],
                             (px, py, c))
                cp.wait_send()
                cp.wait_recv()

    outs = pl.pallas_call(
        body, name=name, in_specs=[HBM] * (2 * na) + [SEM, SEM, ANY], out_specs=[HBM] * (2 * na),
        out_shape=[pltpu.HBM(a.shape, a.dtype) for a in list(thru) + list(land)],
        input_output_aliases={i: i for i in range(2 * na)},
        compiler_params=pltpu.CompilerParams(has_side_effects=pltpu.SideEffectType.DATAFLOW_SIDE_EFFECTING),
    )(*thru, *land, send, recv, after)
    return outs[:na], outs[na:]


def _finish_reduce(big, small, *, name):
    nb = len(big)

    def body(*refs):
        outs, sm = refs[nb + 1:2 * nb + 1], refs[2 * nb + 1]
        send, recv = refs[2 * nb + 2:]
        x, y, c, sibling, chips = _place()
        blk = lambda px, py, pc: sm.at[4 * px + 2 * py + pc]
        for a in range(nb):
            _, hr, cols = big[a].shape
            for r0, nr in _row_chunks(hr, cols * 4):
                _remote(outs[a].at[c, pl.ds(r0, nr), :], outs[a].at[c, pl.ds(r0, nr), :], send.at[a], recv.at[a],
                        sibling).start()
        first = [_remote(blk(x, y, c), blk(x, y, c), send.at[nb], recv.at[nb], sibling)]
        first += [_remote(blk(x, y, c), blk(x, y, c), send.at[nb + 1 + j], recv.at[nb + 1 + j], (*chip, c))
                  for j, chip in enumerate(chips)]
        for cp in first:
            cp.start()
        passed = []
        for j, (px, py) in enumerate(chips):
            _remote(blk(px, py, c), blk(px, py, c), send.at[nb + 1 + j], recv.at[nb + 1 + j], sibling).wait_recv()
            fw = _remote(blk(px, py, c), blk(px, py, c), send.at[nb + 4 + j], recv.at[nb + 4 + j], sibling)
            fw.start()
            passed.append(fw)
        for a in range(nb):
            _remote(outs[a].at[c], outs[a].at[1 - c], send.at[a], recv.at[a], sibling).wait()
        _remote(blk(x, y, c), blk(x, y, 1 - c), send.at[nb], recv.at[nb], sibling).wait_recv()
        for j, (px, py) in enumerate(chips):
            _remote(blk(px, py, c), blk(px, py, 1 - c), send.at[nb + 4 + j], recv.at[nb + 4 + j], sibling).wait_recv()
        for cp in first + passed:
            cp.wait_send()

    n_remote = nb + 7
    arrs = list(big) + [small]
    outs = pl.pallas_call(
        body, name=name, in_specs=[ANY] * (nb + 1), out_specs=[ANY] * (nb + 1),
        out_shape=[jax.ShapeDtypeStruct(a.shape, F32) for a in arrs],
        input_output_aliases={i: i for i in range(nb + 1)},
        scratch_shapes=[pltpu.SemaphoreType.DMA((n_remote,)), pltpu.SemaphoreType.DMA((n_remote,))],
    )(*arrs)
    return outs[:nb], outs[nb]


def _reduce_start(sectioned, where):
    got = _exchange_halves(sectioned, name="rs_exchange_halves")
    chip_sum = [_add_sibling(a, g, where, BF16, name=f"rs_add_sibling_{i}")
                for i, (a, g) in enumerate(zip(sectioned, got))]
    send, recv, thru, land, token = _scatter_start(chip_sum, name="rs_scatter_start")
    return (send, recv, thru, land), token


def _reduce_finish(handle, small, where, after):
    send, recv, thru, land = handle
    got = _exchange_halves([small], name="rs_exchange_small")
    small_sum = _add_sibling(small, got[0], where, F32, name="rs_add_sibling_small")
    small_slots = _scatter_sections([small_sum], name="rs_scatter_small")[0]
    own, slots = _scatter_wait(send, recv, thru, land, after, name="rs_scatter_wait")
    red = [_sum_chips(p, s, where, 2, 0, name=f"rs_sum_chips_{i}") for i, (p, s) in enumerate(zip(own, slots))]
    red_small = _sum_chips(small_sum, small_slots, where, 2 * N_CHIPS, 2, name="rs_sum_chips_small")
    big, sm = _finish_reduce(red, red_small, name="rs_finish")
    return big, sm.reshape(-1, sm.shape[-1])


def kernel(x, c, ctx, c_ctx, w_mod, b_mod, g_pre, g_post, w_in, w_conv, a_log, dt_bias, g_onorm, gm_ln_g, gm_ln_b, w_sp, b_sp, w_pa, w_pb, w_out, loss_target, m_c_ctx, m_w_mod, m_b_mod, m_g_pre, m_g_post, m_w_in, m_w_conv, m_a_log, m_dt_bias, m_g_onorm, m_gm_ln_g, m_gm_ln_b, m_w_sp, m_b_sp, m_w_pa, m_w_pb, m_w_out, v_c_ctx, v_w_mod, v_b_mod, v_g_pre, v_g_post, v_w_in, v_w_conv, v_a_log, v_dt_bias, v_g_onorm, v_gm_ln_g, v_gm_ln_b, v_w_sp, v_b_sp, v_w_pa, v_w_pb, v_w_out):
    names = ["c_ctx", "w_mod", "b_mod", "g_pre", "g_post", "w_in", "w_conv", "a_log", "dt_bias", "g_onorm", "gm_ln_g",
             "gm_ln_b", "w_sp", "b_sp", "w_pa", "w_pb", "w_out"]
    w = dict(zip(names, (c_ctx, w_mod, b_mod, g_pre, g_post, w_in, w_conv, a_log, dt_bias, g_onorm, gm_ln_g, gm_ln_b,
                         w_sp, b_sp, w_pa, w_pb, w_out)))
    m = dict(zip(names, (m_c_ctx, m_w_mod, m_b_mod, m_g_pre, m_g_post, m_w_in, m_w_conv, m_a_log, m_dt_bias, m_g_onorm,
                         m_gm_ln_g, m_gm_ln_b, m_w_sp, m_b_sp, m_w_pa, m_w_pb, m_w_out)))
    v = dict(zip(names, (v_c_ctx, v_w_mod, v_b_mod, v_g_pre, v_g_post, v_w_in, v_w_conv, v_a_log, v_dt_bias, v_g_onorm,
                         v_gm_ln_g, v_gm_ln_b, v_w_sp, v_b_sp, v_w_pa, v_w_pb, v_w_out)))
    xy = 2 * lax.axis_index("x") + lax.axis_index("y")
    where = jnp.stack([lax.axis_index("c"), xy, 2 * xy + lax.axis_index("c")]).astype(jnp.int32)

    shards = [a[0].astype(BF16) for a in (w_mod, w_in)]
    gathered, wconv_all = _gather_shards(shards, w_conv[0], name="gather_weights")
    own = lambda full, shard: lax.dynamic_update_slice(full, shard[None], (xy, 0, 0))
    wm_all, win_all = [own(f, s) for f, s in zip(gathered, shards)]
    wconv_all = own(wconv_all, w_conv[0])
    late_shards = [a[0].astype(BF16) for a in (w_pa, w_pb, w_out)]
    late = _gather_late_start(late_shards, wconv_all, name="gather_late_start")

    def late_weights(after):
        lands = _gather_late_wait(*late[:4], after, name="gather_late_wait")
        return [own(f, s).reshape(D, D) for f, s in zip(lands, late_shards)]
    w_conv_f = jnp.concatenate([wconv_all[s] for s in range(N_CHIPS)], axis=1)
    shard_cols = IN_COLS // N_CHIPS
    cut = OFF_A - shard_cols
    assert 0 < cut and cut + 4 * H < shard_cols
    w_qkv = jnp.concatenate([win_all[0], win_all[1][:, :cut]], axis=1)
    w_ab = jnp.pad(win_all[1][:, cut:cut + 4 * H], ((0, 0), (0, DH - 4 * H)))
    w_rest = jnp.concatenate([win_all[1][:, cut + 4 * H:], win_all[2], win_all[3]], axis=1)

    blk = D // N_CHIPS
    rest_cut = shard_cols - cut - 4 * H
    big_names = ("w_in", "w_pa", "w_pb", "w_out")
    in_flight = []

    def start_reduce(grads):
        dw_qkv, dw_ab, dw_rest = grads["w_in"]
        g_win = jnp.stack([dw_qkv[:, :shard_cols],
                           jnp.concatenate([dw_qkv[:, shard_cols:], dw_ab, dw_rest[:, :rest_cut]], axis=1),
                           dw_rest[:, rest_cut:rest_cut + shard_cols], dw_rest[:, rest_cut + shard_cols:]]).astype(BF16)
        sectioned = [g_win] + [grads[k].reshape(N_CHIPS, blk, D) for k in ("w_pa", "w_pb", "w_out")]
        handle, token = _reduce_start(sectioned, where)
        in_flight.append(handle)
        return token

    loss_local, grad_x, g = _local_step(
        x[0], c, ctx[0], c_ctx, loss_target[0], wm_all, b_mod, g_pre, g_post, w_qkv, w_ab, w_rest,
        w_conv_f, a_log[0], dt_bias[0], g_onorm, gm_ln_g, gm_ln_b, w_sp[0], b_sp[0],
        late_weights, on_large_gradients=start_reduce, run_after=late[4])
    g["loss"] = loss_local
    g["_pad"] = jnp.zeros((SMALL_LAYOUT["_pad"][1],), F32)
    g["mod_factors"] = lax.dynamic_update_slice(jnp.zeros((2 * N_CHIPS, 4 * D), F32),
                                                jnp.concatenate(g["mod_factors"])[None], (where[2], 0))
    tail = jnp.zeros((N_CHIPS * SMALL_ROWS * 128 - sum(s for _, s in SMALL_LAYOUT.values()),), F32)
    flat = jnp.concatenate([g[k].reshape(-1) for k in SMALL_LAYOUT] + [tail])
    reduced, gr_small = _reduce_finish(in_flight[0], flat.reshape(N_CHIPS, SMALL_ROWS, 128), where, g["b_mod"])
    gr_tiny = gr_small[TINY_ROW0:]

    def entry(arr, k):
        off, size = SMALL_LAYOUT[k]
        row, col = off // 128 - TINY_ROW0, off % 128
        return arr[row:row + size // 128].reshape(-1) if size >= 128 else arr[row, col:col + size]

    factors = entry(gr_tiny, "mod_factors").reshape(2 * N_CHIPS, 4 * D)
    pad_rows = lambda a: jnp.pad(a, ((0, 16 - a.shape[0]), (0, 0)))
    lhs = pad_rows(jnp.concatenate([factors[:, :D], g["silu_c_ctx"]], axis=0))
    rhs = pad_rows(jnp.concatenate([factors[:, D:], entry(gr_tiny, "dm_ctx")[None]], axis=0))
    mod_cols = 3 * D // N_CHIPS
    gr_wm = _mm(lhs, lax.dynamic_slice(rhs, (0, xy * mod_cols), (16, mod_cols)), ta=True, name="dw_mod")

    res = {k: _adamw(w[k], gr, m[k], v[k], name=f"adamw_{k}") for k, gr in zip(big_names[1:], reduced[1:])}
    flip = lambda a: jnp.swapaxes(a, 1, 2)
    res["w_in"] = [flip(r) for r in _adamw(flip(w_in), flip(reduced[0].reshape(w_in.shape)), flip(m_w_in), flip(v_w_in),
                                           name="adamw_w_in")]
    res["w_mod"] = _adamw(w_mod[0], gr_wm, m_w_mod[0], v_w_mod[0], name="adamw_w_mod")
    res["w_sp"] = _adamw(w_sp.reshape(-1, 128), gr_small, m_w_sp.reshape(-1, 128), v_w_sp.reshape(-1, 128),
                         name="adamw_w_sp")
    tiny_names = [k for k in SMALL_LAYOUT if k not in ("w_sp", "w_conv", "dm_ctx", "mod_factors", "loss", "_pad")]

    def pack(src):
        parts = [src[k].reshape(-1) if k in tiny_names else jnp.zeros((s,), F32)
                 for k, (_, s) in SMALL_LAYOUT.items() if k != "w_sp"]
        return jnp.concatenate(parts + [tail]).reshape(-1, 128)

    tiny_res = _adamw(pack(w), gr_tiny, pack(m), pack(v), name="adamw_tiny")
    for k in tiny_names:
        res[k] = [entry(r, k) for r in tiny_res]
    g_conv = lax.dynamic_slice(entry(gr_tiny, "w_conv").reshape(3, 3 * D), (0, xy * (3 * D // N_CHIPS)),
                               (3, 3 * D // N_CHIPS))
    conv_res = _adamw(jnp.pad(w_conv[0], ((0, 5), (0, 0))), jnp.pad(g_conv, ((0, 5), (0, 0))),
                      jnp.pad(m_w_conv[0], ((0, 5), (0, 0))), jnp.pad(v_w_conv[0], ((0, 5), (0, 0))), name="adamw_w_conv")
    res["w_conv"] = [r[:3] for r in conv_res]
    res = {k: [r.reshape(w[k].shape) for r in res[k]] for k in names}

    out = [entry(gr_tiny, "loss").reshape(()), grad_x[None]]
    for i in range(4):
        out += [res[k][i] for k in names]
    return tuple(out)
```

```python
import functools

import jax
import jax.numpy as jnp
from jax import lax
from jax.experimental import pallas as pl
from jax.experimental.pallas import tpu as pltpu

F32 = jnp.float32
BF16 = jnp.bfloat16
HI = lax.Precision.HIGHEST
MESH = pl.DeviceIdType.MESH

D = 1024
H = 8
DH = 128
CH = 64
LOG_CH = 6
PAIR = 2 * CH
GM = 128
assert 1 << LOG_CH == CH and PAIR == DH
PREC_POWERS = ((lax.Precision.HIGH, lax.Precision.HIGH),) * 3 + ((None, None),) * 2
assert len(PREC_POWERS) == LOG_CH - 1
HEADS_PER_ITER_FWD = 8
HEADS_PER_ITER_BWD = 8
EPS = 1e-6
N_CHIPS = 4
OFF_A = 3 * D
OFF_ZB = OFF_A + 4 * H
IN_COLS = OFF_ZB + 6 * D
VMEM_LIMIT_V7X = 56 * 1024 * 1024
DMA_CHUNK_BYTES = 2 * 1024 * 1024

ADAM_LR, ADAM_B1, ADAM_B2, ADAM_EPS, ADAM_WD, ADAM_STEP = 0.001, 0.9, 0.999, 1e-08, 0.01, 10

SMALL_LAYOUT = {}
_off = 0
for _n, _s in (("w_sp", H * GM * GM), ("c_ctx", D), ("b_mod", 3 * D), ("g_pre", D), ("g_post", D), ("gm_ln_g", D),
               ("gm_ln_b", D), ("b_sp", H * GM), ("g_onorm", DH), ("a_log", 2 * H), ("dt_bias", 2 * H), ("loss", 1),
               ("_pad", 128 - 4 * H - 1 + 6 * 128), ("w_conv", 3 * 3 * D), ("dm_ctx", 3 * D), ("mod_factors", 8 * 4 * D)):
    SMALL_LAYOUT[_n] = (_off, _s)
    _off += _s
SMALL_ROWS = 368
assert N_CHIPS * SMALL_ROWS * 128 >= _off and (SMALL_ROWS // 2) % 8 == 0 and _off % 128 == 0
TINY_ROW0 = SMALL_LAYOUT["c_ctx"][0] // 128
TINY_ROWS = SMALL_LAYOUT["w_conv"][0] // 128 - TINY_ROW0
assert TINY_ROWS % 8 == 0


def _params(sem=None):
    return pltpu.CompilerParams(dimension_semantics=sem, vmem_limit_bytes=VMEM_LIMIT_V7X)


def _tile(n, cands=(256, 128, 64, 32, 16, 8)):
    if n <= cands[0]:
        return n
    for cand in cands:
        if n % cand == 0 and cand >= 64:
            return cand
    return max(d for d in range(8, cands[0], 8) if n % d == 0)


def _silu(x):
    return x * jax.nn.sigmoid(x)


def _gelu(x):
    return 0.5 * x * (1.0 + jnp.tanh(0.7978845608028654 * (x + 0.044715 * (x * x * x))))


def _mm(a, b, *, ta=False, tb=False, out_dtype=F32, tm=1024, tn=1024, tk=1024, add=None, add_from=0,
        b_sections=False, out_sections=False, after=None, name):
    m, k = (a.shape[1], a.shape[0]) if ta else a.shape
    if b_sections:
        sect = b.shape[2]
        n = b.shape[1] if tb else b.shape[0] * sect
        tn, tk = (tn, sect) if tb else (sect, tk)
    else:
        n = b.shape[0] if tb else b.shape[1]
    tm, tn, tk = min(tm, m), min(tn, n), min(tk, k)
    assert m % tm == 0 and n % tn == 0 and k % tk == 0, (name, m, n, k, tm, tn, tk)
    nk = k // tk
    dims = (((0,) if ta else (1,), (1,) if tb else (0,)), ((), ()))
    has_add = add is not None
    assert not has_add or add.shape == (m, n - add_from * tn), (name, add.shape)

    def body(*refs):
        a_ref, b_ref = refs[:2]
        o_ref = refs[2 + int(has_add) + int(after is not None)]
        acc_ref = refs[-1]
        kk = pl.program_id(2)
        bv = b_ref[0] if b_sections else b_ref[...]
        part = lax.dot_general(a_ref[...].astype(BF16), bv.astype(BF16), dims, preferred_element_type=F32)

        def finish(res):
            if has_add:
                res = res + jnp.where(pl.program_id(1) >= add_from, refs[2][...], 0.0)
            if out_sections:
                o_ref[0] = res.astype(out_dtype)
            else:
                o_ref[...] = res.astype(out_dtype)

        if nk == 1:
            finish(part)
            return

        @pl.when(kk == 0)
        def _():
            acc_ref[...] = part

        @pl.when((kk > 0) & (kk < nk - 1))
        def _():
            acc_ref[...] += part

        @pl.when(kk == nk - 1)
        def _():
            finish(acc_ref[...] + part)

    a_spec = pl.BlockSpec((tk, tm), lambda i, j, q: (q, i)) if ta else pl.BlockSpec((tm, tk), lambda i, j, q: (i, q))
    if b_sections:
        b_spec = (pl.BlockSpec((1, tn, tk), lambda i, j, q: (q, j, 0)) if tb
                  else pl.BlockSpec((1, tk, tn), lambda i, j, q: (j, q, 0)))
    else:
        b_spec = pl.BlockSpec((tn, tk), lambda i, j, q: (j, q)) if tb else pl.BlockSpec((tk, tn), lambda i, j, q: (q, j))
    add_spec = [pl.BlockSpec((tm, tn), lambda i, j, q: (i, jnp.maximum(j - add_from, 0)))] if has_add else []
    if out_sections:
        out_spec, out_shape = pl.BlockSpec((1, tm, tn), lambda i, j, q: (j, i, 0)), (n // tn, m, tn)
    else:
        out_spec, out_shape = pl.BlockSpec((tm, tn), lambda i, j, q: (i, j)), (m, n)
    return pl.pallas_call(
        body, name=name, grid=(m // tm, n // tn, nk),
        in_specs=[a_spec, b_spec] + add_spec + ([pl.BlockSpec(memory_space=pl.ANY)] if after is not None else []),
        out_specs=out_spec, out_shape=jax.ShapeDtypeStruct(out_shape, out_dtype),
        scratch_shapes=[pltpu.VMEM((tm, tn), F32)] if nk > 1 else [],
        compiler_params=_params(("parallel", "parallel", "arbitrary")),
    )(*([a, b] + ([add] if has_add else []) + ([after] if after is not None else [])))


def _h_fn(x, g, m):
    shift, scale = m[:, 0:D], m[:, D:2 * D]
    r = lax.rsqrt(jnp.mean(x * x, axis=-1, keepdims=True) + EPS)
    return (x * r * g) * (1.0 + scale) + shift


def _norm_fwd(x, g, mod, bmod, *, name):
    rows = x.shape[0]
    t = min(512, rows)

    def body(x_ref, g_ref, m_ref, b_ref, h_ref):
        h_ref[...] = _h_fn(x_ref[...], g_ref[...], m_ref[...] + b_ref[...]).astype(BF16)

    vec = lambda w: pl.BlockSpec((1, w), lambda i: (0, 0))
    return pl.pallas_call(
        body, name=name, grid=(rows // t,),
        in_specs=[pl.BlockSpec((t, D), lambda i: (i, 0)), vec(D), vec(3 * D), vec(3 * D)],
        out_specs=pl.BlockSpec((t, D), lambda i: (i, 0)),
        out_shape=jax.ShapeDtypeStruct((rows, D), BF16),
        compiler_params=_params(("parallel",)),
    )(x, g, mod, bmod)


def _norm_bwd(x, g, mod, bmod, dh_parts, resid, *, name):
    rows = x.shape[0]
    t = min(512, rows)
    n_parts = len(dh_parts)
    has_resid = resid is not None

    def body(*refs):
        x_ref, g_ref, m_ref, b_ref = refs[:4]
        parts = refs[4:4 + n_parts]
        r_ref = refs[4 + n_parts] if has_resid else None
        dx_ref, dg_ref, dm_ref = refs[-3:]
        i = pl.program_id(0)
        dh = parts[0][...]
        for p in parts[1:]:
            dh = dh + p[...]
        _, vjp = jax.vjp(_h_fn, x_ref[...], g_ref[...], m_ref[...] + b_ref[...])
        dx, dg, dm = vjp(dh)
        if has_resid:
            dx = dx + r_ref[...]
        dx_ref[...] = dx

        @pl.when(i == 0)
        def _():
            dg_ref[...] = dg
            dm_ref[...] = dm

        @pl.when(i > 0)
        def _():
            dg_ref[...] += dg
            dm_ref[...] += dm

    vec = lambda w: pl.BlockSpec((1, w), lambda i: (0, 0))
    tile = pl.BlockSpec((t, D), lambda i: (i, 0))
    ins = [x, g, mod, bmod, *dh_parts] + ([resid] if has_resid else [])
    return pl.pallas_call(
        body, name=name, grid=(rows // t,),
        in_specs=[tile, vec(D), vec(3 * D), vec(3 * D)] + [tile] * (n_parts + int(has_resid)),
        out_specs=[tile, vec(D), vec(3 * D)],
        out_shape=[jax.ShapeDtypeStruct((rows, D), F32), jax.ShapeDtypeStruct((1, D), F32),
                   jax.ShapeDtypeStruct((1, 3 * D), F32)],
        compiler_params=_params(("arbitrary",)),
    )(*ins)


def _conv_tile(u_ref, r0, t, rows, w0, w1, w2):
    u = u_ref[pl.ds(r0, t), :]
    prev8 = u_ref[pl.ds(pl.multiple_of(jnp.maximum(r0 - 8, 0), 8), 8), :]
    next8 = u_ref[pl.ds(pl.multiple_of(jnp.minimum(r0 + t, rows - 8), 8), 8), :]
    r8 = lax.broadcasted_iota(jnp.int32, (8, DH), 0)
    prev_row = jnp.sum(jnp.where(r8 == 7, prev8, 0.0), axis=0, keepdims=True)
    next_row = jnp.sum(jnp.where(r8 == 0, next8, 0.0), axis=0, keepdims=True)
    prev_row = jnp.where(r0 > 0, prev_row, 0.0)
    next_row = jnp.where(r0 + t < rows, next_row, 0.0)
    ri = lax.broadcasted_iota(jnp.int32, (t, DH), 0)
    um1 = jnp.where(ri == 0, prev_row, pltpu.roll(u, 1, 0))
    up1 = jnp.where(ri == t - 1, next_row, pltpu.roll(u, t - 1, 0))
    return w0 * um1 + w1 * u + w2 * up1, um1, u, up1


def _rowlocal(z, is_norm):
    y = _silu(z)
    yn = y * lax.rsqrt(jnp.sum(y * y, axis=-1, keepdims=True) + EPS)
    return jnp.where(is_norm, yn, y)


def _prep_fwd(p, wconv, n_norm, *, name):
    rows, nb = p.shape[0], p.shape[1] // DH
    t = min(512, rows)

    def body(u_ref, w_ref, o_ref):
        is_norm = pl.program_id(0) < n_norm
        w0, w1, w2 = w_ref[0:1, :], w_ref[1:2, :], w_ref[2:3, :]

        def step(s, carry):
            r0 = pl.multiple_of(s * t, t)
            z, _, _, _ = _conv_tile(u_ref, r0, t, rows, w0, w1, w2)
            o_ref[0, pl.ds(r0, t), :] = _rowlocal(z, is_norm)
            return carry

        lax.fori_loop(0, rows // t, step, 0)

    return pl.pallas_call(
        body, name=name, grid=(nb,),
        in_specs=[pl.BlockSpec((rows, DH), lambda j: (0, j)), pl.BlockSpec((3, DH), lambda j: (0, j))],
        out_specs=pl.BlockSpec((1, rows, DH), lambda j: (j, 0, 0)),
        out_shape=jax.ShapeDtypeStruct((nb, rows, DH), F32),
        compiler_params=_params(("parallel",)),
    )(p, wconv)


def _prep_bwd(p, wconv, d_a, d_b, n_norm, *, name):
    rows, nb = p.shape[0], p.shape[1] // DH
    t = min(512, rows)

    def body(u_ref, w_ref, da_ref, db_ref, du_ref, dw_ref, dz_ref):
        is_norm = pl.program_id(0) < n_norm
        w0, w1, w2 = w_ref[0:1, :], w_ref[1:2, :], w_ref[2:3, :]

        def step1(s, carry):
            a0, a1, a2 = carry
            r0 = pl.multiple_of(s * t, t)
            z, um1, u, up1 = _conv_tile(u_ref, r0, t, rows, w0, w1, w2)
            _, vjp = jax.vjp(lambda zz: _rowlocal(zz, is_norm), z)
            (dz,) = vjp(da_ref[0, pl.ds(r0, t), :] + db_ref[0, pl.ds(r0, t), :])
            dz_ref[pl.ds(r0, t), :] = dz
            red = lambda v: jnp.sum(v, axis=0, keepdims=True)
            return a0 + red(dz * um1), a1 + red(dz * u), a2 + red(dz * up1)

        zero = jnp.zeros((1, DH), F32)
        a0, a1, a2 = lax.fori_loop(0, rows // t, step1, (zero, zero, zero))
        dw_ref[0:1, :] = a0
        dw_ref[1:2, :] = a1
        dw_ref[2:3, :] = a2

        def step2(s, carry):
            r0 = pl.multiple_of(s * t, t)
            du, _, _, _ = _conv_tile(dz_ref, r0, t, rows, w2, w1, w0)
            du_ref[pl.ds(r0, t), :] = du.astype(BF16)
            return carry

        lax.fori_loop(0, rows // t, step2, 0)

    col = pl.BlockSpec((rows, DH), lambda j: (0, j))
    w_spec = pl.BlockSpec((3, DH), lambda j: (0, j))
    d_spec = pl.BlockSpec((1, rows, DH), lambda j: (j, 0, 0))
    return pl.pallas_call(
        body, name=name, grid=(nb,),
        in_specs=[col, w_spec, d_spec, d_spec], out_specs=[col, w_spec],
        out_shape=[jax.ShapeDtypeStruct((rows, nb * DH), BF16), jax.ShapeDtypeStruct((3, nb * DH), F32)],
        scratch_shapes=[pltpu.VMEM((rows, DH), F32)],
        compiler_params=_params(("parallel",)),
    )(p, wconv, d_a, d_b)


def _gates_fn(pab, avec, dvec):
    t = pab.shape[0]
    lane = lax.broadcasted_iota(jnp.int32, pab.shape, 1)
    xg = pab + dvec
    sp = jnp.maximum(xg, 0.0) + jnp.log(1.0 + jnp.exp(-jnp.abs(xg)))
    g = jnp.where(lane < 2 * H, -jnp.exp(avec) * sp, 0.0)
    ii = lax.broadcasted_iota(jnp.int32, (t, t), 0)
    jj = lax.broadcasted_iota(jnp.int32, (t, t), 1)
    same = (ii >> LOG_CH) == (jj >> LOG_CH)
    cum_f = _dot(jnp.where(same & (jj <= ii), 1.0, 0.0), g, precision=HI)
    cum_r = _dot(jnp.where(same & (jj >= ii), 1.0, 0.0), g, precision=HI)
    return jnp.where(lane < H, cum_f, jnp.where(lane < 2 * H, cum_r, jnp.where(lane < 4 * H, jax.nn.sigmoid(pab), 0.0)))


def _gates_fwd(pab, avec, dvec, *, name):
    rows = pab.shape[0]
    t = min(512, rows)

    def body(p_ref, a_ref, d_ref, o_ref):
        o_ref[...] = _gates_fn(p_ref[...], a_ref[...], d_ref[...])

    vec = pl.BlockSpec((1, DH), lambda i: (0, 0))
    tile = pl.BlockSpec((t, DH), lambda i: (i, 0))
    return pl.pallas_call(
        body, name=name, grid=(rows // t,), in_specs=[tile, vec, vec], out_specs=tile,
        out_shape=jax.ShapeDtypeStruct((rows, DH), F32), compiler_params=_params(("parallel",)),
    )(pab, avec, dvec)


def _gates_bwd(pab, avec, dvec, d_a, d_b, *, name):
    rows = pab.shape[0]
    t = min(512, rows)

    def body(p_ref, a_ref, d_ref, da_ref, db_ref, dp_ref, dav_ref, ddv_ref):
        i = pl.program_id(0)
        _, vjp = jax.vjp(_gates_fn, p_ref[...], a_ref[...], d_ref[...])
        dp, dav, ddv = vjp(da_ref[...] + db_ref[...])
        dp_ref[...] = dp.astype(BF16)

        @pl.when(i == 0)
        def _():
            dav_ref[...] = dav
            ddv_ref[...] = ddv

        @pl.when(i > 0)
        def _():
            dav_ref[...] += dav
            ddv_ref[...] += ddv

    vec = pl.BlockSpec((1, DH), lambda i: (0, 0))
    tile = pl.BlockSpec((t, DH), lambda i: (i, 0))
    return pl.pallas_call(
        body, name=name, grid=(rows // t,), in_specs=[tile, vec, vec, tile, tile], out_specs=[tile, vec, vec],
        out_shape=[jax.ShapeDtypeStruct((rows, DH), BF16), jax.ShapeDtypeStruct((1, DH), F32),
                   jax.ShapeDtypeStruct((1, DH), F32)],
        compiler_params=_params(("arbitrary",)),
    )(pab, avec, dvec, d_a, d_b)


def _dot_general(a, b, dims, precision=None):
    return lax.dot_general(a, b, (dims, ((), ())), precision=precision, preferred_element_type=F32)


@functools.partial(jax.custom_vjp, nondiff_argnums=(2,))
def _dot_bf16(a, b, dims):
    return _dot_general(a.astype(BF16), b.astype(BF16), dims)


def _dot_bf16_fwd(a, b, dims):
    return _dot_bf16(a, b, dims), (a, b)


def _dot_bf16_bwd(dims, res, g):
    a, b = res
    (ca,), (cb,) = dims
    da = _dot_bf16(g, b, ((1,), (1 - cb,))) if ca == 1 else _dot_bf16(b, g, ((1 - cb,), (1,)))
    db = _dot_bf16(a, g, ((1 - ca,), (0,))) if cb == 0 else _dot_bf16(g, a, ((0,), (1 - ca,)))
    return da, db


_dot_bf16.defvjp(_dot_bf16_fwd, _dot_bf16_bwd)


def _dot(a, b, dims=((1,), (0,)), precision=None):
    if precision is None and a.dtype == F32 and b.dtype == F32:
        return _dot_bf16(a, b, dims)
    return _dot_general(a, b, dims, precision)


_NT = ((1,), (1,))
_TN = ((0,), (0,))


@jax.custom_vjp
def _saved_inverse(neg_a, inv):
    return inv


def _saved_inverse_fwd(neg_a, inv):
    return inv, inv


def _saved_inverse_bwd(inv, d_inv):
    idx = range(len(inv))
    left = [_dot(inv[i], d_inv[i], _TN) for i in idx]
    d_neg_a = [_dot(left[i], inv[i], _NT) for i in idx]
    return d_neg_a, [jnp.zeros_like(t) for t in inv]


_saved_inverse.defvjp(_saved_inverse_fwd, _saved_inverse_bwd)


def _pairs(s, q, k, v, gcol, bcol, revs, inv_saved=None):
    idx = range(len(revs))
    ii = lax.broadcasted_iota(jnp.int32, (PAIR, PAIR), 0)
    jj = lax.broadcasted_iota(jnp.int32, (PAIR, PAIR), 1)
    same = (ii >> LOG_CH) == (jj >> LOG_CH)
    incl_d = (same & (ii >= jj), same & (ii <= jj))
    strict_d = (same & (ii > jj), same & (ii < jj))
    incl = [incl_d[int(r)] for r in revs]
    strict = [strict_d[int(r)] for r in revs]
    eye = jnp.where(ii == jj, 1.0, 0.0)
    gc_i = [jnp.broadcast_to(gcol[i], (PAIR, DH)) for i in idx]
    gc_j = [gc_i[i].T for i in idx]
    decay = [jnp.where(incl[i], jnp.exp(jnp.where(incl[i], gc_i[i] - gc_j[i], 0.0)), 0.0) for i in idx]
    b_b = [jnp.broadcast_to(bcol[i], (PAIR, DH)) for i in idx]
    kb = [k[i] * b_b[i] for i in idx]
    kk = [_dot(kb[i], k[i], _NT) for i in idx]
    bp = [jnp.where(strict[i], -kk[i] * decay[i], 0.0) for i in idx]
    if inv_saved is not None:
        inv = _saved_inverse(bp, inv_saved)
    else:
        low = bp
        for prec_sq, prec_acc in PREC_POWERS:
            bp = [_dot(bp[i], bp[i], precision=prec_sq) for i in idx]
            more = [_dot(low[i], bp[i], precision=prec_acc) for i in idx]
            low = [low[i] + bp[i] + more[i] for i in idx]
        inv = [eye + low[i] for i in idx]
    eg = [jnp.exp(gc_i[i]) for i in idx]
    sol = [_dot(inv[i], jnp.concatenate([v[i] * b_b[i], kb[i] * eg[i]], axis=1)) for i in idx]
    u_val = [sol[i][:, :DH] for i in idx]
    w_key = [sol[i][:, DH:] for i in idx]
    row = lax.broadcasted_iota(jnp.int32, (PAIR, 1), 0)
    has_q = q[0] is not None
    if has_q:
        qc = [q[i] * (DH ** -0.5) for i in idx]
        qk = [_dot(qc[i], k[i], _NT) for i in idx]
        attn = [qk[i] * decay[i] for i in idx]
        qd = [qc[i] * eg[i] for i in idx]
    outs = [[None, None] for _ in idx]
    zeros = jnp.zeros((CH, DH), F32)
    for step in range(2):
        cidx = [(1 - step) if revs[i] else step for i in idx]
        sl = [slice(c * CH, (c + 1) * CH) for c in cidx]
        last = [c * CH if revs[i] else c * CH + CH - 1 for i, c in zip(idx, cidx)]
        gl = [jnp.sum(jnp.where(row == last[i], gcol[i], 0.0), axis=0, keepdims=True) for i in idx]
        k_tail = [k[i][sl[i]] * jnp.exp(gl[i] - gc_i[i][sl[i]]) for i in idx]
        ws = [_dot(w_key[i][sl[i]], s[i]) for i in idx]
        v_new = [u_val[i][sl[i]] - ws[i] for i in idx]
        if has_q:
            v_pad = [jnp.concatenate([v_new[i], zeros] if cidx[i] == 0 else [zeros, v_new[i]], axis=0) for i in idx]
            o_state = [_dot(qd[i][sl[i]], s[i]) for i in idx]
            o_local = [_dot(attn[i][sl[i]], v_pad[i]) for i in idx]
            for i in idx:
                outs[i][cidx[i]] = o_state[i] + o_local[i]
        kv = [_dot(k_tail[i], v_new[i], _TN) for i in idx]
        s = [s[i] * jnp.exp(gl[i]) + kv[i] for i in idx]
    return s, ([jnp.concatenate(outs[i], axis=0) for i in idx] if has_q else None), inv


def _lane_col(tile, idx):
    lane = lax.broadcasted_iota(jnp.int32, tile.shape, 1)
    return jnp.sum(jnp.where(lane == idx, tile, 0.0), axis=1, keepdims=True)


def _gdn_fwd(qkv, gb, s0f, s0b, has_q, *, name):
    nb, rows, _ = qkv.shape
    n = rows // PAIR
    qoff = H if has_q else 0

    def body(qf_ref, qb_ref, gf_ref, gr_ref, s0f_ref, s0b_ref, of_ref, ob_ref, ssf_ref, ssb_ref, tsf_ref, tsb_ref,
             sf_ref, sb_ref):
        @pl.when(pl.program_id(0) == 0)
        def _():
            sf_ref[...] = s0f_ref[...]
            sb_ref[...] = s0b_ref[...]

        gtiles = (gf_ref[...], gr_ref[...])

        dirs = ((qf_ref, sf_ref, ssf_ref, of_ref), (qb_ref, sb_ref, ssb_ref, ob_ref))
        ts_refs = (tsf_ref, tsb_ref)

        def heads(hg, carry):
            work = [(hg * HEADS_PER_ITER_FWD + j, d) for j in range(HEADS_PER_ITER_FWD) for d in range(2)]
            loaded = []
            for h, d in work:
                q_ref, s_ref, _, _ = dirs[d]
                loaded.append((s_ref[h], q_ref[h] if has_q else None, q_ref[qoff + h], q_ref[qoff + H + h],
                               _lane_col(gtiles[d], d * H + h), _lane_col(gtiles[d], 2 * H + d * H + h)))
            s_new, o, inv = _pairs(*[list(col) for col in zip(*loaded)], revs=[d == 1 for _, d in work])
            for i, (h, d) in enumerate(work):
                _, s_ref, ss_ref, o_ref = dirs[d]
                ss_ref[0, h] = loaded[i][0]
                ts_refs[d][0, h] = inv[i]
                s_ref[h] = s_new[i]
                o_ref[h] = o[i] if has_q else jnp.zeros((PAIR, DH), F32)
            return carry

        if HEADS_PER_ITER_FWD == H:
            heads(0, 0)
        else:
            lax.fori_loop(0, H // HEADS_PER_ITER_FWD, heads, 0)

    fwd3 = lambda i: (0, i, 0)
    rev3 = lambda i: (0, n - 1 - i, 0)
    state = pl.BlockSpec((H, DH, DH), lambda i: (0, 0, 0))
    saved = pl.BlockSpec((1, H, DH, DH), lambda i: (i, 0, 0, 0))
    return pl.pallas_call(
        body, name=name, grid=(n,),
        in_specs=[pl.BlockSpec((nb, PAIR, DH), fwd3), pl.BlockSpec((nb, PAIR, DH), rev3),
                  pl.BlockSpec((PAIR, DH), lambda i: (i, 0)), pl.BlockSpec((PAIR, DH), lambda i: (n - 1 - i, 0)),
                  state, state],
        out_specs=[pl.BlockSpec((H, PAIR, DH), fwd3), pl.BlockSpec((H, PAIR, DH), rev3), saved, saved, saved, saved,
                   state, state],
        out_shape=[jax.ShapeDtypeStruct((H, rows, DH), F32)] * 2 + [jax.ShapeDtypeStruct((n, H, DH, DH), F32)] * 4
        + [jax.ShapeDtypeStruct((H, DH, DH), F32)] * 2,
        compiler_params=_params(("arbitrary",)),
    )(qkv, qkv, gb, gb, s0f, s0b)


def _gdn_bwd(qkv, gb, ssf, ssb, tsf, tsb, do, dsf, dsb, has_q, *, name):
    nb, rows, _ = qkv.shape
    n = rows // PAIR
    qoff = H if has_q else 0

    def body(qf_ref, qb_ref, gf_ref, gr_ref, ssf_ref, ssb_ref, tsf_ref, tsb_ref, dof_ref, dob_ref, dsf0_ref, dsb0_ref,
             dqf_ref, dqb_ref, dgf_ref, dgr_ref, dsf_ref, dsb_ref):
        ts_refs = (tsf_ref, tsb_ref)
        @pl.when(pl.program_id(0) == 0)
        def _():
            dsf_ref[...] = dsf0_ref[...]
            dsb_ref[...] = dsb0_ref[...]

        gtiles = (gf_ref[...], gr_ref[...])
        lane = lax.broadcasted_iota(jnp.int32, (PAIR, DH), 1)

        dirs = ((qf_ref, ssf_ref, dof_ref, dsf_ref, dqf_ref), (qb_ref, ssb_ref, dob_ref, dsb_ref, dqb_ref))

        def heads(hg, carry):
            out = list(carry)
            work = [(hg * HEADS_PER_ITER_BWD + j, d) for j in range(HEADS_PER_ITER_BWD) for d in range(2)]
            revs = [d == 1 for _, d in work]
            s_in, q_in, k_in, v_in, g_in, b_in, ds_out, do_out, inv_in = [], [], [], [], [], [], [], [], []
            for h, d in work:
                q_ref, ss_ref, do_ref, ds_ref, _ = dirs[d]
                s_in.append(ss_ref[0, h])
                inv_in.append(ts_refs[d][0, h])
                q_in.append(q_ref[h] if has_q else None)
                k_in.append(q_ref[qoff + h])
                v_in.append(q_ref[qoff + H + h])
                g_in.append(_lane_col(gtiles[d], d * H + h))
                b_in.append(_lane_col(gtiles[d], 2 * H + d * H + h))
                ds_out.append(ds_ref[h])
                do_out.append(do_ref[h] if has_q else None)
            if has_q:
                _, vjp = jax.vjp(lambda s_, q_, k_, v_, g_, b_: _pairs(s_, q_, k_, v_, g_, b_, revs, inv_in)[:2],
                                 s_in, q_in, k_in, v_in, g_in, b_in)
                ds, dq, dk, dv, dg, db = vjp((ds_out, do_out))
            else:
                _, vjp = jax.vjp(lambda s_, k_, v_, g_, b_: _pairs(s_, q_in, k_, v_, g_, b_, revs, inv_in)[0],
                                 s_in, k_in, v_in, g_in, b_in)
                ds, dk, dv, dg, db = vjp(ds_out)
            for i, (h, d) in enumerate(work):
                dq_ref, ds_ref = dirs[d][4], dirs[d][3]
                ds_ref[h] = ds[i]
                if has_q:
                    dq_ref[h] = dq[i]
                dq_ref[qoff + h], dq_ref[qoff + H + h] = dk[i], dv[i]
                out[d] = (out[d] + jnp.where(lane == d * H + h, dg[i], 0.0)
                          + jnp.where(lane == 2 * H + d * H + h, db[i], 0.0))
            return tuple(out)

        zero = jnp.zeros((PAIR, DH), F32)
        if HEADS_PER_ITER_BWD == H:
            dgf, dgr = heads(0, (zero, zero))
        else:
            dgf, dgr = lax.fori_loop(0, H // HEADS_PER_ITER_BWD, heads, (zero, zero))
        dgf_ref[...] = dgf
        dgr_ref[...] = dgr

    fwd3 = lambda i: (0, n - 1 - i, 0)
    rev3 = lambda i: (0, i, 0)
    state = pl.BlockSpec((H, DH, DH), lambda i: (0, 0, 0))
    saved = pl.BlockSpec((1, H, DH, DH), lambda i: (n - 1 - i, 0, 0, 0))
    gf_spec = pl.BlockSpec((PAIR, DH), lambda i: (n - 1 - i, 0))
    gr_spec = pl.BlockSpec((PAIR, DH), lambda i: (i, 0))
    return pl.pallas_call(
        body, name=name, grid=(n,),
        in_specs=[pl.BlockSpec((nb, PAIR, DH), fwd3), pl.BlockSpec((nb, PAIR, DH), rev3), gf_spec, gr_spec,
                  saved, saved, saved, saved,
                  pl.BlockSpec((H, PAIR, DH), fwd3), pl.BlockSpec((H, PAIR, DH), rev3), state, state],
        out_specs=[pl.BlockSpec((nb, PAIR, DH), fwd3), pl.BlockSpec((nb, PAIR, DH), rev3), gf_spec, gr_spec, state, state],
        out_shape=[jax.ShapeDtypeStruct((nb, rows, DH), F32)] * 2 + [jax.ShapeDtypeStruct((rows, DH), F32)] * 2
        + [jax.ShapeDtypeStruct((H, DH, DH), F32)] * 2,
        compiler_params=_params(("arbitrary",)),
    )(qkv, qkv, gb, gb, ssf, ssb, tsf, tsb, do, do, dsf, dsb)


def _stage1(zb, ua, va, za, o, gon, lng, lnb, wsp, bsp):
    gv = [_gelu(t) for t in va]
    mu = sum(jnp.sum(t, axis=-1, keepdims=True) for t in gv) * (1.0 / D)
    xc = [t - mu for t in gv]
    var = sum(jnp.sum(t * t, axis=-1, keepdims=True) for t in xc) * (1.0 / D)
    rs = lax.rsqrt(var + EPS)
    ya, yb = [], []
    for g in range(H):
        vv = xc[g] * rs * lng[g] + lnb[g]
        s = _dot(wsp[g], vv) + bsp[g]
        ya.append(_gelu(ua[g]) * s * _silu(za[g]))
        r = lax.rsqrt(jnp.mean(o[g] * o[g], axis=-1, keepdims=True) + EPS)
        yb.append(o[g] * r * gon * _silu(zb[g]))
    return ya, yb


def _stage2(ma, mb, ga, gb):
    return jax.nn.sigmoid(ga) * ma + jax.nn.sigmoid(gb) * mb


def _stage3(out, x, tgt, gpost, gate):
    r = out * lax.rsqrt(jnp.mean(out * out, axis=-1, keepdims=True) + EPS) * gpost
    err = x + gate * r - tgt
    return 0.5 * jnp.sum(jnp.mean(err * err, axis=-1, keepdims=True), axis=0, keepdims=True)


def _post(p_rest, o_f, o_b, x, tgt, mod, bmod, gon, lng, lnb, wsp, bspb, wpa, wpb, wout, gpost, *, name):
    rows = x.shape[0]
    n = rows // GM
    lanes = lambda g: slice(g * DH, (g + 1) * DH)
    bdot = lambda a, w_ref: _dot(a.astype(BF16), w_ref[...])
    bdot_t = lambda a, w_ref: _dot(a.astype(BF16), w_ref[...], _NT)

    def body(p_ref, of_ref, ob_ref, x_ref, t_ref, m_ref, bm_ref, gon_ref, lng_ref, lnb_ref, wsp_ref, bsp_ref,
             wpa_ref, wpb_ref, wout_ref, gp_ref,
             loss_ref, dp_ref, do_ref, dx_ref, ya_ref, yb_ref, mg_ref, dma_ref, dmb_ref, dout_ref,
             dvec_ref, dgon_ref, dwsp_ref, dbsp_ref):
        @pl.when(pl.program_id(0) == 0)
        def _():
            loss_ref[...] = jnp.zeros_like(loss_ref)
            dvec_ref[...] = jnp.zeros_like(dvec_ref)
            dgon_ref[...] = jnp.zeros_like(dgon_ref)
            dwsp_ref[...] = jnp.zeros_like(dwsp_ref)
            dbsp_ref[...] = jnp.zeros_like(dbsp_ref)

        piece = lambda blk: [p_ref[:, blk * D + g * DH: blk * D + (g + 1) * DH].astype(F32) for g in range(H)]
        zb, ua, va, za = piece(0), piece(1), piece(2), piece(3)
        o = [of_ref[g] + ob_ref[g] for g in range(H)]
        gon = gon_ref[...]
        lng = [lng_ref[:, lanes(g)] for g in range(H)]
        lnb = [lnb_ref[:, lanes(g)] for g in range(H)]
        wsp = [wsp_ref[g] for g in range(H)]
        bsp = [bsp_ref[g] for g in range(H)]
        (ya, yb), vjp1 = jax.vjp(_stage1, zb, ua, va, za, o, gon, lng, lnb, wsp, bsp)
        y_a, y_b = jnp.concatenate(ya, axis=1), jnp.concatenate(yb, axis=1)
        ma, mb = bdot(y_a, wpa_ref), bdot(y_b, wpb_ref)
        ga, gb = p_ref[:, 4 * D:5 * D].astype(F32), p_ref[:, 5 * D:6 * D].astype(F32)
        merged, vjp2 = jax.vjp(_stage2, ma, mb, ga, gb)
        out = bdot(merged, wout_ref)
        gate = m_ref[:, 2 * D:3 * D] + bm_ref[:, 2 * D:3 * D]
        loss, vjp3 = jax.vjp(_stage3, out, x_ref[...], t_ref[...], gp_ref[...], gate)
        loss_ref[...] += jnp.broadcast_to(loss, loss_ref.shape)

        dout, dx, _, dgpost, dgate = vjp3(jnp.ones((1, 1), F32))
        dx_ref[...] = dx
        dmerged = bdot_t(dout, wout_ref)
        dma, dmb, dga, dgb = vjp2(dmerged)
        dya, dyb = bdot_t(dma, wpa_ref), bdot_t(dmb, wpb_ref)
        dzb, dua, dva, dza, do, dgon, dlng, dlnb, dwsp, dbsp = vjp1(
            ([dya[:, lanes(g)] for g in range(H)], [dyb[:, lanes(g)] for g in range(H)]))

        for blk, dlist in enumerate((dzb, dua, dva, dza)):
            for g in range(H):
                dp_ref[:, blk * D + g * DH: blk * D + (g + 1) * DH] = dlist[g].astype(BF16)
        dp_ref[:, 4 * D:5 * D] = dga.astype(BF16)
        dp_ref[:, 5 * D:6 * D] = dgb.astype(BF16)
        for g in range(H):
            do_ref[g] = do[g]
            dwsp_ref[g] += dwsp[g]
            dbsp_ref[g] += dbsp[g]
            dvec_ref[2:3, lanes(g)] += dlng[g]
            dvec_ref[3:4, lanes(g)] += dlnb[g]
        dvec_ref[0:1, :] += dgpost
        dvec_ref[1:2, :] += dgate
        dgon_ref[0:1, :] += dgon
        ya_ref[...] = y_a.astype(BF16)
        yb_ref[...] = y_b.astype(BF16)
        mg_ref[...] = merged.astype(BF16)
        dma_ref[...] = dma.astype(BF16)
        dmb_ref[...] = dmb.astype(BF16)
        dout_ref[...] = dout.astype(BF16)

    row = lambda w: pl.BlockSpec((GM, w), lambda i: (i, 0))
    heads = pl.BlockSpec((H, GM, DH), lambda i: (0, i, 0))
    full = lambda shape: pl.BlockSpec(shape, lambda i: tuple(0 for _ in shape))
    sds = jax.ShapeDtypeStruct
    return pl.pallas_call(
        body, name=name, grid=(n,),
        in_specs=[row(6 * D), heads, heads, row(D), row(D), full((1, 3 * D)), full((1, 3 * D)), full((1, DH)),
                  full((1, D)), full((1, D)), full((H, GM, GM)), full((H, GM, GM)),
                  full((D, D)), full((D, D)), full((D, D)), full((1, D))],
        out_specs=[full((8, DH)), row(6 * D), heads, row(D)] + [row(D)] * 6
        + [full((8, D)), full((8, DH)), full((H, GM, GM)), full((H, GM, GM))],
        out_shape=[sds((8, DH), F32), sds((rows, 6 * D), BF16), sds((H, rows, DH), F32), sds((rows, D), F32)]
        + [sds((rows, D), BF16)] * 6 + [sds((8, D), F32), sds((8, DH), F32), sds((H, GM, GM), F32), sds((H, GM, GM), F32)],
        compiler_params=_params(("arbitrary",)),
    )(p_rest, o_f, o_b, x, tgt, mod, bmod, gon, lng, lnb, wsp, bspb, wpa, wpb, wout, gpost)


def _silu_rows(c, *, name):
    def body(c_ref, o_ref):
        o_ref[...] = _silu(c_ref[...])

    return pl.pallas_call(body, name=name, out_shape=jax.ShapeDtypeStruct(c.shape, F32))(c)


def _dsilu_mul(c, d, *, name):
    def body(c_ref, d_ref, o_ref):
        _, vjp = jax.vjp(_silu, c_ref[...])
        (o_ref[...],) = vjp(d_ref[...])

    return pl.pallas_call(body, name=name, out_shape=jax.ShapeDtypeStruct(c.shape, F32))(c, d)


def _adamw(w, g, m, v, *, name):
    rows, cols = w.shape[-2:]
    t = _tile(rows if g.ndim == 2 else g.shape[1])
    c1 = 1.0 / (1.0 - ADAM_B1 ** ADAM_STEP)
    c2 = 1.0 / (1.0 - ADAM_B2 ** ADAM_STEP)

    def body(w_ref, g_ref, m_ref, v_ref, go_ref, d_ref, mo_ref, vo_ref):
        blk = lambda r: r[...].reshape(t, cols)
        gv = blk(g_ref)
        mn = ADAM_B1 * blk(m_ref) + (1.0 - ADAM_B1) * gv
        vn = ADAM_B2 * blk(v_ref) + (1.0 - ADAM_B2) * (gv * gv)
        delta = -ADAM_LR * ((mn * c1) / (jnp.sqrt(vn * c2) + ADAM_EPS) + ADAM_WD * blk(w_ref))
        for ref, val in ((go_ref, gv), (d_ref, delta), (mo_ref, mn), (vo_ref, vn)):
            ref[...] = val.reshape(ref.shape)

    tile = (pl.BlockSpec((1, t, cols), lambda i: (0, i, 0)) if w.ndim == 3 else pl.BlockSpec((t, cols), lambda i: (i, 0)))
    if g.ndim == 3:
        per = g.shape[1] // t
        g_tile = pl.BlockSpec((1, t, cols), lambda i: (i // per, i % per, 0))
    else:
        g_tile = pl.BlockSpec((t, cols), lambda i: (i, 0))
    return pl.pallas_call(
        body, name=name, grid=(rows // t,),
        in_specs=[tile, g_tile, tile, tile], out_specs=[tile] * 4,
        out_shape=[jax.ShapeDtypeStruct(w.shape, F32)] * 4,
        compiler_params=_params(("parallel",)),
    )(w, g, m, v)


def _add_sibling(full, got, where, out_dtype, *, name):
    s, rows, cols = full.shape
    hr = rows // 2
    t = _tile(hr)
    nt = hr // t

    def body(w_ref, a_ref, b_ref, o_ref):
        o_ref[...] = (a_ref[...].astype(F32) + b_ref[...].astype(F32)).astype(out_dtype)

    tile = pl.BlockSpec((1, t, cols), lambda j, i, w: (j, i, 0))
    return pl.pallas_call(
        body, name=name,
        grid_spec=pltpu.PrefetchScalarGridSpec(
            num_scalar_prefetch=1, grid=(s, nt),
            in_specs=[pl.BlockSpec((1, t, cols), lambda j, i, w: (j, w[0] * nt + i, 0)), tile], out_specs=tile),
        out_shape=jax.ShapeDtypeStruct((s, hr, cols), out_dtype), compiler_params=_params(("parallel", "parallel")),
    )(where, full, got)


def _sum_chips(own, slots, where, n_out, which, *, name):
    _, hr, cols = own.shape
    t = _tile(hr)

    def body(w_ref, a_ref, s_ref, o_ref):
        f = lambda v: v.astype(F32)
        o_ref[0] = ((f(a_ref[0]) + f(s_ref[0])) + f(s_ref[1])) + f(s_ref[2])

    return pl.pallas_call(
        body, name=name,
        grid_spec=pltpu.PrefetchScalarGridSpec(
            num_scalar_prefetch=1, grid=(hr // t,),
            in_specs=[pl.BlockSpec((1, t, cols), lambda i, w: (w[1], i, 0)),
                      pl.BlockSpec((N_CHIPS - 1, t, cols), lambda i, w: (0, i, 0))],
            out_specs=pl.BlockSpec((1, t, cols), lambda i, w: (w[which], i, 0))),
        out_shape=jax.ShapeDtypeStruct((n_out, hr, cols), F32), compiler_params=_params(("parallel",)),
    )(where, own, slots)


def _local_step(x, c, ctx, c_ctx, tgt, w_mod, b_mod, g_pre, g_post, w_qkv, w_ab, w_rest, w_conv, a_log, dt_bias,
                g_onorm, gm_ln_g, gm_ln_b, w_sp, b_sp, late_weights, on_large_gradients=None, run_after=None):
    rows, rows_c = x.shape[0], ctx.shape[0]
    cc = jnp.zeros((16, D), F32).at[0].set(c[0]).at[1].set(c_ctx)
    scc = _silu_rows(cc, name="silu_cond")
    mod = _mm(scc, w_mod, b_sections=True, after=run_after, name="mod_fwd")
    mod_x, mod_c = mod[0:1], mod[1:2]
    avec = jnp.zeros((1, DH), F32).at[0, :2 * H].set(a_log.reshape(-1))
    dvec = jnp.zeros((1, DH), F32).at[0, :2 * H].set(dt_bias.reshape(-1))
    bspb = jnp.broadcast_to(b_sp[:, :, None], (H, GM, GM))
    w_kv, wconv_kv = w_qkv[:, D:], w_conv[:, D:]

    h_c = _norm_fwd(ctx, g_pre, mod_c, b_mod, name="norm_fwd_ctx")
    pc_kv = _mm(h_c, w_kv, name="inproj_ctx_kv")
    pc_ab = _mm(h_c, w_ab, name="inproj_ctx_ab")
    kv_c = _prep_fwd(pc_kv, wconv_kv, H, name="prep_fwd_ctx")
    gb_c = _gates_fwd(pc_ab, avec, dvec, name="gates_fwd_ctx")
    s_zero = jnp.zeros((H, DH, DH), F32)
    _, _, ssf_c, ssb_c, tsf_c, tsb_c, s_f, s_b = _gdn_fwd(kv_c, gb_c, s_zero, s_zero, False, name="gdn_fwd_ctx")

    h_x = _norm_fwd(x, g_pre, mod_x, b_mod, name="norm_fwd_x")
    p_qkv = _mm(h_x, w_qkv, name="inproj_qkv")
    p_ab = _mm(h_x, w_ab, name="inproj_ab")
    p_rest = _mm(h_x, w_rest, out_dtype=BF16, name="inproj_rest")
    qkv = _prep_fwd(p_qkv, w_conv, 2 * H, name="prep_fwd_x")
    gb_x = _gates_fwd(p_ab, avec, dvec, name="gates_fwd_x")
    o_f, o_b, ssf, ssb, tsf, tsb, _, _ = _gdn_fwd(qkv, gb_x, s_f, s_b, True, name="gdn_fwd_x")

    w_pa, w_pb, w_out = late_weights(o_f)
    (loss_acc, dp_rest, do, dx_res, ya, yb, mg, dma, dmb, dout, dvec_post, dgon, dwsp, dbspb) = _post(
        p_rest, o_f, o_b, x, tgt, mod_x, b_mod, g_onorm, gm_ln_g, gm_ln_b, w_sp, bspb, w_pa, w_pb, w_out, g_post,
        name="post")
    g = {}
    g["w_pa"] = _mm(ya, dma, ta=True, name="dw_pa")
    g["w_pb"] = _mm(yb, dmb, ta=True, name="dw_pb")
    g["w_out"] = _mm(mg, dout, ta=True, name="dw_out")

    zeros_s = jnp.zeros((H, DH, DH), F32)
    dq_f, dq_b, dg_f, dg_b, ds0_f, ds0_b = _gdn_bwd(qkv, gb_x, ssf, ssb, tsf, tsb, do, zeros_s, zeros_s, True,
                                                    name="gdn_bwd_x")
    dp_qkv, dwc_x = _prep_bwd(p_qkv, w_conv, dq_f, dq_b, 2 * H, name="prep_bwd_x")
    dp_ab, dav_x, ddv_x = _gates_bwd(p_ab, avec, dvec, dg_f, dg_b, name="gates_bwd_x")
    dkv_f, dkv_b, dgc_f, dgc_b, _, _ = _gdn_bwd(kv_c, gb_c, ssf_c, ssb_c, tsf_c, tsb_c, jnp.zeros((H, rows_c, DH), F32),
                                                 ds0_f, ds0_b, False, name="gdn_bwd_ctx")
    dpc_kv, dwc_c = _prep_bwd(pc_kv, wconv_kv, dkv_f, dkv_b, H, name="prep_bwd_ctx")
    dpc_ab, dav_c, ddv_c = _gates_bwd(pc_ab, avec, dvec, dgc_f, dgc_b, name="gates_bwd_ctx")

    dw_kv_c = _mm(h_c, dpc_kv, ta=True, name="dw_kv_ctx")
    dw_qkv = _mm(h_x, dp_qkv, ta=True, tn=D, add=dw_kv_c, add_from=1, name="dw_qkv")
    dw_ab_c = _mm(h_c, dpc_ab, ta=True, name="dw_ab_ctx")
    dw_ab = _mm(h_x, dp_ab, ta=True, add=dw_ab_c, name="dw_ab")
    dw_rest = _mm(h_x, dp_rest, ta=True, name="dw_rest")
    g["w_in"] = (dw_qkv, dw_ab[:, :4 * H], dw_rest)
    token = on_large_gradients(g) if on_large_gradients is not None else None
    dh_parts = [_mm(dp_qkv, w_qkv, tb=True, tk=3 * D // 2, after=token, name="dh_qkv"),
                _mm(dp_rest, w_rest, tb=True, tk=3 * D // 2, after=token, name="dh_rest"),
                _mm(dp_ab, w_ab, tb=True, after=token, name="dh_ab")]
    grad_x, dgpre_x, dm_x = _norm_bwd(x, g_pre, mod_x, b_mod, dh_parts, dx_res, name="norm_bwd_x")
    dhc_parts = [_mm(dpc_kv, w_kv, tb=True, after=token, name="dhc_kv"),
                 _mm(dpc_ab, w_ab, tb=True, after=token, name="dhc_ab")]
    _, dgpre_c, dm_c = _norm_bwd(ctx, g_pre, mod_c, b_mod, dhc_parts, None, name="norm_bwd_ctx")

    dm_x = dm_x.at[:, 2 * D:].add(dvec_post[1:2])
    dmod = jnp.zeros((16, 3 * D), F32).at[0].set(dm_x[0]).at[1].set(dm_c[0])
    g["mod_factors"] = (scc[0], dm_x[0])
    g["dm_ctx"] = dm_c[0]
    g["silu_c_ctx"] = scc[1:2]
    dcc = _mm(dmod, w_mod, tb=True, b_sections=True, name="dcc")
    g["c_ctx"] = _dsilu_mul(cc[:8], dcc[:8], name="dc_ctx")[1]
    g["b_mod"] = dm_x + dm_c
    g["g_pre"] = dgpre_x + dgpre_c
    g["g_post"] = dvec_post[0:1]
    g["gm_ln_g"], g["gm_ln_b"] = dvec_post[2:3], dvec_post[3:4]
    g["g_onorm"] = dgon[0:1]
    g["w_sp"] = dwsp
    g["b_sp"] = jnp.sum(dbspb, axis=-1)
    g["w_conv"] = dwc_x.at[:, D:].add(dwc_c)
    g["a_log"] = (dav_x + dav_c)[0, :2 * H].reshape(2, H)
    g["dt_bias"] = (ddv_x + ddv_c)[0, :2 * H].reshape(2, H)
    return loss_acc[0, 0], grad_x, g


ANY = pl.BlockSpec(memory_space=pl.ANY)


def _place():
    x, y, c = lax.axis_index("x"), lax.axis_index("y"), lax.axis_index("c")
    chips = [(1 - x, y), (x, 1 - y), (1 - x, 1 - y)]
    return x, y, c, (x, y, 1 - c), chips


def _gather_shards(big, small, *, name):
    nb = len(big)

    def body(*refs):
        ins, sm_in = refs[:nb], refs[nb]
        outs, sm_out = refs[nb + 1:2 * nb + 1], refs[2 * nb + 1]
        send, recv = refs[2 * nb + 2:]
        x, y, c, sibling, chips = _place()
        mine = 2 * x + y

        def half(a, shard, hc):
            hr = big[a].shape[0] // 2
            return outs[a].at[shard, pl.ds(hc * hr, hr), :]

        def remote(k, src, dst, to):
            return pltpu.make_async_remote_copy(src_ref=src, dst_ref=dst, send_sem=send.at[k], recv_sem=recv.at[k],
                                                device_id=to, device_id_type=MESH)

        sends = []
        for a in range(nb):
            hr = big[a].shape[0] // 2
            for j, chip in enumerate(chips):
                sends.append(remote(a * 3 + j, ins[a].at[pl.ds(c * hr, hr), :], half(a, mine, c), (*chip, c)))
        for j, chip in enumerate(chips):
            sends.append(remote(nb * 3 + j, sm_in, sm_out.at[mine], (*chip, c)))
        for cp in sends:
            cp.start()
        base = nb * 3 + 3
        passed = []
        for a in range(nb):
            for j, (px, py) in enumerate(chips):
                theirs = 2 * px + py
                remote(a * 3 + j, half(a, theirs, c), half(a, theirs, c), sibling).wait_recv()
                fw = remote(base + a * 3 + j, half(a, theirs, c), half(a, theirs, c), sibling)
                fw.start()
                passed.append(fw)
        for a in range(nb):
            for j, (px, py) in enumerate(chips):
                theirs = 2 * px + py
                remote(base + a * 3 + j, half(a, theirs, 1 - c), half(a, theirs, 1 - c), sibling).wait_recv()
        for j, (px, py) in enumerate(chips):
            remote(nb * 3 + j, sm_in, sm_out.at[2 * px + py], sibling).wait_recv()
        for cp in sends + passed:
            cp.wait_send()

    n_remote = 2 * nb * 3 + 3
    outs = pl.pallas_call(
        body, name=name, in_specs=[ANY] * (nb + 1), out_specs=[ANY] * (nb + 1),
        out_shape=[jax.ShapeDtypeStruct((N_CHIPS,) + a.shape, a.dtype) for a in big + [small]],
        scratch_shapes=[pltpu.SemaphoreType.DMA((n_remote,)), pltpu.SemaphoreType.DMA((n_remote,))],
    )(*big, small)
    return outs[:nb], outs[nb]


def _gather_late_start(arrs, after, *, name):
    na = len(arrs)

    def body(*refs):
        ins, land = refs[:na], refs[na:2 * na]
        send, recv = refs[2 * na + 1], refs[2 * na + 2]
        token = refs[-1]
        x, y, c, _, chips = _place()
        for a in range(na):
            hr = arrs[a].shape[0] // 2
            for j, (px, py) in enumerate(chips):
                for other in range(2):
                    k = (a * 3 + j) * 2 + other
                    _remote(ins[a].at[pl.ds(c * hr, hr), :], land[a].at[2 * x + y, pl.ds(c * hr, hr), :],
                            send.at[k], recv.at[k], (px, py, c ^ other)).start()
        token[...] = jnp.zeros_like(token)

    lands = [((N_CHIPS,) + a.shape, a.dtype) for a in arrs]
    held = lambda a: pltpu.with_memory_space_constraint(a, pltpu.HBM)
    outs = pl.pallas_call(
        body, name=name, in_specs=[HBM] * (2 * na) + [ANY],
        out_specs=[SEM, SEM] + [HBM] * (2 * na) + [pl.BlockSpec(memory_space=pltpu.VMEM)],
        out_shape=[pltpu.SemaphoreType.DMA((6 * na,)), pltpu.SemaphoreType.DMA((6 * na,))]
        + [pltpu.HBM(a.shape, a.dtype) for a in arrs] + [pltpu.HBM(s, d) for s, d in lands]
        + [jax.ShapeDtypeStruct((8, 128), F32)],
        input_output_aliases={i: 2 + i for i in range(2 * na)},
        compiler_params=pltpu.CompilerParams(has_side_effects=pltpu.SideEffectType.DATAFLOW_SIDE_EFFECTING),
    )(*[held(a) for a in arrs], *[held(lax.empty(s, d)) for s, d in lands], after)
    return outs[0], outs[1], outs[2:2 + na], outs[2 + na:2 + 2 * na], outs[-1]


def _gather_late_wait(send, recv, thru, land, after, *, name):
    na = len(thru)

    def body(*refs):
        ins, slots = refs[:na], refs[na:2 * na]
        send_sem, recv_sem = refs[2 * na], refs[2 * na + 1]
        x, y, c, _, chips = _place()
        for a in range(na):
            hr = thru[a].shape[0] // 2
            for j, (px, py) in enumerate(chips):
                for other in range(2):
                    k = (a * 3 + j) * 2 + other
                    half = c ^ other
                    cp = _remote(ins[a].at[pl.ds(c * hr, hr), :], slots[a].at[2 * px + py, pl.ds(half * hr, hr), :],
                                 send_sem.at[k], recv_sem.at[k], (px, py, half))
                    cp.wait_send()
                    cp.wait_recv()

    outs = pl.pallas_call(
        body, name=name, in_specs=[HBM] * (2 * na) + [SEM, SEM, ANY], out_specs=[HBM] * (2 * na),
        out_shape=[pltpu.HBM(a.shape, a.dtype) for a in list(thru) + list(land)],
        input_output_aliases={i: i for i in range(2 * na)},
        compiler_params=pltpu.CompilerParams(has_side_effects=pltpu.SideEffectType.DATAFLOW_SIDE_EFFECTING),
    )(*thru, *land, send, recv, after)
    return outs[na:]


def _row_chunks(rows, row_bytes, align=8):
    n = max(1, min(rows // align, -(-rows * row_bytes // DMA_CHUNK_BYTES)))
    per = -(-(-(-rows // n)) // align) * align
    return [(r, min(per, rows - r)) for r in range(0, rows, per)]


def _remote(src, dst, send, recv, to):
    return pltpu.make_async_remote_copy(src_ref=src, dst_ref=dst, send_sem=send, recv_sem=recv, device_id=to,
                                        device_id_type=MESH)


def _exchange_halves(arrs, *, name):
    na = len(arrs)

    def body(*refs):
        ins, got = refs[:na], refs[na:2 * na]
        send, recv = refs[2 * na:]
        x, y, c, sibling, _ = _place()
        for a in range(na):
            ns, rows, cols = arrs[a].shape
            hr = rows // 2
            for s in range(ns):
                for r0, nr in _row_chunks(hr, cols * arrs[a].dtype.itemsize, 16):
                    _remote(ins[a].at[s, pl.ds((1 - c) * hr + r0, nr), :], got[a].at[s, pl.ds(r0, nr), :],
                            send.at[a], recv.at[a], sibling).start()
        for a in range(na):
            hr = arrs[a].shape[1] // 2
            _remote(ins[a].at[:, pl.ds((1 - c) * hr, hr), :], got[a], send.at[a], recv.at[a], sibling).wait()

    return pl.pallas_call(
        body, name=name, in_specs=[ANY] * na, out_specs=[ANY] * na,
        out_shape=[jax.ShapeDtypeStruct((N_CHIPS, a.shape[1] // 2, a.shape[2]), a.dtype) for a in arrs],
        scratch_shapes=[pltpu.SemaphoreType.DMA((na,)), pltpu.SemaphoreType.DMA((na,))],
    )(*arrs)


def _scatter_sections(arrs, *, name):
    na = len(arrs)

    def body(*refs):
        ins, outs = refs[:na], refs[na:2 * na]
        send, recv = refs[2 * na:]
        x, y, c, _, chips = _place()
        for a in range(na):
            _, hr, cols = arrs[a].shape
            for j, (px, py) in enumerate(chips):
                for r0, nr in _row_chunks(hr, cols * arrs[a].dtype.itemsize, 16):
                    _remote(ins[a].at[2 * px + py, pl.ds(r0, nr), :], outs[a].at[j, pl.ds(r0, nr), :],
                            send.at[a * 3 + j], recv.at[a * 3 + j], (px, py, c)).start()
        for a in range(na):
            for j, (px, py) in enumerate(chips):
                _remote(ins[a].at[2 * px + py], outs[a].at[j], send.at[a * 3 + j], recv.at[a * 3 + j], (px, py, c)).wait()

    return pl.pallas_call(
        body, name=name, in_specs=[ANY] * na, out_specs=[ANY] * na,
        out_shape=[jax.ShapeDtypeStruct((N_CHIPS - 1,) + a.shape[1:], a.dtype) for a in arrs],
        scratch_shapes=[pltpu.SemaphoreType.DMA((3 * na,)), pltpu.SemaphoreType.DMA((3 * na,))],
    )(*arrs)


HBM = pl.BlockSpec(memory_space=pltpu.HBM)
SEM = pl.BlockSpec(memory_space=pltpu.SEMAPHORE)


def _scatter_start(arrs, *, name):
    na = len(arrs)

    def body(*refs):
        ins, land = refs[:na], refs[na:2 * na]
        send, recv = refs[2 * na], refs[2 * na + 1]
        token = refs[-1]
        x, y, c, _, chips = _place()
        for a in range(na):
            _, hr, cols = arrs[a].shape
            for j, (px, py) in enumerate(chips):
                for r0, nr in _row_chunks(hr, cols * arrs[a].dtype.itemsize, 16):
                    _remote(ins[a].at[2 * px + py, pl.ds(r0, nr), :], land[a].at[j, pl.ds(r0, nr), :],
                            send.at[a * 3 + j], recv.at[a * 3 + j], (px, py, c)).start()
        token[...] = jnp.zeros_like(token)

    slots = [((N_CHIPS - 1,) + a.shape[1:], a.dtype) for a in arrs]
    held = lambda a: pltpu.with_memory_space_constraint(a, pltpu.HBM)
    outs = pl.pallas_call(
        body, name=name, in_specs=[HBM] * (2 * na),
        out_specs=[SEM, SEM] + [HBM] * (2 * na) + [pl.BlockSpec(memory_space=pltpu.VMEM)],
        out_shape=[pltpu.SemaphoreType.DMA((3 * na,)), pltpu.SemaphoreType.DMA((3 * na,))]
        + [pltpu.HBM(a.shape, a.dtype) for a in arrs] + [pltpu.HBM(s, d) for s, d in slots]
        + [jax.ShapeDtypeStruct((8, 128), F32)],
        input_output_aliases={i: 2 + i for i in range(2 * na)},
        compiler_params=pltpu.CompilerParams(has_side_effects=pltpu.SideEffectType.DATAFLOW_SIDE_EFFECTING),
    )(*[held(a) for a in arrs], *[held(lax.empty(s, d)) for s, d in slots])
    return outs[0], outs[1], outs[2:2 + na], outs[2 + na:2 + 2 * na], outs[-1]


def _scatter_wait(send, recv, thru, land, after, *, name):
    na = len(thru)

    def body(*refs):
        ins, slots = refs[:na], refs[na:2 * na]
        send_sem, recv_sem = refs[2 * na], refs[2 * na + 1]
        x, y, c, _, chips = _place()
        for a in range(na):
            for j, (px, py) in enumerate(chips):
                cp = _remote(ins[a].at[2 * px + py], slots[a].at[j], send_sem.at[a * 3 + j], recv_sem.at[a * 3 + j],
                             (px, py, c))
                cp.wait_send()
                cp.wait_recv()

    outs = pl.pallas_call(
        body, name=name, in_specs=[HBM] * (2 * na) + [SEM, SEM, ANY], out_specs=[HBM] * (2 * na),
        out_shape=[pltpu.HBM(a.shape, a.dtype) for a in list(thru) + list(land)],
        input_output_aliases={i: i for i in range(2 * na)},
        compiler_params=pltpu.CompilerParams(has_side_effects=pltpu.SideEffectType.DATAFLOW_SIDE_EFFECTING),
    )(*thru, *land, send, recv, after)
    return outs[:na], outs[na:]


def _finish_reduce(big, small, *, name):
    nb = len(big)

    def body(*refs):
        outs, sm = refs[nb + 1:2 * nb + 1], refs[2 * nb + 1]
        send, recv = refs[2 * nb + 2:]
        x, y, c, sibling, chips = _place()
        blk = lambda px, py, pc: sm.at[4 * px + 2 * py + pc]
        for a in range(nb):
            _, hr, cols = big[a].shape
            for r0, nr in _row_chunks(hr, cols * 4):
                _remote(outs[a].at[c, pl.ds(r0, nr), :], outs[a].at[c, pl.ds(r0, nr), :], send.at[a], recv.at[a],
                        sibling).start()
        first = [_remote(blk(x, y, c), blk(x, y, c), send.at[nb], recv.at[nb], sibling)]
        first += [_remote(blk(x, y, c), blk(x, y, c), send.at[nb + 1 + j], recv.at[nb + 1 + j], (*chip, c))
                  for j, chip in enumerate(chips)]
        for cp in first:
            cp.start()
        passed = []
        for j, (px, py) in enumerate(chips):
            _remote(blk(px, py, c), blk(px, py, c), send.at[nb + 1 + j], recv.at[nb + 1 + j], sibling).wait_recv()
            fw = _remote(blk(px, py, c), blk(px, py, c), send.at[nb + 4 + j], recv.at[nb + 4 + j], sibling)
            fw.start()
            passed.append(fw)
        for a in range(nb):
            _remote(outs[a].at[c], outs[a].at[1 - c], send.at[a], recv.at[a], sibling).wait()
        _remote(blk(x, y, c), blk(x, y, 1 - c), send.at[nb], recv.at[nb], sibling).wait_recv()
        for j, (px, py) in enumerate(chips):
            _remote(blk(px, py, c), blk(px, py, 1 - c), send.at[nb + 4 + j], recv.at[nb + 4 + j], sibling).wait_recv()
        for cp in first + passed:
            cp.wait_send()

    n_remote = nb + 7
    arrs = list(big) + [small]
    outs = pl.pallas_call(
        body, name=name, in_specs=[ANY] * (nb + 1), out_specs=[ANY] * (nb + 1),
        out_shape=[jax.ShapeDtypeStruct(a.shape, F32) for a in arrs],
        input_output_aliases={i: i for i in range(nb + 1)},
        scratch_shapes=[pltpu.SemaphoreType.DMA((n_remote,)), pltpu.SemaphoreType.DMA((n_remote,))],
    )(*arrs)
    return outs[:nb], outs[nb]


def _reduce_start(sectioned, where):
    got = _exchange_halves(sectioned, name="rs_exchange_halves")
    chip_sum = [_add_sibling(a, g, where, BF16, name=f"rs_add_sibling_{i}")
                for i, (a, g) in enumerate(zip(sectioned, got))]
    send, recv, thru, land, token = _scatter_start(chip_sum, name="rs_scatter_start")
    return (send, recv, thru, land), token


def _reduce_finish(handle, small, where, after):
    send, recv, thru, land = handle
    got = _exchange_halves([small], name="rs_exchange_small")
    small_sum = _add_sibling(small, got[0], where, F32, name="rs_add_sibling_small")
    small_slots = _scatter_sections([small_sum], name="rs_scatter_small")[0]
    own, slots = _scatter_wait(send, recv, thru, land, after, name="rs_scatter_wait")
    red = [_sum_chips(p, s, where, 2, 0, name=f"rs_sum_chips_{i}") for i, (p, s) in enumerate(zip(own, slots))]
    red_small = _sum_chips(small_sum, small_slots, where, 2 * N_CHIPS, 2, name="rs_sum_chips_small")
    big, sm = _finish_reduce(red, red_small, name="rs_finish")
    return big, sm.reshape(-1, sm.shape[-1])


def kernel(x, c, ctx, c_ctx, w_mod, b_mod, g_pre, g_post, w_in, w_conv, a_log, dt_bias, g_onorm, gm_ln_g, gm_ln_b, w_sp, b_sp, w_pa, w_pb, w_out, loss_target, m_c_ctx, m_w_mod, m_b_mod, m_g_pre, m_g_post, m_w_in, m_w_conv, m_a_log, m_dt_bias, m_g_onorm, m_gm_ln_g, m_gm_ln_b, m_w_sp, m_b_sp, m_w_pa, m_w_pb, m_w_out, v_c_ctx, v_w_mod, v_b_mod, v_g_pre, v_g_post, v_w_in, v_w_conv, v_a_log, v_dt_bias, v_g_onorm, v_gm_ln_g, v_gm_ln_b, v_w_sp, v_b_sp, v_w_pa, v_w_pb, v_w_out):
    names = ["c_ctx", "w_mod", "b_mod", "g_pre", "g_post", "w_in", "w_conv", "a_log", "dt_bias", "g_onorm", "gm_ln_g",
             "gm_ln_b", "w_sp", "b_sp", "w_pa", "w_pb", "w_out"]
    w = dict(zip(names, (c_ctx, w_mod, b_mod, g_pre, g_post, w_in, w_conv, a_log, dt_bias, g_onorm, gm_ln_g, gm_ln_b,
                         w_sp, b_sp, w_pa, w_pb, w_out)))
    m = dict(zip(names, (m_c_ctx, m_w_mod, m_b_mod, m_g_pre, m_g_post, m_w_in, m_w_conv, m_a_log, m_dt_bias, m_g_onorm,
                         m_gm_ln_g, m_gm_ln_b, m_w_sp, m_b_sp, m_w_pa, m_w_pb, m_w_out)))
    v = dict(zip(names, (v_c_ctx, v_w_mod, v_b_mod, v_g_pre, v_g_post, v_w_in, v_w_conv, v_a_log, v_dt_bias, v_g_onorm,
                         v_gm_ln_g, v_gm_ln_b, v_w_sp, v_b_sp, v_w_pa, v_w_pb, v_w_out)))
    xy = 2 * lax.axis_index("x") + lax.axis_index("y")
    where = jnp.stack([lax.axis_index("c"), xy, 2 * xy + lax.axis_index("c")]).astype(jnp.int32)

    shards = [a[0].astype(BF16) for a in (w_mod, w_in)]
    gathered, wconv_all = _gather_shards(shards, w_conv[0], name="gather_weights")
    own = lambda full, shard: lax.dynamic_update_slice(full, shard[None], (xy, 0, 0))
    wm_all, win_all = [own(f, s) for f, s in zip(gathered, shards)]
    wconv_all = own(wconv_all, w_conv[0])
    late_shards = [a[0].astype(BF16) for a in (w_pa, w_pb, w_out)]
    late = _gather_late_start(late_shards, wconv_all, name="gather_late_start")

    def late_weights(after):
        lands = _gather_late_wait(*late[:4], after, name="gather_late_wait")
        return [own(f, s).reshape(D, D) for f, s in zip(lands, late_shards)]
    w_conv_f = jnp.concatenate([wconv_all[s] for s in range(N_CHIPS)], axis=1)
    shard_cols = IN_COLS // N_CHIPS
    cut = OFF_A - shard_cols
    assert 0 < cut and cut + 4 * H < shard_cols
    w_qkv = jnp.concatenate([win_all[0], win_all[1][:, :cut]], axis=1)
    w_ab = jnp.pad(win_all[1][:, cut:cut + 4 * H], ((0, 0), (0, DH - 4 * H)))
    w_rest = jnp.concatenate([win_all[1][:, cut + 4 * H:], win_all[2], win_all[3]], axis=1)

    blk = D // N_CHIPS
    rest_cut = shard_cols - cut - 4 * H
    big_names = ("w_in", "w_pa", "w_pb", "w_out")
    in_flight = []

    def start_reduce(grads):
        dw_qkv, dw_ab, dw_rest = grads["w_in"]
        g_win = jnp.stack([dw_qkv[:, :shard_cols],
                           jnp.concatenate([dw_qkv[:, shard_cols:], dw_ab, dw_rest[:, :rest_cut]], axis=1),
                           dw_rest[:, rest_cut:rest_cut + shard_cols], dw_rest[:, rest_cut + shard_cols:]]).astype(BF16)
        sectioned = [g_win] + [grads[k].reshape(N_CHIPS, blk, D) for k in ("w_pa", "w_pb", "w_out")]
        handle, token = _reduce_start(sectioned, where)
        in_flight.append(handle)
        return token

    loss_local, grad_x, g = _local_step(
        x[0], c, ctx[0], c_ctx, loss_target[0], wm_all, b_mod, g_pre, g_post, w_qkv, w_ab, w_rest,
        w_conv_f, a_log[0], dt_bias[0], g_onorm, gm_ln_g, gm_ln_b, w_sp[0], b_sp[0],
        late_weights, on_large_gradients=start_reduce, run_after=late[4])
    g["loss"] = loss_local
    g["_pad"] = jnp.zeros((SMALL_LAYOUT["_pad"][1],), F32)
    g["mod_factors"] = lax.dynamic_update_slice(jnp.zeros((2 * N_CHIPS, 4 * D), F32),
                                                jnp.concatenate(g["mod_factors"])[None], (where[2], 0))
    tail = jnp.zeros((N_CHIPS * SMALL_ROWS * 128 - sum(s for _, s in SMALL_LAYOUT.values()),), F32)
    flat = jnp.concatenate([g[k].reshape(-1) for k in SMALL_LAYOUT] + [tail])
    reduced, gr_small = _reduce_finish(in_flight[0], flat.reshape(N_CHIPS, SMALL_ROWS, 128), where, g["b_mod"])
    gr_tiny = gr_small[TINY_ROW0:]

    def entry(arr, k):
        off, size = SMALL_LAYOUT[k]
        row, col = off // 128 - TINY_ROW0, off % 128
        return arr[row:row + size // 128].reshape(-1) if size >= 128 else arr[row, col:col + size]

    factors = entry(gr_tiny, "mod_factors").reshape(2 * N_CHIPS, 4 * D)
    pad_rows = lambda a: jnp.pad(a, ((0, 16 - a.shape[0]), (0, 0)))
    lhs = pad_rows(jnp.concatenate([factors[:, :D], g["silu_c_ctx"]], axis=0))
    rhs = pad_rows(jnp.concatenate([factors[:, D:], entry(gr_tiny, "dm_ctx")[None]], axis=0))
    mod_cols = 3 * D // N_CHIPS
    gr_wm = _mm(lhs, lax.dynamic_slice(rhs, (0, xy * mod_cols), (16, mod_cols)), ta=True, name="dw_mod")

    res = {k: _adamw(w[k], gr, m[k], v[k], name=f"adamw_{k}") for k, gr in zip(big_names[1:], reduced[1:])}
    flip = lambda a: jnp.swapaxes(a, 1, 2)
    res["w_in"] = [flip(r) for r in _adamw(flip(w_in), flip(reduced[0].reshape(w_in.shape)), flip(m_w_in), flip(v_w_in),
                                           name="adamw_w_in")]
    res["w_mod"] = _adamw(w_mod[0], gr_wm, m_w_mod[0], v_w_mod[0], name="adamw_w_mod")
    res["w_sp"] = _adamw(w_sp.reshape(-1, 128), gr_small, m_w_sp.reshape(-1, 128), v_w_sp.reshape(-1, 128),
                         name="adamw_w_sp")
    tiny = [k for k, (off, _) in SMALL_LAYOUT.items() if TINY_ROW0 <= off // 128 < TINY_ROW0 + TINY_ROWS]
    tiny_names = [k for k in tiny if k not in ("loss", "_pad")]

    def pack(src):
        parts = [src[k].reshape(-1) if k in tiny_names else jnp.zeros((SMALL_LAYOUT[k][1],), F32) for k in tiny]
        return jnp.concatenate(parts).reshape(TINY_ROWS, 128)

    tiny_res = _adamw(pack(w), gr_tiny[:TINY_ROWS], pack(m), pack(v), name="adamw_tiny")
    for k in tiny_names:
        res[k] = [entry(r, k) for r in tiny_res]
    g_conv = lax.dynamic_slice(entry(gr_tiny, "w_conv").reshape(3, 3 * D), (0, xy * (3 * D // N_CHIPS)),
                               (3, 3 * D // N_CHIPS))
    conv_res = _adamw(jnp.pad(w_conv[0], ((0, 5), (0, 0))), jnp.pad(g_conv, ((0, 5), (0, 0))),
                      jnp.pad(m_w_conv[0], ((0, 5), (0, 0))), jnp.pad(v_w_conv[0], ((0, 5), (0, 0))), name="adamw_w_conv")
    res["w_conv"] = [r[:3] for r in conv_res]
    res = {k: [r.reshape(w[k].shape) for r in res[k]] for k in names}

    out = [entry(gr_tiny, "loss").reshape(()), grad_x[None]]
    for i in range(4):
        out += [res[k][i] for k in names]
    return tuple(out)
```

```python
import functools

import jax
import jax.numpy as jnp
from jax import lax
from jax.experimental import pallas as pl
from jax.experimental.pallas import tpu as pltpu

F32 = jnp.float32
BF16 = jnp.bfloat16
HI = lax.Precision.HIGHEST
MESH = pl.DeviceIdType.MESH

D = 1024
H = 8
DH = 128
CH = 64
LOG_CH = 6
PAIR = 2 * CH
GM = 128
assert 1 << LOG_CH == CH and PAIR == DH
PREC_POWERS = ((lax.Precision.HIGH, lax.Precision.HIGH),) * 3 + ((None, None),) * 2
assert len(PREC_POWERS) == LOG_CH - 1
EPS = 1e-6
N_CHIPS = 4
OFF_A = 3 * D
OFF_ZB = OFF_A + 4 * H
IN_COLS = OFF_ZB + 6 * D
VMEM_LIMIT_V7X = 56 * 1024 * 1024
DMA_CHUNK_BYTES = 2 * 1024 * 1024

ADAM_LR, ADAM_B1, ADAM_B2, ADAM_EPS, ADAM_WD, ADAM_STEP = 0.001, 0.9, 0.999, 1e-08, 0.01, 10

SMALL_LAYOUT = {}
_off = 0
for _n, _s in (("w_sp", H * GM * GM), ("c_ctx", D), ("b_mod", 3 * D), ("g_pre", D), ("g_post", D), ("gm_ln_g", D),
               ("gm_ln_b", D), ("b_sp", H * GM), ("g_onorm", DH), ("a_log", 2 * H), ("dt_bias", 2 * H), ("loss", 1),
               ("_pad", 128 - 4 * H - 1 + 6 * 128), ("w_conv", 3 * 3 * D), ("dm_ctx", 3 * D), ("mod_factors", 8 * 4 * D)):
    SMALL_LAYOUT[_n] = (_off, _s)
    _off += _s
SMALL_ROWS = 368
assert N_CHIPS * SMALL_ROWS * 128 >= _off and (SMALL_ROWS // 2) % 8 == 0 and _off % 128 == 0
TINY_ROW0 = SMALL_LAYOUT["c_ctx"][0] // 128
TINY_ROWS = SMALL_LAYOUT["w_conv"][0] // 128 - TINY_ROW0
assert TINY_ROWS % 8 == 0


def _params(sem=None):
    return pltpu.CompilerParams(dimension_semantics=sem, vmem_limit_bytes=VMEM_LIMIT_V7X)


def _tile(n, cands=(256, 128, 64, 32, 16, 8)):
    if n <= cands[0]:
        return n
    for cand in cands:
        if n % cand == 0 and cand >= 64:
            return cand
    return max(d for d in range(8, cands[0], 8) if n % d == 0)


def _silu(x):
    return x * jax.nn.sigmoid(x)


def _gelu(x):
    return 0.5 * x * (1.0 + jnp.tanh(0.7978845608028654 * (x + 0.044715 * (x * x * x))))


def _mm(a, b, *, ta=False, tb=False, out_dtype=F32, tm=1024, tn=1024, tk=1024, add=None, add_from=0,
        b_sections=False, out_sections=False, after=None, name):
    m, k = (a.shape[1], a.shape[0]) if ta else a.shape
    if b_sections:
        sect = b.shape[2]
        n = b.shape[1] if tb else b.shape[0] * sect
        tn, tk = (tn, sect) if tb else (sect, tk)
    else:
        n = b.shape[0] if tb else b.shape[1]
    tm, tn, tk = min(tm, m), min(tn, n), min(tk, k)
    assert m % tm == 0 and n % tn == 0 and k % tk == 0, (name, m, n, k, tm, tn, tk)
    nk = k // tk
    dims = (((0,) if ta else (1,), (1,) if tb else (0,)), ((), ()))
    has_add = add is not None
    assert not has_add or add.shape == (m, n - add_from * tn), (name, add.shape)

    def body(*refs):
        a_ref, b_ref = refs[:2]
        o_ref = refs[2 + int(has_add) + int(after is not None)]
        acc_ref = refs[-1]
        kk = pl.program_id(2)
        bv = b_ref[0] if b_sections else b_ref[...]
        part = lax.dot_general(a_ref[...].astype(BF16), bv.astype(BF16), dims, preferred_element_type=F32)

        def finish(res):
            if has_add:
                res = res + jnp.where(pl.program_id(1) >= add_from, refs[2][...], 0.0)
            if out_sections:
                o_ref[0] = res.astype(out_dtype)
            else:
                o_ref[...] = res.astype(out_dtype)

        if nk == 1:
            finish(part)
            return

        @pl.when(kk == 0)
        def _():
            acc_ref[...] = part

        @pl.when((kk > 0) & (kk < nk - 1))
        def _():
            acc_ref[...] += part

        @pl.when(kk == nk - 1)
        def _():
            finish(acc_ref[...] + part)

    a_spec = pl.BlockSpec((tk, tm), lambda i, j, q: (q, i)) if ta else pl.BlockSpec((tm, tk), lambda i, j, q: (i, q))
    if b_sections:
        b_spec = (pl.BlockSpec((1, tn, tk), lambda i, j, q: (q, j, 0)) if tb
                  else pl.BlockSpec((1, tk, tn), lambda i, j, q: (j, q, 0)))
    else:
        b_spec = pl.BlockSpec((tn, tk), lambda i, j, q: (j, q)) if tb else pl.BlockSpec((tk, tn), lambda i, j, q: (q, j))
    add_spec = [pl.BlockSpec((tm, tn), lambda i, j, q: (i, jnp.maximum(j - add_from, 0)))] if has_add else []
    if out_sections:
        out_spec, out_shape = pl.BlockSpec((1, tm, tn), lambda i, j, q: (j, i, 0)), (n // tn, m, tn)
    else:
        out_spec, out_shape = pl.BlockSpec((tm, tn), lambda i, j, q: (i, j)), (m, n)
    return pl.pallas_call(
        body, name=name, grid=(m // tm, n // tn, nk),
        in_specs=[a_spec, b_spec] + add_spec + ([pl.BlockSpec(memory_space=pl.ANY)] if after is not None else []),
        out_specs=out_spec, out_shape=jax.ShapeDtypeStruct(out_shape, out_dtype),
        scratch_shapes=[pltpu.VMEM((tm, tn), F32)] if nk > 1 else [],
        compiler_params=_params(("parallel", "parallel", "arbitrary")),
    )(*([a, b] + ([add] if has_add else []) + ([after] if after is not None else [])))


def _h_fn(x, g, m):
    shift, scale = m[:, 0:D], m[:, D:2 * D]
    r = lax.rsqrt(jnp.mean(x * x, axis=-1, keepdims=True) + EPS)
    return (x * r * g) * (1.0 + scale) + shift


def _norm_fwd(x, g, mod, bmod, *, name):
    rows = x.shape[0]
    t = min(512, rows)

    def body(x_ref, g_ref, m_ref, b_ref, h_ref):
        h_ref[...] = _h_fn(x_ref[...], g_ref[...], m_ref[...] + b_ref[...]).astype(BF16)

    vec = lambda w: pl.BlockSpec((1, w), lambda i: (0, 0))
    return pl.pallas_call(
        body, name=name, grid=(rows // t,),
        in_specs=[pl.BlockSpec((t, D), lambda i: (i, 0)), vec(D), vec(3 * D), vec(3 * D)],
        out_specs=pl.BlockSpec((t, D), lambda i: (i, 0)),
        out_shape=jax.ShapeDtypeStruct((rows, D), BF16),
        compiler_params=_params(("parallel",)),
    )(x, g, mod, bmod)


def _norm_bwd(x, g, mod, bmod, dh_parts, resid, *, name):
    rows = x.shape[0]
    t = min(512, rows)
    n_parts = len(dh_parts)
    has_resid = resid is not None

    def body(*refs):
        x_ref, g_ref, m_ref, b_ref = refs[:4]
        parts = refs[4:4 + n_parts]
        r_ref = refs[4 + n_parts] if has_resid else None
        dx_ref, dg_ref, dm_ref = refs[-3:]
        i = pl.program_id(0)
        dh = parts[0][...]
        for p in parts[1:]:
            dh = dh + p[...]
        _, vjp = jax.vjp(_h_fn, x_ref[...], g_ref[...], m_ref[...] + b_ref[...])
        dx, dg, dm = vjp(dh)
        if has_resid:
            dx = dx + r_ref[...]
        dx_ref[...] = dx

        @pl.when(i == 0)
        def _():
            dg_ref[...] = dg
            dm_ref[...] = dm

        @pl.when(i > 0)
        def _():
            dg_ref[...] += dg
            dm_ref[...] += dm

    vec = lambda w: pl.BlockSpec((1, w), lambda i: (0, 0))
    tile = pl.BlockSpec((t, D), lambda i: (i, 0))
    ins = [x, g, mod, bmod, *dh_parts] + ([resid] if has_resid else [])
    return pl.pallas_call(
        body, name=name, grid=(rows // t,),
        in_specs=[tile, vec(D), vec(3 * D), vec(3 * D)] + [tile] * (n_parts + int(has_resid)),
        out_specs=[tile, vec(D), vec(3 * D)],
        out_shape=[jax.ShapeDtypeStruct((rows, D), F32), jax.ShapeDtypeStruct((1, D), F32),
                   jax.ShapeDtypeStruct((1, 3 * D), F32)],
        compiler_params=_params(("arbitrary",)),
    )(*ins)


def _conv_tile(u_ref, r0, t, rows, w0, w1, w2):
    u = u_ref[pl.ds(r0, t), :]
    prev8 = u_ref[pl.ds(pl.multiple_of(jnp.maximum(r0 - 8, 0), 8), 8), :]
    next8 = u_ref[pl.ds(pl.multiple_of(jnp.minimum(r0 + t, rows - 8), 8), 8), :]
    r8 = lax.broadcasted_iota(jnp.int32, (8, DH), 0)
    prev_row = jnp.sum(jnp.where(r8 == 7, prev8, 0.0), axis=0, keepdims=True)
    next_row = jnp.sum(jnp.where(r8 == 0, next8, 0.0), axis=0, keepdims=True)
    prev_row = jnp.where(r0 > 0, prev_row, 0.0)
    next_row = jnp.where(r0 + t < rows, next_row, 0.0)
    ri = lax.broadcasted_iota(jnp.int32, (t, DH), 0)
    um1 = jnp.where(ri == 0, prev_row, pltpu.roll(u, 1, 0))
    up1 = jnp.where(ri == t - 1, next_row, pltpu.roll(u, t - 1, 0))
    return w0 * um1 + w1 * u + w2 * up1, um1, u, up1


def _rowlocal(z, is_norm):
    y = _silu(z)
    yn = y * lax.rsqrt(jnp.sum(y * y, axis=-1, keepdims=True) + EPS)
    return jnp.where(is_norm, yn, y)


def _prep_fwd(p, wconv, n_norm, *, name):
    rows, nb = p.shape[0], p.shape[1] // DH
    t = min(512, rows)

    def body(u_ref, w_ref, o_ref):
        is_norm = pl.program_id(0) < n_norm
        w0, w1, w2 = w_ref[0:1, :], w_ref[1:2, :], w_ref[2:3, :]

        def step(s, carry):
            r0 = pl.multiple_of(s * t, t)
            z, _, _, _ = _conv_tile(u_ref, r0, t, rows, w0, w1, w2)
            o_ref[0, pl.ds(r0, t), :] = _rowlocal(z, is_norm)
            return carry

        lax.fori_loop(0, rows // t, step, 0)

    return pl.pallas_call(
        body, name=name, grid=(nb,),
        in_specs=[pl.BlockSpec((rows, DH), lambda j: (0, j)), pl.BlockSpec((3, DH), lambda j: (0, j))],
        out_specs=pl.BlockSpec((1, rows, DH), lambda j: (j, 0, 0)),
        out_shape=jax.ShapeDtypeStruct((nb, rows, DH), F32),
        compiler_params=_params(("parallel",)),
    )(p, wconv)


def _prep_bwd(p, wconv, d_a, d_b, n_norm, *, name):
    rows, nb = p.shape[0], p.shape[1] // DH
    t = min(512, rows)

    def body(u_ref, w_ref, da_ref, db_ref, du_ref, dw_ref, dz_ref):
        is_norm = pl.program_id(0) < n_norm
        w0, w1, w2 = w_ref[0:1, :], w_ref[1:2, :], w_ref[2:3, :]

        def step1(s, carry):
            a0, a1, a2 = carry
            r0 = pl.multiple_of(s * t, t)
            z, um1, u, up1 = _conv_tile(u_ref, r0, t, rows, w0, w1, w2)
            _, vjp = jax.vjp(lambda zz: _rowlocal(zz, is_norm), z)
            (dz,) = vjp(da_ref[0, pl.ds(r0, t), :] + db_ref[0, pl.ds(r0, t), :])
            dz_ref[pl.ds(r0, t), :] = dz
            red = lambda v: jnp.sum(v, axis=0, keepdims=True)
            return a0 + red(dz * um1), a1 + red(dz * u), a2 + red(dz * up1)

        zero = jnp.zeros((1, DH), F32)
        a0, a1, a2 = lax.fori_loop(0, rows // t, step1, (zero, zero, zero))
        dw_ref[0:1, :] = a0
        dw_ref[1:2, :] = a1
        dw_ref[2:3, :] = a2

        def step2(s, carry):
            r0 = pl.multiple_of(s * t, t)
            du, _, _, _ = _conv_tile(dz_ref, r0, t, rows, w2, w1, w0)
            du_ref[pl.ds(r0, t), :] = du.astype(BF16)
            return carry

        lax.fori_loop(0, rows // t, step2, 0)

    col = pl.BlockSpec((rows, DH), lambda j: (0, j))
    w_spec = pl.BlockSpec((3, DH), lambda j: (0, j))
    d_spec = pl.BlockSpec((1, rows, DH), lambda j: (j, 0, 0))
    return pl.pallas_call(
        body, name=name, grid=(nb,),
        in_specs=[col, w_spec, d_spec, d_spec], out_specs=[col, w_spec],
        out_shape=[jax.ShapeDtypeStruct((rows, nb * DH), BF16), jax.ShapeDtypeStruct((3, nb * DH), F32)],
        scratch_shapes=[pltpu.VMEM((rows, DH), F32)],
        compiler_params=_params(("parallel",)),
    )(p, wconv, d_a, d_b)


def _gates_fn(pab, avec, dvec):
    t = pab.shape[0]
    lane = lax.broadcasted_iota(jnp.int32, pab.shape, 1)
    xg = pab + dvec
    sp = jnp.maximum(xg, 0.0) + jnp.log(1.0 + jnp.exp(-jnp.abs(xg)))
    g = jnp.where(lane < 2 * H, -jnp.exp(avec) * sp, 0.0)
    ii = lax.broadcasted_iota(jnp.int32, (t, t), 0)
    jj = lax.broadcasted_iota(jnp.int32, (t, t), 1)
    same = (ii >> LOG_CH) == (jj >> LOG_CH)
    cum_f = _dot(jnp.where(same & (jj <= ii), 1.0, 0.0), g, precision=HI)
    cum_r = _dot(jnp.where(same & (jj >= ii), 1.0, 0.0), g, precision=HI)
    return jnp.where(lane < H, cum_f, jnp.where(lane < 2 * H, cum_r, jnp.where(lane < 4 * H, jax.nn.sigmoid(pab), 0.0)))


def _gates_fwd(pab, avec, dvec, *, name):
    rows = pab.shape[0]
    t = min(512, rows)

    def body(p_ref, a_ref, d_ref, o_ref):
        o_ref[...] = _gates_fn(p_ref[...], a_ref[...], d_ref[...])

    vec = pl.BlockSpec((1, DH), lambda i: (0, 0))
    tile = pl.BlockSpec((t, DH), lambda i: (i, 0))
    return pl.pallas_call(
        body, name=name, grid=(rows // t,), in_specs=[tile, vec, vec], out_specs=tile,
        out_shape=jax.ShapeDtypeStruct((rows, DH), F32), compiler_params=_params(("parallel",)),
    )(pab, avec, dvec)


def _gates_bwd(pab, avec, dvec, d_a, d_b, *, name):
    rows = pab.shape[0]
    t = min(512, rows)

    def body(p_ref, a_ref, d_ref, da_ref, db_ref, dp_ref, dav_ref, ddv_ref):
        i = pl.program_id(0)
        _, vjp = jax.vjp(_gates_fn, p_ref[...], a_ref[...], d_ref[...])
        dp, dav, ddv = vjp(da_ref[...] + db_ref[...])
        dp_ref[...] = dp.astype(BF16)

        @pl.when(i == 0)
        def _():
            dav_ref[...] = dav
            ddv_ref[...] = ddv

        @pl.when(i > 0)
        def _():
            dav_ref[...] += dav
            ddv_ref[...] += ddv

    vec = pl.BlockSpec((1, DH), lambda i: (0, 0))
    tile = pl.BlockSpec((t, DH), lambda i: (i, 0))
    return pl.pallas_call(
        body, name=name, grid=(rows // t,), in_specs=[tile, vec, vec, tile, tile], out_specs=[tile, vec, vec],
        out_shape=[jax.ShapeDtypeStruct((rows, DH), BF16), jax.ShapeDtypeStruct((1, DH), F32),
                   jax.ShapeDtypeStruct((1, DH), F32)],
        compiler_params=_params(("arbitrary",)),
    )(pab, avec, dvec, d_a, d_b)


def _dot_general(a, b, dims, precision=None):
    return lax.dot_general(a, b, (dims, ((), ())), precision=precision, preferred_element_type=F32)


@functools.partial(jax.custom_vjp, nondiff_argnums=(2,))
def _dot_bf16(a, b, dims):
    return _dot_general(a.astype(BF16), b.astype(BF16), dims)


def _dot_bf16_fwd(a, b, dims):
    return _dot_bf16(a, b, dims), (a, b)


def _dot_bf16_bwd(dims, res, g):
    a, b = res
    (ca,), (cb,) = dims
    da = _dot_bf16(g, b, ((1,), (1 - cb,))) if ca == 1 else _dot_bf16(b, g, ((1 - cb,), (1,)))
    db = _dot_bf16(a, g, ((1 - ca,), (0,))) if cb == 0 else _dot_bf16(g, a, ((0,), (1 - ca,)))
    return da, db


_dot_bf16.defvjp(_dot_bf16_fwd, _dot_bf16_bwd)


def _dot(a, b, dims=((1,), (0,)), precision=None):
    if precision is None and a.dtype == F32 and b.dtype == F32:
        return _dot_bf16(a, b, dims)
    return _dot_general(a, b, dims, precision)


_NT = ((1,), (1,))
_TN = ((0,), (0,))


@jax.custom_vjp
def _saved_inverse(neg_a, inv):
    return inv


def _saved_inverse_fwd(neg_a, inv):
    return inv, inv


def _saved_inverse_bwd(inv, d_inv):
    idx = range(len(inv))
    left = [_dot(inv[i], d_inv[i], _TN) for i in idx]
    d_neg_a = [_dot(left[i], inv[i], _NT) for i in idx]
    return d_neg_a, [jnp.zeros_like(t) for t in inv]


_saved_inverse.defvjp(_saved_inverse_fwd, _saved_inverse_bwd)


def _pairs(s, q, k, v, gcol, bcol, revs, inv_saved=None):
    idx = range(len(revs))
    ii = lax.broadcasted_iota(jnp.int32, (PAIR, PAIR), 0)
    jj = lax.broadcasted_iota(jnp.int32, (PAIR, PAIR), 1)
    same = (ii >> LOG_CH) == (jj >> LOG_CH)
    incl_d = (same & (ii >= jj), same & (ii <= jj))
    strict_d = (same & (ii > jj), same & (ii < jj))
    incl = [incl_d[int(r)] for r in revs]
    strict = [strict_d[int(r)] for r in revs]
    eye = jnp.where(ii == jj, 1.0, 0.0)
    gc_i = [jnp.broadcast_to(gcol[i], (PAIR, DH)) for i in idx]
    gc_j = [gc_i[i].T for i in idx]
    decay = [jnp.where(incl[i], jnp.exp(jnp.where(incl[i], gc_i[i] - gc_j[i], 0.0)), 0.0) for i in idx]
    b_b = [jnp.broadcast_to(bcol[i], (PAIR, DH)) for i in idx]
    kb = [k[i] * b_b[i] for i in idx]
    kk = [_dot(kb[i], k[i], _NT) for i in idx]
    bp = [jnp.where(strict[i], -kk[i] * decay[i], 0.0) for i in idx]
    if inv_saved is not None:
        inv = _saved_inverse(bp, inv_saved)
    else:
        low = bp
        for prec_sq, prec_acc in PREC_POWERS:
            bp = [_dot(bp[i], bp[i], precision=prec_sq) for i in idx]
            more = [_dot(low[i], bp[i], precision=prec_acc) for i in idx]
            low = [low[i] + bp[i] + more[i] for i in idx]
        inv = [eye + low[i] for i in idx]
    eg = [jnp.exp(gc_i[i]) for i in idx]
    sol = [_dot(inv[i], jnp.concatenate([v[i] * b_b[i], kb[i] * eg[i]], axis=1)) for i in idx]
    u_val = [sol[i][:, :DH] for i in idx]
    w_key = [sol[i][:, DH:] for i in idx]
    row = lax.broadcasted_iota(jnp.int32, (PAIR, 1), 0)
    has_q = q[0] is not None
    if has_q:
        qc = [q[i] * (DH ** -0.5) for i in idx]
        qk = [_dot(qc[i], k[i], _NT) for i in idx]
        attn = [qk[i] * decay[i] for i in idx]
        qd = [qc[i] * eg[i] for i in idx]
    outs = [[None, None] for _ in idx]
    zeros = jnp.zeros((CH, DH), F32)
    for step in range(2):
        cidx = [(1 - step) if revs[i] else step for i in idx]
        sl = [slice(c * CH, (c + 1) * CH) for c in cidx]
        last = [c * CH if revs[i] else c * CH + CH - 1 for i, c in zip(idx, cidx)]
        gl = [jnp.sum(jnp.where(row == last[i], gcol[i], 0.0), axis=0, keepdims=True) for i in idx]
        k_tail = [k[i][sl[i]] * jnp.exp(gl[i] - gc_i[i][sl[i]]) for i in idx]
        ws = [_dot(w_key[i][sl[i]], s[i]) for i in idx]
        v_new = [u_val[i][sl[i]] - ws[i] for i in idx]
        if has_q:
            v_pad = [jnp.concatenate([v_new[i], zeros] if cidx[i] == 0 else [zeros, v_new[i]], axis=0) for i in idx]
            o_state = [_dot(qd[i][sl[i]], s[i]) for i in idx]
            o_local = [_dot(attn[i][sl[i]], v_pad[i]) for i in idx]
            for i in idx:
                outs[i][cidx[i]] = o_state[i] + o_local[i]
        kv = [_dot(k_tail[i], v_new[i], _TN) for i in idx]
        s = [s[i] * jnp.exp(gl[i]) + kv[i] for i in idx]
    return s, ([jnp.concatenate(outs[i], axis=0) for i in idx] if has_q else None), inv


def _lane_col(tile, idx):
    lane = lax.broadcasted_iota(jnp.int32, tile.shape, 1)
    return jnp.sum(jnp.where(lane == idx, tile, 0.0), axis=1, keepdims=True)


def _gdn_fwd(qkv, gb, s0f, s0b, has_q, *, name):
    nb, rows, _ = qkv.shape
    n = rows // PAIR
    qoff = H if has_q else 0

    def body(qf_ref, qb_ref, gf_ref, gr_ref, s0f_ref, s0b_ref, of_ref, ob_ref, ssf_ref, ssb_ref, tsf_ref, tsb_ref,
             sf_ref, sb_ref):
        @pl.when(pl.program_id(0) == 0)
        def _():
            sf_ref[...] = s0f_ref[...]
            sb_ref[...] = s0b_ref[...]

        gtiles = (gf_ref[...], gr_ref[...])

        dirs = ((qf_ref, sf_ref, ssf_ref, of_ref), (qb_ref, sb_ref, ssb_ref, ob_ref))
        ts_refs = (tsf_ref, tsb_ref)

        work = [(h, d) for h in range(H) for d in range(2)]
        loaded = []
        for h, d in work:
            q_ref, s_ref, _, _ = dirs[d]
            loaded.append((s_ref[h], q_ref[h] if has_q else None, q_ref[qoff + h], q_ref[qoff + H + h],
                           _lane_col(gtiles[d], d * H + h), _lane_col(gtiles[d], 2 * H + d * H + h)))
        s_new, o, inv = _pairs(*[list(col) for col in zip(*loaded)], revs=[d == 1 for _, d in work])
        for i, (h, d) in enumerate(work):
            _, s_ref, ss_ref, o_ref = dirs[d]
            ss_ref[0, h] = loaded[i][0]
            ts_refs[d][0, h] = inv[i]
            s_ref[h] = s_new[i]
            o_ref[h] = o[i] if has_q else jnp.zeros((PAIR, DH), F32)

    fwd3 = lambda i: (0, i, 0)
    rev3 = lambda i: (0, n - 1 - i, 0)
    state = pl.BlockSpec((H, DH, DH), lambda i: (0, 0, 0))
    saved = pl.BlockSpec((1, H, DH, DH), lambda i: (i, 0, 0, 0))
    return pl.pallas_call(
        body, name=name, grid=(n,),
        in_specs=[pl.BlockSpec((nb, PAIR, DH), fwd3), pl.BlockSpec((nb, PAIR, DH), rev3),
                  pl.BlockSpec((PAIR, DH), lambda i: (i, 0)), pl.BlockSpec((PAIR, DH), lambda i: (n - 1 - i, 0)),
                  state, state],
        out_specs=[pl.BlockSpec((H, PAIR, DH), fwd3), pl.BlockSpec((H, PAIR, DH), rev3), saved, saved, saved, saved,
                   state, state],
        out_shape=[jax.ShapeDtypeStruct((H, rows, DH), F32)] * 2 + [jax.ShapeDtypeStruct((n, H, DH, DH), F32)] * 4
        + [jax.ShapeDtypeStruct((H, DH, DH), F32)] * 2,
        compiler_params=_params(("arbitrary",)),
    )(qkv, qkv, gb, gb, s0f, s0b)


def _gdn_bwd(qkv, gb, ssf, ssb, tsf, tsb, do, dsf, dsb, has_q, *, name):
    nb, rows, _ = qkv.shape
    n = rows // PAIR
    qoff = H if has_q else 0

    def body(qf_ref, qb_ref, gf_ref, gr_ref, ssf_ref, ssb_ref, tsf_ref, tsb_ref, dof_ref, dob_ref, dsf0_ref, dsb0_ref,
             dqf_ref, dqb_ref, dgf_ref, dgr_ref, dsf_ref, dsb_ref):
        ts_refs = (tsf_ref, tsb_ref)
        @pl.when(pl.program_id(0) == 0)
        def _():
            dsf_ref[...] = dsf0_ref[...]
            dsb_ref[...] = dsb0_ref[...]

        gtiles = (gf_ref[...], gr_ref[...])
        lane = lax.broadcasted_iota(jnp.int32, (PAIR, DH), 1)

        dirs = ((qf_ref, ssf_ref, dof_ref, dsf_ref, dqf_ref), (qb_ref, ssb_ref, dob_ref, dsb_ref, dqb_ref))

        work = [(h, d) for h in range(H) for d in range(2)]
        revs = [d == 1 for _, d in work]
        s_in, q_in, k_in, v_in, g_in, b_in, ds_out, do_out, inv_in = [], [], [], [], [], [], [], [], []
        for h, d in work:
            q_ref, ss_ref, do_ref, ds_ref, _ = dirs[d]
            s_in.append(ss_ref[0, h])
            inv_in.append(ts_refs[d][0, h])
            q_in.append(q_ref[h] if has_q else None)
            k_in.append(q_ref[qoff + h])
            v_in.append(q_ref[qoff + H + h])
            g_in.append(_lane_col(gtiles[d], d * H + h))
            b_in.append(_lane_col(gtiles[d], 2 * H + d * H + h))
            ds_out.append(ds_ref[h])
            do_out.append(do_ref[h] if has_q else None)
        if has_q:
            _, vjp = jax.vjp(lambda s_, q_, k_, v_, g_, b_: _pairs(s_, q_, k_, v_, g_, b_, revs, inv_in)[:2],
                             s_in, q_in, k_in, v_in, g_in, b_in)
            ds, dq, dk, dv, dg, db = vjp((ds_out, do_out))
        else:
            _, vjp = jax.vjp(lambda s_, k_, v_, g_, b_: _pairs(s_, q_in, k_, v_, g_, b_, revs, inv_in)[0],
                             s_in, k_in, v_in, g_in, b_in)
            ds, dk, dv, dg, db = vjp(ds_out)
        dgb = [jnp.zeros((PAIR, DH), F32)] * 2
        for i, (h, d) in enumerate(work):
            dq_ref, ds_ref = dirs[d][4], dirs[d][3]
            ds_ref[h] = ds[i]
            if has_q:
                dq_ref[h] = dq[i]
            dq_ref[qoff + h], dq_ref[qoff + H + h] = dk[i], dv[i]
            dgb[d] = (dgb[d] + jnp.where(lane == d * H + h, dg[i], 0.0)
                      + jnp.where(lane == 2 * H + d * H + h, db[i], 0.0))
        dgf_ref[...] = dgb[0]
        dgr_ref[...] = dgb[1]

    fwd3 = lambda i: (0, n - 1 - i, 0)
    rev3 = lambda i: (0, i, 0)
    state = pl.BlockSpec((H, DH, DH), lambda i: (0, 0, 0))
    saved = pl.BlockSpec((1, H, DH, DH), lambda i: (n - 1 - i, 0, 0, 0))
    gf_spec = pl.BlockSpec((PAIR, DH), lambda i: (n - 1 - i, 0))
    gr_spec = pl.BlockSpec((PAIR, DH), lambda i: (i, 0))
    return pl.pallas_call(
        body, name=name, grid=(n,),
        in_specs=[pl.BlockSpec((nb, PAIR, DH), fwd3), pl.BlockSpec((nb, PAIR, DH), rev3), gf_spec, gr_spec,
                  saved, saved, saved, saved,
                  pl.BlockSpec((H, PAIR, DH), fwd3), pl.BlockSpec((H, PAIR, DH), rev3), state, state],
        out_specs=[pl.BlockSpec((nb, PAIR, DH), fwd3), pl.BlockSpec((nb, PAIR, DH), rev3), gf_spec, gr_spec, state, state],
        out_shape=[jax.ShapeDtypeStruct((nb, rows, DH), F32)] * 2 + [jax.ShapeDtypeStruct((rows, DH), F32)] * 2
        + [jax.ShapeDtypeStruct((H, DH, DH), F32)] * 2,
        compiler_params=_params(("arbitrary",)),
    )(qkv, qkv, gb, gb, ssf, ssb, tsf, tsb, do, do, dsf, dsb)


def _stage1(zb, ua, va, za, o, gon, lng, lnb, wsp, bsp):
    gv = [_gelu(t) for t in va]
    mu = sum(jnp.sum(t, axis=-1, keepdims=True) for t in gv) * (1.0 / D)
    xc = [t - mu for t in gv]
    var = sum(jnp.sum(t * t, axis=-1, keepdims=True) for t in xc) * (1.0 / D)
    rs = lax.rsqrt(var + EPS)
    ya, yb = [], []
    for g in range(H):
        vv = xc[g] * rs * lng[g] + lnb[g]
        s = _dot(wsp[g], vv) + bsp[g]
        ya.append(_gelu(ua[g]) * s * _silu(za[g]))
        r = lax.rsqrt(jnp.mean(o[g] * o[g], axis=-1, keepdims=True) + EPS)
        yb.append(o[g] * r * gon * _silu(zb[g]))
    return ya, yb


def _stage2(ma, mb, ga, gb):
    return jax.nn.sigmoid(ga) * ma + jax.nn.sigmoid(gb) * mb


def _stage3(out, x, tgt, gpost, gate):
    r = out * lax.rsqrt(jnp.mean(out * out, axis=-1, keepdims=True) + EPS) * gpost
    err = x + gate * r - tgt
    return 0.5 * jnp.sum(jnp.mean(err * err, axis=-1, keepdims=True), axis=0, keepdims=True)


def _post(p_rest, o_f, o_b, x, tgt, mod, bmod, gon, lng, lnb, wsp, bspb, wpa, wpb, wout, gpost, *, name):
    rows = x.shape[0]
    n = rows // GM
    lanes = lambda g: slice(g * DH, (g + 1) * DH)
    bdot = lambda a, w_ref: _dot(a.astype(BF16), w_ref[...])
    bdot_t = lambda a, w_ref: _dot(a.astype(BF16), w_ref[...], _NT)

    def body(p_ref, of_ref, ob_ref, x_ref, t_ref, m_ref, bm_ref, gon_ref, lng_ref, lnb_ref, wsp_ref, bsp_ref,
             wpa_ref, wpb_ref, wout_ref, gp_ref,
             loss_ref, dp_ref, do_ref, dx_ref, ya_ref, yb_ref, mg_ref, dma_ref, dmb_ref, dout_ref,
             dvec_ref, dgon_ref, dwsp_ref, dbsp_ref):
        @pl.when(pl.program_id(0) == 0)
        def _():
            loss_ref[...] = jnp.zeros_like(loss_ref)
            dvec_ref[...] = jnp.zeros_like(dvec_ref)
            dgon_ref[...] = jnp.zeros_like(dgon_ref)
            dwsp_ref[...] = jnp.zeros_like(dwsp_ref)
            dbsp_ref[...] = jnp.zeros_like(dbsp_ref)

        piece = lambda blk: [p_ref[:, blk * D + g * DH: blk * D + (g + 1) * DH].astype(F32) for g in range(H)]
        zb, ua, va, za = piece(0), piece(1), piece(2), piece(3)
        o = [of_ref[g] + ob_ref[g] for g in range(H)]
        gon = gon_ref[...]
        lng = [lng_ref[:, lanes(g)] for g in range(H)]
        lnb = [lnb_ref[:, lanes(g)] for g in range(H)]
        wsp = [wsp_ref[g] for g in range(H)]
        bsp = [bsp_ref[g] for g in range(H)]
        (ya, yb), vjp1 = jax.vjp(_stage1, zb, ua, va, za, o, gon, lng, lnb, wsp, bsp)
        y_a, y_b = jnp.concatenate(ya, axis=1), jnp.concatenate(yb, axis=1)
        ma, mb = bdot(y_a, wpa_ref), bdot(y_b, wpb_ref)
        ga, gb = p_ref[:, 4 * D:5 * D].astype(F32), p_ref[:, 5 * D:6 * D].astype(F32)
        merged, vjp2 = jax.vjp(_stage2, ma, mb, ga, gb)
        out = bdot(merged, wout_ref)
        gate = m_ref[:, 2 * D:3 * D] + bm_ref[:, 2 * D:3 * D]
        loss, vjp3 = jax.vjp(_stage3, out, x_ref[...], t_ref[...], gp_ref[...], gate)
        loss_ref[...] += jnp.broadcast_to(loss, loss_ref.shape)

        dout, dx, _, dgpost, dgate = vjp3(jnp.ones((1, 1), F32))
        dx_ref[...] = dx
        dmerged = bdot_t(dout, wout_ref)
        dma, dmb, dga, dgb = vjp2(dmerged)
        dya, dyb = bdot_t(dma, wpa_ref), bdot_t(dmb, wpb_ref)
        dzb, dua, dva, dza, do, dgon, dlng, dlnb, dwsp, dbsp = vjp1(
            ([dya[:, lanes(g)] for g in range(H)], [dyb[:, lanes(g)] for g in range(H)]))

        for blk, dlist in enumerate((dzb, dua, dva, dza)):
            for g in range(H):
                dp_ref[:, blk * D + g * DH: blk * D + (g + 1) * DH] = dlist[g].astype(BF16)
        dp_ref[:, 4 * D:5 * D] = dga.astype(BF16)
        dp_ref[:, 5 * D:6 * D] = dgb.astype(BF16)
        for g in range(H):
            do_ref[g] = do[g]
            dwsp_ref[g] += dwsp[g]
            dbsp_ref[g] += dbsp[g]
            dvec_ref[2:3, lanes(g)] += dlng[g]
            dvec_ref[3:4, lanes(g)] += dlnb[g]
        dvec_ref[0:1, :] += dgpost
        dvec_ref[1:2, :] += dgate
        dgon_ref[0:1, :] += dgon
        ya_ref[...] = y_a.astype(BF16)
        yb_ref[...] = y_b.astype(BF16)
        mg_ref[...] = merged.astype(BF16)
        dma_ref[...] = dma.astype(BF16)
        dmb_ref[...] = dmb.astype(BF16)
        dout_ref[...] = dout.astype(BF16)

    row = lambda w: pl.BlockSpec((GM, w), lambda i: (i, 0))
    heads = pl.BlockSpec((H, GM, DH), lambda i: (0, i, 0))
    full = lambda shape: pl.BlockSpec(shape, lambda i: tuple(0 for _ in shape))
    sds = jax.ShapeDtypeStruct
    return pl.pallas_call(
        body, name=name, grid=(n,),
        in_specs=[row(6 * D), heads, heads, row(D), row(D), full((1, 3 * D)), full((1, 3 * D)), full((1, DH)),
                  full((1, D)), full((1, D)), full((H, GM, GM)), full((H, GM, GM)),
                  full((D, D)), full((D, D)), full((D, D)), full((1, D))],
        out_specs=[full((8, DH)), row(6 * D), heads, row(D)] + [row(D)] * 6
        + [full((8, D)), full((8, DH)), full((H, GM, GM)), full((H, GM, GM))],
        out_shape=[sds((8, DH), F32), sds((rows, 6 * D), BF16), sds((H, rows, DH), F32), sds((rows, D), F32)]
        + [sds((rows, D), BF16)] * 6 + [sds((8, D), F32), sds((8, DH), F32), sds((H, GM, GM), F32), sds((H, GM, GM), F32)],
        compiler_params=_params(("arbitrary",)),
    )(p_rest, o_f, o_b, x, tgt, mod, bmod, gon, lng, lnb, wsp, bspb, wpa, wpb, wout, gpost)


def _silu_rows(c, *, name):
    def body(c_ref, o_ref):
        o_ref[...] = _silu(c_ref[...])

    return pl.pallas_call(body, name=name, out_shape=jax.ShapeDtypeStruct(c.shape, F32))(c)


def _dsilu_mul(c, d, *, name):
    def body(c_ref, d_ref, o_ref):
        _, vjp = jax.vjp(_silu, c_ref[...])
        (o_ref[...],) = vjp(d_ref[...])

    return pl.pallas_call(body, name=name, out_shape=jax.ShapeDtypeStruct(c.shape, F32))(c, d)


def _adamw(w, g, m, v, *, name):
    rows, cols = w.shape[-2:]
    t = _tile(rows if g.ndim == 2 else g.shape[1])
    c1 = 1.0 / (1.0 - ADAM_B1 ** ADAM_STEP)
    c2 = 1.0 / (1.0 - ADAM_B2 ** ADAM_STEP)

    def body(w_ref, g_ref, m_ref, v_ref, go_ref, d_ref, mo_ref, vo_ref):
        blk = lambda r: r[...].reshape(t, cols)
        gv = blk(g_ref)
        mn = ADAM_B1 * blk(m_ref) + (1.0 - ADAM_B1) * gv
        vn = ADAM_B2 * blk(v_ref) + (1.0 - ADAM_B2) * (gv * gv)
        delta = -ADAM_LR * ((mn * c1) / (jnp.sqrt(vn * c2) + ADAM_EPS) + ADAM_WD * blk(w_ref))
        for ref, val in ((go_ref, gv), (d_ref, delta), (mo_ref, mn), (vo_ref, vn)):
            ref[...] = val.reshape(ref.shape)

    tile = (pl.BlockSpec((1, t, cols), lambda i: (0, i, 0)) if w.ndim == 3 else pl.BlockSpec((t, cols), lambda i: (i, 0)))
    if g.ndim == 3:
        per = g.shape[1] // t
        g_tile = pl.BlockSpec((1, t, cols), lambda i: (i // per, i % per, 0))
    else:
        g_tile = pl.BlockSpec((t, cols), lambda i: (i, 0))
    return pl.pallas_call(
        body, name=name, grid=(rows // t,),
        in_specs=[tile, g_tile, tile, tile], out_specs=[tile] * 4,
        out_shape=[jax.ShapeDtypeStruct(w.shape, F32)] * 4,
        compiler_params=_params(("parallel",)),
    )(w, g, m, v)


def _add_sibling(full, got, where, out_dtype, *, name):
    s, rows, cols = full.shape
    hr = rows // 2
    t = _tile(hr)
    nt = hr // t

    def body(w_ref, a_ref, b_ref, o_ref):
        o_ref[...] = (a_ref[...].astype(F32) + b_ref[...].astype(F32)).astype(out_dtype)

    tile = pl.BlockSpec((1, t, cols), lambda j, i, w: (j, i, 0))
    return pl.pallas_call(
        body, name=name,
        grid_spec=pltpu.PrefetchScalarGridSpec(
            num_scalar_prefetch=1, grid=(s, nt),
            in_specs=[pl.BlockSpec((1, t, cols), lambda j, i, w: (j, w[0] * nt + i, 0)), tile], out_specs=tile),
        out_shape=jax.ShapeDtypeStruct((s, hr, cols), out_dtype), compiler_params=_params(("parallel", "parallel")),
    )(where, full, got)


def _sum_chips(own, slots, where, n_out, which, *, name):
    _, hr, cols = own.shape
    t = _tile(hr)

    def body(w_ref, a_ref, s_ref, o_ref):
        f = lambda v: v.astype(F32)
        o_ref[0] = ((f(a_ref[0]) + f(s_ref[0])) + f(s_ref[1])) + f(s_ref[2])

    return pl.pallas_call(
        body, name=name,
        grid_spec=pltpu.PrefetchScalarGridSpec(
            num_scalar_prefetch=1, grid=(hr // t,),
            in_specs=[pl.BlockSpec((1, t, cols), lambda i, w: (w[1], i, 0)),
                      pl.BlockSpec((N_CHIPS - 1, t, cols), lambda i, w: (0, i, 0))],
            out_specs=pl.BlockSpec((1, t, cols), lambda i, w: (w[which], i, 0))),
        out_shape=jax.ShapeDtypeStruct((n_out, hr, cols), F32), compiler_params=_params(("parallel",)),
    )(where, own, slots)


def _local_step(x, c, ctx, c_ctx, tgt, w_mod, b_mod, g_pre, g_post, w_qkv, w_ab, w_rest, w_conv, a_log, dt_bias,
                g_onorm, gm_ln_g, gm_ln_b, w_sp, b_sp, late_weights, on_large_gradients=None, run_after=None):
    rows, rows_c = x.shape[0], ctx.shape[0]
    cc = jnp.zeros((16, D), F32).at[0].set(c[0]).at[1].set(c_ctx)
    scc = _silu_rows(cc, name="silu_cond")
    mod = _mm(scc, w_mod, b_sections=True, after=run_after, name="mod_fwd")
    mod_x, mod_c = mod[0:1], mod[1:2]
    avec = jnp.zeros((1, DH), F32).at[0, :2 * H].set(a_log.reshape(-1))
    dvec = jnp.zeros((1, DH), F32).at[0, :2 * H].set(dt_bias.reshape(-1))
    bspb = jnp.broadcast_to(b_sp[:, :, None], (H, GM, GM))
    w_kv, wconv_kv = w_qkv[:, D:], w_conv[:, D:]

    h_c = _norm_fwd(ctx, g_pre, mod_c, b_mod, name="norm_fwd_ctx")
    pc_kv = _mm(h_c, w_kv, name="inproj_ctx_kv")
    pc_ab = _mm(h_c, w_ab, name="inproj_ctx_ab")
    kv_c = _prep_fwd(pc_kv, wconv_kv, H, name="prep_fwd_ctx")
    gb_c = _gates_fwd(pc_ab, avec, dvec, name="gates_fwd_ctx")
    s_zero = jnp.zeros((H, DH, DH), F32)
    _, _, ssf_c, ssb_c, tsf_c, tsb_c, s_f, s_b = _gdn_fwd(kv_c, gb_c, s_zero, s_zero, False, name="gdn_fwd_ctx")

    h_x = _norm_fwd(x, g_pre, mod_x, b_mod, name="norm_fwd_x")
    p_qkv = _mm(h_x, w_qkv, name="inproj_qkv")
    p_ab = _mm(h_x, w_ab, name="inproj_ab")
    p_rest = _mm(h_x, w_rest, out_dtype=BF16, name="inproj_rest")
    qkv = _prep_fwd(p_qkv, w_conv, 2 * H, name="prep_fwd_x")
    gb_x = _gates_fwd(p_ab, avec, dvec, name="gates_fwd_x")
    o_f, o_b, ssf, ssb, tsf, tsb, _, _ = _gdn_fwd(qkv, gb_x, s_f, s_b, True, name="gdn_fwd_x")

    w_pa, w_pb, w_out = late_weights(o_f)
    (loss_acc, dp_rest, do, dx_res, ya, yb, mg, dma, dmb, dout, dvec_post, dgon, dwsp, dbspb) = _post(
        p_rest, o_f, o_b, x, tgt, mod_x, b_mod, g_onorm, gm_ln_g, gm_ln_b, w_sp, bspb, w_pa, w_pb, w_out, g_post,
        name="post")
    g = {}
    g["w_pa"] = _mm(ya, dma, ta=True, name="dw_pa")
    g["w_pb"] = _mm(yb, dmb, ta=True, name="dw_pb")
    g["w_out"] = _mm(mg, dout, ta=True, name="dw_out")

    zeros_s = jnp.zeros((H, DH, DH), F32)
    dq_f, dq_b, dg_f, dg_b, ds0_f, ds0_b = _gdn_bwd(qkv, gb_x, ssf, ssb, tsf, tsb, do, zeros_s, zeros_s, True,
                                                    name="gdn_bwd_x")
    dp_qkv, dwc_x = _prep_bwd(p_qkv, w_conv, dq_f, dq_b, 2 * H, name="prep_bwd_x")
    dp_ab, dav_x, ddv_x = _gates_bwd(p_ab, avec, dvec, dg_f, dg_b, name="gates_bwd_x")
    dkv_f, dkv_b, dgc_f, dgc_b, _, _ = _gdn_bwd(kv_c, gb_c, ssf_c, ssb_c, tsf_c, tsb_c, jnp.zeros((H, rows_c, DH), F32),
                                                 ds0_f, ds0_b, False, name="gdn_bwd_ctx")
    dpc_kv, dwc_c = _prep_bwd(pc_kv, wconv_kv, dkv_f, dkv_b, H, name="prep_bwd_ctx")
    dpc_ab, dav_c, ddv_c = _gates_bwd(pc_ab, avec, dvec, dgc_f, dgc_b, name="gates_bwd_ctx")

    dw_kv_c = _mm(h_c, dpc_kv, ta=True, name="dw_kv_ctx")
    dw_qkv = _mm(h_x, dp_qkv, ta=True, tn=D, add=dw_kv_c, add_from=1, out_dtype=BF16, name="dw_qkv")
    dw_ab_c = _mm(h_c, dpc_ab, ta=True, name="dw_ab_ctx")
    dw_ab = _mm(h_x, dp_ab, ta=True, add=dw_ab_c, out_dtype=BF16, name="dw_ab")
    dw_rest = _mm(h_x, dp_rest, ta=True, out_dtype=BF16, name="dw_rest")
    g["w_in"] = (dw_qkv, dw_ab[:, :4 * H], dw_rest)
    token = on_large_gradients(g) if on_large_gradients is not None else None
    dh_parts = [_mm(dp_qkv, w_qkv, tb=True, tk=3 * D // 2, after=token, name="dh_qkv"),
                _mm(dp_rest, w_rest, tb=True, tk=3 * D // 2, after=token, name="dh_rest"),
                _mm(dp_ab, w_ab, tb=True, after=token, name="dh_ab")]
    grad_x, dgpre_x, dm_x = _norm_bwd(x, g_pre, mod_x, b_mod, dh_parts, dx_res, name="norm_bwd_x")
    dhc_parts = [_mm(dpc_kv, w_kv, tb=True, after=token, name="dhc_kv"),
                 _mm(dpc_ab, w_ab, tb=True, after=token, name="dhc_ab")]
    _, dgpre_c, dm_c = _norm_bwd(ctx, g_pre, mod_c, b_mod, dhc_parts, None, name="norm_bwd_ctx")

    dm_x = dm_x.at[:, 2 * D:].add(dvec_post[1:2])
    dmod = jnp.zeros((16, 3 * D), F32).at[0].set(dm_x[0]).at[1].set(dm_c[0])
    g["mod_factors"] = (scc[0], dm_x[0])
    g["dm_ctx"] = dm_c[0]
    g["silu_c_ctx"] = scc[1:2]
    dcc = _mm(dmod, w_mod, tb=True, b_sections=True, name="dcc")
    g["c_ctx"] = _dsilu_mul(cc[:8], dcc[:8], name="dc_ctx")[1]
    g["b_mod"] = dm_x + dm_c
    g["g_pre"] = dgpre_x + dgpre_c
    g["g_post"] = dvec_post[0:1]
    g["gm_ln_g"], g["gm_ln_b"] = dvec_post[2:3], dvec_post[3:4]
    g["g_onorm"] = dgon[0:1]
    g["w_sp"] = dwsp
    g["b_sp"] = jnp.sum(dbspb, axis=-1)
    g["w_conv"] = dwc_x.at[:, D:].add(dwc_c)
    g["a_log"] = (dav_x + dav_c)[0, :2 * H].reshape(2, H)
    g["dt_bias"] = (ddv_x + ddv_c)[0, :2 * H].reshape(2, H)
    return loss_acc[0, 0], grad_x, g


ANY = pl.BlockSpec(memory_space=pl.ANY)


def _place():
    x, y, c = lax.axis_index("x"), lax.axis_index("y"), lax.axis_index("c")
    chips = [(1 - x, y), (x, 1 - y), (1 - x, 1 - y)]
    return x, y, c, (x, y, 1 - c), chips


def _gather_shards(big, small, *, name):
    nb = len(big)

    def body(*refs):
        ins, sm_in = refs[:nb], refs[nb]
        outs, sm_out = refs[nb + 1:2 * nb + 1], refs[2 * nb + 1]
        send, recv = refs[2 * nb + 2:]
        x, y, c, sibling, chips = _place()
        mine = 2 * x + y

        def half(a, shard, hc):
            hr = big[a].shape[0] // 2
            return outs[a].at[shard, pl.ds(hc * hr, hr), :]

        def remote(k, src, dst, to):
            return pltpu.make_async_remote_copy(src_ref=src, dst_ref=dst, send_sem=send.at[k], recv_sem=recv.at[k],
                                                device_id=to, device_id_type=MESH)

        sends = []
        for a in range(nb):
            hr = big[a].shape[0] // 2
            for j, chip in enumerate(chips):
                sends.append(remote(a * 3 + j, ins[a].at[pl.ds(c * hr, hr), :], half(a, mine, c), (*chip, c)))
        for j, chip in enumerate(chips):
            sends.append(remote(nb * 3 + j, sm_in, sm_out.at[mine], (*chip, c)))
        for cp in sends:
            cp.start()
        base = nb * 3 + 3
        passed = []
        for a in range(nb):
            for j, (px, py) in enumerate(chips):
                theirs = 2 * px + py
                remote(a * 3 + j, half(a, theirs, c), half(a, theirs, c), sibling).wait_recv()
                fw = remote(base + a * 3 + j, half(a, theirs, c), half(a, theirs, c), sibling)
                fw.start()
                passed.append(fw)
        for a in range(nb):
            for j, (px, py) in enumerate(chips):
                theirs = 2 * px + py
                remote(base + a * 3 + j, half(a, theirs, 1 - c), half(a, theirs, 1 - c), sibling).wait_recv()
        for j, (px, py) in enumerate(chips):
            remote(nb * 3 + j, sm_in, sm_out.at[2 * px + py], sibling).wait_recv()
        for cp in sends + passed:
            cp.wait_send()

    n_remote = 2 * nb * 3 + 3
    outs = pl.pallas_call(
        body, name=name, in_specs=[ANY] * (nb + 1), out_specs=[ANY] * (nb + 1),
        out_shape=[jax.ShapeDtypeStruct((N_CHIPS,) + a.shape, a.dtype) for a in big + [small]],
        scratch_shapes=[pltpu.SemaphoreType.DMA((n_remote,)), pltpu.SemaphoreType.DMA((n_remote,))],
    )(*big, small)
    return outs[:nb], outs[nb]


def _gather_late_start(arrs, after, *, name):
    na = len(arrs)

    def body(*refs):
        ins, land = refs[:na], refs[na:2 * na]
        send, recv = refs[2 * na + 1], refs[2 * na + 2]
        token = refs[-1]
        x, y, c, _, chips = _place()
        for a in range(na):
            hr = arrs[a].shape[0] // 2
            for j, (px, py) in enumerate(chips):
                for other in range(2):
                    k = (a * 3 + j) * 2 + other
                    _remote(ins[a].at[pl.ds(c * hr, hr), :], land[a].at[2 * x + y, pl.ds(c * hr, hr), :],
                            send.at[k], recv.at[k], (px, py, c ^ other)).start()
        token[...] = jnp.zeros_like(token)

    lands = [((N_CHIPS,) + a.shape, a.dtype) for a in arrs]
    held = lambda a: pltpu.with_memory_space_constraint(a, pltpu.HBM)
    outs = pl.pallas_call(
        body, name=name, in_specs=[HBM] * (2 * na) + [ANY],
        out_specs=[SEM, SEM] + [HBM] * (2 * na) + [pl.BlockSpec(memory_space=pltpu.VMEM)],
        out_shape=[pltpu.SemaphoreType.DMA((6 * na,)), pltpu.SemaphoreType.DMA((6 * na,))]
        + [pltpu.HBM(a.shape, a.dtype) for a in arrs] + [pltpu.HBM(s, d) for s, d in lands]
        + [jax.ShapeDtypeStruct((8, 128), F32)],
        input_output_aliases={i: 2 + i for i in range(2 * na)},
        compiler_params=pltpu.CompilerParams(has_side_effects=pltpu.SideEffectType.DATAFLOW_SIDE_EFFECTING),
    )(*[held(a) for a in arrs], *[held(lax.empty(s, d)) for s, d in lands], after)
    return outs[0], outs[1], outs[2:2 + na], outs[2 + na:2 + 2 * na], outs[-1]


def _gather_late_wait(send, recv, thru, land, after, *, name):
    na = len(thru)

    def body(*refs):
        ins, slots = refs[:na], refs[na:2 * na]
        send_sem, recv_sem = refs[2 * na], refs[2 * na + 1]
        x, y, c, _, chips = _place()
        for a in range(na):
            hr = thru[a].shape[0] // 2
            for j, (px, py) in enumerate(chips):
                for other in range(2):
                    k = (a * 3 + j) * 2 + other
                    half = c ^ other
                    cp = _remote(ins[a].at[pl.ds(c * hr, hr), :], slots[a].at[2 * px + py, pl.ds(half * hr, hr), :],
                                 send_sem.at[k], recv_sem.at[k], (px, py, half))
                    cp.wait_send()
                    cp.wait_recv()

    outs = pl.pallas_call(
        body, name=name, in_specs=[HBM] * (2 * na) + [SEM, SEM, ANY], out_specs=[HBM] * (2 * na),
        out_shape=[pltpu.HBM(a.shape, a.dtype) for a in list(thru) + list(land)],
        input_output_aliases={i: i for i in range(2 * na)},
        compiler_params=pltpu.CompilerParams(has_side_effects=pltpu.SideEffectType.DATAFLOW_SIDE_EFFECTING),
    )(*thru, *land, send, recv, after)
    return outs[na:]


def _row_chunks(rows, row_bytes, align=8):
    n = max(1, min(rows // align, -(-rows * row_bytes // DMA_CHUNK_BYTES)))
    per = -(-(-(-rows // n)) // align) * align
    return [(r, min(per, rows - r)) for r in range(0, rows, per)]


def _remote(src, dst, send, recv, to):
    return pltpu.make_async_remote_copy(src_ref=src, dst_ref=dst, send_sem=send, recv_sem=recv, device_id=to,
                                        device_id_type=MESH)


def _exchange_halves(arrs, *, name):
    na = len(arrs)

    def body(*refs):
        ins, got = refs[:na], refs[na:2 * na]
        send, recv = refs[2 * na:]
        x, y, c, sibling, _ = _place()
        for a in range(na):
            ns, rows, cols = arrs[a].shape
            hr = rows // 2
            for s in range(ns):
                for r0, nr in _row_chunks(hr, cols * arrs[a].dtype.itemsize, 16):
                    _remote(ins[a].at[s, pl.ds((1 - c) * hr + r0, nr), :], got[a].at[s, pl.ds(r0, nr), :],
                            send.at[a], recv.at[a], sibling).start()
        for a in range(na):
            hr = arrs[a].shape[1] // 2
            _remote(ins[a].at[:, pl.ds((1 - c) * hr, hr), :], got[a], send.at[a], recv.at[a], sibling).wait()

    return pl.pallas_call(
        body, name=name, in_specs=[ANY] * na, out_specs=[ANY] * na,
        out_shape=[jax.ShapeDtypeStruct((N_CHIPS, a.shape[1] // 2, a.shape[2]), a.dtype) for a in arrs],
        scratch_shapes=[pltpu.SemaphoreType.DMA((na,)), pltpu.SemaphoreType.DMA((na,))],
    )(*arrs)


def _scatter_sections(arrs, *, name):
    na = len(arrs)

    def body(*refs):
        ins, outs = refs[:na], refs[na:2 * na]
        send, recv = refs[2 * na:]
        x, y, c, _, chips = _place()
        for a in range(na):
            _, hr, cols = arrs[a].shape
            for j, (px, py) in enumerate(chips):
                for r0, nr in _row_chunks(hr, cols * arrs[a].dtype.itemsize, 16):
                    _remote(ins[a].at[2 * px + py, pl.ds(r0, nr), :], outs[a].at[j, pl.ds(r0, nr), :],
                            send.at[a * 3 + j], recv.at[a * 3 + j], (px, py, c)).start()
        for a in range(na):
            for j, (px, py) in enumerate(chips):
                _remote(ins[a].at[2 * px + py], outs[a].at[j], send.at[a * 3 + j], recv.at[a * 3 + j], (px, py, c)).wait()

    return pl.pallas_call(
        body, name=name, in_specs=[ANY] * na, out_specs=[ANY] * na,
        out_shape=[jax.ShapeDtypeStruct((N_CHIPS - 1,) + a.shape[1:], a.dtype) for a in arrs],
        scratch_shapes=[pltpu.SemaphoreType.DMA((3 * na,)), pltpu.SemaphoreType.DMA((3 * na,))],
    )(*arrs)


HBM = pl.BlockSpec(memory_space=pltpu.HBM)
SEM = pl.BlockSpec(memory_space=pltpu.SEMAPHORE)


def _scatter_start(arrs, *, name):
    na = len(arrs)

    def body(*refs):
        ins, land = refs[:na], refs[na:2 * na]
        send, recv = refs[2 * na], refs[2 * na + 1]
        token = refs[-1]
        x, y, c, _, chips = _place()
        for a in range(na):
            _, hr, cols = arrs[a].shape
            for j, (px, py) in enumerate(chips):
                for r0, nr in _row_chunks(hr, cols * arrs[a].dtype.itemsize, 16):
                    _remote(ins[a].at[2 * px + py, pl.ds(r0, nr), :], land[a].at[j, pl.ds(r0, nr), :],
                            send.at[a * 3 + j], recv.at[a * 3 + j], (px, py, c)).start()
        token[...] = jnp.zeros_like(token)

    slots = [((N_CHIPS - 1,) + a.shape[1:], a.dtype) for a in arrs]
    held = lambda a: pltpu.with_memory_space_constraint(a, pltpu.HBM)
    outs = pl.pallas_call(
        body, name=name, in_specs=[HBM] * (2 * na),
        out_specs=[SEM, SEM] + [HBM] * (2 * na) + [pl.BlockSpec(memory_space=pltpu.VMEM)],
        out_shape=[pltpu.SemaphoreType.DMA((3 * na,)), pltpu.SemaphoreType.DMA((3 * na,))]
        + [pltpu.HBM(a.shape, a.dtype) for a in arrs] + [pltpu.HBM(s, d) for s, d in slots]
        + [jax.ShapeDtypeStruct((8, 128), F32)],
        input_output_aliases={i: 2 + i for i in range(2 * na)},
        compiler_params=pltpu.CompilerParams(has_side_effects=pltpu.SideEffectType.DATAFLOW_SIDE_EFFECTING),
    )(*[held(a) for a in arrs], *[held(lax.empty(s, d)) for s, d in slots])
    return outs[0], outs[1], outs[2:2 + na], outs[2 + na:2 + 2 * na], outs[-1]


def _scatter_wait(send, recv, thru, land, after, *, name):
    na = len(thru)

    def body(*refs):
        ins, slots = refs[:na], refs[na:2 * na]
        send_sem, recv_sem = refs[2 * na], refs[2 * na + 1]
        x, y, c, _, chips = _place()
        for a in range(na):
            for j, (px, py) in enumerate(chips):
                cp = _remote(ins[a].at[2 * px + py], slots[a].at[j], send_sem.at[a * 3 + j], recv_sem.at[a * 3 + j],
                             (px, py, c))
                cp.wait_send()
                cp.wait_recv()

    outs = pl.pallas_call(
        body, name=name, in_specs=[HBM] * (2 * na) + [SEM, SEM, ANY], out_specs=[HBM] * (2 * na),
        out_shape=[pltpu.HBM(a.shape, a.dtype) for a in list(thru) + list(land)],
        input_output_aliases={i: i for i in range(2 * na)},
        compiler_params=pltpu.CompilerParams(has_side_effects=pltpu.SideEffectType.DATAFLOW_SIDE_EFFECTING),
    )(*thru, *land, send, recv, after)
    return outs[:na], outs[na:]


def _finish_reduce(big, small, *, name):
    nb = len(big)

    def body(*refs):
        outs, sm = refs[nb + 1:2 * nb + 1], refs[2 * nb + 1]
        send, recv = refs[2 * nb + 2:]
        x, y, c, sibling, chips = _place()
        blk = lambda px, py, pc: sm.at[4 * px + 2 * py + pc]
        for a in range(nb):
            _, hr, cols = big[a].shape
            for r0, nr in _row_chunks(hr, cols * 4):
                _remote(outs[a].at[c, pl.ds(r0, nr), :], outs[a].at[c, pl.ds(r0, nr), :], send.at[a], recv.at[a],
                        sibling).start()
        first = [_remote(blk(x, y, c), blk(x, y, c), send.at[nb], recv.at[nb], sibling)]
        first += [_remote(blk(x, y, c), blk(x, y, c), send.at[nb + 1 + j], recv.at[nb + 1 + j], (*chip, c))
                  for j, chip in enumerate(chips)]
        for cp in first:
            cp.start()
        passed = []
        for j, (px, py) in enumerate(chips):
            _remote(blk(px, py, c), blk(px, py, c), send.at[nb + 1 + j], recv.at[nb + 1 + j], sibling).wait_recv()
            fw = _remote(blk(px, py, c), blk(px, py, c), send.at[nb + 4 + j], recv.at[nb + 4 + j], sibling)
            fw.start()
            passed.append(fw)
        for a in range(nb):
            _remote(outs[a].at[c], outs[a].at[1 - c], send.at[a], recv.at[a], sibling).wait()
        _remote(blk(x, y, c), blk(x, y, 1 - c), send.at[nb], recv.at[nb], sibling).wait_recv()
        for j, (px, py) in enumerate(chips):
            _remote(blk(px, py, c), blk(px, py, 1 - c), send.at[nb + 4 + j], recv.at[nb + 4 + j], sibling).wait_recv()
        for cp in first + passed:
            cp.wait_send()

    n_remote = nb + 7
    arrs = list(big) + [small]
    outs = pl.pallas_call(
        body, name=name, in_specs=[ANY] * (nb + 1), out_specs=[ANY] * (nb + 1),
        out_shape=[jax.ShapeDtypeStruct(a.shape, F32) for a in arrs],
        input_output_aliases={i: i for i in range(nb + 1)},
        scratch_shapes=[pltpu.SemaphoreType.DMA((n_remote,)), pltpu.SemaphoreType.DMA((n_remote,))],
    )(*arrs)
    return outs[:nb], outs[nb]


def _reduce_start(sectioned, where):
    got = _exchange_halves(sectioned, name="rs_exchange_halves")
    chip_sum = [_add_sibling(a, g, where, BF16, name=f"rs_add_sibling_{i}")
                for i, (a, g) in enumerate(zip(sectioned, got))]
    send, recv, thru, land, token = _scatter_start(chip_sum, name="rs_scatter_start")
    return (send, recv, thru, land), token


def _reduce_finish(handle, small, where, after):
    send, recv, thru, land = handle
    got = _exchange_halves([small], name="rs_exchange_small")
    small_sum = _add_sibling(small, got[0], where, F32, name="rs_add_sibling_small")
    small_slots = _scatter_sections([small_sum], name="rs_scatter_small")[0]
    own, slots = _scatter_wait(send, recv, thru, land, after, name="rs_scatter_wait")
    red = [_sum_chips(p, s, where, 2, 0, name=f"rs_sum_chips_{i}") for i, (p, s) in enumerate(zip(own, slots))]
    red_small = _sum_chips(small_sum, small_slots, where, 2 * N_CHIPS, 2, name="rs_sum_chips_small")
    big, sm = _finish_reduce(red, red_small, name="rs_finish")
    return big, sm.reshape(-1, sm.shape[-1])


def kernel(x, c, ctx, c_ctx, w_mod, b_mod, g_pre, g_post, w_in, w_conv, a_log, dt_bias, g_onorm, gm_ln_g, gm_ln_b, w_sp, b_sp, w_pa, w_pb, w_out, loss_target, m_c_ctx, m_w_mod, m_b_mod, m_g_pre, m_g_post, m_w_in, m_w_conv, m_a_log, m_dt_bias, m_g_onorm, m_gm_ln_g, m_gm_ln_b, m_w_sp, m_b_sp, m_w_pa, m_w_pb, m_w_out, v_c_ctx, v_w_mod, v_b_mod, v_g_pre, v_g_post, v_w_in, v_w_conv, v_a_log, v_dt_bias, v_g_onorm, v_gm_ln_g, v_gm_ln_b, v_w_sp, v_b_sp, v_w_pa, v_w_pb, v_w_out):
    names = ["c_ctx", "w_mod", "b_mod", "g_pre", "g_post", "w_in", "w_conv", "a_log", "dt_bias", "g_onorm", "gm_ln_g",
             "gm_ln_b", "w_sp", "b_sp", "w_pa", "w_pb", "w_out"]
    w = dict(zip(names, (c_ctx, w_mod, b_mod, g_pre, g_post, w_in, w_conv, a_log, dt_bias, g_onorm, gm_ln_g, gm_ln_b,
                         w_sp, b_sp, w_pa, w_pb, w_out)))
    m = dict(zip(names, (m_c_ctx, m_w_mod, m_b_mod, m_g_pre, m_g_post, m_w_in, m_w_conv, m_a_log, m_dt_bias, m_g_onorm,
                         m_gm_ln_g, m_gm_ln_b, m_w_sp, m_b_sp, m_w_pa, m_w_pb, m_w_out)))
    v = dict(zip(names, (v_c_ctx, v_w_mod, v_b_mod, v_g_pre, v_g_post, v_w_in, v_w_conv, v_a_log, v_dt_bias, v_g_onorm,
                         v_gm_ln_g, v_gm_ln_b, v_w_sp, v_b_sp, v_w_pa, v_w_pb, v_w_out)))
    xy = 2 * lax.axis_index("x") + lax.axis_index("y")
    where = jnp.stack([lax.axis_index("c"), xy, 2 * xy + lax.axis_index("c")]).astype(jnp.int32)

    shards = [a[0].astype(BF16) for a in (w_mod, w_in)]
    gathered, wconv_all = _gather_shards(shards, w_conv[0], name="gather_weights")
    own = lambda full, shard: lax.dynamic_update_slice(full, shard[None], (xy, 0, 0))
    wm_all, win_all = [own(f, s) for f, s in zip(gathered, shards)]
    wconv_all = own(wconv_all, w_conv[0])
    late_shards = [a[0].astype(BF16) for a in (w_pa, w_pb, w_out)]
    late = _gather_late_start(late_shards, wconv_all, name="gather_late_start")

    def late_weights(after):
        lands = _gather_late_wait(*late[:4], after, name="gather_late_wait")
        return [own(f, s).reshape(D, D) for f, s in zip(lands, late_shards)]
    w_conv_f = jnp.concatenate([wconv_all[s] for s in range(N_CHIPS)], axis=1)
    shard_cols = IN_COLS // N_CHIPS
    cut = OFF_A - shard_cols
    assert 0 < cut and cut + 4 * H < shard_cols
    w_qkv = jnp.concatenate([win_all[0], win_all[1][:, :cut]], axis=1)
    w_ab = jnp.pad(win_all[1][:, cut:cut + 4 * H], ((0, 0), (0, DH - 4 * H)))
    w_rest = jnp.concatenate([win_all[1][:, cut + 4 * H:], win_all[2], win_all[3]], axis=1)

    blk = D // N_CHIPS
    rest_cut = shard_cols - cut - 4 * H
    big_names = ("w_in", "w_pa", "w_pb", "w_out")
    in_flight = []

    def start_reduce(grads):
        dw_qkv, dw_ab, dw_rest = grads["w_in"]
        g_win = jnp.stack([dw_qkv[:, :shard_cols],
                           jnp.concatenate([dw_qkv[:, shard_cols:], dw_ab, dw_rest[:, :rest_cut]], axis=1),
                           dw_rest[:, rest_cut:rest_cut + shard_cols], dw_rest[:, rest_cut + shard_cols:]]).astype(BF16)
        sectioned = [g_win] + [grads[k].reshape(N_CHIPS, blk, D) for k in ("w_pa", "w_pb", "w_out")]
        handle, token = _reduce_start(sectioned, where)
        in_flight.append(handle)
        return token

    loss_local, grad_x, g = _local_step(
        x[0], c, ctx[0], c_ctx, loss_target[0], wm_all, b_mod, g_pre, g_post, w_qkv, w_ab, w_rest,
        w_conv_f, a_log[0], dt_bias[0], g_onorm, gm_ln_g, gm_ln_b, w_sp[0], b_sp[0],
        late_weights, on_large_gradients=start_reduce, run_after=late[4])
    g["loss"] = loss_local
    g["_pad"] = jnp.zeros((SMALL_LAYOUT["_pad"][1],), F32)
    g["mod_factors"] = lax.dynamic_update_slice(jnp.zeros((2 * N_CHIPS, 4 * D), F32),
                                                jnp.concatenate(g["mod_factors"])[None], (where[2], 0))
    tail = jnp.zeros((N_CHIPS * SMALL_ROWS * 128 - sum(s for _, s in SMALL_LAYOUT.values()),), F32)
    flat = jnp.concatenate([g[k].reshape(-1) for k in SMALL_LAYOUT] + [tail])
    reduced, gr_small = _reduce_finish(in_flight[0], flat.reshape(N_CHIPS, SMALL_ROWS, 128), where, g["b_mod"])
    gr_tiny = gr_small[TINY_ROW0:]

    def entry(arr, k):
        off, size = SMALL_LAYOUT[k]
        row, col = off // 128 - TINY_ROW0, off % 128
        return arr[row:row + size // 128].reshape(-1) if size >= 128 else arr[row, col:col + size]

    factors = entry(gr_tiny, "mod_factors").reshape(2 * N_CHIPS, 4 * D)
    pad_rows = lambda a: jnp.pad(a, ((0, 16 - a.shape[0]), (0, 0)))
    lhs = pad_rows(jnp.concatenate([factors[:, :D], g["silu_c_ctx"]], axis=0))
    rhs = pad_rows(jnp.concatenate([factors[:, D:], entry(gr_tiny, "dm_ctx")[None]], axis=0))
    mod_cols = 3 * D // N_CHIPS
    gr_wm = _mm(lhs, lax.dynamic_slice(rhs, (0, xy * mod_cols), (16, mod_cols)), ta=True, name="dw_mod")

    res = {k: _adamw(w[k], gr, m[k], v[k], name=f"adamw_{k}") for k, gr in zip(big_names[1:], reduced[1:])}
    flip = lambda a: jnp.swapaxes(a, 1, 2)
    res["w_in"] = [flip(r) for r in _adamw(flip(w_in), flip(reduced[0].reshape(w_in.shape)), flip(m_w_in), flip(v_w_in),
                                           name="adamw_w_in")]
    res["w_mod"] = _adamw(w_mod[0], gr_wm, m_w_mod[0], v_w_mod[0], name="adamw_w_mod")
    res["w_sp"] = _adamw(w_sp.reshape(-1, 128), gr_small, m_w_sp.reshape(-1, 128), v_w_sp.reshape(-1, 128),
                         name="adamw_w_sp")
    tiny = [k for k, (off, _) in SMALL_LAYOUT.items() if TINY_ROW0 <= off // 128 < TINY_ROW0 + TINY_ROWS]
    tiny_names = [k for k in tiny if k not in ("loss", "_pad")]

    def pack(src):
        parts = [src[k].reshape(-1) if k in tiny_names else jnp.zeros((SMALL_LAYOUT[k][1],), F32) for k in tiny]
        return jnp.concatenate(parts).reshape(TINY_ROWS, 128)

    tiny_res = _adamw(pack(w), gr_tiny[:TINY_ROWS], pack(m), pack(v), name="adamw_tiny")
    for k in tiny_names:
        res[k] = [entry(r, k) for r in tiny_res]
    g_conv = lax.dynamic_slice(entry(gr_tiny, "w_conv").reshape(3, 3 * D), (0, xy * (3 * D // N_CHIPS)),
                               (3, 3 * D // N_CHIPS))
    conv_res = _adamw(jnp.pad(w_conv[0], ((0, 5), (0, 0))), jnp.pad(g_conv, ((0, 5), (0, 0))),
                      jnp.pad(m_w_conv[0], ((0, 5), (0, 0))), jnp.pad(v_w_conv[0], ((0, 5), (0, 0))), name="adamw_w_conv")
    res["w_conv"] = [r[:3] for r in conv_res]
    res = {k: [r.reshape(w[k].shape) for r in res[k]] for k in names}

    out = [entry(gr_tiny, "loss").reshape(()), grad_x[None]]
    for i in range(4):
        out += [res[k][i] for k in names]
    return tuple(out)
```

```python
import functools

import jax
import jax.numpy as jnp
from jax import lax
from jax.experimental import pallas as pl
from jax.experimental.pallas import tpu as pltpu

F32 = jnp.float32
BF16 = jnp.bfloat16
HI = lax.Precision.HIGHEST
MESH = pl.DeviceIdType.MESH

D = 1024
H = 8
DH = 128
CH = 64
LOG_CH = 6
PAIR = 2 * CH
GM = 128
assert 1 << LOG_CH == CH and PAIR == DH
PREC_POWERS = ((lax.Precision.HIGH, lax.Precision.HIGH),) * 3 + ((None, None),) * 2
assert len(PREC_POWERS) == LOG_CH - 1
EPS = 1e-6
N_CHIPS = 4
OFF_A = 3 * D
OFF_ZB = OFF_A + 4 * H
IN_COLS = OFF_ZB + 6 * D
VMEM_LIMIT_V7X = 56 * 1024 * 1024
DMA_CHUNK_BYTES = 2 * 1024 * 1024

ADAM_LR, ADAM_B1, ADAM_B2, ADAM_EPS, ADAM_WD, ADAM_STEP = 0.001, 0.9, 0.999, 1e-08, 0.01, 10

SMALL_LAYOUT = {}
_off = 0
for _n, _s in (("w_sp", H * GM * GM), ("c_ctx", D), ("b_mod", 3 * D), ("g_pre", D), ("g_post", D), ("gm_ln_g", D),
               ("gm_ln_b", D), ("b_sp", H * GM), ("g_onorm", DH), ("a_log", 2 * H), ("dt_bias", 2 * H), ("loss", 1),
               ("_pad", 128 - 4 * H - 1 + 6 * 128), ("w_conv", 3 * 3 * D), ("dm_ctx", 3 * D), ("mod_factors", 8 * 4 * D)):
    SMALL_LAYOUT[_n] = (_off, _s)
    _off += _s
SMALL_ROWS = 368
assert N_CHIPS * SMALL_ROWS * 128 >= _off and (SMALL_ROWS // 2) % 8 == 0 and _off % 128 == 0
TINY_ROW0 = SMALL_LAYOUT["c_ctx"][0] // 128
TINY_ROWS = SMALL_LAYOUT["w_conv"][0] // 128 - TINY_ROW0
assert TINY_ROWS % 8 == 0


def _params(sem=None):
    return pltpu.CompilerParams(dimension_semantics=sem, vmem_limit_bytes=VMEM_LIMIT_V7X)


def _tile(n, cands=(256, 128, 64, 32, 16, 8)):
    if n <= cands[0]:
        return n
    for cand in cands:
        if n % cand == 0 and cand >= 64:
            return cand
    return max(d for d in range(8, cands[0], 8) if n % d == 0)


def _silu(x):
    return x * jax.nn.sigmoid(x)


def _gelu(x):
    return 0.5 * x * (1.0 + jnp.tanh(0.7978845608028654 * (x + 0.044715 * (x * x * x))))


def _mm(a, b, *, ta=False, tb=False, out_dtype=F32, tm=1024, tn=1024, tk=1024, add=None, add_from=0,
        b_sections=False, out_sections=False, after=None, name):
    m, k = (a.shape[1], a.shape[0]) if ta else a.shape
    if b_sections:
        sect = b.shape[2]
        n = b.shape[1] if tb else b.shape[0] * sect
        tn, tk = (tn, sect) if tb else (sect, tk)
    else:
        n = b.shape[0] if tb else b.shape[1]
    tm, tn, tk = min(tm, m), min(tn, n), min(tk, k)
    assert m % tm == 0 and n % tn == 0 and k % tk == 0, (name, m, n, k, tm, tn, tk)
    nk = k // tk
    dims = (((0,) if ta else (1,), (1,) if tb else (0,)), ((), ()))
    has_add = add is not None
    assert not has_add or add.shape == (m, n - add_from * tn), (name, add.shape)

    def body(*refs):
        a_ref, b_ref = refs[:2]
        o_ref = refs[2 + int(has_add) + int(after is not None)]
        acc_ref = refs[-1]
        kk = pl.program_id(2)
        bv = b_ref[0] if b_sections else b_ref[...]
        part = lax.dot_general(a_ref[...].astype(BF16), bv.astype(BF16), dims, preferred_element_type=F32)

        def finish(res):
            if has_add:
                res = res + jnp.where(pl.program_id(1) >= add_from, refs[2][...], 0.0)
            if out_sections:
                o_ref[0] = res.astype(out_dtype)
            else:
                o_ref[...] = res.astype(out_dtype)

        if nk == 1:
            finish(part)
            return

        @pl.when(kk == 0)
        def _():
            acc_ref[...] = part

        @pl.when((kk > 0) & (kk < nk - 1))
        def _():
            acc_ref[...] += part

        @pl.when(kk == nk - 1)
        def _():
            finish(acc_ref[...] + part)

    a_spec = pl.BlockSpec((tk, tm), lambda i, j, q: (q, i)) if ta else pl.BlockSpec((tm, tk), lambda i, j, q: (i, q))
    if b_sections:
        b_spec = (pl.BlockSpec((1, tn, tk), lambda i, j, q: (q, j, 0)) if tb
                  else pl.BlockSpec((1, tk, tn), lambda i, j, q: (j, q, 0)))
    else:
        b_spec = pl.BlockSpec((tn, tk), lambda i, j, q: (j, q)) if tb else pl.BlockSpec((tk, tn), lambda i, j, q: (q, j))
    add_spec = [pl.BlockSpec((tm, tn), lambda i, j, q: (i, jnp.maximum(j - add_from, 0)))] if has_add else []
    if out_sections:
        out_spec, out_shape = pl.BlockSpec((1, tm, tn), lambda i, j, q: (j, i, 0)), (n // tn, m, tn)
    else:
        out_spec, out_shape = pl.BlockSpec((tm, tn), lambda i, j, q: (i, j)), (m, n)
    return pl.pallas_call(
        body, name=name, grid=(m // tm, n // tn, nk),
        in_specs=[a_spec, b_spec] + add_spec + ([pl.BlockSpec(memory_space=pl.ANY)] if after is not None else []),
        out_specs=out_spec, out_shape=jax.ShapeDtypeStruct(out_shape, out_dtype),
        scratch_shapes=[pltpu.VMEM((tm, tn), F32)] if nk > 1 else [],
        compiler_params=_params(("parallel", "parallel", "arbitrary")),
    )(*([a, b] + ([add] if has_add else []) + ([after] if after is not None else [])))


def _h_fn(x, g, m):
    shift, scale = m[:, 0:D], m[:, D:2 * D]
    r = lax.rsqrt(jnp.mean(x * x, axis=-1, keepdims=True) + EPS)
    return (x * r * g) * (1.0 + scale) + shift


def _norm_fwd(x, g, mod, bmod, *, name):
    rows = x.shape[0]
    t = min(512, rows)

    def body(x_ref, g_ref, m_ref, b_ref, h_ref):
        h_ref[...] = _h_fn(x_ref[...], g_ref[...], m_ref[...] + b_ref[...]).astype(BF16)

    vec = lambda w: pl.BlockSpec((1, w), lambda i: (0, 0))
    return pl.pallas_call(
        body, name=name, grid=(rows // t,),
        in_specs=[pl.BlockSpec((t, D), lambda i: (i, 0)), vec(D), vec(3 * D), vec(3 * D)],
        out_specs=pl.BlockSpec((t, D), lambda i: (i, 0)),
        out_shape=jax.ShapeDtypeStruct((rows, D), BF16),
        compiler_params=_params(("parallel",)),
    )(x, g, mod, bmod)


def _norm_bwd(x, g, mod, bmod, dh_parts, resid, *, name):
    rows = x.shape[0]
    t = min(512, rows)
    n_parts = len(dh_parts)
    has_resid = resid is not None

    def body(*refs):
        x_ref, g_ref, m_ref, b_ref = refs[:4]
        parts = refs[4:4 + n_parts]
        r_ref = refs[4 + n_parts] if has_resid else None
        dx_ref, dg_ref, dm_ref = refs[-3:]
        i = pl.program_id(0)
        dh = parts[0][...]
        for p in parts[1:]:
            dh = dh + p[...]
        _, vjp = jax.vjp(_h_fn, x_ref[...], g_ref[...], m_ref[...] + b_ref[...])
        dx, dg, dm = vjp(dh)
        if has_resid:
            dx = dx + r_ref[...]
        dx_ref[...] = dx

        @pl.when(i == 0)
        def _():
            dg_ref[...] = dg
            dm_ref[...] = dm

        @pl.when(i > 0)
        def _():
            dg_ref[...] += dg
            dm_ref[...] += dm

    vec = lambda w: pl.BlockSpec((1, w), lambda i: (0, 0))
    tile = pl.BlockSpec((t, D), lambda i: (i, 0))
    ins = [x, g, mod, bmod, *dh_parts] + ([resid] if has_resid else [])
    return pl.pallas_call(
        body, name=name, grid=(rows // t,),
        in_specs=[tile, vec(D), vec(3 * D), vec(3 * D)] + [tile] * (n_parts + int(has_resid)),
        out_specs=[tile, vec(D), vec(3 * D)],
        out_shape=[jax.ShapeDtypeStruct((rows, D), F32), jax.ShapeDtypeStruct((1, D), F32),
                   jax.ShapeDtypeStruct((1, 3 * D), F32)],
        compiler_params=_params(("arbitrary",)),
    )(*ins)


def _conv_tile(u_ref, r0, t, rows, w0, w1, w2):
    u = u_ref[pl.ds(r0, t), :]
    prev8 = u_ref[pl.ds(pl.multiple_of(jnp.maximum(r0 - 8, 0), 8), 8), :]
    next8 = u_ref[pl.ds(pl.multiple_of(jnp.minimum(r0 + t, rows - 8), 8), 8), :]
    r8 = lax.broadcasted_iota(jnp.int32, (8, DH), 0)
    prev_row = jnp.sum(jnp.where(r8 == 7, prev8, 0.0), axis=0, keepdims=True)
    next_row = jnp.sum(jnp.where(r8 == 0, next8, 0.0), axis=0, keepdims=True)
    prev_row = jnp.where(r0 > 0, prev_row, 0.0)
    next_row = jnp.where(r0 + t < rows, next_row, 0.0)
    ri = lax.broadcasted_iota(jnp.int32, (t, DH), 0)
    um1 = jnp.where(ri == 0, prev_row, pltpu.roll(u, 1, 0))
    up1 = jnp.where(ri == t - 1, next_row, pltpu.roll(u, t - 1, 0))
    return w0 * um1 + w1 * u + w2 * up1, um1, u, up1


def _rowlocal(z, norm):
    y = _silu(z)
    return y * lax.rsqrt(jnp.sum(y * y, axis=-1, keepdims=True) + EPS) if norm else y


def _prep_fwd(p, wconv, n_norm, *, name):
    rows, nb = p.shape[0], p.shape[1] // DH
    t = min(512, rows)
    out = None
    for norm, b0, b1 in ((True, 0, n_norm), (False, n_norm, nb)):
        def body(u_ref, w_ref, *rest, norm=norm):
            o_ref = rest[-1]
            w0, w1, w2 = w_ref[0:1, :], w_ref[1:2, :], w_ref[2:3, :]

            def step(s, carry):
                r0 = pl.multiple_of(s * t, t)
                z, _, _, _ = _conv_tile(u_ref, r0, t, rows, w0, w1, w2)
                o_ref[0, pl.ds(r0, t), :] = _rowlocal(z, norm)
                return carry

            lax.fori_loop(0, rows // t, step, 0)

        held = [] if out is None else [out]
        out = pl.pallas_call(
            body, name=f"{name}_{'norm' if norm else 'plain'}", grid=(b1 - b0,),
            in_specs=[pl.BlockSpec((rows, DH), lambda j, b0=b0: (0, b0 + j)),
                      pl.BlockSpec((3, DH), lambda j, b0=b0: (0, b0 + j))] + [ANY] * len(held),
            out_specs=pl.BlockSpec((1, rows, DH), lambda j, b0=b0: (b0 + j, 0, 0)),
            out_shape=jax.ShapeDtypeStruct((nb, rows, DH), F32),
            input_output_aliases={2: 0} if held else {},
            compiler_params=_params(("parallel",)),
        )(p, wconv, *held)
    return out


def _prep_bwd(p, wconv, d_a, d_b, n_norm, *, name):
    rows, nb = p.shape[0], p.shape[1] // DH
    t = min(512, rows)
    outs = []
    for norm, b0, b1 in ((True, 0, n_norm), (False, n_norm, nb)):
        def body(u_ref, w_ref, da_ref, db_ref, *rest, norm=norm):
            du_ref, dw_ref, dz_ref = rest[-3:]
            w0, w1, w2 = w_ref[0:1, :], w_ref[1:2, :], w_ref[2:3, :]

            def step1(s, carry):
                a0, a1, a2 = carry
                r0 = pl.multiple_of(s * t, t)
                z, um1, u, up1 = _conv_tile(u_ref, r0, t, rows, w0, w1, w2)
                _, vjp = jax.vjp(lambda zz: _rowlocal(zz, norm), z)
                (dz,) = vjp(da_ref[0, pl.ds(r0, t), :] + db_ref[0, pl.ds(r0, t), :])
                dz_ref[pl.ds(r0, t), :] = dz
                red = lambda v: jnp.sum(v, axis=0, keepdims=True)
                return a0 + red(dz * um1), a1 + red(dz * u), a2 + red(dz * up1)

            zero = jnp.zeros((1, DH), F32)
            a0, a1, a2 = lax.fori_loop(0, rows // t, step1, (zero, zero, zero))
            dw_ref[0:1, :] = a0
            dw_ref[1:2, :] = a1
            dw_ref[2:3, :] = a2

            def step2(s, carry):
                r0 = pl.multiple_of(s * t, t)
                du, _, _, _ = _conv_tile(dz_ref, r0, t, rows, w2, w1, w0)
                du_ref[pl.ds(r0, t), :] = du.astype(BF16)
                return carry

            lax.fori_loop(0, rows // t, step2, 0)

        col = pl.BlockSpec((rows, DH), lambda j, b0=b0: (0, b0 + j))
        w_spec = pl.BlockSpec((3, DH), lambda j, b0=b0: (0, b0 + j))
        d_spec = pl.BlockSpec((1, rows, DH), lambda j, b0=b0: (b0 + j, 0, 0))
        outs = pl.pallas_call(
            body, name=f"{name}_{'norm' if norm else 'plain'}", grid=(b1 - b0,),
            in_specs=[col, w_spec, d_spec, d_spec] + [ANY] * len(outs), out_specs=[col, w_spec],
            out_shape=[jax.ShapeDtypeStruct((rows, nb * DH), BF16), jax.ShapeDtypeStruct((3, nb * DH), F32)],
            input_output_aliases={4: 0, 5: 1} if outs else {},
            scratch_shapes=[pltpu.VMEM((rows, DH), F32)],
            compiler_params=_params(("parallel",)),
        )(p, wconv, d_a, d_b, *outs)
    return outs


def _gates_fn(pab, avec, dvec):
    t = pab.shape[0]
    lane = lax.broadcasted_iota(jnp.int32, pab.shape, 1)
    xg = pab + dvec
    sp = jnp.maximum(xg, 0.0) + jnp.log(1.0 + jnp.exp(-jnp.abs(xg)))
    g = jnp.where(lane < 2 * H, -jnp.exp(avec) * sp, 0.0)
    ii = lax.broadcasted_iota(jnp.int32, (t, t), 0)
    jj = lax.broadcasted_iota(jnp.int32, (t, t), 1)
    same = (ii >> LOG_CH) == (jj >> LOG_CH)
    cum_f = _dot(jnp.where(same & (jj <= ii), 1.0, 0.0), g, precision=HI)
    cum_r = _dot(jnp.where(same & (jj >= ii), 1.0, 0.0), g, precision=HI)
    return jnp.where(lane < H, cum_f, jnp.where(lane < 2 * H, cum_r, jnp.where(lane < 4 * H, jax.nn.sigmoid(pab), 0.0)))


def _gates_fwd(pab, avec, dvec, *, name):
    rows = pab.shape[0]
    t = min(512, rows)

    def body(p_ref, a_ref, d_ref, o_ref):
        o_ref[...] = _gates_fn(p_ref[...], a_ref[...], d_ref[...])

    vec = pl.BlockSpec((1, DH), lambda i: (0, 0))
    tile = pl.BlockSpec((t, DH), lambda i: (i, 0))
    return pl.pallas_call(
        body, name=name, grid=(rows // t,), in_specs=[tile, vec, vec], out_specs=tile,
        out_shape=jax.ShapeDtypeStruct((rows, DH), F32), compiler_params=_params(("parallel",)),
    )(pab, avec, dvec)


def _gates_bwd(pab, avec, dvec, d_a, d_b, *, name):
    rows = pab.shape[0]
    t = min(512, rows)

    def body(p_ref, a_ref, d_ref, da_ref, db_ref, dp_ref, dav_ref, ddv_ref):
        i = pl.program_id(0)
        _, vjp = jax.vjp(_gates_fn, p_ref[...], a_ref[...], d_ref[...])
        dp, dav, ddv = vjp(da_ref[...] + db_ref[...])
        dp_ref[...] = dp.astype(BF16)

        @pl.when(i == 0)
        def _():
            dav_ref[...] = dav
            ddv_ref[...] = ddv

        @pl.when(i > 0)
        def _():
            dav_ref[...] += dav
            ddv_ref[...] += ddv

    vec = pl.BlockSpec((1, DH), lambda i: (0, 0))
    tile = pl.BlockSpec((t, DH), lambda i: (i, 0))
    return pl.pallas_call(
        body, name=name, grid=(rows // t,), in_specs=[tile, vec, vec, tile, tile], out_specs=[tile, vec, vec],
        out_shape=[jax.ShapeDtypeStruct((rows, DH), BF16), jax.ShapeDtypeStruct((1, DH), F32),
                   jax.ShapeDtypeStruct((1, DH), F32)],
        compiler_params=_params(("arbitrary",)),
    )(pab, avec, dvec, d_a, d_b)


def _dot_general(a, b, dims, precision=None):
    return lax.dot_general(a, b, (dims, ((), ())), precision=precision, preferred_element_type=F32)


@functools.partial(jax.custom_vjp, nondiff_argnums=(2,))
def _dot_bf16(a, b, dims):
    return _dot_general(a.astype(BF16), b.astype(BF16), dims)


def _dot_bf16_fwd(a, b, dims):
    return _dot_bf16(a, b, dims), (a, b)


def _dot_bf16_bwd(dims, res, g):
    a, b = res
    (ca,), (cb,) = dims
    da = _dot_bf16(g, b, ((1,), (1 - cb,))) if ca == 1 else _dot_bf16(b, g, ((1 - cb,), (1,)))
    db = _dot_bf16(a, g, ((1 - ca,), (0,))) if cb == 0 else _dot_bf16(g, a, ((0,), (1 - ca,)))
    return da, db


_dot_bf16.defvjp(_dot_bf16_fwd, _dot_bf16_bwd)


def _dot(a, b, dims=((1,), (0,)), precision=None):
    if precision is None and a.dtype == F32 and b.dtype == F32:
        return _dot_bf16(a, b, dims)
    return _dot_general(a, b, dims, precision)


_NT = ((1,), (1,))
_TN = ((0,), (0,))


@jax.custom_vjp
def _saved_inverse(neg_a, inv):
    return inv


def _saved_inverse_fwd(neg_a, inv):
    return inv, inv


def _saved_inverse_bwd(inv, d_inv):
    idx = range(len(inv))
    left = [_dot(inv[i], d_inv[i], _TN) for i in idx]
    d_neg_a = [_dot(left[i], inv[i], _NT) for i in idx]
    return d_neg_a, [jnp.zeros_like(t) for t in inv]


_saved_inverse.defvjp(_saved_inverse_fwd, _saved_inverse_bwd)


def _pairs(s, q, k, v, gcol, bcol, revs, inv_saved=None):
    idx = range(len(revs))
    ii = lax.broadcasted_iota(jnp.int32, (PAIR, PAIR), 0)
    jj = lax.broadcasted_iota(jnp.int32, (PAIR, PAIR), 1)
    same = (ii >> LOG_CH) == (jj >> LOG_CH)
    incl_d = (same & (ii >= jj), same & (ii <= jj))
    strict_d = (same & (ii > jj), same & (ii < jj))
    incl = [incl_d[int(r)] for r in revs]
    strict = [strict_d[int(r)] for r in revs]
    eye = jnp.where(ii == jj, 1.0, 0.0)
    gc_i = [jnp.broadcast_to(gcol[i], (PAIR, DH)) for i in idx]
    gc_j = [gc_i[i].T for i in idx]
    decay = [jnp.where(incl[i], jnp.exp(jnp.where(incl[i], gc_i[i] - gc_j[i], 0.0)), 0.0) for i in idx]
    b_b = [jnp.broadcast_to(bcol[i], (PAIR, DH)) for i in idx]
    kb = [k[i] * b_b[i] for i in idx]
    kk = [_dot(kb[i], k[i], _NT) for i in idx]
    bp = [jnp.where(strict[i], -kk[i] * decay[i], 0.0) for i in idx]
    if inv_saved is not None:
        inv = _saved_inverse(bp, inv_saved)
    else:
        low = bp
        for prec_sq, prec_acc in PREC_POWERS:
            bp = [_dot(bp[i], bp[i], precision=prec_sq) for i in idx]
            more = [_dot(low[i], bp[i], precision=prec_acc) for i in idx]
            low = [low[i] + bp[i] + more[i] for i in idx]
        inv = [eye + low[i] for i in idx]
    eg = [jnp.exp(gc_i[i]) for i in idx]
    sol = [_dot(inv[i], jnp.concatenate([v[i] * b_b[i], kb[i] * eg[i]], axis=1)) for i in idx]
    u_val = [sol[i][:, :DH] for i in idx]
    w_key = [sol[i][:, DH:] for i in idx]
    row = lax.broadcasted_iota(jnp.int32, (PAIR, 1), 0)
    has_q = q[0] is not None
    if has_q:
        qc = [q[i] * (DH ** -0.5) for i in idx]
        qk = [_dot(qc[i], k[i], _NT) for i in idx]
        attn = [qk[i] * decay[i] for i in idx]
        qd = [qc[i] * eg[i] for i in idx]
    outs = [[None, None] for _ in idx]
    zeros = jnp.zeros((CH, DH), F32)
    for step in range(2):
        cidx = [(1 - step) if revs[i] else step for i in idx]
        sl = [slice(c * CH, (c + 1) * CH) for c in cidx]
        last = [c * CH if revs[i] else c * CH + CH - 1 for i, c in zip(idx, cidx)]
        gl = [jnp.sum(jnp.where(row == last[i], gcol[i], 0.0), axis=0, keepdims=True) for i in idx]
        k_tail = [k[i][sl[i]] * jnp.exp(gl[i] - gc_i[i][sl[i]]) for i in idx]
        ws = [_dot(w_key[i][sl[i]], s[i]) for i in idx]
        v_new = [u_val[i][sl[i]] - ws[i] for i in idx]
        if has_q:
            v_pad = [jnp.concatenate([v_new[i], zeros] if cidx[i] == 0 else [zeros, v_new[i]], axis=0) for i in idx]
            o_state = [_dot(qd[i][sl[i]], s[i]) for i in idx]
            o_local = [_dot(attn[i][sl[i]], v_pad[i]) for i in idx]
            for i in idx:
                outs[i][cidx[i]] = o_state[i] + o_local[i]
        kv = [_dot(k_tail[i], v_new[i], _TN) for i in idx]
        s = [s[i] * jnp.exp(gl[i]) + kv[i] for i in idx]
    return s, ([jnp.concatenate(outs[i], axis=0) for i in idx] if has_q else None), inv


def _lane_col(tile, idx):
    lane = lax.broadcasted_iota(jnp.int32, tile.shape, 1)
    return jnp.sum(jnp.where(lane == idx, tile, 0.0), axis=1, keepdims=True)


def _gdn_fwd(qkv, gb, s0f, s0b, has_q, *, name):
    nb, rows, _ = qkv.shape
    n = rows // PAIR
    qoff = H if has_q else 0

    def body(qf_ref, qb_ref, gf_ref, gr_ref, s0f_ref, s0b_ref, of_ref, ob_ref, ssf_ref, ssb_ref, tsf_ref, tsb_ref,
             sf_ref, sb_ref):
        @pl.when(pl.program_id(0) == 0)
        def _():
            sf_ref[...] = s0f_ref[...]
            sb_ref[...] = s0b_ref[...]

        gtiles = (gf_ref[...], gr_ref[...])

        dirs = ((qf_ref, sf_ref, ssf_ref, of_ref), (qb_ref, sb_ref, ssb_ref, ob_ref))
        ts_refs = (tsf_ref, tsb_ref)

        work = [(h, d) for h in range(H) for d in range(2)]
        loaded = []
        for h, d in work:
            q_ref, s_ref, _, _ = dirs[d]
            loaded.append((s_ref[h], q_ref[h] if has_q else None, q_ref[qoff + h], q_ref[qoff + H + h],
                           _lane_col(gtiles[d], d * H + h), _lane_col(gtiles[d], 2 * H + d * H + h)))
        s_new, o, inv = _pairs(*[list(col) for col in zip(*loaded)], revs=[d == 1 for _, d in work])
        for i, (h, d) in enumerate(work):
            _, s_ref, ss_ref, o_ref = dirs[d]
            ss_ref[0, h] = loaded[i][0]
            ts_refs[d][0, h] = inv[i]
            s_ref[h] = s_new[i]
            o_ref[h] = o[i] if has_q else jnp.zeros((PAIR, DH), F32)

    fwd3 = lambda i: (0, i, 0)
    rev3 = lambda i: (0, n - 1 - i, 0)
    state = pl.BlockSpec((H, DH, DH), lambda i: (0, 0, 0))
    saved = pl.BlockSpec((1, H, DH, DH), lambda i: (i, 0, 0, 0))
    return pl.pallas_call(
        body, name=name, grid=(n,),
        in_specs=[pl.BlockSpec((nb, PAIR, DH), fwd3), pl.BlockSpec((nb, PAIR, DH), rev3),
                  pl.BlockSpec((PAIR, DH), lambda i: (i, 0)), pl.BlockSpec((PAIR, DH), lambda i: (n - 1 - i, 0)),
                  state, state],
        out_specs=[pl.BlockSpec((H, PAIR, DH), fwd3), pl.BlockSpec((H, PAIR, DH), rev3), saved, saved, saved, saved,
                   state, state],
        out_shape=[jax.ShapeDtypeStruct((H, rows, DH), F32)] * 2 + [jax.ShapeDtypeStruct((n, H, DH, DH), F32)] * 4
        + [jax.ShapeDtypeStruct((H, DH, DH), F32)] * 2,
        compiler_params=_params(("arbitrary",)),
    )(qkv, qkv, gb, gb, s0f, s0b)


def _gdn_bwd(qkv, gb, ssf, ssb, tsf, tsb, do, dsf, dsb, has_q, *, name):
    nb, rows, _ = qkv.shape
    n = rows // PAIR
    qoff = H if has_q else 0

    def body(qf_ref, qb_ref, gf_ref, gr_ref, ssf_ref, ssb_ref, tsf_ref, tsb_ref, dof_ref, dob_ref, dsf0_ref, dsb0_ref,
             dqf_ref, dqb_ref, dgf_ref, dgr_ref, dsf_ref, dsb_ref):
        ts_refs = (tsf_ref, tsb_ref)
        @pl.when(pl.program_id(0) == 0)
        def _():
            dsf_ref[...] = dsf0_ref[...]
            dsb_ref[...] = dsb0_ref[...]

        gtiles = (gf_ref[...], gr_ref[...])
        lane = lax.broadcasted_iota(jnp.int32, (PAIR, DH), 1)

        dirs = ((qf_ref, ssf_ref, dof_ref, dsf_ref, dqf_ref), (qb_ref, ssb_ref, dob_ref, dsb_ref, dqb_ref))

        work = [(h, d) for h in range(H) for d in range(2)]
        revs = [d == 1 for _, d in work]
        s_in, q_in, k_in, v_in, g_in, b_in, ds_out, do_out, inv_in = [], [], [], [], [], [], [], [], []
        for h, d in work:
            q_ref, ss_ref, do_ref, ds_ref, _ = dirs[d]
            s_in.append(ss_ref[0, h])
            inv_in.append(ts_refs[d][0, h])
            q_in.append(q_ref[h] if has_q else None)
            k_in.append(q_ref[qoff + h])
            v_in.append(q_ref[qoff + H + h])
            g_in.append(_lane_col(gtiles[d], d * H + h))
            b_in.append(_lane_col(gtiles[d], 2 * H + d * H + h))
            ds_out.append(ds_ref[h])
            do_out.append(do_ref[h] if has_q else None)
        if has_q:
            _, vjp = jax.vjp(lambda s_, q_, k_, v_, g_, b_: _pairs(s_, q_, k_, v_, g_, b_, revs, inv_in)[:2],
                             s_in, q_in, k_in, v_in, g_in, b_in)
            ds, dq, dk, dv, dg, db = vjp((ds_out, do_out))
        else:
            _, vjp = jax.vjp(lambda s_, k_, v_, g_, b_: _pairs(s_, q_in, k_, v_, g_, b_, revs, inv_in)[0],
                             s_in, k_in, v_in, g_in, b_in)
            ds, dk, dv, dg, db = vjp(ds_out)
        dgb = [jnp.zeros((PAIR, DH), F32)] * 2
        for i, (h, d) in enumerate(work):
            dq_ref, ds_ref = dirs[d][4], dirs[d][3]
            ds_ref[h] = ds[i]
            if has_q:
                dq_ref[h] = dq[i]
            dq_ref[qoff + h], dq_ref[qoff + H + h] = dk[i], dv[i]
            dgb[d] = (dgb[d] + jnp.where(lane == d * H + h, dg[i], 0.0)
                      + jnp.where(lane == 2 * H + d * H + h, db[i], 0.0))
        dgf_ref[...] = dgb[0]
        dgr_ref[...] = dgb[1]

    fwd3 = lambda i: (0, n - 1 - i, 0)
    rev3 = lambda i: (0, i, 0)
    state = pl.BlockSpec((H, DH, DH), lambda i: (0, 0, 0))
    saved = pl.BlockSpec((1, H, DH, DH), lambda i: (n - 1 - i, 0, 0, 0))
    gf_spec = pl.BlockSpec((PAIR, DH), lambda i: (n - 1 - i, 0))
    gr_spec = pl.BlockSpec((PAIR, DH), lambda i: (i, 0))
    return pl.pallas_call(
        body, name=name, grid=(n,),
        in_specs=[pl.BlockSpec((nb, PAIR, DH), fwd3), pl.BlockSpec((nb, PAIR, DH), rev3), gf_spec, gr_spec,
                  saved, saved, saved, saved,
                  pl.BlockSpec((H, PAIR, DH), fwd3), pl.BlockSpec((H, PAIR, DH), rev3), state, state],
        out_specs=[pl.BlockSpec((nb, PAIR, DH), fwd3), pl.BlockSpec((nb, PAIR, DH), rev3), gf_spec, gr_spec, state, state],
        out_shape=[jax.ShapeDtypeStruct((nb, rows, DH), F32)] * 2 + [jax.ShapeDtypeStruct((rows, DH), F32)] * 2
        + [jax.ShapeDtypeStruct((H, DH, DH), F32)] * 2,
        compiler_params=_params(("arbitrary",)),
    )(qkv, qkv, gb, gb, ssf, ssb, tsf, tsb, do, do, dsf, dsb)


def _stage1(zb, ua, va, za, o, gon, lng, lnb, wsp, bsp):
    gv = [_gelu(t) for t in va]
    mu = sum(jnp.sum(t, axis=-1, keepdims=True) for t in gv) * (1.0 / D)
    xc = [t - mu for t in gv]
    var = sum(jnp.sum(t * t, axis=-1, keepdims=True) for t in xc) * (1.0 / D)
    rs = lax.rsqrt(var + EPS)
    ya, yb = [], []
    for g in range(H):
        vv = xc[g] * rs * lng[g] + lnb[g]
        s = _dot(wsp[g], vv) + bsp[g]
        ya.append(_gelu(ua[g]) * s * _silu(za[g]))
        r = lax.rsqrt(jnp.mean(o[g] * o[g], axis=-1, keepdims=True) + EPS)
        yb.append(o[g] * r * gon * _silu(zb[g]))
    return ya, yb


def _stage2(ma, mb, ga, gb):
    return jax.nn.sigmoid(ga) * ma + jax.nn.sigmoid(gb) * mb


def _stage3(out, x, tgt, gpost, gate):
    r = out * lax.rsqrt(jnp.mean(out * out, axis=-1, keepdims=True) + EPS) * gpost
    err = x + gate * r - tgt
    return 0.5 * jnp.sum(jnp.mean(err * err, axis=-1, keepdims=True), axis=0, keepdims=True)


def _post(p_rest, o_f, o_b, x, tgt, mod, bmod, gon, lng, lnb, wsp, bspb, wpa, wpb, wout, gpost, *, name):
    rows = x.shape[0]
    n = rows // GM
    lanes = lambda g: slice(g * DH, (g + 1) * DH)
    bdot = lambda a, w_ref: _dot(a.astype(BF16), w_ref[...])
    bdot_t = lambda a, w_ref: _dot(a.astype(BF16), w_ref[...], _NT)

    def body(p_ref, of_ref, ob_ref, x_ref, t_ref, m_ref, bm_ref, gon_ref, lng_ref, lnb_ref, wsp_ref, bsp_ref,
             wpa_ref, wpb_ref, wout_ref, gp_ref,
             loss_ref, dp_ref, do_ref, dx_ref, ya_ref, yb_ref, mg_ref, dma_ref, dmb_ref, dout_ref,
             dvec_ref, dgon_ref, dwsp_ref, dbsp_ref):
        @pl.when(pl.program_id(0) == 0)
        def _():
            loss_ref[...] = jnp.zeros_like(loss_ref)
            dvec_ref[...] = jnp.zeros_like(dvec_ref)
            dgon_ref[...] = jnp.zeros_like(dgon_ref)
            dwsp_ref[...] = jnp.zeros_like(dwsp_ref)
            dbsp_ref[...] = jnp.zeros_like(dbsp_ref)

        piece = lambda blk: [p_ref[:, blk * D + g * DH: blk * D + (g + 1) * DH].astype(F32) for g in range(H)]
        zb, ua, va, za = piece(0), piece(1), piece(2), piece(3)
        o = [of_ref[g] + ob_ref[g] for g in range(H)]
        gon = gon_ref[...]
        lng = [lng_ref[:, lanes(g)] for g in range(H)]
        lnb = [lnb_ref[:, lanes(g)] for g in range(H)]
        wsp = [wsp_ref[g] for g in range(H)]
        bsp = [bsp_ref[g] for g in range(H)]
        (ya, yb), vjp1 = jax.vjp(_stage1, zb, ua, va, za, o, gon, lng, lnb, wsp, bsp)
        y_a, y_b = jnp.concatenate(ya, axis=1), jnp.concatenate(yb, axis=1)
        ma, mb = bdot(y_a, wpa_ref), bdot(y_b, wpb_ref)
        ga, gb = p_ref[:, 4 * D:5 * D].astype(F32), p_ref[:, 5 * D:6 * D].astype(F32)
        merged, vjp2 = jax.vjp(_stage2, ma, mb, ga, gb)
        out = bdot(merged, wout_ref)
        gate = m_ref[:, 2 * D:3 * D] + bm_ref[:, 2 * D:3 * D]
        loss, vjp3 = jax.vjp(_stage3, out, x_ref[...], t_ref[...], gp_ref[...], gate)
        loss_ref[...] += jnp.broadcast_to(loss, loss_ref.shape)

        dout, dx, _, dgpost, dgate = vjp3(jnp.ones((1, 1), F32))
        dx_ref[...] = dx
        dmerged = bdot_t(dout, wout_ref)
        dma, dmb, dga, dgb = vjp2(dmerged)
        dya, dyb = bdot_t(dma, wpa_ref), bdot_t(dmb, wpb_ref)
        dzb, dua, dva, dza, do, dgon, dlng, dlnb, dwsp, dbsp = vjp1(
            ([dya[:, lanes(g)] for g in range(H)], [dyb[:, lanes(g)] for g in range(H)]))

        for blk, dlist in enumerate((dzb, dua, dva, dza)):
            for g in range(H):
                dp_ref[:, blk * D + g * DH: blk * D + (g + 1) * DH] = dlist[g].astype(BF16)
        dp_ref[:, 4 * D:5 * D] = dga.astype(BF16)
        dp_ref[:, 5 * D:6 * D] = dgb.astype(BF16)
        for g in range(H):
            do_ref[g] = do[g]
            dwsp_ref[g] += dwsp[g]
            dbsp_ref[g] += dbsp[g]
            dvec_ref[2:3, lanes(g)] += dlng[g]
            dvec_ref[3:4, lanes(g)] += dlnb[g]
        dvec_ref[0:1, :] += dgpost
        dvec_ref[1:2, :] += dgate
        dgon_ref[0:1, :] += dgon
        ya_ref[...] = y_a.astype(BF16)
        yb_ref[...] = y_b.astype(BF16)
        mg_ref[...] = merged.astype(BF16)
        dma_ref[...] = dma.astype(BF16)
        dmb_ref[...] = dmb.astype(BF16)
        dout_ref[...] = dout.astype(BF16)

    row = lambda w: pl.BlockSpec((GM, w), lambda i: (i, 0))
    heads = pl.BlockSpec((H, GM, DH), lambda i: (0, i, 0))
    full = lambda shape: pl.BlockSpec(shape, lambda i: tuple(0 for _ in shape))
    sds = jax.ShapeDtypeStruct
    return pl.pallas_call(
        body, name=name, grid=(n,),
        in_specs=[row(6 * D), heads, heads, row(D), row(D), full((1, 3 * D)), full((1, 3 * D)), full((1, DH)),
                  full((1, D)), full((1, D)), full((H, GM, GM)), full((H, GM, GM)),
                  full((D, D)), full((D, D)), full((D, D)), full((1, D))],
        out_specs=[full((8, DH)), row(6 * D), heads, row(D)] + [row(D)] * 6
        + [full((8, D)), full((8, DH)), full((H, GM, GM)), full((H, GM, GM))],
        out_shape=[sds((8, DH), F32), sds((rows, 6 * D), BF16), sds((H, rows, DH), F32), sds((rows, D), F32)]
        + [sds((rows, D), BF16)] * 6 + [sds((8, D), F32), sds((8, DH), F32), sds((H, GM, GM), F32), sds((H, GM, GM), F32)],
        compiler_params=_params(("arbitrary",)),
    )(p_rest, o_f, o_b, x, tgt, mod, bmod, gon, lng, lnb, wsp, bspb, wpa, wpb, wout, gpost)


def _silu_rows(c, *, name):
    def body(c_ref, o_ref):
        o_ref[...] = _silu(c_ref[...])

    return pl.pallas_call(body, name=name, out_shape=jax.ShapeDtypeStruct(c.shape, F32))(c)


def _dsilu_mul(c, d, *, name):
    def body(c_ref, d_ref, o_ref):
        _, vjp = jax.vjp(_silu, c_ref[...])
        (o_ref[...],) = vjp(d_ref[...])

    return pl.pallas_call(body, name=name, out_shape=jax.ShapeDtypeStruct(c.shape, F32))(c, d)


def _adamw(w, g, m, v, *, name):
    rows, cols = w.shape[-2:]
    t = _tile(rows if g.ndim == 2 else g.shape[1])
    c1 = 1.0 / (1.0 - ADAM_B1 ** ADAM_STEP)
    c2 = 1.0 / (1.0 - ADAM_B2 ** ADAM_STEP)

    def body(w_ref, g_ref, m_ref, v_ref, go_ref, d_ref, mo_ref, vo_ref):
        blk = lambda r: r[...].reshape(t, cols)
        gv = blk(g_ref)
        mn = ADAM_B1 * blk(m_ref) + (1.0 - ADAM_B1) * gv
        vn = ADAM_B2 * blk(v_ref) + (1.0 - ADAM_B2) * (gv * gv)
        delta = -ADAM_LR * ((mn * c1) / (jnp.sqrt(vn * c2) + ADAM_EPS) + ADAM_WD * blk(w_ref))
        for ref, val in ((go_ref, gv), (d_ref, delta), (mo_ref, mn), (vo_ref, vn)):
            ref[...] = val.reshape(ref.shape)

    tile = (pl.BlockSpec((1, t, cols), lambda i: (0, i, 0)) if w.ndim == 3 else pl.BlockSpec((t, cols), lambda i: (i, 0)))
    if g.ndim == 3:
        per = g.shape[1] // t
        g_tile = pl.BlockSpec((1, t, cols), lambda i: (i // per, i % per, 0))
    else:
        g_tile = pl.BlockSpec((t, cols), lambda i: (i, 0))
    return pl.pallas_call(
        body, name=name, grid=(rows // t,),
        in_specs=[tile, g_tile, tile, tile], out_specs=[tile] * 4,
        out_shape=[jax.ShapeDtypeStruct(w.shape, F32)] * 4,
        compiler_params=_params(("parallel",)),
    )(w, g, m, v)


def _add_sibling(full, got, where, out_dtype, *, name):
    s, rows, cols = full.shape
    hr = rows // 2
    t = _tile(hr)
    nt = hr // t

    def body(w_ref, a_ref, b_ref, o_ref):
        o_ref[...] = (a_ref[...].astype(F32) + b_ref[...].astype(F32)).astype(out_dtype)

    tile = pl.BlockSpec((1, t, cols), lambda j, i, w: (j, i, 0))
    return pl.pallas_call(
        body, name=name,
        grid_spec=pltpu.PrefetchScalarGridSpec(
            num_scalar_prefetch=1, grid=(s, nt),
            in_specs=[pl.BlockSpec((1, t, cols), lambda j, i, w: (j, w[0] * nt + i, 0)), tile], out_specs=tile),
        out_shape=jax.ShapeDtypeStruct((s, hr, cols), out_dtype), compiler_params=_params(("parallel", "parallel")),
    )(where, full, got)


def _sum_chips(own, slots, where, n_out, which, *, name):
    _, hr, cols = own.shape
    t = _tile(hr)

    def body(w_ref, a_ref, s_ref, o_ref):
        f = lambda v: v.astype(F32)
        o_ref[0] = ((f(a_ref[0]) + f(s_ref[0])) + f(s_ref[1])) + f(s_ref[2])

    return pl.pallas_call(
        body, name=name,
        grid_spec=pltpu.PrefetchScalarGridSpec(
            num_scalar_prefetch=1, grid=(hr // t,),
            in_specs=[pl.BlockSpec((1, t, cols), lambda i, w: (w[1], i, 0)),
                      pl.BlockSpec((N_CHIPS - 1, t, cols), lambda i, w: (0, i, 0))],
            out_specs=pl.BlockSpec((1, t, cols), lambda i, w: (w[which], i, 0))),
        out_shape=jax.ShapeDtypeStruct((n_out, hr, cols), F32), compiler_params=_params(("parallel",)),
    )(where, own, slots)


def _local_step(x, c, ctx, c_ctx, tgt, w_mod, b_mod, g_pre, g_post, w_qkv, w_ab, w_rest, w_conv, a_log, dt_bias,
                g_onorm, gm_ln_g, gm_ln_b, w_sp, b_sp, late_weights, on_large_gradients=None, run_after=None):
    rows, rows_c = x.shape[0], ctx.shape[0]
    cc = jnp.zeros((16, D), F32).at[0].set(c[0]).at[1].set(c_ctx)
    scc = _silu_rows(cc, name="silu_cond")
    mod = _mm(scc, w_mod, b_sections=True, after=run_after, name="mod_fwd")
    mod_x, mod_c = mod[0:1], mod[1:2]
    avec = jnp.zeros((1, DH), F32).at[0, :2 * H].set(a_log.reshape(-1))
    dvec = jnp.zeros((1, DH), F32).at[0, :2 * H].set(dt_bias.reshape(-1))
    bspb = jnp.broadcast_to(b_sp[:, :, None], (H, GM, GM))
    w_kv, wconv_kv = w_qkv[:, D:], w_conv[:, D:]

    h_c = _norm_fwd(ctx, g_pre, mod_c, b_mod, name="norm_fwd_ctx")
    pc_kv = _mm(h_c, w_kv, name="inproj_ctx_kv")
    pc_ab = _mm(h_c, w_ab, name="inproj_ctx_ab")
    kv_c = _prep_fwd(pc_kv, wconv_kv, H, name="prep_fwd_ctx")
    gb_c = _gates_fwd(pc_ab, avec, dvec, name="gates_fwd_ctx")
    s_zero = jnp.zeros((H, DH, DH), F32)
    _, _, ssf_c, ssb_c, tsf_c, tsb_c, s_f, s_b = _gdn_fwd(kv_c, gb_c, s_zero, s_zero, False, name="gdn_fwd_ctx")

    h_x = _norm_fwd(x, g_pre, mod_x, b_mod, name="norm_fwd_x")
    p_qkv = _mm(h_x, w_qkv, name="inproj_qkv")
    p_ab = _mm(h_x, w_ab, name="inproj_ab")
    p_rest = _mm(h_x, w_rest, out_dtype=BF16, name="inproj_rest")
    qkv = _prep_fwd(p_qkv, w_conv, 2 * H, name="prep_fwd_x")
    gb_x = _gates_fwd(p_ab, avec, dvec, name="gates_fwd_x")
    o_f, o_b, ssf, ssb, tsf, tsb, _, _ = _gdn_fwd(qkv, gb_x, s_f, s_b, True, name="gdn_fwd_x")

    w_pa, w_pb, w_out = late_weights(o_f)
    (loss_acc, dp_rest, do, dx_res, ya, yb, mg, dma, dmb, dout, dvec_post, dgon, dwsp, dbspb) = _post(
        p_rest, o_f, o_b, x, tgt, mod_x, b_mod, g_onorm, gm_ln_g, gm_ln_b, w_sp, bspb, w_pa, w_pb, w_out, g_post,
        name="post")
    g = {}
    g["w_pa"] = _mm(ya, dma, ta=True, name="dw_pa")
    g["w_pb"] = _mm(yb, dmb, ta=True, name="dw_pb")
    g["w_out"] = _mm(mg, dout, ta=True, name="dw_out")

    zeros_s = jnp.zeros((H, DH, DH), F32)
    dq_f, dq_b, dg_f, dg_b, ds0_f, ds0_b = _gdn_bwd(qkv, gb_x, ssf, ssb, tsf, tsb, do, zeros_s, zeros_s, True,
                                                    name="gdn_bwd_x")
    dp_qkv, dwc_x = _prep_bwd(p_qkv, w_conv, dq_f, dq_b, 2 * H, name="prep_bwd_x")
    dp_ab, dav_x, ddv_x = _gates_bwd(p_ab, avec, dvec, dg_f, dg_b, name="gates_bwd_x")
    dkv_f, dkv_b, dgc_f, dgc_b, _, _ = _gdn_bwd(kv_c, gb_c, ssf_c, ssb_c, tsf_c, tsb_c, jnp.zeros((H, rows_c, DH), F32),
                                                 ds0_f, ds0_b, False, name="gdn_bwd_ctx")
    dpc_kv, dwc_c = _prep_bwd(pc_kv, wconv_kv, dkv_f, dkv_b, H, name="prep_bwd_ctx")
    dpc_ab, dav_c, ddv_c = _gates_bwd(pc_ab, avec, dvec, dgc_f, dgc_b, name="gates_bwd_ctx")

    dw_kv_c = _mm(h_c, dpc_kv, ta=True, name="dw_kv_ctx")
    dw_qkv = _mm(h_x, dp_qkv, ta=True, tn=D, add=dw_kv_c, add_from=1, out_dtype=BF16, name="dw_qkv")
    dw_ab_c = _mm(h_c, dpc_ab, ta=True, name="dw_ab_ctx")
    dw_ab = _mm(h_x, dp_ab, ta=True, add=dw_ab_c, out_dtype=BF16, name="dw_ab")
    dw_rest = _mm(h_x, dp_rest, ta=True, out_dtype=BF16, name="dw_rest")
    g["w_in"] = (dw_qkv, dw_ab[:, :4 * H], dw_rest)
    token = on_large_gradients(g) if on_large_gradients is not None else None
    dh = _mm(dp_ab, w_ab, tb=True, after=token, name="dh_ab")
    dh = _mm(dp_qkv, w_qkv, tb=True, tk=3 * D // 2, add=dh, after=token, name="dh_qkv")
    dh = _mm(dp_rest, w_rest, tb=True, tk=3 * D // 2, add=dh, after=token, name="dh_rest")
    grad_x, dgpre_x, dm_x = _norm_bwd(x, g_pre, mod_x, b_mod, [dh], dx_res, name="norm_bwd_x")
    dh_c = _mm(dpc_ab, w_ab, tb=True, after=token, name="dhc_ab")
    dh_c = _mm(dpc_kv, w_kv, tb=True, add=dh_c, after=token, name="dhc_kv")
    _, dgpre_c, dm_c = _norm_bwd(ctx, g_pre, mod_c, b_mod, [dh_c], None, name="norm_bwd_ctx")

    dm_x = dm_x.at[:, 2 * D:].add(dvec_post[1:2])
    dmod = jnp.zeros((16, 3 * D), F32).at[0].set(dm_x[0]).at[1].set(dm_c[0])
    g["mod_factors"] = (scc[0], dm_x[0])
    g["dm_ctx"] = dm_c[0]
    g["silu_c_ctx"] = scc[1:2]
    dcc = _mm(dmod, w_mod, tb=True, b_sections=True, name="dcc")
    g["c_ctx"] = _dsilu_mul(cc[:8], dcc[:8], name="dc_ctx")[1]
    g["b_mod"] = dm_x + dm_c
    g["g_pre"] = dgpre_x + dgpre_c
    g["g_post"] = dvec_post[0:1]
    g["gm_ln_g"], g["gm_ln_b"] = dvec_post[2:3], dvec_post[3:4]
    g["g_onorm"] = dgon[0:1]
    g["w_sp"] = dwsp
    g["b_sp"] = jnp.sum(dbspb, axis=-1)
    g["w_conv"] = dwc_x.at[:, D:].add(dwc_c)
    g["a_log"] = (dav_x + dav_c)[0, :2 * H].reshape(2, H)
    g["dt_bias"] = (ddv_x + ddv_c)[0, :2 * H].reshape(2, H)
    return loss_acc[0, 0], grad_x, g


ANY = pl.BlockSpec(memory_space=pl.ANY)


def _place():
    x, y, c = lax.axis_index("x"), lax.axis_index("y"), lax.axis_index("c")
    chips = [(1 - x, y), (x, 1 - y), (1 - x, 1 - y)]
    return x, y, c, (x, y, 1 - c), chips


def _gather_shards(big, small, *, name):
    nb = len(big)

    def body(*refs):
        ins, sm_in = refs[:nb], refs[nb]
        outs, sm_out = refs[nb + 1:2 * nb + 1], refs[2 * nb + 1]
        send, recv = refs[2 * nb + 2:]
        x, y, c, sibling, chips = _place()
        mine = 2 * x + y

        def half(a, shard, hc):
            hr = big[a].shape[0] // 2
            return outs[a].at[shard, pl.ds(hc * hr, hr), :]

        def remote(k, src, dst, to):
            return pltpu.make_async_remote_copy(src_ref=src, dst_ref=dst, send_sem=send.at[k], recv_sem=recv.at[k],
                                                device_id=to, device_id_type=MESH)

        sends = []
        for a in range(nb):
            hr = big[a].shape[0] // 2
            for j, chip in enumerate(chips):
                sends.append(remote(a * 3 + j, ins[a].at[pl.ds(c * hr, hr), :], half(a, mine, c), (*chip, c)))
        for j, chip in enumerate(chips):
            sends.append(remote(nb * 3 + j, sm_in, sm_out.at[mine], (*chip, c)))
        for cp in sends:
            cp.start()
        base = nb * 3 + 3
        passed = []
        for a in range(nb):
            for j, (px, py) in enumerate(chips):
                theirs = 2 * px + py
                remote(a * 3 + j, half(a, theirs, c), half(a, theirs, c), sibling).wait_recv()
                fw = remote(base + a * 3 + j, half(a, theirs, c), half(a, theirs, c), sibling)
                fw.start()
                passed.append(fw)
        for a in range(nb):
            for j, (px, py) in enumerate(chips):
                theirs = 2 * px + py
                remote(base + a * 3 + j, half(a, theirs, 1 - c), half(a, theirs, 1 - c), sibling).wait_recv()
        for j, (px, py) in enumerate(chips):
            remote(nb * 3 + j, sm_in, sm_out.at[2 * px + py], sibling).wait_recv()
        for cp in sends + passed:
            cp.wait_send()

    n_remote = 2 * nb * 3 + 3
    outs = pl.pallas_call(
        body, name=name, in_specs=[ANY] * (nb + 1), out_specs=[ANY] * (nb + 1),
        out_shape=[jax.ShapeDtypeStruct((N_CHIPS,) + a.shape, a.dtype) for a in big + [small]],
        scratch_shapes=[pltpu.SemaphoreType.DMA((n_remote,)), pltpu.SemaphoreType.DMA((n_remote,))],
    )(*big, small)
    return outs[:nb], outs[nb]


def _gather_late_start(arrs, after, *, name):
    na = len(arrs)

    def body(*refs):
        ins, land = refs[:na], refs[na:2 * na]
        send, recv = refs[2 * na + 1], refs[2 * na + 2]
        token = refs[-1]
        x, y, c, _, chips = _place()
        for a in range(na):
            hr = arrs[a].shape[0] // 2
            for j, (px, py) in enumerate(chips):
                for other in range(2):
                    k = (a * 3 + j) * 2 + other
                    _remote(ins[a].at[pl.ds(c * hr, hr), :], land[a].at[2 * x + y, pl.ds(c * hr, hr), :],
                            send.at[k], recv.at[k], (px, py, c ^ other)).start()
        token[...] = jnp.zeros_like(token)

    lands = [((N_CHIPS,) + a.shape, a.dtype) for a in arrs]
    held = lambda a: pltpu.with_memory_space_constraint(a, pltpu.HBM)
    outs = pl.pallas_call(
        body, name=name, in_specs=[HBM] * (2 * na) + [ANY],
        out_specs=[SEM, SEM] + [HBM] * (2 * na) + [pl.BlockSpec(memory_space=pltpu.VMEM)],
        out_shape=[pltpu.SemaphoreType.DMA((6 * na,)), pltpu.SemaphoreType.DMA((6 * na,))]
        + [pltpu.HBM(a.shape, a.dtype) for a in arrs] + [pltpu.HBM(s, d) for s, d in lands]
        + [jax.ShapeDtypeStruct((8, 128), F32)],
        input_output_aliases={i: 2 + i for i in range(2 * na)},
        compiler_params=pltpu.CompilerParams(has_side_effects=pltpu.SideEffectType.DATAFLOW_SIDE_EFFECTING),
    )(*[held(a) for a in arrs], *[held(lax.empty(s, d)) for s, d in lands], after)
    return outs[0], outs[1], outs[2:2 + na], outs[2 + na:2 + 2 * na], outs[-1]


def _gather_late_wait(send, recv, thru, land, after, *, name):
    na = len(thru)

    def body(*refs):
        ins, slots = refs[:na], refs[na:2 * na]
        send_sem, recv_sem = refs[2 * na], refs[2 * na + 1]
        x, y, c, _, chips = _place()
        for a in range(na):
            hr = thru[a].shape[0] // 2
            for j, (px, py) in enumerate(chips):
                for other in range(2):
                    k = (a * 3 + j) * 2 + other
                    half = c ^ other
                    cp = _remote(ins[a].at[pl.ds(c * hr, hr), :], slots[a].at[2 * px + py, pl.ds(half * hr, hr), :],
                                 send_sem.at[k], recv_sem.at[k], (px, py, half))
                    cp.wait_send()
                    cp.wait_recv()

    outs = pl.pallas_call(
        body, name=name, in_specs=[HBM] * (2 * na) + [SEM, SEM, ANY], out_specs=[HBM] * (2 * na),
        out_shape=[pltpu.HBM(a.shape, a.dtype) for a in list(thru) + list(land)],
        input_output_aliases={i: i for i in range(2 * na)},
        compiler_params=pltpu.CompilerParams(has_side_effects=pltpu.SideEffectType.DATAFLOW_SIDE_EFFECTING),
    )(*thru, *land, send, recv, after)
    return outs[na:]


def _row_chunks(rows, row_bytes, align=8):
    n = max(1, min(rows // align, -(-rows * row_bytes // DMA_CHUNK_BYTES)))
    per = -(-(-(-rows // n)) // align) * align
    return [(r, min(per, rows - r)) for r in range(0, rows, per)]


def _remote(src, dst, send, recv, to):
    return pltpu.make_async_remote_copy(src_ref=src, dst_ref=dst, send_sem=send, recv_sem=recv, device_id=to,
                                        device_id_type=MESH)


def _exchange_halves(arrs, *, name):
    na = len(arrs)

    def body(*refs):
        ins, got = refs[:na], refs[na:2 * na]
        send, recv = refs[2 * na:]
        x, y, c, sibling, _ = _place()
        for a in range(na):
            ns, rows, cols = arrs[a].shape
            hr = rows // 2
            for s in range(ns):
                for r0, nr in _row_chunks(hr, cols * arrs[a].dtype.itemsize, 16):
                    _remote(ins[a].at[s, pl.ds((1 - c) * hr + r0, nr), :], got[a].at[s, pl.ds(r0, nr), :],
                            send.at[a], recv.at[a], sibling).start()
        for a in range(na):
            hr = arrs[a].shape[1] // 2
            _remote(ins[a].at[:, pl.ds((1 - c) * hr, hr), :], got[a], send.at[a], recv.at[a], sibling).wait()

    return pl.pallas_call(
        body, name=name, in_specs=[ANY] * na, out_specs=[ANY] * na,
        out_shape=[jax.ShapeDtypeStruct((N_CHIPS, a.shape[1] // 2, a.shape[2]), a.dtype) for a in arrs],
        scratch_shapes=[pltpu.SemaphoreType.DMA((na,)), pltpu.SemaphoreType.DMA((na,))],
    )(*arrs)


def _scatter_sections(arrs, *, name):
    na = len(arrs)

    def body(*refs):
        ins, outs = refs[:na], refs[na:2 * na]
        send, recv = refs[2 * na:]
        x, y, c, _, chips = _place()
        for a in range(na):
            _, hr, cols = arrs[a].shape
            for j, (px, py) in enumerate(chips):
                for r0, nr in _row_chunks(hr, cols * arrs[a].dtype.itemsize, 16):
                    _remote(ins[a].at[2 * px + py, pl.ds(r0, nr), :], outs[a].at[j, pl.ds(r0, nr), :],
                            send.at[a * 3 + j], recv.at[a * 3 + j], (px, py, c)).start()
        for a in range(na):
            for j, (px, py) in enumerate(chips):
                _remote(ins[a].at[2 * px + py], outs[a].at[j], send.at[a * 3 + j], recv.at[a * 3 + j], (px, py, c)).wait()

    return pl.pallas_call(
        body, name=name, in_specs=[ANY] * na, out_specs=[ANY] * na,
        out_shape=[jax.ShapeDtypeStruct((N_CHIPS - 1,) + a.shape[1:], a.dtype) for a in arrs],
        scratch_shapes=[pltpu.SemaphoreType.DMA((3 * na,)), pltpu.SemaphoreType.DMA((3 * na,))],
    )(*arrs)


HBM = pl.BlockSpec(memory_space=pltpu.HBM)
SEM = pl.BlockSpec(memory_space=pltpu.SEMAPHORE)


def _scatter_start(arrs, *, name):
    na = len(arrs)

    def body(*refs):
        ins, land = refs[:na], refs[na:2 * na]
        send, recv = refs[2 * na], refs[2 * na + 1]
        token = refs[-1]
        x, y, c, _, chips = _place()
        for a in range(na):
            _, hr, cols = arrs[a].shape
            for j, (px, py) in enumerate(chips):
                for r0, nr in _row_chunks(hr, cols * arrs[a].dtype.itemsize, 16):
                    _remote(ins[a].at[2 * px + py, pl.ds(r0, nr), :], land[a].at[j, pl.ds(r0, nr), :],
                            send.at[a * 3 + j], recv.at[a * 3 + j], (px, py, c)).start()
        token[...] = jnp.zeros_like(token)

    slots = [((N_CHIPS - 1,) + a.shape[1:], a.dtype) for a in arrs]
    held = lambda a: pltpu.with_memory_space_constraint(a, pltpu.HBM)
    outs = pl.pallas_call(
        body, name=name, in_specs=[HBM] * (2 * na),
        out_specs=[SEM, SEM] + [HBM] * (2 * na) + [pl.BlockSpec(memory_space=pltpu.VMEM)],
        out_shape=[pltpu.SemaphoreType.DMA((3 * na,)), pltpu.SemaphoreType.DMA((3 * na,))]
        + [pltpu.HBM(a.shape, a.dtype) for a in arrs] + [pltpu.HBM(s, d) for s, d in slots]
        + [jax.ShapeDtypeStruct((8, 128), F32)],
        input_output_aliases={i: 2 + i for i in range(2 * na)},
        compiler_params=pltpu.CompilerParams(has_side_effects=pltpu.SideEffectType.DATAFLOW_SIDE_EFFECTING),
    )(*[held(a) for a in arrs], *[held(lax.empty(s, d)) for s, d in slots])
    return outs[0], outs[1], outs[2:2 + na], outs[2 + na:2 + 2 * na], outs[-1]


def _scatter_wait(send, recv, thru, land, after, *, name):
    na = len(thru)

    def body(*refs):
        ins, slots = refs[:na], refs[na:2 * na]
        send_sem, recv_sem = refs[2 * na], refs[2 * na + 1]
        x, y, c, _, chips = _place()
        for a in range(na):
            for j, (px, py) in enumerate(chips):
                cp = _remote(ins[a].at[2 * px + py], slots[a].at[j], send_sem.at[a * 3 + j], recv_sem.at[a * 3 + j],
                             (px, py, c))
                cp.wait_send()
                cp.wait_recv()

    outs = pl.pallas_call(
        body, name=name, in_specs=[HBM] * (2 * na) + [SEM, SEM, ANY], out_specs=[HBM] * (2 * na),
        out_shape=[pltpu.HBM(a.shape, a.dtype) for a in list(thru) + list(land)],
        input_output_aliases={i: i for i in range(2 * na)},
        compiler_params=pltpu.CompilerParams(has_side_effects=pltpu.SideEffectType.DATAFLOW_SIDE_EFFECTING),
    )(*thru, *land, send, recv, after)
    return outs[:na], outs[na:]


def _finish_reduce(big, small, *, name):
    nb = len(big)

    def body(*refs):
        outs, sm = refs[nb + 1:2 * nb + 1], refs[2 * nb + 1]
        send, recv = refs[2 * nb + 2:]
        x, y, c, sibling, chips = _place()
        blk = lambda px, py, pc: sm.at[4 * px + 2 * py + pc]
        for a in range(nb):
            _, hr, cols = big[a].shape
            for r0, nr in _row_chunks(hr, cols * 4):
                _remote(outs[a].at[c, pl.ds(r0, nr), :], outs[a].at[c, pl.ds(r0, nr), :], send.at[a], recv.at[a],
                        sibling).start()
        first = [_remote(blk(x, y, c), blk(x, y, c), send.at[nb], recv.at[nb], sibling)]
        first += [_remote(blk(x, y, c), blk(x, y, c), send.at[nb + 1 + j], recv.at[nb + 1 + j], (*chip, c))
                  for j, chip in enumerate(chips)]
        for cp in first:
            cp.start()
        passed = []
        for j, (px, py) in enumerate(chips):
            _remote(blk(px, py, c), blk(px, py, c), send.at[nb + 1 + j], recv.at[nb + 1 + j], sibling).wait_recv()
            fw = _remote(blk(px, py, c), blk(px, py, c), send.at[nb + 4 + j], recv.at[nb + 4 + j], sibling)
            fw.start()
            passed.append(fw)
        for a in range(nb):
            _remote(outs[a].at[c], outs[a].at[1 - c], send.at[a], recv.at[a], sibling).wait()
        _remote(blk(x, y, c), blk(x, y, 1 - c), send.at[nb], recv.at[nb], sibling).wait_recv()
        for j, (px, py) in enumerate(chips):
            _remote(blk(px, py, c), blk(px, py, 1 - c), send.at[nb + 4 + j], recv.at[nb + 4 + j], sibling).wait_recv()
        for cp in first + passed:
            cp.wait_send()

    n_remote = nb + 7
    arrs = list(big) + [small]
    outs = pl.pallas_call(
        body, name=name, in_specs=[ANY] * (nb + 1), out_specs=[ANY] * (nb + 1),
        out_shape=[jax.ShapeDtypeStruct(a.shape, F32) for a in arrs],
        input_output_aliases={i: i for i in range(nb + 1)},
        scratch_shapes=[pltpu.SemaphoreType.DMA((n_remote,)), pltpu.SemaphoreType.DMA((n_remote,))],
    )(*arrs)
    return outs[:nb], outs[nb]


def _reduce_start(sectioned, where):
    got = _exchange_halves(sectioned, name="rs_exchange_halves")
    chip_sum = [_add_sibling(a, g, where, BF16, name=f"rs_add_sibling_{i}")
                for i, (a, g) in enumerate(zip(sectioned, got))]
    send, recv, thru, land, token = _scatter_start(chip_sum, name="rs_scatter_start")
    return (send, recv, thru, land), token


def _reduce_finish(handle, small, where, after):
    send, recv, thru, land = handle
    got = _exchange_halves([small], name="rs_exchange_small")
    small_sum = _add_sibling(small, got[0], where, F32, name="rs_add_sibling_small")
    small_slots = _scatter_sections([small_sum], name="rs_scatter_small")[0]
    own, slots = _scatter_wait(send, recv, thru, land, after, name="rs_scatter_wait")
    red = [_sum_chips(p, s, where, 2, 0, name=f"rs_sum_chips_{i}") for i, (p, s) in enumerate(zip(own, slots))]
    red_small = _sum_chips(small_sum, small_slots, where, 2 * N_CHIPS, 2, name="rs_sum_chips_small")
    big, sm = _finish_reduce(red, red_small, name="rs_finish")
    return big, sm.reshape(-1, sm.shape[-1])


def kernel(x, c, ctx, c_ctx, w_mod, b_mod, g_pre, g_post, w_in, w_conv, a_log, dt_bias, g_onorm, gm_ln_g, gm_ln_b, w_sp, b_sp, w_pa, w_pb, w_out, loss_target, m_c_ctx, m_w_mod, m_b_mod, m_g_pre, m_g_post, m_w_in, m_w_conv, m_a_log, m_dt_bias, m_g_onorm, m_gm_ln_g, m_gm_ln_b, m_w_sp, m_b_sp, m_w_pa, m_w_pb, m_w_out, v_c_ctx, v_w_mod, v_b_mod, v_g_pre, v_g_post, v_w_in, v_w_conv, v_a_log, v_dt_bias, v_g_onorm, v_gm_ln_g, v_gm_ln_b, v_w_sp, v_b_sp, v_w_pa, v_w_pb, v_w_out):
    names = ["c_ctx", "w_mod", "b_mod", "g_pre", "g_post", "w_in", "w_conv", "a_log", "dt_bias", "g_onorm", "gm_ln_g",
             "gm_ln_b", "w_sp", "b_sp", "w_pa", "w_pb", "w_out"]
    w = dict(zip(names, (c_ctx, w_mod, b_mod, g_pre, g_post, w_in, w_conv, a_log, dt_bias, g_onorm, gm_ln_g, gm_ln_b,
                         w_sp, b_sp, w_pa, w_pb, w_out)))
    m = dict(zip(names, (m_c_ctx, m_w_mod, m_b_mod, m_g_pre, m_g_post, m_w_in, m_w_conv, m_a_log, m_dt_bias, m_g_onorm,
                         m_gm_ln_g, m_gm_ln_b, m_w_sp, m_b_sp, m_w_pa, m_w_pb, m_w_out)))
    v = dict(zip(names, (v_c_ctx, v_w_mod, v_b_mod, v_g_pre, v_g_post, v_w_in, v_w_conv, v_a_log, v_dt_bias, v_g_onorm,
                         v_gm_ln_g, v_gm_ln_b, v_w_sp, v_b_sp, v_w_pa, v_w_pb, v_w_out)))
    xy = 2 * lax.axis_index("x") + lax.axis_index("y")
    where = jnp.stack([lax.axis_index("c"), xy, 2 * xy + lax.axis_index("c")]).astype(jnp.int32)

    shards = [a[0].astype(BF16) for a in (w_mod, w_in)]
    gathered, wconv_all = _gather_shards(shards, w_conv[0], name="gather_weights")
    own = lambda full, shard: lax.dynamic_update_slice(full, shard[None], (xy, 0, 0))
    wm_all, win_all = [own(f, s) for f, s in zip(gathered, shards)]
    wconv_all = own(wconv_all, w_conv[0])
    late_shards = [a[0].astype(BF16) for a in (w_pa, w_pb, w_out)]
    late = _gather_late_start(late_shards, wconv_all, name="gather_late_start")

    def late_weights(after):
        lands = _gather_late_wait(*late[:4], after, name="gather_late_wait")
        return [own(f, s).reshape(D, D) for f, s in zip(lands, late_shards)]
    w_conv_f = jnp.concatenate([wconv_all[s] for s in range(N_CHIPS)], axis=1)
    shard_cols = IN_COLS // N_CHIPS
    cut = OFF_A - shard_cols
    assert 0 < cut and cut + 4 * H < shard_cols
    w_qkv = jnp.concatenate([win_all[0], win_all[1][:, :cut]], axis=1)
    w_ab = jnp.pad(win_all[1][:, cut:cut + 4 * H], ((0, 0), (0, DH - 4 * H)))
    w_rest = jnp.concatenate([win_all[1][:, cut + 4 * H:], win_all[2], win_all[3]], axis=1)

    blk = D // N_CHIPS
    rest_cut = shard_cols - cut - 4 * H
    big_names = ("w_in", "w_pa", "w_pb", "w_out")
    in_flight = []

    def start_reduce(grads):
        dw_qkv, dw_ab, dw_rest = grads["w_in"]
        g_win = jnp.stack([dw_qkv[:, :shard_cols],
                           jnp.concatenate([dw_qkv[:, shard_cols:], dw_ab, dw_rest[:, :rest_cut]], axis=1),
                           dw_rest[:, rest_cut:rest_cut + shard_cols], dw_rest[:, rest_cut + shard_cols:]]).astype(BF16)
        sectioned = [g_win] + [grads[k].reshape(N_CHIPS, blk, D) for k in ("w_pa", "w_pb", "w_out")]
        handle, token = _reduce_start(sectioned, where)
        in_flight.append(handle)
        return token

    loss_local, grad_x, g = _local_step(
        x[0], c, ctx[0], c_ctx, loss_target[0], wm_all, b_mod, g_pre, g_post, w_qkv, w_ab, w_rest,
        w_conv_f, a_log[0], dt_bias[0], g_onorm, gm_ln_g, gm_ln_b, w_sp[0], b_sp[0],
        late_weights, on_large_gradients=start_reduce, run_after=late[4])
    g["loss"] = loss_local
    g["_pad"] = jnp.zeros((SMALL_LAYOUT["_pad"][1],), F32)
    g["mod_factors"] = lax.dynamic_update_slice(jnp.zeros((2 * N_CHIPS, 4 * D), F32),
                                                jnp.concatenate(g["mod_factors"])[None], (where[2], 0))
    tail = jnp.zeros((N_CHIPS * SMALL_ROWS * 128 - sum(s for _, s in SMALL_LAYOUT.values()),), F32)
    flat = jnp.concatenate([g[k].reshape(-1) for k in SMALL_LAYOUT] + [tail])
    reduced, gr_small = _reduce_finish(in_flight[0], flat.reshape(N_CHIPS, SMALL_ROWS, 128), where, g["b_mod"])
    gr_tiny = gr_small[TINY_ROW0:]

    def entry(arr, k):
        off, size = SMALL_LAYOUT[k]
        row, col = off // 128 - TINY_ROW0, off % 128
        return arr[row:row + size // 128].reshape(-1) if size >= 128 else arr[row, col:col + size]

    factors = entry(gr_tiny, "mod_factors").reshape(2 * N_CHIPS, 4 * D)
    pad_rows = lambda a: jnp.pad(a, ((0, 16 - a.shape[0]), (0, 0)))
    lhs = pad_rows(jnp.concatenate([factors[:, :D], g["silu_c_ctx"]], axis=0))
    rhs = pad_rows(jnp.concatenate([factors[:, D:], entry(gr_tiny, "dm_ctx")[None]], axis=0))
    mod_cols = 3 * D // N_CHIPS
    gr_wm = _mm(lhs, lax.dynamic_slice(rhs, (0, xy * mod_cols), (16, mod_cols)), ta=True, name="dw_mod")

    res = {k: _adamw(w[k], gr, m[k], v[k], name=f"adamw_{k}") for k, gr in zip(big_names[1:], reduced[1:])}
    flip = lambda a: jnp.swapaxes(a, 1, 2)
    res["w_in"] = [flip(r) for r in _adamw(flip(w_in), flip(reduced[0].reshape(w_in.shape)), flip(m_w_in), flip(v_w_in),
                                           name="adamw_w_in")]
    res["w_mod"] = _adamw(w_mod[0], gr_wm, m_w_mod[0], v_w_mod[0], name="adamw_w_mod")
    res["w_sp"] = _adamw(w_sp.reshape(-1, 128), gr_small, m_w_sp.reshape(-1, 128), v_w_sp.reshape(-1, 128),
                         name="adamw_w_sp")
    tiny = [k for k, (off, _) in SMALL_LAYOUT.items() if TINY_ROW0 <= off // 128 < TINY_ROW0 + TINY_ROWS]
    tiny_names = [k for k in tiny if k not in ("loss", "_pad")]

    def pack(src):
        parts = [src[k].reshape(-1) if k in tiny_names else jnp.zeros((SMALL_LAYOUT[k][1],), F32) for k in tiny]
        return jnp.concatenate(parts).reshape(TINY_ROWS, 128)

    tiny_res = _adamw(pack(w), gr_tiny[:TINY_ROWS], pack(m), pack(v), name="adamw_tiny")
    for k in tiny_names:
        res[k] = [entry(r, k) for r in tiny_res]
    g_conv = lax.dynamic_slice(entry(gr_tiny, "w_conv").reshape(3, 3 * D), (0, xy * (3 * D // N_CHIPS)),
                               (3, 3 * D // N_CHIPS))
    conv_res = _adamw(jnp.pad(w_conv[0], ((0, 5), (0, 0))), jnp.pad(g_conv, ((0, 5), (0, 0))),
                      jnp.pad(m_w_conv[0], ((0, 5), (0, 0))), jnp.pad(v_w_conv[0], ((0, 5), (0, 0))), name="adamw_w_conv")
    res["w_conv"] = [r[:3] for r in conv_res]
    res = {k: [r.reshape(w[k].shape) for r in res[k]] for k in names}

    out = [entry(gr_tiny, "loss").reshape(()), grad_x[None]]
    for i in range(4):
        out += [res[k][i] for k in names]
    return tuple(out)
```

```python
import functools

import jax
import jax.numpy as jnp
from jax import lax
from jax.experimental import pallas as pl
from jax.experimental.pallas import tpu as pltpu

F32 = jnp.float32
BF16 = jnp.bfloat16
HI = lax.Precision.HIGHEST
MESH = pl.DeviceIdType.MESH

D = 1024
H = 8
DH = 128
CH = 64
LOG_CH = 6
PAIR = 2 * CH
GM = 128
assert 1 << LOG_CH == CH and PAIR == DH
PREC_POWERS = ((lax.Precision.HIGH, lax.Precision.HIGH),) * 3 + ((None, None),) * 2
assert len(PREC_POWERS) == LOG_CH - 1
EPS = 1e-6
N_CHIPS = 4
OFF_A = 3 * D
OFF_ZB = OFF_A + 4 * H
IN_COLS = OFF_ZB + 6 * D
VMEM_LIMIT_V7X = 56 * 1024 * 1024
DMA_CHUNK_BYTES = 2 * 1024 * 1024

ADAM_LR, ADAM_B1, ADAM_B2, ADAM_EPS, ADAM_WD, ADAM_STEP = 0.001, 0.9, 0.999, 1e-08, 0.01, 10

SMALL_LAYOUT = {}
_off = 0
for _n, _s in (("w_sp", H * GM * GM), ("c_ctx", D), ("b_mod", 3 * D), ("g_pre", D), ("g_post", D), ("gm_ln_g", D),
               ("gm_ln_b", D), ("b_sp", H * GM), ("g_onorm", DH), ("a_log", 2 * H), ("dt_bias", 2 * H), ("loss", 1),
               ("_pad", 128 - 4 * H - 1 + 6 * 128), ("w_conv", 3 * 3 * D), ("dm_ctx", 3 * D), ("mod_factors", 8 * 4 * D)):
    SMALL_LAYOUT[_n] = (_off, _s)
    _off += _s
SMALL_ROWS = 368
assert N_CHIPS * SMALL_ROWS * 128 >= _off and (SMALL_ROWS // 2) % 8 == 0 and _off % 128 == 0
TINY_ROW0 = SMALL_LAYOUT["c_ctx"][0] // 128
TINY_ROWS = SMALL_LAYOUT["w_conv"][0] // 128 - TINY_ROW0
assert TINY_ROWS % 8 == 0


def _params(sem=None):
    return pltpu.CompilerParams(dimension_semantics=sem, vmem_limit_bytes=VMEM_LIMIT_V7X)


def _tile(n, cands=(256, 128, 64, 32, 16, 8)):
    if n <= cands[0]:
        return n
    for cand in cands:
        if n % cand == 0 and cand >= 64:
            return cand
    return max(d for d in range(8, cands[0], 8) if n % d == 0)


def _silu(x):
    return x * jax.nn.sigmoid(x)


def _gelu(x):
    return 0.5 * x * (1.0 + jnp.tanh(0.7978845608028654 * (x + 0.044715 * (x * x * x))))


def _mm(a, b, *, ta=False, tb=False, out_dtype=F32, tm=1024, tn=1024, tk=1024, add=None, add_from=0,
        b_sections=False, out_sections=False, after=None, name):
    m, k = (a.shape[1], a.shape[0]) if ta else a.shape
    if b_sections:
        sect = b.shape[2]
        n = b.shape[1] if tb else b.shape[0] * sect
        tn, tk = (tn, sect) if tb else (sect, tk)
    else:
        n = b.shape[0] if tb else b.shape[1]
    tm, tn, tk = min(tm, m), min(tn, n), min(tk, k)
    assert m % tm == 0 and n % tn == 0 and k % tk == 0, (name, m, n, k, tm, tn, tk)
    nk = k // tk
    dims = (((0,) if ta else (1,), (1,) if tb else (0,)), ((), ()))
    has_add = add is not None
    assert not has_add or add.shape == (m, n - add_from * tn), (name, add.shape)

    def body(*refs):
        a_ref, b_ref = refs[:2]
        o_ref = refs[2 + int(has_add) + int(after is not None)]
        acc_ref = refs[-1]
        kk = pl.program_id(2)
        bv = b_ref[0] if b_sections else b_ref[...]
        part = lax.dot_general(a_ref[...].astype(BF16), bv.astype(BF16), dims, preferred_element_type=F32)

        def finish(res):
            if has_add:
                res = res + jnp.where(pl.program_id(1) >= add_from, refs[2][...], 0.0)
            if out_sections:
                o_ref[0] = res.astype(out_dtype)
            else:
                o_ref[...] = res.astype(out_dtype)

        if nk == 1:
            finish(part)
            return

        @pl.when(kk == 0)
        def _():
            acc_ref[...] = part

        @pl.when((kk > 0) & (kk < nk - 1))
        def _():
            acc_ref[...] += part

        @pl.when(kk == nk - 1)
        def _():
            finish(acc_ref[...] + part)

    a_spec = pl.BlockSpec((tk, tm), lambda i, j, q: (q, i)) if ta else pl.BlockSpec((tm, tk), lambda i, j, q: (i, q))
    if b_sections:
        b_spec = (pl.BlockSpec((1, tn, tk), lambda i, j, q: (q, j, 0)) if tb
                  else pl.BlockSpec((1, tk, tn), lambda i, j, q: (j, q, 0)))
    else:
        b_spec = pl.BlockSpec((tn, tk), lambda i, j, q: (j, q)) if tb else pl.BlockSpec((tk, tn), lambda i, j, q: (q, j))
    add_spec = [pl.BlockSpec((tm, tn), lambda i, j, q: (i, jnp.maximum(j - add_from, 0)))] if has_add else []
    if out_sections:
        out_spec, out_shape = pl.BlockSpec((1, tm, tn), lambda i, j, q: (j, i, 0)), (n // tn, m, tn)
    else:
        out_spec, out_shape = pl.BlockSpec((tm, tn), lambda i, j, q: (i, j)), (m, n)
    return pl.pallas_call(
        body, name=name, grid=(m // tm, n // tn, nk),
        in_specs=[a_spec, b_spec] + add_spec + ([pl.BlockSpec(memory_space=pl.ANY)] if after is not None else []),
        out_specs=out_spec, out_shape=jax.ShapeDtypeStruct(out_shape, out_dtype),
        scratch_shapes=[pltpu.VMEM((tm, tn), F32)] if nk > 1 else [],
        compiler_params=_params(("parallel", "parallel", "arbitrary")),
    )(*([a, b] + ([add] if has_add else []) + ([after] if after is not None else [])))


def _h_fn(x, g, m):
    shift, scale = m[:, 0:D], m[:, D:2 * D]
    r = lax.rsqrt(jnp.mean(x * x, axis=-1, keepdims=True) + EPS)
    return (x * r * g) * (1.0 + scale) + shift


def _norm_fwd(x, g, mod, bmod, *, name):
    rows = x.shape[0]
    t = min(512, rows)

    def body(x_ref, g_ref, m_ref, b_ref, h_ref):
        h_ref[...] = _h_fn(x_ref[...], g_ref[...], m_ref[...] + b_ref[...]).astype(BF16)

    vec = lambda w: pl.BlockSpec((1, w), lambda i: (0, 0))
    return pl.pallas_call(
        body, name=name, grid=(rows // t,),
        in_specs=[pl.BlockSpec((t, D), lambda i: (i, 0)), vec(D), vec(3 * D), vec(3 * D)],
        out_specs=pl.BlockSpec((t, D), lambda i: (i, 0)),
        out_shape=jax.ShapeDtypeStruct((rows, D), BF16),
        compiler_params=_params(("parallel",)),
    )(x, g, mod, bmod)


def _norm_bwd(x, g, mod, bmod, dh_parts, resid, *, name):
    rows = x.shape[0]
    t = min(512, rows)
    n_parts = len(dh_parts)
    has_resid = resid is not None

    def body(*refs):
        x_ref, g_ref, m_ref, b_ref = refs[:4]
        parts = refs[4:4 + n_parts]
        r_ref = refs[4 + n_parts] if has_resid else None
        dx_ref, dg_ref, dm_ref = refs[-3:]
        i = pl.program_id(0)
        dh = parts[0][...]
        for p in parts[1:]:
            dh = dh + p[...]
        _, vjp = jax.vjp(_h_fn, x_ref[...], g_ref[...], m_ref[...] + b_ref[...])
        dx, dg, dm = vjp(dh)
        if has_resid:
            dx = dx + r_ref[...]
        dx_ref[...] = dx

        @pl.when(i == 0)
        def _():
            dg_ref[...] = dg
            dm_ref[...] = dm

        @pl.when(i > 0)
        def _():
            dg_ref[...] += dg
            dm_ref[...] += dm

    vec = lambda w: pl.BlockSpec((1, w), lambda i: (0, 0))
    tile = pl.BlockSpec((t, D), lambda i: (i, 0))
    ins = [x, g, mod, bmod, *dh_parts] + ([resid] if has_resid else [])
    return pl.pallas_call(
        body, name=name, grid=(rows // t,),
        in_specs=[tile, vec(D), vec(3 * D), vec(3 * D)] + [tile] * (n_parts + int(has_resid)),
        out_specs=[tile, vec(D), vec(3 * D)],
        out_shape=[jax.ShapeDtypeStruct((rows, D), F32), jax.ShapeDtypeStruct((1, D), F32),
                   jax.ShapeDtypeStruct((1, 3 * D), F32)],
        compiler_params=_params(("arbitrary",)),
    )(*ins)


def _conv_tile(u_ref, r0, t, rows, w0, w1, w2):
    u = u_ref[pl.ds(r0, t), :]
    prev8 = u_ref[pl.ds(pl.multiple_of(jnp.maximum(r0 - 8, 0), 8), 8), :]
    next8 = u_ref[pl.ds(pl.multiple_of(jnp.minimum(r0 + t, rows - 8), 8), 8), :]
    r8 = lax.broadcasted_iota(jnp.int32, (8, DH), 0)
    prev_row = jnp.sum(jnp.where(r8 == 7, prev8, 0.0), axis=0, keepdims=True)
    next_row = jnp.sum(jnp.where(r8 == 0, next8, 0.0), axis=0, keepdims=True)
    prev_row = jnp.where(r0 > 0, prev_row, 0.0)
    next_row = jnp.where(r0 + t < rows, next_row, 0.0)
    ri = lax.broadcasted_iota(jnp.int32, (t, DH), 0)
    um1 = jnp.where(ri == 0, prev_row, pltpu.roll(u, 1, 0))
    up1 = jnp.where(ri == t - 1, next_row, pltpu.roll(u, t - 1, 0))
    return w0 * um1 + w1 * u + w2 * up1, um1, u, up1


def _rowlocal(z, norm):
    y = _silu(z)
    return y * lax.rsqrt(jnp.sum(y * y, axis=-1, keepdims=True) + EPS) if norm else y


def _prep_fwd(p, wconv, n_norm, *, name):
    rows, nb = p.shape[0], p.shape[1] // DH
    t = min(512, rows)
    out = None
    for norm, b0, b1 in ((True, 0, n_norm), (False, n_norm, nb)):
        def body(u_ref, w_ref, *rest, norm=norm):
            o_ref = rest[-1]
            w0, w1, w2 = w_ref[0:1, :], w_ref[1:2, :], w_ref[2:3, :]

            def step(s, carry):
                r0 = pl.multiple_of(s * t, t)
                z, _, _, _ = _conv_tile(u_ref, r0, t, rows, w0, w1, w2)
                o_ref[0, pl.ds(r0, t), :] = _rowlocal(z, norm)
                return carry

            lax.fori_loop(0, rows // t, step, 0)

        held = [] if out is None else [out]
        out = pl.pallas_call(
            body, name=f"{name}_{'norm' if norm else 'plain'}", grid=(b1 - b0,),
            in_specs=[pl.BlockSpec((rows, DH), lambda j, b0=b0: (0, b0 + j)),
                      pl.BlockSpec((3, DH), lambda j, b0=b0: (0, b0 + j))] + [ANY] * len(held),
            out_specs=pl.BlockSpec((1, rows, DH), lambda j, b0=b0: (b0 + j, 0, 0)),
            out_shape=jax.ShapeDtypeStruct((nb, rows, DH), F32),
            input_output_aliases={2: 0} if held else {},
            compiler_params=_params(("parallel",)),
        )(p, wconv, *held)
    return out


def _prep_bwd(p, wconv, d_a, d_b, n_norm, *, name):
    rows, nb = p.shape[0], p.shape[1] // DH
    t = min(512, rows)
    outs = []
    for norm, b0, b1 in ((True, 0, n_norm), (False, n_norm, nb)):
        def body(u_ref, w_ref, da_ref, db_ref, *rest, norm=norm):
            du_ref, dw_ref, dz_ref = rest[-3:]
            w0, w1, w2 = w_ref[0:1, :], w_ref[1:2, :], w_ref[2:3, :]

            def step1(s, carry):
                a0, a1, a2 = carry
                r0 = pl.multiple_of(s * t, t)
                z, um1, u, up1 = _conv_tile(u_ref, r0, t, rows, w0, w1, w2)
                _, vjp = jax.vjp(lambda zz: _rowlocal(zz, norm), z)
                (dz,) = vjp(da_ref[0, pl.ds(r0, t), :] + db_ref[0, pl.ds(r0, t), :])
                dz_ref[pl.ds(r0, t), :] = dz
                red = lambda v: jnp.sum(v, axis=0, keepdims=True)
                return a0 + red(dz * um1), a1 + red(dz * u), a2 + red(dz * up1)

            zero = jnp.zeros((1, DH), F32)
            a0, a1, a2 = lax.fori_loop(0, rows // t, step1, (zero, zero, zero))
            dw_ref[0:1, :] = a0
            dw_ref[1:2, :] = a1
            dw_ref[2:3, :] = a2

            def step2(s, carry):
                r0 = pl.multiple_of(s * t, t)
                du, _, _, _ = _conv_tile(dz_ref, r0, t, rows, w2, w1, w0)
                du_ref[pl.ds(r0, t), :] = du.astype(BF16)
                return carry

            lax.fori_loop(0, rows // t, step2, 0)

        col = pl.BlockSpec((rows, DH), lambda j, b0=b0: (0, b0 + j))
        w_spec = pl.BlockSpec((3, DH), lambda j, b0=b0: (0, b0 + j))
        d_spec = pl.BlockSpec((1, rows, DH), lambda j, b0=b0: (b0 + j, 0, 0))
        outs = pl.pallas_call(
            body, name=f"{name}_{'norm' if norm else 'plain'}", grid=(b1 - b0,),
            in_specs=[col, w_spec, d_spec, d_spec] + [ANY] * len(outs), out_specs=[col, w_spec],
            out_shape=[jax.ShapeDtypeStruct((rows, nb * DH), BF16), jax.ShapeDtypeStruct((3, nb * DH), F32)],
            input_output_aliases={4: 0, 5: 1} if outs else {},
            scratch_shapes=[pltpu.VMEM((rows, DH), F32)],
            compiler_params=_params(("parallel",)),
        )(p, wconv, d_a, d_b, *outs)
    return outs


def _gates_fn(pab, avec, dvec):
    t = pab.shape[0]
    lane = lax.broadcasted_iota(jnp.int32, pab.shape, 1)
    xg = pab + dvec
    sp = jnp.maximum(xg, 0.0) + jnp.log(1.0 + jnp.exp(-jnp.abs(xg)))
    g = jnp.where(lane < 2 * H, -jnp.exp(avec) * sp, 0.0)
    ii = lax.broadcasted_iota(jnp.int32, (t, t), 0)
    jj = lax.broadcasted_iota(jnp.int32, (t, t), 1)
    same = (ii >> LOG_CH) == (jj >> LOG_CH)
    cum_f = _dot(jnp.where(same & (jj <= ii), 1.0, 0.0), g, precision=HI)
    cum_r = _dot(jnp.where(same & (jj >= ii), 1.0, 0.0), g, precision=HI)
    return jnp.where(lane < H, cum_f, jnp.where(lane < 2 * H, cum_r, jnp.where(lane < 4 * H, jax.nn.sigmoid(pab), 0.0)))


def _gates_fwd(pab, avec, dvec, *, name):
    rows = pab.shape[0]
    t = min(512, rows)

    def body(p_ref, a_ref, d_ref, o_ref):
        o_ref[...] = _gates_fn(p_ref[...], a_ref[...], d_ref[...])

    vec = pl.BlockSpec((1, DH), lambda i: (0, 0))
    tile = pl.BlockSpec((t, DH), lambda i: (i, 0))
    return pl.pallas_call(
        body, name=name, grid=(rows // t,), in_specs=[tile, vec, vec], out_specs=tile,
        out_shape=jax.ShapeDtypeStruct((rows, DH), F32), compiler_params=_params(("parallel",)),
    )(pab, avec, dvec)


def _gates_bwd(pab, avec, dvec, d_a, d_b, *, name):
    rows = pab.shape[0]
    t = min(512, rows)

    def body(p_ref, a_ref, d_ref, da_ref, db_ref, dp_ref, dav_ref, ddv_ref):
        i = pl.program_id(0)
        _, vjp = jax.vjp(_gates_fn, p_ref[...], a_ref[...], d_ref[...])
        dp, dav, ddv = vjp(da_ref[...] + db_ref[...])
        dp_ref[...] = dp.astype(BF16)

        @pl.when(i == 0)
        def _():
            dav_ref[...] = dav
            ddv_ref[...] = ddv

        @pl.when(i > 0)
        def _():
            dav_ref[...] += dav
            ddv_ref[...] += ddv

    vec = pl.BlockSpec((1, DH), lambda i: (0, 0))
    tile = pl.BlockSpec((t, DH), lambda i: (i, 0))
    return pl.pallas_call(
        body, name=name, grid=(rows // t,), in_specs=[tile, vec, vec, tile, tile], out_specs=[tile, vec, vec],
        out_shape=[jax.ShapeDtypeStruct((rows, DH), BF16), jax.ShapeDtypeStruct((1, DH), F32),
                   jax.ShapeDtypeStruct((1, DH), F32)],
        compiler_params=_params(("arbitrary",)),
    )(pab, avec, dvec, d_a, d_b)


def _dot_general(a, b, dims, precision=None):
    return lax.dot_general(a, b, (dims, ((), ())), precision=precision, preferred_element_type=F32)


@functools.partial(jax.custom_vjp, nondiff_argnums=(2,))
def _dot_bf16(a, b, dims):
    return _dot_general(a.astype(BF16), b.astype(BF16), dims)


def _dot_bf16_fwd(a, b, dims):
    return _dot_bf16(a, b, dims), (a, b)


def _dot_bf16_bwd(dims, res, g):
    a, b = res
    (ca,), (cb,) = dims
    da = _dot_bf16(g, b, ((1,), (1 - cb,))) if ca == 1 else _dot_bf16(b, g, ((1 - cb,), (1,)))
    db = _dot_bf16(a, g, ((1 - ca,), (0,))) if cb == 0 else _dot_bf16(g, a, ((0,), (1 - ca,)))
    return da, db


_dot_bf16.defvjp(_dot_bf16_fwd, _dot_bf16_bwd)


def _dot(a, b, dims=((1,), (0,)), precision=None):
    if precision is None and a.dtype == F32 and b.dtype == F32:
        return _dot_bf16(a, b, dims)
    return _dot_general(a, b, dims, precision)


_NT = ((1,), (1,))
_TN = ((0,), (0,))


@jax.custom_vjp
def _saved_inverse(neg_a, inv):
    return inv


def _saved_inverse_fwd(neg_a, inv):
    return inv, inv


def _saved_inverse_bwd(inv, d_inv):
    idx = range(len(inv))
    left = [_dot(inv[i], d_inv[i], _TN) for i in idx]
    d_neg_a = [_dot(left[i], inv[i], _NT) for i in idx]
    return d_neg_a, [jnp.zeros_like(t) for t in inv]


_saved_inverse.defvjp(_saved_inverse_fwd, _saved_inverse_bwd)


def _pairs(s, q, k, v, gcol, bcol, revs, inv_saved=None):
    idx = range(len(revs))
    ii = lax.broadcasted_iota(jnp.int32, (PAIR, PAIR), 0)
    jj = lax.broadcasted_iota(jnp.int32, (PAIR, PAIR), 1)
    same = (ii >> LOG_CH) == (jj >> LOG_CH)
    incl_d = (same & (ii >= jj), same & (ii <= jj))
    strict_d = (same & (ii > jj), same & (ii < jj))
    incl = [incl_d[int(r)] for r in revs]
    strict = [strict_d[int(r)] for r in revs]
    eye = jnp.where(ii == jj, 1.0, 0.0)
    gc_i = [jnp.broadcast_to(gcol[i], (PAIR, DH)) for i in idx]
    gc_j = [gc_i[i].T for i in idx]
    decay = [jnp.where(incl[i], jnp.exp(jnp.where(incl[i], gc_i[i] - gc_j[i], 0.0)), 0.0) for i in idx]
    b_b = [jnp.broadcast_to(bcol[i], (PAIR, DH)) for i in idx]
    kb = [k[i] * b_b[i] for i in idx]
    kk = [_dot(kb[i], k[i], _NT) for i in idx]
    bp = [jnp.where(strict[i], -kk[i] * decay[i], 0.0) for i in idx]
    if inv_saved is not None:
        inv = _saved_inverse(bp, inv_saved)
    else:
        low = bp
        for prec_sq, prec_acc in PREC_POWERS:
            bp = [_dot(bp[i], bp[i], precision=prec_sq) for i in idx]
            more = [_dot(low[i], bp[i], precision=prec_acc) for i in idx]
            low = [low[i] + bp[i] + more[i] for i in idx]
        inv = [eye + low[i] for i in idx]
    eg = [jnp.exp(gc_i[i]) for i in idx]
    sol = [_dot(inv[i], jnp.concatenate([v[i] * b_b[i], kb[i] * eg[i]], axis=1)) for i in idx]
    u_val = [sol[i][:, :DH] for i in idx]
    w_key = [sol[i][:, DH:] for i in idx]
    row = lax.broadcasted_iota(jnp.int32, (PAIR, 1), 0)
    has_q = q[0] is not None
    if has_q:
        qc = [q[i] * (DH ** -0.5) for i in idx]
        qk = [_dot(qc[i], k[i], _NT) for i in idx]
        attn = [qk[i] * decay[i] for i in idx]
        qd = [qc[i] * eg[i] for i in idx]
    outs = [[None, None] for _ in idx]
    zeros = jnp.zeros((CH, DH), F32)
    for step in range(2):
        cidx = [(1 - step) if revs[i] else step for i in idx]
        sl = [slice(c * CH, (c + 1) * CH) for c in cidx]
        last = [c * CH if revs[i] else c * CH + CH - 1 for i, c in zip(idx, cidx)]
        gl = [jnp.sum(jnp.where(row == last[i], gcol[i], 0.0), axis=0, keepdims=True) for i in idx]
        k_tail = [k[i][sl[i]] * jnp.exp(gl[i] - gc_i[i][sl[i]]) for i in idx]
        ws = [_dot(w_key[i][sl[i]], s[i]) for i in idx]
        v_new = [u_val[i][sl[i]] - ws[i] for i in idx]
        if has_q:
            v_pad = [jnp.concatenate([v_new[i], zeros] if cidx[i] == 0 else [zeros, v_new[i]], axis=0) for i in idx]
            o_state = [_dot(qd[i][sl[i]], s[i]) for i in idx]
            o_local = [_dot(attn[i][sl[i]], v_pad[i]) for i in idx]
            for i in idx:
                outs[i][cidx[i]] = o_state[i] + o_local[i]
        kv = [_dot(k_tail[i], v_new[i], _TN) for i in idx]
        s = [s[i] * jnp.exp(gl[i]) + kv[i] for i in idx]
    return s, ([jnp.concatenate(outs[i], axis=0) for i in idx] if has_q else None), inv


def _lane_col(tile, idx):
    lane = lax.broadcasted_iota(jnp.int32, tile.shape, 1)
    return jnp.sum(jnp.where(lane == idx, tile, 0.0), axis=1, keepdims=True)


def _gdn_fwd(qkv, gb, s0f, s0b, has_q, *, name):
    nb, rows, _ = qkv.shape
    n = rows // PAIR
    qoff = H if has_q else 0

    def body(qf_ref, qb_ref, gf_ref, gr_ref, s0f_ref, s0b_ref, of_ref, ob_ref, ssf_ref, ssb_ref, tsf_ref, tsb_ref,
             sf_ref, sb_ref):
        @pl.when(pl.program_id(0) == 0)
        def _():
            sf_ref[...] = s0f_ref[...]
            sb_ref[...] = s0b_ref[...]

        gtiles = (gf_ref[...], gr_ref[...])

        dirs = ((qf_ref, sf_ref, ssf_ref, of_ref), (qb_ref, sb_ref, ssb_ref, ob_ref))
        ts_refs = (tsf_ref, tsb_ref)

        work = [(h, d) for h in range(H) for d in range(2)]
        loaded = []
        for h, d in work:
            q_ref, s_ref, _, _ = dirs[d]
            loaded.append((s_ref[h], q_ref[h] if has_q else None, q_ref[qoff + h], q_ref[qoff + H + h],
                           _lane_col(gtiles[d], d * H + h), _lane_col(gtiles[d], 2 * H + d * H + h)))
        s_new, o, inv = _pairs(*[list(col) for col in zip(*loaded)], revs=[d == 1 for _, d in work])
        for i, (h, d) in enumerate(work):
            _, s_ref, ss_ref, o_ref = dirs[d]
            ss_ref[0, h] = loaded[i][0]
            ts_refs[d][0, h] = inv[i]
            s_ref[h] = s_new[i]
            o_ref[h] = o[i] if has_q else jnp.zeros((PAIR, DH), F32)

    fwd3 = lambda i: (0, i, 0)
    rev3 = lambda i: (0, n - 1 - i, 0)
    state = pl.BlockSpec((H, DH, DH), lambda i: (0, 0, 0))
    saved = pl.BlockSpec((1, H, DH, DH), lambda i: (i, 0, 0, 0))
    return pl.pallas_call(
        body, name=name, grid=(n,),
        in_specs=[pl.BlockSpec((nb, PAIR, DH), fwd3), pl.BlockSpec((nb, PAIR, DH), rev3),
                  pl.BlockSpec((PAIR, DH), lambda i: (i, 0)), pl.BlockSpec((PAIR, DH), lambda i: (n - 1 - i, 0)),
                  state, state],
        out_specs=[pl.BlockSpec((H, PAIR, DH), fwd3), pl.BlockSpec((H, PAIR, DH), rev3), saved, saved, saved, saved,
                   state, state],
        out_shape=[jax.ShapeDtypeStruct((H, rows, DH), F32)] * 2 + [jax.ShapeDtypeStruct((n, H, DH, DH), F32)] * 4
        + [jax.ShapeDtypeStruct((H, DH, DH), F32)] * 2,
        compiler_params=_params(("arbitrary",)),
    )(qkv, qkv, gb, gb, s0f, s0b)


def _gdn_bwd(qkv, gb, ssf, ssb, tsf, tsb, do, dsf, dsb, has_q, *, name):
    nb, rows, _ = qkv.shape
    n = rows // PAIR
    qoff = H if has_q else 0

    def body(qf_ref, qb_ref, gf_ref, gr_ref, ssf_ref, ssb_ref, tsf_ref, tsb_ref, dof_ref, dob_ref, dsf0_ref, dsb0_ref,
             dqf_ref, dqb_ref, dgf_ref, dgr_ref, dsf_ref, dsb_ref):
        ts_refs = (tsf_ref, tsb_ref)
        @pl.when(pl.program_id(0) == 0)
        def _():
            dsf_ref[...] = dsf0_ref[...]
            dsb_ref[...] = dsb0_ref[...]

        gtiles = (gf_ref[...], gr_ref[...])
        lane = lax.broadcasted_iota(jnp.int32, (PAIR, DH), 1)

        dirs = ((qf_ref, ssf_ref, dof_ref, dsf_ref, dqf_ref), (qb_ref, ssb_ref, dob_ref, dsb_ref, dqb_ref))

        work = [(h, d) for h in range(H) for d in range(2)]
        revs = [d == 1 for _, d in work]
        s_in, q_in, k_in, v_in, g_in, b_in, ds_out, do_out, inv_in = [], [], [], [], [], [], [], [], []
        for h, d in work:
            q_ref, ss_ref, do_ref, ds_ref, _ = dirs[d]
            s_in.append(ss_ref[0, h])
            inv_in.append(ts_refs[d][0, h])
            q_in.append(q_ref[h] if has_q else None)
            k_in.append(q_ref[qoff + h])
            v_in.append(q_ref[qoff + H + h])
            g_in.append(_lane_col(gtiles[d], d * H + h))
            b_in.append(_lane_col(gtiles[d], 2 * H + d * H + h))
            ds_out.append(ds_ref[h])
            do_out.append(do_ref[h] if has_q else None)
        if has_q:
            _, vjp = jax.vjp(lambda s_, q_, k_, v_, g_, b_: _pairs(s_, q_, k_, v_, g_, b_, revs, inv_in)[:2],
                             s_in, q_in, k_in, v_in, g_in, b_in)
            ds, dq, dk, dv, dg, db = vjp((ds_out, do_out))
        else:
            _, vjp = jax.vjp(lambda s_, k_, v_, g_, b_: _pairs(s_, q_in, k_, v_, g_, b_, revs, inv_in)[0],
                             s_in, k_in, v_in, g_in, b_in)
            ds, dk, dv, dg, db = vjp(ds_out)
        dgb = [jnp.zeros((PAIR, DH), F32)] * 2
        for i, (h, d) in enumerate(work):
            dq_ref, ds_ref = dirs[d][4], dirs[d][3]
            ds_ref[h] = ds[i]
            if has_q:
                dq_ref[h] = dq[i]
            dq_ref[qoff + h], dq_ref[qoff + H + h] = dk[i], dv[i]
            dgb[d] = (dgb[d] + jnp.where(lane == d * H + h, dg[i], 0.0)
                      + jnp.where(lane == 2 * H + d * H + h, db[i], 0.0))
        dgf_ref[...] = dgb[0]
        dgr_ref[...] = dgb[1]

    fwd3 = lambda i: (0, n - 1 - i, 0)
    rev3 = lambda i: (0, i, 0)
    state = pl.BlockSpec((H, DH, DH), lambda i: (0, 0, 0))
    saved = pl.BlockSpec((1, H, DH, DH), lambda i: (n - 1 - i, 0, 0, 0))
    gf_spec = pl.BlockSpec((PAIR, DH), lambda i: (n - 1 - i, 0))
    gr_spec = pl.BlockSpec((PAIR, DH), lambda i: (i, 0))
    return pl.pallas_call(
        body, name=name, grid=(n,),
        in_specs=[pl.BlockSpec((nb, PAIR, DH), fwd3), pl.BlockSpec((nb, PAIR, DH), rev3), gf_spec, gr_spec,
                  saved, saved, saved, saved,
                  pl.BlockSpec((H, PAIR, DH), fwd3), pl.BlockSpec((H, PAIR, DH), rev3), state, state],
        out_specs=[pl.BlockSpec((nb, PAIR, DH), fwd3), pl.BlockSpec((nb, PAIR, DH), rev3), gf_spec, gr_spec, state, state],
        out_shape=[jax.ShapeDtypeStruct((nb, rows, DH), F32)] * 2 + [jax.ShapeDtypeStruct((rows, DH), F32)] * 2
        + [jax.ShapeDtypeStruct((H, DH, DH), F32)] * 2,
        compiler_params=_params(("arbitrary",)),
    )(qkv, qkv, gb, gb, ssf, ssb, tsf, tsb, do, do, dsf, dsb)


def _stage1(zb, ua, va, za, o, gon, lng, lnb, wsp, bsp):
    gv = [_gelu(t) for t in va]
    mu = sum(jnp.sum(t, axis=-1, keepdims=True) for t in gv) * (1.0 / D)
    xc = [t - mu for t in gv]
    var = sum(jnp.sum(t * t, axis=-1, keepdims=True) for t in xc) * (1.0 / D)
    rs = lax.rsqrt(var + EPS)
    ya, yb = [], []
    for g in range(H):
        vv = xc[g] * rs * lng[g] + lnb[g]
        s = _dot(wsp[g], vv) + bsp[g]
        ya.append(_gelu(ua[g]) * s * _silu(za[g]))
        r = lax.rsqrt(jnp.mean(o[g] * o[g], axis=-1, keepdims=True) + EPS)
        yb.append(o[g] * r * gon * _silu(zb[g]))
    return ya, yb


def _stage2(ma, mb, ga, gb):
    return jax.nn.sigmoid(ga) * ma + jax.nn.sigmoid(gb) * mb


def _stage3(out, x, tgt, gpost, gate):
    r = out * lax.rsqrt(jnp.mean(out * out, axis=-1, keepdims=True) + EPS) * gpost
    err = x + gate * r - tgt
    return 0.5 * jnp.sum(jnp.mean(err * err, axis=-1, keepdims=True), axis=0, keepdims=True)


def _post(p_rest, o_f, o_b, x, tgt, mod, bmod, gon, lng, lnb, wsp, bspb, wpa, wpb, wout, gpost, *, name):
    rows = x.shape[0]
    n = rows // GM
    lanes = lambda g: slice(g * DH, (g + 1) * DH)
    bdot = lambda a, w_ref: _dot(a.astype(BF16), w_ref[...])
    bdot_t = lambda a, w_ref: _dot(a.astype(BF16), w_ref[...], _NT)

    def body(p_ref, of_ref, ob_ref, x_ref, t_ref, m_ref, bm_ref, gon_ref, lng_ref, lnb_ref, wsp_ref, bsp_ref,
             wpa_ref, wpb_ref, wout_ref, gp_ref,
             loss_ref, dp_ref, do_ref, dx_ref, ya_ref, yb_ref, mg_ref, dma_ref, dmb_ref, dout_ref,
             dvec_ref, dgon_ref, dwsp_ref, dbsp_ref):
        @pl.when(pl.program_id(0) == 0)
        def _():
            loss_ref[...] = jnp.zeros_like(loss_ref)
            dvec_ref[...] = jnp.zeros_like(dvec_ref)
            dgon_ref[...] = jnp.zeros_like(dgon_ref)
            dwsp_ref[...] = jnp.zeros_like(dwsp_ref)
            dbsp_ref[...] = jnp.zeros_like(dbsp_ref)

        piece = lambda blk: [p_ref[:, blk * D + g * DH: blk * D + (g + 1) * DH].astype(F32) for g in range(H)]
        zb, ua, va, za = piece(0), piece(1), piece(2), piece(3)
        o = [of_ref[g] + ob_ref[g] for g in range(H)]
        gon = gon_ref[...]
        lng = [lng_ref[:, lanes(g)] for g in range(H)]
        lnb = [lnb_ref[:, lanes(g)] for g in range(H)]
        wsp = [wsp_ref[g] for g in range(H)]
        bsp = [bsp_ref[g] for g in range(H)]
        (ya, yb), vjp1 = jax.vjp(_stage1, zb, ua, va, za, o, gon, lng, lnb, wsp, bsp)
        y_a, y_b = jnp.concatenate(ya, axis=1), jnp.concatenate(yb, axis=1)
        ma, mb = bdot(y_a, wpa_ref), bdot(y_b, wpb_ref)
        ga, gb = p_ref[:, 4 * D:5 * D].astype(F32), p_ref[:, 5 * D:6 * D].astype(F32)
        merged, vjp2 = jax.vjp(_stage2, ma, mb, ga, gb)
        out = bdot(merged, wout_ref)
        gate = m_ref[:, 2 * D:3 * D] + bm_ref[:, 2 * D:3 * D]
        loss, vjp3 = jax.vjp(_stage3, out, x_ref[...], t_ref[...], gp_ref[...], gate)
        loss_ref[...] += jnp.broadcast_to(loss, loss_ref.shape)

        dout, dx, _, dgpost, dgate = vjp3(jnp.ones((1, 1), F32))
        dx_ref[...] = dx
        dmerged = bdot_t(dout, wout_ref)
        dma, dmb, dga, dgb = vjp2(dmerged)
        dya, dyb = bdot_t(dma, wpa_ref), bdot_t(dmb, wpb_ref)
        dzb, dua, dva, dza, do, dgon, dlng, dlnb, dwsp, dbsp = vjp1(
            ([dya[:, lanes(g)] for g in range(H)], [dyb[:, lanes(g)] for g in range(H)]))

        for blk, dlist in enumerate((dzb, dua, dva, dza)):
            for g in range(H):
                dp_ref[:, blk * D + g * DH: blk * D + (g + 1) * DH] = dlist[g].astype(BF16)
        dp_ref[:, 4 * D:5 * D] = dga.astype(BF16)
        dp_ref[:, 5 * D:6 * D] = dgb.astype(BF16)
        for g in range(H):
            do_ref[g] = do[g]
            dwsp_ref[g] += dwsp[g]
            dbsp_ref[g] += dbsp[g]
            dvec_ref[2:3, lanes(g)] += dlng[g]
            dvec_ref[3:4, lanes(g)] += dlnb[g]
        dvec_ref[0:1, :] += dgpost
        dvec_ref[1:2, :] += dgate
        dgon_ref[0:1, :] += dgon
        ya_ref[...] = y_a.astype(BF16)
        yb_ref[...] = y_b.astype(BF16)
        mg_ref[...] = merged.astype(BF16)
        dma_ref[...] = dma.astype(BF16)
        dmb_ref[...] = dmb.astype(BF16)
        dout_ref[...] = dout.astype(BF16)

    row = lambda w: pl.BlockSpec((GM, w), lambda i: (i, 0))
    heads = pl.BlockSpec((H, GM, DH), lambda i: (0, i, 0))
    full = lambda shape: pl.BlockSpec(shape, lambda i: tuple(0 for _ in shape))
    sds = jax.ShapeDtypeStruct
    return pl.pallas_call(
        body, name=name, grid=(n,),
        in_specs=[row(6 * D), heads, heads, row(D), row(D), full((1, 3 * D)), full((1, 3 * D)), full((1, DH)),
                  full((1, D)), full((1, D)), full((H, GM, GM)), full((H, GM, GM)),
                  full((D, D)), full((D, D)), full((D, D)), full((1, D))],
        out_specs=[full((8, DH)), row(6 * D), heads, row(D)] + [row(D)] * 6
        + [full((8, D)), full((8, DH)), full((H, GM, GM)), full((H, GM, GM))],
        out_shape=[sds((8, DH), F32), sds((rows, 6 * D), BF16), sds((H, rows, DH), F32), sds((rows, D), F32)]
        + [sds((rows, D), BF16)] * 6 + [sds((8, D), F32), sds((8, DH), F32), sds((H, GM, GM), F32), sds((H, GM, GM), F32)],
        compiler_params=_params(("arbitrary",)),
    )(p_rest, o_f, o_b, x, tgt, mod, bmod, gon, lng, lnb, wsp, bspb, wpa, wpb, wout, gpost)


def _silu_rows(c, *, name):
    def body(c_ref, o_ref):
        o_ref[...] = _silu(c_ref[...])

    return pl.pallas_call(body, name=name, out_shape=jax.ShapeDtypeStruct(c.shape, F32))(c)


def _dsilu_mul(c, d, *, name):
    def body(c_ref, d_ref, o_ref):
        _, vjp = jax.vjp(_silu, c_ref[...])
        (o_ref[...],) = vjp(d_ref[...])

    return pl.pallas_call(body, name=name, out_shape=jax.ShapeDtypeStruct(c.shape, F32))(c, d)


def _adamw(w, g, m, v, *, name):
    rows, cols = w.shape[-2:]
    t = _tile(rows if g.ndim == 2 else g.shape[1])
    c1 = 1.0 / (1.0 - ADAM_B1 ** ADAM_STEP)
    c2 = 1.0 / (1.0 - ADAM_B2 ** ADAM_STEP)

    def body(w_ref, g_ref, m_ref, v_ref, go_ref, d_ref, mo_ref, vo_ref):
        blk = lambda r: r[...].reshape(t, cols)
        gv = blk(g_ref)
        mn = ADAM_B1 * blk(m_ref) + (1.0 - ADAM_B1) * gv
        vn = ADAM_B2 * blk(v_ref) + (1.0 - ADAM_B2) * (gv * gv)
        delta = -ADAM_LR * ((mn * c1) / (jnp.sqrt(vn * c2) + ADAM_EPS) + ADAM_WD * blk(w_ref))
        for ref, val in ((go_ref, gv), (d_ref, delta), (mo_ref, mn), (vo_ref, vn)):
            ref[...] = val.reshape(ref.shape)

    tile = (pl.BlockSpec((1, t, cols), lambda i: (0, i, 0)) if w.ndim == 3 else pl.BlockSpec((t, cols), lambda i: (i, 0)))
    if g.ndim == 3:
        per = g.shape[1] // t
        g_tile = pl.BlockSpec((1, t, cols), lambda i: (i // per, i % per, 0))
    else:
        g_tile = pl.BlockSpec((t, cols), lambda i: (i, 0))
    return pl.pallas_call(
        body, name=name, grid=(rows // t,),
        in_specs=[tile, g_tile, tile, tile], out_specs=[tile] * 4,
        out_shape=[jax.ShapeDtypeStruct(w.shape, F32)] * 4,
        compiler_params=_params(("parallel",)),
    )(w, g, m, v)


def _add_sibling(full, got, where, out_dtype, *, name):
    s, rows, cols = full.shape
    hr = rows // 2
    t = _tile(hr)
    nt = hr // t

    def body(w_ref, a_ref, b_ref, o_ref):
        o_ref[...] = (a_ref[...].astype(F32) + b_ref[...].astype(F32)).astype(out_dtype)

    tile = pl.BlockSpec((1, t, cols), lambda j, i, w: (j, i, 0))
    return pl.pallas_call(
        body, name=name,
        grid_spec=pltpu.PrefetchScalarGridSpec(
            num_scalar_prefetch=1, grid=(s, nt),
            in_specs=[pl.BlockSpec((1, t, cols), lambda j, i, w: (j, w[0] * nt + i, 0)), tile], out_specs=tile),
        out_shape=jax.ShapeDtypeStruct((s, hr, cols), out_dtype), compiler_params=_params(("parallel", "parallel")),
    )(where, full, got)


def _sum_chips(own, slots, where, n_out, which, *, name):
    _, hr, cols = own.shape
    t = _tile(hr)

    def body(w_ref, a_ref, s_ref, o_ref):
        f = lambda v: v.astype(F32)
        o_ref[0] = ((f(a_ref[0]) + f(s_ref[0])) + f(s_ref[1])) + f(s_ref[2])

    return pl.pallas_call(
        body, name=name,
        grid_spec=pltpu.PrefetchScalarGridSpec(
            num_scalar_prefetch=1, grid=(hr // t,),
            in_specs=[pl.BlockSpec((1, t, cols), lambda i, w: (w[1], i, 0)),
                      pl.BlockSpec((N_CHIPS - 1, t, cols), lambda i, w: (0, i, 0))],
            out_specs=pl.BlockSpec((1, t, cols), lambda i, w: (w[which], i, 0))),
        out_shape=jax.ShapeDtypeStruct((n_out, hr, cols), F32), compiler_params=_params(("parallel",)),
    )(where, own, slots)


def _local_step(x, c, ctx, c_ctx, tgt, w_mod, b_mod, g_pre, g_post, w_qkv, w_ab, w_rest, w_conv, a_log, dt_bias,
                g_onorm, gm_ln_g, gm_ln_b, w_sp, b_sp, late_weights, on_large_gradients=None, run_after=None):
    rows, rows_c = x.shape[0], ctx.shape[0]
    cc = jnp.zeros((16, D), F32).at[0].set(c[0]).at[1].set(c_ctx)
    scc = _silu_rows(cc, name="silu_cond")
    mod = _mm(scc, w_mod, b_sections=True, after=run_after, name="mod_fwd")
    mod_x, mod_c = mod[0:1], mod[1:2]
    avec = jnp.zeros((1, DH), F32).at[0, :2 * H].set(a_log.reshape(-1))
    dvec = jnp.zeros((1, DH), F32).at[0, :2 * H].set(dt_bias.reshape(-1))
    bspb = jnp.broadcast_to(b_sp[:, :, None], (H, GM, GM))
    w_kv, wconv_kv = w_qkv[:, D:], w_conv[:, D:]

    h_c = _norm_fwd(ctx, g_pre, mod_c, b_mod, name="norm_fwd_ctx")
    pc_kv = _mm(h_c, w_kv, name="inproj_ctx_kv")
    pc_ab = _mm(h_c, w_ab, name="inproj_ctx_ab")
    kv_c = _prep_fwd(pc_kv, wconv_kv, H, name="prep_fwd_ctx")
    gb_c = _gates_fwd(pc_ab, avec, dvec, name="gates_fwd_ctx")
    s_zero = jnp.zeros((H, DH, DH), F32)
    _, _, ssf_c, ssb_c, tsf_c, tsb_c, s_f, s_b = _gdn_fwd(kv_c, gb_c, s_zero, s_zero, False, name="gdn_fwd_ctx")

    h_x = _norm_fwd(x, g_pre, mod_x, b_mod, name="norm_fwd_x")
    p_qkv = _mm(h_x, w_qkv, name="inproj_qkv")
    p_ab = _mm(h_x, w_ab, name="inproj_ab")
    p_rest = _mm(h_x, w_rest, out_dtype=BF16, name="inproj_rest")
    qkv = _prep_fwd(p_qkv, w_conv, 2 * H, name="prep_fwd_x")
    gb_x = _gates_fwd(p_ab, avec, dvec, name="gates_fwd_x")
    o_f, o_b, ssf, ssb, tsf, tsb, _, _ = _gdn_fwd(qkv, gb_x, s_f, s_b, True, name="gdn_fwd_x")

    w_pa, w_pb, w_out = late_weights(o_f)
    (loss_acc, dp_rest, do, dx_res, ya, yb, mg, dma, dmb, dout, dvec_post, dgon, dwsp, dbspb) = _post(
        p_rest, o_f, o_b, x, tgt, mod_x, b_mod, g_onorm, gm_ln_g, gm_ln_b, w_sp, bspb, w_pa, w_pb, w_out, g_post,
        name="post")
    g = {}
    g["w_pa"] = _mm(ya, dma, ta=True, out_dtype=BF16, name="dw_pa")
    g["w_pb"] = _mm(yb, dmb, ta=True, out_dtype=BF16, name="dw_pb")
    g["w_out"] = _mm(mg, dout, ta=True, out_dtype=BF16, name="dw_out")

    zeros_s = jnp.zeros((H, DH, DH), F32)
    dq_f, dq_b, dg_f, dg_b, ds0_f, ds0_b = _gdn_bwd(qkv, gb_x, ssf, ssb, tsf, tsb, do, zeros_s, zeros_s, True,
                                                    name="gdn_bwd_x")
    dp_qkv, dwc_x = _prep_bwd(p_qkv, w_conv, dq_f, dq_b, 2 * H, name="prep_bwd_x")
    dp_ab, dav_x, ddv_x = _gates_bwd(p_ab, avec, dvec, dg_f, dg_b, name="gates_bwd_x")
    dkv_f, dkv_b, dgc_f, dgc_b, _, _ = _gdn_bwd(kv_c, gb_c, ssf_c, ssb_c, tsf_c, tsb_c, jnp.zeros((H, rows_c, DH), F32),
                                                 ds0_f, ds0_b, False, name="gdn_bwd_ctx")
    dpc_kv, dwc_c = _prep_bwd(pc_kv, wconv_kv, dkv_f, dkv_b, H, name="prep_bwd_ctx")
    dpc_ab, dav_c, ddv_c = _gates_bwd(pc_ab, avec, dvec, dgc_f, dgc_b, name="gates_bwd_ctx")

    dw_kv_c = _mm(h_c, dpc_kv, ta=True, name="dw_kv_ctx")
    dw_qkv = _mm(h_x, dp_qkv, ta=True, tn=D, add=dw_kv_c, add_from=1, out_dtype=BF16, name="dw_qkv")
    dw_ab_c = _mm(h_c, dpc_ab, ta=True, name="dw_ab_ctx")
    dw_ab = _mm(h_x, dp_ab, ta=True, add=dw_ab_c, out_dtype=BF16, name="dw_ab")
    dw_rest = _mm(h_x, dp_rest, ta=True, out_dtype=BF16, name="dw_rest")
    g["w_in"] = (dw_qkv, dw_ab[:, :4 * H], dw_rest)
    token = on_large_gradients(g) if on_large_gradients is not None else None
    dh = _mm(dp_ab, w_ab, tb=True, after=token, name="dh_ab")
    dh = _mm(dp_qkv, w_qkv, tb=True, tk=3 * D // 2, add=dh, after=token, name="dh_qkv")
    dh = _mm(dp_rest, w_rest, tb=True, tk=3 * D // 2, add=dh, after=token, name="dh_rest")
    grad_x, dgpre_x, dm_x = _norm_bwd(x, g_pre, mod_x, b_mod, [dh], dx_res, name="norm_bwd_x")
    dh_c = _mm(dpc_ab, w_ab, tb=True, after=token, name="dhc_ab")
    dh_c = _mm(dpc_kv, w_kv, tb=True, add=dh_c, after=token, name="dhc_kv")
    _, dgpre_c, dm_c = _norm_bwd(ctx, g_pre, mod_c, b_mod, [dh_c], None, name="norm_bwd_ctx")

    dm_x = dm_x.at[:, 2 * D:].add(dvec_post[1:2])
    dmod = jnp.zeros((16, 3 * D), F32).at[0].set(dm_x[0]).at[1].set(dm_c[0])
    g["mod_factors"] = (scc[0], dm_x[0])
    g["dm_ctx"] = dm_c[0]
    g["silu_c_ctx"] = scc[1:2]
    dcc = _mm(dmod, w_mod, tb=True, b_sections=True, name="dcc")
    g["c_ctx"] = _dsilu_mul(cc[:8], dcc[:8], name="dc_ctx")[1]
    g["b_mod"] = dm_x + dm_c
    g["g_pre"] = dgpre_x + dgpre_c
    g["g_post"] = dvec_post[0:1]
    g["gm_ln_g"], g["gm_ln_b"] = dvec_post[2:3], dvec_post[3:4]
    g["g_onorm"] = dgon[0:1]
    g["w_sp"] = dwsp
    g["b_sp"] = jnp.sum(dbspb, axis=-1)
    g["w_conv"] = dwc_x.at[:, D:].add(dwc_c)
    g["a_log"] = (dav_x + dav_c)[0, :2 * H].reshape(2, H)
    g["dt_bias"] = (ddv_x + ddv_c)[0, :2 * H].reshape(2, H)
    return loss_acc[0, 0], grad_x, g


ANY = pl.BlockSpec(memory_space=pl.ANY)


def _place():
    x, y, c = lax.axis_index("x"), lax.axis_index("y"), lax.axis_index("c")
    chips = [(1 - x, y), (x, 1 - y), (1 - x, 1 - y)]
    return x, y, c, (x, y, 1 - c), chips


def _gather_shards(big, small, *, name):
    nb = len(big)

    def body(*refs):
        ins, sm_in = refs[:nb], refs[nb]
        outs, sm_out = refs[nb + 1:2 * nb + 1], refs[2 * nb + 1]
        send, recv = refs[2 * nb + 2:]
        x, y, c, sibling, chips = _place()
        mine = 2 * x + y

        def half(a, shard, hc):
            hr = big[a].shape[0] // 2
            return outs[a].at[shard, pl.ds(hc * hr, hr), :]

        def remote(k, src, dst, to):
            return pltpu.make_async_remote_copy(src_ref=src, dst_ref=dst, send_sem=send.at[k], recv_sem=recv.at[k],
                                                device_id=to, device_id_type=MESH)

        sends = []
        for a in range(nb):
            hr = big[a].shape[0] // 2
            for j, chip in enumerate(chips):
                sends.append(remote(a * 3 + j, ins[a].at[pl.ds(c * hr, hr), :], half(a, mine, c), (*chip, c)))
        for j, chip in enumerate(chips):
            sends.append(remote(nb * 3 + j, sm_in, sm_out.at[mine], (*chip, c)))
        for cp in sends:
            cp.start()
        base = nb * 3 + 3
        passed = []
        for a in range(nb):
            for j, (px, py) in enumerate(chips):
                theirs = 2 * px + py
                remote(a * 3 + j, half(a, theirs, c), half(a, theirs, c), sibling).wait_recv()
                fw = remote(base + a * 3 + j, half(a, theirs, c), half(a, theirs, c), sibling)
                fw.start()
                passed.append(fw)
        for a in range(nb):
            for j, (px, py) in enumerate(chips):
                theirs = 2 * px + py
                remote(base + a * 3 + j, half(a, theirs, 1 - c), half(a, theirs, 1 - c), sibling).wait_recv()
        for j, (px, py) in enumerate(chips):
            remote(nb * 3 + j, sm_in, sm_out.at[2 * px + py], sibling).wait_recv()
        for cp in sends + passed:
            cp.wait_send()

    n_remote = 2 * nb * 3 + 3
    outs = pl.pallas_call(
        body, name=name, in_specs=[ANY] * (nb + 1), out_specs=[ANY] * (nb + 1),
        out_shape=[jax.ShapeDtypeStruct((N_CHIPS,) + a.shape, a.dtype) for a in big + [small]],
        scratch_shapes=[pltpu.SemaphoreType.DMA((n_remote,)), pltpu.SemaphoreType.DMA((n_remote,))],
    )(*big, small)
    return outs[:nb], outs[nb]


def _gather_late_start(arrs, after, *, name):
    na = len(arrs)

    def body(*refs):
        ins, land = refs[:na], refs[na:2 * na]
        send, recv = refs[2 * na + 1], refs[2 * na + 2]
        token = refs[-1]
        x, y, c, _, chips = _place()
        for a in range(na):
            hr = arrs[a].shape[0] // 2
            for j, (px, py) in enumerate(chips):
                for other in range(2):
                    k = (a * 3 + j) * 2 + other
                    _remote(ins[a].at[pl.ds(c * hr, hr), :], land[a].at[2 * x + y, pl.ds(c * hr, hr), :],
                            send.at[k], recv.at[k], (px, py, c ^ other)).start()
        token[...] = jnp.zeros_like(token)

    lands = [((N_CHIPS,) + a.shape, a.dtype) for a in arrs]
    held = lambda a: pltpu.with_memory_space_constraint(a, pltpu.HBM)
    outs = pl.pallas_call(
        body, name=name, in_specs=[HBM] * (2 * na) + [ANY],
        out_specs=[SEM, SEM] + [HBM] * (2 * na) + [pl.BlockSpec(memory_space=pltpu.VMEM)],
        out_shape=[pltpu.SemaphoreType.DMA((6 * na,)), pltpu.SemaphoreType.DMA((6 * na,))]
        + [pltpu.HBM(a.shape, a.dtype) for a in arrs] + [pltpu.HBM(s, d) for s, d in lands]
        + [jax.ShapeDtypeStruct((8, 128), F32)],
        input_output_aliases={i: 2 + i for i in range(2 * na)},
        compiler_params=pltpu.CompilerParams(has_side_effects=pltpu.SideEffectType.DATAFLOW_SIDE_EFFECTING),
    )(*[held(a) for a in arrs], *[held(lax.empty(s, d)) for s, d in lands], after)
    return outs[0], outs[1], outs[2:2 + na], outs[2 + na:2 + 2 * na], outs[-1]


def _gather_late_wait(send, recv, thru, land, after, *, name):
    na = len(thru)

    def body(*refs):
        ins, slots = refs[:na], refs[na:2 * na]
        send_sem, recv_sem = refs[2 * na], refs[2 * na + 1]
        x, y, c, _, chips = _place()
        for a in range(na):
            hr = thru[a].shape[0] // 2
            for j, (px, py) in enumerate(chips):
                for other in range(2):
                    k = (a * 3 + j) * 2 + other
                    half = c ^ other
                    cp = _remote(ins[a].at[pl.ds(c * hr, hr), :], slots[a].at[2 * px + py, pl.ds(half * hr, hr), :],
                                 send_sem.at[k], recv_sem.at[k], (px, py, half))
                    cp.wait_send()
                    cp.wait_recv()

    outs = pl.pallas_call(
        body, name=name, in_specs=[HBM] * (2 * na) + [SEM, SEM, ANY], out_specs=[HBM] * (2 * na),
        out_shape=[pltpu.HBM(a.shape, a.dtype) for a in list(thru) + list(land)],
        input_output_aliases={i: i for i in range(2 * na)},
        compiler_params=pltpu.CompilerParams(has_side_effects=pltpu.SideEffectType.DATAFLOW_SIDE_EFFECTING),
    )(*thru, *land, send, recv, after)
    return outs[na:]


def _row_chunks(rows, row_bytes, align=8):
    n = max(1, min(rows // align, -(-rows * row_bytes // DMA_CHUNK_BYTES)))
    per = -(-(-(-rows // n)) // align) * align
    return [(r, min(per, rows - r)) for r in range(0, rows, per)]


def _remote(src, dst, send, recv, to):
    return pltpu.make_async_remote_copy(src_ref=src, dst_ref=dst, send_sem=send, recv_sem=recv, device_id=to,
                                        device_id_type=MESH)


def _exchange_halves(arrs, *, name):
    na = len(arrs)

    def body(*refs):
        ins, got = refs[:na], refs[na:2 * na]
        send, recv = refs[2 * na:]
        x, y, c, sibling, _ = _place()
        for a in range(na):
            ns, rows, cols = arrs[a].shape
            hr = rows // 2
            for s in range(ns):
                for r0, nr in _row_chunks(hr, cols * arrs[a].dtype.itemsize, 16):
                    _remote(ins[a].at[s, pl.ds((1 - c) * hr + r0, nr), :], got[a].at[s, pl.ds(r0, nr), :],
                            send.at[a], recv.at[a], sibling).start()
        for a in range(na):
            hr = arrs[a].shape[1] // 2
            _remote(ins[a].at[:, pl.ds((1 - c) * hr, hr), :], got[a], send.at[a], recv.at[a], sibling).wait()

    return pl.pallas_call(
        body, name=name, in_specs=[ANY] * na, out_specs=[ANY] * na,
        out_shape=[jax.ShapeDtypeStruct((N_CHIPS, a.shape[1] // 2, a.shape[2]), a.dtype) for a in arrs],
        scratch_shapes=[pltpu.SemaphoreType.DMA((na,)), pltpu.SemaphoreType.DMA((na,))],
    )(*arrs)


def _scatter_sections(arrs, *, name):
    na = len(arrs)

    def body(*refs):
        ins, outs = refs[:na], refs[na:2 * na]
        send, recv = refs[2 * na:]
        x, y, c, _, chips = _place()
        for a in range(na):
            _, hr, cols = arrs[a].shape
            for j, (px, py) in enumerate(chips):
                for r0, nr in _row_chunks(hr, cols * arrs[a].dtype.itemsize, 16):
                    _remote(ins[a].at[2 * px + py, pl.ds(r0, nr), :], outs[a].at[j, pl.ds(r0, nr), :],
                            send.at[a * 3 + j], recv.at[a * 3 + j], (px, py, c)).start()
        for a in range(na):
            for j, (px, py) in enumerate(chips):
                _remote(ins[a].at[2 * px + py], outs[a].at[j], send.at[a * 3 + j], recv.at[a * 3 + j], (px, py, c)).wait()

    return pl.pallas_call(
        body, name=name, in_specs=[ANY] * na, out_specs=[ANY] * na,
        out_shape=[jax.ShapeDtypeStruct((N_CHIPS - 1,) + a.shape[1:], a.dtype) for a in arrs],
        scratch_shapes=[pltpu.SemaphoreType.DMA((3 * na,)), pltpu.SemaphoreType.DMA((3 * na,))],
    )(*arrs)


HBM = pl.BlockSpec(memory_space=pltpu.HBM)
SEM = pl.BlockSpec(memory_space=pltpu.SEMAPHORE)


def _scatter_start(arrs, *, name):
    na = len(arrs)

    def body(*refs):
        ins, land = refs[:na], refs[na:2 * na]
        send, recv = refs[2 * na], refs[2 * na + 1]
        token = refs[-1]
        x, y, c, _, chips = _place()
        for a in range(na):
            _, hr, cols = arrs[a].shape
            for j, (px, py) in enumerate(chips):
                for r0, nr in _row_chunks(hr, cols * arrs[a].dtype.itemsize, 16):
                    _remote(ins[a].at[2 * px + py, pl.ds(r0, nr), :], land[a].at[j, pl.ds(r0, nr), :],
                            send.at[a * 3 + j], recv.at[a * 3 + j], (px, py, c)).start()
        token[...] = jnp.zeros_like(token)

    slots = [((N_CHIPS - 1,) + a.shape[1:], a.dtype) for a in arrs]
    held = lambda a: pltpu.with_memory_space_constraint(a, pltpu.HBM)
    outs = pl.pallas_call(
        body, name=name, in_specs=[HBM] * (2 * na),
        out_specs=[SEM, SEM] + [HBM] * (2 * na) + [pl.BlockSpec(memory_space=pltpu.VMEM)],
        out_shape=[pltpu.SemaphoreType.DMA((3 * na,)), pltpu.SemaphoreType.DMA((3 * na,))]
        + [pltpu.HBM(a.shape, a.dtype) for a in arrs] + [pltpu.HBM(s, d) for s, d in slots]
        + [jax.ShapeDtypeStruct((8, 128), F32)],
        input_output_aliases={i: 2 + i for i in range(2 * na)},
        compiler_params=pltpu.CompilerParams(has_side_effects=pltpu.SideEffectType.DATAFLOW_SIDE_EFFECTING),
    )(*[held(a) for a in arrs], *[held(lax.empty(s, d)) for s, d in slots])
    return outs[0], outs[1], outs[2:2 + na], outs[2 + na:2 + 2 * na], outs[-1]


def _scatter_wait(send, recv, thru, land, after, *, name):
    na = len(thru)

    def body(*refs):
        ins, slots = refs[:na], refs[na:2 * na]
        send_sem, recv_sem = refs[2 * na], refs[2 * na + 1]
        x, y, c, _, chips = _place()
        for a in range(na):
            for j, (px, py) in enumerate(chips):
                cp = _remote(ins[a].at[2 * px + py], slots[a].at[j], send_sem.at[a * 3 + j], recv_sem.at[a * 3 + j],
                             (px, py, c))
                cp.wait_send()
                cp.wait_recv()

    outs = pl.pallas_call(
        body, name=name, in_specs=[HBM] * (2 * na) + [SEM, SEM, ANY], out_specs=[HBM] * (2 * na),
        out_shape=[pltpu.HBM(a.shape, a.dtype) for a in list(thru) + list(land)],
        input_output_aliases={i: i for i in range(2 * na)},
        compiler_params=pltpu.CompilerParams(has_side_effects=pltpu.SideEffectType.DATAFLOW_SIDE_EFFECTING),
    )(*thru, *land, send, recv, after)
    return outs[:na], outs[na:]


def _finish_reduce(big, small, *, name):
    nb = len(big)

    def body(*refs):
        outs, sm = refs[nb + 1:2 * nb + 1], refs[2 * nb + 1]
        send, recv = refs[2 * nb + 2:]
        x, y, c, sibling, chips = _place()
        blk = lambda px, py, pc: sm.at[4 * px + 2 * py + pc]
        for a in range(nb):
            _, hr, cols = big[a].shape
            for r0, nr in _row_chunks(hr, cols * 4):
                _remote(outs[a].at[c, pl.ds(r0, nr), :], outs[a].at[c, pl.ds(r0, nr), :], send.at[a], recv.at[a],
                        sibling).start()
        first = [_remote(blk(x, y, c), blk(x, y, c), send.at[nb], recv.at[nb], sibling)]
        first += [_remote(blk(x, y, c), blk(x, y, c), send.at[nb + 1 + j], recv.at[nb + 1 + j], (*chip, c))
                  for j, chip in enumerate(chips)]
        for cp in first:
            cp.start()
        passed = []
        for j, (px, py) in enumerate(chips):
            _remote(blk(px, py, c), blk(px, py, c), send.at[nb + 1 + j], recv.at[nb + 1 + j], sibling).wait_recv()
            fw = _remote(blk(px, py, c), blk(px, py, c), send.at[nb + 4 + j], recv.at[nb + 4 + j], sibling)
            fw.start()
            passed.append(fw)
        for a in range(nb):
            _remote(outs[a].at[c], outs[a].at[1 - c], send.at[a], recv.at[a], sibling).wait()
        _remote(blk(x, y, c), blk(x, y, 1 - c), send.at[nb], recv.at[nb], sibling).wait_recv()
        for j, (px, py) in enumerate(chips):
            _remote(blk(px, py, c), blk(px, py, 1 - c), send.at[nb + 4 + j], recv.at[nb + 4 + j], sibling).wait_recv()
        for cp in first + passed:
            cp.wait_send()

    n_remote = nb + 7
    arrs = list(big) + [small]
    outs = pl.pallas_call(
        body, name=name, in_specs=[ANY] * (nb + 1), out_specs=[ANY] * (nb + 1),
        out_shape=[jax.ShapeDtypeStruct(a.shape, F32) for a in arrs],
        input_output_aliases={i: i for i in range(nb + 1)},
        scratch_shapes=[pltpu.SemaphoreType.DMA((n_remote,)), pltpu.SemaphoreType.DMA((n_remote,))],
    )(*arrs)
    return outs[:nb], outs[nb]


def _reduce_start(sectioned, where):
    got = _exchange_halves(sectioned, name="rs_exchange_halves")
    chip_sum = [_add_sibling(a, g, where, BF16, name=f"rs_add_sibling_{i}")
                for i, (a, g) in enumerate(zip(sectioned, got))]
    send, recv, thru, land, token = _scatter_start(chip_sum, name="rs_scatter_start")
    return (send, recv, thru, land), token


def _reduce_finish(handle, small, where, after):
    send, recv, thru, land = handle
    got = _exchange_halves([small], name="rs_exchange_small")
    small_sum = _add_sibling(small, got[0], where, F32, name="rs_add_sibling_small")
    small_slots = _scatter_sections([small_sum], name="rs_scatter_small")[0]
    own, slots = _scatter_wait(send, recv, thru, land, after, name="rs_scatter_wait")
    red = [_sum_chips(p, s, where, 2, 0, name=f"rs_sum_chips_{i}") for i, (p, s) in enumerate(zip(own, slots))]
    red_small = _sum_chips(small_sum, small_slots, where, 2 * N_CHIPS, 2, name="rs_sum_chips_small")
    big, sm = _finish_reduce(red, red_small, name="rs_finish")
    return big, sm.reshape(-1, sm.shape[-1])


def kernel(x, c, ctx, c_ctx, w_mod, b_mod, g_pre, g_post, w_in, w_conv, a_log, dt_bias, g_onorm, gm_ln_g, gm_ln_b, w_sp, b_sp, w_pa, w_pb, w_out, loss_target, m_c_ctx, m_w_mod, m_b_mod, m_g_pre, m_g_post, m_w_in, m_w_conv, m_a_log, m_dt_bias, m_g_onorm, m_gm_ln_g, m_gm_ln_b, m_w_sp, m_b_sp, m_w_pa, m_w_pb, m_w_out, v_c_ctx, v_w_mod, v_b_mod, v_g_pre, v_g_post, v_w_in, v_w_conv, v_a_log, v_dt_bias, v_g_onorm, v_gm_ln_g, v_gm_ln_b, v_w_sp, v_b_sp, v_w_pa, v_w_pb, v_w_out):
    names = ["c_ctx", "w_mod", "b_mod", "g_pre", "g_post", "w_in", "w_conv", "a_log", "dt_bias", "g_onorm", "gm_ln_g",
             "gm_ln_b", "w_sp", "b_sp", "w_pa", "w_pb", "w_out"]
    w = dict(zip(names, (c_ctx, w_mod, b_mod, g_pre, g_post, w_in, w_conv, a_log, dt_bias, g_onorm, gm_ln_g, gm_ln_b,
                         w_sp, b_sp, w_pa, w_pb, w_out)))
    m = dict(zip(names, (m_c_ctx, m_w_mod, m_b_mod, m_g_pre, m_g_post, m_w_in, m_w_conv, m_a_log, m_dt_bias, m_g_onorm,
                         m_gm_ln_g, m_gm_ln_b, m_w_sp, m_b_sp, m_w_pa, m_w_pb, m_w_out)))
    v = dict(zip(names, (v_c_ctx, v_w_mod, v_b_mod, v_g_pre, v_g_post, v_w_in, v_w_conv, v_a_log, v_dt_bias, v_g_onorm,
                         v_gm_ln_g, v_gm_ln_b, v_w_sp, v_b_sp, v_w_pa, v_w_pb, v_w_out)))
    xy = 2 * lax.axis_index("x") + lax.axis_index("y")
    where = jnp.stack([lax.axis_index("c"), xy, 2 * xy + lax.axis_index("c")]).astype(jnp.int32)

    shards = [a[0].astype(BF16) for a in (w_mod, w_in)]
    gathered, wconv_all = _gather_shards(shards, w_conv[0], name="gather_weights")
    own = lambda full, shard: lax.dynamic_update_slice(full, shard[None], (xy, 0, 0))
    wm_all, win_all = [own(f, s) for f, s in zip(gathered, shards)]
    wconv_all = own(wconv_all, w_conv[0])
    late_shards = [a[0].astype(BF16) for a in (w_pa, w_pb, w_out)]
    late = _gather_late_start(late_shards, wconv_all, name="gather_late_start")

    def late_weights(after):
        lands = _gather_late_wait(*late[:4], after, name="gather_late_wait")
        return [own(f, s).reshape(D, D) for f, s in zip(lands, late_shards)]
    w_conv_f = jnp.concatenate([wconv_all[s] for s in range(N_CHIPS)], axis=1)
    shard_cols = IN_COLS // N_CHIPS
    cut = OFF_A - shard_cols
    assert 0 < cut and cut + 4 * H < shard_cols
    w_qkv = jnp.concatenate([win_all[0], win_all[1][:, :cut]], axis=1)
    w_ab = jnp.pad(win_all[1][:, cut:cut + 4 * H], ((0, 0), (0, DH - 4 * H)))
    w_rest = jnp.concatenate([win_all[1][:, cut + 4 * H:], win_all[2], win_all[3]], axis=1)

    blk = D // N_CHIPS
    rest_cut = shard_cols - cut - 4 * H
    big_names = ("w_in", "w_pa", "w_pb", "w_out")
    in_flight = []

    def start_reduce(grads):
        dw_qkv, dw_ab, dw_rest = grads["w_in"]
        g_win = jnp.stack([dw_qkv[:, :shard_cols],
                           jnp.concatenate([dw_qkv[:, shard_cols:], dw_ab, dw_rest[:, :rest_cut]], axis=1),
                           dw_rest[:, rest_cut:rest_cut + shard_cols], dw_rest[:, rest_cut + shard_cols:]]).astype(BF16)
        sectioned = [g_win] + [grads[k].reshape(N_CHIPS, blk, D) for k in ("w_pa", "w_pb", "w_out")]
        handle, token = _reduce_start(sectioned, where)
        in_flight.append(handle)
        return token

    loss_local, grad_x, g = _local_step(
        x[0], c, ctx[0], c_ctx, loss_target[0], wm_all, b_mod, g_pre, g_post, w_qkv, w_ab, w_rest,
        w_conv_f, a_log[0], dt_bias[0], g_onorm, gm_ln_g, gm_ln_b, w_sp[0], b_sp[0],
        late_weights, on_large_gradients=start_reduce, run_after=late[4])
    g["loss"] = loss_local
    g["_pad"] = jnp.zeros((SMALL_LAYOUT["_pad"][1],), F32)
    g["mod_factors"] = lax.dynamic_update_slice(jnp.zeros((2 * N_CHIPS, 4 * D), F32),
                                                jnp.concatenate(g["mod_factors"])[None], (where[2], 0))
    tail = jnp.zeros((N_CHIPS * SMALL_ROWS * 128 - sum(s for _, s in SMALL_LAYOUT.values()),), F32)
    flat = jnp.concatenate([g[k].reshape(-1) for k in SMALL_LAYOUT] + [tail])
    reduced, gr_small = _reduce_finish(in_flight[0], flat.reshape(N_CHIPS, SMALL_ROWS, 128), where, g["b_mod"])
    gr_tiny = gr_small[TINY_ROW0:]

    def entry(arr, k):
        off, size = SMALL_LAYOUT[k]
        row, col = off // 128 - TINY_ROW0, off % 128
        return arr[row:row + size // 128].reshape(-1) if size >= 128 else arr[row, col:col + size]

    factors = entry(gr_tiny, "mod_factors").reshape(2 * N_CHIPS, 4 * D)
    pad_rows = lambda a: jnp.pad(a, ((0, 16 - a.shape[0]), (0, 0)))
    lhs = pad_rows(jnp.concatenate([factors[:, :D], g["silu_c_ctx"]], axis=0))
    rhs = pad_rows(jnp.concatenate([factors[:, D:], entry(gr_tiny, "dm_ctx")[None]], axis=0))
    mod_cols = 3 * D // N_CHIPS
    gr_wm = _mm(lhs, lax.dynamic_slice(rhs, (0, xy * mod_cols), (16, mod_cols)), ta=True, name="dw_mod")

    res = {k: _adamw(w[k], gr, m[k], v[k], name=f"adamw_{k}") for k, gr in zip(big_names[1:], reduced[1:])}
    flip = lambda a: jnp.swapaxes(a, 1, 2)
    res["w_in"] = [flip(r) for r in _adamw(flip(w_in), flip(reduced[0].reshape(w_in.shape)), flip(m_w_in), flip(v_w_in),
                                           name="adamw_w_in")]
    res["w_mod"] = _adamw(w_mod[0], gr_wm, m_w_mod[0], v_w_mod[0], name="adamw_w_mod")
    res["w_sp"] = _adamw(w_sp.reshape(-1, 128), gr_small, m_w_sp.reshape(-1, 128), v_w_sp.reshape(-1, 128),
                         name="adamw_w_sp")
    tiny = [k for k, (off, _) in SMALL_LAYOUT.items() if TINY_ROW0 <= off // 128 < TINY_ROW0 + TINY_ROWS]
    tiny_names = [k for k in tiny if k not in ("loss", "_pad")]

    def pack(src):
        parts = [src[k].reshape(-1) if k in tiny_names else jnp.zeros((SMALL_LAYOUT[k][1],), F32) for k in tiny]
        return jnp.concatenate(parts).reshape(TINY_ROWS, 128)

    tiny_res = _adamw(pack(w), gr_tiny[:TINY_ROWS], pack(m), pack(v), name="adamw_tiny")
    for k in tiny_names:
        res[k] = [entry(r, k) for r in tiny_res]
    g_conv = lax.dynamic_slice(entry(gr_tiny, "w_conv").reshape(3, 3 * D), (0, xy * (3 * D // N_CHIPS)),
                               (3, 3 * D // N_CHIPS))
    conv_res = _adamw(jnp.pad(w_conv[0], ((0, 5), (0, 0))), jnp.pad(g_conv, ((0, 5), (0, 0))),
                      jnp.pad(m_w_conv[0], ((0, 5), (0, 0))), jnp.pad(v_w_conv[0], ((0, 5), (0, 0))), name="adamw_w_conv")
    res["w_conv"] = [r[:3] for r in conv_res]
    res = {k: [r.reshape(w[k].shape) for r in res[k]] for k in names}

    out = [entry(gr_tiny, "loss").reshape(()), grad_x[None]]
    for i in range(4):
        out += [res[k][i] for k in names]
    return tuple(out)
```

```python
import functools

import jax
import jax.numpy as jnp
from jax import lax
from jax.experimental import pallas as pl
from jax.experimental.pallas import tpu as pltpu

F32 = jnp.float32
BF16 = jnp.bfloat16
HI = lax.Precision.HIGHEST
MESH = pl.DeviceIdType.MESH

D = 1024
H = 8
DH = 128
CH = 64
LOG_CH = 6
PAIR = 2 * CH
GM = 128
assert 1 << LOG_CH == CH and PAIR == DH
PREC_POWERS = ((lax.Precision.HIGH, lax.Precision.HIGH),) * 3 + ((None, None),) * 2
assert len(PREC_POWERS) == LOG_CH - 1
EPS = 1e-6
N_CHIPS = 4
OFF_A = 3 * D
OFF_ZB = OFF_A + 4 * H
IN_COLS = OFF_ZB + 6 * D
VMEM_LIMIT_V7X = 56 * 1024 * 1024
DMA_CHUNK_BYTES = 2 * 1024 * 1024

ADAM_LR, ADAM_B1, ADAM_B2, ADAM_EPS, ADAM_WD, ADAM_STEP = 0.001, 0.9, 0.999, 1e-08, 0.01, 10

SMALL_LAYOUT = {}
_off = 0
for _n, _s in (("w_sp", H * GM * GM), ("c_ctx", D), ("b_mod", 3 * D), ("g_pre", D), ("g_post", D), ("gm_ln_g", D),
               ("gm_ln_b", D), ("b_sp", H * GM), ("g_onorm", DH), ("a_log", 2 * H), ("dt_bias", 2 * H), ("loss", 1),
               ("_pad", 128 - 4 * H - 1 + 6 * 128), ("w_conv", 3 * 3 * D), ("dm_ctx", 3 * D), ("mod_factors", 8 * 4 * D)):
    SMALL_LAYOUT[_n] = (_off, _s)
    _off += _s
SMALL_ROWS = 368
assert N_CHIPS * SMALL_ROWS * 128 >= _off and (SMALL_ROWS // 2) % 8 == 0 and _off % 128 == 0
TINY_ROW0 = SMALL_LAYOUT["c_ctx"][0] // 128
TINY_ROWS = SMALL_LAYOUT["w_conv"][0] // 128 - TINY_ROW0
assert TINY_ROWS % 8 == 0


def _params(sem=None):
    return pltpu.CompilerParams(dimension_semantics=sem, vmem_limit_bytes=VMEM_LIMIT_V7X)


def _tile(n, cands=(256, 128, 64, 32, 16, 8)):
    if n <= cands[0]:
        return n
    for cand in cands:
        if n % cand == 0 and cand >= 64:
            return cand
    return max(d for d in range(8, cands[0], 8) if n % d == 0)


def _silu(x):
    return x * jax.nn.sigmoid(x)


def _gelu(x):
    return 0.5 * x * (1.0 + jnp.tanh(0.7978845608028654 * (x + 0.044715 * (x * x * x))))


def _mm(a, b, *, ta=False, tb=False, out_dtype=F32, tm=1024, tn=1024, tk=1024, add=None, add_from=0,
        b_sections=False, out_sections=False, after=None, name):
    m, k = (a.shape[1], a.shape[0]) if ta else a.shape
    if b_sections:
        sect = b.shape[2]
        n = b.shape[1] if tb else b.shape[0] * sect
        tn, tk = (tn, sect) if tb else (sect, tk)
    else:
        n = b.shape[0] if tb else b.shape[1]
    tm, tn, tk = min(tm, m), min(tn, n), min(tk, k)
    assert m % tm == 0 and n % tn == 0 and k % tk == 0, (name, m, n, k, tm, tn, tk)
    nk = k // tk
    dims = (((0,) if ta else (1,), (1,) if tb else (0,)), ((), ()))
    has_add = add is not None
    assert not has_add or add.shape == (m, n - add_from * tn), (name, add.shape)

    def body(*refs):
        a_ref, b_ref = refs[:2]
        o_ref = refs[2 + int(has_add) + int(after is not None)]
        acc_ref = refs[-1]
        kk = pl.program_id(2)
        bv = b_ref[0] if b_sections else b_ref[...]
        part = lax.dot_general(a_ref[...].astype(BF16), bv.astype(BF16), dims, preferred_element_type=F32)

        def finish(res):
            if has_add:
                res = res + jnp.where(pl.program_id(1) >= add_from, refs[2][...], 0.0)
            if out_sections:
                o_ref[0] = res.astype(out_dtype)
            else:
                o_ref[...] = res.astype(out_dtype)

        if nk == 1:
            finish(part)
            return

        @pl.when(kk == 0)
        def _():
            acc_ref[...] = part

        @pl.when((kk > 0) & (kk < nk - 1))
        def _():
            acc_ref[...] += part

        @pl.when(kk == nk - 1)
        def _():
            finish(acc_ref[...] + part)

    a_spec = pl.BlockSpec((tk, tm), lambda i, j, q: (q, i)) if ta else pl.BlockSpec((tm, tk), lambda i, j, q: (i, q))
    if b_sections:
        b_spec = (pl.BlockSpec((1, tn, tk), lambda i, j, q: (q, j, 0)) if tb
                  else pl.BlockSpec((1, tk, tn), lambda i, j, q: (j, q, 0)))
    else:
        b_spec = pl.BlockSpec((tn, tk), lambda i, j, q: (j, q)) if tb else pl.BlockSpec((tk, tn), lambda i, j, q: (q, j))
    add_spec = [pl.BlockSpec((tm, tn), lambda i, j, q: (i, jnp.maximum(j - add_from, 0)))] if has_add else []
    if out_sections:
        out_spec, out_shape = pl.BlockSpec((1, tm, tn), lambda i, j, q: (j, i, 0)), (n // tn, m, tn)
    else:
        out_spec, out_shape = pl.BlockSpec((tm, tn), lambda i, j, q: (i, j)), (m, n)
    return pl.pallas_call(
        body, name=name, grid=(m // tm, n // tn, nk),
        in_specs=[a_spec, b_spec] + add_spec + ([pl.BlockSpec(memory_space=pl.ANY)] if after is not None else []),
        out_specs=out_spec, out_shape=jax.ShapeDtypeStruct(out_shape, out_dtype),
        scratch_shapes=[pltpu.VMEM((tm, tn), F32)] if nk > 1 else [],
        compiler_params=_params(("parallel", "parallel", "arbitrary")),
    )(*([a, b] + ([add] if has_add else []) + ([after] if after is not None else [])))


def _h_fn(x, g, m):
    shift, scale = m[:, 0:D], m[:, D:2 * D]
    r = lax.rsqrt(jnp.mean(x * x, axis=-1, keepdims=True) + EPS)
    return (x * r * g) * (1.0 + scale) + shift


def _norm_fwd(x, g, mod, bmod, *, name):
    rows = x.shape[0]
    t = min(512, rows)

    def body(x_ref, g_ref, m_ref, b_ref, h_ref):
        h_ref[...] = _h_fn(x_ref[...], g_ref[...], m_ref[...] + b_ref[...]).astype(BF16)

    vec = lambda w: pl.BlockSpec((1, w), lambda i: (0, 0))
    return pl.pallas_call(
        body, name=name, grid=(rows // t,),
        in_specs=[pl.BlockSpec((t, D), lambda i: (i, 0)), vec(D), vec(3 * D), vec(3 * D)],
        out_specs=pl.BlockSpec((t, D), lambda i: (i, 0)),
        out_shape=jax.ShapeDtypeStruct((rows, D), BF16),
        compiler_params=_params(("parallel",)),
    )(x, g, mod, bmod)


def _norm_bwd(x, g, mod, bmod, dh_parts, resid, *, name):
    rows = x.shape[0]
    t = min(512, rows)
    n_parts = len(dh_parts)
    has_resid = resid is not None

    def body(*refs):
        x_ref, g_ref, m_ref, b_ref = refs[:4]
        parts = refs[4:4 + n_parts]
        r_ref = refs[4 + n_parts] if has_resid else None
        dx_ref, dg_ref, dm_ref = refs[-3:]
        i = pl.program_id(0)
        dh = parts[0][...]
        for p in parts[1:]:
            dh = dh + p[...]
        _, vjp = jax.vjp(_h_fn, x_ref[...], g_ref[...], m_ref[...] + b_ref[...])
        dx, dg, dm = vjp(dh)
        if has_resid:
            dx = dx + r_ref[...]
        dx_ref[...] = dx

        @pl.when(i == 0)
        def _():
            dg_ref[...] = dg
            dm_ref[...] = dm

        @pl.when(i > 0)
        def _():
            dg_ref[...] += dg
            dm_ref[...] += dm

    vec = lambda w: pl.BlockSpec((1, w), lambda i: (0, 0))
    tile = pl.BlockSpec((t, D), lambda i: (i, 0))
    ins = [x, g, mod, bmod, *dh_parts] + ([resid] if has_resid else [])
    return pl.pallas_call(
        body, name=name, grid=(rows // t,),
        in_specs=[tile, vec(D), vec(3 * D), vec(3 * D)] + [tile] * (n_parts + int(has_resid)),
        out_specs=[tile, vec(D), vec(3 * D)],
        out_shape=[jax.ShapeDtypeStruct((rows, D), F32), jax.ShapeDtypeStruct((1, D), F32),
                   jax.ShapeDtypeStruct((1, 3 * D), F32)],
        compiler_params=_params(("arbitrary",)),
    )(*ins)


def _conv_tile(u_ref, r0, t, rows, w0, w1, w2):
    u = u_ref[pl.ds(r0, t), :]
    prev8 = u_ref[pl.ds(pl.multiple_of(jnp.maximum(r0 - 8, 0), 8), 8), :]
    next8 = u_ref[pl.ds(pl.multiple_of(jnp.minimum(r0 + t, rows - 8), 8), 8), :]
    r8 = lax.broadcasted_iota(jnp.int32, (8, DH), 0)
    prev_row = jnp.sum(jnp.where(r8 == 7, prev8, 0.0), axis=0, keepdims=True)
    next_row = jnp.sum(jnp.where(r8 == 0, next8, 0.0), axis=0, keepdims=True)
    prev_row = jnp.where(r0 > 0, prev_row, 0.0)
    next_row = jnp.where(r0 + t < rows, next_row, 0.0)
    ri = lax.broadcasted_iota(jnp.int32, (t, DH), 0)
    um1 = jnp.where(ri == 0, prev_row, pltpu.roll(u, 1, 0))
    up1 = jnp.where(ri == t - 1, next_row, pltpu.roll(u, t - 1, 0))
    return w0 * um1 + w1 * u + w2 * up1, um1, u, up1


def _rowlocal(z, norm):
    y = _silu(z)
    return y * lax.rsqrt(jnp.sum(y * y, axis=-1, keepdims=True) + EPS) if norm else y


def _prep_fwd(p, wconv, n_norm, *, name):
    rows, nb = p.shape[0], p.shape[1] // DH
    t = min(512, rows)
    out = None
    for norm, b0, b1 in ((True, 0, n_norm), (False, n_norm, nb)):
        def body(u_ref, w_ref, *rest, norm=norm):
            o_ref = rest[-1]
            w0, w1, w2 = w_ref[0:1, :], w_ref[1:2, :], w_ref[2:3, :]

            def step(s, carry):
                r0 = pl.multiple_of(s * t, t)
                z, _, _, _ = _conv_tile(u_ref, r0, t, rows, w0, w1, w2)
                o_ref[0, pl.ds(r0, t), :] = _rowlocal(z, norm)
                return carry

            lax.fori_loop(0, rows // t, step, 0)

        held = [] if out is None else [out]
        out = pl.pallas_call(
            body, name=f"{name}_{'norm' if norm else 'plain'}", grid=(b1 - b0,),
            in_specs=[pl.BlockSpec((rows, DH), lambda j, b0=b0: (0, b0 + j)),
                      pl.BlockSpec((3, DH), lambda j, b0=b0: (0, b0 + j))] + [ANY] * len(held),
            out_specs=pl.BlockSpec((1, rows, DH), lambda j, b0=b0: (b0 + j, 0, 0)),
            out_shape=jax.ShapeDtypeStruct((nb, rows, DH), F32),
            input_output_aliases={2: 0} if held else {},
            compiler_params=_params(("parallel",)),
        )(p, wconv, *held)
    return out


def _prep_bwd(p, wconv, d_a, d_b, n_norm, *, name):
    rows, nb = p.shape[0], p.shape[1] // DH
    t = min(512, rows)
    outs = []
    for norm, b0, b1 in ((True, 0, n_norm), (False, n_norm, nb)):
        def body(u_ref, w_ref, da_ref, db_ref, *rest, norm=norm):
            du_ref, dw_ref, dz_ref = rest[-3:]
            w0, w1, w2 = w_ref[0:1, :], w_ref[1:2, :], w_ref[2:3, :]

            def step1(s, carry):
                a0, a1, a2 = carry
                r0 = pl.multiple_of(s * t, t)
                z, um1, u, up1 = _conv_tile(u_ref, r0, t, rows, w0, w1, w2)
                _, vjp = jax.vjp(lambda zz: _rowlocal(zz, norm), z)
                (dz,) = vjp(da_ref[0, pl.ds(r0, t), :] + db_ref[0, pl.ds(r0, t), :])
                dz_ref[pl.ds(r0, t), :] = dz
                red = lambda v: jnp.sum(v, axis=0, keepdims=True)
                return a0 + red(dz * um1), a1 + red(dz * u), a2 + red(dz * up1)

            zero = jnp.zeros((1, DH), F32)
            a0, a1, a2 = lax.fori_loop(0, rows // t, step1, (zero, zero, zero))
            dw_ref[0:1, :] = a0
            dw_ref[1:2, :] = a1
            dw_ref[2:3, :] = a2

            def step2(s, carry):
                r0 = pl.multiple_of(s * t, t)
                du, _, _, _ = _conv_tile(dz_ref, r0, t, rows, w2, w1, w0)
                du_ref[pl.ds(r0, t), :] = du.astype(BF16)
                return carry

            lax.fori_loop(0, rows // t, step2, 0)

        col = pl.BlockSpec((rows, DH), lambda j, b0=b0: (0, b0 + j))
        w_spec = pl.BlockSpec((3, DH), lambda j, b0=b0: (0, b0 + j))
        d_spec = pl.BlockSpec((1, rows, DH), lambda j, b0=b0: (b0 + j, 0, 0))
        outs = pl.pallas_call(
            body, name=f"{name}_{'norm' if norm else 'plain'}", grid=(b1 - b0,),
            in_specs=[col, w_spec, d_spec, d_spec] + [ANY] * len(outs), out_specs=[col, w_spec],
            out_shape=[jax.ShapeDtypeStruct((rows, nb * DH), BF16), jax.ShapeDtypeStruct((3, nb * DH), F32)],
            input_output_aliases={4: 0, 5: 1} if outs else {},
            scratch_shapes=[pltpu.VMEM((rows, DH), F32)],
            compiler_params=_params(("parallel",)),
        )(p, wconv, d_a, d_b, *outs)
    return outs


def _gates_fn(pab, avec, dvec):
    t = pab.shape[0]
    lane = lax.broadcasted_iota(jnp.int32, pab.shape, 1)
    xg = pab + dvec
    sp = jnp.maximum(xg, 0.0) + jnp.log(1.0 + jnp.exp(-jnp.abs(xg)))
    g = jnp.where(lane < 2 * H, -jnp.exp(avec) * sp, 0.0)
    ii = lax.broadcasted_iota(jnp.int32, (t, t), 0)
    jj = lax.broadcasted_iota(jnp.int32, (t, t), 1)
    same = (ii >> LOG_CH) == (jj >> LOG_CH)
    cum_f = _dot(jnp.where(same & (jj <= ii), 1.0, 0.0), g, precision=HI)
    cum_r = _dot(jnp.where(same & (jj >= ii), 1.0, 0.0), g, precision=HI)
    return jnp.where(lane < H, cum_f, jnp.where(lane < 2 * H, cum_r, jnp.where(lane < 4 * H, jax.nn.sigmoid(pab), 0.0)))


def _gates_fwd(pab, avec, dvec, *, name):
    rows = pab.shape[0]
    t = min(2 * PAIR, rows)

    def body(p_ref, a_ref, d_ref, o_ref):
        o_ref[...] = _gates_fn(p_ref[...], a_ref[...], d_ref[...])

    vec = pl.BlockSpec((1, DH), lambda i: (0, 0))
    tile = pl.BlockSpec((t, DH), lambda i: (i, 0))
    return pl.pallas_call(
        body, name=name, grid=(rows // t,), in_specs=[tile, vec, vec], out_specs=tile,
        out_shape=jax.ShapeDtypeStruct((rows, DH), F32), compiler_params=_params(("parallel",)),
    )(pab, avec, dvec)


def _gates_bwd(pab, avec, dvec, d_a, d_b, *, name):
    rows = pab.shape[0]
    t = min(2 * PAIR, rows)

    def body(p_ref, a_ref, d_ref, da_ref, db_ref, dp_ref, dav_ref, ddv_ref):
        i = pl.program_id(0)
        _, vjp = jax.vjp(_gates_fn, p_ref[...], a_ref[...], d_ref[...])
        dp, dav, ddv = vjp(da_ref[...] + db_ref[...])
        dp_ref[...] = dp.astype(BF16)

        @pl.when(i == 0)
        def _():
            dav_ref[...] = dav
            ddv_ref[...] = ddv

        @pl.when(i > 0)
        def _():
            dav_ref[...] += dav
            ddv_ref[...] += ddv

    vec = pl.BlockSpec((1, DH), lambda i: (0, 0))
    tile = pl.BlockSpec((t, DH), lambda i: (i, 0))
    return pl.pallas_call(
        body, name=name, grid=(rows // t,), in_specs=[tile, vec, vec, tile, tile], out_specs=[tile, vec, vec],
        out_shape=[jax.ShapeDtypeStruct((rows, DH), BF16), jax.ShapeDtypeStruct((1, DH), F32),
                   jax.ShapeDtypeStruct((1, DH), F32)],
        compiler_params=_params(("arbitrary",)),
    )(pab, avec, dvec, d_a, d_b)


def _dot_general(a, b, dims, precision=None):
    return lax.dot_general(a, b, (dims, ((), ())), precision=precision, preferred_element_type=F32)


@functools.partial(jax.custom_vjp, nondiff_argnums=(2,))
def _dot_bf16(a, b, dims):
    return _dot_general(a.astype(BF16), b.astype(BF16), dims)


def _dot_bf16_fwd(a, b, dims):
    return _dot_bf16(a, b, dims), (a, b)


def _dot_bf16_bwd(dims, res, g):
    a, b = res
    (ca,), (cb,) = dims
    da = _dot_bf16(g, b, ((1,), (1 - cb,))) if ca == 1 else _dot_bf16(b, g, ((1 - cb,), (1,)))
    db = _dot_bf16(a, g, ((1 - ca,), (0,))) if cb == 0 else _dot_bf16(g, a, ((0,), (1 - ca,)))
    return da, db


_dot_bf16.defvjp(_dot_bf16_fwd, _dot_bf16_bwd)


def _dot(a, b, dims=((1,), (0,)), precision=None):
    if precision is None and a.dtype == F32 and b.dtype == F32:
        return _dot_bf16(a, b, dims)
    return _dot_general(a, b, dims, precision)


_NT = ((1,), (1,))
_TN = ((0,), (0,))


@jax.custom_vjp
def _saved_inverse(neg_a, inv):
    return inv


def _saved_inverse_fwd(neg_a, inv):
    return inv, inv


def _saved_inverse_bwd(inv, d_inv):
    idx = range(len(inv))
    left = [_dot(inv[i], d_inv[i], _TN) for i in idx]
    d_neg_a = [_dot(left[i], inv[i], _NT) for i in idx]
    return d_neg_a, [jnp.zeros_like(t) for t in inv]


_saved_inverse.defvjp(_saved_inverse_fwd, _saved_inverse_bwd)


def _pairs(s, q, k, v, gcol, bcol, revs, inv_saved=None):
    idx = range(len(revs))
    ii = lax.broadcasted_iota(jnp.int32, (PAIR, PAIR), 0)
    jj = lax.broadcasted_iota(jnp.int32, (PAIR, PAIR), 1)
    same = (ii >> LOG_CH) == (jj >> LOG_CH)
    incl_d = (same & (ii >= jj), same & (ii <= jj))
    strict_d = (same & (ii > jj), same & (ii < jj))
    incl = [incl_d[int(r)] for r in revs]
    strict = [strict_d[int(r)] for r in revs]
    eye = jnp.where(ii == jj, 1.0, 0.0)
    gc_i = [jnp.broadcast_to(gcol[i], (PAIR, DH)) for i in idx]
    gc_j = [gc_i[i].T for i in idx]
    decay = [jnp.where(incl[i], jnp.exp(jnp.where(incl[i], gc_i[i] - gc_j[i], 0.0)), 0.0) for i in idx]
    b_b = [jnp.broadcast_to(bcol[i], (PAIR, DH)) for i in idx]
    kb = [k[i] * b_b[i] for i in idx]
    kk = [_dot(kb[i], k[i], _NT) for i in idx]
    bp = [jnp.where(strict[i], -kk[i] * decay[i], 0.0) for i in idx]
    if inv_saved is not None:
        inv = _saved_inverse(bp, inv_saved)
    else:
        low = bp
        for prec_sq, prec_acc in PREC_POWERS:
            bp = [_dot(bp[i], bp[i], precision=prec_sq) for i in idx]
            more = [_dot(low[i], bp[i], precision=prec_acc) for i in idx]
            low = [low[i] + bp[i] + more[i] for i in idx]
        inv = [eye + low[i] for i in idx]
    eg = [jnp.exp(gc_i[i]) for i in idx]
    sol = [_dot(inv[i], jnp.concatenate([v[i] * b_b[i], kb[i] * eg[i]], axis=1)) for i in idx]
    u_val = [sol[i][:, :DH] for i in idx]
    w_key = [sol[i][:, DH:] for i in idx]
    row = lax.broadcasted_iota(jnp.int32, (PAIR, 1), 0)
    has_q = q[0] is not None
    if has_q:
        qc = [q[i] * (DH ** -0.5) for i in idx]
        qk = [_dot(qc[i], k[i], _NT) for i in idx]
        attn = [qk[i] * decay[i] for i in idx]
        qd = [qc[i] * eg[i] for i in idx]
    outs = [[None, None] for _ in idx]
    zeros = jnp.zeros((CH, DH), F32)
    for step in range(2):
        cidx = [(1 - step) if revs[i] else step for i in idx]
        sl = [slice(c * CH, (c + 1) * CH) for c in cidx]
        last = [c * CH if revs[i] else c * CH + CH - 1 for i, c in zip(idx, cidx)]
        gl = [jnp.sum(jnp.where(row == last[i], gcol[i], 0.0), axis=0, keepdims=True) for i in idx]
        k_tail = [k[i][sl[i]] * jnp.exp(gl[i] - gc_i[i][sl[i]]) for i in idx]
        ws = [_dot(w_key[i][sl[i]], s[i]) for i in idx]
        v_new = [u_val[i][sl[i]] - ws[i] for i in idx]
        if has_q:
            v_pad = [jnp.concatenate([v_new[i], zeros] if cidx[i] == 0 else [zeros, v_new[i]], axis=0) for i in idx]
            o_state = [_dot(qd[i][sl[i]], s[i]) for i in idx]
            o_local = [_dot(attn[i][sl[i]], v_pad[i]) for i in idx]
            for i in idx:
                outs[i][cidx[i]] = o_state[i] + o_local[i]
        kv = [_dot(k_tail[i], v_new[i], _TN) for i in idx]
        s = [s[i] * jnp.exp(gl[i]) + kv[i] for i in idx]
    return s, ([jnp.concatenate(outs[i], axis=0) for i in idx] if has_q else None), inv


def _lane_col(tile, idx):
    lane = lax.broadcasted_iota(jnp.int32, tile.shape, 1)
    return jnp.sum(jnp.where(lane == idx, tile, 0.0), axis=1, keepdims=True)


def _gdn_fwd(qkv, gb, s0f, s0b, has_q, *, name):
    nb, rows, _ = qkv.shape
    n = rows // PAIR
    qoff = H if has_q else 0

    def body(qf_ref, qb_ref, gf_ref, gr_ref, s0f_ref, s0b_ref, of_ref, ob_ref, ssf_ref, ssb_ref, tsf_ref, tsb_ref,
             sf_ref, sb_ref):
        @pl.when(pl.program_id(0) == 0)
        def _():
            sf_ref[...] = s0f_ref[...]
            sb_ref[...] = s0b_ref[...]

        gtiles = (gf_ref[...], gr_ref[...])

        dirs = ((qf_ref, sf_ref, ssf_ref, of_ref), (qb_ref, sb_ref, ssb_ref, ob_ref))
        ts_refs = (tsf_ref, tsb_ref)

        work = [(h, d) for h in range(H) for d in range(2)]
        loaded = []
        for h, d in work:
            q_ref, s_ref, _, _ = dirs[d]
            loaded.append((s_ref[h], q_ref[h] if has_q else None, q_ref[qoff + h], q_ref[qoff + H + h],
                           _lane_col(gtiles[d], d * H + h), _lane_col(gtiles[d], 2 * H + d * H + h)))
        s_new, o, inv = _pairs(*[list(col) for col in zip(*loaded)], revs=[d == 1 for _, d in work])
        for i, (h, d) in enumerate(work):
            _, s_ref, ss_ref, o_ref = dirs[d]
            ss_ref[0, h] = loaded[i][0]
            ts_refs[d][0, h] = inv[i]
            s_ref[h] = s_new[i]
            o_ref[h] = o[i] if has_q else jnp.zeros((PAIR, DH), F32)

    fwd3 = lambda i: (0, i, 0)
    rev3 = lambda i: (0, n - 1 - i, 0)
    state = pl.BlockSpec((H, DH, DH), lambda i: (0, 0, 0))
    saved = pl.BlockSpec((1, H, DH, DH), lambda i: (i, 0, 0, 0))
    return pl.pallas_call(
        body, name=name, grid=(n,),
        in_specs=[pl.BlockSpec((nb, PAIR, DH), fwd3), pl.BlockSpec((nb, PAIR, DH), rev3),
                  pl.BlockSpec((PAIR, DH), lambda i: (i, 0)), pl.BlockSpec((PAIR, DH), lambda i: (n - 1 - i, 0)),
                  state, state],
        out_specs=[pl.BlockSpec((H, PAIR, DH), fwd3), pl.BlockSpec((H, PAIR, DH), rev3), saved, saved, saved, saved,
                   state, state],
        out_shape=[jax.ShapeDtypeStruct((H, rows, DH), F32)] * 2 + [jax.ShapeDtypeStruct((n, H, DH, DH), F32)] * 4
        + [jax.ShapeDtypeStruct((H, DH, DH), F32)] * 2,
        compiler_params=_params(("arbitrary",)),
    )(qkv, qkv, gb, gb, s0f, s0b)


def _gdn_bwd(qkv, gb, ssf, ssb, tsf, tsb, do, dsf, dsb, has_q, *, name):
    nb, rows, _ = qkv.shape
    n = rows // PAIR
    qoff = H if has_q else 0

    def body(qf_ref, qb_ref, gf_ref, gr_ref, ssf_ref, ssb_ref, tsf_ref, tsb_ref, dof_ref, dob_ref, dsf0_ref, dsb0_ref,
             dqf_ref, dqb_ref, dgf_ref, dgr_ref, dsf_ref, dsb_ref):
        ts_refs = (tsf_ref, tsb_ref)
        @pl.when(pl.program_id(0) == 0)
        def _():
            dsf_ref[...] = dsf0_ref[...]
            dsb_ref[...] = dsb0_ref[...]

        gtiles = (gf_ref[...], gr_ref[...])
        lane = lax.broadcasted_iota(jnp.int32, (PAIR, DH), 1)

        dirs = ((qf_ref, ssf_ref, dof_ref, dsf_ref, dqf_ref), (qb_ref, ssb_ref, dob_ref, dsb_ref, dqb_ref))

        work = [(h, d) for h in range(H) for d in range(2)]
        revs = [d == 1 for _, d in work]
        s_in, q_in, k_in, v_in, g_in, b_in, ds_out, do_out, inv_in = [], [], [], [], [], [], [], [], []
        for h, d in work:
            q_ref, ss_ref, do_ref, ds_ref, _ = dirs[d]
            s_in.append(ss_ref[0, h])
            inv_in.append(ts_refs[d][0, h])
            q_in.append(q_ref[h] if has_q else None)
            k_in.append(q_ref[qoff + h])
            v_in.append(q_ref[qoff + H + h])
            g_in.append(_lane_col(gtiles[d], d * H + h))
            b_in.append(_lane_col(gtiles[d], 2 * H + d * H + h))
            ds_out.append(ds_ref[h])
            do_out.append(do_ref[h] if has_q else None)
        if has_q:
            _, vjp = jax.vjp(lambda s_, q_, k_, v_, g_, b_: _pairs(s_, q_, k_, v_, g_, b_, revs, inv_in)[:2],
                             s_in, q_in, k_in, v_in, g_in, b_in)
            ds, dq, dk, dv, dg, db = vjp((ds_out, do_out))
        else:
            _, vjp = jax.vjp(lambda s_, k_, v_, g_, b_: _pairs(s_, q_in, k_, v_, g_, b_, revs, inv_in)[0],
                             s_in, k_in, v_in, g_in, b_in)
            ds, dk, dv, dg, db = vjp(ds_out)
        dgb = [jnp.zeros((PAIR, DH), F32)] * 2
        for i, (h, d) in enumerate(work):
            dq_ref, ds_ref = dirs[d][4], dirs[d][3]
            ds_ref[h] = ds[i]
            if has_q:
                dq_ref[h] = dq[i]
            dq_ref[qoff + h], dq_ref[qoff + H + h] = dk[i], dv[i]
            dgb[d] = (dgb[d] + jnp.where(lane == d * H + h, dg[i], 0.0)
                      + jnp.where(lane == 2 * H + d * H + h, db[i], 0.0))
        dgf_ref[...] = dgb[0]
        dgr_ref[...] = dgb[1]

    fwd3 = lambda i: (0, n - 1 - i, 0)
    rev3 = lambda i: (0, i, 0)
    state = pl.BlockSpec((H, DH, DH), lambda i: (0, 0, 0))
    saved = pl.BlockSpec((1, H, DH, DH), lambda i: (n - 1 - i, 0, 0, 0))
    gf_spec = pl.BlockSpec((PAIR, DH), lambda i: (n - 1 - i, 0))
    gr_spec = pl.BlockSpec((PAIR, DH), lambda i: (i, 0))
    return pl.pallas_call(
        body, name=name, grid=(n,),
        in_specs=[pl.BlockSpec((nb, PAIR, DH), fwd3), pl.BlockSpec((nb, PAIR, DH), rev3), gf_spec, gr_spec,
                  saved, saved, saved, saved,
                  pl.BlockSpec((H, PAIR, DH), fwd3), pl.BlockSpec((H, PAIR, DH), rev3), state, state],
        out_specs=[pl.BlockSpec((nb, PAIR, DH), fwd3), pl.BlockSpec((nb, PAIR, DH), rev3), gf_spec, gr_spec, state, state],
        out_shape=[jax.ShapeDtypeStruct((nb, rows, DH), F32)] * 2 + [jax.ShapeDtypeStruct((rows, DH), F32)] * 2
        + [jax.ShapeDtypeStruct((H, DH, DH), F32)] * 2,
        compiler_params=_params(("arbitrary",)),
    )(qkv, qkv, gb, gb, ssf, ssb, tsf, tsb, do, do, dsf, dsb)


def _stage1(zb, ua, va, za, o, gon, lng, lnb, wsp, bsp):
    gv = [_gelu(t) for t in va]
    mu = sum(jnp.sum(t, axis=-1, keepdims=True) for t in gv) * (1.0 / D)
    xc = [t - mu for t in gv]
    var = sum(jnp.sum(t * t, axis=-1, keepdims=True) for t in xc) * (1.0 / D)
    rs = lax.rsqrt(var + EPS)
    ya, yb = [], []
    for g in range(H):
        vv = xc[g] * rs * lng[g] + lnb[g]
        s = _dot(wsp[g], vv) + bsp[g]
        ya.append(_gelu(ua[g]) * s * _silu(za[g]))
        r = lax.rsqrt(jnp.mean(o[g] * o[g], axis=-1, keepdims=True) + EPS)
        yb.append(o[g] * r * gon * _silu(zb[g]))
    return ya, yb


def _stage2(ma, mb, ga, gb):
    return jax.nn.sigmoid(ga) * ma + jax.nn.sigmoid(gb) * mb


def _stage3(out, x, tgt, gpost, gate):
    r = out * lax.rsqrt(jnp.mean(out * out, axis=-1, keepdims=True) + EPS) * gpost
    err = x + gate * r - tgt
    return 0.5 * jnp.sum(jnp.mean(err * err, axis=-1, keepdims=True), axis=0, keepdims=True)


def _post(p_rest, o_f, o_b, x, tgt, mod, bmod, gon, lng, lnb, wsp, bspb, wpa, wpb, wout, gpost, *, name):
    rows = x.shape[0]
    n = rows // GM
    lanes = lambda g: slice(g * DH, (g + 1) * DH)
    bdot = lambda a, w_ref: _dot(a.astype(BF16), w_ref[...])
    bdot_t = lambda a, w_ref: _dot(a.astype(BF16), w_ref[...], _NT)

    def body(p_ref, of_ref, ob_ref, x_ref, t_ref, m_ref, bm_ref, gon_ref, lng_ref, lnb_ref, wsp_ref, bsp_ref,
             wpa_ref, wpb_ref, wout_ref, gp_ref,
             loss_ref, dp_ref, do_ref, dx_ref, ya_ref, yb_ref, mg_ref, dma_ref, dmb_ref, dout_ref,
             dvec_ref, dgon_ref, dwsp_ref, dbsp_ref):
        @pl.when(pl.program_id(0) == 0)
        def _():
            loss_ref[...] = jnp.zeros_like(loss_ref)
            dvec_ref[...] = jnp.zeros_like(dvec_ref)
            dgon_ref[...] = jnp.zeros_like(dgon_ref)
            dwsp_ref[...] = jnp.zeros_like(dwsp_ref)
            dbsp_ref[...] = jnp.zeros_like(dbsp_ref)

        piece = lambda blk: [p_ref[:, blk * D + g * DH: blk * D + (g + 1) * DH].astype(F32) for g in range(H)]
        zb, ua, va, za = piece(0), piece(1), piece(2), piece(3)
        o = [of_ref[g] + ob_ref[g] for g in range(H)]
        gon = gon_ref[...]
        lng = [lng_ref[:, lanes(g)] for g in range(H)]
        lnb = [lnb_ref[:, lanes(g)] for g in range(H)]
        wsp = [wsp_ref[g] for g in range(H)]
        bsp = [bsp_ref[g] for g in range(H)]
        (ya, yb), vjp1 = jax.vjp(_stage1, zb, ua, va, za, o, gon, lng, lnb, wsp, bsp)
        y_a, y_b = jnp.concatenate(ya, axis=1), jnp.concatenate(yb, axis=1)
        ma, mb = bdot(y_a, wpa_ref), bdot(y_b, wpb_ref)
        ga, gb = p_ref[:, 4 * D:5 * D].astype(F32), p_ref[:, 5 * D:6 * D].astype(F32)
        merged, vjp2 = jax.vjp(_stage2, ma, mb, ga, gb)
        out = bdot(merged, wout_ref)
        gate = m_ref[:, 2 * D:3 * D] + bm_ref[:, 2 * D:3 * D]
        loss, vjp3 = jax.vjp(_stage3, out, x_ref[...], t_ref[...], gp_ref[...], gate)
        loss_ref[...] += jnp.broadcast_to(loss, loss_ref.shape)

        dout, dx, _, dgpost, dgate = vjp3(jnp.ones((1, 1), F32))
        dx_ref[...] = dx
        dmerged = bdot_t(dout, wout_ref)
        dma, dmb, dga, dgb = vjp2(dmerged)
        dya, dyb = bdot_t(dma, wpa_ref), bdot_t(dmb, wpb_ref)
        dzb, dua, dva, dza, do, dgon, dlng, dlnb, dwsp, dbsp = vjp1(
            ([dya[:, lanes(g)] for g in range(H)], [dyb[:, lanes(g)] for g in range(H)]))

        for blk, dlist in enumerate((dzb, dua, dva, dza)):
            for g in range(H):
                dp_ref[:, blk * D + g * DH: blk * D + (g + 1) * DH] = dlist[g].astype(BF16)
        dp_ref[:, 4 * D:5 * D] = dga.astype(BF16)
        dp_ref[:, 5 * D:6 * D] = dgb.astype(BF16)
        for g in range(H):
            do_ref[g] = do[g]
            dwsp_ref[g] += dwsp[g]
            dbsp_ref[g] += dbsp[g]
            dvec_ref[2:3, lanes(g)] += dlng[g]
            dvec_ref[3:4, lanes(g)] += dlnb[g]
        dvec_ref[0:1, :] += dgpost
        dvec_ref[1:2, :] += dgate
        dgon_ref[0:1, :] += dgon
        ya_ref[...] = y_a.astype(BF16)
        yb_ref[...] = y_b.astype(BF16)
        mg_ref[...] = merged.astype(BF16)
        dma_ref[...] = dma.astype(BF16)
        dmb_ref[...] = dmb.astype(BF16)
        dout_ref[...] = dout.astype(BF16)

    row = lambda w: pl.BlockSpec((GM, w), lambda i: (i, 0))
    heads = pl.BlockSpec((H, GM, DH), lambda i: (0, i, 0))
    full = lambda shape: pl.BlockSpec(shape, lambda i: tuple(0 for _ in shape))
    sds = jax.ShapeDtypeStruct
    return pl.pallas_call(
        body, name=name, grid=(n,),
        in_specs=[row(6 * D), heads, heads, row(D), row(D), full((1, 3 * D)), full((1, 3 * D)), full((1, DH)),
                  full((1, D)), full((1, D)), full((H, GM, GM)), full((H, GM, GM)),
                  full((D, D)), full((D, D)), full((D, D)), full((1, D))],
        out_specs=[full((8, DH)), row(6 * D), heads, row(D)] + [row(D)] * 6
        + [full((8, D)), full((8, DH)), full((H, GM, GM)), full((H, GM, GM))],
        out_shape=[sds((8, DH), F32), sds((rows, 6 * D), BF16), sds((H, rows, DH), F32), sds((rows, D), F32)]
        + [sds((rows, D), BF16)] * 6 + [sds((8, D), F32), sds((8, DH), F32), sds((H, GM, GM), F32), sds((H, GM, GM), F32)],
        compiler_params=_params(("arbitrary",)),
    )(p_rest, o_f, o_b, x, tgt, mod, bmod, gon, lng, lnb, wsp, bspb, wpa, wpb, wout, gpost)


def _silu_rows(c, *, name):
    def body(c_ref, o_ref):
        o_ref[...] = _silu(c_ref[...])

    return pl.pallas_call(body, name=name, out_shape=jax.ShapeDtypeStruct(c.shape, F32))(c)


def _dsilu_mul(c, d, *, name):
    def body(c_ref, d_ref, o_ref):
        _, vjp = jax.vjp(_silu, c_ref[...])
        (o_ref[...],) = vjp(d_ref[...])

    return pl.pallas_call(body, name=name, out_shape=jax.ShapeDtypeStruct(c.shape, F32))(c, d)


def _adamw(w, g, m, v, *, name):
    rows, cols = w.shape[-2:]
    t = _tile(rows if g.ndim == 2 else g.shape[1])
    c1 = 1.0 / (1.0 - ADAM_B1 ** ADAM_STEP)
    c2 = 1.0 / (1.0 - ADAM_B2 ** ADAM_STEP)

    def body(w_ref, g_ref, m_ref, v_ref, go_ref, d_ref, mo_ref, vo_ref):
        blk = lambda r: r[...].reshape(t, cols)
        gv = blk(g_ref)
        mn = ADAM_B1 * blk(m_ref) + (1.0 - ADAM_B1) * gv
        vn = ADAM_B2 * blk(v_ref) + (1.0 - ADAM_B2) * (gv * gv)
        delta = -ADAM_LR * ((mn * c1) / (jnp.sqrt(vn * c2) + ADAM_EPS) + ADAM_WD * blk(w_ref))
        for ref, val in ((go_ref, gv), (d_ref, delta), (mo_ref, mn), (vo_ref, vn)):
            ref[...] = val.reshape(ref.shape)

    tile = (pl.BlockSpec((1, t, cols), lambda i: (0, i, 0)) if w.ndim == 3 else pl.BlockSpec((t, cols), lambda i: (i, 0)))
    if g.ndim == 3:
        per = g.shape[1] // t
        g_tile = pl.BlockSpec((1, t, cols), lambda i: (i // per, i % per, 0))
    else:
        g_tile = pl.BlockSpec((t, cols), lambda i: (i, 0))
    return pl.pallas_call(
        body, name=name, grid=(rows // t,),
        in_specs=[tile, g_tile, tile, tile], out_specs=[tile] * 4,
        out_shape=[jax.ShapeDtypeStruct(w.shape, F32)] * 4,
        compiler_params=_params(("parallel",)),
    )(w, g, m, v)


def _add_sibling(full, got, where, out_dtype, *, name):
    s, rows, cols = full.shape
    hr = rows // 2
    t = _tile(hr)
    nt = hr // t

    def body(w_ref, a_ref, b_ref, o_ref):
        o_ref[...] = (a_ref[...].astype(F32) + b_ref[...].astype(F32)).astype(out_dtype)

    tile = pl.BlockSpec((1, t, cols), lambda j, i, w: (j, i, 0))
    return pl.pallas_call(
        body, name=name,
        grid_spec=pltpu.PrefetchScalarGridSpec(
            num_scalar_prefetch=1, grid=(s, nt),
            in_specs=[pl.BlockSpec((1, t, cols), lambda j, i, w: (j, w[0] * nt + i, 0)), tile], out_specs=tile),
        out_shape=jax.ShapeDtypeStruct((s, hr, cols), out_dtype), compiler_params=_params(("parallel", "parallel")),
    )(where, full, got)


def _sum_chips(own, slots, where, n_out, which, *, name):
    _, hr, cols = own.shape
    t = _tile(hr)

    def body(w_ref, a_ref, s_ref, o_ref):
        f = lambda v: v.astype(F32)
        o_ref[0] = ((f(a_ref[0]) + f(s_ref[0])) + f(s_ref[1])) + f(s_ref[2])

    return pl.pallas_call(
        body, name=name,
        grid_spec=pltpu.PrefetchScalarGridSpec(
            num_scalar_prefetch=1, grid=(hr // t,),
            in_specs=[pl.BlockSpec((1, t, cols), lambda i, w: (w[1], i, 0)),
                      pl.BlockSpec((N_CHIPS - 1, t, cols), lambda i, w: (0, i, 0))],
            out_specs=pl.BlockSpec((1, t, cols), lambda i, w: (w[which], i, 0))),
        out_shape=jax.ShapeDtypeStruct((n_out, hr, cols), F32), compiler_params=_params(("parallel",)),
    )(where, own, slots)


def _local_step(x, c, ctx, c_ctx, tgt, w_mod, b_mod, g_pre, g_post, w_qkv, w_ab, w_rest, w_conv, a_log, dt_bias,
                g_onorm, gm_ln_g, gm_ln_b, w_sp, b_sp, late_weights, on_large_gradients=None, run_after=None):
    rows, rows_c = x.shape[0], ctx.shape[0]
    cc = jnp.zeros((16, D), F32).at[0].set(c[0]).at[1].set(c_ctx)
    scc = _silu_rows(cc, name="silu_cond")
    mod = _mm(scc, w_mod, b_sections=True, after=run_after, name="mod_fwd")
    mod_x, mod_c = mod[0:1], mod[1:2]
    avec = jnp.zeros((1, DH), F32).at[0, :2 * H].set(a_log.reshape(-1))
    dvec = jnp.zeros((1, DH), F32).at[0, :2 * H].set(dt_bias.reshape(-1))
    bspb = jnp.broadcast_to(b_sp[:, :, None], (H, GM, GM))
    w_kv, wconv_kv = w_qkv[:, D:], w_conv[:, D:]

    h_c = _norm_fwd(ctx, g_pre, mod_c, b_mod, name="norm_fwd_ctx")
    pc_kv = _mm(h_c, w_kv, name="inproj_ctx_kv")
    pc_ab = _mm(h_c, w_ab, name="inproj_ctx_ab")
    kv_c = _prep_fwd(pc_kv, wconv_kv, H, name="prep_fwd_ctx")
    gb_c = _gates_fwd(pc_ab, avec, dvec, name="gates_fwd_ctx")
    s_zero = jnp.zeros((H, DH, DH), F32)
    _, _, ssf_c, ssb_c, tsf_c, tsb_c, s_f, s_b = _gdn_fwd(kv_c, gb_c, s_zero, s_zero, False, name="gdn_fwd_ctx")

    h_x = _norm_fwd(x, g_pre, mod_x, b_mod, name="norm_fwd_x")
    p_qkv = _mm(h_x, w_qkv, name="inproj_qkv")
    p_ab = _mm(h_x, w_ab, name="inproj_ab")
    p_rest = _mm(h_x, w_rest, out_dtype=BF16, name="inproj_rest")
    qkv = _prep_fwd(p_qkv, w_conv, 2 * H, name="prep_fwd_x")
    gb_x = _gates_fwd(p_ab, avec, dvec, name="gates_fwd_x")
    o_f, o_b, ssf, ssb, tsf, tsb, _, _ = _gdn_fwd(qkv, gb_x, s_f, s_b, True, name="gdn_fwd_x")

    w_pa, w_pb, w_out = late_weights(o_f)
    (loss_acc, dp_rest, do, dx_res, ya, yb, mg, dma, dmb, dout, dvec_post, dgon, dwsp, dbspb) = _post(
        p_rest, o_f, o_b, x, tgt, mod_x, b_mod, g_onorm, gm_ln_g, gm_ln_b, w_sp, bspb, w_pa, w_pb, w_out, g_post,
        name="post")
    g = {}
    g["w_pa"] = _mm(ya, dma, ta=True, out_dtype=BF16, name="dw_pa")
    g["w_pb"] = _mm(yb, dmb, ta=True, out_dtype=BF16, name="dw_pb")
    g["w_out"] = _mm(mg, dout, ta=True, out_dtype=BF16, name="dw_out")

    zeros_s = jnp.zeros((H, DH, DH), F32)
    dq_f, dq_b, dg_f, dg_b, ds0_f, ds0_b = _gdn_bwd(qkv, gb_x, ssf, ssb, tsf, tsb, do, zeros_s, zeros_s, True,
                                                    name="gdn_bwd_x")
    dp_qkv, dwc_x = _prep_bwd(p_qkv, w_conv, dq_f, dq_b, 2 * H, name="prep_bwd_x")
    dp_ab, dav_x, ddv_x = _gates_bwd(p_ab, avec, dvec, dg_f, dg_b, name="gates_bwd_x")
    dkv_f, dkv_b, dgc_f, dgc_b, _, _ = _gdn_bwd(kv_c, gb_c, ssf_c, ssb_c, tsf_c, tsb_c, jnp.zeros((H, rows_c, DH), F32),
                                                 ds0_f, ds0_b, False, name="gdn_bwd_ctx")
    dpc_kv, dwc_c = _prep_bwd(pc_kv, wconv_kv, dkv_f, dkv_b, H, name="prep_bwd_ctx")
    dpc_ab, dav_c, ddv_c = _gates_bwd(pc_ab, avec, dvec, dgc_f, dgc_b, name="gates_bwd_ctx")

    dw_kv_c = _mm(h_c, dpc_kv, ta=True, name="dw_kv_ctx")
    dw_qkv = _mm(h_x, dp_qkv, ta=True, tn=D, add=dw_kv_c, add_from=1, out_dtype=BF16, name="dw_qkv")
    dw_ab_c = _mm(h_c, dpc_ab, ta=True, name="dw_ab_ctx")
    dw_ab = _mm(h_x, dp_ab, ta=True, add=dw_ab_c, out_dtype=BF16, name="dw_ab")
    dw_rest = _mm(h_x, dp_rest, ta=True, out_dtype=BF16, name="dw_rest")
    g["w_in"] = (dw_qkv, dw_ab[:, :4 * H], dw_rest)
    token = on_large_gradients(g) if on_large_gradients is not None else None
    dh = _mm(dp_ab, w_ab, tb=True, after=token, name="dh_ab")
    dh = _mm(dp_qkv, w_qkv, tb=True, tk=3 * D // 2, add=dh, after=token, name="dh_qkv")
    dh = _mm(dp_rest, w_rest, tb=True, tk=3 * D // 2, add=dh, after=token, name="dh_rest")
    grad_x, dgpre_x, dm_x = _norm_bwd(x, g_pre, mod_x, b_mod, [dh], dx_res, name="norm_bwd_x")
    dh_c = _mm(dpc_ab, w_ab, tb=True, after=token, name="dhc_ab")
    dh_c = _mm(dpc_kv, w_kv, tb=True, add=dh_c, after=token, name="dhc_kv")
    _, dgpre_c, dm_c = _norm_bwd(ctx, g_pre, mod_c, b_mod, [dh_c], None, name="norm_bwd_ctx")

    dm_x = dm_x.at[:, 2 * D:].add(dvec_post[1:2])
    dmod = jnp.zeros((16, 3 * D), F32).at[0].set(dm_x[0]).at[1].set(dm_c[0])
    g["mod_factors"] = (scc[0], dm_x[0])
    g["dm_ctx"] = dm_c[0]
    g["silu_c_ctx"] = scc[1:2]
    dcc = _mm(dmod, w_mod, tb=True, b_sections=True, name="dcc")
    g["c_ctx"] = _dsilu_mul(cc[:8], dcc[:8], name="dc_ctx")[1]
    g["b_mod"] = dm_x + dm_c
    g["g_pre"] = dgpre_x + dgpre_c
    g["g_post"] = dvec_post[0:1]
    g["gm_ln_g"], g["gm_ln_b"] = dvec_post[2:3], dvec_post[3:4]
    g["g_onorm"] = dgon[0:1]
    g["w_sp"] = dwsp
    g["b_sp"] = jnp.sum(dbspb, axis=-1)
    g["w_conv"] = dwc_x.at[:, D:].add(dwc_c)
    g["a_log"] = (dav_x + dav_c)[0, :2 * H].reshape(2, H)
    g["dt_bias"] = (ddv_x + ddv_c)[0, :2 * H].reshape(2, H)
    return loss_acc[0, 0], grad_x, g


ANY = pl.BlockSpec(memory_space=pl.ANY)


def _place():
    x, y, c = lax.axis_index("x"), lax.axis_index("y"), lax.axis_index("c")
    chips = [(1 - x, y), (x, 1 - y), (1 - x, 1 - y)]
    return x, y, c, (x, y, 1 - c), chips


def _gather_shards(big, small, *, name):
    nb = len(big)

    def body(*refs):
        ins, sm_in = refs[:nb], refs[nb]
        outs, sm_out = refs[nb + 1:2 * nb + 1], refs[2 * nb + 1]
        send, recv = refs[2 * nb + 2:]
        x, y, c, sibling, chips = _place()
        mine = 2 * x + y

        def half(a, shard, hc):
            hr = big[a].shape[0] // 2
            return outs[a].at[shard, pl.ds(hc * hr, hr), :]

        def remote(k, src, dst, to):
            return pltpu.make_async_remote_copy(src_ref=src, dst_ref=dst, send_sem=send.at[k], recv_sem=recv.at[k],
                                                device_id=to, device_id_type=MESH)

        sends = []
        for a in range(nb):
            hr = big[a].shape[0] // 2
            for j, chip in enumerate(chips):
                sends.append(remote(a * 3 + j, ins[a].at[pl.ds(c * hr, hr), :], half(a, mine, c), (*chip, c)))
        for j, chip in enumerate(chips):
            sends.append(remote(nb * 3 + j, sm_in, sm_out.at[mine], (*chip, c)))
        for cp in sends:
            cp.start()
        base = nb * 3 + 3
        passed = []
        for a in range(nb):
            for j, (px, py) in enumerate(chips):
                theirs = 2 * px + py
                remote(a * 3 + j, half(a, theirs, c), half(a, theirs, c), sibling).wait_recv()
                fw = remote(base + a * 3 + j, half(a, theirs, c), half(a, theirs, c), sibling)
                fw.start()
                passed.append(fw)
        for a in range(nb):
            for j, (px, py) in enumerate(chips):
                theirs = 2 * px + py
                remote(base + a * 3 + j, half(a, theirs, 1 - c), half(a, theirs, 1 - c), sibling).wait_recv()
        for j, (px, py) in enumerate(chips):
            remote(nb * 3 + j, sm_in, sm_out.at[2 * px + py], sibling).wait_recv()
        for cp in sends + passed:
            cp.wait_send()

    n_remote = 2 * nb * 3 + 3
    outs = pl.pallas_call(
        body, name=name, in_specs=[ANY] * (nb + 1), out_specs=[ANY] * (nb + 1),
        out_shape=[jax.ShapeDtypeStruct((N_CHIPS,) + a.shape, a.dtype) for a in big + [small]],
        scratch_shapes=[pltpu.SemaphoreType.DMA((n_remote,)), pltpu.SemaphoreType.DMA((n_remote,))],
    )(*big, small)
    return outs[:nb], outs[nb]


def _gather_late_start(arrs, after, *, name):
    na = len(arrs)

    def body(*refs):
        ins, land = refs[:na], refs[na:2 * na]
        send, recv = refs[2 * na + 1], refs[2 * na + 2]
        token = refs[-1]
        x, y, c, _, chips = _place()
        for a in range(na):
            hr = arrs[a].shape[0] // 2
            for j, (px, py) in enumerate(chips):
                for other in range(2):
                    k = (a * 3 + j) * 2 + other
                    _remote(ins[a].at[pl.ds(c * hr, hr), :], land[a].at[2 * x + y, pl.ds(c * hr, hr), :],
                            send.at[k], recv.at[k], (px, py, c ^ other)).start()
        token[...] = jnp.zeros_like(token)

    lands = [((N_CHIPS,) + a.shape, a.dtype) for a in arrs]
    held = lambda a: pltpu.with_memory_space_constraint(a, pltpu.HBM)
    outs = pl.pallas_call(
        body, name=name, in_specs=[HBM] * (2 * na) + [ANY],
        out_specs=[SEM, SEM] + [HBM] * (2 * na) + [pl.BlockSpec(memory_space=pltpu.VMEM)],
        out_shape=[pltpu.SemaphoreType.DMA((6 * na,)), pltpu.SemaphoreType.DMA((6 * na,))]
        + [pltpu.HBM(a.shape, a.dtype) for a in arrs] + [pltpu.HBM(s, d) for s, d in lands]
        + [jax.ShapeDtypeStruct((8, 128), F32)],
        input_output_aliases={i: 2 + i for i in range(2 * na)},
        compiler_params=pltpu.CompilerParams(has_side_effects=pltpu.SideEffectType.DATAFLOW_SIDE_EFFECTING),
    )(*[held(a) for a in arrs], *[held(lax.empty(s, d)) for s, d in lands], after)
    return outs[0], outs[1], outs[2:2 + na], outs[2 + na:2 + 2 * na], outs[-1]


def _gather_late_wait(send, recv, thru, land, after, *, name):
    na = len(thru)

    def body(*refs):
        ins, slots = refs[:na], refs[na:2 * na]
        send_sem, recv_sem = refs[2 * na], refs[2 * na + 1]
        x, y, c, _, chips = _place()
        for a in range(na):
            hr = thru[a].shape[0] // 2
            for j, (px, py) in enumerate(chips):
                for other in range(2):
                    k = (a * 3 + j) * 2 + other
                    half = c ^ other
                    cp = _remote(ins[a].at[pl.ds(c * hr, hr), :], slots[a].at[2 * px + py, pl.ds(half * hr, hr), :],
                                 send_sem.at[k], recv_sem.at[k], (px, py, half))
                    cp.wait_send()
                    cp.wait_recv()

    outs = pl.pallas_call(
        body, name=name, in_specs=[HBM] * (2 * na) + [SEM, SEM, ANY], out_specs=[HBM] * (2 * na),
        out_shape=[pltpu.HBM(a.shape, a.dtype) for a in list(thru) + list(land)],
        input_output_aliases={i: i for i in range(2 * na)},
        compiler_params=pltpu.CompilerParams(has_side_effects=pltpu.SideEffectType.DATAFLOW_SIDE_EFFECTING),
    )(*thru, *land, send, recv, after)
    return outs[na:]


def _row_chunks(rows, row_bytes, align=8):
    n = max(1, min(rows // align, -(-rows * row_bytes // DMA_CHUNK_BYTES)))
    per = -(-(-(-rows // n)) // align) * align
    return [(r, min(per, rows - r)) for r in range(0, rows, per)]


def _remote(src, dst, send, recv, to):
    return pltpu.make_async_remote_copy(src_ref=src, dst_ref=dst, send_sem=send, recv_sem=recv, device_id=to,
                                        device_id_type=MESH)


def _exchange_halves(arrs, *, name):
    na = len(arrs)

    def body(*refs):
        ins, got = refs[:na], refs[na:2 * na]
        send, recv = refs[2 * na:]
        x, y, c, sibling, _ = _place()
        for a in range(na):
            ns, rows, cols = arrs[a].shape
            hr = rows // 2
            for s in range(ns):
                for r0, nr in _row_chunks(hr, cols * arrs[a].dtype.itemsize, 16):
                    _remote(ins[a].at[s, pl.ds((1 - c) * hr + r0, nr), :], got[a].at[s, pl.ds(r0, nr), :],
                            send.at[a], recv.at[a], sibling).start()
        for a in range(na):
            hr = arrs[a].shape[1] // 2
            _remote(ins[a].at[:, pl.ds((1 - c) * hr, hr), :], got[a], send.at[a], recv.at[a], sibling).wait()

    return pl.pallas_call(
        body, name=name, in_specs=[ANY] * na, out_specs=[ANY] * na,
        out_shape=[jax.ShapeDtypeStruct((N_CHIPS, a.shape[1] // 2, a.shape[2]), a.dtype) for a in arrs],
        scratch_shapes=[pltpu.SemaphoreType.DMA((na,)), pltpu.SemaphoreType.DMA((na,))],
    )(*arrs)


def _scatter_sections(arrs, *, name):
    na = len(arrs)

    def body(*refs):
        ins, outs = refs[:na], refs[na:2 * na]
        send, recv = refs[2 * na:]
        x, y, c, _, chips = _place()
        for a in range(na):
            _, hr, cols = arrs[a].shape
            for j, (px, py) in enumerate(chips):
                for r0, nr in _row_chunks(hr, cols * arrs[a].dtype.itemsize, 16):
                    _remote(ins[a].at[2 * px + py, pl.ds(r0, nr), :], outs[a].at[j, pl.ds(r0, nr), :],
                            send.at[a * 3 + j], recv.at[a * 3 + j], (px, py, c)).start()
        for a in range(na):
            for j, (px, py) in enumerate(chips):
                _remote(ins[a].at[2 * px + py], outs[a].at[j], send.at[a * 3 + j], recv.at[a * 3 + j], (px, py, c)).wait()

    return pl.pallas_call(
        body, name=name, in_specs=[ANY] * na, out_specs=[ANY] * na,
        out_shape=[jax.ShapeDtypeStruct((N_CHIPS - 1,) + a.shape[1:], a.dtype) for a in arrs],
        scratch_shapes=[pltpu.SemaphoreType.DMA((3 * na,)), pltpu.SemaphoreType.DMA((3 * na,))],
    )(*arrs)


HBM = pl.BlockSpec(memory_space=pltpu.HBM)
SEM = pl.BlockSpec(memory_space=pltpu.SEMAPHORE)


def _scatter_start(arrs, *, name):
    na = len(arrs)

    def body(*refs):
        ins, land = refs[:na], refs[na:2 * na]
        send, recv = refs[2 * na], refs[2 * na + 1]
        token = refs[-1]
        x, y, c, _, chips = _place()
        for a in range(na):
            _, hr, cols = arrs[a].shape
            for j, (px, py) in enumerate(chips):
                for r0, nr in _row_chunks(hr, cols * arrs[a].dtype.itemsize, 16):
                    _remote(ins[a].at[2 * px + py, pl.ds(r0, nr), :], land[a].at[j, pl.ds(r0, nr), :],
                            send.at[a * 3 + j], recv.at[a * 3 + j], (px, py, c)).start()
        token[...] = jnp.zeros_like(token)

    slots = [((N_CHIPS - 1,) + a.shape[1:], a.dtype) for a in arrs]
    held = lambda a: pltpu.with_memory_space_constraint(a, pltpu.HBM)
    outs = pl.pallas_call(
        body, name=name, in_specs=[HBM] * (2 * na),
        out_specs=[SEM, SEM] + [HBM] * (2 * na) + [pl.BlockSpec(memory_space=pltpu.VMEM)],
        out_shape=[pltpu.SemaphoreType.DMA((3 * na,)), pltpu.SemaphoreType.DMA((3 * na,))]
        + [pltpu.HBM(a.shape, a.dtype) for a in arrs] + [pltpu.HBM(s, d) for s, d in slots]
        + [jax.ShapeDtypeStruct((8, 128), F32)],
        input_output_aliases={i: 2 + i for i in range(2 * na)},
        compiler_params=pltpu.CompilerParams(has_side_effects=pltpu.SideEffectType.DATAFLOW_SIDE_EFFECTING),
    )(*[held(a) for a in arrs], *[held(lax.empty(s, d)) for s, d in slots])
    return outs[0], outs[1], outs[2:2 + na], outs[2 + na:2 + 2 * na], outs[-1]


def _scatter_wait(send, recv, thru, land, after, *, name):
    na = len(thru)

    def body(*refs):
        ins, slots = refs[:na], refs[na:2 * na]
        send_sem, recv_sem = refs[2 * na], refs[2 * na + 1]
        x, y, c, _, chips = _place()
        for a in range(na):
            for j, (px, py) in enumerate(chips):
                cp = _remote(ins[a].at[2 * px + py], slots[a].at[j], send_sem.at[a * 3 + j], recv_sem.at[a * 3 + j],
                             (px, py, c))
                cp.wait_send()
                cp.wait_recv()

    outs = pl.pallas_call(
        body, name=name, in_specs=[HBM] * (2 * na) + [SEM, SEM, ANY], out_specs=[HBM] * (2 * na),
        out_shape=[pltpu.HBM(a.shape, a.dtype) for a in list(thru) + list(land)],
        input_output_aliases={i: i for i in range(2 * na)},
        compiler_params=pltpu.CompilerParams(has_side_effects=pltpu.SideEffectType.DATAFLOW_SIDE_EFFECTING),
    )(*thru, *land, send, recv, after)
    return outs[:na], outs[na:]


def _finish_reduce(big, small, *, name):
    nb = len(big)

    def body(*refs):
        outs, sm = refs[nb + 1:2 * nb + 1], refs[2 * nb + 1]
        send, recv = refs[2 * nb + 2:]
        x, y, c, sibling, chips = _place()
        blk = lambda px, py, pc: sm.at[4 * px + 2 * py + pc]
        for a in range(nb):
            _, hr, cols = big[a].shape
            for r0, nr in _row_chunks(hr, cols * 4):
                _remote(outs[a].at[c, pl.ds(r0, nr), :], outs[a].at[c, pl.ds(r0, nr), :], send.at[a], recv.at[a],
                        sibling).start()
        first = [_remote(blk(x, y, c), blk(x, y, c), send.at[nb], recv.at[nb], sibling)]
        first += [_remote(blk(x, y, c), blk(x, y, c), send.at[nb + 1 + j], recv.at[nb + 1 + j], (*chip, c))
                  for j, chip in enumerate(chips)]
        for cp in first:
            cp.start()
        passed = []
        for j, (px, py) in enumerate(chips):
            _remote(blk(px, py, c), blk(px, py, c), send.at[nb + 1 + j], recv.at[nb + 1 + j], sibling).wait_recv()
            fw = _remote(blk(px, py, c), blk(px, py, c), send.at[nb + 4 + j], recv.at[nb + 4 + j], sibling)
            fw.start()
            passed.append(fw)
        for a in range(nb):
            _remote(outs[a].at[c], outs[a].at[1 - c], send.at[a], recv.at[a], sibling).wait()
        _remote(blk(x, y, c), blk(x, y, 1 - c), send.at[nb], recv.at[nb], sibling).wait_recv()
        for j, (px, py) in enumerate(chips):
            _remote(blk(px, py, c), blk(px, py, 1 - c), send.at[nb + 4 + j], recv.at[nb + 4 + j], sibling).wait_recv()
        for cp in first + passed:
            cp.wait_send()

    n_remote = nb + 7
    arrs = list(big) + [small]
    outs = pl.pallas_call(
        body, name=name, in_specs=[ANY] * (nb + 1), out_specs=[ANY] * (nb + 1),
        out_shape=[jax.ShapeDtypeStruct(a.shape, F32) for a in arrs],
        input_output_aliases={i: i for i in range(nb + 1)},
        scratch_shapes=[pltpu.SemaphoreType.DMA((n_remote,)), pltpu.SemaphoreType.DMA((n_remote,))],
    )(*arrs)
    return outs[:nb], outs[nb]


def _reduce_start(sectioned, where):
    got = _exchange_halves(sectioned, name="rs_exchange_halves")
    chip_sum = [_add_sibling(a, g, where, BF16, name=f"rs_add_sibling_{i}")
                for i, (a, g) in enumerate(zip(sectioned, got))]
    send, recv, thru, land, token = _scatter_start(chip_sum, name="rs_scatter_start")
    return (send, recv, thru, land), token


def _reduce_finish(handle, small, where, after):
    send, recv, thru, land = handle
    got = _exchange_halves([small], name="rs_exchange_small")
    small_sum = _add_sibling(small, got[0], where, F32, name="rs_add_sibling_small")
    small_slots = _scatter_sections([small_sum], name="rs_scatter_small")[0]
    own, slots = _scatter_wait(send, recv, thru, land, after, name="rs_scatter_wait")
    red = [_sum_chips(p, s, where, 2, 0, name=f"rs_sum_chips_{i}") for i, (p, s) in enumerate(zip(own, slots))]
    red_small = _sum_chips(small_sum, small_slots, where, 2 * N_CHIPS, 2, name="rs_sum_chips_small")
    big, sm = _finish_reduce(red, red_small, name="rs_finish")
    return big, sm.reshape(-1, sm.shape[-1])


def kernel(x, c, ctx, c_ctx, w_mod, b_mod, g_pre, g_post, w_in, w_conv, a_log, dt_bias, g_onorm, gm_ln_g, gm_ln_b, w_sp, b_sp, w_pa, w_pb, w_out, loss_target, m_c_ctx, m_w_mod, m_b_mod, m_g_pre, m_g_post, m_w_in, m_w_conv, m_a_log, m_dt_bias, m_g_onorm, m_gm_ln_g, m_gm_ln_b, m_w_sp, m_b_sp, m_w_pa, m_w_pb, m_w_out, v_c_ctx, v_w_mod, v_b_mod, v_g_pre, v_g_post, v_w_in, v_w_conv, v_a_log, v_dt_bias, v_g_onorm, v_gm_ln_g, v_gm_ln_b, v_w_sp, v_b_sp, v_w_pa, v_w_pb, v_w_out):
    names = ["c_ctx", "w_mod", "b_mod", "g_pre", "g_post", "w_in", "w_conv", "a_log", "dt_bias", "g_onorm", "gm_ln_g",
             "gm_ln_b", "w_sp", "b_sp", "w_pa", "w_pb", "w_out"]
    w = dict(zip(names, (c_ctx, w_mod, b_mod, g_pre, g_post, w_in, w_conv, a_log, dt_bias, g_onorm, gm_ln_g, gm_ln_b,
                         w_sp, b_sp, w_pa, w_pb, w_out)))
    m = dict(zip(names, (m_c_ctx, m_w_mod, m_b_mod, m_g_pre, m_g_post, m_w_in, m_w_conv, m_a_log, m_dt_bias, m_g_onorm,
                         m_gm_ln_g, m_gm_ln_b, m_w_sp, m_b_sp, m_w_pa, m_w_pb, m_w_out)))
    v = dict(zip(names, (v_c_ctx, v_w_mod, v_b_mod, v_g_pre, v_g_post, v_w_in, v_w_conv, v_a_log, v_dt_bias, v_g_onorm,
                         v_gm_ln_g, v_gm_ln_b, v_w_sp, v_b_sp, v_w_pa, v_w_pb, v_w_out)))
    xy = 2 * lax.axis_index("x") + lax.axis_index("y")
    where = jnp.stack([lax.axis_index("c"), xy, 2 * xy + lax.axis_index("c")]).astype(jnp.int32)

    shards = [a[0].astype(BF16) for a in (w_mod, w_in)]
    gathered, wconv_all = _gather_shards(shards, w_conv[0], name="gather_weights")
    own = lambda full, shard: lax.dynamic_update_slice(full, shard[None], (xy, 0, 0))
    wm_all, win_all = [own(f, s) for f, s in zip(gathered, shards)]
    wconv_all = own(wconv_all, w_conv[0])
    late_shards = [a[0].astype(BF16) for a in (w_pa, w_pb, w_out)]
    late = _gather_late_start(late_shards, wconv_all, name="gather_late_start")

    def late_weights(after):
        lands = _gather_late_wait(*late[:4], after, name="gather_late_wait")
        return [own(f, s).reshape(D, D) for f, s in zip(lands, late_shards)]
    w_conv_f = jnp.concatenate([wconv_all[s] for s in range(N_CHIPS)], axis=1)
    shard_cols = IN_COLS // N_CHIPS
    cut = OFF_A - shard_cols
    assert 0 < cut and cut + 4 * H < shard_cols
    w_qkv = jnp.concatenate([win_all[0], win_all[1][:, :cut]], axis=1)
    w_ab = jnp.pad(win_all[1][:, cut:cut + 4 * H], ((0, 0), (0, DH - 4 * H)))
    w_rest = jnp.concatenate([win_all[1][:, cut + 4 * H:], win_all[2], win_all[3]], axis=1)

    blk = D // N_CHIPS
    rest_cut = shard_cols - cut - 4 * H
    big_names = ("w_in", "w_pa", "w_pb", "w_out")
    in_flight = []

    def start_reduce(grads):
        dw_qkv, dw_ab, dw_rest = grads["w_in"]
        g_win = jnp.stack([dw_qkv[:, :shard_cols],
                           jnp.concatenate([dw_qkv[:, shard_cols:], dw_ab, dw_rest[:, :rest_cut]], axis=1),
                           dw_rest[:, rest_cut:rest_cut + shard_cols], dw_rest[:, rest_cut + shard_cols:]]).astype(BF16)
        sectioned = [g_win] + [grads[k].reshape(N_CHIPS, blk, D) for k in ("w_pa", "w_pb", "w_out")]
        handle, token = _reduce_start(sectioned, where)
        in_flight.append(handle)
        return token

    loss_local, grad_x, g = _local_step(
        x[0], c, ctx[0], c_ctx, loss_target[0], wm_all, b_mod, g_pre, g_post, w_qkv, w_ab, w_rest,
        w_conv_f, a_log[0], dt_bias[0], g_onorm, gm_ln_g, gm_ln_b, w_sp[0], b_sp[0],
        late_weights, on_large_gradients=start_reduce, run_after=late[4])
    g["loss"] = loss_local
    g["_pad"] = jnp.zeros((SMALL_LAYOUT["_pad"][1],), F32)
    g["mod_factors"] = lax.dynamic_update_slice(jnp.zeros((2 * N_CHIPS, 4 * D), F32),
                                                jnp.concatenate(g["mod_factors"])[None], (where[2], 0))
    tail = jnp.zeros((N_CHIPS * SMALL_ROWS * 128 - sum(s for _, s in SMALL_LAYOUT.values()),), F32)
    flat = jnp.concatenate([g[k].reshape(-1) for k in SMALL_LAYOUT] + [tail])
    reduced, gr_small = _reduce_finish(in_flight[0], flat.reshape(N_CHIPS, SMALL_ROWS, 128), where, g["b_mod"])
    gr_tiny = gr_small[TINY_ROW0:]

    def entry(arr, k):
        off, size = SMALL_LAYOUT[k]
        row, col = off // 128 - TINY_ROW0, off % 128
        return arr[row:row + size // 128].reshape(-1) if size >= 128 else arr[row, col:col + size]

    factors = entry(gr_tiny, "mod_factors").reshape(2 * N_CHIPS, 4 * D)
    pad_rows = lambda a: jnp.pad(a, ((0, 16 - a.shape[0]), (0, 0)))
    lhs = pad_rows(jnp.concatenate([factors[:, :D], g["silu_c_ctx"]], axis=0))
    rhs = pad_rows(jnp.concatenate([factors[:, D:], entry(gr_tiny, "dm_ctx")[None]], axis=0))
    mod_cols = 3 * D // N_CHIPS
    gr_wm = _mm(lhs, lax.dynamic_slice(rhs, (0, xy * mod_cols), (16, mod_cols)), ta=True, name="dw_mod")

    res = {k: _adamw(w[k], gr, m[k], v[k], name=f"adamw_{k}") for k, gr in zip(big_names[1:], reduced[1:])}
    flip = lambda a: jnp.swapaxes(a, 1, 2)
    res["w_in"] = [flip(r) for r in _adamw(flip(w_in), flip(reduced[0].reshape(w_in.shape)), flip(m_w_in), flip(v_w_in),
                                           name="adamw_w_in")]
    res["w_mod"] = _adamw(w_mod[0], gr_wm, m_w_mod[0], v_w_mod[0], name="adamw_w_mod")
    res["w_sp"] = _adamw(w_sp.reshape(-1, 128), gr_small, m_w_sp.reshape(-1, 128), v_w_sp.reshape(-1, 128),
                         name="adamw_w_sp")
    tiny = [k for k, (off, _) in SMALL_LAYOUT.items() if TINY_ROW0 <= off // 128 < TINY_ROW0 + TINY_ROWS]
    tiny_names = [k for k in tiny if k not in ("loss", "_pad")]

    def pack(src):
        parts = [src[k].reshape(-1) if k in tiny_names else jnp.zeros((SMALL_LAYOUT[k][1],), F32) for k in tiny]
        return jnp.concatenate(parts).reshape(TINY_ROWS, 128)

    tiny_res = _adamw(pack(w), gr_tiny[:TINY_ROWS], pack(m), pack(v), name="adamw_tiny")
    for k in tiny_names:
        res[k] = [entry(r, k) for r in tiny_res]
    g_conv = lax.dynamic_slice(entry(gr_tiny, "w_conv").reshape(3, 3 * D), (0, xy * (3 * D // N_CHIPS)),
                               (3, 3 * D // N_CHIPS))
    conv_res = _adamw(jnp.pad(w_conv[0], ((0, 5), (0, 0))), jnp.pad(g_conv, ((0, 5), (0, 0))),
                      jnp.pad(m_w_conv[0], ((0, 5), (0, 0))), jnp.pad(v_w_conv[0], ((0, 5), (0, 0))), name="adamw_w_conv")
    res["w_conv"] = [r[:3] for r in conv_res]
    res = {k: [r.reshape(w[k].shape) for r in res[k]] for k in names}

    out = [entry(gr_tiny, "loss").reshape(()), grad_x[None]]
    for i in range(4):
        out += [res[k][i] for k in names]
    return tuple(out)
```

```python
import functools

import jax
import jax.numpy as jnp
from jax import lax
from jax.experimental import pallas as pl
from jax.experimental.pallas import tpu as pltpu

F32 = jnp.float32
BF16 = jnp.bfloat16
HI = lax.Precision.HIGHEST
MESH = pl.DeviceIdType.MESH

D = 1024
H = 8
DH = 128
CH = 64
LOG_CH = 6
PAIR = 2 * CH
GM = 128
assert 1 << LOG_CH == CH and PAIR == DH
PREC_POWERS = ((lax.Precision.HIGH, lax.Precision.HIGH),) * 3 + ((None, None),) * 2
assert len(PREC_POWERS) == LOG_CH - 1
EPS = 1e-6
N_CHIPS = 4
OFF_A = 3 * D
OFF_ZB = OFF_A + 4 * H
IN_COLS = OFF_ZB + 6 * D
VMEM_LIMIT_V7X = 56 * 1024 * 1024
DMA_CHUNK_BYTES = 2 * 1024 * 1024

ADAM_LR, ADAM_B1, ADAM_B2, ADAM_EPS, ADAM_WD, ADAM_STEP = 0.001, 0.9, 0.999, 1e-08, 0.01, 10

SMALL_LAYOUT = {}
_off = 0
for _n, _s in (("w_sp", H * GM * GM), ("c_ctx", D), ("b_mod", 3 * D), ("g_pre", D), ("g_post", D), ("gm_ln_g", D),
               ("gm_ln_b", D), ("b_sp", H * GM), ("g_onorm", DH), ("a_log", 2 * H), ("dt_bias", 2 * H), ("loss", 1),
               ("_pad", 128 - 4 * H - 1 + 6 * 128), ("w_conv", 3 * 3 * D), ("dm_ctx", 3 * D), ("mod_factors", 8 * 4 * D)):
    SMALL_LAYOUT[_n] = (_off, _s)
    _off += _s
SMALL_ROWS = 368
assert N_CHIPS * SMALL_ROWS * 128 >= _off and (SMALL_ROWS // 2) % 8 == 0 and _off % 128 == 0
TINY_ROW0 = SMALL_LAYOUT["c_ctx"][0] // 128
TINY_ROWS = SMALL_LAYOUT["w_conv"][0] // 128 - TINY_ROW0
assert TINY_ROWS % 8 == 0


def _params(sem=None):
    return pltpu.CompilerParams(dimension_semantics=sem, vmem_limit_bytes=VMEM_LIMIT_V7X)


def _tile(n, cands=(256, 128, 64, 32, 16, 8)):
    if n <= cands[0]:
        return n
    for cand in cands:
        if n % cand == 0 and cand >= 64:
            return cand
    return max(d for d in range(8, cands[0], 8) if n % d == 0)


def _silu(x):
    return x * jax.nn.sigmoid(x)


def _gelu(x):
    return 0.5 * x * (1.0 + jnp.tanh(0.7978845608028654 * (x + 0.044715 * (x * x * x))))


def _mm(a, b, *, ta=False, tb=False, out_dtype=F32, tm=1024, tn=1024, tk=1024, add=None, add_from=0,
        b_sections=False, out_sections=False, after=None, name):
    m, k = (a.shape[1], a.shape[0]) if ta else a.shape
    if b_sections:
        sect = b.shape[2]
        n = b.shape[1] if tb else b.shape[0] * sect
        tn, tk = (tn, sect) if tb else (sect, tk)
    else:
        n = b.shape[0] if tb else b.shape[1]
    tm, tn, tk = min(tm, m), min(tn, n), min(tk, k)
    assert m % tm == 0 and n % tn == 0 and k % tk == 0, (name, m, n, k, tm, tn, tk)
    nk = k // tk
    dims = (((0,) if ta else (1,), (1,) if tb else (0,)), ((), ()))
    has_add = add is not None
    assert not has_add or add.shape == (m, n - add_from * tn), (name, add.shape)

    def body(*refs):
        a_ref, b_ref = refs[:2]
        o_ref = refs[2 + int(has_add) + int(after is not None)]
        acc_ref = refs[-1]
        kk = pl.program_id(2)
        bv = b_ref[0] if b_sections else b_ref[...]
        part = lax.dot_general(a_ref[...].astype(BF16), bv.astype(BF16), dims, preferred_element_type=F32)

        def finish(res):
            if has_add:
                res = res + jnp.where(pl.program_id(1) >= add_from, refs[2][...], 0.0)
            if out_sections:
                o_ref[0] = res.astype(out_dtype)
            else:
                o_ref[...] = res.astype(out_dtype)

        if nk == 1:
            finish(part)
            return

        @pl.when(kk == 0)
        def _():
            acc_ref[...] = part

        @pl.when((kk > 0) & (kk < nk - 1))
        def _():
            acc_ref[...] += part

        @pl.when(kk == nk - 1)
        def _():
            finish(acc_ref[...] + part)

    a_spec = pl.BlockSpec((tk, tm), lambda i, j, q: (q, i)) if ta else pl.BlockSpec((tm, tk), lambda i, j, q: (i, q))
    if b_sections:
        b_spec = (pl.BlockSpec((1, tn, tk), lambda i, j, q: (q, j, 0)) if tb
                  else pl.BlockSpec((1, tk, tn), lambda i, j, q: (j, q, 0)))
    else:
        b_spec = pl.BlockSpec((tn, tk), lambda i, j, q: (j, q)) if tb else pl.BlockSpec((tk, tn), lambda i, j, q: (q, j))
    add_spec = [pl.BlockSpec((tm, tn), lambda i, j, q: (i, jnp.maximum(j - add_from, 0)))] if has_add else []
    if out_sections:
        out_spec, out_shape = pl.BlockSpec((1, tm, tn), lambda i, j, q: (j, i, 0)), (n // tn, m, tn)
    else:
        out_spec, out_shape = pl.BlockSpec((tm, tn), lambda i, j, q: (i, j)), (m, n)
    return pl.pallas_call(
        body, name=name, grid=(m // tm, n // tn, nk),
        in_specs=[a_spec, b_spec] + add_spec + ([pl.BlockSpec(memory_space=pl.ANY)] if after is not None else []),
        out_specs=out_spec, out_shape=jax.ShapeDtypeStruct(out_shape, out_dtype),
        scratch_shapes=[pltpu.VMEM((tm, tn), F32)] if nk > 1 else [],
        compiler_params=_params(("parallel", "parallel", "arbitrary")),
    )(*([a, b] + ([add] if has_add else []) + ([after] if after is not None else [])))


def _h_fn(x, g, m):
    shift, scale = m[:, 0:D], m[:, D:2 * D]
    r = lax.rsqrt(jnp.mean(x * x, axis=-1, keepdims=True) + EPS)
    return (x * r * g) * (1.0 + scale) + shift


def _norm_fwd(x, g, mod, bmod, *, name):
    rows = x.shape[0]
    t = min(512, rows)

    def body(x_ref, g_ref, m_ref, b_ref, h_ref):
        h_ref[...] = _h_fn(x_ref[...], g_ref[...], m_ref[...] + b_ref[...]).astype(BF16)

    vec = lambda w: pl.BlockSpec((1, w), lambda i: (0, 0))
    return pl.pallas_call(
        body, name=name, grid=(rows // t,),
        in_specs=[pl.BlockSpec((t, D), lambda i: (i, 0)), vec(D), vec(3 * D), vec(3 * D)],
        out_specs=pl.BlockSpec((t, D), lambda i: (i, 0)),
        out_shape=jax.ShapeDtypeStruct((rows, D), BF16),
        compiler_params=_params(("parallel",)),
    )(x, g, mod, bmod)


def _norm_bwd(x, g, mod, bmod, dh_parts, resid, *, name):
    rows = x.shape[0]
    t = min(512, rows)
    n_parts = len(dh_parts)
    has_resid = resid is not None

    def body(*refs):
        x_ref, g_ref, m_ref, b_ref = refs[:4]
        parts = refs[4:4 + n_parts]
        r_ref = refs[4 + n_parts] if has_resid else None
        dx_ref, dg_ref, dm_ref = refs[-3:]
        i = pl.program_id(0)
        dh = parts[0][...]
        for p in parts[1:]:
            dh = dh + p[...]
        _, vjp = jax.vjp(_h_fn, x_ref[...], g_ref[...], m_ref[...] + b_ref[...])
        dx, dg, dm = vjp(dh)
        if has_resid:
            dx = dx + r_ref[...]
        dx_ref[...] = dx

        @pl.when(i == 0)
        def _():
            dg_ref[...] = dg
            dm_ref[...] = dm

        @pl.when(i > 0)
        def _():
            dg_ref[...] += dg
            dm_ref[...] += dm

    vec = lambda w: pl.BlockSpec((1, w), lambda i: (0, 0))
    tile = pl.BlockSpec((t, D), lambda i: (i, 0))
    ins = [x, g, mod, bmod, *dh_parts] + ([resid] if has_resid else [])
    return pl.pallas_call(
        body, name=name, grid=(rows // t,),
        in_specs=[tile, vec(D), vec(3 * D), vec(3 * D)] + [tile] * (n_parts + int(has_resid)),
        out_specs=[tile, vec(D), vec(3 * D)],
        out_shape=[jax.ShapeDtypeStruct((rows, D), F32), jax.ShapeDtypeStruct((1, D), F32),
                   jax.ShapeDtypeStruct((1, 3 * D), F32)],
        compiler_params=_params(("arbitrary",)),
    )(*ins)


def _conv_tile(u_ref, r0, t, rows, w0, w1, w2):
    u = u_ref[pl.ds(r0, t), :]
    prev8 = u_ref[pl.ds(pl.multiple_of(jnp.maximum(r0 - 8, 0), 8), 8), :]
    next8 = u_ref[pl.ds(pl.multiple_of(jnp.minimum(r0 + t, rows - 8), 8), 8), :]
    r8 = lax.broadcasted_iota(jnp.int32, (8, DH), 0)
    prev_row = jnp.sum(jnp.where(r8 == 7, prev8, 0.0), axis=0, keepdims=True)
    next_row = jnp.sum(jnp.where(r8 == 0, next8, 0.0), axis=0, keepdims=True)
    prev_row = jnp.where(r0 > 0, prev_row, 0.0)
    next_row = jnp.where(r0 + t < rows, next_row, 0.0)
    ri = lax.broadcasted_iota(jnp.int32, (t, DH), 0)
    um1 = jnp.where(ri == 0, prev_row, pltpu.roll(u, 1, 0))
    up1 = jnp.where(ri == t - 1, next_row, pltpu.roll(u, t - 1, 0))
    return w0 * um1 + w1 * u + w2 * up1, um1, u, up1


def _rowlocal(z, norm):
    y = _silu(z)
    return y * lax.rsqrt(jnp.sum(y * y, axis=-1, keepdims=True) + EPS) if norm else y


def _prep_fwd(p, wconv, n_norm, *, name):
    rows, nb = p.shape[0], p.shape[1] // DH
    t = min(512, rows)
    out = None
    for norm, b0, b1 in ((True, 0, n_norm), (False, n_norm, nb)):
        def body(u_ref, w_ref, *rest, norm=norm):
            o_ref = rest[-1]
            w0, w1, w2 = w_ref[0:1, :], w_ref[1:2, :], w_ref[2:3, :]

            def step(s, carry):
                r0 = pl.multiple_of(s * t, t)
                z, _, _, _ = _conv_tile(u_ref, r0, t, rows, w0, w1, w2)
                o_ref[0, pl.ds(r0, t), :] = _rowlocal(z, norm)
                return carry

            lax.fori_loop(0, rows // t, step, 0)

        held = [] if out is None else [out]
        out = pl.pallas_call(
            body, name=f"{name}_{'norm' if norm else 'plain'}", grid=(b1 - b0,),
            in_specs=[pl.BlockSpec((rows, DH), lambda j, b0=b0: (0, b0 + j)),
                      pl.BlockSpec((3, DH), lambda j, b0=b0: (0, b0 + j))] + [ANY] * len(held),
            out_specs=pl.BlockSpec((1, rows, DH), lambda j, b0=b0: (b0 + j, 0, 0)),
            out_shape=jax.ShapeDtypeStruct((nb, rows, DH), F32),
            input_output_aliases={2: 0} if held else {},
            compiler_params=_params(("parallel",)),
        )(p, wconv, *held)
    return out


def _prep_bwd(p, wconv, d_a, d_b, n_norm, *, name):
    rows, nb = p.shape[0], p.shape[1] // DH
    t = min(512, rows)
    outs = []
    for norm, b0, b1 in ((True, 0, n_norm), (False, n_norm, nb)):
        def body(u_ref, w_ref, da_ref, db_ref, *rest, norm=norm):
            du_ref, dw_ref, dz_ref = rest[-3:]
            w0, w1, w2 = w_ref[0:1, :], w_ref[1:2, :], w_ref[2:3, :]

            def step1(s, carry):
                a0, a1, a2 = carry
                r0 = pl.multiple_of(s * t, t)
                z, um1, u, up1 = _conv_tile(u_ref, r0, t, rows, w0, w1, w2)
                _, vjp = jax.vjp(lambda zz: _rowlocal(zz, norm), z)
                (dz,) = vjp(da_ref[0, pl.ds(r0, t), :] + db_ref[0, pl.ds(r0, t), :])
                dz_ref[pl.ds(r0, t), :] = dz
                red = lambda v: jnp.sum(v, axis=0, keepdims=True)
                return a0 + red(dz * um1), a1 + red(dz * u), a2 + red(dz * up1)

            zero = jnp.zeros((1, DH), F32)
            a0, a1, a2 = lax.fori_loop(0, rows // t, step1, (zero, zero, zero))
            dw_ref[0:1, :] = a0
            dw_ref[1:2, :] = a1
            dw_ref[2:3, :] = a2

            def step2(s, carry):
                r0 = pl.multiple_of(s * t, t)
                du, _, _, _ = _conv_tile(dz_ref, r0, t, rows, w2, w1, w0)
                du_ref[pl.ds(r0, t), :] = du.astype(BF16)
                return carry

            lax.fori_loop(0, rows // t, step2, 0)

        col = pl.BlockSpec((rows, DH), lambda j, b0=b0: (0, b0 + j))
        w_spec = pl.BlockSpec((3, DH), lambda j, b0=b0: (0, b0 + j))
        d_spec = pl.BlockSpec((1, rows, DH), lambda j, b0=b0: (b0 + j, 0, 0))
        outs = pl.pallas_call(
            body, name=f"{name}_{'norm' if norm else 'plain'}", grid=(b1 - b0,),
            in_specs=[col, w_spec, d_spec, d_spec] + [ANY] * len(outs), out_specs=[col, w_spec],
            out_shape=[jax.ShapeDtypeStruct((rows, nb * DH), BF16), jax.ShapeDtypeStruct((3, nb * DH), F32)],
            input_output_aliases={4: 0, 5: 1} if outs else {},
            scratch_shapes=[pltpu.VMEM((rows, DH), F32)],
            compiler_params=_params(("parallel",)),
        )(p, wconv, d_a, d_b, *outs)
    return outs


def _gates_fn(pab, avec, dvec):
    t = pab.shape[0]
    lane = lax.broadcasted_iota(jnp.int32, pab.shape, 1)
    xg = pab + dvec
    sp = jnp.maximum(xg, 0.0) + jnp.log(1.0 + jnp.exp(-jnp.abs(xg)))
    g = jnp.where(lane < 2 * H, -jnp.exp(avec) * sp, 0.0)
    ii = lax.broadcasted_iota(jnp.int32, (t, t), 0)
    jj = lax.broadcasted_iota(jnp.int32, (t, t), 1)
    same = (ii >> LOG_CH) == (jj >> LOG_CH)
    cum_f = _dot(jnp.where(same & (jj <= ii), 1.0, 0.0), g, precision=HI)
    cum_r = _dot(jnp.where(same & (jj >= ii), 1.0, 0.0), g, precision=HI)
    return jnp.where(lane < H, cum_f, jnp.where(lane < 2 * H, cum_r, jnp.where(lane < 4 * H, jax.nn.sigmoid(pab), 0.0)))


def _gates_fwd(pab, avec, dvec, *, name):
    rows = pab.shape[0]
    t = min(2 * PAIR, rows)

    def body(p_ref, a_ref, d_ref, o_ref):
        o_ref[...] = _gates_fn(p_ref[...], a_ref[...], d_ref[...])

    vec = pl.BlockSpec((1, DH), lambda i: (0, 0))
    tile = pl.BlockSpec((t, DH), lambda i: (i, 0))
    return pl.pallas_call(
        body, name=name, grid=(rows // t,), in_specs=[tile, vec, vec], out_specs=tile,
        out_shape=jax.ShapeDtypeStruct((rows, DH), F32), compiler_params=_params(("parallel",)),
    )(pab, avec, dvec)


def _gates_bwd(pab, avec, dvec, d_a, d_b, *, name):
    rows = pab.shape[0]
    t = min(2 * PAIR, rows)

    def body(p_ref, a_ref, d_ref, da_ref, db_ref, dp_ref, dav_ref, ddv_ref):
        i = pl.program_id(0)
        _, vjp = jax.vjp(_gates_fn, p_ref[...], a_ref[...], d_ref[...])
        dp, dav, ddv = vjp(da_ref[...] + db_ref[...])
        dp_ref[...] = dp.astype(BF16)

        @pl.when(i == 0)
        def _():
            dav_ref[...] = dav
            ddv_ref[...] = ddv

        @pl.when(i > 0)
        def _():
            dav_ref[...] += dav
            ddv_ref[...] += ddv

    vec = pl.BlockSpec((1, DH), lambda i: (0, 0))
    tile = pl.BlockSpec((t, DH), lambda i: (i, 0))
    return pl.pallas_call(
        body, name=name, grid=(rows // t,), in_specs=[tile, vec, vec, tile, tile], out_specs=[tile, vec, vec],
        out_shape=[jax.ShapeDtypeStruct((rows, DH), BF16), jax.ShapeDtypeStruct((1, DH), F32),
                   jax.ShapeDtypeStruct((1, DH), F32)],
        compiler_params=_params(("arbitrary",)),
    )(pab, avec, dvec, d_a, d_b)


def _dot_general(a, b, dims, precision=None):
    return lax.dot_general(a, b, (dims, ((), ())), precision=precision, preferred_element_type=F32)


@functools.partial(jax.custom_vjp, nondiff_argnums=(2,))
def _dot_bf16(a, b, dims):
    return _dot_general(a.astype(BF16), b.astype(BF16), dims)


def _dot_bf16_fwd(a, b, dims):
    return _dot_bf16(a, b, dims), (a, b)


def _dot_bf16_bwd(dims, res, g):
    a, b = res
    (ca,), (cb,) = dims
    da = _dot_bf16(g, b, ((1,), (1 - cb,))) if ca == 1 else _dot_bf16(b, g, ((1 - cb,), (1,)))
    db = _dot_bf16(a, g, ((1 - ca,), (0,))) if cb == 0 else _dot_bf16(g, a, ((0,), (1 - ca,)))
    return da, db


_dot_bf16.defvjp(_dot_bf16_fwd, _dot_bf16_bwd)


def _dot(a, b, dims=((1,), (0,)), precision=None):
    if precision is None and a.dtype == F32 and b.dtype == F32:
        return _dot_bf16(a, b, dims)
    return _dot_general(a, b, dims, precision)


_NT = ((1,), (1,))
_TN = ((0,), (0,))


@jax.custom_vjp
def _saved_inverse(neg_a, inv):
    return inv


def _saved_inverse_fwd(neg_a, inv):
    return inv, inv


def _saved_inverse_bwd(inv, d_inv):
    idx = range(len(inv))
    left = [_dot(inv[i], d_inv[i], _TN) for i in idx]
    d_neg_a = [_dot(left[i], inv[i], _NT) for i in idx]
    return d_neg_a, [jnp.zeros_like(t) for t in inv]


_saved_inverse.defvjp(_saved_inverse_fwd, _saved_inverse_bwd)


def _pairs(s, q, k, v, gcol, bcol, revs, inv_saved=None):
    idx = range(len(revs))
    ii = lax.broadcasted_iota(jnp.int32, (PAIR, PAIR), 0)
    jj = lax.broadcasted_iota(jnp.int32, (PAIR, PAIR), 1)
    same = (ii >> LOG_CH) == (jj >> LOG_CH)
    incl_d = (same & (ii >= jj), same & (ii <= jj))
    strict_d = (same & (ii > jj), same & (ii < jj))
    incl = [incl_d[int(r)] for r in revs]
    strict = [strict_d[int(r)] for r in revs]
    eye = jnp.where(ii == jj, 1.0, 0.0)
    gc_i = [jnp.broadcast_to(gcol[i], (PAIR, DH)) for i in idx]
    gc_j = [gc_i[i].T for i in idx]
    decay = [jnp.where(incl[i], jnp.exp(jnp.where(incl[i], gc_i[i] - gc_j[i], 0.0)), 0.0) for i in idx]
    b_b = [jnp.broadcast_to(bcol[i], (PAIR, DH)) for i in idx]
    kb = [k[i] * b_b[i] for i in idx]
    kk = [_dot(kb[i], k[i], _NT) for i in idx]
    bp = [jnp.where(strict[i], -kk[i] * decay[i], 0.0) for i in idx]
    if inv_saved is not None:
        inv = _saved_inverse(bp, inv_saved)
    else:
        low = bp
        for prec_sq, prec_acc in PREC_POWERS:
            bp = [_dot(bp[i], bp[i], precision=prec_sq) for i in idx]
            more = [_dot(low[i], bp[i], precision=prec_acc) for i in idx]
            low = [low[i] + bp[i] + more[i] for i in idx]
        inv = [eye + low[i] for i in idx]
    eg = [jnp.exp(gc_i[i]) for i in idx]
    sol = [_dot(inv[i], jnp.concatenate([v[i] * b_b[i], kb[i] * eg[i]], axis=1)) for i in idx]
    u_val = [sol[i][:, :DH] for i in idx]
    w_key = [sol[i][:, DH:] for i in idx]
    row = lax.broadcasted_iota(jnp.int32, (PAIR, 1), 0)
    has_q = q[0] is not None
    if has_q:
        qc = [q[i] * (DH ** -0.5) for i in idx]
        qk = [_dot(qc[i], k[i], _NT) for i in idx]
        attn = [qk[i] * decay[i] for i in idx]
        qd = [qc[i] * eg[i] for i in idx]
    outs = [[None, None] for _ in idx]
    zeros = jnp.zeros((CH, DH), F32)
    for step in range(2):
        cidx = [(1 - step) if revs[i] else step for i in idx]
        sl = [slice(c * CH, (c + 1) * CH) for c in cidx]
        last = [c * CH if revs[i] else c * CH + CH - 1 for i, c in zip(idx, cidx)]
        gl = [jnp.sum(jnp.where(row == last[i], gcol[i], 0.0), axis=0, keepdims=True) for i in idx]
        k_tail = [k[i][sl[i]] * jnp.exp(gl[i] - gc_i[i][sl[i]]) for i in idx]
        ws = [_dot(w_key[i][sl[i]], s[i]) for i in idx]
        v_new = [u_val[i][sl[i]] - ws[i] for i in idx]
        if has_q:
            v_pad = [jnp.concatenate([v_new[i], zeros] if cidx[i] == 0 else [zeros, v_new[i]], axis=0) for i in idx]
            o_state = [_dot(qd[i][sl[i]], s[i]) for i in idx]
            o_local = [_dot(attn[i][sl[i]], v_pad[i]) for i in idx]
            for i in idx:
                outs[i][cidx[i]] = o_state[i] + o_local[i]
        kv = [_dot(k_tail[i], v_new[i], _TN) for i in idx]
        s = [s[i] * jnp.exp(gl[i]) + kv[i] for i in idx]
    return s, ([jnp.concatenate(outs[i], axis=0) for i in idx] if has_q else None), inv


def _lane_col(tile, idx):
    lane = lax.broadcasted_iota(jnp.int32, tile.shape, 1)
    return jnp.sum(jnp.where(lane == idx, tile, 0.0), axis=1, keepdims=True)


def _gdn_fwd(qkv, gb, s0f, s0b, has_q, *, name):
    nb, rows, _ = qkv.shape
    n = rows // PAIR
    qoff = H if has_q else 0

    def body(qf_ref, qb_ref, gf_ref, gr_ref, s0f_ref, s0b_ref, of_ref, ob_ref, ssf_ref, ssb_ref, tsf_ref, tsb_ref,
             sf_ref, sb_ref):
        @pl.when(pl.program_id(0) == 0)
        def _():
            sf_ref[...] = s0f_ref[...]
            sb_ref[...] = s0b_ref[...]

        gtiles = (gf_ref[...], gr_ref[...])

        dirs = ((qf_ref, sf_ref, ssf_ref, of_ref), (qb_ref, sb_ref, ssb_ref, ob_ref))
        ts_refs = (tsf_ref, tsb_ref)

        work = [(h, d) for h in range(H) for d in range(2)]
        loaded = []
        for h, d in work:
            q_ref, s_ref, _, _ = dirs[d]
            loaded.append((s_ref[h], q_ref[h] if has_q else None, q_ref[qoff + h], q_ref[qoff + H + h],
                           _lane_col(gtiles[d], d * H + h), _lane_col(gtiles[d], 2 * H + d * H + h)))
        s_new, o, inv = _pairs(*[list(col) for col in zip(*loaded)], revs=[d == 1 for _, d in work])
        for i, (h, d) in enumerate(work):
            _, s_ref, ss_ref, o_ref = dirs[d]
            ss_ref[0, h] = loaded[i][0]
            ts_refs[d][0, h] = inv[i]
            s_ref[h] = s_new[i]
            o_ref[h] = o[i] if has_q else jnp.zeros((PAIR, DH), F32)

    fwd3 = lambda i: (0, i, 0)
    rev3 = lambda i: (0, n - 1 - i, 0)
    state = pl.BlockSpec((H, DH, DH), lambda i: (0, 0, 0))
    saved = pl.BlockSpec((1, H, DH, DH), lambda i: (i, 0, 0, 0))
    return pl.pallas_call(
        body, name=name, grid=(n,),
        in_specs=[pl.BlockSpec((nb, PAIR, DH), fwd3), pl.BlockSpec((nb, PAIR, DH), rev3),
                  pl.BlockSpec((PAIR, DH), lambda i: (i, 0)), pl.BlockSpec((PAIR, DH), lambda i: (n - 1 - i, 0)),
                  state, state],
        out_specs=[pl.BlockSpec((H, PAIR, DH), fwd3), pl.BlockSpec((H, PAIR, DH), rev3), saved, saved, saved, saved,
                   state, state],
        out_shape=[jax.ShapeDtypeStruct((H, rows, DH), F32)] * 2 + [jax.ShapeDtypeStruct((n, H, DH, DH), F32)] * 4
        + [jax.ShapeDtypeStruct((H, DH, DH), F32)] * 2,
        compiler_params=_params(("arbitrary",)),
    )(qkv, qkv, gb, gb, s0f, s0b)


def _gdn_bwd(qkv, gb, ssf, ssb, tsf, tsb, do, dsf, dsb, has_q, *, name):
    nb, rows, _ = qkv.shape
    n = rows // PAIR
    qoff = H if has_q else 0

    def body(qf_ref, qb_ref, gf_ref, gr_ref, ssf_ref, ssb_ref, tsf_ref, tsb_ref, dof_ref, dob_ref, dsf0_ref, dsb0_ref,
             dqf_ref, dqb_ref, dgf_ref, dgr_ref, dsf_ref, dsb_ref):
        ts_refs = (tsf_ref, tsb_ref)
        @pl.when(pl.program_id(0) == 0)
        def _():
            dsf_ref[...] = dsf0_ref[...]
            dsb_ref[...] = dsb0_ref[...]

        gtiles = (gf_ref[...], gr_ref[...])
        lane = lax.broadcasted_iota(jnp.int32, (PAIR, DH), 1)

        dirs = ((qf_ref, ssf_ref, dof_ref, dsf_ref, dqf_ref), (qb_ref, ssb_ref, dob_ref, dsb_ref, dqb_ref))

        work = [(h, d) for h in range(H) for d in range(2)]
        revs = [d == 1 for _, d in work]
        s_in, q_in, k_in, v_in, g_in, b_in, ds_out, do_out, inv_in = [], [], [], [], [], [], [], [], []
        for h, d in work:
            q_ref, ss_ref, do_ref, ds_ref, _ = dirs[d]
            s_in.append(ss_ref[0, h])
            inv_in.append(ts_refs[d][0, h])
            q_in.append(q_ref[h] if has_q else None)
            k_in.append(q_ref[qoff + h])
            v_in.append(q_ref[qoff + H + h])
            g_in.append(_lane_col(gtiles[d], d * H + h))
            b_in.append(_lane_col(gtiles[d], 2 * H + d * H + h))
            ds_out.append(ds_ref[h])
            do_out.append(do_ref[h] if has_q else None)
        if has_q:
            _, vjp = jax.vjp(lambda s_, q_, k_, v_, g_, b_: _pairs(s_, q_, k_, v_, g_, b_, revs, inv_in)[:2],
                             s_in, q_in, k_in, v_in, g_in, b_in)
            ds, dq, dk, dv, dg, db = vjp((ds_out, do_out))
        else:
            _, vjp = jax.vjp(lambda s_, k_, v_, g_, b_: _pairs(s_, q_in, k_, v_, g_, b_, revs, inv_in)[0],
                             s_in, k_in, v_in, g_in, b_in)
            ds, dk, dv, dg, db = vjp(ds_out)
        dgb = [jnp.zeros((PAIR, DH), F32)] * 2
        for i, (h, d) in enumerate(work):
            dq_ref, ds_ref = dirs[d][4], dirs[d][3]
            ds_ref[h] = ds[i]
            if has_q:
                dq_ref[h] = dq[i]
            dq_ref[qoff + h], dq_ref[qoff + H + h] = dk[i], dv[i]
            dgb[d] = (dgb[d] + jnp.where(lane == d * H + h, dg[i], 0.0)
                      + jnp.where(lane == 2 * H + d * H + h, db[i], 0.0))
        dgf_ref[...] = dgb[0]
        dgr_ref[...] = dgb[1]

    fwd3 = lambda i: (0, n - 1 - i, 0)
    rev3 = lambda i: (0, i, 0)
    state = pl.BlockSpec((H, DH, DH), lambda i: (0, 0, 0))
    saved = pl.BlockSpec((1, H, DH, DH), lambda i: (n - 1 - i, 0, 0, 0))
    gf_spec = pl.BlockSpec((PAIR, DH), lambda i: (n - 1 - i, 0))
    gr_spec = pl.BlockSpec((PAIR, DH), lambda i: (i, 0))
    return pl.pallas_call(
        body, name=name, grid=(n,),
        in_specs=[pl.BlockSpec((nb, PAIR, DH), fwd3), pl.BlockSpec((nb, PAIR, DH), rev3), gf_spec, gr_spec,
                  saved, saved, saved, saved,
                  pl.BlockSpec((H, PAIR, DH), fwd3), pl.BlockSpec((H, PAIR, DH), rev3), state, state],
        out_specs=[pl.BlockSpec((nb, PAIR, DH), fwd3), pl.BlockSpec((nb, PAIR, DH), rev3), gf_spec, gr_spec, state, state],
        out_shape=[jax.ShapeDtypeStruct((nb, rows, DH), F32)] * 2 + [jax.ShapeDtypeStruct((rows, DH), F32)] * 2
        + [jax.ShapeDtypeStruct((H, DH, DH), F32)] * 2,
        compiler_params=_params(("arbitrary",)),
    )(qkv, qkv, gb, gb, ssf, ssb, tsf, tsb, do, do, dsf, dsb)


def _stage1(zb, ua, va, za, o, gon, lng, lnb, wsp, bsp):
    gv = [_gelu(t) for t in va]
    mu = sum(jnp.sum(t, axis=-1, keepdims=True) for t in gv) * (1.0 / D)
    xc = [t - mu for t in gv]
    var = sum(jnp.sum(t * t, axis=-1, keepdims=True) for t in xc) * (1.0 / D)
    rs = lax.rsqrt(var + EPS)
    ya, yb = [], []
    for g in range(H):
        vv = xc[g] * rs * lng[g] + lnb[g]
        s = _dot(wsp[g], vv) + bsp[g]
        ya.append(_gelu(ua[g]) * s * _silu(za[g]))
        r = lax.rsqrt(jnp.mean(o[g] * o[g], axis=-1, keepdims=True) + EPS)
        yb.append(o[g] * r * gon * _silu(zb[g]))
    return ya, yb


def _stage2(ma, mb, ga, gb):
    return jax.nn.sigmoid(ga) * ma + jax.nn.sigmoid(gb) * mb


def _stage3(out, x, tgt, gpost, gate):
    r = out * lax.rsqrt(jnp.mean(out * out, axis=-1, keepdims=True) + EPS) * gpost
    err = x + gate * r - tgt
    return 0.5 * jnp.sum(jnp.mean(err * err, axis=-1, keepdims=True), axis=0, keepdims=True)


def _post(p_rest, o_f, o_b, x, tgt, mod, bmod, gon, lng, lnb, wsp, bspb, wpa, wpb, wout, gpost, *, name):
    rows = x.shape[0]
    n = rows // GM
    lanes = lambda g: slice(g * DH, (g + 1) * DH)
    bdot = lambda a, w_ref: _dot(a.astype(BF16), w_ref[...])
    bdot_t = lambda a, w_ref: _dot(a.astype(BF16), w_ref[...], _NT)

    def body(p_ref, of_ref, ob_ref, x_ref, t_ref, m_ref, bm_ref, gon_ref, lng_ref, lnb_ref, wsp_ref, bsp_ref,
             wpa_ref, wpb_ref, wout_ref, gp_ref,
             loss_ref, dp_ref, do_ref, dx_ref, ya_ref, yb_ref, mg_ref, dma_ref, dmb_ref, dout_ref,
             dvec_ref, dgon_ref, dwsp_ref, dbsp_ref):
        @pl.when(pl.program_id(0) == 0)
        def _():
            loss_ref[...] = jnp.zeros_like(loss_ref)
            dvec_ref[...] = jnp.zeros_like(dvec_ref)
            dgon_ref[...] = jnp.zeros_like(dgon_ref)
            dwsp_ref[...] = jnp.zeros_like(dwsp_ref)
            dbsp_ref[...] = jnp.zeros_like(dbsp_ref)

        piece = lambda blk: [p_ref[:, blk * D + g * DH: blk * D + (g + 1) * DH].astype(F32) for g in range(H)]
        zb, ua, va, za = piece(0), piece(1), piece(2), piece(3)
        o = [of_ref[g] + ob_ref[g] for g in range(H)]
        gon = gon_ref[...]
        lng = [lng_ref[:, lanes(g)] for g in range(H)]
        lnb = [lnb_ref[:, lanes(g)] for g in range(H)]
        wsp = [wsp_ref[g] for g in range(H)]
        bsp = [bsp_ref[g] for g in range(H)]
        (ya, yb), vjp1 = jax.vjp(_stage1, zb, ua, va, za, o, gon, lng, lnb, wsp, bsp)
        y_a, y_b = jnp.concatenate(ya, axis=1), jnp.concatenate(yb, axis=1)
        ma, mb = bdot(y_a, wpa_ref), bdot(y_b, wpb_ref)
        ga, gb = p_ref[:, 4 * D:5 * D].astype(F32), p_ref[:, 5 * D:6 * D].astype(F32)
        merged, vjp2 = jax.vjp(_stage2, ma, mb, ga, gb)
        out = bdot(merged, wout_ref)
        gate = m_ref[:, 2 * D:3 * D] + bm_ref[:, 2 * D:3 * D]
        loss, vjp3 = jax.vjp(_stage3, out, x_ref[...], t_ref[...], gp_ref[...], gate)
        loss_ref[...] += jnp.broadcast_to(loss, loss_ref.shape)

        dout, dx, _, dgpost, dgate = vjp3(jnp.ones((1, 1), F32))
        dx_ref[...] = dx
        dmerged = bdot_t(dout, wout_ref)
        dma, dmb, dga, dgb = vjp2(dmerged)
        dya, dyb = bdot_t(dma, wpa_ref), bdot_t(dmb, wpb_ref)
        dzb, dua, dva, dza, do, dgon, dlng, dlnb, dwsp, dbsp = vjp1(
            ([dya[:, lanes(g)] for g in range(H)], [dyb[:, lanes(g)] for g in range(H)]))

        for blk, dlist in enumerate((dzb, dua, dva, dza)):
            for g in range(H):
                dp_ref[:, blk * D + g * DH: blk * D + (g + 1) * DH] = dlist[g].astype(BF16)
        dp_ref[:, 4 * D:5 * D] = dga.astype(BF16)
        dp_ref[:, 5 * D:6 * D] = dgb.astype(BF16)
        for g in range(H):
            do_ref[g] = do[g]
            dwsp_ref[g] += dwsp[g]
            dbsp_ref[g] += dbsp[g]
            dvec_ref[2:3, lanes(g)] += dlng[g]
            dvec_ref[3:4, lanes(g)] += dlnb[g]
        dvec_ref[0:1, :] += dgpost
        dvec_ref[1:2, :] += dgate
        dgon_ref[0:1, :] += dgon
        ya_ref[...] = y_a.astype(BF16)
        yb_ref[...] = y_b.astype(BF16)
        mg_ref[...] = merged.astype(BF16)
        dma_ref[...] = dma.astype(BF16)
        dmb_ref[...] = dmb.astype(BF16)
        dout_ref[...] = dout.astype(BF16)

    row = lambda w: pl.BlockSpec((GM, w), lambda i: (i, 0))
    heads = pl.BlockSpec((H, GM, DH), lambda i: (0, i, 0))
    full = lambda shape: pl.BlockSpec(shape, lambda i: tuple(0 for _ in shape))
    sds = jax.ShapeDtypeStruct
    return pl.pallas_call(
        body, name=name, grid=(n,),
        in_specs=[row(6 * D), heads, heads, row(D), row(D), full((1, 3 * D)), full((1, 3 * D)), full((1, DH)),
                  full((1, D)), full((1, D)), full((H, GM, GM)), full((H, GM, GM)),
                  full((D, D)), full((D, D)), full((D, D)), full((1, D))],
        out_specs=[full((8, DH)), row(6 * D), heads, row(D)] + [row(D)] * 6
        + [full((8, D)), full((8, DH)), full((H, GM, GM)), full((H, GM, GM))],
        out_shape=[sds((8, DH), F32), sds((rows, 6 * D), BF16), sds((H, rows, DH), F32), sds((rows, D), F32)]
        + [sds((rows, D), BF16)] * 6 + [sds((8, D), F32), sds((8, DH), F32), sds((H, GM, GM), F32), sds((H, GM, GM), F32)],
        compiler_params=_params(("arbitrary",)),
    )(p_rest, o_f, o_b, x, tgt, mod, bmod, gon, lng, lnb, wsp, bspb, wpa, wpb, wout, gpost)


def _silu_rows(c, *, name):
    def body(c_ref, o_ref):
        o_ref[...] = _silu(c_ref[...])

    return pl.pallas_call(body, name=name, out_shape=jax.ShapeDtypeStruct(c.shape, F32))(c)


def _dsilu_mul(c, d, *, name):
    def body(c_ref, d_ref, o_ref):
        _, vjp = jax.vjp(_silu, c_ref[...])
        (o_ref[...],) = vjp(d_ref[...])

    return pl.pallas_call(body, name=name, out_shape=jax.ShapeDtypeStruct(c.shape, F32))(c, d)


def _adamw(w, g, m, v, *, name):
    rows, cols = w.shape[-2:]
    t = _tile(rows if g.ndim == 2 else g.shape[1])
    c1 = 1.0 / (1.0 - ADAM_B1 ** ADAM_STEP)
    c2 = 1.0 / (1.0 - ADAM_B2 ** ADAM_STEP)

    def body(w_ref, g_ref, m_ref, v_ref, go_ref, d_ref, mo_ref, vo_ref):
        blk = lambda r: r[...].reshape(t, cols)
        gv = blk(g_ref)
        mn = ADAM_B1 * blk(m_ref) + (1.0 - ADAM_B1) * gv
        vn = ADAM_B2 * blk(v_ref) + (1.0 - ADAM_B2) * (gv * gv)
        delta = -ADAM_LR * ((mn * c1) / (jnp.sqrt(vn * c2) + ADAM_EPS) + ADAM_WD * blk(w_ref))
        for ref, val in ((go_ref, gv), (d_ref, delta), (mo_ref, mn), (vo_ref, vn)):
            ref[...] = val.reshape(ref.shape)

    tile = (pl.BlockSpec((1, t, cols), lambda i: (0, i, 0)) if w.ndim == 3 else pl.BlockSpec((t, cols), lambda i: (i, 0)))
    if g.ndim == 3:
        per = g.shape[1] // t
        g_tile = pl.BlockSpec((1, t, cols), lambda i: (i // per, i % per, 0))
    else:
        g_tile = pl.BlockSpec((t, cols), lambda i: (i, 0))
    return pl.pallas_call(
        body, name=name, grid=(rows // t,),
        in_specs=[tile, g_tile, tile, tile], out_specs=[tile] * 4,
        out_shape=[jax.ShapeDtypeStruct(w.shape, F32)] * 4,
        compiler_params=_params(("parallel",)),
    )(w, g, m, v)


def _add_sibling(full, got, where, out_dtype, *, name):
    s, rows, cols = full.shape
    hr = rows // 2
    t = _tile(hr)
    nt = hr // t

    def body(w_ref, a_ref, b_ref, o_ref):
        o_ref[...] = (a_ref[...].astype(F32) + b_ref[...].astype(F32)).astype(out_dtype)

    tile = pl.BlockSpec((1, t, cols), lambda j, i, w: (j, i, 0))
    return pl.pallas_call(
        body, name=name,
        grid_spec=pltpu.PrefetchScalarGridSpec(
            num_scalar_prefetch=1, grid=(s, nt),
            in_specs=[pl.BlockSpec((1, t, cols), lambda j, i, w: (j, w[0] * nt + i, 0)), tile], out_specs=tile),
        out_shape=jax.ShapeDtypeStruct((s, hr, cols), out_dtype), compiler_params=_params(("parallel", "parallel")),
    )(where, full, got)


def _sum_chips(own, slots, where, *, name):
    _, hr, cols = own.shape
    t = _tile(hr)

    def body(w_ref, a_ref, s_ref, o_ref):
        f = lambda v: v.astype(F32)
        o_ref[0] = ((f(a_ref[0]) + f(s_ref[0])) + f(s_ref[1])) + f(s_ref[2])

    return pl.pallas_call(
        body, name=name,
        grid_spec=pltpu.PrefetchScalarGridSpec(
            num_scalar_prefetch=1, grid=(hr // t,),
            in_specs=[pl.BlockSpec((1, t, cols), lambda i, w: (w[1], i, 0)),
                      pl.BlockSpec((N_CHIPS - 1, t, cols), lambda i, w: (0, i, 0))],
            out_specs=pl.BlockSpec((1, t, cols), lambda i, w: (w[0], i, 0))),
        out_shape=jax.ShapeDtypeStruct((2, hr, cols), F32), compiler_params=_params(("parallel",)),
    )(where, own, slots)


def _local_step(x, c, ctx, c_ctx, tgt, w_mod, b_mod, g_pre, g_post, w_qkv, w_ab, w_rest, w_conv, a_log, dt_bias,
                g_onorm, gm_ln_g, gm_ln_b, w_sp, b_sp, late_weights, on_large_gradients=None, run_after=None):
    rows, rows_c = x.shape[0], ctx.shape[0]
    cc = jnp.zeros((16, D), F32).at[0].set(c[0]).at[1].set(c_ctx)
    scc = _silu_rows(cc, name="silu_cond")
    mod = _mm(scc, w_mod, b_sections=True, after=run_after, name="mod_fwd")
    mod_x, mod_c = mod[0:1], mod[1:2]
    avec = jnp.zeros((1, DH), F32).at[0, :2 * H].set(a_log.reshape(-1))
    dvec = jnp.zeros((1, DH), F32).at[0, :2 * H].set(dt_bias.reshape(-1))
    bspb = jnp.broadcast_to(b_sp[:, :, None], (H, GM, GM))
    w_kv, wconv_kv = w_qkv[:, D:], w_conv[:, D:]

    h_c = _norm_fwd(ctx, g_pre, mod_c, b_mod, name="norm_fwd_ctx")
    pc_kv = _mm(h_c, w_kv, name="inproj_ctx_kv")
    pc_ab = _mm(h_c, w_ab, name="inproj_ctx_ab")
    kv_c = _prep_fwd(pc_kv, wconv_kv, H, name="prep_fwd_ctx")
    gb_c = _gates_fwd(pc_ab, avec, dvec, name="gates_fwd_ctx")
    s_zero = jnp.zeros((H, DH, DH), F32)
    _, _, ssf_c, ssb_c, tsf_c, tsb_c, s_f, s_b = _gdn_fwd(kv_c, gb_c, s_zero, s_zero, False, name="gdn_fwd_ctx")

    h_x = _norm_fwd(x, g_pre, mod_x, b_mod, name="norm_fwd_x")
    p_qkv = _mm(h_x, w_qkv, name="inproj_qkv")
    p_ab = _mm(h_x, w_ab, name="inproj_ab")
    p_rest = _mm(h_x, w_rest, out_dtype=BF16, name="inproj_rest")
    qkv = _prep_fwd(p_qkv, w_conv, 2 * H, name="prep_fwd_x")
    gb_x = _gates_fwd(p_ab, avec, dvec, name="gates_fwd_x")
    o_f, o_b, ssf, ssb, tsf, tsb, _, _ = _gdn_fwd(qkv, gb_x, s_f, s_b, True, name="gdn_fwd_x")

    w_pa, w_pb, w_out = late_weights(o_f)
    (loss_acc, dp_rest, do, dx_res, ya, yb, mg, dma, dmb, dout, dvec_post, dgon, dwsp, dbspb) = _post(
        p_rest, o_f, o_b, x, tgt, mod_x, b_mod, g_onorm, gm_ln_g, gm_ln_b, w_sp, bspb, w_pa, w_pb, w_out, g_post,
        name="post")
    g = {}
    g["w_pa"] = _mm(ya, dma, ta=True, out_dtype=BF16, name="dw_pa")
    g["w_pb"] = _mm(yb, dmb, ta=True, out_dtype=BF16, name="dw_pb")
    g["w_out"] = _mm(mg, dout, ta=True, out_dtype=BF16, name="dw_out")

    zeros_s = jnp.zeros((H, DH, DH), F32)
    dq_f, dq_b, dg_f, dg_b, ds0_f, ds0_b = _gdn_bwd(qkv, gb_x, ssf, ssb, tsf, tsb, do, zeros_s, zeros_s, True,
                                                    name="gdn_bwd_x")
    dp_qkv, dwc_x = _prep_bwd(p_qkv, w_conv, dq_f, dq_b, 2 * H, name="prep_bwd_x")
    dp_ab, dav_x, ddv_x = _gates_bwd(p_ab, avec, dvec, dg_f, dg_b, name="gates_bwd_x")
    dkv_f, dkv_b, dgc_f, dgc_b, _, _ = _gdn_bwd(kv_c, gb_c, ssf_c, ssb_c, tsf_c, tsb_c, jnp.zeros((H, rows_c, DH), F32),
                                                 ds0_f, ds0_b, False, name="gdn_bwd_ctx")
    dpc_kv, dwc_c = _prep_bwd(pc_kv, wconv_kv, dkv_f, dkv_b, H, name="prep_bwd_ctx")
    dpc_ab, dav_c, ddv_c = _gates_bwd(pc_ab, avec, dvec, dgc_f, dgc_b, name="gates_bwd_ctx")

    dw_kv_c = _mm(h_c, dpc_kv, ta=True, name="dw_kv_ctx")
    dw_qkv = _mm(h_x, dp_qkv, ta=True, tn=D, add=dw_kv_c, add_from=1, out_dtype=BF16, name="dw_qkv")
    dw_ab_c = _mm(h_c, dpc_ab, ta=True, name="dw_ab_ctx")
    dw_ab = _mm(h_x, dp_ab, ta=True, add=dw_ab_c, out_dtype=BF16, name="dw_ab")
    dw_rest = _mm(h_x, dp_rest, ta=True, out_dtype=BF16, name="dw_rest")
    g["w_in"] = (dw_qkv, dw_ab[:, :4 * H], dw_rest)
    token = on_large_gradients(g) if on_large_gradients is not None else None
    dh = _mm(dp_ab, w_ab, tb=True, after=token, name="dh_ab")
    dh = _mm(dp_qkv, w_qkv, tb=True, tk=3 * D // 2, add=dh, after=token, name="dh_qkv")
    dh = _mm(dp_rest, w_rest, tb=True, tk=3 * D // 2, add=dh, after=token, name="dh_rest")
    grad_x, dgpre_x, dm_x = _norm_bwd(x, g_pre, mod_x, b_mod, [dh], dx_res, name="norm_bwd_x")
    dh_c = _mm(dpc_ab, w_ab, tb=True, after=token, name="dhc_ab")
    dh_c = _mm(dpc_kv, w_kv, tb=True, add=dh_c, after=token, name="dhc_kv")
    _, dgpre_c, dm_c = _norm_bwd(ctx, g_pre, mod_c, b_mod, [dh_c], None, name="norm_bwd_ctx")

    dm_x = dm_x.at[:, 2 * D:].add(dvec_post[1:2])
    dmod = jnp.zeros((16, 3 * D), F32).at[0].set(dm_x[0]).at[1].set(dm_c[0])
    g["mod_factors"] = (scc[0], dm_x[0])
    g["dm_ctx"] = dm_c[0]
    g["silu_c_ctx"] = scc[1:2]
    dcc = _mm(dmod, w_mod, tb=True, b_sections=True, name="dcc")
    g["c_ctx"] = _dsilu_mul(cc[:8], dcc[:8], name="dc_ctx")[1]
    g["b_mod"] = dm_x + dm_c
    g["g_pre"] = dgpre_x + dgpre_c
    g["g_post"] = dvec_post[0:1]
    g["gm_ln_g"], g["gm_ln_b"] = dvec_post[2:3], dvec_post[3:4]
    g["g_onorm"] = dgon[0:1]
    g["w_sp"] = dwsp
    g["b_sp"] = jnp.sum(dbspb, axis=-1)
    g["w_conv"] = dwc_x.at[:, D:].add(dwc_c)
    g["a_log"] = (dav_x + dav_c)[0, :2 * H].reshape(2, H)
    g["dt_bias"] = (ddv_x + ddv_c)[0, :2 * H].reshape(2, H)
    return loss_acc[0, 0], grad_x, g


ANY = pl.BlockSpec(memory_space=pl.ANY)


def _place():
    x, y, c = lax.axis_index("x"), lax.axis_index("y"), lax.axis_index("c")
    chips = [(1 - x, y), (x, 1 - y), (1 - x, 1 - y)]
    return x, y, c, (x, y, 1 - c), chips


def _gather_shards(big, small, *, name):
    nb = len(big)

    def body(*refs):
        ins, sm_in = refs[:nb], refs[nb]
        outs, sm_out = refs[nb + 1:2 * nb + 1], refs[2 * nb + 1]
        send, recv = refs[2 * nb + 2:]
        x, y, c, sibling, chips = _place()
        mine = 2 * x + y

        def half(a, shard, hc):
            hr = big[a].shape[0] // 2
            return outs[a].at[shard, pl.ds(hc * hr, hr), :]

        def remote(k, src, dst, to):
            return pltpu.make_async_remote_copy(src_ref=src, dst_ref=dst, send_sem=send.at[k], recv_sem=recv.at[k],
                                                device_id=to, device_id_type=MESH)

        sends = []
        for a in range(nb):
            hr = big[a].shape[0] // 2
            for j, chip in enumerate(chips):
                sends.append(remote(a * 3 + j, ins[a].at[pl.ds(c * hr, hr), :], half(a, mine, c), (*chip, c)))
        for j, chip in enumerate(chips):
            sends.append(remote(nb * 3 + j, sm_in, sm_out.at[mine], (*chip, c)))
        for cp in sends:
            cp.start()
        base = nb * 3 + 3
        passed = []
        for a in range(nb):
            for j, (px, py) in enumerate(chips):
                theirs = 2 * px + py
                remote(a * 3 + j, half(a, theirs, c), half(a, theirs, c), sibling).wait_recv()
                fw = remote(base + a * 3 + j, half(a, theirs, c), half(a, theirs, c), sibling)
                fw.start()
                passed.append(fw)
        for a in range(nb):
            for j, (px, py) in enumerate(chips):
                theirs = 2 * px + py
                remote(base + a * 3 + j, half(a, theirs, 1 - c), half(a, theirs, 1 - c), sibling).wait_recv()
        for j, (px, py) in enumerate(chips):
            remote(nb * 3 + j, sm_in, sm_out.at[2 * px + py], sibling).wait_recv()
        for cp in sends + passed:
            cp.wait_send()

    n_remote = 2 * nb * 3 + 3
    outs = pl.pallas_call(
        body, name=name, in_specs=[ANY] * (nb + 1), out_specs=[ANY] * (nb + 1),
        out_shape=[jax.ShapeDtypeStruct((N_CHIPS,) + a.shape, a.dtype) for a in big + [small]],
        scratch_shapes=[pltpu.SemaphoreType.DMA((n_remote,)), pltpu.SemaphoreType.DMA((n_remote,))],
    )(*big, small)
    return outs[:nb], outs[nb]


def _gather_late_start(arrs, after, *, name):
    na = len(arrs)

    def body(*refs):
        ins, land = refs[:na], refs[na:2 * na]
        send, recv = refs[2 * na + 1], refs[2 * na + 2]
        token = refs[-1]
        x, y, c, _, chips = _place()
        for a in range(na):
            hr = arrs[a].shape[0] // 2
            for j, (px, py) in enumerate(chips):
                for other in range(2):
                    k = (a * 3 + j) * 2 + other
                    _remote(ins[a].at[pl.ds(c * hr, hr), :], land[a].at[2 * x + y, pl.ds(c * hr, hr), :],
                            send.at[k], recv.at[k], (px, py, c ^ other)).start()
        token[...] = jnp.zeros_like(token)

    lands = [((N_CHIPS,) + a.shape, a.dtype) for a in arrs]
    held = lambda a: pltpu.with_memory_space_constraint(a, pltpu.HBM)
    outs = pl.pallas_call(
        body, name=name, in_specs=[HBM] * (2 * na) + [ANY],
        out_specs=[SEM, SEM] + [HBM] * (2 * na) + [pl.BlockSpec(memory_space=pltpu.VMEM)],
        out_shape=[pltpu.SemaphoreType.DMA((6 * na,)), pltpu.SemaphoreType.DMA((6 * na,))]
        + [pltpu.HBM(a.shape, a.dtype) for a in arrs] + [pltpu.HBM(s, d) for s, d in lands]
        + [jax.ShapeDtypeStruct((8, 128), F32)],
        input_output_aliases={i: 2 + i for i in range(2 * na)},
        compiler_params=pltpu.CompilerParams(has_side_effects=pltpu.SideEffectType.DATAFLOW_SIDE_EFFECTING),
    )(*[held(a) for a in arrs], *[held(lax.empty(s, d)) for s, d in lands], after)
    return outs[0], outs[1], outs[2:2 + na], outs[2 + na:2 + 2 * na], outs[-1]


def _gather_late_wait(send, recv, thru, land, after, *, name):
    na = len(thru)

    def body(*refs):
        ins, slots = refs[:na], refs[na:2 * na]
        send_sem, recv_sem = refs[2 * na], refs[2 * na + 1]
        x, y, c, _, chips = _place()
        for a in range(na):
            hr = thru[a].shape[0] // 2
            for j, (px, py) in enumerate(chips):
                for other in range(2):
                    k = (a * 3 + j) * 2 + other
                    half = c ^ other
                    cp = _remote(ins[a].at[pl.ds(c * hr, hr), :], slots[a].at[2 * px + py, pl.ds(half * hr, hr), :],
                                 send_sem.at[k], recv_sem.at[k], (px, py, half))
                    cp.wait_send()
                    cp.wait_recv()

    outs = pl.pallas_call(
        body, name=name, in_specs=[HBM] * (2 * na) + [SEM, SEM, ANY], out_specs=[HBM] * (2 * na),
        out_shape=[pltpu.HBM(a.shape, a.dtype) for a in list(thru) + list(land)],
        input_output_aliases={i: i for i in range(2 * na)},
        compiler_params=pltpu.CompilerParams(has_side_effects=pltpu.SideEffectType.DATAFLOW_SIDE_EFFECTING),
    )(*thru, *land, send, recv, after)
    return outs[na:]


def _row_chunks(rows, row_bytes, align=8):
    n = max(1, min(rows // align, -(-rows * row_bytes // DMA_CHUNK_BYTES)))
    per = -(-(-(-rows // n)) // align) * align
    return [(r, min(per, rows - r)) for r in range(0, rows, per)]


def _remote(src, dst, send, recv, to):
    return pltpu.make_async_remote_copy(src_ref=src, dst_ref=dst, send_sem=send, recv_sem=recv, device_id=to,
                                        device_id_type=MESH)


def _exchange_halves(arrs, *, name):
    na = len(arrs)

    def body(*refs):
        ins, got = refs[:na], refs[na:2 * na]
        send, recv = refs[2 * na:]
        x, y, c, sibling, _ = _place()
        for a in range(na):
            ns, rows, cols = arrs[a].shape
            hr = rows // 2
            for s in range(ns):
                for r0, nr in _row_chunks(hr, cols * arrs[a].dtype.itemsize, 16):
                    _remote(ins[a].at[s, pl.ds((1 - c) * hr + r0, nr), :], got[a].at[s, pl.ds(r0, nr), :],
                            send.at[a], recv.at[a], sibling).start()
        for a in range(na):
            hr = arrs[a].shape[1] // 2
            _remote(ins[a].at[:, pl.ds((1 - c) * hr, hr), :], got[a], send.at[a], recv.at[a], sibling).wait()

    return pl.pallas_call(
        body, name=name, in_specs=[ANY] * na, out_specs=[ANY] * na,
        out_shape=[jax.ShapeDtypeStruct((N_CHIPS, a.shape[1] // 2, a.shape[2]), a.dtype) for a in arrs],
        scratch_shapes=[pltpu.SemaphoreType.DMA((na,)), pltpu.SemaphoreType.DMA((na,))],
    )(*arrs)


HBM = pl.BlockSpec(memory_space=pltpu.HBM)
SEM = pl.BlockSpec(memory_space=pltpu.SEMAPHORE)


def _scatter_start(arrs, *, name):
    na = len(arrs)

    def body(*refs):
        ins, land = refs[:na], refs[na:2 * na]
        send, recv = refs[2 * na], refs[2 * na + 1]
        token = refs[-1]
        x, y, c, _, chips = _place()
        for a in range(na):
            _, hr, cols = arrs[a].shape
            for j, (px, py) in enumerate(chips):
                for r0, nr in _row_chunks(hr, cols * arrs[a].dtype.itemsize, 16):
                    _remote(ins[a].at[2 * px + py, pl.ds(r0, nr), :], land[a].at[j, pl.ds(r0, nr), :],
                            send.at[a * 3 + j], recv.at[a * 3 + j], (px, py, c)).start()
        token[...] = jnp.zeros_like(token)

    slots = [((N_CHIPS - 1,) + a.shape[1:], a.dtype) for a in arrs]
    held = lambda a: pltpu.with_memory_space_constraint(a, pltpu.HBM)
    outs = pl.pallas_call(
        body, name=name, in_specs=[HBM] * (2 * na),
        out_specs=[SEM, SEM] + [HBM] * (2 * na) + [pl.BlockSpec(memory_space=pltpu.VMEM)],
        out_shape=[pltpu.SemaphoreType.DMA((3 * na,)), pltpu.SemaphoreType.DMA((3 * na,))]
        + [pltpu.HBM(a.shape, a.dtype) for a in arrs] + [pltpu.HBM(s, d) for s, d in slots]
        + [jax.ShapeDtypeStruct((8, 128), F32)],
        input_output_aliases={i: 2 + i for i in range(2 * na)},
        compiler_params=pltpu.CompilerParams(has_side_effects=pltpu.SideEffectType.DATAFLOW_SIDE_EFFECTING),
    )(*[held(a) for a in arrs], *[held(lax.empty(s, d)) for s, d in slots])
    return outs[0], outs[1], outs[2:2 + na], outs[2 + na:2 + 2 * na], outs[-1]


def _scatter_wait(send, recv, thru, land, after, *, name):
    na = len(thru)

    def body(*refs):
        ins, slots = refs[:na], refs[na:2 * na]
        send_sem, recv_sem = refs[2 * na], refs[2 * na + 1]
        x, y, c, _, chips = _place()
        for a in range(na):
            for j, (px, py) in enumerate(chips):
                cp = _remote(ins[a].at[2 * px + py], slots[a].at[j], send_sem.at[a * 3 + j], recv_sem.at[a * 3 + j],
                             (px, py, c))
                cp.wait_send()
                cp.wait_recv()

    outs = pl.pallas_call(
        body, name=name, in_specs=[HBM] * (2 * na) + [SEM, SEM, ANY], out_specs=[HBM] * (2 * na),
        out_shape=[pltpu.HBM(a.shape, a.dtype) for a in list(thru) + list(land)],
        input_output_aliases={i: i for i in range(2 * na)},
        compiler_params=pltpu.CompilerParams(has_side_effects=pltpu.SideEffectType.DATAFLOW_SIDE_EFFECTING),
    )(*thru, *land, send, recv, after)
    return outs[:na], outs[na:]


def _finish_reduce(big, *, name):
    nb = len(big)

    def body(*refs):
        outs = refs[nb:2 * nb]
        send, recv = refs[2 * nb:]
        x, y, c, sibling, _ = _place()
        for a in range(nb):
            _, hr, cols = big[a].shape
            for r0, nr in _row_chunks(hr, cols * 4):
                _remote(outs[a].at[c, pl.ds(r0, nr), :], outs[a].at[c, pl.ds(r0, nr), :], send.at[a], recv.at[a],
                        sibling).start()
        for a in range(nb):
            _remote(outs[a].at[c], outs[a].at[1 - c], send.at[a], recv.at[a], sibling).wait()

    return pl.pallas_call(
        body, name=name, in_specs=[ANY] * nb, out_specs=[ANY] * nb,
        out_shape=[jax.ShapeDtypeStruct(a.shape, F32) for a in big],
        input_output_aliases={i: i for i in range(nb)},
        scratch_shapes=[pltpu.SemaphoreType.DMA((nb,)), pltpu.SemaphoreType.DMA((nb,))],
    )(*big)


def _allreduce_small(small, *, name):
    ns, rows, cols = small.shape
    hr = rows // 2

    def body(in_ref, out_ref, got, part, slots, send, recv):
        x, y, c, sibling, chips = _place()
        give = _remote(in_ref.at[:, pl.ds(pl.multiple_of((1 - c) * hr, 8), hr), :], got, send.at[0], recv.at[0], sibling)
        give.start()
        give.wait()
        part[...] = in_ref[:, pl.ds(pl.multiple_of(c * hr, 8), hr), :] + got[...]
        scatter = [_remote(part.at[2 * px + py], slots.at[j], send.at[1 + j], recv.at[1 + j], (px, py, c))
                   for j, (px, py) in enumerate(chips)]
        for cp in scatter:
            cp.start()
        for cp in scatter:
            cp.wait()
        blk = lambda px, py, pc: out_ref.at[4 * px + 2 * py + pc]
        out_ref[4 * x + 2 * y + c] = ((part[2 * x + y] + slots[0]) + slots[1]) + slots[2]
        first = [_remote(blk(x, y, c), blk(x, y, c), send.at[4], recv.at[4], sibling)]
        first += [_remote(blk(x, y, c), blk(x, y, c), send.at[5 + j], recv.at[5 + j], (*chip, c))
                  for j, chip in enumerate(chips)]
        for cp in first:
            cp.start()
        passed = []
        for j, (px, py) in enumerate(chips):
            _remote(blk(px, py, c), blk(px, py, c), send.at[5 + j], recv.at[5 + j], sibling).wait_recv()
            fw = _remote(blk(px, py, c), blk(px, py, c), send.at[8 + j], recv.at[8 + j], sibling)
            fw.start()
            passed.append(fw)
        _remote(blk(x, y, c), blk(x, y, 1 - c), send.at[4], recv.at[4], sibling).wait_recv()
        for j, (px, py) in enumerate(chips):
            _remote(blk(px, py, c), blk(px, py, 1 - c), send.at[8 + j], recv.at[8 + j], sibling).wait_recv()
        for cp in first + passed:
            cp.wait_send()

    whole = pl.BlockSpec(memory_space=pltpu.VMEM)
    return pl.pallas_call(
        body, name=name, in_specs=[whole], out_specs=whole,
        out_shape=jax.ShapeDtypeStruct((2 * ns, hr, cols), F32),
        scratch_shapes=[pltpu.VMEM((ns, hr, cols), F32), pltpu.VMEM((ns, hr, cols), F32),
                        pltpu.VMEM((N_CHIPS - 1, hr, cols), F32), pltpu.SemaphoreType.DMA((11,)),
                        pltpu.SemaphoreType.DMA((11,))],
    )(small)


def _reduce_start(sectioned, where):
    got = _exchange_halves(sectioned, name="rs_exchange_halves")
    chip_sum = [_add_sibling(a, g, where, BF16, name=f"rs_add_sibling_{i}")
                for i, (a, g) in enumerate(zip(sectioned, got))]
    send, recv, thru, land, token = _scatter_start(chip_sum, name="rs_scatter_start")
    return (send, recv, thru, land), token


def _reduce_finish(handle, small, where, after):
    send, recv, thru, land = handle
    red_small = _allreduce_small(small, name="rs_allreduce_small")
    own, slots = _scatter_wait(send, recv, thru, land, after, name="rs_scatter_wait")
    red = [_sum_chips(p, s, where, name=f"rs_sum_chips_{i}") for i, (p, s) in enumerate(zip(own, slots))]
    return _finish_reduce(red, name="rs_finish"), red_small.reshape(-1, red_small.shape[-1])


def kernel(x, c, ctx, c_ctx, w_mod, b_mod, g_pre, g_post, w_in, w_conv, a_log, dt_bias, g_onorm, gm_ln_g, gm_ln_b, w_sp, b_sp, w_pa, w_pb, w_out, loss_target, m_c_ctx, m_w_mod, m_b_mod, m_g_pre, m_g_post, m_w_in, m_w_conv, m_a_log, m_dt_bias, m_g_onorm, m_gm_ln_g, m_gm_ln_b, m_w_sp, m_b_sp, m_w_pa, m_w_pb, m_w_out, v_c_ctx, v_w_mod, v_b_mod, v_g_pre, v_g_post, v_w_in, v_w_conv, v_a_log, v_dt_bias, v_g_onorm, v_gm_ln_g, v_gm_ln_b, v_w_sp, v_b_sp, v_w_pa, v_w_pb, v_w_out):
    names = ["c_ctx", "w_mod", "b_mod", "g_pre", "g_post", "w_in", "w_conv", "a_log", "dt_bias", "g_onorm", "gm_ln_g",
             "gm_ln_b", "w_sp", "b_sp", "w_pa", "w_pb", "w_out"]
    w = dict(zip(names, (c_ctx, w_mod, b_mod, g_pre, g_post, w_in, w_conv, a_log, dt_bias, g_onorm, gm_ln_g, gm_ln_b,
                         w_sp, b_sp, w_pa, w_pb, w_out)))
    m = dict(zip(names, (m_c_ctx, m_w_mod, m_b_mod, m_g_pre, m_g_post, m_w_in, m_w_conv, m_a_log, m_dt_bias, m_g_onorm,
                         m_gm_ln_g, m_gm_ln_b, m_w_sp, m_b_sp, m_w_pa, m_w_pb, m_w_out)))
    v = dict(zip(names, (v_c_ctx, v_w_mod, v_b_mod, v_g_pre, v_g_post, v_w_in, v_w_conv, v_a_log, v_dt_bias, v_g_onorm,
                         v_gm_ln_g, v_gm_ln_b, v_w_sp, v_b_sp, v_w_pa, v_w_pb, v_w_out)))
    xy = 2 * lax.axis_index("x") + lax.axis_index("y")
    where = jnp.stack([lax.axis_index("c"), xy, 2 * xy + lax.axis_index("c")]).astype(jnp.int32)

    shards = [a[0].astype(BF16) for a in (w_mod, w_in)]
    gathered, wconv_all = _gather_shards(shards, w_conv[0], name="gather_weights")
    own = lambda full, shard: lax.dynamic_update_slice(full, shard[None], (xy, 0, 0))
    wm_all, win_all = [own(f, s) for f, s in zip(gathered, shards)]
    wconv_all = own(wconv_all, w_conv[0])
    late_shards = [a[0].astype(BF16) for a in (w_pa, w_pb, w_out)]
    late = _gather_late_start(late_shards, wconv_all, name="gather_late_start")

    def late_weights(after):
        lands = _gather_late_wait(*late[:4], after, name="gather_late_wait")
        return [own(f, s).reshape(D, D) for f, s in zip(lands, late_shards)]
    w_conv_f = jnp.concatenate([wconv_all[s] for s in range(N_CHIPS)], axis=1)
    shard_cols = IN_COLS // N_CHIPS
    cut = OFF_A - shard_cols
    assert 0 < cut and cut + 4 * H < shard_cols
    w_qkv = jnp.concatenate([win_all[0], win_all[1][:, :cut]], axis=1)
    w_ab = jnp.pad(win_all[1][:, cut:cut + 4 * H], ((0, 0), (0, DH - 4 * H)))
    w_rest = jnp.concatenate([win_all[1][:, cut + 4 * H:], win_all[2], win_all[3]], axis=1)

    blk = D // N_CHIPS
    rest_cut = shard_cols - cut - 4 * H
    big_names = ("w_in", "w_pa", "w_pb", "w_out")
    in_flight = []

    def start_reduce(grads):
        dw_qkv, dw_ab, dw_rest = grads["w_in"]
        g_win = jnp.stack([dw_qkv[:, :shard_cols],
                           jnp.concatenate([dw_qkv[:, shard_cols:], dw_ab, dw_rest[:, :rest_cut]], axis=1),
                           dw_rest[:, rest_cut:rest_cut + shard_cols], dw_rest[:, rest_cut + shard_cols:]]).astype(BF16)
        sectioned = [g_win] + [grads[k].reshape(N_CHIPS, blk, D) for k in ("w_pa", "w_pb", "w_out")]
        handle, token = _reduce_start(sectioned, where)
        in_flight.append(handle)
        return token

    loss_local, grad_x, g = _local_step(
        x[0], c, ctx[0], c_ctx, loss_target[0], wm_all, b_mod, g_pre, g_post, w_qkv, w_ab, w_rest,
        w_conv_f, a_log[0], dt_bias[0], g_onorm, gm_ln_g, gm_ln_b, w_sp[0], b_sp[0],
        late_weights, on_large_gradients=start_reduce, run_after=late[4])
    g["loss"] = loss_local
    g["_pad"] = jnp.zeros((SMALL_LAYOUT["_pad"][1],), F32)
    g["mod_factors"] = lax.dynamic_update_slice(jnp.zeros((2 * N_CHIPS, 4 * D), F32),
                                                jnp.concatenate(g["mod_factors"])[None], (where[2], 0))
    tail = jnp.zeros((N_CHIPS * SMALL_ROWS * 128 - sum(s for _, s in SMALL_LAYOUT.values()),), F32)
    flat = jnp.concatenate([g[k].reshape(-1) for k in SMALL_LAYOUT] + [tail])
    reduced, gr_small = _reduce_finish(in_flight[0], flat.reshape(N_CHIPS, SMALL_ROWS, 128), where, g["b_mod"])
    gr_tiny = gr_small[TINY_ROW0:]

    def entry(arr, k):
        off, size = SMALL_LAYOUT[k]
        row, col = off // 128 - TINY_ROW0, off % 128
        return arr[row:row + size // 128].reshape(-1) if size >= 128 else arr[row, col:col + size]

    factors = entry(gr_tiny, "mod_factors").reshape(2 * N_CHIPS, 4 * D)
    pad_rows = lambda a: jnp.pad(a, ((0, 16 - a.shape[0]), (0, 0)))
    lhs = pad_rows(jnp.concatenate([factors[:, :D], g["silu_c_ctx"]], axis=0))
    rhs = pad_rows(jnp.concatenate([factors[:, D:], entry(gr_tiny, "dm_ctx")[None]], axis=0))
    mod_cols = 3 * D // N_CHIPS
    gr_wm = _mm(lhs, lax.dynamic_slice(rhs, (0, xy * mod_cols), (16, mod_cols)), ta=True, name="dw_mod")

    res = {k: _adamw(w[k], gr, m[k], v[k], name=f"adamw_{k}") for k, gr in zip(big_names[1:], reduced[1:])}
    flip = lambda a: jnp.swapaxes(a, 1, 2)
    res["w_in"] = [flip(r) for r in _adamw(flip(w_in), flip(reduced[0].reshape(w_in.shape)), flip(m_w_in), flip(v_w_in),
                                           name="adamw_w_in")]
    res["w_mod"] = _adamw(w_mod[0], gr_wm, m_w_mod[0], v_w_mod[0], name="adamw_w_mod")
    res["w_sp"] = _adamw(w_sp.reshape(-1, 128), gr_small, m_w_sp.reshape(-1, 128), v_w_sp.reshape(-1, 128),
                         name="adamw_w_sp")
    tiny = [k for k, (off, _) in SMALL_LAYOUT.items() if TINY_ROW0 <= off // 128 < TINY_ROW0 + TINY_ROWS]
    tiny_names = [k for k in tiny if k not in ("loss", "_pad")]

    def pack(src):
        parts = [src[k].reshape(-1) if k in tiny_names else jnp.zeros((SMALL_LAYOUT[k][1],), F32) for k in tiny]
        return jnp.concatenate(parts).reshape(TINY_ROWS, 128)

    tiny_res = _adamw(pack(w), gr_tiny[:TINY_ROWS], pack(m), pack(v), name="adamw_tiny")
    for k in tiny_names:
        res[k] = [entry(r, k) for r in tiny_res]
    g_conv = lax.dynamic_slice(entry(gr_tiny, "w_conv").reshape(3, 3 * D), (0, xy * (3 * D // N_CHIPS)),
                               (3, 3 * D // N_CHIPS))
    conv_res = _adamw(jnp.pad(w_conv[0], ((0, 5), (0, 0))), jnp.pad(g_conv, ((0, 5), (0, 0))),
                      jnp.pad(m_w_conv[0], ((0, 5), (0, 0))), jnp.pad(v_w_conv[0], ((0, 5), (0, 0))), name="adamw_w_conv")
    res["w_conv"] = [r[:3] for r in conv_res]
    res = {k: [r.reshape(w[k].shape) for r in res[k]] for k in names}

    out = [entry(gr_tiny, "loss").reshape(()), grad_x[None]]
    for i in range(4):
        out += [res[k][i] for k in names]
    return tuple(out)
```

```python
import functools

import jax
import jax.numpy as jnp
from jax import lax
from jax.experimental import pallas as pl
from jax.experimental.pallas import tpu as pltpu

F32 = jnp.float32
BF16 = jnp.bfloat16
HI = lax.Precision.HIGHEST
MESH = pl.DeviceIdType.MESH

D = 1024
H = 8
DH = 128
CH = 64
LOG_CH = 6
PAIR = 2 * CH
GM = 128
assert 1 << LOG_CH == CH and PAIR == DH
PREC_POWERS = ((lax.Precision.HIGH, lax.Precision.HIGH),) * 3 + ((None, None),) * 2
assert len(PREC_POWERS) == LOG_CH - 1
EPS = 1e-6
N_CHIPS = 4
OFF_A = 3 * D
OFF_ZB = OFF_A + 4 * H
IN_COLS = OFF_ZB + 6 * D
VMEM_LIMIT_V7X = 56 * 1024 * 1024
DMA_CHUNK_BYTES = 2 * 1024 * 1024

ADAM_LR, ADAM_B1, ADAM_B2, ADAM_EPS, ADAM_WD, ADAM_STEP = 0.001, 0.9, 0.999, 1e-08, 0.01, 10

SMALL_LAYOUT = {}
_off = 0
for _n, _s in (("w_sp", H * GM * GM), ("c_ctx", D), ("b_mod", 3 * D), ("g_pre", D), ("g_post", D), ("gm_ln_g", D),
               ("gm_ln_b", D), ("b_sp", H * GM), ("g_onorm", DH), ("a_log", 2 * H), ("dt_bias", 2 * H), ("loss", 1),
               ("_pad", 128 - 4 * H - 1 + 6 * 128), ("w_conv", 3 * 3 * D), ("dm_ctx", 3 * D), ("mod_factors", 8 * 4 * D)):
    SMALL_LAYOUT[_n] = (_off, _s)
    _off += _s
SMALL_ROWS = 368
assert N_CHIPS * SMALL_ROWS * 128 >= _off and (SMALL_ROWS // 2) % 8 == 0 and _off % 128 == 0
TINY_ROW0 = SMALL_LAYOUT["c_ctx"][0] // 128
TINY_ROWS = SMALL_LAYOUT["w_conv"][0] // 128 - TINY_ROW0
assert TINY_ROWS % 8 == 0


def _params(sem=None):
    return pltpu.CompilerParams(dimension_semantics=sem, vmem_limit_bytes=VMEM_LIMIT_V7X)


def _tile(n, cands=(256, 128, 64, 32, 16, 8)):
    if n <= cands[0]:
        return n
    for cand in cands:
        if n % cand == 0 and cand >= 64:
            return cand
    return max(d for d in range(8, cands[0], 8) if n % d == 0)


def _silu(x):
    return x * jax.nn.sigmoid(x)


def _gelu(x):
    return 0.5 * x * (1.0 + jnp.tanh(0.7978845608028654 * (x + 0.044715 * (x * x * x))))


def _mm(a, b, *, ta=False, tb=False, out_dtype=F32, tm=1024, tn=1024, tk=1024, add=None, add_from=0,
        b_sections=False, out_sections=False, after=None, name):
    m, k = (a.shape[1], a.shape[0]) if ta else a.shape
    if b_sections:
        sect = b.shape[2]
        n = b.shape[1] if tb else b.shape[0] * sect
        tn, tk = (tn, sect) if tb else (sect, tk)
    else:
        n = b.shape[0] if tb else b.shape[1]
    tm, tn, tk = min(tm, m), min(tn, n), min(tk, k)
    assert m % tm == 0 and n % tn == 0 and k % tk == 0, (name, m, n, k, tm, tn, tk)
    nk = k // tk
    dims = (((0,) if ta else (1,), (1,) if tb else (0,)), ((), ()))
    has_add = add is not None
    assert not has_add or add.shape == (m, n - add_from * tn), (name, add.shape)

    def body(*refs):
        a_ref, b_ref = refs[:2]
        o_ref = refs[2 + int(has_add) + int(after is not None)]
        acc_ref = refs[-1]
        kk = pl.program_id(2)
        bv = b_ref[0] if b_sections else b_ref[...]
        part = lax.dot_general(a_ref[...].astype(BF16), bv.astype(BF16), dims, preferred_element_type=F32)

        def finish(res):
            if has_add:
                res = res + jnp.where(pl.program_id(1) >= add_from, refs[2][...], 0.0)
            if out_sections:
                o_ref[0] = res.astype(out_dtype)
            else:
                o_ref[...] = res.astype(out_dtype)

        if nk == 1:
            finish(part)
            return

        @pl.when(kk == 0)
        def _():
            acc_ref[...] = part

        @pl.when((kk > 0) & (kk < nk - 1))
        def _():
            acc_ref[...] += part

        @pl.when(kk == nk - 1)
        def _():
            finish(acc_ref[...] + part)

    a_spec = pl.BlockSpec((tk, tm), lambda i, j, q: (q, i)) if ta else pl.BlockSpec((tm, tk), lambda i, j, q: (i, q))
    if b_sections:
        b_spec = (pl.BlockSpec((1, tn, tk), lambda i, j, q: (q, j, 0)) if tb
                  else pl.BlockSpec((1, tk, tn), lambda i, j, q: (j, q, 0)))
    else:
        b_spec = pl.BlockSpec((tn, tk), lambda i, j, q: (j, q)) if tb else pl.BlockSpec((tk, tn), lambda i, j, q: (q, j))
    add_spec = [pl.BlockSpec((tm, tn), lambda i, j, q: (i, jnp.maximum(j - add_from, 0)))] if has_add else []
    if out_sections:
        out_spec, out_shape = pl.BlockSpec((1, tm, tn), lambda i, j, q: (j, i, 0)), (n // tn, m, tn)
    else:
        out_spec, out_shape = pl.BlockSpec((tm, tn), lambda i, j, q: (i, j)), (m, n)
    return pl.pallas_call(
        body, name=name, grid=(m // tm, n // tn, nk),
        in_specs=[a_spec, b_spec] + add_spec + ([pl.BlockSpec(memory_space=pl.ANY)] if after is not None else []),
        out_specs=out_spec, out_shape=jax.ShapeDtypeStruct(out_shape, out_dtype),
        scratch_shapes=[pltpu.VMEM((tm, tn), F32)] if nk > 1 else [],
        compiler_params=_params(("parallel", "parallel", "arbitrary")),
    )(*([a, b] + ([add] if has_add else []) + ([after] if after is not None else [])))


def _h_fn(x, g, m):
    shift, scale = m[:, 0:D], m[:, D:2 * D]
    r = lax.rsqrt(jnp.mean(x * x, axis=-1, keepdims=True) + EPS)
    return (x * r * g) * (1.0 + scale) + shift


def _norm_fwd(x, g, mod, bmod, *, name):
    rows = x.shape[0]
    t = min(512, rows)

    def body(x_ref, g_ref, m_ref, b_ref, h_ref):
        h_ref[...] = _h_fn(x_ref[...], g_ref[...], m_ref[...] + b_ref[...]).astype(BF16)

    vec = lambda w: pl.BlockSpec((1, w), lambda i: (0, 0))
    return pl.pallas_call(
        body, name=name, grid=(rows // t,),
        in_specs=[pl.BlockSpec((t, D), lambda i: (i, 0)), vec(D), vec(3 * D), vec(3 * D)],
        out_specs=pl.BlockSpec((t, D), lambda i: (i, 0)),
        out_shape=jax.ShapeDtypeStruct((rows, D), BF16),
        compiler_params=_params(("parallel",)),
    )(x, g, mod, bmod)


def _norm_bwd(x, g, mod, bmod, dh_parts, resid, *, name):
    rows = x.shape[0]
    t = min(512, rows)
    n_parts = len(dh_parts)
    has_resid = resid is not None

    def body(*refs):
        x_ref, g_ref, m_ref, b_ref = refs[:4]
        parts = refs[4:4 + n_parts]
        r_ref = refs[4 + n_parts] if has_resid else None
        dx_ref, dg_ref, dm_ref = refs[-3:]
        i = pl.program_id(0)
        dh = parts[0][...]
        for p in parts[1:]:
            dh = dh + p[...]
        _, vjp = jax.vjp(_h_fn, x_ref[...], g_ref[...], m_ref[...] + b_ref[...])
        dx, dg, dm = vjp(dh)
        if has_resid:
            dx = dx + r_ref[...]
        dx_ref[...] = dx

        @pl.when(i == 0)
        def _():
            dg_ref[...] = dg
            dm_ref[...] = dm

        @pl.when(i > 0)
        def _():
            dg_ref[...] += dg
            dm_ref[...] += dm

    vec = lambda w: pl.BlockSpec((1, w), lambda i: (0, 0))
    tile = pl.BlockSpec((t, D), lambda i: (i, 0))
    ins = [x, g, mod, bmod, *dh_parts] + ([resid] if has_resid else [])
    return pl.pallas_call(
        body, name=name, grid=(rows // t,),
        in_specs=[tile, vec(D), vec(3 * D), vec(3 * D)] + [tile] * (n_parts + int(has_resid)),
        out_specs=[tile, vec(D), vec(3 * D)],
        out_shape=[jax.ShapeDtypeStruct((rows, D), F32), jax.ShapeDtypeStruct((1, D), F32),
                   jax.ShapeDtypeStruct((1, 3 * D), F32)],
        compiler_params=_params(("arbitrary",)),
    )(*ins)


def _conv_tile(u_ref, r0, t, rows, w0, w1, w2):
    u = u_ref[pl.ds(r0, t), :]
    prev8 = u_ref[pl.ds(pl.multiple_of(jnp.maximum(r0 - 8, 0), 8), 8), :]
    next8 = u_ref[pl.ds(pl.multiple_of(jnp.minimum(r0 + t, rows - 8), 8), 8), :]
    r8 = lax.broadcasted_iota(jnp.int32, (8, DH), 0)
    prev_row = jnp.sum(jnp.where(r8 == 7, prev8, 0.0), axis=0, keepdims=True)
    next_row = jnp.sum(jnp.where(r8 == 0, next8, 0.0), axis=0, keepdims=True)
    prev_row = jnp.where(r0 > 0, prev_row, 0.0)
    next_row = jnp.where(r0 + t < rows, next_row, 0.0)
    ri = lax.broadcasted_iota(jnp.int32, (t, DH), 0)
    um1 = jnp.where(ri == 0, prev_row, pltpu.roll(u, 1, 0))
    up1 = jnp.where(ri == t - 1, next_row, pltpu.roll(u, t - 1, 0))
    return w0 * um1 + w1 * u + w2 * up1, um1, u, up1


def _rowlocal(z, norm):
    y = _silu(z)
    return y * lax.rsqrt(jnp.sum(y * y, axis=-1, keepdims=True) + EPS) if norm else y


def _prep_fwd(p, wconv, n_norm, *, name):
    rows, nb = p.shape[0], p.shape[1] // DH
    t = min(512, rows)
    out = None
    for norm, b0, b1 in ((True, 0, n_norm), (False, n_norm, nb)):
        def body(u_ref, w_ref, *rest, norm=norm):
            o_ref = rest[-1]
            w0, w1, w2 = w_ref[0:1, :], w_ref[1:2, :], w_ref[2:3, :]

            def step(s, carry):
                r0 = pl.multiple_of(s * t, t)
                z, _, _, _ = _conv_tile(u_ref, r0, t, rows, w0, w1, w2)
                o_ref[0, pl.ds(r0, t), :] = _rowlocal(z, norm)
                return carry

            lax.fori_loop(0, rows // t, step, 0)

        held = [] if out is None else [out]
        out = pl.pallas_call(
            body, name=f"{name}_{'norm' if norm else 'plain'}", grid=(b1 - b0,),
            in_specs=[pl.BlockSpec((rows, DH), lambda j, b0=b0: (0, b0 + j)),
                      pl.BlockSpec((3, DH), lambda j, b0=b0: (0, b0 + j))] + [ANY] * len(held),
            out_specs=pl.BlockSpec((1, rows, DH), lambda j, b0=b0: (b0 + j, 0, 0)),
            out_shape=jax.ShapeDtypeStruct((nb, rows, DH), F32),
            input_output_aliases={2: 0} if held else {},
            compiler_params=_params(("parallel",)),
        )(p, wconv, *held)
    return out


def _prep_bwd(p, wconv, d_a, d_b, n_norm, *, name):
    rows, nb = p.shape[0], p.shape[1] // DH
    t = min(512, rows)
    outs = []
    for norm, b0, b1 in ((True, 0, n_norm), (False, n_norm, nb)):
        def body(u_ref, w_ref, da_ref, db_ref, *rest, norm=norm):
            du_ref, dw_ref, dz_ref = rest[-3:]
            w0, w1, w2 = w_ref[0:1, :], w_ref[1:2, :], w_ref[2:3, :]

            def step1(s, carry):
                a0, a1, a2 = carry
                r0 = pl.multiple_of(s * t, t)
                z, um1, u, up1 = _conv_tile(u_ref, r0, t, rows, w0, w1, w2)
                _, vjp = jax.vjp(lambda zz: _rowlocal(zz, norm), z)
                (dz,) = vjp(da_ref[0, pl.ds(r0, t), :] + db_ref[0, pl.ds(r0, t), :])
                dz_ref[pl.ds(r0, t), :] = dz
                red = lambda v: jnp.sum(v, axis=0, keepdims=True)
                return a0 + red(dz * um1), a1 + red(dz * u), a2 + red(dz * up1)

            zero = jnp.zeros((1, DH), F32)
            a0, a1, a2 = lax.fori_loop(0, rows // t, step1, (zero, zero, zero))
            dw_ref[0:1, :] = a0
            dw_ref[1:2, :] = a1
            dw_ref[2:3, :] = a2

            def step2(s, carry):
                r0 = pl.multiple_of(s * t, t)
                du, _, _, _ = _conv_tile(dz_ref, r0, t, rows, w2, w1, w0)
                du_ref[pl.ds(r0, t), :] = du.astype(BF16)
                return carry

            lax.fori_loop(0, rows // t, step2, 0)

        col = pl.BlockSpec((rows, DH), lambda j, b0=b0: (0, b0 + j))
        w_spec = pl.BlockSpec((3, DH), lambda j, b0=b0: (0, b0 + j))
        d_spec = pl.BlockSpec((1, rows, DH), lambda j, b0=b0: (b0 + j, 0, 0))
        outs = pl.pallas_call(
            body, name=f"{name}_{'norm' if norm else 'plain'}", grid=(b1 - b0,),
            in_specs=[col, w_spec, d_spec, d_spec] + [ANY] * len(outs), out_specs=[col, w_spec],
            out_shape=[jax.ShapeDtypeStruct((rows, nb * DH), BF16), jax.ShapeDtypeStruct((3, nb * DH), F32)],
            input_output_aliases={4: 0, 5: 1} if outs else {},
            scratch_shapes=[pltpu.VMEM((rows, DH), F32)],
            compiler_params=_params(("parallel",)),
        )(p, wconv, d_a, d_b, *outs)
    return outs


def _gates_fn(pab, avec, dvec):
    t = pab.shape[0]
    lane = lax.broadcasted_iota(jnp.int32, pab.shape, 1)
    xg = pab + dvec
    sp = jnp.maximum(xg, 0.0) + jnp.log(1.0 + jnp.exp(-jnp.abs(xg)))
    g = jnp.where(lane < 2 * H, -jnp.exp(avec) * sp, 0.0)
    ii = lax.broadcasted_iota(jnp.int32, (t, t), 0)
    jj = lax.broadcasted_iota(jnp.int32, (t, t), 1)
    same = (ii >> LOG_CH) == (jj >> LOG_CH)
    cum_f = _dot(jnp.where(same & (jj <= ii), 1.0, 0.0), g, precision=HI)
    cum_r = _dot(jnp.where(same & (jj >= ii), 1.0, 0.0), g, precision=HI)
    return jnp.where(lane < H, cum_f, jnp.where(lane < 2 * H, cum_r, jnp.where(lane < 4 * H, jax.nn.sigmoid(pab), 0.0)))


def _gates_fwd(pab, avec, dvec, *, name):
    rows = pab.shape[0]
    t = min(2 * PAIR, rows)

    def body(p_ref, a_ref, d_ref, o_ref):
        o_ref[...] = _gates_fn(p_ref[...], a_ref[...], d_ref[...])

    vec = pl.BlockSpec((1, DH), lambda i: (0, 0))
    tile = pl.BlockSpec((t, DH), lambda i: (i, 0))
    return pl.pallas_call(
        body, name=name, grid=(rows // t,), in_specs=[tile, vec, vec], out_specs=tile,
        out_shape=jax.ShapeDtypeStruct((rows, DH), F32), compiler_params=_params(("parallel",)),
    )(pab, avec, dvec)


def _gates_bwd(pab, avec, dvec, d_a, d_b, *, name):
    rows = pab.shape[0]
    t = min(2 * PAIR, rows)

    def body(p_ref, a_ref, d_ref, da_ref, db_ref, dp_ref, dav_ref, ddv_ref):
        i = pl.program_id(0)
        _, vjp = jax.vjp(_gates_fn, p_ref[...], a_ref[...], d_ref[...])
        dp, dav, ddv = vjp(da_ref[...] + db_ref[...])
        dp_ref[...] = dp.astype(BF16)

        @pl.when(i == 0)
        def _():
            dav_ref[...] = dav
            ddv_ref[...] = ddv

        @pl.when(i > 0)
        def _():
            dav_ref[...] += dav
            ddv_ref[...] += ddv

    vec = pl.BlockSpec((1, DH), lambda i: (0, 0))
    tile = pl.BlockSpec((t, DH), lambda i: (i, 0))
    return pl.pallas_call(
        body, name=name, grid=(rows // t,), in_specs=[tile, vec, vec, tile, tile], out_specs=[tile, vec, vec],
        out_shape=[jax.ShapeDtypeStruct((rows, DH), BF16), jax.ShapeDtypeStruct((1, DH), F32),
                   jax.ShapeDtypeStruct((1, DH), F32)],
        compiler_params=_params(("arbitrary",)),
    )(pab, avec, dvec, d_a, d_b)


def _dot_general(a, b, dims, precision=None):
    return lax.dot_general(a, b, (dims, ((), ())), precision=precision, preferred_element_type=F32)


@functools.partial(jax.custom_vjp, nondiff_argnums=(2,))
def _dot_bf16(a, b, dims):
    return _dot_general(a.astype(BF16), b.astype(BF16), dims)


def _dot_bf16_fwd(a, b, dims):
    return _dot_bf16(a, b, dims), (a, b)


def _dot_bf16_bwd(dims, res, g):
    a, b = res
    (ca,), (cb,) = dims
    da = _dot_bf16(g, b, ((1,), (1 - cb,))) if ca == 1 else _dot_bf16(b, g, ((1 - cb,), (1,)))
    db = _dot_bf16(a, g, ((1 - ca,), (0,))) if cb == 0 else _dot_bf16(g, a, ((0,), (1 - ca,)))
    return da, db


_dot_bf16.defvjp(_dot_bf16_fwd, _dot_bf16_bwd)


def _dot(a, b, dims=((1,), (0,)), precision=None):
    if precision is None and a.dtype == F32 and b.dtype == F32:
        return _dot_bf16(a, b, dims)
    return _dot_general(a, b, dims, precision)


_NT = ((1,), (1,))
_TN = ((0,), (0,))


@jax.custom_vjp
def _saved_inverse(neg_a, inv):
    return inv


def _saved_inverse_fwd(neg_a, inv):
    return inv, inv


def _saved_inverse_bwd(inv, d_inv):
    idx = range(len(inv))
    left = [_dot(inv[i], d_inv[i], _TN) for i in idx]
    d_neg_a = [_dot(left[i], inv[i], _NT) for i in idx]
    return d_neg_a, [jnp.zeros_like(t) for t in inv]


_saved_inverse.defvjp(_saved_inverse_fwd, _saved_inverse_bwd)


def _pairs(s, q, k, v, gcol, bcol, revs, inv_saved=None):
    idx = range(len(revs))
    ii = lax.broadcasted_iota(jnp.int32, (PAIR, PAIR), 0)
    jj = lax.broadcasted_iota(jnp.int32, (PAIR, PAIR), 1)
    same = (ii >> LOG_CH) == (jj >> LOG_CH)
    incl_d = (same & (ii >= jj), same & (ii <= jj))
    strict_d = (same & (ii > jj), same & (ii < jj))
    incl = [incl_d[int(r)] for r in revs]
    strict = [strict_d[int(r)] for r in revs]
    eye = jnp.where(ii == jj, 1.0, 0.0)
    gc_i = [jnp.broadcast_to(gcol[i], (PAIR, DH)) for i in idx]
    gc_j = [gc_i[i].T for i in idx]
    decay = [jnp.where(incl[i], jnp.exp(jnp.where(incl[i], gc_i[i] - gc_j[i], 0.0)), 0.0) for i in idx]
    b_b = [jnp.broadcast_to(bcol[i], (PAIR, DH)) for i in idx]
    kb = [k[i] * b_b[i] for i in idx]
    kk = [_dot(kb[i], k[i], _NT) for i in idx]
    bp = [jnp.where(strict[i], -kk[i] * decay[i], 0.0) for i in idx]
    if inv_saved is not None:
        inv = _saved_inverse(bp, inv_saved)
    else:
        low = bp
        for prec_sq, prec_acc in PREC_POWERS:
            bp = [_dot(bp[i], bp[i], precision=prec_sq) for i in idx]
            more = [_dot(low[i], bp[i], precision=prec_acc) for i in idx]
            low = [low[i] + bp[i] + more[i] for i in idx]
        inv = [eye + low[i] for i in idx]
    eg = [jnp.exp(gc_i[i]) for i in idx]
    sol = [_dot(inv[i], jnp.concatenate([v[i] * b_b[i], kb[i] * eg[i]], axis=1)) for i in idx]
    u_val = [sol[i][:, :DH] for i in idx]
    w_key = [sol[i][:, DH:] for i in idx]
    row = lax.broadcasted_iota(jnp.int32, (PAIR, 1), 0)
    has_q = q[0] is not None
    if has_q:
        qc = [q[i] * (DH ** -0.5) for i in idx]
        qk = [_dot(qc[i], k[i], _NT) for i in idx]
        attn = [qk[i] * decay[i] for i in idx]
        qd = [qc[i] * eg[i] for i in idx]
    outs = [[None, None] for _ in idx]
    zeros = jnp.zeros((CH, DH), F32)
    for step in range(2):
        cidx = [(1 - step) if revs[i] else step for i in idx]
        sl = [slice(c * CH, (c + 1) * CH) for c in cidx]
        last = [c * CH if revs[i] else c * CH + CH - 1 for i, c in zip(idx, cidx)]
        gl = [jnp.sum(jnp.where(row == last[i], gcol[i], 0.0), axis=0, keepdims=True) for i in idx]
        k_tail = [k[i][sl[i]] * jnp.exp(gl[i] - gc_i[i][sl[i]]) for i in idx]
        ws = [_dot(w_key[i][sl[i]], s[i]) for i in idx]
        v_new = [u_val[i][sl[i]] - ws[i] for i in idx]
        if has_q:
            v_pad = [jnp.concatenate([v_new[i], zeros] if cidx[i] == 0 else [zeros, v_new[i]], axis=0) for i in idx]
            o_state = [_dot(qd[i][sl[i]], s[i]) for i in idx]
            o_local = [_dot(attn[i][sl[i]], v_pad[i]) for i in idx]
            for i in idx:
                outs[i][cidx[i]] = o_state[i] + o_local[i]
        kv = [_dot(k_tail[i], v_new[i], _TN) for i in idx]
        s = [s[i] * jnp.exp(gl[i]) + kv[i] for i in idx]
    return s, ([jnp.concatenate(outs[i], axis=0) for i in idx] if has_q else None), inv


def _lane_col(tile, idx):
    lane = lax.broadcasted_iota(jnp.int32, tile.shape, 1)
    return jnp.sum(jnp.where(lane == idx, tile, 0.0), axis=1, keepdims=True)


def _gdn_fwd(qkv, gb, s0f, s0b, has_q, *, name):
    nb, rows, _ = qkv.shape
    n = rows // PAIR
    qoff = H if has_q else 0

    def body(qf_ref, qb_ref, gf_ref, gr_ref, s0f_ref, s0b_ref, of_ref, ob_ref, ssf_ref, ssb_ref, tsf_ref, tsb_ref,
             sf_ref, sb_ref):
        @pl.when(pl.program_id(0) == 0)
        def _():
            sf_ref[...] = s0f_ref[...]
            sb_ref[...] = s0b_ref[...]

        gtiles = (gf_ref[...], gr_ref[...])

        dirs = ((qf_ref, sf_ref, ssf_ref, of_ref), (qb_ref, sb_ref, ssb_ref, ob_ref))
        ts_refs = (tsf_ref, tsb_ref)

        work = [(h, d) for h in range(H) for d in range(2)]
        loaded = []
        for h, d in work:
            q_ref, s_ref, _, _ = dirs[d]
            loaded.append((s_ref[h], q_ref[h] if has_q else None, q_ref[qoff + h], q_ref[qoff + H + h],
                           _lane_col(gtiles[d], d * H + h), _lane_col(gtiles[d], 2 * H + d * H + h)))
        s_new, o, inv = _pairs(*[list(col) for col in zip(*loaded)], revs=[d == 1 for _, d in work])
        for i, (h, d) in enumerate(work):
            _, s_ref, ss_ref, o_ref = dirs[d]
            ss_ref[0, h] = loaded[i][0]
            ts_refs[d][0, h] = inv[i]
            s_ref[h] = s_new[i]
            o_ref[h] = o[i] if has_q else jnp.zeros((PAIR, DH), F32)

    fwd3 = lambda i: (0, i, 0)
    rev3 = lambda i: (0, n - 1 - i, 0)
    state = pl.BlockSpec((H, DH, DH), lambda i: (0, 0, 0))
    saved = pl.BlockSpec((1, H, DH, DH), lambda i: (i, 0, 0, 0))
    return pl.pallas_call(
        body, name=name, grid=(n,),
        in_specs=[pl.BlockSpec((nb, PAIR, DH), fwd3), pl.BlockSpec((nb, PAIR, DH), rev3),
                  pl.BlockSpec((PAIR, DH), lambda i: (i, 0)), pl.BlockSpec((PAIR, DH), lambda i: (n - 1 - i, 0)),
                  state, state],
        out_specs=[pl.BlockSpec((H, PAIR, DH), fwd3), pl.BlockSpec((H, PAIR, DH), rev3), saved, saved, saved, saved,
                   state, state],
        out_shape=[jax.ShapeDtypeStruct((H, rows, DH), F32)] * 2 + [jax.ShapeDtypeStruct((n, H, DH, DH), F32)] * 4
        + [jax.ShapeDtypeStruct((H, DH, DH), F32)] * 2,
        compiler_params=_params(("arbitrary",)),
    )(qkv, qkv, gb, gb, s0f, s0b)


def _gdn_bwd(qkv, gb, ssf, ssb, tsf, tsb, do, dsf, dsb, has_q, *, name):
    nb, rows, _ = qkv.shape
    n = rows // PAIR
    qoff = H if has_q else 0

    def body(qf_ref, qb_ref, gf_ref, gr_ref, ssf_ref, ssb_ref, tsf_ref, tsb_ref, dof_ref, dob_ref, dsf0_ref, dsb0_ref,
             dqf_ref, dqb_ref, dgf_ref, dgr_ref, dsf_ref, dsb_ref):
        ts_refs = (tsf_ref, tsb_ref)
        @pl.when(pl.program_id(0) == 0)
        def _():
            dsf_ref[...] = dsf0_ref[...]
            dsb_ref[...] = dsb0_ref[...]

        gtiles = (gf_ref[...], gr_ref[...])
        lane = lax.broadcasted_iota(jnp.int32, (PAIR, DH), 1)

        dirs = ((qf_ref, ssf_ref, dof_ref, dsf_ref, dqf_ref), (qb_ref, ssb_ref, dob_ref, dsb_ref, dqb_ref))

        work = [(h, d) for h in range(H) for d in range(2)]
        revs = [d == 1 for _, d in work]
        s_in, q_in, k_in, v_in, g_in, b_in, ds_out, do_out, inv_in = [], [], [], [], [], [], [], [], []
        for h, d in work:
            q_ref, ss_ref, do_ref, ds_ref, _ = dirs[d]
            s_in.append(ss_ref[0, h])
            inv_in.append(ts_refs[d][0, h])
            q_in.append(q_ref[h] if has_q else None)
            k_in.append(q_ref[qoff + h])
            v_in.append(q_ref[qoff + H + h])
            g_in.append(_lane_col(gtiles[d], d * H + h))
            b_in.append(_lane_col(gtiles[d], 2 * H + d * H + h))
            ds_out.append(ds_ref[h])
            do_out.append(do_ref[h] if has_q else None)
        if has_q:
            _, vjp = jax.vjp(lambda s_, q_, k_, v_, g_, b_: _pairs(s_, q_, k_, v_, g_, b_, revs, inv_in)[:2],
                             s_in, q_in, k_in, v_in, g_in, b_in)
            ds, dq, dk, dv, dg, db = vjp((ds_out, do_out))
        else:
            _, vjp = jax.vjp(lambda s_, k_, v_, g_, b_: _pairs(s_, q_in, k_, v_, g_, b_, revs, inv_in)[0],
                             s_in, k_in, v_in, g_in, b_in)
            ds, dk, dv, dg, db = vjp(ds_out)
        dgb = [jnp.zeros((PAIR, DH), F32)] * 2
        for i, (h, d) in enumerate(work):
            dq_ref, ds_ref = dirs[d][4], dirs[d][3]
            ds_ref[h] = ds[i]
            if has_q:
                dq_ref[h] = dq[i]
            dq_ref[qoff + h], dq_ref[qoff + H + h] = dk[i], dv[i]
            dgb[d] = (dgb[d] + jnp.where(lane == d * H + h, dg[i], 0.0)
                      + jnp.where(lane == 2 * H + d * H + h, db[i], 0.0))
        dgf_ref[...] = dgb[0]
        dgr_ref[...] = dgb[1]

    fwd3 = lambda i: (0, n - 1 - i, 0)
    rev3 = lambda i: (0, i, 0)
    state = pl.BlockSpec((H, DH, DH), lambda i: (0, 0, 0))
    saved = pl.BlockSpec((1, H, DH, DH), lambda i: (n - 1 - i, 0, 0, 0))
    gf_spec = pl.BlockSpec((PAIR, DH), lambda i: (n - 1 - i, 0))
    gr_spec = pl.BlockSpec((PAIR, DH), lambda i: (i, 0))
    return pl.pallas_call(
        body, name=name, grid=(n,),
        in_specs=[pl.BlockSpec((nb, PAIR, DH), fwd3), pl.BlockSpec((nb, PAIR, DH), rev3), gf_spec, gr_spec,
                  saved, saved, saved, saved,
                  pl.BlockSpec((H, PAIR, DH), fwd3), pl.BlockSpec((H, PAIR, DH), rev3), state, state],
        out_specs=[pl.BlockSpec((nb, PAIR, DH), fwd3), pl.BlockSpec((nb, PAIR, DH), rev3), gf_spec, gr_spec, state, state],
        out_shape=[jax.ShapeDtypeStruct((nb, rows, DH), F32)] * 2 + [jax.ShapeDtypeStruct((rows, DH), F32)] * 2
        + [jax.ShapeDtypeStruct((H, DH, DH), F32)] * 2,
        compiler_params=_params(("arbitrary",)),
    )(qkv, qkv, gb, gb, ssf, ssb, tsf, tsb, do, do, dsf, dsb)


def _stage1(zb, ua, va, za, o, gon, lng, lnb, wsp, bsp):
    gv = [_gelu(t) for t in va]
    mu = sum(jnp.sum(t, axis=-1, keepdims=True) for t in gv) * (1.0 / D)
    xc = [t - mu for t in gv]
    var = sum(jnp.sum(t * t, axis=-1, keepdims=True) for t in xc) * (1.0 / D)
    rs = lax.rsqrt(var + EPS)
    ya, yb = [], []
    for g in range(H):
        vv = xc[g] * rs * lng[g] + lnb[g]
        s = _dot(wsp[g], vv) + bsp[g]
        ya.append(_gelu(ua[g]) * s * _silu(za[g]))
        r = lax.rsqrt(jnp.mean(o[g] * o[g], axis=-1, keepdims=True) + EPS)
        yb.append(o[g] * r * gon * _silu(zb[g]))
    return ya, yb


def _stage2(ma, mb, ga, gb):
    return jax.nn.sigmoid(ga) * ma + jax.nn.sigmoid(gb) * mb


def _stage3(out, x, tgt, gpost, gate):
    r = out * lax.rsqrt(jnp.mean(out * out, axis=-1, keepdims=True) + EPS) * gpost
    err = x + gate * r - tgt
    return 0.5 * jnp.sum(jnp.mean(err * err, axis=-1, keepdims=True), axis=0, keepdims=True)


def _post(p_rest, o_f, o_b, x, tgt, mod, bmod, gon, lng, lnb, wsp, bspb, wpa, wpb, wout, gpost, *, name):
    rows = x.shape[0]
    n = rows // GM
    lanes = lambda g: slice(g * DH, (g + 1) * DH)
    bdot = lambda a, w_ref: _dot(a.astype(BF16), w_ref[...])
    bdot_t = lambda a, w_ref: _dot(a.astype(BF16), w_ref[...], _NT)

    def body(p_ref, of_ref, ob_ref, x_ref, t_ref, m_ref, bm_ref, gon_ref, lng_ref, lnb_ref, wsp_ref, bsp_ref,
             wpa_ref, wpb_ref, wout_ref, gp_ref,
             loss_ref, dp_ref, do_ref, dx_ref, ya_ref, yb_ref, mg_ref, dma_ref, dmb_ref, dout_ref,
             dvec_ref, dgon_ref, dwsp_ref, dbsp_ref):
        @pl.when(pl.program_id(0) == 0)
        def _():
            loss_ref[...] = jnp.zeros_like(loss_ref)
            dvec_ref[...] = jnp.zeros_like(dvec_ref)
            dgon_ref[...] = jnp.zeros_like(dgon_ref)
            dwsp_ref[...] = jnp.zeros_like(dwsp_ref)
            dbsp_ref[...] = jnp.zeros_like(dbsp_ref)

        piece = lambda blk: [p_ref[:, blk * D + g * DH: blk * D + (g + 1) * DH].astype(F32) for g in range(H)]
        zb, ua, va, za = piece(0), piece(1), piece(2), piece(3)
        o = [of_ref[g] + ob_ref[g] for g in range(H)]
        gon = gon_ref[...]
        lng = [lng_ref[:, lanes(g)] for g in range(H)]
        lnb = [lnb_ref[:, lanes(g)] for g in range(H)]
        wsp = [wsp_ref[g] for g in range(H)]
        bsp = [bsp_ref[g] for g in range(H)]
        (ya, yb), vjp1 = jax.vjp(_stage1, zb, ua, va, za, o, gon, lng, lnb, wsp, bsp)
        y_a, y_b = jnp.concatenate(ya, axis=1), jnp.concatenate(yb, axis=1)
        ma, mb = bdot(y_a, wpa_ref), bdot(y_b, wpb_ref)
        ga, gb = p_ref[:, 4 * D:5 * D].astype(F32), p_ref[:, 5 * D:6 * D].astype(F32)
        merged, vjp2 = jax.vjp(_stage2, ma, mb, ga, gb)
        out = bdot(merged, wout_ref)
        gate = m_ref[:, 2 * D:3 * D] + bm_ref[:, 2 * D:3 * D]
        loss, vjp3 = jax.vjp(_stage3, out, x_ref[...], t_ref[...], gp_ref[...], gate)
        loss_ref[...] += jnp.broadcast_to(loss, loss_ref.shape)

        dout, dx, _, dgpost, dgate = vjp3(jnp.ones((1, 1), F32))
        dx_ref[...] = dx
        dmerged = bdot_t(dout, wout_ref)
        dma, dmb, dga, dgb = vjp2(dmerged)
        dya, dyb = bdot_t(dma, wpa_ref), bdot_t(dmb, wpb_ref)
        dzb, dua, dva, dza, do, dgon, dlng, dlnb, dwsp, dbsp = vjp1(
            ([dya[:, lanes(g)] for g in range(H)], [dyb[:, lanes(g)] for g in range(H)]))

        for blk, dlist in enumerate((dzb, dua, dva, dza)):
            for g in range(H):
                dp_ref[:, blk * D + g * DH: blk * D + (g + 1) * DH] = dlist[g].astype(BF16)
        dp_ref[:, 4 * D:5 * D] = dga.astype(BF16)
        dp_ref[:, 5 * D:6 * D] = dgb.astype(BF16)
        for g in range(H):
            do_ref[g] = do[g]
            dwsp_ref[g] += dwsp[g]
            dbsp_ref[g] += dbsp[g]
            dvec_ref[2:3, lanes(g)] += dlng[g]
            dvec_ref[3:4, lanes(g)] += dlnb[g]
        dvec_ref[0:1, :] += dgpost
        dvec_ref[1:2, :] += dgate
        dgon_ref[0:1, :] += dgon
        ya_ref[...] = y_a.astype(BF16)
        yb_ref[...] = y_b.astype(BF16)
        mg_ref[...] = merged.astype(BF16)
        dma_ref[...] = dma.astype(BF16)
        dmb_ref[...] = dmb.astype(BF16)
        dout_ref[...] = dout.astype(BF16)

    row = lambda w: pl.BlockSpec((GM, w), lambda i: (i, 0))
    heads = pl.BlockSpec((H, GM, DH), lambda i: (0, i, 0))
    full = lambda shape: pl.BlockSpec(shape, lambda i: tuple(0 for _ in shape))
    sds = jax.ShapeDtypeStruct
    return pl.pallas_call(
        body, name=name, grid=(n,),
        in_specs=[row(6 * D), heads, heads, row(D), row(D), full((1, 3 * D)), full((1, 3 * D)), full((1, DH)),
                  full((1, D)), full((1, D)), full((H, GM, GM)), full((H, GM, GM)),
                  full((D, D)), full((D, D)), full((D, D)), full((1, D))],
        out_specs=[full((8, DH)), row(6 * D), heads, row(D)] + [row(D)] * 6
        + [full((8, D)), full((8, DH)), full((H, GM, GM)), full((H, GM, GM))],
        out_shape=[sds((8, DH), F32), sds((rows, 6 * D), BF16), sds((H, rows, DH), F32), sds((rows, D), F32)]
        + [sds((rows, D), BF16)] * 6 + [sds((8, D), F32), sds((8, DH), F32), sds((H, GM, GM), F32), sds((H, GM, GM), F32)],
        compiler_params=_params(("arbitrary",)),
    )(p_rest, o_f, o_b, x, tgt, mod, bmod, gon, lng, lnb, wsp, bspb, wpa, wpb, wout, gpost)


def _silu_rows(c, *, name):
    def body(c_ref, o_ref):
        o_ref[...] = _silu(c_ref[...])

    return pl.pallas_call(body, name=name, out_shape=jax.ShapeDtypeStruct(c.shape, F32))(c)


def _dsilu_mul(c, d, *, name):
    def body(c_ref, d_ref, o_ref):
        _, vjp = jax.vjp(_silu, c_ref[...])
        (o_ref[...],) = vjp(d_ref[...])

    return pl.pallas_call(body, name=name, out_shape=jax.ShapeDtypeStruct(c.shape, F32))(c, d)


def _adamw(w, g, m, v, *, name):
    rows, cols = w.shape[-2:]
    t = _tile(rows if g.ndim == 2 else g.shape[1])
    c1 = 1.0 / (1.0 - ADAM_B1 ** ADAM_STEP)
    c2 = 1.0 / (1.0 - ADAM_B2 ** ADAM_STEP)

    def body(w_ref, g_ref, m_ref, v_ref, go_ref, d_ref, mo_ref, vo_ref):
        blk = lambda r: r[...].reshape(t, cols)
        gv = blk(g_ref)
        mn = ADAM_B1 * blk(m_ref) + (1.0 - ADAM_B1) * gv
        vn = ADAM_B2 * blk(v_ref) + (1.0 - ADAM_B2) * (gv * gv)
        delta = -ADAM_LR * ((mn * c1) / (jnp.sqrt(vn * c2) + ADAM_EPS) + ADAM_WD * blk(w_ref))
        for ref, val in ((go_ref, gv), (d_ref, delta), (mo_ref, mn), (vo_ref, vn)):
            ref[...] = val.reshape(ref.shape)

    tile = (pl.BlockSpec((1, t, cols), lambda i: (0, i, 0)) if w.ndim == 3 else pl.BlockSpec((t, cols), lambda i: (i, 0)))
    if g.ndim == 3:
        per = g.shape[1] // t
        g_tile = pl.BlockSpec((1, t, cols), lambda i: (i // per, i % per, 0))
    else:
        g_tile = pl.BlockSpec((t, cols), lambda i: (i, 0))
    return pl.pallas_call(
        body, name=name, grid=(rows // t,),
        in_specs=[tile, g_tile, tile, tile], out_specs=[tile] * 4,
        out_shape=[jax.ShapeDtypeStruct(w.shape, F32)] * 4,
        compiler_params=_params(("parallel",)),
    )(w, g, m, v)


def _add_sibling(full, got, where, out_dtype, *, name):
    s, rows, cols = full.shape
    hr = rows // 2
    t = _tile(hr)
    nt = hr // t

    def body(w_ref, a_ref, b_ref, o_ref):
        o_ref[...] = (a_ref[...].astype(F32) + b_ref[...].astype(F32)).astype(out_dtype)

    tile = pl.BlockSpec((1, t, cols), lambda j, i, w: (j, i, 0))
    return pl.pallas_call(
        body, name=name,
        grid_spec=pltpu.PrefetchScalarGridSpec(
            num_scalar_prefetch=1, grid=(s, nt),
            in_specs=[pl.BlockSpec((1, t, cols), lambda j, i, w: (j, w[0] * nt + i, 0)), tile], out_specs=tile),
        out_shape=jax.ShapeDtypeStruct((s, hr, cols), out_dtype), compiler_params=_params(("parallel", "parallel")),
    )(where, full, got)


def _sum_chips(own, slots, where, n_out, which, *, name):
    _, hr, cols = own.shape
    t = _tile(hr)

    def body(w_ref, a_ref, s_ref, o_ref):
        f = lambda v: v.astype(F32)
        o_ref[0] = ((f(a_ref[0]) + f(s_ref[0])) + f(s_ref[1])) + f(s_ref[2])

    return pl.pallas_call(
        body, name=name,
        grid_spec=pltpu.PrefetchScalarGridSpec(
            num_scalar_prefetch=1, grid=(hr // t,),
            in_specs=[pl.BlockSpec((1, t, cols), lambda i, w: (w[1], i, 0)),
                      pl.BlockSpec((N_CHIPS - 1, t, cols), lambda i, w: (0, i, 0))],
            out_specs=pl.BlockSpec((1, t, cols), lambda i, w: (w[which], i, 0))),
        out_shape=jax.ShapeDtypeStruct((n_out, hr, cols), F32), compiler_params=_params(("parallel",)),
    )(where, own, slots)


def _local_step(x, c, ctx, c_ctx, tgt, w_mod, b_mod, g_pre, g_post, w_qkv, w_ab, w_rest, w_conv, a_log, dt_bias,
                g_onorm, gm_ln_g, gm_ln_b, w_sp, b_sp, late_weights, on_large_gradients=None, run_after=None):
    rows, rows_c = x.shape[0], ctx.shape[0]
    cc = jnp.zeros((16, D), F32).at[0].set(c[0]).at[1].set(c_ctx)
    scc = _silu_rows(cc, name="silu_cond")
    mod = _mm(scc, w_mod, b_sections=True, after=run_after, name="mod_fwd")
    mod_x, mod_c = mod[0:1], mod[1:2]
    avec = jnp.zeros((1, DH), F32).at[0, :2 * H].set(a_log.reshape(-1))
    dvec = jnp.zeros((1, DH), F32).at[0, :2 * H].set(dt_bias.reshape(-1))
    bspb = jnp.broadcast_to(b_sp[:, :, None], (H, GM, GM))
    w_kv, wconv_kv = w_qkv[:, D:], w_conv[:, D:]

    h_c = _norm_fwd(ctx, g_pre, mod_c, b_mod, name="norm_fwd_ctx")
    pc_kv = _mm(h_c, w_kv, name="inproj_ctx_kv")
    pc_ab = _mm(h_c, w_ab, name="inproj_ctx_ab")
    kv_c = _prep_fwd(pc_kv, wconv_kv, H, name="prep_fwd_ctx")
    gb_c = _gates_fwd(pc_ab, avec, dvec, name="gates_fwd_ctx")
    s_zero = jnp.zeros((H, DH, DH), F32)
    _, _, ssf_c, ssb_c, tsf_c, tsb_c, s_f, s_b = _gdn_fwd(kv_c, gb_c, s_zero, s_zero, False, name="gdn_fwd_ctx")

    h_x = _norm_fwd(x, g_pre, mod_x, b_mod, name="norm_fwd_x")
    p_qkv = _mm(h_x, w_qkv, tm=rows, tn=D // 2, name="inproj_qkv")
    p_ab = _mm(h_x, w_ab, name="inproj_ab")
    p_rest = _mm(h_x, w_rest, tm=rows, tn=D // 2, out_dtype=BF16, name="inproj_rest")
    qkv = _prep_fwd(p_qkv, w_conv, 2 * H, name="prep_fwd_x")
    gb_x = _gates_fwd(p_ab, avec, dvec, name="gates_fwd_x")
    o_f, o_b, ssf, ssb, tsf, tsb, _, _ = _gdn_fwd(qkv, gb_x, s_f, s_b, True, name="gdn_fwd_x")

    w_pa, w_pb, w_out = late_weights(o_f)
    (loss_acc, dp_rest, do, dx_res, ya, yb, mg, dma, dmb, dout, dvec_post, dgon, dwsp, dbspb) = _post(
        p_rest, o_f, o_b, x, tgt, mod_x, b_mod, g_onorm, gm_ln_g, gm_ln_b, w_sp, bspb, w_pa, w_pb, w_out, g_post,
        name="post")
    g = {}
    g["w_pa"] = _mm(ya, dma, ta=True, out_dtype=BF16, name="dw_pa")
    g["w_pb"] = _mm(yb, dmb, ta=True, out_dtype=BF16, name="dw_pb")
    g["w_out"] = _mm(mg, dout, ta=True, out_dtype=BF16, name="dw_out")

    zeros_s = jnp.zeros((H, DH, DH), F32)
    dq_f, dq_b, dg_f, dg_b, ds0_f, ds0_b = _gdn_bwd(qkv, gb_x, ssf, ssb, tsf, tsb, do, zeros_s, zeros_s, True,
                                                    name="gdn_bwd_x")
    dp_qkv, dwc_x = _prep_bwd(p_qkv, w_conv, dq_f, dq_b, 2 * H, name="prep_bwd_x")
    dp_ab, dav_x, ddv_x = _gates_bwd(p_ab, avec, dvec, dg_f, dg_b, name="gates_bwd_x")
    dkv_f, dkv_b, dgc_f, dgc_b, _, _ = _gdn_bwd(kv_c, gb_c, ssf_c, ssb_c, tsf_c, tsb_c, jnp.zeros((H, rows_c, DH), F32),
                                                 ds0_f, ds0_b, False, name="gdn_bwd_ctx")
    dpc_kv, dwc_c = _prep_bwd(pc_kv, wconv_kv, dkv_f, dkv_b, H, name="prep_bwd_ctx")
    dpc_ab, dav_c, ddv_c = _gates_bwd(pc_ab, avec, dvec, dgc_f, dgc_b, name="gates_bwd_ctx")

    dw_kv_c = _mm(h_c, dpc_kv, ta=True, name="dw_kv_ctx")
    dw_qkv = _mm(h_x, dp_qkv, ta=True, tn=D, add=dw_kv_c, add_from=1, out_dtype=BF16, name="dw_qkv")
    dw_ab_c = _mm(h_c, dpc_ab, ta=True, name="dw_ab_ctx")
    dw_ab = _mm(h_x, dp_ab, ta=True, add=dw_ab_c, out_dtype=BF16, name="dw_ab")
    dw_rest = _mm(h_x, dp_rest, ta=True, tn=2 * D, out_dtype=BF16, name="dw_rest")
    g["w_in"] = (dw_qkv, dw_ab[:, :4 * H], dw_rest)
    token = on_large_gradients(g) if on_large_gradients is not None else None
    dh = _mm(dp_ab, w_ab, tb=True, after=token, name="dh_ab")
    dh = _mm(dp_qkv, w_qkv, tb=True, tk=3 * D // 2, add=dh, after=token, name="dh_qkv")
    dh = _mm(dp_rest, w_rest, tb=True, tk=3 * D // 2, add=dh, after=token, name="dh_rest")
    grad_x, dgpre_x, dm_x = _norm_bwd(x, g_pre, mod_x, b_mod, [dh], dx_res, name="norm_bwd_x")
    dh_c = _mm(dpc_ab, w_ab, tb=True, after=token, name="dhc_ab")
    dh_c = _mm(dpc_kv, w_kv, tb=True, add=dh_c, after=token, name="dhc_kv")
    _, dgpre_c, dm_c = _norm_bwd(ctx, g_pre, mod_c, b_mod, [dh_c], None, name="norm_bwd_ctx")

    dm_x = dm_x.at[:, 2 * D:].add(dvec_post[1:2])
    dmod = jnp.zeros((16, 3 * D), F32).at[0].set(dm_x[0]).at[1].set(dm_c[0])
    g["mod_factors"] = (scc[0], dm_x[0])
    g["dm_ctx"] = dm_c[0]
    g["silu_c_ctx"] = scc[1:2]
    dcc = _mm(dmod, w_mod, tb=True, b_sections=True, name="dcc")
    g["c_ctx"] = _dsilu_mul(cc[:8], dcc[:8], name="dc_ctx")[1]
    g["b_mod"] = dm_x + dm_c
    g["g_pre"] = dgpre_x + dgpre_c
    g["g_post"] = dvec_post[0:1]
    g["gm_ln_g"], g["gm_ln_b"] = dvec_post[2:3], dvec_post[3:4]
    g["g_onorm"] = dgon[0:1]
    g["w_sp"] = dwsp
    g["b_sp"] = jnp.sum(dbspb, axis=-1)
    g["w_conv"] = dwc_x.at[:, D:].add(dwc_c)
    g["a_log"] = (dav_x + dav_c)[0, :2 * H].reshape(2, H)
    g["dt_bias"] = (ddv_x + ddv_c)[0, :2 * H].reshape(2, H)
    return loss_acc[0, 0], grad_x, g


ANY = pl.BlockSpec(memory_space=pl.ANY)


def _place():
    x, y, c = lax.axis_index("x"), lax.axis_index("y"), lax.axis_index("c")
    chips = [(1 - x, y), (x, 1 - y), (1 - x, 1 - y)]
    return x, y, c, (x, y, 1 - c), chips


def _gather_shards(big, small, *, name):
    nb = len(big)

    def body(*refs):
        ins, sm_in = refs[:nb], refs[nb]
        outs, sm_out = refs[nb + 1:2 * nb + 1], refs[2 * nb + 1]
        send, recv = refs[2 * nb + 2:]
        x, y, c, sibling, chips = _place()
        mine = 2 * x + y

        def half(a, shard, hc):
            hr = big[a].shape[0] // 2
            return outs[a].at[shard, pl.ds(hc * hr, hr), :]

        def remote(k, src, dst, to):
            return pltpu.make_async_remote_copy(src_ref=src, dst_ref=dst, send_sem=send.at[k], recv_sem=recv.at[k],
                                                device_id=to, device_id_type=MESH)

        sends = []
        for a in range(nb):
            hr = big[a].shape[0] // 2
            for j, chip in enumerate(chips):
                sends.append(remote(a * 3 + j, ins[a].at[pl.ds(c * hr, hr), :], half(a, mine, c), (*chip, c)))
        for j, chip in enumerate(chips):
            sends.append(remote(nb * 3 + j, sm_in, sm_out.at[mine], (*chip, c)))
        for cp in sends:
            cp.start()
        base = nb * 3 + 3
        passed = []
        for a in range(nb):
            for j, (px, py) in enumerate(chips):
                theirs = 2 * px + py
                remote(a * 3 + j, half(a, theirs, c), half(a, theirs, c), sibling).wait_recv()
                fw = remote(base + a * 3 + j, half(a, theirs, c), half(a, theirs, c), sibling)
                fw.start()
                passed.append(fw)
        for a in range(nb):
            for j, (px, py) in enumerate(chips):
                theirs = 2 * px + py
                remote(base + a * 3 + j, half(a, theirs, 1 - c), half(a, theirs, 1 - c), sibling).wait_recv()
        for j, (px, py) in enumerate(chips):
            remote(nb * 3 + j, sm_in, sm_out.at[2 * px + py], sibling).wait_recv()
        for cp in sends + passed:
            cp.wait_send()

    n_remote = 2 * nb * 3 + 3
    outs = pl.pallas_call(
        body, name=name, in_specs=[ANY] * (nb + 1), out_specs=[ANY] * (nb + 1),
        out_shape=[jax.ShapeDtypeStruct((N_CHIPS,) + a.shape, a.dtype) for a in big + [small]],
        scratch_shapes=[pltpu.SemaphoreType.DMA((n_remote,)), pltpu.SemaphoreType.DMA((n_remote,))],
    )(*big, small)
    return outs[:nb], outs[nb]


def _gather_late_start(arrs, after, *, name):
    na = len(arrs)

    def body(*refs):
        ins, land = refs[:na], refs[na:2 * na]
        send, recv = refs[2 * na + 1], refs[2 * na + 2]
        token = refs[-1]
        x, y, c, _, chips = _place()
        for a in range(na):
            hr = arrs[a].shape[0] // 2
            for j, (px, py) in enumerate(chips):
                for other in range(2):
                    k = (a * 3 + j) * 2 + other
                    _remote(ins[a].at[pl.ds(c * hr, hr), :], land[a].at[2 * x + y, pl.ds(c * hr, hr), :],
                            send.at[k], recv.at[k], (px, py, c ^ other)).start()
        token[...] = jnp.zeros_like(token)

    lands = [((N_CHIPS,) + a.shape, a.dtype) for a in arrs]
    held = lambda a: pltpu.with_memory_space_constraint(a, pltpu.HBM)
    outs = pl.pallas_call(
        body, name=name, in_specs=[HBM] * (2 * na) + [ANY],
        out_specs=[SEM, SEM] + [HBM] * (2 * na) + [pl.BlockSpec(memory_space=pltpu.VMEM)],
        out_shape=[pltpu.SemaphoreType.DMA((6 * na,)), pltpu.SemaphoreType.DMA((6 * na,))]
        + [pltpu.HBM(a.shape, a.dtype) for a in arrs] + [pltpu.HBM(s, d) for s, d in lands]
        + [jax.ShapeDtypeStruct((8, 128), F32)],
        input_output_aliases={i: 2 + i for i in range(2 * na)},
        compiler_params=pltpu.CompilerParams(has_side_effects=pltpu.SideEffectType.DATAFLOW_SIDE_EFFECTING),
    )(*[held(a) for a in arrs], *[held(lax.empty(s, d)) for s, d in lands], after)
    return outs[0], outs[1], outs[2:2 + na], outs[2 + na:2 + 2 * na], outs[-1]


def _gather_late_wait(send, recv, thru, land, after, *, name):
    na = len(thru)

    def body(*refs):
        ins, slots = refs[:na], refs[na:2 * na]
        send_sem, recv_sem = refs[2 * na], refs[2 * na + 1]
        x, y, c, _, chips = _place()
        for a in range(na):
            hr = thru[a].shape[0] // 2
            for j, (px, py) in enumerate(chips):
                for other in range(2):
                    k = (a * 3 + j) * 2 + other
                    half = c ^ other
                    cp = _remote(ins[a].at[pl.ds(c * hr, hr), :], slots[a].at[2 * px + py, pl.ds(half * hr, hr), :],
                                 send_sem.at[k], recv_sem.at[k], (px, py, half))
                    cp.wait_send()
                    cp.wait_recv()

    outs = pl.pallas_call(
        body, name=name, in_specs=[HBM] * (2 * na) + [SEM, SEM, ANY], out_specs=[HBM] * (2 * na),
        out_shape=[pltpu.HBM(a.shape, a.dtype) for a in list(thru) + list(land)],
        input_output_aliases={i: i for i in range(2 * na)},
        compiler_params=pltpu.CompilerParams(has_side_effects=pltpu.SideEffectType.DATAFLOW_SIDE_EFFECTING),
    )(*thru, *land, send, recv, after)
    return outs[na:]


def _row_chunks(rows, row_bytes, align=8):
    n = max(1, min(rows // align, -(-rows * row_bytes // DMA_CHUNK_BYTES)))
    per = -(-(-(-rows // n)) // align) * align
    return [(r, min(per, rows - r)) for r in range(0, rows, per)]


def _remote(src, dst, send, recv, to):
    return pltpu.make_async_remote_copy(src_ref=src, dst_ref=dst, send_sem=send, recv_sem=recv, device_id=to,
                                        device_id_type=MESH)


def _exchange_halves(arrs, *, name):
    na = len(arrs)

    def body(*refs):
        ins, got = refs[:na], refs[na:2 * na]
        send, recv = refs[2 * na:]
        x, y, c, sibling, _ = _place()
        for a in range(na):
            ns, rows, cols = arrs[a].shape
            hr = rows // 2
            for s in range(ns):
                for r0, nr in _row_chunks(hr, cols * arrs[a].dtype.itemsize, 16):
                    _remote(ins[a].at[s, pl.ds((1 - c) * hr + r0, nr), :], got[a].at[s, pl.ds(r0, nr), :],
                            send.at[a], recv.at[a], sibling).start()
        for a in range(na):
            hr = arrs[a].shape[1] // 2
            _remote(ins[a].at[:, pl.ds((1 - c) * hr, hr), :], got[a], send.at[a], recv.at[a], sibling).wait()

    return pl.pallas_call(
        body, name=name, in_specs=[ANY] * na, out_specs=[ANY] * na,
        out_shape=[jax.ShapeDtypeStruct((N_CHIPS, a.shape[1] // 2, a.shape[2]), a.dtype) for a in arrs],
        scratch_shapes=[pltpu.SemaphoreType.DMA((na,)), pltpu.SemaphoreType.DMA((na,))],
    )(*arrs)


def _scatter_sections(arrs, *, name):
    na = len(arrs)

    def body(*refs):
        ins, outs = refs[:na], refs[na:2 * na]
        send, recv = refs[2 * na:]
        x, y, c, _, chips = _place()
        for a in range(na):
            _, hr, cols = arrs[a].shape
            for j, (px, py) in enumerate(chips):
                for r0, nr in _row_chunks(hr, cols * arrs[a].dtype.itemsize, 16):
                    _remote(ins[a].at[2 * px + py, pl.ds(r0, nr), :], outs[a].at[j, pl.ds(r0, nr), :],
                            send.at[a * 3 + j], recv.at[a * 3 + j], (px, py, c)).start()
        for a in range(na):
            for j, (px, py) in enumerate(chips):
                _remote(ins[a].at[2 * px + py], outs[a].at[j], send.at[a * 3 + j], recv.at[a * 3 + j], (px, py, c)).wait()

    return pl.pallas_call(
        body, name=name, in_specs=[ANY] * na, out_specs=[ANY] * na,
        out_shape=[jax.ShapeDtypeStruct((N_CHIPS - 1,) + a.shape[1:], a.dtype) for a in arrs],
        scratch_shapes=[pltpu.SemaphoreType.DMA((3 * na,)), pltpu.SemaphoreType.DMA((3 * na,))],
    )(*arrs)


HBM = pl.BlockSpec(memory_space=pltpu.HBM)
SEM = pl.BlockSpec(memory_space=pltpu.SEMAPHORE)


def _scatter_start(arrs, *, name):
    na = len(arrs)

    def body(*refs):
        ins, land = refs[:na], refs[na:2 * na]
        send, recv = refs[2 * na], refs[2 * na + 1]
        token = refs[-1]
        x, y, c, _, chips = _place()
        for a in range(na):
            _, hr, cols = arrs[a].shape
            for j, (px, py) in enumerate(chips):
                for r0, nr in _row_chunks(hr, cols * arrs[a].dtype.itemsize, 16):
                    _remote(ins[a].at[2 * px + py, pl.ds(r0, nr), :], land[a].at[j, pl.ds(r0, nr), :],
                            send.at[a * 3 + j], recv.at[a * 3 + j], (px, py, c)).start()
        token[...] = jnp.zeros_like(token)

    slots = [((N_CHIPS - 1,) + a.shape[1:], a.dtype) for a in arrs]
    held = lambda a: pltpu.with_memory_space_constraint(a, pltpu.HBM)
    outs = pl.pallas_call(
        body, name=name, in_specs=[HBM] * (2 * na),
        out_specs=[SEM, SEM] + [HBM] * (2 * na) + [pl.BlockSpec(memory_space=pltpu.VMEM)],
        out_shape=[pltpu.SemaphoreType.DMA((3 * na,)), pltpu.SemaphoreType.DMA((3 * na,))]
        + [pltpu.HBM(a.shape, a.dtype) for a in arrs] + [pltpu.HBM(s, d) for s, d in slots]
        + [jax.ShapeDtypeStruct((8, 128), F32)],
        input_output_aliases={i: 2 + i for i in range(2 * na)},
        compiler_params=pltpu.CompilerParams(has_side_effects=pltpu.SideEffectType.DATAFLOW_SIDE_EFFECTING),
    )(*[held(a) for a in arrs], *[held(lax.empty(s, d)) for s, d in slots])
    return outs[0], outs[1], outs[2:2 + na], outs[2 + na:2 + 2 * na], outs[-1]


def _scatter_wait(send, recv, thru, land, after, *, name):
    na = len(thru)

    def body(*refs):
        ins, slots = refs[:na], refs[na:2 * na]
        send_sem, recv_sem = refs[2 * na], refs[2 * na + 1]
        x, y, c, _, chips = _place()
        for a in range(na):
            for j, (px, py) in enumerate(chips):
                cp = _remote(ins[a].at[2 * px + py], slots[a].at[j], send_sem.at[a * 3 + j], recv_sem.at[a * 3 + j],
                             (px, py, c))
                cp.wait_send()
                cp.wait_recv()

    outs = pl.pallas_call(
        body, name=name, in_specs=[HBM] * (2 * na) + [SEM, SEM, ANY], out_specs=[HBM] * (2 * na),
        out_shape=[pltpu.HBM(a.shape, a.dtype) for a in list(thru) + list(land)],
        input_output_aliases={i: i for i in range(2 * na)},
        compiler_params=pltpu.CompilerParams(has_side_effects=pltpu.SideEffectType.DATAFLOW_SIDE_EFFECTING),
    )(*thru, *land, send, recv, after)
    return outs[:na], outs[na:]


def _finish_reduce(big, small, *, name):
    nb = len(big)

    def body(*refs):
        outs, sm = refs[nb + 1:2 * nb + 1], refs[2 * nb + 1]
        send, recv = refs[2 * nb + 2:]
        x, y, c, sibling, chips = _place()
        blk = lambda px, py, pc: sm.at[4 * px + 2 * py + pc]
        for a in range(nb):
            _, hr, cols = big[a].shape
            for r0, nr in _row_chunks(hr, cols * 4):
                _remote(outs[a].at[c, pl.ds(r0, nr), :], outs[a].at[c, pl.ds(r0, nr), :], send.at[a], recv.at[a],
                        sibling).start()
        first = [_remote(blk(x, y, c), blk(x, y, c), send.at[nb], recv.at[nb], sibling)]
        first += [_remote(blk(x, y, c), blk(x, y, c), send.at[nb + 1 + j], recv.at[nb + 1 + j], (*chip, c))
                  for j, chip in enumerate(chips)]
        for cp in first:
            cp.start()
        passed = []
        for j, (px, py) in enumerate(chips):
            _remote(blk(px, py, c), blk(px, py, c), send.at[nb + 1 + j], recv.at[nb + 1 + j], sibling).wait_recv()
            fw = _remote(blk(px, py, c), blk(px, py, c), send.at[nb + 4 + j], recv.at[nb + 4 + j], sibling)
            fw.start()
            passed.append(fw)
        for a in range(nb):
            _remote(outs[a].at[c], outs[a].at[1 - c], send.at[a], recv.at[a], sibling).wait()
        _remote(blk(x, y, c), blk(x, y, 1 - c), send.at[nb], recv.at[nb], sibling).wait_recv()
        for j, (px, py) in enumerate(chips):
            _remote(blk(px, py, c), blk(px, py, 1 - c), send.at[nb + 4 + j], recv.at[nb + 4 + j], sibling).wait_recv()
        for cp in first + passed:
            cp.wait_send()

    n_remote = nb + 7
    arrs = list(big) + [small]
    outs = pl.pallas_call(
        body, name=name, in_specs=[ANY] * (nb + 1), out_specs=[ANY] * (nb + 1),
        out_shape=[jax.ShapeDtypeStruct(a.shape, F32) for a in arrs],
        input_output_aliases={i: i for i in range(nb + 1)},
        scratch_shapes=[pltpu.SemaphoreType.DMA((n_remote,)), pltpu.SemaphoreType.DMA((n_remote,))],
    )(*arrs)
    return outs[:nb], outs[nb]


def _reduce_start(sectioned, where):
    got = _exchange_halves(sectioned, name="rs_exchange_halves")
    chip_sum = [_add_sibling(a, g, where, BF16, name=f"rs_add_sibling_{i}")
                for i, (a, g) in enumerate(zip(sectioned, got))]
    send, recv, thru, land, token = _scatter_start(chip_sum, name="rs_scatter_start")
    return (send, recv, thru, land), token


def _reduce_finish(handle, small, where, after):
    send, recv, thru, land = handle
    got = _exchange_halves([small], name="rs_exchange_small")
    small_sum = _add_sibling(small, got[0], where, F32, name="rs_add_sibling_small")
    small_slots = _scatter_sections([small_sum], name="rs_scatter_small")[0]
    own, slots = _scatter_wait(send, recv, thru, land, after, name="rs_scatter_wait")
    red = [_sum_chips(p, s, where, 2, 0, name=f"rs_sum_chips_{i}") for i, (p, s) in enumerate(zip(own, slots))]
    red_small = _sum_chips(small_sum, small_slots, where, 2 * N_CHIPS, 2, name="rs_sum_chips_small")
    big, sm = _finish_reduce(red, red_small, name="rs_finish")
    return big, sm.reshape(-1, sm.shape[-1])


def kernel(x, c, ctx, c_ctx, w_mod, b_mod, g_pre, g_post, w_in, w_conv, a_log, dt_bias, g_onorm, gm_ln_g, gm_ln_b, w_sp, b_sp, w_pa, w_pb, w_out, loss_target, m_c_ctx, m_w_mod, m_b_mod, m_g_pre, m_g_post, m_w_in, m_w_conv, m_a_log, m_dt_bias, m_g_onorm, m_gm_ln_g, m_gm_ln_b, m_w_sp, m_b_sp, m_w_pa, m_w_pb, m_w_out, v_c_ctx, v_w_mod, v_b_mod, v_g_pre, v_g_post, v_w_in, v_w_conv, v_a_log, v_dt_bias, v_g_onorm, v_gm_ln_g, v_gm_ln_b, v_w_sp, v_b_sp, v_w_pa, v_w_pb, v_w_out):
    names = ["c_ctx", "w_mod", "b_mod", "g_pre", "g_post", "w_in", "w_conv", "a_log", "dt_bias", "g_onorm", "gm_ln_g",
             "gm_ln_b", "w_sp", "b_sp", "w_pa", "w_pb", "w_out"]
    w = dict(zip(names, (c_ctx, w_mod, b_mod, g_pre, g_post, w_in, w_conv, a_log, dt_bias, g_onorm, gm_ln_g, gm_ln_b,
                         w_sp, b_sp, w_pa, w_pb, w_out)))
    m = dict(zip(names, (m_c_ctx, m_w_mod, m_b_mod, m_g_pre, m_g_post, m_w_in, m_w_conv, m_a_log, m_dt_bias, m_g_onorm,
                         m_gm_ln_g, m_gm_ln_b, m_w_sp, m_b_sp, m_w_pa, m_w_pb, m_w_out)))
    v = dict(zip(names, (v_c_ctx, v_w_mod, v_b_mod, v_g_pre, v_g_post, v_w_in, v_w_conv, v_a_log, v_dt_bias, v_g_onorm,
                         v_gm_ln_g, v_gm_ln_b, v_w_sp, v_b_sp, v_w_pa, v_w_pb, v_w_out)))
    xy = 2 * lax.axis_index("x") + lax.axis_index("y")
    where = jnp.stack([lax.axis_index("c"), xy, 2 * xy + lax.axis_index("c")]).astype(jnp.int32)

    shards = [a[0].astype(BF16) for a in (w_mod, w_in)]
    gathered, wconv_all = _gather_shards(shards, w_conv[0], name="gather_weights")
    own = lambda full, shard: lax.dynamic_update_slice(full, shard[None], (xy, 0, 0))
    wm_all, win_all = [own(f, s) for f, s in zip(gathered, shards)]
    wconv_all = own(wconv_all, w_conv[0])
    late_shards = [a[0].astype(BF16) for a in (w_pa, w_pb, w_out)]
    late = _gather_late_start(late_shards, wconv_all, name="gather_late_start")

    def late_weights(after):
        lands = _gather_late_wait(*late[:4], after, name="gather_late_wait")
        return [own(f, s).reshape(D, D) for f, s in zip(lands, late_shards)]
    w_conv_f = jnp.concatenate([wconv_all[s] for s in range(N_CHIPS)], axis=1)
    shard_cols = IN_COLS // N_CHIPS
    cut = OFF_A - shard_cols
    assert 0 < cut and cut + 4 * H < shard_cols
    w_qkv = jnp.concatenate([win_all[0], win_all[1][:, :cut]], axis=1)
    w_ab = jnp.pad(win_all[1][:, cut:cut + 4 * H], ((0, 0), (0, DH - 4 * H)))
    w_rest = jnp.concatenate([win_all[1][:, cut + 4 * H:], win_all[2], win_all[3]], axis=1)

    blk = D // N_CHIPS
    rest_cut = shard_cols - cut - 4 * H
    big_names = ("w_in", "w_pa", "w_pb", "w_out")
    in_flight = []

    def start_reduce(grads):
        dw_qkv, dw_ab, dw_rest = grads["w_in"]
        g_win = jnp.stack([dw_qkv[:, :shard_cols],
                           jnp.concatenate([dw_qkv[:, shard_cols:], dw_ab, dw_rest[:, :rest_cut]], axis=1),
                           dw_rest[:, rest_cut:rest_cut + shard_cols], dw_rest[:, rest_cut + shard_cols:]]).astype(BF16)
        sectioned = [g_win] + [grads[k].reshape(N_CHIPS, blk, D) for k in ("w_pa", "w_pb", "w_out")]
        handle, token = _reduce_start(sectioned, where)
        in_flight.append(handle)
        return token

    loss_local, grad_x, g = _local_step(
        x[0], c, ctx[0], c_ctx, loss_target[0], wm_all, b_mod, g_pre, g_post, w_qkv, w_ab, w_rest,
        w_conv_f, a_log[0], dt_bias[0], g_onorm, gm_ln_g, gm_ln_b, w_sp[0], b_sp[0],
        late_weights, on_large_gradients=start_reduce, run_after=late[4])
    g["loss"] = loss_local
    g["_pad"] = jnp.zeros((SMALL_LAYOUT["_pad"][1],), F32)
    g["mod_factors"] = lax.dynamic_update_slice(jnp.zeros((2 * N_CHIPS, 4 * D), F32),
                                                jnp.concatenate(g["mod_factors"])[None], (where[2], 0))
    tail = jnp.zeros((N_CHIPS * SMALL_ROWS * 128 - sum(s for _, s in SMALL_LAYOUT.values()),), F32)
    flat = jnp.concatenate([g[k].reshape(-1) for k in SMALL_LAYOUT] + [tail])
    reduced, gr_small = _reduce_finish(in_flight[0], flat.reshape(N_CHIPS, SMALL_ROWS, 128), where, g["b_mod"])
    gr_tiny = gr_small[TINY_ROW0:]

    def entry(arr, k):
        off, size = SMALL_LAYOUT[k]
        row, col = off // 128 - TINY_ROW0, off % 128
        return arr[row:row + size // 128].reshape(-1) if size >= 128 else arr[row, col:col + size]

    factors = entry(gr_tiny, "mod_factors").reshape(2 * N_CHIPS, 4 * D)
    pad_rows = lambda a: jnp.pad(a, ((0, 16 - a.shape[0]), (0, 0)))
    lhs = pad_rows(jnp.concatenate([factors[:, :D], g["silu_c_ctx"]], axis=0))
    rhs = pad_rows(jnp.concatenate([factors[:, D:], entry(gr_tiny, "dm_ctx")[None]], axis=0))
    mod_cols = 3 * D // N_CHIPS
    gr_wm = _mm(lhs, lax.dynamic_slice(rhs, (0, xy * mod_cols), (16, mod_cols)), ta=True, name="dw_mod")

    res = {k: _adamw(w[k], gr, m[k], v[k], name=f"adamw_{k}") for k, gr in zip(big_names[1:], reduced[1:])}
    flip = lambda a: jnp.swapaxes(a, 1, 2)
    res["w_in"] = [flip(r) for r in _adamw(flip(w_in), flip(reduced[0].reshape(w_in.shape)), flip(m_w_in), flip(v_w_in),
                                           name="adamw_w_in")]
    res["w_mod"] = _adamw(w_mod[0], gr_wm, m_w_mod[0], v_w_mod[0], name="adamw_w_mod")
    res["w_sp"] = _adamw(w_sp.reshape(-1, 128), gr_small, m_w_sp.reshape(-1, 128), v_w_sp.reshape(-1, 128),
                         name="adamw_w_sp")
    tiny = [k for k, (off, _) in SMALL_LAYOUT.items() if TINY_ROW0 <= off // 128 < TINY_ROW0 + TINY_ROWS]
    tiny_names = [k for k in tiny if k not in ("loss", "_pad")]

    def pack(src):
        parts = [src[k].reshape(-1) if k in tiny_names else jnp.zeros((SMALL_LAYOUT[k][1],), F32) for k in tiny]
        return jnp.concatenate(parts).reshape(TINY_ROWS, 128)

    tiny_res = _adamw(pack(w), gr_tiny[:TINY_ROWS], pack(m), pack(v), name="adamw_tiny")
    for k in tiny_names:
        res[k] = [entry(r, k) for r in tiny_res]
    g_conv = lax.dynamic_slice(entry(gr_tiny, "w_conv").reshape(3, 3 * D), (0, xy * (3 * D // N_CHIPS)),
                               (3, 3 * D // N_CHIPS))
    conv_res = _adamw(jnp.pad(w_conv[0], ((0, 5), (0, 0))), jnp.pad(g_conv, ((0, 5), (0, 0))),
                      jnp.pad(m_w_conv[0], ((0, 5), (0, 0))), jnp.pad(v_w_conv[0], ((0, 5), (0, 0))), name="adamw_w_conv")
    res["w_conv"] = [r[:3] for r in conv_res]
    res = {k: [r.reshape(w[k].shape) for r in res[k]] for k in names}

    out = [entry(gr_tiny, "loss").reshape(()), grad_x[None]]
    for i in range(4):
        out += [res[k][i] for k in names]
    return tuple(out)
```

```python
import functools

import jax
import jax.numpy as jnp
from jax import lax
from jax.experimental import pallas as pl
from jax.experimental.pallas import tpu as pltpu

F32 = jnp.float32
BF16 = jnp.bfloat16
HI = lax.Precision.HIGHEST
MESH = pl.DeviceIdType.MESH

D = 1024
H = 8
DH = 128
CH = 64
LOG_CH = 6
PAIR = 2 * CH
GM = 128
assert 1 << LOG_CH == CH and PAIR == DH
PREC_POWERS = ((lax.Precision.HIGH, lax.Precision.HIGH),) * 3 + ((None, None),) * 2
assert len(PREC_POWERS) == LOG_CH - 1
EPS = 1e-6
N_CHIPS = 4
OFF_A = 3 * D
OFF_ZB = OFF_A + 4 * H
IN_COLS = OFF_ZB + 6 * D
VMEM_LIMIT_V7X = 56 * 1024 * 1024
DMA_CHUNK_BYTES = 2 * 1024 * 1024

ADAM_LR, ADAM_B1, ADAM_B2, ADAM_EPS, ADAM_WD, ADAM_STEP = 0.001, 0.9, 0.999, 1e-08, 0.01, 10

SMALL_LAYOUT = {}
_off = 0
for _n, _s in (("w_sp", H * GM * GM), ("c_ctx", D), ("b_mod", 3 * D), ("g_pre", D), ("g_post", D), ("gm_ln_g", D),
               ("gm_ln_b", D), ("b_sp", H * GM), ("g_onorm", DH), ("a_log", 2 * H), ("dt_bias", 2 * H), ("loss", 1),
               ("_pad", 128 - 4 * H - 1 + 6 * 128), ("w_conv", 3 * 3 * D), ("dm_ctx", 3 * D), ("mod_factors", 8 * 4 * D)):
    SMALL_LAYOUT[_n] = (_off, _s)
    _off += _s
SMALL_ROWS = 368
assert N_CHIPS * SMALL_ROWS * 128 >= _off and (SMALL_ROWS // 2) % 8 == 0 and _off % 128 == 0
TINY_ROW0 = SMALL_LAYOUT["c_ctx"][0] // 128
TINY_ROWS = SMALL_LAYOUT["w_conv"][0] // 128 - TINY_ROW0
assert TINY_ROWS % 8 == 0


def _params(sem=None):
    return pltpu.CompilerParams(dimension_semantics=sem, vmem_limit_bytes=VMEM_LIMIT_V7X)


def _tile(n, cands=(256, 128, 64, 32, 16, 8)):
    if n <= cands[0]:
        return n
    for cand in cands:
        if n % cand == 0 and cand >= 64:
            return cand
    return max(d for d in range(8, cands[0], 8) if n % d == 0)


def _silu(x):
    return x * jax.nn.sigmoid(x)


def _gelu(x):
    return 0.5 * x * (1.0 + jnp.tanh(0.7978845608028654 * (x + 0.044715 * (x * x * x))))


def _mm(a, b, *, ta=False, tb=False, out_dtype=F32, tm=1024, tn=1024, tk=1024, add=None, add_from=0,
        b_sections=False, out_sections=False, after=None, name):
    m, k = (a.shape[1], a.shape[0]) if ta else a.shape
    if b_sections:
        sect = b.shape[2]
        n = b.shape[1] if tb else b.shape[0] * sect
        tn, tk = (tn, sect) if tb else (sect, tk)
    else:
        n = b.shape[0] if tb else b.shape[1]
    tm, tn, tk = min(tm, m), min(tn, n), min(tk, k)
    assert m % tm == 0 and n % tn == 0 and k % tk == 0, (name, m, n, k, tm, tn, tk)
    nk = k // tk
    dims = (((0,) if ta else (1,), (1,) if tb else (0,)), ((), ()))
    has_add = add is not None
    assert not has_add or add.shape == (m, n - add_from * tn), (name, add.shape)

    def body(*refs):
        a_ref, b_ref = refs[:2]
        o_ref = refs[2 + int(has_add) + int(after is not None)]
        acc_ref = refs[-1]
        kk = pl.program_id(2)
        bv = b_ref[0] if b_sections else b_ref[...]
        part = lax.dot_general(a_ref[...].astype(BF16), bv.astype(BF16), dims, preferred_element_type=F32)

        def finish(res):
            if has_add:
                res = res + jnp.where(pl.program_id(1) >= add_from, refs[2][...], 0.0)
            if out_sections:
                o_ref[0] = res.astype(out_dtype)
            else:
                o_ref[...] = res.astype(out_dtype)

        if nk == 1:
            finish(part)
            return

        @pl.when(kk == 0)
        def _():
            acc_ref[...] = part

        @pl.when((kk > 0) & (kk < nk - 1))
        def _():
            acc_ref[...] += part

        @pl.when(kk == nk - 1)
        def _():
            finish(acc_ref[...] + part)

    a_spec = pl.BlockSpec((tk, tm), lambda i, j, q: (q, i)) if ta else pl.BlockSpec((tm, tk), lambda i, j, q: (i, q))
    if b_sections:
        b_spec = (pl.BlockSpec((1, tn, tk), lambda i, j, q: (q, j, 0)) if tb
                  else pl.BlockSpec((1, tk, tn), lambda i, j, q: (j, q, 0)))
    else:
        b_spec = pl.BlockSpec((tn, tk), lambda i, j, q: (j, q)) if tb else pl.BlockSpec((tk, tn), lambda i, j, q: (q, j))
    add_spec = [pl.BlockSpec((tm, tn), lambda i, j, q: (i, jnp.maximum(j - add_from, 0)))] if has_add else []
    if out_sections:
        out_spec, out_shape = pl.BlockSpec((1, tm, tn), lambda i, j, q: (j, i, 0)), (n // tn, m, tn)
    else:
        out_spec, out_shape = pl.BlockSpec((tm, tn), lambda i, j, q: (i, j)), (m, n)
    return pl.pallas_call(
        body, name=name, grid=(m // tm, n // tn, nk),
        in_specs=[a_spec, b_spec] + add_spec + ([pl.BlockSpec(memory_space=pl.ANY)] if after is not None else []),
        out_specs=out_spec, out_shape=jax.ShapeDtypeStruct(out_shape, out_dtype),
        scratch_shapes=[pltpu.VMEM((tm, tn), F32)] if nk > 1 else [],
        compiler_params=_params(("parallel", "parallel", "arbitrary")),
    )(*([a, b] + ([add] if has_add else []) + ([after] if after is not None else [])))


def _h_fn(x, g, m):
    shift, scale = m[:, 0:D], m[:, D:2 * D]
    r = lax.rsqrt(jnp.mean(x * x, axis=-1, keepdims=True) + EPS)
    return (x * r * g) * (1.0 + scale) + shift


def _norm_fwd(x, g, mod, bmod, *, name):
    rows = x.shape[0]
    t = min(512, rows)

    def body(x_ref, g_ref, m_ref, b_ref, h_ref):
        h_ref[...] = _h_fn(x_ref[...], g_ref[...], m_ref[...] + b_ref[...]).astype(BF16)

    vec = lambda w: pl.BlockSpec((1, w), lambda i: (0, 0))
    return pl.pallas_call(
        body, name=name, grid=(rows // t,),
        in_specs=[pl.BlockSpec((t, D), lambda i: (i, 0)), vec(D), vec(3 * D), vec(3 * D)],
        out_specs=pl.BlockSpec((t, D), lambda i: (i, 0)),
        out_shape=jax.ShapeDtypeStruct((rows, D), BF16),
        compiler_params=_params(("parallel",)),
    )(x, g, mod, bmod)


def _norm_bwd(x, g, mod, bmod, dh_parts, resid, *, name):
    rows = x.shape[0]
    t = min(512, rows)
    n_parts = len(dh_parts)
    has_resid = resid is not None

    def body(*refs):
        x_ref, g_ref, m_ref, b_ref = refs[:4]
        parts = refs[4:4 + n_parts]
        r_ref = refs[4 + n_parts] if has_resid else None
        dx_ref, dg_ref, dm_ref = refs[-3:]
        i = pl.program_id(0)
        dh = parts[0][...]
        for p in parts[1:]:
            dh = dh + p[...]
        _, vjp = jax.vjp(_h_fn, x_ref[...], g_ref[...], m_ref[...] + b_ref[...])
        dx, dg, dm = vjp(dh)
        if has_resid:
            dx = dx + r_ref[...]
        dx_ref[...] = dx

        @pl.when(i == 0)
        def _():
            dg_ref[...] = dg
            dm_ref[...] = dm

        @pl.when(i > 0)
        def _():
            dg_ref[...] += dg
            dm_ref[...] += dm

    vec = lambda w: pl.BlockSpec((1, w), lambda i: (0, 0))
    tile = pl.BlockSpec((t, D), lambda i: (i, 0))
    ins = [x, g, mod, bmod, *dh_parts] + ([resid] if has_resid else [])
    return pl.pallas_call(
        body, name=name, grid=(rows // t,),
        in_specs=[tile, vec(D), vec(3 * D), vec(3 * D)] + [tile] * (n_parts + int(has_resid)),
        out_specs=[tile, vec(D), vec(3 * D)],
        out_shape=[jax.ShapeDtypeStruct((rows, D), F32), jax.ShapeDtypeStruct((1, D), F32),
                   jax.ShapeDtypeStruct((1, 3 * D), F32)],
        compiler_params=_params(("arbitrary",)),
    )(*ins)


def _conv_tile(u_ref, r0, t, rows, w0, w1, w2):
    u = u_ref[pl.ds(r0, t), :]
    prev8 = u_ref[pl.ds(pl.multiple_of(jnp.maximum(r0 - 8, 0), 8), 8), :]
    next8 = u_ref[pl.ds(pl.multiple_of(jnp.minimum(r0 + t, rows - 8), 8), 8), :]
    r8 = lax.broadcasted_iota(jnp.int32, (8, DH), 0)
    prev_row = jnp.sum(jnp.where(r8 == 7, prev8, 0.0), axis=0, keepdims=True)
    next_row = jnp.sum(jnp.where(r8 == 0, next8, 0.0), axis=0, keepdims=True)
    prev_row = jnp.where(r0 > 0, prev_row, 0.0)
    next_row = jnp.where(r0 + t < rows, next_row, 0.0)
    ri = lax.broadcasted_iota(jnp.int32, (t, DH), 0)
    um1 = jnp.where(ri == 0, prev_row, pltpu.roll(u, 1, 0))
    up1 = jnp.where(ri == t - 1, next_row, pltpu.roll(u, t - 1, 0))
    return w0 * um1 + w1 * u + w2 * up1, um1, u, up1


def _rowlocal(z, norm):
    y = _silu(z)
    return y * lax.rsqrt(jnp.sum(y * y, axis=-1, keepdims=True) + EPS) if norm else y


def _prep_fwd(p, wconv, n_norm, *, name):
    rows, nb = p.shape[0], p.shape[1] // DH
    t = min(512, rows)
    out = None
    for norm, b0, b1 in ((True, 0, n_norm), (False, n_norm, nb)):
        def body(u_ref, w_ref, *rest, norm=norm):
            o_ref = rest[-1]
            w0, w1, w2 = w_ref[0:1, :], w_ref[1:2, :], w_ref[2:3, :]

            def step(s, carry):
                r0 = pl.multiple_of(s * t, t)
                z, _, _, _ = _conv_tile(u_ref, r0, t, rows, w0, w1, w2)
                o_ref[0, pl.ds(r0, t), :] = _rowlocal(z, norm)
                return carry

            lax.fori_loop(0, rows // t, step, 0)

        held = [] if out is None else [out]
        out = pl.pallas_call(
            body, name=f"{name}_{'norm' if norm else 'plain'}", grid=(b1 - b0,),
            in_specs=[pl.BlockSpec((rows, DH), lambda j, b0=b0: (0, b0 + j)),
                      pl.BlockSpec((3, DH), lambda j, b0=b0: (0, b0 + j))] + [ANY] * len(held),
            out_specs=pl.BlockSpec((1, rows, DH), lambda j, b0=b0: (b0 + j, 0, 0)),
            out_shape=jax.ShapeDtypeStruct((nb, rows, DH), F32),
            input_output_aliases={2: 0} if held else {},
            compiler_params=_params(("parallel",)),
        )(p, wconv, *held)
    return out


def _prep_bwd(p, wconv, d_a, d_b, n_norm, *, name):
    rows, nb = p.shape[0], p.shape[1] // DH
    t = min(512, rows)
    outs = []
    for norm, b0, b1 in ((True, 0, n_norm), (False, n_norm, nb)):
        def body(u_ref, w_ref, da_ref, db_ref, *rest, norm=norm):
            du_ref, dw_ref, dz_ref = rest[-3:]
            w0, w1, w2 = w_ref[0:1, :], w_ref[1:2, :], w_ref[2:3, :]

            def step1(s, carry):
                a0, a1, a2 = carry
                r0 = pl.multiple_of(s * t, t)
                z, um1, u, up1 = _conv_tile(u_ref, r0, t, rows, w0, w1, w2)
                _, vjp = jax.vjp(lambda zz: _rowlocal(zz, norm), z)
                (dz,) = vjp(da_ref[0, pl.ds(r0, t), :] + db_ref[0, pl.ds(r0, t), :])
                dz_ref[pl.ds(r0, t), :] = dz
                red = lambda v: jnp.sum(v, axis=0, keepdims=True)
                return a0 + red(dz * um1), a1 + red(dz * u), a2 + red(dz * up1)

            zero = jnp.zeros((1, DH), F32)
            a0, a1, a2 = lax.fori_loop(0, rows // t, step1, (zero, zero, zero))
            dw_ref[0:1, :] = a0
            dw_ref[1:2, :] = a1
            dw_ref[2:3, :] = a2

            def step2(s, carry):
                r0 = pl.multiple_of(s * t, t)
                du, _, _, _ = _conv_tile(dz_ref, r0, t, rows, w2, w1, w0)
                du_ref[pl.ds(r0, t), :] = du.astype(BF16)
                return carry

            lax.fori_loop(0, rows // t, step2, 0)

        col = pl.BlockSpec((rows, DH), lambda j, b0=b0: (0, b0 + j))
        w_spec = pl.BlockSpec((3, DH), lambda j, b0=b0: (0, b0 + j))
        d_spec = pl.BlockSpec((1, rows, DH), lambda j, b0=b0: (b0 + j, 0, 0))
        outs = pl.pallas_call(
            body, name=f"{name}_{'norm' if norm else 'plain'}", grid=(b1 - b0,),
            in_specs=[col, w_spec, d_spec, d_spec] + [ANY] * len(outs), out_specs=[col, w_spec],
            out_shape=[jax.ShapeDtypeStruct((rows, nb * DH), BF16), jax.ShapeDtypeStruct((3, nb * DH), F32)],
            input_output_aliases={4: 0, 5: 1} if outs else {},
            scratch_shapes=[pltpu.VMEM((rows, DH), F32)],
            compiler_params=_params(("parallel",)),
        )(p, wconv, d_a, d_b, *outs)
    return outs


def _gates_fn(pab, avec, dvec):
    t = pab.shape[0]
    lane = lax.broadcasted_iota(jnp.int32, pab.shape, 1)
    xg = pab + dvec
    sp = jnp.maximum(xg, 0.0) + jnp.log(1.0 + jnp.exp(-jnp.abs(xg)))
    g = jnp.where(lane < 2 * H, -jnp.exp(avec) * sp, 0.0)
    ii = lax.broadcasted_iota(jnp.int32, (t, t), 0)
    jj = lax.broadcasted_iota(jnp.int32, (t, t), 1)
    same = (ii >> LOG_CH) == (jj >> LOG_CH)
    cum_f = _dot(jnp.where(same & (jj <= ii), 1.0, 0.0), g, precision=HI)
    cum_r = _dot(jnp.where(same & (jj >= ii), 1.0, 0.0), g, precision=HI)
    return jnp.where(lane < H, cum_f, jnp.where(lane < 2 * H, cum_r, jnp.where(lane < 4 * H, jax.nn.sigmoid(pab), 0.0)))


def _gates_fwd(pab, avec, dvec, *, name):
    rows = pab.shape[0]
    t = min(2 * PAIR, rows)

    def body(p_ref, a_ref, d_ref, o_ref):
        o_ref[...] = _gates_fn(p_ref[...], a_ref[...], d_ref[...])

    vec = pl.BlockSpec((1, DH), lambda i: (0, 0))
    tile = pl.BlockSpec((t, DH), lambda i: (i, 0))
    return pl.pallas_call(
        body, name=name, grid=(rows // t,), in_specs=[tile, vec, vec], out_specs=tile,
        out_shape=jax.ShapeDtypeStruct((rows, DH), F32), compiler_params=_params(("parallel",)),
    )(pab, avec, dvec)


def _gates_bwd(pab, avec, dvec, d_a, d_b, *, name):
    rows = pab.shape[0]
    t = min(2 * PAIR, rows)

    def body(p_ref, a_ref, d_ref, da_ref, db_ref, dp_ref, dav_ref, ddv_ref):
        i = pl.program_id(0)
        _, vjp = jax.vjp(_gates_fn, p_ref[...], a_ref[...], d_ref[...])
        dp, dav, ddv = vjp(da_ref[...] + db_ref[...])
        dp_ref[...] = dp.astype(BF16)

        @pl.when(i == 0)
        def _():
            dav_ref[...] = dav
            ddv_ref[...] = ddv

        @pl.when(i > 0)
        def _():
            dav_ref[...] += dav
            ddv_ref[...] += ddv

    vec = pl.BlockSpec((1, DH), lambda i: (0, 0))
    tile = pl.BlockSpec((t, DH), lambda i: (i, 0))
    return pl.pallas_call(
        body, name=name, grid=(rows // t,), in_specs=[tile, vec, vec, tile, tile], out_specs=[tile, vec, vec],
        out_shape=[jax.ShapeDtypeStruct((rows, DH), BF16), jax.ShapeDtypeStruct((1, DH), F32),
                   jax.ShapeDtypeStruct((1, DH), F32)],
        compiler_params=_params(("arbitrary",)),
    )(pab, avec, dvec, d_a, d_b)


def _dot_general(a, b, dims, precision=None):
    return lax.dot_general(a, b, (dims, ((), ())), precision=precision, preferred_element_type=F32)


@functools.partial(jax.custom_vjp, nondiff_argnums=(2,))
def _dot_bf16(a, b, dims):
    return _dot_general(a.astype(BF16), b.astype(BF16), dims)


def _dot_bf16_fwd(a, b, dims):
    return _dot_bf16(a, b, dims), (a, b)


def _dot_bf16_bwd(dims, res, g):
    a, b = res
    (ca,), (cb,) = dims
    da = _dot_bf16(g, b, ((1,), (1 - cb,))) if ca == 1 else _dot_bf16(b, g, ((1 - cb,), (1,)))
    db = _dot_bf16(a, g, ((1 - ca,), (0,))) if cb == 0 else _dot_bf16(g, a, ((0,), (1 - ca,)))
    return da, db


_dot_bf16.defvjp(_dot_bf16_fwd, _dot_bf16_bwd)


def _dot(a, b, dims=((1,), (0,)), precision=None):
    if precision is None and a.dtype == F32 and b.dtype == F32:
        return _dot_bf16(a, b, dims)
    return _dot_general(a, b, dims, precision)


_NT = ((1,), (1,))
_TN = ((0,), (0,))


@jax.custom_vjp
def _saved_inverse(neg_a, inv):
    return inv


def _saved_inverse_fwd(neg_a, inv):
    return inv, inv


def _saved_inverse_bwd(inv, d_inv):
    idx = range(len(inv))
    left = [_dot(inv[i], d_inv[i], _TN) for i in idx]
    d_neg_a = [_dot(left[i], inv[i], _NT) for i in idx]
    return d_neg_a, [jnp.zeros_like(t) for t in inv]


_saved_inverse.defvjp(_saved_inverse_fwd, _saved_inverse_bwd)


def _pairs(s, q, k, v, gcol, bcol, revs, inv_saved=None):
    idx = range(len(revs))
    ii = lax.broadcasted_iota(jnp.int32, (PAIR, PAIR), 0)
    jj = lax.broadcasted_iota(jnp.int32, (PAIR, PAIR), 1)
    same = (ii >> LOG_CH) == (jj >> LOG_CH)
    incl_d = (same & (ii >= jj), same & (ii <= jj))
    strict_d = (same & (ii > jj), same & (ii < jj))
    incl = [incl_d[int(r)] for r in revs]
    strict = [strict_d[int(r)] for r in revs]
    eye = jnp.where(ii == jj, 1.0, 0.0)
    gc_i = [jnp.broadcast_to(gcol[i], (PAIR, DH)) for i in idx]
    gc_j = [gc_i[i].T for i in idx]
    decay = [jnp.where(incl[i], jnp.exp(jnp.where(incl[i], gc_i[i] - gc_j[i], 0.0)), 0.0) for i in idx]
    b_b = [jnp.broadcast_to(bcol[i], (PAIR, DH)) for i in idx]
    kb = [k[i] * b_b[i] for i in idx]
    kk = [_dot(kb[i], k[i], _NT) for i in idx]
    bp = [jnp.where(strict[i], -kk[i] * decay[i], 0.0) for i in idx]
    if inv_saved is not None:
        inv = _saved_inverse(bp, inv_saved)
    else:
        low = bp
        for prec_sq, prec_acc in PREC_POWERS:
            bp = [_dot(bp[i], bp[i], precision=prec_sq) for i in idx]
            more = [_dot(low[i], bp[i], precision=prec_acc) for i in idx]
            low = [low[i] + bp[i] + more[i] for i in idx]
        inv = [eye + low[i] for i in idx]
    eg = [jnp.exp(gc_i[i]) for i in idx]
    sol = [_dot(inv[i], jnp.concatenate([v[i] * b_b[i], kb[i] * eg[i]], axis=1)) for i in idx]
    u_val = [sol[i][:, :DH] for i in idx]
    w_key = [sol[i][:, DH:] for i in idx]
    row = lax.broadcasted_iota(jnp.int32, (PAIR, 1), 0)
    has_q = q[0] is not None
    if has_q:
        qc = [q[i] * (DH ** -0.5) for i in idx]
        qk = [_dot(qc[i], k[i], _NT) for i in idx]
        attn = [qk[i] * decay[i] for i in idx]
        qd = [qc[i] * eg[i] for i in idx]
    outs = [[None, None] for _ in idx]
    zeros = jnp.zeros((CH, DH), F32)
    for step in range(2):
        cidx = [(1 - step) if revs[i] else step for i in idx]
        sl = [slice(c * CH, (c + 1) * CH) for c in cidx]
        last = [c * CH if revs[i] else c * CH + CH - 1 for i, c in zip(idx, cidx)]
        gl = [jnp.sum(jnp.where(row == last[i], gcol[i], 0.0), axis=0, keepdims=True) for i in idx]
        k_tail = [k[i][sl[i]] * jnp.exp(gl[i] - gc_i[i][sl[i]]) for i in idx]
        ws = [_dot(w_key[i][sl[i]], s[i]) for i in idx]
        v_new = [u_val[i][sl[i]] - ws[i] for i in idx]
        if has_q:
            v_pad = [jnp.concatenate([v_new[i], zeros] if cidx[i] == 0 else [zeros, v_new[i]], axis=0) for i in idx]
            o_state = [_dot(qd[i][sl[i]], s[i]) for i in idx]
            o_local = [_dot(attn[i][sl[i]], v_pad[i]) for i in idx]
            for i in idx:
                outs[i][cidx[i]] = o_state[i] + o_local[i]
        kv = [_dot(k_tail[i], v_new[i], _TN) for i in idx]
        s = [s[i] * jnp.exp(gl[i]) + kv[i] for i in idx]
    return s, ([jnp.concatenate(outs[i], axis=0) for i in idx] if has_q else None), inv


def _lane_col(tile, idx):
    lane = lax.broadcasted_iota(jnp.int32, tile.shape, 1)
    return jnp.sum(jnp.where(lane == idx, tile, 0.0), axis=1, keepdims=True)


def _gdn_fwd(qkv, gb, s0f, s0b, has_q, *, name):
    nb, rows, _ = qkv.shape
    n = rows // PAIR
    qoff = H if has_q else 0

    def body(qf_ref, qb_ref, gf_ref, gr_ref, s0f_ref, s0b_ref, of_ref, ob_ref, ssf_ref, ssb_ref, tsf_ref, tsb_ref,
             sf_ref, sb_ref):
        @pl.when(pl.program_id(0) == 0)
        def _():
            sf_ref[...] = s0f_ref[...]
            sb_ref[...] = s0b_ref[...]

        gtiles = (gf_ref[...], gr_ref[...])

        dirs = ((qf_ref, sf_ref, ssf_ref, of_ref), (qb_ref, sb_ref, ssb_ref, ob_ref))
        ts_refs = (tsf_ref, tsb_ref)

        work = [(h, d) for h in range(H) for d in range(2)]
        loaded = []
        for h, d in work:
            q_ref, s_ref, _, _ = dirs[d]
            loaded.append((s_ref[h], q_ref[h] if has_q else None, q_ref[qoff + h], q_ref[qoff + H + h],
                           _lane_col(gtiles[d], d * H + h), _lane_col(gtiles[d], 2 * H + d * H + h)))
        s_new, o, inv = _pairs(*[list(col) for col in zip(*loaded)], revs=[d == 1 for _, d in work])
        for i, (h, d) in enumerate(work):
            _, s_ref, ss_ref, o_ref = dirs[d]
            ss_ref[0, h] = loaded[i][0]
            ts_refs[d][0, h] = inv[i]
            s_ref[h] = s_new[i]
            o_ref[h] = o[i] if has_q else jnp.zeros((PAIR, DH), F32)

    fwd3 = lambda i: (0, i, 0)
    rev3 = lambda i: (0, n - 1 - i, 0)
    state = pl.BlockSpec((H, DH, DH), lambda i: (0, 0, 0))
    saved = pl.BlockSpec((1, H, DH, DH), lambda i: (i, 0, 0, 0))
    return pl.pallas_call(
        body, name=name, grid=(n,),
        in_specs=[pl.BlockSpec((nb, PAIR, DH), fwd3), pl.BlockSpec((nb, PAIR, DH), rev3),
                  pl.BlockSpec((PAIR, DH), lambda i: (i, 0)), pl.BlockSpec((PAIR, DH), lambda i: (n - 1 - i, 0)),
                  state, state],
        out_specs=[pl.BlockSpec((H, PAIR, DH), fwd3), pl.BlockSpec((H, PAIR, DH), rev3), saved, saved, saved, saved,
                   state, state],
        out_shape=[jax.ShapeDtypeStruct((H, rows, DH), F32)] * 2 + [jax.ShapeDtypeStruct((n, H, DH, DH), F32)] * 4
        + [jax.ShapeDtypeStruct((H, DH, DH), F32)] * 2,
        compiler_params=_params(("arbitrary",)),
    )(qkv, qkv, gb, gb, s0f, s0b)


def _gdn_bwd(qkv, gb, ssf, ssb, tsf, tsb, do, dsf, dsb, has_q, *, name):
    nb, rows, _ = qkv.shape
    n = rows // PAIR
    qoff = H if has_q else 0

    def body(qf_ref, qb_ref, gf_ref, gr_ref, ssf_ref, ssb_ref, tsf_ref, tsb_ref, dof_ref, dob_ref, dsf0_ref, dsb0_ref,
             dqf_ref, dqb_ref, dgf_ref, dgr_ref, dsf_ref, dsb_ref):
        ts_refs = (tsf_ref, tsb_ref)
        @pl.when(pl.program_id(0) == 0)
        def _():
            dsf_ref[...] = dsf0_ref[...]
            dsb_ref[...] = dsb0_ref[...]

        gtiles = (gf_ref[...], gr_ref[...])
        lane = lax.broadcasted_iota(jnp.int32, (PAIR, DH), 1)

        dirs = ((qf_ref, ssf_ref, dof_ref, dsf_ref, dqf_ref), (qb_ref, ssb_ref, dob_ref, dsb_ref, dqb_ref))

        work = [(h, d) for h in range(H) for d in range(2)]
        revs = [d == 1 for _, d in work]
        s_in, q_in, k_in, v_in, g_in, b_in, ds_out, do_out, inv_in = [], [], [], [], [], [], [], [], []
        for h, d in work:
            q_ref, ss_ref, do_ref, ds_ref, _ = dirs[d]
            s_in.append(ss_ref[0, h])
            inv_in.append(ts_refs[d][0, h])
            q_in.append(q_ref[h] if has_q else None)
            k_in.append(q_ref[qoff + h])
            v_in.append(q_ref[qoff + H + h])
            g_in.append(_lane_col(gtiles[d], d * H + h))
            b_in.append(_lane_col(gtiles[d], 2 * H + d * H + h))
            ds_out.append(ds_ref[h])
            do_out.append(do_ref[h] if has_q else None)
        if has_q:
            _, vjp = jax.vjp(lambda s_, q_, k_, v_, g_, b_: _pairs(s_, q_, k_, v_, g_, b_, revs, inv_in)[:2],
                             s_in, q_in, k_in, v_in, g_in, b_in)
            ds, dq, dk, dv, dg, db = vjp((ds_out, do_out))
        else:
            _, vjp = jax.vjp(lambda s_, k_, v_, g_, b_: _pairs(s_, q_in, k_, v_, g_, b_, revs, inv_in)[0],
                             s_in, k_in, v_in, g_in, b_in)
            ds, dk, dv, dg, db = vjp(ds_out)
        dgb = [jnp.zeros((PAIR, DH), F32)] * 2
        for i, (h, d) in enumerate(work):
            dq_ref, ds_ref = dirs[d][4], dirs[d][3]
            ds_ref[h] = ds[i]
            if has_q:
                dq_ref[h] = dq[i]
            dq_ref[qoff + h], dq_ref[qoff + H + h] = dk[i], dv[i]
            dgb[d] = (dgb[d] + jnp.where(lane == d * H + h, dg[i], 0.0)
                      + jnp.where(lane == 2 * H + d * H + h, db[i], 0.0))
        dgf_ref[...] = dgb[0]
        dgr_ref[...] = dgb[1]

    fwd3 = lambda i: (0, n - 1 - i, 0)
    rev3 = lambda i: (0, i, 0)
    state = pl.BlockSpec((H, DH, DH), lambda i: (0, 0, 0))
    saved = pl.BlockSpec((1, H, DH, DH), lambda i: (n - 1 - i, 0, 0, 0))
    gf_spec = pl.BlockSpec((PAIR, DH), lambda i: (n - 1 - i, 0))
    gr_spec = pl.BlockSpec((PAIR, DH), lambda i: (i, 0))
    return pl.pallas_call(
        body, name=name, grid=(n,),
        in_specs=[pl.BlockSpec((nb, PAIR, DH), fwd3), pl.BlockSpec((nb, PAIR, DH), rev3), gf_spec, gr_spec,
                  saved, saved, saved, saved,
                  pl.BlockSpec((H, PAIR, DH), fwd3), pl.BlockSpec((H, PAIR, DH), rev3), state, state],
        out_specs=[pl.BlockSpec((nb, PAIR, DH), fwd3), pl.BlockSpec((nb, PAIR, DH), rev3), gf_spec, gr_spec, state, state],
        out_shape=[jax.ShapeDtypeStruct((nb, rows, DH), F32)] * 2 + [jax.ShapeDtypeStruct((rows, DH), F32)] * 2
        + [jax.ShapeDtypeStruct((H, DH, DH), F32)] * 2,
        compiler_params=_params(("arbitrary",)),
    )(qkv, qkv, gb, gb, ssf, ssb, tsf, tsb, do, do, dsf, dsb)


def _stage1(zb, ua, va, za, o, gon, lng, lnb, wsp, bsp):
    gv = [_gelu(t) for t in va]
    mu = sum(jnp.sum(t, axis=-1, keepdims=True) for t in gv) * (1.0 / D)
    xc = [t - mu for t in gv]
    var = sum(jnp.sum(t * t, axis=-1, keepdims=True) for t in xc) * (1.0 / D)
    rs = lax.rsqrt(var + EPS)
    ya, yb = [], []
    for g in range(H):
        vv = xc[g] * rs * lng[g] + lnb[g]
        s = _dot(wsp[g], vv) + bsp[g]
        ya.append(_gelu(ua[g]) * s * _silu(za[g]))
        r = lax.rsqrt(jnp.mean(o[g] * o[g], axis=-1, keepdims=True) + EPS)
        yb.append(o[g] * r * gon * _silu(zb[g]))
    return ya, yb


def _stage2(ma, mb, ga, gb):
    return jax.nn.sigmoid(ga) * ma + jax.nn.sigmoid(gb) * mb


def _stage3(out, x, tgt, gpost, gate):
    r = out * lax.rsqrt(jnp.mean(out * out, axis=-1, keepdims=True) + EPS) * gpost
    err = x + gate * r - tgt
    return 0.5 * jnp.sum(jnp.mean(err * err, axis=-1, keepdims=True), axis=0, keepdims=True)


def _post(p_rest, o_f, o_b, x, tgt, mod, bmod, gon, lng, lnb, wsp, bspb, wpa, wpb, wout, gpost, *, name):
    rows = x.shape[0]
    n = rows // GM
    lanes = lambda g: slice(g * DH, (g + 1) * DH)
    bdot = lambda a, w_ref: _dot(a.astype(BF16), w_ref[...])
    bdot_t = lambda a, w_ref: _dot(a.astype(BF16), w_ref[...], _NT)

    def body(p_ref, of_ref, ob_ref, x_ref, t_ref, m_ref, bm_ref, gon_ref, lng_ref, lnb_ref, wsp_ref, bsp_ref,
             wpa_ref, wpb_ref, wout_ref, gp_ref,
             loss_ref, dp_ref, do_ref, dx_ref, ya_ref, yb_ref, mg_ref, dma_ref, dmb_ref, dout_ref,
             dvec_ref, dgon_ref, dwsp_ref, dbsp_ref):
        @pl.when(pl.program_id(0) == 0)
        def _():
            loss_ref[...] = jnp.zeros_like(loss_ref)
            dvec_ref[...] = jnp.zeros_like(dvec_ref)
            dgon_ref[...] = jnp.zeros_like(dgon_ref)
            dwsp_ref[...] = jnp.zeros_like(dwsp_ref)
            dbsp_ref[...] = jnp.zeros_like(dbsp_ref)

        piece = lambda blk: [p_ref[:, blk * D + g * DH: blk * D + (g + 1) * DH].astype(F32) for g in range(H)]
        zb, ua, va, za = piece(0), piece(1), piece(2), piece(3)
        o = [of_ref[g] + ob_ref[g] for g in range(H)]
        gon = gon_ref[...]
        lng = [lng_ref[:, lanes(g)] for g in range(H)]
        lnb = [lnb_ref[:, lanes(g)] for g in range(H)]
        wsp = [wsp_ref[g] for g in range(H)]
        bsp = [bsp_ref[g] for g in range(H)]
        (ya, yb), vjp1 = jax.vjp(_stage1, zb, ua, va, za, o, gon, lng, lnb, wsp, bsp)
        y_a, y_b = jnp.concatenate(ya, axis=1), jnp.concatenate(yb, axis=1)
        ma, mb = bdot(y_a, wpa_ref), bdot(y_b, wpb_ref)
        ga, gb = p_ref[:, 4 * D:5 * D].astype(F32), p_ref[:, 5 * D:6 * D].astype(F32)
        merged, vjp2 = jax.vjp(_stage2, ma, mb, ga, gb)
        out = bdot(merged, wout_ref)
        gate = m_ref[:, 2 * D:3 * D] + bm_ref[:, 2 * D:3 * D]
        loss, vjp3 = jax.vjp(_stage3, out, x_ref[...], t_ref[...], gp_ref[...], gate)
        loss_ref[...] += jnp.broadcast_to(loss, loss_ref.shape)

        dout, dx, _, dgpost, dgate = vjp3(jnp.ones((1, 1), F32))
        dx_ref[...] = dx
        dmerged = bdot_t(dout, wout_ref)
        dma, dmb, dga, dgb = vjp2(dmerged)
        dya, dyb = bdot_t(dma, wpa_ref), bdot_t(dmb, wpb_ref)
        dzb, dua, dva, dza, do, dgon, dlng, dlnb, dwsp, dbsp = vjp1(
            ([dya[:, lanes(g)] for g in range(H)], [dyb[:, lanes(g)] for g in range(H)]))

        for blk, dlist in enumerate((dzb, dua, dva, dza)):
            for g in range(H):
                dp_ref[:, blk * D + g * DH: blk * D + (g + 1) * DH] = dlist[g].astype(BF16)
        dp_ref[:, 4 * D:5 * D] = dga.astype(BF16)
        dp_ref[:, 5 * D:6 * D] = dgb.astype(BF16)
        for g in range(H):
            do_ref[g] = do[g]
            dwsp_ref[g] += dwsp[g]
            dbsp_ref[g] += dbsp[g]
            dvec_ref[2:3, lanes(g)] += dlng[g]
            dvec_ref[3:4, lanes(g)] += dlnb[g]
        dvec_ref[0:1, :] += dgpost
        dvec_ref[1:2, :] += dgate
        dgon_ref[0:1, :] += dgon
        ya_ref[...] = y_a.astype(BF16)
        yb_ref[...] = y_b.astype(BF16)
        mg_ref[...] = merged.astype(BF16)
        dma_ref[...] = dma.astype(BF16)
        dmb_ref[...] = dmb.astype(BF16)
        dout_ref[...] = dout.astype(BF16)

    row = lambda w: pl.BlockSpec((GM, w), lambda i: (i, 0))
    heads = pl.BlockSpec((H, GM, DH), lambda i: (0, i, 0))
    full = lambda shape: pl.BlockSpec(shape, lambda i: tuple(0 for _ in shape))
    sds = jax.ShapeDtypeStruct
    return pl.pallas_call(
        body, name=name, grid=(n,),
        in_specs=[row(6 * D), heads, heads, row(D), row(D), full((1, 3 * D)), full((1, 3 * D)), full((1, DH)),
                  full((1, D)), full((1, D)), full((H, GM, GM)), full((H, GM, GM)),
                  full((D, D)), full((D, D)), full((D, D)), full((1, D))],
        out_specs=[full((8, DH)), row(6 * D), heads, row(D)] + [row(D)] * 6
        + [full((8, D)), full((8, DH)), full((H, GM, GM)), full((H, GM, GM))],
        out_shape=[sds((8, DH), F32), sds((rows, 6 * D), BF16), sds((H, rows, DH), F32), sds((rows, D), F32)]
        + [sds((rows, D), BF16)] * 6 + [sds((8, D), F32), sds((8, DH), F32), sds((H, GM, GM), F32), sds((H, GM, GM), F32)],
        compiler_params=_params(("arbitrary",)),
    )(p_rest, o_f, o_b, x, tgt, mod, bmod, gon, lng, lnb, wsp, bspb, wpa, wpb, wout, gpost)


def _silu_rows(c, *, name):
    def body(c_ref, o_ref):
        o_ref[...] = _silu(c_ref[...])

    return pl.pallas_call(body, name=name, out_shape=jax.ShapeDtypeStruct(c.shape, F32))(c)


def _dsilu_mul(c, d, *, name):
    def body(c_ref, d_ref, o_ref):
        _, vjp = jax.vjp(_silu, c_ref[...])
        (o_ref[...],) = vjp(d_ref[...])

    return pl.pallas_call(body, name=name, out_shape=jax.ShapeDtypeStruct(c.shape, F32))(c, d)


def _adamw(w, g, m, v, *, name):
    rows, cols = w.shape[-2:]
    t = _tile(rows if g.ndim == 2 else g.shape[1])
    c1 = 1.0 / (1.0 - ADAM_B1 ** ADAM_STEP)
    c2 = 1.0 / (1.0 - ADAM_B2 ** ADAM_STEP)

    def body(w_ref, g_ref, m_ref, v_ref, go_ref, d_ref, mo_ref, vo_ref):
        blk = lambda r: r[...].reshape(t, cols)
        gv = blk(g_ref)
        mn = ADAM_B1 * blk(m_ref) + (1.0 - ADAM_B1) * gv
        vn = ADAM_B2 * blk(v_ref) + (1.0 - ADAM_B2) * (gv * gv)
        delta = -ADAM_LR * ((mn * c1) / (jnp.sqrt(vn * c2) + ADAM_EPS) + ADAM_WD * blk(w_ref))
        for ref, val in ((go_ref, gv), (d_ref, delta), (mo_ref, mn), (vo_ref, vn)):
            ref[...] = val.reshape(ref.shape)

    tile = (pl.BlockSpec((1, t, cols), lambda i: (0, i, 0)) if w.ndim == 3 else pl.BlockSpec((t, cols), lambda i: (i, 0)))
    if g.ndim == 3:
        per = g.shape[1] // t
        g_tile = pl.BlockSpec((1, t, cols), lambda i: (i // per, i % per, 0))
    else:
        g_tile = pl.BlockSpec((t, cols), lambda i: (i, 0))
    return pl.pallas_call(
        body, name=name, grid=(rows // t,),
        in_specs=[tile, g_tile, tile, tile], out_specs=[tile] * 4,
        out_shape=[jax.ShapeDtypeStruct(w.shape, F32)] * 4,
        compiler_params=_params(("parallel",)),
    )(w, g, m, v)


def _add_sibling(full, got, where, out_dtype, *, name):
    s, rows, cols = full.shape
    hr = rows // 2
    t = _tile(hr)
    nt = hr // t

    def body(w_ref, a_ref, b_ref, o_ref):
        o_ref[...] = (a_ref[...].astype(F32) + b_ref[...].astype(F32)).astype(out_dtype)

    tile = pl.BlockSpec((1, t, cols), lambda j, i, w: (j, i, 0))
    return pl.pallas_call(
        body, name=name,
        grid_spec=pltpu.PrefetchScalarGridSpec(
            num_scalar_prefetch=1, grid=(s, nt),
            in_specs=[pl.BlockSpec((1, t, cols), lambda j, i, w: (j, w[0] * nt + i, 0)), tile], out_specs=tile),
        out_shape=jax.ShapeDtypeStruct((s, hr, cols), out_dtype), compiler_params=_params(("parallel", "parallel")),
    )(where, full, got)


def _sum_chips(own, slots, where, n_out, which, *, name):
    _, hr, cols = own.shape
    t = _tile(hr)

    def body(w_ref, a_ref, s_ref, o_ref):
        f = lambda v: v.astype(F32)
        o_ref[0] = ((f(a_ref[0]) + f(s_ref[0])) + f(s_ref[1])) + f(s_ref[2])

    return pl.pallas_call(
        body, name=name,
        grid_spec=pltpu.PrefetchScalarGridSpec(
            num_scalar_prefetch=1, grid=(hr // t,),
            in_specs=[pl.BlockSpec((1, t, cols), lambda i, w: (w[1], i, 0)),
                      pl.BlockSpec((N_CHIPS - 1, t, cols), lambda i, w: (0, i, 0))],
            out_specs=pl.BlockSpec((1, t, cols), lambda i, w: (w[which], i, 0))),
        out_shape=jax.ShapeDtypeStruct((n_out, hr, cols), F32), compiler_params=_params(("parallel",)),
    )(where, own, slots)


def _local_step(x, c, ctx, c_ctx, tgt, w_mod, b_mod, g_pre, g_post, w_qkv, w_ab, w_rest, w_conv, a_log, dt_bias,
                g_onorm, gm_ln_g, gm_ln_b, w_sp, b_sp, late_weights, on_large_gradients=None, run_after=None):
    rows, rows_c = x.shape[0], ctx.shape[0]
    cc = jnp.zeros((16, D), F32).at[0].set(c[0]).at[1].set(c_ctx)
    scc = _silu_rows(cc, name="silu_cond")
    mod = _mm(scc, w_mod, b_sections=True, after=run_after, name="mod_fwd")
    mod_x, mod_c = mod[0:1], mod[1:2]
    avec = jnp.zeros((1, DH), F32).at[0, :2 * H].set(a_log.reshape(-1))
    dvec = jnp.zeros((1, DH), F32).at[0, :2 * H].set(dt_bias.reshape(-1))
    bspb = jnp.broadcast_to(b_sp[:, :, None], (H, GM, GM))
    w_kv, wconv_kv = w_qkv[:, D:], w_conv[:, D:]

    h_c = _norm_fwd(ctx, g_pre, mod_c, b_mod, name="norm_fwd_ctx")
    pc_kv = _mm(h_c, w_kv, name="inproj_ctx_kv")
    pc_ab = _mm(h_c, w_ab, name="inproj_ctx_ab")
    kv_c = _prep_fwd(pc_kv, wconv_kv, H, name="prep_fwd_ctx")
    gb_c = _gates_fwd(pc_ab, avec, dvec, name="gates_fwd_ctx")
    s_zero = jnp.zeros((H, DH, DH), F32)
    _, _, ssf_c, ssb_c, tsf_c, tsb_c, s_f, s_b = _gdn_fwd(kv_c, gb_c, s_zero, s_zero, False, name="gdn_fwd_ctx")

    h_x = _norm_fwd(x, g_pre, mod_x, b_mod, name="norm_fwd_x")
    p_qkv = _mm(h_x, w_qkv, tm=rows, tn=D // 2, name="inproj_qkv")
    p_ab = _mm(h_x, w_ab, name="inproj_ab")
    p_rest = _mm(h_x, w_rest, tm=rows, tn=D // 2, out_dtype=BF16, name="inproj_rest")
    qkv = _prep_fwd(p_qkv, w_conv, 2 * H, name="prep_fwd_x")
    gb_x = _gates_fwd(p_ab, avec, dvec, name="gates_fwd_x")
    o_f, o_b, ssf, ssb, tsf, tsb, _, _ = _gdn_fwd(qkv, gb_x, s_f, s_b, True, name="gdn_fwd_x")

    w_pa, w_pb, w_out = late_weights(o_f)
    (loss_acc, dp_rest, do, dx_res, ya, yb, mg, dma, dmb, dout, dvec_post, dgon, dwsp, dbspb) = _post(
        p_rest, o_f, o_b, x, tgt, mod_x, b_mod, g_onorm, gm_ln_g, gm_ln_b, w_sp, bspb, w_pa, w_pb, w_out, g_post,
        name="post")
    g = {}
    g["w_pa"] = _mm(ya, dma, ta=True, out_dtype=BF16, name="dw_pa")
    g["w_pb"] = _mm(yb, dmb, ta=True, out_dtype=BF16, name="dw_pb")
    g["w_out"] = _mm(mg, dout, ta=True, out_dtype=BF16, name="dw_out")

    zeros_s = jnp.zeros((H, DH, DH), F32)
    dq_f, dq_b, dg_f, dg_b, ds0_f, ds0_b = _gdn_bwd(qkv, gb_x, ssf, ssb, tsf, tsb, do, zeros_s, zeros_s, True,
                                                    name="gdn_bwd_x")
    dp_qkv, dwc_x = _prep_bwd(p_qkv, w_conv, dq_f, dq_b, 2 * H, name="prep_bwd_x")
    dp_ab, dav_x, ddv_x = _gates_bwd(p_ab, avec, dvec, dg_f, dg_b, name="gates_bwd_x")
    dkv_f, dkv_b, dgc_f, dgc_b, _, _ = _gdn_bwd(kv_c, gb_c, ssf_c, ssb_c, tsf_c, tsb_c, jnp.zeros((H, rows_c, DH), F32),
                                                 ds0_f, ds0_b, False, name="gdn_bwd_ctx")
    dpc_kv, dwc_c = _prep_bwd(pc_kv, wconv_kv, dkv_f, dkv_b, H, name="prep_bwd_ctx")
    dpc_ab, dav_c, ddv_c = _gates_bwd(pc_ab, avec, dvec, dgc_f, dgc_b, name="gates_bwd_ctx")

    dw_kv_c = _mm(h_c, dpc_kv, ta=True, name="dw_kv_ctx")
    dw_qkv = _mm(h_x, dp_qkv, ta=True, tn=D, tk=2 * D, add=dw_kv_c, add_from=1, out_dtype=BF16, name="dw_qkv")
    dw_ab_c = _mm(h_c, dpc_ab, ta=True, name="dw_ab_ctx")
    dw_ab = _mm(h_x, dp_ab, ta=True, add=dw_ab_c, out_dtype=BF16, name="dw_ab")
    dw_rest = _mm(h_x, dp_rest, ta=True, tn=2 * D, tk=2 * D, out_dtype=BF16, name="dw_rest")
    g["w_in"] = (dw_qkv, dw_ab[:, :4 * H], dw_rest)
    token = on_large_gradients(g) if on_large_gradients is not None else None
    dh = _mm(dp_ab, w_ab, tb=True, after=token, name="dh_ab")
    dh = _mm(dp_qkv, w_qkv, tb=True, tk=3 * D, add=dh, after=token, name="dh_qkv")
    dh = _mm(dp_rest, w_rest, tb=True, tk=3 * D, add=dh, after=token, name="dh_rest")
    grad_x, dgpre_x, dm_x = _norm_bwd(x, g_pre, mod_x, b_mod, [dh], dx_res, name="norm_bwd_x")
    dh_c = _mm(dpc_ab, w_ab, tb=True, after=token, name="dhc_ab")
    dh_c = _mm(dpc_kv, w_kv, tb=True, add=dh_c, after=token, name="dhc_kv")
    _, dgpre_c, dm_c = _norm_bwd(ctx, g_pre, mod_c, b_mod, [dh_c], None, name="norm_bwd_ctx")

    dm_x = dm_x.at[:, 2 * D:].add(dvec_post[1:2])
    dmod = jnp.zeros((16, 3 * D), F32).at[0].set(dm_x[0]).at[1].set(dm_c[0])
    g["mod_factors"] = (scc[0], dm_x[0])
    g["dm_ctx"] = dm_c[0]
    g["silu_c_ctx"] = scc[1:2]
    dcc = _mm(dmod, w_mod, tb=True, b_sections=True, name="dcc")
    g["c_ctx"] = _dsilu_mul(cc[:8], dcc[:8], name="dc_ctx")[1]
    g["b_mod"] = dm_x + dm_c
    g["g_pre"] = dgpre_x + dgpre_c
    g["g_post"] = dvec_post[0:1]
    g["gm_ln_g"], g["gm_ln_b"] = dvec_post[2:3], dvec_post[3:4]
    g["g_onorm"] = dgon[0:1]
    g["w_sp"] = dwsp
    g["b_sp"] = jnp.sum(dbspb, axis=-1)
    g["w_conv"] = dwc_x.at[:, D:].add(dwc_c)
    g["a_log"] = (dav_x + dav_c)[0, :2 * H].reshape(2, H)
    g["dt_bias"] = (ddv_x + ddv_c)[0, :2 * H].reshape(2, H)
    return loss_acc[0, 0], grad_x, g


ANY = pl.BlockSpec(memory_space=pl.ANY)


def _place():
    x, y, c = lax.axis_index("x"), lax.axis_index("y"), lax.axis_index("c")
    chips = [(1 - x, y), (x, 1 - y), (1 - x, 1 - y)]
    return x, y, c, (x, y, 1 - c), chips


def _gather_shards(big, small, *, name):
    nb = len(big)

    def body(*refs):
        ins, sm_in = refs[:nb], refs[nb]
        outs, sm_out = refs[nb + 1:2 * nb + 1], refs[2 * nb + 1]
        send, recv = refs[2 * nb + 2:]
        x, y, c, sibling, chips = _place()
        mine = 2 * x + y

        def half(a, shard, hc):
            hr = big[a].shape[0] // 2
            return outs[a].at[shard, pl.ds(hc * hr, hr), :]

        def remote(k, src, dst, to):
            return pltpu.make_async_remote_copy(src_ref=src, dst_ref=dst, send_sem=send.at[k], recv_sem=recv.at[k],
                                                device_id=to, device_id_type=MESH)

        sends = []
        for a in range(nb):
            hr = big[a].shape[0] // 2
            for j, chip in enumerate(chips):
                sends.append(remote(a * 3 + j, ins[a].at[pl.ds(c * hr, hr), :], half(a, mine, c), (*chip, c)))
        for j, chip in enumerate(chips):
            sends.append(remote(nb * 3 + j, sm_in, sm_out.at[mine], (*chip, c)))
        for cp in sends:
            cp.start()
        base = nb * 3 + 3
        passed = []
        for a in range(nb):
            for j, (px, py) in enumerate(chips):
                theirs = 2 * px + py
                remote(a * 3 + j, half(a, theirs, c), half(a, theirs, c), sibling).wait_recv()
                fw = remote(base + a * 3 + j, half(a, theirs, c), half(a, theirs, c), sibling)
                fw.start()
                passed.append(fw)
        for a in range(nb):
            for j, (px, py) in enumerate(chips):
                theirs = 2 * px + py
                remote(base + a * 3 + j, half(a, theirs, 1 - c), half(a, theirs, 1 - c), sibling).wait_recv()
        for j, (px, py) in enumerate(chips):
            remote(nb * 3 + j, sm_in, sm_out.at[2 * px + py], sibling).wait_recv()
        for cp in sends + passed:
            cp.wait_send()

    n_remote = 2 * nb * 3 + 3
    outs = pl.pallas_call(
        body, name=name, in_specs=[ANY] * (nb + 1), out_specs=[ANY] * (nb + 1),
        out_shape=[jax.ShapeDtypeStruct((N_CHIPS,) + a.shape, a.dtype) for a in big + [small]],
        scratch_shapes=[pltpu.SemaphoreType.DMA((n_remote,)), pltpu.SemaphoreType.DMA((n_remote,))],
    )(*big, small)
    return outs[:nb], outs[nb]


def _gather_late_start(arrs, after, *, name):
    na = len(arrs)

    def body(*refs):
        ins, land = refs[:na], refs[na:2 * na]
        send, recv = refs[2 * na + 1], refs[2 * na + 2]
        token = refs[-1]
        x, y, c, _, chips = _place()
        for a in range(na):
            hr = arrs[a].shape[0] // 2
            for j, (px, py) in enumerate(chips):
                for other in range(2):
                    k = (a * 3 + j) * 2 + other
                    _remote(ins[a].at[pl.ds(c * hr, hr), :], land[a].at[2 * x + y, pl.ds(c * hr, hr), :],
                            send.at[k], recv.at[k], (px, py, c ^ other)).start()
        token[...] = jnp.zeros_like(token)

    lands = [((N_CHIPS,) + a.shape, a.dtype) for a in arrs]
    held = lambda a: pltpu.with_memory_space_constraint(a, pltpu.HBM)
    outs = pl.pallas_call(
        body, name=name, in_specs=[HBM] * (2 * na) + [ANY],
        out_specs=[SEM, SEM] + [HBM] * (2 * na) + [pl.BlockSpec(memory_space=pltpu.VMEM)],
        out_shape=[pltpu.SemaphoreType.DMA((6 * na,)), pltpu.SemaphoreType.DMA((6 * na,))]
        + [pltpu.HBM(a.shape, a.dtype) for a in arrs] + [pltpu.HBM(s, d) for s, d in lands]
        + [jax.ShapeDtypeStruct((8, 128), F32)],
        input_output_aliases={i: 2 + i for i in range(2 * na)},
        compiler_params=pltpu.CompilerParams(has_side_effects=pltpu.SideEffectType.DATAFLOW_SIDE_EFFECTING),
    )(*[held(a) for a in arrs], *[held(lax.empty(s, d)) for s, d in lands], after)
    return outs[0], outs[1], outs[2:2 + na], outs[2 + na:2 + 2 * na], outs[-1]


def _gather_late_wait(send, recv, thru, land, after, *, name):
    na = len(thru)

    def body(*refs):
        ins, slots = refs[:na], refs[na:2 * na]
        send_sem, recv_sem = refs[2 * na], refs[2 * na + 1]
        x, y, c, _, chips = _place()
        for a in range(na):
            hr = thru[a].shape[0] // 2
            for j, (px, py) in enumerate(chips):
                for other in range(2):
                    k = (a * 3 + j) * 2 + other
                    half = c ^ other
                    cp = _remote(ins[a].at[pl.ds(c * hr, hr), :], slots[a].at[2 * px + py, pl.ds(half * hr, hr), :],
                                 send_sem.at[k], recv_sem.at[k], (px, py, half))
                    cp.wait_send()
                    cp.wait_recv()

    outs = pl.pallas_call(
        body, name=name, in_specs=[HBM] * (2 * na) + [SEM, SEM, ANY], out_specs=[HBM] * (2 * na),
        out_shape=[pltpu.HBM(a.shape, a.dtype) for a in list(thru) + list(land)],
        input_output_aliases={i: i for i in range(2 * na)},
        compiler_params=pltpu.CompilerParams(has_side_effects=pltpu.SideEffectType.DATAFLOW_SIDE_EFFECTING),
    )(*thru, *land, send, recv, after)
    return outs[na:]


def _row_chunks(rows, row_bytes, align=8):
    n = max(1, min(rows // align, -(-rows * row_bytes // DMA_CHUNK_BYTES)))
    per = -(-(-(-rows // n)) // align) * align
    return [(r, min(per, rows - r)) for r in range(0, rows, per)]


def _remote(src, dst, send, recv, to):
    return pltpu.make_async_remote_copy(src_ref=src, dst_ref=dst, send_sem=send, recv_sem=recv, device_id=to,
                                        device_id_type=MESH)


def _exchange_halves(arrs, *, name):
    na = len(arrs)

    def body(*refs):
        ins, got = refs[:na], refs[na:2 * na]
        send, recv = refs[2 * na:]
        x, y, c, sibling, _ = _place()
        for a in range(na):
            ns, rows, cols = arrs[a].shape
            hr = rows // 2
            for s in range(ns):
                for r0, nr in _row_chunks(hr, cols * arrs[a].dtype.itemsize, 16):
                    _remote(ins[a].at[s, pl.ds((1 - c) * hr + r0, nr), :], got[a].at[s, pl.ds(r0, nr), :],
                            send.at[a], recv.at[a], sibling).start()
        for a in range(na):
            hr = arrs[a].shape[1] // 2
            _remote(ins[a].at[:, pl.ds((1 - c) * hr, hr), :], got[a], send.at[a], recv.at[a], sibling).wait()

    return pl.pallas_call(
        body, name=name, in_specs=[ANY] * na, out_specs=[ANY] * na,
        out_shape=[jax.ShapeDtypeStruct((N_CHIPS, a.shape[1] // 2, a.shape[2]), a.dtype) for a in arrs],
        scratch_shapes=[pltpu.SemaphoreType.DMA((na,)), pltpu.SemaphoreType.DMA((na,))],
    )(*arrs)


def _scatter_sections(arrs, *, name):
    na = len(arrs)

    def body(*refs):
        ins, outs = refs[:na], refs[na:2 * na]
        send, recv = refs[2 * na:]
        x, y, c, _, chips = _place()
        for a in range(na):
            _, hr, cols = arrs[a].shape
            for j, (px, py) in enumerate(chips):
                for r0, nr in _row_chunks(hr, cols * arrs[a].dtype.itemsize, 16):
                    _remote(ins[a].at[2 * px + py, pl.ds(r0, nr), :], outs[a].at[j, pl.ds(r0, nr), :],
                            send.at[a * 3 + j], recv.at[a * 3 + j], (px, py, c)).start()
        for a in range(na):
            for j, (px, py) in enumerate(chips):
                _remote(ins[a].at[2 * px + py], outs[a].at[j], send.at[a * 3 + j], recv.at[a * 3 + j], (px, py, c)).wait()

    return pl.pallas_call(
        body, name=name, in_specs=[ANY] * na, out_specs=[ANY] * na,
        out_shape=[jax.ShapeDtypeStruct((N_CHIPS - 1,) + a.shape[1:], a.dtype) for a in arrs],
        scratch_shapes=[pltpu.SemaphoreType.DMA((3 * na,)), pltpu.SemaphoreType.DMA((3 * na,))],
    )(*arrs)


HBM = pl.BlockSpec(memory_space=pltpu.HBM)
SEM = pl.BlockSpec(memory_space=pltpu.SEMAPHORE)


def _scatter_start(arrs, *, name):
    na = len(arrs)

    def body(*refs):
        ins, land = refs[:na], refs[na:2 * na]
        send, recv = refs[2 * na], refs[2 * na + 1]
        token = refs[-1]
        x, y, c, _, chips = _place()
        for a in range(na):
            _, hr, cols = arrs[a].shape
            for j, (px, py) in enumerate(chips):
                for r0, nr in _row_chunks(hr, cols * arrs[a].dtype.itemsize, 16):
                    _remote(ins[a].at[2 * px + py, pl.ds(r0, nr), :], land[a].at[j, pl.ds(r0, nr), :],
                            send.at[a * 3 + j], recv.at[a * 3 + j], (px, py, c)).start()
        token[...] = jnp.zeros_like(token)

    slots = [((N_CHIPS - 1,) + a.shape[1:], a.dtype) for a in arrs]
    held = lambda a: pltpu.with_memory_space_constraint(a, pltpu.HBM)
    outs = pl.pallas_call(
        body, name=name, in_specs=[HBM] * (2 * na),
        out_specs=[SEM, SEM] + [HBM] * (2 * na) + [pl.BlockSpec(memory_space=pltpu.VMEM)],
        out_shape=[pltpu.SemaphoreType.DMA((3 * na,)), pltpu.SemaphoreType.DMA((3 * na,))]
        + [pltpu.HBM(a.shape, a.dtype) for a in arrs] + [pltpu.HBM(s, d) for s, d in slots]
        + [jax.ShapeDtypeStruct((8, 128), F32)],
        input_output_aliases={i: 2 + i for i in range(2 * na)},
        compiler_params=pltpu.CompilerParams(has_side_effects=pltpu.SideEffectType.DATAFLOW_SIDE_EFFECTING),
    )(*[held(a) for a in arrs], *[held(lax.empty(s, d)) for s, d in slots])
    return outs[0], outs[1], outs[2:2 + na], outs[2 + na:2 + 2 * na], outs[-1]


def _scatter_wait(send, recv, thru, land, after, *, name):
    na = len(thru)

    def body(*refs):
        ins, slots = refs[:na], refs[na:2 * na]
        send_sem, recv_sem = refs[2 * na], refs[2 * na + 1]
        x, y, c, _, chips = _place()
        for a in range(na):
            for j, (px, py) in enumerate(chips):
                cp = _remote(ins[a].at[2 * px + py], slots[a].at[j], send_sem.at[a * 3 + j], recv_sem.at[a * 3 + j],
                             (px, py, c))
                cp.wait_send()
                cp.wait_recv()

    outs = pl.pallas_call(
        body, name=name, in_specs=[HBM] * (2 * na) + [SEM, SEM, ANY], out_specs=[HBM] * (2 * na),
        out_shape=[pltpu.HBM(a.shape, a.dtype) for a in list(thru) + list(land)],
        input_output_aliases={i: i for i in range(2 * na)},
        compiler_params=pltpu.CompilerParams(has_side_effects=pltpu.SideEffectType.DATAFLOW_SIDE_EFFECTING),
    )(*thru, *land, send, recv, after)
    return outs[:na], outs[na:]


def _finish_reduce(big, small, *, name):
    nb = len(big)

    def body(*refs):
        outs, sm = refs[nb + 1:2 * nb + 1], refs[2 * nb + 1]
        send, recv = refs[2 * nb + 2:]
        x, y, c, sibling, chips = _place()
        blk = lambda px, py, pc: sm.at[4 * px + 2 * py + pc]
        for a in range(nb):
            _, hr, cols = big[a].shape
            for r0, nr in _row_chunks(hr, cols * 4):
                _remote(outs[a].at[c, pl.ds(r0, nr), :], outs[a].at[c, pl.ds(r0, nr), :], send.at[a], recv.at[a],
                        sibling).start()
        first = [_remote(blk(x, y, c), blk(x, y, c), send.at[nb], recv.at[nb], sibling)]
        first += [_remote(blk(x, y, c), blk(x, y, c), send.at[nb + 1 + j], recv.at[nb + 1 + j], (*chip, c))
                  for j, chip in enumerate(chips)]
        for cp in first:
            cp.start()
        passed = []
        for j, (px, py) in enumerate(chips):
            _remote(blk(px, py, c), blk(px, py, c), send.at[nb + 1 + j], recv.at[nb + 1 + j], sibling).wait_recv()
            fw = _remote(blk(px, py, c), blk(px, py, c), send.at[nb + 4 + j], recv.at[nb + 4 + j], sibling)
            fw.start()
            passed.append(fw)
        for a in range(nb):
            _remote(outs[a].at[c], outs[a].at[1 - c], send.at[a], recv.at[a], sibling).wait()
        _remote(blk(x, y, c), blk(x, y, 1 - c), send.at[nb], recv.at[nb], sibling).wait_recv()
        for j, (px, py) in enumerate(chips):
            _remote(blk(px, py, c), blk(px, py, 1 - c), send.at[nb + 4 + j], recv.at[nb + 4 + j], sibling).wait_recv()
        for cp in first + passed:
            cp.wait_send()

    n_remote = nb + 7
    arrs = list(big) + [small]
    outs = pl.pallas_call(
        body, name=name, in_specs=[ANY] * (nb + 1), out_specs=[ANY] * (nb + 1),
        out_shape=[jax.ShapeDtypeStruct(a.shape, F32) for a in arrs],
        input_output_aliases={i: i for i in range(nb + 1)},
        scratch_shapes=[pltpu.SemaphoreType.DMA((n_remote,)), pltpu.SemaphoreType.DMA((n_remote,))],
    )(*arrs)
    return outs[:nb], outs[nb]


def _reduce_start(sectioned, where):
    got = _exchange_halves(sectioned, name="rs_exchange_halves")
    chip_sum = [_add_sibling(a, g, where, BF16, name=f"rs_add_sibling_{i}")
                for i, (a, g) in enumerate(zip(sectioned, got))]
    send, recv, thru, land, token = _scatter_start(chip_sum, name="rs_scatter_start")
    return (send, recv, thru, land), token


def _reduce_finish(handle, small, where, after):
    send, recv, thru, land = handle
    got = _exchange_halves([small], name="rs_exchange_small")
    small_sum = _add_sibling(small, got[0], where, F32, name="rs_add_sibling_small")
    small_slots = _scatter_sections([small_sum], name="rs_scatter_small")[0]
    own, slots = _scatter_wait(send, recv, thru, land, after, name="rs_scatter_wait")
    red = [_sum_chips(p, s, where, 2, 0, name=f"rs_sum_chips_{i}") for i, (p, s) in enumerate(zip(own, slots))]
    red_small = _sum_chips(small_sum, small_slots, where, 2 * N_CHIPS, 2, name="rs_sum_chips_small")
    big, sm = _finish_reduce(red, red_small, name="rs_finish")
    return big, sm.reshape(-1, sm.shape[-1])


def kernel(x, c, ctx, c_ctx, w_mod, b_mod, g_pre, g_post, w_in, w_conv, a_log, dt_bias, g_onorm, gm_ln_g, gm_ln_b, w_sp, b_sp, w_pa, w_pb, w_out, loss_target, m_c_ctx, m_w_mod, m_b_mod, m_g_pre, m_g_post, m_w_in, m_w_conv, m_a_log, m_dt_bias, m_g_onorm, m_gm_ln_g, m_gm_ln_b, m_w_sp, m_b_sp, m_w_pa, m_w_pb, m_w_out, v_c_ctx, v_w_mod, v_b_mod, v_g_pre, v_g_post, v_w_in, v_w_conv, v_a_log, v_dt_bias, v_g_onorm, v_gm_ln_g, v_gm_ln_b, v_w_sp, v_b_sp, v_w_pa, v_w_pb, v_w_out):
    names = ["c_ctx", "w_mod", "b_mod", "g_pre", "g_post", "w_in", "w_conv", "a_log", "dt_bias", "g_onorm", "gm_ln_g",
             "gm_ln_b", "w_sp", "b_sp", "w_pa", "w_pb", "w_out"]
    w = dict(zip(names, (c_ctx, w_mod, b_mod, g_pre, g_post, w_in, w_conv, a_log, dt_bias, g_onorm, gm_ln_g, gm_ln_b,
                         w_sp, b_sp, w_pa, w_pb, w_out)))
    m = dict(zip(names, (m_c_ctx, m_w_mod, m_b_mod, m_g_pre, m_g_post, m_w_in, m_w_conv, m_a_log, m_dt_bias, m_g_onorm,
                         m_gm_ln_g, m_gm_ln_b, m_w_sp, m_b_sp, m_w_pa, m_w_pb, m_w_out)))
    v = dict(zip(names, (v_c_ctx, v_w_mod, v_b_mod, v_g_pre, v_g_post, v_w_in, v_w_conv, v_a_log, v_dt_bias, v_g_onorm,
                         v_gm_ln_g, v_gm_ln_b, v_w_sp, v_b_sp, v_w_pa, v_w_pb, v_w_out)))
    xy = 2 * lax.axis_index("x") + lax.axis_index("y")
    where = jnp.stack([lax.axis_index("c"), xy, 2 * xy + lax.axis_index("c")]).astype(jnp.int32)

    shards = [a[0].astype(BF16) for a in (w_mod, w_in)]
    gathered, wconv_all = _gather_shards(shards, w_conv[0], name="gather_weights")
    own = lambda full, shard: lax.dynamic_update_slice(full, shard[None], (xy, 0, 0))
    wm_all, win_all = [own(f, s) for f, s in zip(gathered, shards)]
    wconv_all = own(wconv_all, w_conv[0])
    late_shards = [a[0].astype(BF16) for a in (w_pa, w_pb, w_out)]
    late = _gather_late_start(late_shards, wconv_all, name="gather_late_start")

    def late_weights(after):
        lands = _gather_late_wait(*late[:4], after, name="gather_late_wait")
        return [own(f, s).reshape(D, D) for f, s in zip(lands, late_shards)]
    w_conv_f = jnp.concatenate([wconv_all[s] for s in range(N_CHIPS)], axis=1)
    shard_cols = IN_COLS // N_CHIPS
    cut = OFF_A - shard_cols
    assert 0 < cut and cut + 4 * H < shard_cols
    w_qkv = jnp.concatenate([win_all[0], win_all[1][:, :cut]], axis=1)
    w_ab = jnp.pad(win_all[1][:, cut:cut + 4 * H], ((0, 0), (0, DH - 4 * H)))
    w_rest = jnp.concatenate([win_all[1][:, cut + 4 * H:], win_all[2], win_all[3]], axis=1)

    blk = D // N_CHIPS
    rest_cut = shard_cols - cut - 4 * H
    big_names = ("w_in", "w_pa", "w_pb", "w_out")
    in_flight = []

    def start_reduce(grads):
        dw_qkv, dw_ab, dw_rest = grads["w_in"]
        g_win = jnp.stack([dw_qkv[:, :shard_cols],
                           jnp.concatenate([dw_qkv[:, shard_cols:], dw_ab, dw_rest[:, :rest_cut]], axis=1),
                           dw_rest[:, rest_cut:rest_cut + shard_cols], dw_rest[:, rest_cut + shard_cols:]]).astype(BF16)
        sectioned = [g_win] + [grads[k].reshape(N_CHIPS, blk, D) for k in ("w_pa", "w_pb", "w_out")]
        handle, token = _reduce_start(sectioned, where)
        in_flight.append(handle)
        return token

    loss_local, grad_x, g = _local_step(
        x[0], c, ctx[0], c_ctx, loss_target[0], wm_all, b_mod, g_pre, g_post, w_qkv, w_ab, w_rest,
        w_conv_f, a_log[0], dt_bias[0], g_onorm, gm_ln_g, gm_ln_b, w_sp[0], b_sp[0],
        late_weights, on_large_gradients=start_reduce, run_after=late[4])
    g["loss"] = loss_local
    g["_pad"] = jnp.zeros((SMALL_LAYOUT["_pad"][1],), F32)
    g["mod_factors"] = lax.dynamic_update_slice(jnp.zeros((2 * N_CHIPS, 4 * D), F32),
                                                jnp.concatenate(g["mod_factors"])[None], (where[2], 0))
    tail = jnp.zeros((N_CHIPS * SMALL_ROWS * 128 - sum(s for _, s in SMALL_LAYOUT.values()),), F32)
    flat = jnp.concatenate([g[k].reshape(-1) for k in SMALL_LAYOUT] + [tail])
    reduced, gr_small = _reduce_finish(in_flight[0], flat.reshape(N_CHIPS, SMALL_ROWS, 128), where, g["b_mod"])
    gr_tiny = gr_small[TINY_ROW0:]

    def entry(arr, k):
        off, size = SMALL_LAYOUT[k]
        row, col = off // 128 - TINY_ROW0, off % 128
        return arr[row:row + size // 128].reshape(-1) if size >= 128 else arr[row, col:col + size]

    factors = entry(gr_tiny, "mod_factors").reshape(2 * N_CHIPS, 4 * D)
    pad_rows = lambda a: jnp.pad(a, ((0, 16 - a.shape[0]), (0, 0)))
    lhs = pad_rows(jnp.concatenate([factors[:, :D], g["silu_c_ctx"]], axis=0))
    rhs = pad_rows(jnp.concatenate([factors[:, D:], entry(gr_tiny, "dm_ctx")[None]], axis=0))
    mod_cols = 3 * D // N_CHIPS
    gr_wm = _mm(lhs, lax.dynamic_slice(rhs, (0, xy * mod_cols), (16, mod_cols)), ta=True, name="dw_mod")

    res = {k: _adamw(w[k], gr, m[k], v[k], name=f"adamw_{k}") for k, gr in zip(big_names[1:], reduced[1:])}
    flip = lambda a: jnp.swapaxes(a, 1, 2)
    res["w_in"] = [flip(r) for r in _adamw(flip(w_in), flip(reduced[0].reshape(w_in.shape)), flip(m_w_in), flip(v_w_in),
                                           name="adamw_w_in")]
    res["w_mod"] = _adamw(w_mod[0], gr_wm, m_w_mod[0], v_w_mod[0], name="adamw_w_mod")
    res["w_sp"] = _adamw(w_sp.reshape(-1, 128), gr_small, m_w_sp.reshape(-1, 128), v_w_sp.reshape(-1, 128),
                         name="adamw_w_sp")
    tiny = [k for k, (off, _) in SMALL_LAYOUT.items() if TINY_ROW0 <= off // 128 < TINY_ROW0 + TINY_ROWS]
    tiny_names = [k for k in tiny if k not in ("loss", "_pad")]

    def pack(src):
        parts = [src[k].reshape(-1) if k in tiny_names else jnp.zeros((SMALL_LAYOUT[k][1],), F32) for k in tiny]
        return jnp.concatenate(parts).reshape(TINY_ROWS, 128)

    tiny_res = _adamw(pack(w), gr_tiny[:TINY_ROWS], pack(m), pack(v), name="adamw_tiny")
    for k in tiny_names:
        res[k] = [entry(r, k) for r in tiny_res]
    g_conv = lax.dynamic_slice(entry(gr_tiny, "w_conv").reshape(3, 3 * D), (0, xy * (3 * D // N_CHIPS)),
                               (3, 3 * D // N_CHIPS))
    conv_res = _adamw(jnp.pad(w_conv[0], ((0, 5), (0, 0))), jnp.pad(g_conv, ((0, 5), (0, 0))),
                      jnp.pad(m_w_conv[0], ((0, 5), (0, 0))), jnp.pad(v_w_conv[0], ((0, 5), (0, 0))), name="adamw_w_conv")
    res["w_conv"] = [r[:3] for r in conv_res]
    res = {k: [r.reshape(w[k].shape) for r in res[k]] for k in names}

    out = [entry(gr_tiny, "loss").reshape(()), grad_x[None]]
    for i in range(4):
        out += [res[k][i] for k in names]
    return tuple(out)
```

```python
import functools

import jax
import jax.numpy as jnp
from jax import lax
from jax.experimental import pallas as pl
from jax.experimental.pallas import tpu as pltpu

F32 = jnp.float32
BF16 = jnp.bfloat16
HI = lax.Precision.HIGHEST
MESH = pl.DeviceIdType.MESH

D = 1024
H = 8
DH = 128
CH = 64
LOG_CH = 6
PAIR = 2 * CH
GM = 128
assert 1 << LOG_CH == CH and PAIR == DH
PREC_POWERS = ((lax.Precision.HIGH, lax.Precision.HIGH),) * 3 + ((None, None),) * 2
assert len(PREC_POWERS) == LOG_CH - 1
EPS = 1e-6
N_CHIPS = 4
OFF_A = 3 * D
OFF_ZB = OFF_A + 4 * H
IN_COLS = OFF_ZB + 6 * D
VMEM_LIMIT_V7X = 56 * 1024 * 1024
DMA_CHUNK_BYTES = 2 * 1024 * 1024

ADAM_LR, ADAM_B1, ADAM_B2, ADAM_EPS, ADAM_WD, ADAM_STEP = 0.001, 0.9, 0.999, 1e-08, 0.01, 10

SMALL_LAYOUT = {}
_off = 0
for _n, _s in (("w_sp", H * GM * GM), ("c_ctx", D), ("b_mod", 3 * D), ("g_pre", D), ("g_post", D), ("gm_ln_g", D),
               ("gm_ln_b", D), ("b_sp", H * GM), ("g_onorm", DH), ("a_log", 2 * H), ("dt_bias", 2 * H), ("loss", 1),
               ("_pad", 128 - 4 * H - 1 + 6 * 128), ("w_conv", 3 * 3 * D), ("dm_ctx", 3 * D), ("mod_factors", 8 * 4 * D)):
    SMALL_LAYOUT[_n] = (_off, _s)
    _off += _s
SMALL_ROWS = 368
assert N_CHIPS * SMALL_ROWS * 128 >= _off and (SMALL_ROWS // 2) % 8 == 0 and _off % 128 == 0
TINY_ROW0 = SMALL_LAYOUT["c_ctx"][0] // 128
TINY_ROWS = SMALL_LAYOUT["w_conv"][0] // 128 - TINY_ROW0
assert TINY_ROWS % 8 == 0


def _params(sem=None):
    return pltpu.CompilerParams(dimension_semantics=sem, vmem_limit_bytes=VMEM_LIMIT_V7X)


def _tile(n, cands=(256, 128, 64, 32, 16, 8)):
    if n <= cands[0]:
        return n
    for cand in cands:
        if n % cand == 0 and cand >= 64:
            return cand
    return max(d for d in range(8, cands[0], 8) if n % d == 0)


def _silu(x):
    return x * jax.nn.sigmoid(x)


def _gelu(x):
    return 0.5 * x * (1.0 + jnp.tanh(0.7978845608028654 * (x + 0.044715 * (x * x * x))))


def _mm(a, b, *, ta=False, tb=False, out_dtype=F32, tm=1024, tn=1024, tk=1024, add=None, add_from=0,
        b_sections=False, out_sections=False, after=None, name):
    m, k = (a.shape[1], a.shape[0]) if ta else a.shape
    if b_sections:
        sect = b.shape[2]
        n = b.shape[1] if tb else b.shape[0] * sect
        tn, tk = (tn, sect) if tb else (sect, tk)
    else:
        n = b.shape[0] if tb else b.shape[1]
    tm, tn, tk = min(tm, m), min(tn, n), min(tk, k)
    assert m % tm == 0 and n % tn == 0 and k % tk == 0, (name, m, n, k, tm, tn, tk)
    nk = k // tk
    dims = (((0,) if ta else (1,), (1,) if tb else (0,)), ((), ()))
    has_add = add is not None
    assert not has_add or add.shape == (m, n - add_from * tn), (name, add.shape)

    def body(*refs):
        a_ref, b_ref = refs[:2]
        o_ref = refs[2 + int(has_add) + int(after is not None)]
        acc_ref = refs[-1]
        kk = pl.program_id(2)
        bv = b_ref[0] if b_sections else b_ref[...]
        part = lax.dot_general(a_ref[...].astype(BF16), bv.astype(BF16), dims, preferred_element_type=F32)

        def finish(res):
            if has_add:
                res = res + jnp.where(pl.program_id(1) >= add_from, refs[2][...], 0.0)
            if out_sections:
                o_ref[0] = res.astype(out_dtype)
            else:
                o_ref[...] = res.astype(out_dtype)

        if nk == 1:
            finish(part)
            return

        @pl.when(kk == 0)
        def _():
            acc_ref[...] = part

        @pl.when((kk > 0) & (kk < nk - 1))
        def _():
            acc_ref[...] += part

        @pl.when(kk == nk - 1)
        def _():
            finish(acc_ref[...] + part)

    a_spec = pl.BlockSpec((tk, tm), lambda i, j, q: (q, i)) if ta else pl.BlockSpec((tm, tk), lambda i, j, q: (i, q))
    if b_sections:
        b_spec = (pl.BlockSpec((1, tn, tk), lambda i, j, q: (q, j, 0)) if tb
                  else pl.BlockSpec((1, tk, tn), lambda i, j, q: (j, q, 0)))
    else:
        b_spec = pl.BlockSpec((tn, tk), lambda i, j, q: (j, q)) if tb else pl.BlockSpec((tk, tn), lambda i, j, q: (q, j))
    add_spec = [pl.BlockSpec((tm, tn), lambda i, j, q: (i, jnp.maximum(j - add_from, 0)))] if has_add else []
    if out_sections:
        out_spec, out_shape = pl.BlockSpec((1, tm, tn), lambda i, j, q: (j, i, 0)), (n // tn, m, tn)
    else:
        out_spec, out_shape = pl.BlockSpec((tm, tn), lambda i, j, q: (i, j)), (m, n)
    return pl.pallas_call(
        body, name=name, grid=(m // tm, n // tn, nk),
        in_specs=[a_spec, b_spec] + add_spec + ([pl.BlockSpec(memory_space=pl.ANY)] if after is not None else []),
        out_specs=out_spec, out_shape=jax.ShapeDtypeStruct(out_shape, out_dtype),
        scratch_shapes=[pltpu.VMEM((tm, tn), F32)] if nk > 1 else [],
        compiler_params=_params(("parallel", "parallel", "arbitrary")),
    )(*([a, b] + ([add] if has_add else []) + ([after] if after is not None else [])))


def _h_fn(x, g, m):
    shift, scale = m[:, 0:D], m[:, D:2 * D]
    r = lax.rsqrt(jnp.mean(x * x, axis=-1, keepdims=True) + EPS)
    return (x * r * g) * (1.0 + scale) + shift


def _norm_fwd(x, g, mod, bmod, *, name):
    rows = x.shape[0]
    t = min(512, rows)

    def body(x_ref, g_ref, m_ref, b_ref, h_ref):
        h_ref[...] = _h_fn(x_ref[...], g_ref[...], m_ref[...] + b_ref[...]).astype(BF16)

    vec = lambda w: pl.BlockSpec((1, w), lambda i: (0, 0))
    return pl.pallas_call(
        body, name=name, grid=(rows // t,),
        in_specs=[pl.BlockSpec((t, D), lambda i: (i, 0)), vec(D), vec(3 * D), vec(3 * D)],
        out_specs=pl.BlockSpec((t, D), lambda i: (i, 0)),
        out_shape=jax.ShapeDtypeStruct((rows, D), BF16),
        compiler_params=_params(("parallel",)),
    )(x, g, mod, bmod)


def _norm_bwd(x, g, mod, bmod, dh_parts, resid, *, name):
    rows = x.shape[0]
    t = min(512, rows)
    n_parts = len(dh_parts)
    has_resid = resid is not None

    def body(*refs):
        x_ref, g_ref, m_ref, b_ref = refs[:4]
        parts = refs[4:4 + n_parts]
        r_ref = refs[4 + n_parts] if has_resid else None
        dx_ref, dg_ref, dm_ref = refs[-3:]
        i = pl.program_id(0)
        dh = parts[0][...]
        for p in parts[1:]:
            dh = dh + p[...]
        _, vjp = jax.vjp(_h_fn, x_ref[...], g_ref[...], m_ref[...] + b_ref[...])
        dx, dg, dm = vjp(dh)
        if has_resid:
            dx = dx + r_ref[...]
        dx_ref[...] = dx

        @pl.when(i == 0)
        def _():
            dg_ref[...] = dg
            dm_ref[...] = dm

        @pl.when(i > 0)
        def _():
            dg_ref[...] += dg
            dm_ref[...] += dm

    vec = lambda w: pl.BlockSpec((1, w), lambda i: (0, 0))
    tile = pl.BlockSpec((t, D), lambda i: (i, 0))
    ins = [x, g, mod, bmod, *dh_parts] + ([resid] if has_resid else [])
    return pl.pallas_call(
        body, name=name, grid=(rows // t,),
        in_specs=[tile, vec(D), vec(3 * D), vec(3 * D)] + [tile] * (n_parts + int(has_resid)),
        out_specs=[tile, vec(D), vec(3 * D)],
        out_shape=[jax.ShapeDtypeStruct((rows, D), F32), jax.ShapeDtypeStruct((1, D), F32),
                   jax.ShapeDtypeStruct((1, 3 * D), F32)],
        compiler_params=_params(("arbitrary",)),
    )(*ins)


def _conv_tile(u_ref, r0, t, rows, w0, w1, w2):
    u = u_ref[pl.ds(r0, t), :]
    prev8 = u_ref[pl.ds(pl.multiple_of(jnp.maximum(r0 - 8, 0), 8), 8), :]
    next8 = u_ref[pl.ds(pl.multiple_of(jnp.minimum(r0 + t, rows - 8), 8), 8), :]
    r8 = lax.broadcasted_iota(jnp.int32, (8, DH), 0)
    prev_row = jnp.sum(jnp.where(r8 == 7, prev8, 0.0), axis=0, keepdims=True)
    next_row = jnp.sum(jnp.where(r8 == 0, next8, 0.0), axis=0, keepdims=True)
    prev_row = jnp.where(r0 > 0, prev_row, 0.0)
    next_row = jnp.where(r0 + t < rows, next_row, 0.0)
    ri = lax.broadcasted_iota(jnp.int32, (t, DH), 0)
    um1 = jnp.where(ri == 0, prev_row, pltpu.roll(u, 1, 0))
    up1 = jnp.where(ri == t - 1, next_row, pltpu.roll(u, t - 1, 0))
    return w0 * um1 + w1 * u + w2 * up1, um1, u, up1


def _rowlocal(z, norm):
    y = _silu(z)
    return y * lax.rsqrt(jnp.sum(y * y, axis=-1, keepdims=True) + EPS) if norm else y


def _prep_fwd(p, wconv, n_norm, *, name):
    rows, nb = p.shape[0], p.shape[1] // DH
    t = min(512, rows)
    out = None
    for norm, b0, b1 in ((True, 0, n_norm), (False, n_norm, nb)):
        def body(u_ref, w_ref, *rest, norm=norm):
            o_ref = rest[-1]
            w0, w1, w2 = w_ref[0:1, :], w_ref[1:2, :], w_ref[2:3, :]

            def step(s, carry):
                r0 = pl.multiple_of(s * t, t)
                z, _, _, _ = _conv_tile(u_ref, r0, t, rows, w0, w1, w2)
                o_ref[0, pl.ds(r0, t), :] = _rowlocal(z, norm)
                return carry

            lax.fori_loop(0, rows // t, step, 0)

        held = [] if out is None else [out]
        out = pl.pallas_call(
            body, name=f"{name}_{'norm' if norm else 'plain'}", grid=(b1 - b0,),
            in_specs=[pl.BlockSpec((rows, DH), lambda j, b0=b0: (0, b0 + j)),
                      pl.BlockSpec((3, DH), lambda j, b0=b0: (0, b0 + j))] + [ANY] * len(held),
            out_specs=pl.BlockSpec((1, rows, DH), lambda j, b0=b0: (b0 + j, 0, 0)),
            out_shape=jax.ShapeDtypeStruct((nb, rows, DH), F32),
            input_output_aliases={2: 0} if held else {},
            compiler_params=_params(("parallel",)),
        )(p, wconv, *held)
    return out


def _prep_bwd(p, wconv, d_a, d_b, n_norm, *, name):
    rows, nb = p.shape[0], p.shape[1] // DH
    t = min(512, rows)
    outs = []
    for norm, b0, b1 in ((True, 0, n_norm), (False, n_norm, nb)):
        def body(u_ref, w_ref, da_ref, db_ref, *rest, norm=norm):
            du_ref, dw_ref, dz_ref = rest[-3:]
            w0, w1, w2 = w_ref[0:1, :], w_ref[1:2, :], w_ref[2:3, :]

            def step1(s, carry):
                a0, a1, a2 = carry
                r0 = pl.multiple_of(s * t, t)
                z, um1, u, up1 = _conv_tile(u_ref, r0, t, rows, w0, w1, w2)
                _, vjp = jax.vjp(lambda zz: _rowlocal(zz, norm), z)
                (dz,) = vjp(da_ref[0, pl.ds(r0, t), :] + db_ref[0, pl.ds(r0, t), :])
                dz_ref[pl.ds(r0, t), :] = dz
                red = lambda v: jnp.sum(v, axis=0, keepdims=True)
                return a0 + red(dz * um1), a1 + red(dz * u), a2 + red(dz * up1)

            zero = jnp.zeros((1, DH), F32)
            a0, a1, a2 = lax.fori_loop(0, rows // t, step1, (zero, zero, zero))
            dw_ref[0:1, :] = a0
            dw_ref[1:2, :] = a1
            dw_ref[2:3, :] = a2

            def step2(s, carry):
                r0 = pl.multiple_of(s * t, t)
                du, _, _, _ = _conv_tile(dz_ref, r0, t, rows, w2, w1, w0)
                du_ref[pl.ds(r0, t), :] = du.astype(BF16)
                return carry

            lax.fori_loop(0, rows // t, step2, 0)

        col = pl.BlockSpec((rows, DH), lambda j, b0=b0: (0, b0 + j))
        w_spec = pl.BlockSpec((3, DH), lambda j, b0=b0: (0, b0 + j))
        d_spec = pl.BlockSpec((1, rows, DH), lambda j, b0=b0: (b0 + j, 0, 0))
        outs = pl.pallas_call(
            body, name=f"{name}_{'norm' if norm else 'plain'}", grid=(b1 - b0,),
            in_specs=[col, w_spec, d_spec, d_spec] + [ANY] * len(outs), out_specs=[col, w_spec],
            out_shape=[jax.ShapeDtypeStruct((rows, nb * DH), BF16), jax.ShapeDtypeStruct((3, nb * DH), F32)],
            input_output_aliases={4: 0, 5: 1} if outs else {},
            scratch_shapes=[pltpu.VMEM((rows, DH), F32)],
            compiler_params=_params(("parallel",)),
        )(p, wconv, d_a, d_b, *outs)
    return outs


def _gates_fn(pab, avec, dvec):
    t = pab.shape[0]
    lane = lax.broadcasted_iota(jnp.int32, pab.shape, 1)
    xg = pab + dvec
    sp = jnp.maximum(xg, 0.0) + jnp.log(1.0 + jnp.exp(-jnp.abs(xg)))
    g = jnp.where(lane < 2 * H, -jnp.exp(avec) * sp, 0.0)
    ii = lax.broadcasted_iota(jnp.int32, (t, t), 0)
    jj = lax.broadcasted_iota(jnp.int32, (t, t), 1)
    same = (ii >> LOG_CH) == (jj >> LOG_CH)
    cum_f = _dot(jnp.where(same & (jj <= ii), 1.0, 0.0), g, precision=HI)
    cum_r = _dot(jnp.where(same & (jj >= ii), 1.0, 0.0), g, precision=HI)
    return jnp.where(lane < H, cum_f, jnp.where(lane < 2 * H, cum_r, jnp.where(lane < 4 * H, jax.nn.sigmoid(pab), 0.0)))


def _gates_fwd(pab, avec, dvec, *, name):
    rows = pab.shape[0]
    t = min(2 * PAIR, rows)

    def body(p_ref, a_ref, d_ref, o_ref):
        o_ref[...] = _gates_fn(p_ref[...], a_ref[...], d_ref[...])

    vec = pl.BlockSpec((1, DH), lambda i: (0, 0))
    tile = pl.BlockSpec((t, DH), lambda i: (i, 0))
    return pl.pallas_call(
        body, name=name, grid=(rows // t,), in_specs=[tile, vec, vec], out_specs=tile,
        out_shape=jax.ShapeDtypeStruct((rows, DH), F32), compiler_params=_params(("parallel",)),
    )(pab, avec, dvec)


def _gates_bwd(pab, avec, dvec, d_a, d_b, *, name):
    rows = pab.shape[0]
    t = min(2 * PAIR, rows)

    def body(p_ref, a_ref, d_ref, da_ref, db_ref, dp_ref, dav_ref, ddv_ref):
        i = pl.program_id(0)
        _, vjp = jax.vjp(_gates_fn, p_ref[...], a_ref[...], d_ref[...])
        dp, dav, ddv = vjp(da_ref[...] + db_ref[...])
        dp_ref[...] = dp.astype(BF16)

        @pl.when(i == 0)
        def _():
            dav_ref[...] = dav
            ddv_ref[...] = ddv

        @pl.when(i > 0)
        def _():
            dav_ref[...] += dav
            ddv_ref[...] += ddv

    vec = pl.BlockSpec((1, DH), lambda i: (0, 0))
    tile = pl.BlockSpec((t, DH), lambda i: (i, 0))
    return pl.pallas_call(
        body, name=name, grid=(rows // t,), in_specs=[tile, vec, vec, tile, tile], out_specs=[tile, vec, vec],
        out_shape=[jax.ShapeDtypeStruct((rows, DH), BF16), jax.ShapeDtypeStruct((1, DH), F32),
                   jax.ShapeDtypeStruct((1, DH), F32)],
        compiler_params=_params(("arbitrary",)),
    )(pab, avec, dvec, d_a, d_b)


def _dot_general(a, b, dims, precision=None):
    return lax.dot_general(a, b, (dims, ((), ())), precision=precision, preferred_element_type=F32)


@functools.partial(jax.custom_vjp, nondiff_argnums=(2,))
def _dot_bf16(a, b, dims):
    return _dot_general(a.astype(BF16), b.astype(BF16), dims)


def _dot_bf16_fwd(a, b, dims):
    return _dot_bf16(a, b, dims), (a, b)


def _dot_bf16_bwd(dims, res, g):
    a, b = res
    (ca,), (cb,) = dims
    da = _dot_bf16(g, b, ((1,), (1 - cb,))) if ca == 1 else _dot_bf16(b, g, ((1 - cb,), (1,)))
    db = _dot_bf16(a, g, ((1 - ca,), (0,))) if cb == 0 else _dot_bf16(g, a, ((0,), (1 - ca,)))
    return da, db


_dot_bf16.defvjp(_dot_bf16_fwd, _dot_bf16_bwd)


def _dot(a, b, dims=((1,), (0,)), precision=None):
    if precision is None and a.dtype == F32 and b.dtype == F32:
        return _dot_bf16(a, b, dims)
    return _dot_general(a, b, dims, precision)


_NT = ((1,), (1,))
_TN = ((0,), (0,))


@jax.custom_vjp
def _saved_inverse(neg_a, inv):
    return inv


def _saved_inverse_fwd(neg_a, inv):
    return inv, inv


def _saved_inverse_bwd(inv, d_inv):
    idx = range(len(inv))
    left = [_dot(inv[i], d_inv[i], _TN) for i in idx]
    d_neg_a = [_dot(left[i], inv[i], _NT) for i in idx]
    return d_neg_a, [jnp.zeros_like(t) for t in inv]


_saved_inverse.defvjp(_saved_inverse_fwd, _saved_inverse_bwd)


def _pairs(s, q, k, v, gcol, bcol, revs, inv_saved=None):
    idx = range(len(revs))
    ii = lax.broadcasted_iota(jnp.int32, (PAIR, PAIR), 0)
    jj = lax.broadcasted_iota(jnp.int32, (PAIR, PAIR), 1)
    same = (ii >> LOG_CH) == (jj >> LOG_CH)
    incl_d = (same & (ii >= jj), same & (ii <= jj))
    strict_d = (same & (ii > jj), same & (ii < jj))
    incl = [incl_d[int(r)] for r in revs]
    strict = [strict_d[int(r)] for r in revs]
    eye = jnp.where(ii == jj, 1.0, 0.0)
    gc_i = [jnp.broadcast_to(gcol[i], (PAIR, DH)) for i in idx]
    gc_j = [gc_i[i].T for i in idx]
    decay = [jnp.where(incl[i], jnp.exp(jnp.where(incl[i], gc_i[i] - gc_j[i], 0.0)), 0.0) for i in idx]
    b_b = [jnp.broadcast_to(bcol[i], (PAIR, DH)) for i in idx]
    kb = [k[i] * b_b[i] for i in idx]
    kk = [_dot(kb[i], k[i], _NT) for i in idx]
    bp = [jnp.where(strict[i], -kk[i] * decay[i], 0.0) for i in idx]
    if inv_saved is not None:
        inv = _saved_inverse(bp, inv_saved)
    else:
        low = bp
        for prec_sq, prec_acc in PREC_POWERS:
            bp = [_dot(bp[i], bp[i], precision=prec_sq) for i in idx]
            more = [_dot(low[i], bp[i], precision=prec_acc) for i in idx]
            low = [low[i] + bp[i] + more[i] for i in idx]
        inv = [eye + low[i] for i in idx]
    eg = [jnp.exp(gc_i[i]) for i in idx]
    sol = [_dot(inv[i], jnp.concatenate([v[i] * b_b[i], kb[i] * eg[i]], axis=1)) for i in idx]
    u_val = [sol[i][:, :DH] for i in idx]
    w_key = [sol[i][:, DH:] for i in idx]
    row = lax.broadcasted_iota(jnp.int32, (PAIR, 1), 0)
    has_q = q[0] is not None
    if has_q:
        qc = [q[i] * (DH ** -0.5) for i in idx]
        qk = [_dot(qc[i], k[i], _NT) for i in idx]
        attn = [qk[i] * decay[i] for i in idx]
        qd = [qc[i] * eg[i] for i in idx]
    outs = [[None, None] for _ in idx]
    zeros = jnp.zeros((CH, DH), F32)
    for step in range(2):
        cidx = [(1 - step) if revs[i] else step for i in idx]
        sl = [slice(c * CH, (c + 1) * CH) for c in cidx]
        last = [c * CH if revs[i] else c * CH + CH - 1 for i, c in zip(idx, cidx)]
        gl = [jnp.sum(jnp.where(row == last[i], gcol[i], 0.0), axis=0, keepdims=True) for i in idx]
        k_tail = [k[i][sl[i]] * jnp.exp(gl[i] - gc_i[i][sl[i]]) for i in idx]
        ws = [_dot(w_key[i][sl[i]], s[i]) for i in idx]
        v_new = [u_val[i][sl[i]] - ws[i] for i in idx]
        if has_q:
            v_pad = [jnp.concatenate([v_new[i], zeros] if cidx[i] == 0 else [zeros, v_new[i]], axis=0) for i in idx]
            o_state = [_dot(qd[i][sl[i]], s[i]) for i in idx]
            o_local = [_dot(attn[i][sl[i]], v_pad[i]) for i in idx]
            for i in idx:
                outs[i][cidx[i]] = o_state[i] + o_local[i]
        kv = [_dot(k_tail[i], v_new[i], _TN) for i in idx]
        s = [s[i] * jnp.exp(gl[i]) + kv[i] for i in idx]
    return s, ([jnp.concatenate(outs[i], axis=0) for i in idx] if has_q else None), inv


def _lane_col(tile, idx):
    lane = lax.broadcasted_iota(jnp.int32, tile.shape, 1)
    return jnp.sum(jnp.where(lane == idx, tile, 0.0), axis=1, keepdims=True)


def _gdn_fwd(qkv, gb, s0f, s0b, has_q, *, name):
    nb, rows, _ = qkv.shape
    n = rows // PAIR
    qoff = H if has_q else 0

    def body(qf_ref, qb_ref, gf_ref, gr_ref, s0f_ref, s0b_ref, of_ref, ob_ref, ssf_ref, ssb_ref, tsf_ref, tsb_ref,
             sf_ref, sb_ref):
        @pl.when(pl.program_id(0) == 0)
        def _():
            sf_ref[...] = s0f_ref[...]
            sb_ref[...] = s0b_ref[...]

        gtiles = (gf_ref[...], gr_ref[...])

        dirs = ((qf_ref, sf_ref, ssf_ref, of_ref), (qb_ref, sb_ref, ssb_ref, ob_ref))
        ts_refs = (tsf_ref, tsb_ref)

        work = [(h, d) for h in range(H) for d in range(2)]
        loaded = []
        for h, d in work:
            q_ref, s_ref, _, _ = dirs[d]
            loaded.append((s_ref[h], q_ref[h] if has_q else None, q_ref[qoff + h], q_ref[qoff + H + h],
                           _lane_col(gtiles[d], d * H + h), _lane_col(gtiles[d], 2 * H + d * H + h)))
        s_new, o, inv = _pairs(*[list(col) for col in zip(*loaded)], revs=[d == 1 for _, d in work])
        for i, (h, d) in enumerate(work):
            _, s_ref, ss_ref, o_ref = dirs[d]
            ss_ref[0, h] = loaded[i][0]
            ts_refs[d][0, h] = inv[i]
            s_ref[h] = s_new[i]
            o_ref[h] = o[i] if has_q else jnp.zeros((PAIR, DH), F32)

    fwd3 = lambda i: (0, i, 0)
    rev3 = lambda i: (0, n - 1 - i, 0)
    state = pl.BlockSpec((H, DH, DH), lambda i: (0, 0, 0))
    saved = pl.BlockSpec((1, H, DH, DH), lambda i: (i, 0, 0, 0))
    return pl.pallas_call(
        body, name=name, grid=(n,),
        in_specs=[pl.BlockSpec((nb, PAIR, DH), fwd3), pl.BlockSpec((nb, PAIR, DH), rev3),
                  pl.BlockSpec((PAIR, DH), lambda i: (i, 0)), pl.BlockSpec((PAIR, DH), lambda i: (n - 1 - i, 0)),
                  state, state],
        out_specs=[pl.BlockSpec((H, PAIR, DH), fwd3), pl.BlockSpec((H, PAIR, DH), rev3), saved, saved, saved, saved,
                   state, state],
        out_shape=[jax.ShapeDtypeStruct((H, rows, DH), F32)] * 2 + [jax.ShapeDtypeStruct((n, H, DH, DH), F32)] * 4
        + [jax.ShapeDtypeStruct((H, DH, DH), F32)] * 2,
        compiler_params=_params(("arbitrary",)),
    )(qkv, qkv, gb, gb, s0f, s0b)


def _gdn_bwd(qkv, gb, ssf, ssb, tsf, tsb, do, dsf, dsb, has_q, *, name):
    nb, rows, _ = qkv.shape
    n = rows // PAIR
    qoff = H if has_q else 0

    def body(qf_ref, qb_ref, gf_ref, gr_ref, ssf_ref, ssb_ref, tsf_ref, tsb_ref, dof_ref, dob_ref, dsf0_ref, dsb0_ref,
             dqf_ref, dqb_ref, dgf_ref, dgr_ref, dsf_ref, dsb_ref):
        ts_refs = (tsf_ref, tsb_ref)
        @pl.when(pl.program_id(0) == 0)
        def _():
            dsf_ref[...] = dsf0_ref[...]
            dsb_ref[...] = dsb0_ref[...]

        gtiles = (gf_ref[...], gr_ref[...])
        lane = lax.broadcasted_iota(jnp.int32, (PAIR, DH), 1)

        dirs = ((qf_ref, ssf_ref, dof_ref, dsf_ref, dqf_ref), (qb_ref, ssb_ref, dob_ref, dsb_ref, dqb_ref))

        work = [(h, d) for h in range(H) for d in range(2)]
        revs = [d == 1 for _, d in work]
        s_in, q_in, k_in, v_in, g_in, b_in, ds_out, do_out, inv_in = [], [], [], [], [], [], [], [], []
        for h, d in work:
            q_ref, ss_ref, do_ref, ds_ref, _ = dirs[d]
            s_in.append(ss_ref[0, h])
            inv_in.append(ts_refs[d][0, h])
            q_in.append(q_ref[h] if has_q else None)
            k_in.append(q_ref[qoff + h])
            v_in.append(q_ref[qoff + H + h])
            g_in.append(_lane_col(gtiles[d], d * H + h))
            b_in.append(_lane_col(gtiles[d], 2 * H + d * H + h))
            ds_out.append(ds_ref[h])
            do_out.append(do_ref[h] if has_q else None)
        if has_q:
            _, vjp = jax.vjp(lambda s_, q_, k_, v_, g_, b_: _pairs(s_, q_, k_, v_, g_, b_, revs, inv_in)[:2],
                             s_in, q_in, k_in, v_in, g_in, b_in)
            ds, dq, dk, dv, dg, db = vjp((ds_out, do_out))
        else:
            _, vjp = jax.vjp(lambda s_, k_, v_, g_, b_: _pairs(s_, q_in, k_, v_, g_, b_, revs, inv_in)[0],
                             s_in, k_in, v_in, g_in, b_in)
            ds, dk, dv, dg, db = vjp(ds_out)
        dgb = [jnp.zeros((PAIR, DH), F32)] * 2
        for i, (h, d) in enumerate(work):
            dq_ref, ds_ref = dirs[d][4], dirs[d][3]
            ds_ref[h] = ds[i]
            if has_q:
                dq_ref[h] = dq[i]
            dq_ref[qoff + h], dq_ref[qoff + H + h] = dk[i], dv[i]
            dgb[d] = (dgb[d] + jnp.where(lane == d * H + h, dg[i], 0.0)
                      + jnp.where(lane == 2 * H + d * H + h, db[i], 0.0))
        dgf_ref[...] = dgb[0]
        dgr_ref[...] = dgb[1]

    fwd3 = lambda i: (0, n - 1 - i, 0)
    rev3 = lambda i: (0, i, 0)
    state = pl.BlockSpec((H, DH, DH), lambda i: (0, 0, 0))
    saved = pl.BlockSpec((1, H, DH, DH), lambda i: (n - 1 - i, 0, 0, 0))
    gf_spec = pl.BlockSpec((PAIR, DH), lambda i: (n - 1 - i, 0))
    gr_spec = pl.BlockSpec((PAIR, DH), lambda i: (i, 0))
    return pl.pallas_call(
        body, name=name, grid=(n,),
        in_specs=[pl.BlockSpec((nb, PAIR, DH), fwd3), pl.BlockSpec((nb, PAIR, DH), rev3), gf_spec, gr_spec,
                  saved, saved, saved, saved,
                  pl.BlockSpec((H, PAIR, DH), fwd3), pl.BlockSpec((H, PAIR, DH), rev3), state, state],
        out_specs=[pl.BlockSpec((nb, PAIR, DH), fwd3), pl.BlockSpec((nb, PAIR, DH), rev3), gf_spec, gr_spec, state, state],
        out_shape=[jax.ShapeDtypeStruct((nb, rows, DH), F32)] * 2 + [jax.ShapeDtypeStruct((rows, DH), F32)] * 2
        + [jax.ShapeDtypeStruct((H, DH, DH), F32)] * 2,
        compiler_params=_params(("arbitrary",)),
    )(qkv, qkv, gb, gb, ssf, ssb, tsf, tsb, do, do, dsf, dsb)


def _stage1(zb, ua, va, za, o, gon, lng, lnb, wsp, bsp):
    gv = [_gelu(t) for t in va]
    mu = sum(jnp.sum(t, axis=-1, keepdims=True) for t in gv) * (1.0 / D)
    xc = [t - mu for t in gv]
    var = sum(jnp.sum(t * t, axis=-1, keepdims=True) for t in xc) * (1.0 / D)
    rs = lax.rsqrt(var + EPS)
    ya, yb = [], []
    for g in range(H):
        vv = xc[g] * rs * lng[g] + lnb[g]
        s = _dot(wsp[g], vv) + bsp[g]
        ya.append(_gelu(ua[g]) * s * _silu(za[g]))
        r = lax.rsqrt(jnp.mean(o[g] * o[g], axis=-1, keepdims=True) + EPS)
        yb.append(o[g] * r * gon * _silu(zb[g]))
    return ya, yb


def _stage2(ma, mb, ga, gb):
    return jax.nn.sigmoid(ga) * ma + jax.nn.sigmoid(gb) * mb


def _stage3(out, x, tgt, gpost, gate):
    r = out * lax.rsqrt(jnp.mean(out * out, axis=-1, keepdims=True) + EPS) * gpost
    err = x + gate * r - tgt
    return 0.5 * jnp.sum(jnp.mean(err * err, axis=-1, keepdims=True), axis=0, keepdims=True)


def _post(p_rest, o_f, o_b, x, tgt, mod, bmod, gon, lng, lnb, wsp, bspb, wpa, wpb, wout, gpost, *, name):
    rows = x.shape[0]
    n = rows // GM
    lanes = lambda g: slice(g * DH, (g + 1) * DH)
    bdot = lambda a, w_ref: _dot(a.astype(BF16), w_ref[...])
    bdot_t = lambda a, w_ref: _dot(a.astype(BF16), w_ref[...], _NT)

    def body(p_ref, of_ref, ob_ref, x_ref, t_ref, m_ref, bm_ref, gon_ref, lng_ref, lnb_ref, wsp_ref, bsp_ref,
             wpa_ref, wpb_ref, wout_ref, gp_ref,
             loss_ref, dp_ref, do_ref, dx_ref, ya_ref, yb_ref, mg_ref, dma_ref, dmb_ref, dout_ref,
             dvec_ref, dgon_ref, dwsp_ref, dbsp_ref):
        @pl.when(pl.program_id(0) == 0)
        def _():
            loss_ref[...] = jnp.zeros_like(loss_ref)
            dvec_ref[...] = jnp.zeros_like(dvec_ref)
            dgon_ref[...] = jnp.zeros_like(dgon_ref)
            dwsp_ref[...] = jnp.zeros_like(dwsp_ref)
            dbsp_ref[...] = jnp.zeros_like(dbsp_ref)

        piece = lambda blk: [p_ref[:, blk * D + g * DH: blk * D + (g + 1) * DH].astype(F32) for g in range(H)]
        zb, ua, va, za = piece(0), piece(1), piece(2), piece(3)
        o = [of_ref[g] + ob_ref[g] for g in range(H)]
        gon = gon_ref[...]
        lng = [lng_ref[:, lanes(g)] for g in range(H)]
        lnb = [lnb_ref[:, lanes(g)] for g in range(H)]
        wsp = [wsp_ref[g] for g in range(H)]
        bsp = [bsp_ref[g] for g in range(H)]
        (ya, yb), vjp1 = jax.vjp(_stage1, zb, ua, va, za, o, gon, lng, lnb, wsp, bsp)
        y_a, y_b = jnp.concatenate(ya, axis=1), jnp.concatenate(yb, axis=1)
        ma, mb = bdot(y_a, wpa_ref), bdot(y_b, wpb_ref)
        ga, gb = p_ref[:, 4 * D:5 * D].astype(F32), p_ref[:, 5 * D:6 * D].astype(F32)
        merged, vjp2 = jax.vjp(_stage2, ma, mb, ga, gb)
        out = bdot(merged, wout_ref)
        gate = m_ref[:, 2 * D:3 * D] + bm_ref[:, 2 * D:3 * D]
        loss, vjp3 = jax.vjp(_stage3, out, x_ref[...], t_ref[...], gp_ref[...], gate)
        loss_ref[...] += jnp.broadcast_to(loss, loss_ref.shape)

        dout, dx, _, dgpost, dgate = vjp3(jnp.ones((1, 1), F32))
        dx_ref[...] = dx
        dmerged = bdot_t(dout, wout_ref)
        dma, dmb, dga, dgb = vjp2(dmerged)
        dya, dyb = bdot_t(dma, wpa_ref), bdot_t(dmb, wpb_ref)
        dzb, dua, dva, dza, do, dgon, dlng, dlnb, dwsp, dbsp = vjp1(
            ([dya[:, lanes(g)] for g in range(H)], [dyb[:, lanes(g)] for g in range(H)]))

        for blk, dlist in enumerate((dzb, dua, dva, dza)):
            for g in range(H):
                dp_ref[:, blk * D + g * DH: blk * D + (g + 1) * DH] = dlist[g].astype(BF16)
        dp_ref[:, 4 * D:5 * D] = dga.astype(BF16)
        dp_ref[:, 5 * D:6 * D] = dgb.astype(BF16)
        for g in range(H):
            do_ref[g] = do[g]
            dwsp_ref[g] += dwsp[g]
            dbsp_ref[g] += dbsp[g]
            dvec_ref[2:3, lanes(g)] += dlng[g]
            dvec_ref[3:4, lanes(g)] += dlnb[g]
        dvec_ref[0:1, :] += dgpost
        dvec_ref[1:2, :] += dgate
        dgon_ref[0:1, :] += dgon
        ya_ref[...] = y_a.astype(BF16)
        yb_ref[...] = y_b.astype(BF16)
        mg_ref[...] = merged.astype(BF16)
        dma_ref[...] = dma.astype(BF16)
        dmb_ref[...] = dmb.astype(BF16)
        dout_ref[...] = dout.astype(BF16)

    row = lambda w: pl.BlockSpec((GM, w), lambda i: (i, 0))
    heads = pl.BlockSpec((H, GM, DH), lambda i: (0, i, 0))
    full = lambda shape: pl.BlockSpec(shape, lambda i: tuple(0 for _ in shape))
    sds = jax.ShapeDtypeStruct
    return pl.pallas_call(
        body, name=name, grid=(n,),
        in_specs=[row(6 * D), heads, heads, row(D), row(D), full((1, 3 * D)), full((1, 3 * D)), full((1, DH)),
                  full((1, D)), full((1, D)), full((H, GM, GM)), full((H, GM, GM)),
                  full((D, D)), full((D, D)), full((D, D)), full((1, D))],
        out_specs=[full((8, DH)), row(6 * D), heads, row(D)] + [row(D)] * 6
        + [full((8, D)), full((8, DH)), full((H, GM, GM)), full((H, GM, GM))],
        out_shape=[sds((8, DH), F32), sds((rows, 6 * D), BF16), sds((H, rows, DH), F32), sds((rows, D), F32)]
        + [sds((rows, D), BF16)] * 6 + [sds((8, D), F32), sds((8, DH), F32), sds((H, GM, GM), F32), sds((H, GM, GM), F32)],
        compiler_params=_params(("arbitrary",)),
    )(p_rest, o_f, o_b, x, tgt, mod, bmod, gon, lng, lnb, wsp, bspb, wpa, wpb, wout, gpost)


def _silu_rows(c, *, name):
    def body(c_ref, o_ref):
        o_ref[...] = _silu(c_ref[...])

    return pl.pallas_call(body, name=name, out_shape=jax.ShapeDtypeStruct(c.shape, F32))(c)


def _dsilu_mul(c, d, *, name):
    def body(c_ref, d_ref, o_ref):
        _, vjp = jax.vjp(_silu, c_ref[...])
        (o_ref[...],) = vjp(d_ref[...])

    return pl.pallas_call(body, name=name, out_shape=jax.ShapeDtypeStruct(c.shape, F32))(c, d)


def _adamw(w, g, m, v, *, name):
    rows, cols = w.shape[-2:]
    t = _tile(rows if g.ndim == 2 else g.shape[1])
    c1 = 1.0 / (1.0 - ADAM_B1 ** ADAM_STEP)
    c2 = 1.0 / (1.0 - ADAM_B2 ** ADAM_STEP)

    def body(w_ref, g_ref, m_ref, v_ref, go_ref, d_ref, mo_ref, vo_ref):
        blk = lambda r: r[...].reshape(t, cols)
        gv = blk(g_ref)
        mn = ADAM_B1 * blk(m_ref) + (1.0 - ADAM_B1) * gv
        vn = ADAM_B2 * blk(v_ref) + (1.0 - ADAM_B2) * (gv * gv)
        delta = -ADAM_LR * ((mn * c1) / (jnp.sqrt(vn * c2) + ADAM_EPS) + ADAM_WD * blk(w_ref))
        for ref, val in ((go_ref, gv), (d_ref, delta), (mo_ref, mn), (vo_ref, vn)):
            ref[...] = val.reshape(ref.shape)

    tile = (pl.BlockSpec((1, t, cols), lambda i: (0, i, 0)) if w.ndim == 3 else pl.BlockSpec((t, cols), lambda i: (i, 0)))
    if g.ndim == 3:
        per = g.shape[1] // t
        g_tile = pl.BlockSpec((1, t, cols), lambda i: (i // per, i % per, 0))
    else:
        g_tile = pl.BlockSpec((t, cols), lambda i: (i, 0))
    return pl.pallas_call(
        body, name=name, grid=(rows // t,),
        in_specs=[tile, g_tile, tile, tile], out_specs=[tile] * 4,
        out_shape=[jax.ShapeDtypeStruct(w.shape, F32)] * 4,
        compiler_params=_params(("parallel",)),
    )(w, g, m, v)


def _add_sibling(fulls, gots, where, out_dtype, *, name):
    rows, cols = fulls[0].shape[1:]
    hr = rows // 2
    t = _tile(hr)
    nt = hr // t
    first = [sum(a.shape[0] for a in fulls[:k]) for k in range(len(fulls) + 1)]
    n = len(fulls)

    def body(w_ref, *refs):
        o_ref = refs[-1]
        j = pl.program_id(0)
        for k in range(n):
            @pl.when((j >= first[k]) & (j < first[k + 1]))
            def _(k=k):
                o_ref[...] = (refs[k][...].astype(F32) + refs[n + k][...].astype(F32)).astype(out_dtype)

    def tile(k, row0):
        def index(j, i, w):
            rb = jnp.where(j < first[k], 0, jnp.where(j >= first[k + 1], nt - 1, i))
            return jnp.clip(j - first[k], 0, fulls[k].shape[0] - 1), row0(w) + rb, 0
        return pl.BlockSpec((1, t, cols), index)

    return pl.pallas_call(
        body, name=name,
        grid_spec=pltpu.PrefetchScalarGridSpec(
            num_scalar_prefetch=1, grid=(first[-1], nt),
            in_specs=[tile(k, lambda w: w[0] * nt) for k in range(n)] + [tile(k, lambda w: 0) for k in range(n)],
            out_specs=pl.BlockSpec((1, t, cols), lambda j, i, w: (j, i, 0))),
        out_shape=jax.ShapeDtypeStruct((first[-1], hr, cols), out_dtype),
        compiler_params=_params(("parallel", "parallel")),
    )(where, *fulls, *gots)


def _sum_chips(own, slots, where, n_out, which, *, name):
    _, hr, cols = own.shape
    t = _tile(hr)

    def body(w_ref, a_ref, s_ref, o_ref):
        f = lambda v: v.astype(F32)
        o_ref[0] = ((f(a_ref[0]) + f(s_ref[0])) + f(s_ref[1])) + f(s_ref[2])

    return pl.pallas_call(
        body, name=name,
        grid_spec=pltpu.PrefetchScalarGridSpec(
            num_scalar_prefetch=1, grid=(hr // t,),
            in_specs=[pl.BlockSpec((1, t, cols), lambda i, w: (w[1], i, 0)),
                      pl.BlockSpec((N_CHIPS - 1, t, cols), lambda i, w: (0, i, 0))],
            out_specs=pl.BlockSpec((1, t, cols), lambda i, w: (w[which], i, 0))),
        out_shape=jax.ShapeDtypeStruct((n_out, hr, cols), F32), compiler_params=_params(("parallel",)),
    )(where, own, slots)


def _local_step(x, c, ctx, c_ctx, tgt, w_mod, b_mod, g_pre, g_post, w_qkv, w_ab, w_rest, w_conv, a_log, dt_bias,
                g_onorm, gm_ln_g, gm_ln_b, w_sp, b_sp, late_weights, on_large_gradients=None, run_after=None):
    rows, rows_c = x.shape[0], ctx.shape[0]
    cc = jnp.zeros((16, D), F32).at[0].set(c[0]).at[1].set(c_ctx)
    scc = _silu_rows(cc, name="silu_cond")
    mod = _mm(scc, w_mod, b_sections=True, after=run_after, name="mod_fwd")
    mod_x, mod_c = mod[0:1], mod[1:2]
    avec = jnp.zeros((1, DH), F32).at[0, :2 * H].set(a_log.reshape(-1))
    dvec = jnp.zeros((1, DH), F32).at[0, :2 * H].set(dt_bias.reshape(-1))
    bspb = jnp.broadcast_to(b_sp[:, :, None], (H, GM, GM))
    w_kv, wconv_kv = w_qkv[:, D:], w_conv[:, D:]

    h_c = _norm_fwd(ctx, g_pre, mod_c, b_mod, name="norm_fwd_ctx")
    pc_kv = _mm(h_c, w_kv, name="inproj_ctx_kv")
    pc_ab = _mm(h_c, w_ab, name="inproj_ctx_ab")
    kv_c = _prep_fwd(pc_kv, wconv_kv, H, name="prep_fwd_ctx")
    gb_c = _gates_fwd(pc_ab, avec, dvec, name="gates_fwd_ctx")
    s_zero = jnp.zeros((H, DH, DH), F32)
    _, _, ssf_c, ssb_c, tsf_c, tsb_c, s_f, s_b = _gdn_fwd(kv_c, gb_c, s_zero, s_zero, False, name="gdn_fwd_ctx")

    h_x = _norm_fwd(x, g_pre, mod_x, b_mod, name="norm_fwd_x")
    p_qkv = _mm(h_x, w_qkv, tm=rows, tn=D // 2, name="inproj_qkv")
    p_ab = _mm(h_x, w_ab, name="inproj_ab")
    p_rest = _mm(h_x, w_rest, tm=rows, tn=D // 2, out_dtype=BF16, name="inproj_rest")
    qkv = _prep_fwd(p_qkv, w_conv, 2 * H, name="prep_fwd_x")
    gb_x = _gates_fwd(p_ab, avec, dvec, name="gates_fwd_x")
    o_f, o_b, ssf, ssb, tsf, tsb, _, _ = _gdn_fwd(qkv, gb_x, s_f, s_b, True, name="gdn_fwd_x")

    w_pa, w_pb, w_out = late_weights(o_f)
    (loss_acc, dp_rest, do, dx_res, ya, yb, mg, dma, dmb, dout, dvec_post, dgon, dwsp, dbspb) = _post(
        p_rest, o_f, o_b, x, tgt, mod_x, b_mod, g_onorm, gm_ln_g, gm_ln_b, w_sp, bspb, w_pa, w_pb, w_out, g_post,
        name="post")
    g = {}
    g["w_pa"] = _mm(ya, dma, ta=True, out_dtype=BF16, name="dw_pa")
    g["w_pb"] = _mm(yb, dmb, ta=True, out_dtype=BF16, name="dw_pb")
    g["w_out"] = _mm(mg, dout, ta=True, out_dtype=BF16, name="dw_out")

    zeros_s = jnp.zeros((H, DH, DH), F32)
    dq_f, dq_b, dg_f, dg_b, ds0_f, ds0_b = _gdn_bwd(qkv, gb_x, ssf, ssb, tsf, tsb, do, zeros_s, zeros_s, True,
                                                    name="gdn_bwd_x")
    dp_qkv, dwc_x = _prep_bwd(p_qkv, w_conv, dq_f, dq_b, 2 * H, name="prep_bwd_x")
    dp_ab, dav_x, ddv_x = _gates_bwd(p_ab, avec, dvec, dg_f, dg_b, name="gates_bwd_x")
    dkv_f, dkv_b, dgc_f, dgc_b, _, _ = _gdn_bwd(kv_c, gb_c, ssf_c, ssb_c, tsf_c, tsb_c, jnp.zeros((H, rows_c, DH), F32),
                                                 ds0_f, ds0_b, False, name="gdn_bwd_ctx")
    dpc_kv, dwc_c = _prep_bwd(pc_kv, wconv_kv, dkv_f, dkv_b, H, name="prep_bwd_ctx")
    dpc_ab, dav_c, ddv_c = _gates_bwd(pc_ab, avec, dvec, dgc_f, dgc_b, name="gates_bwd_ctx")

    dw_kv_c = _mm(h_c, dpc_kv, ta=True, name="dw_kv_ctx")
    dw_qkv = _mm(h_x, dp_qkv, ta=True, tn=D, tk=2 * D, add=dw_kv_c, add_from=1, out_dtype=BF16, name="dw_qkv")
    dw_ab_c = _mm(h_c, dpc_ab, ta=True, name="dw_ab_ctx")
    dw_ab = _mm(h_x, dp_ab, ta=True, add=dw_ab_c, out_dtype=BF16, name="dw_ab")
    dw_rest = _mm(h_x, dp_rest, ta=True, tn=2 * D, tk=2 * D, out_dtype=BF16, name="dw_rest")
    g["w_in"] = (dw_qkv, dw_ab[:, :4 * H], dw_rest)
    token = on_large_gradients(g) if on_large_gradients is not None else None
    dh = _mm(dp_ab, w_ab, tb=True, after=token, name="dh_ab")
    dh = _mm(dp_qkv, w_qkv, tb=True, tk=3 * D, add=dh, after=token, name="dh_qkv")
    dh = _mm(dp_rest, w_rest, tb=True, tk=3 * D, add=dh, after=token, name="dh_rest")
    grad_x, dgpre_x, dm_x = _norm_bwd(x, g_pre, mod_x, b_mod, [dh], dx_res, name="norm_bwd_x")
    dh_c = _mm(dpc_ab, w_ab, tb=True, after=token, name="dhc_ab")
    dh_c = _mm(dpc_kv, w_kv, tb=True, add=dh_c, after=token, name="dhc_kv")
    _, dgpre_c, dm_c = _norm_bwd(ctx, g_pre, mod_c, b_mod, [dh_c], None, name="norm_bwd_ctx")

    dm_x = dm_x.at[:, 2 * D:].add(dvec_post[1:2])
    dmod = jnp.zeros((16, 3 * D), F32).at[0].set(dm_x[0]).at[1].set(dm_c[0])
    g["mod_factors"] = (scc[0], dm_x[0])
    g["dm_ctx"] = dm_c[0]
    g["silu_c_ctx"] = scc[1:2]
    dcc = _mm(dmod, w_mod, tb=True, b_sections=True, name="dcc")
    g["c_ctx"] = _dsilu_mul(cc[:8], dcc[:8], name="dc_ctx")[1]
    g["b_mod"] = dm_x + dm_c
    g["g_pre"] = dgpre_x + dgpre_c
    g["g_post"] = dvec_post[0:1]
    g["gm_ln_g"], g["gm_ln_b"] = dvec_post[2:3], dvec_post[3:4]
    g["g_onorm"] = dgon[0:1]
    g["w_sp"] = dwsp
    g["b_sp"] = jnp.sum(dbspb, axis=-1)
    g["w_conv"] = dwc_x.at[:, D:].add(dwc_c)
    g["a_log"] = (dav_x + dav_c)[0, :2 * H].reshape(2, H)
    g["dt_bias"] = (ddv_x + ddv_c)[0, :2 * H].reshape(2, H)
    return loss_acc[0, 0], grad_x, g


ANY = pl.BlockSpec(memory_space=pl.ANY)


def _place():
    x, y, c = lax.axis_index("x"), lax.axis_index("y"), lax.axis_index("c")
    chips = [(1 - x, y), (x, 1 - y), (1 - x, 1 - y)]
    return x, y, c, (x, y, 1 - c), chips


def _gather_shards(big, small, *, name):
    nb = len(big)

    def body(*refs):
        ins, sm_in = refs[:nb], refs[nb]
        outs, sm_out = refs[nb + 1:2 * nb + 1], refs[2 * nb + 1]
        send, recv = refs[2 * nb + 2:]
        x, y, c, sibling, chips = _place()
        mine = 2 * x + y

        def half(a, shard, hc):
            hr = big[a].shape[0] // 2
            return outs[a].at[shard, pl.ds(hc * hr, hr), :]

        def remote(k, src, dst, to):
            return pltpu.make_async_remote_copy(src_ref=src, dst_ref=dst, send_sem=send.at[k], recv_sem=recv.at[k],
                                                device_id=to, device_id_type=MESH)

        sends = []
        for a in range(nb):
            hr = big[a].shape[0] // 2
            for j, chip in enumerate(chips):
                sends.append(remote(a * 3 + j, ins[a].at[pl.ds(c * hr, hr), :], half(a, mine, c), (*chip, c)))
        for j, chip in enumerate(chips):
            sends.append(remote(nb * 3 + j, sm_in, sm_out.at[mine], (*chip, c)))
        for cp in sends:
            cp.start()
        base = nb * 3 + 3
        passed = []
        for a in range(nb):
            for j, (px, py) in enumerate(chips):
                theirs = 2 * px + py
                remote(a * 3 + j, half(a, theirs, c), half(a, theirs, c), sibling).wait_recv()
                fw = remote(base + a * 3 + j, half(a, theirs, c), half(a, theirs, c), sibling)
                fw.start()
                passed.append(fw)
        for a in range(nb):
            for j, (px, py) in enumerate(chips):
                theirs = 2 * px + py
                remote(base + a * 3 + j, half(a, theirs, 1 - c), half(a, theirs, 1 - c), sibling).wait_recv()
        for j, (px, py) in enumerate(chips):
            remote(nb * 3 + j, sm_in, sm_out.at[2 * px + py], sibling).wait_recv()
        for cp in sends + passed:
            cp.wait_send()

    n_remote = 2 * nb * 3 + 3
    outs = pl.pallas_call(
        body, name=name, in_specs=[ANY] * (nb + 1), out_specs=[ANY] * (nb + 1),
        out_shape=[jax.ShapeDtypeStruct((N_CHIPS,) + a.shape, a.dtype) for a in big + [small]],
        scratch_shapes=[pltpu.SemaphoreType.DMA((n_remote,)), pltpu.SemaphoreType.DMA((n_remote,))],
    )(*big, small)
    return outs[:nb], outs[nb]


def _gather_late_start(arrs, after, *, name):
    na = len(arrs)

    def body(*refs):
        ins, land = refs[:na], refs[na:2 * na]
        send, recv = refs[2 * na + 1], refs[2 * na + 2]
        token = refs[-1]
        x, y, c, _, chips = _place()
        for a in range(na):
            hr = arrs[a].shape[0] // 2
            for j, (px, py) in enumerate(chips):
                for other in range(2):
                    k = (a * 3 + j) * 2 + other
                    _remote(ins[a].at[pl.ds(c * hr, hr), :], land[a].at[2 * x + y, pl.ds(c * hr, hr), :],
                            send.at[k], recv.at[k], (px, py, c ^ other)).start()
        token[...] = jnp.zeros_like(token)

    lands = [((N_CHIPS,) + a.shape, a.dtype) for a in arrs]
    held = lambda a: pltpu.with_memory_space_constraint(a, pltpu.HBM)
    outs = pl.pallas_call(
        body, name=name, in_specs=[HBM] * (2 * na) + [ANY],
        out_specs=[SEM, SEM] + [HBM] * (2 * na) + [pl.BlockSpec(memory_space=pltpu.VMEM)],
        out_shape=[pltpu.SemaphoreType.DMA((6 * na,)), pltpu.SemaphoreType.DMA((6 * na,))]
        + [pltpu.HBM(a.shape, a.dtype) for a in arrs] + [pltpu.HBM(s, d) for s, d in lands]
        + [jax.ShapeDtypeStruct((8, 128), F32)],
        input_output_aliases={i: 2 + i for i in range(2 * na)},
        compiler_params=pltpu.CompilerParams(has_side_effects=pltpu.SideEffectType.DATAFLOW_SIDE_EFFECTING),
    )(*[held(a) for a in arrs], *[held(lax.empty(s, d)) for s, d in lands], after)
    return outs[0], outs[1], outs[2:2 + na], outs[2 + na:2 + 2 * na], outs[-1]


def _gather_late_wait(send, recv, thru, land, after, *, name):
    na = len(thru)

    def body(*refs):
        ins, slots = refs[:na], refs[na:2 * na]
        send_sem, recv_sem = refs[2 * na], refs[2 * na + 1]
        x, y, c, _, chips = _place()
        for a in range(na):
            hr = thru[a].shape[0] // 2
            for j, (px, py) in enumerate(chips):
                for other in range(2):
                    k = (a * 3 + j) * 2 + other
                    half = c ^ other
                    cp = _remote(ins[a].at[pl.ds(c * hr, hr), :], slots[a].at[2 * px + py, pl.ds(half * hr, hr), :],
                                 send_sem.at[k], recv_sem.at[k], (px, py, half))
                    cp.wait_send()
                    cp.wait_recv()

    outs = pl.pallas_call(
        body, name=name, in_specs=[HBM] * (2 * na) + [SEM, SEM, ANY], out_specs=[HBM] * (2 * na),
        out_shape=[pltpu.HBM(a.shape, a.dtype) for a in list(thru) + list(land)],
        input_output_aliases={i: i for i in range(2 * na)},
        compiler_params=pltpu.CompilerParams(has_side_effects=pltpu.SideEffectType.DATAFLOW_SIDE_EFFECTING),
    )(*thru, *land, send, recv, after)
    return outs[na:]


def _row_chunks(rows, row_bytes, align=8):
    n = max(1, min(rows // align, -(-rows * row_bytes // DMA_CHUNK_BYTES)))
    per = -(-(-(-rows // n)) // align) * align
    return [(r, min(per, rows - r)) for r in range(0, rows, per)]


def _remote(src, dst, send, recv, to):
    return pltpu.make_async_remote_copy(src_ref=src, dst_ref=dst, send_sem=send, recv_sem=recv, device_id=to,
                                        device_id_type=MESH)


def _exchange_halves(arrs, *, name):
    na = len(arrs)

    def body(*refs):
        ins, got = refs[:na], refs[na:2 * na]
        send, recv = refs[2 * na:]
        x, y, c, sibling, _ = _place()
        for a in range(na):
            ns, rows, cols = arrs[a].shape
            hr = rows // 2
            for s in range(ns):
                for r0, nr in _row_chunks(hr, cols * arrs[a].dtype.itemsize, 16):
                    _remote(ins[a].at[s, pl.ds((1 - c) * hr + r0, nr), :], got[a].at[s, pl.ds(r0, nr), :],
                            send.at[a], recv.at[a], sibling).start()
        for a in range(na):
            hr = arrs[a].shape[1] // 2
            _remote(ins[a].at[:, pl.ds((1 - c) * hr, hr), :], got[a], send.at[a], recv.at[a], sibling).wait()

    return pl.pallas_call(
        body, name=name, in_specs=[ANY] * na, out_specs=[ANY] * na,
        out_shape=[jax.ShapeDtypeStruct((a.shape[0], a.shape[1] // 2, a.shape[2]), a.dtype) for a in arrs],
        scratch_shapes=[pltpu.SemaphoreType.DMA((na,)), pltpu.SemaphoreType.DMA((na,))],
    )(*arrs)


def _scatter_sections(arrs, *, name):
    na = len(arrs)

    def body(*refs):
        ins, outs = refs[:na], refs[na:2 * na]
        send, recv = refs[2 * na:]
        x, y, c, _, chips = _place()
        for a in range(na):
            _, hr, cols = arrs[a].shape
            for j, (px, py) in enumerate(chips):
                for r0, nr in _row_chunks(hr, cols * arrs[a].dtype.itemsize, 16):
                    _remote(ins[a].at[2 * px + py, pl.ds(r0, nr), :], outs[a].at[j, pl.ds(r0, nr), :],
                            send.at[a * 3 + j], recv.at[a * 3 + j], (px, py, c)).start()
        for a in range(na):
            for j, (px, py) in enumerate(chips):
                _remote(ins[a].at[2 * px + py], outs[a].at[j], send.at[a * 3 + j], recv.at[a * 3 + j], (px, py, c)).wait()

    return pl.pallas_call(
        body, name=name, in_specs=[ANY] * na, out_specs=[ANY] * na,
        out_shape=[jax.ShapeDtypeStruct((N_CHIPS - 1,) + a.shape[1:], a.dtype) for a in arrs],
        scratch_shapes=[pltpu.SemaphoreType.DMA((3 * na,)), pltpu.SemaphoreType.DMA((3 * na,))],
    )(*arrs)


HBM = pl.BlockSpec(memory_space=pltpu.HBM)
SEM = pl.BlockSpec(memory_space=pltpu.SEMAPHORE)


def _scatter_start(arrs, *, name):
    na = len(arrs)

    def body(*refs):
        ins, land = refs[:na], refs[na:2 * na]
        send, recv = refs[2 * na], refs[2 * na + 1]
        token = refs[-1]
        x, y, c, _, chips = _place()
        for a in range(na):
            _, hr, cols = arrs[a].shape
            for j, (px, py) in enumerate(chips):
                for r0, nr in _row_chunks(hr, cols * arrs[a].dtype.itemsize, 16):
                    _remote(ins[a].at[2 * px + py, pl.ds(r0, nr), :], land[a].at[j, pl.ds(r0, nr), :],
                            send.at[a * 3 + j], recv.at[a * 3 + j], (px, py, c)).start()
        token[...] = jnp.zeros_like(token)

    slots = [((N_CHIPS - 1,) + a.shape[1:], a.dtype) for a in arrs]
    held = lambda a: pltpu.with_memory_space_constraint(a, pltpu.HBM)
    outs = pl.pallas_call(
        body, name=name, in_specs=[HBM] * (2 * na),
        out_specs=[SEM, SEM] + [HBM] * (2 * na) + [pl.BlockSpec(memory_space=pltpu.VMEM)],
        out_shape=[pltpu.SemaphoreType.DMA((3 * na,)), pltpu.SemaphoreType.DMA((3 * na,))]
        + [pltpu.HBM(a.shape, a.dtype) for a in arrs] + [pltpu.HBM(s, d) for s, d in slots]
        + [jax.ShapeDtypeStruct((8, 128), F32)],
        input_output_aliases={i: 2 + i for i in range(2 * na)},
        compiler_params=pltpu.CompilerParams(has_side_effects=pltpu.SideEffectType.DATAFLOW_SIDE_EFFECTING),
    )(*[held(a) for a in arrs], *[held(lax.empty(s, d)) for s, d in slots])
    return outs[0], outs[1], outs[2:2 + na], outs[2 + na:2 + 2 * na], outs[-1]


def _scatter_wait(send, recv, thru, land, after, *, name):
    na = len(thru)

    def body(*refs):
        ins, slots = refs[:na], refs[na:2 * na]
        send_sem, recv_sem = refs[2 * na], refs[2 * na + 1]
        x, y, c, _, chips = _place()
        for a in range(na):
            for j, (px, py) in enumerate(chips):
                cp = _remote(ins[a].at[2 * px + py], slots[a].at[j], send_sem.at[a * 3 + j], recv_sem.at[a * 3 + j],
                             (px, py, c))
                cp.wait_send()
                cp.wait_recv()

    outs = pl.pallas_call(
        body, name=name, in_specs=[HBM] * (2 * na) + [SEM, SEM, ANY], out_specs=[HBM] * (2 * na),
        out_shape=[pltpu.HBM(a.shape, a.dtype) for a in list(thru) + list(land)],
        input_output_aliases={i: i for i in range(2 * na)},
        compiler_params=pltpu.CompilerParams(has_side_effects=pltpu.SideEffectType.DATAFLOW_SIDE_EFFECTING),
    )(*thru, *land, send, recv, after)
    return outs[:na], outs[na:]


def _finish_reduce(big, small, *, name):
    nb = len(big)

    def body(*refs):
        outs, sm = refs[nb + 1:2 * nb + 1], refs[2 * nb + 1]
        send, recv = refs[2 * nb + 2:]
        x, y, c, sibling, chips = _place()
        blk = lambda px, py, pc: sm.at[4 * px + 2 * py + pc]
        for a in range(nb):
            _, hr, cols = big[a].shape
            for r0, nr in _row_chunks(hr, cols * 4):
                _remote(outs[a].at[c, pl.ds(r0, nr), :], outs[a].at[c, pl.ds(r0, nr), :], send.at[a], recv.at[a],
                        sibling).start()
        first = [_remote(blk(x, y, c), blk(x, y, c), send.at[nb], recv.at[nb], sibling)]
        first += [_remote(blk(x, y, c), blk(x, y, c), send.at[nb + 1 + j], recv.at[nb + 1 + j], (*chip, c))
                  for j, chip in enumerate(chips)]
        for cp in first:
            cp.start()
        passed = []
        for j, (px, py) in enumerate(chips):
            _remote(blk(px, py, c), blk(px, py, c), send.at[nb + 1 + j], recv.at[nb + 1 + j], sibling).wait_recv()
            fw = _remote(blk(px, py, c), blk(px, py, c), send.at[nb + 4 + j], recv.at[nb + 4 + j], sibling)
            fw.start()
            passed.append(fw)
        for a in range(nb):
            _remote(outs[a].at[c], outs[a].at[1 - c], send.at[a], recv.at[a], sibling).wait()
        _remote(blk(x, y, c), blk(x, y, 1 - c), send.at[nb], recv.at[nb], sibling).wait_recv()
        for j, (px, py) in enumerate(chips):
            _remote(blk(px, py, c), blk(px, py, 1 - c), send.at[nb + 4 + j], recv.at[nb + 4 + j], sibling).wait_recv()
        for cp in first + passed:
            cp.wait_send()

    n_remote = nb + 7
    arrs = list(big) + [small]
    outs = pl.pallas_call(
        body, name=name, in_specs=[ANY] * (nb + 1), out_specs=[ANY] * (nb + 1),
        out_shape=[jax.ShapeDtypeStruct(a.shape, F32) for a in arrs],
        input_output_aliases={i: i for i in range(nb + 1)},
        scratch_shapes=[pltpu.SemaphoreType.DMA((n_remote,)), pltpu.SemaphoreType.DMA((n_remote,))],
    )(*arrs)
    return outs[:nb], outs[nb]


def _reduce_start(sectioned, where):
    got = _exchange_halves([a for parts in sectioned for a in parts], name="rs_exchange_halves")
    first = [sum(len(parts) for parts in sectioned[:i]) for i in range(len(sectioned) + 1)]
    chip_sum = [_add_sibling(parts, got[first[i]:first[i + 1]], where, BF16, name=f"rs_add_sibling_{i}")
                for i, parts in enumerate(sectioned)]
    send, recv, thru, land, token = _scatter_start(chip_sum, name="rs_scatter_start")
    return (send, recv, thru, land), token


def _reduce_finish(handle, small, where, after):
    send, recv, thru, land = handle
    got = _exchange_halves([small], name="rs_exchange_small")
    small_sum = _add_sibling([small], got, where, F32, name="rs_add_sibling_small")
    small_slots = _scatter_sections([small_sum], name="rs_scatter_small")[0]
    own, slots = _scatter_wait(send, recv, thru, land, after, name="rs_scatter_wait")
    red = [_sum_chips(p, s, where, 2, 0, name=f"rs_sum_chips_{i}") for i, (p, s) in enumerate(zip(own, slots))]
    red_small = _sum_chips(small_sum, small_slots, where, 2 * N_CHIPS, 2, name="rs_sum_chips_small")
    big, sm = _finish_reduce(red, red_small, name="rs_finish")
    return big, sm.reshape(-1, sm.shape[-1])


def kernel(x, c, ctx, c_ctx, w_mod, b_mod, g_pre, g_post, w_in, w_conv, a_log, dt_bias, g_onorm, gm_ln_g, gm_ln_b, w_sp, b_sp, w_pa, w_pb, w_out, loss_target, m_c_ctx, m_w_mod, m_b_mod, m_g_pre, m_g_post, m_w_in, m_w_conv, m_a_log, m_dt_bias, m_g_onorm, m_gm_ln_g, m_gm_ln_b, m_w_sp, m_b_sp, m_w_pa, m_w_pb, m_w_out, v_c_ctx, v_w_mod, v_b_mod, v_g_pre, v_g_post, v_w_in, v_w_conv, v_a_log, v_dt_bias, v_g_onorm, v_gm_ln_g, v_gm_ln_b, v_w_sp, v_b_sp, v_w_pa, v_w_pb, v_w_out):
    names = ["c_ctx", "w_mod", "b_mod", "g_pre", "g_post", "w_in", "w_conv", "a_log", "dt_bias", "g_onorm", "gm_ln_g",
             "gm_ln_b", "w_sp", "b_sp", "w_pa", "w_pb", "w_out"]
    w = dict(zip(names, (c_ctx, w_mod, b_mod, g_pre, g_post, w_in, w_conv, a_log, dt_bias, g_onorm, gm_ln_g, gm_ln_b,
                         w_sp, b_sp, w_pa, w_pb, w_out)))
    m = dict(zip(names, (m_c_ctx, m_w_mod, m_b_mod, m_g_pre, m_g_post, m_w_in, m_w_conv, m_a_log, m_dt_bias, m_g_onorm,
                         m_gm_ln_g, m_gm_ln_b, m_w_sp, m_b_sp, m_w_pa, m_w_pb, m_w_out)))
    v = dict(zip(names, (v_c_ctx, v_w_mod, v_b_mod, v_g_pre, v_g_post, v_w_in, v_w_conv, v_a_log, v_dt_bias, v_g_onorm,
                         v_gm_ln_g, v_gm_ln_b, v_w_sp, v_b_sp, v_w_pa, v_w_pb, v_w_out)))
    xy = 2 * lax.axis_index("x") + lax.axis_index("y")
    where = jnp.stack([lax.axis_index("c"), xy, 2 * xy + lax.axis_index("c")]).astype(jnp.int32)

    shards = [a[0].astype(BF16) for a in (w_mod, w_in)]
    gathered, wconv_all = _gather_shards(shards, w_conv[0], name="gather_weights")
    own = lambda full, shard: lax.dynamic_update_slice(full, shard[None], (xy, 0, 0))
    wm_all, win_all = [own(f, s) for f, s in zip(gathered, shards)]
    wconv_all = own(wconv_all, w_conv[0])
    late_shards = [a[0].astype(BF16) for a in (w_pa, w_pb, w_out)]
    late = _gather_late_start(late_shards, wconv_all, name="gather_late_start")

    def late_weights(after):
        lands = _gather_late_wait(*late[:4], after, name="gather_late_wait")
        return [own(f, s).reshape(D, D) for f, s in zip(lands, late_shards)]
    w_conv_f = jnp.concatenate([wconv_all[s] for s in range(N_CHIPS)], axis=1)
    shard_cols = IN_COLS // N_CHIPS
    cut = OFF_A - shard_cols
    assert 0 < cut and cut + 4 * H < shard_cols
    w_qkv = jnp.concatenate([win_all[0], win_all[1][:, :cut]], axis=1)
    w_ab = jnp.pad(win_all[1][:, cut:cut + 4 * H], ((0, 0), (0, DH - 4 * H)))
    w_rest = jnp.concatenate([win_all[1][:, cut + 4 * H:], win_all[2], win_all[3]], axis=1)

    blk = D // N_CHIPS
    rest_cut = shard_cols - cut - 4 * H
    big_names = ("w_in", "w_pa", "w_pb", "w_out")
    in_flight = []

    def start_reduce(grads):
        dw_qkv, dw_ab, dw_rest = grads["w_in"]
        g_win = [dw_qkv[:, :shard_cols],
                 jnp.concatenate([dw_qkv[:, shard_cols:], dw_ab, dw_rest[:, :rest_cut]], axis=1),
                 dw_rest[:, rest_cut:rest_cut + shard_cols], dw_rest[:, rest_cut + shard_cols:]]
        sectioned = [[a.astype(BF16)[None] for a in g_win]]
        sectioned += [[grads[k].reshape(N_CHIPS, blk, D)] for k in ("w_pa", "w_pb", "w_out")]
        handle, token = _reduce_start(sectioned, where)
        in_flight.append(handle)
        return token

    loss_local, grad_x, g = _local_step(
        x[0], c, ctx[0], c_ctx, loss_target[0], wm_all, b_mod, g_pre, g_post, w_qkv, w_ab, w_rest,
        w_conv_f, a_log[0], dt_bias[0], g_onorm, gm_ln_g, gm_ln_b, w_sp[0], b_sp[0],
        late_weights, on_large_gradients=start_reduce, run_after=late[4])
    g["loss"] = loss_local
    g["_pad"] = jnp.zeros((SMALL_LAYOUT["_pad"][1],), F32)
    g["mod_factors"] = lax.dynamic_update_slice(jnp.zeros((2 * N_CHIPS, 4 * D), F32),
                                                jnp.concatenate(g["mod_factors"])[None], (where[2], 0))
    tail = jnp.zeros((N_CHIPS * SMALL_ROWS * 128 - sum(s for _, s in SMALL_LAYOUT.values()),), F32)
    flat = jnp.concatenate([g[k].reshape(-1) for k in SMALL_LAYOUT] + [tail])
    reduced, gr_small = _reduce_finish(in_flight[0], flat.reshape(N_CHIPS, SMALL_ROWS, 128), where, g["b_mod"])
    gr_tiny = gr_small[TINY_ROW0:]

    def entry(arr, k):
        off, size = SMALL_LAYOUT[k]
        row, col = off // 128 - TINY_ROW0, off % 128
        return arr[row:row + size // 128].reshape(-1) if size >= 128 else arr[row, col:col + size]

    factors = entry(gr_tiny, "mod_factors").reshape(2 * N_CHIPS, 4 * D)
    pad_rows = lambda a: jnp.pad(a, ((0, 16 - a.shape[0]), (0, 0)))
    lhs = pad_rows(jnp.concatenate([factors[:, :D], g["silu_c_ctx"]], axis=0))
    rhs = pad_rows(jnp.concatenate([factors[:, D:], entry(gr_tiny, "dm_ctx")[None]], axis=0))
    mod_cols = 3 * D // N_CHIPS
    gr_wm = _mm(lhs, lax.dynamic_slice(rhs, (0, xy * mod_cols), (16, mod_cols)), ta=True, name="dw_mod")

    res = {k: _adamw(w[k], gr, m[k], v[k], name=f"adamw_{k}") for k, gr in zip(big_names[1:], reduced[1:])}
    flip = lambda a: jnp.swapaxes(a, 1, 2)
    res["w_in"] = [flip(r) for r in _adamw(flip(w_in), flip(reduced[0].reshape(w_in.shape)), flip(m_w_in), flip(v_w_in),
                                           name="adamw_w_in")]
    res["w_mod"] = _adamw(w_mod[0], gr_wm, m_w_mod[0], v_w_mod[0], name="adamw_w_mod")
    res["w_sp"] = _adamw(w_sp.reshape(-1, 128), gr_small, m_w_sp.reshape(-1, 128), v_w_sp.reshape(-1, 128),
                         name="adamw_w_sp")
    tiny = [k for k, (off, _) in SMALL_LAYOUT.items() if TINY_ROW0 <= off // 128 < TINY_ROW0 + TINY_ROWS]
    tiny_names = [k for k in tiny if k not in ("loss", "_pad")]

    def pack(src):
        parts = [src[k].reshape(-1) if k in tiny_names else jnp.zeros((SMALL_LAYOUT[k][1],), F32) for k in tiny]
        return jnp.concatenate(parts).reshape(TINY_ROWS, 128)

    tiny_res = _adamw(pack(w), gr_tiny[:TINY_ROWS], pack(m), pack(v), name="adamw_tiny")
    for k in tiny_names:
        res[k] = [entry(r, k) for r in tiny_res]
    g_conv = lax.dynamic_slice(entry(gr_tiny, "w_conv").reshape(3, 3 * D), (0, xy * (3 * D // N_CHIPS)),
                               (3, 3 * D // N_CHIPS))
    conv_res = _adamw(jnp.pad(w_conv[0], ((0, 5), (0, 0))), jnp.pad(g_conv, ((0, 5), (0, 0))),
                      jnp.pad(m_w_conv[0], ((0, 5), (0, 0))), jnp.pad(v_w_conv[0], ((0, 5), (0, 0))), name="adamw_w_conv")
    res["w_conv"] = [r[:3] for r in conv_res]
    res = {k: [r.reshape(w[k].shape) for r in res[k]] for k in names}

    out = [entry(gr_tiny, "loss").reshape(()), grad_x[None]]
    for i in range(4):
        out += [res[k][i] for k in names]
    return tuple(out)
```

```python
import functools

import jax
import jax.numpy as jnp
from jax import lax
from jax.experimental import pallas as pl
from jax.experimental.pallas import tpu as pltpu

F32 = jnp.float32
BF16 = jnp.bfloat16
HI = lax.Precision.HIGHEST
MESH = pl.DeviceIdType.MESH

D = 1024
H = 8
DH = 128
CH = 64
LOG_CH = 6
PAIR = 2 * CH
GM = 128
assert 1 << LOG_CH == CH and PAIR == DH
PREC_POWERS = ((lax.Precision.HIGH, lax.Precision.HIGH),) * 3 + ((None, None),) * 2
assert len(PREC_POWERS) == LOG_CH - 1
EPS = 1e-6
N_CHIPS = 4
OFF_A = 3 * D
OFF_ZB = OFF_A + 4 * H
IN_COLS = OFF_ZB + 6 * D
VMEM_LIMIT_V7X = 56 * 1024 * 1024
DMA_CHUNK_BYTES = 2 * 1024 * 1024

ADAM_LR, ADAM_B1, ADAM_B2, ADAM_EPS, ADAM_WD, ADAM_STEP = 0.001, 0.9, 0.999, 1e-08, 0.01, 10

SMALL_LAYOUT = {}
_off = 0
for _n, _s in (("w_sp", H * GM * GM), ("c_ctx", D), ("b_mod", 3 * D), ("g_pre", D), ("g_post", D), ("gm_ln_g", D),
               ("gm_ln_b", D), ("b_sp", H * GM), ("g_onorm", DH), ("a_log", 2 * H), ("dt_bias", 2 * H), ("loss", 1),
               ("_pad", 128 - 4 * H - 1 + 6 * 128), ("w_conv", 3 * 3 * D), ("dm_ctx", 3 * D), ("mod_factors", 8 * 4 * D)):
    SMALL_LAYOUT[_n] = (_off, _s)
    _off += _s
SMALL_ROWS = 368
assert N_CHIPS * SMALL_ROWS * 128 >= _off and (SMALL_ROWS // 2) % 8 == 0 and _off % 128 == 0
TINY_ROW0 = SMALL_LAYOUT["c_ctx"][0] // 128
TINY_ROWS = SMALL_LAYOUT["w_conv"][0] // 128 - TINY_ROW0
assert TINY_ROWS % 8 == 0


def _params(sem=None):
    return pltpu.CompilerParams(dimension_semantics=sem, vmem_limit_bytes=VMEM_LIMIT_V7X)


def _tile(n, cands=(256, 128, 64, 32, 16, 8)):
    if n <= cands[0]:
        return n
    for cand in cands:
        if n % cand == 0 and cand >= 64:
            return cand
    return max(d for d in range(8, cands[0], 8) if n % d == 0)


def _silu(x):
    return x * jax.nn.sigmoid(x)


def _gelu(x):
    return 0.5 * x * (1.0 + jnp.tanh(0.7978845608028654 * (x + 0.044715 * (x * x * x))))


def _mm(a, b, *, ta=False, tb=False, out_dtype=F32, tm=1024, tn=1024, tk=1024, add=None, add_from=0,
        b_sections=False, out_sections=False, after=None, name):
    m, k = (a.shape[1], a.shape[0]) if ta else a.shape
    if b_sections:
        sect = b.shape[2]
        n = b.shape[1] if tb else b.shape[0] * sect
        tn, tk = (tn, sect) if tb else (sect, tk)
    else:
        n = b.shape[0] if tb else b.shape[1]
    tm, tn, tk = min(tm, m), min(tn, n), min(tk, k)
    assert m % tm == 0 and n % tn == 0 and k % tk == 0, (name, m, n, k, tm, tn, tk)
    nk = k // tk
    dims = (((0,) if ta else (1,), (1,) if tb else (0,)), ((), ()))
    has_add = add is not None
    assert not has_add or add.shape == (m, n - add_from * tn), (name, add.shape)

    def body(*refs):
        a_ref, b_ref = refs[:2]
        o_ref = refs[2 + int(has_add) + int(after is not None)]
        acc_ref = refs[-1]
        kk = pl.program_id(2)
        bv = b_ref[0] if b_sections else b_ref[...]
        part = lax.dot_general(a_ref[...].astype(BF16), bv.astype(BF16), dims, preferred_element_type=F32)

        def finish(res):
            if has_add:
                res = res + jnp.where(pl.program_id(1) >= add_from, refs[2][...], 0.0)
            if out_sections:
                o_ref[0] = res.astype(out_dtype)
            else:
                o_ref[...] = res.astype(out_dtype)

        if nk == 1:
            finish(part)
            return

        @pl.when(kk == 0)
        def _():
            acc_ref[...] = part

        @pl.when((kk > 0) & (kk < nk - 1))
        def _():
            acc_ref[...] += part

        @pl.when(kk == nk - 1)
        def _():
            finish(acc_ref[...] + part)

    a_spec = pl.BlockSpec((tk, tm), lambda i, j, q: (q, i)) if ta else pl.BlockSpec((tm, tk), lambda i, j, q: (i, q))
    if b_sections:
        b_spec = (pl.BlockSpec((1, tn, tk), lambda i, j, q: (q, j, 0)) if tb
                  else pl.BlockSpec((1, tk, tn), lambda i, j, q: (j, q, 0)))
    else:
        b_spec = pl.BlockSpec((tn, tk), lambda i, j, q: (j, q)) if tb else pl.BlockSpec((tk, tn), lambda i, j, q: (q, j))
    add_spec = [pl.BlockSpec((tm, tn), lambda i, j, q: (i, jnp.maximum(j - add_from, 0)))] if has_add else []
    if out_sections:
        out_spec, out_shape = pl.BlockSpec((1, tm, tn), lambda i, j, q: (j, i, 0)), (n // tn, m, tn)
    else:
        out_spec, out_shape = pl.BlockSpec((tm, tn), lambda i, j, q: (i, j)), (m, n)
    return pl.pallas_call(
        body, name=name, grid=(m // tm, n // tn, nk),
        in_specs=[a_spec, b_spec] + add_spec + ([pl.BlockSpec(memory_space=pl.ANY)] if after is not None else []),
        out_specs=out_spec, out_shape=jax.ShapeDtypeStruct(out_shape, out_dtype),
        scratch_shapes=[pltpu.VMEM((tm, tn), F32)] if nk > 1 else [],
        compiler_params=_params(("parallel", "parallel", "arbitrary")),
    )(*([a, b] + ([add] if has_add else []) + ([after] if after is not None else [])))


def _h_fn(x, g, m):
    shift, scale = m[:, 0:D], m[:, D:2 * D]
    r = lax.rsqrt(jnp.mean(x * x, axis=-1, keepdims=True) + EPS)
    return (x * r * g) * (1.0 + scale) + shift


def _norm_fwd(x, g, mod, bmod, *, name):
    rows = x.shape[0]
    t = min(512, rows)

    def body(x_ref, g_ref, m_ref, b_ref, h_ref):
        h_ref[...] = _h_fn(x_ref[...], g_ref[...], m_ref[...] + b_ref[...]).astype(BF16)

    vec = lambda w: pl.BlockSpec((1, w), lambda i: (0, 0))
    return pl.pallas_call(
        body, name=name, grid=(rows // t,),
        in_specs=[pl.BlockSpec((t, D), lambda i: (i, 0)), vec(D), vec(3 * D), vec(3 * D)],
        out_specs=pl.BlockSpec((t, D), lambda i: (i, 0)),
        out_shape=jax.ShapeDtypeStruct((rows, D), BF16),
        compiler_params=_params(("parallel",)),
    )(x, g, mod, bmod)


def _norm_bwd(x, g, mod, bmod, dh_parts, resid, *, name):
    rows = x.shape[0]
    t = min(512, rows)
    n_parts = len(dh_parts)
    has_resid = resid is not None

    def body(*refs):
        x_ref, g_ref, m_ref, b_ref = refs[:4]
        parts = refs[4:4 + n_parts]
        r_ref = refs[4 + n_parts] if has_resid else None
        dx_ref, dg_ref, dm_ref = refs[-3:]
        i = pl.program_id(0)
        dh = parts[0][...]
        for p in parts[1:]:
            dh = dh + p[...]
        _, vjp = jax.vjp(_h_fn, x_ref[...], g_ref[...], m_ref[...] + b_ref[...])
        dx, dg, dm = vjp(dh)
        if has_resid:
            dx = dx + r_ref[...]
        dx_ref[...] = dx

        @pl.when(i == 0)
        def _():
            dg_ref[...] = dg
            dm_ref[...] = dm

        @pl.when(i > 0)
        def _():
            dg_ref[...] += dg
            dm_ref[...] += dm

    vec = lambda w: pl.BlockSpec((1, w), lambda i: (0, 0))
    tile = pl.BlockSpec((t, D), lambda i: (i, 0))
    ins = [x, g, mod, bmod, *dh_parts] + ([resid] if has_resid else [])
    return pl.pallas_call(
        body, name=name, grid=(rows // t,),
        in_specs=[tile, vec(D), vec(3 * D), vec(3 * D)] + [tile] * (n_parts + int(has_resid)),
        out_specs=[tile, vec(D), vec(3 * D)],
        out_shape=[jax.ShapeDtypeStruct((rows, D), F32), jax.ShapeDtypeStruct((1, D), F32),
                   jax.ShapeDtypeStruct((1, 3 * D), F32)],
        compiler_params=_params(("arbitrary",)),
    )(*ins)


def _conv_tile(u_ref, r0, t, rows, w0, w1, w2):
    u = u_ref[pl.ds(r0, t), :]
    prev8 = u_ref[pl.ds(pl.multiple_of(jnp.maximum(r0 - 8, 0), 8), 8), :]
    next8 = u_ref[pl.ds(pl.multiple_of(jnp.minimum(r0 + t, rows - 8), 8), 8), :]
    r8 = lax.broadcasted_iota(jnp.int32, (8, DH), 0)
    prev_row = jnp.sum(jnp.where(r8 == 7, prev8, 0.0), axis=0, keepdims=True)
    next_row = jnp.sum(jnp.where(r8 == 0, next8, 0.0), axis=0, keepdims=True)
    prev_row = jnp.where(r0 > 0, prev_row, 0.0)
    next_row = jnp.where(r0 + t < rows, next_row, 0.0)
    ri = lax.broadcasted_iota(jnp.int32, (t, DH), 0)
    um1 = jnp.where(ri == 0, prev_row, pltpu.roll(u, 1, 0))
    up1 = jnp.where(ri == t - 1, next_row, pltpu.roll(u, t - 1, 0))
    return w0 * um1 + w1 * u + w2 * up1, um1, u, up1


def _rowlocal(z, norm):
    y = _silu(z)
    return y * lax.rsqrt(jnp.sum(y * y, axis=-1, keepdims=True) + EPS) if norm else y


def _prep_fwd(p, wconv, n_norm, *, name):
    rows, nb = p.shape[0], p.shape[1] // DH
    t = min(512, rows)
    out = None
    for norm, b0, b1 in ((True, 0, n_norm), (False, n_norm, nb)):
        def body(u_ref, w_ref, *rest, norm=norm):
            o_ref = rest[-1]
            w0, w1, w2 = w_ref[0:1, :], w_ref[1:2, :], w_ref[2:3, :]

            def step(s, carry):
                r0 = pl.multiple_of(s * t, t)
                z, _, _, _ = _conv_tile(u_ref, r0, t, rows, w0, w1, w2)
                o_ref[0, pl.ds(r0, t), :] = _rowlocal(z, norm)
                return carry

            lax.fori_loop(0, rows // t, step, 0)

        held = [] if out is None else [out]
        out = pl.pallas_call(
            body, name=f"{name}_{'norm' if norm else 'plain'}", grid=(b1 - b0,),
            in_specs=[pl.BlockSpec((rows, DH), lambda j, b0=b0: (0, b0 + j)),
                      pl.BlockSpec((3, DH), lambda j, b0=b0: (0, b0 + j))] + [ANY] * len(held),
            out_specs=pl.BlockSpec((1, rows, DH), lambda j, b0=b0: (b0 + j, 0, 0)),
            out_shape=jax.ShapeDtypeStruct((nb, rows, DH), F32),
            input_output_aliases={2: 0} if held else {},
            compiler_params=_params(("parallel",)),
        )(p, wconv, *held)
    return out


def _prep_bwd(p, wconv, d_a, d_b, n_norm, *, name):
    rows, nb = p.shape[0], p.shape[1] // DH
    t = min(512, rows)
    outs = []
    for norm, b0, b1 in ((True, 0, n_norm), (False, n_norm, nb)):
        def body(u_ref, w_ref, da_ref, db_ref, *rest, norm=norm):
            du_ref, dw_ref, dz_ref = rest[-3:]
            w0, w1, w2 = w_ref[0:1, :], w_ref[1:2, :], w_ref[2:3, :]

            def step1(s, carry):
                a0, a1, a2 = carry
                r0 = pl.multiple_of(s * t, t)
                z, um1, u, up1 = _conv_tile(u_ref, r0, t, rows, w0, w1, w2)
                _, vjp = jax.vjp(lambda zz: _rowlocal(zz, norm), z)
                (dz,) = vjp(da_ref[0, pl.ds(r0, t), :] + db_ref[0, pl.ds(r0, t), :])
                dz_ref[pl.ds(r0, t), :] = dz
                red = lambda v: jnp.sum(v, axis=0, keepdims=True)
                return a0 + red(dz * um1), a1 + red(dz * u), a2 + red(dz * up1)

            zero = jnp.zeros((1, DH), F32)
            a0, a1, a2 = lax.fori_loop(0, rows // t, step1, (zero, zero, zero))
            dw_ref[0:1, :] = a0
            dw_ref[1:2, :] = a1
            dw_ref[2:3, :] = a2

            def step2(s, carry):
                r0 = pl.multiple_of(s * t, t)
                du, _, _, _ = _conv_tile(dz_ref, r0, t, rows, w2, w1, w0)
                du_ref[pl.ds(r0, t), :] = du.astype(BF16)
                return carry

            lax.fori_loop(0, rows // t, step2, 0)

        col = pl.BlockSpec((rows, DH), lambda j, b0=b0: (0, b0 + j))
        w_spec = pl.BlockSpec((3, DH), lambda j, b0=b0: (0, b0 + j))
        d_spec = pl.BlockSpec((1, rows, DH), lambda j, b0=b0: (b0 + j, 0, 0))
        outs = pl.pallas_call(
            body, name=f"{name}_{'norm' if norm else 'plain'}", grid=(b1 - b0,),
            in_specs=[col, w_spec, d_spec, d_spec] + [ANY] * len(outs), out_specs=[col, w_spec],
            out_shape=[jax.ShapeDtypeStruct((rows, nb * DH), BF16), jax.ShapeDtypeStruct((3, nb * DH), F32)],
            input_output_aliases={4: 0, 5: 1} if outs else {},
            scratch_shapes=[pltpu.VMEM((rows, DH), F32)],
            compiler_params=_params(("parallel",)),
        )(p, wconv, d_a, d_b, *outs)
    return outs


def _gates_fn(pab, avec, dvec):
    t = pab.shape[0]
    lane = lax.broadcasted_iota(jnp.int32, pab.shape, 1)
    xg = pab + dvec
    sp = jnp.maximum(xg, 0.0) + jnp.log(1.0 + jnp.exp(-jnp.abs(xg)))
    g = jnp.where(lane < 2 * H, -jnp.exp(avec) * sp, 0.0)
    ii = lax.broadcasted_iota(jnp.int32, (t, t), 0)
    jj = lax.broadcasted_iota(jnp.int32, (t, t), 1)
    same = (ii >> LOG_CH) == (jj >> LOG_CH)
    cum_f = _dot(jnp.where(same & (jj <= ii), 1.0, 0.0), g, precision=HI)
    cum_r = _dot(jnp.where(same & (jj >= ii), 1.0, 0.0), g, precision=HI)
    return jnp.where(lane < H, cum_f, jnp.where(lane < 2 * H, cum_r, jnp.where(lane < 4 * H, jax.nn.sigmoid(pab), 0.0)))


def _gates_fwd(pab, avec, dvec, *, name):
    rows = pab.shape[0]
    t = min(2 * PAIR, rows)

    def body(p_ref, a_ref, d_ref, o_ref):
        o_ref[...] = _gates_fn(p_ref[...], a_ref[...], d_ref[...])

    vec = pl.BlockSpec((1, DH), lambda i: (0, 0))
    tile = pl.BlockSpec((t, DH), lambda i: (i, 0))
    return pl.pallas_call(
        body, name=name, grid=(rows // t,), in_specs=[tile, vec, vec], out_specs=tile,
        out_shape=jax.ShapeDtypeStruct((rows, DH), F32), compiler_params=_params(("parallel",)),
    )(pab, avec, dvec)


def _gates_bwd(pab, avec, dvec, d_a, d_b, *, name):
    rows = pab.shape[0]
    t = min(2 * PAIR, rows)

    def body(p_ref, a_ref, d_ref, da_ref, db_ref, dp_ref, dav_ref, ddv_ref):
        i = pl.program_id(0)
        _, vjp = jax.vjp(_gates_fn, p_ref[...], a_ref[...], d_ref[...])
        dp, dav, ddv = vjp(da_ref[...] + db_ref[...])
        dp_ref[...] = dp.astype(BF16)

        @pl.when(i == 0)
        def _():
            dav_ref[...] = dav
            ddv_ref[...] = ddv

        @pl.when(i > 0)
        def _():
            dav_ref[...] += dav
            ddv_ref[...] += ddv

    vec = pl.BlockSpec((1, DH), lambda i: (0, 0))
    tile = pl.BlockSpec((t, DH), lambda i: (i, 0))
    return pl.pallas_call(
        body, name=name, grid=(rows // t,), in_specs=[tile, vec, vec, tile, tile], out_specs=[tile, vec, vec],
        out_shape=[jax.ShapeDtypeStruct((rows, DH), BF16), jax.ShapeDtypeStruct((1, DH), F32),
                   jax.ShapeDtypeStruct((1, DH), F32)],
        compiler_params=_params(("arbitrary",)),
    )(pab, avec, dvec, d_a, d_b)


def _dot_general(a, b, dims, precision=None):
    return lax.dot_general(a, b, (dims, ((), ())), precision=precision, preferred_element_type=F32)


@functools.partial(jax.custom_vjp, nondiff_argnums=(2,))
def _dot_bf16(a, b, dims):
    return _dot_general(a.astype(BF16), b.astype(BF16), dims)


def _dot_bf16_fwd(a, b, dims):
    return _dot_bf16(a, b, dims), (a, b)


def _dot_bf16_bwd(dims, res, g):
    a, b = res
    (ca,), (cb,) = dims
    da = _dot_bf16(g, b, ((1,), (1 - cb,))) if ca == 1 else _dot_bf16(b, g, ((1 - cb,), (1,)))
    db = _dot_bf16(a, g, ((1 - ca,), (0,))) if cb == 0 else _dot_bf16(g, a, ((0,), (1 - ca,)))
    return da, db


_dot_bf16.defvjp(_dot_bf16_fwd, _dot_bf16_bwd)


def _dot(a, b, dims=((1,), (0,)), precision=None):
    if precision is None and a.dtype == F32 and b.dtype == F32:
        return _dot_bf16(a, b, dims)
    return _dot_general(a, b, dims, precision)


_NT = ((1,), (1,))
_TN = ((0,), (0,))


@jax.custom_vjp
def _saved_inverse(neg_a, inv):
    return inv


def _saved_inverse_fwd(neg_a, inv):
    return inv, inv


def _saved_inverse_bwd(inv, d_inv):
    idx = range(len(inv))
    left = [_dot(inv[i], d_inv[i], _TN) for i in idx]
    d_neg_a = [_dot(left[i], inv[i], _NT) for i in idx]
    return d_neg_a, [jnp.zeros_like(t) for t in inv]


_saved_inverse.defvjp(_saved_inverse_fwd, _saved_inverse_bwd)


def _pairs(s, q, k, v, gcol, bcol, revs, inv_saved=None):
    idx = range(len(revs))
    ii = lax.broadcasted_iota(jnp.int32, (PAIR, PAIR), 0)
    jj = lax.broadcasted_iota(jnp.int32, (PAIR, PAIR), 1)
    same = (ii >> LOG_CH) == (jj >> LOG_CH)
    incl_d = (same & (ii >= jj), same & (ii <= jj))
    strict_d = (same & (ii > jj), same & (ii < jj))
    incl = [incl_d[int(r)] for r in revs]
    strict = [strict_d[int(r)] for r in revs]
    eye = jnp.where(ii == jj, 1.0, 0.0)
    gc_i = [jnp.broadcast_to(gcol[i], (PAIR, DH)) for i in idx]
    gc_j = [gc_i[i].T for i in idx]
    decay = [jnp.where(incl[i], jnp.exp(jnp.where(incl[i], gc_i[i] - gc_j[i], 0.0)), 0.0) for i in idx]
    b_b = [jnp.broadcast_to(bcol[i], (PAIR, DH)) for i in idx]
    kb = [k[i] * b_b[i] for i in idx]
    kk = [_dot(kb[i], k[i], _NT) for i in idx]
    bp = [jnp.where(strict[i], -kk[i] * decay[i], 0.0) for i in idx]
    if inv_saved is not None:
        inv = _saved_inverse(bp, inv_saved)
    else:
        low = bp
        for prec_sq, prec_acc in PREC_POWERS:
            bp = [_dot(bp[i], bp[i], precision=prec_sq) for i in idx]
            more = [_dot(low[i], bp[i], precision=prec_acc) for i in idx]
            low = [low[i] + bp[i] + more[i] for i in idx]
        inv = [eye + low[i] for i in idx]
    eg = [jnp.exp(gc_i[i]) for i in idx]
    sol = [_dot(inv[i], jnp.concatenate([v[i] * b_b[i], kb[i] * eg[i]], axis=1)) for i in idx]
    u_val = [sol[i][:, :DH] for i in idx]
    w_key = [sol[i][:, DH:] for i in idx]
    row = lax.broadcasted_iota(jnp.int32, (PAIR, 1), 0)
    has_q = q[0] is not None
    if has_q:
        qc = [q[i] * (DH ** -0.5) for i in idx]
        qk = [_dot(qc[i], k[i], _NT) for i in idx]
        attn = [qk[i] * decay[i] for i in idx]
        qd = [qc[i] * eg[i] for i in idx]
    outs = [[None, None] for _ in idx]
    zeros = jnp.zeros((CH, DH), F32)
    for step in range(2):
        cidx = [(1 - step) if revs[i] else step for i in idx]
        sl = [slice(c * CH, (c + 1) * CH) for c in cidx]
        last = [c * CH if revs[i] else c * CH + CH - 1 for i, c in zip(idx, cidx)]
        gl = [jnp.sum(jnp.where(row == last[i], gcol[i], 0.0), axis=0, keepdims=True) for i in idx]
        k_tail = [k[i][sl[i]] * jnp.exp(gl[i] - gc_i[i][sl[i]]) for i in idx]
        ws = [_dot(w_key[i][sl[i]], s[i]) for i in idx]
        v_new = [u_val[i][sl[i]] - ws[i] for i in idx]
        if has_q:
            v_pad = [jnp.concatenate([v_new[i], zeros] if cidx[i] == 0 else [zeros, v_new[i]], axis=0) for i in idx]
            o_state = [_dot(qd[i][sl[i]], s[i]) for i in idx]
            o_local = [_dot(attn[i][sl[i]], v_pad[i]) for i in idx]
            for i in idx:
                outs[i][cidx[i]] = o_state[i] + o_local[i]
        kv = [_dot(k_tail[i], v_new[i], _TN) for i in idx]
        s = [s[i] * jnp.exp(gl[i]) + kv[i] for i in idx]
    return s, ([jnp.concatenate(outs[i], axis=0) for i in idx] if has_q else None), inv


def _lane_col(tile, idx):
    lane = lax.broadcasted_iota(jnp.int32, tile.shape, 1)
    return jnp.sum(jnp.where(lane == idx, tile, 0.0), axis=1, keepdims=True)


def _gdn_fwd(qkv, gb, s0f, s0b, has_q, *, name):
    nb, rows, _ = qkv.shape
    n = rows // PAIR
    qoff = H if has_q else 0

    def body(qf_ref, qb_ref, gf_ref, gr_ref, s0f_ref, s0b_ref, of_ref, ob_ref, ssf_ref, ssb_ref, tsf_ref, tsb_ref,
             sf_ref, sb_ref):
        @pl.when(pl.program_id(0) == 0)
        def _():
            sf_ref[...] = s0f_ref[...]
            sb_ref[...] = s0b_ref[...]

        gtiles = (gf_ref[...], gr_ref[...])

        dirs = ((qf_ref, sf_ref, ssf_ref, of_ref), (qb_ref, sb_ref, ssb_ref, ob_ref))
        ts_refs = (tsf_ref, tsb_ref)

        work = [(h, d) for h in range(H) for d in range(2)]
        loaded = []
        for h, d in work:
            q_ref, s_ref, _, _ = dirs[d]
            loaded.append((s_ref[h], q_ref[h] if has_q else None, q_ref[qoff + h], q_ref[qoff + H + h],
                           _lane_col(gtiles[d], d * H + h), _lane_col(gtiles[d], 2 * H + d * H + h)))
        s_new, o, inv = _pairs(*[list(col) for col in zip(*loaded)], revs=[d == 1 for _, d in work])
        for i, (h, d) in enumerate(work):
            _, s_ref, ss_ref, o_ref = dirs[d]
            ss_ref[0, h] = loaded[i][0]
            ts_refs[d][0, h] = inv[i]
            s_ref[h] = s_new[i]
            o_ref[h] = o[i] if has_q else jnp.zeros((PAIR, DH), F32)

    fwd3 = lambda i: (0, i, 0)
    rev3 = lambda i: (0, n - 1 - i, 0)
    state = pl.BlockSpec((H, DH, DH), lambda i: (0, 0, 0))
    saved = pl.BlockSpec((1, H, DH, DH), lambda i: (i, 0, 0, 0))
    return pl.pallas_call(
        body, name=name, grid=(n,),
        in_specs=[pl.BlockSpec((nb, PAIR, DH), fwd3), pl.BlockSpec((nb, PAIR, DH), rev3),
                  pl.BlockSpec((PAIR, DH), lambda i: (i, 0)), pl.BlockSpec((PAIR, DH), lambda i: (n - 1 - i, 0)),
                  state, state],
        out_specs=[pl.BlockSpec((H, PAIR, DH), fwd3), pl.BlockSpec((H, PAIR, DH), rev3), saved, saved, saved, saved,
                   state, state],
        out_shape=[jax.ShapeDtypeStruct((H, rows, DH), F32)] * 2 + [jax.ShapeDtypeStruct((n, H, DH, DH), F32)] * 4
        + [jax.ShapeDtypeStruct((H, DH, DH), F32)] * 2,
        compiler_params=_params(("arbitrary",)),
    )(qkv, qkv, gb, gb, s0f, s0b)


def _gdn_bwd(qkv, gb, ssf, ssb, tsf, tsb, do, dsf, dsb, has_q, *, name):
    nb, rows, _ = qkv.shape
    n = rows // PAIR
    qoff = H if has_q else 0

    def body(qf_ref, qb_ref, gf_ref, gr_ref, ssf_ref, ssb_ref, tsf_ref, tsb_ref, dof_ref, dob_ref, dsf0_ref, dsb0_ref,
             dqf_ref, dqb_ref, dgf_ref, dgr_ref, dsf_ref, dsb_ref):
        ts_refs = (tsf_ref, tsb_ref)
        @pl.when(pl.program_id(0) == 0)
        def _():
            dsf_ref[...] = dsf0_ref[...]
            dsb_ref[...] = dsb0_ref[...]

        gtiles = (gf_ref[...], gr_ref[...])
        lane = lax.broadcasted_iota(jnp.int32, (PAIR, DH), 1)

        dirs = ((qf_ref, ssf_ref, dof_ref, dsf_ref, dqf_ref), (qb_ref, ssb_ref, dob_ref, dsb_ref, dqb_ref))

        work = [(h, d) for h in range(H) for d in range(2)]
        revs = [d == 1 for _, d in work]
        s_in, q_in, k_in, v_in, g_in, b_in, ds_out, do_out, inv_in = [], [], [], [], [], [], [], [], []
        for h, d in work:
            q_ref, ss_ref, do_ref, ds_ref, _ = dirs[d]
            s_in.append(ss_ref[0, h])
            inv_in.append(ts_refs[d][0, h])
            q_in.append(q_ref[h] if has_q else None)
            k_in.append(q_ref[qoff + h])
            v_in.append(q_ref[qoff + H + h])
            g_in.append(_lane_col(gtiles[d], d * H + h))
            b_in.append(_lane_col(gtiles[d], 2 * H + d * H + h))
            ds_out.append(ds_ref[h])
            do_out.append(do_ref[h] if has_q else None)
        if has_q:
            _, vjp = jax.vjp(lambda s_, q_, k_, v_, g_, b_: _pairs(s_, q_, k_, v_, g_, b_, revs, inv_in)[:2],
                             s_in, q_in, k_in, v_in, g_in, b_in)
            ds, dq, dk, dv, dg, db = vjp((ds_out, do_out))
        else:
            _, vjp = jax.vjp(lambda s_, k_, v_, g_, b_: _pairs(s_, q_in, k_, v_, g_, b_, revs, inv_in)[0],
                             s_in, k_in, v_in, g_in, b_in)
            ds, dk, dv, dg, db = vjp(ds_out)
        dgb = [jnp.zeros((PAIR, DH), F32)] * 2
        for i, (h, d) in enumerate(work):
            dq_ref, ds_ref = dirs[d][4], dirs[d][3]
            ds_ref[h] = ds[i]
            if has_q:
                dq_ref[h] = dq[i]
            dq_ref[qoff + h], dq_ref[qoff + H + h] = dk[i], dv[i]
            dgb[d] = (dgb[d] + jnp.where(lane == d * H + h, dg[i], 0.0)
                      + jnp.where(lane == 2 * H + d * H + h, db[i], 0.0))
        dgf_ref[...] = dgb[0]
        dgr_ref[...] = dgb[1]

    fwd3 = lambda i: (0, n - 1 - i, 0)
    rev3 = lambda i: (0, i, 0)
    state = pl.BlockSpec((H, DH, DH), lambda i: (0, 0, 0))
    saved = pl.BlockSpec((1, H, DH, DH), lambda i: (n - 1 - i, 0, 0, 0))
    gf_spec = pl.BlockSpec((PAIR, DH), lambda i: (n - 1 - i, 0))
    gr_spec = pl.BlockSpec((PAIR, DH), lambda i: (i, 0))
    return pl.pallas_call(
        body, name=name, grid=(n,),
        in_specs=[pl.BlockSpec((nb, PAIR, DH), fwd3), pl.BlockSpec((nb, PAIR, DH), rev3), gf_spec, gr_spec,
                  saved, saved, saved, saved,
                  pl.BlockSpec((H, PAIR, DH), fwd3), pl.BlockSpec((H, PAIR, DH), rev3), state, state],
        out_specs=[pl.BlockSpec((nb, PAIR, DH), fwd3), pl.BlockSpec((nb, PAIR, DH), rev3), gf_spec, gr_spec, state, state],
        out_shape=[jax.ShapeDtypeStruct((nb, rows, DH), F32)] * 2 + [jax.ShapeDtypeStruct((rows, DH), F32)] * 2
        + [jax.ShapeDtypeStruct((H, DH, DH), F32)] * 2,
        compiler_params=_params(("arbitrary",)),
    )(qkv, qkv, gb, gb, ssf, ssb, tsf, tsb, do, do, dsf, dsb)


def _stage1(zb, ua, va, za, o, gon, lng, lnb, wsp, bsp):
    gv = [_gelu(t) for t in va]
    mu = sum(jnp.sum(t, axis=-1, keepdims=True) for t in gv) * (1.0 / D)
    xc = [t - mu for t in gv]
    var = sum(jnp.sum(t * t, axis=-1, keepdims=True) for t in xc) * (1.0 / D)
    rs = lax.rsqrt(var + EPS)
    ya, yb = [], []
    for g in range(H):
        vv = xc[g] * rs * lng[g] + lnb[g]
        s = _dot(wsp[g], vv) + bsp[g]
        ya.append(_gelu(ua[g]) * s * _silu(za[g]))
        r = lax.rsqrt(jnp.mean(o[g] * o[g], axis=-1, keepdims=True) + EPS)
        yb.append(o[g] * r * gon * _silu(zb[g]))
    return ya, yb


def _stage2(ma, mb, ga, gb):
    return jax.nn.sigmoid(ga) * ma + jax.nn.sigmoid(gb) * mb


def _stage3(out, x, tgt, gpost, gate):
    r = out * lax.rsqrt(jnp.mean(out * out, axis=-1, keepdims=True) + EPS) * gpost
    err = x + gate * r - tgt
    return 0.5 * jnp.sum(jnp.mean(err * err, axis=-1, keepdims=True), axis=0, keepdims=True)


def _post(p_rest, o_f, o_b, x, tgt, mod, bmod, gon, lng, lnb, wsp, bspb, wpa, wpb, wout, gpost, *, name):
    rows = x.shape[0]
    n = rows // GM
    lanes = lambda g: slice(g * DH, (g + 1) * DH)
    bdot = lambda a, w_ref: _dot(a.astype(BF16), w_ref[...])
    bdot_t = lambda a, w_ref: _dot(a.astype(BF16), w_ref[...], _NT)

    def body(p_ref, of_ref, ob_ref, x_ref, t_ref, m_ref, bm_ref, gon_ref, lng_ref, lnb_ref, wsp_ref, bsp_ref,
             wpa_ref, wpb_ref, wout_ref, gp_ref,
             loss_ref, dp_ref, do_ref, dx_ref, ya_ref, yb_ref, mg_ref, dma_ref, dmb_ref, dout_ref,
             dvec_ref, dgon_ref, dwsp_ref, dbsp_ref):
        @pl.when(pl.program_id(0) == 0)
        def _():
            loss_ref[...] = jnp.zeros_like(loss_ref)
            dvec_ref[...] = jnp.zeros_like(dvec_ref)
            dgon_ref[...] = jnp.zeros_like(dgon_ref)
            dwsp_ref[...] = jnp.zeros_like(dwsp_ref)
            dbsp_ref[...] = jnp.zeros_like(dbsp_ref)

        piece = lambda blk: [p_ref[:, blk * D + g * DH: blk * D + (g + 1) * DH].astype(F32) for g in range(H)]
        zb, ua, va, za = piece(0), piece(1), piece(2), piece(3)
        o = [of_ref[g] + ob_ref[g] for g in range(H)]
        gon = gon_ref[...]
        lng = [lng_ref[:, lanes(g)] for g in range(H)]
        lnb = [lnb_ref[:, lanes(g)] for g in range(H)]
        wsp = [wsp_ref[g] for g in range(H)]
        bsp = [bsp_ref[g] for g in range(H)]
        (ya, yb), vjp1 = jax.vjp(_stage1, zb, ua, va, za, o, gon, lng, lnb, wsp, bsp)
        y_a, y_b = jnp.concatenate(ya, axis=1), jnp.concatenate(yb, axis=1)
        ma, mb = bdot(y_a, wpa_ref), bdot(y_b, wpb_ref)
        ga, gb = p_ref[:, 4 * D:5 * D].astype(F32), p_ref[:, 5 * D:6 * D].astype(F32)
        merged, vjp2 = jax.vjp(_stage2, ma, mb, ga, gb)
        out = bdot(merged, wout_ref)
        gate = m_ref[:, 2 * D:3 * D] + bm_ref[:, 2 * D:3 * D]
        loss, vjp3 = jax.vjp(_stage3, out, x_ref[...], t_ref[...], gp_ref[...], gate)
        loss_ref[...] += jnp.broadcast_to(loss, loss_ref.shape)

        dout, dx, _, dgpost, dgate = vjp3(jnp.ones((1, 1), F32))
        dx_ref[...] = dx
        dmerged = bdot_t(dout, wout_ref)
        dma, dmb, dga, dgb = vjp2(dmerged)
        dya, dyb = bdot_t(dma, wpa_ref), bdot_t(dmb, wpb_ref)
        dzb, dua, dva, dza, do, dgon, dlng, dlnb, dwsp, dbsp = vjp1(
            ([dya[:, lanes(g)] for g in range(H)], [dyb[:, lanes(g)] for g in range(H)]))

        for blk, dlist in enumerate((dzb, dua, dva, dza)):
            for g in range(H):
                dp_ref[:, blk * D + g * DH: blk * D + (g + 1) * DH] = dlist[g].astype(BF16)
        dp_ref[:, 4 * D:5 * D] = dga.astype(BF16)
        dp_ref[:, 5 * D:6 * D] = dgb.astype(BF16)
        for g in range(H):
            do_ref[g] = do[g]
            dwsp_ref[g] += dwsp[g]
            dbsp_ref[g] += dbsp[g]
            dvec_ref[2:3, lanes(g)] += dlng[g]
            dvec_ref[3:4, lanes(g)] += dlnb[g]
        dvec_ref[0:1, :] += dgpost
        dvec_ref[1:2, :] += dgate
        dgon_ref[0:1, :] += dgon
        ya_ref[...] = y_a.astype(BF16)
        yb_ref[...] = y_b.astype(BF16)
        mg_ref[...] = merged.astype(BF16)
        dma_ref[...] = dma.astype(BF16)
        dmb_ref[...] = dmb.astype(BF16)
        dout_ref[...] = dout.astype(BF16)

    row = lambda w: pl.BlockSpec((GM, w), lambda i: (i, 0))
    heads = pl.BlockSpec((H, GM, DH), lambda i: (0, i, 0))
    full = lambda shape: pl.BlockSpec(shape, lambda i: tuple(0 for _ in shape))
    sds = jax.ShapeDtypeStruct
    return pl.pallas_call(
        body, name=name, grid=(n,),
        in_specs=[row(6 * D), heads, heads, row(D), row(D), full((1, 3 * D)), full((1, 3 * D)), full((1, DH)),
                  full((1, D)), full((1, D)), full((H, GM, GM)), full((H, GM, GM)),
                  full((D, D)), full((D, D)), full((D, D)), full((1, D))],
        out_specs=[full((8, DH)), row(6 * D), heads, row(D)] + [row(D)] * 6
        + [full((8, D)), full((8, DH)), full((H, GM, GM)), full((H, GM, GM))],
        out_shape=[sds((8, DH), F32), sds((rows, 6 * D), BF16), sds((H, rows, DH), F32), sds((rows, D), F32)]
        + [sds((rows, D), BF16)] * 6 + [sds((8, D), F32), sds((8, DH), F32), sds((H, GM, GM), F32), sds((H, GM, GM), F32)],
        compiler_params=_params(("arbitrary",)),
    )(p_rest, o_f, o_b, x, tgt, mod, bmod, gon, lng, lnb, wsp, bspb, wpa, wpb, wout, gpost)


def _silu_rows(c, *, name):
    def body(c_ref, o_ref):
        o_ref[...] = _silu(c_ref[...])

    return pl.pallas_call(body, name=name, out_shape=jax.ShapeDtypeStruct(c.shape, F32))(c)


def _dsilu_mul(c, d, *, name):
    def body(c_ref, d_ref, o_ref):
        _, vjp = jax.vjp(_silu, c_ref[...])
        (o_ref[...],) = vjp(d_ref[...])

    return pl.pallas_call(body, name=name, out_shape=jax.ShapeDtypeStruct(c.shape, F32))(c, d)


def _adamw(w, g, m, v, *, name):
    rows, cols = w.shape[-2:]
    t = _tile(rows if g.ndim == 2 else g.shape[1])
    c1 = 1.0 / (1.0 - ADAM_B1 ** ADAM_STEP)
    c2 = 1.0 / (1.0 - ADAM_B2 ** ADAM_STEP)

    def body(w_ref, g_ref, m_ref, v_ref, go_ref, d_ref, mo_ref, vo_ref):
        blk = lambda r: r[...].reshape(t, cols)
        gv = blk(g_ref)
        mn = ADAM_B1 * blk(m_ref) + (1.0 - ADAM_B1) * gv
        vn = ADAM_B2 * blk(v_ref) + (1.0 - ADAM_B2) * (gv * gv)
        delta = -ADAM_LR * ((mn * c1) / (jnp.sqrt(vn * c2) + ADAM_EPS) + ADAM_WD * blk(w_ref))
        for ref, val in ((go_ref, gv), (d_ref, delta), (mo_ref, mn), (vo_ref, vn)):
            ref[...] = val.reshape(ref.shape)

    tile = (pl.BlockSpec((1, t, cols), lambda i: (0, i, 0)) if w.ndim == 3 else pl.BlockSpec((t, cols), lambda i: (i, 0)))
    if g.ndim == 3:
        per = g.shape[1] // t
        g_tile = pl.BlockSpec((1, t, cols), lambda i: (i // per, i % per, 0))
    else:
        g_tile = pl.BlockSpec((t, cols), lambda i: (i, 0))
    return pl.pallas_call(
        body, name=name, grid=(rows // t,),
        in_specs=[tile, g_tile, tile, tile], out_specs=[tile] * 4,
        out_shape=[jax.ShapeDtypeStruct(w.shape, F32)] * 4,
        compiler_params=_params(("parallel",)),
    )(w, g, m, v)


def _add_sibling(fulls, gots, where, out_dtype, *, name):
    rows, cols = fulls[0].shape[1:]
    hr = rows // 2
    t = _tile(hr)
    nt = hr // t
    first = [sum(a.shape[0] for a in fulls[:k]) for k in range(len(fulls) + 1)]
    n = len(fulls)

    def body(w_ref, *refs):
        o_ref = refs[-1]
        j = pl.program_id(0)
        for k in range(n):
            @pl.when((j >= first[k]) & (j < first[k + 1]))
            def _(k=k):
                o_ref[...] = (refs[k][...].astype(F32) + refs[n + k][...].astype(F32)).astype(out_dtype)

    def tile(k, row0):
        def index(j, i, w):
            rb = jnp.where(j < first[k], 0, jnp.where(j >= first[k + 1], nt - 1, i))
            return jnp.clip(j - first[k], 0, fulls[k].shape[0] - 1), row0(w) + rb, 0
        return pl.BlockSpec((1, t, cols), index)

    return pl.pallas_call(
        body, name=name,
        grid_spec=pltpu.PrefetchScalarGridSpec(
            num_scalar_prefetch=1, grid=(first[-1], nt),
            in_specs=[tile(k, lambda w: w[0] * nt) for k in range(n)] + [tile(k, lambda w: 0) for k in range(n)],
            out_specs=pl.BlockSpec((1, t, cols), lambda j, i, w: (j, i, 0))),
        out_shape=jax.ShapeDtypeStruct((first[-1], hr, cols), out_dtype),
        compiler_params=_params(("parallel", "parallel")),
    )(where, *fulls, *gots)


def _sum_chips(own, slots, where, n_out, which, *, name):
    _, hr, cols = own.shape
    t = _tile(hr)

    def body(w_ref, a_ref, s_ref, o_ref):
        f = lambda v: v.astype(F32)
        o_ref[0] = ((f(a_ref[0]) + f(s_ref[0])) + f(s_ref[1])) + f(s_ref[2])

    return pl.pallas_call(
        body, name=name,
        grid_spec=pltpu.PrefetchScalarGridSpec(
            num_scalar_prefetch=1, grid=(hr // t,),
            in_specs=[pl.BlockSpec((1, t, cols), lambda i, w: (w[1], i, 0)),
                      pl.BlockSpec((N_CHIPS - 1, t, cols), lambda i, w: (0, i, 0))],
            out_specs=pl.BlockSpec((1, t, cols), lambda i, w: (w[which], i, 0))),
        out_shape=jax.ShapeDtypeStruct((n_out, hr, cols), F32), compiler_params=_params(("parallel",)),
    )(where, own, slots)


def _local_step(x, c, ctx, c_ctx, tgt, w_mod, b_mod, g_pre, g_post, w_qkv, w_ab, w_rest, w_conv, a_log, dt_bias,
                g_onorm, gm_ln_g, gm_ln_b, w_sp, b_sp, late_weights, on_large_gradients=None, run_after=None):
    rows, rows_c = x.shape[0], ctx.shape[0]
    cc = jnp.zeros((16, D), F32).at[0].set(c[0]).at[1].set(c_ctx)
    scc = _silu_rows(cc, name="silu_cond")
    mod = _mm(scc, w_mod, b_sections=True, after=run_after, name="mod_fwd")
    mod_x, mod_c = mod[0:1], mod[1:2]
    avec = jnp.zeros((1, DH), F32).at[0, :2 * H].set(a_log.reshape(-1))
    dvec = jnp.zeros((1, DH), F32).at[0, :2 * H].set(dt_bias.reshape(-1))
    bspb = jnp.broadcast_to(b_sp[:, :, None], (H, GM, GM))
    w_kv, wconv_kv = w_qkv[:, D:], w_conv[:, D:]

    h_c = _norm_fwd(ctx, g_pre, mod_c, b_mod, name="norm_fwd_ctx")
    pc_kv = _mm(h_c, w_kv, name="inproj_ctx_kv")
    pc_ab = _mm(h_c, w_ab, name="inproj_ctx_ab")
    kv_c = _prep_fwd(pc_kv, wconv_kv, H, name="prep_fwd_ctx")
    gb_c = _gates_fwd(pc_ab, avec, dvec, name="gates_fwd_ctx")
    s_zero = jnp.zeros((H, DH, DH), F32)
    _, _, ssf_c, ssb_c, tsf_c, tsb_c, s_f, s_b = _gdn_fwd(kv_c, gb_c, s_zero, s_zero, False, name="gdn_fwd_ctx")

    h_x = _norm_fwd(x, g_pre, mod_x, b_mod, name="norm_fwd_x")
    p_qkv = _mm(h_x, w_qkv, tm=rows, tn=D // 2, name="inproj_qkv")
    p_ab = _mm(h_x, w_ab, name="inproj_ab")
    p_rest = _mm(h_x, w_rest, tm=rows, tn=D // 2, out_dtype=BF16, name="inproj_rest")
    qkv = _prep_fwd(p_qkv, w_conv, 2 * H, name="prep_fwd_x")
    gb_x = _gates_fwd(p_ab, avec, dvec, name="gates_fwd_x")
    o_f, o_b, ssf, ssb, tsf, tsb, _, _ = _gdn_fwd(qkv, gb_x, s_f, s_b, True, name="gdn_fwd_x")

    w_pa, w_pb, w_out = late_weights(o_f)
    (loss_acc, dp_rest, do, dx_res, ya, yb, mg, dma, dmb, dout, dvec_post, dgon, dwsp, dbspb) = _post(
        p_rest, o_f, o_b, x, tgt, mod_x, b_mod, g_onorm, gm_ln_g, gm_ln_b, w_sp, bspb, w_pa, w_pb, w_out, g_post,
        name="post")
    g = {}
    g["w_pa"] = _mm(ya, dma, ta=True, out_dtype=BF16, name="dw_pa")
    g["w_pb"] = _mm(yb, dmb, ta=True, out_dtype=BF16, name="dw_pb")
    g["w_out"] = _mm(mg, dout, ta=True, out_dtype=BF16, name="dw_out")

    zeros_s = jnp.zeros((H, DH, DH), F32)
    dq_f, dq_b, dg_f, dg_b, ds0_f, ds0_b = _gdn_bwd(qkv, gb_x, ssf, ssb, tsf, tsb, do, zeros_s, zeros_s, True,
                                                    name="gdn_bwd_x")
    dp_qkv, dwc_x = _prep_bwd(p_qkv, w_conv, dq_f, dq_b, 2 * H, name="prep_bwd_x")
    dp_ab, dav_x, ddv_x = _gates_bwd(p_ab, avec, dvec, dg_f, dg_b, name="gates_bwd_x")
    dkv_f, dkv_b, dgc_f, dgc_b, _, _ = _gdn_bwd(kv_c, gb_c, ssf_c, ssb_c, tsf_c, tsb_c, jnp.zeros((H, rows_c, DH), F32),
                                                 ds0_f, ds0_b, False, name="gdn_bwd_ctx")
    dpc_kv, dwc_c = _prep_bwd(pc_kv, wconv_kv, dkv_f, dkv_b, H, name="prep_bwd_ctx")
    dpc_ab, dav_c, ddv_c = _gates_bwd(pc_ab, avec, dvec, dgc_f, dgc_b, name="gates_bwd_ctx")

    dw_kv_c = _mm(h_c, dpc_kv, ta=True, name="dw_kv_ctx")
    dw_qkv = _mm(h_x, dp_qkv, ta=True, tn=D, tk=2 * D, add=dw_kv_c, add_from=1, out_dtype=BF16, name="dw_qkv")
    dw_ab_c = _mm(h_c, dpc_ab, ta=True, name="dw_ab_ctx")
    dw_ab = _mm(h_x, dp_ab, ta=True, add=dw_ab_c, out_dtype=BF16, name="dw_ab")
    dw_rest = _mm(h_x, dp_rest, ta=True, tn=2 * D, tk=2 * D, out_dtype=BF16, name="dw_rest")
    g["w_in"] = (dw_qkv, dw_ab[:, :4 * H], dw_rest)
    token = on_large_gradients(g) if on_large_gradients is not None else None
    dh = _mm(dp_ab, w_ab, tb=True, after=token, name="dh_ab")
    dh = _mm(dp_qkv, w_qkv, tb=True, tk=3 * D, add=dh, after=token, name="dh_qkv")
    dh = _mm(dp_rest, w_rest, tb=True, tk=3 * D, add=dh, after=token, name="dh_rest")
    grad_x, dgpre_x, dm_x = _norm_bwd(x, g_pre, mod_x, b_mod, [dh], dx_res, name="norm_bwd_x")
    dh_c = _mm(dpc_ab, w_ab, tb=True, after=token, name="dhc_ab")
    dh_c = _mm(dpc_kv, w_kv, tb=True, add=dh_c, after=token, name="dhc_kv")
    _, dgpre_c, dm_c = _norm_bwd(ctx, g_pre, mod_c, b_mod, [dh_c], None, name="norm_bwd_ctx")

    dm_x = dm_x.at[:, 2 * D:].add(dvec_post[1:2])
    dmod = jnp.zeros((16, 3 * D), F32).at[0].set(dm_x[0]).at[1].set(dm_c[0])
    g["mod_factors"] = (scc[0], dm_x[0])
    g["dm_ctx"] = dm_c[0]
    g["silu_c_ctx"] = scc[1:2]
    dcc = _mm(dmod, w_mod, tb=True, b_sections=True, name="dcc")
    g["c_ctx"] = _dsilu_mul(cc[:8], dcc[:8], name="dc_ctx")[1]
    g["b_mod"] = dm_x + dm_c
    g["g_pre"] = dgpre_x + dgpre_c
    g["g_post"] = dvec_post[0:1]
    g["gm_ln_g"], g["gm_ln_b"] = dvec_post[2:3], dvec_post[3:4]
    g["g_onorm"] = dgon[0:1]
    g["w_sp"] = dwsp
    g["b_sp"] = jnp.sum(dbspb, axis=-1)
    g["w_conv"] = dwc_x.at[:, D:].add(dwc_c)
    g["a_log"] = (dav_x + dav_c)[0, :2 * H].reshape(2, H)
    g["dt_bias"] = (ddv_x + ddv_c)[0, :2 * H].reshape(2, H)
    return loss_acc[0, 0], grad_x, g


ANY = pl.BlockSpec(memory_space=pl.ANY)


def _place():
    x, y, c = lax.axis_index("x"), lax.axis_index("y"), lax.axis_index("c")
    chips = [(1 - x, y), (x, 1 - y), (1 - x, 1 - y)]
    return x, y, c, (x, y, 1 - c), chips


def _gather_shards(big, small, split, *, name):
    nb = len(big)
    first = [sum(N_CHIPS if sp else 1 for sp in split[:a]) for a in range(nb + 1)]

    def body(*refs):
        ins, sm_in = refs[:nb], refs[nb]
        outs, sm_out = refs[nb + 1:nb + 1 + first[-1]], refs[nb + 1 + first[-1]]
        send, recv, local = refs[nb + 2 + first[-1]:]
        x, y, c, sibling, chips = _place()
        mine = 2 * x + y

        def slot(a, shard):
            return outs[first[a] + shard] if split[a] else outs[first[a]].at[shard]

        def half(a, shard, hc):
            hr = big[a].shape[0] // 2
            return slot(a, shard).at[pl.ds(hc * hr, hr), :]

        def in_slot(a, shard, act):
            if not split[a]:
                act(shard)
                return
            for s in range(N_CHIPS):
                pl.when(shard == s)(functools.partial(act, s))

        def remote(k, src, dst, to):
            return pltpu.make_async_remote_copy(src_ref=src, dst_ref=dst, send_sem=send.at[k], recv_sem=recv.at[k],
                                                device_id=to, device_id_type=MESH)

        def to_chip(a, j, chip, s):
            hr = big[a].shape[0] // 2
            return remote(a * 3 + j, ins[a].at[pl.ds(c * hr, hr), :], half(a, s, c), (*chip, c))

        def own_copy(a, s, rows):
            return pltpu.make_async_copy(ins[a].at[rows, :], slot(a, s).at[rows, :], local.at[a])

        base = nb * 3 + 3

        def pass_on(a, j, s):
            return remote(base + a * 3 + j, half(a, s, c), half(a, s, c), sibling)

        for a in range(nb):
            for j, chip in enumerate(chips):
                in_slot(a, mine, lambda s, a=a, j=j, chip=chip: to_chip(a, j, chip, s).start())
            if split[a]:
                for r0, nr in _row_chunks(big[a].shape[0], big[a].shape[1] * big[a].dtype.itemsize):
                    in_slot(a, mine, lambda s, a=a, r0=r0, nr=nr: own_copy(a, s, pl.ds(r0, nr)).start())
        small_sends = [remote(nb * 3 + j, sm_in, sm_out.at[mine], (*chip, c)) for j, chip in enumerate(chips)]
        for cp in small_sends:
            cp.start()
        for a in range(nb):
            for j, (px, py) in enumerate(chips):
                def landed(s, a=a, j=j):
                    remote(a * 3 + j, half(a, s, c), half(a, s, c), sibling).wait_recv()
                    pass_on(a, j, s).start()
                in_slot(a, 2 * px + py, landed)
        for a in range(nb):
            for j, (px, py) in enumerate(chips):
                in_slot(a, 2 * px + py, lambda s, a=a, j=j: remote(base + a * 3 + j, half(a, s, 1 - c), half(a, s, 1 - c),
                                                                    sibling).wait_recv())
        for j, (px, py) in enumerate(chips):
            remote(nb * 3 + j, sm_in, sm_out.at[2 * px + py], sibling).wait_recv()
        for a in range(nb):
            for j, chip in enumerate(chips):
                in_slot(a, mine, lambda s, a=a, j=j, chip=chip: to_chip(a, j, chip, s).wait_send())
                in_slot(a, 2 * chip[0] + chip[1], lambda s, a=a, j=j: pass_on(a, j, s).wait_send())
            if split[a]:
                in_slot(a, mine, lambda s, a=a: own_copy(a, s, slice(None)).wait())
        for cp in small_sends:
            cp.wait_send()

    n_remote = 2 * nb * 3 + 3
    shapes = [s for a, sp in zip(big, split) for s in ([a.shape] * N_CHIPS if sp else [(N_CHIPS,) + a.shape])]
    dtypes = [a.dtype for a, sp in zip(big, split) for _ in range(N_CHIPS if sp else 1)]
    outs = pl.pallas_call(
        body, name=name, in_specs=[ANY] * (nb + 1), out_specs=[ANY] * (first[-1] + 1),
        out_shape=[jax.ShapeDtypeStruct(s, d) for s, d in zip(shapes, dtypes)]
        + [jax.ShapeDtypeStruct((N_CHIPS,) + small.shape, small.dtype)],
        scratch_shapes=[pltpu.SemaphoreType.DMA((n_remote,)), pltpu.SemaphoreType.DMA((n_remote,)),
                        pltpu.SemaphoreType.DMA((nb,))],
    )(*big, small)
    per_shard = [outs[first[a]:first[a + 1]] if split[a] else outs[first[a]] for a in range(nb)]
    return per_shard, outs[first[-1]]


def _gather_late_start(arrs, after, *, name):
    na = len(arrs)

    def body(*refs):
        ins, land = refs[:na], refs[na:2 * na]
        send, recv = refs[2 * na + 1], refs[2 * na + 2]
        token = refs[-1]
        x, y, c, _, chips = _place()
        for a in range(na):
            hr = arrs[a].shape[0] // 2
            for j, (px, py) in enumerate(chips):
                for other in range(2):
                    k = (a * 3 + j) * 2 + other
                    _remote(ins[a].at[pl.ds(c * hr, hr), :], land[a].at[2 * x + y, pl.ds(c * hr, hr), :],
                            send.at[k], recv.at[k], (px, py, c ^ other)).start()
        token[...] = jnp.zeros_like(token)

    lands = [((N_CHIPS,) + a.shape, a.dtype) for a in arrs]
    held = lambda a: pltpu.with_memory_space_constraint(a, pltpu.HBM)
    outs = pl.pallas_call(
        body, name=name, in_specs=[HBM] * (2 * na) + [ANY],
        out_specs=[SEM, SEM] + [HBM] * (2 * na) + [pl.BlockSpec(memory_space=pltpu.VMEM)],
        out_shape=[pltpu.SemaphoreType.DMA((6 * na,)), pltpu.SemaphoreType.DMA((6 * na,))]
        + [pltpu.HBM(a.shape, a.dtype) for a in arrs] + [pltpu.HBM(s, d) for s, d in lands]
        + [jax.ShapeDtypeStruct((8, 128), F32)],
        input_output_aliases={i: 2 + i for i in range(2 * na)},
        compiler_params=pltpu.CompilerParams(has_side_effects=pltpu.SideEffectType.DATAFLOW_SIDE_EFFECTING),
    )(*[held(a) for a in arrs], *[held(lax.empty(s, d)) for s, d in lands], after)
    return outs[0], outs[1], outs[2:2 + na], outs[2 + na:2 + 2 * na], outs[-1]


def _gather_late_wait(send, recv, thru, land, after, *, name):
    na = len(thru)

    def body(*refs):
        ins, slots = refs[:na], refs[na:2 * na]
        send_sem, recv_sem = refs[2 * na], refs[2 * na + 1]
        x, y, c, _, chips = _place()
        for a in range(na):
            hr = thru[a].shape[0] // 2
            for j, (px, py) in enumerate(chips):
                for other in range(2):
                    k = (a * 3 + j) * 2 + other
                    half = c ^ other
                    cp = _remote(ins[a].at[pl.ds(c * hr, hr), :], slots[a].at[2 * px + py, pl.ds(half * hr, hr), :],
                                 send_sem.at[k], recv_sem.at[k], (px, py, half))
                    cp.wait_send()
                    cp.wait_recv()

    outs = pl.pallas_call(
        body, name=name, in_specs=[HBM] * (2 * na) + [SEM, SEM, ANY], out_specs=[HBM] * (2 * na),
        out_shape=[pltpu.HBM(a.shape, a.dtype) for a in list(thru) + list(land)],
        input_output_aliases={i: i for i in range(2 * na)},
        compiler_params=pltpu.CompilerParams(has_side_effects=pltpu.SideEffectType.DATAFLOW_SIDE_EFFECTING),
    )(*thru, *land, send, recv, after)
    return outs[na:]


def _row_chunks(rows, row_bytes, align=8):
    n = max(1, min(rows // align, -(-rows * row_bytes // DMA_CHUNK_BYTES)))
    per = -(-(-(-rows // n)) // align) * align
    return [(r, min(per, rows - r)) for r in range(0, rows, per)]


def _remote(src, dst, send, recv, to):
    return pltpu.make_async_remote_copy(src_ref=src, dst_ref=dst, send_sem=send, recv_sem=recv, device_id=to,
                                        device_id_type=MESH)


def _exchange_halves(arrs, *, name):
    na = len(arrs)

    def body(*refs):
        ins, got = refs[:na], refs[na:2 * na]
        send, recv = refs[2 * na:]
        x, y, c, sibling, _ = _place()
        for a in range(na):
            ns, rows, cols = arrs[a].shape
            hr = rows // 2
            for s in range(ns):
                for r0, nr in _row_chunks(hr, cols * arrs[a].dtype.itemsize, 16):
                    _remote(ins[a].at[s, pl.ds((1 - c) * hr + r0, nr), :], got[a].at[s, pl.ds(r0, nr), :],
                            send.at[a], recv.at[a], sibling).start()
        for a in range(na):
            hr = arrs[a].shape[1] // 2
            _remote(ins[a].at[:, pl.ds((1 - c) * hr, hr), :], got[a], send.at[a], recv.at[a], sibling).wait()

    return pl.pallas_call(
        body, name=name, in_specs=[ANY] * na, out_specs=[ANY] * na,
        out_shape=[jax.ShapeDtypeStruct((a.shape[0], a.shape[1] // 2, a.shape[2]), a.dtype) for a in arrs],
        scratch_shapes=[pltpu.SemaphoreType.DMA((na,)), pltpu.SemaphoreType.DMA((na,))],
    )(*arrs)


def _scatter_sections(arrs, *, name):
    na = len(arrs)

    def body(*refs):
        ins, outs = refs[:na], refs[na:2 * na]
        send, recv = refs[2 * na:]
        x, y, c, _, chips = _place()
        for a in range(na):
            _, hr, cols = arrs[a].shape
            for j, (px, py) in enumerate(chips):
                for r0, nr in _row_chunks(hr, cols * arrs[a].dtype.itemsize, 16):
                    _remote(ins[a].at[2 * px + py, pl.ds(r0, nr), :], outs[a].at[j, pl.ds(r0, nr), :],
                            send.at[a * 3 + j], recv.at[a * 3 + j], (px, py, c)).start()
        for a in range(na):
            for j, (px, py) in enumerate(chips):
                _remote(ins[a].at[2 * px + py], outs[a].at[j], send.at[a * 3 + j], recv.at[a * 3 + j], (px, py, c)).wait()

    return pl.pallas_call(
        body, name=name, in_specs=[ANY] * na, out_specs=[ANY] * na,
        out_shape=[jax.ShapeDtypeStruct((N_CHIPS - 1,) + a.shape[1:], a.dtype) for a in arrs],
        scratch_shapes=[pltpu.SemaphoreType.DMA((3 * na,)), pltpu.SemaphoreType.DMA((3 * na,))],
    )(*arrs)


HBM = pl.BlockSpec(memory_space=pltpu.HBM)
SEM = pl.BlockSpec(memory_space=pltpu.SEMAPHORE)


def _scatter_start(arrs, *, name):
    na = len(arrs)

    def body(*refs):
        ins, land = refs[:na], refs[na:2 * na]
        send, recv = refs[2 * na], refs[2 * na + 1]
        token = refs[-1]
        x, y, c, _, chips = _place()
        for a in range(na):
            _, hr, cols = arrs[a].shape
            for j, (px, py) in enumerate(chips):
                for r0, nr in _row_chunks(hr, cols * arrs[a].dtype.itemsize, 16):
                    _remote(ins[a].at[2 * px + py, pl.ds(r0, nr), :], land[a].at[j, pl.ds(r0, nr), :],
                            send.at[a * 3 + j], recv.at[a * 3 + j], (px, py, c)).start()
        token[...] = jnp.zeros_like(token)

    slots = [((N_CHIPS - 1,) + a.shape[1:], a.dtype) for a in arrs]
    held = lambda a: pltpu.with_memory_space_constraint(a, pltpu.HBM)
    outs = pl.pallas_call(
        body, name=name, in_specs=[HBM] * (2 * na),
        out_specs=[SEM, SEM] + [HBM] * (2 * na) + [pl.BlockSpec(memory_space=pltpu.VMEM)],
        out_shape=[pltpu.SemaphoreType.DMA((3 * na,)), pltpu.SemaphoreType.DMA((3 * na,))]
        + [pltpu.HBM(a.shape, a.dtype) for a in arrs] + [pltpu.HBM(s, d) for s, d in slots]
        + [jax.ShapeDtypeStruct((8, 128), F32)],
        input_output_aliases={i: 2 + i for i in range(2 * na)},
        compiler_params=pltpu.CompilerParams(has_side_effects=pltpu.SideEffectType.DATAFLOW_SIDE_EFFECTING),
    )(*[held(a) for a in arrs], *[held(lax.empty(s, d)) for s, d in slots])
    return outs[0], outs[1], outs[2:2 + na], outs[2 + na:2 + 2 * na], outs[-1]


def _scatter_wait(send, recv, thru, land, after, *, name):
    na = len(thru)

    def body(*refs):
        ins, slots = refs[:na], refs[na:2 * na]
        send_sem, recv_sem = refs[2 * na], refs[2 * na + 1]
        x, y, c, _, chips = _place()
        for a in range(na):
            for j, (px, py) in enumerate(chips):
                cp = _remote(ins[a].at[2 * px + py], slots[a].at[j], send_sem.at[a * 3 + j], recv_sem.at[a * 3 + j],
                             (px, py, c))
                cp.wait_send()
                cp.wait_recv()

    outs = pl.pallas_call(
        body, name=name, in_specs=[HBM] * (2 * na) + [SEM, SEM, ANY], out_specs=[HBM] * (2 * na),
        out_shape=[pltpu.HBM(a.shape, a.dtype) for a in list(thru) + list(land)],
        input_output_aliases={i: i for i in range(2 * na)},
        compiler_params=pltpu.CompilerParams(has_side_effects=pltpu.SideEffectType.DATAFLOW_SIDE_EFFECTING),
    )(*thru, *land, send, recv, after)
    return outs[:na], outs[na:]


def _finish_reduce(big, small, *, name):
    nb = len(big)

    def body(*refs):
        outs, sm = refs[nb + 1:2 * nb + 1], refs[2 * nb + 1]
        send, recv = refs[2 * nb + 2:]
        x, y, c, sibling, chips = _place()
        blk = lambda px, py, pc: sm.at[4 * px + 2 * py + pc]
        for a in range(nb):
            _, hr, cols = big[a].shape
            for r0, nr in _row_chunks(hr, cols * 4):
                _remote(outs[a].at[c, pl.ds(r0, nr), :], outs[a].at[c, pl.ds(r0, nr), :], send.at[a], recv.at[a],
                        sibling).start()
        first = [_remote(blk(x, y, c), blk(x, y, c), send.at[nb], recv.at[nb], sibling)]
        first += [_remote(blk(x, y, c), blk(x, y, c), send.at[nb + 1 + j], recv.at[nb + 1 + j], (*chip, c))
                  for j, chip in enumerate(chips)]
        for cp in first:
            cp.start()
        passed = []
        for j, (px, py) in enumerate(chips):
            _remote(blk(px, py, c), blk(px, py, c), send.at[nb + 1 + j], recv.at[nb + 1 + j], sibling).wait_recv()
            fw = _remote(blk(px, py, c), blk(px, py, c), send.at[nb + 4 + j], recv.at[nb + 4 + j], sibling)
            fw.start()
            passed.append(fw)
        for a in range(nb):
            _remote(outs[a].at[c], outs[a].at[1 - c], send.at[a], recv.at[a], sibling).wait()
        _remote(blk(x, y, c), blk(x, y, 1 - c), send.at[nb], recv.at[nb], sibling).wait_recv()
        for j, (px, py) in enumerate(chips):
            _remote(blk(px, py, c), blk(px, py, 1 - c), send.at[nb + 4 + j], recv.at[nb + 4 + j], sibling).wait_recv()
        for cp in first + passed:
            cp.wait_send()

    n_remote = nb + 7
    arrs = list(big) + [small]
    outs = pl.pallas_call(
        body, name=name, in_specs=[ANY] * (nb + 1), out_specs=[ANY] * (nb + 1),
        out_shape=[jax.ShapeDtypeStruct(a.shape, F32) for a in arrs],
        input_output_aliases={i: i for i in range(nb + 1)},
        scratch_shapes=[pltpu.SemaphoreType.DMA((n_remote,)), pltpu.SemaphoreType.DMA((n_remote,))],
    )(*arrs)
    return outs[:nb], outs[nb]


def _reduce_start(sectioned, where):
    got = _exchange_halves([a for parts in sectioned for a in parts], name="rs_exchange_halves")
    first = [sum(len(parts) for parts in sectioned[:i]) for i in range(len(sectioned) + 1)]
    chip_sum = [_add_sibling(parts, got[first[i]:first[i + 1]], where, BF16, name=f"rs_add_sibling_{i}")
                for i, parts in enumerate(sectioned)]
    send, recv, thru, land, token = _scatter_start(chip_sum, name="rs_scatter_start")
    return (send, recv, thru, land), token


def _reduce_finish(handle, small, where, after):
    send, recv, thru, land = handle
    got = _exchange_halves([small], name="rs_exchange_small")
    small_sum = _add_sibling([small], got, where, F32, name="rs_add_sibling_small")
    small_slots = _scatter_sections([small_sum], name="rs_scatter_small")[0]
    own, slots = _scatter_wait(send, recv, thru, land, after, name="rs_scatter_wait")
    red = [_sum_chips(p, s, where, 2, 0, name=f"rs_sum_chips_{i}") for i, (p, s) in enumerate(zip(own, slots))]
    red_small = _sum_chips(small_sum, small_slots, where, 2 * N_CHIPS, 2, name="rs_sum_chips_small")
    big, sm = _finish_reduce(red, red_small, name="rs_finish")
    return big, sm.reshape(-1, sm.shape[-1])


def kernel(x, c, ctx, c_ctx, w_mod, b_mod, g_pre, g_post, w_in, w_conv, a_log, dt_bias, g_onorm, gm_ln_g, gm_ln_b, w_sp, b_sp, w_pa, w_pb, w_out, loss_target, m_c_ctx, m_w_mod, m_b_mod, m_g_pre, m_g_post, m_w_in, m_w_conv, m_a_log, m_dt_bias, m_g_onorm, m_gm_ln_g, m_gm_ln_b, m_w_sp, m_b_sp, m_w_pa, m_w_pb, m_w_out, v_c_ctx, v_w_mod, v_b_mod, v_g_pre, v_g_post, v_w_in, v_w_conv, v_a_log, v_dt_bias, v_g_onorm, v_gm_ln_g, v_gm_ln_b, v_w_sp, v_b_sp, v_w_pa, v_w_pb, v_w_out):
    names = ["c_ctx", "w_mod", "b_mod", "g_pre", "g_post", "w_in", "w_conv", "a_log", "dt_bias", "g_onorm", "gm_ln_g",
             "gm_ln_b", "w_sp", "b_sp", "w_pa", "w_pb", "w_out"]
    w = dict(zip(names, (c_ctx, w_mod, b_mod, g_pre, g_post, w_in, w_conv, a_log, dt_bias, g_onorm, gm_ln_g, gm_ln_b,
                         w_sp, b_sp, w_pa, w_pb, w_out)))
    m = dict(zip(names, (m_c_ctx, m_w_mod, m_b_mod, m_g_pre, m_g_post, m_w_in, m_w_conv, m_a_log, m_dt_bias, m_g_onorm,
                         m_gm_ln_g, m_gm_ln_b, m_w_sp, m_b_sp, m_w_pa, m_w_pb, m_w_out)))
    v = dict(zip(names, (v_c_ctx, v_w_mod, v_b_mod, v_g_pre, v_g_post, v_w_in, v_w_conv, v_a_log, v_dt_bias, v_g_onorm,
                         v_gm_ln_g, v_gm_ln_b, v_w_sp, v_b_sp, v_w_pa, v_w_pb, v_w_out)))
    xy = 2 * lax.axis_index("x") + lax.axis_index("y")
    where = jnp.stack([lax.axis_index("c"), xy, 2 * xy + lax.axis_index("c")]).astype(jnp.int32)

    shards = [a[0].astype(BF16) for a in (w_mod, w_in)]
    (wm_all, win_all), wconv_all = _gather_shards(shards, w_conv[0], (False, True), name="gather_weights")
    own = lambda full, shard: lax.dynamic_update_slice(full, shard[None], (xy, 0, 0))
    wm_all = own(wm_all, shards[0])
    wconv_all = own(wconv_all, w_conv[0])
    late_shards = [a[0].astype(BF16) for a in (w_pa, w_pb, w_out)]
    late = _gather_late_start(late_shards, wconv_all, name="gather_late_start")

    def late_weights(after):
        lands = _gather_late_wait(*late[:4], after, name="gather_late_wait")
        return [own(f, s).reshape(D, D) for f, s in zip(lands, late_shards)]
    w_conv_f = jnp.concatenate([wconv_all[s] for s in range(N_CHIPS)], axis=1)
    shard_cols = IN_COLS // N_CHIPS
    cut = OFF_A - shard_cols
    assert 0 < cut and cut + 4 * H < shard_cols
    w_qkv = jnp.concatenate([win_all[0], win_all[1][:, :cut]], axis=1)
    w_ab = jnp.pad(win_all[1][:, cut:cut + 4 * H], ((0, 0), (0, DH - 4 * H)))
    w_rest = jnp.concatenate([win_all[1][:, cut + 4 * H:], win_all[2], win_all[3]], axis=1)

    blk = D // N_CHIPS
    rest_cut = shard_cols - cut - 4 * H
    big_names = ("w_in", "w_pa", "w_pb", "w_out")
    in_flight = []

    def start_reduce(grads):
        dw_qkv, dw_ab, dw_rest = grads["w_in"]
        g_win = [dw_qkv[:, :shard_cols],
                 jnp.concatenate([dw_qkv[:, shard_cols:], dw_ab, dw_rest[:, :rest_cut]], axis=1),
                 dw_rest[:, rest_cut:rest_cut + shard_cols], dw_rest[:, rest_cut + shard_cols:]]
        sectioned = [[a.astype(BF16)[None] for a in g_win]]
        sectioned += [[grads[k].reshape(N_CHIPS, blk, D)] for k in ("w_pa", "w_pb", "w_out")]
        handle, token = _reduce_start(sectioned, where)
        in_flight.append(handle)
        return token

    loss_local, grad_x, g = _local_step(
        x[0], c, ctx[0], c_ctx, loss_target[0], wm_all, b_mod, g_pre, g_post, w_qkv, w_ab, w_rest,
        w_conv_f, a_log[0], dt_bias[0], g_onorm, gm_ln_g, gm_ln_b, w_sp[0], b_sp[0],
        late_weights, on_large_gradients=start_reduce, run_after=late[4])
    g["loss"] = loss_local
    g["_pad"] = jnp.zeros((SMALL_LAYOUT["_pad"][1],), F32)
    g["mod_factors"] = lax.dynamic_update_slice(jnp.zeros((2 * N_CHIPS, 4 * D), F32),
                                                jnp.concatenate(g["mod_factors"])[None], (where[2], 0))
    tail = jnp.zeros((N_CHIPS * SMALL_ROWS * 128 - sum(s for _, s in SMALL_LAYOUT.values()),), F32)
    flat = jnp.concatenate([g[k].reshape(-1) for k in SMALL_LAYOUT] + [tail])
    reduced, gr_small = _reduce_finish(in_flight[0], flat.reshape(N_CHIPS, SMALL_ROWS, 128), where, g["b_mod"])
    gr_tiny = gr_small[TINY_ROW0:]

    def entry(arr, k):
        off, size = SMALL_LAYOUT[k]
        row, col = off // 128 - TINY_ROW0, off % 128
        return arr[row:row + size // 128].reshape(-1) if size >= 128 else arr[row, col:col + size]

    factors = entry(gr_tiny, "mod_factors").reshape(2 * N_CHIPS, 4 * D)
    pad_rows = lambda a: jnp.pad(a, ((0, 16 - a.shape[0]), (0, 0)))
    lhs = pad_rows(jnp.concatenate([factors[:, :D], g["silu_c_ctx"]], axis=0))
    rhs = pad_rows(jnp.concatenate([factors[:, D:], entry(gr_tiny, "dm_ctx")[None]], axis=0))
    mod_cols = 3 * D // N_CHIPS
    gr_wm = _mm(lhs, lax.dynamic_slice(rhs, (0, xy * mod_cols), (16, mod_cols)), ta=True, name="dw_mod")

    res = {k: _adamw(w[k], gr, m[k], v[k], name=f"adamw_{k}") for k, gr in zip(big_names[1:], reduced[1:])}
    flip = lambda a: jnp.swapaxes(a, 1, 2)
    res["w_in"] = [flip(r) for r in _adamw(flip(w_in), flip(reduced[0].reshape(w_in.shape)), flip(m_w_in), flip(v_w_in),
                                           name="adamw_w_in")]
    res["w_mod"] = _adamw(w_mod[0], gr_wm, m_w_mod[0], v_w_mod[0], name="adamw_w_mod")
    res["w_sp"] = _adamw(w_sp.reshape(-1, 128), gr_small, m_w_sp.reshape(-1, 128), v_w_sp.reshape(-1, 128),
                         name="adamw_w_sp")
    tiny = [k for k, (off, _) in SMALL_LAYOUT.items() if TINY_ROW0 <= off // 128 < TINY_ROW0 + TINY_ROWS]
    tiny_names = [k for k in tiny if k not in ("loss", "_pad")]

    def pack(src):
        parts = [src[k].reshape(-1) if k in tiny_names else jnp.zeros((SMALL_LAYOUT[k][1],), F32) for k in tiny]
        return jnp.concatenate(parts).reshape(TINY_ROWS, 128)

    tiny_res = _adamw(pack(w), gr_tiny[:TINY_ROWS], pack(m), pack(v), name="adamw_tiny")
    for k in tiny_names:
        res[k] = [entry(r, k) for r in tiny_res]
    g_conv = lax.dynamic_slice(entry(gr_tiny, "w_conv").reshape(3, 3 * D), (0, xy * (3 * D // N_CHIPS)),
                               (3, 3 * D // N_CHIPS))
    conv_res = _adamw(jnp.pad(w_conv[0], ((0, 5), (0, 0))), jnp.pad(g_conv, ((0, 5), (0, 0))),
                      jnp.pad(m_w_conv[0], ((0, 5), (0, 0))), jnp.pad(v_w_conv[0], ((0, 5), (0, 0))), name="adamw_w_conv")
    res["w_conv"] = [r[:3] for r in conv_res]
    res = {k: [r.reshape(w[k].shape) for r in res[k]] for k in names}

    out = [entry(gr_tiny, "loss").reshape(()), grad_x[None]]
    for i in range(4):
        out += [res[k][i] for k in names]
    return tuple(out)
```

```python
import functools

import jax
import jax.numpy as jnp
from jax import lax
from jax.experimental import pallas as pl
from jax.experimental.pallas import tpu as pltpu

F32 = jnp.float32
BF16 = jnp.bfloat16
HI = lax.Precision.HIGHEST
MESH = pl.DeviceIdType.MESH

D = 1024
H = 8
DH = 128
CH = 64
LOG_CH = 6
PAIR = 2 * CH
GM = 128
assert 1 << LOG_CH == CH and PAIR == DH
PREC_POWERS = ((lax.Precision.HIGH, lax.Precision.HIGH),) * 3 + ((None, None),) * 2
assert len(PREC_POWERS) == LOG_CH - 1
EPS = 1e-6
N_CHIPS = 4
OFF_A = 3 * D
OFF_ZB = OFF_A + 4 * H
IN_COLS = OFF_ZB + 6 * D
VMEM_LIMIT_V7X = 56 * 1024 * 1024
DMA_CHUNK_BYTES = 2 * 1024 * 1024

ADAM_LR, ADAM_B1, ADAM_B2, ADAM_EPS, ADAM_WD, ADAM_STEP = 0.001, 0.9, 0.999, 1e-08, 0.01, 10

SMALL_LAYOUT = {}
_off = 0
for _n, _s in (("w_sp", H * GM * GM), ("c_ctx", D), ("b_mod", 3 * D), ("g_pre", D), ("g_post", D), ("gm_ln_g", D),
               ("gm_ln_b", D), ("b_sp", H * GM), ("g_onorm", DH), ("a_log", 2 * H), ("dt_bias", 2 * H), ("loss", 1),
               ("_pad", 128 - 4 * H - 1 + 6 * 128), ("w_conv", 3 * 3 * D), ("dm_ctx", 3 * D), ("mod_factors", 8 * 4 * D)):
    SMALL_LAYOUT[_n] = (_off, _s)
    _off += _s
SMALL_ROWS = 368
assert N_CHIPS * SMALL_ROWS * 128 >= _off and (SMALL_ROWS // 2) % 8 == 0 and _off % 128 == 0
TINY_ROW0 = SMALL_LAYOUT["c_ctx"][0] // 128
TINY_ROWS = SMALL_LAYOUT["w_conv"][0] // 128 - TINY_ROW0
assert TINY_ROWS % 8 == 0


def _params(sem=None):
    return pltpu.CompilerParams(dimension_semantics=sem, vmem_limit_bytes=VMEM_LIMIT_V7X)


def _tile(n, cands=(256, 128, 64, 32, 16, 8)):
    if n <= cands[0]:
        return n
    for cand in cands:
        if n % cand == 0 and cand >= 64:
            return cand
    return max(d for d in range(8, cands[0], 8) if n % d == 0)


def _silu(x):
    return x * jax.nn.sigmoid(x)


def _gelu(x):
    return 0.5 * x * (1.0 + jnp.tanh(0.7978845608028654 * (x + 0.044715 * (x * x * x))))


def _mm(a, b, *, ta=False, tb=False, out_dtype=F32, tm=1024, tn=1024, tk=1024, add=None, add_from=0,
        b_sections=False, out_sections=False, after=None, name):
    m, k = (a.shape[1], a.shape[0]) if ta else a.shape
    if b_sections:
        sect = b.shape[2]
        n = b.shape[1] if tb else b.shape[0] * sect
        tn, tk = (tn, sect) if tb else (sect, tk)
    else:
        n = b.shape[0] if tb else b.shape[1]
    tm, tn, tk = min(tm, m), min(tn, n), min(tk, k)
    assert m % tm == 0 and n % tn == 0 and k % tk == 0, (name, m, n, k, tm, tn, tk)
    nk = k // tk
    dims = (((0,) if ta else (1,), (1,) if tb else (0,)), ((), ()))
    has_add = add is not None
    assert not has_add or add.shape == (m, n - add_from * tn), (name, add.shape)

    def body(*refs):
        a_ref, b_ref = refs[:2]
        o_ref = refs[2 + int(has_add) + int(after is not None)]
        acc_ref = refs[-1]
        kk = pl.program_id(2)
        bv = b_ref[0] if b_sections else b_ref[...]
        part = lax.dot_general(a_ref[...].astype(BF16), bv.astype(BF16), dims, preferred_element_type=F32)

        def finish(res):
            if has_add:
                res = res + jnp.where(pl.program_id(1) >= add_from, refs[2][...], 0.0)
            if out_sections:
                o_ref[0] = res.astype(out_dtype)
            else:
                o_ref[...] = res.astype(out_dtype)

        if nk == 1:
            finish(part)
            return

        @pl.when(kk == 0)
        def _():
            acc_ref[...] = part

        @pl.when((kk > 0) & (kk < nk - 1))
        def _():
            acc_ref[...] += part

        @pl.when(kk == nk - 1)
        def _():
            finish(acc_ref[...] + part)

    a_spec = pl.BlockSpec((tk, tm), lambda i, j, q: (q, i)) if ta else pl.BlockSpec((tm, tk), lambda i, j, q: (i, q))
    if b_sections:
        b_spec = (pl.BlockSpec((1, tn, tk), lambda i, j, q: (q, j, 0)) if tb
                  else pl.BlockSpec((1, tk, tn), lambda i, j, q: (j, q, 0)))
    else:
        b_spec = pl.BlockSpec((tn, tk), lambda i, j, q: (j, q)) if tb else pl.BlockSpec((tk, tn), lambda i, j, q: (q, j))
    add_spec = [pl.BlockSpec((tm, tn), lambda i, j, q: (i, jnp.maximum(j - add_from, 0)))] if has_add else []
    if out_sections:
        out_spec, out_shape = pl.BlockSpec((1, tm, tn), lambda i, j, q: (j, i, 0)), (n // tn, m, tn)
    else:
        out_spec, out_shape = pl.BlockSpec((tm, tn), lambda i, j, q: (i, j)), (m, n)
    return pl.pallas_call(
        body, name=name, grid=(m // tm, n // tn, nk),
        in_specs=[a_spec, b_spec] + add_spec + ([pl.BlockSpec(memory_space=pl.ANY)] if after is not None else []),
        out_specs=out_spec, out_shape=jax.ShapeDtypeStruct(out_shape, out_dtype),
        scratch_shapes=[pltpu.VMEM((tm, tn), F32)] if nk > 1 else [],
        compiler_params=_params(("parallel", "parallel", "arbitrary")),
    )(*([a, b] + ([add] if has_add else []) + ([after] if after is not None else [])))


def _h_fn(x, g, m):
    shift, scale = m[:, 0:D], m[:, D:2 * D]
    r = lax.rsqrt(jnp.mean(x * x, axis=-1, keepdims=True) + EPS)
    return (x * r * g) * (1.0 + scale) + shift


def _norm_fwd(x, g, mod, bmod, *, name):
    rows = x.shape[0]
    t = min(512, rows)

    def body(x_ref, g_ref, m_ref, b_ref, h_ref):
        h_ref[...] = _h_fn(x_ref[...], g_ref[...], m_ref[...] + b_ref[...]).astype(BF16)

    vec = lambda w: pl.BlockSpec((1, w), lambda i: (0, 0))
    return pl.pallas_call(
        body, name=name, grid=(rows // t,),
        in_specs=[pl.BlockSpec((t, D), lambda i: (i, 0)), vec(D), vec(3 * D), vec(3 * D)],
        out_specs=pl.BlockSpec((t, D), lambda i: (i, 0)),
        out_shape=jax.ShapeDtypeStruct((rows, D), BF16),
        compiler_params=_params(("parallel",)),
    )(x, g, mod, bmod)


def _norm_bwd(x, g, mod, bmod, dh_parts, resid, *, name):
    rows = x.shape[0]
    t = min(512, rows)
    n_parts = len(dh_parts)
    has_resid = resid is not None

    def body(*refs):
        x_ref, g_ref, m_ref, b_ref = refs[:4]
        parts = refs[4:4 + n_parts]
        r_ref = refs[4 + n_parts] if has_resid else None
        dx_ref, dg_ref, dm_ref = refs[-3:]
        i = pl.program_id(0)
        dh = parts[0][...]
        for p in parts[1:]:
            dh = dh + p[...]
        _, vjp = jax.vjp(_h_fn, x_ref[...], g_ref[...], m_ref[...] + b_ref[...])
        dx, dg, dm = vjp(dh)
        if has_resid:
            dx = dx + r_ref[...]
        dx_ref[...] = dx

        @pl.when(i == 0)
        def _():
            dg_ref[...] = dg
            dm_ref[...] = dm

        @pl.when(i > 0)
        def _():
            dg_ref[...] += dg
            dm_ref[...] += dm

    vec = lambda w: pl.BlockSpec((1, w), lambda i: (0, 0))
    tile = pl.BlockSpec((t, D), lambda i: (i, 0))
    ins = [x, g, mod, bmod, *dh_parts] + ([resid] if has_resid else [])
    return pl.pallas_call(
        body, name=name, grid=(rows // t,),
        in_specs=[tile, vec(D), vec(3 * D), vec(3 * D)] + [tile] * (n_parts + int(has_resid)),
        out_specs=[tile, vec(D), vec(3 * D)],
        out_shape=[jax.ShapeDtypeStruct((rows, D), F32), jax.ShapeDtypeStruct((1, D), F32),
                   jax.ShapeDtypeStruct((1, 3 * D), F32)],
        compiler_params=_params(("arbitrary",)),
    )(*ins)


def _conv_tile(u_ref, r0, t, rows, w0, w1, w2):
    u = u_ref[pl.ds(r0, t), :]
    prev8 = u_ref[pl.ds(pl.multiple_of(jnp.maximum(r0 - 8, 0), 8), 8), :]
    next8 = u_ref[pl.ds(pl.multiple_of(jnp.minimum(r0 + t, rows - 8), 8), 8), :]
    r8 = lax.broadcasted_iota(jnp.int32, (8, DH), 0)
    prev_row = jnp.sum(jnp.where(r8 == 7, prev8, 0.0), axis=0, keepdims=True)
    next_row = jnp.sum(jnp.where(r8 == 0, next8, 0.0), axis=0, keepdims=True)
    prev_row = jnp.where(r0 > 0, prev_row, 0.0)
    next_row = jnp.where(r0 + t < rows, next_row, 0.0)
    ri = lax.broadcasted_iota(jnp.int32, (t, DH), 0)
    um1 = jnp.where(ri == 0, prev_row, pltpu.roll(u, 1, 0))
    up1 = jnp.where(ri == t - 1, next_row, pltpu.roll(u, t - 1, 0))
    return w0 * um1 + w1 * u + w2 * up1, um1, u, up1


def _rowlocal(z, norm):
    y = _silu(z)
    return y * lax.rsqrt(jnp.sum(y * y, axis=-1, keepdims=True) + EPS) if norm else y


def _prep_fwd(p, wconv, n_norm, *, name):
    rows, nb = p.shape[0], p.shape[1] // DH
    t = min(512, rows)
    out = None
    for norm, b0, b1 in ((True, 0, n_norm), (False, n_norm, nb)):
        def body(u_ref, w_ref, *rest, norm=norm):
            o_ref = rest[-1]
            w0, w1, w2 = w_ref[0:1, :], w_ref[1:2, :], w_ref[2:3, :]

            def step(s, carry):
                r0 = pl.multiple_of(s * t, t)
                z, _, _, _ = _conv_tile(u_ref, r0, t, rows, w0, w1, w2)
                o_ref[0, pl.ds(r0, t), :] = _rowlocal(z, norm)
                return carry

            lax.fori_loop(0, rows // t, step, 0)

        held = [] if out is None else [out]
        out = pl.pallas_call(
            body, name=f"{name}_{'norm' if norm else 'plain'}", grid=(b1 - b0,),
            in_specs=[pl.BlockSpec((rows, DH), lambda j, b0=b0: (0, b0 + j)),
                      pl.BlockSpec((3, DH), lambda j, b0=b0: (0, b0 + j))] + [ANY] * len(held),
            out_specs=pl.BlockSpec((1, rows, DH), lambda j, b0=b0: (b0 + j, 0, 0)),
            out_shape=jax.ShapeDtypeStruct((nb, rows, DH), F32),
            input_output_aliases={2: 0} if held else {},
            compiler_params=_params(("parallel",)),
        )(p, wconv, *held)
    return out


def _prep_bwd(p, wconv, d_a, d_b, n_norm, *, name):
    rows, nb = p.shape[0], p.shape[1] // DH
    t = min(512, rows)
    outs = []
    for norm, b0, b1 in ((True, 0, n_norm), (False, n_norm, nb)):
        def body(u_ref, w_ref, da_ref, db_ref, *rest, norm=norm):
            du_ref, dw_ref, dz_ref = rest[-3:]
            w0, w1, w2 = w_ref[0:1, :], w_ref[1:2, :], w_ref[2:3, :]

            def step1(s, carry):
                a0, a1, a2 = carry
                r0 = pl.multiple_of(s * t, t)
                z, um1, u, up1 = _conv_tile(u_ref, r0, t, rows, w0, w1, w2)
                _, vjp = jax.vjp(lambda zz: _rowlocal(zz, norm), z)
                (dz,) = vjp(da_ref[0, pl.ds(r0, t), :] + db_ref[0, pl.ds(r0, t), :])
                dz_ref[pl.ds(r0, t), :] = dz
                red = lambda v: jnp.sum(v, axis=0, keepdims=True)
                return a0 + red(dz * um1), a1 + red(dz * u), a2 + red(dz * up1)

            zero = jnp.zeros((1, DH), F32)
            a0, a1, a2 = lax.fori_loop(0, rows // t, step1, (zero, zero, zero))
            dw_ref[0:1, :] = a0
            dw_ref[1:2, :] = a1
            dw_ref[2:3, :] = a2

            def step2(s, carry):
                r0 = pl.multiple_of(s * t, t)
                du, _, _, _ = _conv_tile(dz_ref, r0, t, rows, w2, w1, w0)
                du_ref[pl.ds(r0, t), :] = du.astype(BF16)
                return carry

            lax.fori_loop(0, rows // t, step2, 0)

        col = pl.BlockSpec((rows, DH), lambda j, b0=b0: (0, b0 + j))
        w_spec = pl.BlockSpec((3, DH), lambda j, b0=b0: (0, b0 + j))
        d_spec = pl.BlockSpec((1, rows, DH), lambda j, b0=b0: (b0 + j, 0, 0))
        outs = pl.pallas_call(
            body, name=f"{name}_{'norm' if norm else 'plain'}", grid=(b1 - b0,),
            in_specs=[col, w_spec, d_spec, d_spec] + [ANY] * len(outs), out_specs=[col, w_spec],
            out_shape=[jax.ShapeDtypeStruct((rows, nb * DH), BF16), jax.ShapeDtypeStruct((3, nb * DH), F32)],
            input_output_aliases={4: 0, 5: 1} if outs else {},
            scratch_shapes=[pltpu.VMEM((rows, DH), F32)],
            compiler_params=_params(("parallel",)),
        )(p, wconv, d_a, d_b, *outs)
    return outs


def _gates_fn(pab, avec, dvec):
    t = pab.shape[0]
    lane = lax.broadcasted_iota(jnp.int32, pab.shape, 1)
    xg = pab + dvec
    sp = jnp.maximum(xg, 0.0) + jnp.log(1.0 + jnp.exp(-jnp.abs(xg)))
    g = jnp.where(lane < 2 * H, -jnp.exp(avec) * sp, 0.0)
    ii = lax.broadcasted_iota(jnp.int32, (t, t), 0)
    jj = lax.broadcasted_iota(jnp.int32, (t, t), 1)
    same = (ii >> LOG_CH) == (jj >> LOG_CH)
    cum_f = _dot(jnp.where(same & (jj <= ii), 1.0, 0.0), g, precision=HI)
    cum_r = _dot(jnp.where(same & (jj >= ii), 1.0, 0.0), g, precision=HI)
    return jnp.where(lane < H, cum_f, jnp.where(lane < 2 * H, cum_r, jnp.where(lane < 4 * H, jax.nn.sigmoid(pab), 0.0)))


def _gates_fwd(pab, avec, dvec, *, name):
    rows = pab.shape[0]
    t = min(2 * PAIR, rows)

    def body(p_ref, a_ref, d_ref, o_ref):
        o_ref[...] = _gates_fn(p_ref[...], a_ref[...], d_ref[...])

    vec = pl.BlockSpec((1, DH), lambda i: (0, 0))
    tile = pl.BlockSpec((t, DH), lambda i: (i, 0))
    return pl.pallas_call(
        body, name=name, grid=(rows // t,), in_specs=[tile, vec, vec], out_specs=tile,
        out_shape=jax.ShapeDtypeStruct((rows, DH), F32), compiler_params=_params(("parallel",)),
    )(pab, avec, dvec)


def _gates_bwd(pab, avec, dvec, d_a, d_b, *, name):
    rows = pab.shape[0]
    t = min(2 * PAIR, rows)

    def body(p_ref, a_ref, d_ref, da_ref, db_ref, dp_ref, dav_ref, ddv_ref):
        i = pl.program_id(0)
        _, vjp = jax.vjp(_gates_fn, p_ref[...], a_ref[...], d_ref[...])
        dp, dav, ddv = vjp(da_ref[...] + db_ref[...])
        dp_ref[...] = dp.astype(BF16)

        @pl.when(i == 0)
        def _():
            dav_ref[...] = dav
            ddv_ref[...] = ddv

        @pl.when(i > 0)
        def _():
            dav_ref[...] += dav
            ddv_ref[...] += ddv

    vec = pl.BlockSpec((1, DH), lambda i: (0, 0))
    tile = pl.BlockSpec((t, DH), lambda i: (i, 0))
    return pl.pallas_call(
        body, name=name, grid=(rows // t,), in_specs=[tile, vec, vec, tile, tile], out_specs=[tile, vec, vec],
        out_shape=[jax.ShapeDtypeStruct((rows, DH), BF16), jax.ShapeDtypeStruct((1, DH), F32),
                   jax.ShapeDtypeStruct((1, DH), F32)],
        compiler_params=_params(("arbitrary",)),
    )(pab, avec, dvec, d_a, d_b)


def _dot_general(a, b, dims, precision=None):
    return lax.dot_general(a, b, (dims, ((), ())), precision=precision, preferred_element_type=F32)


@functools.partial(jax.custom_vjp, nondiff_argnums=(2,))
def _dot_bf16(a, b, dims):
    return _dot_general(a.astype(BF16), b.astype(BF16), dims)


def _dot_bf16_fwd(a, b, dims):
    return _dot_bf16(a, b, dims), (a, b)


def _dot_bf16_bwd(dims, res, g):
    a, b = res
    (ca,), (cb,) = dims
    da = _dot_bf16(g, b, ((1,), (1 - cb,))) if ca == 1 else _dot_bf16(b, g, ((1 - cb,), (1,)))
    db = _dot_bf16(a, g, ((1 - ca,), (0,))) if cb == 0 else _dot_bf16(g, a, ((0,), (1 - ca,)))
    return da, db


_dot_bf16.defvjp(_dot_bf16_fwd, _dot_bf16_bwd)


def _dot(a, b, dims=((1,), (0,)), precision=None):
    if precision is None and a.dtype == F32 and b.dtype == F32:
        return _dot_bf16(a, b, dims)
    return _dot_general(a, b, dims, precision)


_NT = ((1,), (1,))
_TN = ((0,), (0,))


@jax.custom_vjp
def _saved_inverse(neg_a, inv):
    return inv


def _saved_inverse_fwd(neg_a, inv):
    return inv, inv


def _saved_inverse_bwd(inv, d_inv):
    idx = range(len(inv))
    left = [_dot(inv[i], d_inv[i], _TN) for i in idx]
    d_neg_a = [_dot(left[i], inv[i], _NT) for i in idx]
    return d_neg_a, [jnp.zeros_like(t) for t in inv]


_saved_inverse.defvjp(_saved_inverse_fwd, _saved_inverse_bwd)


def _pairs(s, q, k, v, gcol, bcol, revs, inv_saved=None):
    idx = range(len(revs))
    ii = lax.broadcasted_iota(jnp.int32, (PAIR, PAIR), 0)
    jj = lax.broadcasted_iota(jnp.int32, (PAIR, PAIR), 1)
    same = (ii >> LOG_CH) == (jj >> LOG_CH)
    incl_d = (same & (ii >= jj), same & (ii <= jj))
    strict_d = (same & (ii > jj), same & (ii < jj))
    incl = [incl_d[int(r)] for r in revs]
    strict = [strict_d[int(r)] for r in revs]
    eye = jnp.where(ii == jj, 1.0, 0.0)
    gc_i = [jnp.broadcast_to(gcol[i], (PAIR, DH)) for i in idx]
    gc_j = [gc_i[i].T for i in idx]
    decay = [jnp.where(incl[i], jnp.exp(jnp.where(incl[i], gc_i[i] - gc_j[i], 0.0)), 0.0) for i in idx]
    b_b = [jnp.broadcast_to(bcol[i], (PAIR, DH)) for i in idx]
    kb = [k[i] * b_b[i] for i in idx]
    kk = [_dot(kb[i], k[i], _NT) for i in idx]
    bp = [jnp.where(strict[i], -kk[i] * decay[i], 0.0) for i in idx]
    if inv_saved is not None:
        inv = _saved_inverse(bp, inv_saved)
    else:
        low = bp
        for prec_sq, prec_acc in PREC_POWERS:
            bp = [_dot(bp[i], bp[i], precision=prec_sq) for i in idx]
            more = [_dot(low[i], bp[i], precision=prec_acc) for i in idx]
            low = [low[i] + bp[i] + more[i] for i in idx]
        inv = [eye + low[i] for i in idx]
    eg = [jnp.exp(gc_i[i]) for i in idx]
    sol = [_dot(inv[i], jnp.concatenate([v[i] * b_b[i], kb[i] * eg[i]], axis=1)) for i in idx]
    u_val = [sol[i][:, :DH] for i in idx]
    w_key = [sol[i][:, DH:] for i in idx]
    row = lax.broadcasted_iota(jnp.int32, (PAIR, 1), 0)
    has_q = q[0] is not None
    if has_q:
        qc = [q[i] * (DH ** -0.5) for i in idx]
        qk = [_dot(qc[i], k[i], _NT) for i in idx]
        attn = [qk[i] * decay[i] for i in idx]
        qd = [qc[i] * eg[i] for i in idx]
    outs = [[None, None] for _ in idx]
    zeros = jnp.zeros((CH, DH), F32)
    for step in range(2):
        cidx = [(1 - step) if revs[i] else step for i in idx]
        sl = [slice(c * CH, (c + 1) * CH) for c in cidx]
        last = [c * CH if revs[i] else c * CH + CH - 1 for i, c in zip(idx, cidx)]
        gl = [jnp.sum(jnp.where(row == last[i], gcol[i], 0.0), axis=0, keepdims=True) for i in idx]
        k_tail = [k[i][sl[i]] * jnp.exp(gl[i] - gc_i[i][sl[i]]) for i in idx]
        ws = [_dot(w_key[i][sl[i]], s[i]) for i in idx]
        v_new = [u_val[i][sl[i]] - ws[i] for i in idx]
        if has_q:
            v_pad = [jnp.concatenate([v_new[i], zeros] if cidx[i] == 0 else [zeros, v_new[i]], axis=0) for i in idx]
            o_state = [_dot(qd[i][sl[i]], s[i]) for i in idx]
            o_local = [_dot(attn[i][sl[i]], v_pad[i]) for i in idx]
            for i in idx:
                outs[i][cidx[i]] = o_state[i] + o_local[i]
        kv = [_dot(k_tail[i], v_new[i], _TN) for i in idx]
        s = [s[i] * jnp.exp(gl[i]) + kv[i] for i in idx]
    return s, ([jnp.concatenate(outs[i], axis=0) for i in idx] if has_q else None), inv


def _lane_col(tile, idx):
    lane = lax.broadcasted_iota(jnp.int32, tile.shape, 1)
    return jnp.sum(jnp.where(lane == idx, tile, 0.0), axis=1, keepdims=True)


def _gdn_fwd(qkv, gb, s0f, s0b, has_q, *, name):
    nb, rows, _ = qkv.shape
    n = rows // PAIR
    qoff = H if has_q else 0

    def body(qf_ref, qb_ref, gf_ref, gr_ref, s0f_ref, s0b_ref, of_ref, ob_ref, ssf_ref, ssb_ref, tsf_ref, tsb_ref,
             sf_ref, sb_ref):
        @pl.when(pl.program_id(0) == 0)
        def _():
            sf_ref[...] = s0f_ref[...]
            sb_ref[...] = s0b_ref[...]

        gtiles = (gf_ref[...], gr_ref[...])

        dirs = ((qf_ref, sf_ref, ssf_ref, of_ref), (qb_ref, sb_ref, ssb_ref, ob_ref))
        ts_refs = (tsf_ref, tsb_ref)

        work = [(h, d) for h in range(H) for d in range(2)]
        loaded = []
        for h, d in work:
            q_ref, s_ref, _, _ = dirs[d]
            loaded.append((s_ref[h], q_ref[h] if has_q else None, q_ref[qoff + h], q_ref[qoff + H + h],
                           _lane_col(gtiles[d], d * H + h), _lane_col(gtiles[d], 2 * H + d * H + h)))
        s_new, o, inv = _pairs(*[list(col) for col in zip(*loaded)], revs=[d == 1 for _, d in work])
        for i, (h, d) in enumerate(work):
            _, s_ref, ss_ref, o_ref = dirs[d]
            ss_ref[0, h] = loaded[i][0]
            ts_refs[d][0, h] = inv[i]
            s_ref[h] = s_new[i]
            o_ref[h] = o[i] if has_q else jnp.zeros((PAIR, DH), F32)

    fwd3 = lambda i: (0, i, 0)
    rev3 = lambda i: (0, n - 1 - i, 0)
    state = pl.BlockSpec((H, DH, DH), lambda i: (0, 0, 0))
    saved = pl.BlockSpec((1, H, DH, DH), lambda i: (i, 0, 0, 0))
    return pl.pallas_call(
        body, name=name, grid=(n,),
        in_specs=[pl.BlockSpec((nb, PAIR, DH), fwd3), pl.BlockSpec((nb, PAIR, DH), rev3),
                  pl.BlockSpec((PAIR, DH), lambda i: (i, 0)), pl.BlockSpec((PAIR, DH), lambda i: (n - 1 - i, 0)),
                  state, state],
        out_specs=[pl.BlockSpec((H, PAIR, DH), fwd3), pl.BlockSpec((H, PAIR, DH), rev3), saved, saved, saved, saved,
                   state, state],
        out_shape=[jax.ShapeDtypeStruct((H, rows, DH), F32)] * 2 + [jax.ShapeDtypeStruct((n, H, DH, DH), F32)] * 4
        + [jax.ShapeDtypeStruct((H, DH, DH), F32)] * 2,
        compiler_params=_params(("arbitrary",)),
    )(qkv, qkv, gb, gb, s0f, s0b)


def _gdn_bwd(qkv, gb, ssf, ssb, tsf, tsb, do, dsf, dsb, has_q, *, name):
    nb, rows, _ = qkv.shape
    n = rows // PAIR
    qoff = H if has_q else 0

    def body(qf_ref, qb_ref, gf_ref, gr_ref, ssf_ref, ssb_ref, tsf_ref, tsb_ref, dof_ref, dob_ref, dsf0_ref, dsb0_ref,
             dqf_ref, dqb_ref, dgf_ref, dgr_ref, dsf_ref, dsb_ref):
        ts_refs = (tsf_ref, tsb_ref)
        @pl.when(pl.program_id(0) == 0)
        def _():
            dsf_ref[...] = dsf0_ref[...]
            dsb_ref[...] = dsb0_ref[...]

        gtiles = (gf_ref[...], gr_ref[...])
        lane = lax.broadcasted_iota(jnp.int32, (PAIR, DH), 1)

        dirs = ((qf_ref, ssf_ref, dof_ref, dsf_ref, dqf_ref), (qb_ref, ssb_ref, dob_ref, dsb_ref, dqb_ref))

        work = [(h, d) for h in range(H) for d in range(2)]
        revs = [d == 1 for _, d in work]
        s_in, q_in, k_in, v_in, g_in, b_in, ds_out, do_out, inv_in = [], [], [], [], [], [], [], [], []
        for h, d in work:
            q_ref, ss_ref, do_ref, ds_ref, _ = dirs[d]
            s_in.append(ss_ref[0, h])
            inv_in.append(ts_refs[d][0, h])
            q_in.append(q_ref[h] if has_q else None)
            k_in.append(q_ref[qoff + h])
            v_in.append(q_ref[qoff + H + h])
            g_in.append(_lane_col(gtiles[d], d * H + h))
            b_in.append(_lane_col(gtiles[d], 2 * H + d * H + h))
            ds_out.append(ds_ref[h])
            do_out.append(do_ref[h] if has_q else None)
        if has_q:
            _, vjp = jax.vjp(lambda s_, q_, k_, v_, g_, b_: _pairs(s_, q_, k_, v_, g_, b_, revs, inv_in)[:2],
                             s_in, q_in, k_in, v_in, g_in, b_in)
            ds, dq, dk, dv, dg, db = vjp((ds_out, do_out))
        else:
            _, vjp = jax.vjp(lambda s_, k_, v_, g_, b_: _pairs(s_, q_in, k_, v_, g_, b_, revs, inv_in)[0],
                             s_in, k_in, v_in, g_in, b_in)
            ds, dk, dv, dg, db = vjp(ds_out)
        dgb = [jnp.zeros((PAIR, DH), F32)] * 2
        for i, (h, d) in enumerate(work):
            dq_ref, ds_ref = dirs[d][4], dirs[d][3]
            ds_ref[h] = ds[i]
            if has_q:
                dq_ref[h] = dq[i]
            dq_ref[qoff + h], dq_ref[qoff + H + h] = dk[i], dv[i]
            dgb[d] = (dgb[d] + jnp.where(lane == d * H + h, dg[i], 0.0)
                      + jnp.where(lane == 2 * H + d * H + h, db[i], 0.0))
        dgf_ref[...] = dgb[0]
        dgr_ref[...] = dgb[1]

    fwd3 = lambda i: (0, n - 1 - i, 0)
    rev3 = lambda i: (0, i, 0)
    state = pl.BlockSpec((H, DH, DH), lambda i: (0, 0, 0))
    saved = pl.BlockSpec((1, H, DH, DH), lambda i: (n - 1 - i, 0, 0, 0))
    gf_spec = pl.BlockSpec((PAIR, DH), lambda i: (n - 1 - i, 0))
    gr_spec = pl.BlockSpec((PAIR, DH), lambda i: (i, 0))
    return pl.pallas_call(
        body, name=name, grid=(n,),
        in_specs=[pl.BlockSpec((nb, PAIR, DH), fwd3), pl.BlockSpec((nb, PAIR, DH), rev3), gf_spec, gr_spec,
                  saved, saved, saved, saved,
                  pl.BlockSpec((H, PAIR, DH), fwd3), pl.BlockSpec((H, PAIR, DH), rev3), state, state],
        out_specs=[pl.BlockSpec((nb, PAIR, DH), fwd3), pl.BlockSpec((nb, PAIR, DH), rev3), gf_spec, gr_spec, state, state],
        out_shape=[jax.ShapeDtypeStruct((nb, rows, DH), F32)] * 2 + [jax.ShapeDtypeStruct((rows, DH), F32)] * 2
        + [jax.ShapeDtypeStruct((H, DH, DH), F32)] * 2,
        compiler_params=_params(("arbitrary",)),
    )(qkv, qkv, gb, gb, ssf, ssb, tsf, tsb, do, do, dsf, dsb)


def _stage1(zb, ua, va, za, o, gon, lng, lnb, wsp, bsp):
    gv = [_gelu(t) for t in va]
    mu = sum(jnp.sum(t, axis=-1, keepdims=True) for t in gv) * (1.0 / D)
    xc = [t - mu for t in gv]
    var = sum(jnp.sum(t * t, axis=-1, keepdims=True) for t in xc) * (1.0 / D)
    rs = lax.rsqrt(var + EPS)
    ya, yb = [], []
    for g in range(H):
        vv = xc[g] * rs * lng[g] + lnb[g]
        s = _dot(wsp[g], vv) + bsp[g]
        ya.append(_gelu(ua[g]) * s * _silu(za[g]))
        r = lax.rsqrt(jnp.mean(o[g] * o[g], axis=-1, keepdims=True) + EPS)
        yb.append(o[g] * r * gon * _silu(zb[g]))
    return ya, yb


def _stage2(ma, mb, ga, gb):
    return jax.nn.sigmoid(ga) * ma + jax.nn.sigmoid(gb) * mb


def _stage3(out, x, tgt, gpost, gate):
    r = out * lax.rsqrt(jnp.mean(out * out, axis=-1, keepdims=True) + EPS) * gpost
    err = x + gate * r - tgt
    return 0.5 * jnp.sum(jnp.mean(err * err, axis=-1, keepdims=True), axis=0, keepdims=True)


def _post(p_rest, o_f, o_b, x, tgt, mod, bmod, gon, lng, lnb, wsp, bspb, wpa, wpb, wout, gpost, *, name):
    rows = x.shape[0]
    n = rows // GM
    lanes = lambda g: slice(g * DH, (g + 1) * DH)
    bdot = lambda a, w_ref: _dot(a.astype(BF16), w_ref[...])
    bdot_t = lambda a, w_ref: _dot(a.astype(BF16), w_ref[...], _NT)

    def body(p_ref, of_ref, ob_ref, x_ref, t_ref, m_ref, bm_ref, gon_ref, lng_ref, lnb_ref, wsp_ref, bsp_ref,
             wpa_ref, wpb_ref, wout_ref, gp_ref,
             loss_ref, dp_ref, do_ref, dx_ref, ya_ref, yb_ref, mg_ref, dma_ref, dmb_ref, dout_ref,
             dvec_ref, dgon_ref, dwsp_ref, dbsp_ref):
        @pl.when(pl.program_id(0) == 0)
        def _():
            loss_ref[...] = jnp.zeros_like(loss_ref)
            dvec_ref[...] = jnp.zeros_like(dvec_ref)
            dgon_ref[...] = jnp.zeros_like(dgon_ref)
            dwsp_ref[...] = jnp.zeros_like(dwsp_ref)
            dbsp_ref[...] = jnp.zeros_like(dbsp_ref)

        piece = lambda blk: [p_ref[:, blk * D + g * DH: blk * D + (g + 1) * DH].astype(F32) for g in range(H)]
        zb, ua, va, za = piece(0), piece(1), piece(2), piece(3)
        o = [of_ref[g] + ob_ref[g] for g in range(H)]
        gon = gon_ref[...]
        lng = [lng_ref[:, lanes(g)] for g in range(H)]
        lnb = [lnb_ref[:, lanes(g)] for g in range(H)]
        wsp = [wsp_ref[g] for g in range(H)]
        bsp = [bsp_ref[g] for g in range(H)]
        (ya, yb), vjp1 = jax.vjp(_stage1, zb, ua, va, za, o, gon, lng, lnb, wsp, bsp)
        y_a, y_b = jnp.concatenate(ya, axis=1), jnp.concatenate(yb, axis=1)
        ma, mb = bdot(y_a, wpa_ref), bdot(y_b, wpb_ref)
        ga, gb = p_ref[:, 4 * D:5 * D].astype(F32), p_ref[:, 5 * D:6 * D].astype(F32)
        merged, vjp2 = jax.vjp(_stage2, ma, mb, ga, gb)
        out = bdot(merged, wout_ref)
        gate = m_ref[:, 2 * D:3 * D] + bm_ref[:, 2 * D:3 * D]
        loss, vjp3 = jax.vjp(_stage3, out, x_ref[...], t_ref[...], gp_ref[...], gate)
        loss_ref[...] += jnp.broadcast_to(loss, loss_ref.shape)

        dout, dx, _, dgpost, dgate = vjp3(jnp.ones((1, 1), F32))
        dx_ref[...] = dx
        dmerged = bdot_t(dout, wout_ref)
        dma, dmb, dga, dgb = vjp2(dmerged)
        dya, dyb = bdot_t(dma, wpa_ref), bdot_t(dmb, wpb_ref)
        dzb, dua, dva, dza, do, dgon, dlng, dlnb, dwsp, dbsp = vjp1(
            ([dya[:, lanes(g)] for g in range(H)], [dyb[:, lanes(g)] for g in range(H)]))

        for blk, dlist in enumerate((dzb, dua, dva, dza)):
            for g in range(H):
                dp_ref[:, blk * D + g * DH: blk * D + (g + 1) * DH] = dlist[g].astype(BF16)
        dp_ref[:, 4 * D:5 * D] = dga.astype(BF16)
        dp_ref[:, 5 * D:6 * D] = dgb.astype(BF16)
        for g in range(H):
            do_ref[g] = do[g]
            dwsp_ref[g] += dwsp[g]
            dbsp_ref[g] += dbsp[g]
            dvec_ref[2:3, lanes(g)] += dlng[g]
            dvec_ref[3:4, lanes(g)] += dlnb[g]
        dvec_ref[0:1, :] += dgpost
        dvec_ref[1:2, :] += dgate
        dgon_ref[0:1, :] += dgon
        ya_ref[...] = y_a.astype(BF16)
        yb_ref[...] = y_b.astype(BF16)
        mg_ref[...] = merged.astype(BF16)
        dma_ref[...] = dma.astype(BF16)
        dmb_ref[...] = dmb.astype(BF16)
        dout_ref[...] = dout.astype(BF16)

    row = lambda w: pl.BlockSpec((GM, w), lambda i: (i, 0))
    heads = pl.BlockSpec((H, GM, DH), lambda i: (0, i, 0))
    full = lambda shape: pl.BlockSpec(shape, lambda i: tuple(0 for _ in shape))
    sds = jax.ShapeDtypeStruct
    return pl.pallas_call(
        body, name=name, grid=(n,),
        in_specs=[row(6 * D), heads, heads, row(D), row(D), full((1, 3 * D)), full((1, 3 * D)), full((1, DH)),
                  full((1, D)), full((1, D)), full((H, GM, GM)), full((H, GM, GM)),
                  full((D, D)), full((D, D)), full((D, D)), full((1, D))],
        out_specs=[full((8, DH)), row(6 * D), heads, row(D)] + [row(D)] * 6
        + [full((8, D)), full((8, DH)), full((H, GM, GM)), full((H, GM, GM))],
        out_shape=[sds((8, DH), F32), sds((rows, 6 * D), BF16), sds((H, rows, DH), F32), sds((rows, D), F32)]
        + [sds((rows, D), BF16)] * 6 + [sds((8, D), F32), sds((8, DH), F32), sds((H, GM, GM), F32), sds((H, GM, GM), F32)],
        compiler_params=_params(("arbitrary",)),
    )(p_rest, o_f, o_b, x, tgt, mod, bmod, gon, lng, lnb, wsp, bspb, wpa, wpb, wout, gpost)


def _silu_rows(c, *, name):
    def body(c_ref, o_ref):
        o_ref[...] = _silu(c_ref[...])

    return pl.pallas_call(body, name=name, out_shape=jax.ShapeDtypeStruct(c.shape, F32))(c)


def _dsilu_mul(c, d, *, name):
    def body(c_ref, d_ref, o_ref):
        _, vjp = jax.vjp(_silu, c_ref[...])
        (o_ref[...],) = vjp(d_ref[...])

    return pl.pallas_call(body, name=name, out_shape=jax.ShapeDtypeStruct(c.shape, F32))(c, d)


def _adamw(w, g, m, v, *, name):
    rows, cols = w.shape[-2:]
    t = _tile(rows if g.ndim == 2 else g.shape[1])
    c1 = 1.0 / (1.0 - ADAM_B1 ** ADAM_STEP)
    c2 = 1.0 / (1.0 - ADAM_B2 ** ADAM_STEP)

    def body(w_ref, g_ref, m_ref, v_ref, go_ref, d_ref, mo_ref, vo_ref):
        blk = lambda r: r[...].reshape(t, cols)
        gv = blk(g_ref)
        mn = ADAM_B1 * blk(m_ref) + (1.0 - ADAM_B1) * gv
        vn = ADAM_B2 * blk(v_ref) + (1.0 - ADAM_B2) * (gv * gv)
        delta = -ADAM_LR * ((mn * c1) / (jnp.sqrt(vn * c2) + ADAM_EPS) + ADAM_WD * blk(w_ref))
        for ref, val in ((go_ref, gv), (d_ref, delta), (mo_ref, mn), (vo_ref, vn)):
            ref[...] = val.reshape(ref.shape)

    tile = (pl.BlockSpec((1, t, cols), lambda i: (0, i, 0)) if w.ndim == 3 else pl.BlockSpec((t, cols), lambda i: (i, 0)))
    if g.ndim == 3:
        per = g.shape[1] // t
        g_tile = pl.BlockSpec((1, t, cols), lambda i: (i // per, i % per, 0))
    else:
        g_tile = pl.BlockSpec((t, cols), lambda i: (i, 0))
    return pl.pallas_call(
        body, name=name, grid=(rows // t,),
        in_specs=[tile, g_tile, tile, tile], out_specs=[tile] * 4,
        out_shape=[jax.ShapeDtypeStruct(w.shape, F32)] * 4,
        compiler_params=_params(("parallel",)),
    )(w, g, m, v)


def _add_sibling(fulls, gots, where, out_dtype, *, name):
    rows, cols = fulls[0].shape[1:]
    hr = rows // 2
    t = _tile(hr)
    nt = hr // t
    first = [sum(a.shape[0] for a in fulls[:k]) for k in range(len(fulls) + 1)]
    n = len(fulls)

    def body(w_ref, *refs):
        o_ref = refs[-1]
        j = pl.program_id(0)
        for k in range(n):
            @pl.when((j >= first[k]) & (j < first[k + 1]))
            def _(k=k):
                o_ref[...] = (refs[k][...].astype(F32) + refs[n + k][...].astype(F32)).astype(out_dtype)

    def tile(k, row0):
        def index(j, i, w):
            rb = jnp.where(j < first[k], 0, jnp.where(j >= first[k + 1], nt - 1, i))
            return jnp.clip(j - first[k], 0, fulls[k].shape[0] - 1), row0(w) + rb, 0
        return pl.BlockSpec((1, t, cols), index)

    return pl.pallas_call(
        body, name=name,
        grid_spec=pltpu.PrefetchScalarGridSpec(
            num_scalar_prefetch=1, grid=(first[-1], nt),
            in_specs=[tile(k, lambda w: w[0] * nt) for k in range(n)] + [tile(k, lambda w: 0) for k in range(n)],
            out_specs=pl.BlockSpec((1, t, cols), lambda j, i, w: (j, i, 0))),
        out_shape=jax.ShapeDtypeStruct((first[-1], hr, cols), out_dtype),
        compiler_params=_params(("parallel", "parallel")),
    )(where, *fulls, *gots)


def _sum_chips(own, slots, where, n_out, which, *, name):
    _, hr, cols = own.shape
    t = _tile(hr)

    def body(w_ref, a_ref, s_ref, o_ref):
        f = lambda v: v.astype(F32)
        o_ref[0] = ((f(a_ref[0]) + f(s_ref[0])) + f(s_ref[1])) + f(s_ref[2])

    return pl.pallas_call(
        body, name=name,
        grid_spec=pltpu.PrefetchScalarGridSpec(
            num_scalar_prefetch=1, grid=(hr // t,),
            in_specs=[pl.BlockSpec((1, t, cols), lambda i, w: (w[1], i, 0)),
                      pl.BlockSpec((N_CHIPS - 1, t, cols), lambda i, w: (0, i, 0))],
            out_specs=pl.BlockSpec((1, t, cols), lambda i, w: (w[which], i, 0))),
        out_shape=jax.ShapeDtypeStruct((n_out, hr, cols), F32), compiler_params=_params(("parallel",)),
    )(where, own, slots)


def _local_step(x, c, ctx, c_ctx, tgt, w_mod, b_mod, g_pre, g_post, w_qkv, w_ab, w_rest, w_conv, a_log, dt_bias,
                g_onorm, gm_ln_g, gm_ln_b, w_sp, b_sp, late_weights, on_large_gradients=None, run_after=None):
    rows, rows_c = x.shape[0], ctx.shape[0]
    cc = jnp.zeros((16, D), F32).at[0].set(c[0]).at[1].set(c_ctx)
    scc = _silu_rows(cc, name="silu_cond")
    mod = _mm(scc, w_mod, b_sections=True, after=run_after, name="mod_fwd")
    mod_x, mod_c = mod[0:1], mod[1:2]
    avec = jnp.zeros((1, DH), F32).at[0, :2 * H].set(a_log.reshape(-1))
    dvec = jnp.zeros((1, DH), F32).at[0, :2 * H].set(dt_bias.reshape(-1))
    bspb = jnp.broadcast_to(b_sp[:, :, None], (H, GM, GM))
    w_kv, wconv_kv = w_qkv[:, D:], w_conv[:, D:]

    h_c = _norm_fwd(ctx, g_pre, mod_c, b_mod, name="norm_fwd_ctx")
    pc_kv = _mm(h_c, w_kv, name="inproj_ctx_kv")
    pc_ab = _mm(h_c, w_ab, name="inproj_ctx_ab")
    kv_c = _prep_fwd(pc_kv, wconv_kv, H, name="prep_fwd_ctx")
    gb_c = _gates_fwd(pc_ab, avec, dvec, name="gates_fwd_ctx")
    s_zero = jnp.zeros((H, DH, DH), F32)
    _, _, ssf_c, ssb_c, tsf_c, tsb_c, s_f, s_b = _gdn_fwd(kv_c, gb_c, s_zero, s_zero, False, name="gdn_fwd_ctx")

    h_x = _norm_fwd(x, g_pre, mod_x, b_mod, name="norm_fwd_x")
    p_qkv = _mm(h_x, w_qkv, tm=rows, tn=D // 2, name="inproj_qkv")
    p_ab = _mm(h_x, w_ab, name="inproj_ab")
    p_rest = _mm(h_x, w_rest, tm=rows, tn=D // 2, out_dtype=BF16, name="inproj_rest")
    qkv = _prep_fwd(p_qkv, w_conv, 2 * H, name="prep_fwd_x")
    gb_x = _gates_fwd(p_ab, avec, dvec, name="gates_fwd_x")
    o_f, o_b, ssf, ssb, tsf, tsb, _, _ = _gdn_fwd(qkv, gb_x, s_f, s_b, True, name="gdn_fwd_x")

    w_pa, w_pb, w_out = late_weights(o_f)
    (loss_acc, dp_rest, do, dx_res, ya, yb, mg, dma, dmb, dout, dvec_post, dgon, dwsp, dbspb) = _post(
        p_rest, o_f, o_b, x, tgt, mod_x, b_mod, g_onorm, gm_ln_g, gm_ln_b, w_sp, bspb, w_pa, w_pb, w_out, g_post,
        name="post")
    g = {}
    g["w_pa"] = _mm(ya, dma, ta=True, out_dtype=BF16, name="dw_pa")
    g["w_pb"] = _mm(yb, dmb, ta=True, out_dtype=BF16, name="dw_pb")
    g["w_out"] = _mm(mg, dout, ta=True, out_dtype=BF16, name="dw_out")

    zeros_s = jnp.zeros((H, DH, DH), F32)
    dq_f, dq_b, dg_f, dg_b, ds0_f, ds0_b = _gdn_bwd(qkv, gb_x, ssf, ssb, tsf, tsb, do, zeros_s, zeros_s, True,
                                                    name="gdn_bwd_x")
    dp_qkv, dwc_x = _prep_bwd(p_qkv, w_conv, dq_f, dq_b, 2 * H, name="prep_bwd_x")
    dp_ab, dav_x, ddv_x = _gates_bwd(p_ab, avec, dvec, dg_f, dg_b, name="gates_bwd_x")
    dkv_f, dkv_b, dgc_f, dgc_b, _, _ = _gdn_bwd(kv_c, gb_c, ssf_c, ssb_c, tsf_c, tsb_c, jnp.zeros((H, rows_c, DH), F32),
                                                 ds0_f, ds0_b, False, name="gdn_bwd_ctx")
    dpc_kv, dwc_c = _prep_bwd(pc_kv, wconv_kv, dkv_f, dkv_b, H, name="prep_bwd_ctx")
    dpc_ab, dav_c, ddv_c = _gates_bwd(pc_ab, avec, dvec, dgc_f, dgc_b, name="gates_bwd_ctx")

    dw_kv_c = _mm(h_c, dpc_kv, ta=True, name="dw_kv_ctx")
    dw_qkv = _mm(h_x, dp_qkv, ta=True, tn=D, tk=2 * D, add=dw_kv_c, add_from=1, out_dtype=BF16, name="dw_qkv")
    dw_ab_c = _mm(h_c, dpc_ab, ta=True, name="dw_ab_ctx")
    dw_ab = _mm(h_x, dp_ab, ta=True, add=dw_ab_c, out_dtype=BF16, name="dw_ab")
    dw_rest = _mm(h_x, dp_rest, ta=True, tn=2 * D, tk=2 * D, out_dtype=BF16, name="dw_rest")
    g["w_in"] = (dw_qkv, dw_ab[:, :4 * H], dw_rest)
    token, scatter = on_large_gradients(g) if on_large_gradients is not None else (None, None)
    dh = _mm(dp_ab, w_ab, tb=True, after=token, name="dh_ab")
    dh = _mm(dp_qkv, w_qkv, tb=True, tk=3 * D, add=dh, after=token, name="dh_qkv")
    token = scatter(dh) if scatter is not None else None
    dh = _mm(dp_rest, w_rest, tb=True, tk=3 * D, add=dh, after=token, name="dh_rest")
    grad_x, dgpre_x, dm_x = _norm_bwd(x, g_pre, mod_x, b_mod, [dh], dx_res, name="norm_bwd_x")
    dh_c = _mm(dpc_ab, w_ab, tb=True, after=token, name="dhc_ab")
    dh_c = _mm(dpc_kv, w_kv, tb=True, add=dh_c, after=token, name="dhc_kv")
    _, dgpre_c, dm_c = _norm_bwd(ctx, g_pre, mod_c, b_mod, [dh_c], None, name="norm_bwd_ctx")

    dm_x = dm_x.at[:, 2 * D:].add(dvec_post[1:2])
    dmod = jnp.zeros((16, 3 * D), F32).at[0].set(dm_x[0]).at[1].set(dm_c[0])
    g["mod_factors"] = (scc[0], dm_x[0])
    g["dm_ctx"] = dm_c[0]
    g["silu_c_ctx"] = scc[1:2]
    dcc = _mm(dmod, w_mod, tb=True, b_sections=True, name="dcc")
    g["c_ctx"] = _dsilu_mul(cc[:8], dcc[:8], name="dc_ctx")[1]
    g["b_mod"] = dm_x + dm_c
    g["g_pre"] = dgpre_x + dgpre_c
    g["g_post"] = dvec_post[0:1]
    g["gm_ln_g"], g["gm_ln_b"] = dvec_post[2:3], dvec_post[3:4]
    g["g_onorm"] = dgon[0:1]
    g["w_sp"] = dwsp
    g["b_sp"] = jnp.sum(dbspb, axis=-1)
    g["w_conv"] = dwc_x.at[:, D:].add(dwc_c)
    g["a_log"] = (dav_x + dav_c)[0, :2 * H].reshape(2, H)
    g["dt_bias"] = (ddv_x + ddv_c)[0, :2 * H].reshape(2, H)
    return loss_acc[0, 0], grad_x, g


ANY = pl.BlockSpec(memory_space=pl.ANY)


def _place():
    x, y, c = lax.axis_index("x"), lax.axis_index("y"), lax.axis_index("c")
    chips = [(1 - x, y), (x, 1 - y), (1 - x, 1 - y)]
    return x, y, c, (x, y, 1 - c), chips


def _gather_shards(big, small, *, name):
    nb = len(big)

    def body(*refs):
        ins, sm_in = refs[:nb], refs[nb]
        outs, sm_out = refs[nb + 1:2 * nb + 1], refs[2 * nb + 1]
        send, recv = refs[2 * nb + 2:]
        x, y, c, sibling, chips = _place()
        mine = 2 * x + y

        def half(a, shard, hc):
            hr = big[a].shape[0] // 2
            return outs[a].at[shard, pl.ds(hc * hr, hr), :]

        def remote(k, src, dst, to):
            return pltpu.make_async_remote_copy(src_ref=src, dst_ref=dst, send_sem=send.at[k], recv_sem=recv.at[k],
                                                device_id=to, device_id_type=MESH)

        sends = []
        for a in range(nb):
            hr = big[a].shape[0] // 2
            for j, chip in enumerate(chips):
                sends.append(remote(a * 3 + j, ins[a].at[pl.ds(c * hr, hr), :], half(a, mine, c), (*chip, c)))
        for j, chip in enumerate(chips):
            sends.append(remote(nb * 3 + j, sm_in, sm_out.at[mine], (*chip, c)))
        for cp in sends:
            cp.start()
        base = nb * 3 + 3
        passed = []
        for a in range(nb):
            for j, (px, py) in enumerate(chips):
                theirs = 2 * px + py
                remote(a * 3 + j, half(a, theirs, c), half(a, theirs, c), sibling).wait_recv()
                fw = remote(base + a * 3 + j, half(a, theirs, c), half(a, theirs, c), sibling)
                fw.start()
                passed.append(fw)
        for a in range(nb):
            for j, (px, py) in enumerate(chips):
                theirs = 2 * px + py
                remote(base + a * 3 + j, half(a, theirs, 1 - c), half(a, theirs, 1 - c), sibling).wait_recv()
        for j, (px, py) in enumerate(chips):
            remote(nb * 3 + j, sm_in, sm_out.at[2 * px + py], sibling).wait_recv()
        for cp in sends + passed:
            cp.wait_send()

    n_remote = 2 * nb * 3 + 3
    outs = pl.pallas_call(
        body, name=name, in_specs=[ANY] * (nb + 1), out_specs=[ANY] * (nb + 1),
        out_shape=[jax.ShapeDtypeStruct((N_CHIPS,) + a.shape, a.dtype) for a in big + [small]],
        scratch_shapes=[pltpu.SemaphoreType.DMA((n_remote,)), pltpu.SemaphoreType.DMA((n_remote,))],
    )(*big, small)
    return outs[:nb], outs[nb]


def _gather_late_start(arrs, after, *, name):
    na = len(arrs)

    def body(*refs):
        ins, land = refs[:na], refs[na:2 * na]
        send, recv = refs[2 * na + 1], refs[2 * na + 2]
        token = refs[-1]
        x, y, c, _, chips = _place()
        for a in range(na):
            hr = arrs[a].shape[0] // 2
            for j, (px, py) in enumerate(chips):
                for other in range(2):
                    k = (a * 3 + j) * 2 + other
                    _remote(ins[a].at[pl.ds(c * hr, hr), :], land[a].at[2 * x + y, pl.ds(c * hr, hr), :],
                            send.at[k], recv.at[k], (px, py, c ^ other)).start()
        token[...] = jnp.zeros_like(token)

    lands = [((N_CHIPS,) + a.shape, a.dtype) for a in arrs]
    held = lambda a: pltpu.with_memory_space_constraint(a, pltpu.HBM)
    outs = pl.pallas_call(
        body, name=name, in_specs=[HBM] * (2 * na) + [ANY],
        out_specs=[SEM, SEM] + [HBM] * (2 * na) + [pl.BlockSpec(memory_space=pltpu.VMEM)],
        out_shape=[pltpu.SemaphoreType.DMA((6 * na,)), pltpu.SemaphoreType.DMA((6 * na,))]
        + [pltpu.HBM(a.shape, a.dtype) for a in arrs] + [pltpu.HBM(s, d) for s, d in lands]
        + [jax.ShapeDtypeStruct((8, 128), F32)],
        input_output_aliases={i: 2 + i for i in range(2 * na)},
        compiler_params=pltpu.CompilerParams(has_side_effects=pltpu.SideEffectType.DATAFLOW_SIDE_EFFECTING),
    )(*[held(a) for a in arrs], *[held(lax.empty(s, d)) for s, d in lands], after)
    return outs[0], outs[1], outs[2:2 + na], outs[2 + na:2 + 2 * na], outs[-1]


def _gather_late_wait(send, recv, thru, land, after, *, name):
    na = len(thru)

    def body(*refs):
        ins, slots = refs[:na], refs[na:2 * na]
        send_sem, recv_sem = refs[2 * na], refs[2 * na + 1]
        x, y, c, _, chips = _place()
        for a in range(na):
            hr = thru[a].shape[0] // 2
            for j, (px, py) in enumerate(chips):
                for other in range(2):
                    k = (a * 3 + j) * 2 + other
                    half = c ^ other
                    cp = _remote(ins[a].at[pl.ds(c * hr, hr), :], slots[a].at[2 * px + py, pl.ds(half * hr, hr), :],
                                 send_sem.at[k], recv_sem.at[k], (px, py, half))
                    cp.wait_send()
                    cp.wait_recv()

    outs = pl.pallas_call(
        body, name=name, in_specs=[HBM] * (2 * na) + [SEM, SEM, ANY], out_specs=[HBM] * (2 * na),
        out_shape=[pltpu.HBM(a.shape, a.dtype) for a in list(thru) + list(land)],
        input_output_aliases={i: i for i in range(2 * na)},
        compiler_params=pltpu.CompilerParams(has_side_effects=pltpu.SideEffectType.DATAFLOW_SIDE_EFFECTING),
    )(*thru, *land, send, recv, after)
    return outs[na:]


def _row_chunks(rows, row_bytes, align=8):
    n = max(1, min(rows // align, -(-rows * row_bytes // DMA_CHUNK_BYTES)))
    per = -(-(-(-rows // n)) // align) * align
    return [(r, min(per, rows - r)) for r in range(0, rows, per)]


def _remote(src, dst, send, recv, to):
    return pltpu.make_async_remote_copy(src_ref=src, dst_ref=dst, send_sem=send, recv_sem=recv, device_id=to,
                                        device_id_type=MESH)


def _exchange_halves(arrs, *, name):
    na = len(arrs)

    def body(*refs):
        ins, got = refs[:na], refs[na:2 * na]
        send, recv = refs[2 * na:]
        x, y, c, sibling, _ = _place()
        for a in range(na):
            ns, rows, cols = arrs[a].shape
            hr = rows // 2
            for s in range(ns):
                for r0, nr in _row_chunks(hr, cols * arrs[a].dtype.itemsize, 16):
                    _remote(ins[a].at[s, pl.ds((1 - c) * hr + r0, nr), :], got[a].at[s, pl.ds(r0, nr), :],
                            send.at[a], recv.at[a], sibling).start()
        for a in range(na):
            hr = arrs[a].shape[1] // 2
            _remote(ins[a].at[:, pl.ds((1 - c) * hr, hr), :], got[a], send.at[a], recv.at[a], sibling).wait()

    return pl.pallas_call(
        body, name=name, in_specs=[ANY] * na, out_specs=[ANY] * na,
        out_shape=[jax.ShapeDtypeStruct((a.shape[0], a.shape[1] // 2, a.shape[2]), a.dtype) for a in arrs],
        scratch_shapes=[pltpu.SemaphoreType.DMA((na,)), pltpu.SemaphoreType.DMA((na,))],
    )(*arrs)


def _scatter_sections(arrs, *, name):
    na = len(arrs)

    def body(*refs):
        ins, outs = refs[:na], refs[na:2 * na]
        send, recv = refs[2 * na:]
        x, y, c, _, chips = _place()
        for a in range(na):
            _, hr, cols = arrs[a].shape
            for j, (px, py) in enumerate(chips):
                for r0, nr in _row_chunks(hr, cols * arrs[a].dtype.itemsize, 16):
                    _remote(ins[a].at[2 * px + py, pl.ds(r0, nr), :], outs[a].at[j, pl.ds(r0, nr), :],
                            send.at[a * 3 + j], recv.at[a * 3 + j], (px, py, c)).start()
        for a in range(na):
            for j, (px, py) in enumerate(chips):
                _remote(ins[a].at[2 * px + py], outs[a].at[j], send.at[a * 3 + j], recv.at[a * 3 + j], (px, py, c)).wait()

    return pl.pallas_call(
        body, name=name, in_specs=[ANY] * na, out_specs=[ANY] * na,
        out_shape=[jax.ShapeDtypeStruct((N_CHIPS - 1,) + a.shape[1:], a.dtype) for a in arrs],
        scratch_shapes=[pltpu.SemaphoreType.DMA((3 * na,)), pltpu.SemaphoreType.DMA((3 * na,))],
    )(*arrs)


HBM = pl.BlockSpec(memory_space=pltpu.HBM)
SEM = pl.BlockSpec(memory_space=pltpu.SEMAPHORE)


def _scatter_start(arrs, *, name):
    na = len(arrs)

    def body(*refs):
        ins, land = refs[:na], refs[na:2 * na]
        send, recv = refs[2 * na], refs[2 * na + 1]
        token = refs[-1]
        x, y, c, _, chips = _place()
        for a in range(na):
            _, hr, cols = arrs[a].shape
            for j, (px, py) in enumerate(chips):
                for r0, nr in _row_chunks(hr, cols * arrs[a].dtype.itemsize, 16):
                    _remote(ins[a].at[2 * px + py, pl.ds(r0, nr), :], land[a].at[j, pl.ds(r0, nr), :],
                            send.at[a * 3 + j], recv.at[a * 3 + j], (px, py, c)).start()
        token[...] = jnp.zeros_like(token)

    slots = [((N_CHIPS - 1,) + a.shape[1:], a.dtype) for a in arrs]
    held = lambda a: pltpu.with_memory_space_constraint(a, pltpu.HBM)
    outs = pl.pallas_call(
        body, name=name, in_specs=[HBM] * (2 * na),
        out_specs=[SEM, SEM] + [HBM] * (2 * na) + [pl.BlockSpec(memory_space=pltpu.VMEM)],
        out_shape=[pltpu.SemaphoreType.DMA((3 * na,)), pltpu.SemaphoreType.DMA((3 * na,))]
        + [pltpu.HBM(a.shape, a.dtype) for a in arrs] + [pltpu.HBM(s, d) for s, d in slots]
        + [jax.ShapeDtypeStruct((8, 128), F32)],
        input_output_aliases={i: 2 + i for i in range(2 * na)},
        compiler_params=pltpu.CompilerParams(has_side_effects=pltpu.SideEffectType.DATAFLOW_SIDE_EFFECTING),
    )(*[held(a) for a in arrs], *[held(lax.empty(s, d)) for s, d in slots])
    return outs[0], outs[1], outs[2:2 + na], outs[2 + na:2 + 2 * na], outs[-1]


def _scatter_wait(send, recv, thru, land, after, *, name):
    na = len(thru)

    def body(*refs):
        ins, slots = refs[:na], refs[na:2 * na]
        send_sem, recv_sem = refs[2 * na], refs[2 * na + 1]
        x, y, c, _, chips = _place()
        for a in range(na):
            for j, (px, py) in enumerate(chips):
                cp = _remote(ins[a].at[2 * px + py], slots[a].at[j], send_sem.at[a * 3 + j], recv_sem.at[a * 3 + j],
                             (px, py, c))
                cp.wait_send()
                cp.wait_recv()

    outs = pl.pallas_call(
        body, name=name, in_specs=[HBM] * (2 * na) + [SEM, SEM, ANY], out_specs=[HBM] * (2 * na),
        out_shape=[pltpu.HBM(a.shape, a.dtype) for a in list(thru) + list(land)],
        input_output_aliases={i: i for i in range(2 * na)},
        compiler_params=pltpu.CompilerParams(has_side_effects=pltpu.SideEffectType.DATAFLOW_SIDE_EFFECTING),
    )(*thru, *land, send, recv, after)
    return outs[:na], outs[na:]


def _exchange_start(arrs, *, name):
    na = len(arrs)

    def body(*refs):
        ins, got = refs[:na], refs[na:2 * na]
        send, recv = refs[2 * na], refs[2 * na + 1]
        token = refs[-1]
        x, y, c, sibling, _ = _place()
        for a in range(na):
            ns, rows, cols = arrs[a].shape
            hr = rows // 2
            for s in range(ns):
                for r0, nr in _row_chunks(hr, cols * arrs[a].dtype.itemsize, 16):
                    _remote(ins[a].at[s, pl.ds((1 - c) * hr + r0, nr), :], got[a].at[s, pl.ds(r0, nr), :],
                            send.at[a], recv.at[a], sibling).start()
        token[...] = jnp.zeros_like(token)

    lands = [((a.shape[0], a.shape[1] // 2, a.shape[2]), a.dtype) for a in arrs]
    held = lambda a: pltpu.with_memory_space_constraint(a, pltpu.HBM)
    outs = pl.pallas_call(
        body, name=name, in_specs=[HBM] * (2 * na),
        out_specs=[SEM, SEM] + [HBM] * (2 * na) + [pl.BlockSpec(memory_space=pltpu.VMEM)],
        out_shape=[pltpu.SemaphoreType.DMA((na,)), pltpu.SemaphoreType.DMA((na,))]
        + [pltpu.HBM(a.shape, a.dtype) for a in arrs] + [pltpu.HBM(s, d) for s, d in lands]
        + [jax.ShapeDtypeStruct((8, 128), F32)],
        input_output_aliases={i: 2 + i for i in range(2 * na)},
        compiler_params=pltpu.CompilerParams(has_side_effects=pltpu.SideEffectType.DATAFLOW_SIDE_EFFECTING),
    )(*[held(a) for a in arrs], *[held(lax.empty(s, d)) for s, d in lands])
    return outs[0], outs[1], outs[2:2 + na], outs[2 + na:2 + 2 * na], outs[-1]


def _exchange_wait(send, recv, thru, got, after, *, name):
    na = len(thru)

    def body(*refs):
        ins, lands = refs[:na], refs[na:2 * na]
        send_sem, recv_sem = refs[2 * na], refs[2 * na + 1]
        x, y, c, sibling, _ = _place()
        for a in range(na):
            hr = thru[a].shape[1] // 2
            cp = _remote(ins[a].at[:, pl.ds((1 - c) * hr, hr), :], lands[a], send_sem.at[a], recv_sem.at[a],
                         sibling)
            cp.wait_send()
            cp.wait_recv()

    outs = pl.pallas_call(
        body, name=name, in_specs=[HBM] * (2 * na) + [SEM, SEM, ANY], out_specs=[HBM] * (2 * na),
        out_shape=[pltpu.HBM(a.shape, a.dtype) for a in list(thru) + list(got)],
        input_output_aliases={i: i for i in range(2 * na)},
        compiler_params=pltpu.CompilerParams(has_side_effects=pltpu.SideEffectType.DATAFLOW_SIDE_EFFECTING),
    )(*thru, *got, send, recv, after)
    return outs[:na], outs[na:]


def _finish_reduce(big, small, *, name):
    nb = len(big)

    def body(*refs):
        outs, sm = refs[nb + 1:2 * nb + 1], refs[2 * nb + 1]
        send, recv = refs[2 * nb + 2:]
        x, y, c, sibling, chips = _place()
        blk = lambda px, py, pc: sm.at[4 * px + 2 * py + pc]
        for a in range(nb):
            _, hr, cols = big[a].shape
            for r0, nr in _row_chunks(hr, cols * 4):
                _remote(outs[a].at[c, pl.ds(r0, nr), :], outs[a].at[c, pl.ds(r0, nr), :], send.at[a], recv.at[a],
                        sibling).start()
        first = [_remote(blk(x, y, c), blk(x, y, c), send.at[nb], recv.at[nb], sibling)]
        first += [_remote(blk(x, y, c), blk(x, y, c), send.at[nb + 1 + j], recv.at[nb + 1 + j], (*chip, c))
                  for j, chip in enumerate(chips)]
        for cp in first:
            cp.start()
        passed = []
        for j, (px, py) in enumerate(chips):
            _remote(blk(px, py, c), blk(px, py, c), send.at[nb + 1 + j], recv.at[nb + 1 + j], sibling).wait_recv()
            fw = _remote(blk(px, py, c), blk(px, py, c), send.at[nb + 4 + j], recv.at[nb + 4 + j], sibling)
            fw.start()
            passed.append(fw)
        for a in range(nb):
            _remote(outs[a].at[c], outs[a].at[1 - c], send.at[a], recv.at[a], sibling).wait()
        _remote(blk(x, y, c), blk(x, y, 1 - c), send.at[nb], recv.at[nb], sibling).wait_recv()
        for j, (px, py) in enumerate(chips):
            _remote(blk(px, py, c), blk(px, py, 1 - c), send.at[nb + 4 + j], recv.at[nb + 4 + j], sibling).wait_recv()
        for cp in first + passed:
            cp.wait_send()

    n_remote = nb + 7
    arrs = list(big) + [small]
    outs = pl.pallas_call(
        body, name=name, in_specs=[ANY] * (nb + 1), out_specs=[ANY] * (nb + 1),
        out_shape=[jax.ShapeDtypeStruct(a.shape, F32) for a in arrs],
        input_output_aliases={i: i for i in range(nb + 1)},
        scratch_shapes=[pltpu.SemaphoreType.DMA((n_remote,)), pltpu.SemaphoreType.DMA((n_remote,))],
    )(*arrs)
    return outs[:nb], outs[nb]


def _reduce_start(sectioned, where):
    counts = [len(parts) for parts in sectioned]
    first = [sum(counts[:i]) for i in range(len(sectioned) + 1)]
    send, recv, thru, got, token = _exchange_start([a for parts in sectioned for a in parts], name="rs_exchange_start")

    def resume(after):
        mine, theirs = _exchange_wait(send, recv, thru, got, after, name="rs_exchange_wait")
        chip_sum = [_add_sibling(mine[first[i]:first[i + 1]], theirs[first[i]:first[i + 1]], where, BF16,
                                 name=f"rs_add_sibling_{i}") for i in range(len(sectioned))]
        s_send, s_recv, s_thru, s_land, s_token = _scatter_start(chip_sum, name="rs_scatter_start")
        return (s_send, s_recv, s_thru, s_land), s_token

    return token, resume


def _reduce_finish(handle, small, where, after):
    send, recv, thru, land = handle
    got = _exchange_halves([small], name="rs_exchange_small")
    small_sum = _add_sibling([small], got, where, F32, name="rs_add_sibling_small")
    small_slots = _scatter_sections([small_sum], name="rs_scatter_small")[0]
    own, slots = _scatter_wait(send, recv, thru, land, after, name="rs_scatter_wait")
    red = [_sum_chips(p, s, where, 2, 0, name=f"rs_sum_chips_{i}") for i, (p, s) in enumerate(zip(own, slots))]
    red_small = _sum_chips(small_sum, small_slots, where, 2 * N_CHIPS, 2, name="rs_sum_chips_small")
    big, sm = _finish_reduce(red, red_small, name="rs_finish")
    return big, sm.reshape(-1, sm.shape[-1])


def kernel(x, c, ctx, c_ctx, w_mod, b_mod, g_pre, g_post, w_in, w_conv, a_log, dt_bias, g_onorm, gm_ln_g, gm_ln_b, w_sp, b_sp, w_pa, w_pb, w_out, loss_target, m_c_ctx, m_w_mod, m_b_mod, m_g_pre, m_g_post, m_w_in, m_w_conv, m_a_log, m_dt_bias, m_g_onorm, m_gm_ln_g, m_gm_ln_b, m_w_sp, m_b_sp, m_w_pa, m_w_pb, m_w_out, v_c_ctx, v_w_mod, v_b_mod, v_g_pre, v_g_post, v_w_in, v_w_conv, v_a_log, v_dt_bias, v_g_onorm, v_gm_ln_g, v_gm_ln_b, v_w_sp, v_b_sp, v_w_pa, v_w_pb, v_w_out):
    names = ["c_ctx", "w_mod", "b_mod", "g_pre", "g_post", "w_in", "w_conv", "a_log", "dt_bias", "g_onorm", "gm_ln_g",
             "gm_ln_b", "w_sp", "b_sp", "w_pa", "w_pb", "w_out"]
    w = dict(zip(names, (c_ctx, w_mod, b_mod, g_pre, g_post, w_in, w_conv, a_log, dt_bias, g_onorm, gm_ln_g, gm_ln_b,
                         w_sp, b_sp, w_pa, w_pb, w_out)))
    m = dict(zip(names, (m_c_ctx, m_w_mod, m_b_mod, m_g_pre, m_g_post, m_w_in, m_w_conv, m_a_log, m_dt_bias, m_g_onorm,
                         m_gm_ln_g, m_gm_ln_b, m_w_sp, m_b_sp, m_w_pa, m_w_pb, m_w_out)))
    v = dict(zip(names, (v_c_ctx, v_w_mod, v_b_mod, v_g_pre, v_g_post, v_w_in, v_w_conv, v_a_log, v_dt_bias, v_g_onorm,
                         v_gm_ln_g, v_gm_ln_b, v_w_sp, v_b_sp, v_w_pa, v_w_pb, v_w_out)))
    xy = 2 * lax.axis_index("x") + lax.axis_index("y")
    where = jnp.stack([lax.axis_index("c"), xy, 2 * xy + lax.axis_index("c")]).astype(jnp.int32)

    shards = [a[0].astype(BF16) for a in (w_mod, w_in)]
    gathered, wconv_all = _gather_shards(shards, w_conv[0], name="gather_weights")
    own = lambda full, shard: lax.dynamic_update_slice(full, shard[None], (xy, 0, 0))
    wm_all, win_all = [own(f, s) for f, s in zip(gathered, shards)]
    wconv_all = own(wconv_all, w_conv[0])
    late_shards = [a[0].astype(BF16) for a in (w_pa, w_pb, w_out)]
    late = _gather_late_start(late_shards, wconv_all, name="gather_late_start")

    def late_weights(after):
        lands = _gather_late_wait(*late[:4], after, name="gather_late_wait")
        return [own(f, s).reshape(D, D) for f, s in zip(lands, late_shards)]
    w_conv_f = jnp.concatenate([wconv_all[s] for s in range(N_CHIPS)], axis=1)
    shard_cols = IN_COLS // N_CHIPS
    cut = OFF_A - shard_cols
    assert 0 < cut and cut + 4 * H < shard_cols
    w_qkv = jnp.concatenate([win_all[0], win_all[1][:, :cut]], axis=1)
    w_ab = jnp.pad(win_all[1][:, cut:cut + 4 * H], ((0, 0), (0, DH - 4 * H)))
    w_rest = jnp.concatenate([win_all[1][:, cut + 4 * H:], win_all[2], win_all[3]], axis=1)

    blk = D // N_CHIPS
    rest_cut = shard_cols - cut - 4 * H
    big_names = ("w_in", "w_pa", "w_pb", "w_out")
    in_flight = []

    def start_reduce(grads):
        dw_qkv, dw_ab, dw_rest = grads["w_in"]
        g_win = [dw_qkv[:, :shard_cols],
                 jnp.concatenate([dw_qkv[:, shard_cols:], dw_ab, dw_rest[:, :rest_cut]], axis=1),
                 dw_rest[:, rest_cut:rest_cut + shard_cols], dw_rest[:, rest_cut + shard_cols:]]
        sectioned = [[a.astype(BF16)[None] for a in g_win]]
        sectioned += [[grads[k].reshape(N_CHIPS, blk, D)] for k in ("w_pa", "w_pb", "w_out")]
        token, resume = _reduce_start(sectioned, where)

        def scatter(after):
            handle, scatter_token = resume(after)
            in_flight.append(handle)
            return scatter_token

        return token, scatter

    loss_local, grad_x, g = _local_step(
        x[0], c, ctx[0], c_ctx, loss_target[0], wm_all, b_mod, g_pre, g_post, w_qkv, w_ab, w_rest,
        w_conv_f, a_log[0], dt_bias[0], g_onorm, gm_ln_g, gm_ln_b, w_sp[0], b_sp[0],
        late_weights, on_large_gradients=start_reduce, run_after=late[4])
    g["loss"] = loss_local
    g["_pad"] = jnp.zeros((SMALL_LAYOUT["_pad"][1],), F32)
    g["mod_factors"] = lax.dynamic_update_slice(jnp.zeros((2 * N_CHIPS, 4 * D), F32),
                                                jnp.concatenate(g["mod_factors"])[None], (where[2], 0))
    tail = jnp.zeros((N_CHIPS * SMALL_ROWS * 128 - sum(s for _, s in SMALL_LAYOUT.values()),), F32)
    flat = jnp.concatenate([g[k].reshape(-1) for k in SMALL_LAYOUT] + [tail])
    reduced, gr_small = _reduce_finish(in_flight[0], flat.reshape(N_CHIPS, SMALL_ROWS, 128), where, g["b_mod"])
    gr_tiny = gr_small[TINY_ROW0:]

    def entry(arr, k):
        off, size = SMALL_LAYOUT[k]
        row, col = off // 128 - TINY_ROW0, off % 128
        return arr[row:row + size // 128].reshape(-1) if size >= 128 else arr[row, col:col + size]

    factors = entry(gr_tiny, "mod_factors").reshape(2 * N_CHIPS, 4 * D)
    pad_rows = lambda a: jnp.pad(a, ((0, 16 - a.shape[0]), (0, 0)))
    lhs = pad_rows(jnp.concatenate([factors[:, :D], g["silu_c_ctx"]], axis=0))
    rhs = pad_rows(jnp.concatenate([factors[:, D:], entry(gr_tiny, "dm_ctx")[None]], axis=0))
    mod_cols = 3 * D // N_CHIPS
    gr_wm = _mm(lhs, lax.dynamic_slice(rhs, (0, xy * mod_cols), (16, mod_cols)), ta=True, name="dw_mod")

    res = {k: _adamw(w[k], gr, m[k], v[k], name=f"adamw_{k}") for k, gr in zip(big_names[1:], reduced[1:])}
    flip = lambda a: jnp.swapaxes(a, 1, 2)
    res["w_in"] = [flip(r) for r in _adamw(flip(w_in), flip(reduced[0].reshape(w_in.shape)), flip(m_w_in), flip(v_w_in),
                                           name="adamw_w_in")]
    res["w_mod"] = _adamw(w_mod[0], gr_wm, m_w_mod[0], v_w_mod[0], name="adamw_w_mod")
    res["w_sp"] = _adamw(w_sp.reshape(-1, 128), gr_small, m_w_sp.reshape(-1, 128), v_w_sp.reshape(-1, 128),
                         name="adamw_w_sp")
    tiny = [k for k, (off, _) in SMALL_LAYOUT.items() if TINY_ROW0 <= off // 128 < TINY_ROW0 + TINY_ROWS]
    tiny_names = [k for k in tiny if k not in ("loss", "_pad")]

    def pack(src):
        parts = [src[k].reshape(-1) if k in tiny_names else jnp.zeros((SMALL_LAYOUT[k][1],), F32) for k in tiny]
        return jnp.concatenate(parts).reshape(TINY_ROWS, 128)

    tiny_res = _adamw(pack(w), gr_tiny[:TINY_ROWS], pack(m), pack(v), name="adamw_tiny")
    for k in tiny_names:
        res[k] = [entry(r, k) for r in tiny_res]
    g_conv = lax.dynamic_slice(entry(gr_tiny, "w_conv").reshape(3, 3 * D), (0, xy * (3 * D // N_CHIPS)),
                               (3, 3 * D // N_CHIPS))
    conv_res = _adamw(jnp.pad(w_conv[0], ((0, 5), (0, 0))), jnp.pad(g_conv, ((0, 5), (0, 0))),
                      jnp.pad(m_w_conv[0], ((0, 5), (0, 0))), jnp.pad(v_w_conv[0], ((0, 5), (0, 0))), name="adamw_w_conv")
    res["w_conv"] = [r[:3] for r in conv_res]
    res = {k: [r.reshape(w[k].shape) for r in res[k]] for k in names}

    out = [entry(gr_tiny, "loss").reshape(()), grad_x[None]]
    for i in range(4):
        out += [res[k][i] for k in names]
    return tuple(out)
```

```python
import functools

import jax
import jax.numpy as jnp
from jax import lax
from jax.experimental import pallas as pl
from jax.experimental.pallas import tpu as pltpu

F32 = jnp.float32
BF16 = jnp.bfloat16
HI = lax.Precision.HIGHEST
MESH = pl.DeviceIdType.MESH

D = 1024
H = 8
DH = 128
CH = 64
LOG_CH = 6
PAIR = 2 * CH
GM = 128
assert 1 << LOG_CH == CH and PAIR == DH
PREC_POWERS = ((lax.Precision.HIGH, lax.Precision.HIGH),) * 3 + ((None, None),) * 2
assert len(PREC_POWERS) == LOG_CH - 1
EPS = 1e-6
N_CHIPS = 4
OFF_A = 3 * D
OFF_ZB = OFF_A + 4 * H
IN_COLS = OFF_ZB + 6 * D
VMEM_LIMIT_V7X = 56 * 1024 * 1024
DMA_CHUNK_BYTES = 2 * 1024 * 1024

ADAM_LR, ADAM_B1, ADAM_B2, ADAM_EPS, ADAM_WD, ADAM_STEP = 0.001, 0.9, 0.999, 1e-08, 0.01, 10

SMALL_LAYOUT = {}
_off = 0
for _n, _s in (("w_sp", H * GM * GM), ("c_ctx", D), ("b_mod", 3 * D), ("g_pre", D), ("g_post", D), ("gm_ln_g", D),
               ("gm_ln_b", D), ("b_sp", H * GM), ("g_onorm", DH), ("a_log", 2 * H), ("dt_bias", 2 * H), ("loss", 1),
               ("_pad", 128 - 4 * H - 1 + 6 * 128), ("w_conv", 3 * 3 * D), ("dm_ctx", 3 * D), ("mod_factors", 8 * 4 * D)):
    SMALL_LAYOUT[_n] = (_off, _s)
    _off += _s
SMALL_ROWS = 368
assert N_CHIPS * SMALL_ROWS * 128 >= _off and (SMALL_ROWS // 2) % 8 == 0 and _off % 128 == 0
TINY_ROW0 = SMALL_LAYOUT["c_ctx"][0] // 128
TINY_ROWS = SMALL_LAYOUT["w_conv"][0] // 128 - TINY_ROW0
assert TINY_ROWS % 8 == 0


def _params(sem=None):
    return pltpu.CompilerParams(dimension_semantics=sem, vmem_limit_bytes=VMEM_LIMIT_V7X)


def _tile(n, cands=(256, 128, 64, 32, 16, 8)):
    if n <= cands[0]:
        return n
    for cand in cands:
        if n % cand == 0 and cand >= 64:
            return cand
    return max(d for d in range(8, cands[0], 8) if n % d == 0)


def _silu(x):
    return x * jax.nn.sigmoid(x)


def _gelu(x):
    return 0.5 * x * (1.0 + jnp.tanh(0.7978845608028654 * (x + 0.044715 * (x * x * x))))


def _mm(a, b, *, ta=False, tb=False, out_dtype=F32, tm=1024, tn=1024, tk=1024, add=None, add_from=0,
        b_sections=False, out_sections=False, after=None, name):
    m, k = (a.shape[1], a.shape[0]) if ta else a.shape
    if b_sections:
        sect = b.shape[2]
        n = b.shape[1] if tb else b.shape[0] * sect
        tn, tk = (tn, sect) if tb else (sect, tk)
    else:
        n = b.shape[0] if tb else b.shape[1]
    tm, tn, tk = min(tm, m), min(tn, n), min(tk, k)
    assert m % tm == 0 and n % tn == 0 and k % tk == 0, (name, m, n, k, tm, tn, tk)
    nk = k // tk
    dims = (((0,) if ta else (1,), (1,) if tb else (0,)), ((), ()))
    has_add = add is not None
    assert not has_add or add.shape == (m, n - add_from * tn), (name, add.shape)

    def body(*refs):
        a_ref, b_ref = refs[:2]
        o_ref = refs[2 + int(has_add) + int(after is not None)]
        acc_ref = refs[-1]
        kk = pl.program_id(2)
        bv = b_ref[0] if b_sections else b_ref[...]
        part = lax.dot_general(a_ref[...].astype(BF16), bv.astype(BF16), dims, preferred_element_type=F32)

        def finish(res):
            if has_add:
                res = res + jnp.where(pl.program_id(1) >= add_from, refs[2][...], 0.0)
            if out_sections:
                o_ref[0] = res.astype(out_dtype)
            else:
                o_ref[...] = res.astype(out_dtype)

        if nk == 1:
            finish(part)
            return

        @pl.when(kk == 0)
        def _():
            acc_ref[...] = part

        @pl.when((kk > 0) & (kk < nk - 1))
        def _():
            acc_ref[...] += part

        @pl.when(kk == nk - 1)
        def _():
            finish(acc_ref[...] + part)

    a_spec = pl.BlockSpec((tk, tm), lambda i, j, q: (q, i)) if ta else pl.BlockSpec((tm, tk), lambda i, j, q: (i, q))
    if b_sections:
        b_spec = (pl.BlockSpec((1, tn, tk), lambda i, j, q: (q, j, 0)) if tb
                  else pl.BlockSpec((1, tk, tn), lambda i, j, q: (j, q, 0)))
    else:
        b_spec = pl.BlockSpec((tn, tk), lambda i, j, q: (j, q)) if tb else pl.BlockSpec((tk, tn), lambda i, j, q: (q, j))
    add_spec = [pl.BlockSpec((tm, tn), lambda i, j, q: (i, jnp.maximum(j - add_from, 0)))] if has_add else []
    if out_sections:
        out_spec, out_shape = pl.BlockSpec((1, tm, tn), lambda i, j, q: (j, i, 0)), (n // tn, m, tn)
    else:
        out_spec, out_shape = pl.BlockSpec((tm, tn), lambda i, j, q: (i, j)), (m, n)
    return pl.pallas_call(
        body, name=name, grid=(m // tm, n // tn, nk),
        in_specs=[a_spec, b_spec] + add_spec + ([pl.BlockSpec(memory_space=pl.ANY)] if after is not None else []),
        out_specs=out_spec, out_shape=jax.ShapeDtypeStruct(out_shape, out_dtype),
        scratch_shapes=[pltpu.VMEM((tm, tn), F32)] if nk > 1 else [],
        compiler_params=_params(("parallel", "parallel", "arbitrary")),
    )(*([a, b] + ([add] if has_add else []) + ([after] if after is not None else [])))


def _h_fn(x, g, m):
    shift, scale = m[:, 0:D], m[:, D:2 * D]
    r = lax.rsqrt(jnp.mean(x * x, axis=-1, keepdims=True) + EPS)
    return (x * r * g) * (1.0 + scale) + shift


def _norm_fwd(x, g, mod, bmod, *, name):
    rows = x.shape[0]
    t = min(512, rows)

    def body(x_ref, g_ref, m_ref, b_ref, h_ref):
        h_ref[...] = _h_fn(x_ref[...], g_ref[...], m_ref[...] + b_ref[...]).astype(BF16)

    vec = lambda w: pl.BlockSpec((1, w), lambda i: (0, 0))
    return pl.pallas_call(
        body, name=name, grid=(rows // t,),
        in_specs=[pl.BlockSpec((t, D), lambda i: (i, 0)), vec(D), vec(3 * D), vec(3 * D)],
        out_specs=pl.BlockSpec((t, D), lambda i: (i, 0)),
        out_shape=jax.ShapeDtypeStruct((rows, D), BF16),
        compiler_params=_params(("parallel",)),
    )(x, g, mod, bmod)


def _norm_bwd(x, g, mod, bmod, dh_parts, resid, *, name):
    rows = x.shape[0]
    t = min(512, rows)
    n_parts = len(dh_parts)
    has_resid = resid is not None

    def body(*refs):
        x_ref, g_ref, m_ref, b_ref = refs[:4]
        parts = refs[4:4 + n_parts]
        r_ref = refs[4 + n_parts] if has_resid else None
        dx_ref, dg_ref, dm_ref = refs[-3:]
        i = pl.program_id(0)
        dh = parts[0][...]
        for p in parts[1:]:
            dh = dh + p[...]
        _, vjp = jax.vjp(_h_fn, x_ref[...], g_ref[...], m_ref[...] + b_ref[...])
        dx, dg, dm = vjp(dh)
        if has_resid:
            dx = dx + r_ref[...]
        dx_ref[...] = dx

        @pl.when(i == 0)
        def _():
            dg_ref[...] = dg
            dm_ref[...] = dm

        @pl.when(i > 0)
        def _():
            dg_ref[...] += dg
            dm_ref[...] += dm

    vec = lambda w: pl.BlockSpec((1, w), lambda i: (0, 0))
    tile = pl.BlockSpec((t, D), lambda i: (i, 0))
    ins = [x, g, mod, bmod, *dh_parts] + ([resid] if has_resid else [])
    return pl.pallas_call(
        body, name=name, grid=(rows // t,),
        in_specs=[tile, vec(D), vec(3 * D), vec(3 * D)] + [tile] * (n_parts + int(has_resid)),
        out_specs=[tile, vec(D), vec(3 * D)],
        out_shape=[jax.ShapeDtypeStruct((rows, D), F32), jax.ShapeDtypeStruct((1, D), F32),
                   jax.ShapeDtypeStruct((1, 3 * D), F32)],
        compiler_params=_params(("arbitrary",)),
    )(*ins)


def _conv_tile(u_ref, r0, t, rows, w0, w1, w2):
    u = u_ref[pl.ds(r0, t), :]
    prev8 = u_ref[pl.ds(pl.multiple_of(jnp.maximum(r0 - 8, 0), 8), 8), :]
    next8 = u_ref[pl.ds(pl.multiple_of(jnp.minimum(r0 + t, rows - 8), 8), 8), :]
    r8 = lax.broadcasted_iota(jnp.int32, (8, DH), 0)
    prev_row = jnp.sum(jnp.where(r8 == 7, prev8, 0.0), axis=0, keepdims=True)
    next_row = jnp.sum(jnp.where(r8 == 0, next8, 0.0), axis=0, keepdims=True)
    prev_row = jnp.where(r0 > 0, prev_row, 0.0)
    next_row = jnp.where(r0 + t < rows, next_row, 0.0)
    ri = lax.broadcasted_iota(jnp.int32, (t, DH), 0)
    um1 = jnp.where(ri == 0, prev_row, pltpu.roll(u, 1, 0))
    up1 = jnp.where(ri == t - 1, next_row, pltpu.roll(u, t - 1, 0))
    return w0 * um1 + w1 * u + w2 * up1, um1, u, up1


def _rowlocal(z, norm):
    y = _silu(z)
    return y * lax.rsqrt(jnp.sum(y * y, axis=-1, keepdims=True) + EPS) if norm else y


def _prep_fwd(p, wconv, n_norm, *, name):
    rows, nb = p.shape[0], p.shape[1] // DH
    t = min(512, rows)
    out = None
    for norm, b0, b1 in ((True, 0, n_norm), (False, n_norm, nb)):
        def body(u_ref, w_ref, *rest, norm=norm):
            o_ref = rest[-1]
            w0, w1, w2 = w_ref[0:1, :], w_ref[1:2, :], w_ref[2:3, :]

            def step(s, carry):
                r0 = pl.multiple_of(s * t, t)
                z, _, _, _ = _conv_tile(u_ref, r0, t, rows, w0, w1, w2)
                o_ref[0, pl.ds(r0, t), :] = _rowlocal(z, norm)
                return carry

            lax.fori_loop(0, rows // t, step, 0)

        held = [] if out is None else [out]
        out = pl.pallas_call(
            body, name=f"{name}_{'norm' if norm else 'plain'}", grid=(b1 - b0,),
            in_specs=[pl.BlockSpec((rows, DH), lambda j, b0=b0: (0, b0 + j)),
                      pl.BlockSpec((3, DH), lambda j, b0=b0: (0, b0 + j))] + [ANY] * len(held),
            out_specs=pl.BlockSpec((1, rows, DH), lambda j, b0=b0: (b0 + j, 0, 0)),
            out_shape=jax.ShapeDtypeStruct((nb, rows, DH), F32),
            input_output_aliases={2: 0} if held else {},
            compiler_params=_params(("parallel",)),
        )(p, wconv, *held)
    return out


def _prep_bwd(p, wconv, d_a, d_b, n_norm, *, name):
    rows, nb = p.shape[0], p.shape[1] // DH
    t = min(512, rows)
    outs = []
    for norm, b0, b1 in ((True, 0, n_norm), (False, n_norm, nb)):
        def body(u_ref, w_ref, da_ref, db_ref, *rest, norm=norm):
            du_ref, dw_ref, dz_ref = rest[-3:]
            w0, w1, w2 = w_ref[0:1, :], w_ref[1:2, :], w_ref[2:3, :]

            def step1(s, carry):
                a0, a1, a2 = carry
                r0 = pl.multiple_of(s * t, t)
                z, um1, u, up1 = _conv_tile(u_ref, r0, t, rows, w0, w1, w2)
                _, vjp = jax.vjp(lambda zz: _rowlocal(zz, norm), z)
                (dz,) = vjp(da_ref[0, pl.ds(r0, t), :] + db_ref[0, pl.ds(r0, t), :])
                dz_ref[pl.ds(r0, t), :] = dz
                red = lambda v: jnp.sum(v, axis=0, keepdims=True)
                return a0 + red(dz * um1), a1 + red(dz * u), a2 + red(dz * up1)

            zero = jnp.zeros((1, DH), F32)
            a0, a1, a2 = lax.fori_loop(0, rows // t, step1, (zero, zero, zero))
            dw_ref[0:1, :] = a0
            dw_ref[1:2, :] = a1
            dw_ref[2:3, :] = a2

            def step2(s, carry):
                r0 = pl.multiple_of(s * t, t)
                du, _, _, _ = _conv_tile(dz_ref, r0, t, rows, w2, w1, w0)
                du_ref[pl.ds(r0, t), :] = du.astype(BF16)
                return carry

            lax.fori_loop(0, rows // t, step2, 0)

        col = pl.BlockSpec((rows, DH), lambda j, b0=b0: (0, b0 + j))
        w_spec = pl.BlockSpec((3, DH), lambda j, b0=b0: (0, b0 + j))
        d_spec = pl.BlockSpec((1, rows, DH), lambda j, b0=b0: (b0 + j, 0, 0))
        outs = pl.pallas_call(
            body, name=f"{name}_{'norm' if norm else 'plain'}", grid=(b1 - b0,),
            in_specs=[col, w_spec, d_spec, d_spec] + [ANY] * len(outs), out_specs=[col, w_spec],
            out_shape=[jax.ShapeDtypeStruct((rows, nb * DH), BF16), jax.ShapeDtypeStruct((3, nb * DH), F32)],
            input_output_aliases={4: 0, 5: 1} if outs else {},
            scratch_shapes=[pltpu.VMEM((rows, DH), F32)],
            compiler_params=_params(("parallel",)),
        )(p, wconv, d_a, d_b, *outs)
    return outs


def _gates_fn(pab, avec, dvec):
    t = pab.shape[0]
    lane = lax.broadcasted_iota(jnp.int32, pab.shape, 1)
    xg = pab + dvec
    sp = jnp.maximum(xg, 0.0) + jnp.log(1.0 + jnp.exp(-jnp.abs(xg)))
    g = jnp.where(lane < 2 * H, -jnp.exp(avec) * sp, 0.0)
    ii = lax.broadcasted_iota(jnp.int32, (t, t), 0)
    jj = lax.broadcasted_iota(jnp.int32, (t, t), 1)
    same = (ii >> LOG_CH) == (jj >> LOG_CH)
    cum_f = _dot(jnp.where(same & (jj <= ii), 1.0, 0.0), g, precision=HI)
    cum_r = _dot(jnp.where(same & (jj >= ii), 1.0, 0.0), g, precision=HI)
    return jnp.where(lane < H, cum_f, jnp.where(lane < 2 * H, cum_r, jnp.where(lane < 4 * H, jax.nn.sigmoid(pab), 0.0)))


def _gates_fwd(pab, avec, dvec, *, name):
    rows = pab.shape[0]
    t = min(2 * PAIR, rows)

    def body(p_ref, a_ref, d_ref, o_ref):
        o_ref[...] = _gates_fn(p_ref[...], a_ref[...], d_ref[...])

    vec = pl.BlockSpec((1, DH), lambda i: (0, 0))
    tile = pl.BlockSpec((t, DH), lambda i: (i, 0))
    return pl.pallas_call(
        body, name=name, grid=(rows // t,), in_specs=[tile, vec, vec], out_specs=tile,
        out_shape=jax.ShapeDtypeStruct((rows, DH), F32), compiler_params=_params(("parallel",)),
    )(pab, avec, dvec)


def _gates_bwd(pab, avec, dvec, d_a, d_b, *, name):
    rows = pab.shape[0]
    t = min(2 * PAIR, rows)

    def body(p_ref, a_ref, d_ref, da_ref, db_ref, dp_ref, dav_ref, ddv_ref):
        i = pl.program_id(0)
        _, vjp = jax.vjp(_gates_fn, p_ref[...], a_ref[...], d_ref[...])
        dp, dav, ddv = vjp(da_ref[...] + db_ref[...])
        dp_ref[...] = dp.astype(BF16)

        @pl.when(i == 0)
        def _():
            dav_ref[...] = dav
            ddv_ref[...] = ddv

        @pl.when(i > 0)
        def _():
            dav_ref[...] += dav
            ddv_ref[...] += ddv

    vec = pl.BlockSpec((1, DH), lambda i: (0, 0))
    tile = pl.BlockSpec((t, DH), lambda i: (i, 0))
    return pl.pallas_call(
        body, name=name, grid=(rows // t,), in_specs=[tile, vec, vec, tile, tile], out_specs=[tile, vec, vec],
        out_shape=[jax.ShapeDtypeStruct((rows, DH), BF16), jax.ShapeDtypeStruct((1, DH), F32),
                   jax.ShapeDtypeStruct((1, DH), F32)],
        compiler_params=_params(("arbitrary",)),
    )(pab, avec, dvec, d_a, d_b)


def _dot_general(a, b, dims, precision=None):
    return lax.dot_general(a, b, (dims, ((), ())), precision=precision, preferred_element_type=F32)


@functools.partial(jax.custom_vjp, nondiff_argnums=(2,))
def _dot_bf16(a, b, dims):
    return _dot_general(a.astype(BF16), b.astype(BF16), dims)


def _dot_bf16_fwd(a, b, dims):
    return _dot_bf16(a, b, dims), (a, b)


def _dot_bf16_bwd(dims, res, g):
    a, b = res
    (ca,), (cb,) = dims
    da = _dot_bf16(g, b, ((1,), (1 - cb,))) if ca == 1 else _dot_bf16(b, g, ((1 - cb,), (1,)))
    db = _dot_bf16(a, g, ((1 - ca,), (0,))) if cb == 0 else _dot_bf16(g, a, ((0,), (1 - ca,)))
    return da, db


_dot_bf16.defvjp(_dot_bf16_fwd, _dot_bf16_bwd)


def _dot(a, b, dims=((1,), (0,)), precision=None):
    if precision is None and a.dtype == F32 and b.dtype == F32:
        return _dot_bf16(a, b, dims)
    return _dot_general(a, b, dims, precision)


_NT = ((1,), (1,))
_TN = ((0,), (0,))


@jax.custom_vjp
def _saved_inverse(neg_a, inv):
    return inv


def _saved_inverse_fwd(neg_a, inv):
    return inv, inv


def _saved_inverse_bwd(inv, d_inv):
    idx = range(len(inv))
    left = [_dot(inv[i], d_inv[i], _TN) for i in idx]
    d_neg_a = [_dot(left[i], inv[i], _NT) for i in idx]
    return d_neg_a, [jnp.zeros_like(t) for t in inv]


_saved_inverse.defvjp(_saved_inverse_fwd, _saved_inverse_bwd)


def _pairs(s, q, k, v, gcol, bcol, revs, inv_saved=None):
    idx = range(len(revs))
    ii = lax.broadcasted_iota(jnp.int32, (PAIR, PAIR), 0)
    jj = lax.broadcasted_iota(jnp.int32, (PAIR, PAIR), 1)
    same = (ii >> LOG_CH) == (jj >> LOG_CH)
    incl_d = (same & (ii >= jj), same & (ii <= jj))
    strict_d = (same & (ii > jj), same & (ii < jj))
    incl = [incl_d[int(r)] for r in revs]
    strict = [strict_d[int(r)] for r in revs]
    eye = jnp.where(ii == jj, 1.0, 0.0)
    gc_i = [jnp.broadcast_to(gcol[i], (PAIR, DH)) for i in idx]
    gc_j = [gc_i[i].T for i in idx]
    decay = [jnp.where(incl[i], jnp.exp(jnp.where(incl[i], gc_i[i] - gc_j[i], 0.0)), 0.0) for i in idx]
    b_b = [jnp.broadcast_to(bcol[i], (PAIR, DH)) for i in idx]
    kb = [k[i] * b_b[i] for i in idx]
    kk = [_dot(kb[i], k[i], _NT) for i in idx]
    bp = [jnp.where(strict[i], -kk[i] * decay[i], 0.0) for i in idx]
    if inv_saved is not None:
        inv = _saved_inverse(bp, inv_saved)
    else:
        low = bp
        for prec_sq, prec_acc in PREC_POWERS:
            bp = [_dot(bp[i], bp[i], precision=prec_sq) for i in idx]
            more = [_dot(low[i], bp[i], precision=prec_acc) for i in idx]
            low = [low[i] + bp[i] + more[i] for i in idx]
        inv = [eye + low[i] for i in idx]
    eg = [jnp.exp(gc_i[i]) for i in idx]
    sol = [_dot(inv[i], jnp.concatenate([v[i] * b_b[i], kb[i] * eg[i]], axis=1)) for i in idx]
    u_val = [sol[i][:, :DH] for i in idx]
    w_key = [sol[i][:, DH:] for i in idx]
    row = lax.broadcasted_iota(jnp.int32, (PAIR, 1), 0)
    has_q = q[0] is not None
    if has_q:
        qc = [q[i] * (DH ** -0.5) for i in idx]
        qk = [_dot(qc[i], k[i], _NT) for i in idx]
        attn = [qk[i] * decay[i] for i in idx]
        qd = [qc[i] * eg[i] for i in idx]
    outs = [[None, None] for _ in idx]
    zeros = jnp.zeros((CH, DH), F32)
    for step in range(2):
        cidx = [(1 - step) if revs[i] else step for i in idx]
        sl = [slice(c * CH, (c + 1) * CH) for c in cidx]
        last = [c * CH if revs[i] else c * CH + CH - 1 for i, c in zip(idx, cidx)]
        gl = [jnp.sum(jnp.where(row == last[i], gcol[i], 0.0), axis=0, keepdims=True) for i in idx]
        k_tail = [k[i][sl[i]] * jnp.exp(gl[i] - gc_i[i][sl[i]]) for i in idx]
        ws = [_dot(w_key[i][sl[i]], s[i]) for i in idx]
        v_new = [u_val[i][sl[i]] - ws[i] for i in idx]
        if has_q:
            v_pad = [jnp.concatenate([v_new[i], zeros] if cidx[i] == 0 else [zeros, v_new[i]], axis=0) for i in idx]
            o_state = [_dot(qd[i][sl[i]], s[i]) for i in idx]
            o_local = [_dot(attn[i][sl[i]], v_pad[i]) for i in idx]
            for i in idx:
                outs[i][cidx[i]] = o_state[i] + o_local[i]
        kv = [_dot(k_tail[i], v_new[i], _TN) for i in idx]
        s = [s[i] * jnp.exp(gl[i]) + kv[i] for i in idx]
    return s, ([jnp.concatenate(outs[i], axis=0) for i in idx] if has_q else None), inv


def _lane_col(tile, idx):
    lane = lax.broadcasted_iota(jnp.int32, tile.shape, 1)
    return jnp.sum(jnp.where(lane == idx, tile, 0.0), axis=1, keepdims=True)


def _gdn_fwd(qkv, gb, s0f, s0b, has_q, *, name):
    nb, rows, _ = qkv.shape
    n = rows // PAIR
    qoff = H if has_q else 0

    def body(qf_ref, qb_ref, gf_ref, gr_ref, s0f_ref, s0b_ref, of_ref, ob_ref, ssf_ref, ssb_ref, tsf_ref, tsb_ref,
             sf_ref, sb_ref):
        @pl.when(pl.program_id(0) == 0)
        def _():
            sf_ref[...] = s0f_ref[...]
            sb_ref[...] = s0b_ref[...]

        gtiles = (gf_ref[...], gr_ref[...])

        dirs = ((qf_ref, sf_ref, ssf_ref, of_ref), (qb_ref, sb_ref, ssb_ref, ob_ref))
        ts_refs = (tsf_ref, tsb_ref)

        work = [(h, d) for h in range(H) for d in range(2)]
        loaded = []
        for h, d in work:
            q_ref, s_ref, _, _ = dirs[d]
            loaded.append((s_ref[h], q_ref[h] if has_q else None, q_ref[qoff + h], q_ref[qoff + H + h],
                           _lane_col(gtiles[d], d * H + h), _lane_col(gtiles[d], 2 * H + d * H + h)))
        s_new, o, inv = _pairs(*[list(col) for col in zip(*loaded)], revs=[d == 1 for _, d in work])
        for i, (h, d) in enumerate(work):
            _, s_ref, ss_ref, o_ref = dirs[d]
            ss_ref[0, h] = loaded[i][0]
            ts_refs[d][0, h] = inv[i]
            s_ref[h] = s_new[i]
            o_ref[h] = o[i] if has_q else jnp.zeros((PAIR, DH), F32)

    fwd3 = lambda i: (0, i, 0)
    rev3 = lambda i: (0, n - 1 - i, 0)
    state = pl.BlockSpec((H, DH, DH), lambda i: (0, 0, 0))
    saved = pl.BlockSpec((1, H, DH, DH), lambda i: (i, 0, 0, 0))
    return pl.pallas_call(
        body, name=name, grid=(n,),
        in_specs=[pl.BlockSpec((nb, PAIR, DH), fwd3), pl.BlockSpec((nb, PAIR, DH), rev3),
                  pl.BlockSpec((PAIR, DH), lambda i: (i, 0)), pl.BlockSpec((PAIR, DH), lambda i: (n - 1 - i, 0)),
                  state, state],
        out_specs=[pl.BlockSpec((H, PAIR, DH), fwd3), pl.BlockSpec((H, PAIR, DH), rev3), saved, saved, saved, saved,
                   state, state],
        out_shape=[jax.ShapeDtypeStruct((H, rows, DH), F32)] * 2 + [jax.ShapeDtypeStruct((n, H, DH, DH), F32)] * 4
        + [jax.ShapeDtypeStruct((H, DH, DH), F32)] * 2,
        compiler_params=_params(("arbitrary",)),
    )(qkv, qkv, gb, gb, s0f, s0b)


def _gdn_bwd(qkv, gb, ssf, ssb, tsf, tsb, do, dsf, dsb, has_q, *, name):
    nb, rows, _ = qkv.shape
    n = rows // PAIR
    qoff = H if has_q else 0

    def body(qf_ref, qb_ref, gf_ref, gr_ref, ssf_ref, ssb_ref, tsf_ref, tsb_ref, dof_ref, dob_ref, dsf0_ref, dsb0_ref,
             dqf_ref, dqb_ref, dgf_ref, dgr_ref, dsf_ref, dsb_ref):
        ts_refs = (tsf_ref, tsb_ref)
        @pl.when(pl.program_id(0) == 0)
        def _():
            dsf_ref[...] = dsf0_ref[...]
            dsb_ref[...] = dsb0_ref[...]

        gtiles = (gf_ref[...], gr_ref[...])
        lane = lax.broadcasted_iota(jnp.int32, (PAIR, DH), 1)

        dirs = ((qf_ref, ssf_ref, dof_ref, dsf_ref, dqf_ref), (qb_ref, ssb_ref, dob_ref, dsb_ref, dqb_ref))

        work = [(h, d) for h in range(H) for d in range(2)]
        revs = [d == 1 for _, d in work]
        s_in, q_in, k_in, v_in, g_in, b_in, ds_out, do_out, inv_in = [], [], [], [], [], [], [], [], []
        for h, d in work:
            q_ref, ss_ref, do_ref, ds_ref, _ = dirs[d]
            s_in.append(ss_ref[0, h])
            inv_in.append(ts_refs[d][0, h])
            q_in.append(q_ref[h] if has_q else None)
            k_in.append(q_ref[qoff + h])
            v_in.append(q_ref[qoff + H + h])
            g_in.append(_lane_col(gtiles[d], d * H + h))
            b_in.append(_lane_col(gtiles[d], 2 * H + d * H + h))
            ds_out.append(ds_ref[h])
            do_out.append(do_ref[h] if has_q else None)
        if has_q:
            _, vjp = jax.vjp(lambda s_, q_, k_, v_, g_, b_: _pairs(s_, q_, k_, v_, g_, b_, revs, inv_in)[:2],
                             s_in, q_in, k_in, v_in, g_in, b_in)
            ds, dq, dk, dv, dg, db = vjp((ds_out, do_out))
        else:
            _, vjp = jax.vjp(lambda s_, k_, v_, g_, b_: _pairs(s_, q_in, k_, v_, g_, b_, revs, inv_in)[0],
                             s_in, k_in, v_in, g_in, b_in)
            ds, dk, dv, dg, db = vjp(ds_out)
        dgb = [jnp.zeros((PAIR, DH), F32)] * 2
        for i, (h, d) in enumerate(work):
            dq_ref, ds_ref = dirs[d][4], dirs[d][3]
            ds_ref[h] = ds[i]
            if has_q:
                dq_ref[h] = dq[i]
            dq_ref[qoff + h], dq_ref[qoff + H + h] = dk[i], dv[i]
            dgb[d] = (dgb[d] + jnp.where(lane == d * H + h, dg[i], 0.0)
                      + jnp.where(lane == 2 * H + d * H + h, db[i], 0.0))
        dgf_ref[...] = dgb[0]
        dgr_ref[...] = dgb[1]

    fwd3 = lambda i: (0, n - 1 - i, 0)
    rev3 = lambda i: (0, i, 0)
    state = pl.BlockSpec((H, DH, DH), lambda i: (0, 0, 0))
    saved = pl.BlockSpec((1, H, DH, DH), lambda i: (n - 1 - i, 0, 0, 0))
    gf_spec = pl.BlockSpec((PAIR, DH), lambda i: (n - 1 - i, 0))
    gr_spec = pl.BlockSpec((PAIR, DH), lambda i: (i, 0))
    return pl.pallas_call(
        body, name=name, grid=(n,),
        in_specs=[pl.BlockSpec((nb, PAIR, DH), fwd3), pl.BlockSpec((nb, PAIR, DH), rev3), gf_spec, gr_spec,
                  saved, saved, saved, saved,
                  pl.BlockSpec((H, PAIR, DH), fwd3), pl.BlockSpec((H, PAIR, DH), rev3), state, state],
        out_specs=[pl.BlockSpec((nb, PAIR, DH), fwd3), pl.BlockSpec((nb, PAIR, DH), rev3), gf_spec, gr_spec, state, state],
        out_shape=[jax.ShapeDtypeStruct((nb, rows, DH), F32)] * 2 + [jax.ShapeDtypeStruct((rows, DH), F32)] * 2
        + [jax.ShapeDtypeStruct((H, DH, DH), F32)] * 2,
        compiler_params=_params(("arbitrary",)),
    )(qkv, qkv, gb, gb, ssf, ssb, tsf, tsb, do, do, dsf, dsb)


def _stage1(zb, ua, va, za, o, gon, lng, lnb, wsp, bsp):
    gv = [_gelu(t) for t in va]
    mu = sum(jnp.sum(t, axis=-1, keepdims=True) for t in gv) * (1.0 / D)
    xc = [t - mu for t in gv]
    var = sum(jnp.sum(t * t, axis=-1, keepdims=True) for t in xc) * (1.0 / D)
    rs = lax.rsqrt(var + EPS)
    ya, yb = [], []
    for g in range(H):
        vv = xc[g] * rs * lng[g] + lnb[g]
        s = _dot(wsp[g], vv) + bsp[g]
        ya.append(_gelu(ua[g]) * s * _silu(za[g]))
        r = lax.rsqrt(jnp.mean(o[g] * o[g], axis=-1, keepdims=True) + EPS)
        yb.append(o[g] * r * gon * _silu(zb[g]))
    return ya, yb


def _stage2(ma, mb, ga, gb):
    return jax.nn.sigmoid(ga) * ma + jax.nn.sigmoid(gb) * mb


def _stage3(out, x, tgt, gpost, gate):
    r = out * lax.rsqrt(jnp.mean(out * out, axis=-1, keepdims=True) + EPS) * gpost
    err = x + gate * r - tgt
    return 0.5 * jnp.sum(jnp.mean(err * err, axis=-1, keepdims=True), axis=0, keepdims=True)


def _post(p_rest, o_f, o_b, x, tgt, mod, bmod, gon, lng, lnb, wsp, bspb, wpa, wpb, wout, gpost, *, name):
    rows = x.shape[0]
    n = rows // GM
    lanes = lambda g: slice(g * DH, (g + 1) * DH)
    bdot = lambda a, w_ref: _dot(a.astype(BF16), w_ref[...])
    bdot_t = lambda a, w_ref: _dot(a.astype(BF16), w_ref[...], _NT)

    def body(p_ref, of_ref, ob_ref, x_ref, t_ref, m_ref, bm_ref, gon_ref, lng_ref, lnb_ref, wsp_ref, bsp_ref,
             wpa_ref, wpb_ref, wout_ref, gp_ref,
             loss_ref, dp_ref, do_ref, dx_ref, ya_ref, yb_ref, mg_ref, dma_ref, dmb_ref, dout_ref,
             dvec_ref, dgon_ref, dwsp_ref, dbsp_ref):
        @pl.when(pl.program_id(0) == 0)
        def _():
            loss_ref[...] = jnp.zeros_like(loss_ref)
            dvec_ref[...] = jnp.zeros_like(dvec_ref)
            dgon_ref[...] = jnp.zeros_like(dgon_ref)
            dwsp_ref[...] = jnp.zeros_like(dwsp_ref)
            dbsp_ref[...] = jnp.zeros_like(dbsp_ref)

        piece = lambda blk: [p_ref[:, blk * D + g * DH: blk * D + (g + 1) * DH].astype(F32) for g in range(H)]
        zb, ua, va, za = piece(0), piece(1), piece(2), piece(3)
        o = [of_ref[g] + ob_ref[g] for g in range(H)]
        gon = gon_ref[...]
        lng = [lng_ref[:, lanes(g)] for g in range(H)]
        lnb = [lnb_ref[:, lanes(g)] for g in range(H)]
        wsp = [wsp_ref[g] for g in range(H)]
        bsp = [bsp_ref[g] for g in range(H)]
        (ya, yb), vjp1 = jax.vjp(_stage1, zb, ua, va, za, o, gon, lng, lnb, wsp, bsp)
        y_a, y_b = jnp.concatenate(ya, axis=1), jnp.concatenate(yb, axis=1)
        ma, mb = bdot(y_a, wpa_ref), bdot(y_b, wpb_ref)
        ga, gb = p_ref[:, 4 * D:5 * D].astype(F32), p_ref[:, 5 * D:6 * D].astype(F32)
        merged, vjp2 = jax.vjp(_stage2, ma, mb, ga, gb)
        out = bdot(merged, wout_ref)
        gate = m_ref[:, 2 * D:3 * D] + bm_ref[:, 2 * D:3 * D]
        loss, vjp3 = jax.vjp(_stage3, out, x_ref[...], t_ref[...], gp_ref[...], gate)
        loss_ref[...] += jnp.broadcast_to(loss, loss_ref.shape)

        dout, dx, _, dgpost, dgate = vjp3(jnp.ones((1, 1), F32))
        dx_ref[...] = dx
        dmerged = bdot_t(dout, wout_ref)
        dma, dmb, dga, dgb = vjp2(dmerged)
        dya, dyb = bdot_t(dma, wpa_ref), bdot_t(dmb, wpb_ref)
        dzb, dua, dva, dza, do, dgon, dlng, dlnb, dwsp, dbsp = vjp1(
            ([dya[:, lanes(g)] for g in range(H)], [dyb[:, lanes(g)] for g in range(H)]))

        for blk, dlist in enumerate((dzb, dua, dva, dza)):
            for g in range(H):
                dp_ref[:, blk * D + g * DH: blk * D + (g + 1) * DH] = dlist[g].astype(BF16)
        dp_ref[:, 4 * D:5 * D] = dga.astype(BF16)
        dp_ref[:, 5 * D:6 * D] = dgb.astype(BF16)
        for g in range(H):
            do_ref[g] = do[g]
            dwsp_ref[g] += dwsp[g]
            dbsp_ref[g] += dbsp[g]
            dvec_ref[2:3, lanes(g)] += dlng[g]
            dvec_ref[3:4, lanes(g)] += dlnb[g]
        dvec_ref[0:1, :] += dgpost
        dvec_ref[1:2, :] += dgate
        dgon_ref[0:1, :] += dgon
        ya_ref[...] = y_a.astype(BF16)
        yb_ref[...] = y_b.astype(BF16)
        mg_ref[...] = merged.astype(BF16)
        dma_ref[...] = dma.astype(BF16)
        dmb_ref[...] = dmb.astype(BF16)
        dout_ref[...] = dout.astype(BF16)

    row = lambda w: pl.BlockSpec((GM, w), lambda i: (i, 0))
    heads = pl.BlockSpec((H, GM, DH), lambda i: (0, i, 0))
    full = lambda shape: pl.BlockSpec(shape, lambda i: tuple(0 for _ in shape))
    sds = jax.ShapeDtypeStruct
    return pl.pallas_call(
        body, name=name, grid=(n,),
        in_specs=[row(6 * D), heads, heads, row(D), row(D), full((1, 3 * D)), full((1, 3 * D)), full((1, DH)),
                  full((1, D)), full((1, D)), full((H, GM, GM)), full((H, GM, GM)),
                  full((D, D)), full((D, D)), full((D, D)), full((1, D))],
        out_specs=[full((8, DH)), row(6 * D), heads, row(D)] + [row(D)] * 6
        + [full((8, D)), full((8, DH)), full((H, GM, GM)), full((H, GM, GM))],
        out_shape=[sds((8, DH), F32), sds((rows, 6 * D), BF16), sds((H, rows, DH), F32), sds((rows, D), F32)]
        + [sds((rows, D), BF16)] * 6 + [sds((8, D), F32), sds((8, DH), F32), sds((H, GM, GM), F32), sds((H, GM, GM), F32)],
        compiler_params=_params(("arbitrary",)),
    )(p_rest, o_f, o_b, x, tgt, mod, bmod, gon, lng, lnb, wsp, bspb, wpa, wpb, wout, gpost)


def _silu_rows(c, *, name):
    def body(c_ref, o_ref):
        o_ref[...] = _silu(c_ref[...])

    return pl.pallas_call(body, name=name, out_shape=jax.ShapeDtypeStruct(c.shape, F32))(c)


def _dsilu_mul(c, d, *, name):
    def body(c_ref, d_ref, o_ref):
        _, vjp = jax.vjp(_silu, c_ref[...])
        (o_ref[...],) = vjp(d_ref[...])

    return pl.pallas_call(body, name=name, out_shape=jax.ShapeDtypeStruct(c.shape, F32))(c, d)


def _adamw(w, g, m, v, *, name):
    rows, cols = w.shape[-2:]
    t = _tile(rows if g.ndim == 2 else g.shape[1])
    c1 = 1.0 / (1.0 - ADAM_B1 ** ADAM_STEP)
    c2 = 1.0 / (1.0 - ADAM_B2 ** ADAM_STEP)

    def body(w_ref, g_ref, m_ref, v_ref, go_ref, d_ref, mo_ref, vo_ref):
        blk = lambda r: r[...].reshape(t, cols)
        gv = blk(g_ref)
        mn = ADAM_B1 * blk(m_ref) + (1.0 - ADAM_B1) * gv
        vn = ADAM_B2 * blk(v_ref) + (1.0 - ADAM_B2) * (gv * gv)
        delta = -ADAM_LR * ((mn * c1) / (jnp.sqrt(vn * c2) + ADAM_EPS) + ADAM_WD * blk(w_ref))
        for ref, val in ((go_ref, gv), (d_ref, delta), (mo_ref, mn), (vo_ref, vn)):
            ref[...] = val.reshape(ref.shape)

    tile = (pl.BlockSpec((1, t, cols), lambda i: (0, i, 0)) if w.ndim == 3 else pl.BlockSpec((t, cols), lambda i: (i, 0)))
    if g.ndim == 3:
        per = g.shape[1] // t
        g_tile = pl.BlockSpec((1, t, cols), lambda i: (i // per, i % per, 0))
    else:
        g_tile = pl.BlockSpec((t, cols), lambda i: (i, 0))
    return pl.pallas_call(
        body, name=name, grid=(rows // t,),
        in_specs=[tile, g_tile, tile, tile], out_specs=[tile] * 4,
        out_shape=[jax.ShapeDtypeStruct(w.shape, F32)] * 4,
        compiler_params=_params(("parallel",)),
    )(w, g, m, v)


def _add_sibling(fulls, gots, where, out_dtype, *, name):
    rows, cols = fulls[0].shape[1:]
    hr = rows // 2
    t = _tile(hr)
    nt = hr // t
    first = [sum(a.shape[0] for a in fulls[:k]) for k in range(len(fulls) + 1)]
    n = len(fulls)

    def body(w_ref, *refs):
        o_ref = refs[-1]
        j = pl.program_id(0)
        for k in range(n):
            @pl.when((j >= first[k]) & (j < first[k + 1]))
            def _(k=k):
                o_ref[...] = (refs[k][...].astype(F32) + refs[n + k][...].astype(F32)).astype(out_dtype)

    def tile(k, row0):
        def index(j, i, w):
            rb = jnp.where(j < first[k], 0, jnp.where(j >= first[k + 1], nt - 1, i))
            return jnp.clip(j - first[k], 0, fulls[k].shape[0] - 1), row0(w) + rb, 0
        return pl.BlockSpec((1, t, cols), index)

    return pl.pallas_call(
        body, name=name,
        grid_spec=pltpu.PrefetchScalarGridSpec(
            num_scalar_prefetch=1, grid=(first[-1], nt),
            in_specs=[tile(k, lambda w: w[0] * nt) for k in range(n)] + [tile(k, lambda w: 0) for k in range(n)],
            out_specs=pl.BlockSpec((1, t, cols), lambda j, i, w: (j, i, 0))),
        out_shape=jax.ShapeDtypeStruct((first[-1], hr, cols), out_dtype),
        compiler_params=_params(("parallel", "parallel")),
    )(where, *fulls, *gots)


def _sum_chips(own, slots, where, n_out, which, *, name):
    _, hr, cols = own.shape
    t = _tile(hr)

    def body(w_ref, a_ref, s_ref, o_ref):
        f = lambda v: v.astype(F32)
        o_ref[0] = ((f(a_ref[0]) + f(s_ref[0])) + f(s_ref[1])) + f(s_ref[2])

    return pl.pallas_call(
        body, name=name,
        grid_spec=pltpu.PrefetchScalarGridSpec(
            num_scalar_prefetch=1, grid=(hr // t,),
            in_specs=[pl.BlockSpec((1, t, cols), lambda i, w: (w[1], i, 0)),
                      pl.BlockSpec((N_CHIPS - 1, t, cols), lambda i, w: (0, i, 0))],
            out_specs=pl.BlockSpec((1, t, cols), lambda i, w: (w[which], i, 0))),
        out_shape=jax.ShapeDtypeStruct((n_out, hr, cols), F32), compiler_params=_params(("parallel",)),
    )(where, own, slots)


def _local_step(x, c, ctx, c_ctx, tgt, w_mod, b_mod, g_pre, g_post, w_qkv, w_ab, w_rest, w_conv, a_log, dt_bias,
                g_onorm, gm_ln_g, gm_ln_b, w_sp, b_sp, late_weights, on_large_gradients=None, run_after=None):
    rows, rows_c = x.shape[0], ctx.shape[0]
    cc = jnp.zeros((16, D), F32).at[0].set(c[0]).at[1].set(c_ctx)
    scc = _silu_rows(cc, name="silu_cond")
    mod = _mm(scc, w_mod, b_sections=True, after=run_after, name="mod_fwd")
    mod_x, mod_c = mod[0:1], mod[1:2]
    avec = jnp.zeros((1, DH), F32).at[0, :2 * H].set(a_log.reshape(-1))
    dvec = jnp.zeros((1, DH), F32).at[0, :2 * H].set(dt_bias.reshape(-1))
    bspb = jnp.broadcast_to(b_sp[:, :, None], (H, GM, GM))
    w_kv, wconv_kv = w_qkv[:, D:], w_conv[:, D:]

    h_c = _norm_fwd(ctx, g_pre, mod_c, b_mod, name="norm_fwd_ctx")
    pc_kv = _mm(h_c, w_kv, name="inproj_ctx_kv")
    pc_ab = _mm(h_c, w_ab, name="inproj_ctx_ab")
    kv_c = _prep_fwd(pc_kv, wconv_kv, H, name="prep_fwd_ctx")
    gb_c = _gates_fwd(pc_ab, avec, dvec, name="gates_fwd_ctx")
    s_zero = jnp.zeros((H, DH, DH), F32)
    _, _, ssf_c, ssb_c, tsf_c, tsb_c, s_f, s_b = _gdn_fwd(kv_c, gb_c, s_zero, s_zero, False, name="gdn_fwd_ctx")

    h_x = _norm_fwd(x, g_pre, mod_x, b_mod, name="norm_fwd_x")
    p_qkv = _mm(h_x, w_qkv, tm=rows, tn=D // 2, name="inproj_qkv")
    p_ab = _mm(h_x, w_ab, name="inproj_ab")
    p_rest = _mm(h_x, w_rest, tm=rows, tn=D // 2, out_dtype=BF16, name="inproj_rest")
    qkv = _prep_fwd(p_qkv, w_conv, 2 * H, name="prep_fwd_x")
    gb_x = _gates_fwd(p_ab, avec, dvec, name="gates_fwd_x")
    o_f, o_b, ssf, ssb, tsf, tsb, _, _ = _gdn_fwd(qkv, gb_x, s_f, s_b, True, name="gdn_fwd_x")

    w_pa, w_pb, w_out = late_weights(o_f)
    (loss_acc, dp_rest, do, dx_res, ya, yb, mg, dma, dmb, dout, dvec_post, dgon, dwsp, dbspb) = _post(
        p_rest, o_f, o_b, x, tgt, mod_x, b_mod, g_onorm, gm_ln_g, gm_ln_b, w_sp, bspb, w_pa, w_pb, w_out, g_post,
        name="post")
    g = {}
    g["w_pa"] = _mm(ya, dma, ta=True, out_dtype=BF16, name="dw_pa")
    g["w_pb"] = _mm(yb, dmb, ta=True, out_dtype=BF16, name="dw_pb")
    g["w_out"] = _mm(mg, dout, ta=True, out_dtype=BF16, name="dw_out")

    zeros_s = jnp.zeros((H, DH, DH), F32)
    dq_f, dq_b, dg_f, dg_b, ds0_f, ds0_b = _gdn_bwd(qkv, gb_x, ssf, ssb, tsf, tsb, do, zeros_s, zeros_s, True,
                                                    name="gdn_bwd_x")
    dp_qkv, dwc_x = _prep_bwd(p_qkv, w_conv, dq_f, dq_b, 2 * H, name="prep_bwd_x")
    dp_ab, dav_x, ddv_x = _gates_bwd(p_ab, avec, dvec, dg_f, dg_b, name="gates_bwd_x")
    dkv_f, dkv_b, dgc_f, dgc_b, _, _ = _gdn_bwd(kv_c, gb_c, ssf_c, ssb_c, tsf_c, tsb_c, jnp.zeros((H, rows_c, DH), F32),
                                                 ds0_f, ds0_b, False, name="gdn_bwd_ctx")
    dpc_kv, dwc_c = _prep_bwd(pc_kv, wconv_kv, dkv_f, dkv_b, H, name="prep_bwd_ctx")
    dpc_ab, dav_c, ddv_c = _gates_bwd(pc_ab, avec, dvec, dgc_f, dgc_b, name="gates_bwd_ctx")

    dw_kv_c = _mm(h_c, dpc_kv, ta=True, name="dw_kv_ctx")
    dw_qkv = _mm(h_x, dp_qkv, ta=True, tn=D, tk=2 * D, add=dw_kv_c, add_from=1, out_dtype=BF16, name="dw_qkv")
    dw_ab_c = _mm(h_c, dpc_ab, ta=True, name="dw_ab_ctx")
    dw_ab = _mm(h_x, dp_ab, ta=True, add=dw_ab_c, out_dtype=BF16, name="dw_ab")
    dw_rest = _mm(h_x, dp_rest, ta=True, tn=2 * D, tk=2 * D, out_dtype=BF16, name="dw_rest")
    g["w_in"] = (dw_qkv, dw_ab[:, :4 * H], dw_rest)
    token, scatter = on_large_gradients(g) if on_large_gradients is not None else (None, None)
    dh = _mm(dp_ab, w_ab, tb=True, after=token, name="dh_ab")
    dh = _mm(dp_qkv, w_qkv, tb=True, tk=3 * D, add=dh, after=token, name="dh_qkv")
    token = scatter(dh) if scatter is not None else None
    dh = _mm(dp_rest, w_rest, tb=True, tk=3 * D, add=dh, after=token, name="dh_rest")
    grad_x, dgpre_x, dm_x = _norm_bwd(x, g_pre, mod_x, b_mod, [dh], dx_res, name="norm_bwd_x")
    dh_c = _mm(dpc_ab, w_ab, tb=True, after=token, name="dhc_ab")
    dh_c = _mm(dpc_kv, w_kv, tb=True, add=dh_c, after=token, name="dhc_kv")
    _, dgpre_c, dm_c = _norm_bwd(ctx, g_pre, mod_c, b_mod, [dh_c], None, name="norm_bwd_ctx")

    dm_x = dm_x.at[:, 2 * D:].add(dvec_post[1:2])
    dmod = jnp.zeros((16, 3 * D), F32).at[0].set(dm_x[0]).at[1].set(dm_c[0])
    g["mod_factors"] = (scc[0], dm_x[0])
    g["dm_ctx"] = dm_c[0]
    g["silu_c_ctx"] = scc[1:2]
    dcc = _mm(dmod, w_mod, tb=True, b_sections=True, name="dcc")
    g["c_ctx"] = _dsilu_mul(cc[:8], dcc[:8], name="dc_ctx")[1]
    g["b_mod"] = dm_x + dm_c
    g["g_pre"] = dgpre_x + dgpre_c
    g["g_post"] = dvec_post[0:1]
    g["gm_ln_g"], g["gm_ln_b"] = dvec_post[2:3], dvec_post[3:4]
    g["g_onorm"] = dgon[0:1]
    g["w_sp"] = dwsp
    g["b_sp"] = jnp.sum(dbspb, axis=-1)
    g["w_conv"] = dwc_x.at[:, D:].add(dwc_c)
    g["a_log"] = (dav_x + dav_c)[0, :2 * H].reshape(2, H)
    g["dt_bias"] = (ddv_x + ddv_c)[0, :2 * H].reshape(2, H)
    return loss_acc[0, 0], grad_x, g


ANY = pl.BlockSpec(memory_space=pl.ANY)


def _place():
    x, y, c = lax.axis_index("x"), lax.axis_index("y"), lax.axis_index("c")
    chips = [(1 - x, y), (x, 1 - y), (1 - x, 1 - y)]
    return x, y, c, (x, y, 1 - c), chips


def _gather_shards(big, small, *, name):
    nb = len(big)

    def body(*refs):
        ins, sm_in = refs[:nb], refs[nb]
        outs, sm_out = refs[nb + 1:2 * nb + 1], refs[2 * nb + 1]
        send, recv = refs[2 * nb + 2:]
        x, y, c, sibling, chips = _place()
        mine = 2 * x + y

        def half(a, shard, hc):
            hr = big[a].shape[0] // 2
            return outs[a].at[shard, pl.ds(hc * hr, hr), :]

        def remote(k, src, dst, to):
            return pltpu.make_async_remote_copy(src_ref=src, dst_ref=dst, send_sem=send.at[k], recv_sem=recv.at[k],
                                                device_id=to, device_id_type=MESH)

        sends = []
        for a in range(nb):
            hr = big[a].shape[0] // 2
            for j, chip in enumerate(chips):
                sends.append(remote(a * 3 + j, ins[a].at[pl.ds(c * hr, hr), :], half(a, mine, c), (*chip, c)))
        for j, chip in enumerate(chips):
            sends.append(remote(nb * 3 + j, sm_in, sm_out.at[mine], (*chip, c)))
        for cp in sends:
            cp.start()
        base = nb * 3 + 3
        passed = []
        for a in range(nb):
            for j, (px, py) in enumerate(chips):
                theirs = 2 * px + py
                remote(a * 3 + j, half(a, theirs, c), half(a, theirs, c), sibling).wait_recv()
                fw = remote(base + a * 3 + j, half(a, theirs, c), half(a, theirs, c), sibling)
                fw.start()
                passed.append(fw)
        for a in range(nb):
            for j, (px, py) in enumerate(chips):
                theirs = 2 * px + py
                remote(base + a * 3 + j, half(a, theirs, 1 - c), half(a, theirs, 1 - c), sibling).wait_recv()
        for j, (px, py) in enumerate(chips):
            remote(nb * 3 + j, sm_in, sm_out.at[2 * px + py], sibling).wait_recv()
        for cp in sends + passed:
            cp.wait_send()

    n_remote = 2 * nb * 3 + 3
    outs = pl.pallas_call(
        body, name=name, in_specs=[ANY] * (nb + 1), out_specs=[ANY] * (nb + 1),
        out_shape=[jax.ShapeDtypeStruct((N_CHIPS,) + a.shape, a.dtype) for a in big + [small]],
        scratch_shapes=[pltpu.SemaphoreType.DMA((n_remote,)), pltpu.SemaphoreType.DMA((n_remote,))],
    )(*big, small)
    return outs[:nb], outs[nb]


def _gather_late_start(arrs, after, *, name):
    na = len(arrs)

    def body(*refs):
        ins, land = refs[:na], refs[na:2 * na]
        send, recv = refs[2 * na + 1], refs[2 * na + 2]
        token = refs[-1]
        x, y, c, _, chips = _place()
        for a in range(na):
            hr = arrs[a].shape[0] // 2
            for j, (px, py) in enumerate(chips):
                for other in range(2):
                    k = (a * 3 + j) * 2 + other
                    _remote(ins[a].at[pl.ds(c * hr, hr), :], land[a].at[2 * x + y, pl.ds(c * hr, hr), :],
                            send.at[k], recv.at[k], (px, py, c ^ other)).start()
        token[...] = jnp.zeros_like(token)

    lands = [((N_CHIPS,) + a.shape, a.dtype) for a in arrs]
    held = lambda a: pltpu.with_memory_space_constraint(a, pltpu.HBM)
    outs = pl.pallas_call(
        body, name=name, in_specs=[HBM] * (2 * na) + [ANY],
        out_specs=[SEM, SEM] + [HBM] * (2 * na) + [pl.BlockSpec(memory_space=pltpu.VMEM)],
        out_shape=[pltpu.SemaphoreType.DMA((6 * na,)), pltpu.SemaphoreType.DMA((6 * na,))]
        + [pltpu.HBM(a.shape, a.dtype) for a in arrs] + [pltpu.HBM(s, d) for s, d in lands]
        + [jax.ShapeDtypeStruct((8, 128), F32)],
        input_output_aliases={i: 2 + i for i in range(2 * na)},
        compiler_params=pltpu.CompilerParams(has_side_effects=pltpu.SideEffectType.DATAFLOW_SIDE_EFFECTING),
    )(*[held(a) for a in arrs], *[held(lax.empty(s, d)) for s, d in lands], after)
    return outs[0], outs[1], outs[2:2 + na], outs[2 + na:2 + 2 * na], outs[-1]


def _gather_late_wait(send, recv, thru, land, after, *, name):
    na = len(thru)

    def body(*refs):
        ins, slots = refs[:na], refs[na:2 * na]
        send_sem, recv_sem = refs[2 * na], refs[2 * na + 1]
        x, y, c, _, chips = _place()
        for a in range(na):
            hr = thru[a].shape[0] // 2
            for j, (px, py) in enumerate(chips):
                for other in range(2):
                    k = (a * 3 + j) * 2 + other
                    half = c ^ other
                    cp = _remote(ins[a].at[pl.ds(c * hr, hr), :], slots[a].at[2 * px + py, pl.ds(half * hr, hr), :],
                                 send_sem.at[k], recv_sem.at[k], (px, py, half))
                    cp.wait_send()
                    cp.wait_recv()

    outs = pl.pallas_call(
        body, name=name, in_specs=[HBM] * (2 * na) + [SEM, SEM, ANY], out_specs=[HBM] * (2 * na),
        out_shape=[pltpu.HBM(a.shape, a.dtype) for a in list(thru) + list(land)],
        input_output_aliases={i: i for i in range(2 * na)},
        compiler_params=pltpu.CompilerParams(has_side_effects=pltpu.SideEffectType.DATAFLOW_SIDE_EFFECTING),
    )(*thru, *land, send, recv, after)
    return outs[na:]


def _row_chunks(rows, row_bytes, align=8):
    n = max(1, min(rows // align, -(-rows * row_bytes // DMA_CHUNK_BYTES)))
    per = -(-(-(-rows // n)) // align) * align
    return [(r, min(per, rows - r)) for r in range(0, rows, per)]


def _remote(src, dst, send, recv, to):
    return pltpu.make_async_remote_copy(src_ref=src, dst_ref=dst, send_sem=send, recv_sem=recv, device_id=to,
                                        device_id_type=MESH)


def _exchange_halves(arrs, *, name):
    na = len(arrs)

    def body(*refs):
        ins, got = refs[:na], refs[na:2 * na]
        send, recv = refs[2 * na:]
        x, y, c, sibling, _ = _place()
        for a in range(na):
            ns, rows, cols = arrs[a].shape
            hr = rows // 2
            for s in range(ns):
                for r0, nr in _row_chunks(hr, cols * arrs[a].dtype.itemsize, 16):
                    _remote(ins[a].at[s, pl.ds((1 - c) * hr + r0, nr), :], got[a].at[s, pl.ds(r0, nr), :],
                            send.at[a], recv.at[a], sibling).start()
        for a in range(na):
            hr = arrs[a].shape[1] // 2
            _remote(ins[a].at[:, pl.ds((1 - c) * hr, hr), :], got[a], send.at[a], recv.at[a], sibling).wait()

    return pl.pallas_call(
        body, name=name, in_specs=[ANY] * na, out_specs=[ANY] * na,
        out_shape=[jax.ShapeDtypeStruct((a.shape[0], a.shape[1] // 2, a.shape[2]), a.dtype) for a in arrs],
        scratch_shapes=[pltpu.SemaphoreType.DMA((na,)), pltpu.SemaphoreType.DMA((na,))],
    )(*arrs)


HBM = pl.BlockSpec(memory_space=pltpu.HBM)
SEM = pl.BlockSpec(memory_space=pltpu.SEMAPHORE)


def _scatter_start(arrs, *, name):
    na = len(arrs)

    def body(*refs):
        ins, land = refs[:na], refs[na:2 * na]
        send, recv = refs[2 * na], refs[2 * na + 1]
        token = refs[-1]
        x, y, c, _, chips = _place()
        for a in range(na):
            _, hr, cols = arrs[a].shape
            for j, (px, py) in enumerate(chips):
                for r0, nr in _row_chunks(hr, cols * arrs[a].dtype.itemsize, 16):
                    _remote(ins[a].at[2 * px + py, pl.ds(r0, nr), :], land[a].at[j, pl.ds(r0, nr), :],
                            send.at[a * 3 + j], recv.at[a * 3 + j], (px, py, c)).start()
        token[...] = jnp.zeros_like(token)

    slots = [((N_CHIPS - 1,) + a.shape[1:], a.dtype) for a in arrs]
    held = lambda a: pltpu.with_memory_space_constraint(a, pltpu.HBM)
    outs = pl.pallas_call(
        body, name=name, in_specs=[HBM] * (2 * na),
        out_specs=[SEM, SEM] + [HBM] * (2 * na) + [pl.BlockSpec(memory_space=pltpu.VMEM)],
        out_shape=[pltpu.SemaphoreType.DMA((3 * na,)), pltpu.SemaphoreType.DMA((3 * na,))]
        + [pltpu.HBM(a.shape, a.dtype) for a in arrs] + [pltpu.HBM(s, d) for s, d in slots]
        + [jax.ShapeDtypeStruct((8, 128), F32)],
        input_output_aliases={i: 2 + i for i in range(2 * na)},
        compiler_params=pltpu.CompilerParams(has_side_effects=pltpu.SideEffectType.DATAFLOW_SIDE_EFFECTING),
    )(*[held(a) for a in arrs], *[held(lax.empty(s, d)) for s, d in slots])
    return outs[0], outs[1], outs[2:2 + na], outs[2 + na:2 + 2 * na], outs[-1]


def _scatter_wait(send, recv, thru, land, after, *, name):
    na = len(thru)

    def body(*refs):
        ins, slots = refs[:na], refs[na:2 * na]
        send_sem, recv_sem = refs[2 * na], refs[2 * na + 1]
        x, y, c, _, chips = _place()
        for a in range(na):
            for j, (px, py) in enumerate(chips):
                cp = _remote(ins[a].at[2 * px + py], slots[a].at[j], send_sem.at[a * 3 + j], recv_sem.at[a * 3 + j],
                             (px, py, c))
                cp.wait_send()
                cp.wait_recv()

    outs = pl.pallas_call(
        body, name=name, in_specs=[HBM] * (2 * na) + [SEM, SEM, ANY], out_specs=[HBM] * (2 * na),
        out_shape=[pltpu.HBM(a.shape, a.dtype) for a in list(thru) + list(land)],
        input_output_aliases={i: i for i in range(2 * na)},
        compiler_params=pltpu.CompilerParams(has_side_effects=pltpu.SideEffectType.DATAFLOW_SIDE_EFFECTING),
    )(*thru, *land, send, recv, after)
    return outs[:na], outs[na:]


def _exchange_start(arrs, *, name):
    na = len(arrs)

    def body(*refs):
        ins, got = refs[:na], refs[na:2 * na]
        send, recv = refs[2 * na], refs[2 * na + 1]
        token = refs[-1]
        x, y, c, sibling, _ = _place()
        for a in range(na):
            ns, rows, cols = arrs[a].shape
            hr = rows // 2
            for s in range(ns):
                for r0, nr in _row_chunks(hr, cols * arrs[a].dtype.itemsize, 16):
                    _remote(ins[a].at[s, pl.ds((1 - c) * hr + r0, nr), :], got[a].at[s, pl.ds(r0, nr), :],
                            send.at[a], recv.at[a], sibling).start()
        token[...] = jnp.zeros_like(token)

    lands = [((a.shape[0], a.shape[1] // 2, a.shape[2]), a.dtype) for a in arrs]
    held = lambda a: pltpu.with_memory_space_constraint(a, pltpu.HBM)
    outs = pl.pallas_call(
        body, name=name, in_specs=[HBM] * (2 * na),
        out_specs=[SEM, SEM] + [HBM] * (2 * na) + [pl.BlockSpec(memory_space=pltpu.VMEM)],
        out_shape=[pltpu.SemaphoreType.DMA((na,)), pltpu.SemaphoreType.DMA((na,))]
        + [pltpu.HBM(a.shape, a.dtype) for a in arrs] + [pltpu.HBM(s, d) for s, d in lands]
        + [jax.ShapeDtypeStruct((8, 128), F32)],
        input_output_aliases={i: 2 + i for i in range(2 * na)},
        compiler_params=pltpu.CompilerParams(has_side_effects=pltpu.SideEffectType.DATAFLOW_SIDE_EFFECTING),
    )(*[held(a) for a in arrs], *[held(lax.empty(s, d)) for s, d in lands])
    return outs[0], outs[1], outs[2:2 + na], outs[2 + na:2 + 2 * na], outs[-1]


def _exchange_wait(send, recv, thru, got, after, *, name):
    na = len(thru)

    def body(*refs):
        ins, lands = refs[:na], refs[na:2 * na]
        send_sem, recv_sem = refs[2 * na], refs[2 * na + 1]
        x, y, c, sibling, _ = _place()
        for a in range(na):
            hr = thru[a].shape[1] // 2
            cp = _remote(ins[a].at[:, pl.ds((1 - c) * hr, hr), :], lands[a], send_sem.at[a], recv_sem.at[a],
                         sibling)
            cp.wait_send()
            cp.wait_recv()

    outs = pl.pallas_call(
        body, name=name, in_specs=[HBM] * (2 * na) + [SEM, SEM, ANY], out_specs=[HBM] * (2 * na),
        out_shape=[pltpu.HBM(a.shape, a.dtype) for a in list(thru) + list(got)],
        input_output_aliases={i: i for i in range(2 * na)},
        compiler_params=pltpu.CompilerParams(has_side_effects=pltpu.SideEffectType.DATAFLOW_SIDE_EFFECTING),
    )(*thru, *got, send, recv, after)
    return outs[:na], outs[na:]


def _finish_reduce(big, small, *, name):
    nb = len(big)

    def body(*refs):
        outs, sm = refs[nb + 1:2 * nb + 1], refs[2 * nb + 1]
        send, recv = refs[2 * nb + 2:]
        x, y, c, sibling, chips = _place()
        blk = lambda px, py, pc: sm.at[4 * px + 2 * py + pc]
        for a in range(nb):
            _, hr, cols = big[a].shape
            for r0, nr in _row_chunks(hr, cols * 4):
                _remote(outs[a].at[c, pl.ds(r0, nr), :], outs[a].at[c, pl.ds(r0, nr), :], send.at[a], recv.at[a],
                        sibling).start()
        first = [_remote(blk(x, y, c), blk(x, y, c), send.at[nb], recv.at[nb], sibling)]
        first += [_remote(blk(x, y, c), blk(x, y, c), send.at[nb + 1 + j], recv.at[nb + 1 + j], (*chip, c))
                  for j, chip in enumerate(chips)]
        for cp in first:
            cp.start()
        passed = []
        for j, (px, py) in enumerate(chips):
            _remote(blk(px, py, c), blk(px, py, c), send.at[nb + 1 + j], recv.at[nb + 1 + j], sibling).wait_recv()
            fw = _remote(blk(px, py, c), blk(px, py, c), send.at[nb + 4 + j], recv.at[nb + 4 + j], sibling)
            fw.start()
            passed.append(fw)
        for a in range(nb):
            _remote(outs[a].at[c], outs[a].at[1 - c], send.at[a], recv.at[a], sibling).wait()
        _remote(blk(x, y, c), blk(x, y, 1 - c), send.at[nb], recv.at[nb], sibling).wait_recv()
        for j, (px, py) in enumerate(chips):
            _remote(blk(px, py, c), blk(px, py, 1 - c), send.at[nb + 4 + j], recv.at[nb + 4 + j], sibling).wait_recv()
        for cp in first + passed:
            cp.wait_send()

    n_remote = nb + 7
    arrs = list(big) + [small]
    outs = pl.pallas_call(
        body, name=name, in_specs=[ANY] * (nb + 1), out_specs=[ANY] * (nb + 1),
        out_shape=[jax.ShapeDtypeStruct(a.shape, F32) for a in arrs],
        input_output_aliases={i: i for i in range(nb + 1)},
        scratch_shapes=[pltpu.SemaphoreType.DMA((n_remote,)), pltpu.SemaphoreType.DMA((n_remote,))],
    )(*arrs)
    return outs[:nb], outs[nb]


def _reduce_start(sectioned, where):
    counts = [len(parts) for parts in sectioned]
    first = [sum(counts[:i]) for i in range(len(sectioned) + 1)]
    send, recv, thru, got, token = _exchange_start([a for parts in sectioned for a in parts], name="rs_exchange_start")

    def resume(after):
        mine, theirs = _exchange_wait(send, recv, thru, got, after, name="rs_exchange_wait")
        chip_sum = [_add_sibling(mine[first[i]:first[i + 1]], theirs[first[i]:first[i + 1]], where, BF16,
                                 name=f"rs_add_sibling_{i}") for i in range(len(sectioned))]
        s_send, s_recv, s_thru, s_land, s_token = _scatter_start(chip_sum, name="rs_scatter_start")
        return (s_send, s_recv, s_thru, s_land), s_token

    return token, resume


def _reduce_finish(handle, small, where):
    send, recv, thru, land = handle
    got = _exchange_halves([small], name="rs_exchange_small")
    small_sum = _add_sibling([small], got, where, F32, name="rs_add_sibling_small")
    sm_send, sm_recv, sm_thru, sm_land, sm_token = _scatter_start([small_sum], name="rs_scatter_small_start")
    own, slots = _scatter_wait(send, recv, thru, land, sm_token, name="rs_scatter_wait")
    red = [_sum_chips(p, s, where, 2, 0, name=f"rs_sum_chips_{i}") for i, (p, s) in enumerate(zip(own, slots))]
    (small_sum,), (small_slots,) = _scatter_wait(sm_send, sm_recv, sm_thru, sm_land, red[0], name="rs_scatter_small_wait")
    red_small = _sum_chips(small_sum, small_slots, where, 2 * N_CHIPS, 2, name="rs_sum_chips_small")
    big, sm = _finish_reduce(red, red_small, name="rs_finish")
    return big, sm.reshape(-1, sm.shape[-1])


def kernel(x, c, ctx, c_ctx, w_mod, b_mod, g_pre, g_post, w_in, w_conv, a_log, dt_bias, g_onorm, gm_ln_g, gm_ln_b, w_sp, b_sp, w_pa, w_pb, w_out, loss_target, m_c_ctx, m_w_mod, m_b_mod, m_g_pre, m_g_post, m_w_in, m_w_conv, m_a_log, m_dt_bias, m_g_onorm, m_gm_ln_g, m_gm_ln_b, m_w_sp, m_b_sp, m_w_pa, m_w_pb, m_w_out, v_c_ctx, v_w_mod, v_b_mod, v_g_pre, v_g_post, v_w_in, v_w_conv, v_a_log, v_dt_bias, v_g_onorm, v_gm_ln_g, v_gm_ln_b, v_w_sp, v_b_sp, v_w_pa, v_w_pb, v_w_out):
    names = ["c_ctx", "w_mod", "b_mod", "g_pre", "g_post", "w_in", "w_conv", "a_log", "dt_bias", "g_onorm", "gm_ln_g",
             "gm_ln_b", "w_sp", "b_sp", "w_pa", "w_pb", "w_out"]
    w = dict(zip(names, (c_ctx, w_mod, b_mod, g_pre, g_post, w_in, w_conv, a_log, dt_bias, g_onorm, gm_ln_g, gm_ln_b,
                         w_sp, b_sp, w_pa, w_pb, w_out)))
    m = dict(zip(names, (m_c_ctx, m_w_mod, m_b_mod, m_g_pre, m_g_post, m_w_in, m_w_conv, m_a_log, m_dt_bias, m_g_onorm,
                         m_gm_ln_g, m_gm_ln_b, m_w_sp, m_b_sp, m_w_pa, m_w_pb, m_w_out)))
    v = dict(zip(names, (v_c_ctx, v_w_mod, v_b_mod, v_g_pre, v_g_post, v_w_in, v_w_conv, v_a_log, v_dt_bias, v_g_onorm,
                         v_gm_ln_g, v_gm_ln_b, v_w_sp, v_b_sp, v_w_pa, v_w_pb, v_w_out)))
    xy = 2 * lax.axis_index("x") + lax.axis_index("y")
    where = jnp.stack([lax.axis_index("c"), xy, 2 * xy + lax.axis_index("c")]).astype(jnp.int32)

    shards = [a[0].astype(BF16) for a in (w_mod, w_in)]
    gathered, wconv_all = _gather_shards(shards, w_conv[0], name="gather_weights")
    own = lambda full, shard: lax.dynamic_update_slice(full, shard[None], (xy, 0, 0))
    wm_all, win_all = [own(f, s) for f, s in zip(gathered, shards)]
    wconv_all = own(wconv_all, w_conv[0])
    late_shards = [a[0].astype(BF16) for a in (w_pa, w_pb, w_out)]
    late = _gather_late_start(late_shards, wconv_all, name="gather_late_start")

    def late_weights(after):
        lands = _gather_late_wait(*late[:4], after, name="gather_late_wait")
        return [own(f, s).reshape(D, D) for f, s in zip(lands, late_shards)]
    w_conv_f = jnp.concatenate([wconv_all[s] for s in range(N_CHIPS)], axis=1)
    shard_cols = IN_COLS // N_CHIPS
    cut = OFF_A - shard_cols
    assert 0 < cut and cut + 4 * H < shard_cols
    w_qkv = jnp.concatenate([win_all[0], win_all[1][:, :cut]], axis=1)
    w_ab = jnp.pad(win_all[1][:, cut:cut + 4 * H], ((0, 0), (0, DH - 4 * H)))
    w_rest = jnp.concatenate([win_all[1][:, cut + 4 * H:], win_all[2], win_all[3]], axis=1)

    blk = D // N_CHIPS
    rest_cut = shard_cols - cut - 4 * H
    big_names = ("w_in", "w_pa", "w_pb", "w_out")
    in_flight = []

    def start_reduce(grads):
        dw_qkv, dw_ab, dw_rest = grads["w_in"]
        g_win = [dw_qkv[:, :shard_cols],
                 jnp.concatenate([dw_qkv[:, shard_cols:], dw_ab, dw_rest[:, :rest_cut]], axis=1),
                 dw_rest[:, rest_cut:rest_cut + shard_cols], dw_rest[:, rest_cut + shard_cols:]]
        sectioned = [[a.astype(BF16)[None] for a in g_win]]
        sectioned += [[grads[k].reshape(N_CHIPS, blk, D)] for k in ("w_pa", "w_pb", "w_out")]
        token, resume = _reduce_start(sectioned, where)

        def scatter(after):
            handle, scatter_token = resume(after)
            in_flight.append(handle)
            return scatter_token

        return token, scatter

    loss_local, grad_x, g = _local_step(
        x[0], c, ctx[0], c_ctx, loss_target[0], wm_all, b_mod, g_pre, g_post, w_qkv, w_ab, w_rest,
        w_conv_f, a_log[0], dt_bias[0], g_onorm, gm_ln_g, gm_ln_b, w_sp[0], b_sp[0],
        late_weights, on_large_gradients=start_reduce, run_after=late[4])
    g["loss"] = loss_local
    g["_pad"] = jnp.zeros((SMALL_LAYOUT["_pad"][1],), F32)
    g["mod_factors"] = lax.dynamic_update_slice(jnp.zeros((2 * N_CHIPS, 4 * D), F32),
                                                jnp.concatenate(g["mod_factors"])[None], (where[2], 0))
    tail = jnp.zeros((N_CHIPS * SMALL_ROWS * 128 - sum(s for _, s in SMALL_LAYOUT.values()),), F32)
    flat = jnp.concatenate([g[k].reshape(-1) for k in SMALL_LAYOUT] + [tail])
    reduced, gr_small = _reduce_finish(in_flight[0], flat.reshape(N_CHIPS, SMALL_ROWS, 128), where)
    gr_tiny = gr_small[TINY_ROW0:]

    def entry(arr, k):
        off, size = SMALL_LAYOUT[k]
        row, col = off // 128 - TINY_ROW0, off % 128
        return arr[row:row + size // 128].reshape(-1) if size >= 128 else arr[row, col:col + size]

    factors = entry(gr_tiny, "mod_factors").reshape(2 * N_CHIPS, 4 * D)
    pad_rows = lambda a: jnp.pad(a, ((0, 16 - a.shape[0]), (0, 0)))
    lhs = pad_rows(jnp.concatenate([factors[:, :D], g["silu_c_ctx"]], axis=0))
    rhs = pad_rows(jnp.concatenate([factors[:, D:], entry(gr_tiny, "dm_ctx")[None]], axis=0))
    mod_cols = 3 * D // N_CHIPS
    gr_wm = _mm(lhs, lax.dynamic_slice(rhs, (0, xy * mod_cols), (16, mod_cols)), ta=True, name="dw_mod")

    res = {k: _adamw(w[k], gr, m[k], v[k], name=f"adamw_{k}") for k, gr in zip(big_names[1:], reduced[1:])}
    flip = lambda a: jnp.swapaxes(a, 1, 2)
    res["w_in"] = [flip(r) for r in _adamw(flip(w_in), flip(reduced[0].reshape(w_in.shape)), flip(m_w_in), flip(v_w_in),
                                           name="adamw_w_in")]
    res["w_mod"] = _adamw(w_mod[0], gr_wm, m_w_mod[0], v_w_mod[0], name="adamw_w_mod")
    res["w_sp"] = _adamw(w_sp.reshape(-1, 128), gr_small, m_w_sp.reshape(-1, 128), v_w_sp.reshape(-1, 128),
                         name="adamw_w_sp")
    tiny = [k for k, (off, _) in SMALL_LAYOUT.items() if TINY_ROW0 <= off // 128 < TINY_ROW0 + TINY_ROWS]
    tiny_names = [k for k in tiny if k not in ("loss", "_pad")]

    def pack(src):
        parts = [src[k].reshape(-1) if k in tiny_names else jnp.zeros((SMALL_LAYOUT[k][1],), F32) for k in tiny]
        return jnp.concatenate(parts).reshape(TINY_ROWS, 128)

    tiny_res = _adamw(pack(w), gr_tiny[:TINY_ROWS], pack(m), pack(v), name="adamw_tiny")
    for k in tiny_names:
        res[k] = [entry(r, k) for r in tiny_res]
    g_conv = lax.dynamic_slice(entry(gr_tiny, "w_conv").reshape(3, 3 * D), (0, xy * (3 * D // N_CHIPS)),
                               (3, 3 * D // N_CHIPS))
    conv_res = _adamw(jnp.pad(w_conv[0], ((0, 5), (0, 0))), jnp.pad(g_conv, ((0, 5), (0, 0))),
                      jnp.pad(m_w_conv[0], ((0, 5), (0, 0))), jnp.pad(v_w_conv[0], ((0, 5), (0, 0))), name="adamw_w_conv")
    res["w_conv"] = [r[:3] for r in conv_res]
    res = {k: [r.reshape(w[k].shape) for r in res[k]] for k in names}

    out = [entry(gr_tiny, "loss").reshape(()), grad_x[None]]
    for i in range(4):
        out += [res[k][i] for k in names]
    return tuple(out)
```

```python
import functools

import jax
import jax.numpy as jnp
from jax import lax
from jax.experimental import pallas as pl
from jax.experimental.pallas import tpu as pltpu

F32 = jnp.float32
BF16 = jnp.bfloat16
HI = lax.Precision.HIGHEST
MESH = pl.DeviceIdType.MESH

D = 1024
H = 8
DH = 128
CH = 64
LOG_CH = 6
PAIR = 2 * CH
GM = 128
assert 1 << LOG_CH == CH and PAIR == DH
PREC_POWERS = ((lax.Precision.HIGH, lax.Precision.HIGH),) * 3 + ((None, None),) * 2
assert len(PREC_POWERS) == LOG_CH - 1
EPS = 1e-6
N_CHIPS = 4
OFF_A = 3 * D
OFF_ZB = OFF_A + 4 * H
IN_COLS = OFF_ZB + 6 * D
VMEM_LIMIT_V7X = 56 * 1024 * 1024
DMA_CHUNK_BYTES = 2 * 1024 * 1024

ADAM_LR, ADAM_B1, ADAM_B2, ADAM_EPS, ADAM_WD, ADAM_STEP = 0.001, 0.9, 0.999, 1e-08, 0.01, 10

SMALL_LAYOUT = {}
_off = 0
for _n, _s in (("w_sp", H * GM * GM), ("c_ctx", D), ("b_mod", 3 * D), ("g_pre", D), ("g_post", D), ("gm_ln_g", D),
               ("gm_ln_b", D), ("b_sp", H * GM), ("g_onorm", DH), ("a_log", 2 * H), ("dt_bias", 2 * H), ("loss", 1),
               ("_pad", 128 - 4 * H - 1 + 6 * 128), ("w_conv", 3 * 3 * D), ("dm_ctx", 3 * D), ("mod_factors", 8 * 4 * D)):
    SMALL_LAYOUT[_n] = (_off, _s)
    _off += _s
SMALL_ROWS = 368
assert N_CHIPS * SMALL_ROWS * 128 >= _off and (SMALL_ROWS // 2) % 8 == 0 and _off % 128 == 0
TINY_ROW0 = SMALL_LAYOUT["c_ctx"][0] // 128
TINY_ROWS = SMALL_LAYOUT["w_conv"][0] // 128 - TINY_ROW0
assert TINY_ROWS % 8 == 0


def _params(sem=None):
    return pltpu.CompilerParams(dimension_semantics=sem, vmem_limit_bytes=VMEM_LIMIT_V7X)


def _tile(n, cands=(256, 128, 64, 32, 16, 8)):
    if n <= cands[0]:
        return n
    for cand in cands:
        if n % cand == 0 and cand >= 64:
            return cand
    return max(d for d in range(8, cands[0], 8) if n % d == 0)


def _silu(x):
    return x * jax.nn.sigmoid(x)


def _gelu(x):
    return 0.5 * x * (1.0 + jnp.tanh(0.7978845608028654 * (x + 0.044715 * (x * x * x))))


def _mm(a, b, *, ta=False, tb=False, out_dtype=F32, tm=1024, tn=1024, tk=1024, add=None, add_from=0,
        b_sections=False, out_sections=False, after=None, name):
    m, k = (a.shape[1], a.shape[0]) if ta else a.shape
    if b_sections:
        sect = b.shape[2]
        n = b.shape[1] if tb else b.shape[0] * sect
        tn, tk = (tn, sect) if tb else (sect, tk)
    else:
        n = b.shape[0] if tb else b.shape[1]
    tm, tn, tk = min(tm, m), min(tn, n), min(tk, k)
    assert m % tm == 0 and n % tn == 0 and k % tk == 0, (name, m, n, k, tm, tn, tk)
    nk = k // tk
    dims = (((0,) if ta else (1,), (1,) if tb else (0,)), ((), ()))
    has_add = add is not None
    assert not has_add or add.shape == (m, n - add_from * tn), (name, add.shape)

    def body(*refs):
        a_ref, b_ref = refs[:2]
        o_ref = refs[2 + int(has_add) + int(after is not None)]
        acc_ref = refs[-1]
        kk = pl.program_id(2)
        bv = b_ref[0] if b_sections else b_ref[...]
        part = lax.dot_general(a_ref[...].astype(BF16), bv.astype(BF16), dims, preferred_element_type=F32)

        def finish(res):
            if has_add:
                res = res + jnp.where(pl.program_id(1) >= add_from, refs[2][...], 0.0)
            if out_sections:
                o_ref[0] = res.astype(out_dtype)
            else:
                o_ref[...] = res.astype(out_dtype)

        if nk == 1:
            finish(part)
            return

        @pl.when(kk == 0)
        def _():
            acc_ref[...] = part

        @pl.when((kk > 0) & (kk < nk - 1))
        def _():
            acc_ref[...] += part

        @pl.when(kk == nk - 1)
        def _():
            finish(acc_ref[...] + part)

    a_spec = pl.BlockSpec((tk, tm), lambda i, j, q: (q, i)) if ta else pl.BlockSpec((tm, tk), lambda i, j, q: (i, q))
    if b_sections:
        b_spec = (pl.BlockSpec((1, tn, tk), lambda i, j, q: (q, j, 0)) if tb
                  else pl.BlockSpec((1, tk, tn), lambda i, j, q: (j, q, 0)))
    else:
        b_spec = pl.BlockSpec((tn, tk), lambda i, j, q: (j, q)) if tb else pl.BlockSpec((tk, tn), lambda i, j, q: (q, j))
    add_spec = [pl.BlockSpec((tm, tn), lambda i, j, q: (i, jnp.maximum(j - add_from, 0)))] if has_add else []
    if out_sections:
        out_spec, out_shape = pl.BlockSpec((1, tm, tn), lambda i, j, q: (j, i, 0)), (n // tn, m, tn)
    else:
        out_spec, out_shape = pl.BlockSpec((tm, tn), lambda i, j, q: (i, j)), (m, n)
    return pl.pallas_call(
        body, name=name, grid=(m // tm, n // tn, nk),
        in_specs=[a_spec, b_spec] + add_spec + ([pl.BlockSpec(memory_space=pl.ANY)] if after is not None else []),
        out_specs=out_spec, out_shape=jax.ShapeDtypeStruct(out_shape, out_dtype),
        scratch_shapes=[pltpu.VMEM((tm, tn), F32)] if nk > 1 else [],
        compiler_params=_params(("parallel", "parallel", "arbitrary")),
    )(*([a, b] + ([add] if has_add else []) + ([after] if after is not None else [])))


def _h_fn(x, g, m):
    shift, scale = m[:, 0:D], m[:, D:2 * D]
    r = lax.rsqrt(jnp.mean(x * x, axis=-1, keepdims=True) + EPS)
    return (x * r * g) * (1.0 + scale) + shift


def _norm_fwd(x, g, mod, bmod, *, name):
    rows = x.shape[0]
    t = min(512, rows)

    def body(x_ref, g_ref, m_ref, b_ref, h_ref):
        h_ref[...] = _h_fn(x_ref[...], g_ref[...], m_ref[...] + b_ref[...]).astype(BF16)

    vec = lambda w: pl.BlockSpec((1, w), lambda i: (0, 0))
    return pl.pallas_call(
        body, name=name, grid=(rows // t,),
        in_specs=[pl.BlockSpec((t, D), lambda i: (i, 0)), vec(D), vec(3 * D), vec(3 * D)],
        out_specs=pl.BlockSpec((t, D), lambda i: (i, 0)),
        out_shape=jax.ShapeDtypeStruct((rows, D), BF16),
        compiler_params=_params(("parallel",)),
    )(x, g, mod, bmod)


def _norm_bwd(x, g, mod, bmod, dh_parts, resid, *, name):
    rows = x.shape[0]
    t = min(512, rows)
    n_parts = len(dh_parts)
    has_resid = resid is not None

    def body(*refs):
        x_ref, g_ref, m_ref, b_ref = refs[:4]
        parts = refs[4:4 + n_parts]
        r_ref = refs[4 + n_parts] if has_resid else None
        dx_ref, dg_ref, dm_ref = refs[-3:]
        i = pl.program_id(0)
        dh = parts[0][...]
        for p in parts[1:]:
            dh = dh + p[...]
        _, vjp = jax.vjp(_h_fn, x_ref[...], g_ref[...], m_ref[...] + b_ref[...])
        dx, dg, dm = vjp(dh)
        if has_resid:
            dx = dx + r_ref[...]
        dx_ref[...] = dx

        @pl.when(i == 0)
        def _():
            dg_ref[...] = dg
            dm_ref[...] = dm

        @pl.when(i > 0)
        def _():
            dg_ref[...] += dg
            dm_ref[...] += dm

    vec = lambda w: pl.BlockSpec((1, w), lambda i: (0, 0))
    tile = pl.BlockSpec((t, D), lambda i: (i, 0))
    ins = [x, g, mod, bmod, *dh_parts] + ([resid] if has_resid else [])
    return pl.pallas_call(
        body, name=name, grid=(rows // t,),
        in_specs=[tile, vec(D), vec(3 * D), vec(3 * D)] + [tile] * (n_parts + int(has_resid)),
        out_specs=[tile, vec(D), vec(3 * D)],
        out_shape=[jax.ShapeDtypeStruct((rows, D), F32), jax.ShapeDtypeStruct((1, D), F32),
                   jax.ShapeDtypeStruct((1, 3 * D), F32)],
        compiler_params=_params(("arbitrary",)),
    )(*ins)


def _conv_tile(u_ref, r0, t, rows, w0, w1, w2):
    u = u_ref[pl.ds(r0, t), :]
    prev8 = u_ref[pl.ds(pl.multiple_of(jnp.maximum(r0 - 8, 0), 8), 8), :]
    next8 = u_ref[pl.ds(pl.multiple_of(jnp.minimum(r0 + t, rows - 8), 8), 8), :]
    r8 = lax.broadcasted_iota(jnp.int32, (8, DH), 0)
    prev_row = jnp.sum(jnp.where(r8 == 7, prev8, 0.0), axis=0, keepdims=True)
    next_row = jnp.sum(jnp.where(r8 == 0, next8, 0.0), axis=0, keepdims=True)
    prev_row = jnp.where(r0 > 0, prev_row, 0.0)
    next_row = jnp.where(r0 + t < rows, next_row, 0.0)
    ri = lax.broadcasted_iota(jnp.int32, (t, DH), 0)
    um1 = jnp.where(ri == 0, prev_row, pltpu.roll(u, 1, 0))
    up1 = jnp.where(ri == t - 1, next_row, pltpu.roll(u, t - 1, 0))
    return w0 * um1 + w1 * u + w2 * up1, um1, u, up1


def _rowlocal(z, norm):
    y = _silu(z)
    return y * lax.rsqrt(jnp.sum(y * y, axis=-1, keepdims=True) + EPS) if norm else y


def _prep_fwd(p, wconv, n_norm, *, name):
    rows, nb = p.shape[0], p.shape[1] // DH
    t = min(512, rows)
    out = None
    for norm, b0, b1 in ((True, 0, n_norm), (False, n_norm, nb)):
        def body(u_ref, w_ref, *rest, norm=norm):
            o_ref = rest[-1]
            w0, w1, w2 = w_ref[0:1, :], w_ref[1:2, :], w_ref[2:3, :]

            def step(s, carry):
                r0 = pl.multiple_of(s * t, t)
                z, _, _, _ = _conv_tile(u_ref, r0, t, rows, w0, w1, w2)
                o_ref[0, pl.ds(r0, t), :] = _rowlocal(z, norm)
                return carry

            lax.fori_loop(0, rows // t, step, 0)

        held = [] if out is None else [out]
        out = pl.pallas_call(
            body, name=f"{name}_{'norm' if norm else 'plain'}", grid=(b1 - b0,),
            in_specs=[pl.BlockSpec((rows, DH), lambda j, b0=b0: (0, b0 + j)),
                      pl.BlockSpec((3, DH), lambda j, b0=b0: (0, b0 + j))] + [ANY] * len(held),
            out_specs=pl.BlockSpec((1, rows, DH), lambda j, b0=b0: (b0 + j, 0, 0)),
            out_shape=jax.ShapeDtypeStruct((nb, rows, DH), F32),
            input_output_aliases={2: 0} if held else {},
            compiler_params=_params(("parallel",)),
        )(p, wconv, *held)
    return out


def _prep_bwd(p, wconv, d_a, d_b, n_norm, *, name):
    rows, nb = p.shape[0], p.shape[1] // DH
    t = min(512, rows)
    outs = []
    for norm, b0, b1 in ((True, 0, n_norm), (False, n_norm, nb)):
        def body(u_ref, w_ref, da_ref, db_ref, *rest, norm=norm):
            du_ref, dw_ref, dz_ref = rest[-3:]
            w0, w1, w2 = w_ref[0:1, :], w_ref[1:2, :], w_ref[2:3, :]

            def step1(s, carry):
                a0, a1, a2 = carry
                r0 = pl.multiple_of(s * t, t)
                z, um1, u, up1 = _conv_tile(u_ref, r0, t, rows, w0, w1, w2)
                _, vjp = jax.vjp(lambda zz: _rowlocal(zz, norm), z)
                (dz,) = vjp(da_ref[0, pl.ds(r0, t), :] + db_ref[0, pl.ds(r0, t), :])
                dz_ref[pl.ds(r0, t), :] = dz
                red = lambda v: jnp.sum(v, axis=0, keepdims=True)
                return a0 + red(dz * um1), a1 + red(dz * u), a2 + red(dz * up1)

            zero = jnp.zeros((1, DH), F32)
            a0, a1, a2 = lax.fori_loop(0, rows // t, step1, (zero, zero, zero))
            dw_ref[0:1, :] = a0
            dw_ref[1:2, :] = a1
            dw_ref[2:3, :] = a2

            def step2(s, carry):
                r0 = pl.multiple_of(s * t, t)
                du, _, _, _ = _conv_tile(dz_ref, r0, t, rows, w2, w1, w0)
                du_ref[pl.ds(r0, t), :] = du.astype(BF16)
                return carry

            lax.fori_loop(0, rows // t, step2, 0)

        col = pl.BlockSpec((rows, DH), lambda j, b0=b0: (0, b0 + j))
        w_spec = pl.BlockSpec((3, DH), lambda j, b0=b0: (0, b0 + j))
        d_spec = pl.BlockSpec((1, rows, DH), lambda j, b0=b0: (b0 + j, 0, 0))
        outs = pl.pallas_call(
            body, name=f"{name}_{'norm' if norm else 'plain'}", grid=(b1 - b0,),
            in_specs=[col, w_spec, d_spec, d_spec] + [ANY] * len(outs), out_specs=[col, w_spec],
            out_shape=[jax.ShapeDtypeStruct((rows, nb * DH), BF16), jax.ShapeDtypeStruct((3, nb * DH), F32)],
            input_output_aliases={4: 0, 5: 1} if outs else {},
            scratch_shapes=[pltpu.VMEM((rows, DH), F32)],
            compiler_params=_params(("parallel",)),
        )(p, wconv, d_a, d_b, *outs)
    return outs


def _gates_fn(pab, avec, dvec):
    t = pab.shape[0]
    lane = lax.broadcasted_iota(jnp.int32, pab.shape, 1)
    xg = pab + dvec
    sp = jnp.maximum(xg, 0.0) + jnp.log(1.0 + jnp.exp(-jnp.abs(xg)))
    g = jnp.where(lane < 2 * H, -jnp.exp(avec) * sp, 0.0)
    ii = lax.broadcasted_iota(jnp.int32, (t, t), 0)
    jj = lax.broadcasted_iota(jnp.int32, (t, t), 1)
    same = (ii >> LOG_CH) == (jj >> LOG_CH)
    cum_f = _dot(jnp.where(same & (jj <= ii), 1.0, 0.0), g, precision=HI)
    cum_r = _dot(jnp.where(same & (jj >= ii), 1.0, 0.0), g, precision=HI)
    return jnp.where(lane < H, cum_f, jnp.where(lane < 2 * H, cum_r, jnp.where(lane < 4 * H, jax.nn.sigmoid(pab), 0.0)))


def _gates_fwd(pab, avec, dvec, *, name):
    rows = pab.shape[0]
    t = min(2 * PAIR, rows)

    def body(p_ref, a_ref, d_ref, o_ref):
        o_ref[...] = _gates_fn(p_ref[...], a_ref[...], d_ref[...])

    vec = pl.BlockSpec((1, DH), lambda i: (0, 0))
    tile = pl.BlockSpec((t, DH), lambda i: (i, 0))
    return pl.pallas_call(
        body, name=name, grid=(rows // t,), in_specs=[tile, vec, vec], out_specs=tile,
        out_shape=jax.ShapeDtypeStruct((rows, DH), F32), compiler_params=_params(("parallel",)),
    )(pab, avec, dvec)


def _gates_bwd(pab, avec, dvec, d_a, d_b, *, name):
    rows = pab.shape[0]
    t = min(2 * PAIR, rows)

    def body(p_ref, a_ref, d_ref, da_ref, db_ref, dp_ref, dav_ref, ddv_ref):
        i = pl.program_id(0)
        _, vjp = jax.vjp(_gates_fn, p_ref[...], a_ref[...], d_ref[...])
        dp, dav, ddv = vjp(da_ref[...] + db_ref[...])
        dp_ref[...] = dp.astype(BF16)

        @pl.when(i == 0)
        def _():
            dav_ref[...] = dav
            ddv_ref[...] = ddv

        @pl.when(i > 0)
        def _():
            dav_ref[...] += dav
            ddv_ref[...] += ddv

    vec = pl.BlockSpec((1, DH), lambda i: (0, 0))
    tile = pl.BlockSpec((t, DH), lambda i: (i, 0))
    return pl.pallas_call(
        body, name=name, grid=(rows // t,), in_specs=[tile, vec, vec, tile, tile], out_specs=[tile, vec, vec],
        out_shape=[jax.ShapeDtypeStruct((rows, DH), BF16), jax.ShapeDtypeStruct((1, DH), F32),
                   jax.ShapeDtypeStruct((1, DH), F32)],
        compiler_params=_params(("arbitrary",)),
    )(pab, avec, dvec, d_a, d_b)


def _dot_general(a, b, dims, precision=None):
    return lax.dot_general(a, b, (dims, ((), ())), precision=precision, preferred_element_type=F32)


@functools.partial(jax.custom_vjp, nondiff_argnums=(2,))
def _dot_bf16(a, b, dims):
    return _dot_general(a.astype(BF16), b.astype(BF16), dims)


def _dot_bf16_fwd(a, b, dims):
    return _dot_bf16(a, b, dims), (a, b)


def _dot_bf16_bwd(dims, res, g):
    a, b = res
    (ca,), (cb,) = dims
    da = _dot_bf16(g, b, ((1,), (1 - cb,))) if ca == 1 else _dot_bf16(b, g, ((1 - cb,), (1,)))
    db = _dot_bf16(a, g, ((1 - ca,), (0,))) if cb == 0 else _dot_bf16(g, a, ((0,), (1 - ca,)))
    return da, db


_dot_bf16.defvjp(_dot_bf16_fwd, _dot_bf16_bwd)


def _dot(a, b, dims=((1,), (0,)), precision=None):
    if precision is None and a.dtype == F32 and b.dtype == F32:
        return _dot_bf16(a, b, dims)
    return _dot_general(a, b, dims, precision)


_NT = ((1,), (1,))
_TN = ((0,), (0,))


@jax.custom_vjp
def _saved_inverse(neg_a, inv):
    return inv


def _saved_inverse_fwd(neg_a, inv):
    return inv, inv


def _saved_inverse_bwd(inv, d_inv):
    idx = range(len(inv))
    left = [_dot(inv[i], d_inv[i], _TN) for i in idx]
    d_neg_a = [_dot(left[i], inv[i], _NT) for i in idx]
    return d_neg_a, [jnp.zeros_like(t) for t in inv]


_saved_inverse.defvjp(_saved_inverse_fwd, _saved_inverse_bwd)


def _pairs(s, q, k, v, gcol, bcol, revs, inv_saved=None):
    idx = range(len(revs))
    ii = lax.broadcasted_iota(jnp.int32, (PAIR, PAIR), 0)
    jj = lax.broadcasted_iota(jnp.int32, (PAIR, PAIR), 1)
    same = (ii >> LOG_CH) == (jj >> LOG_CH)
    incl_d = (same & (ii >= jj), same & (ii <= jj))
    strict_d = (same & (ii > jj), same & (ii < jj))
    incl = [incl_d[int(r)] for r in revs]
    strict = [strict_d[int(r)] for r in revs]
    eye = jnp.where(ii == jj, 1.0, 0.0)
    gc_i = [jnp.broadcast_to(gcol[i], (PAIR, DH)) for i in idx]
    gc_j = [gc_i[i].T for i in idx]
    decay = [jnp.where(incl[i], jnp.exp(jnp.where(incl[i], gc_i[i] - gc_j[i], 0.0)), 0.0) for i in idx]
    b_b = [jnp.broadcast_to(bcol[i], (PAIR, DH)) for i in idx]
    kb = [k[i] * b_b[i] for i in idx]
    kk = [_dot(kb[i], k[i], _NT) for i in idx]
    bp = [jnp.where(strict[i], -kk[i] * decay[i], 0.0) for i in idx]
    if inv_saved is not None:
        inv = _saved_inverse(bp, inv_saved)
    else:
        low = bp
        for prec_sq, prec_acc in PREC_POWERS:
            bp = [_dot(bp[i], bp[i], precision=prec_sq) for i in idx]
            more = [_dot(low[i], bp[i], precision=prec_acc) for i in idx]
            low = [low[i] + bp[i] + more[i] for i in idx]
        inv = [eye + low[i] for i in idx]
    eg = [jnp.exp(gc_i[i]) for i in idx]
    sol = [_dot(inv[i], jnp.concatenate([v[i] * b_b[i], kb[i] * eg[i]], axis=1)) for i in idx]
    u_val = [sol[i][:, :DH] for i in idx]
    w_key = [sol[i][:, DH:] for i in idx]
    row = lax.broadcasted_iota(jnp.int32, (PAIR, 1), 0)
    has_q = q[0] is not None
    if has_q:
        qc = [q[i] * (DH ** -0.5) for i in idx]
        qk = [_dot(qc[i], k[i], _NT) for i in idx]
        attn = [qk[i] * decay[i] for i in idx]
        qd = [qc[i] * eg[i] for i in idx]
    outs = [[None, None] for _ in idx]
    zeros = jnp.zeros((CH, DH), F32)
    for step in range(2):
        cidx = [(1 - step) if revs[i] else step for i in idx]
        sl = [slice(c * CH, (c + 1) * CH) for c in cidx]
        last = [c * CH if revs[i] else c * CH + CH - 1 for i, c in zip(idx, cidx)]
        gl = [jnp.sum(jnp.where(row == last[i], gcol[i], 0.0), axis=0, keepdims=True) for i in idx]
        k_tail = [k[i][sl[i]] * jnp.exp(gl[i] - gc_i[i][sl[i]]) for i in idx]
        ws = [_dot(w_key[i][sl[i]], s[i]) for i in idx]
        v_new = [u_val[i][sl[i]] - ws[i] for i in idx]
        if has_q:
            v_pad = [jnp.concatenate([v_new[i], zeros] if cidx[i] == 0 else [zeros, v_new[i]], axis=0) for i in idx]
            o_state = [_dot(qd[i][sl[i]], s[i]) for i in idx]
            o_local = [_dot(attn[i][sl[i]], v_pad[i]) for i in idx]
            for i in idx:
                outs[i][cidx[i]] = o_state[i] + o_local[i]
        kv = [_dot(k_tail[i], v_new[i], _TN) for i in idx]
        s = [s[i] * jnp.exp(gl[i]) + kv[i] for i in idx]
    return s, ([jnp.concatenate(outs[i], axis=0) for i in idx] if has_q else None), inv


def _lane_col(tile, idx):
    lane = lax.broadcasted_iota(jnp.int32, tile.shape, 1)
    return jnp.sum(jnp.where(lane == idx, tile, 0.0), axis=1, keepdims=True)


def _gdn_fwd(qkv, gb, s0f, s0b, has_q, *, name):
    nb, rows, _ = qkv.shape
    n = rows // PAIR
    qoff = H if has_q else 0

    def body(qf_ref, qb_ref, gf_ref, gr_ref, s0f_ref, s0b_ref, of_ref, ob_ref, ssf_ref, ssb_ref, tsf_ref, tsb_ref,
             sf_ref, sb_ref):
        @pl.when(pl.program_id(0) == 0)
        def _():
            sf_ref[...] = s0f_ref[...]
            sb_ref[...] = s0b_ref[...]

        gtiles = (gf_ref[...], gr_ref[...])

        dirs = ((qf_ref, sf_ref, ssf_ref, of_ref), (qb_ref, sb_ref, ssb_ref, ob_ref))
        ts_refs = (tsf_ref, tsb_ref)

        work = [(h, d) for h in range(H) for d in range(2)]
        loaded = []
        for h, d in work:
            q_ref, s_ref, _, _ = dirs[d]
            loaded.append((s_ref[h], q_ref[h] if has_q else None, q_ref[qoff + h], q_ref[qoff + H + h],
                           _lane_col(gtiles[d], d * H + h), _lane_col(gtiles[d], 2 * H + d * H + h)))
        s_new, o, inv = _pairs(*[list(col) for col in zip(*loaded)], revs=[d == 1 for _, d in work])
        for i, (h, d) in enumerate(work):
            _, s_ref, ss_ref, o_ref = dirs[d]
            ss_ref[0, h] = loaded[i][0]
            ts_refs[d][0, h] = inv[i]
            s_ref[h] = s_new[i]
            o_ref[h] = o[i] if has_q else jnp.zeros((PAIR, DH), F32)

    fwd3 = lambda i: (0, i, 0)
    rev3 = lambda i: (0, n - 1 - i, 0)
    state = pl.BlockSpec((H, DH, DH), lambda i: (0, 0, 0))
    saved = pl.BlockSpec((1, H, DH, DH), lambda i: (i, 0, 0, 0))
    return pl.pallas_call(
        body, name=name, grid=(n,),
        in_specs=[pl.BlockSpec((nb, PAIR, DH), fwd3), pl.BlockSpec((nb, PAIR, DH), rev3),
                  pl.BlockSpec((PAIR, DH), lambda i: (i, 0)), pl.BlockSpec((PAIR, DH), lambda i: (n - 1 - i, 0)),
                  state, state],
        out_specs=[pl.BlockSpec((H, PAIR, DH), fwd3), pl.BlockSpec((H, PAIR, DH), rev3), saved, saved, saved, saved,
                   state, state],
        out_shape=[jax.ShapeDtypeStruct((H, rows, DH), F32)] * 2 + [jax.ShapeDtypeStruct((n, H, DH, DH), F32)] * 4
        + [jax.ShapeDtypeStruct((H, DH, DH), F32)] * 2,
        compiler_params=_params(("arbitrary",)),
    )(qkv, qkv, gb, gb, s0f, s0b)


def _gdn_bwd(qkv, gb, ssf, ssb, tsf, tsb, do, dsf, dsb, has_q, *, name):
    nb, rows, _ = qkv.shape
    n = rows // PAIR
    qoff = H if has_q else 0

    def body(qf_ref, qb_ref, gf_ref, gr_ref, ssf_ref, ssb_ref, tsf_ref, tsb_ref, dof_ref, dob_ref, dsf0_ref, dsb0_ref,
             dqf_ref, dqb_ref, dgf_ref, dgr_ref, dsf_ref, dsb_ref):
        ts_refs = (tsf_ref, tsb_ref)
        @pl.when(pl.program_id(0) == 0)
        def _():
            dsf_ref[...] = dsf0_ref[...]
            dsb_ref[...] = dsb0_ref[...]

        gtiles = (gf_ref[...], gr_ref[...])
        lane = lax.broadcasted_iota(jnp.int32, (PAIR, DH), 1)

        dirs = ((qf_ref, ssf_ref, dof_ref, dsf_ref, dqf_ref), (qb_ref, ssb_ref, dob_ref, dsb_ref, dqb_ref))

        work = [(h, d) for h in range(H) for d in range(2)]
        revs = [d == 1 for _, d in work]
        s_in, q_in, k_in, v_in, g_in, b_in, ds_out, do_out, inv_in = [], [], [], [], [], [], [], [], []
        for h, d in work:
            q_ref, ss_ref, do_ref, ds_ref, _ = dirs[d]
            s_in.append(ss_ref[0, h])
            inv_in.append(ts_refs[d][0, h])
            q_in.append(q_ref[h] if has_q else None)
            k_in.append(q_ref[qoff + h])
            v_in.append(q_ref[qoff + H + h])
            g_in.append(_lane_col(gtiles[d], d * H + h))
            b_in.append(_lane_col(gtiles[d], 2 * H + d * H + h))
            ds_out.append(ds_ref[h])
            do_out.append(do_ref[h] if has_q else None)
        if has_q:
            _, vjp = jax.vjp(lambda s_, q_, k_, v_, g_, b_: _pairs(s_, q_, k_, v_, g_, b_, revs, inv_in)[:2],
                             s_in, q_in, k_in, v_in, g_in, b_in)
            ds, dq, dk, dv, dg, db = vjp((ds_out, do_out))
        else:
            _, vjp = jax.vjp(lambda s_, k_, v_, g_, b_: _pairs(s_, q_in, k_, v_, g_, b_, revs, inv_in)[0],
                             s_in, k_in, v_in, g_in, b_in)
            ds, dk, dv, dg, db = vjp(ds_out)
        dgb = [jnp.zeros((PAIR, DH), F32)] * 2
        for i, (h, d) in enumerate(work):
            dq_ref, ds_ref = dirs[d][4], dirs[d][3]
            ds_ref[h] = ds[i]
            if has_q:
                dq_ref[h] = dq[i]
            dq_ref[qoff + h], dq_ref[qoff + H + h] = dk[i], dv[i]
            dgb[d] = (dgb[d] + jnp.where(lane == d * H + h, dg[i], 0.0)
                      + jnp.where(lane == 2 * H + d * H + h, db[i], 0.0))
        dgf_ref[...] = dgb[0]
        dgr_ref[...] = dgb[1]

    fwd3 = lambda i: (0, n - 1 - i, 0)
    rev3 = lambda i: (0, i, 0)
    state = pl.BlockSpec((H, DH, DH), lambda i: (0, 0, 0))
    saved = pl.BlockSpec((1, H, DH, DH), lambda i: (n - 1 - i, 0, 0, 0))
    gf_spec = pl.BlockSpec((PAIR, DH), lambda i: (n - 1 - i, 0))
    gr_spec = pl.BlockSpec((PAIR, DH), lambda i: (i, 0))
    return pl.pallas_call(
        body, name=name, grid=(n,),
        in_specs=[pl.BlockSpec((nb, PAIR, DH), fwd3), pl.BlockSpec((nb, PAIR, DH), rev3), gf_spec, gr_spec,
                  saved, saved, saved, saved,
                  pl.BlockSpec((H, PAIR, DH), fwd3), pl.BlockSpec((H, PAIR, DH), rev3), state, state],
        out_specs=[pl.BlockSpec((nb, PAIR, DH), fwd3), pl.BlockSpec((nb, PAIR, DH), rev3), gf_spec, gr_spec, state, state],
        out_shape=[jax.ShapeDtypeStruct((nb, rows, DH), F32)] * 2 + [jax.ShapeDtypeStruct((rows, DH), F32)] * 2
        + [jax.ShapeDtypeStruct((H, DH, DH), F32)] * 2,
        compiler_params=_params(("arbitrary",)),
    )(qkv, qkv, gb, gb, ssf, ssb, tsf, tsb, do, do, dsf, dsb)


def _stage1(zb, ua, va, za, o, gon, lng, lnb, wsp, bsp):
    gv = [_gelu(t) for t in va]
    mu = sum(jnp.sum(t, axis=-1, keepdims=True) for t in gv) * (1.0 / D)
    xc = [t - mu for t in gv]
    var = sum(jnp.sum(t * t, axis=-1, keepdims=True) for t in xc) * (1.0 / D)
    rs = lax.rsqrt(var + EPS)
    ya, yb = [], []
    for g in range(H):
        vv = xc[g] * rs * lng[g] + lnb[g]
        s = _dot(wsp[g], vv) + bsp[g]
        ya.append(_gelu(ua[g]) * s * _silu(za[g]))
        r = lax.rsqrt(jnp.mean(o[g] * o[g], axis=-1, keepdims=True) + EPS)
        yb.append(o[g] * r * gon * _silu(zb[g]))
    return ya, yb


def _stage2(ma, mb, ga, gb):
    return jax.nn.sigmoid(ga) * ma + jax.nn.sigmoid(gb) * mb


def _stage3(out, x, tgt, gpost, gate):
    r = out * lax.rsqrt(jnp.mean(out * out, axis=-1, keepdims=True) + EPS) * gpost
    err = x + gate * r - tgt
    return 0.5 * jnp.sum(jnp.mean(err * err, axis=-1, keepdims=True), axis=0, keepdims=True)


def _post(p_rest, o_f, o_b, x, tgt, mod, bmod, gon, lng, lnb, wsp, bspb, wpa, wpb, wout, gpost, *, name):
    rows = x.shape[0]
    n = rows // GM
    lanes = lambda g: slice(g * DH, (g + 1) * DH)
    bdot = lambda a, w_ref: _dot(a.astype(BF16), w_ref[...])
    bdot_t = lambda a, w_ref: _dot(a.astype(BF16), w_ref[...], _NT)

    def body(p_ref, of_ref, ob_ref, x_ref, t_ref, m_ref, bm_ref, gon_ref, lng_ref, lnb_ref, wsp_ref, bsp_ref,
             wpa_ref, wpb_ref, wout_ref, gp_ref,
             loss_ref, dp_ref, do_ref, dx_ref, ya_ref, yb_ref, mg_ref, dma_ref, dmb_ref, dout_ref,
             dvec_ref, dgon_ref, dwsp_ref, dbsp_ref):
        @pl.when(pl.program_id(0) == 0)
        def _():
            loss_ref[...] = jnp.zeros_like(loss_ref)
            dvec_ref[...] = jnp.zeros_like(dvec_ref)
            dgon_ref[...] = jnp.zeros_like(dgon_ref)
            dwsp_ref[...] = jnp.zeros_like(dwsp_ref)
            dbsp_ref[...] = jnp.zeros_like(dbsp_ref)

        piece = lambda blk: [p_ref[:, blk * D + g * DH: blk * D + (g + 1) * DH].astype(F32) for g in range(H)]
        zb, ua, va, za = piece(0), piece(1), piece(2), piece(3)
        o = [of_ref[g] + ob_ref[g] for g in range(H)]
        gon = gon_ref[...]
        lng = [lng_ref[:, lanes(g)] for g in range(H)]
        lnb = [lnb_ref[:, lanes(g)] for g in range(H)]
        wsp = [wsp_ref[g] for g in range(H)]
        bsp = [bsp_ref[g] for g in range(H)]
        (ya, yb), vjp1 = jax.vjp(_stage1, zb, ua, va, za, o, gon, lng, lnb, wsp, bsp)
        y_a, y_b = jnp.concatenate(ya, axis=1), jnp.concatenate(yb, axis=1)
        ma, mb = bdot(y_a, wpa_ref), bdot(y_b, wpb_ref)
        ga, gb = p_ref[:, 4 * D:5 * D].astype(F32), p_ref[:, 5 * D:6 * D].astype(F32)
        merged, vjp2 = jax.vjp(_stage2, ma, mb, ga, gb)
        out = bdot(merged, wout_ref)
        gate = m_ref[:, 2 * D:3 * D] + bm_ref[:, 2 * D:3 * D]
        loss, vjp3 = jax.vjp(_stage3, out, x_ref[...], t_ref[...], gp_ref[...], gate)
        loss_ref[...] += jnp.broadcast_to(loss, loss_ref.shape)

        dout, dx, _, dgpost, dgate = vjp3(jnp.ones((1, 1), F32))
        dx_ref[...] = dx
        dmerged = bdot_t(dout, wout_ref)
        dma, dmb, dga, dgb = vjp2(dmerged)
        dya, dyb = bdot_t(dma, wpa_ref), bdot_t(dmb, wpb_ref)
        dzb, dua, dva, dza, do, dgon, dlng, dlnb, dwsp, dbsp = vjp1(
            ([dya[:, lanes(g)] for g in range(H)], [dyb[:, lanes(g)] for g in range(H)]))

        for blk, dlist in enumerate((dzb, dua, dva, dza)):
            for g in range(H):
                dp_ref[:, blk * D + g * DH: blk * D + (g + 1) * DH] = dlist[g].astype(BF16)
        dp_ref[:, 4 * D:5 * D] = dga.astype(BF16)
        dp_ref[:, 5 * D:6 * D] = dgb.astype(BF16)
        for g in range(H):
            do_ref[g] = do[g]
            dwsp_ref[g] += dwsp[g]
            dbsp_ref[g] += dbsp[g]
            dvec_ref[2:3, lanes(g)] += dlng[g]
            dvec_ref[3:4, lanes(g)] += dlnb[g]
        dvec_ref[0:1, :] += dgpost
        dvec_ref[1:2, :] += dgate
        dgon_ref[0:1, :] += dgon
        ya_ref[...] = y_a.astype(BF16)
        yb_ref[...] = y_b.astype(BF16)
        mg_ref[...] = merged.astype(BF16)
        dma_ref[...] = dma.astype(BF16)
        dmb_ref[...] = dmb.astype(BF16)
        dout_ref[...] = dout.astype(BF16)

    row = lambda w: pl.BlockSpec((GM, w), lambda i: (i, 0))
    heads = pl.BlockSpec((H, GM, DH), lambda i: (0, i, 0))
    full = lambda shape: pl.BlockSpec(shape, lambda i: tuple(0 for _ in shape))
    sds = jax.ShapeDtypeStruct
    return pl.pallas_call(
        body, name=name, grid=(n,),
        in_specs=[row(6 * D), heads, heads, row(D), row(D), full((1, 3 * D)), full((1, 3 * D)), full((1, DH)),
                  full((1, D)), full((1, D)), full((H, GM, GM)), full((H, GM, GM)),
                  full((D, D)), full((D, D)), full((D, D)), full((1, D))],
        out_specs=[full((8, DH)), row(6 * D), heads, row(D)] + [row(D)] * 6
        + [full((8, D)), full((8, DH)), full((H, GM, GM)), full((H, GM, GM))],
        out_shape=[sds((8, DH), F32), sds((rows, 6 * D), BF16), sds((H, rows, DH), F32), sds((rows, D), F32)]
        + [sds((rows, D), BF16)] * 6 + [sds((8, D), F32), sds((8, DH), F32), sds((H, GM, GM), F32), sds((H, GM, GM), F32)],
        compiler_params=_params(("arbitrary",)),
    )(p_rest, o_f, o_b, x, tgt, mod, bmod, gon, lng, lnb, wsp, bspb, wpa, wpb, wout, gpost)


def _silu_rows(c, *, name):
    def body(c_ref, o_ref):
        o_ref[...] = _silu(c_ref[...])

    return pl.pallas_call(body, name=name, out_shape=jax.ShapeDtypeStruct(c.shape, F32))(c)


def _dsilu_mul(c, d, *, name):
    def body(c_ref, d_ref, o_ref):
        _, vjp = jax.vjp(_silu, c_ref[...])
        (o_ref[...],) = vjp(d_ref[...])

    return pl.pallas_call(body, name=name, out_shape=jax.ShapeDtypeStruct(c.shape, F32))(c, d)


def _adamw(w, g, m, v, *, name):
    rows, cols = w.shape[-2:]
    t = _tile(rows if g.ndim == 2 else g.shape[1])
    c1 = 1.0 / (1.0 - ADAM_B1 ** ADAM_STEP)
    c2 = 1.0 / (1.0 - ADAM_B2 ** ADAM_STEP)

    def body(w_ref, g_ref, m_ref, v_ref, go_ref, d_ref, mo_ref, vo_ref):
        blk = lambda r: r[...].reshape(t, cols)
        gv = blk(g_ref)
        mn = ADAM_B1 * blk(m_ref) + (1.0 - ADAM_B1) * gv
        vn = ADAM_B2 * blk(v_ref) + (1.0 - ADAM_B2) * (gv * gv)
        delta = -ADAM_LR * ((mn * c1) / (jnp.sqrt(vn * c2) + ADAM_EPS) + ADAM_WD * blk(w_ref))
        for ref, val in ((go_ref, gv), (d_ref, delta), (mo_ref, mn), (vo_ref, vn)):
            ref[...] = val.reshape(ref.shape)

    tile = (pl.BlockSpec((1, t, cols), lambda i: (0, i, 0)) if w.ndim == 3 else pl.BlockSpec((t, cols), lambda i: (i, 0)))
    if g.ndim == 3:
        per = g.shape[1] // t
        g_tile = pl.BlockSpec((1, t, cols), lambda i: (i // per, i % per, 0))
    else:
        g_tile = pl.BlockSpec((t, cols), lambda i: (i, 0))
    return pl.pallas_call(
        body, name=name, grid=(rows // t,),
        in_specs=[tile, g_tile, tile, tile], out_specs=[tile] * 4,
        out_shape=[jax.ShapeDtypeStruct(w.shape, F32)] * 4,
        compiler_params=_params(("parallel",)),
    )(w, g, m, v)


def _add_sibling(fulls, gots, where, out_dtype, *, name):
    rows, cols = fulls[0].shape[1:]
    hr = rows // 2
    t = _tile(hr)
    nt = hr // t
    first = [sum(a.shape[0] for a in fulls[:k]) for k in range(len(fulls) + 1)]
    n = len(fulls)

    def body(w_ref, *refs):
        o_ref = refs[-1]
        j = pl.program_id(0)
        for k in range(n):
            @pl.when((j >= first[k]) & (j < first[k + 1]))
            def _(k=k):
                o_ref[...] = (refs[k][...].astype(F32) + refs[n + k][...].astype(F32)).astype(out_dtype)

    def tile(k, row0):
        def index(j, i, w):
            rb = jnp.where(j < first[k], 0, jnp.where(j >= first[k + 1], nt - 1, i))
            return jnp.clip(j - first[k], 0, fulls[k].shape[0] - 1), row0(w) + rb, 0
        return pl.BlockSpec((1, t, cols), index)

    return pl.pallas_call(
        body, name=name,
        grid_spec=pltpu.PrefetchScalarGridSpec(
            num_scalar_prefetch=1, grid=(first[-1], nt),
            in_specs=[tile(k, lambda w: w[0] * nt) for k in range(n)] + [tile(k, lambda w: 0) for k in range(n)],
            out_specs=pl.BlockSpec((1, t, cols), lambda j, i, w: (j, i, 0))),
        out_shape=jax.ShapeDtypeStruct((first[-1], hr, cols), out_dtype),
        compiler_params=_params(("parallel", "parallel")),
    )(where, *fulls, *gots)


def _sum_chips(own, slots, where, n_out, which, *, name):
    _, hr, cols = own.shape
    t = _tile(hr)

    def body(w_ref, a_ref, s_ref, o_ref):
        f = lambda v: v.astype(F32)
        o_ref[0] = ((f(a_ref[0]) + f(s_ref[0])) + f(s_ref[1])) + f(s_ref[2])

    return pl.pallas_call(
        body, name=name,
        grid_spec=pltpu.PrefetchScalarGridSpec(
            num_scalar_prefetch=1, grid=(hr // t,),
            in_specs=[pl.BlockSpec((1, t, cols), lambda i, w: (w[1], i, 0)),
                      pl.BlockSpec((N_CHIPS - 1, t, cols), lambda i, w: (0, i, 0))],
            out_specs=pl.BlockSpec((1, t, cols), lambda i, w: (w[which], i, 0))),
        out_shape=jax.ShapeDtypeStruct((n_out, hr, cols), F32), compiler_params=_params(("parallel",)),
    )(where, own, slots)


def _local_step(x, c, ctx, c_ctx, tgt, w_mod, b_mod, g_pre, g_post, w_qkv, w_ab, w_rest, w_conv, a_log, dt_bias,
                g_onorm, gm_ln_g, gm_ln_b, w_sp, b_sp, late_weights, on_large_gradients=None, run_after=None):
    rows, rows_c = x.shape[0], ctx.shape[0]
    cc = jnp.zeros((16, D), F32).at[0].set(c[0]).at[1].set(c_ctx)
    scc = _silu_rows(cc, name="silu_cond")
    mod = _mm(scc, w_mod, b_sections=True, after=run_after, name="mod_fwd")
    mod_x, mod_c = mod[0:1], mod[1:2]
    avec = jnp.zeros((1, DH), F32).at[0, :2 * H].set(a_log.reshape(-1))
    dvec = jnp.zeros((1, DH), F32).at[0, :2 * H].set(dt_bias.reshape(-1))
    bspb = jnp.broadcast_to(b_sp[:, :, None], (H, GM, GM))
    w_kv, wconv_kv = w_qkv[:, D:], w_conv[:, D:]

    h_c = _norm_fwd(ctx, g_pre, mod_c, b_mod, name="norm_fwd_ctx")
    pc_kv = _mm(h_c, w_kv, name="inproj_ctx_kv")
    pc_ab = _mm(h_c, w_ab, name="inproj_ctx_ab")
    kv_c = _prep_fwd(pc_kv, wconv_kv, H, name="prep_fwd_ctx")
    gb_c = _gates_fwd(pc_ab, avec, dvec, name="gates_fwd_ctx")
    s_zero = jnp.zeros((H, DH, DH), F32)
    _, _, ssf_c, ssb_c, tsf_c, tsb_c, s_f, s_b = _gdn_fwd(kv_c, gb_c, s_zero, s_zero, False, name="gdn_fwd_ctx")

    h_x = _norm_fwd(x, g_pre, mod_x, b_mod, name="norm_fwd_x")
    p_qkv = _mm(h_x, w_qkv, tm=rows, tn=D // 2, name="inproj_qkv")
    p_ab = _mm(h_x, w_ab, name="inproj_ab")
    p_rest = _mm(h_x, w_rest, tm=rows, tn=D // 2, out_dtype=BF16, name="inproj_rest")
    qkv = _prep_fwd(p_qkv, w_conv, 2 * H, name="prep_fwd_x")
    gb_x = _gates_fwd(p_ab, avec, dvec, name="gates_fwd_x")
    o_f, o_b, ssf, ssb, tsf, tsb, _, _ = _gdn_fwd(qkv, gb_x, s_f, s_b, True, name="gdn_fwd_x")

    w_pa, w_pb, w_out = late_weights(o_f)
    (loss_acc, dp_rest, do, dx_res, ya, yb, mg, dma, dmb, dout, dvec_post, dgon, dwsp, dbspb) = _post(
        p_rest, o_f, o_b, x, tgt, mod_x, b_mod, g_onorm, gm_ln_g, gm_ln_b, w_sp, bspb, w_pa, w_pb, w_out, g_post,
        name="post")
    g = {}
    g["w_pa"] = _mm(ya, dma, ta=True, out_dtype=BF16, name="dw_pa")
    g["w_pb"] = _mm(yb, dmb, ta=True, out_dtype=BF16, name="dw_pb")
    g["w_out"] = _mm(mg, dout, ta=True, out_dtype=BF16, name="dw_out")

    zeros_s = jnp.zeros((H, DH, DH), F32)
    dq_f, dq_b, dg_f, dg_b, ds0_f, ds0_b = _gdn_bwd(qkv, gb_x, ssf, ssb, tsf, tsb, do, zeros_s, zeros_s, True,
                                                    name="gdn_bwd_x")
    dp_qkv, dwc_x = _prep_bwd(p_qkv, w_conv, dq_f, dq_b, 2 * H, name="prep_bwd_x")
    dp_ab, dav_x, ddv_x = _gates_bwd(p_ab, avec, dvec, dg_f, dg_b, name="gates_bwd_x")
    dkv_f, dkv_b, dgc_f, dgc_b, _, _ = _gdn_bwd(kv_c, gb_c, ssf_c, ssb_c, tsf_c, tsb_c, jnp.zeros((H, rows_c, DH), F32),
                                                 ds0_f, ds0_b, False, name="gdn_bwd_ctx")
    dpc_kv, dwc_c = _prep_bwd(pc_kv, wconv_kv, dkv_f, dkv_b, H, name="prep_bwd_ctx")
    dpc_ab, dav_c, ddv_c = _gates_bwd(pc_ab, avec, dvec, dgc_f, dgc_b, name="gates_bwd_ctx")

    dw_kv_c = _mm(h_c, dpc_kv, ta=True, name="dw_kv_ctx")
    dw_qkv = _mm(h_x, dp_qkv, ta=True, tn=D, tk=2 * D, add=dw_kv_c, add_from=1, out_dtype=BF16, name="dw_qkv")
    dw_ab_c = _mm(h_c, dpc_ab, ta=True, name="dw_ab_ctx")
    dw_ab = _mm(h_x, dp_ab, ta=True, add=dw_ab_c, out_dtype=BF16, name="dw_ab")
    dw_rest = _mm(h_x, dp_rest, ta=True, tn=2 * D, tk=2 * D, out_dtype=BF16, name="dw_rest")
    g["w_in"] = (dw_qkv, dw_ab[:, :4 * H], dw_rest)
    token, scatter = on_large_gradients(g) if on_large_gradients is not None else (None, None)
    dh = _mm(dp_ab, w_ab, tb=True, after=token, name="dh_ab")
    dh = _mm(dp_qkv, w_qkv, tb=True, tk=3 * D, add=dh, after=token, name="dh_qkv")
    token = scatter(dh) if scatter is not None else None
    dh = _mm(dp_rest, w_rest, tb=True, tk=3 * D, add=dh, after=token, name="dh_rest")
    grad_x, dgpre_x, dm_x = _norm_bwd(x, g_pre, mod_x, b_mod, [dh], dx_res, name="norm_bwd_x")
    dh_c = _mm(dpc_ab, w_ab, tb=True, after=token, name="dhc_ab")
    dh_c = _mm(dpc_kv, w_kv, tb=True, add=dh_c, after=token, name="dhc_kv")
    _, dgpre_c, dm_c = _norm_bwd(ctx, g_pre, mod_c, b_mod, [dh_c], None, name="norm_bwd_ctx")

    dm_x = dm_x.at[:, 2 * D:].add(dvec_post[1:2])
    dmod = jnp.zeros((16, 3 * D), F32).at[0].set(dm_x[0]).at[1].set(dm_c[0])
    g["mod_factors"] = (scc[0], dm_x[0])
    g["dm_ctx"] = dm_c[0]
    g["silu_c_ctx"] = scc[1:2]
    dcc = _mm(dmod, w_mod, tb=True, b_sections=True, name="dcc")
    g["c_ctx"] = _dsilu_mul(cc[:8], dcc[:8], name="dc_ctx")[1]
    g["b_mod"] = dm_x + dm_c
    g["g_pre"] = dgpre_x + dgpre_c
    g["g_post"] = dvec_post[0:1]
    g["gm_ln_g"], g["gm_ln_b"] = dvec_post[2:3], dvec_post[3:4]
    g["g_onorm"] = dgon[0:1]
    g["w_sp"] = dwsp
    g["b_sp"] = jnp.sum(dbspb, axis=-1)
    g["w_conv"] = dwc_x.at[:, D:].add(dwc_c)
    g["a_log"] = (dav_x + dav_c)[0, :2 * H].reshape(2, H)
    g["dt_bias"] = (ddv_x + ddv_c)[0, :2 * H].reshape(2, H)
    return loss_acc[0, 0], grad_x, g


ANY = pl.BlockSpec(memory_space=pl.ANY)


def _place():
    x, y, c = lax.axis_index("x"), lax.axis_index("y"), lax.axis_index("c")
    chips = [(1 - x, y), (x, 1 - y), (1 - x, 1 - y)]
    return x, y, c, (x, y, 1 - c), chips


def _gather_shards(big, small, *, name):
    nb = len(big)

    def body(*refs):
        ins, sm_in = refs[:nb], refs[nb]
        outs, sm_out = refs[nb + 1:2 * nb + 1], refs[2 * nb + 1]
        send, recv = refs[2 * nb + 2:]
        x, y, c, sibling, chips = _place()
        mine = 2 * x + y

        def half(a, shard, hc):
            hr = big[a].shape[0] // 2
            return outs[a].at[shard, pl.ds(hc * hr, hr), :]

        def remote(k, src, dst, to):
            return pltpu.make_async_remote_copy(src_ref=src, dst_ref=dst, send_sem=send.at[k], recv_sem=recv.at[k],
                                                device_id=to, device_id_type=MESH)

        sends = []
        for a in range(nb):
            hr = big[a].shape[0] // 2
            for j, chip in enumerate(chips):
                sends.append(remote(a * 3 + j, ins[a].at[pl.ds(c * hr, hr), :], half(a, mine, c), (*chip, c)))
        for j, chip in enumerate(chips):
            sends.append(remote(nb * 3 + j, sm_in, sm_out.at[mine], (*chip, c)))
        for cp in sends:
            cp.start()
        base = nb * 3 + 3
        passed = []
        for a in range(nb):
            for j, (px, py) in enumerate(chips):
                theirs = 2 * px + py
                remote(a * 3 + j, half(a, theirs, c), half(a, theirs, c), sibling).wait_recv()
                fw = remote(base + a * 3 + j, half(a, theirs, c), half(a, theirs, c), sibling)
                fw.start()
                passed.append(fw)
        for a in range(nb):
            for j, (px, py) in enumerate(chips):
                theirs = 2 * px + py
                remote(base + a * 3 + j, half(a, theirs, 1 - c), half(a, theirs, 1 - c), sibling).wait_recv()
        for j, (px, py) in enumerate(chips):
            remote(nb * 3 + j, sm_in, sm_out.at[2 * px + py], sibling).wait_recv()
        for cp in sends + passed:
            cp.wait_send()

    n_remote = 2 * nb * 3 + 3
    outs = pl.pallas_call(
        body, name=name, in_specs=[ANY] * (nb + 1), out_specs=[ANY] * (nb + 1),
        out_shape=[jax.ShapeDtypeStruct((N_CHIPS,) + a.shape, a.dtype) for a in big + [small]],
        scratch_shapes=[pltpu.SemaphoreType.DMA((n_remote,)), pltpu.SemaphoreType.DMA((n_remote,))],
    )(*big, small)
    return outs[:nb], outs[nb]


def _gather_late_start(arrs, after, *, name):
    na = len(arrs)

    def body(*refs):
        ins, land = refs[:na], refs[na:2 * na]
        send, recv = refs[2 * na + 1], refs[2 * na + 2]
        token = refs[-1]
        x, y, c, _, chips = _place()
        for a in range(na):
            hr = arrs[a].shape[0] // 2
            for j, (px, py) in enumerate(chips):
                for other in range(2):
                    k = (a * 3 + j) * 2 + other
                    _remote(ins[a].at[pl.ds(c * hr, hr), :], land[a].at[2 * x + y, pl.ds(c * hr, hr), :],
                            send.at[k], recv.at[k], (px, py, c ^ other)).start()
        token[...] = jnp.zeros_like(token)

    lands = [((N_CHIPS,) + a.shape, a.dtype) for a in arrs]
    held = lambda a: pltpu.with_memory_space_constraint(a, pltpu.HBM)
    outs = pl.pallas_call(
        body, name=name, in_specs=[HBM] * (2 * na) + [ANY],
        out_specs=[SEM, SEM] + [HBM] * (2 * na) + [pl.BlockSpec(memory_space=pltpu.VMEM)],
        out_shape=[pltpu.SemaphoreType.DMA((6 * na,)), pltpu.SemaphoreType.DMA((6 * na,))]
        + [pltpu.HBM(a.shape, a.dtype) for a in arrs] + [pltpu.HBM(s, d) for s, d in lands]
        + [jax.ShapeDtypeStruct((8, 128), F32)],
        input_output_aliases={i: 2 + i for i in range(2 * na)},
        compiler_params=pltpu.CompilerParams(has_side_effects=pltpu.SideEffectType.DATAFLOW_SIDE_EFFECTING),
    )(*[held(a) for a in arrs], *[held(lax.empty(s, d)) for s, d in lands], after)
    return outs[0], outs[1], outs[2:2 + na], outs[2 + na:2 + 2 * na], outs[-1]


def _gather_late_wait(send, recv, thru, land, after, *, name):
    na = len(thru)

    def body(*refs):
        ins, slots = refs[:na], refs[na:2 * na]
        send_sem, recv_sem = refs[2 * na], refs[2 * na + 1]
        x, y, c, _, chips = _place()
        for a in range(na):
            hr = thru[a].shape[0] // 2
            for j, (px, py) in enumerate(chips):
                for other in range(2):
                    k = (a * 3 + j) * 2 + other
                    half = c ^ other
                    cp = _remote(ins[a].at[pl.ds(c * hr, hr), :], slots[a].at[2 * px + py, pl.ds(half * hr, hr), :],
                                 send_sem.at[k], recv_sem.at[k], (px, py, half))
                    cp.wait_send()
                    cp.wait_recv()

    outs = pl.pallas_call(
        body, name=name, in_specs=[HBM] * (2 * na) + [SEM, SEM, ANY], out_specs=[HBM] * (2 * na),
        out_shape=[pltpu.HBM(a.shape, a.dtype) for a in list(thru) + list(land)],
        input_output_aliases={i: i for i in range(2 * na)},
        compiler_params=pltpu.CompilerParams(has_side_effects=pltpu.SideEffectType.DATAFLOW_SIDE_EFFECTING),
    )(*thru, *land, send, recv, after)
    return outs[na:]


def _row_chunks(rows, row_bytes, align=8):
    n = max(1, min(rows // align, -(-rows * row_bytes // DMA_CHUNK_BYTES)))
    per = -(-(-(-rows // n)) // align) * align
    return [(r, min(per, rows - r)) for r in range(0, rows, per)]


def _remote(src, dst, send, recv, to):
    return pltpu.make_async_remote_copy(src_ref=src, dst_ref=dst, send_sem=send, recv_sem=recv, device_id=to,
                                        device_id_type=MESH)


def _exchange_halves(arrs, *, name):
    na = len(arrs)

    def body(*refs):
        ins, got = refs[:na], refs[na:2 * na]
        send, recv = refs[2 * na:]
        x, y, c, sibling, _ = _place()
        for a in range(na):
            ns, rows, cols = arrs[a].shape
            hr = rows // 2
            for s in range(ns):
                for r0, nr in _row_chunks(hr, cols * arrs[a].dtype.itemsize, 16):
                    _remote(ins[a].at[s, pl.ds((1 - c) * hr + r0, nr), :], got[a].at[s, pl.ds(r0, nr), :],
                            send.at[a], recv.at[a], sibling).start()
        for a in range(na):
            hr = arrs[a].shape[1] // 2
            _remote(ins[a].at[:, pl.ds((1 - c) * hr, hr), :], got[a], send.at[a], recv.at[a], sibling).wait()

    return pl.pallas_call(
        body, name=name, in_specs=[ANY] * na, out_specs=[ANY] * na,
        out_shape=[jax.ShapeDtypeStruct((a.shape[0], a.shape[1] // 2, a.shape[2]), a.dtype) for a in arrs],
        scratch_shapes=[pltpu.SemaphoreType.DMA((na,)), pltpu.SemaphoreType.DMA((na,))],
    )(*arrs)


HBM = pl.BlockSpec(memory_space=pltpu.HBM)
SEM = pl.BlockSpec(memory_space=pltpu.SEMAPHORE)


def _scatter_start(arrs, *, name):
    na = len(arrs)

    def body(*refs):
        ins, land = refs[:na], refs[na:2 * na]
        send, recv = refs[2 * na], refs[2 * na + 1]
        token = refs[-1]
        x, y, c, _, chips = _place()
        for a in range(na):
            _, hr, cols = arrs[a].shape
            for j, (px, py) in enumerate(chips):
                for r0, nr in _row_chunks(hr, cols * arrs[a].dtype.itemsize, 16):
                    _remote(ins[a].at[2 * px + py, pl.ds(r0, nr), :], land[a].at[j, pl.ds(r0, nr), :],
                            send.at[a * 3 + j], recv.at[a * 3 + j], (px, py, c)).start()
        token[...] = jnp.zeros_like(token)

    slots = [((N_CHIPS - 1,) + a.shape[1:], a.dtype) for a in arrs]
    held = lambda a: pltpu.with_memory_space_constraint(a, pltpu.HBM)
    outs = pl.pallas_call(
        body, name=name, in_specs=[HBM] * (2 * na),
        out_specs=[SEM, SEM] + [HBM] * (2 * na) + [pl.BlockSpec(memory_space=pltpu.VMEM)],
        out_shape=[pltpu.SemaphoreType.DMA((3 * na,)), pltpu.SemaphoreType.DMA((3 * na,))]
        + [pltpu.HBM(a.shape, a.dtype) for a in arrs] + [pltpu.HBM(s, d) for s, d in slots]
        + [jax.ShapeDtypeStruct((8, 128), F32)],
        input_output_aliases={i: 2 + i for i in range(2 * na)},
        compiler_params=pltpu.CompilerParams(has_side_effects=pltpu.SideEffectType.DATAFLOW_SIDE_EFFECTING),
    )(*[held(a) for a in arrs], *[held(lax.empty(s, d)) for s, d in slots])
    return outs[0], outs[1], outs[2:2 + na], outs[2 + na:2 + 2 * na], outs[-1]


def _scatter_wait(send, recv, thru, land, after, *, name):
    na = len(thru)

    def body(*refs):
        ins, slots = refs[:na], refs[na:2 * na]
        send_sem, recv_sem = refs[2 * na], refs[2 * na + 1]
        x, y, c, _, chips = _place()
        for a in range(na):
            for j, (px, py) in enumerate(chips):
                cp = _remote(ins[a].at[2 * px + py], slots[a].at[j], send_sem.at[a * 3 + j], recv_sem.at[a * 3 + j],
                             (px, py, c))
                cp.wait_send()
                cp.wait_recv()

    outs = pl.pallas_call(
        body, name=name, in_specs=[HBM] * (2 * na) + [SEM, SEM, ANY], out_specs=[HBM] * (2 * na),
        out_shape=[pltpu.HBM(a.shape, a.dtype) for a in list(thru) + list(land)],
        input_output_aliases={i: i for i in range(2 * na)},
        compiler_params=pltpu.CompilerParams(has_side_effects=pltpu.SideEffectType.DATAFLOW_SIDE_EFFECTING),
    )(*thru, *land, send, recv, after)
    return outs[:na], outs[na:]


def _exchange_start(arrs, *, name):
    na = len(arrs)

    def body(*refs):
        ins, got = refs[:na], refs[na:2 * na]
        send, recv = refs[2 * na], refs[2 * na + 1]
        token = refs[-1]
        x, y, c, sibling, _ = _place()
        for a in range(na):
            ns, rows, cols = arrs[a].shape
            hr = rows // 2
            for s in range(ns):
                for r0, nr in _row_chunks(hr, cols * arrs[a].dtype.itemsize, 16):
                    _remote(ins[a].at[s, pl.ds((1 - c) * hr + r0, nr), :], got[a].at[s, pl.ds(r0, nr), :],
                            send.at[a], recv.at[a], sibling).start()
        token[...] = jnp.zeros_like(token)

    lands = [((a.shape[0], a.shape[1] // 2, a.shape[2]), a.dtype) for a in arrs]
    held = lambda a: pltpu.with_memory_space_constraint(a, pltpu.HBM)
    outs = pl.pallas_call(
        body, name=name, in_specs=[HBM] * (2 * na),
        out_specs=[SEM, SEM] + [HBM] * (2 * na) + [pl.BlockSpec(memory_space=pltpu.VMEM)],
        out_shape=[pltpu.SemaphoreType.DMA((na,)), pltpu.SemaphoreType.DMA((na,))]
        + [pltpu.HBM(a.shape, a.dtype) for a in arrs] + [pltpu.HBM(s, d) for s, d in lands]
        + [jax.ShapeDtypeStruct((8, 128), F32)],
        input_output_aliases={i: 2 + i for i in range(2 * na)},
        compiler_params=pltpu.CompilerParams(has_side_effects=pltpu.SideEffectType.DATAFLOW_SIDE_EFFECTING),
    )(*[held(a) for a in arrs], *[held(lax.empty(s, d)) for s, d in lands])
    return outs[0], outs[1], outs[2:2 + na], outs[2 + na:2 + 2 * na], outs[-1]


def _exchange_wait(send, recv, thru, got, after, *, name):
    na = len(thru)

    def body(*refs):
        ins, lands = refs[:na], refs[na:2 * na]
        send_sem, recv_sem = refs[2 * na], refs[2 * na + 1]
        x, y, c, sibling, _ = _place()
        for a in range(na):
            hr = thru[a].shape[1] // 2
            cp = _remote(ins[a].at[:, pl.ds((1 - c) * hr, hr), :], lands[a], send_sem.at[a], recv_sem.at[a],
                         sibling)
            cp.wait_send()
            cp.wait_recv()

    outs = pl.pallas_call(
        body, name=name, in_specs=[HBM] * (2 * na) + [SEM, SEM, ANY], out_specs=[HBM] * (2 * na),
        out_shape=[pltpu.HBM(a.shape, a.dtype) for a in list(thru) + list(got)],
        input_output_aliases={i: i for i in range(2 * na)},
        compiler_params=pltpu.CompilerParams(has_side_effects=pltpu.SideEffectType.DATAFLOW_SIDE_EFFECTING),
    )(*thru, *got, send, recv, after)
    return outs[:na], outs[na:]


def _finish_reduce(big, small, *, name):
    nb = len(big)

    def body(*refs):
        outs, sm = refs[nb + 1:2 * nb + 1], refs[2 * nb + 1]
        send, recv = refs[2 * nb + 2:]
        x, y, c, sibling, chips = _place()
        blk = lambda px, py, pc: sm.at[4 * px + 2 * py + pc]
        for a in range(nb):
            _, hr, cols = big[a].shape
            for r0, nr in _row_chunks(hr, cols * 4):
                _remote(outs[a].at[c, pl.ds(r0, nr), :], outs[a].at[c, pl.ds(r0, nr), :], send.at[a], recv.at[a],
                        sibling).start()
        first = [_remote(blk(x, y, c), blk(x, y, c), send.at[nb], recv.at[nb], sibling)]
        first += [_remote(blk(x, y, c), blk(x, y, c), send.at[nb + 1 + j], recv.at[nb + 1 + j], (*chip, c))
                  for j, chip in enumerate(chips)]
        for cp in first:
            cp.start()
        passed = []
        for j, (px, py) in enumerate(chips):
            _remote(blk(px, py, c), blk(px, py, c), send.at[nb + 1 + j], recv.at[nb + 1 + j], sibling).wait_recv()
            fw = _remote(blk(px, py, c), blk(px, py, c), send.at[nb + 4 + j], recv.at[nb + 4 + j], sibling)
            fw.start()
            passed.append(fw)
        for a in range(nb):
            _remote(outs[a].at[c], outs[a].at[1 - c], send.at[a], recv.at[a], sibling).wait()
        _remote(blk(x, y, c), blk(x, y, 1 - c), send.at[nb], recv.at[nb], sibling).wait_recv()
        for j, (px, py) in enumerate(chips):
            _remote(blk(px, py, c), blk(px, py, 1 - c), send.at[nb + 4 + j], recv.at[nb + 4 + j], sibling).wait_recv()
        for cp in first + passed:
            cp.wait_send()

    n_remote = nb + 7
    arrs = list(big) + [small]
    outs = pl.pallas_call(
        body, name=name, in_specs=[ANY] * (nb + 1), out_specs=[ANY] * (nb + 1),
        out_shape=[jax.ShapeDtypeStruct(a.shape, F32) for a in arrs],
        input_output_aliases={i: i for i in range(nb + 1)},
        scratch_shapes=[pltpu.SemaphoreType.DMA((n_remote,)), pltpu.SemaphoreType.DMA((n_remote,))],
    )(*arrs)
    return outs[:nb], outs[nb]


def _reduce_start(sectioned, where):
    counts = [len(parts) for parts in sectioned]
    first = [sum(counts[:i]) for i in range(len(sectioned) + 1)]
    send, recv, thru, got, token = _exchange_start([a for parts in sectioned for a in parts], name="rs_exchange_start")

    def resume(after):
        mine, theirs = _exchange_wait(send, recv, thru, got, after, name="rs_exchange_wait")
        chip_sum = [_add_sibling(mine[first[i]:first[i + 1]], theirs[first[i]:first[i + 1]], where, BF16,
                                 name=f"rs_add_sibling_{i}") for i in range(len(sectioned))]
        s_send, s_recv, s_thru, s_land, s_token = _scatter_start(chip_sum, name="rs_scatter_start")
        return (s_send, s_recv, s_thru, s_land), s_token

    return token, resume


def _reduce_finish(handle, small, where):
    send, recv, thru, land = handle
    e_send, e_recv, e_thru, e_got, e_token = _exchange_start([small], name="rs_exchange_small_start")
    own, slots = _scatter_wait(send, recv, thru, land, e_token, name="rs_scatter_wait")
    red = [_sum_chips(own[0], slots[0], where, 2, 0, name="rs_sum_chips_0")]
    mine, got = _exchange_wait(e_send, e_recv, e_thru, e_got, red[0], name="rs_exchange_small_wait")
    small_sum = _add_sibling(mine, got, where, F32, name="rs_add_sibling_small")
    sm_send, sm_recv, sm_thru, sm_land, sm_token = _scatter_start([small_sum], name="rs_scatter_small_start")
    where_after = where + sm_token[0, :where.shape[0]].astype(jnp.int32)
    red += [_sum_chips(p, s, where_after, 2, 0, name=f"rs_sum_chips_{i + 1}")
            for i, (p, s) in enumerate(zip(own[1:], slots[1:]))]
    (small_sum,), (small_slots,) = _scatter_wait(sm_send, sm_recv, sm_thru, sm_land, red[-1], name="rs_scatter_small_wait")
    red_small = _sum_chips(small_sum, small_slots, where, 2 * N_CHIPS, 2, name="rs_sum_chips_small")
    big, sm = _finish_reduce(red, red_small, name="rs_finish")
    return big, sm.reshape(-1, sm.shape[-1])


def kernel(x, c, ctx, c_ctx, w_mod, b_mod, g_pre, g_post, w_in, w_conv, a_log, dt_bias, g_onorm, gm_ln_g, gm_ln_b, w_sp, b_sp, w_pa, w_pb, w_out, loss_target, m_c_ctx, m_w_mod, m_b_mod, m_g_pre, m_g_post, m_w_in, m_w_conv, m_a_log, m_dt_bias, m_g_onorm, m_gm_ln_g, m_gm_ln_b, m_w_sp, m_b_sp, m_w_pa, m_w_pb, m_w_out, v_c_ctx, v_w_mod, v_b_mod, v_g_pre, v_g_post, v_w_in, v_w_conv, v_a_log, v_dt_bias, v_g_onorm, v_gm_ln_g, v_gm_ln_b, v_w_sp, v_b_sp, v_w_pa, v_w_pb, v_w_out):
    names = ["c_ctx", "w_mod", "b_mod", "g_pre", "g_post", "w_in", "w_conv", "a_log", "dt_bias", "g_onorm", "gm_ln_g",
             "gm_ln_b", "w_sp", "b_sp", "w_pa", "w_pb", "w_out"]
    w = dict(zip(names, (c_ctx, w_mod, b_mod, g_pre, g_post, w_in, w_conv, a_log, dt_bias, g_onorm, gm_ln_g, gm_ln_b,
                         w_sp, b_sp, w_pa, w_pb, w_out)))
    m = dict(zip(names, (m_c_ctx, m_w_mod, m_b_mod, m_g_pre, m_g_post, m_w_in, m_w_conv, m_a_log, m_dt_bias, m_g_onorm,
                         m_gm_ln_g, m_gm_ln_b, m_w_sp, m_b_sp, m_w_pa, m_w_pb, m_w_out)))
    v = dict(zip(names, (v_c_ctx, v_w_mod, v_b_mod, v_g_pre, v_g_post, v_w_in, v_w_conv, v_a_log, v_dt_bias, v_g_onorm,
                         v_gm_ln_g, v_gm_ln_b, v_w_sp, v_b_sp, v_w_pa, v_w_pb, v_w_out)))
    xy = 2 * lax.axis_index("x") + lax.axis_index("y")
    where = jnp.stack([lax.axis_index("c"), xy, 2 * xy + lax.axis_index("c")]).astype(jnp.int32)

    shards = [a[0].astype(BF16) for a in (w_mod, w_in)]
    gathered, wconv_all = _gather_shards(shards, w_conv[0], name="gather_weights")
    own = lambda full, shard: lax.dynamic_update_slice(full, shard[None], (xy, 0, 0))
    wm_all, win_all = [own(f, s) for f, s in zip(gathered, shards)]
    wconv_all = own(wconv_all, w_conv[0])
    late_shards = [a[0].astype(BF16) for a in (w_pa, w_pb, w_out)]
    late = _gather_late_start(late_shards, wconv_all, name="gather_late_start")

    def late_weights(after):
        lands = _gather_late_wait(*late[:4], after, name="gather_late_wait")
        return [own(f, s).reshape(D, D) for f, s in zip(lands, late_shards)]
    w_conv_f = jnp.concatenate([wconv_all[s] for s in range(N_CHIPS)], axis=1)
    shard_cols = IN_COLS // N_CHIPS
    cut = OFF_A - shard_cols
    assert 0 < cut and cut + 4 * H < shard_cols
    w_qkv = jnp.concatenate([win_all[0], win_all[1][:, :cut]], axis=1)
    w_ab = jnp.pad(win_all[1][:, cut:cut + 4 * H], ((0, 0), (0, DH - 4 * H)))
    w_rest = jnp.concatenate([win_all[1][:, cut + 4 * H:], win_all[2], win_all[3]], axis=1)

    blk = D // N_CHIPS
    rest_cut = shard_cols - cut - 4 * H
    big_names = ("w_in", "w_pa", "w_pb", "w_out")
    in_flight = []

    def start_reduce(grads):
        dw_qkv, dw_ab, dw_rest = grads["w_in"]
        g_win = [dw_qkv[:, :shard_cols],
                 jnp.concatenate([dw_qkv[:, shard_cols:], dw_ab, dw_rest[:, :rest_cut]], axis=1),
                 dw_rest[:, rest_cut:rest_cut + shard_cols], dw_rest[:, rest_cut + shard_cols:]]
        sectioned = [[a.astype(BF16)[None] for a in g_win]]
        sectioned += [[grads[k].reshape(N_CHIPS, blk, D)] for k in ("w_pa", "w_pb", "w_out")]
        token, resume = _reduce_start(sectioned, where)

        def scatter(after):
            handle, scatter_token = resume(after)
            in_flight.append(handle)
            return scatter_token

        return token, scatter

    loss_local, grad_x, g = _local_step(
        x[0], c, ctx[0], c_ctx, loss_target[0], wm_all, b_mod, g_pre, g_post, w_qkv, w_ab, w_rest,
        w_conv_f, a_log[0], dt_bias[0], g_onorm, gm_ln_g, gm_ln_b, w_sp[0], b_sp[0],
        late_weights, on_large_gradients=start_reduce, run_after=late[4])
    g["loss"] = loss_local
    g["_pad"] = jnp.zeros((SMALL_LAYOUT["_pad"][1],), F32)
    g["mod_factors"] = lax.dynamic_update_slice(jnp.zeros((2 * N_CHIPS, 4 * D), F32),
                                                jnp.concatenate(g["mod_factors"])[None], (where[2], 0))
    tail = jnp.zeros((N_CHIPS * SMALL_ROWS * 128 - sum(s for _, s in SMALL_LAYOUT.values()),), F32)
    flat = jnp.concatenate([g[k].reshape(-1) for k in SMALL_LAYOUT] + [tail])
    reduced, gr_small = _reduce_finish(in_flight[0], flat.reshape(N_CHIPS, SMALL_ROWS, 128), where)
    gr_tiny = gr_small[TINY_ROW0:]

    def entry(arr, k):
        off, size = SMALL_LAYOUT[k]
        row, col = off // 128 - TINY_ROW0, off % 128
        return arr[row:row + size // 128].reshape(-1) if size >= 128 else arr[row, col:col + size]

    factors = entry(gr_tiny, "mod_factors").reshape(2 * N_CHIPS, 4 * D)
    pad_rows = lambda a: jnp.pad(a, ((0, 16 - a.shape[0]), (0, 0)))
    lhs = pad_rows(jnp.concatenate([factors[:, :D], g["silu_c_ctx"]], axis=0))
    rhs = pad_rows(jnp.concatenate([factors[:, D:], entry(gr_tiny, "dm_ctx")[None]], axis=0))
    mod_cols = 3 * D // N_CHIPS
    gr_wm = _mm(lhs, lax.dynamic_slice(rhs, (0, xy * mod_cols), (16, mod_cols)), ta=True, name="dw_mod")

    res = {k: _adamw(w[k], gr, m[k], v[k], name=f"adamw_{k}") for k, gr in zip(big_names[1:], reduced[1:])}
    flip = lambda a: jnp.swapaxes(a, 1, 2)
    res["w_in"] = [flip(r) for r in _adamw(flip(w_in), flip(reduced[0].reshape(w_in.shape)), flip(m_w_in), flip(v_w_in),
                                           name="adamw_w_in")]
    res["w_mod"] = _adamw(w_mod[0], gr_wm, m_w_mod[0], v_w_mod[0], name="adamw_w_mod")
    res["w_sp"] = _adamw(w_sp.reshape(-1, 128), gr_small, m_w_sp.reshape(-1, 128), v_w_sp.reshape(-1, 128),
                         name="adamw_w_sp")
    tiny = [k for k, (off, _) in SMALL_LAYOUT.items() if TINY_ROW0 <= off // 128 < TINY_ROW0 + TINY_ROWS]
    tiny_names = [k for k in tiny if k not in ("loss", "_pad")]

    def pack(src):
        parts = [src[k].reshape(-1) if k in tiny_names else jnp.zeros((SMALL_LAYOUT[k][1],), F32) for k in tiny]
        return jnp.concatenate(parts).reshape(TINY_ROWS, 128)

    tiny_res = _adamw(pack(w), gr_tiny[:TINY_ROWS], pack(m), pack(v), name="adamw_tiny")
    for k in tiny_names:
        res[k] = [entry(r, k) for r in tiny_res]
    g_conv = lax.dynamic_slice(entry(gr_tiny, "w_conv").reshape(3, 3 * D), (0, xy * (3 * D // N_CHIPS)),
                               (3, 3 * D // N_CHIPS))
    conv_res = _adamw(jnp.pad(w_conv[0], ((0, 5), (0, 0))), jnp.pad(g_conv, ((0, 5), (0, 0))),
                      jnp.pad(m_w_conv[0], ((0, 5), (0, 0))), jnp.pad(v_w_conv[0], ((0, 5), (0, 0))), name="adamw_w_conv")
    res["w_conv"] = [r[:3] for r in conv_res]
    res = {k: [r.reshape(w[k].shape) for r in res[k]] for k in names}

    out = [entry(gr_tiny, "loss").reshape(()), grad_x[None]]
    for i in range(4):
        out += [res[k][i] for k in names]
    return tuple(out)
```
